```python
import jax, jax.numpy as jnp
from jax import lax
import numpy as np

D_MODEL = 1024
BATCH = 8
SEQ = 4096
DEPTH = 1

POOL_WIDTH = D_MODEL // 2
POOL_WINDOWS = (2, 4, 8, 16)
POOL_GROUP = POOL_WIDTH // len(POOL_WINDOWS)
LRU_WIDTH = D_MODEL - POOL_WIDTH
LRU_HEADS = 8
LRU_HEAD_DIM = LRU_WIDTH // LRU_HEADS
CONV_WIDTH = 4
LRU_C = 8.0
MIX_WIDTH = POOL_WIDTH + LRU_WIDTH
IN_WIDTH = POOL_WIDTH + 2 * LRU_WIDTH
D_FF = ((8 * D_MODEL // 3 + 255) // 256) * 256
EPS = 1e-6

kernel_name = "hymba_pool_rglru_block"


def rmsnorm(x, g):
    xf = x.astype(jnp.float32)
    y = xf * lax.rsqrt(jnp.mean(xf * xf, axis=-1, keepdims=True) + EPS)
    return (y * g.astype(jnp.float32)).astype(x.dtype)


def pool_mixer(u, pool_w, pool_scale):
    B, S, _ = u.shape
    uf = u.astype(jnp.float32)
    pos = jnp.arange(1, S + 1, dtype=jnp.float32)[None, :, None]
    outs = []
    for g, w in enumerate(POOL_WINDOWS):
        ug = uf[..., g * POOL_GROUP:(g + 1) * POOL_GROUP]
        cs = jnp.cumsum(ug, axis=1)
        lag = jnp.pad(cs[:, :S - w], ((0, 0), (w, 0), (0, 0)))
        mean = (cs - lag) / jnp.minimum(pos, float(w))
        outs.append(mean - ug)
    d = jnp.stack(outs, axis=2)
    y = jnp.einsum('bsgc,gcd->bsgd', d, pool_w.astype(jnp.float32)).reshape(B, S, POOL_WIDTH)
    return (y * pool_scale.astype(jnp.float32)).astype(u.dtype)


def causal_depthwise_conv(u, w, b):
    C = u.shape[-1]
    y = lax.conv_general_dilated(
        u, w[:, None, :].astype(u.dtype), window_strides=(1,),
        padding=[(CONV_WIDTH - 1, 0)], dimension_numbers=('NWC', 'WIO', 'NWC'),
        feature_group_count=C)
    return y + b.astype(u.dtype)


def _lin_combine(c1, c2):
    a1, b1 = c1
    a2, b2 = c2
    return a1 * a2, a2 * b1 + b2


def rg_lru(xc, w_a, b_a, w_i, b_i, lam):
    B, S, C = xc.shape
    xf = xc.astype(jnp.float32)
    xh = xf.reshape(B, S, LRU_HEADS, LRU_HEAD_DIM)
    r = jax.nn.sigmoid(jnp.einsum('bshc,hcd->bshd', xh, w_a.astype(jnp.float32)).reshape(B, S, C) + b_a.astype(jnp.float32))
    i = jax.nn.sigmoid(jnp.einsum('bshc,hcd->bshd', xh, w_i.astype(jnp.float32)).reshape(B, S, C) + b_i.astype(jnp.float32))
    log_a = -LRU_C * r * jax.nn.softplus(-lam.astype(jnp.float32))
    a = jnp.exp(log_a)
    mult = jnp.sqrt(jnp.maximum(-jnp.expm1(2.0 * log_a), 0.0))
    bterm = mult * (i * xf)
    _, h = lax.associative_scan(_lin_combine, (a, bterm), axis=1)
    return h.astype(xc.dtype)


def _fwd_setup_inputs(seed: int = 0) -> dict:
    key = jax.random.key(seed)
    ks = jax.random.split(key, 24)
    f32 = jnp.float32

    def nrm(k, shape, fan_in):
        return jax.random.normal(k, shape, f32) * (fan_in ** -0.5)

    def gain(k, shape):
        return 1.0 + 0.05 * jax.random.normal(k, shape, f32)

    x = jax.random.normal(ks[0], (BATCH, SEQ, D_MODEL), f32)
    ln1_g = gain(ks[1], (DEPTH, D_MODEL))
    w_in = nrm(ks[2], (DEPTH, D_MODEL, IN_WIDTH), D_MODEL)
    pool_w = nrm(ks[3], (DEPTH, len(POOL_WINDOWS), POOL_GROUP, POOL_GROUP), POOL_GROUP)
    pool_scale = 1.0 + 0.1 * jax.random.normal(ks[4], (DEPTH, POOL_WIDTH), f32)
    conv_w = nrm(ks[5], (DEPTH, CONV_WIDTH, LRU_WIDTH), CONV_WIDTH)
    conv_b = 0.02 * jax.random.normal(ks[6], (DEPTH, LRU_WIDTH), f32)
    w_a = nrm(ks[7], (DEPTH, LRU_HEADS, LRU_HEAD_DIM, LRU_HEAD_DIM), LRU_HEAD_DIM)
    b_a = 0.02 * jax.random.normal(ks[8], (DEPTH, LRU_WIDTH), f32)
    w_i = nrm(ks[9], (DEPTH, LRU_HEADS, LRU_HEAD_DIM, LRU_HEAD_DIM), LRU_HEAD_DIM)
    b_i = 0.02 * jax.random.normal(ks[10], (DEPTH, LRU_WIDTH), f32)
    u = jax.random.uniform(ks[11], (DEPTH, LRU_WIDTH), f32, minval=0.9, maxval=0.999)
    s = u ** (1.0 / LRU_C)
    lam = jnp.log(s) - jnp.log1p(-s)
    gn_pool_g = gain(ks[12], (DEPTH, POOL_WIDTH))
    gn_lru_g = gain(ks[13], (DEPTH, LRU_WIDTH))
    w_out = nrm(ks[14], (DEPTH, MIX_WIDTH, D_MODEL), MIX_WIDTH)
    ln2_g = gain(ks[15], (DEPTH, D_MODEL))
    w_ffn_gate = nrm(ks[16], (DEPTH, D_MODEL, D_FF), D_MODEL)
    w_ffn_up = nrm(ks[17], (DEPTH, D_MODEL, D_FF), D_MODEL)
    w_ffn_down = nrm(ks[18], (DEPTH, D_FF, D_MODEL), D_FF)
    lnf_g = gain(ks[19], (D_MODEL,))
    return {"x": x, "ln1_g": ln1_g, "w_in": w_in, "pool_w": pool_w, "pool_scale": pool_scale,
            "conv_w": conv_w, "conv_b": conv_b, "w_a": w_a, "b_a": b_a, "w_i": w_i, "b_i": b_i,
            "lam": lam, "gn_pool_g": gn_pool_g, "gn_lru_g": gn_lru_g, "w_out": w_out,
            "ln2_g": ln2_g, "w_ffn_gate": w_ffn_gate, "w_ffn_up": w_ffn_up,
            "w_ffn_down": w_ffn_down, "lnf_g": lnf_g}


def _fwd_reference(x, ln1_g, w_in, pool_w, pool_scale, conv_w, conv_b, w_a, b_a, w_i, b_i,
              lam, gn_pool_g, gn_lru_g, w_out, ln2_g, w_ffn_gate, w_ffn_up, w_ffn_down, lnf_g):
    h = x
    for l in range(DEPTH):
        n = rmsnorm(h, ln1_g[l])
        proj = n @ w_in[l]
        u_pool = proj[..., :POOL_WIDTH]
        u_lru = proj[..., POOL_WIDTH:POOL_WIDTH + LRU_WIDTH]
        u_gate = proj[..., POOL_WIDTH + LRU_WIDTH:]
        y_pool = pool_mixer(u_pool, pool_w[l], pool_scale[l])
        xc = causal_depthwise_conv(u_lru, conv_w[l], conv_b[l])
        y_lru = rg_lru(xc, w_a[l], b_a[l], w_i[l], b_i[l], lam[l]) * jax.nn.gelu(u_gate)
        mix = jnp.concatenate([rmsnorm(y_pool, gn_pool_g[l]), rmsnorm(y_lru, gn_lru_g[l])], axis=-1)
        h = h + mix @ w_out[l]
        n2 = rmsnorm(h, ln2_g[l])
        h = h + (jax.nn.silu(n2 @ w_ffn_gate[l]) * (n2 @ w_ffn_up[l])) @ w_ffn_down[l]
    return rmsnorm(h, lnf_g)


import jax as _jax
import jax.numpy as _jnp

TWIN_FORMAT = 'train_step'
FWD_PARAMS = ['x', 'ln1_g', 'w_in', 'pool_w', 'pool_scale', 'conv_w', 'conv_b', 'w_a', 'b_a', 'w_i', 'b_i', 'lam', 'gn_pool_g', 'gn_lru_g', 'w_out', 'ln2_g', 'w_ffn_gate', 'w_ffn_up', 'w_ffn_down', 'lnf_g']
TWIN_WEIGHTS = ['ln1_g', 'w_in', 'pool_w', 'pool_scale', 'conv_w', 'conv_b', 'w_a', 'b_a', 'w_i', 'b_i', 'lam', 'gn_pool_g', 'gn_lru_g', 'w_out', 'ln2_g', 'w_ffn_gate', 'w_ffn_up', 'w_ffn_down', 'lnf_g']
TWIN_DIFF_INPUT = 'x'
TWIN_INPUTS = ['x', 'ln1_g', 'w_in', 'pool_w', 'pool_scale', 'conv_w', 'conv_b', 'w_a', 'b_a', 'w_i', 'b_i', 'lam', 'gn_pool_g', 'gn_lru_g', 'w_out', 'ln2_g', 'w_ffn_gate', 'w_ffn_up', 'w_ffn_down', 'lnf_g', 'loss_target', 'm_ln1_g', 'm_w_in', 'm_pool_w', 'm_pool_scale', 'm_conv_w', 'm_conv_b', 'm_w_a', 'm_b_a', 'm_w_i', 'm_b_i', 'm_lam', 'm_gn_pool_g', 'm_gn_lru_g', 'm_w_out', 'm_ln2_g', 'm_w_ffn_gate', 'm_w_ffn_up', 'm_w_ffn_down', 'm_lnf_g', 'v_ln1_g', 'v_w_in', 'v_pool_w', 'v_pool_scale', 'v_conv_w', 'v_conv_b', 'v_w_a', 'v_b_a', 'v_w_i', 'v_b_i', 'v_lam', 'v_gn_pool_g', 'v_gn_lru_g', 'v_w_out', 'v_ln2_g', 'v_w_ffn_gate', 'v_w_ffn_up', 'v_w_ffn_down', 'v_lnf_g']
TWIN_OUTPUTS = ['loss', 'grad_x', 'grad_ln1_g', 'grad_w_in', 'grad_pool_w', 'grad_pool_scale', 'grad_conv_w', 'grad_conv_b', 'grad_w_a', 'grad_b_a', 'grad_w_i', 'grad_b_i', 'grad_lam', 'grad_gn_pool_g', 'grad_gn_lru_g', 'grad_w_out', 'grad_ln2_g', 'grad_w_ffn_gate', 'grad_w_ffn_up', 'grad_w_ffn_down', 'grad_lnf_g', 'delta_ln1_g', 'delta_w_in', 'delta_pool_w', 'delta_pool_scale', 'delta_conv_w', 'delta_conv_b', 'delta_w_a', 'delta_b_a', 'delta_w_i', 'delta_b_i', 'delta_lam', 'delta_gn_pool_g', 'delta_gn_lru_g', 'delta_w_out', 'delta_ln2_g', 'delta_w_ffn_gate', 'delta_w_ffn_up', 'delta_w_ffn_down', 'delta_lnf_g', 'new_m_ln1_g', 'new_m_w_in', 'new_m_pool_w', 'new_m_pool_scale', 'new_m_conv_w', 'new_m_conv_b', 'new_m_w_a', 'new_m_b_a', 'new_m_w_i', 'new_m_b_i', 'new_m_lam', 'new_m_gn_pool_g', 'new_m_gn_lru_g', 'new_m_w_out', 'new_m_ln2_g', 'new_m_w_ffn_gate', 'new_m_w_ffn_up', 'new_m_w_ffn_down', 'new_m_lnf_g', 'new_v_ln1_g', 'new_v_w_in', 'new_v_pool_w', 'new_v_pool_scale', 'new_v_conv_w', 'new_v_conv_b', 'new_v_w_a', 'new_v_b_a', 'new_v_w_i', 'new_v_b_i', 'new_v_lam', 'new_v_gn_pool_g', 'new_v_gn_lru_g', 'new_v_w_out', 'new_v_ln2_g', 'new_v_w_ffn_gate', 'new_v_w_ffn_up', 'new_v_w_ffn_down', 'new_v_lnf_g']
TWIN_LEAF_KINDS = {'loss': 'loss', 'grad_x': 'grad_x', 'grad_ln1_g': 'grad_w', 'grad_w_in': 'grad_w', 'grad_pool_w': 'grad_w', 'grad_pool_scale': 'grad_w', 'grad_conv_w': 'grad_w', 'grad_conv_b': 'grad_w', 'grad_w_a': 'grad_w', 'grad_b_a': 'grad_w', 'grad_w_i': 'grad_w', 'grad_b_i': 'grad_w', 'grad_lam': 'grad_w', 'grad_gn_pool_g': 'grad_w', 'grad_gn_lru_g': 'grad_w', 'grad_w_out': 'grad_w', 'grad_ln2_g': 'grad_w', 'grad_w_ffn_gate': 'grad_w', 'grad_w_ffn_up': 'grad_w', 'grad_w_ffn_down': 'grad_w', 'grad_lnf_g': 'grad_w', 'delta_ln1_g': 'delta_w', 'delta_w_in': 'delta_w', 'delta_pool_w': 'delta_w', 'delta_pool_scale': 'delta_w', 'delta_conv_w': 'delta_w', 'delta_conv_b': 'delta_w', 'delta_w_a': 'delta_w', 'delta_b_a': 'delta_w', 'delta_w_i': 'delta_w', 'delta_b_i': 'delta_w', 'delta_lam': 'delta_w', 'delta_gn_pool_g': 'delta_w', 'delta_gn_lru_g': 'delta_w', 'delta_w_out': 'delta_w', 'delta_ln2_g': 'delta_w', 'delta_w_ffn_gate': 'delta_w', 'delta_w_ffn_up': 'delta_w', 'delta_w_ffn_down': 'delta_w', 'delta_lnf_g': 'delta_w', 'new_m_ln1_g': 'new_m', 'new_m_w_in': 'new_m', 'new_m_pool_w': 'new_m', 'new_m_pool_scale': 'new_m', 'new_m_conv_w': 'new_m', 'new_m_conv_b': 'new_m', 'new_m_w_a': 'new_m', 'new_m_b_a': 'new_m', 'new_m_w_i': 'new_m', 'new_m_b_i': 'new_m', 'new_m_lam': 'new_m', 'new_m_gn_pool_g': 'new_m', 'new_m_gn_lru_g': 'new_m', 'new_m_w_out': 'new_m', 'new_m_ln2_g': 'new_m', 'new_m_w_ffn_gate': 'new_m', 'new_m_w_ffn_up': 'new_m', 'new_m_w_ffn_down': 'new_m', 'new_m_lnf_g': 'new_m', 'new_v_ln1_g': 'new_v', 'new_v_w_in': 'new_v', 'new_v_pool_w': 'new_v', 'new_v_pool_scale': 'new_v', 'new_v_conv_w': 'new_v', 'new_v_conv_b': 'new_v', 'new_v_w_a': 'new_v', 'new_v_b_a': 'new_v', 'new_v_w_i': 'new_v', 'new_v_b_i': 'new_v', 'new_v_lam': 'new_v', 'new_v_gn_pool_g': 'new_v', 'new_v_gn_lru_g': 'new_v', 'new_v_w_out': 'new_v', 'new_v_ln2_g': 'new_v', 'new_v_w_ffn_gate': 'new_v', 'new_v_w_ffn_up': 'new_v', 'new_v_w_ffn_down': 'new_v', 'new_v_lnf_g': 'new_v'}


def _forward(args):
    return _fwd_reference(*[args[k] for k in FWD_PARAMS])


def _output_shape():
    def fwd():
        inp = _fwd_setup_inputs(0)
        return _fwd_reference(*[inp[k] for k in FWD_PARAMS])
    out = _jax.eval_shape(fwd)
    return out.shape, out.dtype

N_MICROBATCH = 1
ADAM_LR = 0.001
ADAM_B1 = 0.9
ADAM_B2 = 0.999
ADAM_EPS = 1e-08
ADAM_WD = 0.01
ADAM_STEP = 10
PER_EXAMPLE_BATCH_AXIS = {'x': 0, 'loss_target': 0}
SHARED_INPUTS = []
_WEIGHT_DTYPES = {'ln1_g': _jnp.float32, 'w_in': _jnp.float32, 'pool_w': _jnp.float32, 'pool_scale': _jnp.float32, 'conv_w': _jnp.float32, 'conv_b': _jnp.float32, 'w_a': _jnp.float32, 'b_a': _jnp.float32, 'w_i': _jnp.float32, 'b_i': _jnp.float32, 'lam': _jnp.float32, 'gn_pool_g': _jnp.float32, 'gn_lru_g': _jnp.float32, 'w_out': _jnp.float32, 'ln2_g': _jnp.float32, 'w_ffn_gate': _jnp.float32, 'w_ffn_up': _jnp.float32, 'w_ffn_down': _jnp.float32, 'lnf_g': _jnp.float32}
MOMENT_SCALE = {'ln1_g': 1.667083e-01, 'w_in': 1.373057e-01, 'pool_w': 1.431255e-01, 'pool_scale': 1.643850e-01, 'conv_w': 1.470138e-01, 'conv_b': 1.514937e+00, 'w_a': 4.936875e-02, 'b_a': 4.226489e-02, 'w_i': 8.978898e-02, 'b_i': 4.519685e-02, 'lam': 7.883771e-02, 'gn_pool_g': 1.603620e-01, 'gn_lru_g': 1.649744e-01, 'w_out': 1.491431e-01, 'ln2_g': 1.096070e-01, 'w_ffn_gate': 4.311008e-02, 'w_ffn_up': 4.238134e-02, 'w_ffn_down': 7.048785e-02, 'lnf_g': 3.202264e+01}


def _to_microbatches(a, axis):
    t = _jnp.moveaxis(a, axis, 0)
    t = t.reshape((N_MICROBATCH, t.shape[0] // N_MICROBATCH) + t.shape[1:])
    return _jnp.moveaxis(t, 1, axis + 1)


def setup_inputs(seed: int = 0) -> dict:
    inp = _fwd_setup_inputs(seed)
    key = _jax.random.fold_in(_jax.random.key(seed), 7919)
    shape, _ = _output_shape()
    out = dict(inp)
    out["loss_target"] = _jax.random.normal(_jax.random.fold_in(key, 0), shape, _jnp.float32)
    for i, name in enumerate(TWIN_WEIGHTS):
        w = inp[name].astype(_jnp.float32)
        if MOMENT_SCALE is None:
            s = _jnp.sqrt(_jnp.mean(_jnp.square(w)) + 1e-30)
        else:
            s = MOMENT_SCALE[name]
        km, kv = _jax.random.split(_jax.random.fold_in(key, i + 1))
        out[name] = w
        out["m_" + name] = s * _jax.random.normal(km, w.shape, _jnp.float32)
        out["v_" + name] = (s * s) * _jax.random.uniform(kv, w.shape, _jnp.float32, 0.5, 1.5)
    if N_MICROBATCH > 1:
        for name, axis in PER_EXAMPLE_BATCH_AXIS.items():
            out[name] = _to_microbatches(out[name], axis)
    return {'x': out['x'], 'ln1_g': out['ln1_g'], 'w_in': out['w_in'], 'pool_w': out['pool_w'], 'pool_scale': out['pool_scale'], 'conv_w': out['conv_w'], 'conv_b': out['conv_b'], 'w_a': out['w_a'], 'b_a': out['b_a'], 'w_i': out['w_i'], 'b_i': out['b_i'], 'lam': out['lam'], 'gn_pool_g': out['gn_pool_g'], 'gn_lru_g': out['gn_lru_g'], 'w_out': out['w_out'], 'ln2_g': out['ln2_g'], 'w_ffn_gate': out['w_ffn_gate'], 'w_ffn_up': out['w_ffn_up'], 'w_ffn_down': out['w_ffn_down'], 'lnf_g': out['lnf_g'], 'loss_target': out['loss_target'], 'm_ln1_g': out['m_ln1_g'], 'm_w_in': out['m_w_in'], 'm_pool_w': out['m_pool_w'], 'm_pool_scale': out['m_pool_scale'], 'm_conv_w': out['m_conv_w'], 'm_conv_b': out['m_conv_b'], 'm_w_a': out['m_w_a'], 'm_b_a': out['m_b_a'], 'm_w_i': out['m_w_i'], 'm_b_i': out['m_b_i'], 'm_lam': out['m_lam'], 'm_gn_pool_g': out['m_gn_pool_g'], 'm_gn_lru_g': out['m_gn_lru_g'], 'm_w_out': out['m_w_out'], 'm_ln2_g': out['m_ln2_g'], 'm_w_ffn_gate': out['m_w_ffn_gate'], 'm_w_ffn_up': out['m_w_ffn_up'], 'm_w_ffn_down': out['m_w_ffn_down'], 'm_lnf_g': out['m_lnf_g'], 'v_ln1_g': out['v_ln1_g'], 'v_w_in': out['v_w_in'], 'v_pool_w': out['v_pool_w'], 'v_pool_scale': out['v_pool_scale'], 'v_conv_w': out['v_conv_w'], 'v_conv_b': out['v_conv_b'], 'v_w_a': out['v_w_a'], 'v_b_a': out['v_b_a'], 'v_w_i': out['v_w_i'], 'v_b_i': out['v_b_i'], 'v_lam': out['v_lam'], 'v_gn_pool_g': out['v_gn_pool_g'], 'v_gn_lru_g': out['v_gn_lru_g'], 'v_w_out': out['v_w_out'], 'v_ln2_g': out['v_ln2_g'], 'v_w_ffn_gate': out['v_w_ffn_gate'], 'v_w_ffn_up': out['v_w_ffn_up'], 'v_w_ffn_down': out['v_w_ffn_down'], 'v_lnf_g': out['v_lnf_g']}


def _loss(weights, diff, rest, loss_target):
    with _jax.named_scope("forward"):
        args = {**rest, TWIN_DIFF_INPUT: diff, **{k: w.astype(_WEIGHT_DTYPES[k]) for k, w in weights.items()}}
        y = _forward(args)
    with _jax.named_scope("loss_head"):
        err = _jnp.square(y.astype(_jnp.float32) - loss_target)
        return 0.5 * _jnp.sum(_jnp.mean(err, axis=-1)) if err.ndim else 0.5 * err


def _adamw(w, g, m, v):
    m = ADAM_B1 * m + (1.0 - ADAM_B1) * g
    v = ADAM_B2 * v + (1.0 - ADAM_B2) * _jnp.square(g)
    m_hat = m / (1.0 - ADAM_B1 ** ADAM_STEP)
    v_hat = v / (1.0 - ADAM_B2 ** ADAM_STEP)
    delta = -ADAM_LR * (m_hat / (_jnp.sqrt(v_hat) + ADAM_EPS) + ADAM_WD * w)
    return delta, m, v


def reference(x, ln1_g, w_in, pool_w, pool_scale, conv_w, conv_b, w_a, b_a, w_i, b_i, lam, gn_pool_g, gn_lru_g, w_out, ln2_g, w_ffn_gate, w_ffn_up, w_ffn_down, lnf_g, loss_target, m_ln1_g, m_w_in, m_pool_w, m_pool_scale, m_conv_w, m_conv_b, m_w_a, m_b_a, m_w_i, m_b_i, m_lam, m_gn_pool_g, m_gn_lru_g, m_w_out, m_ln2_g, m_w_ffn_gate, m_w_ffn_up, m_w_ffn_down, m_lnf_g, v_ln1_g, v_w_in, v_pool_w, v_pool_scale, v_conv_w, v_conv_b, v_w_a, v_b_a, v_w_i, v_b_i, v_lam, v_gn_pool_g, v_gn_lru_g, v_w_out, v_ln2_g, v_w_ffn_gate, v_w_ffn_up, v_w_ffn_down, v_lnf_g):
    given = dict(x=x, ln1_g=ln1_g, w_in=w_in, pool_w=pool_w, pool_scale=pool_scale, conv_w=conv_w, conv_b=conv_b, w_a=w_a, b_a=b_a, w_i=w_i, b_i=b_i, lam=lam, gn_pool_g=gn_pool_g, gn_lru_g=gn_lru_g, w_out=w_out, ln2_g=ln2_g, w_ffn_gate=w_ffn_gate, w_ffn_up=w_ffn_up, w_ffn_down=w_ffn_down, lnf_g=lnf_g, loss_target=loss_target, m_ln1_g=m_ln1_g, m_w_in=m_w_in, m_pool_w=m_pool_w, m_pool_scale=m_pool_scale, m_conv_w=m_conv_w, m_conv_b=m_conv_b, m_w_a=m_w_a, m_b_a=m_b_a, m_w_i=m_w_i, m_b_i=m_b_i, m_lam=m_lam, m_gn_pool_g=m_gn_pool_g, m_gn_lru_g=m_gn_lru_g, m_w_out=m_w_out, m_ln2_g=m_ln2_g, m_w_ffn_gate=m_w_ffn_gate, m_w_ffn_up=m_w_ffn_up, m_w_ffn_down=m_w_ffn_down, m_lnf_g=m_lnf_g, v_ln1_g=v_ln1_g, v_w_in=v_w_in, v_pool_w=v_pool_w, v_pool_scale=v_pool_scale, v_conv_w=v_conv_w, v_conv_b=v_conv_b, v_w_a=v_w_a, v_b_a=v_b_a, v_w_i=v_w_i, v_b_i=v_b_i, v_lam=v_lam, v_gn_pool_g=v_gn_pool_g, v_gn_lru_g=v_gn_lru_g, v_w_out=v_w_out, v_ln2_g=v_ln2_g, v_w_ffn_gate=v_w_ffn_gate, v_w_ffn_up=v_w_ffn_up, v_w_ffn_down=v_w_ffn_down, v_lnf_g=v_lnf_g)
    weights = {n: given[n] for n in TWIN_WEIGHTS}
    shared = {n: given[n] for n in SHARED_INPUTS}
    per_example = {n: given[n] for n in ['x']}
    grad_fn = _jax.value_and_grad(_loss, argnums=(0, 1))

    def one_microbatch(ex, loss_target):
        ex = dict(ex)
        diff = ex.pop(TWIN_DIFF_INPUT)
        return grad_fn(weights, diff, {**shared, **ex}, loss_target)

    if N_MICROBATCH == 1:
        loss, (grad_w, grad_x) = one_microbatch(per_example, given["loss_target"])
    else:
        def body(carry, xs):
            loss_sum, grad_sum = carry
            l_k, (gw_k, gx_k) = one_microbatch(xs[0], xs[1])
            with _jax.named_scope("update"):
                return (loss_sum + l_k, _jax.tree.map(_jnp.add, grad_sum, gw_k)), gx_k

        init = (_jnp.zeros((), _jnp.float32), _jax.tree.map(_jnp.zeros_like, weights))
        (loss, grad_w), grad_x = _jax.lax.scan(body, init, (per_example, given["loss_target"]))
    with _jax.named_scope("update"):
        delta_w, new_m, new_v = {}, {}, {}
        for n in TWIN_WEIGHTS:
            delta_w[n], new_m[n], new_v[n] = _adamw(weights[n], grad_w[n], given["m_" + n], given["v_" + n])
    return (loss, grad_x, *[grad_w[n] for n in TWIN_WEIGHTS], *[delta_w[n] for n in TWIN_WEIGHTS],
            *[new_m[n] for n in TWIN_WEIGHTS], *[new_v[n] for n in TWIN_WEIGHTS])
```

```python
import functools

import jax
import jax.numpy as jnp
from jax import lax
from jax.experimental import pallas as pl
from jax.experimental.pallas import tpu as pltpu

F32 = jnp.float32
BF16 = jnp.bfloat16

N_DEV = 8
D_MODEL = 1024
POOL_WIDTH = 512
LRU_WIDTH = 512
IN_WIDTH = 1536
D_FF = 2816
N_SLAB = 4
SLAB = 128
CONV_WIDTH = 4
LRU_C = 8.0
EPS = 1e-6
HALO = 16
FF_CHUNK = 256

ADAM_LR = 0.001
ADAM_B1 = 0.9
ADAM_B2 = 0.999
ADAM_EPS = 1e-08
ADAM_WD = 0.01
ADAM_STEP = 10

MIB = 1 << 20
MESH = pl.DeviceIdType.MESH

SMALL_NAMES = ("ln1_g", "pool_w", "pool_scale", "conv_b", "w_a", "b_a", "w_i", "b_i", "lam",
               "gn_pool_g", "gn_lru_g", "ln2_g", "lnf_g")


def _params(vmem_mib, n_axes=0):
    sem = ("arbitrary",) * n_axes if n_axes else None
    return pltpu.CompilerParams(dimension_semantics=sem, vmem_limit_bytes=vmem_mib * MIB)


def _mm(a, b):
    return jnp.dot(a, b, preferred_element_type=F32)


def _mm_nt(a, b):
    return lax.dot_general(a, b, (((1,), (1,)), ((), ())), preferred_element_type=F32)


def _mm_tn(a, b):
    return lax.dot_general(a, b, (((0,), (0,)), ((), ())), preferred_element_type=F32)


def _rms(x, g):
    rstd = lax.rsqrt(jnp.mean(x * x, axis=-1, keepdims=True) + EPS)
    xhat = x * rstd
    return xhat * g, xhat, rstd


def _rms_bwd(dy, xhat, rstd, g):
    gy = dy * g
    dx = rstd * (gy - xhat * jnp.mean(gy * xhat, axis=-1, keepdims=True))
    return dx, jnp.sum(dy * xhat, axis=0, keepdims=True)


def _gelu(z):
    t = jnp.tanh(0.7978845608028654 * (z + 0.044715 * z * z * z))
    return 0.5 * z * (1.0 + t), t


def _gelu_grad(z, t):
    return 0.5 * (1.0 + t) + 0.5 * z * (1.0 - t * t) * 0.7978845608028654 * (1.0 + 3.0 * 0.044715 * z * z)


def _softplus_neg(lam):
    x = -lam
    e = jnp.exp(-jnp.abs(x))
    u = 1.0 + e
    l1p = jnp.where(u == 1.0, e, jnp.log(u) * e / (u - 1.0))
    return jnp.maximum(x, 0.0) + l1p


def _expm1(x):
    p = x * (1.0 + x * (0.5 + x * (1.0 / 6.0 + x * (1.0 / 24.0 + x * (1.0 / 120.0)))))
    return jnp.where(jnp.abs(x) < 0.1, p, jnp.exp(x) - 1.0)


def _down(v, d):
    return pltpu.roll(v, d, 0)


def _up(v, d):
    return pltpu.roll(v, v.shape[0] - d, 0)


def _token_block(s):
    return 512 if s % 512 == 0 and s > 512 else 256


def _time_chunk(s):
    return 256 if s % 256 == 0 else s


def _fwd_in(x, ln1_g, w_in):
    s = x.shape[0]
    tb = _token_block(s)

    def body(x_ref, g_ref, w_ref, up_ref, ul_ref, ug_ref):
        n, _, _ = _rms(x_ref[...], g_ref[...])
        proj = _mm(n.astype(BF16), w_ref[...])
        up_ref[...] = proj[:, :POOL_WIDTH]
        ul_ref[...] = proj[:, POOL_WIDTH:POOL_WIDTH + LRU_WIDTH]
        ug_ref[...] = proj[:, POOL_WIDTH + LRU_WIDTH:]

    out = jax.ShapeDtypeStruct((s, 512), F32)
    return pl.pallas_call(
        body, name="fwd_in", grid=(s // tb,),
        in_specs=[pl.BlockSpec((tb, D_MODEL), lambda i: (i, 0)),
                  pl.BlockSpec((1, D_MODEL), lambda i: (0, 0)),
                  pl.BlockSpec((D_MODEL, IN_WIDTH), lambda i: (0, 0))],
        out_specs=[pl.BlockSpec((tb, 512), lambda i: (i, 0))] * 3,
        out_shape=[out, out, out],
        compiler_params=_params(40, 1),
    )(x, ln1_g, w_in)


def _pool_denominator(t0, row, window):
    return jnp.minimum((t0 + row + 1).astype(F32), window)


def _causal_window(ext, onehot):
    s2 = ext + _down(ext, 1)
    s4 = s2 + _down(s2, 2)
    s8 = s4 + _down(s4, 4)
    s16 = s8 + _down(s8, 8)
    return (onehot[0] * s2 + onehot[1] * s4 + onehot[2] * s8 + onehot[3] * s16)[HALO:]


def _anticausal_window(ext, onehot, rows):
    s2 = ext + _up(ext, 1)
    s4 = s2 + _up(s2, 2)
    s8 = s4 + _up(s4, 4)
    s16 = s8 + _up(s8, 8)
    return (onehot[0] * s2 + onehot[1] * s4 + onehot[2] * s8 + onehot[3] * s16)[:rows]


def _conv_taps(ext):
    return [ext[HALO:], _down(ext, 1)[HALO:], _down(ext, 2)[HALO:], _down(ext, 3)[HALO:]]


def _conv(taps, cw, cb):
    return cw[3:4] * taps[0] + cw[2:3] * taps[1] + cw[1:2] * taps[2] + cw[0:1] * taps[3] + cb


def _lru_gates(xc, wa, ba, wi, bi, sp):
    xb = xc.astype(BF16)
    r = jax.nn.sigmoid(_mm(xb, wa) + ba)
    i = jax.nn.sigmoid(_mm(xb, wi) + bi)
    la = (-LRU_C) * r * sp
    a = jnp.exp(la)
    mult = jnp.sqrt(jnp.maximum(-_expm1(2.0 * la), 0.0))
    return xb, r, i, a, mult


def _scan_causal(a, b, h_prev, row, rows):
    d = 1
    while d < rows:
        head = row < d
        a_sh = jnp.where(head, 1.0, _down(a, d))
        b_sh = jnp.where(head, 0.0, _down(b, d))
        b = a * b_sh + b
        a = a * a_sh
        d *= 2
    return b + a * h_prev


def _scan_anticausal(a, b, l_next, row, rows):
    d = 1
    while d < rows:
        tail = row >= rows - d
        a_sh = jnp.where(tail, 1.0, _up(a, d))
        b_sh = jnp.where(tail, 0.0, _up(b, d))
        b = a * b_sh + b
        a = a * a_sh
        d *= 2
    return b + a * l_next


def _slab_scalars():
    slab = pl.program_id(0)
    onehot = [jnp.where(slab == k, 1.0, 0.0).astype(F32) for k in range(N_SLAB)]
    window = jnp.left_shift(jnp.int32(2), slab).astype(F32)
    return onehot, window


def _slab_specs(s):
    seq = pl.BlockSpec((s, SLAB), lambda k: (0, k))
    mat = pl.BlockSpec((1, SLAB, SLAB), lambda k: (k, 0, 0))
    vec = pl.BlockSpec((1, SLAB), lambda k: (0, k))
    taps = pl.BlockSpec((CONV_WIDTH, SLAB), lambda k: (0, k))
    return seq, mat, vec, taps


def _mixer_fwd(u_pool, u_lru, u_gate, pool_w, pool_scale, conv_w, conv_b, wa_bd, b_a, wi_bd, b_i, lam):
    s = u_pool.shape[0]
    tc = _time_chunk(s)
    n_chunks = s // tc

    def body(up_ref, ul_ref, ug_ref, pw_ref, ps_ref, cw_ref, cb_ref, wa_ref, ba_ref, wi_ref, bi_ref, lam_ref,
             yp_ref, h_ref, yl_ref):
        onehot, window = _slab_scalars()
        pw = pw_ref[0].astype(BF16)
        wa = wa_ref[0].astype(BF16)
        wi = wi_ref[0].astype(BF16)
        ps, cw, cb, ba, bi = ps_ref[...], cw_ref[...], cb_ref[...], ba_ref[...], bi_ref[...]
        sp = _softplus_neg(lam_ref[...])
        row = lax.broadcasted_iota(jnp.int32, (tc, SLAB), 0)

        def chunk(t0, ext_p, ext_l, h_prev):
            rows = pl.ds(t0, tc)
            d = _causal_window(ext_p, onehot) / _pool_denominator(t0, row, window) - ext_p[HALO:]
            yp_ref[rows, :] = _mm(d.astype(BF16), pw) * ps
            xc = _conv(_conv_taps(ext_l), cw, cb)
            _, _, i, a, mult = _lru_gates(xc, wa, ba, wi, bi, sp)
            h = _scan_causal(a, mult * (i * xc), h_prev, row, tc)
            h_ref[rows, :] = h
            yl_ref[rows, :] = h * _gelu(ug_ref[rows, :])[0]
            return h[tc - 1:tc, :]

        pad = jnp.zeros((HALO, SLAB), F32)
        h0 = chunk(0, jnp.concatenate([pad, up_ref[pl.ds(0, tc), :]], axis=0),
                   jnp.concatenate([pad, ul_ref[pl.ds(0, tc), :]], axis=0), jnp.zeros((1, SLAB), F32))

        def step(c, h_prev):
            t0 = pl.multiple_of(c * tc, tc)
            ext = pl.ds(pl.multiple_of(c * tc - HALO, HALO), tc + HALO)
            return chunk(t0, up_ref[ext, :], ul_ref[ext, :], h_prev)

        lax.fori_loop(1, n_chunks, step, h0)

    seq, mat, vec, taps = _slab_specs(s)
    out = jax.ShapeDtypeStruct((s, 512), F32)
    return pl.pallas_call(
        body, name="mixer_fwd", grid=(N_SLAB,),
        in_specs=[seq, seq, seq, mat, vec, taps, vec, mat, vec, mat, vec, vec],
        out_specs=[seq, seq, seq], out_shape=[out, out, out],
        compiler_params=_params(48, 1),
    )(u_pool, u_lru, u_gate, pool_w, pool_scale, conv_w, conv_b, wa_bd, b_a, wi_bd, b_i, lam)


def _fwd_out(x, y_pool, y_lru, gn_pool_g, gn_lru_g, w_out, ln2_g):
    s = x.shape[0]
    tb = _token_block(s)

    def body(x_ref, yp_ref, yl_ref, gp_ref, gl_ref, w_ref, g2_ref, h1_ref, n2_ref):
        mp, _, _ = _rms(yp_ref[...], gp_ref[...])
        ml, _, _ = _rms(yl_ref[...], gl_ref[...])
        h1 = x_ref[...] + _mm(mp.astype(BF16), w_ref[:POOL_WIDTH, :]) + _mm(ml.astype(BF16), w_ref[POOL_WIDTH:, :])
        h1_ref[...] = h1
        n2_ref[...] = _rms(h1, g2_ref[...])[0].astype(BF16)

    row = pl.BlockSpec((tb, D_MODEL), lambda i: (i, 0))
    half = pl.BlockSpec((tb, 512), lambda i: (i, 0))
    return pl.pallas_call(
        body, name="fwd_out", grid=(s // tb,),
        in_specs=[row, half, half, pl.BlockSpec((1, 512), lambda i: (0, 0)), pl.BlockSpec((1, 512), lambda i: (0, 0)),
                  pl.BlockSpec((D_MODEL, D_MODEL), lambda i: (0, 0)), pl.BlockSpec((1, D_MODEL), lambda i: (0, 0))],
        out_specs=[row, row],
        out_shape=[jax.ShapeDtypeStruct((s, D_MODEL), F32), jax.ShapeDtypeStruct((s, D_MODEL), BF16)],
        compiler_params=_params(40, 1),
    )(x, y_pool, y_lru, gn_pool_g, gn_lru_g, w_out, ln2_g)


def _ffn_fwd(h1, n2, target, lnf_g, w_gate, w_up, w_down):
    s = h1.shape[0]
    tb = 256
    n_ff = D_FF // FF_CHUNK

    def body(h1_ref, n2_ref, t_ref, gf_ref, wg_hbm, wu_hbm, wd_hbm,
             g_ref, u_ref, dh_ref, dhb_ref, dgf_ref, sq_ref, wg, wu, wd, sem):
        @pl.when(pl.program_id(0) == 0)
        def _():
            loads = [pltpu.make_async_copy(src, dst, sem.at[k])
                     for k, (src, dst) in enumerate(((wg_hbm, wg), (wu_hbm, wu), (wd_hbm, wd)))]
            for cp in loads:
                cp.start()
            for cp in loads:
                cp.wait()
            dgf_ref[...] = jnp.zeros_like(dgf_ref)
            sq_ref[...] = jnp.zeros_like(sq_ref)

        n2v = n2_ref[...]
        acc = jnp.zeros((tb, D_MODEL), F32)
        for c in range(n_ff):
            cols = slice(c * FF_CHUNK, (c + 1) * FF_CHUNK)
            g = _mm(n2v, wg[:, cols])
            u = _mm(n2v, wu[:, cols])
            g_ref[:, cols] = g.astype(BF16)
            u_ref[:, cols] = u.astype(BF16)
            act = g * jax.nn.sigmoid(g) * u
            acc = acc + _mm(act.astype(BF16), wd[cols, :])
        gf = gf_ref[...]
        y, xhat, rstd = _rms(h1_ref[...] + acc, gf)
        err = y - t_ref[...]
        sq_ref[...] += jnp.sum(err * err, axis=0, keepdims=True)
        dh2, dgf = _rms_bwd(err * (1.0 / D_MODEL), xhat, rstd, gf)
        dgf_ref[...] += dgf
        dh_ref[...] = dh2
        dhb_ref[...] = dh2.astype(BF16)

    row = pl.BlockSpec((tb, D_MODEL), lambda i: (i, 0))
    ff = pl.BlockSpec((tb, D_FF), lambda i: (i, 0))
    vec = pl.BlockSpec((1, D_MODEL), lambda i: (0, 0))
    anyspace = pl.BlockSpec(memory_space=pl.ANY)
    return pl.pallas_call(
        body, name="ffn_fwd", grid=(s // tb,),
        in_specs=[row, row, row, vec, anyspace, anyspace, anyspace],
        out_specs=[ff, ff, row, row, vec, vec],
        out_shape=[jax.ShapeDtypeStruct((s, D_FF), BF16), jax.ShapeDtypeStruct((s, D_FF), BF16),
                   jax.ShapeDtypeStruct((s, D_MODEL), F32), jax.ShapeDtypeStruct((s, D_MODEL), BF16),
                   jax.ShapeDtypeStruct((1, D_MODEL), F32), jax.ShapeDtypeStruct((1, D_MODEL), F32)],
        scratch_shapes=[pltpu.VMEM((D_MODEL, D_FF), BF16), pltpu.VMEM((D_MODEL, D_FF), BF16),
                        pltpu.VMEM((D_FF, D_MODEL), BF16), pltpu.SemaphoreType.DMA((3,))],
        compiler_params=_params(56, 1),
    )(h1, n2, target, lnf_g, w_gate, w_up, w_down)


def _ffn_bwd(n2, dh2b, g, u, w_gate, w_up, w_down):
    s = n2.shape[0]
    tb = min(1024, s)
    n_ff = D_FF // FF_CHUNK
    n_tb = s // tb

    def body(n2_ref, dh_ref, g_ref, u_ref, wg_ref, wu_ref, wd_ref, dwg_ref, dwu_ref, dwd_ref, dn2_ref, dn2_acc):
        j = pl.program_id(0)
        t = pl.program_id(1)

        @pl.when(t == 0)
        def _():
            dwg_ref[...] = jnp.zeros_like(dwg_ref)
            dwu_ref[...] = jnp.zeros_like(dwu_ref)
            dwd_ref[...] = jnp.zeros_like(dwd_ref)

        n2v = n2_ref[...]
        dh = dh_ref[...]
        gv = g_ref[...].astype(F32)
        uv = u_ref[...].astype(F32)
        sg = jax.nn.sigmoid(gv)
        silu = gv * sg
        dact = _mm_nt(dh, wd_ref[...])
        dub = (dact * silu).astype(BF16)
        dgb = (dact * uv * (sg * (1.0 + gv * (1.0 - sg)))).astype(BF16)
        dwd_ref[...] += _mm_tn((silu * uv).astype(BF16), dh)
        dwg_ref[...] += _mm_tn(n2v, dgb)
        dwu_ref[...] += _mm_tn(n2v, dub)
        part = _mm_nt(dgb, wg_ref[...]) + _mm_nt(dub, wu_ref[...])
        rows = pl.ds(pl.multiple_of(t * tb, tb), tb)

        @pl.when(j == 0)
        def _():
            dn2_acc[rows, :] = part

        @pl.when(jnp.logical_and(j > 0, j < n_ff - 1))
        def _():
            dn2_acc[rows, :] += part

        @pl.when(j == n_ff - 1)
        def _():
            dn2_ref[...] = dn2_acc[rows, :] + part

    row = pl.BlockSpec((tb, D_MODEL), lambda j, t: (t, 0))
    act = pl.BlockSpec((tb, FF_CHUNK), lambda j, t: (t, j))
    w_col = pl.BlockSpec((D_MODEL, FF_CHUNK), lambda j, t: (0, j))
    w_row = pl.BlockSpec((FF_CHUNK, D_MODEL), lambda j, t: (j, 0))
    last = pl.BlockSpec((tb, D_MODEL), lambda j, t: (jnp.where(j == n_ff - 1, t, 0), 0))
    return pl.pallas_call(
        body, name="ffn_bwd", grid=(n_ff, n_tb),
        in_specs=[row, row, act, act, w_col, w_col, w_row],
        out_specs=[w_col, w_col, w_row, last],
        out_shape=[jax.ShapeDtypeStruct((D_MODEL, D_FF), F32), jax.ShapeDtypeStruct((D_MODEL, D_FF), F32),
                   jax.ShapeDtypeStruct((D_FF, D_MODEL), F32), jax.ShapeDtypeStruct((s, D_MODEL), F32)],
        scratch_shapes=[pltpu.VMEM((s, D_MODEL), F32)],
        compiler_params=_params(56, 2),
    )(n2, dh2b, g, u, w_gate, w_up, w_down)


def _bwd_out(dn2, dh2, h1, y_pool, y_lru, gn_pool_g, gn_lru_g, w_out, ln2_g):
    s = h1.shape[0]
    tb = _token_block(s)

    def body(dn2_ref, dh2_ref, h1_ref, yp_ref, yl_ref, gp_ref, gl_ref, w_ref, g2_ref,
             dh1_ref, dyp_ref, dyl_ref, dw_ref, dg2_ref, dgp_ref, dgl_ref):
        @pl.when(pl.program_id(0) == 0)
        def _():
            dw_ref[...] = jnp.zeros_like(dw_ref)
            dg2_ref[...] = jnp.zeros_like(dg2_ref)
            dgp_ref[...] = jnp.zeros_like(dgp_ref)
            dgl_ref[...] = jnp.zeros_like(dgl_ref)

        g2 = g2_ref[...]
        _, xhat2, rstd2 = _rms(h1_ref[...], g2)
        dres, dg2 = _rms_bwd(dn2_ref[...], xhat2, rstd2, g2)
        dg2_ref[...] += dg2
        dh1 = dh2_ref[...] + dres
        dh1_ref[...] = dh1
        dh1b = dh1.astype(BF16)
        gp, gl = gp_ref[...], gl_ref[...]
        mp, xhat_p, rstd_p = _rms(yp_ref[...], gp)
        ml, xhat_l, rstd_l = _rms(yl_ref[...], gl)
        dw_ref[:POOL_WIDTH, :] += _mm_tn(mp.astype(BF16), dh1b)
        dw_ref[POOL_WIDTH:, :] += _mm_tn(ml.astype(BF16), dh1b)
        dyp, dgp = _rms_bwd(_mm_nt(dh1b, w_ref[:POOL_WIDTH, :]), xhat_p, rstd_p, gp)
        dyl, dgl = _rms_bwd(_mm_nt(dh1b, w_ref[POOL_WIDTH:, :]), xhat_l, rstd_l, gl)
        dyp_ref[...] = dyp
        dyl_ref[...] = dyl
        dgp_ref[...] += dgp
        dgl_ref[...] += dgl

    row = pl.BlockSpec((tb, D_MODEL), lambda i: (i, 0))
    half = pl.BlockSpec((tb, 512), lambda i: (i, 0))
    vec = pl.BlockSpec((1, D_MODEL), lambda i: (0, 0))
    hvec = pl.BlockSpec((1, 512), lambda i: (0, 0))
    mat = pl.BlockSpec((D_MODEL, D_MODEL), lambda i: (0, 0))
    return pl.pallas_call(
        body, name="bwd_out", grid=(s // tb,),
        in_specs=[row, row, row, half, half, hvec, hvec, mat, vec],
        out_specs=[row, half, half, mat, vec, hvec, hvec],
        out_shape=[jax.ShapeDtypeStruct((s, D_MODEL), F32), jax.ShapeDtypeStruct((s, 512), F32),
                   jax.ShapeDtypeStruct((s, 512), F32), jax.ShapeDtypeStruct((D_MODEL, D_MODEL), F32),
                   jax.ShapeDtypeStruct((1, D_MODEL), F32), jax.ShapeDtypeStruct((1, 512), F32),
                   jax.ShapeDtypeStruct((1, 512), F32)],
        compiler_params=_params(48, 1),
    )(dn2, dh2, h1, y_pool, y_lru, gn_pool_g, gn_lru_g, w_out, ln2_g)


def _mixer_bwd(u_pool, u_lru, u_gate, h, dy_pool, dy_lru,
               pool_w, pool_scale, conv_w, conv_b, wa_bd, b_a, wi_bd, b_i, lam):
    s = u_pool.shape[0]
    tc = _time_chunk(s)
    n_chunks = s // tc

    def body(up_ref, ul_ref, ug_ref, h_ref, dyp_ref, dyl_ref,
             pw_ref, ps_ref, cw_ref, cb_ref, wa_ref, ba_ref, wi_ref, bi_ref, lam_ref,
             dup_ref, dul_ref, dug_ref, dpw_ref, dps_ref, dcw_ref, dcb_ref, dwa_ref, dba_ref, dwi_ref, dbi_ref, dlam_ref):
        onehot, window = _slab_scalars()
        pw = pw_ref[0].astype(BF16)
        wa = wa_ref[0].astype(BF16)
        wi = wi_ref[0].astype(BF16)
        ps, cw, cb, ba, bi = ps_ref[...], cw_ref[...], cb_ref[...], ba_ref[...], bi_ref[...]
        lam_v = lam_ref[...]
        sp = _softplus_neg(lam_v)
        row = lax.broadcasted_iota(jnp.int32, (tc, SLAB), 0)
        for ref in (dpw_ref, dps_ref, dcw_ref, dcb_ref, dwa_ref, dba_ref, dwi_ref, dbi_ref, dlam_ref):
            ref[...] = jnp.zeros_like(ref)

        def chunk(t0, ext_p, ext_l, ext_h, carry):
            l_next, a_next, dxc_next, ddn_next = carry
            rows = pl.ds(t0, tc)
            taps = _conv_taps(ext_l)
            xc = _conv(taps, cw, cb)
            xb, r, i, a, mult = _lru_gates(xc, wa, ba, wi, bi, sp)
            hv = ext_h[HALO:]
            h_before = _down(ext_h, 1)[HALO:]
            ug = ug_ref[rows, :]
            dyl = dyl_ref[rows, :]
            gel, th = _gelu(ug)
            dug_ref[rows, :] = dyl * hv * _gelu_grad(ug, th)
            a_after = jnp.where(row == tc - 1, a_next, _up(a, 1))
            l = _scan_anticausal(a_after, dyl * gel, l_next, row, tc)
            dmult = l * (i * xc)
            di = l * mult * xc
            dxc = l * mult * i
            dla = (l * h_before) * a - jnp.where(mult > 0.0, dmult * (a * a) / mult, 0.0)
            dlam_ref[...] += jnp.sum(dla * r, axis=0, keepdims=True)
            dpa = (dla * ((-LRU_C) * sp)) * (r * (1.0 - r))
            dpi = di * (i * (1.0 - i))
            dpab = dpa.astype(BF16)
            dpib = dpi.astype(BF16)
            dwa_ref[0] += _mm_tn(xb, dpab)
            dwi_ref[0] += _mm_tn(xb, dpib)
            dba_ref[...] += jnp.sum(dpa, axis=0, keepdims=True)
            dbi_ref[...] += jnp.sum(dpi, axis=0, keepdims=True)
            dxc = dxc + _mm_nt(dpab, wa) + _mm_nt(dpib, wi)
            ext_d = jnp.concatenate([dxc, dxc_next], axis=0)
            dul_ref[rows, :] = (cw[3:4] * dxc + cw[2:3] * _up(ext_d, 1)[:tc]
                                + cw[1:2] * _up(ext_d, 2)[:tc] + cw[0:1] * _up(ext_d, 3)[:tc])
            for k in range(CONV_WIDTH):
                dcw_ref[k:k + 1, :] += jnp.sum(dxc * taps[CONV_WIDTH - 1 - k], axis=0, keepdims=True)
            dcb_ref[...] += jnp.sum(dxc, axis=0, keepdims=True)
            denom = _pool_denominator(t0, row, window)
            db = (_causal_window(ext_p, onehot) / denom - ext_p[HALO:]).astype(BF16)
            dyp = dyp_ref[rows, :]
            dps_ref[...] += jnp.sum(dyp * _mm(db, pw), axis=0, keepdims=True)
            dys = (dyp * ps).astype(BF16)
            dpw_ref[0] += _mm_tn(db, dys)
            dd = _mm_nt(dys, pw)
            ddn = dd / denom
            ext_q = jnp.concatenate([ddn, ddn_next], axis=0)
            dup_ref[rows, :] = _anticausal_window(ext_q, onehot, tc) - dd
            return l[0:1, :], a[0:1, :], dxc[0:8, :], ddn[0:HALO, :]

        def step(k, carry):
            c = n_chunks - 1 - k
            t0 = pl.multiple_of(c * tc, tc)
            ext = pl.ds(pl.multiple_of(c * tc - HALO, HALO), tc + HALO)
            return chunk(t0, up_ref[ext, :], ul_ref[ext, :], h_ref[ext, :], carry)

        carry = (jnp.zeros((1, SLAB), F32), jnp.zeros((1, SLAB), F32),
                 jnp.zeros((8, SLAB), F32), jnp.zeros((HALO, SLAB), F32))
        carry = lax.fori_loop(0, n_chunks - 1, step, carry)
        pad = jnp.zeros((HALO, SLAB), F32)
        first = pl.ds(0, tc)
        chunk(0, jnp.concatenate([pad, up_ref[first, :]], axis=0), jnp.concatenate([pad, ul_ref[first, :]], axis=0),
              jnp.concatenate([pad, h_ref[first, :]], axis=0), carry)
        dlam_ref[...] = dlam_ref[...] * (LRU_C * jax.nn.sigmoid(-lam_v))

    seq, mat, vec, taps = _slab_specs(s)
    full = jax.ShapeDtypeStruct((s, 512), F32)
    mats = jax.ShapeDtypeStruct((N_SLAB, SLAB, SLAB), F32)
    vecs = jax.ShapeDtypeStruct((1, 512), F32)
    return pl.pallas_call(
        body, name="mixer_bwd", grid=(N_SLAB,),
        in_specs=[seq] * 6 + [mat, vec, taps, vec, mat, vec, mat, vec, vec],
        out_specs=[seq, seq, seq, mat, vec, taps, vec, mat, vec, mat, vec, vec],
        out_shape=[full, full, full, mats, vecs, jax.ShapeDtypeStruct((CONV_WIDTH, 512), F32), vecs,
                   mats, vecs, mats, vecs, vecs],
        compiler_params=_params(56, 1),
    )(u_pool, u_lru, u_gate, h, dy_pool, dy_lru, pool_w, pool_scale, conv_w, conv_b, wa_bd, b_a, wi_bd, b_i, lam)


def _bwd_in(x, dh1, du_pool, du_lru, du_gate, ln1_g, w_in):
    s = x.shape[0]
    tb = _token_block(s)

    def body(x_ref, dh1_ref, dup_ref, dul_ref, dug_ref, g_ref, w_ref, dx_ref, dw_ref, dg_ref):
        @pl.when(pl.program_id(0) == 0)
        def _():
            dw_ref[...] = jnp.zeros_like(dw_ref)
            dg_ref[...] = jnp.zeros_like(dg_ref)

        g1 = g_ref[...]
        n, xhat, rstd = _rms(x_ref[...], g1)
        nb = n.astype(BF16)
        dn = jnp.zeros((tb, D_MODEL), F32)
        for k, ref in enumerate((dup_ref, dul_ref, dug_ref)):
            cols = slice(k * 512, (k + 1) * 512)
            db = ref[...].astype(BF16)
            dw_ref[:, cols] += _mm_tn(nb, db)
            dn = dn + _mm_nt(db, w_ref[:, cols])
        dx, dg1 = _rms_bwd(dn, xhat, rstd, g1)
        dx_ref[...] = dh1_ref[...] + dx
        dg_ref[...] += dg1

    row = pl.BlockSpec((tb, D_MODEL), lambda i: (i, 0))
    half = pl.BlockSpec((tb, 512), lambda i: (i, 0))
    vec = pl.BlockSpec((1, D_MODEL), lambda i: (0, 0))
    mat = pl.BlockSpec((D_MODEL, IN_WIDTH), lambda i: (0, 0))
    return pl.pallas_call(
        body, name="bwd_in", grid=(s // tb,),
        in_specs=[row, row, half, half, half, vec, mat],
        out_specs=[row, mat, vec],
        out_shape=[jax.ShapeDtypeStruct((s, D_MODEL), F32), jax.ShapeDtypeStruct((D_MODEL, IN_WIDTH), F32),
                   jax.ShapeDtypeStruct((1, D_MODEL), F32)],
        compiler_params=_params(48, 1),
    )(x, dh1, du_pool, du_lru, du_gate, ln1_g, w_in)


def _mesh_position():
    x, y, c = lax.axis_index("x"), lax.axis_index("y"), lax.axis_index("c")
    return x, y, c, 4 * x + 2 * y + c


def _peer(x, y, c, p):
    px = 1 - x if p & 4 else x
    py = 1 - y if p & 2 else y
    pc = 1 - c if p & 1 else c
    return (px, py, pc), 4 * px + 2 * py + pc


def _all_gather(shards):
    n = len(shards)

    def body(*refs):
        ins, outs = refs[:n], refs[n:2 * n]
        send_sems, recv_sems, local_sems = refs[2 * n:]
        x, y, c, me = _mesh_position()
        local = [pltpu.make_async_copy(ins[m], outs[m].at[me], local_sems.at[m]) for m in range(n)]
        for cp in local:
            cp.start()
        sends = []
        for p in range(1, N_DEV):
            peer, _ = _peer(x, y, c, p)
            for m in range(n):
                cp = pltpu.make_async_remote_copy(ins[m], outs[m].at[me], send_sems.at[m, p], recv_sems.at[m, p],
                                                  device_id=peer, device_id_type=MESH)
                cp.start()
                sends.append(cp)
        for p in range(1, N_DEV):
            peer, peer_index = _peer(x, y, c, p)
            for m in range(n):
                pltpu.make_async_remote_copy(ins[m], outs[m].at[peer_index], send_sems.at[m, p], recv_sems.at[m, p],
                                             device_id=peer, device_id_type=MESH).wait_recv()
        for cp in sends:
            cp.wait_send()
        for cp in local:
            cp.wait()

    hbm = pl.BlockSpec(memory_space=pl.ANY)
    return pl.pallas_call(
        body, name="gather_weights",
        in_specs=[hbm] * n, out_specs=[hbm] * n,
        out_shape=[jax.ShapeDtypeStruct((N_DEV,) + a.shape, a.dtype) for a in shards],
        scratch_shapes=[pltpu.SemaphoreType.DMA((n, N_DEV)), pltpu.SemaphoreType.DMA((n, N_DEV)),
                        pltpu.SemaphoreType.DMA((n,))],
    )(*shards)


def _exchange(stacks):
    n = len(stacks)

    def body(*refs):
        ins, outs = refs[:n], refs[n:2 * n]
        send_sems, recv_sems, local_sems = refs[2 * n:]
        x, y, c, me = _mesh_position()
        local = [pltpu.make_async_copy(ins[m].at[me], outs[m].at[me], local_sems.at[m]) for m in range(n)]
        for cp in local:
            cp.start()
        sends = []
        for p in range(1, N_DEV):
            peer, peer_index = _peer(x, y, c, p)
            for m in range(n):
                cp = pltpu.make_async_remote_copy(ins[m].at[peer_index], outs[m].at[me],
                                                  send_sems.at[m, p], recv_sems.at[m, p],
                                                  device_id=peer, device_id_type=MESH)
                cp.start()
                sends.append(cp)
        for p in range(1, N_DEV):
            peer, peer_index = _peer(x, y, c, p)
            for m in range(n):
                pltpu.make_async_remote_copy(ins[m].at[me], outs[m].at[peer_index],
                                             send_sems.at[m, p], recv_sems.at[m, p],
                                             device_id=peer, device_id_type=MESH).wait_recv()
        for cp in sends:
            cp.wait_send()
        for cp in local:
            cp.wait()

    hbm = pl.BlockSpec(memory_space=pl.ANY)
    return pl.pallas_call(
        body, name="exchange_grads",
        in_specs=[hbm] * n, out_specs=[hbm] * n,
        out_shape=[jax.ShapeDtypeStruct(a.shape, a.dtype) for a in stacks],
        scratch_shapes=[pltpu.SemaphoreType.DMA((n, N_DEV)), pltpu.SemaphoreType.DMA((n, N_DEV)),
                        pltpu.SemaphoreType.DMA((n,))],
    )(*stacks)


def _reduce_adam(parts, w, m, v, name):
    rows, cols = w.shape
    rb = rows
    for cand in (256, 176, 128):
        if rows % cand == 0 and rows > cand:
            rb = cand
            break
    bc1 = 1.0 - ADAM_B1 ** ADAM_STEP
    bc2 = 1.0 - ADAM_B2 ** ADAM_STEP

    def body(p_ref, w_ref, m_ref, v_ref, g_out, d_out, m_out, v_out):
        g = p_ref[0].astype(F32)
        for j in range(1, N_DEV):
            g = g + p_ref[j].astype(F32)
        m_new = ADAM_B1 * m_ref[...] + (1.0 - ADAM_B1) * g
        v_new = ADAM_B2 * v_ref[...] + (1.0 - ADAM_B2) * (g * g)
        g_out[...] = g
        m_out[...] = m_new
        v_out[...] = v_new
        d_out[...] = (-ADAM_LR) * ((m_new / bc1) / (jnp.sqrt(v_new / bc2) + ADAM_EPS) + ADAM_WD * w_ref[...])

    blk = pl.BlockSpec((rb, cols), lambda i: (i, 0))
    out = jax.ShapeDtypeStruct((rows, cols), F32)
    return pl.pallas_call(
        body, name=name, grid=(rows // rb,),
        in_specs=[pl.BlockSpec((N_DEV, rb, cols), lambda i: (0, i, 0)), blk, blk, blk],
        out_specs=[blk] * 4, out_shape=[out] * 4,
        compiler_params=_params(40, 1),
    )(parts, w, m, v)


def _cols_from_stack(stack):
    n, r, c = stack.shape
    return jnp.transpose(stack, (1, 0, 2)).reshape(r, n * c)


def _stack_from_cols(full):
    r, c8 = full.shape
    return jnp.transpose(full.reshape(r, N_DEV, c8 // N_DEV), (1, 0, 2))


def _block_diag(w):
    z = jnp.zeros((N_SLAB, 64, 64), w.dtype)
    pairs = w.reshape(N_SLAB, 2, 64, 64)
    top = jnp.concatenate([pairs[:, 0], z], axis=2)
    bottom = jnp.concatenate([z, pairs[:, 1]], axis=2)
    return jnp.concatenate([top, bottom], axis=1)


def _diag_blocks(w):
    return jnp.stack([w[:, :64, :64], w[:, 64:, 64:]], axis=1).reshape(8, 64, 64)


def _pack(arrays):
    flat = jnp.concatenate([a.reshape(-1) for a in arrays])
    rows = -(-flat.shape[0] // (8 * SLAB)) * 8
    return jnp.pad(flat, (0, rows * SLAB - flat.shape[0])).reshape(rows, SLAB)


def _unpack(packed, like):
    flat = packed.reshape(-1)
    out, at = [], 0
    for a in like:
        out.append(flat[at:at + a.size].reshape(a.shape))
        at += a.size
    return out


def kernel(x, ln1_g, w_in, pool_w, pool_scale, conv_w, conv_b, w_a, b_a, w_i, b_i, lam, gn_pool_g, gn_lru_g, w_out, ln2_g, w_ffn_gate, w_ffn_up, w_ffn_down, lnf_g, loss_target, m_ln1_g, m_w_in, m_pool_w, m_pool_scale, m_conv_w, m_conv_b, m_w_a, m_b_a, m_w_i, m_b_i, m_lam, m_gn_pool_g, m_gn_lru_g, m_w_out, m_ln2_g, m_w_ffn_gate, m_w_ffn_up, m_w_ffn_down, m_lnf_g, v_ln1_g, v_w_in, v_pool_w, v_pool_scale, v_conv_w, v_conv_b, v_w_a, v_b_a, v_w_i, v_b_i, v_lam, v_gn_pool_g, v_gn_lru_g, v_w_out, v_ln2_g, v_w_ffn_gate, v_w_ffn_up, v_w_ffn_down, v_lnf_g):
    weights = dict(ln1_g=ln1_g, w_in=w_in, pool_w=pool_w, pool_scale=pool_scale, conv_w=conv_w, conv_b=conv_b,
                   w_a=w_a, b_a=b_a, w_i=w_i, b_i=b_i, lam=lam, gn_pool_g=gn_pool_g, gn_lru_g=gn_lru_g,
                   w_out=w_out, ln2_g=ln2_g, w_ffn_gate=w_ffn_gate, w_ffn_up=w_ffn_up, w_ffn_down=w_ffn_down,
                   lnf_g=lnf_g)
    mom1 = dict(ln1_g=m_ln1_g, w_in=m_w_in, pool_w=m_pool_w, pool_scale=m_pool_scale, conv_w=m_conv_w,
                conv_b=m_conv_b, w_a=m_w_a, b_a=m_b_a, w_i=m_w_i, b_i=m_b_i, lam=m_lam, gn_pool_g=m_gn_pool_g,
                gn_lru_g=m_gn_lru_g, w_out=m_w_out, ln2_g=m_ln2_g, w_ffn_gate=m_w_ffn_gate,
                w_ffn_up=m_w_ffn_up, w_ffn_down=m_w_ffn_down, lnf_g=m_lnf_g)
    mom2 = dict(ln1_g=v_ln1_g, w_in=v_w_in, pool_w=v_pool_w, pool_scale=v_pool_scale, conv_w=v_conv_w,
                conv_b=v_conv_b, w_a=v_w_a, b_a=v_b_a, w_i=v_w_i, b_i=v_b_i, lam=v_lam, gn_pool_g=v_gn_pool_g,
                gn_lru_g=v_gn_lru_g, w_out=v_w_out, ln2_g=v_ln2_g, w_ffn_gate=v_w_ffn_gate,
                w_ffn_up=v_w_ffn_up, w_ffn_down=v_w_ffn_down, lnf_g=v_lnf_g)

    xs = x[0]
    target = loss_target[0]

    g_in, g_out, g_gate, g_up, g_down, g_conv = _all_gather(
        [w_in[0].astype(BF16), w_out[0].astype(BF16), w_ffn_gate[0].astype(BF16), w_ffn_up[0].astype(BF16),
         w_ffn_down[0].astype(BF16), conv_w[0]])
    w_in_f = _cols_from_stack(g_in)
    w_out_f = g_out.reshape(D_MODEL, D_MODEL)
    w_gate_f = _cols_from_stack(g_gate)
    w_up_f = _cols_from_stack(g_up)
    w_down_f = g_down.reshape(D_FF, D_MODEL)
    conv_w_f = _cols_from_stack(g_conv)

    wa_bd = _block_diag(w_a[0])
    wi_bd = _block_diag(w_i[0])
    lnf_row = lnf_g.reshape(1, D_MODEL)

    u_pool, u_lru, u_gate = _fwd_in(xs, ln1_g, w_in_f)
    y_pool, h, y_lru = _mixer_fwd(u_pool, u_lru, u_gate, pool_w[0], pool_scale, conv_w_f, conv_b,
                                  wa_bd, b_a, wi_bd, b_i, lam)
    h1, n2 = _fwd_out(xs, y_pool, y_lru, gn_pool_g, gn_lru_g, w_out_f, ln2_g)
    g_act, u_act, dh2, dh2b, d_lnf, sq = _ffn_fwd(h1, n2, target, lnf_row, w_gate_f, w_up_f, w_down_f)
    loss = lax.psum((0.5 / D_MODEL) * jnp.sum(sq), ("x", "y", "c"))

    d_gate, d_up, d_down, dn2 = _ffn_bwd(n2, dh2b, g_act, u_act, w_gate_f, w_up_f, w_down_f)
    dh1, dy_pool, dy_lru, d_out, d_ln2, d_gnp, d_gnl = _bwd_out(dn2, dh2, h1, y_pool, y_lru, gn_pool_g, gn_lru_g,
                                                                 w_out_f, ln2_g)
    (du_pool, du_lru, du_gate, d_pw, d_ps, d_cw, d_cb, d_wa, d_ba, d_wi, d_bi, d_lam) = _mixer_bwd(
        u_pool, u_lru, u_gate, h, dy_pool, dy_lru, pool_w[0], pool_scale, conv_w_f, conv_b,
        wa_bd, b_a, wi_bd, b_i, lam)
    grad_x, d_in, d_ln1 = _bwd_in(xs, dh1, du_pool, du_lru, du_gate, ln1_g, w_in_f)

    small_grads = dict(ln1_g=d_ln1, pool_w=d_pw[None], pool_scale=d_ps, conv_b=d_cb, w_a=_diag_blocks(d_wa)[None],
                       b_a=d_ba, w_i=_diag_blocks(d_wi)[None], b_i=d_bi, lam=d_lam, gn_pool_g=d_gnp,
                       gn_lru_g=d_gnl, ln2_g=d_ln2, lnf_g=d_lnf.reshape(D_MODEL))
    small_like = [weights[k] for k in SMALL_NAMES]
    packed = _pack([small_grads[k] for k in SMALL_NAMES])

    stacks = [_stack_from_cols(d_in).astype(BF16),
              d_out.reshape(N_DEV, D_MODEL // N_DEV, D_MODEL).astype(BF16),
              _stack_from_cols(d_gate).astype(BF16),
              _stack_from_cols(d_up).astype(BF16),
              d_down.reshape(N_DEV, D_FF // N_DEV, D_MODEL).astype(BF16),
              _stack_from_cols(d_cw),
              jnp.broadcast_to(packed[None], (N_DEV,) + packed.shape)]
    r_in, r_out, r_gate, r_up, r_down, r_conv, r_small = _exchange(stacks)

    results = {}
    for name, parts in (("w_in", r_in), ("w_out", r_out), ("w_ffn_gate", r_gate), ("w_ffn_up", r_up),
                        ("w_ffn_down", r_down), ("conv_w", r_conv)):
        g, d, m1, m2 = _reduce_adam(parts, weights[name][0], mom1[name][0], mom2[name][0], "adam_" + name)
        results[name] = (g[None], d[None], m1[None], m2[None])
    sg, sd, sm1, sm2 = _reduce_adam(r_small, _pack(small_like), _pack([mom1[k] for k in SMALL_NAMES]),
                                    _pack([mom2[k] for k in SMALL_NAMES]), "adam_small")
    for k, g, d, m1, m2 in zip(SMALL_NAMES, _unpack(sg, small_like), _unpack(sd, small_like),
                               _unpack(sm1, small_like), _unpack(sm2, small_like)):
        results[k] = (g, d, m1, m2)

    order = ["ln1_g", "w_in", "pool_w", "pool_scale", "conv_w", "conv_b", "w_a", "b_a", "w_i", "b_i", "lam",
             "gn_pool_g", "gn_lru_g", "w_out", "ln2_g", "w_ffn_gate", "w_ffn_up", "w_ffn_down", "lnf_g"]
    return (loss, grad_x[None],
            *[results[k][0] for k in order], *[results[k][1] for k in order],
            *[results[k][2] for k in order], *[results[k][3] for k in order])
```

```python
import functools

import jax
import jax.numpy as jnp
from jax import lax
from jax.experimental import pallas as pl
from jax.experimental.pallas import tpu as pltpu

F32 = jnp.float32
BF16 = jnp.bfloat16

N_DEV = 8
D_MODEL = 1024
POOL_WIDTH = 512
LRU_WIDTH = 512
IN_WIDTH = 1536
D_FF = 2816
N_SLAB = 4
SLAB = 128
CONV_WIDTH = 4
LRU_C = 8.0
EPS = 1e-6
HALO = 16
FF_CHUNK = 256

ADAM_LR = 0.001
ADAM_B1 = 0.9
ADAM_B2 = 0.999
ADAM_EPS = 1e-08
ADAM_WD = 0.01
ADAM_STEP = 10

MIB = 1 << 20
MESH = pl.DeviceIdType.MESH

SMALL_NAMES = ("ln1_g", "pool_w", "pool_scale", "conv_b", "w_a", "b_a", "w_i", "b_i", "lam",
               "gn_pool_g", "gn_lru_g", "ln2_g", "lnf_g")


def _params(vmem_mib, n_axes=0):
    sem = ("arbitrary",) * n_axes if n_axes else None
    return pltpu.CompilerParams(dimension_semantics=sem, vmem_limit_bytes=vmem_mib * MIB)


def _mm(a, b):
    return jnp.dot(a, b, preferred_element_type=F32)


def _mm_nt(a, b):
    return lax.dot_general(a, b, (((1,), (1,)), ((), ())), preferred_element_type=F32)


def _mm_tn(a, b):
    return lax.dot_general(a, b, (((0,), (0,)), ((), ())), preferred_element_type=F32)


def _rms(x, g):
    rstd = lax.rsqrt(jnp.mean(x * x, axis=-1, keepdims=True) + EPS)
    xhat = x * rstd
    return xhat * g, xhat, rstd


def _rms_bwd(dy, xhat, rstd, g):
    gy = dy * g
    dx = rstd * (gy - xhat * jnp.mean(gy * xhat, axis=-1, keepdims=True))
    return dx, jnp.sum(dy * xhat, axis=0, keepdims=True)


def _gelu(z):
    t = jnp.tanh(0.7978845608028654 * (z + 0.044715 * z * z * z))
    return 0.5 * z * (1.0 + t), t


def _gelu_grad(z, t):
    return 0.5 * (1.0 + t) + 0.5 * z * (1.0 - t * t) * 0.7978845608028654 * (1.0 + 3.0 * 0.044715 * z * z)


def _softplus_neg(lam):
    x = -lam
    e = jnp.exp(-jnp.abs(x))
    u = 1.0 + e
    l1p = jnp.where(u == 1.0, e, jnp.log(u) * e / (u - 1.0))
    return jnp.maximum(x, 0.0) + l1p


def _expm1(x):
    p = x * (1.0 + x * (0.5 + x * (1.0 / 6.0 + x * (1.0 / 24.0 + x * (1.0 / 120.0)))))
    return jnp.where(jnp.abs(x) < 0.1, p, jnp.exp(x) - 1.0)


def _down(v, d):
    return pltpu.roll(v, d, 0)


def _up(v, d):
    return pltpu.roll(v, v.shape[0] - d, 0)


def _token_block(s):
    return 512 if s % 512 == 0 and s > 512 else 256


def _time_chunk(s):
    return 256 if s % 256 == 0 else s


def _fwd_in(x, ln1_g, w_in_t):
    s = x.shape[0]
    tb = _token_block(s)

    def body(x_ref, g_ref, w_ref, up_ref, ul_ref, ug_ref):
        n, _, _ = _rms(x_ref[...], g_ref[...])
        proj = _mm_nt(n.astype(BF16), w_ref[...])
        up_ref[...] = proj[:, :POOL_WIDTH]
        ul_ref[...] = proj[:, POOL_WIDTH:POOL_WIDTH + LRU_WIDTH]
        ug_ref[...] = proj[:, POOL_WIDTH + LRU_WIDTH:]

    out = jax.ShapeDtypeStruct((s, 512), F32)
    return pl.pallas_call(
        body, name="fwd_in", grid=(s // tb,),
        in_specs=[pl.BlockSpec((tb, D_MODEL), lambda i: (i, 0)),
                  pl.BlockSpec((1, D_MODEL), lambda i: (0, 0)),
                  pl.BlockSpec((IN_WIDTH, D_MODEL), lambda i: (0, 0))],
        out_specs=[pl.BlockSpec((tb, 512), lambda i: (i, 0))] * 3,
        out_shape=[out, out, out],
        compiler_params=_params(40, 1),
    )(x, ln1_g, w_in_t)


def _pool_denominator(t0, row, window):
    return jnp.minimum((t0 + row + 1).astype(F32), window)


def _causal_window(ext, onehot):
    s2 = ext + _down(ext, 1)
    s4 = s2 + _down(s2, 2)
    s8 = s4 + _down(s4, 4)
    s16 = s8 + _down(s8, 8)
    return (onehot[0] * s2 + onehot[1] * s4 + onehot[2] * s8 + onehot[3] * s16)[HALO:]


def _anticausal_window(ext, onehot, rows):
    s2 = ext + _up(ext, 1)
    s4 = s2 + _up(s2, 2)
    s8 = s4 + _up(s4, 4)
    s16 = s8 + _up(s8, 8)
    return (onehot[0] * s2 + onehot[1] * s4 + onehot[2] * s8 + onehot[3] * s16)[:rows]


def _conv_taps(ext):
    return [ext[HALO:], _down(ext, 1)[HALO:], _down(ext, 2)[HALO:], _down(ext, 3)[HALO:]]


def _conv(taps, cw, cb):
    return cw[3:4] * taps[0] + cw[2:3] * taps[1] + cw[1:2] * taps[2] + cw[0:1] * taps[3] + cb


def _lru_gates(xc, wa, ba, wi, bi, sp):
    xb = xc.astype(BF16)
    r = jax.nn.sigmoid(_mm(xb, wa) + ba)
    i = jax.nn.sigmoid(_mm(xb, wi) + bi)
    la = (-LRU_C) * r * sp
    a = jnp.exp(la)
    mult = jnp.sqrt(jnp.maximum(-_expm1(2.0 * la), 0.0))
    return xb, r, i, a, mult


def _scan_causal(a, b, h_prev, row, rows):
    d = 1
    while d < rows:
        head = row < d
        a_sh = jnp.where(head, 1.0, _down(a, d))
        b_sh = jnp.where(head, 0.0, _down(b, d))
        b = a * b_sh + b
        a = a * a_sh
        d *= 2
    return b + a * h_prev


def _scan_anticausal(a, b, l_next, row, rows):
    d = 1
    while d < rows:
        tail = row >= rows - d
        a_sh = jnp.where(tail, 1.0, _up(a, d))
        b_sh = jnp.where(tail, 0.0, _up(b, d))
        b = a * b_sh + b
        a = a * a_sh
        d *= 2
    return b + a * l_next


def _slab_scalars():
    slab = pl.program_id(0)
    onehot = [jnp.where(slab == k, 1.0, 0.0).astype(F32) for k in range(N_SLAB)]
    window = jnp.left_shift(jnp.int32(2), slab).astype(F32)
    return onehot, window


def _slab_specs(s):
    seq = pl.BlockSpec((s, SLAB), lambda k: (0, k))
    mat = pl.BlockSpec((1, SLAB, SLAB), lambda k: (k, 0, 0))
    vec = pl.BlockSpec((1, SLAB), lambda k: (0, k))
    taps = pl.BlockSpec((CONV_WIDTH, SLAB), lambda k: (0, k))
    return seq, mat, vec, taps


def _mixer_fwd(u_pool, u_lru, u_gate, pool_w, pool_scale, conv_w, conv_b, wa_bd, b_a, wi_bd, b_i, lam):
    s = u_pool.shape[0]
    tc = _time_chunk(s)
    n_chunks = s // tc

    def body(up_ref, ul_ref, ug_ref, pw_ref, ps_ref, cw_ref, cb_ref, wa_ref, ba_ref, wi_ref, bi_ref, lam_ref,
             yp_ref, h_ref, yl_ref):
        onehot, window = _slab_scalars()
        pw = pw_ref[0].astype(BF16)
        wa = wa_ref[0].astype(BF16)
        wi = wi_ref[0].astype(BF16)
        ps, cw, cb, ba, bi = ps_ref[...], cw_ref[...], cb_ref[...], ba_ref[...], bi_ref[...]
        sp = _softplus_neg(lam_ref[...])
        row = lax.broadcasted_iota(jnp.int32, (tc, SLAB), 0)

        def chunk(t0, ext_p, ext_l, h_prev):
            rows = pl.ds(t0, tc)
            d = _causal_window(ext_p, onehot) / _pool_denominator(t0, row, window) - ext_p[HALO:]
            yp_ref[rows, :] = _mm(d.astype(BF16), pw) * ps
            xc = _conv(_conv_taps(ext_l), cw, cb)
            _, _, i, a, mult = _lru_gates(xc, wa, ba, wi, bi, sp)
            h = _scan_causal(a, mult * (i * xc), h_prev, row, tc)
            h_ref[rows, :] = h
            yl_ref[rows, :] = h * _gelu(ug_ref[rows, :])[0]
            return h[tc - 1:tc, :]

        pad = jnp.zeros((HALO, SLAB), F32)
        h0 = chunk(0, jnp.concatenate([pad, up_ref[pl.ds(0, tc), :]], axis=0),
                   jnp.concatenate([pad, ul_ref[pl.ds(0, tc), :]], axis=0), jnp.zeros((1, SLAB), F32))

        def step(c, h_prev):
            t0 = pl.multiple_of(c * tc, tc)
            ext = pl.ds(pl.multiple_of(c * tc - HALO, HALO), tc + HALO)
            return chunk(t0, up_ref[ext, :], ul_ref[ext, :], h_prev)

        lax.fori_loop(1, n_chunks, step, h0)

    seq, mat, vec, taps = _slab_specs(s)
    out = jax.ShapeDtypeStruct((s, 512), F32)
    return pl.pallas_call(
        body, name="mixer_fwd", grid=(N_SLAB,),
        in_specs=[seq, seq, seq, mat, vec, taps, vec, mat, vec, mat, vec, vec],
        out_specs=[seq, seq, seq], out_shape=[out, out, out],
        compiler_params=_params(48, 1),
    )(u_pool, u_lru, u_gate, pool_w, pool_scale, conv_w, conv_b, wa_bd, b_a, wi_bd, b_i, lam)


def _fwd_out(x, y_pool, y_lru, gn_pool_g, gn_lru_g, w_out, ln2_g):
    s = x.shape[0]
    tb = _token_block(s)

    def body(x_ref, yp_ref, yl_ref, gp_ref, gl_ref, w_ref, g2_ref, h1_ref, n2_ref):
        mp, _, _ = _rms(yp_ref[...], gp_ref[...])
        ml, _, _ = _rms(yl_ref[...], gl_ref[...])
        h1 = x_ref[...] + _mm(mp.astype(BF16), w_ref[:POOL_WIDTH, :]) + _mm(ml.astype(BF16), w_ref[POOL_WIDTH:, :])
        h1_ref[...] = h1
        n2_ref[...] = _rms(h1, g2_ref[...])[0].astype(BF16)

    row = pl.BlockSpec((tb, D_MODEL), lambda i: (i, 0))
    half = pl.BlockSpec((tb, 512), lambda i: (i, 0))
    return pl.pallas_call(
        body, name="fwd_out", grid=(s // tb,),
        in_specs=[row, half, half, pl.BlockSpec((1, 512), lambda i: (0, 0)), pl.BlockSpec((1, 512), lambda i: (0, 0)),
                  pl.BlockSpec((D_MODEL, D_MODEL), lambda i: (0, 0)), pl.BlockSpec((1, D_MODEL), lambda i: (0, 0))],
        out_specs=[row, row],
        out_shape=[jax.ShapeDtypeStruct((s, D_MODEL), F32), jax.ShapeDtypeStruct((s, D_MODEL), BF16)],
        compiler_params=_params(40, 1),
    )(x, y_pool, y_lru, gn_pool_g, gn_lru_g, w_out, ln2_g)


def _ffn_fwd(h1, n2, target, lnf_g, w_gate, w_up, w_down):
    s = h1.shape[0]
    tb = 256
    n_ff = D_FF // FF_CHUNK

    def body(h1_ref, n2_ref, t_ref, gf_ref, wg_hbm, wu_hbm, wd_hbm,
             g_ref, u_ref, dh_ref, dhb_ref, dgf_ref, sq_ref, wg, wu, wd, sem):
        @pl.when(pl.program_id(0) == 0)
        def _():
            loads = [pltpu.make_async_copy(src, dst, sem.at[k])
                     for k, (src, dst) in enumerate(((wg_hbm, wg), (wu_hbm, wu), (wd_hbm, wd)))]
            for cp in loads:
                cp.start()
            for cp in loads:
                cp.wait()
            dgf_ref[...] = jnp.zeros_like(dgf_ref)
            sq_ref[...] = jnp.zeros_like(sq_ref)

        n2v = n2_ref[...]
        acc = jnp.zeros((tb, D_MODEL), F32)
        for c in range(n_ff):
            cols = slice(c * FF_CHUNK, (c + 1) * FF_CHUNK)
            g = _mm_nt(n2v, wg[cols, :])
            u = _mm_nt(n2v, wu[cols, :])
            g_ref[:, cols] = g.astype(BF16)
            u_ref[:, cols] = u.astype(BF16)
            act = g * jax.nn.sigmoid(g) * u
            acc = acc + _mm(act.astype(BF16), wd[cols, :])
        gf = gf_ref[...]
        y, xhat, rstd = _rms(h1_ref[...] + acc, gf)
        err = y - t_ref[...]
        sq_ref[...] += jnp.sum(err * err, axis=0, keepdims=True)
        dh2, dgf = _rms_bwd(err * (1.0 / D_MODEL), xhat, rstd, gf)
        dgf_ref[...] += dgf
        dh_ref[...] = dh2
        dhb_ref[...] = dh2.astype(BF16)

    row = pl.BlockSpec((tb, D_MODEL), lambda i: (i, 0))
    ff = pl.BlockSpec((tb, D_FF), lambda i: (i, 0))
    vec = pl.BlockSpec((1, D_MODEL), lambda i: (0, 0))
    anyspace = pl.BlockSpec(memory_space=pl.ANY)
    return pl.pallas_call(
        body, name="ffn_fwd", grid=(s // tb,),
        in_specs=[row, row, row, vec, anyspace, anyspace, anyspace],
        out_specs=[ff, ff, row, row, vec, vec],
        out_shape=[jax.ShapeDtypeStruct((s, D_FF), BF16), jax.ShapeDtypeStruct((s, D_FF), BF16),
                   jax.ShapeDtypeStruct((s, D_MODEL), F32), jax.ShapeDtypeStruct((s, D_MODEL), BF16),
                   jax.ShapeDtypeStruct((1, D_MODEL), F32), jax.ShapeDtypeStruct((1, D_MODEL), F32)],
        scratch_shapes=[pltpu.VMEM((D_FF, D_MODEL), BF16), pltpu.VMEM((D_FF, D_MODEL), BF16),
                        pltpu.VMEM((D_FF, D_MODEL), BF16), pltpu.SemaphoreType.DMA((3,))],
        compiler_params=_params(56, 1),
    )(h1, n2, target, lnf_g, w_gate, w_up, w_down)


def _ffn_bwd(n2, dh2b, g, u, w_gate_t, w_up_t, w_down):
    s = n2.shape[0]
    tb = min(1024, s)
    n_ff = D_FF // FF_CHUNK
    n_tb = s // tb

    def body(n2_ref, dh_ref, g_ref, u_ref, wg_ref, wu_ref, wd_ref, dwg_ref, dwu_ref, dwd_ref, dn2_ref,
             dn2_acc, acc_g, acc_u, acc_d):
        j = pl.program_id(0)
        t = pl.program_id(1)

        @pl.when(t == 0)
        def _():
            acc_g[...] = jnp.zeros_like(acc_g)
            acc_u[...] = jnp.zeros_like(acc_u)
            acc_d[...] = jnp.zeros_like(acc_d)

        n2v = n2_ref[...]
        dh = dh_ref[...]
        gv = g_ref[...].astype(F32)
        uv = u_ref[...].astype(F32)
        sg = jax.nn.sigmoid(gv)
        silu = gv * sg
        dact = _mm_nt(dh, wd_ref[...])
        dub = (dact * silu).astype(BF16)
        dgb = (dact * uv * (sg * (1.0 + gv * (1.0 - sg)))).astype(BF16)
        acc_d[...] += _mm_tn((silu * uv).astype(BF16), dh)
        acc_g[...] += _mm_tn(dgb, n2v)
        acc_u[...] += _mm_tn(dub, n2v)
        part = _mm(dgb, wg_ref[...]) + _mm(dub, wu_ref[...])
        rows = pl.ds(pl.multiple_of(t * tb, tb), tb)

        @pl.when(t == n_tb - 1)
        def _():
            dwg_ref[...] = acc_g[...].astype(BF16)
            dwu_ref[...] = acc_u[...].astype(BF16)
            dwd_ref[...] = acc_d[...].astype(BF16)

        @pl.when(j == 0)
        def _():
            dn2_acc[rows, :] = part

        @pl.when(jnp.logical_and(j > 0, j < n_ff - 1))
        def _():
            dn2_acc[rows, :] += part

        @pl.when(j == n_ff - 1)
        def _():
            dn2_ref[...] = dn2_acc[rows, :] + part

    row = pl.BlockSpec((tb, D_MODEL), lambda j, t: (t, 0))
    act = pl.BlockSpec((tb, FF_CHUNK), lambda j, t: (t, j))
    w_row = pl.BlockSpec((FF_CHUNK, D_MODEL), lambda j, t: (j, 0))
    last = pl.BlockSpec((tb, D_MODEL), lambda j, t: (jnp.where(j == n_ff - 1, t, 0), 0))
    grad = jax.ShapeDtypeStruct((D_FF, D_MODEL), BF16)
    chunk_acc = pltpu.VMEM((FF_CHUNK, D_MODEL), F32)
    return pl.pallas_call(
        body, name="ffn_bwd", grid=(n_ff, n_tb),
        in_specs=[row, row, act, act, w_row, w_row, w_row],
        out_specs=[w_row, w_row, w_row, last],
        out_shape=[grad, grad, grad, jax.ShapeDtypeStruct((s, D_MODEL), F32)],
        scratch_shapes=[pltpu.VMEM((s, D_MODEL), F32), chunk_acc, chunk_acc, chunk_acc],
        compiler_params=_params(56, 2),
    )(n2, dh2b, g, u, w_gate_t, w_up_t, w_down)


def _bwd_out(dn2, dh2, h1, y_pool, y_lru, gn_pool_g, gn_lru_g, w_out, ln2_g):
    s = h1.shape[0]
    tb = _token_block(s)

    def body(dn2_ref, dh2_ref, h1_ref, yp_ref, yl_ref, gp_ref, gl_ref, w_ref, g2_ref,
             dh1_ref, dyp_ref, dyl_ref, dwb_ref, dg2_ref, dgp_ref, dgl_ref, dw_ref):
        @pl.when(pl.program_id(0) == 0)
        def _():
            dw_ref[...] = jnp.zeros_like(dw_ref)
            dg2_ref[...] = jnp.zeros_like(dg2_ref)
            dgp_ref[...] = jnp.zeros_like(dgp_ref)
            dgl_ref[...] = jnp.zeros_like(dgl_ref)

        g2 = g2_ref[...]
        _, xhat2, rstd2 = _rms(h1_ref[...], g2)
        dres, dg2 = _rms_bwd(dn2_ref[...], xhat2, rstd2, g2)
        dg2_ref[...] += dg2
        dh1 = dh2_ref[...] + dres
        dh1_ref[...] = dh1
        dh1b = dh1.astype(BF16)
        gp, gl = gp_ref[...], gl_ref[...]
        mp, xhat_p, rstd_p = _rms(yp_ref[...], gp)
        ml, xhat_l, rstd_l = _rms(yl_ref[...], gl)
        dw_ref[:POOL_WIDTH, :] += _mm_tn(mp.astype(BF16), dh1b)
        dw_ref[POOL_WIDTH:, :] += _mm_tn(ml.astype(BF16), dh1b)
        dyp, dgp = _rms_bwd(_mm_nt(dh1b, w_ref[:POOL_WIDTH, :]), xhat_p, rstd_p, gp)
        dyl, dgl = _rms_bwd(_mm_nt(dh1b, w_ref[POOL_WIDTH:, :]), xhat_l, rstd_l, gl)
        dyp_ref[...] = dyp
        dyl_ref[...] = dyl
        dgp_ref[...] += dgp
        dgl_ref[...] += dgl

        @pl.when(pl.program_id(0) == s // tb - 1)
        def _():
            dwb_ref[...] = dw_ref[...].astype(BF16)

    row = pl.BlockSpec((tb, D_MODEL), lambda i: (i, 0))
    half = pl.BlockSpec((tb, 512), lambda i: (i, 0))
    vec = pl.BlockSpec((1, D_MODEL), lambda i: (0, 0))
    hvec = pl.BlockSpec((1, 512), lambda i: (0, 0))
    mat = pl.BlockSpec((D_MODEL, D_MODEL), lambda i: (0, 0))
    return pl.pallas_call(
        body, name="bwd_out", grid=(s // tb,),
        in_specs=[row, row, row, half, half, hvec, hvec, mat, vec],
        out_specs=[row, half, half, mat, vec, hvec, hvec],
        out_shape=[jax.ShapeDtypeStruct((s, D_MODEL), F32), jax.ShapeDtypeStruct((s, 512), F32),
                   jax.ShapeDtypeStruct((s, 512), F32), jax.ShapeDtypeStruct((D_MODEL, D_MODEL), BF16),
                   jax.ShapeDtypeStruct((1, D_MODEL), F32), jax.ShapeDtypeStruct((1, 512), F32),
                   jax.ShapeDtypeStruct((1, 512), F32)],
        scratch_shapes=[pltpu.VMEM((D_MODEL, D_MODEL), F32)],
        compiler_params=_params(48, 1),
    )(dn2, dh2, h1, y_pool, y_lru, gn_pool_g, gn_lru_g, w_out, ln2_g)


def _mixer_bwd(u_pool, u_lru, u_gate, h, dy_pool, dy_lru,
               pool_w, pool_scale, conv_w, conv_b, wa_bd, b_a, wi_bd, b_i, lam):
    s = u_pool.shape[0]
    tc = _time_chunk(s)
    n_chunks = s // tc

    def body(up_ref, ul_ref, ug_ref, h_ref, dyp_ref, dyl_ref,
             pw_ref, ps_ref, cw_ref, cb_ref, wa_ref, ba_ref, wi_ref, bi_ref, lam_ref,
             dup_ref, dul_ref, dug_ref, dpw_ref, dps_ref, dcw_ref, dcb_ref, dwa_ref, dba_ref, dwi_ref, dbi_ref, dlam_ref):
        onehot, window = _slab_scalars()
        pw = pw_ref[0].astype(BF16)
        wa = wa_ref[0].astype(BF16)
        wi = wi_ref[0].astype(BF16)
        ps, cw, cb, ba, bi = ps_ref[...], cw_ref[...], cb_ref[...], ba_ref[...], bi_ref[...]
        lam_v = lam_ref[...]
        sp = _softplus_neg(lam_v)
        row = lax.broadcasted_iota(jnp.int32, (tc, SLAB), 0)
        for ref in (dpw_ref, dps_ref, dcw_ref, dcb_ref, dwa_ref, dba_ref, dwi_ref, dbi_ref, dlam_ref):
            ref[...] = jnp.zeros_like(ref)

        def chunk(t0, ext_p, ext_l, ext_h, carry):
            l_next, a_next, dxc_next, ddn_next = carry
            rows = pl.ds(t0, tc)
            taps = _conv_taps(ext_l)
            xc = _conv(taps, cw, cb)
            xb, r, i, a, mult = _lru_gates(xc, wa, ba, wi, bi, sp)
            hv = ext_h[HALO:]
            h_before = _down(ext_h, 1)[HALO:]
            ug = ug_ref[rows, :]
            dyl = dyl_ref[rows, :]
            gel, th = _gelu(ug)
            dug_ref[rows, :] = dyl * hv * _gelu_grad(ug, th)
            a_after = jnp.where(row == tc - 1, a_next, _up(a, 1))
            l = _scan_anticausal(a_after, dyl * gel, l_next, row, tc)
            dmult = l * (i * xc)
            di = l * mult * xc
            dxc = l * mult * i
            dla = (l * h_before) * a - jnp.where(mult > 0.0, dmult * (a * a) / mult, 0.0)
            dlam_ref[...] += jnp.sum(dla * r, axis=0, keepdims=True)
            dpa = (dla * ((-LRU_C) * sp)) * (r * (1.0 - r))
            dpi = di * (i * (1.0 - i))
            dpab = dpa.astype(BF16)
            dpib = dpi.astype(BF16)
            dwa_ref[0] += _mm_tn(xb, dpab)
            dwi_ref[0] += _mm_tn(xb, dpib)
            dba_ref[...] += jnp.sum(dpa, axis=0, keepdims=True)
            dbi_ref[...] += jnp.sum(dpi, axis=0, keepdims=True)
            dxc = dxc + _mm_nt(dpab, wa) + _mm_nt(dpib, wi)
            ext_d = jnp.concatenate([dxc, dxc_next], axis=0)
            dul_ref[rows, :] = (cw[3:4] * dxc + cw[2:3] * _up(ext_d, 1)[:tc]
                                + cw[1:2] * _up(ext_d, 2)[:tc] + cw[0:1] * _up(ext_d, 3)[:tc])
            for k in range(CONV_WIDTH):
                dcw_ref[k:k + 1, :] += jnp.sum(dxc * taps[CONV_WIDTH - 1 - k], axis=0, keepdims=True)
            dcb_ref[...] += jnp.sum(dxc, axis=0, keepdims=True)
            denom = _pool_denominator(t0, row, window)
            db = (_causal_window(ext_p, onehot) / denom - ext_p[HALO:]).astype(BF16)
            dyp = dyp_ref[rows, :]
            dps_ref[...] += jnp.sum(dyp * _mm(db, pw), axis=0, keepdims=True)
            dys = (dyp * ps).astype(BF16)
            dpw_ref[0] += _mm_tn(db, dys)
            dd = _mm_nt(dys, pw)
            ddn = dd / denom
            ext_q = jnp.concatenate([ddn, ddn_next], axis=0)
            dup_ref[rows, :] = _anticausal_window(ext_q, onehot, tc) - dd
            return l[0:1, :], a[0:1, :], dxc[0:8, :], ddn[0:HALO, :]

        def step(k, carry):
            c = n_chunks - 1 - k
            t0 = pl.multiple_of(c * tc, tc)
            ext = pl.ds(pl.multiple_of(c * tc - HALO, HALO), tc + HALO)
            return chunk(t0, up_ref[ext, :], ul_ref[ext, :], h_ref[ext, :], carry)

        carry = (jnp.zeros((1, SLAB), F32), jnp.zeros((1, SLAB), F32),
                 jnp.zeros((8, SLAB), F32), jnp.zeros((HALO, SLAB), F32))
        carry = lax.fori_loop(0, n_chunks - 1, step, carry)
        pad = jnp.zeros((HALO, SLAB), F32)
        first = pl.ds(0, tc)
        chunk(0, jnp.concatenate([pad, up_ref[first, :]], axis=0), jnp.concatenate([pad, ul_ref[first, :]], axis=0),
              jnp.concatenate([pad, h_ref[first, :]], axis=0), carry)
        dlam_ref[...] = dlam_ref[...] * (LRU_C * jax.nn.sigmoid(-lam_v))

    seq, mat, vec, taps = _slab_specs(s)
    full = jax.ShapeDtypeStruct((s, 512), F32)
    mats = jax.ShapeDtypeStruct((N_SLAB, SLAB, SLAB), F32)
    vecs = jax.ShapeDtypeStruct((1, 512), F32)
    return pl.pallas_call(
        body, name="mixer_bwd", grid=(N_SLAB,),
        in_specs=[seq] * 6 + [mat, vec, taps, vec, mat, vec, mat, vec, vec],
        out_specs=[seq, seq, seq, mat, vec, taps, vec, mat, vec, mat, vec, vec],
        out_shape=[full, full, full, mats, vecs, jax.ShapeDtypeStruct((CONV_WIDTH, 512), F32), vecs,
                   mats, vecs, mats, vecs, vecs],
        compiler_params=_params(56, 1),
    )(u_pool, u_lru, u_gate, h, dy_pool, dy_lru, pool_w, pool_scale, conv_w, conv_b, wa_bd, b_a, wi_bd, b_i, lam)


def _bwd_in(x, dh1, du_pool, du_lru, du_gate, ln1_g, w_in_t):
    s = x.shape[0]
    tb = _token_block(s)

    def body(x_ref, dh1_ref, dup_ref, dul_ref, dug_ref, g_ref, w_ref, dx_ref, dwb_ref, dg_ref, dw_ref):
        @pl.when(pl.program_id(0) == 0)
        def _():
            dw_ref[...] = jnp.zeros_like(dw_ref)
            dg_ref[...] = jnp.zeros_like(dg_ref)

        g1 = g_ref[...]
        n, xhat, rstd = _rms(x_ref[...], g1)
        nb = n.astype(BF16)
        dn = jnp.zeros((tb, D_MODEL), F32)
        for k, ref in enumerate((dup_ref, dul_ref, dug_ref)):
            rows = slice(k * 512, (k + 1) * 512)
            db = ref[...].astype(BF16)
            dw_ref[rows, :] += _mm_tn(db, nb)
            dn = dn + _mm(db, w_ref[rows, :])
        dx, dg1 = _rms_bwd(dn, xhat, rstd, g1)
        dx_ref[...] = dh1_ref[...] + dx
        dg_ref[...] += dg1

        @pl.when(pl.program_id(0) == s // tb - 1)
        def _():
            dwb_ref[...] = dw_ref[...].astype(BF16)

    row = pl.BlockSpec((tb, D_MODEL), lambda i: (i, 0))
    half = pl.BlockSpec((tb, 512), lambda i: (i, 0))
    vec = pl.BlockSpec((1, D_MODEL), lambda i: (0, 0))
    mat = pl.BlockSpec((IN_WIDTH, D_MODEL), lambda i: (0, 0))
    return pl.pallas_call(
        body, name="bwd_in", grid=(s // tb,),
        in_specs=[row, row, half, half, half, vec, mat],
        out_specs=[row, mat, vec],
        out_shape=[jax.ShapeDtypeStruct((s, D_MODEL), F32), jax.ShapeDtypeStruct((IN_WIDTH, D_MODEL), BF16),
                   jax.ShapeDtypeStruct((1, D_MODEL), F32)],
        scratch_shapes=[pltpu.VMEM((IN_WIDTH, D_MODEL), F32)],
        compiler_params=_params(48, 1),
    )(x, dh1, du_pool, du_lru, du_gate, ln1_g, w_in_t)


def _mesh_position():
    x, y, c = lax.axis_index("x"), lax.axis_index("y"), lax.axis_index("c")
    return x, y, c, 4 * x + 2 * y + c


def _peer(x, y, c, p):
    px = 1 - x if p & 4 else x
    py = 1 - y if p & 2 else y
    pc = 1 - c if p & 1 else c
    return (px, py, pc), 4 * px + 2 * py + pc


def _all_gather(shards):
    n = len(shards)

    def body(*refs):
        ins, outs = refs[:n], refs[n:2 * n]
        send_sems, recv_sems, local_sems = refs[2 * n:]
        x, y, c, me = _mesh_position()
        local = [pltpu.make_async_copy(ins[m], outs[m].at[me], local_sems.at[m]) for m in range(n)]
        for cp in local:
            cp.start()
        sends = []
        for p in range(1, N_DEV):
            peer, _ = _peer(x, y, c, p)
            for m in range(n):
                cp = pltpu.make_async_remote_copy(ins[m], outs[m].at[me], send_sems.at[m, p], recv_sems.at[m, p],
                                                  device_id=peer, device_id_type=MESH)
                cp.start()
                sends.append(cp)
        for p in range(1, N_DEV):
            peer, peer_index = _peer(x, y, c, p)
            for m in range(n):
                pltpu.make_async_remote_copy(ins[m], outs[m].at[peer_index], send_sems.at[m, p], recv_sems.at[m, p],
                                             device_id=peer, device_id_type=MESH).wait_recv()
        for cp in sends:
            cp.wait_send()
        for cp in local:
            cp.wait()

    hbm = pl.BlockSpec(memory_space=pl.ANY)
    return pl.pallas_call(
        body, name="gather_weights",
        in_specs=[hbm] * n, out_specs=[hbm] * n,
        out_shape=[jax.ShapeDtypeStruct((N_DEV,) + a.shape, a.dtype) for a in shards],
        scratch_shapes=[pltpu.SemaphoreType.DMA((n, N_DEV)), pltpu.SemaphoreType.DMA((n, N_DEV)),
                        pltpu.SemaphoreType.DMA((n,))],
    )(*shards)


def _exchange(stacks, shared):
    n = len(stacks) + 1

    def body(*refs):
        ins, outs = refs[:n], refs[n:2 * n]
        send_sems, recv_sems, local_sems = refs[2 * n:]
        x, y, c, me = _mesh_position()

        def source(m, index):
            return ins[m] if m == n - 1 else ins[m].at[index]

        local = [pltpu.make_async_copy(source(m, me), outs[m].at[me], local_sems.at[m]) for m in range(n)]
        for cp in local:
            cp.start()
        sends = []
        for p in range(1, N_DEV):
            peer, peer_index = _peer(x, y, c, p)
            for m in range(n):
                cp = pltpu.make_async_remote_copy(source(m, peer_index), outs[m].at[me],
                                                  send_sems.at[m, p], recv_sems.at[m, p],
                                                  device_id=peer, device_id_type=MESH)
                cp.start()
                sends.append(cp)
        for p in range(1, N_DEV):
            peer, peer_index = _peer(x, y, c, p)
            for m in range(n):
                pltpu.make_async_remote_copy(source(m, me), outs[m].at[peer_index],
                                             send_sems.at[m, p], recv_sems.at[m, p],
                                             device_id=peer, device_id_type=MESH).wait_recv()
        for cp in sends:
            cp.wait_send()
        for cp in local:
            cp.wait()

    hbm = pl.BlockSpec(memory_space=pl.ANY)
    return pl.pallas_call(
        body, name="exchange_grads",
        in_specs=[hbm] * n, out_specs=[hbm] * n,
        out_shape=[jax.ShapeDtypeStruct(a.shape, a.dtype) for a in stacks]
        + [jax.ShapeDtypeStruct((N_DEV,) + shared.shape, shared.dtype)],
        scratch_shapes=[pltpu.SemaphoreType.DMA((n, N_DEV)), pltpu.SemaphoreType.DMA((n, N_DEV)),
                        pltpu.SemaphoreType.DMA((n,))],
    )(*stacks, shared)


def _reduce_adam(parts, w, m, v, name):
    rows, cols = w.shape
    rb = rows
    for cand in (256, 176, 128):
        if rows % cand == 0 and rows > cand:
            rb = cand
            break
    bc1 = 1.0 - ADAM_B1 ** ADAM_STEP
    bc2 = 1.0 - ADAM_B2 ** ADAM_STEP

    def body(p_ref, w_ref, m_ref, v_ref, g_out, d_out, m_out, v_out):
        g = p_ref[0].astype(F32)
        for j in range(1, N_DEV):
            g = g + p_ref[j].astype(F32)
        m_new = ADAM_B1 * m_ref[...] + (1.0 - ADAM_B1) * g
        v_new = ADAM_B2 * v_ref[...] + (1.0 - ADAM_B2) * (g * g)
        g_out[...] = g
        m_out[...] = m_new
        v_out[...] = v_new
        d_out[...] = (-ADAM_LR) * ((m_new / bc1) / (jnp.sqrt(v_new / bc2) + ADAM_EPS) + ADAM_WD * w_ref[...])

    blk = pl.BlockSpec((rb, cols), lambda i: (i, 0))
    out = jax.ShapeDtypeStruct((rows, cols), F32)
    return pl.pallas_call(
        body, name=name, grid=(rows // rb,),
        in_specs=[pl.BlockSpec((N_DEV, rb, cols), lambda i: (0, i, 0)), blk, blk, blk],
        out_specs=[blk] * 4, out_shape=[out] * 4,
        compiler_params=_params(40, 1),
    )(parts, w, m, v)


def _cols_from_stack(stack):
    n, r, c = stack.shape
    return jnp.transpose(stack, (1, 0, 2)).reshape(r, n * c)


def _stack_from_cols(full):
    r, c8 = full.shape
    return jnp.transpose(full.reshape(r, N_DEV, c8 // N_DEV), (1, 0, 2))


def _block_diag(w):
    z = jnp.zeros((N_SLAB, 64, 64), w.dtype)
    pairs = w.reshape(N_SLAB, 2, 64, 64)
    top = jnp.concatenate([pairs[:, 0], z], axis=2)
    bottom = jnp.concatenate([z, pairs[:, 1]], axis=2)
    return jnp.concatenate([top, bottom], axis=1)


def _diag_blocks(w):
    return jnp.stack([w[:, :64, :64], w[:, 64:, 64:]], axis=1).reshape(8, 64, 64)


def _pack(arrays):
    flat = jnp.concatenate([a.reshape(-1) for a in arrays])
    rows = -(-flat.shape[0] // (8 * SLAB)) * 8
    return jnp.pad(flat, (0, rows * SLAB - flat.shape[0])).reshape(rows, SLAB)


def _unpack(packed, like):
    flat = packed.reshape(-1)
    out, at = [], 0
    for a in like:
        out.append(flat[at:at + a.size].reshape(a.shape))
        at += a.size
    return out


def kernel(x, ln1_g, w_in, pool_w, pool_scale, conv_w, conv_b, w_a, b_a, w_i, b_i, lam, gn_pool_g, gn_lru_g, w_out, ln2_g, w_ffn_gate, w_ffn_up, w_ffn_down, lnf_g, loss_target, m_ln1_g, m_w_in, m_pool_w, m_pool_scale, m_conv_w, m_conv_b, m_w_a, m_b_a, m_w_i, m_b_i, m_lam, m_gn_pool_g, m_gn_lru_g, m_w_out, m_ln2_g, m_w_ffn_gate, m_w_ffn_up, m_w_ffn_down, m_lnf_g, v_ln1_g, v_w_in, v_pool_w, v_pool_scale, v_conv_w, v_conv_b, v_w_a, v_b_a, v_w_i, v_b_i, v_lam, v_gn_pool_g, v_gn_lru_g, v_w_out, v_ln2_g, v_w_ffn_gate, v_w_ffn_up, v_w_ffn_down, v_lnf_g):
    weights = dict(ln1_g=ln1_g, w_in=w_in, pool_w=pool_w, pool_scale=pool_scale, conv_w=conv_w, conv_b=conv_b,
                   w_a=w_a, b_a=b_a, w_i=w_i, b_i=b_i, lam=lam, gn_pool_g=gn_pool_g, gn_lru_g=gn_lru_g,
                   w_out=w_out, ln2_g=ln2_g, w_ffn_gate=w_ffn_gate, w_ffn_up=w_ffn_up, w_ffn_down=w_ffn_down,
                   lnf_g=lnf_g)
    mom1 = dict(ln1_g=m_ln1_g, w_in=m_w_in, pool_w=m_pool_w, pool_scale=m_pool_scale, conv_w=m_conv_w,
                conv_b=m_conv_b, w_a=m_w_a, b_a=m_b_a, w_i=m_w_i, b_i=m_b_i, lam=m_lam, gn_pool_g=m_gn_pool_g,
                gn_lru_g=m_gn_lru_g, w_out=m_w_out, ln2_g=m_ln2_g, w_ffn_gate=m_w_ffn_gate,
                w_ffn_up=m_w_ffn_up, w_ffn_down=m_w_ffn_down, lnf_g=m_lnf_g)
    mom2 = dict(ln1_g=v_ln1_g, w_in=v_w_in, pool_w=v_pool_w, pool_scale=v_pool_scale, conv_w=v_conv_w,
                conv_b=v_conv_b, w_a=v_w_a, b_a=v_b_a, w_i=v_w_i, b_i=v_b_i, lam=v_lam, gn_pool_g=v_gn_pool_g,
                gn_lru_g=v_gn_lru_g, w_out=v_w_out, ln2_g=v_ln2_g, w_ffn_gate=v_w_ffn_gate,
                w_ffn_up=v_w_ffn_up, w_ffn_down=v_w_ffn_down, lnf_g=v_lnf_g)

    xs = x[0]
    target = loss_target[0]

    shard = dict(w_in=lambda a: a[0].T, w_ffn_gate=lambda a: a[0].T, w_ffn_up=lambda a: a[0].T,
                 w_out=lambda a: a[0], w_ffn_down=lambda a: a[0], conv_w=lambda a: a[0])
    unshard = dict(w_in=lambda a: a.T[None], w_ffn_gate=lambda a: a.T[None], w_ffn_up=lambda a: a.T[None],
                   w_out=lambda a: a[None], w_ffn_down=lambda a: a[None], conv_w=lambda a: a[None])

    g_in, g_out, g_gate, g_up, g_down, g_conv = _all_gather(
        [shard[k](weights[k]).astype(BF16) for k in ("w_in", "w_out", "w_ffn_gate", "w_ffn_up", "w_ffn_down")]
        + [conv_w[0]])
    w_in_f = g_in.reshape(IN_WIDTH, D_MODEL)
    w_out_f = g_out.reshape(D_MODEL, D_MODEL)
    w_gate_f = g_gate.reshape(D_FF, D_MODEL)
    w_up_f = g_up.reshape(D_FF, D_MODEL)
    w_down_f = g_down.reshape(D_FF, D_MODEL)
    conv_w_f = _cols_from_stack(g_conv)

    wa_bd = _block_diag(w_a[0])
    wi_bd = _block_diag(w_i[0])
    lnf_row = lnf_g.reshape(1, D_MODEL)

    u_pool, u_lru, u_gate = _fwd_in(xs, ln1_g, w_in_f)
    y_pool, h, y_lru = _mixer_fwd(u_pool, u_lru, u_gate, pool_w[0], pool_scale, conv_w_f, conv_b,
                                  wa_bd, b_a, wi_bd, b_i, lam)
    h1, n2 = _fwd_out(xs, y_pool, y_lru, gn_pool_g, gn_lru_g, w_out_f, ln2_g)
    g_act, u_act, dh2, dh2b, d_lnf, sq = _ffn_fwd(h1, n2, target, lnf_row, w_gate_f, w_up_f, w_down_f)
    loss = lax.psum((0.5 / D_MODEL) * jnp.sum(sq), ("x", "y", "c"))

    d_gate, d_up, d_down, dn2 = _ffn_bwd(n2, dh2b, g_act, u_act, w_gate_f, w_up_f, w_down_f)
    dh1, dy_pool, dy_lru, d_out, d_ln2, d_gnp, d_gnl = _bwd_out(dn2, dh2, h1, y_pool, y_lru, gn_pool_g, gn_lru_g,
                                                                 w_out_f, ln2_g)
    (du_pool, du_lru, du_gate, d_pw, d_ps, d_cw, d_cb, d_wa, d_ba, d_wi, d_bi, d_lam) = _mixer_bwd(
        u_pool, u_lru, u_gate, h, dy_pool, dy_lru, pool_w[0], pool_scale, conv_w_f, conv_b,
        wa_bd, b_a, wi_bd, b_i, lam)
    grad_x, d_in, d_ln1 = _bwd_in(xs, dh1, du_pool, du_lru, du_gate, ln1_g, w_in_f)

    small_grads = dict(ln1_g=d_ln1, pool_w=d_pw[None], pool_scale=d_ps, conv_b=d_cb, w_a=_diag_blocks(d_wa)[None],
                       b_a=d_ba, w_i=_diag_blocks(d_wi)[None], b_i=d_bi, lam=d_lam, gn_pool_g=d_gnp,
                       gn_lru_g=d_gnl, ln2_g=d_ln2, lnf_g=d_lnf.reshape(D_MODEL))
    small_like = [weights[k] for k in SMALL_NAMES]
    packed = _pack([small_grads[k] for k in SMALL_NAMES])

    stacks = [d_in.reshape(N_DEV, IN_WIDTH // N_DEV, D_MODEL),
              d_out.reshape(N_DEV, D_MODEL // N_DEV, D_MODEL),
              d_gate.reshape(N_DEV, D_FF // N_DEV, D_MODEL),
              d_up.reshape(N_DEV, D_FF // N_DEV, D_MODEL),
              d_down.reshape(N_DEV, D_FF // N_DEV, D_MODEL),
              _stack_from_cols(d_cw)]
    r_in, r_out, r_gate, r_up, r_down, r_conv, r_small = _exchange(stacks, packed)

    results = {}
    for name, parts in (("w_in", r_in), ("w_out", r_out), ("w_ffn_gate", r_gate), ("w_ffn_up", r_up),
                        ("w_ffn_down", r_down), ("conv_w", r_conv)):
        outs = _reduce_adam(parts, shard[name](weights[name]), shard[name](mom1[name]), shard[name](mom2[name]),
                            "adam_" + name)
        results[name] = tuple(unshard[name](o) for o in outs)
    sg, sd, sm1, sm2 = _reduce_adam(r_small, _pack(small_like), _pack([mom1[k] for k in SMALL_NAMES]),
                                    _pack([mom2[k] for k in SMALL_NAMES]), "adam_small")
    for k, g, d, m1, m2 in zip(SMALL_NAMES, _unpack(sg, small_like), _unpack(sd, small_like),
                               _unpack(sm1, small_like), _unpack(sm2, small_like)):
        results[k] = (g, d, m1, m2)

    order = ["ln1_g", "w_in", "pool_w", "pool_scale", "conv_w", "conv_b", "w_a", "b_a", "w_i", "b_i", "lam",
             "gn_pool_g", "gn_lru_g", "w_out", "ln2_g", "w_ffn_gate", "w_ffn_up", "w_ffn_down", "lnf_g"]
    return (loss, grad_x[None],
            *[results[k][0] for k in order], *[results[k][1] for k in order],
            *[results[k][2] for k in order], *[results[k][3] for k in order])
```

```python
import functools

import jax
import jax.numpy as jnp
from jax import lax
from jax.experimental import pallas as pl
from jax.experimental.pallas import tpu as pltpu

F32 = jnp.float32
BF16 = jnp.bfloat16

N_DEV = 8
D_MODEL = 1024
POOL_WIDTH = 512
LRU_WIDTH = 512
IN_WIDTH = 1536
D_FF = 2816
N_SLAB = 4
SLAB = 128
CONV_WIDTH = 4
LRU_C = 8.0
EPS = 1e-6
HALO = 16
FF_CHUNK = 256

ADAM_LR = 0.001
ADAM_B1 = 0.9
ADAM_B2 = 0.999
ADAM_EPS = 1e-08
ADAM_WD = 0.01
ADAM_STEP = 10

MIB = 1 << 20
MESH = pl.DeviceIdType.MESH

SMALL_NAMES = ("ln1_g", "pool_w", "pool_scale", "conv_b", "w_a", "b_a", "w_i", "b_i", "lam",
               "gn_pool_g", "gn_lru_g", "ln2_g", "lnf_g")


def _params(vmem_mib, n_axes=0):
    sem = ("arbitrary",) * n_axes if n_axes else None
    return pltpu.CompilerParams(dimension_semantics=sem, vmem_limit_bytes=vmem_mib * MIB)


def _mm(a, b):
    return jnp.dot(a, b, preferred_element_type=F32)


def _mm_nt(a, b):
    return lax.dot_general(a, b, (((1,), (1,)), ((), ())), preferred_element_type=F32)


def _mm_tn(a, b):
    return lax.dot_general(a, b, (((0,), (0,)), ((), ())), preferred_element_type=F32)


def _rms(x, g):
    rstd = lax.rsqrt(jnp.mean(x * x, axis=-1, keepdims=True) + EPS)
    xhat = x * rstd
    return xhat * g, xhat, rstd


def _rms_bwd(dy, xhat, rstd, g):
    gy = dy * g
    dx = rstd * (gy - xhat * jnp.mean(gy * xhat, axis=-1, keepdims=True))
    return dx, jnp.sum(dy * xhat, axis=0, keepdims=True)


def _gelu(z):
    t = jnp.tanh(0.7978845608028654 * (z + 0.044715 * z * z * z))
    return 0.5 * z * (1.0 + t), t


def _gelu_grad(z, t):
    return 0.5 * (1.0 + t) + 0.5 * z * (1.0 - t * t) * 0.7978845608028654 * (1.0 + 3.0 * 0.044715 * z * z)


def _softplus_neg(lam):
    x = -lam
    e = jnp.exp(-jnp.abs(x))
    u = 1.0 + e
    l1p = jnp.where(u == 1.0, e, jnp.log(u) * e / (u - 1.0))
    return jnp.maximum(x, 0.0) + l1p


def _expm1(x):
    p = x * (1.0 + x * (0.5 + x * (1.0 / 6.0 + x * (1.0 / 24.0 + x * (1.0 / 120.0)))))
    return jnp.where(jnp.abs(x) < 0.1, p, jnp.exp(x) - 1.0)


def _down(v, d):
    return pltpu.roll(v, d, 0)


def _up(v, d):
    return pltpu.roll(v, v.shape[0] - d, 0)


def _token_block(s):
    return 512 if s % 512 == 0 and s > 512 else 256


def _time_chunk(s):
    return 256 if s % 256 == 0 else s


def _fwd_in(x, ln1_g, w_in_t):
    s = x.shape[0]
    tb = _token_block(s)

    def body(x_ref, g_ref, w_ref, up_ref, ul_ref, ug_ref):
        n, _, _ = _rms(x_ref[...], g_ref[...])
        proj = _mm_nt(n.astype(BF16), w_ref[...])
        up_ref[...] = proj[:, :POOL_WIDTH]
        ul_ref[...] = proj[:, POOL_WIDTH:POOL_WIDTH + LRU_WIDTH]
        ug_ref[...] = proj[:, POOL_WIDTH + LRU_WIDTH:]

    out = jax.ShapeDtypeStruct((s, 512), F32)
    return pl.pallas_call(
        body, name="fwd_in", grid=(s // tb,),
        in_specs=[pl.BlockSpec((tb, D_MODEL), lambda i: (i, 0)),
                  pl.BlockSpec((1, D_MODEL), lambda i: (0, 0)),
                  pl.BlockSpec((IN_WIDTH, D_MODEL), lambda i: (0, 0))],
        out_specs=[pl.BlockSpec((tb, 512), lambda i: (i, 0))] * 3,
        out_shape=[out, out, out],
        compiler_params=_params(40, 1),
    )(x, ln1_g, w_in_t)


def _pool_denominator(t0, row, window):
    return jnp.minimum((t0 + row + 1).astype(F32), window)


def _causal_window(ext, onehot):
    s2 = ext + _down(ext, 1)
    s4 = s2 + _down(s2, 2)
    s8 = s4 + _down(s4, 4)
    s16 = s8 + _down(s8, 8)
    return (onehot[0] * s2 + onehot[1] * s4 + onehot[2] * s8 + onehot[3] * s16)[HALO:]


def _anticausal_window(ext, onehot, rows):
    s2 = ext + _up(ext, 1)
    s4 = s2 + _up(s2, 2)
    s8 = s4 + _up(s4, 4)
    s16 = s8 + _up(s8, 8)
    return (onehot[0] * s2 + onehot[1] * s4 + onehot[2] * s8 + onehot[3] * s16)[:rows]


def _conv_taps(ext):
    return [ext[HALO:], _down(ext, 1)[HALO:], _down(ext, 2)[HALO:], _down(ext, 3)[HALO:]]


def _conv(taps, cw, cb):
    return cw[3:4] * taps[0] + cw[2:3] * taps[1] + cw[1:2] * taps[2] + cw[0:1] * taps[3] + cb


def _lru_gates(xc, wa, ba, wi, bi, sp):
    xb = xc.astype(BF16)
    r = jax.nn.sigmoid(_mm(xb, wa) + ba)
    i = jax.nn.sigmoid(_mm(xb, wi) + bi)
    la = (-LRU_C) * r * sp
    a = jnp.exp(la)
    mult = jnp.sqrt(jnp.maximum(-_expm1(2.0 * la), 0.0))
    return xb, r, i, a, mult


def _scan_causal(a, b, h_prev, row, rows):
    d = 1
    while d < rows:
        head = row < d
        a_sh = jnp.where(head, 1.0, _down(a, d))
        b_sh = jnp.where(head, 0.0, _down(b, d))
        b = a * b_sh + b
        a = a * a_sh
        d *= 2
    return b + a * h_prev


def _scan_anticausal(a, b, l_next, row, rows):
    d = 1
    while d < rows:
        tail = row >= rows - d
        a_sh = jnp.where(tail, 1.0, _up(a, d))
        b_sh = jnp.where(tail, 0.0, _up(b, d))
        b = a * b_sh + b
        a = a * a_sh
        d *= 2
    return b + a * l_next


def _slab_scalars():
    slab = pl.program_id(0)
    onehot = [jnp.where(slab == k, 1.0, 0.0).astype(F32) for k in range(N_SLAB)]
    window = jnp.left_shift(jnp.int32(2), slab).astype(F32)
    return onehot, window


def _slab_specs(s):
    seq = pl.BlockSpec((s, SLAB), lambda k: (0, k))
    mat = pl.BlockSpec((1, SLAB, SLAB), lambda k: (k, 0, 0))
    vec = pl.BlockSpec((1, SLAB), lambda k: (0, k))
    taps = pl.BlockSpec((CONV_WIDTH, SLAB), lambda k: (0, k))
    return seq, mat, vec, taps


def _mixer_fwd(u_pool, u_lru, u_gate, pool_w, pool_scale, conv_w, conv_b, wa_bd, b_a, wi_bd, b_i, lam):
    s = u_pool.shape[0]
    tc = _time_chunk(s)
    n_chunks = s // tc

    def body(up_ref, ul_ref, ug_ref, pw_ref, ps_ref, cw_ref, cb_ref, wa_ref, ba_ref, wi_ref, bi_ref, lam_ref,
             yp_ref, h_ref, yl_ref):
        onehot, window = _slab_scalars()
        pw = pw_ref[0].astype(BF16)
        wa = wa_ref[0].astype(BF16)
        wi = wi_ref[0].astype(BF16)
        ps, cw, cb, ba, bi = ps_ref[...], cw_ref[...], cb_ref[...], ba_ref[...], bi_ref[...]
        sp = _softplus_neg(lam_ref[...])
        row = lax.broadcasted_iota(jnp.int32, (tc, SLAB), 0)

        def chunk(t0, ext_p, ext_l, h_prev):
            rows = pl.ds(t0, tc)
            d = _causal_window(ext_p, onehot) / _pool_denominator(t0, row, window) - ext_p[HALO:]
            yp_ref[rows, :] = _mm(d.astype(BF16), pw) * ps
            xc = _conv(_conv_taps(ext_l), cw, cb)
            _, _, i, a, mult = _lru_gates(xc, wa, ba, wi, bi, sp)
            h = _scan_causal(a, mult * (i * xc), h_prev, row, tc)
            h_ref[rows, :] = h
            yl_ref[rows, :] = h * _gelu(ug_ref[rows, :])[0]
            return h[tc - 1:tc, :]

        pad = jnp.zeros((HALO, SLAB), F32)
        h0 = chunk(0, jnp.concatenate([pad, up_ref[pl.ds(0, tc), :]], axis=0),
                   jnp.concatenate([pad, ul_ref[pl.ds(0, tc), :]], axis=0), jnp.zeros((1, SLAB), F32))

        def step(c, h_prev):
            t0 = pl.multiple_of(c * tc, tc)
            ext = pl.ds(pl.multiple_of(c * tc - HALO, HALO), tc + HALO)
            return chunk(t0, up_ref[ext, :], ul_ref[ext, :], h_prev)

        lax.fori_loop(1, n_chunks, step, h0)

    seq, mat, vec, taps = _slab_specs(s)
    out = jax.ShapeDtypeStruct((s, 512), F32)
    return pl.pallas_call(
        body, name="mixer_fwd", grid=(N_SLAB,),
        in_specs=[seq, seq, seq, mat, vec, taps, vec, mat, vec, mat, vec, vec],
        out_specs=[seq, seq, seq], out_shape=[out, out, out],
        compiler_params=_params(48, 1),
    )(u_pool, u_lru, u_gate, pool_w, pool_scale, conv_w, conv_b, wa_bd, b_a, wi_bd, b_i, lam)


def _fwd_out(x, y_pool, y_lru, gn_pool_g, gn_lru_g, w_out, ln2_g):
    s = x.shape[0]
    tb = _token_block(s)

    def body(x_ref, yp_ref, yl_ref, gp_ref, gl_ref, w_ref, g2_ref, h1_ref, n2_ref):
        mp, _, _ = _rms(yp_ref[...], gp_ref[...])
        ml, _, _ = _rms(yl_ref[...], gl_ref[...])
        h1 = x_ref[...] + _mm(mp.astype(BF16), w_ref[:POOL_WIDTH, :]) + _mm(ml.astype(BF16), w_ref[POOL_WIDTH:, :])
        h1_ref[...] = h1
        n2_ref[...] = _rms(h1, g2_ref[...])[0].astype(BF16)

    row = pl.BlockSpec((tb, D_MODEL), lambda i: (i, 0))
    half = pl.BlockSpec((tb, 512), lambda i: (i, 0))
    return pl.pallas_call(
        body, name="fwd_out", grid=(s // tb,),
        in_specs=[row, half, half, pl.BlockSpec((1, 512), lambda i: (0, 0)), pl.BlockSpec((1, 512), lambda i: (0, 0)),
                  pl.BlockSpec((D_MODEL, D_MODEL), lambda i: (0, 0)), pl.BlockSpec((1, D_MODEL), lambda i: (0, 0))],
        out_specs=[row, row],
        out_shape=[jax.ShapeDtypeStruct((s, D_MODEL), F32), jax.ShapeDtypeStruct((s, D_MODEL), BF16)],
        compiler_params=_params(40, 1),
    )(x, y_pool, y_lru, gn_pool_g, gn_lru_g, w_out, ln2_g)


def _ffn_fwd(h1, n2, target, lnf_g, w_gate, w_up, w_down):
    s = h1.shape[0]
    tb = 256
    n_ff = D_FF // FF_CHUNK

    def body(h1_ref, n2_ref, t_ref, gf_ref, wg_hbm, wu_hbm, wd_hbm,
             g_ref, u_ref, dh_ref, dhb_ref, dgf_ref, sq_ref, wg, wu, wd, sem):
        @pl.when(pl.program_id(0) == 0)
        def _():
            loads = [pltpu.make_async_copy(src, dst, sem.at[k])
                     for k, (src, dst) in enumerate(((wg_hbm, wg), (wu_hbm, wu), (wd_hbm, wd)))]
            for cp in loads:
                cp.start()
            for cp in loads:
                cp.wait()
            dgf_ref[...] = jnp.zeros_like(dgf_ref)
            sq_ref[...] = jnp.zeros_like(sq_ref)

        n2v = n2_ref[...]
        acc = jnp.zeros((tb, D_MODEL), F32)
        for c in range(n_ff):
            cols = slice(c * FF_CHUNK, (c + 1) * FF_CHUNK)
            g = _mm_nt(n2v, wg[cols, :])
            u = _mm_nt(n2v, wu[cols, :])
            g_ref[:, cols] = g.astype(BF16)
            u_ref[:, cols] = u.astype(BF16)
            act = g * jax.nn.sigmoid(g) * u
            acc = acc + _mm(act.astype(BF16), wd[cols, :])
        gf = gf_ref[...]
        y, xhat, rstd = _rms(h1_ref[...] + acc, gf)
        err = y - t_ref[...]
        sq_ref[...] += jnp.sum(err * err, axis=0, keepdims=True)
        dh2, dgf = _rms_bwd(err * (1.0 / D_MODEL), xhat, rstd, gf)
        dgf_ref[...] += dgf
        dh_ref[...] = dh2
        dhb_ref[...] = dh2.astype(BF16)

    row = pl.BlockSpec((tb, D_MODEL), lambda i: (i, 0))
    ff = pl.BlockSpec((tb, D_FF), lambda i: (i, 0))
    vec = pl.BlockSpec((1, D_MODEL), lambda i: (0, 0))
    anyspace = pl.BlockSpec(memory_space=pl.ANY)
    return pl.pallas_call(
        body, name="ffn_fwd", grid=(s // tb,),
        in_specs=[row, row, row, vec, anyspace, anyspace, anyspace],
        out_specs=[ff, ff, row, row, vec, vec],
        out_shape=[jax.ShapeDtypeStruct((s, D_FF), BF16), jax.ShapeDtypeStruct((s, D_FF), BF16),
                   jax.ShapeDtypeStruct((s, D_MODEL), F32), jax.ShapeDtypeStruct((s, D_MODEL), BF16),
                   jax.ShapeDtypeStruct((1, D_MODEL), F32), jax.ShapeDtypeStruct((1, D_MODEL), F32)],
        scratch_shapes=[pltpu.VMEM((D_FF, D_MODEL), BF16), pltpu.VMEM((D_FF, D_MODEL), BF16),
                        pltpu.VMEM((D_FF, D_MODEL), BF16), pltpu.SemaphoreType.DMA((3,))],
        compiler_params=_params(56, 1),
    )(h1, n2, target, lnf_g, w_gate, w_up, w_down)


def _ffn_bwd(n2, dh2b, g, u, w_gate_t, w_up_t, w_down):
    s = n2.shape[0]
    tb = min(1024, s)
    n_ff = D_FF // FF_CHUNK
    n_tb = s // tb

    def body(n2_ref, dh_ref, g_ref, u_ref, wg_ref, wu_ref, wd_ref, dwg_ref, dwu_ref, dwd_ref, dn2_ref,
             dn2_acc, acc_g, acc_u, acc_d):
        j = pl.program_id(0)
        t = pl.program_id(1)

        @pl.when(t == 0)
        def _():
            acc_g[...] = jnp.zeros_like(acc_g)
            acc_u[...] = jnp.zeros_like(acc_u)
            acc_d[...] = jnp.zeros_like(acc_d)

        n2v = n2_ref[...]
        dh = dh_ref[...]
        gv = g_ref[...].astype(F32)
        uv = u_ref[...].astype(F32)
        sg = jax.nn.sigmoid(gv)
        silu = gv * sg
        dact = _mm_nt(dh, wd_ref[...])
        dub = (dact * silu).astype(BF16)
        dgb = (dact * uv * (sg * (1.0 + gv * (1.0 - sg)))).astype(BF16)
        acc_d[...] += _mm_tn((silu * uv).astype(BF16), dh)
        acc_g[...] += _mm_tn(dgb, n2v)
        acc_u[...] += _mm_tn(dub, n2v)
        part = _mm(dgb, wg_ref[...]) + _mm(dub, wu_ref[...])
        rows = pl.ds(pl.multiple_of(t * tb, tb), tb)

        @pl.when(t == n_tb - 1)
        def _():
            dwg_ref[...] = acc_g[...].astype(BF16)
            dwu_ref[...] = acc_u[...].astype(BF16)
            dwd_ref[...] = acc_d[...].astype(BF16)

        @pl.when(j == 0)
        def _():
            dn2_acc[rows, :] = part

        @pl.when(jnp.logical_and(j > 0, j < n_ff - 1))
        def _():
            dn2_acc[rows, :] += part

        @pl.when(j == n_ff - 1)
        def _():
            dn2_ref[...] = dn2_acc[rows, :] + part

    row = pl.BlockSpec((tb, D_MODEL), lambda j, t: (t, 0))
    act = pl.BlockSpec((tb, FF_CHUNK), lambda j, t: (t, j))
    w_row = pl.BlockSpec((FF_CHUNK, D_MODEL), lambda j, t: (j, 0))
    last = pl.BlockSpec((tb, D_MODEL), lambda j, t: (jnp.where(j == n_ff - 1, t, 0), 0))
    grad = jax.ShapeDtypeStruct((D_FF, D_MODEL), BF16)
    chunk_acc = pltpu.VMEM((FF_CHUNK, D_MODEL), F32)
    return pl.pallas_call(
        body, name="ffn_bwd", grid=(n_ff, n_tb),
        in_specs=[row, row, act, act, w_row, w_row, w_row],
        out_specs=[w_row, w_row, w_row, last],
        out_shape=[grad, grad, grad, jax.ShapeDtypeStruct((s, D_MODEL), F32)],
        scratch_shapes=[pltpu.VMEM((s, D_MODEL), F32), chunk_acc, chunk_acc, chunk_acc],
        compiler_params=_params(56, 2),
    )(n2, dh2b, g, u, w_gate_t, w_up_t, w_down)


def _bwd_out(dn2, dh2, h1, y_pool, y_lru, gn_pool_g, gn_lru_g, w_out, ln2_g, after):
    s = h1.shape[0]
    tb = _token_block(s)

    def body(dn2_ref, dh2_ref, h1_ref, yp_ref, yl_ref, gp_ref, gl_ref, w_ref, g2_ref, _after,
             dh1_ref, dyp_ref, dyl_ref, dwb_ref, dg2_ref, dgp_ref, dgl_ref, dw_ref):
        @pl.when(pl.program_id(0) == 0)
        def _():
            dw_ref[...] = jnp.zeros_like(dw_ref)
            dg2_ref[...] = jnp.zeros_like(dg2_ref)
            dgp_ref[...] = jnp.zeros_like(dgp_ref)
            dgl_ref[...] = jnp.zeros_like(dgl_ref)

        g2 = g2_ref[...]
        _, xhat2, rstd2 = _rms(h1_ref[...], g2)
        dres, dg2 = _rms_bwd(dn2_ref[...], xhat2, rstd2, g2)
        dg2_ref[...] += dg2
        dh1 = dh2_ref[...] + dres
        dh1_ref[...] = dh1
        dh1b = dh1.astype(BF16)
        gp, gl = gp_ref[...], gl_ref[...]
        mp, xhat_p, rstd_p = _rms(yp_ref[...], gp)
        ml, xhat_l, rstd_l = _rms(yl_ref[...], gl)
        dw_ref[:POOL_WIDTH, :] += _mm_tn(mp.astype(BF16), dh1b)
        dw_ref[POOL_WIDTH:, :] += _mm_tn(ml.astype(BF16), dh1b)
        dyp, dgp = _rms_bwd(_mm_nt(dh1b, w_ref[:POOL_WIDTH, :]), xhat_p, rstd_p, gp)
        dyl, dgl = _rms_bwd(_mm_nt(dh1b, w_ref[POOL_WIDTH:, :]), xhat_l, rstd_l, gl)
        dyp_ref[...] = dyp
        dyl_ref[...] = dyl
        dgp_ref[...] += dgp
        dgl_ref[...] += dgl

        @pl.when(pl.program_id(0) == s // tb - 1)
        def _():
            dwb_ref[...] = dw_ref[...].astype(BF16)

    row = pl.BlockSpec((tb, D_MODEL), lambda i: (i, 0))
    half = pl.BlockSpec((tb, 512), lambda i: (i, 0))
    vec = pl.BlockSpec((1, D_MODEL), lambda i: (0, 0))
    hvec = pl.BlockSpec((1, 512), lambda i: (0, 0))
    mat = pl.BlockSpec((D_MODEL, D_MODEL), lambda i: (0, 0))
    return pl.pallas_call(
        body, name="bwd_out", grid=(s // tb,),
        in_specs=[row, row, row, half, half, hvec, hvec, mat, vec, pl.BlockSpec(memory_space=pl.ANY)],
        out_specs=[row, half, half, mat, vec, hvec, hvec],
        out_shape=[jax.ShapeDtypeStruct((s, D_MODEL), F32), jax.ShapeDtypeStruct((s, 512), F32),
                   jax.ShapeDtypeStruct((s, 512), F32), jax.ShapeDtypeStruct((D_MODEL, D_MODEL), BF16),
                   jax.ShapeDtypeStruct((1, D_MODEL), F32), jax.ShapeDtypeStruct((1, 512), F32),
                   jax.ShapeDtypeStruct((1, 512), F32)],
        scratch_shapes=[pltpu.VMEM((D_MODEL, D_MODEL), F32)],
        compiler_params=_params(48, 1),
    )(dn2, dh2, h1, y_pool, y_lru, gn_pool_g, gn_lru_g, w_out, ln2_g, after)


def _mixer_bwd(u_pool, u_lru, u_gate, h, dy_pool, dy_lru,
               pool_w, pool_scale, conv_w, conv_b, wa_bd, b_a, wi_bd, b_i, lam):
    s = u_pool.shape[0]
    tc = _time_chunk(s)
    n_chunks = s // tc

    def body(up_ref, ul_ref, ug_ref, h_ref, dyp_ref, dyl_ref,
             pw_ref, ps_ref, cw_ref, cb_ref, wa_ref, ba_ref, wi_ref, bi_ref, lam_ref,
             dup_ref, dul_ref, dug_ref, dpw_ref, dps_ref, dcw_ref, dcb_ref, dwa_ref, dba_ref, dwi_ref, dbi_ref, dlam_ref):
        onehot, window = _slab_scalars()
        pw = pw_ref[0].astype(BF16)
        wa = wa_ref[0].astype(BF16)
        wi = wi_ref[0].astype(BF16)
        ps, cw, cb, ba, bi = ps_ref[...], cw_ref[...], cb_ref[...], ba_ref[...], bi_ref[...]
        lam_v = lam_ref[...]
        sp = _softplus_neg(lam_v)
        row = lax.broadcasted_iota(jnp.int32, (tc, SLAB), 0)
        for ref in (dpw_ref, dps_ref, dcw_ref, dcb_ref, dwa_ref, dba_ref, dwi_ref, dbi_ref, dlam_ref):
            ref[...] = jnp.zeros_like(ref)

        def chunk(t0, ext_p, ext_l, ext_h, carry):
            l_next, a_next, dxc_next, ddn_next = carry
            rows = pl.ds(t0, tc)
            taps = _conv_taps(ext_l)
            xc = _conv(taps, cw, cb)
            xb, r, i, a, mult = _lru_gates(xc, wa, ba, wi, bi, sp)
            hv = ext_h[HALO:]
            h_before = _down(ext_h, 1)[HALO:]
            ug = ug_ref[rows, :]
            dyl = dyl_ref[rows, :]
            gel, th = _gelu(ug)
            dug_ref[rows, :] = dyl * hv * _gelu_grad(ug, th)
            a_after = jnp.where(row == tc - 1, a_next, _up(a, 1))
            l = _scan_anticausal(a_after, dyl * gel, l_next, row, tc)
            dmult = l * (i * xc)
            di = l * mult * xc
            dxc = l * mult * i
            dla = (l * h_before) * a - jnp.where(mult > 0.0, dmult * (a * a) / mult, 0.0)
            dlam_ref[...] += jnp.sum(dla * r, axis=0, keepdims=True)
            dpa = (dla * ((-LRU_C) * sp)) * (r * (1.0 - r))
            dpi = di * (i * (1.0 - i))
            dpab = dpa.astype(BF16)
            dpib = dpi.astype(BF16)
            dwa_ref[0] += _mm_tn(xb, dpab)
            dwi_ref[0] += _mm_tn(xb, dpib)
            dba_ref[...] += jnp.sum(dpa, axis=0, keepdims=True)
            dbi_ref[...] += jnp.sum(dpi, axis=0, keepdims=True)
            dxc = dxc + _mm_nt(dpab, wa) + _mm_nt(dpib, wi)
            ext_d = jnp.concatenate([dxc, dxc_next], axis=0)
            dul_ref[rows, :] = (cw[3:4] * dxc + cw[2:3] * _up(ext_d, 1)[:tc]
                                + cw[1:2] * _up(ext_d, 2)[:tc] + cw[0:1] * _up(ext_d, 3)[:tc])
            for k in range(CONV_WIDTH):
                dcw_ref[k:k + 1, :] += jnp.sum(dxc * taps[CONV_WIDTH - 1 - k], axis=0, keepdims=True)
            dcb_ref[...] += jnp.sum(dxc, axis=0, keepdims=True)
            denom = _pool_denominator(t0, row, window)
            db = (_causal_window(ext_p, onehot) / denom - ext_p[HALO:]).astype(BF16)
            dyp = dyp_ref[rows, :]
            dps_ref[...] += jnp.sum(dyp * _mm(db, pw), axis=0, keepdims=True)
            dys = (dyp * ps).astype(BF16)
            dpw_ref[0] += _mm_tn(db, dys)
            dd = _mm_nt(dys, pw)
            ddn = dd / denom
            ext_q = jnp.concatenate([ddn, ddn_next], axis=0)
            dup_ref[rows, :] = _anticausal_window(ext_q, onehot, tc) - dd
            return l[0:1, :], a[0:1, :], dxc[0:8, :], ddn[0:HALO, :]

        def step(k, carry):
            c = n_chunks - 1 - k
            t0 = pl.multiple_of(c * tc, tc)
            ext = pl.ds(pl.multiple_of(c * tc - HALO, HALO), tc + HALO)
            return chunk(t0, up_ref[ext, :], ul_ref[ext, :], h_ref[ext, :], carry)

        carry = (jnp.zeros((1, SLAB), F32), jnp.zeros((1, SLAB), F32),
                 jnp.zeros((8, SLAB), F32), jnp.zeros((HALO, SLAB), F32))
        carry = lax.fori_loop(0, n_chunks - 1, step, carry)
        pad = jnp.zeros((HALO, SLAB), F32)
        first = pl.ds(0, tc)
        chunk(0, jnp.concatenate([pad, up_ref[first, :]], axis=0), jnp.concatenate([pad, ul_ref[first, :]], axis=0),
              jnp.concatenate([pad, h_ref[first, :]], axis=0), carry)
        dlam_ref[...] = dlam_ref[...] * (LRU_C * jax.nn.sigmoid(-lam_v))

    seq, mat, vec, taps = _slab_specs(s)
    full = jax.ShapeDtypeStruct((s, 512), F32)
    mats = jax.ShapeDtypeStruct((N_SLAB, SLAB, SLAB), F32)
    vecs = jax.ShapeDtypeStruct((1, 512), F32)
    return pl.pallas_call(
        body, name="mixer_bwd", grid=(N_SLAB,),
        in_specs=[seq] * 6 + [mat, vec, taps, vec, mat, vec, mat, vec, vec],
        out_specs=[seq, seq, seq, mat, vec, taps, vec, mat, vec, mat, vec, vec],
        out_shape=[full, full, full, mats, vecs, jax.ShapeDtypeStruct((CONV_WIDTH, 512), F32), vecs,
                   mats, vecs, mats, vecs, vecs],
        compiler_params=_params(56, 1),
    )(u_pool, u_lru, u_gate, h, dy_pool, dy_lru, pool_w, pool_scale, conv_w, conv_b, wa_bd, b_a, wi_bd, b_i, lam)


def _bwd_in(x, dh1, du_pool, du_lru, du_gate, ln1_g, w_in_t):
    s = x.shape[0]
    tb = _token_block(s)

    def body(x_ref, dh1_ref, dup_ref, dul_ref, dug_ref, g_ref, w_ref, dx_ref, dwb_ref, dg_ref, dw_ref):
        @pl.when(pl.program_id(0) == 0)
        def _():
            dw_ref[...] = jnp.zeros_like(dw_ref)
            dg_ref[...] = jnp.zeros_like(dg_ref)

        g1 = g_ref[...]
        n, xhat, rstd = _rms(x_ref[...], g1)
        nb = n.astype(BF16)
        dn = jnp.zeros((tb, D_MODEL), F32)
        for k, ref in enumerate((dup_ref, dul_ref, dug_ref)):
            rows = slice(k * 512, (k + 1) * 512)
            db = ref[...].astype(BF16)
            dw_ref[rows, :] += _mm_tn(db, nb)
            dn = dn + _mm(db, w_ref[rows, :])
        dx, dg1 = _rms_bwd(dn, xhat, rstd, g1)
        dx_ref[...] = dh1_ref[...] + dx
        dg_ref[...] += dg1

        @pl.when(pl.program_id(0) == s // tb - 1)
        def _():
            dwb_ref[...] = dw_ref[...].astype(BF16)

    row = pl.BlockSpec((tb, D_MODEL), lambda i: (i, 0))
    half = pl.BlockSpec((tb, 512), lambda i: (i, 0))
    vec = pl.BlockSpec((1, D_MODEL), lambda i: (0, 0))
    mat = pl.BlockSpec((IN_WIDTH, D_MODEL), lambda i: (0, 0))
    return pl.pallas_call(
        body, name="bwd_in", grid=(s // tb,),
        in_specs=[row, row, half, half, half, vec, mat],
        out_specs=[row, mat, vec],
        out_shape=[jax.ShapeDtypeStruct((s, D_MODEL), F32), jax.ShapeDtypeStruct((IN_WIDTH, D_MODEL), BF16),
                   jax.ShapeDtypeStruct((1, D_MODEL), F32)],
        scratch_shapes=[pltpu.VMEM((IN_WIDTH, D_MODEL), F32)],
        compiler_params=_params(48, 1),
    )(x, dh1, du_pool, du_lru, du_gate, ln1_g, w_in_t)


def _mesh_position():
    x, y, c = lax.axis_index("x"), lax.axis_index("y"), lax.axis_index("c")
    return x, y, c, 4 * x + 2 * y + c


def _peer(x, y, c, p):
    px = 1 - x if p & 4 else x
    py = 1 - y if p & 2 else y
    pc = 1 - c if p & 1 else c
    return (px, py, pc), 4 * px + 2 * py + pc


HBM_SPEC = pl.BlockSpec(memory_space=pltpu.HBM)
SEM_SPEC = pl.BlockSpec(memory_space=pltpu.SEMAPHORE)
DATAFLOW = pltpu.SideEffectType.DATAFLOW_SIDE_EFFECTING


def _part(ref, whole, index):
    return ref if whole else ref.at[index]


def _push_start(name, sources, whole, groups):
    n = len(sources)
    lands = [lax.empty((N_DEV,) + (a.shape if w else a.shape[1:]), a.dtype) for a, w in zip(sources, whole)]
    n_sem = 2 * len(groups)

    def body(*refs):
        src, land = refs[:n], refs[n:2 * n]
        sems = refs[2 * n:2 * n + n_sem]
        local_sems = refs[-1]
        token = refs[-2]
        x, y, c, me = _mesh_position()
        local = [pltpu.make_async_copy(_part(src[m], whole[m], me), land[m].at[me], local_sems.at[m])
                 for m in range(n)]
        for cp in local:
            cp.start()
        for k, group in enumerate(groups):
            for p in range(1, N_DEV):
                peer, peer_index = _peer(x, y, c, p)
                for i, m in enumerate(group):
                    pltpu.make_async_remote_copy(_part(src[m], whole[m], peer_index), land[m].at[me],
                                                 sems[2 * k].at[i * N_DEV + p], sems[2 * k + 1].at[i * N_DEV + p],
                                                 device_id=peer, device_id_type=MESH).start()
        for cp in local:
            cp.wait()
        token[...] = jnp.zeros_like(token)

    sem_shapes = []
    for group in groups:
        sem_shapes += [pltpu.SemaphoreType.DMA((len(group) * N_DEV,))] * 2
    outs = pl.pallas_call(
        body, name=name,
        in_specs=[HBM_SPEC] * (2 * n),
        out_specs=[SEM_SPEC] * n_sem + [HBM_SPEC] * (2 * n) + [pl.BlockSpec(memory_space=pltpu.VMEM)],
        out_shape=sem_shapes + [pltpu.HBM(a.shape, a.dtype) for a in list(sources) + lands]
        + [jax.ShapeDtypeStruct((8, SLAB), F32)],
        input_output_aliases={i: n_sem + i for i in range(2 * n)},
        scratch_shapes=[pltpu.SemaphoreType.DMA((n,))],
        compiler_params=pltpu.CompilerParams(has_side_effects=DATAFLOW),
    )(*[pltpu.with_memory_space_constraint(a, pltpu.HBM) for a in list(sources) + lands])
    sems = [(outs[2 * k], outs[2 * k + 1]) for k in range(len(groups))]
    return sems, outs[n_sem:n_sem + n], outs[n_sem + n:n_sem + 2 * n], outs[-1]


def _push_wait(name, sems, sources, lands, whole, after):
    n = len(sources)
    send_sems, recv_sems = sems

    def body(*refs):
        src, land = refs[:n], refs[n:2 * n]
        send, recv = refs[2 * n], refs[2 * n + 1]
        x, y, c, me = _mesh_position()
        for p in range(1, N_DEV):
            peer, peer_index = _peer(x, y, c, p)
            for m in range(n):
                cp = pltpu.make_async_remote_copy(_part(src[m], whole[m], peer_index), land[m].at[peer_index],
                                                  send.at[m * N_DEV + p], recv.at[m * N_DEV + p],
                                                  device_id=peer, device_id_type=MESH)
                cp.wait_send()
                cp.wait_recv()

    outs = pl.pallas_call(
        body, name=name,
        in_specs=[HBM_SPEC] * (2 * n) + [SEM_SPEC, SEM_SPEC, pl.BlockSpec(memory_space=pl.ANY)],
        out_specs=[HBM_SPEC] * (2 * n),
        out_shape=[pltpu.HBM(a.shape, a.dtype) for a in list(sources) + list(lands)],
        input_output_aliases={i: i for i in range(2 * n)},
        compiler_params=pltpu.CompilerParams(has_side_effects=DATAFLOW),
    )(*sources, *lands, send_sems, recv_sems, after)
    return outs[n:]


def _exchange(stacks, shared):
    n = len(stacks) + 1

    def body(*refs):
        ins, outs = refs[:n], refs[n:2 * n]
        send_sems, recv_sems, local_sems = refs[2 * n:]
        x, y, c, me = _mesh_position()

        def source(m, index):
            return ins[m] if m == n - 1 else ins[m].at[index]

        local = [pltpu.make_async_copy(source(m, me), outs[m].at[me], local_sems.at[m]) for m in range(n)]
        for cp in local:
            cp.start()
        sends = []
        for p in range(1, N_DEV):
            peer, peer_index = _peer(x, y, c, p)
            for m in range(n):
                cp = pltpu.make_async_remote_copy(source(m, peer_index), outs[m].at[me],
                                                  send_sems.at[m, p], recv_sems.at[m, p],
                                                  device_id=peer, device_id_type=MESH)
                cp.start()
                sends.append(cp)
        for p in range(1, N_DEV):
            peer, peer_index = _peer(x, y, c, p)
            for m in range(n):
                pltpu.make_async_remote_copy(source(m, me), outs[m].at[peer_index],
                                             send_sems.at[m, p], recv_sems.at[m, p],
                                             device_id=peer, device_id_type=MESH).wait_recv()
        for cp in sends:
            cp.wait_send()
        for cp in local:
            cp.wait()

    hbm = pl.BlockSpec(memory_space=pl.ANY)
    return pl.pallas_call(
        body, name="exchange_grads",
        in_specs=[hbm] * n, out_specs=[hbm] * n,
        out_shape=[jax.ShapeDtypeStruct(a.shape, a.dtype) for a in stacks]
        + [jax.ShapeDtypeStruct((N_DEV,) + shared.shape, shared.dtype)],
        scratch_shapes=[pltpu.SemaphoreType.DMA((n, N_DEV)), pltpu.SemaphoreType.DMA((n, N_DEV)),
                        pltpu.SemaphoreType.DMA((n,))],
    )(*stacks, shared)


def _reduce_adam(parts, w, m, v, name):
    rows, cols = w.shape
    rb = rows
    for cand in (256, 176, 128):
        if rows % cand == 0 and rows > cand:
            rb = cand
            break
    bc1 = 1.0 - ADAM_B1 ** ADAM_STEP
    bc2 = 1.0 - ADAM_B2 ** ADAM_STEP

    def body(p_ref, w_ref, m_ref, v_ref, g_out, d_out, m_out, v_out):
        g = p_ref[0].astype(F32)
        for j in range(1, N_DEV):
            g = g + p_ref[j].astype(F32)
        m_new = ADAM_B1 * m_ref[...] + (1.0 - ADAM_B1) * g
        v_new = ADAM_B2 * v_ref[...] + (1.0 - ADAM_B2) * (g * g)
        g_out[...] = g
        m_out[...] = m_new
        v_out[...] = v_new
        d_out[...] = (-ADAM_LR) * ((m_new / bc1) / (jnp.sqrt(v_new / bc2) + ADAM_EPS) + ADAM_WD * w_ref[...])

    blk = pl.BlockSpec((rb, cols), lambda i: (i, 0))
    out = jax.ShapeDtypeStruct((rows, cols), F32)
    return pl.pallas_call(
        body, name=name, grid=(rows // rb,),
        in_specs=[pl.BlockSpec((N_DEV, rb, cols), lambda i: (0, i, 0)), blk, blk, blk],
        out_specs=[blk] * 4, out_shape=[out] * 4,
        compiler_params=_params(40, 1),
    )(parts, w, m, v)


def _cols_from_stack(stack):
    n, r, c = stack.shape
    return jnp.transpose(stack, (1, 0, 2)).reshape(r, n * c)


def _stack_from_cols(full):
    r, c8 = full.shape
    return jnp.transpose(full.reshape(r, N_DEV, c8 // N_DEV), (1, 0, 2))


def _block_diag(w):
    z = jnp.zeros((N_SLAB, 64, 64), w.dtype)
    pairs = w.reshape(N_SLAB, 2, 64, 64)
    top = jnp.concatenate([pairs[:, 0], z], axis=2)
    bottom = jnp.concatenate([z, pairs[:, 1]], axis=2)
    return jnp.concatenate([top, bottom], axis=1)


def _diag_blocks(w):
    return jnp.stack([w[:, :64, :64], w[:, 64:, 64:]], axis=1).reshape(8, 64, 64)


def _pack(arrays):
    flat = jnp.concatenate([a.reshape(-1) for a in arrays])
    rows = -(-flat.shape[0] // (8 * SLAB)) * 8
    return jnp.pad(flat, (0, rows * SLAB - flat.shape[0])).reshape(rows, SLAB)


def _unpack(packed, like):
    flat = packed.reshape(-1)
    out, at = [], 0
    for a in like:
        out.append(flat[at:at + a.size].reshape(a.shape))
        at += a.size
    return out


def kernel(x, ln1_g, w_in, pool_w, pool_scale, conv_w, conv_b, w_a, b_a, w_i, b_i, lam, gn_pool_g, gn_lru_g, w_out, ln2_g, w_ffn_gate, w_ffn_up, w_ffn_down, lnf_g, loss_target, m_ln1_g, m_w_in, m_pool_w, m_pool_scale, m_conv_w, m_conv_b, m_w_a, m_b_a, m_w_i, m_b_i, m_lam, m_gn_pool_g, m_gn_lru_g, m_w_out, m_ln2_g, m_w_ffn_gate, m_w_ffn_up, m_w_ffn_down, m_lnf_g, v_ln1_g, v_w_in, v_pool_w, v_pool_scale, v_conv_w, v_conv_b, v_w_a, v_b_a, v_w_i, v_b_i, v_lam, v_gn_pool_g, v_gn_lru_g, v_w_out, v_ln2_g, v_w_ffn_gate, v_w_ffn_up, v_w_ffn_down, v_lnf_g):
    weights = dict(ln1_g=ln1_g, w_in=w_in, pool_w=pool_w, pool_scale=pool_scale, conv_w=conv_w, conv_b=conv_b,
                   w_a=w_a, b_a=b_a, w_i=w_i, b_i=b_i, lam=lam, gn_pool_g=gn_pool_g, gn_lru_g=gn_lru_g,
                   w_out=w_out, ln2_g=ln2_g, w_ffn_gate=w_ffn_gate, w_ffn_up=w_ffn_up, w_ffn_down=w_ffn_down,
                   lnf_g=lnf_g)
    mom1 = dict(ln1_g=m_ln1_g, w_in=m_w_in, pool_w=m_pool_w, pool_scale=m_pool_scale, conv_w=m_conv_w,
                conv_b=m_conv_b, w_a=m_w_a, b_a=m_b_a, w_i=m_w_i, b_i=m_b_i, lam=m_lam, gn_pool_g=m_gn_pool_g,
                gn_lru_g=m_gn_lru_g, w_out=m_w_out, ln2_g=m_ln2_g, w_ffn_gate=m_w_ffn_gate,
                w_ffn_up=m_w_ffn_up, w_ffn_down=m_w_ffn_down, lnf_g=m_lnf_g)
    mom2 = dict(ln1_g=v_ln1_g, w_in=v_w_in, pool_w=v_pool_w, pool_scale=v_pool_scale, conv_w=v_conv_w,
                conv_b=v_conv_b, w_a=v_w_a, b_a=v_b_a, w_i=v_w_i, b_i=v_b_i, lam=v_lam, gn_pool_g=v_gn_pool_g,
                gn_lru_g=v_gn_lru_g, w_out=v_w_out, ln2_g=v_ln2_g, w_ffn_gate=v_w_ffn_gate,
                w_ffn_up=v_w_ffn_up, w_ffn_down=v_w_ffn_down, lnf_g=v_lnf_g)

    xs = x[0]
    target = loss_target[0]

    shard = dict(w_in=lambda a: a[0].T, w_ffn_gate=lambda a: a[0].T, w_ffn_up=lambda a: a[0].T,
                 w_out=lambda a: a[0], w_ffn_down=lambda a: a[0], conv_w=lambda a: a[0])
    unshard = dict(w_in=lambda a: a.T[None], w_ffn_gate=lambda a: a.T[None], w_ffn_up=lambda a: a.T[None],
                   w_out=lambda a: a[None], w_ffn_down=lambda a: a[None], conv_w=lambda a: a[None])

    gathered = ("w_in", "conv_w", "w_out", "w_ffn_gate", "w_ffn_up", "w_ffn_down")
    sources = [shard[k](weights[k]) if k == "conv_w" else shard[k](weights[k]).astype(BF16) for k in gathered]
    sems, sources, lands, _ = _push_start("gather_start", sources, [True] * 6, [(0, 1), (2,), (3, 4, 5)])
    g_in, g_conv = _push_wait("gather_wait_in", sems[0], sources[0:2], lands[0:2], [True] * 2, xs)
    w_in_f = g_in.reshape(IN_WIDTH, D_MODEL)
    conv_w_f = _cols_from_stack(g_conv)

    wa_bd = _block_diag(w_a[0])
    wi_bd = _block_diag(w_i[0])
    lnf_row = lnf_g.reshape(1, D_MODEL)

    u_pool, u_lru, u_gate = _fwd_in(xs, ln1_g, w_in_f)
    y_pool, h, y_lru = _mixer_fwd(u_pool, u_lru, u_gate, pool_w[0], pool_scale, conv_w_f, conv_b,
                                  wa_bd, b_a, wi_bd, b_i, lam)
    (g_out,) = _push_wait("gather_wait_out", sems[1], sources[2:3], lands[2:3], [True], y_pool)
    w_out_f = g_out.reshape(D_MODEL, D_MODEL)
    h1, n2 = _fwd_out(xs, y_pool, y_lru, gn_pool_g, gn_lru_g, w_out_f, ln2_g)
    g_gate, g_up, g_down = _push_wait("gather_wait_ffn", sems[2], sources[3:6], lands[3:6], [True] * 3, n2)
    w_gate_f = g_gate.reshape(D_FF, D_MODEL)
    w_up_f = g_up.reshape(D_FF, D_MODEL)
    w_down_f = g_down.reshape(D_FF, D_MODEL)
    g_act, u_act, dh2, dh2b, d_lnf, sq = _ffn_fwd(h1, n2, target, lnf_row, w_gate_f, w_up_f, w_down_f)
    loss = lax.psum((0.5 / D_MODEL) * jnp.sum(sq), ("x", "y", "c"))

    d_gate, d_up, d_down, dn2 = _ffn_bwd(n2, dh2b, g_act, u_act, w_gate_f, w_up_f, w_down_f)
    ffn_stacks = [d.reshape(N_DEV, D_FF // N_DEV, D_MODEL) for d in (d_gate, d_up, d_down)]
    ffn_sems, ffn_stacks, ffn_lands, started = _push_start("ffn_grads_start", ffn_stacks, [False] * 3, [(0, 1, 2)])
    dh1, dy_pool, dy_lru, d_out, d_ln2, d_gnp, d_gnl = _bwd_out(dn2, dh2, h1, y_pool, y_lru, gn_pool_g, gn_lru_g,
                                                                 w_out_f, ln2_g, started)
    (du_pool, du_lru, du_gate, d_pw, d_ps, d_cw, d_cb, d_wa, d_ba, d_wi, d_bi, d_lam) = _mixer_bwd(
        u_pool, u_lru, u_gate, h, dy_pool, dy_lru, pool_w[0], pool_scale, conv_w_f, conv_b,
        wa_bd, b_a, wi_bd, b_i, lam)
    grad_x, d_in, d_ln1 = _bwd_in(xs, dh1, du_pool, du_lru, du_gate, ln1_g, w_in_f)

    small_grads = dict(ln1_g=d_ln1, pool_w=d_pw[None], pool_scale=d_ps, conv_b=d_cb, w_a=_diag_blocks(d_wa)[None],
                       b_a=d_ba, w_i=_diag_blocks(d_wi)[None], b_i=d_bi, lam=d_lam, gn_pool_g=d_gnp,
                       gn_lru_g=d_gnl, ln2_g=d_ln2, lnf_g=d_lnf.reshape(D_MODEL))
    small_like = [weights[k] for k in SMALL_NAMES]
    packed = _pack([small_grads[k] for k in SMALL_NAMES])

    stacks = [d_in.reshape(N_DEV, IN_WIDTH // N_DEV, D_MODEL),
              d_out.reshape(N_DEV, D_MODEL // N_DEV, D_MODEL),
              _stack_from_cols(d_cw)]
    r_in, r_out, r_conv, r_small = _exchange(stacks, packed)
    r_gate, r_up, r_down = _push_wait("ffn_grads_wait", ffn_sems[0], ffn_stacks, ffn_lands, [False] * 3, r_small)

    results = {}
    for name, parts in (("w_in", r_in), ("w_out", r_out), ("w_ffn_gate", r_gate), ("w_ffn_up", r_up),
                        ("w_ffn_down", r_down), ("conv_w", r_conv)):
        outs = _reduce_adam(parts, shard[name](weights[name]), shard[name](mom1[name]), shard[name](mom2[name]),
                            "adam_" + name)
        results[name] = tuple(unshard[name](o) for o in outs)
    sg, sd, sm1, sm2 = _reduce_adam(r_small, _pack(small_like), _pack([mom1[k] for k in SMALL_NAMES]),
                                    _pack([mom2[k] for k in SMALL_NAMES]), "adam_small")
    for k, g, d, m1, m2 in zip(SMALL_NAMES, _unpack(sg, small_like), _unpack(sd, small_like),
                               _unpack(sm1, small_like), _unpack(sm2, small_like)):
        results[k] = (g, d, m1, m2)

    order = ["ln1_g", "w_in", "pool_w", "pool_scale", "conv_w", "conv_b", "w_a", "b_a", "w_i", "b_i", "lam",
             "gn_pool_g", "gn_lru_g", "w_out", "ln2_g", "w_ffn_gate", "w_ffn_up", "w_ffn_down", "lnf_g"]
    return (loss, grad_x[None],
            *[results[k][0] for k in order], *[results[k][1] for k in order],
            *[results[k][2] for k in order], *[results[k][3] for k in order])
```

```python
import jax
import jax.numpy as jnp
from jax import lax
from jax.experimental import pallas as pl
from jax.experimental.pallas import tpu as pltpu

F32 = jnp.float32
BF16 = jnp.bfloat16

N_DEV = 8
D_MODEL = 1024
POOL_WIDTH = 512
LRU_WIDTH = 512
IN_WIDTH = 1536
D_FF = 2816
N_SLAB = 4
SLAB = 128
CONV_WIDTH = 4
LRU_C = 8.0
EPS = 1e-6
HALO = 16
FF_CHUNK = 256

ADAM_LR = 0.001
ADAM_B1 = 0.9
ADAM_B2 = 0.999
ADAM_EPS = 1e-08
ADAM_WD = 0.01
ADAM_STEP = 10

MIB = 1 << 20
MESH = pl.DeviceIdType.MESH


def _params(vmem_mib, n_axes=0):
    sem = ("arbitrary",) * n_axes if n_axes else None
    return pltpu.CompilerParams(dimension_semantics=sem, vmem_limit_bytes=vmem_mib * MIB)


def _mm(a, b):
    return jnp.dot(a, b, preferred_element_type=F32)


def _mm_nt(a, b):
    return lax.dot_general(a, b, (((1,), (1,)), ((), ())), preferred_element_type=F32)


def _mm_tn(a, b):
    return lax.dot_general(a, b, (((0,), (0,)), ((), ())), preferred_element_type=F32)


def _rms(x, g):
    rstd = lax.rsqrt(jnp.mean(x * x, axis=-1, keepdims=True) + EPS)
    xhat = x * rstd
    return xhat * g, xhat, rstd


def _rms_bwd(dy, xhat, rstd, g):
    gy = dy * g
    dx = rstd * (gy - xhat * jnp.mean(gy * xhat, axis=-1, keepdims=True))
    return dx, jnp.sum(dy * xhat, axis=0, keepdims=True)


def _gelu(z):
    t = jnp.tanh(0.7978845608028654 * (z + 0.044715 * z * z * z))
    return 0.5 * z * (1.0 + t), t


def _gelu_grad(z, t):
    return 0.5 * (1.0 + t) + 0.5 * z * (1.0 - t * t) * 0.7978845608028654 * (1.0 + 3.0 * 0.044715 * z * z)


def _softplus_neg(lam):
    x = -lam
    e = jnp.exp(-jnp.abs(x))
    u = 1.0 + e
    l1p = jnp.where(u == 1.0, e, jnp.log(u) * e / (u - 1.0))
    return jnp.maximum(x, 0.0) + l1p


def _expm1(x):
    p = x * (1.0 + x * (0.5 + x * (1.0 / 6.0 + x * (1.0 / 24.0 + x * (1.0 / 120.0)))))
    return jnp.where(jnp.abs(x) < 0.1, p, jnp.exp(x) - 1.0)


def _down(v, d):
    return pltpu.roll(v, d, 0)


def _up(v, d):
    return pltpu.roll(v, v.shape[0] - d, 0)


def _token_block(s):
    return 512 if s % 512 == 0 and s > 512 else 256


def _time_chunk(s):
    return 256 if s % 256 == 0 else s


def _fwd_in(x, ln1_g, w_in_t):
    s = x.shape[0]
    tb = _token_block(s)

    def body(x_ref, g_ref, w_ref, up_ref, ul_ref, ug_ref):
        n, _, _ = _rms(x_ref[...], g_ref[...])
        proj = _mm_nt(n.astype(BF16), w_ref[...])
        up_ref[...] = proj[:, :POOL_WIDTH]
        ul_ref[...] = proj[:, POOL_WIDTH:POOL_WIDTH + LRU_WIDTH]
        ug_ref[...] = proj[:, POOL_WIDTH + LRU_WIDTH:]

    out = jax.ShapeDtypeStruct((s, 512), F32)
    return pl.pallas_call(
        body, name="fwd_in", grid=(s // tb,),
        in_specs=[pl.BlockSpec((tb, D_MODEL), lambda i: (i, 0)),
                  pl.BlockSpec((1, D_MODEL), lambda i: (0, 0)),
                  pl.BlockSpec((IN_WIDTH, D_MODEL), lambda i: (0, 0))],
        out_specs=[pl.BlockSpec((tb, 512), lambda i: (i, 0))] * 3,
        out_shape=[out, out, out],
        compiler_params=_params(40, 1),
    )(x, ln1_g, w_in_t)


def _pool_denominator(t0, row, window):
    return jnp.minimum((t0 + row + 1).astype(F32), window)


def _causal_window(ext, onehot):
    s2 = ext + _down(ext, 1)
    s4 = s2 + _down(s2, 2)
    s8 = s4 + _down(s4, 4)
    s16 = s8 + _down(s8, 8)
    return (onehot[0] * s2 + onehot[1] * s4 + onehot[2] * s8 + onehot[3] * s16)[HALO:]


def _anticausal_window(ext, onehot, rows):
    s2 = ext + _up(ext, 1)
    s4 = s2 + _up(s2, 2)
    s8 = s4 + _up(s4, 4)
    s16 = s8 + _up(s8, 8)
    return (onehot[0] * s2 + onehot[1] * s4 + onehot[2] * s8 + onehot[3] * s16)[:rows]


def _conv_taps(ext):
    return [ext[HALO:], _down(ext, 1)[HALO:], _down(ext, 2)[HALO:], _down(ext, 3)[HALO:]]


def _conv(taps, cw, cb):
    return cw[3:4] * taps[0] + cw[2:3] * taps[1] + cw[1:2] * taps[2] + cw[0:1] * taps[3] + cb


def _lru_gates(xc, wa, ba, wi, bi, sp):
    xb = xc.astype(BF16)
    r = jax.nn.sigmoid(_mm(xb, wa) + ba)
    i = jax.nn.sigmoid(_mm(xb, wi) + bi)
    la = (-LRU_C) * r * sp
    a = jnp.exp(la)
    mult = jnp.sqrt(jnp.maximum(-_expm1(2.0 * la), 0.0))
    return xb, r, i, a, mult


def _scan_causal(a, b, h_prev, row, rows):
    d = 1
    while d < rows:
        head = row < d
        a_sh = jnp.where(head, 1.0, _down(a, d))
        b_sh = jnp.where(head, 0.0, _down(b, d))
        b = a * b_sh + b
        a = a * a_sh
        d *= 2
    return b + a * h_prev


def _scan_anticausal(a, b, l_next, row, rows):
    d = 1
    while d < rows:
        tail = row >= rows - d
        a_sh = jnp.where(tail, 1.0, _up(a, d))
        b_sh = jnp.where(tail, 0.0, _up(b, d))
        b = a * b_sh + b
        a = a * a_sh
        d *= 2
    return b + a * l_next


def _slab_scalars():
    slab = pl.program_id(0)
    onehot = [jnp.where(slab == k, 1.0, 0.0).astype(F32) for k in range(N_SLAB)]
    window = jnp.left_shift(jnp.int32(2), slab).astype(F32)
    return onehot, window


def _slab_specs(s):
    seq = pl.BlockSpec((s, SLAB), lambda k: (0, k))
    mat = pl.BlockSpec((1, SLAB, SLAB), lambda k: (k, 0, 0))
    vec = pl.BlockSpec((1, SLAB), lambda k: (0, k))
    taps = pl.BlockSpec((CONV_WIDTH, SLAB), lambda k: (0, k))
    return seq, mat, vec, taps


def _mixer_fwd(u_pool, u_lru, u_gate, pool_w, pool_scale, conv_w, conv_b, wa_bd, b_a, wi_bd, b_i, lam):
    s = u_pool.shape[0]
    tc = _time_chunk(s)
    n_chunks = s // tc

    def body(up_ref, ul_ref, ug_ref, pw_ref, ps_ref, cw_ref, cb_ref, wa_ref, ba_ref, wi_ref, bi_ref, lam_ref,
             yp_ref, h_ref, yl_ref):
        onehot, window = _slab_scalars()
        pw = pw_ref[0].astype(BF16)
        wa = wa_ref[0].astype(BF16)
        wi = wi_ref[0].astype(BF16)
        ps, cw, cb, ba, bi = ps_ref[...], cw_ref[...], cb_ref[...], ba_ref[...], bi_ref[...]
        sp = _softplus_neg(lam_ref[...])
        row = lax.broadcasted_iota(jnp.int32, (tc, SLAB), 0)

        def chunk(t0, ext_p, ext_l, h_prev):
            rows = pl.ds(t0, tc)
            d = _causal_window(ext_p, onehot) / _pool_denominator(t0, row, window) - ext_p[HALO:]
            yp_ref[rows, :] = _mm(d.astype(BF16), pw) * ps
            xc = _conv(_conv_taps(ext_l), cw, cb)
            _, _, i, a, mult = _lru_gates(xc, wa, ba, wi, bi, sp)
            h = _scan_causal(a, mult * (i * xc), h_prev, row, tc)
            h_ref[rows, :] = h
            yl_ref[rows, :] = h * _gelu(ug_ref[rows, :])[0]
            return h[tc - 1:tc, :]

        pad = jnp.zeros((HALO, SLAB), F32)
        h0 = chunk(0, jnp.concatenate([pad, up_ref[pl.ds(0, tc), :]], axis=0),
                   jnp.concatenate([pad, ul_ref[pl.ds(0, tc), :]], axis=0), jnp.zeros((1, SLAB), F32))

        def step(c, h_prev):
            t0 = pl.multiple_of(c * tc, tc)
            ext = pl.ds(pl.multiple_of(c * tc - HALO, HALO), tc + HALO)
            return chunk(t0, up_ref[ext, :], ul_ref[ext, :], h_prev)

        lax.fori_loop(1, n_chunks, step, h0)

    seq, mat, vec, taps = _slab_specs(s)
    out = jax.ShapeDtypeStruct((s, 512), F32)
    return pl.pallas_call(
        body, name="mixer_fwd", grid=(N_SLAB,),
        in_specs=[seq, seq, seq, mat, vec, taps, vec, mat, vec, mat, vec, vec],
        out_specs=[seq, seq, seq], out_shape=[out, out, out],
        compiler_params=_params(48, 1),
    )(u_pool, u_lru, u_gate, pool_w, pool_scale, conv_w, conv_b, wa_bd, b_a, wi_bd, b_i, lam)


def _fwd_out(x, y_pool, y_lru, gn_pool_g, gn_lru_g, w_out, ln2_g):
    s = x.shape[0]
    tb = _token_block(s)

    def body(x_ref, yp_ref, yl_ref, gp_ref, gl_ref, w_ref, g2_ref, h1_ref, n2_ref):
        mp, _, _ = _rms(yp_ref[...], gp_ref[...])
        ml, _, _ = _rms(yl_ref[...], gl_ref[...])
        h1 = x_ref[...] + _mm(mp.astype(BF16), w_ref[:POOL_WIDTH, :]) + _mm(ml.astype(BF16), w_ref[POOL_WIDTH:, :])
        h1_ref[...] = h1
        n2_ref[...] = _rms(h1, g2_ref[...])[0].astype(BF16)

    row = pl.BlockSpec((tb, D_MODEL), lambda i: (i, 0))
    half = pl.BlockSpec((tb, 512), lambda i: (i, 0))
    return pl.pallas_call(
        body, name="fwd_out", grid=(s // tb,),
        in_specs=[row, half, half, pl.BlockSpec((1, 512), lambda i: (0, 0)), pl.BlockSpec((1, 512), lambda i: (0, 0)),
                  pl.BlockSpec((D_MODEL, D_MODEL), lambda i: (0, 0)), pl.BlockSpec((1, D_MODEL), lambda i: (0, 0))],
        out_specs=[row, row],
        out_shape=[jax.ShapeDtypeStruct((s, D_MODEL), F32), jax.ShapeDtypeStruct((s, D_MODEL), BF16)],
        compiler_params=_params(40, 1),
    )(x, y_pool, y_lru, gn_pool_g, gn_lru_g, w_out, ln2_g)


def _ffn_fwd(h1, n2, target, lnf_g, w_gate, w_up, w_down):
    s = h1.shape[0]
    tb = 256
    n_ff = D_FF // FF_CHUNK

    def body(h1_ref, n2_ref, t_ref, gf_ref, wg_hbm, wu_hbm, wd_hbm,
             g_ref, u_ref, dh_ref, dhb_ref, dgf_ref, sq_ref, wg, wu, wd, sem):
        @pl.when(pl.program_id(0) == 0)
        def _():
            loads = [pltpu.make_async_copy(src, dst, sem.at[k])
                     for k, (src, dst) in enumerate(((wg_hbm, wg), (wu_hbm, wu), (wd_hbm, wd)))]
            for cp in loads:
                cp.start()
            for cp in loads:
                cp.wait()
            dgf_ref[...] = jnp.zeros_like(dgf_ref)
            sq_ref[...] = jnp.zeros_like(sq_ref)

        n2v = n2_ref[...]
        acc = jnp.zeros((tb, D_MODEL), F32)
        for c in range(n_ff):
            cols = slice(c * FF_CHUNK, (c + 1) * FF_CHUNK)
            g = _mm_nt(n2v, wg[cols, :])
            u = _mm_nt(n2v, wu[cols, :])
            g_ref[:, cols] = g.astype(BF16)
            u_ref[:, cols] = u.astype(BF16)
            act = g * jax.nn.sigmoid(g) * u
            acc = acc + _mm(act.astype(BF16), wd[cols, :])
        gf = gf_ref[...]
        y, xhat, rstd = _rms(h1_ref[...] + acc, gf)
        err = y - t_ref[...]
        sq_ref[...] += jnp.sum(err * err, axis=0, keepdims=True)
        dh2, dgf = _rms_bwd(err * (1.0 / D_MODEL), xhat, rstd, gf)
        dgf_ref[...] += dgf
        dh_ref[...] = dh2
        dhb_ref[...] = dh2.astype(BF16)

    row = pl.BlockSpec((tb, D_MODEL), lambda i: (i, 0))
    ff = pl.BlockSpec((tb, D_FF), lambda i: (i, 0))
    vec = pl.BlockSpec((1, D_MODEL), lambda i: (0, 0))
    anyspace = pl.BlockSpec(memory_space=pl.ANY)
    return pl.pallas_call(
        body, name="ffn_fwd", grid=(s // tb,),
        in_specs=[row, row, row, vec, anyspace, anyspace, anyspace],
        out_specs=[ff, ff, row, row, vec, vec],
        out_shape=[jax.ShapeDtypeStruct((s, D_FF), BF16), jax.ShapeDtypeStruct((s, D_FF), BF16),
                   jax.ShapeDtypeStruct((s, D_MODEL), F32), jax.ShapeDtypeStruct((s, D_MODEL), BF16),
                   jax.ShapeDtypeStruct((1, D_MODEL), F32), jax.ShapeDtypeStruct((1, D_MODEL), F32)],
        scratch_shapes=[pltpu.VMEM((D_FF, D_MODEL), BF16), pltpu.VMEM((D_FF, D_MODEL), BF16),
                        pltpu.VMEM((D_FF, D_MODEL), BF16), pltpu.SemaphoreType.DMA((3,))],
        compiler_params=_params(56, 1),
    )(h1, n2, target, lnf_g, w_gate, w_up, w_down)


def _ffn_bwd(n2, dh2b, g, u, w_gate_t, w_up_t, w_down):
    s = n2.shape[0]
    tb = min(1024, s)
    n_ff = D_FF // FF_CHUNK
    n_tb = s // tb

    def body(n2_ref, dh_ref, g_ref, u_ref, wg_ref, wu_ref, wd_ref, dwg_ref, dwu_ref, dwd_ref, dn2_ref,
             dn2_acc, acc_g, acc_u, acc_d):
        j = pl.program_id(0)
        t = pl.program_id(1)

        @pl.when(t == 0)
        def _():
            acc_g[...] = jnp.zeros_like(acc_g)
            acc_u[...] = jnp.zeros_like(acc_u)
            acc_d[...] = jnp.zeros_like(acc_d)

        n2v = n2_ref[...]
        dh = dh_ref[...]
        gv = g_ref[...].astype(F32)
        uv = u_ref[...].astype(F32)
        sg = jax.nn.sigmoid(gv)
        silu = gv * sg
        dact = _mm_nt(dh, wd_ref[...])
        dub = (dact * silu).astype(BF16)
        dgb = (dact * uv * (sg * (1.0 + gv * (1.0 - sg)))).astype(BF16)
        acc_d[...] += _mm_tn((silu * uv).astype(BF16), dh)
        acc_g[...] += _mm_tn(dgb, n2v)
        acc_u[...] += _mm_tn(dub, n2v)
        part = _mm(dgb, wg_ref[...]) + _mm(dub, wu_ref[...])
        rows = pl.ds(pl.multiple_of(t * tb, tb), tb)

        @pl.when(t == n_tb - 1)
        def _():
            dwg_ref[...] = acc_g[...].astype(BF16)
            dwu_ref[...] = acc_u[...].astype(BF16)
            dwd_ref[...] = acc_d[...].astype(BF16)

        @pl.when(j == 0)
        def _():
            dn2_acc[rows, :] = part

        @pl.when(jnp.logical_and(j > 0, j < n_ff - 1))
        def _():
            dn2_acc[rows, :] += part

        @pl.when(j == n_ff - 1)
        def _():
            dn2_ref[...] = dn2_acc[rows, :] + part

    row = pl.BlockSpec((tb, D_MODEL), lambda j, t: (t, 0))
    act = pl.BlockSpec((tb, FF_CHUNK), lambda j, t: (t, j))
    w_row = pl.BlockSpec((FF_CHUNK, D_MODEL), lambda j, t: (j, 0))
    last = pl.BlockSpec((tb, D_MODEL), lambda j, t: (jnp.where(j == n_ff - 1, t, 0), 0))
    grad = jax.ShapeDtypeStruct((D_FF, D_MODEL), BF16)
    chunk_acc = pltpu.VMEM((FF_CHUNK, D_MODEL), F32)
    return pl.pallas_call(
        body, name="ffn_bwd", grid=(n_ff, n_tb),
        in_specs=[row, row, act, act, w_row, w_row, w_row],
        out_specs=[w_row, w_row, w_row, last],
        out_shape=[grad, grad, grad, jax.ShapeDtypeStruct((s, D_MODEL), F32)],
        scratch_shapes=[pltpu.VMEM((s, D_MODEL), F32), chunk_acc, chunk_acc, chunk_acc],
        compiler_params=_params(56, 2),
    )(n2, dh2b, g, u, w_gate_t, w_up_t, w_down)


def _bwd_out(dn2, dh2, h1, y_pool, y_lru, gn_pool_g, gn_lru_g, w_out, ln2_g, after):
    s = h1.shape[0]
    tb = _token_block(s)

    def body(dn2_ref, dh2_ref, h1_ref, yp_ref, yl_ref, gp_ref, gl_ref, w_ref, g2_ref, _after,
             dh1_ref, dyp_ref, dyl_ref, dwb_ref, dg2_ref, dgp_ref, dgl_ref, dw_ref):
        @pl.when(pl.program_id(0) == 0)
        def _():
            dw_ref[...] = jnp.zeros_like(dw_ref)
            dg2_ref[...] = jnp.zeros_like(dg2_ref)
            dgp_ref[...] = jnp.zeros_like(dgp_ref)
            dgl_ref[...] = jnp.zeros_like(dgl_ref)

        g2 = g2_ref[...]
        _, xhat2, rstd2 = _rms(h1_ref[...], g2)
        dres, dg2 = _rms_bwd(dn2_ref[...], xhat2, rstd2, g2)
        dg2_ref[...] += dg2
        dh1 = dh2_ref[...] + dres
        dh1_ref[...] = dh1
        dh1b = dh1.astype(BF16)
        gp, gl = gp_ref[...], gl_ref[...]
        mp, xhat_p, rstd_p = _rms(yp_ref[...], gp)
        ml, xhat_l, rstd_l = _rms(yl_ref[...], gl)
        dw_ref[:POOL_WIDTH, :] += _mm_tn(mp.astype(BF16), dh1b)
        dw_ref[POOL_WIDTH:, :] += _mm_tn(ml.astype(BF16), dh1b)
        dyp, dgp = _rms_bwd(_mm_nt(dh1b, w_ref[:POOL_WIDTH, :]), xhat_p, rstd_p, gp)
        dyl, dgl = _rms_bwd(_mm_nt(dh1b, w_ref[POOL_WIDTH:, :]), xhat_l, rstd_l, gl)
        dyp_ref[...] = dyp
        dyl_ref[...] = dyl
        dgp_ref[...] += dgp
        dgl_ref[...] += dgl

        @pl.when(pl.program_id(0) == s // tb - 1)
        def _():
            dwb_ref[...] = dw_ref[...].astype(BF16)

    row = pl.BlockSpec((tb, D_MODEL), lambda i: (i, 0))
    half = pl.BlockSpec((tb, 512), lambda i: (i, 0))
    vec = pl.BlockSpec((1, D_MODEL), lambda i: (0, 0))
    hvec = pl.BlockSpec((1, 512), lambda i: (0, 0))
    mat = pl.BlockSpec((D_MODEL, D_MODEL), lambda i: (0, 0))
    return pl.pallas_call(
        body, name="bwd_out", grid=(s // tb,),
        in_specs=[row, row, row, half, half, hvec, hvec, mat, vec, pl.BlockSpec(memory_space=pl.ANY)],
        out_specs=[row, half, half, mat, vec, hvec, hvec],
        out_shape=[jax.ShapeDtypeStruct((s, D_MODEL), F32), jax.ShapeDtypeStruct((s, 512), F32),
                   jax.ShapeDtypeStruct((s, 512), F32), jax.ShapeDtypeStruct((D_MODEL, D_MODEL), BF16),
                   jax.ShapeDtypeStruct((1, D_MODEL), F32), jax.ShapeDtypeStruct((1, 512), F32),
                   jax.ShapeDtypeStruct((1, 512), F32)],
        scratch_shapes=[pltpu.VMEM((D_MODEL, D_MODEL), F32)],
        compiler_params=_params(48, 1),
    )(dn2, dh2, h1, y_pool, y_lru, gn_pool_g, gn_lru_g, w_out, ln2_g, after)


def _mixer_bwd(u_pool, u_lru, u_gate, h, dy_pool, dy_lru,
               pool_w, pool_scale, conv_w, conv_b, wa_bd, b_a, wi_bd, b_i, lam):
    s = u_pool.shape[0]
    tc = _time_chunk(s)
    n_chunks = s // tc

    def body(up_ref, ul_ref, ug_ref, h_ref, dyp_ref, dyl_ref,
             pw_ref, ps_ref, cw_ref, cb_ref, wa_ref, ba_ref, wi_ref, bi_ref, lam_ref,
             dup_ref, dul_ref, dug_ref, dpw_ref, dps_ref, dcw_ref, dcb_ref, dwa_ref, dba_ref, dwi_ref, dbi_ref, dlam_ref):
        onehot, window = _slab_scalars()
        pw = pw_ref[0].astype(BF16)
        wa = wa_ref[0].astype(BF16)
        wi = wi_ref[0].astype(BF16)
        ps, cw, cb, ba, bi = ps_ref[...], cw_ref[...], cb_ref[...], ba_ref[...], bi_ref[...]
        lam_v = lam_ref[...]
        sp = _softplus_neg(lam_v)
        row = lax.broadcasted_iota(jnp.int32, (tc, SLAB), 0)
        for ref in (dpw_ref, dps_ref, dcw_ref, dcb_ref, dwa_ref, dba_ref, dwi_ref, dbi_ref, dlam_ref):
            ref[...] = jnp.zeros_like(ref)

        def chunk(t0, ext_p, ext_l, ext_h, carry):
            l_next, a_next, dxc_next, ddn_next = carry
            rows = pl.ds(t0, tc)
            taps = _conv_taps(ext_l)
            xc = _conv(taps, cw, cb)
            xb, r, i, a, mult = _lru_gates(xc, wa, ba, wi, bi, sp)
            hv = ext_h[HALO:]
            h_before = _down(ext_h, 1)[HALO:]
            ug = ug_ref[rows, :]
            dyl = dyl_ref[rows, :]
            gel, th = _gelu(ug)
            dug_ref[rows, :] = dyl * hv * _gelu_grad(ug, th)
            a_after = jnp.where(row == tc - 1, a_next, _up(a, 1))
            l = _scan_anticausal(a_after, dyl * gel, l_next, row, tc)
            dmult = l * (i * xc)
            di = l * mult * xc
            dxc = l * mult * i
            dla = (l * h_before) * a - jnp.where(mult > 0.0, dmult * (a * a) / mult, 0.0)
            dlam_ref[...] += jnp.sum(dla * r, axis=0, keepdims=True)
            dpa = (dla * ((-LRU_C) * sp)) * (r * (1.0 - r))
            dpi = di * (i * (1.0 - i))
            dpab = dpa.astype(BF16)
            dpib = dpi.astype(BF16)
            dwa_ref[0] += _mm_tn(xb, dpab)
            dwi_ref[0] += _mm_tn(xb, dpib)
            dba_ref[...] += jnp.sum(dpa, axis=0, keepdims=True)
            dbi_ref[...] += jnp.sum(dpi, axis=0, keepdims=True)
            dxc = dxc + _mm_nt(dpab, wa) + _mm_nt(dpib, wi)
            ext_d = jnp.concatenate([dxc, dxc_next], axis=0)
            dul_ref[rows, :] = (cw[3:4] * dxc + cw[2:3] * _up(ext_d, 1)[:tc]
                                + cw[1:2] * _up(ext_d, 2)[:tc] + cw[0:1] * _up(ext_d, 3)[:tc])
            for k in range(CONV_WIDTH):
                dcw_ref[k:k + 1, :] += jnp.sum(dxc * taps[CONV_WIDTH - 1 - k], axis=0, keepdims=True)
            dcb_ref[...] += jnp.sum(dxc, axis=0, keepdims=True)
            denom = _pool_denominator(t0, row, window)
            db = (_causal_window(ext_p, onehot) / denom - ext_p[HALO:]).astype(BF16)
            dyp = dyp_ref[rows, :]
            dps_ref[...] += jnp.sum(dyp * _mm(db, pw), axis=0, keepdims=True)
            dys = (dyp * ps).astype(BF16)
            dpw_ref[0] += _mm_tn(db, dys)
            dd = _mm_nt(dys, pw)
            ddn = dd / denom
            ext_q = jnp.concatenate([ddn, ddn_next], axis=0)
            dup_ref[rows, :] = _anticausal_window(ext_q, onehot, tc) - dd
            return l[0:1, :], a[0:1, :], dxc[0:8, :], ddn[0:HALO, :]

        def step(k, carry):
            c = n_chunks - 1 - k
            t0 = pl.multiple_of(c * tc, tc)
            ext = pl.ds(pl.multiple_of(c * tc - HALO, HALO), tc + HALO)
            return chunk(t0, up_ref[ext, :], ul_ref[ext, :], h_ref[ext, :], carry)

        carry = (jnp.zeros((1, SLAB), F32), jnp.zeros((1, SLAB), F32),
                 jnp.zeros((8, SLAB), F32), jnp.zeros((HALO, SLAB), F32))
        carry = lax.fori_loop(0, n_chunks - 1, step, carry)
        pad = jnp.zeros((HALO, SLAB), F32)
        first = pl.ds(0, tc)
        chunk(0, jnp.concatenate([pad, up_ref[first, :]], axis=0), jnp.concatenate([pad, ul_ref[first, :]], axis=0),
              jnp.concatenate([pad, h_ref[first, :]], axis=0), carry)
        dlam_ref[...] = dlam_ref[...] * (LRU_C * jax.nn.sigmoid(-lam_v))

    seq, mat, vec, taps = _slab_specs(s)
    full = jax.ShapeDtypeStruct((s, 512), F32)
    mats = jax.ShapeDtypeStruct((N_SLAB, SLAB, SLAB), F32)
    vecs = jax.ShapeDtypeStruct((1, 512), F32)
    return pl.pallas_call(
        body, name="mixer_bwd", grid=(N_SLAB,),
        in_specs=[seq] * 6 + [mat, vec, taps, vec, mat, vec, mat, vec, vec],
        out_specs=[seq, seq, seq, mat, vec, taps, vec, mat, vec, mat, vec, vec],
        out_shape=[full, full, full, mats, vecs, jax.ShapeDtypeStruct((CONV_WIDTH, 512), F32), vecs,
                   mats, vecs, mats, vecs, vecs],
        compiler_params=_params(56, 1),
    )(u_pool, u_lru, u_gate, h, dy_pool, dy_lru, pool_w, pool_scale, conv_w, conv_b, wa_bd, b_a, wi_bd, b_i, lam)


def _bwd_in(x, dh1, du_pool, du_lru, du_gate, ln1_g, w_in_t):
    s = x.shape[0]
    tb = _token_block(s)

    def body(x_ref, dh1_ref, dup_ref, dul_ref, dug_ref, g_ref, w_ref, dx_ref, dwb_ref, dg_ref, dw_ref):
        @pl.when(pl.program_id(0) == 0)
        def _():
            dw_ref[...] = jnp.zeros_like(dw_ref)
            dg_ref[...] = jnp.zeros_like(dg_ref)

        g1 = g_ref[...]
        n, xhat, rstd = _rms(x_ref[...], g1)
        nb = n.astype(BF16)
        dn = jnp.zeros((tb, D_MODEL), F32)
        for k, ref in enumerate((dup_ref, dul_ref, dug_ref)):
            rows = slice(k * 512, (k + 1) * 512)
            db = ref[...].astype(BF16)
            dw_ref[rows, :] += _mm_tn(db, nb)
            dn = dn + _mm(db, w_ref[rows, :])
        dx, dg1 = _rms_bwd(dn, xhat, rstd, g1)
        dx_ref[...] = dh1_ref[...] + dx
        dg_ref[...] += dg1

        @pl.when(pl.program_id(0) == s // tb - 1)
        def _():
            dwb_ref[...] = dw_ref[...].astype(BF16)

    row = pl.BlockSpec((tb, D_MODEL), lambda i: (i, 0))
    half = pl.BlockSpec((tb, 512), lambda i: (i, 0))
    vec = pl.BlockSpec((1, D_MODEL), lambda i: (0, 0))
    mat = pl.BlockSpec((IN_WIDTH, D_MODEL), lambda i: (0, 0))
    return pl.pallas_call(
        body, name="bwd_in", grid=(s // tb,),
        in_specs=[row, row, half, half, half, vec, mat],
        out_specs=[row, mat, vec],
        out_shape=[jax.ShapeDtypeStruct((s, D_MODEL), F32), jax.ShapeDtypeStruct((IN_WIDTH, D_MODEL), BF16),
                   jax.ShapeDtypeStruct((1, D_MODEL), F32)],
        scratch_shapes=[pltpu.VMEM((IN_WIDTH, D_MODEL), F32)],
        compiler_params=_params(48, 1),
    )(x, dh1, du_pool, du_lru, du_gate, ln1_g, w_in_t)


def _mesh_position():
    x, y, c = lax.axis_index("x"), lax.axis_index("y"), lax.axis_index("c")
    return x, y, c, 4 * x + 2 * y + c


def _peer(x, y, c, p):
    px = 1 - x if p & 4 else x
    py = 1 - y if p & 2 else y
    pc = 1 - c if p & 1 else c
    return (px, py, pc), 4 * px + 2 * py + pc


HBM_SPEC = pl.BlockSpec(memory_space=pltpu.HBM)
SEM_SPEC = pl.BlockSpec(memory_space=pltpu.SEMAPHORE)
DATAFLOW = pltpu.SideEffectType.DATAFLOW_SIDE_EFFECTING


def _part(ref, whole, index):
    return ref if whole else ref.at[index]


def _push_start(name, sources, whole, groups):
    n = len(sources)
    lands = [lax.empty((N_DEV,) + (a.shape if w else a.shape[1:]), a.dtype) for a, w in zip(sources, whole)]
    n_sem = 2 * len(groups)

    def body(*refs):
        src, land = refs[:n], refs[n:2 * n]
        sems = refs[2 * n:2 * n + n_sem]
        token = refs[-1]
        x, y, c, me = _mesh_position()
        for k, group in enumerate(groups):
            for p in range(1, N_DEV):
                peer, peer_index = _peer(x, y, c, p)
                for i, m in enumerate(group):
                    pltpu.make_async_remote_copy(_part(src[m], whole[m], peer_index), land[m].at[me],
                                                 sems[2 * k].at[i * N_DEV + p], sems[2 * k + 1].at[i * N_DEV + p],
                                                 device_id=peer, device_id_type=MESH).start()
        token[...] = jnp.zeros_like(token)

    sem_shapes = []
    for group in groups:
        sem_shapes += [pltpu.SemaphoreType.DMA((len(group) * N_DEV,))] * 2
    outs = pl.pallas_call(
        body, name=name,
        in_specs=[HBM_SPEC] * (2 * n),
        out_specs=[SEM_SPEC] * n_sem + [HBM_SPEC] * (2 * n) + [pl.BlockSpec(memory_space=pltpu.VMEM)],
        out_shape=sem_shapes + [pltpu.HBM(a.shape, a.dtype) for a in list(sources) + lands]
        + [jax.ShapeDtypeStruct((8, SLAB), F32)],
        input_output_aliases={i: n_sem + i for i in range(2 * n)},
        compiler_params=pltpu.CompilerParams(has_side_effects=DATAFLOW),
    )(*[pltpu.with_memory_space_constraint(a, pltpu.HBM) for a in list(sources) + lands])
    sems = [(outs[2 * k], outs[2 * k + 1]) for k in range(len(groups))]
    return sems, outs[n_sem:n_sem + n], outs[n_sem + n:n_sem + 2 * n], outs[-1]


def _push_wait(name, sems, sources, lands, whole, after):
    n = len(sources)
    send_sems, recv_sems = sems

    def body(*refs):
        src, land = refs[:n], refs[n:2 * n]
        send, recv = refs[2 * n], refs[2 * n + 1]
        local_sems = refs[-1]
        x, y, c, me = _mesh_position()
        local = [pltpu.make_async_copy(_part(src[m], whole[m], me), land[m].at[me], local_sems.at[m])
                 for m in range(n)]
        for cp in local:
            cp.start()
        for p in range(1, N_DEV):
            peer, peer_index = _peer(x, y, c, p)
            for m in range(n):
                cp = pltpu.make_async_remote_copy(_part(src[m], whole[m], peer_index), land[m].at[peer_index],
                                                  send.at[m * N_DEV + p], recv.at[m * N_DEV + p],
                                                  device_id=peer, device_id_type=MESH)
                cp.wait_send()
                cp.wait_recv()
        for cp in local:
            cp.wait()

    outs = pl.pallas_call(
        body, name=name,
        in_specs=[HBM_SPEC] * (2 * n) + [SEM_SPEC, SEM_SPEC, pl.BlockSpec(memory_space=pl.ANY)],
        out_specs=[HBM_SPEC] * (2 * n),
        out_shape=[pltpu.HBM(a.shape, a.dtype) for a in list(sources) + list(lands)],
        input_output_aliases={i: i for i in range(2 * n)},
        scratch_shapes=[pltpu.SemaphoreType.DMA((n,))],
        compiler_params=pltpu.CompilerParams(has_side_effects=DATAFLOW),
    )(*sources, *lands, send_sems, recv_sems, after)
    return outs[n:]


def _exchange(stacks, shared):
    n = len(stacks) + 1

    def body(*refs):
        ins, outs = refs[:n], refs[n:2 * n]
        send_sems, recv_sems, local_sems = refs[2 * n:]
        x, y, c, me = _mesh_position()

        def source(m, index):
            return ins[m] if m == n - 1 else ins[m].at[index]

        local = [pltpu.make_async_copy(source(m, me), outs[m].at[me], local_sems.at[m]) for m in range(n)]
        for cp in local:
            cp.start()
        sends = []
        for p in range(1, N_DEV):
            peer, peer_index = _peer(x, y, c, p)
            for m in range(n):
                cp = pltpu.make_async_remote_copy(source(m, peer_index), outs[m].at[me],
                                                  send_sems.at[m, p], recv_sems.at[m, p],
                                                  device_id=peer, device_id_type=MESH)
                cp.start()
                sends.append(cp)
        for p in range(1, N_DEV):
            peer, peer_index = _peer(x, y, c, p)
            for m in range(n):
                pltpu.make_async_remote_copy(source(m, me), outs[m].at[peer_index],
                                             send_sems.at[m, p], recv_sems.at[m, p],
                                             device_id=peer, device_id_type=MESH).wait_recv()
        for cp in sends:
            cp.wait_send()
        for cp in local:
            cp.wait()

    hbm = pl.BlockSpec(memory_space=pl.ANY)
    return pl.pallas_call(
        body, name="exchange_grads",
        in_specs=[hbm] * n, out_specs=[hbm] * n,
        out_shape=[jax.ShapeDtypeStruct(a.shape, a.dtype) for a in stacks]
        + [jax.ShapeDtypeStruct((N_DEV,) + shared.shape, shared.dtype)],
        scratch_shapes=[pltpu.SemaphoreType.DMA((n, N_DEV)), pltpu.SemaphoreType.DMA((n, N_DEV)),
                        pltpu.SemaphoreType.DMA((n,))],
    )(*stacks, shared)


def _reduce_adam(parts, w, m, v, name):
    rows, cols = w.shape
    rb = rows
    for cand in (256, 176, 128):
        if rows % cand == 0 and rows > cand:
            rb = cand
            break

    def body(p_ref, w_ref, m_ref, v_ref, g_out, d_out, m_out, v_out):
        g = p_ref[0].astype(F32)
        for j in range(1, N_DEV):
            g = g + p_ref[j].astype(F32)
        g_out[...] = g
        d_out[...], m_out[...], v_out[...] = _adam(g, w_ref[...], m_ref[...], v_ref[...])

    blk = pl.BlockSpec((rb, cols), lambda i: (i, 0))
    out = jax.ShapeDtypeStruct((rows, cols), F32)
    return pl.pallas_call(
        body, name=name, grid=(rows // rb,),
        in_specs=[pl.BlockSpec((N_DEV, rb, cols), lambda i: (0, i, 0)), blk, blk, blk],
        out_specs=[blk] * 4, out_shape=[out] * 4,
        compiler_params=_params(40, 1),
    )(parts, w, m, v)


def _cols_from_stack(stack):
    n, r, c = stack.shape
    return jnp.transpose(stack, (1, 0, 2)).reshape(r, n * c)


def _stack_from_cols(full):
    r, c8 = full.shape
    return jnp.transpose(full.reshape(r, N_DEV, c8 // N_DEV), (1, 0, 2))


def _block_diag(w):
    z = jnp.zeros((N_SLAB, 64, 64), w.dtype)
    pairs = w.reshape(N_SLAB, 2, 64, 64)
    top = jnp.concatenate([pairs[:, 0], z], axis=2)
    bottom = jnp.concatenate([z, pairs[:, 1]], axis=2)
    return jnp.concatenate([top, bottom], axis=1)


def _diag_blocks(w):
    return jnp.stack([w[:, :64, :64], w[:, 64:, 64:]], axis=1).reshape(8, 64, 64)


def _adam(g, w, m, v):
    m_new = ADAM_B1 * m + (1.0 - ADAM_B1) * g
    v_new = ADAM_B2 * v + (1.0 - ADAM_B2) * (g * g)
    m_hat = m_new / (1.0 - ADAM_B1 ** ADAM_STEP)
    v_hat = v_new / (1.0 - ADAM_B2 ** ADAM_STEP)
    return (-ADAM_LR) * (m_hat / (jnp.sqrt(v_hat) + ADAM_EPS) + ADAM_WD * w), m_new, v_new


WIDE = ("ln1_g", "ln2_g", "lnf_g")
HALF = ("pool_scale", "conv_b", "b_a", "b_i", "lam", "gn_pool_g", "gn_lru_g")
VECTORS = [(k, D_MODEL) for k in WIDE] + [(k, 512) for k in HALF]
VECTOR_ROWS = sum(width // SLAB for _, width in VECTORS)
MATRIX_AT = -(-VECTOR_ROWS // 8) * 8
MATRIX_ROWS = N_SLAB * SLAB
N_MATRIX = 3
LOSS_ROW = MATRIX_AT + N_MATRIX * MATRIX_ROWS
PACK_ROWS = LOSS_ROW + 8


def _pack_small(vectors, matrices, sq):
    n_vec = len(vectors)

    def body(*refs):
        vec, mat, sq_ref, out = refs[:n_vec], refs[n_vec:n_vec + N_MATRIX], refs[-2], refs[-1]
        out[...] = jnp.zeros_like(out)
        row = 0
        for ref, (_, width) in zip(vec, VECTORS):
            for k in range(width // SLAB):
                out[row:row + 1, :] = ref[:, k * SLAB:(k + 1) * SLAB]
                row += 1
        for i, ref in enumerate(mat):
            for s in range(N_SLAB):
                at = MATRIX_AT + i * MATRIX_ROWS + s * SLAB
                out[at:at + SLAB, :] = ref[s]
        total = sq_ref[:, 0:SLAB]
        for k in range(1, D_MODEL // SLAB):
            total = total + sq_ref[:, k * SLAB:(k + 1) * SLAB]
        out[LOSS_ROW:LOSS_ROW + 1, :] = total

    return pl.pallas_call(
        body, name="pack_small", out_shape=jax.ShapeDtypeStruct((PACK_ROWS, SLAB), F32),
    )(*vectors, *matrices, sq)


def _small_reduce_adam(parts, vec_w, vec_m, vec_v, pool_wmv):
    n_vec = len(VECTORS)

    def body(*refs):
        p_ref = refs[0]
        w_refs, m_refs, v_refs = (refs[1 + k * n_vec:1 + (k + 1) * n_vec] for k in range(3))
        pw_w, pw_m, pw_v = refs[1 + 3 * n_vec:4 + 3 * n_vec]
        outs = refs[4 + 3 * n_vec:-1]
        total = refs[-1]
        total[...] = p_ref[0]
        for j in range(1, N_DEV):
            total[...] += p_ref[j]
        row = 0
        for i, (_, width) in enumerate(VECTORS):
            n_rows = width // SLAB
            g = jnp.concatenate([total[row + k:row + k + 1, :] for k in range(n_rows)], axis=1)
            row += n_rows
            d, m_new, v_new = _adam(g, w_refs[i][...], m_refs[i][...], v_refs[i][...])
            for ref, val in zip(outs[4 * i:4 * i + 4], (g, d, m_new, v_new)):
                ref[...] = val
        tail = outs[4 * n_vec:]
        g = total[MATRIX_AT:MATRIX_AT + MATRIX_ROWS, :]
        d, m_new, v_new = _adam(g, pw_w[...], pw_m[...], pw_v[...])
        for ref, val in zip(tail[0:4], (g, d, m_new, v_new)):
            ref[...] = val
        tail[4][...] = total[MATRIX_AT + MATRIX_ROWS:MATRIX_AT + 2 * MATRIX_ROWS, :]
        tail[5][...] = total[MATRIX_AT + 2 * MATRIX_ROWS:MATRIX_AT + 3 * MATRIX_ROWS, :]
        tail[6][...] = (0.5 / D_MODEL) * jnp.sum(total[LOSS_ROW:LOSS_ROW + 1, :], axis=1, keepdims=True)

    out_shape = []
    for _, width in VECTORS:
        out_shape += [jax.ShapeDtypeStruct((1, width), F32)] * 4
    out_shape += [jax.ShapeDtypeStruct((MATRIX_ROWS, SLAB), F32)] * 6 + [jax.ShapeDtypeStruct((1, 1), F32)]
    outs = pl.pallas_call(
        body, name="adam_small", out_shape=out_shape,
        scratch_shapes=[pltpu.VMEM((PACK_ROWS, SLAB), F32)],
        compiler_params=_params(40),
    )(parts, *vec_w, *vec_m, *vec_v, *pool_wmv)
    vec_out = [tuple(outs[4 * i:4 * i + 4]) for i in range(n_vec)]
    tail = outs[4 * n_vec:]
    return vec_out, tuple(tail[0:4]), tail[4], tail[5], tail[6]


def _gate_adam(grads, ws, ms, vs):
    n = len(grads)

    def body(*refs):
        ins, outs = refs[:4 * n], refs[4 * n:]
        for i in range(n):
            d, m_new, v_new = _adam(ins[i][...], ins[n + i][...], ins[2 * n + i][...], ins[3 * n + i][...])
            for ref, val in zip(outs[3 * i:3 * i + 3], (d, m_new, v_new)):
                ref[...] = val

    outs = pl.pallas_call(
        body, name="adam_gates", out_shape=[jax.ShapeDtypeStruct(grads[0].shape, F32)] * (3 * n),
    )(*grads, *ws, *ms, *vs)
    return [tuple(outs[3 * i:3 * i + 3]) for i in range(n)]


def kernel(x, ln1_g, w_in, pool_w, pool_scale, conv_w, conv_b, w_a, b_a, w_i, b_i, lam, gn_pool_g, gn_lru_g, w_out, ln2_g, w_ffn_gate, w_ffn_up, w_ffn_down, lnf_g, loss_target, m_ln1_g, m_w_in, m_pool_w, m_pool_scale, m_conv_w, m_conv_b, m_w_a, m_b_a, m_w_i, m_b_i, m_lam, m_gn_pool_g, m_gn_lru_g, m_w_out, m_ln2_g, m_w_ffn_gate, m_w_ffn_up, m_w_ffn_down, m_lnf_g, v_ln1_g, v_w_in, v_pool_w, v_pool_scale, v_conv_w, v_conv_b, v_w_a, v_b_a, v_w_i, v_b_i, v_lam, v_gn_pool_g, v_gn_lru_g, v_w_out, v_ln2_g, v_w_ffn_gate, v_w_ffn_up, v_w_ffn_down, v_lnf_g):
    weights = dict(ln1_g=ln1_g, w_in=w_in, pool_w=pool_w, pool_scale=pool_scale, conv_w=conv_w, conv_b=conv_b,
                   w_a=w_a, b_a=b_a, w_i=w_i, b_i=b_i, lam=lam, gn_pool_g=gn_pool_g, gn_lru_g=gn_lru_g,
                   w_out=w_out, ln2_g=ln2_g, w_ffn_gate=w_ffn_gate, w_ffn_up=w_ffn_up, w_ffn_down=w_ffn_down,
                   lnf_g=lnf_g)
    mom1 = dict(ln1_g=m_ln1_g, w_in=m_w_in, pool_w=m_pool_w, pool_scale=m_pool_scale, conv_w=m_conv_w,
                conv_b=m_conv_b, w_a=m_w_a, b_a=m_b_a, w_i=m_w_i, b_i=m_b_i, lam=m_lam, gn_pool_g=m_gn_pool_g,
                gn_lru_g=m_gn_lru_g, w_out=m_w_out, ln2_g=m_ln2_g, w_ffn_gate=m_w_ffn_gate,
                w_ffn_up=m_w_ffn_up, w_ffn_down=m_w_ffn_down, lnf_g=m_lnf_g)
    mom2 = dict(ln1_g=v_ln1_g, w_in=v_w_in, pool_w=v_pool_w, pool_scale=v_pool_scale, conv_w=v_conv_w,
                conv_b=v_conv_b, w_a=v_w_a, b_a=v_b_a, w_i=v_w_i, b_i=v_b_i, lam=v_lam, gn_pool_g=v_gn_pool_g,
                gn_lru_g=v_gn_lru_g, w_out=v_w_out, ln2_g=v_ln2_g, w_ffn_gate=v_w_ffn_gate,
                w_ffn_up=v_w_ffn_up, w_ffn_down=v_w_ffn_down, lnf_g=v_lnf_g)

    xs = x[0]
    target = loss_target[0]

    shard = dict(w_in=lambda a: a[0].T, w_ffn_gate=lambda a: a[0].T, w_ffn_up=lambda a: a[0].T,
                 w_out=lambda a: a[0], w_ffn_down=lambda a: a[0], conv_w=lambda a: a[0])
    unshard = dict(w_in=lambda a: a.T[None], w_ffn_gate=lambda a: a.T[None], w_ffn_up=lambda a: a.T[None],
                   w_out=lambda a: a[None], w_ffn_down=lambda a: a[None], conv_w=lambda a: a[None])

    gathered = ("w_in", "conv_w", "w_out", "w_ffn_gate", "w_ffn_up", "w_ffn_down")
    sources = [shard[k](weights[k]) if k == "conv_w" else shard[k](weights[k]).astype(BF16) for k in gathered]
    sems, sources, lands, _ = _push_start("gather_start", sources, [True] * 6, [(0, 1), (2,), (3, 4, 5)])
    g_in, g_conv = _push_wait("gather_wait_in", sems[0], sources[0:2], lands[0:2], [True] * 2, xs)
    w_in_f = g_in.reshape(IN_WIDTH, D_MODEL)
    conv_w_f = _cols_from_stack(g_conv)

    wa_bd = _block_diag(w_a[0])
    wi_bd = _block_diag(w_i[0])
    lnf_row = lnf_g.reshape(1, D_MODEL)

    u_pool, u_lru, u_gate = _fwd_in(xs, ln1_g, w_in_f)
    y_pool, h, y_lru = _mixer_fwd(u_pool, u_lru, u_gate, pool_w[0], pool_scale, conv_w_f, conv_b,
                                  wa_bd, b_a, wi_bd, b_i, lam)
    (g_out,) = _push_wait("gather_wait_out", sems[1], sources[2:3], lands[2:3], [True], y_pool)
    w_out_f = g_out.reshape(D_MODEL, D_MODEL)
    h1, n2 = _fwd_out(xs, y_pool, y_lru, gn_pool_g, gn_lru_g, w_out_f, ln2_g)
    g_gate, g_up, g_down = _push_wait("gather_wait_ffn", sems[2], sources[3:6], lands[3:6], [True] * 3, n2)
    w_gate_f = g_gate.reshape(D_FF, D_MODEL)
    w_up_f = g_up.reshape(D_FF, D_MODEL)
    w_down_f = g_down.reshape(D_FF, D_MODEL)
    g_act, u_act, dh2, dh2b, d_lnf, sq = _ffn_fwd(h1, n2, target, lnf_row, w_gate_f, w_up_f, w_down_f)

    d_gate, d_up, d_down, dn2 = _ffn_bwd(n2, dh2b, g_act, u_act, w_gate_f, w_up_f, w_down_f)
    ffn_stacks = [d.reshape(N_DEV, D_FF // N_DEV, D_MODEL) for d in (d_gate, d_up, d_down)]
    ffn_sems, ffn_stacks, ffn_lands, started = _push_start("ffn_grads_start", ffn_stacks, [False] * 3, [(0, 1, 2)])
    dh1, dy_pool, dy_lru, d_out, d_ln2, d_gnp, d_gnl = _bwd_out(dn2, dh2, h1, y_pool, y_lru, gn_pool_g, gn_lru_g,
                                                                 w_out_f, ln2_g, started)
    (du_pool, du_lru, du_gate, d_pw, d_ps, d_cw, d_cb, d_wa, d_ba, d_wi, d_bi, d_lam) = _mixer_bwd(
        u_pool, u_lru, u_gate, h, dy_pool, dy_lru, pool_w[0], pool_scale, conv_w_f, conv_b,
        wa_bd, b_a, wi_bd, b_i, lam)
    grad_x, d_in, d_ln1 = _bwd_in(xs, dh1, du_pool, du_lru, du_gate, ln1_g, w_in_f)

    vec_grads = dict(ln1_g=d_ln1, ln2_g=d_ln2, lnf_g=d_lnf, pool_scale=d_ps, conv_b=d_cb, b_a=d_ba, b_i=d_bi,
                     lam=d_lam, gn_pool_g=d_gnp, gn_lru_g=d_gnl)
    packed = _pack_small([vec_grads[k] for k, _ in VECTORS], [d_pw, d_wa, d_wi], sq)

    stacks = [d_in.reshape(N_DEV, IN_WIDTH // N_DEV, D_MODEL),
              d_out.reshape(N_DEV, D_MODEL // N_DEV, D_MODEL),
              _stack_from_cols(d_cw)]
    r_in, r_out, r_conv, r_small = _exchange(stacks, packed)
    r_gate, r_up, r_down = _push_wait("ffn_grads_wait", ffn_sems[0], ffn_stacks, ffn_lands, [False] * 3, r_small)

    results = {}
    for name, parts in (("w_in", r_in), ("w_out", r_out), ("w_ffn_gate", r_gate), ("w_ffn_up", r_up),
                        ("w_ffn_down", r_down), ("conv_w", r_conv)):
        outs = _reduce_adam(parts, shard[name](weights[name]), shard[name](mom1[name]), shard[name](mom2[name]),
                            "adam_" + name)
        results[name] = tuple(unshard[name](o) for o in outs)

    def as_row(a, width):
        return a.reshape(1, width)

    def as_matrix(a):
        return a.reshape(MATRIX_ROWS, SLAB)

    def as_heads(a):
        return a.reshape(MATRIX_ROWS, 64)

    vec_out, pool_out, g_wa_bd, g_wi_bd, loss_11 = _small_reduce_adam(
        r_small, [as_row(weights[k], w) for k, w in VECTORS], [as_row(mom1[k], w) for k, w in VECTORS],
        [as_row(mom2[k], w) for k, w in VECTORS], [as_matrix(t["pool_w"]) for t in (weights, mom1, mom2)])
    for (k, _), outs in zip(VECTORS, vec_out):
        results[k] = tuple(o.reshape(weights[k].shape) for o in outs)
    results["pool_w"] = tuple(o.reshape(pool_w.shape) for o in pool_out)
    gate_names = ("w_a", "w_i")
    gate_grads = [as_heads(_diag_blocks(g.reshape(N_SLAB, SLAB, SLAB))) for g in (g_wa_bd, g_wi_bd)]
    gate_out = _gate_adam(gate_grads, *[[as_heads(t[k]) for k in gate_names] for t in (weights, mom1, mom2)])
    for k, g, outs in zip(gate_names, gate_grads, gate_out):
        results[k] = tuple(o.reshape(weights[k].shape) for o in (g,) + outs)
    loss = loss_11[0, 0]

    order = ["ln1_g", "w_in", "pool_w", "pool_scale", "conv_w", "conv_b", "w_a", "b_a", "w_i", "b_i", "lam",
             "gn_pool_g", "gn_lru_g", "w_out", "ln2_g", "w_ffn_gate", "w_ffn_up", "w_ffn_down", "lnf_g"]
    return (loss, grad_x[None],
            *[results[k][0] for k in order], *[results[k][1] for k in order],
            *[results[k][2] for k in order], *[results[k][3] for k in order])
```

```python
from typing import Any, NamedTuple

import jax
import jax.numpy as jnp
from jax import lax
from jax.experimental import pallas as pl
from jax.experimental.pallas import tpu as pltpu

F32 = jnp.float32
BF16 = jnp.bfloat16

N_DEV = 8
D_MODEL = 1024
POOL_WIDTH = 512
LRU_WIDTH = 512
IN_WIDTH = 1536
D_FF = 2816
N_SLAB = 4
SLAB = 128
CONV_WIDTH = 4
LRU_C = 8.0
EPS = 1e-6
HALO = 16
FF_CHUNK = 256

ADAM_LR = 0.001
ADAM_B1 = 0.9
ADAM_B2 = 0.999
ADAM_EPS = 1e-08
ADAM_WD = 0.01
ADAM_STEP = 10

MIB = 1 << 20
MESH = pl.DeviceIdType.MESH


def _params(vmem_mib, n_axes=0):
    sem = ("arbitrary",) * n_axes if n_axes else None
    return pltpu.CompilerParams(dimension_semantics=sem, vmem_limit_bytes=vmem_mib * MIB)


def _mm(a, b):
    return jnp.dot(a, b, preferred_element_type=F32)


def _mm_nt(a, b):
    return lax.dot_general(a, b, (((1,), (1,)), ((), ())), preferred_element_type=F32)


def _mm_tn(a, b):
    return lax.dot_general(a, b, (((0,), (0,)), ((), ())), preferred_element_type=F32)


def _rms(x, g):
    rstd = lax.rsqrt(jnp.mean(x * x, axis=-1, keepdims=True) + EPS)
    xhat = x * rstd
    return xhat * g, xhat, rstd


def _rms_bwd(dy, xhat, rstd, g):
    gy = dy * g
    dx = rstd * (gy - xhat * jnp.mean(gy * xhat, axis=-1, keepdims=True))
    return dx, jnp.sum(dy * xhat, axis=0, keepdims=True)


def _gelu(z):
    t = jnp.tanh(0.7978845608028654 * (z + 0.044715 * z * z * z))
    return 0.5 * z * (1.0 + t), t


def _gelu_grad(z, t):
    return 0.5 * (1.0 + t) + 0.5 * z * (1.0 - t * t) * 0.7978845608028654 * (1.0 + 3.0 * 0.044715 * z * z)


def _softplus_neg(lam):
    x = -lam
    e = jnp.exp(-jnp.abs(x))
    u = 1.0 + e
    l1p = jnp.where(u == 1.0, e, jnp.log(u) * e / (u - 1.0))
    return jnp.maximum(x, 0.0) + l1p


def _expm1(x):
    p = x * (1.0 + x * (0.5 + x * (1.0 / 6.0 + x * (1.0 / 24.0 + x * (1.0 / 120.0)))))
    return jnp.where(jnp.abs(x) < 0.1, p, jnp.exp(x) - 1.0)


def _down(v, d):
    return pltpu.roll(v, d, 0)


def _up(v, d):
    return pltpu.roll(v, v.shape[0] - d, 0)


def _token_block(s):
    return 512 if s % 512 == 0 and s > 512 else 256


def _time_chunk(s):
    return 256 if s % 256 == 0 else s


def _fwd_in(x, ln1_g, w_in_t):
    s = x.shape[0]
    tb = _token_block(s)

    def body(x_ref, g_ref, w_ref, up_ref, ul_ref, ug_ref):
        n, _, _ = _rms(x_ref[...], g_ref[...])
        proj = _mm_nt(n.astype(BF16), w_ref[...])
        up_ref[...] = proj[:, :POOL_WIDTH]
        ul_ref[...] = proj[:, POOL_WIDTH:POOL_WIDTH + LRU_WIDTH]
        ug_ref[...] = proj[:, POOL_WIDTH + LRU_WIDTH:]

    out = jax.ShapeDtypeStruct((s, 512), F32)
    return pl.pallas_call(
        body, name="fwd_in", grid=(s // tb,),
        in_specs=[pl.BlockSpec((tb, D_MODEL), lambda i: (i, 0)),
                  pl.BlockSpec((1, D_MODEL), lambda i: (0, 0)),
                  pl.BlockSpec((IN_WIDTH, D_MODEL), lambda i: (0, 0))],
        out_specs=[pl.BlockSpec((tb, 512), lambda i: (i, 0))] * 3,
        out_shape=[out, out, out],
        compiler_params=_params(40, 1),
    )(x, ln1_g, w_in_t)


def _pool_denominator(t0, row, window):
    return jnp.minimum((t0 + row + 1).astype(F32), window)


def _causal_window(ext, onehot):
    s2 = ext + _down(ext, 1)
    s4 = s2 + _down(s2, 2)
    s8 = s4 + _down(s4, 4)
    s16 = s8 + _down(s8, 8)
    return (onehot[0] * s2 + onehot[1] * s4 + onehot[2] * s8 + onehot[3] * s16)[HALO:]


def _anticausal_window(ext, onehot, rows):
    s2 = ext + _up(ext, 1)
    s4 = s2 + _up(s2, 2)
    s8 = s4 + _up(s4, 4)
    s16 = s8 + _up(s8, 8)
    return (onehot[0] * s2 + onehot[1] * s4 + onehot[2] * s8 + onehot[3] * s16)[:rows]


def _conv_taps(ext):
    return [ext[HALO:], _down(ext, 1)[HALO:], _down(ext, 2)[HALO:], _down(ext, 3)[HALO:]]


def _conv(taps, cw, cb):
    return cw[3:4] * taps[0] + cw[2:3] * taps[1] + cw[1:2] * taps[2] + cw[0:1] * taps[3] + cb


def _lru_gates(xc, wa, ba, wi, bi, sp):
    xb = xc.astype(BF16)
    r = jax.nn.sigmoid(_mm(xb, wa) + ba)
    i = jax.nn.sigmoid(_mm(xb, wi) + bi)
    la = (-LRU_C) * r * sp
    a = jnp.exp(la)
    mult = jnp.sqrt(jnp.maximum(-_expm1(2.0 * la), 0.0))
    return xb, r, i, a, mult


def _scan_causal(a, b, h_prev, row, rows):
    d = 1
    while d < rows:
        head = row < d
        a_sh = jnp.where(head, 1.0, _down(a, d))
        b_sh = jnp.where(head, 0.0, _down(b, d))
        b = a * b_sh + b
        a = a * a_sh
        d *= 2
    return b + a * h_prev


def _scan_anticausal(a, b, l_next, row, rows):
    d = 1
    while d < rows:
        tail = row >= rows - d
        a_sh = jnp.where(tail, 1.0, _up(a, d))
        b_sh = jnp.where(tail, 0.0, _up(b, d))
        b = a * b_sh + b
        a = a * a_sh
        d *= 2
    return b + a * l_next


def _slab_scalars():
    slab = pl.program_id(0)
    onehot = [jnp.where(slab == k, 1.0, 0.0).astype(F32) for k in range(N_SLAB)]
    window = jnp.left_shift(jnp.int32(2), slab).astype(F32)
    return onehot, window


def _slab_specs(s):
    seq = pl.BlockSpec((s, SLAB), lambda k: (0, k))
    mat = pl.BlockSpec((1, SLAB, SLAB), lambda k: (k, 0, 0))
    vec = pl.BlockSpec((1, SLAB), lambda k: (0, k))
    taps = pl.BlockSpec((CONV_WIDTH, SLAB), lambda k: (0, k))
    return seq, mat, vec, taps


def _mixer_fwd(u_pool, u_lru, u_gate, pool_w, pool_scale, conv_w, conv_b, wa_bd, b_a, wi_bd, b_i, lam):
    s = u_pool.shape[0]
    tc = _time_chunk(s)
    n_chunks = s // tc

    def body(up_ref, ul_ref, ug_ref, pw_ref, ps_ref, cw_ref, cb_ref, wa_ref, ba_ref, wi_ref, bi_ref, lam_ref,
             yp_ref, h_ref, yl_ref):
        onehot, window = _slab_scalars()
        pw = pw_ref[0].astype(BF16)
        wa = wa_ref[0].astype(BF16)
        wi = wi_ref[0].astype(BF16)
        ps, cw, cb, ba, bi = ps_ref[...], cw_ref[...], cb_ref[...], ba_ref[...], bi_ref[...]
        sp = _softplus_neg(lam_ref[...])
        row = lax.broadcasted_iota(jnp.int32, (tc, SLAB), 0)

        def chunk(t0, ext_p, ext_l, h_prev):
            rows = pl.ds(t0, tc)
            d = _causal_window(ext_p, onehot) / _pool_denominator(t0, row, window) - ext_p[HALO:]
            yp_ref[rows, :] = _mm(d.astype(BF16), pw) * ps
            xc = _conv(_conv_taps(ext_l), cw, cb)
            _, _, i, a, mult = _lru_gates(xc, wa, ba, wi, bi, sp)
            h = _scan_causal(a, mult * (i * xc), h_prev, row, tc)
            h_ref[rows, :] = h
            yl_ref[rows, :] = h * _gelu(ug_ref[rows, :])[0]
            return h[tc - 1:tc, :]

        pad = jnp.zeros((HALO, SLAB), F32)
        h0 = chunk(0, jnp.concatenate([pad, up_ref[pl.ds(0, tc), :]], axis=0),
                   jnp.concatenate([pad, ul_ref[pl.ds(0, tc), :]], axis=0), jnp.zeros((1, SLAB), F32))

        def step(c, h_prev):
            t0 = pl.multiple_of(c * tc, tc)
            ext = pl.ds(pl.multiple_of(c * tc - HALO, HALO), tc + HALO)
            return chunk(t0, up_ref[ext, :], ul_ref[ext, :], h_prev)

        lax.fori_loop(1, n_chunks, step, h0)

    seq, mat, vec, taps = _slab_specs(s)
    out = jax.ShapeDtypeStruct((s, 512), F32)
    return pl.pallas_call(
        body, name="mixer_fwd", grid=(N_SLAB,),
        in_specs=[seq, seq, seq, mat, vec, taps, vec, mat, vec, mat, vec, vec],
        out_specs=[seq, seq, seq], out_shape=[out, out, out],
        compiler_params=_params(48, 1),
    )(u_pool, u_lru, u_gate, pool_w, pool_scale, conv_w, conv_b, wa_bd, b_a, wi_bd, b_i, lam)


def _fwd_out(x, y_pool, y_lru, gn_pool_g, gn_lru_g, w_out, ln2_g):
    s = x.shape[0]
    tb = _token_block(s)

    def body(x_ref, yp_ref, yl_ref, gp_ref, gl_ref, w_ref, g2_ref, h1_ref, n2_ref):
        mp, _, _ = _rms(yp_ref[...], gp_ref[...])
        ml, _, _ = _rms(yl_ref[...], gl_ref[...])
        h1 = x_ref[...] + _mm(mp.astype(BF16), w_ref[:POOL_WIDTH, :]) + _mm(ml.astype(BF16), w_ref[POOL_WIDTH:, :])
        h1_ref[...] = h1
        n2_ref[...] = _rms(h1, g2_ref[...])[0].astype(BF16)

    row = pl.BlockSpec((tb, D_MODEL), lambda i: (i, 0))
    half = pl.BlockSpec((tb, 512), lambda i: (i, 0))
    return pl.pallas_call(
        body, name="fwd_out", grid=(s // tb,),
        in_specs=[row, half, half, pl.BlockSpec((1, 512), lambda i: (0, 0)), pl.BlockSpec((1, 512), lambda i: (0, 0)),
                  pl.BlockSpec((D_MODEL, D_MODEL), lambda i: (0, 0)), pl.BlockSpec((1, D_MODEL), lambda i: (0, 0))],
        out_specs=[row, row],
        out_shape=[jax.ShapeDtypeStruct((s, D_MODEL), F32), jax.ShapeDtypeStruct((s, D_MODEL), BF16)],
        compiler_params=_params(40, 1),
    )(x, y_pool, y_lru, gn_pool_g, gn_lru_g, w_out, ln2_g)


def _ffn_fwd(h1, n2, target, lnf_g, w_gate, w_up, w_down):
    s = h1.shape[0]
    tb = 256
    n_ff = D_FF // FF_CHUNK

    def body(h1_ref, n2_ref, t_ref, gf_ref, wg_hbm, wu_hbm, wd_hbm,
             g_ref, u_ref, dh_ref, dhb_ref, dgf_ref, sq_ref, wg, wu, wd, sem):
        @pl.when(pl.program_id(0) == 0)
        def _():
            loads = [pltpu.make_async_copy(src, dst, sem.at[k])
                     for k, (src, dst) in enumerate(((wg_hbm, wg), (wu_hbm, wu), (wd_hbm, wd)))]
            for cp in loads:
                cp.start()
            for cp in loads:
                cp.wait()
            dgf_ref[...] = jnp.zeros_like(dgf_ref)
            sq_ref[...] = jnp.zeros_like(sq_ref)

        n2v = n2_ref[...]
        acc = jnp.zeros((tb, D_MODEL), F32)
        for c in range(n_ff):
            cols = slice(c * FF_CHUNK, (c + 1) * FF_CHUNK)
            g = _mm_nt(n2v, wg[cols, :])
            u = _mm_nt(n2v, wu[cols, :])
            g_ref[:, cols] = g.astype(BF16)
            u_ref[:, cols] = u.astype(BF16)
            act = g * jax.nn.sigmoid(g) * u
            acc = acc + _mm(act.astype(BF16), wd[cols, :])
        gf = gf_ref[...]
        y, xhat, rstd = _rms(h1_ref[...] + acc, gf)
        err = y - t_ref[...]
        sq_ref[...] += jnp.sum(err * err, axis=0, keepdims=True)
        dh2, dgf = _rms_bwd(err * (1.0 / D_MODEL), xhat, rstd, gf)
        dgf_ref[...] += dgf
        dh_ref[...] = dh2
        dhb_ref[...] = dh2.astype(BF16)

    row = pl.BlockSpec((tb, D_MODEL), lambda i: (i, 0))
    ff = pl.BlockSpec((tb, D_FF), lambda i: (i, 0))
    vec = pl.BlockSpec((1, D_MODEL), lambda i: (0, 0))
    anyspace = pl.BlockSpec(memory_space=pl.ANY)
    return pl.pallas_call(
        body, name="ffn_fwd", grid=(s // tb,),
        in_specs=[row, row, row, vec, anyspace, anyspace, anyspace],
        out_specs=[ff, ff, row, row, vec, vec],
        out_shape=[jax.ShapeDtypeStruct((s, D_FF), BF16), jax.ShapeDtypeStruct((s, D_FF), BF16),
                   jax.ShapeDtypeStruct((s, D_MODEL), F32), jax.ShapeDtypeStruct((s, D_MODEL), BF16),
                   jax.ShapeDtypeStruct((1, D_MODEL), F32), jax.ShapeDtypeStruct((1, D_MODEL), F32)],
        scratch_shapes=[pltpu.VMEM((D_FF, D_MODEL), BF16), pltpu.VMEM((D_FF, D_MODEL), BF16),
                        pltpu.VMEM((D_FF, D_MODEL), BF16), pltpu.SemaphoreType.DMA((3,))],
        compiler_params=_params(56, 1),
    )(h1, n2, target, lnf_g, w_gate, w_up, w_down)


def _ffn_bwd(n2, dh2b, g, u, w_gate_t, w_up_t, w_down):
    s = n2.shape[0]
    tb = min(1024, s)
    n_ff = D_FF // FF_CHUNK
    n_tb = s // tb

    def body(n2_ref, dh_ref, g_ref, u_ref, wg_ref, wu_ref, wd_ref, dwg_ref, dwu_ref, dwd_ref, dn2_ref,
             dn2_acc, acc_g, acc_u, acc_d):
        j = pl.program_id(0)
        t = pl.program_id(1)

        @pl.when(t == 0)
        def _():
            acc_g[...] = jnp.zeros_like(acc_g)
            acc_u[...] = jnp.zeros_like(acc_u)
            acc_d[...] = jnp.zeros_like(acc_d)

        n2v = n2_ref[...]
        dh = dh_ref[...]
        gv = g_ref[...].astype(F32)
        uv = u_ref[...].astype(F32)
        sg = jax.nn.sigmoid(gv)
        silu = gv * sg
        dact = _mm_nt(dh, wd_ref[...])
        dub = (dact * silu).astype(BF16)
        dgb = (dact * uv * (sg * (1.0 + gv * (1.0 - sg)))).astype(BF16)
        acc_d[...] += _mm_tn((silu * uv).astype(BF16), dh)
        acc_g[...] += _mm_tn(dgb, n2v)
        acc_u[...] += _mm_tn(dub, n2v)
        part = _mm(dgb, wg_ref[...]) + _mm(dub, wu_ref[...])
        rows = pl.ds(pl.multiple_of(t * tb, tb), tb)

        @pl.when(t == n_tb - 1)
        def _():
            dwg_ref[...] = acc_g[...].astype(BF16)
            dwu_ref[...] = acc_u[...].astype(BF16)
            dwd_ref[...] = acc_d[...].astype(BF16)

        @pl.when(j == 0)
        def _():
            dn2_acc[rows, :] = part

        @pl.when(jnp.logical_and(j > 0, j < n_ff - 1))
        def _():
            dn2_acc[rows, :] += part

        @pl.when(j == n_ff - 1)
        def _():
            dn2_ref[...] = dn2_acc[rows, :] + part

    row = pl.BlockSpec((tb, D_MODEL), lambda j, t: (t, 0))
    act = pl.BlockSpec((tb, FF_CHUNK), lambda j, t: (t, j))
    w_row = pl.BlockSpec((FF_CHUNK, D_MODEL), lambda j, t: (j, 0))
    last = pl.BlockSpec((tb, D_MODEL), lambda j, t: (jnp.where(j == n_ff - 1, t, 0), 0))
    grad = jax.ShapeDtypeStruct((D_FF, D_MODEL), BF16)
    chunk_acc = pltpu.VMEM((FF_CHUNK, D_MODEL), F32)
    return pl.pallas_call(
        body, name="ffn_bwd", grid=(n_ff, n_tb),
        in_specs=[row, row, act, act, w_row, w_row, w_row],
        out_specs=[w_row, w_row, w_row, last],
        out_shape=[grad, grad, grad, jax.ShapeDtypeStruct((s, D_MODEL), F32)],
        scratch_shapes=[pltpu.VMEM((s, D_MODEL), F32), chunk_acc, chunk_acc, chunk_acc],
        compiler_params=_params(56, 2),
    )(n2, dh2b, g, u, w_gate_t, w_up_t, w_down)


def _bwd_out(dn2, dh2, h1, y_pool, y_lru, gn_pool_g, gn_lru_g, w_out, ln2_g, after):
    s = h1.shape[0]
    tb = _token_block(s)

    def body(dn2_ref, dh2_ref, h1_ref, yp_ref, yl_ref, gp_ref, gl_ref, w_ref, g2_ref, _after,
             dh1_ref, dyp_ref, dyl_ref, dwb_ref, dg2_ref, dgp_ref, dgl_ref, dw_ref):
        @pl.when(pl.program_id(0) == 0)
        def _():
            dw_ref[...] = jnp.zeros_like(dw_ref)
            dg2_ref[...] = jnp.zeros_like(dg2_ref)
            dgp_ref[...] = jnp.zeros_like(dgp_ref)
            dgl_ref[...] = jnp.zeros_like(dgl_ref)

        g2 = g2_ref[...]
        _, xhat2, rstd2 = _rms(h1_ref[...], g2)
        dres, dg2 = _rms_bwd(dn2_ref[...], xhat2, rstd2, g2)
        dg2_ref[...] += dg2
        dh1 = dh2_ref[...] + dres
        dh1_ref[...] = dh1
        dh1b = dh1.astype(BF16)
        gp, gl = gp_ref[...], gl_ref[...]
        mp, xhat_p, rstd_p = _rms(yp_ref[...], gp)
        ml, xhat_l, rstd_l = _rms(yl_ref[...], gl)
        dw_ref[:POOL_WIDTH, :] += _mm_tn(mp.astype(BF16), dh1b)
        dw_ref[POOL_WIDTH:, :] += _mm_tn(ml.astype(BF16), dh1b)
        dyp, dgp = _rms_bwd(_mm_nt(dh1b, w_ref[:POOL_WIDTH, :]), xhat_p, rstd_p, gp)
        dyl, dgl = _rms_bwd(_mm_nt(dh1b, w_ref[POOL_WIDTH:, :]), xhat_l, rstd_l, gl)
        dyp_ref[...] = dyp
        dyl_ref[...] = dyl
        dgp_ref[...] += dgp
        dgl_ref[...] += dgl

        @pl.when(pl.program_id(0) == s // tb - 1)
        def _():
            dwb_ref[...] = dw_ref[...].astype(BF16)

    row = pl.BlockSpec((tb, D_MODEL), lambda i: (i, 0))
    half = pl.BlockSpec((tb, 512), lambda i: (i, 0))
    vec = pl.BlockSpec((1, D_MODEL), lambda i: (0, 0))
    hvec = pl.BlockSpec((1, 512), lambda i: (0, 0))
    mat = pl.BlockSpec((D_MODEL, D_MODEL), lambda i: (0, 0))
    return pl.pallas_call(
        body, name="bwd_out", grid=(s // tb,),
        in_specs=[row, row, row, half, half, hvec, hvec, mat, vec, pl.BlockSpec(memory_space=pl.ANY)],
        out_specs=[row, half, half, mat, vec, hvec, hvec],
        out_shape=[jax.ShapeDtypeStruct((s, D_MODEL), F32), jax.ShapeDtypeStruct((s, 512), F32),
                   jax.ShapeDtypeStruct((s, 512), F32), jax.ShapeDtypeStruct((D_MODEL, D_MODEL), BF16),
                   jax.ShapeDtypeStruct((1, D_MODEL), F32), jax.ShapeDtypeStruct((1, 512), F32),
                   jax.ShapeDtypeStruct((1, 512), F32)],
        scratch_shapes=[pltpu.VMEM((D_MODEL, D_MODEL), F32)],
        compiler_params=_params(48, 1),
    )(dn2, dh2, h1, y_pool, y_lru, gn_pool_g, gn_lru_g, w_out, ln2_g, after)


def _mixer_bwd(u_pool, u_lru, u_gate, h, dy_pool, dy_lru,
               pool_w, pool_scale, conv_w, conv_b, wa_bd, b_a, wi_bd, b_i, lam, after):
    s = u_pool.shape[0]
    tc = _time_chunk(s)
    n_chunks = s // tc

    def body(up_ref, ul_ref, ug_ref, h_ref, dyp_ref, dyl_ref,
             pw_ref, ps_ref, cw_ref, cb_ref, wa_ref, ba_ref, wi_ref, bi_ref, lam_ref, _after,
             dup_ref, dul_ref, dug_ref, dpw_ref, dps_ref, dcw_ref, dcb_ref, dwa_ref, dba_ref, dwi_ref, dbi_ref, dlam_ref):
        onehot, window = _slab_scalars()
        pw = pw_ref[0].astype(BF16)
        wa = wa_ref[0].astype(BF16)
        wi = wi_ref[0].astype(BF16)
        ps, cw, cb, ba, bi = ps_ref[...], cw_ref[...], cb_ref[...], ba_ref[...], bi_ref[...]
        lam_v = lam_ref[...]
        sp = _softplus_neg(lam_v)
        row = lax.broadcasted_iota(jnp.int32, (tc, SLAB), 0)
        for ref in (dpw_ref, dps_ref, dcw_ref, dcb_ref, dwa_ref, dba_ref, dwi_ref, dbi_ref, dlam_ref):
            ref[...] = jnp.zeros_like(ref)

        def chunk(t0, ext_p, ext_l, ext_h, carry):
            l_next, a_next, dxc_next, ddn_next = carry
            rows = pl.ds(t0, tc)
            taps = _conv_taps(ext_l)
            xc = _conv(taps, cw, cb)
            xb, r, i, a, mult = _lru_gates(xc, wa, ba, wi, bi, sp)
            hv = ext_h[HALO:]
            h_before = _down(ext_h, 1)[HALO:]
            ug = ug_ref[rows, :]
            dyl = dyl_ref[rows, :]
            gel, th = _gelu(ug)
            dug_ref[rows, :] = dyl * hv * _gelu_grad(ug, th)
            a_after = jnp.where(row == tc - 1, a_next, _up(a, 1))
            l = _scan_anticausal(a_after, dyl * gel, l_next, row, tc)
            dmult = l * (i * xc)
            di = l * mult * xc
            dxc = l * mult * i
            dla = (l * h_before) * a - jnp.where(mult > 0.0, dmult * (a * a) / mult, 0.0)
            dlam_ref[...] += jnp.sum(dla * r, axis=0, keepdims=True)
            dpa = (dla * ((-LRU_C) * sp)) * (r * (1.0 - r))
            dpi = di * (i * (1.0 - i))
            dpab = dpa.astype(BF16)
            dpib = dpi.astype(BF16)
            dwa_ref[0] += _mm_tn(xb, dpab)
            dwi_ref[0] += _mm_tn(xb, dpib)
            dba_ref[...] += jnp.sum(dpa, axis=0, keepdims=True)
            dbi_ref[...] += jnp.sum(dpi, axis=0, keepdims=True)
            dxc = dxc + _mm_nt(dpab, wa) + _mm_nt(dpib, wi)
            ext_d = jnp.concatenate([dxc, dxc_next], axis=0)
            dul_ref[rows, :] = (cw[3:4] * dxc + cw[2:3] * _up(ext_d, 1)[:tc]
                                + cw[1:2] * _up(ext_d, 2)[:tc] + cw[0:1] * _up(ext_d, 3)[:tc])
            for k in range(CONV_WIDTH):
                dcw_ref[k:k + 1, :] += jnp.sum(dxc * taps[CONV_WIDTH - 1 - k], axis=0, keepdims=True)
            dcb_ref[...] += jnp.sum(dxc, axis=0, keepdims=True)
            denom = _pool_denominator(t0, row, window)
            db = (_causal_window(ext_p, onehot) / denom - ext_p[HALO:]).astype(BF16)
            dyp = dyp_ref[rows, :]
            dps_ref[...] += jnp.sum(dyp * _mm(db, pw), axis=0, keepdims=True)
            dys = (dyp * ps).astype(BF16)
            dpw_ref[0] += _mm_tn(db, dys)
            dd = _mm_nt(dys, pw)
            ddn = dd / denom
            ext_q = jnp.concatenate([ddn, ddn_next], axis=0)
            dup_ref[rows, :] = _anticausal_window(ext_q, onehot, tc) - dd
            return l[0:1, :], a[0:1, :], dxc[0:8, :], ddn[0:HALO, :]

        def step(k, carry):
            c = n_chunks - 1 - k
            t0 = pl.multiple_of(c * tc, tc)
            ext = pl.ds(pl.multiple_of(c * tc - HALO, HALO), tc + HALO)
            return chunk(t0, up_ref[ext, :], ul_ref[ext, :], h_ref[ext, :], carry)

        carry = (jnp.zeros((1, SLAB), F32), jnp.zeros((1, SLAB), F32),
                 jnp.zeros((8, SLAB), F32), jnp.zeros((HALO, SLAB), F32))
        carry = lax.fori_loop(0, n_chunks - 1, step, carry)
        pad = jnp.zeros((HALO, SLAB), F32)
        first = pl.ds(0, tc)
        chunk(0, jnp.concatenate([pad, up_ref[first, :]], axis=0), jnp.concatenate([pad, ul_ref[first, :]], axis=0),
              jnp.concatenate([pad, h_ref[first, :]], axis=0), carry)
        dlam_ref[...] = dlam_ref[...] * (LRU_C * jax.nn.sigmoid(-lam_v))

    seq, mat, vec, taps = _slab_specs(s)
    full = jax.ShapeDtypeStruct((s, 512), F32)
    mats = jax.ShapeDtypeStruct((N_SLAB, SLAB, SLAB), F32)
    vecs = jax.ShapeDtypeStruct((1, 512), F32)
    return pl.pallas_call(
        body, name="mixer_bwd", grid=(N_SLAB,),
        in_specs=[seq] * 6 + [mat, vec, taps, vec, mat, vec, mat, vec, vec, pl.BlockSpec(memory_space=pl.ANY)],
        out_specs=[seq, seq, seq, mat, vec, taps, vec, mat, vec, mat, vec, vec],
        out_shape=[full, full, full, mats, vecs, jax.ShapeDtypeStruct((CONV_WIDTH, 512), F32), vecs,
                   mats, vecs, mats, vecs, vecs],
        compiler_params=_params(56, 1),
    )(u_pool, u_lru, u_gate, h, dy_pool, dy_lru, pool_w, pool_scale, conv_w, conv_b, wa_bd, b_a, wi_bd, b_i, lam,
      after)


def _bwd_in(x, dh1, du_pool, du_lru, du_gate, ln1_g, w_in_t):
    s = x.shape[0]
    tb = _token_block(s)

    def body(x_ref, dh1_ref, dup_ref, dul_ref, dug_ref, g_ref, w_ref, dx_ref, dwb_ref, dg_ref, dw_ref):
        @pl.when(pl.program_id(0) == 0)
        def _():
            dw_ref[...] = jnp.zeros_like(dw_ref)
            dg_ref[...] = jnp.zeros_like(dg_ref)

        g1 = g_ref[...]
        n, xhat, rstd = _rms(x_ref[...], g1)
        nb = n.astype(BF16)
        dn = jnp.zeros((tb, D_MODEL), F32)
        for k, ref in enumerate((dup_ref, dul_ref, dug_ref)):
            rows = slice(k * 512, (k + 1) * 512)
            db = ref[...].astype(BF16)
            dw_ref[rows, :] += _mm_tn(db, nb)
            dn = dn + _mm(db, w_ref[rows, :])
        dx, dg1 = _rms_bwd(dn, xhat, rstd, g1)
        dx_ref[...] = dh1_ref[...] + dx
        dg_ref[...] += dg1

        @pl.when(pl.program_id(0) == s // tb - 1)
        def _():
            dwb_ref[...] = dw_ref[...].astype(BF16)

    row = pl.BlockSpec((tb, D_MODEL), lambda i: (i, 0))
    half = pl.BlockSpec((tb, 512), lambda i: (i, 0))
    vec = pl.BlockSpec((1, D_MODEL), lambda i: (0, 0))
    mat = pl.BlockSpec((IN_WIDTH, D_MODEL), lambda i: (0, 0))
    return pl.pallas_call(
        body, name="bwd_in", grid=(s // tb,),
        in_specs=[row, row, half, half, half, vec, mat],
        out_specs=[row, mat, vec],
        out_shape=[jax.ShapeDtypeStruct((s, D_MODEL), F32), jax.ShapeDtypeStruct((IN_WIDTH, D_MODEL), BF16),
                   jax.ShapeDtypeStruct((1, D_MODEL), F32)],
        scratch_shapes=[pltpu.VMEM((IN_WIDTH, D_MODEL), F32)],
        compiler_params=_params(48, 1),
    )(x, dh1, du_pool, du_lru, du_gate, ln1_g, w_in_t)


def _mesh_position():
    x, y, c = lax.axis_index("x"), lax.axis_index("y"), lax.axis_index("c")
    return x, y, c, 4 * x + 2 * y + c


def _peer(x, y, c, p):
    px = 1 - x if p & 4 else x
    py = 1 - y if p & 2 else y
    pc = 1 - c if p & 1 else c
    return (px, py, pc), 4 * px + 2 * py + pc


HBM_SPEC = pl.BlockSpec(memory_space=pltpu.HBM)
SEM_SPEC = pl.BlockSpec(memory_space=pltpu.SEMAPHORE)
DATAFLOW = pltpu.SideEffectType.DATAFLOW_SIDE_EFFECTING


class Copy(NamedTuple):
    src: int
    src_at: Any
    dst: int
    dst_at: Any
    peer: int
    group: int
    slot: int


SIBLING = (1,)
SAME_CORE = (2, 4, 6)
EVERYONE = tuple(range(1, N_DEV))


def _same(index):
    return index


def _chip(index):
    return jnp.right_shift(index, 1)


def _fan_out(srcs, lands, peers, group):
    return [Copy(s, None, d, _same, p, group, N_DEV * i + p) for i, (s, d) in enumerate(zip(srcs, lands)) for p in peers]


def _relay(lands, peers):
    return [Copy(b, lambda s, q=q: jnp.bitwise_xor(s, q), b, lambda s, q=q: jnp.bitwise_xor(s, q), 1, 0, N_DEV * i + q)
            for i, b in enumerate(lands) for q in peers]


def _to_sibling(stacks, lands):
    return [Copy(s, lambda me, k=k: 2 * k + 1 - jnp.bitwise_and(me, 1), d, lambda me, k=k: k, 1, 0, 4 * i + k)
            for i, (s, d) in enumerate(zip(stacks, lands)) for k in range(N_DEV // 2)]


def _to_chips(sums, lands, first_slot=0):
    return [Copy(s, lambda me, p=p: jnp.bitwise_xor(_chip(me), p // 2), d, _chip, p, 0, first_slot + 4 * i + p // 2)
            for i, (s, d) in enumerate(zip(sums, lands)) for p in SAME_CORE]


def _comm_call(name, bufs, wait=None, local=(), start=None, after=None):
    nb = len(bufs)
    slots = list(start[1]) if start else []
    n_out_sem = 2 * len(slots)

    def body(*refs):
        b = refs[:nb]
        at = nb
        if wait:
            w_send, w_recv = refs[at], refs[at + 1]
            at += 2
        if after is not None:
            at += 1
        out_sems = refs[at:at + n_out_sem]
        token = refs[at + n_out_sem + nb]
        x, y, c, me = _mesh_position()

        def part(i, row_of, sender):
            return b[i] if row_of is None else b[i].at[row_of(sender)]

        if wait:
            for cp in wait[1]:
                peer, peer_index = _peer(x, y, c, cp.peer)
                arrival = pltpu.make_async_remote_copy(part(cp.src, cp.src_at, me), part(cp.dst, cp.dst_at, peer_index),
                                                       w_send.at[cp.slot], w_recv.at[cp.slot],
                                                       device_id=peer, device_id_type=MESH)
                arrival.wait_send()
                arrival.wait_recv()
        if local:
            local_sems = refs[-1]
            copies = [pltpu.make_async_copy(part(sb, sf, me), part(db, df, me), local_sems.at[k])
                      for k, (sb, sf, db, df) in enumerate(local)]
            for cp in copies:
                cp.start()
            for cp in copies:
                cp.wait()
        if start:
            for cp in start[0]:
                peer, _ = _peer(x, y, c, cp.peer)
                pltpu.make_async_remote_copy(part(cp.src, cp.src_at, me), part(cp.dst, cp.dst_at, me),
                                             out_sems[2 * cp.group].at[cp.slot], out_sems[2 * cp.group + 1].at[cp.slot],
                                             device_id=peer, device_id_type=MESH).start()
        token[...] = jnp.zeros_like(token)

    sem_shapes = []
    for n_slots in slots:
        sem_shapes += [pltpu.SemaphoreType.DMA((n_slots,))] * 2
    operands = [pltpu.with_memory_space_constraint(a, pltpu.HBM) for a in bufs]
    in_specs = [HBM_SPEC] * nb
    if wait:
        operands += list(wait[0])
        in_specs += [SEM_SPEC, SEM_SPEC]
    if after is not None:
        operands.append(after)
        in_specs.append(pl.BlockSpec(memory_space=pl.ANY))
    outs = pl.pallas_call(
        body, name=name, in_specs=in_specs,
        out_specs=[SEM_SPEC] * n_out_sem + [HBM_SPEC] * nb + [pl.BlockSpec(memory_space=pltpu.VMEM)],
        out_shape=sem_shapes + [pltpu.HBM(a.shape, a.dtype) for a in bufs] + [jax.ShapeDtypeStruct((8, SLAB), F32)],
        input_output_aliases={i: n_out_sem + i for i in range(nb)},
        scratch_shapes=[pltpu.SemaphoreType.DMA((len(local),))] if local else [],
        compiler_params=pltpu.CompilerParams(has_side_effects=DATAFLOW),
    )(*operands)
    sems = [(outs[2 * k], outs[2 * k + 1]) for k in range(len(slots))]
    return sems, list(outs[n_out_sem:n_out_sem + nb]), outs[-1]


def _pair_sum(stacks, lands, core):
    n = len(stacks)

    def body(core_ref, *refs):
        del core_ref
        for m in range(n):
            mine, theirs, out = refs[m], refs[n + m], refs[2 * n + m]
            out[0] = (mine[0, 0].astype(F32) + theirs[0].astype(F32)).astype(out.dtype)

    in_specs = [pl.BlockSpec((1, 1) + a.shape[1:], lambda k, core_ref: (k, core_ref[0], 0, 0)) for a in stacks]
    in_specs += [pl.BlockSpec((1,) + a.shape[1:], lambda k, core_ref: (k, 0, 0)) for a in lands]
    return pl.pallas_call(
        body, name="pair_sum_" + "_".join(str(a.shape[1]) for a in stacks),
        grid_spec=pltpu.PrefetchScalarGridSpec(
            num_scalar_prefetch=1, grid=(N_DEV // 2,), in_specs=in_specs,
            out_specs=[pl.BlockSpec((1,) + a.shape[1:], lambda k, core_ref: (k, 0, 0)) for a in lands]),
        out_shape=[jax.ShapeDtypeStruct(a.shape, a.dtype) for a in lands],
        compiler_params=_params(40, 1),
    )(core, *[a.reshape((N_DEV // 2, 2) + a.shape[1:]) for a in stacks], *lands)


def _pair_add(a, b):
    def body(a_ref, b_ref, out):
        out[...] = a_ref[...] + b_ref[...]

    return pl.pallas_call(body, name="pair_add", out_shape=jax.ShapeDtypeStruct(a.shape, a.dtype))(a, b)


def _reduce_adam(parts, w, m, v, name):
    rows, cols = w.shape
    n_parts = parts.shape[0]
    rb = rows
    for cand in (256, 176, 128):
        if rows % cand == 0 and rows > cand:
            rb = cand
            break

    def body(p_ref, w_ref, m_ref, v_ref, g_out, d_out, m_out, v_out):
        g = p_ref[0].astype(F32)
        for j in range(1, n_parts):
            g = g + p_ref[j].astype(F32)
        g_out[...] = g
        d_out[...], m_out[...], v_out[...] = _adam(g, w_ref[...], m_ref[...], v_ref[...])

    blk = pl.BlockSpec((rb, cols), lambda i: (i, 0))
    out = jax.ShapeDtypeStruct((rows, cols), F32)
    return pl.pallas_call(
        body, name=name, grid=(rows // rb,),
        in_specs=[pl.BlockSpec((n_parts, rb, cols), lambda i: (0, i, 0)), blk, blk, blk],
        out_specs=[blk] * 4, out_shape=[out] * 4,
        compiler_params=_params(40, 1),
    )(parts, w, m, v)


def _cols_from_stack(stack):
    n, r, c = stack.shape
    return jnp.transpose(stack, (1, 0, 2)).reshape(r, n * c)


def _block_diag(w):
    z = jnp.zeros((N_SLAB, 64, 64), w.dtype)
    pairs = w.reshape(N_SLAB, 2, 64, 64)
    top = jnp.concatenate([pairs[:, 0], z], axis=2)
    bottom = jnp.concatenate([z, pairs[:, 1]], axis=2)
    return jnp.concatenate([top, bottom], axis=1)


def _diag_blocks(w):
    return jnp.stack([w[:, :64, :64], w[:, 64:, 64:]], axis=1).reshape(8, 64, 64)


def _adam(g, w, m, v):
    m_new = ADAM_B1 * m + (1.0 - ADAM_B1) * g
    v_new = ADAM_B2 * v + (1.0 - ADAM_B2) * (g * g)
    m_hat = m_new / (1.0 - ADAM_B1 ** ADAM_STEP)
    v_hat = v_new / (1.0 - ADAM_B2 ** ADAM_STEP)
    return (-ADAM_LR) * (m_hat / (jnp.sqrt(v_hat) + ADAM_EPS) + ADAM_WD * w), m_new, v_new


WIDE = ("ln1_g", "ln2_g", "lnf_g")
HALF = ("pool_scale", "conv_b", "b_a", "b_i", "lam", "gn_pool_g", "gn_lru_g")
VECTORS = [(k, D_MODEL) for k in WIDE] + [(k, 512) for k in HALF]
VECTOR_ROWS = sum(width // SLAB for _, width in VECTORS)
MATRIX_AT = -(-VECTOR_ROWS // 8) * 8
MATRIX_ROWS = N_SLAB * SLAB
N_MATRIX = 3
LOSS_ROW = MATRIX_AT + N_MATRIX * MATRIX_ROWS
CONV_AT = LOSS_ROW + 8
CONV_LANES = LRU_WIDTH // SLAB
PACK_ROWS = CONV_AT + CONV_WIDTH * CONV_LANES


def _pack_small(vectors, matrices, conv, sq):
    n_vec = len(vectors)

    def body(*refs):
        vec, mat, cw_ref, sq_ref, out = refs[:n_vec], refs[n_vec:n_vec + N_MATRIX], refs[-3], refs[-2], refs[-1]
        out[...] = jnp.zeros_like(out)
        row = 0
        for ref, (_, width) in zip(vec, VECTORS):
            for k in range(width // SLAB):
                out[row:row + 1, :] = ref[:, k * SLAB:(k + 1) * SLAB]
                row += 1
        for i, ref in enumerate(mat):
            for s in range(N_SLAB):
                at = MATRIX_AT + i * MATRIX_ROWS + s * SLAB
                out[at:at + SLAB, :] = ref[s]
        for tap in range(CONV_WIDTH):
            for k in range(CONV_LANES):
                at = CONV_AT + tap * CONV_LANES + k
                out[at:at + 1, :] = cw_ref[tap:tap + 1, k * SLAB:(k + 1) * SLAB]
        total = sq_ref[:, 0:SLAB]
        for k in range(1, D_MODEL // SLAB):
            total = total + sq_ref[:, k * SLAB:(k + 1) * SLAB]
        out[LOSS_ROW:LOSS_ROW + 1, :] = total

    return pl.pallas_call(
        body, name="pack_small", out_shape=jax.ShapeDtypeStruct((PACK_ROWS, SLAB), F32),
    )(*vectors, *matrices, conv, sq)


def _small_reduce_adam(parts, vec_w, vec_m, vec_v, pool_wmv):
    n_vec = len(VECTORS)
    n_parts = parts.shape[0]

    def body(*refs):
        p_ref = refs[0]
        w_refs, m_refs, v_refs = (refs[1 + k * n_vec:1 + (k + 1) * n_vec] for k in range(3))
        pw_w, pw_m, pw_v = refs[1 + 3 * n_vec:4 + 3 * n_vec]
        outs = refs[4 + 3 * n_vec:-1]
        total = refs[-1]
        total[...] = p_ref[0]
        for j in range(1, n_parts):
            total[...] += p_ref[j]
        row = 0
        for i, (_, width) in enumerate(VECTORS):
            n_rows = width // SLAB
            g = jnp.concatenate([total[row + k:row + k + 1, :] for k in range(n_rows)], axis=1)
            row += n_rows
            d, m_new, v_new = _adam(g, w_refs[i][...], m_refs[i][...], v_refs[i][...])
            for ref, val in zip(outs[4 * i:4 * i + 4], (g, d, m_new, v_new)):
                ref[...] = val
        tail = outs[4 * n_vec:]
        g = total[MATRIX_AT:MATRIX_AT + MATRIX_ROWS, :]
        d, m_new, v_new = _adam(g, pw_w[...], pw_m[...], pw_v[...])
        for ref, val in zip(tail[0:4], (g, d, m_new, v_new)):
            ref[...] = val
        tail[4][...] = total[MATRIX_AT + MATRIX_ROWS:MATRIX_AT + 2 * MATRIX_ROWS, :]
        tail[5][...] = total[MATRIX_AT + 2 * MATRIX_ROWS:MATRIX_AT + 3 * MATRIX_ROWS, :]
        for tap in range(CONV_WIDTH):
            at = CONV_AT + tap * CONV_LANES
            tail[6][tap:tap + 1, :] = jnp.concatenate([total[at + k:at + k + 1, :] for k in range(CONV_LANES)], axis=1)
        tail[7][...] = (0.5 / D_MODEL) * jnp.sum(total[LOSS_ROW:LOSS_ROW + 1, :], axis=1, keepdims=True)

    out_shape = []
    for _, width in VECTORS:
        out_shape += [jax.ShapeDtypeStruct((1, width), F32)] * 4
    out_shape += [jax.ShapeDtypeStruct((MATRIX_ROWS, SLAB), F32)] * 6
    out_shape += [jax.ShapeDtypeStruct((CONV_WIDTH, LRU_WIDTH), F32), jax.ShapeDtypeStruct((1, 1), F32)]
    outs = pl.pallas_call(
        body, name="adam_small", out_shape=out_shape,
        scratch_shapes=[pltpu.VMEM((PACK_ROWS, SLAB), F32)],
        compiler_params=_params(40),
    )(parts, *vec_w, *vec_m, *vec_v, *pool_wmv)
    vec_out = [tuple(outs[4 * i:4 * i + 4]) for i in range(n_vec)]
    tail = outs[4 * n_vec:]
    return vec_out, tuple(tail[0:4]), tail[4], tail[5], tail[6], tail[7]


def _plain_adam(grads, ws, ms, vs):
    n = len(grads)

    def body(*refs):
        ins, outs = refs[:4 * n], refs[4 * n:]
        for i in range(n):
            d, m_new, v_new = _adam(ins[i][...], ins[n + i][...], ins[2 * n + i][...], ins[3 * n + i][...])
            for ref, val in zip(outs[3 * i:3 * i + 3], (d, m_new, v_new)):
                ref[...] = val

    out_shape = []
    for g in grads:
        out_shape += [jax.ShapeDtypeStruct(g.shape, F32)] * 3
    outs = pl.pallas_call(body, name="adam_plain", out_shape=out_shape)(*grads, *ws, *ms, *vs)
    return [tuple(outs[3 * i:3 * i + 3]) for i in range(n)]


def kernel(x, ln1_g, w_in, pool_w, pool_scale, conv_w, conv_b, w_a, b_a, w_i, b_i, lam, gn_pool_g, gn_lru_g, w_out, ln2_g, w_ffn_gate, w_ffn_up, w_ffn_down, lnf_g, loss_target, m_ln1_g, m_w_in, m_pool_w, m_pool_scale, m_conv_w, m_conv_b, m_w_a, m_b_a, m_w_i, m_b_i, m_lam, m_gn_pool_g, m_gn_lru_g, m_w_out, m_ln2_g, m_w_ffn_gate, m_w_ffn_up, m_w_ffn_down, m_lnf_g, v_ln1_g, v_w_in, v_pool_w, v_pool_scale, v_conv_w, v_conv_b, v_w_a, v_b_a, v_w_i, v_b_i, v_lam, v_gn_pool_g, v_gn_lru_g, v_w_out, v_ln2_g, v_w_ffn_gate, v_w_ffn_up, v_w_ffn_down, v_lnf_g):
    weights = dict(ln1_g=ln1_g, w_in=w_in, pool_w=pool_w, pool_scale=pool_scale, conv_w=conv_w, conv_b=conv_b,
                   w_a=w_a, b_a=b_a, w_i=w_i, b_i=b_i, lam=lam, gn_pool_g=gn_pool_g, gn_lru_g=gn_lru_g,
                   w_out=w_out, ln2_g=ln2_g, w_ffn_gate=w_ffn_gate, w_ffn_up=w_ffn_up, w_ffn_down=w_ffn_down,
                   lnf_g=lnf_g)
    mom1 = dict(ln1_g=m_ln1_g, w_in=m_w_in, pool_w=m_pool_w, pool_scale=m_pool_scale, conv_w=m_conv_w,
                conv_b=m_conv_b, w_a=m_w_a, b_a=m_b_a, w_i=m_w_i, b_i=m_b_i, lam=m_lam, gn_pool_g=m_gn_pool_g,
                gn_lru_g=m_gn_lru_g, w_out=m_w_out, ln2_g=m_ln2_g, w_ffn_gate=m_w_ffn_gate,
                w_ffn_up=m_w_ffn_up, w_ffn_down=m_w_ffn_down, lnf_g=m_lnf_g)
    mom2 = dict(ln1_g=v_ln1_g, w_in=v_w_in, pool_w=v_pool_w, pool_scale=v_pool_scale, conv_w=v_conv_w,
                conv_b=v_conv_b, w_a=v_w_a, b_a=v_b_a, w_i=v_w_i, b_i=v_b_i, lam=v_lam, gn_pool_g=v_gn_pool_g,
                gn_lru_g=v_gn_lru_g, w_out=v_w_out, ln2_g=v_ln2_g, w_ffn_gate=v_w_ffn_gate,
                w_ffn_up=v_w_ffn_up, w_ffn_down=v_w_ffn_down, lnf_g=v_lnf_g)

    xs = x[0]
    target = loss_target[0]

    shard = dict(w_in=lambda a: a[0].T, w_ffn_gate=lambda a: a[0].T, w_ffn_up=lambda a: a[0].T,
                 w_out=lambda a: a[0], w_ffn_down=lambda a: a[0], conv_w=lambda a: a[0])
    unshard = dict(w_in=lambda a: a.T[None], w_ffn_gate=lambda a: a.T[None], w_ffn_up=lambda a: a.T[None],
                   w_out=lambda a: a[None], w_ffn_down=lambda a: a[None], conv_w=lambda a: a[None])

    gathered = ("w_in", "conv_w", "w_out", "w_ffn_gate", "w_ffn_up", "w_ffn_down")
    sources = [shard[k](weights[k]) if k == "conv_w" else shard[k](weights[k]).astype(BF16) for k in gathered]
    lands = [lax.empty((N_DEV,) + a.shape, a.dtype) for a in sources]
    start = (_fan_out((0, 1), (6, 7), EVERYONE, 0) + _fan_out((2,), (8,), EVERYONE, 1)
             + _fan_out((3, 4, 5), (9, 10, 11), SIBLING + SAME_CORE, 2))
    sems, bufs, _ = _comm_call("gather_start", sources + lands, start=(start, (16, 8, 24)))
    sources, lands = bufs[:6], bufs[6:]
    _, bufs, _ = _comm_call("gather_wait_in", sources[0:2] + lands[0:2],
                            wait=(sems[0], _fan_out((0, 1), (2, 3), EVERYONE, 0)),
                            local=[(0, None, 2, _same), (1, None, 3, _same)])
    w_in_f = bufs[2].reshape(IN_WIDTH, D_MODEL)
    conv_w_f = _cols_from_stack(bufs[3])

    wa_bd = _block_diag(w_a[0])
    wi_bd = _block_diag(w_i[0])
    lnf_row = lnf_g.reshape(1, D_MODEL)

    u_pool, u_lru, u_gate = _fwd_in(xs, ln1_g, w_in_f)
    y_pool, h, y_lru = _mixer_fwd(u_pool, u_lru, u_gate, pool_w[0], pool_scale, conv_w_f, conv_b,
                                  wa_bd, b_a, wi_bd, b_i, lam)
    _, bufs, _ = _comm_call("gather_wait_out", [sources[2], lands[2]],
                            wait=(sems[1], _fan_out((0,), (1,), EVERYONE, 0)), local=[(0, None, 1, _same)], after=y_pool)
    w_out_f = bufs[1].reshape(D_MODEL, D_MODEL)
    h1, n2 = _fwd_out(xs, y_pool, y_lru, gn_pool_g, gn_lru_g, w_out_f, ln2_g)
    relay_sems, bufs, _ = _comm_call("gather_relay_ffn", sources[3:6] + lands[3:6],
                                     wait=(sems[2], _fan_out((0, 1, 2), (3, 4, 5), SIBLING + SAME_CORE, 0)),
                                     local=[(i, None, 3 + i, _same) for i in range(3)],
                                     start=(_relay((3, 4, 5), SAME_CORE), (24,)), after=n2)
    _, bufs, _ = _comm_call("gather_wait_ffn", bufs[3:6], wait=(relay_sems[0], _relay((0, 1, 2), SAME_CORE)))
    w_gate_f = bufs[0].reshape(D_FF, D_MODEL)
    w_up_f = bufs[1].reshape(D_FF, D_MODEL)
    w_down_f = bufs[2].reshape(D_FF, D_MODEL)
    g_act, u_act, dh2, dh2b, d_lnf, sq = _ffn_fwd(h1, n2, target, lnf_row, w_gate_f, w_up_f, w_down_f)

    core = lax.axis_index("c").astype(jnp.int32).reshape(1)

    def pair_step(tag, stacks, extra, after):
        n = len(stacks)
        lands = [lax.empty((N_DEV // 2,) + a.shape[1:], a.dtype) for a in stacks]
        copies = _to_sibling(range(n), range(n, 2 * n))
        if extra is not None:
            copies.append(Copy(2 * n, None, 2 * n + 1, None, 1, 0, 4 * n))
            lands_extra = [extra, lax.empty(extra.shape, extra.dtype)]
        else:
            lands_extra = []
        sem, bufs, token = _comm_call(tag + "_pair_start", list(stacks) + lands + lands_extra,
                                      start=(copies, (4 * n + 1,)), after=after)
        return sem[0], bufs, copies, token

    def pair_finish(tag, sem, bufs, copies, n, after):
        _, bufs, _ = _comm_call(tag + "_pair_wait", bufs, wait=(sem, copies), after=after)
        sums = list(_pair_sum(bufs[:n], bufs[n:2 * n], core))
        if len(bufs) > 2 * n:
            sums.append(_pair_add(bufs[2 * n], bufs[2 * n + 1]))
        return sums

    def chip_step(tag, sums, n):
        lands = [lax.empty(a.shape, a.dtype) for a in sums[:n]]
        copies = _to_chips(range(n), range(len(sums), len(sums) + n))
        local = [(i, _chip, len(sums) + i, _chip) for i in range(n)]
        if len(sums) > n:
            lands.append(lax.empty((N_DEV // 2,) + sums[n].shape, sums[n].dtype))
            copies += [Copy(n, None, 2 * n + 1, _chip, p, 0, 4 * n + p // 2) for p in SAME_CORE]
            local.append((n, None, 2 * n + 1, _chip))
        sem, bufs, token = _comm_call(tag + "_chip_start", list(sums) + lands, start=(copies, (4 * n + 4,)))
        return sem[0], bufs, copies, local, token

    def chip_finish(tag, sem, bufs, copies, local, after):
        _, bufs, _ = _comm_call(tag + "_chip_wait", bufs, wait=(sem, copies), local=local, after=after)
        return bufs[len(bufs) // 2:]

    d_gate, d_up, d_down, dn2 = _ffn_bwd(n2, dh2b, g_act, u_act, w_gate_f, w_up_f, w_down_f)
    ffn_stacks = [d.reshape(N_DEV, D_FF // N_DEV, D_MODEL) for d in (d_gate, d_up, d_down)]
    sem, bufs, copies, token = pair_step("ffn", ffn_stacks, None, None)
    dh1, dy_pool, dy_lru, d_out, d_ln2, d_gnp, d_gnl = _bwd_out(dn2, dh2, h1, y_pool, y_lru, gn_pool_g, gn_lru_g,
                                                                 w_out_f, ln2_g, token)
    ffn_sums = pair_finish("ffn", sem, bufs, copies, 3, dh1)
    ffn_sem, ffn_bufs, ffn_copies, ffn_local, token = chip_step("ffn", ffn_sums, 3)
    (du_pool, du_lru, du_gate, d_pw, d_ps, d_cw, d_cb, d_wa, d_ba, d_wi, d_bi, d_lam) = _mixer_bwd(
        u_pool, u_lru, u_gate, h, dy_pool, dy_lru, pool_w[0], pool_scale, conv_w_f, conv_b,
        wa_bd, b_a, wi_bd, b_i, lam, token)
    grad_x, d_in, d_ln1 = _bwd_in(xs, dh1, du_pool, du_lru, du_gate, ln1_g, w_in_f)

    vec_grads = dict(ln1_g=d_ln1, ln2_g=d_ln2, lnf_g=d_lnf, pool_scale=d_ps, conv_b=d_cb, b_a=d_ba, b_i=d_bi,
                     lam=d_lam, gn_pool_g=d_gnp, gn_lru_g=d_gnl)
    packed = _pack_small([vec_grads[k] for k, _ in VECTORS], [d_pw, d_wa, d_wi], d_cw, sq)
    rest_stacks = [d_in.reshape(N_DEV, IN_WIDTH // N_DEV, D_MODEL), d_out.reshape(N_DEV, D_MODEL // N_DEV, D_MODEL)]
    sem, bufs, copies, _ = pair_step("rest", rest_stacks, packed, None)
    rest_sums = pair_finish("rest", sem, bufs, copies, 2, None)
    rest_sem, rest_bufs, rest_copies, rest_local, token = chip_step("rest", rest_sums, 2)

    results = {}
    r_gate, r_up, r_down = chip_finish("ffn", ffn_sem, ffn_bufs, ffn_copies, ffn_local, token)
    for name, parts in (("w_ffn_gate", r_gate), ("w_ffn_up", r_up), ("w_ffn_down", r_down)):
        outs = _reduce_adam(parts, shard[name](weights[name]), shard[name](mom1[name]), shard[name](mom2[name]),
                            "adam_" + name)
        results[name] = tuple(unshard[name](o) for o in outs)
    r_in, r_out, r_small = chip_finish("rest", rest_sem, rest_bufs, rest_copies, rest_local, outs[0])
    for name, parts in (("w_in", r_in), ("w_out", r_out)):
        outs = _reduce_adam(parts, shard[name](weights[name]), shard[name](mom1[name]), shard[name](mom2[name]),
                            "adam_" + name)
        results[name] = tuple(unshard[name](o) for o in outs)

    def as_row(a, width):
        return a.reshape(1, width)

    def as_matrix(a):
        return a.reshape(MATRIX_ROWS, SLAB)

    def as_heads(a):
        return a.reshape(MATRIX_ROWS, 64)

    vec_out, pool_out, g_wa_bd, g_wi_bd, g_conv, loss_11 = _small_reduce_adam(
        r_small, [as_row(weights[k], w) for k, w in VECTORS], [as_row(mom1[k], w) for k, w in VECTORS],
        [as_row(mom2[k], w) for k, w in VECTORS], [as_matrix(t["pool_w"]) for t in (weights, mom1, mom2)])
    for (k, _), outs in zip(VECTORS, vec_out):
        results[k] = tuple(o.reshape(weights[k].shape) for o in outs)
    results["pool_w"] = tuple(o.reshape(pool_w.shape) for o in pool_out)
    my_columns = conv_w.shape[-1]
    my_index = 4 * lax.axis_index("x") + 2 * lax.axis_index("y") + lax.axis_index("c")
    plain_names = ("w_a", "w_i", "conv_w")
    plain_grads = [as_heads(_diag_blocks(g.reshape(N_SLAB, SLAB, SLAB))) for g in (g_wa_bd, g_wi_bd)]
    plain_grads.append(lax.dynamic_slice_in_dim(g_conv, my_index * my_columns, my_columns, axis=1))
    views = (as_heads, as_heads, lambda a: a[0])
    plain_out = _plain_adam(plain_grads, *[[view(t[k]) for k, view in zip(plain_names, views)]
                                           for t in (weights, mom1, mom2)])
    for k, g, outs in zip(plain_names, plain_grads, plain_out):
        results[k] = tuple(o.reshape(weights[k].shape) for o in (g,) + outs)
    loss = loss_11[0, 0]

    order = ["ln1_g", "w_in", "pool_w", "pool_scale", "conv_w", "conv_b", "w_a", "b_a", "w_i", "b_i", "lam",
             "gn_pool_g", "gn_lru_g", "w_out", "ln2_g", "w_ffn_gate", "w_ffn_up", "w_ffn_down", "lnf_g"]
    return (loss, grad_x[None],
            *[results[k][0] for k in order], *[results[k][1] for k in order],
            *[results[k][2] for k in order], *[results[k][3] for k in order])
```

```python
from typing import Any, NamedTuple

import jax
import jax.numpy as jnp
from jax import lax
from jax.experimental import pallas as pl
from jax.experimental.pallas import tpu as pltpu

F32 = jnp.float32
BF16 = jnp.bfloat16

N_DEV = 8
D_MODEL = 1024
POOL_WIDTH = 512
LRU_WIDTH = 512
IN_WIDTH = 1536
D_FF = 2816
N_SLAB = 4
SLAB = 128
CONV_WIDTH = 4
LRU_C = 8.0
EPS = 1e-6
HALO = 16
FF_CHUNK = 256

ADAM_LR = 0.001
ADAM_B1 = 0.9
ADAM_B2 = 0.999
ADAM_EPS = 1e-08
ADAM_WD = 0.01
ADAM_STEP = 10

MIB = 1 << 20
MESH = pl.DeviceIdType.MESH


def _params(vmem_mib, n_axes=0):
    sem = ("arbitrary",) * n_axes if n_axes else None
    return pltpu.CompilerParams(dimension_semantics=sem, vmem_limit_bytes=vmem_mib * MIB)


def _mm(a, b):
    return jnp.dot(a, b, preferred_element_type=F32)


def _mm_nt(a, b):
    return lax.dot_general(a, b, (((1,), (1,)), ((), ())), preferred_element_type=F32)


def _mm_tn(a, b):
    return lax.dot_general(a, b, (((0,), (0,)), ((), ())), preferred_element_type=F32)


def _rms(x, g):
    rstd = lax.rsqrt(jnp.mean(x * x, axis=-1, keepdims=True) + EPS)
    xhat = x * rstd
    return xhat * g, xhat, rstd


def _rms_bwd(dy, xhat, rstd, g):
    gy = dy * g
    dx = rstd * (gy - xhat * jnp.mean(gy * xhat, axis=-1, keepdims=True))
    return dx, jnp.sum(dy * xhat, axis=0, keepdims=True)


def _gelu(z):
    t = jnp.tanh(0.7978845608028654 * (z + 0.044715 * z * z * z))
    return 0.5 * z * (1.0 + t), t


def _gelu_grad(z, t):
    return 0.5 * (1.0 + t) + 0.5 * z * (1.0 - t * t) * 0.7978845608028654 * (1.0 + 3.0 * 0.044715 * z * z)


def _softplus_neg(lam):
    x = -lam
    e = jnp.exp(-jnp.abs(x))
    u = 1.0 + e
    l1p = jnp.where(u == 1.0, e, jnp.log(u) * e / (u - 1.0))
    return jnp.maximum(x, 0.0) + l1p


def _expm1(x):
    p = x * (1.0 + x * (0.5 + x * (1.0 / 6.0 + x * (1.0 / 24.0 + x * (1.0 / 120.0)))))
    return jnp.where(jnp.abs(x) < 0.1, p, jnp.exp(x) - 1.0)


def _down(v, d):
    return pltpu.roll(v, d, 0)


def _up(v, d):
    return pltpu.roll(v, v.shape[0] - d, 0)


def _token_block(s):
    return 512 if s % 512 == 0 and s > 512 else 256


def _time_chunk(s):
    return 256 if s % 256 == 0 else s


def _fwd_in(x, ln1_g, w_in_t):
    s = x.shape[0]
    tb = _token_block(s)

    def body(x_ref, g_ref, w_ref, up_ref, ul_ref, ug_ref):
        n, _, _ = _rms(x_ref[...], g_ref[...])
        proj = _mm_nt(n.astype(BF16), w_ref[...])
        up_ref[...] = proj[:, :POOL_WIDTH]
        ul_ref[...] = proj[:, POOL_WIDTH:POOL_WIDTH + LRU_WIDTH]
        ug_ref[...] = proj[:, POOL_WIDTH + LRU_WIDTH:]

    out = jax.ShapeDtypeStruct((s, 512), F32)
    return pl.pallas_call(
        body, name="fwd_in", grid=(s // tb,),
        in_specs=[pl.BlockSpec((tb, D_MODEL), lambda i: (i, 0)),
                  pl.BlockSpec((1, D_MODEL), lambda i: (0, 0)),
                  pl.BlockSpec((IN_WIDTH, D_MODEL), lambda i: (0, 0))],
        out_specs=[pl.BlockSpec((tb, 512), lambda i: (i, 0))] * 3,
        out_shape=[out, out, out],
        compiler_params=_params(40, 1),
    )(x, ln1_g, w_in_t)


def _pool_denominator(t0, row, window):
    return jnp.minimum((t0 + row + 1).astype(F32), window)


def _causal_window(ext, onehot):
    s2 = ext + _down(ext, 1)
    s4 = s2 + _down(s2, 2)
    s8 = s4 + _down(s4, 4)
    s16 = s8 + _down(s8, 8)
    return (onehot[0] * s2 + onehot[1] * s4 + onehot[2] * s8 + onehot[3] * s16)[HALO:]


def _anticausal_window(ext, onehot, rows):
    s2 = ext + _up(ext, 1)
    s4 = s2 + _up(s2, 2)
    s8 = s4 + _up(s4, 4)
    s16 = s8 + _up(s8, 8)
    return (onehot[0] * s2 + onehot[1] * s4 + onehot[2] * s8 + onehot[3] * s16)[:rows]


def _conv_taps(ext):
    return [ext[HALO:], _down(ext, 1)[HALO:], _down(ext, 2)[HALO:], _down(ext, 3)[HALO:]]


def _conv(taps, cw, cb):
    return cw[3:4] * taps[0] + cw[2:3] * taps[1] + cw[1:2] * taps[2] + cw[0:1] * taps[3] + cb


def _lru_gates(xc, wa, ba, wi, bi, sp):
    xb = xc.astype(BF16)
    r = jax.nn.sigmoid(_mm(xb, wa) + ba)
    i = jax.nn.sigmoid(_mm(xb, wi) + bi)
    la = (-LRU_C) * r * sp
    a = jnp.exp(la)
    mult = jnp.sqrt(jnp.maximum(-_expm1(2.0 * la), 0.0))
    return xb, r, i, a, mult


def _scan_causal(a, b, h_prev, row, rows):
    d = 1
    while d < rows:
        head = row < d
        a_sh = jnp.where(head, 1.0, _down(a, d))
        b_sh = jnp.where(head, 0.0, _down(b, d))
        b = a * b_sh + b
        a = a * a_sh
        d *= 2
    return b + a * h_prev


def _scan_anticausal(a, b, l_next, row, rows):
    d = 1
    while d < rows:
        tail = row >= rows - d
        a_sh = jnp.where(tail, 1.0, _up(a, d))
        b_sh = jnp.where(tail, 0.0, _up(b, d))
        b = a * b_sh + b
        a = a * a_sh
        d *= 2
    return b + a * l_next


def _slab_scalars():
    slab = pl.program_id(0)
    onehot = [jnp.where(slab == k, 1.0, 0.0).astype(F32) for k in range(N_SLAB)]
    window = jnp.left_shift(jnp.int32(2), slab).astype(F32)
    return onehot, window


def _slab_specs(s):
    seq = pl.BlockSpec((s, SLAB), lambda k: (0, k))
    mat = pl.BlockSpec((1, SLAB, SLAB), lambda k: (k, 0, 0))
    vec = pl.BlockSpec((1, SLAB), lambda k: (0, k))
    taps = pl.BlockSpec((CONV_WIDTH, SLAB), lambda k: (0, k))
    return seq, mat, vec, taps


def _mixer_fwd(u_pool, u_lru, u_gate, pool_w, pool_scale, conv_w, conv_b, wa_bd, b_a, wi_bd, b_i, lam):
    s = u_pool.shape[0]
    tc = _time_chunk(s)
    n_chunks = s // tc

    def body(up_ref, ul_ref, ug_ref, pw_ref, ps_ref, cw_ref, cb_ref, wa_ref, ba_ref, wi_ref, bi_ref, lam_ref,
             yp_ref, h_ref, yl_ref):
        onehot, window = _slab_scalars()
        pw = pw_ref[0].astype(BF16)
        wa = wa_ref[0].astype(BF16)
        wi = wi_ref[0].astype(BF16)
        ps, cw, cb, ba, bi = ps_ref[...], cw_ref[...], cb_ref[...], ba_ref[...], bi_ref[...]
        sp = _softplus_neg(lam_ref[...])
        row = lax.broadcasted_iota(jnp.int32, (tc, SLAB), 0)

        def chunk(t0, ext_p, ext_l, h_prev):
            rows = pl.ds(t0, tc)
            d = _causal_window(ext_p, onehot) / _pool_denominator(t0, row, window) - ext_p[HALO:]
            yp_ref[rows, :] = _mm(d.astype(BF16), pw) * ps
            xc = _conv(_conv_taps(ext_l), cw, cb)
            _, _, i, a, mult = _lru_gates(xc, wa, ba, wi, bi, sp)
            h = _scan_causal(a, mult * (i * xc), h_prev, row, tc)
            h_ref[rows, :] = h
            yl_ref[rows, :] = h * _gelu(ug_ref[rows, :])[0]
            return h[tc - 1:tc, :]

        pad = jnp.zeros((HALO, SLAB), F32)
        h0 = chunk(0, jnp.concatenate([pad, up_ref[pl.ds(0, tc), :]], axis=0),
                   jnp.concatenate([pad, ul_ref[pl.ds(0, tc), :]], axis=0), jnp.zeros((1, SLAB), F32))

        def step(c, h_prev):
            t0 = pl.multiple_of(c * tc, tc)
            ext = pl.ds(pl.multiple_of(c * tc - HALO, HALO), tc + HALO)
            return chunk(t0, up_ref[ext, :], ul_ref[ext, :], h_prev)

        lax.fori_loop(1, n_chunks, step, h0)

    seq, mat, vec, taps = _slab_specs(s)
    out = jax.ShapeDtypeStruct((s, 512), F32)
    return pl.pallas_call(
        body, name="mixer_fwd", grid=(N_SLAB,),
        in_specs=[seq, seq, seq, mat, vec, taps, vec, mat, vec, mat, vec, vec],
        out_specs=[seq, seq, seq], out_shape=[out, out, out],
        compiler_params=_params(48, 1),
    )(u_pool, u_lru, u_gate, pool_w, pool_scale, conv_w, conv_b, wa_bd, b_a, wi_bd, b_i, lam)


def _fwd_out(x, y_pool, y_lru, gn_pool_g, gn_lru_g, w_out, ln2_g):
    s = x.shape[0]
    tb = _token_block(s)

    def body(x_ref, yp_ref, yl_ref, gp_ref, gl_ref, w_ref, g2_ref, h1_ref, n2_ref):
        mp, _, _ = _rms(yp_ref[...], gp_ref[...])
        ml, _, _ = _rms(yl_ref[...], gl_ref[...])
        h1 = x_ref[...] + _mm(mp.astype(BF16), w_ref[:POOL_WIDTH, :]) + _mm(ml.astype(BF16), w_ref[POOL_WIDTH:, :])
        h1_ref[...] = h1
        n2_ref[...] = _rms(h1, g2_ref[...])[0].astype(BF16)

    row = pl.BlockSpec((tb, D_MODEL), lambda i: (i, 0))
    half = pl.BlockSpec((tb, 512), lambda i: (i, 0))
    return pl.pallas_call(
        body, name="fwd_out", grid=(s // tb,),
        in_specs=[row, half, half, pl.BlockSpec((1, 512), lambda i: (0, 0)), pl.BlockSpec((1, 512), lambda i: (0, 0)),
                  pl.BlockSpec((D_MODEL, D_MODEL), lambda i: (0, 0)), pl.BlockSpec((1, D_MODEL), lambda i: (0, 0))],
        out_specs=[row, row],
        out_shape=[jax.ShapeDtypeStruct((s, D_MODEL), F32), jax.ShapeDtypeStruct((s, D_MODEL), BF16)],
        compiler_params=_params(40, 1),
    )(x, y_pool, y_lru, gn_pool_g, gn_lru_g, w_out, ln2_g)


def _ffn_fwd(h1, n2, target, lnf_g, w_gate, w_up, w_down):
    s = h1.shape[0]
    tb = 256
    n_ff = D_FF // FF_CHUNK

    def body(h1_ref, n2_ref, t_ref, gf_ref, wg_hbm, wu_hbm, wd_hbm,
             g_ref, u_ref, dh_ref, dhb_ref, dgf_ref, sq_ref, wg, wu, wd, sem):
        @pl.when(pl.program_id(0) == 0)
        def _():
            loads = [pltpu.make_async_copy(src, dst, sem.at[k])
                     for k, (src, dst) in enumerate(((wg_hbm, wg), (wu_hbm, wu), (wd_hbm, wd)))]
            for cp in loads:
                cp.start()
            for cp in loads:
                cp.wait()
            dgf_ref[...] = jnp.zeros_like(dgf_ref)
            sq_ref[...] = jnp.zeros_like(sq_ref)

        n2v = n2_ref[...]
        acc = jnp.zeros((tb, D_MODEL), F32)
        for c in range(n_ff):
            cols = slice(c * FF_CHUNK, (c + 1) * FF_CHUNK)
            g = _mm_nt(n2v, wg[cols, :])
            u = _mm_nt(n2v, wu[cols, :])
            g_ref[:, cols] = g.astype(BF16)
            u_ref[:, cols] = u.astype(BF16)
            act = g * jax.nn.sigmoid(g) * u
            acc = acc + _mm(act.astype(BF16), wd[cols, :])
        gf = gf_ref[...]
        y, xhat, rstd = _rms(h1_ref[...] + acc, gf)
        err = y - t_ref[...]
        sq_ref[...] += jnp.sum(err * err, axis=0, keepdims=True)
        dh2, dgf = _rms_bwd(err * (1.0 / D_MODEL), xhat, rstd, gf)
        dgf_ref[...] += dgf
        dh_ref[...] = dh2
        dhb_ref[...] = dh2.astype(BF16)

    row = pl.BlockSpec((tb, D_MODEL), lambda i: (i, 0))
    ff = pl.BlockSpec((tb, D_FF), lambda i: (i, 0))
    vec = pl.BlockSpec((1, D_MODEL), lambda i: (0, 0))
    anyspace = pl.BlockSpec(memory_space=pl.ANY)
    return pl.pallas_call(
        body, name="ffn_fwd", grid=(s // tb,),
        in_specs=[row, row, row, vec, anyspace, anyspace, anyspace],
        out_specs=[ff, ff, row, row, vec, vec],
        out_shape=[jax.ShapeDtypeStruct((s, D_FF), BF16), jax.ShapeDtypeStruct((s, D_FF), BF16),
                   jax.ShapeDtypeStruct((s, D_MODEL), F32), jax.ShapeDtypeStruct((s, D_MODEL), BF16),
                   jax.ShapeDtypeStruct((1, D_MODEL), F32), jax.ShapeDtypeStruct((1, D_MODEL), F32)],
        scratch_shapes=[pltpu.VMEM((D_FF, D_MODEL), BF16), pltpu.VMEM((D_FF, D_MODEL), BF16),
                        pltpu.VMEM((D_FF, D_MODEL), BF16), pltpu.SemaphoreType.DMA((3,))],
        compiler_params=_params(56, 1),
    )(h1, n2, target, lnf_g, w_gate, w_up, w_down)


def _ffn_bwd(n2, dh2b, g, u, w_gate_t, w_up_t, w_down):
    s = n2.shape[0]
    tb = min(1024, s)
    n_ff = D_FF // FF_CHUNK
    n_tb = s // tb

    def body(n2_ref, dh_ref, g_ref, u_ref, wg_ref, wu_ref, wd_ref, dwg_ref, dwu_ref, dwd_ref, dn2_ref,
             dn2_acc, acc_g, acc_u, acc_d):
        j = pl.program_id(0)
        t = pl.program_id(1)

        @pl.when(t == 0)
        def _():
            acc_g[...] = jnp.zeros_like(acc_g)
            acc_u[...] = jnp.zeros_like(acc_u)
            acc_d[...] = jnp.zeros_like(acc_d)

        n2v = n2_ref[...]
        dh = dh_ref[...]
        gv = g_ref[...].astype(F32)
        uv = u_ref[...].astype(F32)
        sg = jax.nn.sigmoid(gv)
        silu = gv * sg
        dact = _mm_nt(dh, wd_ref[...])
        dub = (dact * silu).astype(BF16)
        dgb = (dact * uv * (sg * (1.0 + gv * (1.0 - sg)))).astype(BF16)
        acc_d[...] += _mm_tn((silu * uv).astype(BF16), dh)
        acc_g[...] += _mm_tn(dgb, n2v)
        acc_u[...] += _mm_tn(dub, n2v)
        part = _mm(dgb, wg_ref[...]) + _mm(dub, wu_ref[...])
        rows = pl.ds(pl.multiple_of(t * tb, tb), tb)

        @pl.when(t == n_tb - 1)
        def _():
            dwg_ref[...] = acc_g[...].astype(BF16)
            dwu_ref[...] = acc_u[...].astype(BF16)
            dwd_ref[...] = acc_d[...].astype(BF16)

        @pl.when(j == 0)
        def _():
            dn2_acc[rows, :] = part

        @pl.when(jnp.logical_and(j > 0, j < n_ff - 1))
        def _():
            dn2_acc[rows, :] += part

        @pl.when(j == n_ff - 1)
        def _():
            dn2_ref[...] = dn2_acc[rows, :] + part

    row = pl.BlockSpec((tb, D_MODEL), lambda j, t: (t, 0))
    act = pl.BlockSpec((tb, FF_CHUNK), lambda j, t: (t, j))
    w_row = pl.BlockSpec((FF_CHUNK, D_MODEL), lambda j, t: (j, 0))
    last = pl.BlockSpec((tb, D_MODEL), lambda j, t: (jnp.where(j == n_ff - 1, t, 0), 0))
    grad = jax.ShapeDtypeStruct((D_FF, D_MODEL), BF16)
    chunk_acc = pltpu.VMEM((FF_CHUNK, D_MODEL), F32)
    return pl.pallas_call(
        body, name="ffn_bwd", grid=(n_ff, n_tb),
        in_specs=[row, row, act, act, w_row, w_row, w_row],
        out_specs=[w_row, w_row, w_row, last],
        out_shape=[grad, grad, grad, jax.ShapeDtypeStruct((s, D_MODEL), F32)],
        scratch_shapes=[pltpu.VMEM((s, D_MODEL), F32), chunk_acc, chunk_acc, chunk_acc],
        compiler_params=_params(56, 2),
    )(n2, dh2b, g, u, w_gate_t, w_up_t, w_down)


def _bwd_out(dn2, dh2, h1, y_pool, y_lru, gn_pool_g, gn_lru_g, w_out, ln2_g, after):
    s = h1.shape[0]
    tb = _token_block(s)

    def body(dn2_ref, dh2_ref, h1_ref, yp_ref, yl_ref, gp_ref, gl_ref, w_ref, g2_ref, _after,
             dh1_ref, dyp_ref, dyl_ref, dwb_ref, dg2_ref, dgp_ref, dgl_ref, dw_ref):
        @pl.when(pl.program_id(0) == 0)
        def _():
            dw_ref[...] = jnp.zeros_like(dw_ref)
            dg2_ref[...] = jnp.zeros_like(dg2_ref)
            dgp_ref[...] = jnp.zeros_like(dgp_ref)
            dgl_ref[...] = jnp.zeros_like(dgl_ref)

        g2 = g2_ref[...]
        _, xhat2, rstd2 = _rms(h1_ref[...], g2)
        dres, dg2 = _rms_bwd(dn2_ref[...], xhat2, rstd2, g2)
        dg2_ref[...] += dg2
        dh1 = dh2_ref[...] + dres
        dh1_ref[...] = dh1
        dh1b = dh1.astype(BF16)
        gp, gl = gp_ref[...], gl_ref[...]
        mp, xhat_p, rstd_p = _rms(yp_ref[...], gp)
        ml, xhat_l, rstd_l = _rms(yl_ref[...], gl)
        dw_ref[:POOL_WIDTH, :] += _mm_tn(mp.astype(BF16), dh1b)
        dw_ref[POOL_WIDTH:, :] += _mm_tn(ml.astype(BF16), dh1b)
        dyp, dgp = _rms_bwd(_mm_nt(dh1b, w_ref[:POOL_WIDTH, :]), xhat_p, rstd_p, gp)
        dyl, dgl = _rms_bwd(_mm_nt(dh1b, w_ref[POOL_WIDTH:, :]), xhat_l, rstd_l, gl)
        dyp_ref[...] = dyp
        dyl_ref[...] = dyl
        dgp_ref[...] += dgp
        dgl_ref[...] += dgl

        @pl.when(pl.program_id(0) == s // tb - 1)
        def _():
            dwb_ref[...] = dw_ref[...].astype(BF16)

    row = pl.BlockSpec((tb, D_MODEL), lambda i: (i, 0))
    half = pl.BlockSpec((tb, 512), lambda i: (i, 0))
    vec = pl.BlockSpec((1, D_MODEL), lambda i: (0, 0))
    hvec = pl.BlockSpec((1, 512), lambda i: (0, 0))
    mat = pl.BlockSpec((D_MODEL, D_MODEL), lambda i: (0, 0))
    return pl.pallas_call(
        body, name="bwd_out", grid=(s // tb,),
        in_specs=[row, row, row, half, half, hvec, hvec, mat, vec, pl.BlockSpec(memory_space=pl.ANY)],
        out_specs=[row, half, half, mat, vec, hvec, hvec],
        out_shape=[jax.ShapeDtypeStruct((s, D_MODEL), F32), jax.ShapeDtypeStruct((s, 512), F32),
                   jax.ShapeDtypeStruct((s, 512), F32), jax.ShapeDtypeStruct((D_MODEL, D_MODEL), BF16),
                   jax.ShapeDtypeStruct((1, D_MODEL), F32), jax.ShapeDtypeStruct((1, 512), F32),
                   jax.ShapeDtypeStruct((1, 512), F32)],
        scratch_shapes=[pltpu.VMEM((D_MODEL, D_MODEL), F32)],
        compiler_params=_params(48, 1),
    )(dn2, dh2, h1, y_pool, y_lru, gn_pool_g, gn_lru_g, w_out, ln2_g, after)


def _mixer_bwd(u_pool, u_lru, u_gate, h, dy_pool, dy_lru,
               pool_w, pool_scale, conv_w, conv_b, wa_bd, b_a, wi_bd, b_i, lam, after):
    s = u_pool.shape[0]
    tc = _time_chunk(s)
    n_chunks = s // tc

    def body(up_ref, ul_ref, ug_ref, h_ref, dyp_ref, dyl_ref,
             pw_ref, ps_ref, cw_ref, cb_ref, wa_ref, ba_ref, wi_ref, bi_ref, lam_ref, _after,
             dup_ref, dul_ref, dug_ref, dpw_ref, dps_ref, dcw_ref, dcb_ref, dwa_ref, dba_ref, dwi_ref, dbi_ref, dlam_ref):
        onehot, window = _slab_scalars()
        pw = pw_ref[0].astype(BF16)
        wa = wa_ref[0].astype(BF16)
        wi = wi_ref[0].astype(BF16)
        ps, cw, cb, ba, bi = ps_ref[...], cw_ref[...], cb_ref[...], ba_ref[...], bi_ref[...]
        lam_v = lam_ref[...]
        sp = _softplus_neg(lam_v)
        row = lax.broadcasted_iota(jnp.int32, (tc, SLAB), 0)
        for ref in (dpw_ref, dps_ref, dcw_ref, dcb_ref, dwa_ref, dba_ref, dwi_ref, dbi_ref, dlam_ref):
            ref[...] = jnp.zeros_like(ref)

        def chunk(t0, ext_p, ext_l, ext_h, carry):
            l_next, a_next, dxc_next, ddn_next = carry
            rows = pl.ds(t0, tc)
            taps = _conv_taps(ext_l)
            xc = _conv(taps, cw, cb)
            xb, r, i, a, mult = _lru_gates(xc, wa, ba, wi, bi, sp)
            hv = ext_h[HALO:]
            h_before = _down(ext_h, 1)[HALO:]
            ug = ug_ref[rows, :]
            dyl = dyl_ref[rows, :]
            gel, th = _gelu(ug)
            dug_ref[rows, :] = dyl * hv * _gelu_grad(ug, th)
            a_after = jnp.where(row == tc - 1, a_next, _up(a, 1))
            l = _scan_anticausal(a_after, dyl * gel, l_next, row, tc)
            dmult = l * (i * xc)
            di = l * mult * xc
            dxc = l * mult * i
            dla = (l * h_before) * a - jnp.where(mult > 0.0, dmult * (a * a) / mult, 0.0)
            dlam_ref[...] += jnp.sum(dla * r, axis=0, keepdims=True)
            dpa = (dla * ((-LRU_C) * sp)) * (r * (1.0 - r))
            dpi = di * (i * (1.0 - i))
            dpab = dpa.astype(BF16)
            dpib = dpi.astype(BF16)
            dwa_ref[0] += _mm_tn(xb, dpab)
            dwi_ref[0] += _mm_tn(xb, dpib)
            dba_ref[...] += jnp.sum(dpa, axis=0, keepdims=True)
            dbi_ref[...] += jnp.sum(dpi, axis=0, keepdims=True)
            dxc = dxc + _mm_nt(dpab, wa) + _mm_nt(dpib, wi)
            ext_d = jnp.concatenate([dxc, dxc_next], axis=0)
            dul_ref[rows, :] = (cw[3:4] * dxc + cw[2:3] * _up(ext_d, 1)[:tc]
                                + cw[1:2] * _up(ext_d, 2)[:tc] + cw[0:1] * _up(ext_d, 3)[:tc])
            for k in range(CONV_WIDTH):
                dcw_ref[k:k + 1, :] += jnp.sum(dxc * taps[CONV_WIDTH - 1 - k], axis=0, keepdims=True)
            dcb_ref[...] += jnp.sum(dxc, axis=0, keepdims=True)
            denom = _pool_denominator(t0, row, window)
            db = (_causal_window(ext_p, onehot) / denom - ext_p[HALO:]).astype(BF16)
            dyp = dyp_ref[rows, :]
            dps_ref[...] += jnp.sum(dyp * _mm(db, pw), axis=0, keepdims=True)
            dys = (dyp * ps).astype(BF16)
            dpw_ref[0] += _mm_tn(db, dys)
            dd = _mm_nt(dys, pw)
            ddn = dd / denom
            ext_q = jnp.concatenate([ddn, ddn_next], axis=0)
            dup_ref[rows, :] = _anticausal_window(ext_q, onehot, tc) - dd
            return l[0:1, :], a[0:1, :], dxc[0:8, :], ddn[0:HALO, :]

        def step(k, carry):
            c = n_chunks - 1 - k
            t0 = pl.multiple_of(c * tc, tc)
            ext = pl.ds(pl.multiple_of(c * tc - HALO, HALO), tc + HALO)
            return chunk(t0, up_ref[ext, :], ul_ref[ext, :], h_ref[ext, :], carry)

        carry = (jnp.zeros((1, SLAB), F32), jnp.zeros((1, SLAB), F32),
                 jnp.zeros((8, SLAB), F32), jnp.zeros((HALO, SLAB), F32))
        carry = lax.fori_loop(0, n_chunks - 1, step, carry)
        pad = jnp.zeros((HALO, SLAB), F32)
        first = pl.ds(0, tc)
        chunk(0, jnp.concatenate([pad, up_ref[first, :]], axis=0), jnp.concatenate([pad, ul_ref[first, :]], axis=0),
              jnp.concatenate([pad, h_ref[first, :]], axis=0), carry)
        dlam_ref[...] = dlam_ref[...] * (LRU_C * jax.nn.sigmoid(-lam_v))

    seq, mat, vec, taps = _slab_specs(s)
    full = jax.ShapeDtypeStruct((s, 512), F32)
    mats = jax.ShapeDtypeStruct((N_SLAB, SLAB, SLAB), F32)
    vecs = jax.ShapeDtypeStruct((1, 512), F32)
    return pl.pallas_call(
        body, name="mixer_bwd", grid=(N_SLAB,),
        in_specs=[seq] * 6 + [mat, vec, taps, vec, mat, vec, mat, vec, vec, pl.BlockSpec(memory_space=pl.ANY)],
        out_specs=[seq, seq, seq, mat, vec, taps, vec, mat, vec, mat, vec, vec],
        out_shape=[full, full, full, mats, vecs, jax.ShapeDtypeStruct((CONV_WIDTH, 512), F32), vecs,
                   mats, vecs, mats, vecs, vecs],
        compiler_params=_params(56, 1),
    )(u_pool, u_lru, u_gate, h, dy_pool, dy_lru, pool_w, pool_scale, conv_w, conv_b, wa_bd, b_a, wi_bd, b_i, lam,
      after)


def _bwd_in(x, dh1, du_pool, du_lru, du_gate, ln1_g, w_in_t):
    s = x.shape[0]
    tb = _token_block(s)

    def body(x_ref, dh1_ref, dup_ref, dul_ref, dug_ref, g_ref, w_ref, dx_ref, dwb_ref, dg_ref, dw_ref):
        @pl.when(pl.program_id(0) == 0)
        def _():
            dw_ref[...] = jnp.zeros_like(dw_ref)
            dg_ref[...] = jnp.zeros_like(dg_ref)

        g1 = g_ref[...]
        n, xhat, rstd = _rms(x_ref[...], g1)
        nb = n.astype(BF16)
        dn = jnp.zeros((tb, D_MODEL), F32)
        for k, ref in enumerate((dup_ref, dul_ref, dug_ref)):
            rows = slice(k * 512, (k + 1) * 512)
            db = ref[...].astype(BF16)
            dw_ref[rows, :] += _mm_tn(db, nb)
            dn = dn + _mm(db, w_ref[rows, :])
        dx, dg1 = _rms_bwd(dn, xhat, rstd, g1)
        dx_ref[...] = dh1_ref[...] + dx
        dg_ref[...] += dg1

        @pl.when(pl.program_id(0) == s // tb - 1)
        def _():
            dwb_ref[...] = dw_ref[...].astype(BF16)

    row = pl.BlockSpec((tb, D_MODEL), lambda i: (i, 0))
    half = pl.BlockSpec((tb, 512), lambda i: (i, 0))
    vec = pl.BlockSpec((1, D_MODEL), lambda i: (0, 0))
    mat = pl.BlockSpec((IN_WIDTH, D_MODEL), lambda i: (0, 0))
    return pl.pallas_call(
        body, name="bwd_in", grid=(s // tb,),
        in_specs=[row, row, half, half, half, vec, mat],
        out_specs=[row, mat, vec],
        out_shape=[jax.ShapeDtypeStruct((s, D_MODEL), F32), jax.ShapeDtypeStruct((IN_WIDTH, D_MODEL), BF16),
                   jax.ShapeDtypeStruct((1, D_MODEL), F32)],
        scratch_shapes=[pltpu.VMEM((IN_WIDTH, D_MODEL), F32)],
        compiler_params=_params(48, 1),
    )(x, dh1, du_pool, du_lru, du_gate, ln1_g, w_in_t)


def _mesh_position():
    x, y, c = lax.axis_index("x"), lax.axis_index("y"), lax.axis_index("c")
    return x, y, c, 4 * x + 2 * y + c


def _peer(x, y, c, p):
    px = 1 - x if p & 4 else x
    py = 1 - y if p & 2 else y
    pc = 1 - c if p & 1 else c
    return (px, py, pc), 4 * px + 2 * py + pc


HBM_SPEC = pl.BlockSpec(memory_space=pltpu.HBM)
SEM_SPEC = pl.BlockSpec(memory_space=pltpu.SEMAPHORE)
DATAFLOW = pltpu.SideEffectType.DATAFLOW_SIDE_EFFECTING


class Copy(NamedTuple):
    src: int
    src_at: Any
    dst: int
    dst_at: Any
    peer: int
    group: int
    slot: int
    rows: Any = None


SIBLING = (1,)
SAME_CORE = (2, 4, 6)
EVERYONE = tuple(range(1, N_DEV))


def _same(index):
    return index


def _chip(index):
    return jnp.right_shift(index, 1)


def _fan_out(srcs, lands, peers, group):
    return [Copy(s, None, d, _same, p, group, N_DEV * i + p) for i, (s, d) in enumerate(zip(srcs, lands)) for p in peers]


ICI_CHUNK_ROWS = 32


def _in_chunks(copies, n_rows):
    return [cp._replace(rows=(first, ICI_CHUNK_ROWS)) for cp in copies for first in range(0, n_rows, ICI_CHUNK_ROWS)]


def _numbered(copies, group=0):
    return [cp._replace(group=group, slot=i) for i, cp in enumerate(copies)]


def _relay(lands, peers):
    return [Copy(b, lambda s, q=q: jnp.bitwise_xor(s, q), b, lambda s, q=q: jnp.bitwise_xor(s, q), 1, 0, N_DEV * i + q)
            for i, b in enumerate(lands) for q in peers]


def _to_sibling(stacks, lands):
    return [Copy(s, lambda me, k=k: 2 * k + 1 - jnp.bitwise_and(me, 1), d, lambda me, k=k: k, 1, 0, 4 * i + k)
            for i, (s, d) in enumerate(zip(stacks, lands)) for k in range(N_DEV // 2)]


def _to_chips(sums, lands):
    return [Copy(s, lambda me, p=p: jnp.bitwise_xor(_chip(me), p // 2), d, _chip, p, 0, 4 * i + p // 2)
            for i, (s, d) in enumerate(zip(sums, lands)) for p in SAME_CORE]


def _comm_call(name, bufs, wait=None, local=(), start=None, after=None):
    nb = len(bufs)
    slots = list(start[1]) if start else []
    n_out_sem = 2 * len(slots)

    def body(*refs):
        b = refs[:nb]
        at = nb
        if wait:
            w_send, w_recv = refs[at], refs[at + 1]
            at += 2
        if after is not None:
            at += 1
        out_sems = refs[at:at + n_out_sem]
        token = refs[at + n_out_sem + nb]
        x, y, c, me = _mesh_position()

        def part(i, row_of, sender, rows=None):
            ref = b[i] if row_of is None else b[i].at[row_of(sender)]
            return ref if rows is None else ref.at[pl.ds(rows[0], rows[1])]

        if wait:
            for cp in wait[1]:
                peer, peer_index = _peer(x, y, c, cp.peer)
                arrival = pltpu.make_async_remote_copy(part(cp.src, cp.src_at, me, cp.rows),
                                                       part(cp.dst, cp.dst_at, peer_index, cp.rows),
                                                       w_send.at[cp.slot], w_recv.at[cp.slot],
                                                       device_id=peer, device_id_type=MESH)
                arrival.wait_send()
                arrival.wait_recv()
        if local:
            local_sems = refs[-1]
            copies = [pltpu.make_async_copy(part(sb, sf, me), part(db, df, me), local_sems.at[k])
                      for k, (sb, sf, db, df) in enumerate(local)]
            for cp in copies:
                cp.start()
            for cp in copies:
                cp.wait()
        if start:
            for cp in start[0]:
                peer, _ = _peer(x, y, c, cp.peer)
                pltpu.make_async_remote_copy(part(cp.src, cp.src_at, me, cp.rows), part(cp.dst, cp.dst_at, me, cp.rows),
                                             out_sems[2 * cp.group].at[cp.slot], out_sems[2 * cp.group + 1].at[cp.slot],
                                             device_id=peer, device_id_type=MESH).start()
        token[...] = jnp.zeros_like(token)

    sem_shapes = []
    for n_slots in slots:
        sem_shapes += [pltpu.SemaphoreType.DMA((n_slots,))] * 2
    operands = [pltpu.with_memory_space_constraint(a, pltpu.HBM) for a in bufs]
    in_specs = [HBM_SPEC] * nb
    if wait:
        operands += list(wait[0])
        in_specs += [SEM_SPEC, SEM_SPEC]
    if after is not None:
        operands.append(after)
        in_specs.append(pl.BlockSpec(memory_space=pl.ANY))
    outs = pl.pallas_call(
        body, name=name, in_specs=in_specs,
        out_specs=[SEM_SPEC] * n_out_sem + [HBM_SPEC] * nb + [pl.BlockSpec(memory_space=pltpu.VMEM)],
        out_shape=sem_shapes + [pltpu.HBM(a.shape, a.dtype) for a in bufs] + [jax.ShapeDtypeStruct((8, SLAB), F32)],
        input_output_aliases={i: n_out_sem + i for i in range(nb)},
        scratch_shapes=[pltpu.SemaphoreType.DMA((len(local),))] if local else [],
        compiler_params=pltpu.CompilerParams(has_side_effects=DATAFLOW),
    )(*operands)
    sems = [(outs[2 * k], outs[2 * k + 1]) for k in range(len(slots))]
    return sems, list(outs[n_out_sem:n_out_sem + nb]), outs[-1]


def _pair_sum(stacks, lands, core):
    n = len(stacks)

    def body(core_ref, *refs):
        del core_ref
        for m in range(n):
            mine, theirs, out = refs[m], refs[n + m], refs[2 * n + m]
            out[0] = (mine[0, 0].astype(F32) + theirs[0].astype(F32)).astype(out.dtype)

    in_specs = [pl.BlockSpec((1, 1) + a.shape[1:], lambda k, core_ref: (k, core_ref[0], 0, 0)) for a in stacks]
    in_specs += [pl.BlockSpec((1,) + a.shape[1:], lambda k, core_ref: (k, 0, 0)) for a in lands]
    return pl.pallas_call(
        body, name="pair_sum_" + "_".join(str(a.shape[1]) for a in stacks),
        grid_spec=pltpu.PrefetchScalarGridSpec(
            num_scalar_prefetch=1, grid=(N_DEV // 2,), in_specs=in_specs,
            out_specs=[pl.BlockSpec((1,) + a.shape[1:], lambda k, core_ref: (k, 0, 0)) for a in lands]),
        out_shape=[jax.ShapeDtypeStruct(a.shape, a.dtype) for a in lands],
        compiler_params=_params(40, 1),
    )(core, *[a.reshape((N_DEV // 2, 2) + a.shape[1:]) for a in stacks], *lands)


def _pair_add(a, b):
    def body(a_ref, b_ref, out):
        out[...] = a_ref[...] + b_ref[...]

    return pl.pallas_call(body, name="pair_add", out_shape=jax.ShapeDtypeStruct(a.shape, a.dtype))(a, b)


def _reduce_adam(parts, w, m, v, name):
    rows, cols = w.shape
    n_parts = parts.shape[0]
    rb = rows
    for cand in (256, 176, 128):
        if rows % cand == 0 and rows > cand:
            rb = cand
            break

    def body(p_ref, w_ref, m_ref, v_ref, g_out, d_out, m_out, v_out):
        g = p_ref[0].astype(F32)
        for j in range(1, n_parts):
            g = g + p_ref[j].astype(F32)
        g_out[...] = g
        d_out[...], m_out[...], v_out[...] = _adam(g, w_ref[...], m_ref[...], v_ref[...])

    blk = pl.BlockSpec((rb, cols), lambda i: (i, 0))
    out = jax.ShapeDtypeStruct((rows, cols), F32)
    return pl.pallas_call(
        body, name=name, grid=(rows // rb,),
        in_specs=[pl.BlockSpec((n_parts, rb, cols), lambda i: (0, i, 0)), blk, blk, blk],
        out_specs=[blk] * 4, out_shape=[out] * 4,
        compiler_params=_params(40, 1),
    )(parts, w, m, v)


def _cols_from_stack(stack):
    n, r, c = stack.shape
    return jnp.transpose(stack, (1, 0, 2)).reshape(r, n * c)


def _block_diag(w):
    z = jnp.zeros((N_SLAB, 64, 64), w.dtype)
    pairs = w.reshape(N_SLAB, 2, 64, 64)
    top = jnp.concatenate([pairs[:, 0], z], axis=2)
    bottom = jnp.concatenate([z, pairs[:, 1]], axis=2)
    return jnp.concatenate([top, bottom], axis=1)


def _diag_blocks(w):
    return jnp.stack([w[:, :64, :64], w[:, 64:, 64:]], axis=1).reshape(8, 64, 64)


def _adam(g, w, m, v):
    m_new = ADAM_B1 * m + (1.0 - ADAM_B1) * g
    v_new = ADAM_B2 * v + (1.0 - ADAM_B2) * (g * g)
    m_hat = m_new / (1.0 - ADAM_B1 ** ADAM_STEP)
    v_hat = v_new / (1.0 - ADAM_B2 ** ADAM_STEP)
    return (-ADAM_LR) * (m_hat / (jnp.sqrt(v_hat) + ADAM_EPS) + ADAM_WD * w), m_new, v_new


WIDE = ("ln1_g", "ln2_g", "lnf_g")
HALF = ("pool_scale", "conv_b", "b_a", "b_i", "lam", "gn_pool_g", "gn_lru_g")
VECTORS = [(k, D_MODEL) for k in WIDE] + [(k, 512) for k in HALF]
VECTOR_ROWS = sum(width // SLAB for _, width in VECTORS)
MATRIX_AT = -(-VECTOR_ROWS // 8) * 8
MATRIX_ROWS = N_SLAB * SLAB
N_MATRIX = 3
LOSS_ROW = MATRIX_AT + N_MATRIX * MATRIX_ROWS
CONV_AT = LOSS_ROW + 8
CONV_LANES = LRU_WIDTH // SLAB
PACK_ROWS = -(-(CONV_AT + CONV_WIDTH * CONV_LANES) // ICI_CHUNK_ROWS) * ICI_CHUNK_ROWS


def _pack_small(vectors, matrices, conv, sq):
    n_vec = len(vectors)

    def body(*refs):
        vec, mat, cw_ref, sq_ref, out = refs[:n_vec], refs[n_vec:n_vec + N_MATRIX], refs[-3], refs[-2], refs[-1]
        out[...] = jnp.zeros_like(out)
        row = 0
        for ref, (_, width) in zip(vec, VECTORS):
            for k in range(width // SLAB):
                out[row:row + 1, :] = ref[:, k * SLAB:(k + 1) * SLAB]
                row += 1
        for i, ref in enumerate(mat):
            for s in range(N_SLAB):
                at = MATRIX_AT + i * MATRIX_ROWS + s * SLAB
                out[at:at + SLAB, :] = ref[s]
        for tap in range(CONV_WIDTH):
            for k in range(CONV_LANES):
                at = CONV_AT + tap * CONV_LANES + k
                out[at:at + 1, :] = cw_ref[tap:tap + 1, k * SLAB:(k + 1) * SLAB]
        total = sq_ref[:, 0:SLAB]
        for k in range(1, D_MODEL // SLAB):
            total = total + sq_ref[:, k * SLAB:(k + 1) * SLAB]
        out[LOSS_ROW:LOSS_ROW + 1, :] = total

    return pl.pallas_call(
        body, name="pack_small", out_shape=jax.ShapeDtypeStruct((PACK_ROWS, SLAB), F32),
    )(*vectors, *matrices, conv, sq)


def _small_reduce_adam(parts, vec_w, vec_m, vec_v, pool_wmv):
    n_vec = len(VECTORS)
    n_parts = parts.shape[0]

    def body(*refs):
        p_ref = refs[0]
        w_refs, m_refs, v_refs = (refs[1 + k * n_vec:1 + (k + 1) * n_vec] for k in range(3))
        pw_w, pw_m, pw_v = refs[1 + 3 * n_vec:4 + 3 * n_vec]
        outs = refs[4 + 3 * n_vec:-1]
        total = refs[-1]
        total[...] = p_ref[0]
        for j in range(1, n_parts):
            total[...] += p_ref[j]
        row = 0
        for i, (_, width) in enumerate(VECTORS):
            n_rows = width // SLAB
            g = jnp.concatenate([total[row + k:row + k + 1, :] for k in range(n_rows)], axis=1)
            row += n_rows
            d, m_new, v_new = _adam(g, w_refs[i][...], m_refs[i][...], v_refs[i][...])
            for ref, val in zip(outs[4 * i:4 * i + 4], (g, d, m_new, v_new)):
                ref[...] = val
        tail = outs[4 * n_vec:]
        g = total[MATRIX_AT:MATRIX_AT + MATRIX_ROWS, :]
        d, m_new, v_new = _adam(g, pw_w[...], pw_m[...], pw_v[...])
        for ref, val in zip(tail[0:4], (g, d, m_new, v_new)):
            ref[...] = val
        tail[4][...] = total[MATRIX_AT + MATRIX_ROWS:MATRIX_AT + 2 * MATRIX_ROWS, :]
        tail[5][...] = total[MATRIX_AT + 2 * MATRIX_ROWS:MATRIX_AT + 3 * MATRIX_ROWS, :]
        for tap in range(CONV_WIDTH):
            at = CONV_AT + tap * CONV_LANES
            tail[6][tap:tap + 1, :] = jnp.concatenate([total[at + k:at + k + 1, :] for k in range(CONV_LANES)], axis=1)
        tail[7][...] = (0.5 / D_MODEL) * jnp.sum(total[LOSS_ROW:LOSS_ROW + 1, :], axis=1, keepdims=True)

    out_shape = []
    for _, width in VECTORS:
        out_shape += [jax.ShapeDtypeStruct((1, width), F32)] * 4
    out_shape += [jax.ShapeDtypeStruct((MATRIX_ROWS, SLAB), F32)] * 6
    out_shape += [jax.ShapeDtypeStruct((CONV_WIDTH, LRU_WIDTH), F32), jax.ShapeDtypeStruct((1, 1), F32)]
    outs = pl.pallas_call(
        body, name="adam_small", out_shape=out_shape,
        scratch_shapes=[pltpu.VMEM((PACK_ROWS, SLAB), F32)],
        compiler_params=_params(40),
    )(parts, *vec_w, *vec_m, *vec_v, *pool_wmv)
    vec_out = [tuple(outs[4 * i:4 * i + 4]) for i in range(n_vec)]
    tail = outs[4 * n_vec:]
    return vec_out, tuple(tail[0:4]), tail[4], tail[5], tail[6], tail[7]


def _plain_adam(grads, ws, ms, vs):
    n = len(grads)

    def body(*refs):
        ins, outs = refs[:4 * n], refs[4 * n:]
        for i in range(n):
            d, m_new, v_new = _adam(ins[i][...], ins[n + i][...], ins[2 * n + i][...], ins[3 * n + i][...])
            for ref, val in zip(outs[3 * i:3 * i + 3], (d, m_new, v_new)):
                ref[...] = val

    out_shape = []
    for g in grads:
        out_shape += [jax.ShapeDtypeStruct(g.shape, F32)] * 3
    outs = pl.pallas_call(body, name="adam_plain", out_shape=out_shape)(*grads, *ws, *ms, *vs)
    return [tuple(outs[3 * i:3 * i + 3]) for i in range(n)]


def kernel(x, ln1_g, w_in, pool_w, pool_scale, conv_w, conv_b, w_a, b_a, w_i, b_i, lam, gn_pool_g, gn_lru_g, w_out, ln2_g, w_ffn_gate, w_ffn_up, w_ffn_down, lnf_g, loss_target, m_ln1_g, m_w_in, m_pool_w, m_pool_scale, m_conv_w, m_conv_b, m_w_a, m_b_a, m_w_i, m_b_i, m_lam, m_gn_pool_g, m_gn_lru_g, m_w_out, m_ln2_g, m_w_ffn_gate, m_w_ffn_up, m_w_ffn_down, m_lnf_g, v_ln1_g, v_w_in, v_pool_w, v_pool_scale, v_conv_w, v_conv_b, v_w_a, v_b_a, v_w_i, v_b_i, v_lam, v_gn_pool_g, v_gn_lru_g, v_w_out, v_ln2_g, v_w_ffn_gate, v_w_ffn_up, v_w_ffn_down, v_lnf_g):
    weights = dict(ln1_g=ln1_g, w_in=w_in, pool_w=pool_w, pool_scale=pool_scale, conv_w=conv_w, conv_b=conv_b,
                   w_a=w_a, b_a=b_a, w_i=w_i, b_i=b_i, lam=lam, gn_pool_g=gn_pool_g, gn_lru_g=gn_lru_g,
                   w_out=w_out, ln2_g=ln2_g, w_ffn_gate=w_ffn_gate, w_ffn_up=w_ffn_up, w_ffn_down=w_ffn_down,
                   lnf_g=lnf_g)
    mom1 = dict(ln1_g=m_ln1_g, w_in=m_w_in, pool_w=m_pool_w, pool_scale=m_pool_scale, conv_w=m_conv_w,
                conv_b=m_conv_b, w_a=m_w_a, b_a=m_b_a, w_i=m_w_i, b_i=m_b_i, lam=m_lam, gn_pool_g=m_gn_pool_g,
                gn_lru_g=m_gn_lru_g, w_out=m_w_out, ln2_g=m_ln2_g, w_ffn_gate=m_w_ffn_gate,
                w_ffn_up=m_w_ffn_up, w_ffn_down=m_w_ffn_down, lnf_g=m_lnf_g)
    mom2 = dict(ln1_g=v_ln1_g, w_in=v_w_in, pool_w=v_pool_w, pool_scale=v_pool_scale, conv_w=v_conv_w,
                conv_b=v_conv_b, w_a=v_w_a, b_a=v_b_a, w_i=v_w_i, b_i=v_b_i, lam=v_lam, gn_pool_g=v_gn_pool_g,
                gn_lru_g=v_gn_lru_g, w_out=v_w_out, ln2_g=v_ln2_g, w_ffn_gate=v_w_ffn_gate,
                w_ffn_up=v_w_ffn_up, w_ffn_down=v_w_ffn_down, lnf_g=v_lnf_g)

    xs = x[0]
    target = loss_target[0]

    shard = dict(w_in=lambda a: a[0].T, w_ffn_gate=lambda a: a[0].T, w_ffn_up=lambda a: a[0].T,
                 w_out=lambda a: a[0], w_ffn_down=lambda a: a[0], conv_w=lambda a: a[0])
    unshard = dict(w_in=lambda a: a.T[None], w_ffn_gate=lambda a: a.T[None], w_ffn_up=lambda a: a.T[None],
                   w_out=lambda a: a[None], w_ffn_down=lambda a: a[None], conv_w=lambda a: a[None])

    gathered = ("w_in", "conv_w", "w_out", "w_ffn_gate", "w_ffn_up", "w_ffn_down")
    sources = [shard[k](weights[k]) if k == "conv_w" else shard[k](weights[k]).astype(BF16) for k in gathered]
    lands = [lax.empty((N_DEV,) + a.shape, a.dtype) for a in sources]

    def in_copies(srcs, dsts, group=0):
        w_in_rows = IN_WIDTH // N_DEV
        return _numbered(_in_chunks(_fan_out(srcs[:1], dsts[:1], EVERYONE, 0), w_in_rows)
                         + _fan_out(srcs[1:], dsts[1:], EVERYONE, 0), group)

    def out_copies(src, dst, group=0):
        return _numbered(_in_chunks(_fan_out((src,), (dst,), EVERYONE, 0), D_MODEL // N_DEV), group)

    def ffn_copies(srcs, dsts, group=0):
        return _numbered(_in_chunks(_fan_out(srcs, dsts, SAME_CORE, 0), D_FF // N_DEV)
                         + _fan_out(srcs, dsts, SIBLING, 0), group)

    start = in_copies((0, 1), (6, 7), 0) + out_copies(2, 8, 1) + ffn_copies((3, 4, 5), (9, 10, 11), 2)
    slots = [sum(cp.group == g for cp in start) for g in range(3)]
    sems, bufs, _ = _comm_call("gather_start", sources + lands, start=(start, slots))
    sources, lands = bufs[:6], bufs[6:]
    _, bufs, _ = _comm_call("gather_wait_in", sources[0:2] + lands[0:2], wait=(sems[0], in_copies((0, 1), (2, 3))),
                            local=[(0, None, 2, _same), (1, None, 3, _same)])
    w_in_f = bufs[2].reshape(IN_WIDTH, D_MODEL)
    conv_w_f = _cols_from_stack(bufs[3])

    wa_bd = _block_diag(w_a[0])
    wi_bd = _block_diag(w_i[0])
    lnf_row = lnf_g.reshape(1, D_MODEL)

    u_pool, u_lru, u_gate = _fwd_in(xs, ln1_g, w_in_f)
    y_pool, h, y_lru = _mixer_fwd(u_pool, u_lru, u_gate, pool_w[0], pool_scale, conv_w_f, conv_b,
                                  wa_bd, b_a, wi_bd, b_i, lam)
    _, bufs, _ = _comm_call("gather_wait_out", [sources[2], lands[2]], wait=(sems[1], out_copies(0, 1)),
                            local=[(0, None, 1, _same)], after=y_pool)
    w_out_f = bufs[1].reshape(D_MODEL, D_MODEL)
    h1, n2 = _fwd_out(xs, y_pool, y_lru, gn_pool_g, gn_lru_g, w_out_f, ln2_g)
    relay = _numbered(_relay((3, 4, 5), SAME_CORE))
    relay_sems, bufs, _ = _comm_call("gather_relay_ffn", sources[3:6] + lands[3:6],
                                     wait=(sems[2], ffn_copies((0, 1, 2), (3, 4, 5))),
                                     local=[(i, None, 3 + i, _same) for i in range(3)],
                                     start=(relay, (len(relay),)), after=n2)
    _, bufs, _ = _comm_call("gather_wait_ffn", bufs[3:6],
                            wait=(relay_sems[0], _numbered(_relay((0, 1, 2), SAME_CORE))))
    w_gate_f = bufs[0].reshape(D_FF, D_MODEL)
    w_up_f = bufs[1].reshape(D_FF, D_MODEL)
    w_down_f = bufs[2].reshape(D_FF, D_MODEL)
    g_act, u_act, dh2, dh2b, d_lnf, sq = _ffn_fwd(h1, n2, target, lnf_row, w_gate_f, w_up_f, w_down_f)

    core = lax.axis_index("c").astype(jnp.int32).reshape(1)

    def pair_step(tag, stacks, extra, after):
        n = len(stacks)
        lands = [lax.empty((N_DEV // 2,) + a.shape[1:], a.dtype) for a in stacks]
        copies = _to_sibling(range(n), range(n, 2 * n))
        if extra is not None:
            copies.append(Copy(2 * n, None, 2 * n + 1, None, 1, 0, 0))
            lands_extra = [extra, lax.empty(extra.shape, extra.dtype)]
        else:
            lands_extra = []
        copies = _numbered(copies)
        sem, bufs, token = _comm_call(tag + "_pair_start", list(stacks) + lands + lands_extra,
                                      start=(copies, (len(copies),)), after=after)
        return sem[0], bufs, copies, token

    def pair_finish(tag, sem, bufs, copies, n, after):
        _, bufs, _ = _comm_call(tag + "_pair_wait", bufs, wait=(sem, copies), after=after)
        sums = list(_pair_sum(bufs[:n], bufs[n:2 * n], core))
        if len(bufs) > 2 * n:
            sums.append(_pair_add(bufs[2 * n], bufs[2 * n + 1]))
        return sums

    def chip_step(tag, sums, n):
        lands = [lax.empty(a.shape, a.dtype) for a in sums[:n]]
        copies = []
        for i in range(n):
            copies += _in_chunks(_to_chips((i,), (len(sums) + i,)), sums[i].shape[1])
        local = [(i, _chip, len(sums) + i, _chip) for i in range(n)]
        if len(sums) > n:
            lands.append(lax.empty((N_DEV // 2,) + sums[n].shape, sums[n].dtype))
            copies += _in_chunks([Copy(n, None, 2 * n + 1, _chip, p, 0, 0) for p in SAME_CORE], sums[n].shape[0])
            local.append((n, None, 2 * n + 1, _chip))
        copies = _numbered(copies)
        sem, bufs, token = _comm_call(tag + "_chip_start", list(sums) + lands, start=(copies, (len(copies),)))
        return sem[0], bufs, copies, local, token

    def chip_finish(tag, sem, bufs, copies, local, after):
        _, bufs, _ = _comm_call(tag + "_chip_wait", bufs, wait=(sem, copies), local=local, after=after)
        return bufs[len(bufs) // 2:]

    d_gate, d_up, d_down, dn2 = _ffn_bwd(n2, dh2b, g_act, u_act, w_gate_f, w_up_f, w_down_f)
    ffn_stacks = [d.reshape(N_DEV, D_FF // N_DEV, D_MODEL) for d in (d_gate, d_up, d_down)]
    sem, bufs, copies, token = pair_step("ffn", ffn_stacks, None, None)
    dh1, dy_pool, dy_lru, d_out, d_ln2, d_gnp, d_gnl = _bwd_out(dn2, dh2, h1, y_pool, y_lru, gn_pool_g, gn_lru_g,
                                                                 w_out_f, ln2_g, token)
    ffn_sums = pair_finish("ffn", sem, bufs, copies, 3, dh1)
    ffn_sem, ffn_bufs, ffn_copies, ffn_local, token = chip_step("ffn", ffn_sums, 3)
    (du_pool, du_lru, du_gate, d_pw, d_ps, d_cw, d_cb, d_wa, d_ba, d_wi, d_bi, d_lam) = _mixer_bwd(
        u_pool, u_lru, u_gate, h, dy_pool, dy_lru, pool_w[0], pool_scale, conv_w_f, conv_b,
        wa_bd, b_a, wi_bd, b_i, lam, token)
    grad_x, d_in, d_ln1 = _bwd_in(xs, dh1, du_pool, du_lru, du_gate, ln1_g, w_in_f)

    vec_grads = dict(ln1_g=d_ln1, ln2_g=d_ln2, lnf_g=d_lnf, pool_scale=d_ps, conv_b=d_cb, b_a=d_ba, b_i=d_bi,
                     lam=d_lam, gn_pool_g=d_gnp, gn_lru_g=d_gnl)
    packed = _pack_small([vec_grads[k] for k, _ in VECTORS], [d_pw, d_wa, d_wi], d_cw, sq)
    rest_stacks = [d_in.reshape(N_DEV, IN_WIDTH // N_DEV, D_MODEL), d_out.reshape(N_DEV, D_MODEL // N_DEV, D_MODEL)]
    sem, bufs, copies, _ = pair_step("rest", rest_stacks, packed, None)
    rest_sums = pair_finish("rest", sem, bufs, copies, 2, None)
    rest_sem, rest_bufs, rest_copies, rest_local, token = chip_step("rest", rest_sums, 2)

    results = {}
    r_gate, r_up, r_down = chip_finish("ffn", ffn_sem, ffn_bufs, ffn_copies, ffn_local, token)
    for name, parts in (("w_ffn_gate", r_gate), ("w_ffn_up", r_up), ("w_ffn_down", r_down)):
        outs = _reduce_adam(parts, shard[name](weights[name]), shard[name](mom1[name]), shard[name](mom2[name]),
                            "adam_" + name)
        results[name] = tuple(unshard[name](o) for o in outs)
    r_in, r_out, r_small = chip_finish("rest", rest_sem, rest_bufs, rest_copies, rest_local, outs[0])
    for name, parts in (("w_in", r_in), ("w_out", r_out)):
        outs = _reduce_adam(parts, shard[name](weights[name]), shard[name](mom1[name]), shard[name](mom2[name]),
                            "adam_" + name)
        results[name] = tuple(unshard[name](o) for o in outs)

    def as_row(a, width):
        return a.reshape(1, width)

    def as_matrix(a):
        return a.reshape(MATRIX_ROWS, SLAB)

    def as_heads(a):
        return a.reshape(MATRIX_ROWS, 64)

    vec_out, pool_out, g_wa_bd, g_wi_bd, g_conv, loss_11 = _small_reduce_adam(
        r_small, [as_row(weights[k], w) for k, w in VECTORS], [as_row(mom1[k], w) for k, w in VECTORS],
        [as_row(mom2[k], w) for k, w in VECTORS], [as_matrix(t["pool_w"]) for t in (weights, mom1, mom2)])
    for (k, _), outs in zip(VECTORS, vec_out):
        results[k] = tuple(o.reshape(weights[k].shape) for o in outs)
    results["pool_w"] = tuple(o.reshape(pool_w.shape) for o in pool_out)
    my_columns = conv_w.shape[-1]
    my_index = 4 * lax.axis_index("x") + 2 * lax.axis_index("y") + lax.axis_index("c")
    plain_names = ("w_a", "w_i", "conv_w")
    plain_grads = [as_heads(_diag_blocks(g.reshape(N_SLAB, SLAB, SLAB))) for g in (g_wa_bd, g_wi_bd)]
    plain_grads.append(lax.dynamic_slice_in_dim(g_conv, my_index * my_columns, my_columns, axis=1))
    views = (as_heads, as_heads, lambda a: a[0])
    plain_out = _plain_adam(plain_grads, *[[view(t[k]) for k, view in zip(plain_names, views)]
                                           for t in (weights, mom1, mom2)])
    for k, g, outs in zip(plain_names, plain_grads, plain_out):
        results[k] = tuple(o.reshape(weights[k].shape) for o in (g,) + outs)
    loss = loss_11[0, 0]

    order = ["ln1_g", "w_in", "pool_w", "pool_scale", "conv_w", "conv_b", "w_a", "b_a", "w_i", "b_i", "lam",
             "gn_pool_g", "gn_lru_g", "w_out", "ln2_g", "w_ffn_gate", "w_ffn_up", "w_ffn_down", "lnf_g"]
    return (loss, grad_x[None],
            *[results[k][0] for k in order], *[results[k][1] for k in order],
            *[results[k][2] for k in order], *[results[k][3] for k in order])
```

```python
from typing import Any, NamedTuple

import jax
import jax.numpy as jnp
from jax import lax
from jax.experimental import pallas as pl
from jax.experimental.pallas import tpu as pltpu

F32 = jnp.float32
BF16 = jnp.bfloat16

N_DEV = 8
D_MODEL = 1024
POOL_WIDTH = 512
LRU_WIDTH = 512
IN_WIDTH = 1536
D_FF = 2816
N_SLAB = 4
SLAB = 128
CONV_WIDTH = 4
LRU_C = 8.0
EPS = 1e-6
HALO = 16
FF_CHUNK = 256

ADAM_LR = 0.001
ADAM_B1 = 0.9
ADAM_B2 = 0.999
ADAM_EPS = 1e-08
ADAM_WD = 0.01
ADAM_STEP = 10

MIB = 1 << 20
MESH = pl.DeviceIdType.MESH


def _params(vmem_mib, n_axes=0):
    sem = ("arbitrary",) * n_axes if n_axes else None
    return pltpu.CompilerParams(dimension_semantics=sem, vmem_limit_bytes=vmem_mib * MIB)


def _mm(a, b):
    return jnp.dot(a, b, preferred_element_type=F32)


def _mm_nt(a, b):
    return lax.dot_general(a, b, (((1,), (1,)), ((), ())), preferred_element_type=F32)


def _mm_tn(a, b):
    return lax.dot_general(a, b, (((0,), (0,)), ((), ())), preferred_element_type=F32)


def _rms(x, g):
    rstd = lax.rsqrt(jnp.mean(x * x, axis=-1, keepdims=True) + EPS)
    xhat = x * rstd
    return xhat * g, xhat, rstd


def _rms_bwd(dy, xhat, rstd, g):
    gy = dy * g
    dx = rstd * (gy - xhat * jnp.mean(gy * xhat, axis=-1, keepdims=True))
    return dx, jnp.sum(dy * xhat, axis=0, keepdims=True)


def _gelu(z):
    t = jnp.tanh(0.7978845608028654 * (z + 0.044715 * z * z * z))
    return 0.5 * z * (1.0 + t), t


def _gelu_grad(z, t):
    return 0.5 * (1.0 + t) + 0.5 * z * (1.0 - t * t) * 0.7978845608028654 * (1.0 + 3.0 * 0.044715 * z * z)


def _softplus_neg(lam):
    x = -lam
    e = jnp.exp(-jnp.abs(x))
    u = 1.0 + e
    l1p = jnp.where(u == 1.0, e, jnp.log(u) * e / (u - 1.0))
    return jnp.maximum(x, 0.0) + l1p


def _expm1(x):
    p = x * (1.0 + x * (0.5 + x * (1.0 / 6.0 + x * (1.0 / 24.0 + x * (1.0 / 120.0)))))
    return jnp.where(jnp.abs(x) < 0.1, p, jnp.exp(x) - 1.0)


def _down(v, d):
    return pltpu.roll(v, d, 0)


def _up(v, d):
    return pltpu.roll(v, v.shape[0] - d, 0)


def _token_block(s):
    return 512 if s % 512 == 0 and s > 512 else 256


def _time_chunk(s):
    return 256 if s % 256 == 0 else s


def _fwd_in(x, ln1_g, w_in_t):
    s = x.shape[0]
    tb = _token_block(s)

    def body(x_ref, g_ref, w_ref, up_ref, ul_ref, ug_ref):
        n, _, _ = _rms(x_ref[...], g_ref[...])
        proj = _mm_nt(n.astype(BF16), w_ref[...])
        up_ref[...] = proj[:, :POOL_WIDTH]
        ul_ref[...] = proj[:, POOL_WIDTH:POOL_WIDTH + LRU_WIDTH]
        ug_ref[...] = proj[:, POOL_WIDTH + LRU_WIDTH:]

    out = jax.ShapeDtypeStruct((s, 512), F32)
    return pl.pallas_call(
        body, name="fwd_in", grid=(s // tb,),
        in_specs=[pl.BlockSpec((tb, D_MODEL), lambda i: (i, 0)),
                  pl.BlockSpec((1, D_MODEL), lambda i: (0, 0)),
                  pl.BlockSpec((IN_WIDTH, D_MODEL), lambda i: (0, 0))],
        out_specs=[pl.BlockSpec((tb, 512), lambda i: (i, 0))] * 3,
        out_shape=[out, out, out],
        compiler_params=_params(40, 1),
    )(x, ln1_g, w_in_t)


def _pool_denominator(t0, row, window):
    return jnp.minimum((t0 + row + 1).astype(F32), window)


def _causal_window(ext, onehot):
    s2 = ext + _down(ext, 1)
    s4 = s2 + _down(s2, 2)
    s8 = s4 + _down(s4, 4)
    s16 = s8 + _down(s8, 8)
    return (onehot[0] * s2 + onehot[1] * s4 + onehot[2] * s8 + onehot[3] * s16)[HALO:]


def _anticausal_window(ext, onehot, rows):
    s2 = ext + _up(ext, 1)
    s4 = s2 + _up(s2, 2)
    s8 = s4 + _up(s4, 4)
    s16 = s8 + _up(s8, 8)
    return (onehot[0] * s2 + onehot[1] * s4 + onehot[2] * s8 + onehot[3] * s16)[:rows]


def _conv_taps(ext):
    return [ext[HALO:], _down(ext, 1)[HALO:], _down(ext, 2)[HALO:], _down(ext, 3)[HALO:]]


def _conv(taps, cw, cb):
    return cw[3:4] * taps[0] + cw[2:3] * taps[1] + cw[1:2] * taps[2] + cw[0:1] * taps[3] + cb


def _lru_gates(xc, wa, ba, wi, bi, sp):
    xb = xc.astype(BF16)
    r = jax.nn.sigmoid(_mm(xb, wa) + ba)
    i = jax.nn.sigmoid(_mm(xb, wi) + bi)
    la = (-LRU_C) * r * sp
    a = jnp.exp(la)
    mult = jnp.sqrt(jnp.maximum(-_expm1(2.0 * la), 0.0))
    return xb, r, i, a, mult


def _scan_causal(a, b, h_prev, row, rows):
    d = 1
    while d < rows:
        head = row < d
        a_sh = jnp.where(head, 1.0, _down(a, d))
        b_sh = jnp.where(head, 0.0, _down(b, d))
        b = a * b_sh + b
        a = a * a_sh
        d *= 2
    return b + a * h_prev


def _scan_anticausal(a, b, l_next, row, rows):
    d = 1
    while d < rows:
        tail = row >= rows - d
        a_sh = jnp.where(tail, 1.0, _up(a, d))
        b_sh = jnp.where(tail, 0.0, _up(b, d))
        b = a * b_sh + b
        a = a * a_sh
        d *= 2
    return b + a * l_next


def _slab_scalars():
    slab = pl.program_id(0)
    onehot = [jnp.where(slab == k, 1.0, 0.0).astype(F32) for k in range(N_SLAB)]
    window = jnp.left_shift(jnp.int32(2), slab).astype(F32)
    return onehot, window


def _slab_specs(s):
    seq = pl.BlockSpec((s, SLAB), lambda k: (0, k))
    mat = pl.BlockSpec((1, SLAB, SLAB), lambda k: (k, 0, 0))
    vec = pl.BlockSpec((1, SLAB), lambda k: (0, k))
    taps = pl.BlockSpec((CONV_WIDTH, SLAB), lambda k: (0, k))
    return seq, mat, vec, taps


def _mixer_fwd(u_pool, u_lru, u_gate, pool_w, pool_scale, conv_w, conv_b, wa_bd, b_a, wi_bd, b_i, lam):
    s = u_pool.shape[0]
    tc = _time_chunk(s)
    n_chunks = s // tc

    def body(up_ref, ul_ref, ug_ref, pw_ref, ps_ref, cw_ref, cb_ref, wa_ref, ba_ref, wi_ref, bi_ref, lam_ref,
             yp_ref, h_ref, yl_ref):
        onehot, window = _slab_scalars()
        pw = pw_ref[0].astype(BF16)
        wa = wa_ref[0].astype(BF16)
        wi = wi_ref[0].astype(BF16)
        ps, cw, cb, ba, bi = ps_ref[...], cw_ref[...], cb_ref[...], ba_ref[...], bi_ref[...]
        sp = _softplus_neg(lam_ref[...])
        row = lax.broadcasted_iota(jnp.int32, (tc, SLAB), 0)

        def chunk(t0, ext_p, ext_l, h_prev):
            rows = pl.ds(t0, tc)
            d = _causal_window(ext_p, onehot) / _pool_denominator(t0, row, window) - ext_p[HALO:]
            yp_ref[rows, :] = _mm(d.astype(BF16), pw) * ps
            xc = _conv(_conv_taps(ext_l), cw, cb)
            _, _, i, a, mult = _lru_gates(xc, wa, ba, wi, bi, sp)
            h = _scan_causal(a, mult * (i * xc), h_prev, row, tc)
            h_ref[rows, :] = h
            yl_ref[rows, :] = h * _gelu(ug_ref[rows, :])[0]
            return h[tc - 1:tc, :]

        pad = jnp.zeros((HALO, SLAB), F32)
        h0 = chunk(0, jnp.concatenate([pad, up_ref[pl.ds(0, tc), :]], axis=0),
                   jnp.concatenate([pad, ul_ref[pl.ds(0, tc), :]], axis=0), jnp.zeros((1, SLAB), F32))

        def step(c, h_prev):
            t0 = pl.multiple_of(c * tc, tc)
            ext = pl.ds(pl.multiple_of(c * tc - HALO, HALO), tc + HALO)
            return chunk(t0, up_ref[ext, :], ul_ref[ext, :], h_prev)

        lax.fori_loop(1, n_chunks, step, h0)

    seq, mat, vec, taps = _slab_specs(s)
    out = jax.ShapeDtypeStruct((s, 512), F32)
    return pl.pallas_call(
        body, name="mixer_fwd", grid=(N_SLAB,),
        in_specs=[seq, seq, seq, mat, vec, taps, vec, mat, vec, mat, vec, vec],
        out_specs=[seq, seq, seq], out_shape=[out, out, out],
        compiler_params=_params(48, 1),
    )(u_pool, u_lru, u_gate, pool_w, pool_scale, conv_w, conv_b, wa_bd, b_a, wi_bd, b_i, lam)


def _fwd_out(x, y_pool, y_lru, gn_pool_g, gn_lru_g, w_out, ln2_g):
    s = x.shape[0]
    tb = _token_block(s)

    def body(x_ref, yp_ref, yl_ref, gp_ref, gl_ref, w_ref, g2_ref, h1_ref, n2_ref):
        mp, _, _ = _rms(yp_ref[...], gp_ref[...])
        ml, _, _ = _rms(yl_ref[...], gl_ref[...])
        h1 = x_ref[...] + _mm(mp.astype(BF16), w_ref[:POOL_WIDTH, :]) + _mm(ml.astype(BF16), w_ref[POOL_WIDTH:, :])
        h1_ref[...] = h1
        n2_ref[...] = _rms(h1, g2_ref[...])[0].astype(BF16)

    row = pl.BlockSpec((tb, D_MODEL), lambda i: (i, 0))
    half = pl.BlockSpec((tb, 512), lambda i: (i, 0))
    return pl.pallas_call(
        body, name="fwd_out", grid=(s // tb,),
        in_specs=[row, half, half, pl.BlockSpec((1, 512), lambda i: (0, 0)), pl.BlockSpec((1, 512), lambda i: (0, 0)),
                  pl.BlockSpec((D_MODEL, D_MODEL), lambda i: (0, 0)), pl.BlockSpec((1, D_MODEL), lambda i: (0, 0))],
        out_specs=[row, row],
        out_shape=[jax.ShapeDtypeStruct((s, D_MODEL), F32), jax.ShapeDtypeStruct((s, D_MODEL), BF16)],
        compiler_params=_params(40, 1),
    )(x, y_pool, y_lru, gn_pool_g, gn_lru_g, w_out, ln2_g)


def _ffn_fwd(h1, n2, target, lnf_g, w_gate, w_up, w_down):
    s = h1.shape[0]
    tb = 256
    n_ff = D_FF // FF_CHUNK

    def body(h1_ref, n2_ref, t_ref, gf_ref, wg_hbm, wu_hbm, wd_hbm,
             g_ref, u_ref, dh_ref, dhb_ref, dgf_ref, sq_ref, wg, wu, wd, sem):
        @pl.when(pl.program_id(0) == 0)
        def _():
            loads = [pltpu.make_async_copy(src, dst, sem.at[k])
                     for k, (src, dst) in enumerate(((wg_hbm, wg), (wu_hbm, wu), (wd_hbm, wd)))]
            for cp in loads:
                cp.start()
            for cp in loads:
                cp.wait()
            dgf_ref[...] = jnp.zeros_like(dgf_ref)
            sq_ref[...] = jnp.zeros_like(sq_ref)

        n2v = n2_ref[...]
        acc = jnp.zeros((tb, D_MODEL), F32)
        for c in range(n_ff):
            cols = slice(c * FF_CHUNK, (c + 1) * FF_CHUNK)
            g = _mm_nt(n2v, wg[cols, :])
            u = _mm_nt(n2v, wu[cols, :])
            g_ref[:, cols] = g.astype(BF16)
            u_ref[:, cols] = u.astype(BF16)
            act = g * jax.nn.sigmoid(g) * u
            acc = acc + _mm(act.astype(BF16), wd[cols, :])
        gf = gf_ref[...]
        y, xhat, rstd = _rms(h1_ref[...] + acc, gf)
        err = y - t_ref[...]
        sq_ref[...] += jnp.sum(err * err, axis=0, keepdims=True)
        dh2, dgf = _rms_bwd(err * (1.0 / D_MODEL), xhat, rstd, gf)
        dgf_ref[...] += dgf
        dh_ref[...] = dh2
        dhb_ref[...] = dh2.astype(BF16)

    row = pl.BlockSpec((tb, D_MODEL), lambda i: (i, 0))
    ff = pl.BlockSpec((tb, D_FF), lambda i: (i, 0))
    vec = pl.BlockSpec((1, D_MODEL), lambda i: (0, 0))
    anyspace = pl.BlockSpec(memory_space=pl.ANY)
    return pl.pallas_call(
        body, name="ffn_fwd", grid=(s // tb,),
        in_specs=[row, row, row, vec, anyspace, anyspace, anyspace],
        out_specs=[ff, ff, row, row, vec, vec],
        out_shape=[jax.ShapeDtypeStruct((s, D_FF), BF16), jax.ShapeDtypeStruct((s, D_FF), BF16),
                   jax.ShapeDtypeStruct((s, D_MODEL), F32), jax.ShapeDtypeStruct((s, D_MODEL), BF16),
                   jax.ShapeDtypeStruct((1, D_MODEL), F32), jax.ShapeDtypeStruct((1, D_MODEL), F32)],
        scratch_shapes=[pltpu.VMEM((D_FF, D_MODEL), BF16), pltpu.VMEM((D_FF, D_MODEL), BF16),
                        pltpu.VMEM((D_FF, D_MODEL), BF16), pltpu.SemaphoreType.DMA((3,))],
        compiler_params=_params(56, 1),
    )(h1, n2, target, lnf_g, w_gate, w_up, w_down)


def _ffn_bwd(n2, dh2b, g, u, w_gate_t, w_up_t, w_down):
    s = n2.shape[0]
    tb = min(1024, s)
    n_ff = D_FF // FF_CHUNK
    n_tb = s // tb

    def body(n2_ref, dh_ref, g_ref, u_ref, wg_ref, wu_ref, wd_ref, dwg_ref, dwu_ref, dwd_ref, dn2_ref,
             dn2_acc, acc_g, acc_u, acc_d):
        j = pl.program_id(0)
        t = pl.program_id(1)

        @pl.when(t == 0)
        def _():
            acc_g[...] = jnp.zeros_like(acc_g)
            acc_u[...] = jnp.zeros_like(acc_u)
            acc_d[...] = jnp.zeros_like(acc_d)

        n2v = n2_ref[...]
        dh = dh_ref[...]
        gv = g_ref[...].astype(F32)
        uv = u_ref[...].astype(F32)
        sg = jax.nn.sigmoid(gv)
        silu = gv * sg
        dact = _mm_nt(dh, wd_ref[...])
        dub = (dact * silu).astype(BF16)
        dgb = (dact * uv * (sg * (1.0 + gv * (1.0 - sg)))).astype(BF16)
        acc_d[...] += _mm_tn((silu * uv).astype(BF16), dh)
        acc_g[...] += _mm_tn(dgb, n2v)
        acc_u[...] += _mm_tn(dub, n2v)
        part = _mm(dgb, wg_ref[...]) + _mm(dub, wu_ref[...])
        rows = pl.ds(pl.multiple_of(t * tb, tb), tb)

        @pl.when(t == n_tb - 1)
        def _():
            dwg_ref[...] = acc_g[...].astype(BF16)
            dwu_ref[...] = acc_u[...].astype(BF16)
            dwd_ref[...] = acc_d[...].astype(BF16)

        @pl.when(j == 0)
        def _():
            dn2_acc[rows, :] = part

        @pl.when(jnp.logical_and(j > 0, j < n_ff - 1))
        def _():
            dn2_acc[rows, :] += part

        @pl.when(j == n_ff - 1)
        def _():
            dn2_ref[...] = dn2_acc[rows, :] + part

    row = pl.BlockSpec((tb, D_MODEL), lambda j, t: (t, 0))
    act = pl.BlockSpec((tb, FF_CHUNK), lambda j, t: (t, j))
    w_row = pl.BlockSpec((FF_CHUNK, D_MODEL), lambda j, t: (j, 0))
    last = pl.BlockSpec((tb, D_MODEL), lambda j, t: (jnp.where(j == n_ff - 1, t, 0), 0))
    grad = jax.ShapeDtypeStruct((D_FF, D_MODEL), BF16)
    chunk_acc = pltpu.VMEM((FF_CHUNK, D_MODEL), F32)
    return pl.pallas_call(
        body, name="ffn_bwd", grid=(n_ff, n_tb),
        in_specs=[row, row, act, act, w_row, w_row, w_row],
        out_specs=[w_row, w_row, w_row, last],
        out_shape=[grad, grad, grad, jax.ShapeDtypeStruct((s, D_MODEL), F32)],
        scratch_shapes=[pltpu.VMEM((s, D_MODEL), F32), chunk_acc, chunk_acc, chunk_acc],
        compiler_params=_params(56, 2),
    )(n2, dh2b, g, u, w_gate_t, w_up_t, w_down)


def _bwd_out(dn2, dh2, h1, y_pool, y_lru, gn_pool_g, gn_lru_g, w_out, ln2_g, after):
    s = h1.shape[0]
    tb = _token_block(s)

    def body(dn2_ref, dh2_ref, h1_ref, yp_ref, yl_ref, gp_ref, gl_ref, w_ref, g2_ref, _after,
             dh1_ref, dyp_ref, dyl_ref, dwb_ref, dg2_ref, dgp_ref, dgl_ref, dw_ref):
        @pl.when(pl.program_id(0) == 0)
        def _():
            dw_ref[...] = jnp.zeros_like(dw_ref)
            dg2_ref[...] = jnp.zeros_like(dg2_ref)
            dgp_ref[...] = jnp.zeros_like(dgp_ref)
            dgl_ref[...] = jnp.zeros_like(dgl_ref)

        g2 = g2_ref[...]
        _, xhat2, rstd2 = _rms(h1_ref[...], g2)
        dres, dg2 = _rms_bwd(dn2_ref[...], xhat2, rstd2, g2)
        dg2_ref[...] += dg2
        dh1 = dh2_ref[...] + dres
        dh1_ref[...] = dh1
        dh1b = dh1.astype(BF16)
        gp, gl = gp_ref[...], gl_ref[...]
        mp, xhat_p, rstd_p = _rms(yp_ref[...], gp)
        ml, xhat_l, rstd_l = _rms(yl_ref[...], gl)
        dw_ref[:POOL_WIDTH, :] += _mm_tn(mp.astype(BF16), dh1b)
        dw_ref[POOL_WIDTH:, :] += _mm_tn(ml.astype(BF16), dh1b)
        dyp, dgp = _rms_bwd(_mm_nt(dh1b, w_ref[:POOL_WIDTH, :]), xhat_p, rstd_p, gp)
        dyl, dgl = _rms_bwd(_mm_nt(dh1b, w_ref[POOL_WIDTH:, :]), xhat_l, rstd_l, gl)
        dyp_ref[...] = dyp
        dyl_ref[...] = dyl
        dgp_ref[...] += dgp
        dgl_ref[...] += dgl

        @pl.when(pl.program_id(0) == s // tb - 1)
        def _():
            dwb_ref[...] = dw_ref[...].astype(BF16)

    row = pl.BlockSpec((tb, D_MODEL), lambda i: (i, 0))
    half = pl.BlockSpec((tb, 512), lambda i: (i, 0))
    vec = pl.BlockSpec((1, D_MODEL), lambda i: (0, 0))
    hvec = pl.BlockSpec((1, 512), lambda i: (0, 0))
    mat = pl.BlockSpec((D_MODEL, D_MODEL), lambda i: (0, 0))
    return pl.pallas_call(
        body, name="bwd_out", grid=(s // tb,),
        in_specs=[row, row, row, half, half, hvec, hvec, mat, vec, pl.BlockSpec(memory_space=pl.ANY)],
        out_specs=[row, half, half, mat, vec, hvec, hvec],
        out_shape=[jax.ShapeDtypeStruct((s, D_MODEL), F32), jax.ShapeDtypeStruct((s, 512), F32),
                   jax.ShapeDtypeStruct((s, 512), F32), jax.ShapeDtypeStruct((D_MODEL, D_MODEL), BF16),
                   jax.ShapeDtypeStruct((1, D_MODEL), F32), jax.ShapeDtypeStruct((1, 512), F32),
                   jax.ShapeDtypeStruct((1, 512), F32)],
        scratch_shapes=[pltpu.VMEM((D_MODEL, D_MODEL), F32)],
        compiler_params=_params(48, 1),
    )(dn2, dh2, h1, y_pool, y_lru, gn_pool_g, gn_lru_g, w_out, ln2_g, after)


def _mixer_bwd(u_pool, u_lru, u_gate, h, dy_pool, dy_lru,
               pool_w, pool_scale, conv_w, conv_b, wa_bd, b_a, wi_bd, b_i, lam, after):
    s = u_pool.shape[0]
    tc = _time_chunk(s)
    n_chunks = s // tc

    def body(up_ref, ul_ref, ug_ref, h_ref, dyp_ref, dyl_ref,
             pw_ref, ps_ref, cw_ref, cb_ref, wa_ref, ba_ref, wi_ref, bi_ref, lam_ref, _after,
             dup_ref, dul_ref, dug_ref, dpw_ref, dps_ref, dcw_ref, dcb_ref, dwa_ref, dba_ref, dwi_ref, dbi_ref, dlam_ref):
        onehot, window = _slab_scalars()
        pw = pw_ref[0].astype(BF16)
        wa = wa_ref[0].astype(BF16)
        wi = wi_ref[0].astype(BF16)
        ps, cw, cb, ba, bi = ps_ref[...], cw_ref[...], cb_ref[...], ba_ref[...], bi_ref[...]
        lam_v = lam_ref[...]
        sp = _softplus_neg(lam_v)
        row = lax.broadcasted_iota(jnp.int32, (tc, SLAB), 0)
        for ref in (dpw_ref, dps_ref, dcw_ref, dcb_ref, dwa_ref, dba_ref, dwi_ref, dbi_ref, dlam_ref):
            ref[...] = jnp.zeros_like(ref)

        def chunk(t0, ext_p, ext_l, ext_h, carry):
            l_next, a_next, dxc_next, ddn_next = carry
            rows = pl.ds(t0, tc)
            taps = _conv_taps(ext_l)
            xc = _conv(taps, cw, cb)
            xb, r, i, a, mult = _lru_gates(xc, wa, ba, wi, bi, sp)
            hv = ext_h[HALO:]
            h_before = _down(ext_h, 1)[HALO:]
            ug = ug_ref[rows, :]
            dyl = dyl_ref[rows, :]
            gel, th = _gelu(ug)
            dug_ref[rows, :] = dyl * hv * _gelu_grad(ug, th)
            a_after = jnp.where(row == tc - 1, a_next, _up(a, 1))
            l = _scan_anticausal(a_after, dyl * gel, l_next, row, tc)
            dmult = l * (i * xc)
            di = l * mult * xc
            dxc = l * mult * i
            dla = (l * h_before) * a - jnp.where(mult > 0.0, dmult * (a * a) / mult, 0.0)
            dlam_ref[...] += jnp.sum(dla * r, axis=0, keepdims=True)
            dpa = (dla * ((-LRU_C) * sp)) * (r * (1.0 - r))
            dpi = di * (i * (1.0 - i))
            dpab = dpa.astype(BF16)
            dpib = dpi.astype(BF16)
            dwa_ref[0] += _mm_tn(xb, dpab)
            dwi_ref[0] += _mm_tn(xb, dpib)
            dba_ref[...] += jnp.sum(dpa, axis=0, keepdims=True)
            dbi_ref[...] += jnp.sum(dpi, axis=0, keepdims=True)
            dxc = dxc + _mm_nt(dpab, wa) + _mm_nt(dpib, wi)
            ext_d = jnp.concatenate([dxc, dxc_next], axis=0)
            dul_ref[rows, :] = (cw[3:4] * dxc + cw[2:3] * _up(ext_d, 1)[:tc]
                                + cw[1:2] * _up(ext_d, 2)[:tc] + cw[0:1] * _up(ext_d, 3)[:tc])
            for k in range(CONV_WIDTH):
                dcw_ref[k:k + 1, :] += jnp.sum(dxc * taps[CONV_WIDTH - 1 - k], axis=0, keepdims=True)
            dcb_ref[...] += jnp.sum(dxc, axis=0, keepdims=True)
            denom = _pool_denominator(t0, row, window)
            db = (_causal_window(ext_p, onehot) / denom - ext_p[HALO:]).astype(BF16)
            dyp = dyp_ref[rows, :]
            dps_ref[...] += jnp.sum(dyp * _mm(db, pw), axis=0, keepdims=True)
            dys = (dyp * ps).astype(BF16)
            dpw_ref[0] += _mm_tn(db, dys)
            dd = _mm_nt(dys, pw)
            ddn = dd / denom
            ext_q = jnp.concatenate([ddn, ddn_next], axis=0)
            dup_ref[rows, :] = _anticausal_window(ext_q, onehot, tc) - dd
            return l[0:1, :], a[0:1, :], dxc[0:8, :], ddn[0:HALO, :]

        def step(k, carry):
            c = n_chunks - 1 - k
            t0 = pl.multiple_of(c * tc, tc)
            ext = pl.ds(pl.multiple_of(c * tc - HALO, HALO), tc + HALO)
            return chunk(t0, up_ref[ext, :], ul_ref[ext, :], h_ref[ext, :], carry)

        carry = (jnp.zeros((1, SLAB), F32), jnp.zeros((1, SLAB), F32),
                 jnp.zeros((8, SLAB), F32), jnp.zeros((HALO, SLAB), F32))
        carry = lax.fori_loop(0, n_chunks - 1, step, carry)
        pad = jnp.zeros((HALO, SLAB), F32)
        first = pl.ds(0, tc)
        chunk(0, jnp.concatenate([pad, up_ref[first, :]], axis=0), jnp.concatenate([pad, ul_ref[first, :]], axis=0),
              jnp.concatenate([pad, h_ref[first, :]], axis=0), carry)
        dlam_ref[...] = dlam_ref[...] * (LRU_C * jax.nn.sigmoid(-lam_v))

    seq, mat, vec, taps = _slab_specs(s)
    full = jax.ShapeDtypeStruct((s, 512), F32)
    mats = jax.ShapeDtypeStruct((N_SLAB, SLAB, SLAB), F32)
    vecs = jax.ShapeDtypeStruct((1, 512), F32)
    return pl.pallas_call(
        body, name="mixer_bwd", grid=(N_SLAB,),
        in_specs=[seq] * 6 + [mat, vec, taps, vec, mat, vec, mat, vec, vec, pl.BlockSpec(memory_space=pl.ANY)],
        out_specs=[seq, seq, seq, mat, vec, taps, vec, mat, vec, mat, vec, vec],
        out_shape=[full, full, full, mats, vecs, jax.ShapeDtypeStruct((CONV_WIDTH, 512), F32), vecs,
                   mats, vecs, mats, vecs, vecs],
        compiler_params=_params(56, 1),
    )(u_pool, u_lru, u_gate, h, dy_pool, dy_lru, pool_w, pool_scale, conv_w, conv_b, wa_bd, b_a, wi_bd, b_i, lam,
      after)


def _bwd_in(x, dh1, du_pool, du_lru, du_gate, ln1_g, w_in_t):
    s = x.shape[0]
    tb = _token_block(s)

    def body(x_ref, dh1_ref, dup_ref, dul_ref, dug_ref, g_ref, w_ref, dx_ref, dwb_ref, dg_ref, dw_ref):
        @pl.when(pl.program_id(0) == 0)
        def _():
            dw_ref[...] = jnp.zeros_like(dw_ref)
            dg_ref[...] = jnp.zeros_like(dg_ref)

        g1 = g_ref[...]
        n, xhat, rstd = _rms(x_ref[...], g1)
        nb = n.astype(BF16)
        dn = jnp.zeros((tb, D_MODEL), F32)
        for k, ref in enumerate((dup_ref, dul_ref, dug_ref)):
            rows = slice(k * 512, (k + 1) * 512)
            db = ref[...].astype(BF16)
            dw_ref[rows, :] += _mm_tn(db, nb)
            dn = dn + _mm(db, w_ref[rows, :])
        dx, dg1 = _rms_bwd(dn, xhat, rstd, g1)
        dx_ref[...] = dh1_ref[...] + dx
        dg_ref[...] += dg1

        @pl.when(pl.program_id(0) == s // tb - 1)
        def _():
            dwb_ref[...] = dw_ref[...].astype(BF16)

    row = pl.BlockSpec((tb, D_MODEL), lambda i: (i, 0))
    half = pl.BlockSpec((tb, 512), lambda i: (i, 0))
    vec = pl.BlockSpec((1, D_MODEL), lambda i: (0, 0))
    mat = pl.BlockSpec((IN_WIDTH, D_MODEL), lambda i: (0, 0))
    return pl.pallas_call(
        body, name="bwd_in", grid=(s // tb,),
        in_specs=[row, row, half, half, half, vec, mat],
        out_specs=[row, mat, vec],
        out_shape=[jax.ShapeDtypeStruct((s, D_MODEL), F32), jax.ShapeDtypeStruct((IN_WIDTH, D_MODEL), BF16),
                   jax.ShapeDtypeStruct((1, D_MODEL), F32)],
        scratch_shapes=[pltpu.VMEM((IN_WIDTH, D_MODEL), F32)],
        compiler_params=_params(48, 1),
    )(x, dh1, du_pool, du_lru, du_gate, ln1_g, w_in_t)


def _mesh_position():
    x, y, c = lax.axis_index("x"), lax.axis_index("y"), lax.axis_index("c")
    return x, y, c, 4 * x + 2 * y + c


def _peer(x, y, c, p):
    px = 1 - x if p & 4 else x
    py = 1 - y if p & 2 else y
    pc = 1 - c if p & 1 else c
    return (px, py, pc), 4 * px + 2 * py + pc


HBM_SPEC = pl.BlockSpec(memory_space=pltpu.HBM)
SEM_SPEC = pl.BlockSpec(memory_space=pltpu.SEMAPHORE)
DATAFLOW = pltpu.SideEffectType.DATAFLOW_SIDE_EFFECTING


class Copy(NamedTuple):
    src: int
    src_at: Any
    dst: int
    dst_at: Any
    peer: int
    group: int
    slot: int
    rows: Any = None


SIBLING = (1,)
SAME_CORE = (2, 4, 6)
EVERYONE = tuple(range(1, N_DEV))


def _same(index):
    return index


def _chip(index):
    return jnp.right_shift(index, 1)


def _fan_out(srcs, lands, peers, group):
    return [Copy(s, None, d, _same, p, group, N_DEV * i + p) for i, (s, d) in enumerate(zip(srcs, lands)) for p in peers]


ICI_CHUNK_ROWS = 32


def _in_chunks(copies, n_rows):
    return [cp._replace(rows=(first, ICI_CHUNK_ROWS)) for cp in copies for first in range(0, n_rows, ICI_CHUNK_ROWS)]


def _numbered(copies, group=0):
    return [cp._replace(group=group, slot=i) for i, cp in enumerate(copies)]


def _relay(lands, peers):
    return [Copy(b, lambda s, q=q: jnp.bitwise_xor(s, q), b, lambda s, q=q: jnp.bitwise_xor(s, q), 1, 0, N_DEV * i + q)
            for i, b in enumerate(lands) for q in peers]


def _to_sibling(stacks, lands):
    return [Copy(s, lambda me, k=k: 2 * k + 1 - jnp.bitwise_and(me, 1), d, lambda me, k=k: k, 1, 0, 4 * i + k)
            for i, (s, d) in enumerate(zip(stacks, lands)) for k in range(N_DEV // 2)]


def _to_chips(sums, lands):
    return [Copy(s, lambda me, p=p: jnp.bitwise_xor(_chip(me), p // 2), d, _chip, p, 0, 4 * i + p // 2)
            for i, (s, d) in enumerate(zip(sums, lands)) for p in SAME_CORE]


def _comm_call(name, bufs, wait=None, local=(), start=None, after=None):
    nb = len(bufs)
    slots = list(start[1]) if start else []
    n_out_sem = 2 * len(slots)

    def body(*refs):
        b = refs[:nb]
        at = nb
        if wait:
            w_send, w_recv = refs[at], refs[at + 1]
            at += 2
        if after is not None:
            at += 1
        out_sems = refs[at:at + n_out_sem]
        token = refs[at + n_out_sem + nb]
        x, y, c, me = _mesh_position()

        def part(i, row_of, sender, rows=None):
            ref = b[i] if row_of is None else b[i].at[row_of(sender)]
            return ref if rows is None else ref.at[pl.ds(rows[0], rows[1])]

        if wait:
            for cp in wait[1]:
                peer, peer_index = _peer(x, y, c, cp.peer)
                arrival = pltpu.make_async_remote_copy(part(cp.src, cp.src_at, me, cp.rows),
                                                       part(cp.dst, cp.dst_at, peer_index, cp.rows),
                                                       w_send.at[cp.slot], w_recv.at[cp.slot],
                                                       device_id=peer, device_id_type=MESH)
                arrival.wait_send()
                arrival.wait_recv()
        if local:
            local_sems = refs[-1]
            copies = [pltpu.make_async_copy(part(sb, sf, me), part(db, df, me), local_sems.at[k])
                      for k, (sb, sf, db, df) in enumerate(local)]
            for cp in copies:
                cp.start()
            for cp in copies:
                cp.wait()
        if start:
            for cp in start[0]:
                peer, _ = _peer(x, y, c, cp.peer)
                pltpu.make_async_remote_copy(part(cp.src, cp.src_at, me, cp.rows), part(cp.dst, cp.dst_at, me, cp.rows),
                                             out_sems[2 * cp.group].at[cp.slot], out_sems[2 * cp.group + 1].at[cp.slot],
                                             device_id=peer, device_id_type=MESH).start()
        token[...] = jnp.zeros_like(token)

    sem_shapes = []
    for n_slots in slots:
        sem_shapes += [pltpu.SemaphoreType.DMA((n_slots,))] * 2
    operands = [pltpu.with_memory_space_constraint(a, pltpu.HBM) for a in bufs]
    in_specs = [HBM_SPEC] * nb
    if wait:
        operands += list(wait[0])
        in_specs += [SEM_SPEC, SEM_SPEC]
    if after is not None:
        operands.append(after)
        in_specs.append(pl.BlockSpec(memory_space=pl.ANY))
    outs = pl.pallas_call(
        body, name=name, in_specs=in_specs,
        out_specs=[SEM_SPEC] * n_out_sem + [HBM_SPEC] * nb + [pl.BlockSpec(memory_space=pltpu.VMEM)],
        out_shape=sem_shapes + [pltpu.HBM(a.shape, a.dtype) for a in bufs] + [jax.ShapeDtypeStruct((8, SLAB), F32)],
        input_output_aliases={i: n_out_sem + i for i in range(nb)},
        scratch_shapes=[pltpu.SemaphoreType.DMA((len(local),))] if local else [],
        compiler_params=pltpu.CompilerParams(has_side_effects=DATAFLOW),
    )(*operands)
    sems = [(outs[2 * k], outs[2 * k + 1]) for k in range(len(slots))]
    return sems, list(outs[n_out_sem:n_out_sem + nb]), outs[-1]


def _pair_sum(stacks, lands, place):
    n = len(stacks)

    def body(place_ref, *refs):
        k = pl.program_id(0)
        for m in range(n):
            mine, theirs, out, land = refs[m], refs[n + m], refs[2 * n + m], refs[3 * n + m]
            total = (mine[0, 0].astype(F32) + theirs[0].astype(F32)).astype(out.dtype)
            out[0] = total

            @pl.when(k == place_ref[1])
            def _():
                land[0] = total

    in_specs = [pl.BlockSpec((1, 1) + a.shape[1:], lambda k, place_ref: (k, place_ref[0], 0, 0)) for a in stacks]
    in_specs += [pl.BlockSpec((1,) + a.shape[1:], lambda k, place_ref: (k, 0, 0)) for a in lands]
    out_specs = [pl.BlockSpec((1,) + a.shape[1:], lambda k, place_ref: (k, 0, 0)) for a in lands]
    out_specs += [pl.BlockSpec((1,) + a.shape[1:], lambda k, place_ref: (place_ref[1], 0, 0)) for a in lands]
    outs = pl.pallas_call(
        body, name="pair_sum_" + "_".join(str(a.shape[1]) for a in stacks),
        grid_spec=pltpu.PrefetchScalarGridSpec(num_scalar_prefetch=1, grid=(N_DEV // 2,), in_specs=in_specs,
                                               out_specs=out_specs),
        out_shape=[jax.ShapeDtypeStruct(a.shape, a.dtype) for a in lands] * 2,
        compiler_params=_params(40, 1),
    )(place, *[a.reshape((N_DEV // 2, 2) + a.shape[1:]) for a in stacks], *lands)
    return list(outs[:n]), list(outs[n:])


def _pair_add(a, b, place):
    def body(place_ref, a_ref, b_ref, out, land):
        del place_ref
        out[...] = a_ref[...] + b_ref[...]
        land[0] = out[...]

    whole = pl.BlockSpec(a.shape, lambda i, place_ref: (0, 0))
    return pl.pallas_call(
        body, name="pair_add",
        grid_spec=pltpu.PrefetchScalarGridSpec(
            num_scalar_prefetch=1, grid=(1,), in_specs=[whole, whole],
            out_specs=[whole, pl.BlockSpec((1,) + a.shape, lambda i, place_ref: (place_ref[1], 0, 0))]),
        out_shape=[jax.ShapeDtypeStruct(a.shape, a.dtype), jax.ShapeDtypeStruct((N_DEV // 2,) + a.shape, a.dtype)],
    )(place, a, b)


def _reduce_adam(parts, w, m, v, name):
    rows, cols = w.shape
    n_parts = parts.shape[0]
    rb = rows
    for cand in (256, 176, 128):
        if rows % cand == 0 and rows > cand:
            rb = cand
            break

    def body(p_ref, w_ref, m_ref, v_ref, g_out, d_out, m_out, v_out):
        g = p_ref[0].astype(F32)
        for j in range(1, n_parts):
            g = g + p_ref[j].astype(F32)
        g_out[...] = g
        d_out[...], m_out[...], v_out[...] = _adam(g, w_ref[...], m_ref[...], v_ref[...])

    blk = pl.BlockSpec((rb, cols), lambda i: (i, 0))
    out = jax.ShapeDtypeStruct((rows, cols), F32)
    return pl.pallas_call(
        body, name=name, grid=(rows // rb,),
        in_specs=[pl.BlockSpec((n_parts, rb, cols), lambda i: (0, i, 0)), blk, blk, blk],
        out_specs=[blk] * 4, out_shape=[out] * 4,
        compiler_params=_params(40, 1),
    )(parts, w, m, v)


def _cols_from_stack(stack):
    n, r, c = stack.shape
    return jnp.transpose(stack, (1, 0, 2)).reshape(r, n * c)


def _block_diag(w):
    z = jnp.zeros((N_SLAB, 64, 64), w.dtype)
    pairs = w.reshape(N_SLAB, 2, 64, 64)
    top = jnp.concatenate([pairs[:, 0], z], axis=2)
    bottom = jnp.concatenate([z, pairs[:, 1]], axis=2)
    return jnp.concatenate([top, bottom], axis=1)


def _diag_blocks(w):
    return jnp.stack([w[:, :64, :64], w[:, 64:, 64:]], axis=1).reshape(8, 64, 64)


def _adam(g, w, m, v):
    m_new = ADAM_B1 * m + (1.0 - ADAM_B1) * g
    v_new = ADAM_B2 * v + (1.0 - ADAM_B2) * (g * g)
    m_hat = m_new / (1.0 - ADAM_B1 ** ADAM_STEP)
    v_hat = v_new / (1.0 - ADAM_B2 ** ADAM_STEP)
    return (-ADAM_LR) * (m_hat / (jnp.sqrt(v_hat) + ADAM_EPS) + ADAM_WD * w), m_new, v_new


WIDE = ("ln1_g", "ln2_g", "lnf_g")
HALF = ("pool_scale", "conv_b", "b_a", "b_i", "lam", "gn_pool_g", "gn_lru_g")
VECTORS = [(k, D_MODEL) for k in WIDE] + [(k, 512) for k in HALF]
VECTOR_ROWS = sum(width // SLAB for _, width in VECTORS)
MATRIX_AT = -(-VECTOR_ROWS // 8) * 8
MATRIX_ROWS = N_SLAB * SLAB
N_MATRIX = 3
LOSS_ROW = MATRIX_AT + N_MATRIX * MATRIX_ROWS
CONV_AT = LOSS_ROW + 8
CONV_LANES = LRU_WIDTH // SLAB
PACK_ROWS = -(-(CONV_AT + CONV_WIDTH * CONV_LANES) // ICI_CHUNK_ROWS) * ICI_CHUNK_ROWS


def _pack_small(vectors, matrices, conv, sq):
    n_vec = len(vectors)

    def body(*refs):
        vec, mat, cw_ref, sq_ref, out = refs[:n_vec], refs[n_vec:n_vec + N_MATRIX], refs[-3], refs[-2], refs[-1]
        out[...] = jnp.zeros_like(out)
        row = 0
        for ref, (_, width) in zip(vec, VECTORS):
            for k in range(width // SLAB):
                out[row:row + 1, :] = ref[:, k * SLAB:(k + 1) * SLAB]
                row += 1
        for i, ref in enumerate(mat):
            for s in range(N_SLAB):
                at = MATRIX_AT + i * MATRIX_ROWS + s * SLAB
                out[at:at + SLAB, :] = ref[s]
        for tap in range(CONV_WIDTH):
            for k in range(CONV_LANES):
                at = CONV_AT + tap * CONV_LANES + k
                out[at:at + 1, :] = cw_ref[tap:tap + 1, k * SLAB:(k + 1) * SLAB]
        total = sq_ref[:, 0:SLAB]
        for k in range(1, D_MODEL // SLAB):
            total = total + sq_ref[:, k * SLAB:(k + 1) * SLAB]
        out[LOSS_ROW:LOSS_ROW + 1, :] = total

    return pl.pallas_call(
        body, name="pack_small", out_shape=jax.ShapeDtypeStruct((PACK_ROWS, SLAB), F32),
    )(*vectors, *matrices, conv, sq)


def _small_reduce_adam(parts, vec_w, vec_m, vec_v, pool_wmv):
    n_vec = len(VECTORS)
    n_parts = parts.shape[0]

    def body(*refs):
        p_ref = refs[0]
        w_refs, m_refs, v_refs = (refs[1 + k * n_vec:1 + (k + 1) * n_vec] for k in range(3))
        pw_w, pw_m, pw_v = refs[1 + 3 * n_vec:4 + 3 * n_vec]
        outs = refs[4 + 3 * n_vec:-1]
        total = refs[-1]
        total[...] = p_ref[0]
        for j in range(1, n_parts):
            total[...] += p_ref[j]
        row = 0
        for i, (_, width) in enumerate(VECTORS):
            n_rows = width // SLAB
            g = jnp.concatenate([total[row + k:row + k + 1, :] for k in range(n_rows)], axis=1)
            row += n_rows
            d, m_new, v_new = _adam(g, w_refs[i][...], m_refs[i][...], v_refs[i][...])
            for ref, val in zip(outs[4 * i:4 * i + 4], (g, d, m_new, v_new)):
                ref[...] = val
        tail = outs[4 * n_vec:]
        g = total[MATRIX_AT:MATRIX_AT + MATRIX_ROWS, :]
        d, m_new, v_new = _adam(g, pw_w[...], pw_m[...], pw_v[...])
        for ref, val in zip(tail[0:4], (g, d, m_new, v_new)):
            ref[...] = val
        tail[4][...] = total[MATRIX_AT + MATRIX_ROWS:MATRIX_AT + 2 * MATRIX_ROWS, :]
        tail[5][...] = total[MATRIX_AT + 2 * MATRIX_ROWS:MATRIX_AT + 3 * MATRIX_ROWS, :]
        for tap in range(CONV_WIDTH):
            at = CONV_AT + tap * CONV_LANES
            tail[6][tap:tap + 1, :] = jnp.concatenate([total[at + k:at + k + 1, :] for k in range(CONV_LANES)], axis=1)
        tail[7][...] = (0.5 / D_MODEL) * jnp.sum(total[LOSS_ROW:LOSS_ROW + 1, :], axis=1, keepdims=True)

    out_shape = []
    for _, width in VECTORS:
        out_shape += [jax.ShapeDtypeStruct((1, width), F32)] * 4
    out_shape += [jax.ShapeDtypeStruct((MATRIX_ROWS, SLAB), F32)] * 6
    out_shape += [jax.ShapeDtypeStruct((CONV_WIDTH, LRU_WIDTH), F32), jax.ShapeDtypeStruct((1, 1), F32)]
    outs = pl.pallas_call(
        body, name="adam_small", out_shape=out_shape,
        scratch_shapes=[pltpu.VMEM((PACK_ROWS, SLAB), F32)],
        compiler_params=_params(40),
    )(parts, *vec_w, *vec_m, *vec_v, *pool_wmv)
    vec_out = [tuple(outs[4 * i:4 * i + 4]) for i in range(n_vec)]
    tail = outs[4 * n_vec:]
    return vec_out, tuple(tail[0:4]), tail[4], tail[5], tail[6], tail[7]


def _plain_adam(grads, ws, ms, vs):
    n = len(grads)

    def body(*refs):
        ins, outs = refs[:4 * n], refs[4 * n:]
        for i in range(n):
            d, m_new, v_new = _adam(ins[i][...], ins[n + i][...], ins[2 * n + i][...], ins[3 * n + i][...])
            for ref, val in zip(outs[3 * i:3 * i + 3], (d, m_new, v_new)):
                ref[...] = val

    out_shape = []
    for g in grads:
        out_shape += [jax.ShapeDtypeStruct(g.shape, F32)] * 3
    outs = pl.pallas_call(body, name="adam_plain", out_shape=out_shape)(*grads, *ws, *ms, *vs)
    return [tuple(outs[3 * i:3 * i + 3]) for i in range(n)]


def kernel(x, ln1_g, w_in, pool_w, pool_scale, conv_w, conv_b, w_a, b_a, w_i, b_i, lam, gn_pool_g, gn_lru_g, w_out, ln2_g, w_ffn_gate, w_ffn_up, w_ffn_down, lnf_g, loss_target, m_ln1_g, m_w_in, m_pool_w, m_pool_scale, m_conv_w, m_conv_b, m_w_a, m_b_a, m_w_i, m_b_i, m_lam, m_gn_pool_g, m_gn_lru_g, m_w_out, m_ln2_g, m_w_ffn_gate, m_w_ffn_up, m_w_ffn_down, m_lnf_g, v_ln1_g, v_w_in, v_pool_w, v_pool_scale, v_conv_w, v_conv_b, v_w_a, v_b_a, v_w_i, v_b_i, v_lam, v_gn_pool_g, v_gn_lru_g, v_w_out, v_ln2_g, v_w_ffn_gate, v_w_ffn_up, v_w_ffn_down, v_lnf_g):
    weights = dict(ln1_g=ln1_g, w_in=w_in, pool_w=pool_w, pool_scale=pool_scale, conv_w=conv_w, conv_b=conv_b,
                   w_a=w_a, b_a=b_a, w_i=w_i, b_i=b_i, lam=lam, gn_pool_g=gn_pool_g, gn_lru_g=gn_lru_g,
                   w_out=w_out, ln2_g=ln2_g, w_ffn_gate=w_ffn_gate, w_ffn_up=w_ffn_up, w_ffn_down=w_ffn_down,
                   lnf_g=lnf_g)
    mom1 = dict(ln1_g=m_ln1_g, w_in=m_w_in, pool_w=m_pool_w, pool_scale=m_pool_scale, conv_w=m_conv_w,
                conv_b=m_conv_b, w_a=m_w_a, b_a=m_b_a, w_i=m_w_i, b_i=m_b_i, lam=m_lam, gn_pool_g=m_gn_pool_g,
                gn_lru_g=m_gn_lru_g, w_out=m_w_out, ln2_g=m_ln2_g, w_ffn_gate=m_w_ffn_gate,
                w_ffn_up=m_w_ffn_up, w_ffn_down=m_w_ffn_down, lnf_g=m_lnf_g)
    mom2 = dict(ln1_g=v_ln1_g, w_in=v_w_in, pool_w=v_pool_w, pool_scale=v_pool_scale, conv_w=v_conv_w,
                conv_b=v_conv_b, w_a=v_w_a, b_a=v_b_a, w_i=v_w_i, b_i=v_b_i, lam=v_lam, gn_pool_g=v_gn_pool_g,
                gn_lru_g=v_gn_lru_g, w_out=v_w_out, ln2_g=v_ln2_g, w_ffn_gate=v_w_ffn_gate,
                w_ffn_up=v_w_ffn_up, w_ffn_down=v_w_ffn_down, lnf_g=v_lnf_g)

    xs = x[0]
    target = loss_target[0]

    shard = dict(w_in=lambda a: a[0].T, w_ffn_gate=lambda a: a[0].T, w_ffn_up=lambda a: a[0].T,
                 w_out=lambda a: a[0], w_ffn_down=lambda a: a[0], conv_w=lambda a: a[0])
    unshard = dict(w_in=lambda a: a.T[None], w_ffn_gate=lambda a: a.T[None], w_ffn_up=lambda a: a.T[None],
                   w_out=lambda a: a[None], w_ffn_down=lambda a: a[None], conv_w=lambda a: a[None])

    gathered = ("w_in", "conv_w", "w_out", "w_ffn_gate", "w_ffn_up", "w_ffn_down")
    sources = [shard[k](weights[k]) if k == "conv_w" else shard[k](weights[k]).astype(BF16) for k in gathered]
    my_index = 4 * lax.axis_index("x") + 2 * lax.axis_index("y") + lax.axis_index("c")
    lands = [lax.dynamic_update_index_in_dim(lax.empty((N_DEV,) + a.shape, a.dtype), a, my_index, 0) for a in sources]

    def in_copies(srcs, dsts, group=0):
        w_in_rows = IN_WIDTH // N_DEV
        return _numbered(_in_chunks(_fan_out(srcs[:1], dsts[:1], EVERYONE, 0), w_in_rows)
                         + _fan_out(srcs[1:], dsts[1:], EVERYONE, 0), group)

    def out_copies(src, dst, group=0):
        return _numbered(_in_chunks(_fan_out((src,), (dst,), EVERYONE, 0), D_MODEL // N_DEV), group)

    def ffn_copies(srcs, dsts, group=0):
        return _numbered(_in_chunks(_fan_out(srcs, dsts, SAME_CORE, 0), D_FF // N_DEV)
                         + _fan_out(srcs, dsts, SIBLING, 0), group)

    start = in_copies((0, 1), (6, 7), 0) + out_copies(2, 8, 1) + ffn_copies((3, 4, 5), (9, 10, 11), 2)
    slots = [sum(cp.group == g for cp in start) for g in range(3)]
    sems, bufs, _ = _comm_call("gather_start", sources + lands, start=(start, slots))
    sources, lands = bufs[:6], bufs[6:]
    _, bufs, _ = _comm_call("gather_wait_in", sources[0:2] + lands[0:2], wait=(sems[0], in_copies((0, 1), (2, 3))))
    w_in_f = bufs[2].reshape(IN_WIDTH, D_MODEL)
    conv_w_f = _cols_from_stack(bufs[3])

    wa_bd = _block_diag(w_a[0])
    wi_bd = _block_diag(w_i[0])
    lnf_row = lnf_g.reshape(1, D_MODEL)

    u_pool, u_lru, u_gate = _fwd_in(xs, ln1_g, w_in_f)
    y_pool, h, y_lru = _mixer_fwd(u_pool, u_lru, u_gate, pool_w[0], pool_scale, conv_w_f, conv_b,
                                  wa_bd, b_a, wi_bd, b_i, lam)
    _, bufs, _ = _comm_call("gather_wait_out", [sources[2], lands[2]], wait=(sems[1], out_copies(0, 1)), after=y_pool)
    w_out_f = bufs[1].reshape(D_MODEL, D_MODEL)
    h1, n2 = _fwd_out(xs, y_pool, y_lru, gn_pool_g, gn_lru_g, w_out_f, ln2_g)
    relay = _numbered(_relay((3, 4, 5), SAME_CORE))
    relay_sems, bufs, _ = _comm_call("gather_relay_ffn", sources[3:6] + lands[3:6],
                                     wait=(sems[2], ffn_copies((0, 1, 2), (3, 4, 5))),
                                     start=(relay, (len(relay),)), after=n2)
    _, bufs, _ = _comm_call("gather_wait_ffn", bufs[3:6],
                            wait=(relay_sems[0], _numbered(_relay((0, 1, 2), SAME_CORE))))
    w_gate_f = bufs[0].reshape(D_FF, D_MODEL)
    w_up_f = bufs[1].reshape(D_FF, D_MODEL)
    w_down_f = bufs[2].reshape(D_FF, D_MODEL)
    g_act, u_act, dh2, dh2b, d_lnf, sq = _ffn_fwd(h1, n2, target, lnf_row, w_gate_f, w_up_f, w_down_f)

    place = jnp.stack([lax.axis_index("c"), 2 * lax.axis_index("x") + lax.axis_index("y")]).astype(jnp.int32)

    def pair_step(tag, stacks, extra, after):
        n = len(stacks)
        lands = [lax.empty((N_DEV // 2,) + a.shape[1:], a.dtype) for a in stacks]
        copies = _to_sibling(range(n), range(n, 2 * n))
        if extra is not None:
            copies.append(Copy(2 * n, None, 2 * n + 1, None, 1, 0, 0))
            lands_extra = [extra, lax.empty(extra.shape, extra.dtype)]
        else:
            lands_extra = []
        copies = _numbered(copies)
        sem, bufs, token = _comm_call(tag + "_pair_start", list(stacks) + lands + lands_extra,
                                      start=(copies, (len(copies),)), after=after)
        return sem[0], bufs, copies, token

    def pair_finish(tag, sem, bufs, copies, n, after):
        _, bufs, _ = _comm_call(tag + "_pair_wait", bufs, wait=(sem, copies), after=after)
        sums, lands = _pair_sum(bufs[:n], bufs[n:2 * n], place)
        if len(bufs) > 2 * n:
            total, land = _pair_add(bufs[2 * n], bufs[2 * n + 1], place)
            sums.append(total)
            lands.append(land)
        return sums, lands

    def chip_step(tag, sums, lands, n):
        copies = []
        for i in range(n):
            copies += _in_chunks(_to_chips((i,), (len(sums) + i,)), sums[i].shape[1])
        if len(sums) > n:
            copies += _in_chunks([Copy(n, None, 2 * n + 1, _chip, p, 0, 0) for p in SAME_CORE], sums[n].shape[0])
        copies = _numbered(copies)
        sem, bufs, token = _comm_call(tag + "_chip_start", list(sums) + list(lands), start=(copies, (len(copies),)))
        return sem[0], bufs, copies, token

    def chip_finish(tag, sem, bufs, copies, after):
        _, bufs, _ = _comm_call(tag + "_chip_wait", bufs, wait=(sem, copies), after=after)
        return bufs[len(bufs) // 2:]

    d_gate, d_up, d_down, dn2 = _ffn_bwd(n2, dh2b, g_act, u_act, w_gate_f, w_up_f, w_down_f)
    ffn_stacks = [d.reshape(N_DEV, D_FF // N_DEV, D_MODEL) for d in (d_gate, d_up, d_down)]
    sem, bufs, copies, token = pair_step("ffn", ffn_stacks, None, None)
    dh1, dy_pool, dy_lru, d_out, d_ln2, d_gnp, d_gnl = _bwd_out(dn2, dh2, h1, y_pool, y_lru, gn_pool_g, gn_lru_g,
                                                                 w_out_f, ln2_g, token)
    ffn_sums, ffn_lands = pair_finish("ffn", sem, bufs, copies, 3, dh1)
    ffn_sem, ffn_bufs, ffn_copies, token = chip_step("ffn", ffn_sums, ffn_lands, 3)
    (du_pool, du_lru, du_gate, d_pw, d_ps, d_cw, d_cb, d_wa, d_ba, d_wi, d_bi, d_lam) = _mixer_bwd(
        u_pool, u_lru, u_gate, h, dy_pool, dy_lru, pool_w[0], pool_scale, conv_w_f, conv_b,
        wa_bd, b_a, wi_bd, b_i, lam, token)
    grad_x, d_in, d_ln1 = _bwd_in(xs, dh1, du_pool, du_lru, du_gate, ln1_g, w_in_f)

    vec_grads = dict(ln1_g=d_ln1, ln2_g=d_ln2, lnf_g=d_lnf, pool_scale=d_ps, conv_b=d_cb, b_a=d_ba, b_i=d_bi,
                     lam=d_lam, gn_pool_g=d_gnp, gn_lru_g=d_gnl)
    packed = _pack_small([vec_grads[k] for k, _ in VECTORS], [d_pw, d_wa, d_wi], d_cw, sq)
    rest_stacks = [d_in.reshape(N_DEV, IN_WIDTH // N_DEV, D_MODEL), d_out.reshape(N_DEV, D_MODEL // N_DEV, D_MODEL)]
    sem, bufs, copies, _ = pair_step("rest", rest_stacks, packed, None)
    rest_sums, rest_lands = pair_finish("rest", sem, bufs, copies, 2, None)
    rest_sem, rest_bufs, rest_copies, token = chip_step("rest", rest_sums, rest_lands, 2)

    results = {}
    r_gate, r_up, r_down = chip_finish("ffn", ffn_sem, ffn_bufs, ffn_copies, token)
    for name, parts in (("w_ffn_gate", r_gate), ("w_ffn_up", r_up), ("w_ffn_down", r_down)):
        outs = _reduce_adam(parts, shard[name](weights[name]), shard[name](mom1[name]), shard[name](mom2[name]),
                            "adam_" + name)
        results[name] = tuple(unshard[name](o) for o in outs)
    r_in, r_out, r_small = chip_finish("rest", rest_sem, rest_bufs, rest_copies, outs[0])
    for name, parts in (("w_in", r_in), ("w_out", r_out)):
        outs = _reduce_adam(parts, shard[name](weights[name]), shard[name](mom1[name]), shard[name](mom2[name]),
                            "adam_" + name)
        results[name] = tuple(unshard[name](o) for o in outs)

    def as_row(a, width):
        return a.reshape(1, width)

    def as_matrix(a):
        return a.reshape(MATRIX_ROWS, SLAB)

    def as_heads(a):
        return a.reshape(MATRIX_ROWS, 64)

    vec_out, pool_out, g_wa_bd, g_wi_bd, g_conv, loss_11 = _small_reduce_adam(
        r_small, [as_row(weights[k], w) for k, w in VECTORS], [as_row(mom1[k], w) for k, w in VECTORS],
        [as_row(mom2[k], w) for k, w in VECTORS], [as_matrix(t["pool_w"]) for t in (weights, mom1, mom2)])
    for (k, _), outs in zip(VECTORS, vec_out):
        results[k] = tuple(o.reshape(weights[k].shape) for o in outs)
    results["pool_w"] = tuple(o.reshape(pool_w.shape) for o in pool_out)
    my_columns = conv_w.shape[-1]
    my_index = 4 * lax.axis_index("x") + 2 * lax.axis_index("y") + lax.axis_index("c")
    plain_names = ("w_a", "w_i", "conv_w")
    plain_grads = [as_heads(_diag_blocks(g.reshape(N_SLAB, SLAB, SLAB))) for g in (g_wa_bd, g_wi_bd)]
    plain_grads.append(lax.dynamic_slice_in_dim(g_conv, my_index * my_columns, my_columns, axis=1))
    views = (as_heads, as_heads, lambda a: a[0])
    plain_out = _plain_adam(plain_grads, *[[view(t[k]) for k, view in zip(plain_names, views)]
                                           for t in (weights, mom1, mom2)])
    for k, g, outs in zip(plain_names, plain_grads, plain_out):
        results[k] = tuple(o.reshape(weights[k].shape) for o in (g,) + outs)
    loss = loss_11[0, 0]

    order = ["ln1_g", "w_in", "pool_w", "pool_scale", "conv_w", "conv_b", "w_a", "b_a", "w_i", "b_i", "lam",
             "gn_pool_g", "gn_lru_g", "w_out", "ln2_g", "w_ffn_gate", "w_ffn_up", "w_ffn_down", "lnf_g"]
    return (loss, grad_x[None],
            *[results[k][0] for k in order], *[results[k][1] for k in order],
            *[results[k][2] for k in order], *[results[k][3] for k in order])
```

```python
from typing import Any, NamedTuple

import jax
import jax.numpy as jnp
from jax import lax
from jax.experimental import pallas as pl
from jax.experimental.pallas import tpu as pltpu

F32 = jnp.float32
BF16 = jnp.bfloat16

N_DEV = 8
D_MODEL = 1024
POOL_WIDTH = 512
LRU_WIDTH = 512
IN_WIDTH = 1536
D_FF = 2816
N_SLAB = 4
SLAB = 128
CONV_WIDTH = 4
LRU_C = 8.0
EPS = 1e-6
HALO = 16
FF_CHUNK = 256

ADAM_LR = 0.001
ADAM_B1 = 0.9
ADAM_B2 = 0.999
ADAM_EPS = 1e-08
ADAM_WD = 0.01
ADAM_STEP = 10

MIB = 1 << 20
MESH = pl.DeviceIdType.MESH


def _params(vmem_mib, n_axes=0):
    sem = ("arbitrary",) * n_axes if n_axes else None
    return pltpu.CompilerParams(dimension_semantics=sem, vmem_limit_bytes=vmem_mib * MIB)


def _mm(a, b):
    return jnp.dot(a, b, preferred_element_type=F32)


def _mm_nt(a, b):
    return lax.dot_general(a, b, (((1,), (1,)), ((), ())), preferred_element_type=F32)


def _mm_tn(a, b):
    return lax.dot_general(a, b, (((0,), (0,)), ((), ())), preferred_element_type=F32)


def _rms(x, g):
    rstd = lax.rsqrt(jnp.mean(x * x, axis=-1, keepdims=True) + EPS)
    xhat = x * rstd
    return xhat * g, xhat, rstd


def _rms_bwd(dy, xhat, rstd, g):
    gy = dy * g
    dx = rstd * (gy - xhat * jnp.mean(gy * xhat, axis=-1, keepdims=True))
    return dx, jnp.sum(dy * xhat, axis=0, keepdims=True)


def _gelu(z):
    t = jnp.tanh(0.7978845608028654 * (z + 0.044715 * z * z * z))
    return 0.5 * z * (1.0 + t), t


def _gelu_grad(z, t):
    return 0.5 * (1.0 + t) + 0.5 * z * (1.0 - t * t) * 0.7978845608028654 * (1.0 + 3.0 * 0.044715 * z * z)


def _softplus_neg(lam):
    x = -lam
    e = jnp.exp(-jnp.abs(x))
    u = 1.0 + e
    l1p = jnp.where(u == 1.0, e, jnp.log(u) * e / (u - 1.0))
    return jnp.maximum(x, 0.0) + l1p


def _expm1(x):
    p = x * (1.0 + x * (0.5 + x * (1.0 / 6.0 + x * (1.0 / 24.0 + x * (1.0 / 120.0)))))
    return jnp.where(jnp.abs(x) < 0.1, p, jnp.exp(x) - 1.0)


def _down(v, d):
    return pltpu.roll(v, d, 0)


def _up(v, d):
    return pltpu.roll(v, v.shape[0] - d, 0)


def _token_block(s):
    return 512 if s % 512 == 0 and s > 512 else 256


def _time_chunk(s):
    return 256 if s % 256 == 0 else s


def _fwd_in(x, ln1_g, w_in_t):
    s = x.shape[0]
    tb = _token_block(s)

    def body(x_ref, g_ref, w_ref, up_ref, ul_ref, ug_ref):
        n, _, _ = _rms(x_ref[...], g_ref[...])
        proj = _mm_nt(n.astype(BF16), w_ref[...])
        up_ref[...] = proj[:, :POOL_WIDTH]
        ul_ref[...] = proj[:, POOL_WIDTH:POOL_WIDTH + LRU_WIDTH]
        ug_ref[...] = proj[:, POOL_WIDTH + LRU_WIDTH:]

    out = jax.ShapeDtypeStruct((s, 512), F32)
    return pl.pallas_call(
        body, name="fwd_in", grid=(s // tb,),
        in_specs=[pl.BlockSpec((tb, D_MODEL), lambda i: (i, 0)),
                  pl.BlockSpec((1, D_MODEL), lambda i: (0, 0)),
                  pl.BlockSpec((IN_WIDTH, D_MODEL), lambda i: (0, 0))],
        out_specs=[pl.BlockSpec((tb, 512), lambda i: (i, 0))] * 3,
        out_shape=[out, out, out],
        compiler_params=_params(40, 1),
    )(x, ln1_g, w_in_t)


def _pool_denominator(t0, row, window):
    return jnp.minimum((t0 + row + 1).astype(F32), window)


def _causal_window(ext, onehot):
    s2 = ext + _down(ext, 1)
    s4 = s2 + _down(s2, 2)
    s8 = s4 + _down(s4, 4)
    s16 = s8 + _down(s8, 8)
    return (onehot[0] * s2 + onehot[1] * s4 + onehot[2] * s8 + onehot[3] * s16)[HALO:]


def _anticausal_window(ext, onehot, rows):
    s2 = ext + _up(ext, 1)
    s4 = s2 + _up(s2, 2)
    s8 = s4 + _up(s4, 4)
    s16 = s8 + _up(s8, 8)
    return (onehot[0] * s2 + onehot[1] * s4 + onehot[2] * s8 + onehot[3] * s16)[:rows]


def _conv_taps(ext):
    return [ext[HALO:], _down(ext, 1)[HALO:], _down(ext, 2)[HALO:], _down(ext, 3)[HALO:]]


def _conv(taps, cw, cb):
    return cw[3:4] * taps[0] + cw[2:3] * taps[1] + cw[1:2] * taps[2] + cw[0:1] * taps[3] + cb


def _lru_gates(xc, wa, ba, wi, bi, sp):
    xb = xc.astype(BF16)
    r = jax.nn.sigmoid(_mm(xb, wa) + ba)
    i = jax.nn.sigmoid(_mm(xb, wi) + bi)
    la = (-LRU_C) * r * sp
    a = jnp.exp(la)
    mult = jnp.sqrt(jnp.maximum(-_expm1(2.0 * la), 0.0))
    return xb, r, i, a, mult


def _scan_causal(a, b, h_prev, row, rows):
    d = 1
    while d < rows:
        head = row < d
        a_sh = jnp.where(head, 1.0, _down(a, d))
        b_sh = jnp.where(head, 0.0, _down(b, d))
        b = a * b_sh + b
        a = a * a_sh
        d *= 2
    return b + a * h_prev


def _scan_anticausal(a, b, l_next, row, rows):
    d = 1
    while d < rows:
        tail = row >= rows - d
        a_sh = jnp.where(tail, 1.0, _up(a, d))
        b_sh = jnp.where(tail, 0.0, _up(b, d))
        b = a * b_sh + b
        a = a * a_sh
        d *= 2
    return b + a * l_next


def _slab_scalars():
    slab = pl.program_id(0)
    onehot = [jnp.where(slab == k, 1.0, 0.0).astype(F32) for k in range(N_SLAB)]
    window = jnp.left_shift(jnp.int32(2), slab).astype(F32)
    return onehot, window


def _slab_specs(s):
    seq = pl.BlockSpec((s, SLAB), lambda k: (0, k))
    mat = pl.BlockSpec((1, SLAB, SLAB), lambda k: (k, 0, 0))
    vec = pl.BlockSpec((1, SLAB), lambda k: (0, k))
    taps = pl.BlockSpec((CONV_WIDTH, SLAB), lambda k: (0, k))
    return seq, mat, vec, taps


def _mixer_fwd(u_pool, u_lru, u_gate, pool_w, pool_scale, conv_w, conv_b, wa_bd, b_a, wi_bd, b_i, lam):
    s = u_pool.shape[0]
    tc = _time_chunk(s)
    n_chunks = s // tc

    def body(up_ref, ul_ref, ug_ref, pw_ref, ps_ref, cw_ref, cb_ref, wa_ref, ba_ref, wi_ref, bi_ref, lam_ref,
             yp_ref, h_ref, yl_ref):
        onehot, window = _slab_scalars()
        pw = pw_ref[0].astype(BF16)
        wa = wa_ref[0].astype(BF16)
        wi = wi_ref[0].astype(BF16)
        ps, cw, cb, ba, bi = ps_ref[...], cw_ref[...], cb_ref[...], ba_ref[...], bi_ref[...]
        sp = _softplus_neg(lam_ref[...])
        row = lax.broadcasted_iota(jnp.int32, (tc, SLAB), 0)

        def chunk(t0, ext_p, ext_l, h_prev):
            rows = pl.ds(t0, tc)
            d = _causal_window(ext_p, onehot) / _pool_denominator(t0, row, window) - ext_p[HALO:]
            yp_ref[rows, :] = _mm(d.astype(BF16), pw) * ps
            xc = _conv(_conv_taps(ext_l), cw, cb)
            _, _, i, a, mult = _lru_gates(xc, wa, ba, wi, bi, sp)
            h = _scan_causal(a, mult * (i * xc), h_prev, row, tc)
            h_ref[rows, :] = h
            yl_ref[rows, :] = h * _gelu(ug_ref[rows, :])[0]
            return h[tc - 1:tc, :]

        pad = jnp.zeros((HALO, SLAB), F32)
        h0 = chunk(0, jnp.concatenate([pad, up_ref[pl.ds(0, tc), :]], axis=0),
                   jnp.concatenate([pad, ul_ref[pl.ds(0, tc), :]], axis=0), jnp.zeros((1, SLAB), F32))

        def step(c, h_prev):
            t0 = pl.multiple_of(c * tc, tc)
            ext = pl.ds(pl.multiple_of(c * tc - HALO, HALO), tc + HALO)
            return chunk(t0, up_ref[ext, :], ul_ref[ext, :], h_prev)

        lax.fori_loop(1, n_chunks, step, h0)

    seq, mat, vec, taps = _slab_specs(s)
    out = jax.ShapeDtypeStruct((s, 512), F32)
    return pl.pallas_call(
        body, name="mixer_fwd", grid=(N_SLAB,),
        in_specs=[seq, seq, seq, mat, vec, taps, vec, mat, vec, mat, vec, vec],
        out_specs=[seq, seq, seq], out_shape=[out, out, out],
        compiler_params=_params(48, 1),
    )(u_pool, u_lru, u_gate, pool_w, pool_scale, conv_w, conv_b, wa_bd, b_a, wi_bd, b_i, lam)


def _fwd_out(x, y_pool, y_lru, gn_pool_g, gn_lru_g, w_out, ln2_g):
    s = x.shape[0]
    tb = _token_block(s)

    def body(x_ref, yp_ref, yl_ref, gp_ref, gl_ref, w_ref, g2_ref, h1_ref, n2_ref):
        mp, _, _ = _rms(yp_ref[...], gp_ref[...])
        ml, _, _ = _rms(yl_ref[...], gl_ref[...])
        h1 = x_ref[...] + _mm(mp.astype(BF16), w_ref[:POOL_WIDTH, :]) + _mm(ml.astype(BF16), w_ref[POOL_WIDTH:, :])
        h1_ref[...] = h1
        n2_ref[...] = _rms(h1, g2_ref[...])[0].astype(BF16)

    row = pl.BlockSpec((tb, D_MODEL), lambda i: (i, 0))
    half = pl.BlockSpec((tb, 512), lambda i: (i, 0))
    return pl.pallas_call(
        body, name="fwd_out", grid=(s // tb,),
        in_specs=[row, half, half, pl.BlockSpec((1, 512), lambda i: (0, 0)), pl.BlockSpec((1, 512), lambda i: (0, 0)),
                  pl.BlockSpec((D_MODEL, D_MODEL), lambda i: (0, 0)), pl.BlockSpec((1, D_MODEL), lambda i: (0, 0))],
        out_specs=[row, row],
        out_shape=[jax.ShapeDtypeStruct((s, D_MODEL), F32), jax.ShapeDtypeStruct((s, D_MODEL), BF16)],
        compiler_params=_params(40, 1),
    )(x, y_pool, y_lru, gn_pool_g, gn_lru_g, w_out, ln2_g)


def _ffn_fwd(h1, n2, target, lnf_g, w_gate, w_up, w_down):
    s = h1.shape[0]
    tb = min(1024, s)
    sub = 256
    n_ff = D_FF // FF_CHUNK

    def body(h1_ref, n2_ref, t_ref, gf_ref, wg_ref, wu_ref, wd_ref,
             g_ref, u_ref, dh_ref, dhb_ref, dgf_ref, sq_ref, acc):
        i = pl.program_id(0)
        j = pl.program_id(1)

        @pl.when(jnp.logical_and(i == 0, j == 0))
        def _():
            dgf_ref[...] = jnp.zeros_like(dgf_ref)
            sq_ref[...] = jnp.zeros_like(sq_ref)

        n2v = n2_ref[...]
        g = _mm_nt(n2v, wg_ref[...])
        u = _mm_nt(n2v, wu_ref[...])
        g_ref[...] = g.astype(BF16)
        u_ref[...] = u.astype(BF16)
        part = _mm((g * jax.nn.sigmoid(g) * u).astype(BF16), wd_ref[...])

        @pl.when(j == 0)
        def _():
            acc[...] = part

        @pl.when(j > 0)
        def _():
            acc[...] += part

        @pl.when(j == n_ff - 1)
        def _():
            gf = gf_ref[...]
            for r in range(tb // sub):
                rows = slice(r * sub, (r + 1) * sub)
                y, xhat, rstd = _rms(h1_ref[rows, :] + acc[rows, :], gf)
                err = y - t_ref[rows, :]
                sq_ref[...] += jnp.sum(err * err, axis=0, keepdims=True)
                dh2, dgf = _rms_bwd(err * (1.0 / D_MODEL), xhat, rstd, gf)
                dgf_ref[...] += dgf
                dh_ref[rows, :] = dh2
                dhb_ref[rows, :] = dh2.astype(BF16)

    row = pl.BlockSpec((tb, D_MODEL), lambda i, j: (i, 0))
    act = pl.BlockSpec((tb, FF_CHUNK), lambda i, j: (i, j))
    w_row = pl.BlockSpec((FF_CHUNK, D_MODEL), lambda i, j: (j, 0))
    vec = pl.BlockSpec((1, D_MODEL), lambda i, j: (0, 0))
    return pl.pallas_call(
        body, name="ffn_fwd", grid=(s // tb, n_ff),
        in_specs=[row, row, row, vec, w_row, w_row, w_row],
        out_specs=[act, act, row, row, vec, vec],
        out_shape=[jax.ShapeDtypeStruct((s, D_FF), BF16), jax.ShapeDtypeStruct((s, D_FF), BF16),
                   jax.ShapeDtypeStruct((s, D_MODEL), F32), jax.ShapeDtypeStruct((s, D_MODEL), BF16),
                   jax.ShapeDtypeStruct((1, D_MODEL), F32), jax.ShapeDtypeStruct((1, D_MODEL), F32)],
        scratch_shapes=[pltpu.VMEM((tb, D_MODEL), F32)],
        compiler_params=_params(56, 2),
    )(h1, n2, target, lnf_g, w_gate, w_up, w_down)


def _ffn_bwd(n2, dh2b, g, u, w_gate_t, w_up_t, w_down):
    s = n2.shape[0]
    tb = min(1024, s)
    n_ff = D_FF // FF_CHUNK
    n_tb = s // tb

    def body(n2_ref, dh_ref, g_ref, u_ref, wg_ref, wu_ref, wd_ref, dwg_ref, dwu_ref, dwd_ref, dn2_ref,
             dn2_acc, acc_g, acc_u, acc_d):
        j = pl.program_id(0)
        t = pl.program_id(1)

        @pl.when(t == 0)
        def _():
            acc_g[...] = jnp.zeros_like(acc_g)
            acc_u[...] = jnp.zeros_like(acc_u)
            acc_d[...] = jnp.zeros_like(acc_d)

        n2v = n2_ref[...]
        dh = dh_ref[...]
        gv = g_ref[...].astype(F32)
        uv = u_ref[...].astype(F32)
        sg = jax.nn.sigmoid(gv)
        silu = gv * sg
        dact = _mm_nt(dh, wd_ref[...])
        dub = (dact * silu).astype(BF16)
        dgb = (dact * uv * (sg * (1.0 + gv * (1.0 - sg)))).astype(BF16)
        acc_d[...] += _mm_tn((silu * uv).astype(BF16), dh)
        acc_g[...] += _mm_tn(dgb, n2v)
        acc_u[...] += _mm_tn(dub, n2v)
        part = _mm(jnp.concatenate([dgb, dub], axis=1), jnp.concatenate([wg_ref[...], wu_ref[...]], axis=0))
        rows = pl.ds(pl.multiple_of(t * tb, tb), tb)

        @pl.when(t == n_tb - 1)
        def _():
            dwg_ref[...] = acc_g[...].astype(BF16)
            dwu_ref[...] = acc_u[...].astype(BF16)
            dwd_ref[...] = acc_d[...].astype(BF16)

        @pl.when(j == 0)
        def _():
            dn2_acc[rows, :] = part

        @pl.when(jnp.logical_and(j > 0, j < n_ff - 1))
        def _():
            dn2_acc[rows, :] += part

        @pl.when(j == n_ff - 1)
        def _():
            dn2_ref[...] = dn2_acc[rows, :] + part

    row = pl.BlockSpec((tb, D_MODEL), lambda j, t: (t, 0))
    act = pl.BlockSpec((tb, FF_CHUNK), lambda j, t: (t, j))
    w_row = pl.BlockSpec((FF_CHUNK, D_MODEL), lambda j, t: (j, 0))
    last = pl.BlockSpec((tb, D_MODEL), lambda j, t: (jnp.where(j == n_ff - 1, t, 0), 0))
    grad = jax.ShapeDtypeStruct((D_FF, D_MODEL), BF16)
    chunk_acc = pltpu.VMEM((FF_CHUNK, D_MODEL), F32)
    return pl.pallas_call(
        body, name="ffn_bwd", grid=(n_ff, n_tb),
        in_specs=[row, row, act, act, w_row, w_row, w_row],
        out_specs=[w_row, w_row, w_row, last],
        out_shape=[grad, grad, grad, jax.ShapeDtypeStruct((s, D_MODEL), F32)],
        scratch_shapes=[pltpu.VMEM((s, D_MODEL), F32), chunk_acc, chunk_acc, chunk_acc],
        compiler_params=_params(56, 2),
    )(n2, dh2b, g, u, w_gate_t, w_up_t, w_down)


def _bwd_out(dn2, dh2, h1, y_pool, y_lru, gn_pool_g, gn_lru_g, w_out, ln2_g, after):
    s = h1.shape[0]
    tb = _token_block(s)

    def body(dn2_ref, dh2_ref, h1_ref, yp_ref, yl_ref, gp_ref, gl_ref, w_ref, g2_ref, _after,
             dh1_ref, dyp_ref, dyl_ref, dwb_ref, dg2_ref, dgp_ref, dgl_ref, dw_ref):
        @pl.when(pl.program_id(0) == 0)
        def _():
            dw_ref[...] = jnp.zeros_like(dw_ref)
            dg2_ref[...] = jnp.zeros_like(dg2_ref)
            dgp_ref[...] = jnp.zeros_like(dgp_ref)
            dgl_ref[...] = jnp.zeros_like(dgl_ref)

        g2 = g2_ref[...]
        _, xhat2, rstd2 = _rms(h1_ref[...], g2)
        dres, dg2 = _rms_bwd(dn2_ref[...], xhat2, rstd2, g2)
        dg2_ref[...] += dg2
        dh1 = dh2_ref[...] + dres
        dh1_ref[...] = dh1
        dh1b = dh1.astype(BF16)
        gp, gl = gp_ref[...], gl_ref[...]
        mp, xhat_p, rstd_p = _rms(yp_ref[...], gp)
        ml, xhat_l, rstd_l = _rms(yl_ref[...], gl)
        dw_ref[:POOL_WIDTH, :] += _mm_tn(mp.astype(BF16), dh1b)
        dw_ref[POOL_WIDTH:, :] += _mm_tn(ml.astype(BF16), dh1b)
        dyp, dgp = _rms_bwd(_mm_nt(dh1b, w_ref[:POOL_WIDTH, :]), xhat_p, rstd_p, gp)
        dyl, dgl = _rms_bwd(_mm_nt(dh1b, w_ref[POOL_WIDTH:, :]), xhat_l, rstd_l, gl)
        dyp_ref[...] = dyp
        dyl_ref[...] = dyl
        dgp_ref[...] += dgp
        dgl_ref[...] += dgl

        @pl.when(pl.program_id(0) == s // tb - 1)
        def _():
            dwb_ref[...] = dw_ref[...].astype(BF16)

    row = pl.BlockSpec((tb, D_MODEL), lambda i: (i, 0))
    half = pl.BlockSpec((tb, 512), lambda i: (i, 0))
    vec = pl.BlockSpec((1, D_MODEL), lambda i: (0, 0))
    hvec = pl.BlockSpec((1, 512), lambda i: (0, 0))
    mat = pl.BlockSpec((D_MODEL, D_MODEL), lambda i: (0, 0))
    return pl.pallas_call(
        body, name="bwd_out", grid=(s // tb,),
        in_specs=[row, row, row, half, half, hvec, hvec, mat, vec, pl.BlockSpec(memory_space=pl.ANY)],
        out_specs=[row, half, half, mat, vec, hvec, hvec],
        out_shape=[jax.ShapeDtypeStruct((s, D_MODEL), F32), jax.ShapeDtypeStruct((s, 512), F32),
                   jax.ShapeDtypeStruct((s, 512), F32), jax.ShapeDtypeStruct((D_MODEL, D_MODEL), BF16),
                   jax.ShapeDtypeStruct((1, D_MODEL), F32), jax.ShapeDtypeStruct((1, 512), F32),
                   jax.ShapeDtypeStruct((1, 512), F32)],
        scratch_shapes=[pltpu.VMEM((D_MODEL, D_MODEL), F32)],
        compiler_params=_params(48, 1),
    )(dn2, dh2, h1, y_pool, y_lru, gn_pool_g, gn_lru_g, w_out, ln2_g, after)


def _mixer_bwd(u_pool, u_lru, u_gate, h, dy_pool, dy_lru,
               pool_w, pool_scale, conv_w, conv_b, wa_bd, b_a, wi_bd, b_i, lam, after):
    s = u_pool.shape[0]
    tc = _time_chunk(s)
    n_chunks = s // tc

    def body(up_ref, ul_ref, ug_ref, h_ref, dyp_ref, dyl_ref,
             pw_ref, ps_ref, cw_ref, cb_ref, wa_ref, ba_ref, wi_ref, bi_ref, lam_ref, _after,
             dup_ref, dul_ref, dug_ref, dpw_ref, dps_ref, dcw_ref, dcb_ref, dwa_ref, dba_ref, dwi_ref, dbi_ref, dlam_ref):
        onehot, window = _slab_scalars()
        pw = pw_ref[0].astype(BF16)
        wa = wa_ref[0].astype(BF16)
        wi = wi_ref[0].astype(BF16)
        ps, cw, cb, ba, bi = ps_ref[...], cw_ref[...], cb_ref[...], ba_ref[...], bi_ref[...]
        lam_v = lam_ref[...]
        sp = _softplus_neg(lam_v)
        row = lax.broadcasted_iota(jnp.int32, (tc, SLAB), 0)
        for ref in (dpw_ref, dps_ref, dcw_ref, dcb_ref, dwa_ref, dba_ref, dwi_ref, dbi_ref, dlam_ref):
            ref[...] = jnp.zeros_like(ref)

        def chunk(t0, ext_p, ext_l, ext_h, carry):
            l_next, a_next, dxc_next, ddn_next = carry
            rows = pl.ds(t0, tc)
            taps = _conv_taps(ext_l)
            xc = _conv(taps, cw, cb)
            xb, r, i, a, mult = _lru_gates(xc, wa, ba, wi, bi, sp)
            hv = ext_h[HALO:]
            h_before = _down(ext_h, 1)[HALO:]
            ug = ug_ref[rows, :]
            dyl = dyl_ref[rows, :]
            gel, th = _gelu(ug)
            dug_ref[rows, :] = dyl * hv * _gelu_grad(ug, th)
            a_after = jnp.where(row == tc - 1, a_next, _up(a, 1))
            l = _scan_anticausal(a_after, dyl * gel, l_next, row, tc)
            dmult = l * (i * xc)
            di = l * mult * xc
            dxc = l * mult * i
            dla = (l * h_before) * a - jnp.where(mult > 0.0, dmult * (a * a) / mult, 0.0)
            dlam_ref[...] += jnp.sum(dla * r, axis=0, keepdims=True)
            dpa = (dla * ((-LRU_C) * sp)) * (r * (1.0 - r))
            dpi = di * (i * (1.0 - i))
            dpab = dpa.astype(BF16)
            dpib = dpi.astype(BF16)
            dwa_ref[0] += _mm_tn(xb, dpab)
            dwi_ref[0] += _mm_tn(xb, dpib)
            dba_ref[...] += jnp.sum(dpa, axis=0, keepdims=True)
            dbi_ref[...] += jnp.sum(dpi, axis=0, keepdims=True)
            dxc = dxc + _mm_nt(dpab, wa) + _mm_nt(dpib, wi)
            ext_d = jnp.concatenate([dxc, dxc_next], axis=0)
            dul_ref[rows, :] = (cw[3:4] * dxc + cw[2:3] * _up(ext_d, 1)[:tc]
                                + cw[1:2] * _up(ext_d, 2)[:tc] + cw[0:1] * _up(ext_d, 3)[:tc])
            for k in range(CONV_WIDTH):
                dcw_ref[k:k + 1, :] += jnp.sum(dxc * taps[CONV_WIDTH - 1 - k], axis=0, keepdims=True)
            dcb_ref[...] += jnp.sum(dxc, axis=0, keepdims=True)
            denom = _pool_denominator(t0, row, window)
            db = (_causal_window(ext_p, onehot) / denom - ext_p[HALO:]).astype(BF16)
            dyp = dyp_ref[rows, :]
            dps_ref[...] += jnp.sum(dyp * _mm(db, pw), axis=0, keepdims=True)
            dys = (dyp * ps).astype(BF16)
            dpw_ref[0] += _mm_tn(db, dys)
            dd = _mm_nt(dys, pw)
            ddn = dd / denom
            ext_q = jnp.concatenate([ddn, ddn_next], axis=0)
            dup_ref[rows, :] = _anticausal_window(ext_q, onehot, tc) - dd
            return l[0:1, :], a[0:1, :], dxc[0:8, :], ddn[0:HALO, :]

        def step(k, carry):
            c = n_chunks - 1 - k
            t0 = pl.multiple_of(c * tc, tc)
            ext = pl.ds(pl.multiple_of(c * tc - HALO, HALO), tc + HALO)
            return chunk(t0, up_ref[ext, :], ul_ref[ext, :], h_ref[ext, :], carry)

        carry = (jnp.zeros((1, SLAB), F32), jnp.zeros((1, SLAB), F32),
                 jnp.zeros((8, SLAB), F32), jnp.zeros((HALO, SLAB), F32))
        carry = lax.fori_loop(0, n_chunks - 1, step, carry)
        pad = jnp.zeros((HALO, SLAB), F32)
        first = pl.ds(0, tc)
        chunk(0, jnp.concatenate([pad, up_ref[first, :]], axis=0), jnp.concatenate([pad, ul_ref[first, :]], axis=0),
              jnp.concatenate([pad, h_ref[first, :]], axis=0), carry)
        dlam_ref[...] = dlam_ref[...] * (LRU_C * jax.nn.sigmoid(-lam_v))

    seq, mat, vec, taps = _slab_specs(s)
    full = jax.ShapeDtypeStruct((s, 512), F32)
    mats = jax.ShapeDtypeStruct((N_SLAB, SLAB, SLAB), F32)
    vecs = jax.ShapeDtypeStruct((1, 512), F32)
    return pl.pallas_call(
        body, name="mixer_bwd", grid=(N_SLAB,),
        in_specs=[seq] * 6 + [mat, vec, taps, vec, mat, vec, mat, vec, vec, pl.BlockSpec(memory_space=pl.ANY)],
        out_specs=[seq, seq, seq, mat, vec, taps, vec, mat, vec, mat, vec, vec],
        out_shape=[full, full, full, mats, vecs, jax.ShapeDtypeStruct((CONV_WIDTH, 512), F32), vecs,
                   mats, vecs, mats, vecs, vecs],
        compiler_params=_params(56, 1),
    )(u_pool, u_lru, u_gate, h, dy_pool, dy_lru, pool_w, pool_scale, conv_w, conv_b, wa_bd, b_a, wi_bd, b_i, lam,
      after)


def _bwd_in(x, dh1, du_pool, du_lru, du_gate, ln1_g, w_in_t):
    s = x.shape[0]
    tb = _token_block(s)

    def body(x_ref, dh1_ref, dup_ref, dul_ref, dug_ref, g_ref, w_ref, dx_ref, dwb_ref, dg_ref, dw_ref):
        @pl.when(pl.program_id(0) == 0)
        def _():
            dw_ref[...] = jnp.zeros_like(dw_ref)
            dg_ref[...] = jnp.zeros_like(dg_ref)

        g1 = g_ref[...]
        n, xhat, rstd = _rms(x_ref[...], g1)
        nb = n.astype(BF16)
        dn = jnp.zeros((tb, D_MODEL), F32)
        for k, ref in enumerate((dup_ref, dul_ref, dug_ref)):
            rows = slice(k * 512, (k + 1) * 512)
            db = ref[...].astype(BF16)
            dw_ref[rows, :] += _mm_tn(db, nb)
            dn = dn + _mm(db, w_ref[rows, :])
        dx, dg1 = _rms_bwd(dn, xhat, rstd, g1)
        dx_ref[...] = dh1_ref[...] + dx
        dg_ref[...] += dg1

        @pl.when(pl.program_id(0) == s // tb - 1)
        def _():
            dwb_ref[...] = dw_ref[...].astype(BF16)

    row = pl.BlockSpec((tb, D_MODEL), lambda i: (i, 0))
    half = pl.BlockSpec((tb, 512), lambda i: (i, 0))
    vec = pl.BlockSpec((1, D_MODEL), lambda i: (0, 0))
    mat = pl.BlockSpec((IN_WIDTH, D_MODEL), lambda i: (0, 0))
    return pl.pallas_call(
        body, name="bwd_in", grid=(s // tb,),
        in_specs=[row, row, half, half, half, vec, mat],
        out_specs=[row, mat, vec],
        out_shape=[jax.ShapeDtypeStruct((s, D_MODEL), F32), jax.ShapeDtypeStruct((IN_WIDTH, D_MODEL), BF16),
                   jax.ShapeDtypeStruct((1, D_MODEL), F32)],
        scratch_shapes=[pltpu.VMEM((IN_WIDTH, D_MODEL), F32)],
        compiler_params=_params(48, 1),
    )(x, dh1, du_pool, du_lru, du_gate, ln1_g, w_in_t)


def _mesh_position():
    x, y, c = lax.axis_index("x"), lax.axis_index("y"), lax.axis_index("c")
    return x, y, c, 4 * x + 2 * y + c


def _peer(x, y, c, p):
    px = 1 - x if p & 4 else x
    py = 1 - y if p & 2 else y
    pc = 1 - c if p & 1 else c
    return (px, py, pc), 4 * px + 2 * py + pc


HBM_SPEC = pl.BlockSpec(memory_space=pltpu.HBM)
SEM_SPEC = pl.BlockSpec(memory_space=pltpu.SEMAPHORE)
DATAFLOW = pltpu.SideEffectType.DATAFLOW_SIDE_EFFECTING


class Copy(NamedTuple):
    src: int
    src_at: Any
    dst: int
    dst_at: Any
    peer: int
    group: int
    slot: int
    rows: Any = None


SIBLING = (1,)
SAME_CORE = (2, 4, 6)
EVERYONE = tuple(range(1, N_DEV))


def _same(index):
    return index


def _chip(index):
    return jnp.right_shift(index, 1)


def _fan_out(srcs, lands, peers, group):
    return [Copy(s, None, d, _same, p, group, N_DEV * i + p) for i, (s, d) in enumerate(zip(srcs, lands)) for p in peers]


ICI_CHUNK_ROWS = 32


def _in_chunks(copies, n_rows):
    return [cp._replace(rows=(first, ICI_CHUNK_ROWS)) for cp in copies for first in range(0, n_rows, ICI_CHUNK_ROWS)]


def _numbered(copies, group=0):
    return [cp._replace(group=group, slot=i) for i, cp in enumerate(copies)]


def _relay(lands, peers):
    return [Copy(b, lambda s, q=q: jnp.bitwise_xor(s, q), b, lambda s, q=q: jnp.bitwise_xor(s, q), 1, 0, N_DEV * i + q)
            for i, b in enumerate(lands) for q in peers]


def _to_sibling(stacks, lands):
    return [Copy(s, lambda me, k=k: 2 * k + 1 - jnp.bitwise_and(me, 1), d, lambda me, k=k: k, 1, 0, 4 * i + k)
            for i, (s, d) in enumerate(zip(stacks, lands)) for k in range(N_DEV // 2)]


def _to_chips(sums, lands):
    return [Copy(s, lambda me, p=p: jnp.bitwise_xor(_chip(me), p // 2), d, _chip, p, 0, 4 * i + p // 2)
            for i, (s, d) in enumerate(zip(sums, lands)) for p in SAME_CORE]


def _comm_call(name, bufs, wait=None, local=(), start=None, after=None):
    nb = len(bufs)
    slots = list(start[1]) if start else []
    n_out_sem = 2 * len(slots)

    def body(*refs):
        b = refs[:nb]
        at = nb
        if wait:
            w_send, w_recv = refs[at], refs[at + 1]
            at += 2
        if after is not None:
            at += 1
        out_sems = refs[at:at + n_out_sem]
        token = refs[at + n_out_sem + nb]
        x, y, c, me = _mesh_position()

        def part(i, row_of, sender, rows=None):
            ref = b[i] if row_of is None else b[i].at[row_of(sender)]
            return ref if rows is None else ref.at[pl.ds(rows[0], rows[1])]

        if wait:
            for cp in wait[1]:
                peer, peer_index = _peer(x, y, c, cp.peer)
                arrival = pltpu.make_async_remote_copy(part(cp.src, cp.src_at, me, cp.rows),
                                                       part(cp.dst, cp.dst_at, peer_index, cp.rows),
                                                       w_send.at[cp.slot], w_recv.at[cp.slot],
                                                       device_id=peer, device_id_type=MESH)
                arrival.wait_send()
                arrival.wait_recv()
        if local:
            local_sems = refs[-1]
            copies = [pltpu.make_async_copy(part(sb, sf, me), part(db, df, me), local_sems.at[k])
                      for k, (sb, sf, db, df) in enumerate(local)]
            for cp in copies:
                cp.start()
            for cp in copies:
                cp.wait()
        if start:
            for cp in start[0]:
                peer, _ = _peer(x, y, c, cp.peer)
                pltpu.make_async_remote_copy(part(cp.src, cp.src_at, me, cp.rows), part(cp.dst, cp.dst_at, me, cp.rows),
                                             out_sems[2 * cp.group].at[cp.slot], out_sems[2 * cp.group + 1].at[cp.slot],
                                             device_id=peer, device_id_type=MESH).start()
        token[...] = jnp.zeros_like(token)

    sem_shapes = []
    for n_slots in slots:
        sem_shapes += [pltpu.SemaphoreType.DMA((n_slots,))] * 2
    operands = [pltpu.with_memory_space_constraint(a, pltpu.HBM) for a in bufs]
    in_specs = [HBM_SPEC] * nb
    if wait:
        operands += list(wait[0])
        in_specs += [SEM_SPEC, SEM_SPEC]
    if after is not None:
        operands.append(after)
        in_specs.append(pl.BlockSpec(memory_space=pl.ANY))
    outs = pl.pallas_call(
        body, name=name, in_specs=in_specs,
        out_specs=[SEM_SPEC] * n_out_sem + [HBM_SPEC] * nb + [pl.BlockSpec(memory_space=pltpu.VMEM)],
        out_shape=sem_shapes + [pltpu.HBM(a.shape, a.dtype) for a in bufs] + [jax.ShapeDtypeStruct((8, SLAB), F32)],
        input_output_aliases={i: n_out_sem + i for i in range(nb)},
        scratch_shapes=[pltpu.SemaphoreType.DMA((len(local),))] if local else [],
        compiler_params=pltpu.CompilerParams(has_side_effects=DATAFLOW),
    )(*operands)
    sems = [(outs[2 * k], outs[2 * k + 1]) for k in range(len(slots))]
    return sems, list(outs[n_out_sem:n_out_sem + nb]), outs[-1]


def _pair_sum(stacks, lands, place):
    n = len(stacks)

    def body(place_ref, *refs):
        k = pl.program_id(0)
        for m in range(n):
            mine, theirs, out, land = refs[m], refs[n + m], refs[2 * n + m], refs[3 * n + m]
            total = (mine[0, 0].astype(F32) + theirs[0].astype(F32)).astype(out.dtype)
            out[0] = total

            @pl.when(k == place_ref[1])
            def _():
                land[0] = total

    in_specs = [pl.BlockSpec((1, 1) + a.shape[1:], lambda k, place_ref: (k, place_ref[0], 0, 0)) for a in stacks]
    in_specs += [pl.BlockSpec((1,) + a.shape[1:], lambda k, place_ref: (k, 0, 0)) for a in lands]
    out_specs = [pl.BlockSpec((1,) + a.shape[1:], lambda k, place_ref: (k, 0, 0)) for a in lands]
    out_specs += [pl.BlockSpec((1,) + a.shape[1:], lambda k, place_ref: (place_ref[1], 0, 0)) for a in lands]
    outs = pl.pallas_call(
        body, name="pair_sum_" + "_".join(str(a.shape[1]) for a in stacks),
        grid_spec=pltpu.PrefetchScalarGridSpec(num_scalar_prefetch=1, grid=(N_DEV // 2,), in_specs=in_specs,
                                               out_specs=out_specs),
        out_shape=[jax.ShapeDtypeStruct(a.shape, a.dtype) for a in lands] * 2,
        compiler_params=_params(40, 1),
    )(place, *[a.reshape((N_DEV // 2, 2) + a.shape[1:]) for a in stacks], *lands)
    return list(outs[:n]), list(outs[n:])


def _pair_add(a, b, place):
    def body(place_ref, a_ref, b_ref, out, land):
        del place_ref
        out[...] = a_ref[...] + b_ref[...]
        land[0] = out[...]

    whole = pl.BlockSpec(a.shape, lambda i, place_ref: (0, 0))
    return pl.pallas_call(
        body, name="pair_add",
        grid_spec=pltpu.PrefetchScalarGridSpec(
            num_scalar_prefetch=1, grid=(1,), in_specs=[whole, whole],
            out_specs=[whole, pl.BlockSpec((1,) + a.shape, lambda i, place_ref: (place_ref[1], 0, 0))]),
        out_shape=[jax.ShapeDtypeStruct(a.shape, a.dtype), jax.ShapeDtypeStruct((N_DEV // 2,) + a.shape, a.dtype)],
    )(place, a, b)


def _reduce_adam(parts, w, m, v, name):
    rows, cols = w.shape
    n_parts = parts.shape[0]
    rb = rows
    for cand in (256, 176, 128):
        if rows % cand == 0 and rows > cand:
            rb = cand
            break

    def body(p_ref, w_ref, m_ref, v_ref, g_out, d_out, m_out, v_out):
        g = p_ref[0].astype(F32)
        for j in range(1, n_parts):
            g = g + p_ref[j].astype(F32)
        g_out[...] = g
        d_out[...], m_out[...], v_out[...] = _adam(g, w_ref[...], m_ref[...], v_ref[...])

    blk = pl.BlockSpec((rb, cols), lambda i: (i, 0))
    out = jax.ShapeDtypeStruct((rows, cols), F32)
    return pl.pallas_call(
        body, name=name, grid=(rows // rb,),
        in_specs=[pl.BlockSpec((n_parts, rb, cols), lambda i: (0, i, 0)), blk, blk, blk],
        out_specs=[blk] * 4, out_shape=[out] * 4,
        compiler_params=_params(40, 1),
    )(parts, w, m, v)


def _cols_from_stack(stack):
    n, r, c = stack.shape
    return jnp.transpose(stack, (1, 0, 2)).reshape(r, n * c)


def _block_diag(w):
    z = jnp.zeros((N_SLAB, 64, 64), w.dtype)
    pairs = w.reshape(N_SLAB, 2, 64, 64)
    top = jnp.concatenate([pairs[:, 0], z], axis=2)
    bottom = jnp.concatenate([z, pairs[:, 1]], axis=2)
    return jnp.concatenate([top, bottom], axis=1)


def _diag_blocks(w):
    return jnp.stack([w[:, :64, :64], w[:, 64:, 64:]], axis=1).reshape(8, 64, 64)


def _adam(g, w, m, v):
    m_new = ADAM_B1 * m + (1.0 - ADAM_B1) * g
    v_new = ADAM_B2 * v + (1.0 - ADAM_B2) * (g * g)
    m_hat = m_new / (1.0 - ADAM_B1 ** ADAM_STEP)
    v_hat = v_new / (1.0 - ADAM_B2 ** ADAM_STEP)
    return (-ADAM_LR) * (m_hat / (jnp.sqrt(v_hat) + ADAM_EPS) + ADAM_WD * w), m_new, v_new


WIDE = ("ln1_g", "ln2_g", "lnf_g")
HALF = ("pool_scale", "conv_b", "b_a", "b_i", "lam", "gn_pool_g", "gn_lru_g")
VECTORS = [(k, D_MODEL) for k in WIDE] + [(k, 512) for k in HALF]
VECTOR_ROWS = sum(width // SLAB for _, width in VECTORS)
MATRIX_AT = -(-VECTOR_ROWS // 8) * 8
MATRIX_ROWS = N_SLAB * SLAB
N_MATRIX = 3
LOSS_ROW = MATRIX_AT + N_MATRIX * MATRIX_ROWS
CONV_AT = LOSS_ROW + 8
CONV_LANES = LRU_WIDTH // SLAB
PACK_ROWS = -(-(CONV_AT + CONV_WIDTH * CONV_LANES) // ICI_CHUNK_ROWS) * ICI_CHUNK_ROWS


def _pack_small(vectors, matrices, conv, sq):
    n_vec = len(vectors)

    def body(*refs):
        vec, mat, cw_ref, sq_ref, out = refs[:n_vec], refs[n_vec:n_vec + N_MATRIX], refs[-3], refs[-2], refs[-1]
        out[...] = jnp.zeros_like(out)
        row = 0
        for ref, (_, width) in zip(vec, VECTORS):
            for k in range(width // SLAB):
                out[row:row + 1, :] = ref[:, k * SLAB:(k + 1) * SLAB]
                row += 1
        for i, ref in enumerate(mat):
            for s in range(N_SLAB):
                at = MATRIX_AT + i * MATRIX_ROWS + s * SLAB
                out[at:at + SLAB, :] = ref[s]
        for tap in range(CONV_WIDTH):
            for k in range(CONV_LANES):
                at = CONV_AT + tap * CONV_LANES + k
                out[at:at + 1, :] = cw_ref[tap:tap + 1, k * SLAB:(k + 1) * SLAB]
        total = sq_ref[:, 0:SLAB]
        for k in range(1, D_MODEL // SLAB):
            total = total + sq_ref[:, k * SLAB:(k + 1) * SLAB]
        out[LOSS_ROW:LOSS_ROW + 1, :] = total

    return pl.pallas_call(
        body, name="pack_small", out_shape=jax.ShapeDtypeStruct((PACK_ROWS, SLAB), F32),
    )(*vectors, *matrices, conv, sq)


def _small_reduce_adam(parts, vec_w, vec_m, vec_v, pool_wmv):
    n_vec = len(VECTORS)
    n_parts = parts.shape[0]

    def body(*refs):
        p_ref = refs[0]
        w_refs, m_refs, v_refs = (refs[1 + k * n_vec:1 + (k + 1) * n_vec] for k in range(3))
        pw_w, pw_m, pw_v = refs[1 + 3 * n_vec:4 + 3 * n_vec]
        outs = refs[4 + 3 * n_vec:-1]
        total = refs[-1]
        total[...] = p_ref[0]
        for j in range(1, n_parts):
            total[...] += p_ref[j]
        row = 0
        for i, (_, width) in enumerate(VECTORS):
            n_rows = width // SLAB
            g = jnp.concatenate([total[row + k:row + k + 1, :] for k in range(n_rows)], axis=1)
            row += n_rows
            d, m_new, v_new = _adam(g, w_refs[i][...], m_refs[i][...], v_refs[i][...])
            for ref, val in zip(outs[4 * i:4 * i + 4], (g, d, m_new, v_new)):
                ref[...] = val
        tail = outs[4 * n_vec:]
        g = total[MATRIX_AT:MATRIX_AT + MATRIX_ROWS, :]
        d, m_new, v_new = _adam(g, pw_w[...], pw_m[...], pw_v[...])
        for ref, val in zip(tail[0:4], (g, d, m_new, v_new)):
            ref[...] = val
        tail[4][...] = total[MATRIX_AT + MATRIX_ROWS:MATRIX_AT + 2 * MATRIX_ROWS, :]
        tail[5][...] = total[MATRIX_AT + 2 * MATRIX_ROWS:MATRIX_AT + 3 * MATRIX_ROWS, :]
        for tap in range(CONV_WIDTH):
            at = CONV_AT + tap * CONV_LANES
            tail[6][tap:tap + 1, :] = jnp.concatenate([total[at + k:at + k + 1, :] for k in range(CONV_LANES)], axis=1)
        tail[7][...] = (0.5 / D_MODEL) * jnp.sum(total[LOSS_ROW:LOSS_ROW + 1, :], axis=1, keepdims=True)

    out_shape = []
    for _, width in VECTORS:
        out_shape += [jax.ShapeDtypeStruct((1, width), F32)] * 4
    out_shape += [jax.ShapeDtypeStruct((MATRIX_ROWS, SLAB), F32)] * 6
    out_shape += [jax.ShapeDtypeStruct((CONV_WIDTH, LRU_WIDTH), F32), jax.ShapeDtypeStruct((1, 1), F32)]
    outs = pl.pallas_call(
        body, name="adam_small", out_shape=out_shape,
        scratch_shapes=[pltpu.VMEM((PACK_ROWS, SLAB), F32)],
        compiler_params=_params(40),
    )(parts, *vec_w, *vec_m, *vec_v, *pool_wmv)
    vec_out = [tuple(outs[4 * i:4 * i + 4]) for i in range(n_vec)]
    tail = outs[4 * n_vec:]
    return vec_out, tuple(tail[0:4]), tail[4], tail[5], tail[6], tail[7]


def _plain_adam(grads, ws, ms, vs):
    n = len(grads)

    def body(*refs):
        ins, outs = refs[:4 * n], refs[4 * n:]
        for i in range(n):
            d, m_new, v_new = _adam(ins[i][...], ins[n + i][...], ins[2 * n + i][...], ins[3 * n + i][...])
            for ref, val in zip(outs[3 * i:3 * i + 3], (d, m_new, v_new)):
                ref[...] = val

    out_shape = []
    for g in grads:
        out_shape += [jax.ShapeDtypeStruct(g.shape, F32)] * 3
    outs = pl.pallas_call(body, name="adam_plain", out_shape=out_shape)(*grads, *ws, *ms, *vs)
    return [tuple(outs[3 * i:3 * i + 3]) for i in range(n)]


def kernel(x, ln1_g, w_in, pool_w, pool_scale, conv_w, conv_b, w_a, b_a, w_i, b_i, lam, gn_pool_g, gn_lru_g, w_out, ln2_g, w_ffn_gate, w_ffn_up, w_ffn_down, lnf_g, loss_target, m_ln1_g, m_w_in, m_pool_w, m_pool_scale, m_conv_w, m_conv_b, m_w_a, m_b_a, m_w_i, m_b_i, m_lam, m_gn_pool_g, m_gn_lru_g, m_w_out, m_ln2_g, m_w_ffn_gate, m_w_ffn_up, m_w_ffn_down, m_lnf_g, v_ln1_g, v_w_in, v_pool_w, v_pool_scale, v_conv_w, v_conv_b, v_w_a, v_b_a, v_w_i, v_b_i, v_lam, v_gn_pool_g, v_gn_lru_g, v_w_out, v_ln2_g, v_w_ffn_gate, v_w_ffn_up, v_w_ffn_down, v_lnf_g):
    weights = dict(ln1_g=ln1_g, w_in=w_in, pool_w=pool_w, pool_scale=pool_scale, conv_w=conv_w, conv_b=conv_b,
                   w_a=w_a, b_a=b_a, w_i=w_i, b_i=b_i, lam=lam, gn_pool_g=gn_pool_g, gn_lru_g=gn_lru_g,
                   w_out=w_out, ln2_g=ln2_g, w_ffn_gate=w_ffn_gate, w_ffn_up=w_ffn_up, w_ffn_down=w_ffn_down,
                   lnf_g=lnf_g)
    mom1 = dict(ln1_g=m_ln1_g, w_in=m_w_in, pool_w=m_pool_w, pool_scale=m_pool_scale, conv_w=m_conv_w,
                conv_b=m_conv_b, w_a=m_w_a, b_a=m_b_a, w_i=m_w_i, b_i=m_b_i, lam=m_lam, gn_pool_g=m_gn_pool_g,
                gn_lru_g=m_gn_lru_g, w_out=m_w_out, ln2_g=m_ln2_g, w_ffn_gate=m_w_ffn_gate,
                w_ffn_up=m_w_ffn_up, w_ffn_down=m_w_ffn_down, lnf_g=m_lnf_g)
    mom2 = dict(ln1_g=v_ln1_g, w_in=v_w_in, pool_w=v_pool_w, pool_scale=v_pool_scale, conv_w=v_conv_w,
                conv_b=v_conv_b, w_a=v_w_a, b_a=v_b_a, w_i=v_w_i, b_i=v_b_i, lam=v_lam, gn_pool_g=v_gn_pool_g,
                gn_lru_g=v_gn_lru_g, w_out=v_w_out, ln2_g=v_ln2_g, w_ffn_gate=v_w_ffn_gate,
                w_ffn_up=v_w_ffn_up, w_ffn_down=v_w_ffn_down, lnf_g=v_lnf_g)

    xs = x[0]
    target = loss_target[0]

    shard = dict(w_in=lambda a: a[0].T, w_ffn_gate=lambda a: a[0].T, w_ffn_up=lambda a: a[0].T,
                 w_out=lambda a: a[0], w_ffn_down=lambda a: a[0], conv_w=lambda a: a[0])
    unshard = dict(w_in=lambda a: a.T[None], w_ffn_gate=lambda a: a.T[None], w_ffn_up=lambda a: a.T[None],
                   w_out=lambda a: a[None], w_ffn_down=lambda a: a[None], conv_w=lambda a: a[None])

    gathered = ("w_in", "conv_w", "w_out", "w_ffn_gate", "w_ffn_up", "w_ffn_down")
    sources = [shard[k](weights[k]) if k == "conv_w" else shard[k](weights[k]).astype(BF16) for k in gathered]
    my_index = 4 * lax.axis_index("x") + 2 * lax.axis_index("y") + lax.axis_index("c")
    lands = [lax.dynamic_update_index_in_dim(lax.empty((N_DEV,) + a.shape, a.dtype), a, my_index, 0) for a in sources]

    def in_copies(srcs, dsts, group=0):
        w_in_rows = IN_WIDTH // N_DEV
        return _numbered(_in_chunks(_fan_out(srcs[:1], dsts[:1], EVERYONE, 0), w_in_rows)
                         + _fan_out(srcs[1:], dsts[1:], EVERYONE, 0), group)

    def out_copies(src, dst, group=0):
        return _numbered(_in_chunks(_fan_out((src,), (dst,), EVERYONE, 0), D_MODEL // N_DEV), group)

    def ffn_copies(srcs, dsts, group=0):
        return _numbered(_in_chunks(_fan_out(srcs, dsts, SAME_CORE, 0), D_FF // N_DEV)
                         + _fan_out(srcs, dsts, SIBLING, 0), group)

    start = in_copies((0, 1), (6, 7), 0) + out_copies(2, 8, 1) + ffn_copies((3, 4, 5), (9, 10, 11), 2)
    slots = [sum(cp.group == g for cp in start) for g in range(3)]
    sems, bufs, _ = _comm_call("gather_start", sources + lands, start=(start, slots))
    sources, lands = bufs[:6], bufs[6:]
    _, bufs, _ = _comm_call("gather_wait_in", sources[0:2] + lands[0:2], wait=(sems[0], in_copies((0, 1), (2, 3))))
    w_in_f = bufs[2].reshape(IN_WIDTH, D_MODEL)
    conv_w_f = _cols_from_stack(bufs[3])

    wa_bd = _block_diag(w_a[0])
    wi_bd = _block_diag(w_i[0])
    lnf_row = lnf_g.reshape(1, D_MODEL)

    u_pool, u_lru, u_gate = _fwd_in(xs, ln1_g, w_in_f)
    y_pool, h, y_lru = _mixer_fwd(u_pool, u_lru, u_gate, pool_w[0], pool_scale, conv_w_f, conv_b,
                                  wa_bd, b_a, wi_bd, b_i, lam)
    _, bufs, _ = _comm_call("gather_wait_out", [sources[2], lands[2]], wait=(sems[1], out_copies(0, 1)), after=y_pool)
    w_out_f = bufs[1].reshape(D_MODEL, D_MODEL)
    h1, n2 = _fwd_out(xs, y_pool, y_lru, gn_pool_g, gn_lru_g, w_out_f, ln2_g)
    relay = _numbered(_relay((3, 4, 5), SAME_CORE))
    relay_sems, bufs, _ = _comm_call("gather_relay_ffn", sources[3:6] + lands[3:6],
                                     wait=(sems[2], ffn_copies((0, 1, 2), (3, 4, 5))),
                                     start=(relay, (len(relay),)), after=n2)
    _, bufs, _ = _comm_call("gather_wait_ffn", bufs[3:6],
                            wait=(relay_sems[0], _numbered(_relay((0, 1, 2), SAME_CORE))))
    w_gate_f = bufs[0].reshape(D_FF, D_MODEL)
    w_up_f = bufs[1].reshape(D_FF, D_MODEL)
    w_down_f = bufs[2].reshape(D_FF, D_MODEL)
    g_act, u_act, dh2, dh2b, d_lnf, sq = _ffn_fwd(h1, n2, target, lnf_row, w_gate_f, w_up_f, w_down_f)

    place = jnp.stack([lax.axis_index("c"), 2 * lax.axis_index("x") + lax.axis_index("y")]).astype(jnp.int32)

    def pair_step(tag, stacks, extra, after):
        n = len(stacks)
        lands = [lax.empty((N_DEV // 2,) + a.shape[1:], a.dtype) for a in stacks]
        copies = _to_sibling(range(n), range(n, 2 * n))
        if extra is not None:
            copies.append(Copy(2 * n, None, 2 * n + 1, None, 1, 0, 0))
            lands_extra = [extra, lax.empty(extra.shape, extra.dtype)]
        else:
            lands_extra = []
        copies = _numbered(copies)
        sem, bufs, token = _comm_call(tag + "_pair_start", list(stacks) + lands + lands_extra,
                                      start=(copies, (len(copies),)), after=after)
        return sem[0], bufs, copies, token

    def pair_finish(tag, sem, bufs, copies, n, after):
        _, bufs, _ = _comm_call(tag + "_pair_wait", bufs, wait=(sem, copies), after=after)
        sums, lands = _pair_sum(bufs[:n], bufs[n:2 * n], place)
        if len(bufs) > 2 * n:
            total, land = _pair_add(bufs[2 * n], bufs[2 * n + 1], place)
            sums.append(total)
            lands.append(land)
        return sums, lands

    def chip_step(tag, sums, lands, n):
        copies = []
        for i in range(n):
            copies += _in_chunks(_to_chips((i,), (len(sums) + i,)), sums[i].shape[1])
        if len(sums) > n:
            copies += _in_chunks([Copy(n, None, 2 * n + 1, _chip, p, 0, 0) for p in SAME_CORE], sums[n].shape[0])
        copies = _numbered(copies)
        sem, bufs, token = _comm_call(tag + "_chip_start", list(sums) + list(lands), start=(copies, (len(copies),)))
        return sem[0], bufs, copies, token

    def chip_finish(tag, sem, bufs, copies, after):
        _, bufs, _ = _comm_call(tag + "_chip_wait", bufs, wait=(sem, copies), after=after)
        return bufs[len(bufs) // 2:]

    d_gate, d_up, d_down, dn2 = _ffn_bwd(n2, dh2b, g_act, u_act, w_gate_f, w_up_f, w_down_f)
    ffn_stacks = [d.reshape(N_DEV, D_FF // N_DEV, D_MODEL) for d in (d_gate, d_up, d_down)]
    sem, bufs, copies, token = pair_step("ffn", ffn_stacks, None, None)
    dh1, dy_pool, dy_lru, d_out, d_ln2, d_gnp, d_gnl = _bwd_out(dn2, dh2, h1, y_pool, y_lru, gn_pool_g, gn_lru_g,
                                                                 w_out_f, ln2_g, token)
    ffn_sums, ffn_lands = pair_finish("ffn", sem, bufs, copies, 3, dh1)
    ffn_sem, ffn_bufs, ffn_copies, token = chip_step("ffn", ffn_sums, ffn_lands, 3)
    (du_pool, du_lru, du_gate, d_pw, d_ps, d_cw, d_cb, d_wa, d_ba, d_wi, d_bi, d_lam) = _mixer_bwd(
        u_pool, u_lru, u_gate, h, dy_pool, dy_lru, pool_w[0], pool_scale, conv_w_f, conv_b,
        wa_bd, b_a, wi_bd, b_i, lam, token)
    grad_x, d_in, d_ln1 = _bwd_in(xs, dh1, du_pool, du_lru, du_gate, ln1_g, w_in_f)

    vec_grads = dict(ln1_g=d_ln1, ln2_g=d_ln2, lnf_g=d_lnf, pool_scale=d_ps, conv_b=d_cb, b_a=d_ba, b_i=d_bi,
                     lam=d_lam, gn_pool_g=d_gnp, gn_lru_g=d_gnl)
    packed = _pack_small([vec_grads[k] for k, _ in VECTORS], [d_pw, d_wa, d_wi], d_cw, sq)
    rest_stacks = [d_in.reshape(N_DEV, IN_WIDTH // N_DEV, D_MODEL), d_out.reshape(N_DEV, D_MODEL // N_DEV, D_MODEL)]
    sem, bufs, copies, _ = pair_step("rest", rest_stacks, packed, None)
    rest_sums, rest_lands = pair_finish("rest", sem, bufs, copies, 2, None)
    rest_sem, rest_bufs, rest_copies, token = chip_step("rest", rest_sums, rest_lands, 2)

    results = {}
    r_gate, r_up, r_down = chip_finish("ffn", ffn_sem, ffn_bufs, ffn_copies, token)
    for name, parts in (("w_ffn_gate", r_gate), ("w_ffn_up", r_up), ("w_ffn_down", r_down)):
        outs = _reduce_adam(parts, shard[name](weights[name]), shard[name](mom1[name]), shard[name](mom2[name]),
                            "adam_" + name)
        results[name] = tuple(unshard[name](o) for o in outs)
    r_in, r_out, r_small = chip_finish("rest", rest_sem, rest_bufs, rest_copies, outs[0])
    for name, parts in (("w_in", r_in), ("w_out", r_out)):
        outs = _reduce_adam(parts, shard[name](weights[name]), shard[name](mom1[name]), shard[name](mom2[name]),
                            "adam_" + name)
        results[name] = tuple(unshard[name](o) for o in outs)

    def as_row(a, width):
        return a.reshape(1, width)

    def as_matrix(a):
        return a.reshape(MATRIX_ROWS, SLAB)

    def as_heads(a):
        return a.reshape(MATRIX_ROWS, 64)

    vec_out, pool_out, g_wa_bd, g_wi_bd, g_conv, loss_11 = _small_reduce_adam(
        r_small, [as_row(weights[k], w) for k, w in VECTORS], [as_row(mom1[k], w) for k, w in VECTORS],
        [as_row(mom2[k], w) for k, w in VECTORS], [as_matrix(t["pool_w"]) for t in (weights, mom1, mom2)])
    for (k, _), outs in zip(VECTORS, vec_out):
        results[k] = tuple(o.reshape(weights[k].shape) for o in outs)
    results["pool_w"] = tuple(o.reshape(pool_w.shape) for o in pool_out)
    my_columns = conv_w.shape[-1]
    my_index = 4 * lax.axis_index("x") + 2 * lax.axis_index("y") + lax.axis_index("c")
    plain_names = ("w_a", "w_i", "conv_w")
    plain_grads = [as_heads(_diag_blocks(g.reshape(N_SLAB, SLAB, SLAB))) for g in (g_wa_bd, g_wi_bd)]
    plain_grads.append(lax.dynamic_slice_in_dim(g_conv, my_index * my_columns, my_columns, axis=1))
    views = (as_heads, as_heads, lambda a: a[0])
    plain_out = _plain_adam(plain_grads, *[[view(t[k]) for k, view in zip(plain_names, views)]
                                           for t in (weights, mom1, mom2)])
    for k, g, outs in zip(plain_names, plain_grads, plain_out):
        results[k] = tuple(o.reshape(weights[k].shape) for o in (g,) + outs)
    loss = loss_11[0, 0]

    order = ["ln1_g", "w_in", "pool_w", "pool_scale", "conv_w", "conv_b", "w_a", "b_a", "w_i", "b_i", "lam",
             "gn_pool_g", "gn_lru_g", "w_out", "ln2_g", "w_ffn_gate", "w_ffn_up", "w_ffn_down", "lnf_g"]
    return (loss, grad_x[None],
            *[results[k][0] for k in order], *[results[k][1] for k in order],
            *[results[k][2] for k in order], *[results[k][3] for k in order])
```

```python
from typing import Any, NamedTuple

import jax
import jax.numpy as jnp
from jax import lax
from jax.experimental import pallas as pl
from jax.experimental.pallas import tpu as pltpu

F32 = jnp.float32
BF16 = jnp.bfloat16

N_DEV = 8
D_MODEL = 1024
POOL_WIDTH = 512
LRU_WIDTH = 512
IN_WIDTH = 1536
D_FF = 2816
N_SLAB = 4
SLAB = 128
CONV_WIDTH = 4
LRU_C = 8.0
EPS = 1e-6
HALO = 16
FF_CHUNK = 256

ADAM_LR = 0.001
ADAM_B1 = 0.9
ADAM_B2 = 0.999
ADAM_EPS = 1e-08
ADAM_WD = 0.01
ADAM_STEP = 10

MIB = 1 << 20
MESH = pl.DeviceIdType.MESH


def _params(vmem_mib, n_axes=0):
    sem = ("arbitrary",) * n_axes if n_axes else None
    return pltpu.CompilerParams(dimension_semantics=sem, vmem_limit_bytes=vmem_mib * MIB)


def _mm(a, b):
    return jnp.dot(a, b, preferred_element_type=F32)


def _mm_nt(a, b):
    return lax.dot_general(a, b, (((1,), (1,)), ((), ())), preferred_element_type=F32)


def _mm_tn(a, b):
    return lax.dot_general(a, b, (((0,), (0,)), ((), ())), preferred_element_type=F32)


def _rms(x, g):
    rstd = lax.rsqrt(jnp.mean(x * x, axis=-1, keepdims=True) + EPS)
    xhat = x * rstd
    return xhat * g, xhat, rstd


def _rms_bwd(dy, xhat, rstd, g):
    gy = dy * g
    dx = rstd * (gy - xhat * jnp.mean(gy * xhat, axis=-1, keepdims=True))
    return dx, jnp.sum(dy * xhat, axis=0, keepdims=True)


def _gelu(z):
    t = jnp.tanh(0.7978845608028654 * (z + 0.044715 * z * z * z))
    return 0.5 * z * (1.0 + t), t


def _gelu_grad(z, t):
    return 0.5 * (1.0 + t) + 0.5 * z * (1.0 - t * t) * 0.7978845608028654 * (1.0 + 3.0 * 0.044715 * z * z)


def _softplus_neg(lam):
    x = -lam
    e = jnp.exp(-jnp.abs(x))
    u = 1.0 + e
    l1p = jnp.where(u == 1.0, e, jnp.log(u) * e / (u - 1.0))
    return jnp.maximum(x, 0.0) + l1p


def _expm1(x):
    p = x * (1.0 + x * (0.5 + x * (1.0 / 6.0 + x * (1.0 / 24.0 + x * (1.0 / 120.0)))))
    return jnp.where(jnp.abs(x) < 0.1, p, jnp.exp(x) - 1.0)


def _down(v, d):
    return pltpu.roll(v, d, 0)


def _up(v, d):
    return pltpu.roll(v, v.shape[0] - d, 0)


def _token_block(s):
    return 512 if s % 512 == 0 and s > 512 else 256


def _time_chunk(s):
    return 256 if s % 256 == 0 else s


def _fwd_in(x, ln1_g, w_in_t):
    s = x.shape[0]
    tb = _token_block(s)

    def body(x_ref, g_ref, w_ref, up_ref, ul_ref, ug_ref):
        n, _, _ = _rms(x_ref[...], g_ref[...])
        proj = _mm_nt(n.astype(BF16), w_ref[...])
        up_ref[...] = proj[:, :POOL_WIDTH]
        ul_ref[...] = proj[:, POOL_WIDTH:POOL_WIDTH + LRU_WIDTH]
        ug_ref[...] = proj[:, POOL_WIDTH + LRU_WIDTH:]

    out = jax.ShapeDtypeStruct((s, 512), F32)
    return pl.pallas_call(
        body, name="fwd_in", grid=(s // tb,),
        in_specs=[pl.BlockSpec((tb, D_MODEL), lambda i: (i, 0)),
                  pl.BlockSpec((1, D_MODEL), lambda i: (0, 0)),
                  pl.BlockSpec((IN_WIDTH, D_MODEL), lambda i: (0, 0))],
        out_specs=[pl.BlockSpec((tb, 512), lambda i: (i, 0))] * 3,
        out_shape=[out, out, out],
        compiler_params=_params(40, 1),
    )(x, ln1_g, w_in_t)


def _pool_denominator(t0, row, window):
    return jnp.minimum((t0 + row + 1).astype(F32), window)


def _causal_window(ext, onehot):
    s2 = ext + _down(ext, 1)
    s4 = s2 + _down(s2, 2)
    s8 = s4 + _down(s4, 4)
    s16 = s8 + _down(s8, 8)
    return (onehot[0] * s2 + onehot[1] * s4 + onehot[2] * s8 + onehot[3] * s16)[HALO:]


def _anticausal_window(ext, onehot, rows):
    s2 = ext + _up(ext, 1)
    s4 = s2 + _up(s2, 2)
    s8 = s4 + _up(s4, 4)
    s16 = s8 + _up(s8, 8)
    return (onehot[0] * s2 + onehot[1] * s4 + onehot[2] * s8 + onehot[3] * s16)[:rows]


def _conv_taps(ext):
    return [ext[HALO:], _down(ext, 1)[HALO:], _down(ext, 2)[HALO:], _down(ext, 3)[HALO:]]


def _conv(taps, cw, cb):
    return cw[3:4] * taps[0] + cw[2:3] * taps[1] + cw[1:2] * taps[2] + cw[0:1] * taps[3] + cb


def _lru_gates(xc, wa, ba, wi, bi, sp):
    xb = xc.astype(BF16)
    r = jax.nn.sigmoid(_mm(xb, wa) + ba)
    i = jax.nn.sigmoid(_mm(xb, wi) + bi)
    la = (-LRU_C) * r * sp
    a = jnp.exp(la)
    mult = jnp.sqrt(jnp.maximum(-_expm1(2.0 * la), 0.0))
    return xb, r, i, a, mult


def _scan_causal(a, b, h_prev, row, rows):
    d = 1
    while d < rows:
        head = row < d
        a_sh = jnp.where(head, 1.0, _down(a, d))
        b_sh = jnp.where(head, 0.0, _down(b, d))
        b = a * b_sh + b
        a = a * a_sh
        d *= 2
    return b + a * h_prev


def _scan_anticausal(a, b, l_next, row, rows):
    d = 1
    while d < rows:
        tail = row >= rows - d
        a_sh = jnp.where(tail, 1.0, _up(a, d))
        b_sh = jnp.where(tail, 0.0, _up(b, d))
        b = a * b_sh + b
        a = a * a_sh
        d *= 2
    return b + a * l_next


def _slab_scalars():
    slab = pl.program_id(0)
    onehot = [jnp.where(slab == k, 1.0, 0.0).astype(F32) for k in range(N_SLAB)]
    window = jnp.left_shift(jnp.int32(2), slab).astype(F32)
    return onehot, window


def _slab_specs(s):
    seq = pl.BlockSpec((s, SLAB), lambda k: (0, k))
    mat = pl.BlockSpec((1, SLAB, SLAB), lambda k: (k, 0, 0))
    vec = pl.BlockSpec((1, SLAB), lambda k: (0, k))
    taps = pl.BlockSpec((CONV_WIDTH, SLAB), lambda k: (0, k))
    return seq, mat, vec, taps


def _mixer_fwd(u_pool, u_lru, u_gate, pool_w, pool_scale, conv_w, conv_b, wa_bd, b_a, wi_bd, b_i, lam):
    s = u_pool.shape[0]
    tc = _time_chunk(s)
    n_chunks = s // tc

    def body(up_ref, ul_ref, ug_ref, pw_ref, ps_ref, cw_ref, cb_ref, wa_ref, ba_ref, wi_ref, bi_ref, lam_ref,
             yp_ref, h_ref, yl_ref):
        onehot, window = _slab_scalars()
        pw = pw_ref[0].astype(BF16)
        wa = wa_ref[0].astype(BF16)
        wi = wi_ref[0].astype(BF16)
        ps, cw, cb, ba, bi = ps_ref[...], cw_ref[...], cb_ref[...], ba_ref[...], bi_ref[...]
        sp = _softplus_neg(lam_ref[...])
        row = lax.broadcasted_iota(jnp.int32, (tc, SLAB), 0)

        def chunk(t0, ext_p, ext_l, h_prev):
            rows = pl.ds(t0, tc)
            d = _causal_window(ext_p, onehot) / _pool_denominator(t0, row, window) - ext_p[HALO:]
            yp_ref[rows, :] = _mm(d.astype(BF16), pw) * ps
            xc = _conv(_conv_taps(ext_l), cw, cb)
            _, _, i, a, mult = _lru_gates(xc, wa, ba, wi, bi, sp)
            h = _scan_causal(a, mult * (i * xc), h_prev, row, tc)
            h_ref[rows, :] = h
            yl_ref[rows, :] = h * _gelu(ug_ref[rows, :])[0]
            return h[tc - 1:tc, :]

        pad = jnp.zeros((HALO, SLAB), F32)
        h0 = chunk(0, jnp.concatenate([pad, up_ref[pl.ds(0, tc), :]], axis=0),
                   jnp.concatenate([pad, ul_ref[pl.ds(0, tc), :]], axis=0), jnp.zeros((1, SLAB), F32))

        def step(c, h_prev):
            t0 = pl.multiple_of(c * tc, tc)
            ext = pl.ds(pl.multiple_of(c * tc - HALO, HALO), tc + HALO)
            return chunk(t0, up_ref[ext, :], ul_ref[ext, :], h_prev)

        lax.fori_loop(1, n_chunks, step, h0)

    seq, mat, vec, taps = _slab_specs(s)
    out = jax.ShapeDtypeStruct((s, 512), F32)
    return pl.pallas_call(
        body, name="mixer_fwd", grid=(N_SLAB,),
        in_specs=[seq, seq, seq, mat, vec, taps, vec, mat, vec, mat, vec, vec],
        out_specs=[seq, seq, seq], out_shape=[out, out, out],
        compiler_params=_params(48, 1),
    )(u_pool, u_lru, u_gate, pool_w, pool_scale, conv_w, conv_b, wa_bd, b_a, wi_bd, b_i, lam)


def _fwd_out(x, y_pool, y_lru, gn_pool_g, gn_lru_g, w_out, ln2_g):
    s = x.shape[0]
    tb = _token_block(s)

    def body(x_ref, yp_ref, yl_ref, gp_ref, gl_ref, w_ref, g2_ref, h1_ref, n2_ref):
        mp, _, _ = _rms(yp_ref[...], gp_ref[...])
        ml, _, _ = _rms(yl_ref[...], gl_ref[...])
        h1 = x_ref[...] + _mm(mp.astype(BF16), w_ref[:POOL_WIDTH, :]) + _mm(ml.astype(BF16), w_ref[POOL_WIDTH:, :])
        h1_ref[...] = h1
        n2_ref[...] = _rms(h1, g2_ref[...])[0].astype(BF16)

    row = pl.BlockSpec((tb, D_MODEL), lambda i: (i, 0))
    half = pl.BlockSpec((tb, 512), lambda i: (i, 0))
    return pl.pallas_call(
        body, name="fwd_out", grid=(s // tb,),
        in_specs=[row, half, half, pl.BlockSpec((1, 512), lambda i: (0, 0)), pl.BlockSpec((1, 512), lambda i: (0, 0)),
                  pl.BlockSpec((D_MODEL, D_MODEL), lambda i: (0, 0)), pl.BlockSpec((1, D_MODEL), lambda i: (0, 0))],
        out_specs=[row, row],
        out_shape=[jax.ShapeDtypeStruct((s, D_MODEL), F32), jax.ShapeDtypeStruct((s, D_MODEL), BF16)],
        compiler_params=_params(40, 1),
    )(x, y_pool, y_lru, gn_pool_g, gn_lru_g, w_out, ln2_g)


def _ffn_fwd(h1, n2, target, lnf_g, w_gate, w_up, w_down):
    s = h1.shape[0]
    tb = 256
    n_ff = D_FF // FF_CHUNK

    def body(h1_ref, n2_ref, t_ref, gf_ref, wg_hbm, wu_hbm, wd_hbm,
             g_ref, u_ref, dh_ref, dhb_ref, dgf_ref, sq_ref, wg, wu, wd, sem):
        first = pl.program_id(0) == 0

        def loads(c):
            rows = pl.ds(c * FF_CHUNK, FF_CHUNK)
            return [pltpu.make_async_copy(src.at[rows, :], dst.at[rows, :], sem.at[k, c])
                    for k, (src, dst) in enumerate(((wg_hbm, wg), (wu_hbm, wu), (wd_hbm, wd)))]

        @pl.when(first)
        def _():
            for c in range(n_ff):
                for cp in loads(c):
                    cp.start()
            dgf_ref[...] = jnp.zeros_like(dgf_ref)
            sq_ref[...] = jnp.zeros_like(sq_ref)

        n2v = n2_ref[...]
        acc = jnp.zeros((tb, D_MODEL), F32)
        for c in range(n_ff):
            @pl.when(first)
            def _(c=c):
                for cp in loads(c):
                    cp.wait()

            cols = slice(c * FF_CHUNK, (c + 1) * FF_CHUNK)
            g = _mm_nt(n2v, wg[cols, :])
            u = _mm_nt(n2v, wu[cols, :])
            g_ref[:, cols] = g.astype(BF16)
            u_ref[:, cols] = u.astype(BF16)
            act = g * jax.nn.sigmoid(g) * u
            acc = acc + _mm(act.astype(BF16), wd[cols, :])
        gf = gf_ref[...]
        y, xhat, rstd = _rms(h1_ref[...] + acc, gf)
        err = y - t_ref[...]
        sq_ref[...] += jnp.sum(err * err, axis=0, keepdims=True)
        dh2, dgf = _rms_bwd(err * (1.0 / D_MODEL), xhat, rstd, gf)
        dgf_ref[...] += dgf
        dh_ref[...] = dh2
        dhb_ref[...] = dh2.astype(BF16)

    row = pl.BlockSpec((tb, D_MODEL), lambda i: (i, 0))
    ff = pl.BlockSpec((tb, D_FF), lambda i: (i, 0))
    vec = pl.BlockSpec((1, D_MODEL), lambda i: (0, 0))
    anyspace = pl.BlockSpec(memory_space=pl.ANY)
    return pl.pallas_call(
        body, name="ffn_fwd", grid=(s // tb,),
        in_specs=[row, row, row, vec, anyspace, anyspace, anyspace],
        out_specs=[ff, ff, row, row, vec, vec],
        out_shape=[jax.ShapeDtypeStruct((s, D_FF), BF16), jax.ShapeDtypeStruct((s, D_FF), BF16),
                   jax.ShapeDtypeStruct((s, D_MODEL), F32), jax.ShapeDtypeStruct((s, D_MODEL), BF16),
                   jax.ShapeDtypeStruct((1, D_MODEL), F32), jax.ShapeDtypeStruct((1, D_MODEL), F32)],
        scratch_shapes=[pltpu.VMEM((D_FF, D_MODEL), BF16), pltpu.VMEM((D_FF, D_MODEL), BF16),
                        pltpu.VMEM((D_FF, D_MODEL), BF16), pltpu.SemaphoreType.DMA((3, n_ff))],
        compiler_params=_params(56, 1),
    )(h1, n2, target, lnf_g, w_gate, w_up, w_down)


def _ffn_bwd(n2, dh2b, g, u, w_gate_t, w_up_t, w_down):
    s = n2.shape[0]
    tb = min(1024, s)
    n_ff = D_FF // FF_CHUNK
    n_tb = s // tb

    def body(n2_ref, dh_ref, g_ref, u_ref, wg_ref, wu_ref, wd_ref, dwg_ref, dwu_ref, dwd_ref, dn2_ref,
             dn2_acc, acc_g, acc_u, acc_d):
        j = pl.program_id(0)
        t = pl.program_id(1)

        @pl.when(t == 0)
        def _():
            acc_g[...] = jnp.zeros_like(acc_g)
            acc_u[...] = jnp.zeros_like(acc_u)
            acc_d[...] = jnp.zeros_like(acc_d)

        n2v = n2_ref[...]
        dh = dh_ref[...]
        gv = g_ref[...].astype(F32)
        uv = u_ref[...].astype(F32)
        sg = jax.nn.sigmoid(gv)
        silu = gv * sg
        dact = _mm_nt(dh, wd_ref[...])
        dub = (dact * silu).astype(BF16)
        dgb = (dact * uv * (sg * (1.0 + gv * (1.0 - sg)))).astype(BF16)
        acc_d[...] += _mm_tn((silu * uv).astype(BF16), dh)
        acc_g[...] += _mm_tn(dgb, n2v)
        acc_u[...] += _mm_tn(dub, n2v)
        part = _mm(jnp.concatenate([dgb, dub], axis=1), jnp.concatenate([wg_ref[...], wu_ref[...]], axis=0))
        rows = pl.ds(pl.multiple_of(t * tb, tb), tb)

        @pl.when(t == n_tb - 1)
        def _():
            dwg_ref[...] = acc_g[...].astype(BF16)
            dwu_ref[...] = acc_u[...].astype(BF16)
            dwd_ref[...] = acc_d[...].astype(BF16)

        @pl.when(j == 0)
        def _():
            dn2_acc[rows, :] = part

        @pl.when(jnp.logical_and(j > 0, j < n_ff - 1))
        def _():
            dn2_acc[rows, :] += part

        @pl.when(j == n_ff - 1)
        def _():
            dn2_ref[...] = dn2_acc[rows, :] + part

    row = pl.BlockSpec((tb, D_MODEL), lambda j, t: (t, 0))
    act = pl.BlockSpec((tb, FF_CHUNK), lambda j, t: (t, j))
    w_row = pl.BlockSpec((FF_CHUNK, D_MODEL), lambda j, t: (j, 0))
    last = pl.BlockSpec((tb, D_MODEL), lambda j, t: (jnp.where(j == n_ff - 1, t, 0), 0))
    grad = jax.ShapeDtypeStruct((D_FF, D_MODEL), BF16)
    chunk_acc = pltpu.VMEM((FF_CHUNK, D_MODEL), F32)
    return pl.pallas_call(
        body, name="ffn_bwd", grid=(n_ff, n_tb),
        in_specs=[row, row, act, act, w_row, w_row, w_row],
        out_specs=[w_row, w_row, w_row, last],
        out_shape=[grad, grad, grad, jax.ShapeDtypeStruct((s, D_MODEL), F32)],
        scratch_shapes=[pltpu.VMEM((s, D_MODEL), F32), chunk_acc, chunk_acc, chunk_acc],
        compiler_params=_params(56, 2),
    )(n2, dh2b, g, u, w_gate_t, w_up_t, w_down)


def _bwd_out(dn2, dh2, h1, y_pool, y_lru, gn_pool_g, gn_lru_g, w_out, ln2_g, after):
    s = h1.shape[0]
    tb = _token_block(s)

    def body(dn2_ref, dh2_ref, h1_ref, yp_ref, yl_ref, gp_ref, gl_ref, w_ref, g2_ref, _after,
             dh1_ref, dyp_ref, dyl_ref, dwb_ref, dg2_ref, dgp_ref, dgl_ref, dw_ref):
        @pl.when(pl.program_id(0) == 0)
        def _():
            dw_ref[...] = jnp.zeros_like(dw_ref)
            dg2_ref[...] = jnp.zeros_like(dg2_ref)
            dgp_ref[...] = jnp.zeros_like(dgp_ref)
            dgl_ref[...] = jnp.zeros_like(dgl_ref)

        g2 = g2_ref[...]
        _, xhat2, rstd2 = _rms(h1_ref[...], g2)
        dres, dg2 = _rms_bwd(dn2_ref[...], xhat2, rstd2, g2)
        dg2_ref[...] += dg2
        dh1 = dh2_ref[...] + dres
        dh1_ref[...] = dh1
        dh1b = dh1.astype(BF16)
        gp, gl = gp_ref[...], gl_ref[...]
        mp, xhat_p, rstd_p = _rms(yp_ref[...], gp)
        ml, xhat_l, rstd_l = _rms(yl_ref[...], gl)
        dw_ref[:POOL_WIDTH, :] += _mm_tn(mp.astype(BF16), dh1b)
        dw_ref[POOL_WIDTH:, :] += _mm_tn(ml.astype(BF16), dh1b)
        dyp, dgp = _rms_bwd(_mm_nt(dh1b, w_ref[:POOL_WIDTH, :]), xhat_p, rstd_p, gp)
        dyl, dgl = _rms_bwd(_mm_nt(dh1b, w_ref[POOL_WIDTH:, :]), xhat_l, rstd_l, gl)
        dyp_ref[...] = dyp
        dyl_ref[...] = dyl
        dgp_ref[...] += dgp
        dgl_ref[...] += dgl

        @pl.when(pl.program_id(0) == s // tb - 1)
        def _():
            dwb_ref[...] = dw_ref[...].astype(BF16)

    row = pl.BlockSpec((tb, D_MODEL), lambda i: (i, 0))
    half = pl.BlockSpec((tb, 512), lambda i: (i, 0))
    vec = pl.BlockSpec((1, D_MODEL), lambda i: (0, 0))
    hvec = pl.BlockSpec((1, 512), lambda i: (0, 0))
    mat = pl.BlockSpec((D_MODEL, D_MODEL), lambda i: (0, 0))
    return pl.pallas_call(
        body, name="bwd_out", grid=(s // tb,),
        in_specs=[row, row, row, half, half, hvec, hvec, mat, vec, pl.BlockSpec(memory_space=pl.ANY)],
        out_specs=[row, half, half, mat, vec, hvec, hvec],
        out_shape=[jax.ShapeDtypeStruct((s, D_MODEL), F32), jax.ShapeDtypeStruct((s, 512), F32),
                   jax.ShapeDtypeStruct((s, 512), F32), jax.ShapeDtypeStruct((D_MODEL, D_MODEL), BF16),
                   jax.ShapeDtypeStruct((1, D_MODEL), F32), jax.ShapeDtypeStruct((1, 512), F32),
                   jax.ShapeDtypeStruct((1, 512), F32)],
        scratch_shapes=[pltpu.VMEM((D_MODEL, D_MODEL), F32)],
        compiler_params=_params(48, 1),
    )(dn2, dh2, h1, y_pool, y_lru, gn_pool_g, gn_lru_g, w_out, ln2_g, after)


def _mixer_bwd(u_pool, u_lru, u_gate, h, dy_pool, dy_lru,
               pool_w, pool_scale, conv_w, conv_b, wa_bd, b_a, wi_bd, b_i, lam, after):
    s = u_pool.shape[0]
    tc = _time_chunk(s)
    n_chunks = s // tc

    def body(up_ref, ul_ref, ug_ref, h_ref, dyp_ref, dyl_ref,
             pw_ref, ps_ref, cw_ref, cb_ref, wa_ref, ba_ref, wi_ref, bi_ref, lam_ref, _after,
             dup_ref, dul_ref, dug_ref, dpw_ref, dps_ref, dcw_ref, dcb_ref, dwa_ref, dba_ref, dwi_ref, dbi_ref, dlam_ref):
        onehot, window = _slab_scalars()
        pw = pw_ref[0].astype(BF16)
        wa = wa_ref[0].astype(BF16)
        wi = wi_ref[0].astype(BF16)
        ps, cw, cb, ba, bi = ps_ref[...], cw_ref[...], cb_ref[...], ba_ref[...], bi_ref[...]
        lam_v = lam_ref[...]
        sp = _softplus_neg(lam_v)
        row = lax.broadcasted_iota(jnp.int32, (tc, SLAB), 0)
        for ref in (dpw_ref, dps_ref, dcw_ref, dcb_ref, dwa_ref, dba_ref, dwi_ref, dbi_ref, dlam_ref):
            ref[...] = jnp.zeros_like(ref)

        def chunk(t0, ext_p, ext_l, ext_h, carry):
            l_next, a_next, dxc_next, ddn_next = carry
            rows = pl.ds(t0, tc)
            taps = _conv_taps(ext_l)
            xc = _conv(taps, cw, cb)
            xb, r, i, a, mult = _lru_gates(xc, wa, ba, wi, bi, sp)
            hv = ext_h[HALO:]
            h_before = _down(ext_h, 1)[HALO:]
            ug = ug_ref[rows, :]
            dyl = dyl_ref[rows, :]
            gel, th = _gelu(ug)
            dug_ref[rows, :] = dyl * hv * _gelu_grad(ug, th)
            a_after = jnp.where(row == tc - 1, a_next, _up(a, 1))
            l = _scan_anticausal(a_after, dyl * gel, l_next, row, tc)
            dmult = l * (i * xc)
            di = l * mult * xc
            dxc = l * mult * i
            dla = (l * h_before) * a - jnp.where(mult > 0.0, dmult * (a * a) / mult, 0.0)
            dlam_ref[...] += jnp.sum(dla * r, axis=0, keepdims=True)
            dpa = (dla * ((-LRU_C) * sp)) * (r * (1.0 - r))
            dpi = di * (i * (1.0 - i))
            dpab = dpa.astype(BF16)
            dpib = dpi.astype(BF16)
            dwa_ref[0] += _mm_tn(xb, dpab)
            dwi_ref[0] += _mm_tn(xb, dpib)
            dba_ref[...] += jnp.sum(dpa, axis=0, keepdims=True)
            dbi_ref[...] += jnp.sum(dpi, axis=0, keepdims=True)
            dxc = dxc + _mm_nt(dpab, wa) + _mm_nt(dpib, wi)
            ext_d = jnp.concatenate([dxc, dxc_next], axis=0)
            dul_ref[rows, :] = (cw[3:4] * dxc + cw[2:3] * _up(ext_d, 1)[:tc]
                                + cw[1:2] * _up(ext_d, 2)[:tc] + cw[0:1] * _up(ext_d, 3)[:tc])
            for k in range(CONV_WIDTH):
                dcw_ref[k:k + 1, :] += jnp.sum(dxc * taps[CONV_WIDTH - 1 - k], axis=0, keepdims=True)
            dcb_ref[...] += jnp.sum(dxc, axis=0, keepdims=True)
            denom = _pool_denominator(t0, row, window)
            db = (_causal_window(ext_p, onehot) / denom - ext_p[HALO:]).astype(BF16)
            dyp = dyp_ref[rows, :]
            dps_ref[...] += jnp.sum(dyp * _mm(db, pw), axis=0, keepdims=True)
            dys = (dyp * ps).astype(BF16)
            dpw_ref[0] += _mm_tn(db, dys)
            dd = _mm_nt(dys, pw)
            ddn = dd / denom
            ext_q = jnp.concatenate([ddn, ddn_next], axis=0)
            dup_ref[rows, :] = _anticausal_window(ext_q, onehot, tc) - dd
            return l[0:1, :], a[0:1, :], dxc[0:8, :], ddn[0:HALO, :]

        def step(k, carry):
            c = n_chunks - 1 - k
            t0 = pl.multiple_of(c * tc, tc)
            ext = pl.ds(pl.multiple_of(c * tc - HALO, HALO), tc + HALO)
            return chunk(t0, up_ref[ext, :], ul_ref[ext, :], h_ref[ext, :], carry)

        carry = (jnp.zeros((1, SLAB), F32), jnp.zeros((1, SLAB), F32),
                 jnp.zeros((8, SLAB), F32), jnp.zeros((HALO, SLAB), F32))
        carry = lax.fori_loop(0, n_chunks - 1, step, carry)
        pad = jnp.zeros((HALO, SLAB), F32)
        first = pl.ds(0, tc)
        chunk(0, jnp.concatenate([pad, up_ref[first, :]], axis=0), jnp.concatenate([pad, ul_ref[first, :]], axis=0),
              jnp.concatenate([pad, h_ref[first, :]], axis=0), carry)
        dlam_ref[...] = dlam_ref[...] * (LRU_C * jax.nn.sigmoid(-lam_v))

    seq, mat, vec, taps = _slab_specs(s)
    full = jax.ShapeDtypeStruct((s, 512), F32)
    mats = jax.ShapeDtypeStruct((N_SLAB, SLAB, SLAB), F32)
    vecs = jax.ShapeDtypeStruct((1, 512), F32)
    return pl.pallas_call(
        body, name="mixer_bwd", grid=(N_SLAB,),
        in_specs=[seq] * 6 + [mat, vec, taps, vec, mat, vec, mat, vec, vec, pl.BlockSpec(memory_space=pl.ANY)],
        out_specs=[seq, seq, seq, mat, vec, taps, vec, mat, vec, mat, vec, vec],
        out_shape=[full, full, full, mats, vecs, jax.ShapeDtypeStruct((CONV_WIDTH, 512), F32), vecs,
                   mats, vecs, mats, vecs, vecs],
        compiler_params=_params(56, 1),
    )(u_pool, u_lru, u_gate, h, dy_pool, dy_lru, pool_w, pool_scale, conv_w, conv_b, wa_bd, b_a, wi_bd, b_i, lam,
      after)


def _bwd_in(x, dh1, du_pool, du_lru, du_gate, ln1_g, w_in_t, after):
    s = x.shape[0]
    tb = _token_block(s)

    def body(x_ref, dh1_ref, dup_ref, dul_ref, dug_ref, g_ref, w_ref, _after, dx_ref, dwb_ref, dg_ref, dw_ref):
        @pl.when(pl.program_id(0) == 0)
        def _():
            dw_ref[...] = jnp.zeros_like(dw_ref)
            dg_ref[...] = jnp.zeros_like(dg_ref)

        g1 = g_ref[...]
        n, xhat, rstd = _rms(x_ref[...], g1)
        nb = n.astype(BF16)
        dn = jnp.zeros((tb, D_MODEL), F32)
        for k, ref in enumerate((dup_ref, dul_ref, dug_ref)):
            rows = slice(k * 512, (k + 1) * 512)
            db = ref[...].astype(BF16)
            dw_ref[rows, :] += _mm_tn(db, nb)
            dn = dn + _mm(db, w_ref[rows, :])
        dx, dg1 = _rms_bwd(dn, xhat, rstd, g1)
        dx_ref[...] = dh1_ref[...] + dx
        dg_ref[...] += dg1

        @pl.when(pl.program_id(0) == s // tb - 1)
        def _():
            dwb_ref[...] = dw_ref[...].astype(BF16)

    row = pl.BlockSpec((tb, D_MODEL), lambda i: (i, 0))
    half = pl.BlockSpec((tb, 512), lambda i: (i, 0))
    vec = pl.BlockSpec((1, D_MODEL), lambda i: (0, 0))
    mat = pl.BlockSpec((IN_WIDTH, D_MODEL), lambda i: (0, 0))
    return pl.pallas_call(
        body, name="bwd_in", grid=(s // tb,),
        in_specs=[row, row, half, half, half, vec, mat, pl.BlockSpec(memory_space=pl.ANY)],
        out_specs=[row, mat, vec],
        out_shape=[jax.ShapeDtypeStruct((s, D_MODEL), F32), jax.ShapeDtypeStruct((IN_WIDTH, D_MODEL), BF16),
                   jax.ShapeDtypeStruct((1, D_MODEL), F32)],
        scratch_shapes=[pltpu.VMEM((IN_WIDTH, D_MODEL), F32)],
        compiler_params=_params(48, 1),
    )(x, dh1, du_pool, du_lru, du_gate, ln1_g, w_in_t, after)


def _mesh_position():
    x, y, c = lax.axis_index("x"), lax.axis_index("y"), lax.axis_index("c")
    return x, y, c, 4 * x + 2 * y + c


def _peer(x, y, c, p):
    px = 1 - x if p & 4 else x
    py = 1 - y if p & 2 else y
    pc = 1 - c if p & 1 else c
    return (px, py, pc), 4 * px + 2 * py + pc


HBM_SPEC = pl.BlockSpec(memory_space=pltpu.HBM)
SEM_SPEC = pl.BlockSpec(memory_space=pltpu.SEMAPHORE)
DATAFLOW = pltpu.SideEffectType.DATAFLOW_SIDE_EFFECTING


class Copy(NamedTuple):
    src: int
    src_at: Any
    dst: int
    dst_at: Any
    peer: int
    group: int
    slot: int
    rows: Any = None


SIBLING = (1,)
SAME_CORE = (2, 4, 6)
EVERYONE = tuple(range(1, N_DEV))


def _same(index):
    return index


def _chip(index):
    return jnp.right_shift(index, 1)


def _fan_out(srcs, lands, peers, group):
    return [Copy(s, None, d, _same, p, group, N_DEV * i + p) for i, (s, d) in enumerate(zip(srcs, lands)) for p in peers]


def _scatter(stacks, lands, peers):
    return [Copy(s, lambda me, p=p: jnp.bitwise_xor(me, p), d, _same, p, 0, 0)
            for s, d in zip(stacks, lands) for p in peers]


ICI_CHUNK_ROWS = 32


def _in_chunks(copies, n_rows):
    return [cp._replace(rows=(first, ICI_CHUNK_ROWS)) for cp in copies for first in range(0, n_rows, ICI_CHUNK_ROWS)]


def _numbered(copies, group=0):
    return [cp._replace(group=group, slot=i) for i, cp in enumerate(copies)]


def _relay(lands, peers):
    return [Copy(b, lambda s, q=q: jnp.bitwise_xor(s, q), b, lambda s, q=q: jnp.bitwise_xor(s, q), 1, 0, N_DEV * i + q)
            for i, b in enumerate(lands) for q in peers]


def _to_sibling(stacks, lands):
    return [Copy(s, lambda me, k=k: 2 * k + 1 - jnp.bitwise_and(me, 1), d, lambda me, k=k: k, 1, 0, 4 * i + k)
            for i, (s, d) in enumerate(zip(stacks, lands)) for k in range(N_DEV // 2)]


def _to_chips(sums, lands):
    return [Copy(s, lambda me, p=p: jnp.bitwise_xor(_chip(me), p // 2), d, _chip, p, 0, 4 * i + p // 2)
            for i, (s, d) in enumerate(zip(sums, lands)) for p in SAME_CORE]


def _comm_call(name, bufs, wait=None, local=(), start=None, after=None):
    nb = len(bufs)
    slots = list(start[1]) if start else []
    n_out_sem = 2 * len(slots)

    def body(*refs):
        b = refs[:nb]
        at = nb
        if wait:
            w_send, w_recv = refs[at], refs[at + 1]
            at += 2
        if after is not None:
            at += 1
        out_sems = refs[at:at + n_out_sem]
        token = refs[at + n_out_sem + nb]
        x, y, c, me = _mesh_position()

        def part(i, row_of, sender, rows=None):
            ref = b[i] if row_of is None else b[i].at[row_of(sender)]
            return ref if rows is None else ref.at[pl.ds(rows[0], rows[1])]

        if wait:
            for cp in wait[1]:
                peer, peer_index = _peer(x, y, c, cp.peer)
                arrival = pltpu.make_async_remote_copy(part(cp.src, cp.src_at, me, cp.rows),
                                                       part(cp.dst, cp.dst_at, peer_index, cp.rows),
                                                       w_send.at[cp.slot], w_recv.at[cp.slot],
                                                       device_id=peer, device_id_type=MESH)
                arrival.wait_send()
                arrival.wait_recv()
        if local:
            local_sems = refs[-1]
            copies = [pltpu.make_async_copy(part(sb, sf, me), part(db, df, me), local_sems.at[k])
                      for k, (sb, sf, db, df) in enumerate(local)]
            for cp in copies:
                cp.start()
            for cp in copies:
                cp.wait()
        if start:
            for cp in start[0]:
                peer, _ = _peer(x, y, c, cp.peer)
                pltpu.make_async_remote_copy(part(cp.src, cp.src_at, me, cp.rows), part(cp.dst, cp.dst_at, me, cp.rows),
                                             out_sems[2 * cp.group].at[cp.slot], out_sems[2 * cp.group + 1].at[cp.slot],
                                             device_id=peer, device_id_type=MESH).start()
        token[...] = jnp.zeros_like(token)

    sem_shapes = []
    for n_slots in slots:
        sem_shapes += [pltpu.SemaphoreType.DMA((n_slots,))] * 2
    operands = [pltpu.with_memory_space_constraint(a, pltpu.HBM) for a in bufs]
    in_specs = [HBM_SPEC] * nb
    if wait:
        operands += list(wait[0])
        in_specs += [SEM_SPEC, SEM_SPEC]
    if after is not None:
        operands.append(after)
        in_specs.append(pl.BlockSpec(memory_space=pl.ANY))
    outs = pl.pallas_call(
        body, name=name, in_specs=in_specs,
        out_specs=[SEM_SPEC] * n_out_sem + [HBM_SPEC] * nb + [pl.BlockSpec(memory_space=pltpu.VMEM)],
        out_shape=sem_shapes + [pltpu.HBM(a.shape, a.dtype) for a in bufs] + [jax.ShapeDtypeStruct((8, SLAB), F32)],
        input_output_aliases={i: n_out_sem + i for i in range(nb)},
        scratch_shapes=[pltpu.SemaphoreType.DMA((len(local),))] if local else [],
        compiler_params=pltpu.CompilerParams(has_side_effects=DATAFLOW),
    )(*operands)
    sems = [(outs[2 * k], outs[2 * k + 1]) for k in range(len(slots))]
    return sems, list(outs[n_out_sem:n_out_sem + nb]), outs[-1]


def _pair_sum(stacks, lands, place):
    n = len(stacks)

    def body(place_ref, *refs):
        k = pl.program_id(0)
        for m in range(n):
            mine, theirs, out, land = refs[m], refs[n + m], refs[2 * n + m], refs[3 * n + m]
            total = (mine[0, 0].astype(F32) + theirs[0].astype(F32)).astype(out.dtype)
            out[0] = total

            @pl.when(k == place_ref[1])
            def _():
                land[0] = total

    in_specs = [pl.BlockSpec((1, 1) + a.shape[1:], lambda k, place_ref: (k, place_ref[0], 0, 0)) for a in stacks]
    in_specs += [pl.BlockSpec((1,) + a.shape[1:], lambda k, place_ref: (k, 0, 0)) for a in lands]
    out_specs = [pl.BlockSpec((1,) + a.shape[1:], lambda k, place_ref: (k, 0, 0)) for a in lands]
    out_specs += [pl.BlockSpec((1,) + a.shape[1:], lambda k, place_ref: (place_ref[1], 0, 0)) for a in lands]
    outs = pl.pallas_call(
        body, name="pair_sum_" + "_".join(str(a.shape[1]) for a in stacks),
        grid_spec=pltpu.PrefetchScalarGridSpec(num_scalar_prefetch=1, grid=(N_DEV // 2,), in_specs=in_specs,
                                               out_specs=out_specs),
        out_shape=[jax.ShapeDtypeStruct(a.shape, a.dtype) for a in lands] * 2,
        compiler_params=_params(40, 1),
    )(place, *[a.reshape((N_DEV // 2, 2) + a.shape[1:]) for a in stacks], *lands)
    return list(outs[:n]), list(outs[n:])


def _pair_add(a, b, place):
    def body(place_ref, a_ref, b_ref, out, land):
        del place_ref
        out[...] = a_ref[...] + b_ref[...]
        land[0] = out[...]

    whole = pl.BlockSpec(a.shape, lambda i, place_ref: (0, 0))
    return pl.pallas_call(
        body, name="pair_add",
        grid_spec=pltpu.PrefetchScalarGridSpec(
            num_scalar_prefetch=1, grid=(1,), in_specs=[whole, whole],
            out_specs=[whole, pl.BlockSpec((1,) + a.shape, lambda i, place_ref: (place_ref[1], 0, 0))]),
        out_shape=[jax.ShapeDtypeStruct(a.shape, a.dtype), jax.ShapeDtypeStruct((N_DEV // 2,) + a.shape, a.dtype)],
    )(place, a, b)


def _reduce_adam(parts, w, m, v, name):
    rows, cols = w.shape
    n_parts = parts.shape[0]
    rb = rows
    for cand in (256, 176, 128):
        if rows % cand == 0 and rows > cand:
            rb = cand
            break

    def body(p_ref, w_ref, m_ref, v_ref, g_out, d_out, m_out, v_out):
        g = p_ref[0].astype(F32)
        for j in range(1, n_parts):
            g = g + p_ref[j].astype(F32)
        g_out[...] = g
        d_out[...], m_out[...], v_out[...] = _adam(g, w_ref[...], m_ref[...], v_ref[...])

    blk = pl.BlockSpec((rb, cols), lambda i: (i, 0))
    out = jax.ShapeDtypeStruct((rows, cols), F32)
    return pl.pallas_call(
        body, name=name, grid=(rows // rb,),
        in_specs=[pl.BlockSpec((n_parts, rb, cols), lambda i: (0, i, 0)), blk, blk, blk],
        out_specs=[blk] * 4, out_shape=[out] * 4,
        compiler_params=_params(40, 1),
    )(parts, w, m, v)


def _cols_from_stack(stack):
    n, r, c = stack.shape
    return jnp.transpose(stack, (1, 0, 2)).reshape(r, n * c)


def _block_diag(w):
    z = jnp.zeros((N_SLAB, 64, 64), w.dtype)
    pairs = w.reshape(N_SLAB, 2, 64, 64)
    top = jnp.concatenate([pairs[:, 0], z], axis=2)
    bottom = jnp.concatenate([z, pairs[:, 1]], axis=2)
    return jnp.concatenate([top, bottom], axis=1)


def _diag_blocks(w):
    return jnp.stack([w[:, :64, :64], w[:, 64:, 64:]], axis=1).reshape(8, 64, 64)


def _adam(g, w, m, v):
    m_new = ADAM_B1 * m + (1.0 - ADAM_B1) * g
    v_new = ADAM_B2 * v + (1.0 - ADAM_B2) * (g * g)
    m_hat = m_new / (1.0 - ADAM_B1 ** ADAM_STEP)
    v_hat = v_new / (1.0 - ADAM_B2 ** ADAM_STEP)
    return (-ADAM_LR) * (m_hat / (jnp.sqrt(v_hat) + ADAM_EPS) + ADAM_WD * w), m_new, v_new


WIDE = ("ln2_g", "lnf_g")
HALF = ("pool_scale", "conv_b", "b_a", "b_i", "lam", "gn_pool_g", "gn_lru_g")
VECTORS = [(k, D_MODEL) for k in WIDE] + [(k, 512) for k in HALF]
VECTOR_ROWS = sum(width // SLAB for _, width in VECTORS)
LOSS_ROW = -(-VECTOR_ROWS // 8) * 8
CONV_AT = LOSS_ROW + 8
CONV_LANES = LRU_WIDTH // SLAB
PACK_F_ROWS = CONV_AT + CONV_WIDTH * CONV_LANES
MATRIX_ROWS = N_SLAB * SLAB
HEAD = SLAB // 2
GATE_ROWS = N_SLAB * HEAD
PACK_B_ROWS = MATRIX_ROWS + 2 * GATE_ROWS


def _pack_small(vectors, pool_g, wa_g, wi_g, conv, sq):
    n_vec = len(vectors)

    def body(*refs):
        vec = refs[:n_vec]
        pw_ref, wa_ref, wi_ref, cw_ref, sq_ref, out, out_b = refs[n_vec:]
        out[...] = jnp.zeros_like(out)
        row = 0
        for ref, (_, width) in zip(vec, VECTORS):
            for k in range(width // SLAB):
                out[row:row + 1, :] = ref[:, k * SLAB:(k + 1) * SLAB]
                row += 1
        for tap in range(CONV_WIDTH):
            for k in range(CONV_LANES):
                at = CONV_AT + tap * CONV_LANES + k
                out[at:at + 1, :] = cw_ref[tap:tap + 1, k * SLAB:(k + 1) * SLAB]
        total = sq_ref[:, 0:SLAB]
        for k in range(1, D_MODEL // SLAB):
            total = total + sq_ref[:, k * SLAB:(k + 1) * SLAB]
        out[LOSS_ROW:LOSS_ROW + 1, :] = total
        left = lax.broadcasted_iota(jnp.int32, (HEAD, SLAB), 1) < HEAD
        for s in range(N_SLAB):
            out_b[s * SLAB:(s + 1) * SLAB, :] = pw_ref[s].astype(BF16)
            for i, ref in enumerate((wa_ref, wi_ref)):
                at = MATRIX_ROWS + i * GATE_ROWS + s * HEAD
                out_b[at:at + HEAD, :] = jnp.where(left, ref[s, 0:HEAD, :], ref[s, HEAD:SLAB, :]).astype(BF16)

    return pl.pallas_call(
        body, name="pack_small",
        out_shape=[jax.ShapeDtypeStruct((PACK_F_ROWS, SLAB), F32), jax.ShapeDtypeStruct((PACK_B_ROWS, SLAB), BF16)],
    )(*vectors, pool_g, wa_g, wi_g, conv, sq)


def _small_reduce_adam(parts, parts_b, vec_w, vec_m, vec_v, pool_wmv):
    n_vec = len(VECTORS)
    n_parts = parts.shape[0]

    def body(*refs):
        p_ref, pb_ref = refs[0], refs[1]
        w_refs, m_refs, v_refs = (refs[2 + k * n_vec:2 + (k + 1) * n_vec] for k in range(3))
        pw_w, pw_m, pw_v = refs[2 + 3 * n_vec:5 + 3 * n_vec]
        outs = refs[5 + 3 * n_vec:-1]
        total = refs[-1]
        total[...] = p_ref[0]
        for j in range(1, n_parts):
            total[...] += p_ref[j]
        row = 0
        for i, (_, width) in enumerate(VECTORS):
            n_rows = width // SLAB
            g = jnp.concatenate([total[row + k:row + k + 1, :] for k in range(n_rows)], axis=1)
            row += n_rows
            d, m_new, v_new = _adam(g, w_refs[i][...], m_refs[i][...], v_refs[i][...])
            for ref, val in zip(outs[4 * i:4 * i + 4], (g, d, m_new, v_new)):
                ref[...] = val
        tail = outs[4 * n_vec:]

        def summed(first, count):
            g = pb_ref[0, first:first + count, :].astype(F32)
            for j in range(1, n_parts):
                g = g + pb_ref[j, first:first + count, :].astype(F32)
            return g

        g = summed(0, MATRIX_ROWS)
        d, m_new, v_new = _adam(g, pw_w[...], pw_m[...], pw_v[...])
        for ref, val in zip(tail[0:4], (g, d, m_new, v_new)):
            ref[...] = val
        tail[4][...] = summed(MATRIX_ROWS, GATE_ROWS)
        tail[5][...] = summed(MATRIX_ROWS + GATE_ROWS, GATE_ROWS)
        for tap in range(CONV_WIDTH):
            at = CONV_AT + tap * CONV_LANES
            tail[6][tap:tap + 1, :] = jnp.concatenate([total[at + k:at + k + 1, :] for k in range(CONV_LANES)], axis=1)
        tail[7][...] = (0.5 / D_MODEL) * jnp.sum(total[LOSS_ROW:LOSS_ROW + 1, :], axis=1, keepdims=True)

    out_shape = []
    for _, width in VECTORS:
        out_shape += [jax.ShapeDtypeStruct((1, width), F32)] * 4
    out_shape += [jax.ShapeDtypeStruct((MATRIX_ROWS, SLAB), F32)] * 4 + [jax.ShapeDtypeStruct((GATE_ROWS, SLAB), F32)] * 2
    out_shape += [jax.ShapeDtypeStruct((CONV_WIDTH, LRU_WIDTH), F32), jax.ShapeDtypeStruct((1, 1), F32)]
    outs = pl.pallas_call(
        body, name="adam_small", out_shape=out_shape,
        scratch_shapes=[pltpu.VMEM((PACK_F_ROWS, SLAB), F32)],
        compiler_params=_params(40),
    )(parts, parts_b, *vec_w, *vec_m, *vec_v, *pool_wmv)
    vec_out = [tuple(outs[4 * i:4 * i + 4]) for i in range(n_vec)]
    tail = outs[4 * n_vec:]
    return vec_out, tuple(tail[0:4]), tail[4], tail[5], tail[6], tail[7]


def _plain_adam(grads, ws, ms, vs):
    n = len(grads)

    def body(*refs):
        ins, outs = refs[:4 * n], refs[4 * n:]
        for i in range(n):
            d, m_new, v_new = _adam(ins[i][...], ins[n + i][...], ins[2 * n + i][...], ins[3 * n + i][...])
            for ref, val in zip(outs[3 * i:3 * i + 3], (d, m_new, v_new)):
                ref[...] = val

    out_shape = []
    for g in grads:
        out_shape += [jax.ShapeDtypeStruct(g.shape, F32)] * 3
    outs = pl.pallas_call(body, name="adam_plain", out_shape=out_shape)(*grads, *ws, *ms, *vs)
    return [tuple(outs[3 * i:3 * i + 3]) for i in range(n)]


def kernel(x, ln1_g, w_in, pool_w, pool_scale, conv_w, conv_b, w_a, b_a, w_i, b_i, lam, gn_pool_g, gn_lru_g, w_out, ln2_g, w_ffn_gate, w_ffn_up, w_ffn_down, lnf_g, loss_target, m_ln1_g, m_w_in, m_pool_w, m_pool_scale, m_conv_w, m_conv_b, m_w_a, m_b_a, m_w_i, m_b_i, m_lam, m_gn_pool_g, m_gn_lru_g, m_w_out, m_ln2_g, m_w_ffn_gate, m_w_ffn_up, m_w_ffn_down, m_lnf_g, v_ln1_g, v_w_in, v_pool_w, v_pool_scale, v_conv_w, v_conv_b, v_w_a, v_b_a, v_w_i, v_b_i, v_lam, v_gn_pool_g, v_gn_lru_g, v_w_out, v_ln2_g, v_w_ffn_gate, v_w_ffn_up, v_w_ffn_down, v_lnf_g):
    weights = dict(ln1_g=ln1_g, w_in=w_in, pool_w=pool_w, pool_scale=pool_scale, conv_w=conv_w, conv_b=conv_b,
                   w_a=w_a, b_a=b_a, w_i=w_i, b_i=b_i, lam=lam, gn_pool_g=gn_pool_g, gn_lru_g=gn_lru_g,
                   w_out=w_out, ln2_g=ln2_g, w_ffn_gate=w_ffn_gate, w_ffn_up=w_ffn_up, w_ffn_down=w_ffn_down,
                   lnf_g=lnf_g)
    mom1 = dict(ln1_g=m_ln1_g, w_in=m_w_in, pool_w=m_pool_w, pool_scale=m_pool_scale, conv_w=m_conv_w,
                conv_b=m_conv_b, w_a=m_w_a, b_a=m_b_a, w_i=m_w_i, b_i=m_b_i, lam=m_lam, gn_pool_g=m_gn_pool_g,
                gn_lru_g=m_gn_lru_g, w_out=m_w_out, ln2_g=m_ln2_g, w_ffn_gate=m_w_ffn_gate,
                w_ffn_up=m_w_ffn_up, w_ffn_down=m_w_ffn_down, lnf_g=m_lnf_g)
    mom2 = dict(ln1_g=v_ln1_g, w_in=v_w_in, pool_w=v_pool_w, pool_scale=v_pool_scale, conv_w=v_conv_w,
                conv_b=v_conv_b, w_a=v_w_a, b_a=v_b_a, w_i=v_w_i, b_i=v_b_i, lam=v_lam, gn_pool_g=v_gn_pool_g,
                gn_lru_g=v_gn_lru_g, w_out=v_w_out, ln2_g=v_ln2_g, w_ffn_gate=v_w_ffn_gate,
                w_ffn_up=v_w_ffn_up, w_ffn_down=v_w_ffn_down, lnf_g=v_lnf_g)

    xs = x[0]
    target = loss_target[0]

    shard = dict(w_in=lambda a: a[0].T, w_ffn_gate=lambda a: a[0].T, w_ffn_up=lambda a: a[0].T,
                 w_out=lambda a: a[0], w_ffn_down=lambda a: a[0], conv_w=lambda a: a[0])
    unshard = dict(w_in=lambda a: a.T[None], w_ffn_gate=lambda a: a.T[None], w_ffn_up=lambda a: a.T[None],
                   w_out=lambda a: a[None], w_ffn_down=lambda a: a[None], conv_w=lambda a: a[None])

    gathered = ("w_in", "conv_w", "w_out", "w_ffn_gate", "w_ffn_up", "w_ffn_down")
    sources = [shard[k](weights[k]) if k == "conv_w" else shard[k](weights[k]).astype(BF16) for k in gathered]
    my_index = 4 * lax.axis_index("x") + 2 * lax.axis_index("y") + lax.axis_index("c")
    lands = [lax.dynamic_update_index_in_dim(lax.empty((N_DEV,) + a.shape, a.dtype), a, my_index, 0) for a in sources]

    def in_copies(srcs, dsts, group=0):
        w_in_rows = IN_WIDTH // N_DEV
        return _numbered(_in_chunks(_fan_out(srcs[:1], dsts[:1], EVERYONE, 0), w_in_rows)
                         + _fan_out(srcs[1:], dsts[1:], EVERYONE, 0), group)

    def out_copies(src, dst, group=0):
        return _numbered(_in_chunks(_fan_out((src,), (dst,), EVERYONE, 0), D_MODEL // N_DEV), group)

    def ffn_copies(srcs, dsts, group=0):
        return _numbered(_in_chunks(_fan_out(srcs, dsts, SAME_CORE, 0), D_FF // N_DEV)
                         + _fan_out(srcs, dsts, SIBLING, 0), group)

    start = in_copies((0, 1), (6, 7), 0) + out_copies(2, 8, 1) + ffn_copies((3, 4, 5), (9, 10, 11), 2)
    slots = [sum(cp.group == g for cp in start) for g in range(3)]
    sems, bufs, _ = _comm_call("gather_start", sources + lands, start=(start, slots))
    sources, lands = bufs[:6], bufs[6:]
    _, bufs, _ = _comm_call("gather_wait_in", sources[0:2] + lands[0:2], wait=(sems[0], in_copies((0, 1), (2, 3))))
    w_in_f = bufs[2].reshape(IN_WIDTH, D_MODEL)
    conv_w_f = _cols_from_stack(bufs[3])

    wa_bd = _block_diag(w_a[0])
    wi_bd = _block_diag(w_i[0])
    lnf_row = lnf_g.reshape(1, D_MODEL)

    u_pool, u_lru, u_gate = _fwd_in(xs, ln1_g, w_in_f)
    y_pool, h, y_lru = _mixer_fwd(u_pool, u_lru, u_gate, pool_w[0], pool_scale, conv_w_f, conv_b,
                                  wa_bd, b_a, wi_bd, b_i, lam)
    _, bufs, _ = _comm_call("gather_wait_out", [sources[2], lands[2]], wait=(sems[1], out_copies(0, 1)), after=y_pool)
    w_out_f = bufs[1].reshape(D_MODEL, D_MODEL)
    h1, n2 = _fwd_out(xs, y_pool, y_lru, gn_pool_g, gn_lru_g, w_out_f, ln2_g)
    relay = _numbered(_relay((3, 4, 5), SAME_CORE))
    relay_sems, bufs, _ = _comm_call("gather_relay_ffn", sources[3:6] + lands[3:6],
                                     wait=(sems[2], ffn_copies((0, 1, 2), (3, 4, 5))),
                                     start=(relay, (len(relay),)), after=n2)
    _, bufs, _ = _comm_call("gather_wait_ffn", bufs[3:6],
                            wait=(relay_sems[0], _numbered(_relay((0, 1, 2), SAME_CORE))))
    w_gate_f = bufs[0].reshape(D_FF, D_MODEL)
    w_up_f = bufs[1].reshape(D_FF, D_MODEL)
    w_down_f = bufs[2].reshape(D_FF, D_MODEL)
    g_act, u_act, dh2, dh2b, d_lnf, sq = _ffn_fwd(h1, n2, target, lnf_row, w_gate_f, w_up_f, w_down_f)

    place = jnp.stack([lax.axis_index("c"), 2 * lax.axis_index("x") + lax.axis_index("y")]).astype(jnp.int32)

    def pair_step(tag, stacks, extra, after):
        n = len(stacks)
        lands = [lax.empty((N_DEV // 2,) + a.shape[1:], a.dtype) for a in stacks]
        copies = _to_sibling(range(n), range(n, 2 * n))
        if extra is not None:
            copies.append(Copy(2 * n, None, 2 * n + 1, None, 1, 0, 0))
            lands_extra = [extra, lax.empty(extra.shape, extra.dtype)]
        else:
            lands_extra = []
        copies = _numbered(copies)
        sem, bufs, token = _comm_call(tag + "_pair_start", list(stacks) + lands + lands_extra,
                                      start=(copies, (len(copies),)), after=after)
        return sem[0], bufs, copies, token

    def pair_finish(tag, sem, bufs, copies, n, after):
        _, bufs, _ = _comm_call(tag + "_pair_wait", bufs, wait=(sem, copies), after=after)
        sums, lands = _pair_sum(bufs[:n], bufs[n:2 * n], place)
        if len(bufs) > 2 * n:
            total, land = _pair_add(bufs[2 * n], bufs[2 * n + 1], place)
            sums.append(total)
            lands.append(land)
        return sums, lands

    def chip_step(tag, sums, lands, n):
        copies = []
        for i in range(n):
            copies += _in_chunks(_to_chips((i,), (len(sums) + i,)), sums[i].shape[1])
        if len(sums) > n:
            copies += _in_chunks([Copy(n, None, 2 * n + 1, _chip, p, 0, 0) for p in SAME_CORE], sums[n].shape[0])
        copies = _numbered(copies)
        sem, bufs, token = _comm_call(tag + "_chip_start", list(sums) + list(lands), start=(copies, (len(copies),)))
        return sem[0], bufs, copies, token

    def chip_finish(tag, sem, bufs, copies, after):
        _, bufs, _ = _comm_call(tag + "_chip_wait", bufs, wait=(sem, copies), after=after)
        return bufs[len(bufs) // 2:]

    d_gate, d_up, d_down, dn2 = _ffn_bwd(n2, dh2b, g_act, u_act, w_gate_f, w_up_f, w_down_f)
    ffn_stacks = [d.reshape(N_DEV, D_FF // N_DEV, D_MODEL) for d in (d_gate, d_up, d_down)]
    sem, bufs, copies, token = pair_step("ffn", ffn_stacks, None, None)
    dh1, dy_pool, dy_lru, d_out, d_ln2, d_gnp, d_gnl = _bwd_out(dn2, dh2, h1, y_pool, y_lru, gn_pool_g, gn_lru_g,
                                                                 w_out_f, ln2_g, token)
    ffn_sums, ffn_lands = pair_finish("ffn", sem, bufs, copies, 3, dh1)
    ffn_sem, ffn_bufs, ffn_copies, token = chip_step("ffn", ffn_sums, ffn_lands, 3)
    (du_pool, du_lru, du_gate, d_pw, d_ps, d_cw, d_cb, d_wa, d_ba, d_wi, d_bi, d_lam) = _mixer_bwd(
        u_pool, u_lru, u_gate, h, dy_pool, dy_lru, pool_w[0], pool_scale, conv_w_f, conv_b,
        wa_bd, b_a, wi_bd, b_i, lam, token)

    def direct(tag, stacks, wholes, after):
        sources = list(stacks) + list(wholes)
        n, n_st = len(sources), len(stacks)
        lands = [lax.dynamic_update_index_in_dim(
            lax.empty(a.shape if i < n_st else (N_DEV,) + a.shape, a.dtype),
            lax.dynamic_index_in_dim(a, my_index, 0, keepdims=False) if i < n_st else a, my_index, 0)
            for i, a in enumerate(sources)]
        copies = _numbered(_scatter(range(n_st), range(n, n + n_st), EVERYONE)
                           + _fan_out(range(n_st, n), range(n + n_st, 2 * n), EVERYONE, 0))
        sem, bufs, token = _comm_call(tag + "_start", sources + lands, start=(copies, (len(copies),)), after=after)
        return (tag, sem[0], bufs, copies), token

    def direct_finish(started, after):
        tag, sem, bufs, copies = started
        _, bufs, _ = _comm_call(tag + "_wait", bufs, wait=(sem, copies), after=after)
        return bufs[len(bufs) // 2:]

    vec_grads = dict(ln2_g=d_ln2, lnf_g=d_lnf, pool_scale=d_ps, conv_b=d_cb, b_a=d_ba, b_i=d_bi,
                     lam=d_lam, gn_pool_g=d_gnp, gn_lru_g=d_gnl)
    packed, packed_b = _pack_small([vec_grads[k] for k, _ in VECTORS], d_pw, d_wa, d_wi, d_cw, sq)
    small_started, token = direct("small", [d_out.reshape(N_DEV, D_MODEL // N_DEV, D_MODEL)], [packed, packed_b], None)
    grad_x, d_in, d_ln1 = _bwd_in(xs, dh1, du_pool, du_lru, du_gate, ln1_g, w_in_f, token)
    in_started, token = direct("in", [d_in.reshape(N_DEV, IN_WIDTH // N_DEV, D_MODEL)], [d_ln1], None)

    results = {}

    def reduce_adam(name, parts):
        outs = _reduce_adam(parts, shard[name](weights[name]), shard[name](mom1[name]), shard[name](mom2[name]),
                            "adam_" + name)
        results[name] = tuple(unshard[name](o) for o in outs)
        return outs[0]

    shard["ln1_g"] = unshard["ln1_g"] = lambda a: a
    r_gate, r_up, r_down = chip_finish("ffn", ffn_sem, ffn_bufs, ffn_copies, token)
    for name, parts in (("w_ffn_gate", r_gate), ("w_ffn_up", r_up), ("w_ffn_down", r_down)):
        done = reduce_adam(name, parts)
    r_out, r_small, r_small_b = direct_finish(small_started, done)
    done = reduce_adam("w_out", r_out)

    def as_row(a, width):
        return a.reshape(1, width)

    def as_matrix(a):
        return a.reshape(MATRIX_ROWS, SLAB)

    def as_heads(a):
        return a.reshape(2 * GATE_ROWS, HEAD)

    def heads_apart(g):
        return jnp.transpose(g.reshape(N_SLAB, HEAD, 2, HEAD), (0, 2, 1, 3)).reshape(2 * GATE_ROWS, HEAD)

    vec_out, pool_out, g_wa, g_wi, g_conv, loss_11 = _small_reduce_adam(
        r_small, r_small_b, [as_row(weights[k], w) for k, w in VECTORS], [as_row(mom1[k], w) for k, w in VECTORS],
        [as_row(mom2[k], w) for k, w in VECTORS], [as_matrix(t["pool_w"]) for t in (weights, mom1, mom2)])
    for (k, _), outs in zip(VECTORS, vec_out):
        results[k] = tuple(o.reshape(weights[k].shape) for o in outs)
    results["pool_w"] = tuple(o.reshape(pool_w.shape) for o in pool_out)
    my_columns = conv_w.shape[-1]
    plain_names = ("w_a", "w_i", "conv_w")
    plain_grads = [heads_apart(g) for g in (g_wa, g_wi)]
    plain_grads.append(lax.dynamic_slice_in_dim(g_conv, my_index * my_columns, my_columns, axis=1))
    views = (as_heads, as_heads, lambda a: a[0])
    plain_out = _plain_adam(plain_grads, *[[view(t[k]) for k, view in zip(plain_names, views)]
                                           for t in (weights, mom1, mom2)])
    for k, g, outs in zip(plain_names, plain_grads, plain_out):
        results[k] = tuple(o.reshape(weights[k].shape) for o in (g,) + outs)
    loss = loss_11[0, 0]
    r_in, r_ln1 = direct_finish(in_started, plain_out[0][0])
    reduce_adam("w_in", r_in)
    reduce_adam("ln1_g", r_ln1)

    order = ["ln1_g", "w_in", "pool_w", "pool_scale", "conv_w", "conv_b", "w_a", "b_a", "w_i", "b_i", "lam",
             "gn_pool_g", "gn_lru_g", "w_out", "ln2_g", "w_ffn_gate", "w_ffn_up", "w_ffn_down", "lnf_g"]
    return (loss, grad_x[None],
            *[results[k][0] for k in order], *[results[k][1] for k in order],
            *[results[k][2] for k in order], *[results[k][3] for k in order])
```

```python
from typing import Any, NamedTuple

import jax
import jax.numpy as jnp
from jax import lax
from jax.experimental import pallas as pl
from jax.experimental.pallas import tpu as pltpu

F32 = jnp.float32
BF16 = jnp.bfloat16

N_DEV = 8
D_MODEL = 1024
POOL_WIDTH = 512
LRU_WIDTH = 512
IN_WIDTH = 1536
D_FF = 2816
N_SLAB = 4
SLAB = 128
CONV_WIDTH = 4
LRU_C = 8.0
EPS = 1e-6
HALO = 16
FF_CHUNK = 256

ADAM_LR = 0.001
ADAM_B1 = 0.9
ADAM_B2 = 0.999
ADAM_EPS = 1e-08
ADAM_WD = 0.01
ADAM_STEP = 10

MIB = 1 << 20
MESH = pl.DeviceIdType.MESH


def _params(vmem_mib, n_axes=0):
    sem = ("arbitrary",) * n_axes if n_axes else None
    return pltpu.CompilerParams(dimension_semantics=sem, vmem_limit_bytes=vmem_mib * MIB)


def _mm(a, b):
    return jnp.dot(a, b, preferred_element_type=F32)


def _mm_nt(a, b):
    return lax.dot_general(a, b, (((1,), (1,)), ((), ())), preferred_element_type=F32)


def _mm_tn(a, b):
    return lax.dot_general(a, b, (((0,), (0,)), ((), ())), preferred_element_type=F32)


def _rms(x, g):
    rstd = lax.rsqrt(jnp.mean(x * x, axis=-1, keepdims=True) + EPS)
    xhat = x * rstd
    return xhat * g, xhat, rstd


def _rms_bwd(dy, xhat, rstd, g):
    gy = dy * g
    dx = rstd * (gy - xhat * jnp.mean(gy * xhat, axis=-1, keepdims=True))
    return dx, jnp.sum(dy * xhat, axis=0, keepdims=True)


def _gelu(z):
    t = jnp.tanh(0.7978845608028654 * (z + 0.044715 * z * z * z))
    return 0.5 * z * (1.0 + t), t


def _gelu_grad(z, t):
    return 0.5 * (1.0 + t) + 0.5 * z * (1.0 - t * t) * 0.7978845608028654 * (1.0 + 3.0 * 0.044715 * z * z)


def _softplus_neg(lam):
    x = -lam
    e = jnp.exp(-jnp.abs(x))
    u = 1.0 + e
    l1p = jnp.where(u == 1.0, e, jnp.log(u) * e / (u - 1.0))
    return jnp.maximum(x, 0.0) + l1p


def _expm1(x):
    p = x * (1.0 + x * (0.5 + x * (1.0 / 6.0 + x * (1.0 / 24.0 + x * (1.0 / 120.0)))))
    return jnp.where(jnp.abs(x) < 0.1, p, jnp.exp(x) - 1.0)


def _down(v, d):
    return pltpu.roll(v, d, 0)


def _up(v, d):
    return pltpu.roll(v, v.shape[0] - d, 0)


def _token_block(s):
    return 512 if s % 512 == 0 and s > 512 else 256


def _time_chunk(s):
    return 256 if s % 256 == 0 else s


def _fwd_in(x, ln1_g, w_in_t):
    s = x.shape[0]
    tb = _token_block(s)

    def body(x_ref, g_ref, w_ref, up_ref, ul_ref, ug_ref):
        n, _, _ = _rms(x_ref[...], g_ref[...])
        proj = _mm_nt(n.astype(BF16), w_ref[...])
        up_ref[...] = proj[:, :POOL_WIDTH]
        ul_ref[...] = proj[:, POOL_WIDTH:POOL_WIDTH + LRU_WIDTH]
        ug_ref[...] = proj[:, POOL_WIDTH + LRU_WIDTH:]

    out = jax.ShapeDtypeStruct((s, 512), F32)
    return pl.pallas_call(
        body, name="fwd_in", grid=(s // tb,),
        in_specs=[pl.BlockSpec((tb, D_MODEL), lambda i: (i, 0)),
                  pl.BlockSpec((1, D_MODEL), lambda i: (0, 0)),
                  pl.BlockSpec((IN_WIDTH, D_MODEL), lambda i: (0, 0))],
        out_specs=[pl.BlockSpec((tb, 512), lambda i: (i, 0))] * 3,
        out_shape=[out, out, out],
        compiler_params=_params(40, 1),
    )(x, ln1_g, w_in_t)


def _pool_denominator(t0, row, window):
    return jnp.minimum((t0 + row + 1).astype(F32), window)


def _causal_window(ext, onehot):
    s2 = ext + _down(ext, 1)
    s4 = s2 + _down(s2, 2)
    s8 = s4 + _down(s4, 4)
    s16 = s8 + _down(s8, 8)
    return (onehot[0] * s2 + onehot[1] * s4 + onehot[2] * s8 + onehot[3] * s16)[HALO:]


def _anticausal_window(ext, onehot, rows):
    s2 = ext + _up(ext, 1)
    s4 = s2 + _up(s2, 2)
    s8 = s4 + _up(s4, 4)
    s16 = s8 + _up(s8, 8)
    return (onehot[0] * s2 + onehot[1] * s4 + onehot[2] * s8 + onehot[3] * s16)[:rows]


def _conv_taps(ext):
    return [ext[HALO:], _down(ext, 1)[HALO:], _down(ext, 2)[HALO:], _down(ext, 3)[HALO:]]


def _conv(taps, cw, cb):
    return cw[3:4] * taps[0] + cw[2:3] * taps[1] + cw[1:2] * taps[2] + cw[0:1] * taps[3] + cb


def _lru_gates(xc, wa, ba, wi, bi, sp):
    xb = xc.astype(BF16)
    r = jax.nn.sigmoid(_mm(xb, wa) + ba)
    i = jax.nn.sigmoid(_mm(xb, wi) + bi)
    la = (-LRU_C) * r * sp
    a = jnp.exp(la)
    mult = jnp.sqrt(jnp.maximum(-_expm1(2.0 * la), 0.0))
    return xb, r, i, a, mult


def _scan_causal(a, b, h_prev, row, rows):
    d = 1
    while d < rows:
        head = row < d
        a_sh = jnp.where(head, 1.0, _down(a, d))
        b_sh = jnp.where(head, 0.0, _down(b, d))
        b = a * b_sh + b
        a = a * a_sh
        d *= 2
    return b + a * h_prev


def _scan_anticausal(a, b, l_next, row, rows):
    d = 1
    while d < rows:
        tail = row >= rows - d
        a_sh = jnp.where(tail, 1.0, _up(a, d))
        b_sh = jnp.where(tail, 0.0, _up(b, d))
        b = a * b_sh + b
        a = a * a_sh
        d *= 2
    return b + a * l_next


def _slab_scalars():
    slab = pl.program_id(0)
    onehot = [jnp.where(slab == k, 1.0, 0.0).astype(F32) for k in range(N_SLAB)]
    window = jnp.left_shift(jnp.int32(2), slab).astype(F32)
    return onehot, window


def _slab_specs(s):
    seq = pl.BlockSpec((s, SLAB), lambda k: (0, k))
    mat = pl.BlockSpec((1, SLAB, SLAB), lambda k: (k, 0, 0))
    vec = pl.BlockSpec((1, SLAB), lambda k: (0, k))
    taps = pl.BlockSpec((CONV_WIDTH, SLAB), lambda k: (0, k))
    return seq, mat, vec, taps


def _mixer_fwd(u_pool, u_lru, u_gate, pool_w, pool_scale, conv_w, conv_b, wa_bd, b_a, wi_bd, b_i, lam):
    s = u_pool.shape[0]
    tc = _time_chunk(s)
    n_chunks = s // tc

    def body(up_ref, ul_ref, ug_ref, pw_ref, ps_ref, cw_ref, cb_ref, wa_ref, ba_ref, wi_ref, bi_ref, lam_ref,
             yp_ref, h_ref, yl_ref):
        onehot, window = _slab_scalars()
        pw = pw_ref[0].astype(BF16)
        wa = wa_ref[0].astype(BF16)
        wi = wi_ref[0].astype(BF16)
        ps, cw, cb, ba, bi = ps_ref[...], cw_ref[...], cb_ref[...], ba_ref[...], bi_ref[...]
        sp = _softplus_neg(lam_ref[...])
        row = lax.broadcasted_iota(jnp.int32, (tc, SLAB), 0)

        def chunk(t0, ext_p, ext_l, h_prev):
            rows = pl.ds(t0, tc)
            d = _causal_window(ext_p, onehot) / _pool_denominator(t0, row, window) - ext_p[HALO:]
            yp_ref[rows, :] = _mm(d.astype(BF16), pw) * ps
            xc = _conv(_conv_taps(ext_l), cw, cb)
            _, _, i, a, mult = _lru_gates(xc, wa, ba, wi, bi, sp)
            h = _scan_causal(a, mult * (i * xc), h_prev, row, tc)
            h_ref[rows, :] = h
            yl_ref[rows, :] = h * _gelu(ug_ref[rows, :])[0]
            return h[tc - 1:tc, :]

        pad = jnp.zeros((HALO, SLAB), F32)
        h0 = chunk(0, jnp.concatenate([pad, up_ref[pl.ds(0, tc), :]], axis=0),
                   jnp.concatenate([pad, ul_ref[pl.ds(0, tc), :]], axis=0), jnp.zeros((1, SLAB), F32))

        def step(c, h_prev):
            t0 = pl.multiple_of(c * tc, tc)
            ext = pl.ds(pl.multiple_of(c * tc - HALO, HALO), tc + HALO)
            return chunk(t0, up_ref[ext, :], ul_ref[ext, :], h_prev)

        lax.fori_loop(1, n_chunks, step, h0)

    seq, mat, vec, taps = _slab_specs(s)
    out = jax.ShapeDtypeStruct((s, 512), F32)
    return pl.pallas_call(
        body, name="mixer_fwd", grid=(N_SLAB,),
        in_specs=[seq, seq, seq, mat, vec, taps, vec, mat, vec, mat, vec, vec],
        out_specs=[seq, seq, seq], out_shape=[out, out, out],
        compiler_params=_params(48, 1),
    )(u_pool, u_lru, u_gate, pool_w, pool_scale, conv_w, conv_b, wa_bd, b_a, wi_bd, b_i, lam)


def _fwd_out(x, y_pool, y_lru, gn_pool_g, gn_lru_g, w_out, ln2_g):
    s = x.shape[0]
    tb = _token_block(s)

    def body(x_ref, yp_ref, yl_ref, gp_ref, gl_ref, w_ref, g2_ref, h1_ref, n2_ref):
        mp, _, _ = _rms(yp_ref[...], gp_ref[...])
        ml, _, _ = _rms(yl_ref[...], gl_ref[...])
        h1 = x_ref[...] + _mm(mp.astype(BF16), w_ref[:POOL_WIDTH, :]) + _mm(ml.astype(BF16), w_ref[POOL_WIDTH:, :])
        h1_ref[...] = h1
        n2_ref[...] = _rms(h1, g2_ref[...])[0].astype(BF16)

    row = pl.BlockSpec((tb, D_MODEL), lambda i: (i, 0))
    half = pl.BlockSpec((tb, 512), lambda i: (i, 0))
    return pl.pallas_call(
        body, name="fwd_out", grid=(s // tb,),
        in_specs=[row, half, half, pl.BlockSpec((1, 512), lambda i: (0, 0)), pl.BlockSpec((1, 512), lambda i: (0, 0)),
                  pl.BlockSpec((D_MODEL, D_MODEL), lambda i: (0, 0)), pl.BlockSpec((1, D_MODEL), lambda i: (0, 0))],
        out_specs=[row, row],
        out_shape=[jax.ShapeDtypeStruct((s, D_MODEL), F32), jax.ShapeDtypeStruct((s, D_MODEL), BF16)],
        compiler_params=_params(40, 1),
    )(x, y_pool, y_lru, gn_pool_g, gn_lru_g, w_out, ln2_g)


def _ffn_fwd(h1, n2, target, lnf_g, w_gate, w_up, w_down):
    s = h1.shape[0]
    tb = 256
    n_ff = D_FF // FF_CHUNK

    def body(h1_ref, n2_ref, t_ref, gf_ref, wg_hbm, wu_hbm, wd_hbm,
             g_ref, u_ref, dh_ref, dhb_ref, dgf_ref, sq_ref, wg, wu, wd, sem):
        @pl.when(pl.program_id(0) == 0)
        def _():
            loads = [pltpu.make_async_copy(src, dst, sem.at[k])
                     for k, (src, dst) in enumerate(((wg_hbm, wg), (wu_hbm, wu), (wd_hbm, wd)))]
            for cp in loads:
                cp.start()
            for cp in loads:
                cp.wait()
            dgf_ref[...] = jnp.zeros_like(dgf_ref)
            sq_ref[...] = jnp.zeros_like(sq_ref)

        n2v = n2_ref[...]
        acc = jnp.zeros((tb, D_MODEL), F32)
        for c in range(n_ff):
            cols = slice(c * FF_CHUNK, (c + 1) * FF_CHUNK)
            g = _mm_nt(n2v, wg[cols, :])
            u = _mm_nt(n2v, wu[cols, :])
            g_ref[:, cols] = g.astype(BF16)
            u_ref[:, cols] = u.astype(BF16)
            act = g * jax.nn.sigmoid(g) * u
            acc = acc + _mm(act.astype(BF16), wd[cols, :])
        gf = gf_ref[...]
        y, xhat, rstd = _rms(h1_ref[...] + acc, gf)
        err = y - t_ref[...]
        sq_ref[...] += jnp.sum(err * err, axis=0, keepdims=True)
        dh2, dgf = _rms_bwd(err * (1.0 / D_MODEL), xhat, rstd, gf)
        dgf_ref[...] += dgf
        dh_ref[...] = dh2
        dhb_ref[...] = dh2.astype(BF16)

    row = pl.BlockSpec((tb, D_MODEL), lambda i: (i, 0))
    ff = pl.BlockSpec((tb, D_FF), lambda i: (i, 0))
    vec = pl.BlockSpec((1, D_MODEL), lambda i: (0, 0))
    anyspace = pl.BlockSpec(memory_space=pl.ANY)
    return pl.pallas_call(
        body, name="ffn_fwd", grid=(s // tb,),
        in_specs=[row, row, row, vec, anyspace, anyspace, anyspace],
        out_specs=[ff, ff, row, row, vec, vec],
        out_shape=[jax.ShapeDtypeStruct((s, D_FF), BF16), jax.ShapeDtypeStruct((s, D_FF), BF16),
                   jax.ShapeDtypeStruct((s, D_MODEL), F32), jax.ShapeDtypeStruct((s, D_MODEL), BF16),
                   jax.ShapeDtypeStruct((1, D_MODEL), F32), jax.ShapeDtypeStruct((1, D_MODEL), F32)],
        scratch_shapes=[pltpu.VMEM((D_FF, D_MODEL), BF16), pltpu.VMEM((D_FF, D_MODEL), BF16),
                        pltpu.VMEM((D_FF, D_MODEL), BF16), pltpu.SemaphoreType.DMA((3,))],
        compiler_params=_params(56, 1),
    )(h1, n2, target, lnf_g, w_gate, w_up, w_down)


def _ffn_bwd(n2, dh2b, g, u, w_gate_t, w_up_t, w_down):
    s = n2.shape[0]
    tb = min(1024, s)
    n_ff = D_FF // FF_CHUNK
    n_tb = s // tb

    def body(n2_ref, dh_ref, g_ref, u_ref, wg_ref, wu_ref, wd_ref, dwg_ref, dwu_ref, dwd_ref, dn2_ref,
             dn2_acc, acc_g, acc_u, acc_d):
        j = pl.program_id(0)
        t = pl.program_id(1)

        @pl.when(t == 0)
        def _():
            acc_g[...] = jnp.zeros_like(acc_g)
            acc_u[...] = jnp.zeros_like(acc_u)
            acc_d[...] = jnp.zeros_like(acc_d)

        n2v = n2_ref[...]
        dh = dh_ref[...]
        gv = g_ref[...].astype(F32)
        uv = u_ref[...].astype(F32)
        sg = jax.nn.sigmoid(gv)
        silu = gv * sg
        dact = _mm_nt(dh, wd_ref[...])
        dub = (dact * silu).astype(BF16)
        dgb = (dact * uv * (sg * (1.0 + gv * (1.0 - sg)))).astype(BF16)
        acc_d[...] += _mm_tn((silu * uv).astype(BF16), dh)
        acc_g[...] += _mm_tn(dgb, n2v)
        acc_u[...] += _mm_tn(dub, n2v)
        part = _mm(jnp.concatenate([dgb, dub], axis=1), jnp.concatenate([wg_ref[...], wu_ref[...]], axis=0))
        rows = pl.ds(pl.multiple_of(t * tb, tb), tb)

        @pl.when(t == n_tb - 1)
        def _():
            dwg_ref[...] = acc_g[...].astype(BF16)
            dwu_ref[...] = acc_u[...].astype(BF16)
            dwd_ref[...] = acc_d[...].astype(BF16)

        @pl.when(j == 0)
        def _():
            dn2_acc[rows, :] = part

        @pl.when(jnp.logical_and(j > 0, j < n_ff - 1))
        def _():
            dn2_acc[rows, :] += part

        @pl.when(j == n_ff - 1)
        def _():
            dn2_ref[...] = dn2_acc[rows, :] + part

    row = pl.BlockSpec((tb, D_MODEL), lambda j, t: (t, 0))
    act = pl.BlockSpec((tb, FF_CHUNK), lambda j, t: (t, j))
    w_row = pl.BlockSpec((FF_CHUNK, D_MODEL), lambda j, t: (j, 0))
    last = pl.BlockSpec((tb, D_MODEL), lambda j, t: (jnp.where(j == n_ff - 1, t, 0), 0))
    grad = jax.ShapeDtypeStruct((D_FF, D_MODEL), BF16)
    chunk_acc = pltpu.VMEM((FF_CHUNK, D_MODEL), F32)
    return pl.pallas_call(
        body, name="ffn_bwd", grid=(n_ff, n_tb),
        in_specs=[row, row, act, act, w_row, w_row, w_row],
        out_specs=[w_row, w_row, w_row, last],
        out_shape=[grad, grad, grad, jax.ShapeDtypeStruct((s, D_MODEL), F32)],
        scratch_shapes=[pltpu.VMEM((s, D_MODEL), F32), chunk_acc, chunk_acc, chunk_acc],
        compiler_params=_params(56, 2),
    )(n2, dh2b, g, u, w_gate_t, w_up_t, w_down)


def _bwd_out(dn2, dh2, h1, y_pool, y_lru, gn_pool_g, gn_lru_g, w_out, ln2_g, after):
    s = h1.shape[0]
    tb = _token_block(s)

    def body(dn2_ref, dh2_ref, h1_ref, yp_ref, yl_ref, gp_ref, gl_ref, w_ref, g2_ref, _after,
             dh1_ref, dyp_ref, dyl_ref, dwb_ref, dg2_ref, dgp_ref, dgl_ref, dw_ref):
        @pl.when(pl.program_id(0) == 0)
        def _():
            dw_ref[...] = jnp.zeros_like(dw_ref)
            dg2_ref[...] = jnp.zeros_like(dg2_ref)
            dgp_ref[...] = jnp.zeros_like(dgp_ref)
            dgl_ref[...] = jnp.zeros_like(dgl_ref)

        g2 = g2_ref[...]
        _, xhat2, rstd2 = _rms(h1_ref[...], g2)
        dres, dg2 = _rms_bwd(dn2_ref[...], xhat2, rstd2, g2)
        dg2_ref[...] += dg2
        dh1 = dh2_ref[...] + dres
        dh1_ref[...] = dh1
        dh1b = dh1.astype(BF16)
        gp, gl = gp_ref[...], gl_ref[...]
        mp, xhat_p, rstd_p = _rms(yp_ref[...], gp)
        ml, xhat_l, rstd_l = _rms(yl_ref[...], gl)
        dw_ref[:POOL_WIDTH, :] += _mm_tn(mp.astype(BF16), dh1b)
        dw_ref[POOL_WIDTH:, :] += _mm_tn(ml.astype(BF16), dh1b)
        dyp, dgp = _rms_bwd(_mm_nt(dh1b, w_ref[:POOL_WIDTH, :]), xhat_p, rstd_p, gp)
        dyl, dgl = _rms_bwd(_mm_nt(dh1b, w_ref[POOL_WIDTH:, :]), xhat_l, rstd_l, gl)
        dyp_ref[...] = dyp
        dyl_ref[...] = dyl
        dgp_ref[...] += dgp
        dgl_ref[...] += dgl

        @pl.when(pl.program_id(0) == s // tb - 1)
        def _():
            dwb_ref[...] = dw_ref[...].astype(BF16)

    row = pl.BlockSpec((tb, D_MODEL), lambda i: (i, 0))
    half = pl.BlockSpec((tb, 512), lambda i: (i, 0))
    vec = pl.BlockSpec((1, D_MODEL), lambda i: (0, 0))
    hvec = pl.BlockSpec((1, 512), lambda i: (0, 0))
    mat = pl.BlockSpec((D_MODEL, D_MODEL), lambda i: (0, 0))
    return pl.pallas_call(
        body, name="bwd_out", grid=(s // tb,),
        in_specs=[row, row, row, half, half, hvec, hvec, mat, vec, pl.BlockSpec(memory_space=pl.ANY)],
        out_specs=[row, half, half, mat, vec, hvec, hvec],
        out_shape=[jax.ShapeDtypeStruct((s, D_MODEL), F32), jax.ShapeDtypeStruct((s, 512), F32),
                   jax.ShapeDtypeStruct((s, 512), F32), jax.ShapeDtypeStruct((D_MODEL, D_MODEL), BF16),
                   jax.ShapeDtypeStruct((1, D_MODEL), F32), jax.ShapeDtypeStruct((1, 512), F32),
                   jax.ShapeDtypeStruct((1, 512), F32)],
        scratch_shapes=[pltpu.VMEM((D_MODEL, D_MODEL), F32)],
        compiler_params=_params(48, 1),
    )(dn2, dh2, h1, y_pool, y_lru, gn_pool_g, gn_lru_g, w_out, ln2_g, after)


def _mixer_bwd(u_pool, u_lru, u_gate, h, dy_pool, dy_lru,
               pool_w, pool_scale, conv_w, conv_b, wa_bd, b_a, wi_bd, b_i, lam, after):
    s = u_pool.shape[0]
    tc = _time_chunk(s)
    n_chunks = s // tc

    def body(up_ref, ul_ref, ug_ref, h_ref, dyp_ref, dyl_ref,
             pw_ref, ps_ref, cw_ref, cb_ref, wa_ref, ba_ref, wi_ref, bi_ref, lam_ref, _after,
             dup_ref, dul_ref, dug_ref, dpw_ref, dps_ref, dcw_ref, dcb_ref, dwa_ref, dba_ref, dwi_ref, dbi_ref, dlam_ref):
        onehot, window = _slab_scalars()
        pw = pw_ref[0].astype(BF16)
        wa = wa_ref[0].astype(BF16)
        wi = wi_ref[0].astype(BF16)
        ps, cw, cb, ba, bi = ps_ref[...], cw_ref[...], cb_ref[...], ba_ref[...], bi_ref[...]
        lam_v = lam_ref[...]
        sp = _softplus_neg(lam_v)
        row = lax.broadcasted_iota(jnp.int32, (tc, SLAB), 0)
        for ref in (dpw_ref, dps_ref, dcw_ref, dcb_ref, dwa_ref, dba_ref, dwi_ref, dbi_ref, dlam_ref):
            ref[...] = jnp.zeros_like(ref)

        def chunk(t0, ext_p, ext_l, ext_h, carry):
            l_next, a_next, dxc_next, ddn_next = carry
            rows = pl.ds(t0, tc)
            taps = _conv_taps(ext_l)
            xc = _conv(taps, cw, cb)
            xb, r, i, a, mult = _lru_gates(xc, wa, ba, wi, bi, sp)
            hv = ext_h[HALO:]
            h_before = _down(ext_h, 1)[HALO:]
            ug = ug_ref[rows, :]
            dyl = dyl_ref[rows, :]
            gel, th = _gelu(ug)
            dug_ref[rows, :] = dyl * hv * _gelu_grad(ug, th)
            a_after = jnp.where(row == tc - 1, a_next, _up(a, 1))
            l = _scan_anticausal(a_after, dyl * gel, l_next, row, tc)
            dmult = l * (i * xc)
            di = l * mult * xc
            dxc = l * mult * i
            dla = (l * h_before) * a - jnp.where(mult > 0.0, dmult * (a * a) / mult, 0.0)
            dlam_ref[...] += jnp.sum(dla * r, axis=0, keepdims=True)
            dpa = (dla * ((-LRU_C) * sp)) * (r * (1.0 - r))
            dpi = di * (i * (1.0 - i))
            dpab = dpa.astype(BF16)
            dpib = dpi.astype(BF16)
            dwa_ref[0] += _mm_tn(xb, dpab)
            dwi_ref[0] += _mm_tn(xb, dpib)
            dba_ref[...] += jnp.sum(dpa, axis=0, keepdims=True)
            dbi_ref[...] += jnp.sum(dpi, axis=0, keepdims=True)
            dxc = dxc + _mm_nt(dpab, wa) + _mm_nt(dpib, wi)
            ext_d = jnp.concatenate([dxc, dxc_next], axis=0)
            dul_ref[rows, :] = (cw[3:4] * dxc + cw[2:3] * _up(ext_d, 1)[:tc]
                                + cw[1:2] * _up(ext_d, 2)[:tc] + cw[0:1] * _up(ext_d, 3)[:tc])
            for k in range(CONV_WIDTH):
                dcw_ref[k:k + 1, :] += jnp.sum(dxc * taps[CONV_WIDTH - 1 - k], axis=0, keepdims=True)
            dcb_ref[...] += jnp.sum(dxc, axis=0, keepdims=True)
            denom = _pool_denominator(t0, row, window)
            db = (_causal_window(ext_p, onehot) / denom - ext_p[HALO:]).astype(BF16)
            dyp = dyp_ref[rows, :]
            dps_ref[...] += jnp.sum(dyp * _mm(db, pw), axis=0, keepdims=True)
            dys = (dyp * ps).astype(BF16)
            dpw_ref[0] += _mm_tn(db, dys)
            dd = _mm_nt(dys, pw)
            ddn = dd / denom
            ext_q = jnp.concatenate([ddn, ddn_next], axis=0)
            dup_ref[rows, :] = _anticausal_window(ext_q, onehot, tc) - dd
            return l[0:1, :], a[0:1, :], dxc[0:8, :], ddn[0:HALO, :]

        def step(k, carry):
            c = n_chunks - 1 - k
            t0 = pl.multiple_of(c * tc, tc)
            ext = pl.ds(pl.multiple_of(c * tc - HALO, HALO), tc + HALO)
            return chunk(t0, up_ref[ext, :], ul_ref[ext, :], h_ref[ext, :], carry)

        carry = (jnp.zeros((1, SLAB), F32), jnp.zeros((1, SLAB), F32),
                 jnp.zeros((8, SLAB), F32), jnp.zeros((HALO, SLAB), F32))
        carry = lax.fori_loop(0, n_chunks - 1, step, carry)
        pad = jnp.zeros((HALO, SLAB), F32)
        first = pl.ds(0, tc)
        chunk(0, jnp.concatenate([pad, up_ref[first, :]], axis=0), jnp.concatenate([pad, ul_ref[first, :]], axis=0),
              jnp.concatenate([pad, h_ref[first, :]], axis=0), carry)
        dlam_ref[...] = dlam_ref[...] * (LRU_C * jax.nn.sigmoid(-lam_v))

    seq, mat, vec, taps = _slab_specs(s)
    full = jax.ShapeDtypeStruct((s, 512), F32)
    mats = jax.ShapeDtypeStruct((N_SLAB, SLAB, SLAB), F32)
    vecs = jax.ShapeDtypeStruct((1, 512), F32)
    return pl.pallas_call(
        body, name="mixer_bwd", grid=(N_SLAB,),
        in_specs=[seq] * 6 + [mat, vec, taps, vec, mat, vec, mat, vec, vec, pl.BlockSpec(memory_space=pl.ANY)],
        out_specs=[seq, seq, seq, mat, vec, taps, vec, mat, vec, mat, vec, vec],
        out_shape=[full, full, full, mats, vecs, jax.ShapeDtypeStruct((CONV_WIDTH, 512), F32), vecs,
                   mats, vecs, mats, vecs, vecs],
        compiler_params=_params(56, 1),
    )(u_pool, u_lru, u_gate, h, dy_pool, dy_lru, pool_w, pool_scale, conv_w, conv_b, wa_bd, b_a, wi_bd, b_i, lam,
      after)


def _bwd_in(x, dh1, du_pool, du_lru, du_gate, ln1_g, w_in_t, after):
    s = x.shape[0]
    tb = _token_block(s)

    def body(x_ref, dh1_ref, dup_ref, dul_ref, dug_ref, g_ref, w_ref, _after, dx_ref, dwb_ref, dg_ref, dw_ref):
        @pl.when(pl.program_id(0) == 0)
        def _():
            dw_ref[...] = jnp.zeros_like(dw_ref)
            dg_ref[...] = jnp.zeros_like(dg_ref)

        g1 = g_ref[...]
        n, xhat, rstd = _rms(x_ref[...], g1)
        nb = n.astype(BF16)
        dn = jnp.zeros((tb, D_MODEL), F32)
        for k, ref in enumerate((dup_ref, dul_ref, dug_ref)):
            rows = slice(k * 512, (k + 1) * 512)
            db = ref[...].astype(BF16)
            dw_ref[rows, :] += _mm_tn(db, nb)
            dn = dn + _mm(db, w_ref[rows, :])
        dx, dg1 = _rms_bwd(dn, xhat, rstd, g1)
        dx_ref[...] = dh1_ref[...] + dx
        dg_ref[...] += dg1

        @pl.when(pl.program_id(0) == s // tb - 1)
        def _():
            dwb_ref[...] = dw_ref[...].astype(BF16)

    row = pl.BlockSpec((tb, D_MODEL), lambda i: (i, 0))
    half = pl.BlockSpec((tb, 512), lambda i: (i, 0))
    vec = pl.BlockSpec((1, D_MODEL), lambda i: (0, 0))
    mat = pl.BlockSpec((IN_WIDTH, D_MODEL), lambda i: (0, 0))
    return pl.pallas_call(
        body, name="bwd_in", grid=(s // tb,),
        in_specs=[row, row, half, half, half, vec, mat, pl.BlockSpec(memory_space=pl.ANY)],
        out_specs=[row, mat, vec],
        out_shape=[jax.ShapeDtypeStruct((s, D_MODEL), F32), jax.ShapeDtypeStruct((IN_WIDTH, D_MODEL), BF16),
                   jax.ShapeDtypeStruct((1, D_MODEL), F32)],
        scratch_shapes=[pltpu.VMEM((IN_WIDTH, D_MODEL), F32)],
        compiler_params=_params(48, 1),
    )(x, dh1, du_pool, du_lru, du_gate, ln1_g, w_in_t, after)


def _mesh_position():
    x, y, c = lax.axis_index("x"), lax.axis_index("y"), lax.axis_index("c")
    return x, y, c, 4 * x + 2 * y + c


def _peer(x, y, c, p):
    px = 1 - x if p & 4 else x
    py = 1 - y if p & 2 else y
    pc = 1 - c if p & 1 else c
    return (px, py, pc), 4 * px + 2 * py + pc


HBM_SPEC = pl.BlockSpec(memory_space=pltpu.HBM)
SEM_SPEC = pl.BlockSpec(memory_space=pltpu.SEMAPHORE)
DATAFLOW = pltpu.SideEffectType.DATAFLOW_SIDE_EFFECTING


class Copy(NamedTuple):
    src: int
    src_at: Any
    dst: int
    dst_at: Any
    peer: int
    group: int
    slot: int


SIBLING = (1,)
SAME_CORE = (2, 4, 6)
EVERYONE = tuple(range(1, N_DEV))


def _same(index):
    return index


def _chip(index):
    return jnp.right_shift(index, 1)


def _fan_out(srcs, lands, peers, group):
    return [Copy(s, None, d, _same, p, group, N_DEV * i + p) for i, (s, d) in enumerate(zip(srcs, lands)) for p in peers]


def _scatter(stacks, lands, peers):
    return [Copy(s, lambda me, p=p: jnp.bitwise_xor(me, p), d, _same, p, 0, 0)
            for s, d in zip(stacks, lands) for p in peers]


def _numbered(copies, group=0):
    return [cp._replace(group=group, slot=i) for i, cp in enumerate(copies)]


def _relay(lands, peers):
    return [Copy(b, lambda s, q=q: jnp.bitwise_xor(s, q), b, lambda s, q=q: jnp.bitwise_xor(s, q), 1, 0, N_DEV * i + q)
            for i, b in enumerate(lands) for q in peers]


def _to_sibling(stacks, lands):
    return [Copy(s, lambda me, k=k: 2 * k + 1 - jnp.bitwise_and(me, 1), d, lambda me, k=k: k, 1, 0, 4 * i + k)
            for i, (s, d) in enumerate(zip(stacks, lands)) for k in range(N_DEV // 2)]


def _to_chips(sums, lands):
    return [Copy(s, lambda me, p=p: jnp.bitwise_xor(_chip(me), p // 2), d, _chip, p, 0, 4 * i + p // 2)
            for i, (s, d) in enumerate(zip(sums, lands)) for p in SAME_CORE]


def _comm_call(name, bufs, wait=None, start=None, after=()):
    nb = len(bufs)
    slots = list(start[1]) if start else []
    n_out_sem = 2 * len(slots)
    after = [a for a in (after if isinstance(after, (list, tuple)) else [after]) if a is not None]

    def body(*refs):
        b = refs[:nb]
        at = nb
        if wait:
            w_send, w_recv = refs[at], refs[at + 1]
            at += 2
        at += len(after)
        out_sems = refs[at:at + n_out_sem]
        token = refs[at + n_out_sem + nb]
        x, y, c, me = _mesh_position()

        def part(i, row_of, sender):
            return b[i] if row_of is None else b[i].at[row_of(sender)]

        if wait:
            for cp in wait[1]:
                peer, peer_index = _peer(x, y, c, cp.peer)
                arrival = pltpu.make_async_remote_copy(part(cp.src, cp.src_at, me), part(cp.dst, cp.dst_at, peer_index),
                                                       w_send.at[cp.slot], w_recv.at[cp.slot],
                                                       device_id=peer, device_id_type=MESH)
                arrival.wait_send()
                arrival.wait_recv()
        if start:
            for cp in start[0]:
                peer, _ = _peer(x, y, c, cp.peer)
                pltpu.make_async_remote_copy(part(cp.src, cp.src_at, me), part(cp.dst, cp.dst_at, me),
                                             out_sems[2 * cp.group].at[cp.slot], out_sems[2 * cp.group + 1].at[cp.slot],
                                             device_id=peer, device_id_type=MESH).start()
        token[...] = jnp.zeros_like(token)

    sem_shapes = []
    for n_slots in slots:
        sem_shapes += [pltpu.SemaphoreType.DMA((n_slots,))] * 2
    operands = [pltpu.with_memory_space_constraint(a, pltpu.HBM) for a in bufs]
    in_specs = [HBM_SPEC] * nb
    if wait:
        operands += list(wait[0])
        in_specs += [SEM_SPEC, SEM_SPEC]
    operands += after
    in_specs += [pl.BlockSpec(memory_space=pl.ANY)] * len(after)
    outs = pl.pallas_call(
        body, name=name, in_specs=in_specs,
        out_specs=[SEM_SPEC] * n_out_sem + [HBM_SPEC] * nb + [pl.BlockSpec(memory_space=pltpu.VMEM)],
        out_shape=sem_shapes + [pltpu.HBM(a.shape, a.dtype) for a in bufs] + [jax.ShapeDtypeStruct((8, SLAB), F32)],
        input_output_aliases={i: n_out_sem + i for i in range(nb)},
        compiler_params=pltpu.CompilerParams(has_side_effects=DATAFLOW),
    )(*operands)
    sems = [(outs[2 * k], outs[2 * k + 1]) for k in range(len(slots))]
    return sems, list(outs[n_out_sem:n_out_sem + nb]), outs[-1]


def _pair_sum(stacks, lands, place):
    n = len(stacks)

    def body(place_ref, *refs):
        k = pl.program_id(0)
        for m in range(n):
            mine, theirs, out, land = refs[m], refs[n + m], refs[2 * n + m], refs[3 * n + m]
            total = (mine[0, 0].astype(F32) + theirs[0].astype(F32)).astype(out.dtype)
            out[0] = total

            @pl.when(k == place_ref[1])
            def _():
                land[0] = total

    in_specs = [pl.BlockSpec((1, 1) + a.shape[1:], lambda k, place_ref: (k, place_ref[0], 0, 0)) for a in stacks]
    in_specs += [pl.BlockSpec((1,) + a.shape[1:], lambda k, place_ref: (k, 0, 0)) for a in lands]
    out_specs = [pl.BlockSpec((1,) + a.shape[1:], lambda k, place_ref: (k, 0, 0)) for a in lands]
    out_specs += [pl.BlockSpec((1,) + a.shape[1:], lambda k, place_ref: (place_ref[1], 0, 0)) for a in lands]
    outs = pl.pallas_call(
        body, name="pair_sum_" + "_".join(str(a.shape[1]) for a in stacks),
        grid_spec=pltpu.PrefetchScalarGridSpec(num_scalar_prefetch=1, grid=(N_DEV // 2,), in_specs=in_specs,
                                               out_specs=out_specs),
        out_shape=[jax.ShapeDtypeStruct(a.shape, a.dtype) for a in lands] * 2,
        compiler_params=_params(40, 1),
    )(place, *[a.reshape((N_DEV // 2, 2) + a.shape[1:]) for a in stacks], *lands)
    return list(outs[:n]), list(outs[n:])


def _reduce_adam(parts, w, m, v, name):
    rows, cols = w.shape
    n_parts = parts.shape[0]
    rb = rows
    for cand in (256, 176, 128):
        if rows % cand == 0 and rows > cand:
            rb = cand
            break

    def body(p_ref, w_ref, m_ref, v_ref, g_out, d_out, m_out, v_out):
        g = p_ref[0].astype(F32)
        for j in range(1, n_parts):
            g = g + p_ref[j].astype(F32)
        g_out[...] = g
        d_out[...], m_out[...], v_out[...] = _adam(g, w_ref[...], m_ref[...], v_ref[...])

    blk = pl.BlockSpec((rb, cols), lambda i: (i, 0))
    out = jax.ShapeDtypeStruct((rows, cols), F32)
    return pl.pallas_call(
        body, name=name, grid=(rows // rb,),
        in_specs=[pl.BlockSpec((n_parts, rb, cols), lambda i: (0, i, 0)), blk, blk, blk],
        out_specs=[blk] * 4, out_shape=[out] * 4,
        compiler_params=_params(40, 1),
    )(parts, w, m, v)


def _cols_from_stack(stack):
    n, r, c = stack.shape
    return jnp.transpose(stack, (1, 0, 2)).reshape(r, n * c)


def _block_diag(w):
    z = jnp.zeros((N_SLAB, 64, 64), w.dtype)
    pairs = w.reshape(N_SLAB, 2, 64, 64)
    top = jnp.concatenate([pairs[:, 0], z], axis=2)
    bottom = jnp.concatenate([z, pairs[:, 1]], axis=2)
    return jnp.concatenate([top, bottom], axis=1)


def _adam(g, w, m, v):
    m_new = ADAM_B1 * m + (1.0 - ADAM_B1) * g
    v_new = ADAM_B2 * v + (1.0 - ADAM_B2) * (g * g)
    m_hat = m_new / (1.0 - ADAM_B1 ** ADAM_STEP)
    v_hat = v_new / (1.0 - ADAM_B2 ** ADAM_STEP)
    return (-ADAM_LR) * (m_hat / (jnp.sqrt(v_hat) + ADAM_EPS) + ADAM_WD * w), m_new, v_new


WIDE = ("ln2_g", "lnf_g")
HALF = ("pool_scale", "conv_b", "b_a", "b_i", "lam", "gn_pool_g", "gn_lru_g")
VECTORS = [(k, D_MODEL) for k in WIDE] + [(k, 512) for k in HALF]
VECTOR_ROWS = sum(width // SLAB for _, width in VECTORS)
LOSS_ROW = -(-VECTOR_ROWS // 8) * 8
CONV_AT = LOSS_ROW + 8
CONV_LANES = LRU_WIDTH // SLAB
PACK_F_ROWS = CONV_AT + CONV_WIDTH * CONV_LANES
MATRIX_ROWS = N_SLAB * SLAB
HEAD = SLAB // 2
GATE_ROWS = N_SLAB * HEAD
PACK_B_ROWS = MATRIX_ROWS + 2 * GATE_ROWS


def _pack_small(vectors, pool_g, wa_g, wi_g, conv, sq):
    n_vec = len(vectors)

    def body(*refs):
        vec = refs[:n_vec]
        pw_ref, wa_ref, wi_ref, cw_ref, sq_ref, out, out_b = refs[n_vec:]
        out[...] = jnp.zeros_like(out)
        row = 0
        for ref, (_, width) in zip(vec, VECTORS):
            for k in range(width // SLAB):
                out[row:row + 1, :] = ref[:, k * SLAB:(k + 1) * SLAB]
                row += 1
        for tap in range(CONV_WIDTH):
            for k in range(CONV_LANES):
                at = CONV_AT + tap * CONV_LANES + k
                out[at:at + 1, :] = cw_ref[tap:tap + 1, k * SLAB:(k + 1) * SLAB]
        total = sq_ref[:, 0:SLAB]
        for k in range(1, D_MODEL // SLAB):
            total = total + sq_ref[:, k * SLAB:(k + 1) * SLAB]
        out[LOSS_ROW:LOSS_ROW + 1, :] = total
        left = lax.broadcasted_iota(jnp.int32, (HEAD, SLAB), 1) < HEAD
        for s in range(N_SLAB):
            out_b[s * SLAB:(s + 1) * SLAB, :] = pw_ref[s].astype(BF16)
            for i, ref in enumerate((wa_ref, wi_ref)):
                at = MATRIX_ROWS + i * GATE_ROWS + s * HEAD
                out_b[at:at + HEAD, :] = jnp.where(left, ref[s, 0:HEAD, :], ref[s, HEAD:SLAB, :]).astype(BF16)

    return pl.pallas_call(
        body, name="pack_small",
        out_shape=[jax.ShapeDtypeStruct((PACK_F_ROWS, SLAB), F32), jax.ShapeDtypeStruct((PACK_B_ROWS, SLAB), BF16)],
    )(*vectors, pool_g, wa_g, wi_g, conv, sq)


def _small_reduce_adam(parts, parts_b, vec_w, vec_m, vec_v, pool_wmv):
    n_vec = len(VECTORS)
    n_parts = parts.shape[0]

    def body(*refs):
        p_ref, pb_ref = refs[0], refs[1]
        w_refs, m_refs, v_refs = (refs[2 + k * n_vec:2 + (k + 1) * n_vec] for k in range(3))
        pw_w, pw_m, pw_v = refs[2 + 3 * n_vec:5 + 3 * n_vec]
        outs = refs[5 + 3 * n_vec:-1]
        total = refs[-1]
        total[...] = p_ref[0]
        for j in range(1, n_parts):
            total[...] += p_ref[j]
        row = 0
        for i, (_, width) in enumerate(VECTORS):
            n_rows = width // SLAB
            g = jnp.concatenate([total[row + k:row + k + 1, :] for k in range(n_rows)], axis=1)
            row += n_rows
            d, m_new, v_new = _adam(g, w_refs[i][...], m_refs[i][...], v_refs[i][...])
            for ref, val in zip(outs[4 * i:4 * i + 4], (g, d, m_new, v_new)):
                ref[...] = val
        tail = outs[4 * n_vec:]

        def summed(first, count):
            g = pb_ref[0, first:first + count, :].astype(F32)
            for j in range(1, n_parts):
                g = g + pb_ref[j, first:first + count, :].astype(F32)
            return g

        g = summed(0, MATRIX_ROWS)
        d, m_new, v_new = _adam(g, pw_w[...], pw_m[...], pw_v[...])
        for ref, val in zip(tail[0:4], (g, d, m_new, v_new)):
            ref[...] = val
        tail[4][...] = summed(MATRIX_ROWS, GATE_ROWS)
        tail[5][...] = summed(MATRIX_ROWS + GATE_ROWS, GATE_ROWS)
        for tap in range(CONV_WIDTH):
            at = CONV_AT + tap * CONV_LANES
            tail[6][tap:tap + 1, :] = jnp.concatenate([total[at + k:at + k + 1, :] for k in range(CONV_LANES)], axis=1)
        tail[7][...] = (0.5 / D_MODEL) * jnp.sum(total[LOSS_ROW:LOSS_ROW + 1, :], axis=1, keepdims=True)

    out_shape = []
    for _, width in VECTORS:
        out_shape += [jax.ShapeDtypeStruct((1, width), F32)] * 4
    out_shape += [jax.ShapeDtypeStruct((MATRIX_ROWS, SLAB), F32)] * 4 + [jax.ShapeDtypeStruct((GATE_ROWS, SLAB), F32)] * 2
    out_shape += [jax.ShapeDtypeStruct((CONV_WIDTH, LRU_WIDTH), F32), jax.ShapeDtypeStruct((1, 1), F32)]
    outs = pl.pallas_call(
        body, name="adam_small", out_shape=out_shape,
        scratch_shapes=[pltpu.VMEM((PACK_F_ROWS, SLAB), F32)],
        compiler_params=_params(40),
    )(parts, parts_b, *vec_w, *vec_m, *vec_v, *pool_wmv)
    vec_out = [tuple(outs[4 * i:4 * i + 4]) for i in range(n_vec)]
    tail = outs[4 * n_vec:]
    return vec_out, tuple(tail[0:4]), tail[4], tail[5], tail[6], tail[7]


def _plain_adam(grads, ws, ms, vs):
    n = len(grads)

    def body(*refs):
        ins, outs = refs[:4 * n], refs[4 * n:]
        for i in range(n):
            d, m_new, v_new = _adam(ins[i][...], ins[n + i][...], ins[2 * n + i][...], ins[3 * n + i][...])
            for ref, val in zip(outs[3 * i:3 * i + 3], (d, m_new, v_new)):
                ref[...] = val

    out_shape = []
    for g in grads:
        out_shape += [jax.ShapeDtypeStruct(g.shape, F32)] * 3
    outs = pl.pallas_call(body, name="adam_plain", out_shape=out_shape)(*grads, *ws, *ms, *vs)
    return [tuple(outs[3 * i:3 * i + 3]) for i in range(n)]


def kernel(x, ln1_g, w_in, pool_w, pool_scale, conv_w, conv_b, w_a, b_a, w_i, b_i, lam, gn_pool_g, gn_lru_g, w_out, ln2_g, w_ffn_gate, w_ffn_up, w_ffn_down, lnf_g, loss_target, m_ln1_g, m_w_in, m_pool_w, m_pool_scale, m_conv_w, m_conv_b, m_w_a, m_b_a, m_w_i, m_b_i, m_lam, m_gn_pool_g, m_gn_lru_g, m_w_out, m_ln2_g, m_w_ffn_gate, m_w_ffn_up, m_w_ffn_down, m_lnf_g, v_ln1_g, v_w_in, v_pool_w, v_pool_scale, v_conv_w, v_conv_b, v_w_a, v_b_a, v_w_i, v_b_i, v_lam, v_gn_pool_g, v_gn_lru_g, v_w_out, v_ln2_g, v_w_ffn_gate, v_w_ffn_up, v_w_ffn_down, v_lnf_g):
    weights = dict(ln1_g=ln1_g, w_in=w_in, pool_w=pool_w, pool_scale=pool_scale, conv_w=conv_w, conv_b=conv_b,
                   w_a=w_a, b_a=b_a, w_i=w_i, b_i=b_i, lam=lam, gn_pool_g=gn_pool_g, gn_lru_g=gn_lru_g,
                   w_out=w_out, ln2_g=ln2_g, w_ffn_gate=w_ffn_gate, w_ffn_up=w_ffn_up, w_ffn_down=w_ffn_down,
                   lnf_g=lnf_g)
    mom1 = dict(ln1_g=m_ln1_g, w_in=m_w_in, pool_w=m_pool_w, pool_scale=m_pool_scale, conv_w=m_conv_w,
                conv_b=m_conv_b, w_a=m_w_a, b_a=m_b_a, w_i=m_w_i, b_i=m_b_i, lam=m_lam, gn_pool_g=m_gn_pool_g,
                gn_lru_g=m_gn_lru_g, w_out=m_w_out, ln2_g=m_ln2_g, w_ffn_gate=m_w_ffn_gate,
                w_ffn_up=m_w_ffn_up, w_ffn_down=m_w_ffn_down, lnf_g=m_lnf_g)
    mom2 = dict(ln1_g=v_ln1_g, w_in=v_w_in, pool_w=v_pool_w, pool_scale=v_pool_scale, conv_w=v_conv_w,
                conv_b=v_conv_b, w_a=v_w_a, b_a=v_b_a, w_i=v_w_i, b_i=v_b_i, lam=v_lam, gn_pool_g=v_gn_pool_g,
                gn_lru_g=v_gn_lru_g, w_out=v_w_out, ln2_g=v_ln2_g, w_ffn_gate=v_w_ffn_gate,
                w_ffn_up=v_w_ffn_up, w_ffn_down=v_w_ffn_down, lnf_g=v_lnf_g)

    xs = x[0]
    target = loss_target[0]

    shard = dict(w_in=lambda a: a[0].T, w_ffn_gate=lambda a: a[0].T, w_ffn_up=lambda a: a[0].T,
                 w_out=lambda a: a[0], w_ffn_down=lambda a: a[0], conv_w=lambda a: a[0])
    unshard = dict(w_in=lambda a: a.T[None], w_ffn_gate=lambda a: a.T[None], w_ffn_up=lambda a: a.T[None],
                   w_out=lambda a: a[None], w_ffn_down=lambda a: a[None], conv_w=lambda a: a[None])

    gathered = ("w_in", "conv_w", "w_out", "w_ffn_gate", "w_ffn_up", "w_ffn_down")
    groups = ((0, 1), (2,), (3, 4, 5))
    sources = [shard[k](weights[k]) if k == "conv_w" else shard[k](weights[k]).astype(BF16) for k in gathered]
    my_index = 4 * lax.axis_index("x") + 2 * lax.axis_index("y") + lax.axis_index("c")
    lands = [lax.dynamic_update_index_in_dim(lax.empty((N_DEV,) + a.shape, a.dtype), a, my_index, 0) for a in sources]

    def first_hop(n, group=0):
        return _numbered(_fan_out(range(n), range(n, 2 * n), SAME_CORE + SIBLING, 0), group)

    start = []
    for g, members in enumerate(groups):
        start += _numbered(_fan_out(members, [6 + m for m in members], SAME_CORE + SIBLING, 0), g)
    sems, bufs, _ = _comm_call("gather_start", sources + lands,
                               start=(start, [sum(cp.group == g for cp in start) for g in range(len(groups))]))
    sources, lands = bufs[:6], bufs[6:]

    def gathered_group(tag, g, after):
        members = groups[g]
        n = len(members)
        relay = _numbered(_relay(range(n, 2 * n), SAME_CORE))
        relay_sems, bufs, _ = _comm_call("gather_relay_" + tag, [sources[m] for m in members] + [lands[m] for m in members],
                                         wait=(sems[g], first_hop(n)), start=(relay, (len(relay),)), after=after)
        _, bufs, _ = _comm_call("gather_wait_" + tag, bufs[n:], wait=(relay_sems[0], _numbered(_relay(range(n), SAME_CORE))))
        return bufs

    g_in, g_conv = gathered_group("in", 0, None)
    w_in_f = g_in.reshape(IN_WIDTH, D_MODEL)
    conv_w_f = _cols_from_stack(g_conv)

    wa_bd = _block_diag(w_a[0])
    wi_bd = _block_diag(w_i[0])
    lnf_row = lnf_g.reshape(1, D_MODEL)

    u_pool, u_lru, u_gate = _fwd_in(xs, ln1_g, w_in_f)
    y_pool, h, y_lru = _mixer_fwd(u_pool, u_lru, u_gate, pool_w[0], pool_scale, conv_w_f, conv_b,
                                  wa_bd, b_a, wi_bd, b_i, lam)
    (g_out,) = gathered_group("out", 1, y_pool)
    w_out_f = g_out.reshape(D_MODEL, D_MODEL)
    h1, n2 = _fwd_out(xs, y_pool, y_lru, gn_pool_g, gn_lru_g, w_out_f, ln2_g)
    g_gate, g_up, g_down = gathered_group("ffn", 2, n2)
    w_gate_f = g_gate.reshape(D_FF, D_MODEL)
    w_up_f = g_up.reshape(D_FF, D_MODEL)
    w_down_f = g_down.reshape(D_FF, D_MODEL)
    g_act, u_act, dh2, dh2b, d_lnf, sq = _ffn_fwd(h1, n2, target, lnf_row, w_gate_f, w_up_f, w_down_f)

    place = jnp.stack([lax.axis_index("c"), 2 * lax.axis_index("x") + lax.axis_index("y")]).astype(jnp.int32)

    d_gate, d_up, d_down, dn2 = _ffn_bwd(n2, dh2b, g_act, u_act, w_gate_f, w_up_f, w_down_f)
    ffn_stacks = [d.reshape(N_DEV, D_FF // N_DEV, D_MODEL) for d in (d_gate, d_up, d_down)]
    pair_lands = [lax.empty((N_DEV // 2,) + a.shape[1:], a.dtype) for a in ffn_stacks]
    pair_copies = _numbered(_to_sibling(range(3), range(3, 6)))
    sem, bufs, token = _comm_call("ffn_pair_start", ffn_stacks + pair_lands, start=(pair_copies, (len(pair_copies),)))
    dh1, dy_pool, dy_lru, d_out, d_ln2, d_gnp, d_gnl = _bwd_out(dn2, dh2, h1, y_pool, y_lru, gn_pool_g, gn_lru_g,
                                                                 w_out_f, ln2_g, token)
    _, bufs, _ = _comm_call("ffn_pair_wait", bufs, wait=(sem[0], pair_copies), after=dh1)
    ffn_sums, ffn_lands = _pair_sum(bufs[:3], bufs[3:], place)
    ffn_copies = _numbered(_to_chips(range(3), range(3, 6)))
    ffn_sem, ffn_bufs, token = _comm_call("ffn_chip_start", ffn_sums + ffn_lands, start=(ffn_copies, (len(ffn_copies),)))
    (du_pool, du_lru, du_gate, d_pw, d_ps, d_cw, d_cb, d_wa, d_ba, d_wi, d_bi, d_lam) = _mixer_bwd(
        u_pool, u_lru, u_gate, h, dy_pool, dy_lru, pool_w[0], pool_scale, conv_w_f, conv_b,
        wa_bd, b_a, wi_bd, b_i, lam, token)

    def direct(tag, stacks, wholes, after):
        sources = list(stacks) + list(wholes)
        n, n_st = len(sources), len(stacks)
        lands = [lax.dynamic_update_index_in_dim(
            lax.empty(a.shape if i < n_st else (N_DEV,) + a.shape, a.dtype),
            lax.dynamic_index_in_dim(a, my_index, 0, keepdims=False) if i < n_st else a, my_index, 0)
            for i, a in enumerate(sources)]
        copies = _numbered(_scatter(range(n_st), range(n, n + n_st), EVERYONE)
                           + _fan_out(range(n_st, n), range(n + n_st, 2 * n), EVERYONE, 0))
        sem, bufs, token = _comm_call(tag + "_start", sources + lands, start=(copies, (len(copies),)), after=after)
        return (tag, sem[0], bufs, copies), token

    def direct_finish(started, after):
        tag, sem, bufs, copies = started
        _, bufs, _ = _comm_call(tag + "_wait", bufs, wait=(sem, copies), after=after)
        return bufs[len(bufs) // 2:]

    vec_grads = dict(ln2_g=d_ln2, lnf_g=d_lnf, pool_scale=d_ps, conv_b=d_cb, b_a=d_ba, b_i=d_bi,
                     lam=d_lam, gn_pool_g=d_gnp, gn_lru_g=d_gnl)
    packed, packed_b = _pack_small([vec_grads[k] for k, _ in VECTORS], d_pw, d_wa, d_wi, d_cw, sq)
    small_started, token = direct("small", [d_out.reshape(N_DEV, D_MODEL // N_DEV, D_MODEL)], [packed, packed_b], None)
    grad_x, d_in, d_ln1 = _bwd_in(xs, dh1, du_pool, du_lru, du_gate, ln1_g, w_in_f, token)
    in_started, token = direct("in", [d_in.reshape(N_DEV, IN_WIDTH // N_DEV, D_MODEL)], [d_ln1], None)

    results = {}

    def reduce_adam(name, parts):
        outs = _reduce_adam(parts, shard[name](weights[name]), shard[name](mom1[name]), shard[name](mom2[name]),
                            "adam_" + name)
        results[name] = tuple(unshard[name](o) for o in outs)
        return outs[0]

    shard["ln1_g"] = unshard["ln1_g"] = lambda a: a
    _, ffn_bufs, _ = _comm_call("ffn_chip_wait", ffn_bufs, wait=(ffn_sem[0], ffn_copies), after=token)
    done = [reduce_adam(name, parts)
            for name, parts in zip(("w_ffn_gate", "w_ffn_up", "w_ffn_down"), ffn_bufs[3:])]
    r_out, r_small, r_small_b = direct_finish(small_started, done)
    done.append(reduce_adam("w_out", r_out))

    def as_row(a, width):
        return a.reshape(1, width)

    def as_matrix(a):
        return a.reshape(MATRIX_ROWS, SLAB)

    def as_heads(a):
        return a.reshape(2 * GATE_ROWS, HEAD)

    def heads_apart(g):
        return jnp.transpose(g.reshape(N_SLAB, HEAD, 2, HEAD), (0, 2, 1, 3)).reshape(2 * GATE_ROWS, HEAD)

    vec_out, pool_out, g_wa, g_wi, g_conv, loss_11 = _small_reduce_adam(
        r_small, r_small_b, [as_row(weights[k], w) for k, w in VECTORS], [as_row(mom1[k], w) for k, w in VECTORS],
        [as_row(mom2[k], w) for k, w in VECTORS], [as_matrix(t["pool_w"]) for t in (weights, mom1, mom2)])
    for (k, _), outs in zip(VECTORS, vec_out):
        results[k] = tuple(o.reshape(weights[k].shape) for o in outs)
    results["pool_w"] = tuple(o.reshape(pool_w.shape) for o in pool_out)
    my_columns = conv_w.shape[-1]
    plain_names = ("w_a", "w_i", "conv_w")
    plain_grads = [heads_apart(g) for g in (g_wa, g_wi)]
    plain_grads.append(lax.dynamic_slice_in_dim(g_conv, my_index * my_columns, my_columns, axis=1))
    views = (as_heads, as_heads, lambda a: a[0])
    plain_out = _plain_adam(plain_grads, *[[view(t[k]) for k, view in zip(plain_names, views)]
                                           for t in (weights, mom1, mom2)])
    for k, g, outs in zip(plain_names, plain_grads, plain_out):
        results[k] = tuple(o.reshape(weights[k].shape) for o in (g,) + outs)
    loss = loss_11[0, 0]
    r_in, r_ln1 = direct_finish(in_started, done + [plain_out[0][0], loss_11])
    reduce_adam("w_in", r_in)
    reduce_adam("ln1_g", r_ln1)

    order = ["ln1_g", "w_in", "pool_w", "pool_scale", "conv_w", "conv_b", "w_a", "b_a", "w_i", "b_i", "lam",
             "gn_pool_g", "gn_lru_g", "w_out", "ln2_g", "w_ffn_gate", "w_ffn_up", "w_ffn_down", "lnf_g"]
    return (loss, grad_x[None],
            *[results[k][0] for k in order], *[results[k][1] for k in order],
            *[results[k][2] for k in order], *[results[k][3] for k in order])
```

```python
from typing import Any, NamedTuple

import jax
import jax.numpy as jnp
from jax import lax
from jax.experimental import pallas as pl
from jax.experimental.pallas import tpu as pltpu

F32 = jnp.float32
BF16 = jnp.bfloat16

N_DEV = 8
D_MODEL = 1024
POOL_WIDTH = 512
LRU_WIDTH = 512
IN_WIDTH = 1536
D_FF = 2816
N_SLAB = 4
SLAB = 128
CONV_WIDTH = 4
LRU_C = 8.0
EPS = 1e-6
HALO = 16
FF_CHUNK = 256

ADAM_LR = 0.001
ADAM_B1 = 0.9
ADAM_B2 = 0.999
ADAM_EPS = 1e-08
ADAM_WD = 0.01
ADAM_STEP = 10

MIB = 1 << 20
MESH = pl.DeviceIdType.MESH


def _params(vmem_mib, n_axes=0):
    sem = ("arbitrary",) * n_axes if n_axes else None
    return pltpu.CompilerParams(dimension_semantics=sem, vmem_limit_bytes=vmem_mib * MIB)


def _in_hbm(*arrays):
    return [pltpu.with_memory_space_constraint(a, pltpu.HBM) for a in arrays]


def _mm(a, b):
    return jnp.dot(a, b, preferred_element_type=F32)


def _mm_nt(a, b):
    return lax.dot_general(a, b, (((1,), (1,)), ((), ())), preferred_element_type=F32)


def _mm_tn(a, b):
    return lax.dot_general(a, b, (((0,), (0,)), ((), ())), preferred_element_type=F32)


def _rms(x, g):
    rstd = lax.rsqrt(jnp.mean(x * x, axis=-1, keepdims=True) + EPS)
    xhat = x * rstd
    return xhat * g, xhat, rstd


def _rms_bwd(dy, xhat, rstd, g):
    gy = dy * g
    dx = rstd * (gy - xhat * jnp.mean(gy * xhat, axis=-1, keepdims=True))
    return dx, jnp.sum(dy * xhat, axis=0, keepdims=True)


def _gelu(z):
    t = jnp.tanh(0.7978845608028654 * (z + 0.044715 * z * z * z))
    return 0.5 * z * (1.0 + t), t


def _gelu_grad(z, t):
    return 0.5 * (1.0 + t) + 0.5 * z * (1.0 - t * t) * 0.7978845608028654 * (1.0 + 3.0 * 0.044715 * z * z)


def _softplus_neg(lam):
    x = -lam
    e = jnp.exp(-jnp.abs(x))
    u = 1.0 + e
    l1p = jnp.where(u == 1.0, e, jnp.log(u) * e / (u - 1.0))
    return jnp.maximum(x, 0.0) + l1p


def _expm1(x):
    p = x * (1.0 + x * (0.5 + x * (1.0 / 6.0 + x * (1.0 / 24.0 + x * (1.0 / 120.0)))))
    return jnp.where(jnp.abs(x) < 0.1, p, jnp.exp(x) - 1.0)


def _down(v, d):
    return pltpu.roll(v, d, 0)


def _up(v, d):
    return pltpu.roll(v, v.shape[0] - d, 0)


def _token_block(s):
    return 512 if s % 512 == 0 and s > 512 else 256


def _time_chunk(s):
    return 256 if s % 256 == 0 else s


def _fwd_in(x, ln1_g, w_in_t):
    s = x.shape[0]
    tb = _token_block(s)

    def body(x_ref, g_ref, w_ref, up_ref, ul_ref, ug_ref):
        n, _, _ = _rms(x_ref[...], g_ref[...])
        proj = _mm_nt(n.astype(BF16), w_ref[...])
        up_ref[...] = proj[:, :POOL_WIDTH]
        ul_ref[...] = proj[:, POOL_WIDTH:POOL_WIDTH + LRU_WIDTH]
        ug_ref[...] = proj[:, POOL_WIDTH + LRU_WIDTH:]

    out = pltpu.HBM((s, 512), F32)
    return pl.pallas_call(
        body, name="fwd_in", grid=(s // tb,),
        in_specs=[pl.BlockSpec((tb, D_MODEL), lambda i: (i, 0)),
                  pl.BlockSpec((1, D_MODEL), lambda i: (0, 0)),
                  pl.BlockSpec((IN_WIDTH, D_MODEL), lambda i: (0, 0))],
        out_specs=[pl.BlockSpec((tb, 512), lambda i: (i, 0))] * 3,
        out_shape=[out, out, out],
        compiler_params=_params(40, 1),
    )(*_in_hbm(x, ln1_g, w_in_t))


def _pool_denominator(t0, row, window):
    return jnp.minimum((t0 + row + 1).astype(F32), window)


def _causal_window(ext, deeper):
    s = ext + _down(ext, 1)
    s = s + deeper[0] * _down(s, 2)
    s = s + deeper[1] * _down(s, 4)
    s = s + deeper[2] * _down(s, 8)
    return s[HALO:]


def _anticausal_window(ext, deeper, rows):
    s = ext + _up(ext, 1)
    s = s + deeper[0] * _up(s, 2)
    s = s + deeper[1] * _up(s, 4)
    s = s + deeper[2] * _up(s, 8)
    return s[:rows]


def _conv_taps(ext):
    return [ext[HALO:], _down(ext, 1)[HALO:], _down(ext, 2)[HALO:], _down(ext, 3)[HALO:]]


def _conv(taps, cw, cb):
    return cw[3:4] * taps[0] + cw[2:3] * taps[1] + cw[1:2] * taps[2] + cw[0:1] * taps[3] + cb


def _lru_gates(xc, wa, ba, wi, bi, sp):
    xb = xc.astype(BF16)
    r = jax.nn.sigmoid(_mm(xb, wa) + ba)
    i = jax.nn.sigmoid(_mm(xb, wi) + bi)
    la = (-LRU_C) * r * sp
    a = jnp.exp(la)
    mult = jnp.sqrt(jnp.maximum(-_expm1(2.0 * la), 0.0))
    return xb, r, i, a, mult


SUBLANES = 8


def _scan_causal(a, b, h_prev, row, rows):
    d = 1
    while d < min(SUBLANES, rows):
        head = row < d
        a_sh = jnp.where(head, 1.0, _down(a, d))
        b_sh = jnp.where(head, 0.0, _down(b, d))
        b = a * b_sh + b
        a = a * a_sh
        d *= 2
    while d < rows:
        b = jnp.concatenate([b[:d], a[d:] * b[:-d] + b[d:]], axis=0)
        a = jnp.concatenate([a[:d], a[d:] * a[:-d]], axis=0)
        d *= 2
    return b + a * h_prev


def _scan_anticausal(a, b, l_next, row, rows):
    d = 1
    while d < min(SUBLANES, rows):
        tail = row >= rows - d
        a_sh = jnp.where(tail, 1.0, _up(a, d))
        b_sh = jnp.where(tail, 0.0, _up(b, d))
        b = a * b_sh + b
        a = a * a_sh
        d *= 2
    while d < rows:
        b = jnp.concatenate([a[:-d] * b[d:] + b[:-d], b[-d:]], axis=0)
        a = jnp.concatenate([a[:-d] * a[d:], a[-d:]], axis=0)
        d *= 2
    return b + a * l_next


def _slab_scalars():
    slab = pl.program_id(0)
    deeper = [jnp.where(slab > k, 1.0, 0.0).astype(F32) for k in range(N_SLAB - 1)]
    window = jnp.left_shift(jnp.int32(2), slab).astype(F32)
    inverse = jnp.where(slab == 0, 0.5, jnp.where(slab == 1, 0.25, jnp.where(slab == 2, 0.125, 0.0625))).astype(F32)
    return deeper, (window, inverse)


def _window_mean(total, t0, row, window, at_start):
    if at_start:
        return total / _pool_denominator(t0, row, window[0])
    return total * window[1]


def _slab_specs(s):
    seq = pl.BlockSpec((s, SLAB), lambda k: (0, k))
    mat = pl.BlockSpec((1, SLAB, SLAB), lambda k: (k, 0, 0))
    vec = pl.BlockSpec((1, SLAB), lambda k: (0, k))
    taps = pl.BlockSpec((CONV_WIDTH, SLAB), lambda k: (0, k))
    return seq, mat, vec, taps


def _mixer_fwd(u_pool, u_lru, u_gate, pool_w, pool_scale, conv_w, conv_b, wa_bd, b_a, wi_bd, b_i, lam):
    s = u_pool.shape[0]
    tc = _time_chunk(s)
    n_chunks = s // tc

    def body(up_ref, ul_ref, ug_ref, pw_ref, ps_ref, cw_ref, cb_ref, wa_ref, ba_ref, wi_ref, bi_ref, lam_ref,
             yp_ref, h_ref, yl_ref):
        deeper, window = _slab_scalars()
        pw = pw_ref[0].astype(BF16)
        wa = wa_ref[0].astype(BF16)
        wi = wi_ref[0].astype(BF16)
        ps, cw, cb, ba, bi = ps_ref[...], cw_ref[...], cb_ref[...], ba_ref[...], bi_ref[...]
        sp = _softplus_neg(lam_ref[...])
        row = lax.broadcasted_iota(jnp.int32, (tc, SLAB), 0)

        def chunk(t0, ext_p, ext_l, h_prev, at_start=False):
            rows = pl.ds(t0, tc)
            d = _window_mean(_causal_window(ext_p, deeper), t0, row, window, at_start) - ext_p[HALO:]
            yp_ref[rows, :] = _mm(d.astype(BF16), pw) * ps
            xc = _conv(_conv_taps(ext_l), cw, cb)
            _, _, i, a, mult = _lru_gates(xc, wa, ba, wi, bi, sp)
            h = _scan_causal(a, mult * (i * xc), h_prev, row, tc)
            h_ref[rows, :] = h
            yl_ref[rows, :] = h * _gelu(ug_ref[rows, :])[0]
            return h[tc - 1:tc, :]

        pad = jnp.zeros((HALO, SLAB), F32)
        h0 = chunk(0, jnp.concatenate([pad, up_ref[pl.ds(0, tc), :]], axis=0),
                   jnp.concatenate([pad, ul_ref[pl.ds(0, tc), :]], axis=0), jnp.zeros((1, SLAB), F32), at_start=True)

        def step(c, h_prev):
            t0 = pl.multiple_of(c * tc, tc)
            ext = pl.ds(pl.multiple_of(c * tc - HALO, HALO), tc + HALO)
            return chunk(t0, up_ref[ext, :], ul_ref[ext, :], h_prev)

        lax.fori_loop(1, n_chunks, step, h0)

    seq, mat, vec, taps = _slab_specs(s)
    out = pltpu.HBM((s, 512), F32)
    return pl.pallas_call(
        body, name="mixer_fwd", grid=(N_SLAB,),
        in_specs=[seq, seq, seq, mat, vec, taps, vec, mat, vec, mat, vec, vec],
        out_specs=[seq, seq, seq], out_shape=[out, out, out],
        compiler_params=_params(48, 1),
    )(*_in_hbm(u_pool, u_lru, u_gate, pool_w, pool_scale, conv_w, conv_b, wa_bd, b_a, wi_bd, b_i, lam))


def _fwd_out(x, y_pool, y_lru, gn_pool_g, gn_lru_g, w_out, ln2_g):
    s = x.shape[0]
    tb = _token_block(s)

    def body(x_ref, yp_ref, yl_ref, gp_ref, gl_ref, w_ref, g2_ref, h1_ref, n2_ref):
        mp, _, _ = _rms(yp_ref[...], gp_ref[...])
        ml, _, _ = _rms(yl_ref[...], gl_ref[...])
        h1 = x_ref[...] + _mm(mp.astype(BF16), w_ref[:POOL_WIDTH, :]) + _mm(ml.astype(BF16), w_ref[POOL_WIDTH:, :])
        h1_ref[...] = h1
        n2_ref[...] = _rms(h1, g2_ref[...])[0].astype(BF16)

    row = pl.BlockSpec((tb, D_MODEL), lambda i: (i, 0))
    half = pl.BlockSpec((tb, 512), lambda i: (i, 0))
    return pl.pallas_call(
        body, name="fwd_out", grid=(s // tb,),
        in_specs=[row, half, half, pl.BlockSpec((1, 512), lambda i: (0, 0)), pl.BlockSpec((1, 512), lambda i: (0, 0)),
                  pl.BlockSpec((D_MODEL, D_MODEL), lambda i: (0, 0)), pl.BlockSpec((1, D_MODEL), lambda i: (0, 0))],
        out_specs=[row, row],
        out_shape=[pltpu.HBM((s, D_MODEL), F32), pltpu.HBM((s, D_MODEL), BF16)],
        compiler_params=_params(40, 1),
    )(*_in_hbm(x, y_pool, y_lru, gn_pool_g, gn_lru_g, w_out, ln2_g))


def _ffn_fwd(h1, n2, target, lnf_g, w_gate, w_up, w_down):
    s = h1.shape[0]
    tb = 256
    n_ff = D_FF // FF_CHUNK

    def body(h1_ref, n2_ref, t_ref, gf_ref, wg_hbm, wu_hbm, wd_hbm,
             g_ref, u_ref, dh_ref, dhb_ref, dgf_ref, sq_ref, wg, wu, wd, sem):
        @pl.when(pl.program_id(0) == 0)
        def _():
            loads = [pltpu.make_async_copy(src, dst, sem.at[k])
                     for k, (src, dst) in enumerate(((wg_hbm, wg), (wu_hbm, wu), (wd_hbm, wd)))]
            for cp in loads:
                cp.start()
            for cp in loads:
                cp.wait()
            dgf_ref[...] = jnp.zeros_like(dgf_ref)
            sq_ref[...] = jnp.zeros_like(sq_ref)

        n2v = n2_ref[...]
        acc = jnp.zeros((tb, D_MODEL), F32)
        for c in range(n_ff):
            cols = slice(c * FF_CHUNK, (c + 1) * FF_CHUNK)
            g = _mm_nt(n2v, wg[cols, :])
            u = _mm_nt(n2v, wu[cols, :])
            g_ref[:, cols] = g.astype(BF16)
            u_ref[:, cols] = u.astype(BF16)
            act = g * jax.nn.sigmoid(g) * u
            acc = acc + _mm(act.astype(BF16), wd[cols, :])
        gf = gf_ref[...]
        y, xhat, rstd = _rms(h1_ref[...] + acc, gf)
        err = y - t_ref[...]
        sq_ref[...] += jnp.sum(err * err, axis=0, keepdims=True)
        dh2, dgf = _rms_bwd(err * (1.0 / D_MODEL), xhat, rstd, gf)
        dgf_ref[...] += dgf
        dh_ref[...] = dh2
        dhb_ref[...] = dh2.astype(BF16)

    row = pl.BlockSpec((tb, D_MODEL), lambda i: (i, 0))
    ff = pl.BlockSpec((tb, D_FF), lambda i: (i, 0))
    vec = pl.BlockSpec((1, D_MODEL), lambda i: (0, 0))
    anyspace = pl.BlockSpec(memory_space=pl.ANY)
    return pl.pallas_call(
        body, name="ffn_fwd", grid=(s // tb,),
        in_specs=[row, row, row, vec, anyspace, anyspace, anyspace],
        out_specs=[ff, ff, row, row, vec, vec],
        out_shape=[pltpu.HBM((s, D_FF), BF16), pltpu.HBM((s, D_FF), BF16),
                   pltpu.HBM((s, D_MODEL), F32), pltpu.HBM((s, D_MODEL), BF16),
                   jax.ShapeDtypeStruct((1, D_MODEL), F32), jax.ShapeDtypeStruct((1, D_MODEL), F32)],
        scratch_shapes=[pltpu.VMEM((D_FF, D_MODEL), BF16), pltpu.VMEM((D_FF, D_MODEL), BF16),
                        pltpu.VMEM((D_FF, D_MODEL), BF16), pltpu.SemaphoreType.DMA((3,))],
        compiler_params=_params(56, 1),
    )(*_in_hbm(h1, n2, target, lnf_g, w_gate, w_up, w_down))


def _ffn_bwd(n2, dh2b, g, u, w_gate_t, w_up_t, w_down):
    s = n2.shape[0]
    tb = min(1024, s)
    n_ff = D_FF // FF_CHUNK
    n_tb = s // tb

    def body(n2_ref, dh_ref, g_ref, u_ref, wg_ref, wu_ref, wd_ref, dwg_ref, dwu_ref, dwd_ref, dn2_ref,
             dn2_acc, acc_g, acc_u, acc_d):
        j = pl.program_id(0)
        t = pl.program_id(1)

        @pl.when(t == 0)
        def _():
            acc_g[...] = jnp.zeros_like(acc_g)
            acc_u[...] = jnp.zeros_like(acc_u)
            acc_d[...] = jnp.zeros_like(acc_d)

        n2v = n2_ref[...]
        dh = dh_ref[...]
        gv = g_ref[...].astype(F32)
        uv = u_ref[...].astype(F32)
        sg = jax.nn.sigmoid(gv)
        silu = gv * sg
        dact = _mm_nt(dh, wd_ref[...])
        dub = (dact * silu).astype(BF16)
        dgb = (dact * uv * (sg * (1.0 + gv * (1.0 - sg)))).astype(BF16)
        acc_d[...] += _mm_tn((silu * uv).astype(BF16), dh)
        acc_g[...] += _mm_tn(dgb, n2v)
        acc_u[...] += _mm_tn(dub, n2v)
        part = _mm(jnp.concatenate([dgb, dub], axis=1), jnp.concatenate([wg_ref[...], wu_ref[...]], axis=0))
        rows = pl.ds(pl.multiple_of(t * tb, tb), tb)

        @pl.when(t == n_tb - 1)
        def _():
            dwg_ref[...] = acc_g[...].astype(BF16)
            dwu_ref[...] = acc_u[...].astype(BF16)
            dwd_ref[...] = acc_d[...].astype(BF16)

        @pl.when(j == 0)
        def _():
            dn2_acc[rows, :] = part

        @pl.when(jnp.logical_and(j > 0, j < n_ff - 1))
        def _():
            dn2_acc[rows, :] += part

        @pl.when(j == n_ff - 1)
        def _():
            dn2_ref[...] = dn2_acc[rows, :] + part

    row = pl.BlockSpec((tb, D_MODEL), lambda j, t: (t, 0))
    act = pl.BlockSpec((tb, FF_CHUNK), lambda j, t: (t, j))
    w_row = pl.BlockSpec((FF_CHUNK, D_MODEL), lambda j, t: (j, 0))
    last = pl.BlockSpec((tb, D_MODEL), lambda j, t: (jnp.where(j == n_ff - 1, t, 0), 0))
    grad = pltpu.HBM((D_FF, D_MODEL), BF16)
    chunk_acc = pltpu.VMEM((FF_CHUNK, D_MODEL), F32)
    return pl.pallas_call(
        body, name="ffn_bwd", grid=(n_ff, n_tb),
        in_specs=[row, row, act, act, w_row, w_row, w_row],
        out_specs=[w_row, w_row, w_row, last],
        out_shape=[grad, grad, grad, pltpu.HBM((s, D_MODEL), F32)],
        scratch_shapes=[pltpu.VMEM((s, D_MODEL), F32), chunk_acc, chunk_acc, chunk_acc],
        compiler_params=_params(56, 2),
    )(*_in_hbm(n2, dh2b, g, u, w_gate_t, w_up_t, w_down))


def _bwd_out(dn2, dh2, h1, y_pool, y_lru, gn_pool_g, gn_lru_g, w_out, ln2_g, after):
    s = h1.shape[0]
    tb = _token_block(s)

    def body(dn2_ref, dh2_ref, h1_ref, yp_ref, yl_ref, gp_ref, gl_ref, w_ref, g2_ref, _after,
             dh1_ref, dyp_ref, dyl_ref, dwb_ref, dg2_ref, dgp_ref, dgl_ref, dw_ref):
        @pl.when(pl.program_id(0) == 0)
        def _():
            dw_ref[...] = jnp.zeros_like(dw_ref)
            dg2_ref[...] = jnp.zeros_like(dg2_ref)
            dgp_ref[...] = jnp.zeros_like(dgp_ref)
            dgl_ref[...] = jnp.zeros_like(dgl_ref)

        g2 = g2_ref[...]
        _, xhat2, rstd2 = _rms(h1_ref[...], g2)
        dres, dg2 = _rms_bwd(dn2_ref[...], xhat2, rstd2, g2)
        dg2_ref[...] += dg2
        dh1 = dh2_ref[...] + dres
        dh1_ref[...] = dh1
        dh1b = dh1.astype(BF16)
        gp, gl = gp_ref[...], gl_ref[...]
        mp, xhat_p, rstd_p = _rms(yp_ref[...], gp)
        ml, xhat_l, rstd_l = _rms(yl_ref[...], gl)
        dw_ref[:POOL_WIDTH, :] += _mm_tn(mp.astype(BF16), dh1b)
        dw_ref[POOL_WIDTH:, :] += _mm_tn(ml.astype(BF16), dh1b)
        dyp, dgp = _rms_bwd(_mm_nt(dh1b, w_ref[:POOL_WIDTH, :]), xhat_p, rstd_p, gp)
        dyl, dgl = _rms_bwd(_mm_nt(dh1b, w_ref[POOL_WIDTH:, :]), xhat_l, rstd_l, gl)
        dyp_ref[...] = dyp
        dyl_ref[...] = dyl
        dgp_ref[...] += dgp
        dgl_ref[...] += dgl

        @pl.when(pl.program_id(0) == s // tb - 1)
        def _():
            dwb_ref[...] = dw_ref[...].astype(BF16)

    row = pl.BlockSpec((tb, D_MODEL), lambda i: (i, 0))
    half = pl.BlockSpec((tb, 512), lambda i: (i, 0))
    vec = pl.BlockSpec((1, D_MODEL), lambda i: (0, 0))
    hvec = pl.BlockSpec((1, 512), lambda i: (0, 0))
    mat = pl.BlockSpec((D_MODEL, D_MODEL), lambda i: (0, 0))
    return pl.pallas_call(
        body, name="bwd_out", grid=(s // tb,),
        in_specs=[row, row, row, half, half, hvec, hvec, mat, vec, pl.BlockSpec(memory_space=pl.ANY)],
        out_specs=[row, half, half, mat, vec, hvec, hvec],
        out_shape=[pltpu.HBM((s, D_MODEL), F32), pltpu.HBM((s, 512), F32),
                   pltpu.HBM((s, 512), F32), pltpu.HBM((D_MODEL, D_MODEL), BF16),
                   jax.ShapeDtypeStruct((1, D_MODEL), F32), jax.ShapeDtypeStruct((1, 512), F32),
                   jax.ShapeDtypeStruct((1, 512), F32)],
        scratch_shapes=[pltpu.VMEM((D_MODEL, D_MODEL), F32)],
        compiler_params=_params(48, 1),
    )(*_in_hbm(dn2, dh2, h1, y_pool, y_lru, gn_pool_g, gn_lru_g, w_out, ln2_g), after)


def _mixer_bwd(u_pool, u_lru, u_gate, h, dy_pool, dy_lru,
               pool_w, pool_scale, conv_w, conv_b, wa_bd, b_a, wi_bd, b_i, lam, after):
    s = u_pool.shape[0]
    tc = _time_chunk(s)
    n_chunks = s // tc

    def body(up_ref, ul_ref, ug_ref, h_ref, dyp_ref, dyl_ref,
             pw_ref, ps_ref, cw_ref, cb_ref, wa_ref, ba_ref, wi_ref, bi_ref, lam_ref, _after,
             dup_ref, dul_ref, dug_ref, dpw_ref, dps_ref, dcw_ref, dcb_ref, dwa_ref, dba_ref, dwi_ref, dbi_ref, dlam_ref):
        deeper, window = _slab_scalars()
        pw = pw_ref[0].astype(BF16)
        wa = wa_ref[0].astype(BF16)
        wi = wi_ref[0].astype(BF16)
        ps, cw, cb, ba, bi = ps_ref[...], cw_ref[...], cb_ref[...], ba_ref[...], bi_ref[...]
        lam_v = lam_ref[...]
        sp = _softplus_neg(lam_v)
        row = lax.broadcasted_iota(jnp.int32, (tc, SLAB), 0)
        for ref in (dpw_ref, dps_ref, dcw_ref, dcb_ref, dwa_ref, dba_ref, dwi_ref, dbi_ref, dlam_ref):
            ref[...] = jnp.zeros_like(ref)

        def chunk(t0, ext_p, ext_l, ext_h, carry, at_start=False):
            l_next, a_next, dxc_next, ddn_next = carry
            rows = pl.ds(t0, tc)
            taps = _conv_taps(ext_l)
            xc = _conv(taps, cw, cb)
            xb, r, i, a, mult = _lru_gates(xc, wa, ba, wi, bi, sp)
            hv = ext_h[HALO:]
            h_before = _down(ext_h, 1)[HALO:]
            ug = ug_ref[rows, :]
            dyl = dyl_ref[rows, :]
            gel, th = _gelu(ug)
            dug_ref[rows, :] = dyl * hv * _gelu_grad(ug, th)
            a_after = jnp.where(row == tc - 1, a_next, _up(a, 1))
            l = _scan_anticausal(a_after, dyl * gel, l_next, row, tc)
            dmult = l * (i * xc)
            di = l * mult * xc
            dxc = l * mult * i
            dla = (l * h_before) * a - jnp.where(mult > 0.0, dmult * (a * a) / mult, 0.0)
            dlam_ref[...] += jnp.sum(dla * r, axis=0, keepdims=True)
            dpa = (dla * ((-LRU_C) * sp)) * (r * (1.0 - r))
            dpi = di * (i * (1.0 - i))
            dpab = dpa.astype(BF16)
            dpib = dpi.astype(BF16)
            dwa_ref[0] += _mm_tn(xb, dpab)
            dwi_ref[0] += _mm_tn(xb, dpib)
            dba_ref[...] += jnp.sum(dpa, axis=0, keepdims=True)
            dbi_ref[...] += jnp.sum(dpi, axis=0, keepdims=True)
            dxc = dxc + _mm_nt(dpab, wa) + _mm_nt(dpib, wi)
            ext_d = jnp.concatenate([dxc, dxc_next], axis=0)
            dul_ref[rows, :] = (cw[3:4] * dxc + cw[2:3] * _up(ext_d, 1)[:tc]
                                + cw[1:2] * _up(ext_d, 2)[:tc] + cw[0:1] * _up(ext_d, 3)[:tc])
            for k in range(CONV_WIDTH):
                dcw_ref[k:k + 1, :] += jnp.sum(dxc * taps[CONV_WIDTH - 1 - k], axis=0, keepdims=True)
            dcb_ref[...] += jnp.sum(dxc, axis=0, keepdims=True)
            db = (_window_mean(_causal_window(ext_p, deeper), t0, row, window, at_start) - ext_p[HALO:]).astype(BF16)
            dyp = dyp_ref[rows, :]
            dps_ref[...] += jnp.sum(dyp * _mm(db, pw), axis=0, keepdims=True)
            dys = (dyp * ps).astype(BF16)
            dpw_ref[0] += _mm_tn(db, dys)
            dd = _mm_nt(dys, pw)
            ddn = _window_mean(dd, t0, row, window, at_start)
            ext_q = jnp.concatenate([ddn, ddn_next], axis=0)
            dup_ref[rows, :] = _anticausal_window(ext_q, deeper, tc) - dd
            return l[0:1, :], a[0:1, :], dxc[0:8, :], ddn[0:HALO, :]

        def step(k, carry):
            c = n_chunks - 1 - k
            t0 = pl.multiple_of(c * tc, tc)
            ext = pl.ds(pl.multiple_of(c * tc - HALO, HALO), tc + HALO)
            return chunk(t0, up_ref[ext, :], ul_ref[ext, :], h_ref[ext, :], carry)

        carry = (jnp.zeros((1, SLAB), F32), jnp.zeros((1, SLAB), F32),
                 jnp.zeros((8, SLAB), F32), jnp.zeros((HALO, SLAB), F32))
        carry = lax.fori_loop(0, n_chunks - 1, step, carry)
        pad = jnp.zeros((HALO, SLAB), F32)
        first = pl.ds(0, tc)
        chunk(0, jnp.concatenate([pad, up_ref[first, :]], axis=0), jnp.concatenate([pad, ul_ref[first, :]], axis=0),
              jnp.concatenate([pad, h_ref[first, :]], axis=0), carry, at_start=True)
        dlam_ref[...] = dlam_ref[...] * (LRU_C * jax.nn.sigmoid(-lam_v))

    seq, mat, vec, taps = _slab_specs(s)
    full = pltpu.HBM((s, 512), F32)
    mats = jax.ShapeDtypeStruct((N_SLAB, SLAB, SLAB), F32)
    vecs = jax.ShapeDtypeStruct((1, 512), F32)
    return pl.pallas_call(
        body, name="mixer_bwd", grid=(N_SLAB,),
        in_specs=[seq] * 6 + [mat, vec, taps, vec, mat, vec, mat, vec, vec, pl.BlockSpec(memory_space=pl.ANY)],
        out_specs=[seq, seq, seq, mat, vec, taps, vec, mat, vec, mat, vec, vec],
        out_shape=[full, full, full, mats, vecs, jax.ShapeDtypeStruct((CONV_WIDTH, 512), F32), vecs,
                   mats, vecs, mats, vecs, vecs],
        compiler_params=_params(56, 1),
    )(*_in_hbm(u_pool, u_lru, u_gate, h, dy_pool, dy_lru, pool_w, pool_scale, conv_w, conv_b, wa_bd, b_a, wi_bd,
               b_i, lam), after)


def _bwd_in(x, dh1, du_pool, du_lru, du_gate, ln1_g, w_in_t, after):
    s = x.shape[0]
    tb = _token_block(s)

    def body(x_ref, dh1_ref, dup_ref, dul_ref, dug_ref, g_ref, w_ref, _after, dx_ref, dwb_ref, dg_ref, dw_ref):
        @pl.when(pl.program_id(0) == 0)
        def _():
            dw_ref[...] = jnp.zeros_like(dw_ref)
            dg_ref[...] = jnp.zeros_like(dg_ref)

        g1 = g_ref[...]
        n, xhat, rstd = _rms(x_ref[...], g1)
        nb = n.astype(BF16)
        dn = jnp.zeros((tb, D_MODEL), F32)
        for k, ref in enumerate((dup_ref, dul_ref, dug_ref)):
            rows = slice(k * 512, (k + 1) * 512)
            db = ref[...].astype(BF16)
            dw_ref[rows, :] += _mm_tn(db, nb)
            dn = dn + _mm(db, w_ref[rows, :])
        dx, dg1 = _rms_bwd(dn, xhat, rstd, g1)
        dx_ref[...] = dh1_ref[...] + dx
        dg_ref[...] += dg1

        @pl.when(pl.program_id(0) == s // tb - 1)
        def _():
            dwb_ref[...] = dw_ref[...].astype(BF16)

    row = pl.BlockSpec((tb, D_MODEL), lambda i: (i, 0))
    half = pl.BlockSpec((tb, 512), lambda i: (i, 0))
    vec = pl.BlockSpec((1, D_MODEL), lambda i: (0, 0))
    mat = pl.BlockSpec((IN_WIDTH, D_MODEL), lambda i: (0, 0))
    return pl.pallas_call(
        body, name="bwd_in", grid=(s // tb,),
        in_specs=[row, row, half, half, half, vec, mat, pl.BlockSpec(memory_space=pl.ANY)],
        out_specs=[row, mat, vec],
        out_shape=[pltpu.HBM((s, D_MODEL), F32), pltpu.HBM((IN_WIDTH, D_MODEL), BF16),
                   jax.ShapeDtypeStruct((1, D_MODEL), F32)],
        scratch_shapes=[pltpu.VMEM((IN_WIDTH, D_MODEL), F32)],
        compiler_params=_params(48, 1),
    )(*_in_hbm(x, dh1, du_pool, du_lru, du_gate, ln1_g, w_in_t), after)


def _mesh_position():
    x, y, c = lax.axis_index("x"), lax.axis_index("y"), lax.axis_index("c")
    return x, y, c, 4 * x + 2 * y + c


def _peer(x, y, c, p):
    px = 1 - x if p & 4 else x
    py = 1 - y if p & 2 else y
    pc = 1 - c if p & 1 else c
    return (px, py, pc), 4 * px + 2 * py + pc


HBM_SPEC = pl.BlockSpec(memory_space=pltpu.HBM)
SEM_SPEC = pl.BlockSpec(memory_space=pltpu.SEMAPHORE)
DATAFLOW = pltpu.SideEffectType.DATAFLOW_SIDE_EFFECTING


class Copy(NamedTuple):
    src: int
    src_at: Any
    dst: int
    dst_at: Any
    peer: int
    group: int
    slot: int


SIBLING = (1,)
SAME_CORE = (2, 4, 6)
EVERYONE = tuple(range(1, N_DEV))


def _same(index):
    return index


def _chip(index):
    return jnp.right_shift(index, 1)


def _fan_out(srcs, lands, peers, group):
    return [Copy(s, None, d, _same, p, group, N_DEV * i + p) for i, (s, d) in enumerate(zip(srcs, lands)) for p in peers]


def _scatter(stacks, lands, peers):
    return [Copy(s, lambda me, p=p: jnp.bitwise_xor(me, p), d, _same, p, 0, 0)
            for s, d in zip(stacks, lands) for p in peers]


def _numbered(copies, group=0):
    return [cp._replace(group=group, slot=i) for i, cp in enumerate(copies)]


def _relay(lands, peers):
    return [Copy(b, lambda s, q=q: jnp.bitwise_xor(s, q), b, lambda s, q=q: jnp.bitwise_xor(s, q), 1, 0, N_DEV * i + q)
            for i, b in enumerate(lands) for q in peers]


def _to_sibling(stacks, lands):
    return [Copy(s, lambda me, k=k: 2 * k + 1 - jnp.bitwise_and(me, 1), d, lambda me, k=k: k, 1, 0, 4 * i + k)
            for i, (s, d) in enumerate(zip(stacks, lands)) for k in range(N_DEV // 2)]


def _to_chips(sums, lands):
    return [Copy(s, lambda me, p=p: jnp.bitwise_xor(_chip(me), p // 2), d, _chip, p, 0, 4 * i + p // 2)
            for i, (s, d) in enumerate(zip(sums, lands)) for p in SAME_CORE]


def _comm_call(name, bufs, wait=None, start=None, after=()):
    nb = len(bufs)
    slots = list(start[1]) if start else []
    n_out_sem = 2 * len(slots)
    after = [a for a in (after if isinstance(after, (list, tuple)) else [after]) if a is not None]

    def body(*refs):
        b = refs[:nb]
        at = nb
        if wait:
            w_send, w_recv = refs[at], refs[at + 1]
            at += 2
        at += len(after)
        out_sems = refs[at:at + n_out_sem]
        token = refs[at + n_out_sem + nb]
        x, y, c, me = _mesh_position()

        def part(i, row_of, sender):
            return b[i] if row_of is None else b[i].at[row_of(sender)]

        if wait:
            for cp in wait[1]:
                peer, peer_index = _peer(x, y, c, cp.peer)
                arrival = pltpu.make_async_remote_copy(part(cp.src, cp.src_at, me), part(cp.dst, cp.dst_at, peer_index),
                                                       w_send.at[cp.slot], w_recv.at[cp.slot],
                                                       device_id=peer, device_id_type=MESH)
                arrival.wait_send()
                arrival.wait_recv()
        if start:
            for cp in start[0]:
                peer, _ = _peer(x, y, c, cp.peer)
                pltpu.make_async_remote_copy(part(cp.src, cp.src_at, me), part(cp.dst, cp.dst_at, me),
                                             out_sems[2 * cp.group].at[cp.slot], out_sems[2 * cp.group + 1].at[cp.slot],
                                             device_id=peer, device_id_type=MESH).start()
        token[...] = jnp.zeros_like(token)

    sem_shapes = []
    for n_slots in slots:
        sem_shapes += [pltpu.SemaphoreType.DMA((n_slots,))] * 2
    operands = [pltpu.with_memory_space_constraint(a, pltpu.HBM) for a in bufs]
    in_specs = [HBM_SPEC] * nb
    if wait:
        operands += list(wait[0])
        in_specs += [SEM_SPEC, SEM_SPEC]
    operands += after
    in_specs += [pl.BlockSpec(memory_space=pl.ANY)] * len(after)
    outs = pl.pallas_call(
        body, name=name, in_specs=in_specs,
        out_specs=[SEM_SPEC] * n_out_sem + [HBM_SPEC] * nb + [pl.BlockSpec(memory_space=pltpu.VMEM)],
        out_shape=sem_shapes + [pltpu.HBM(a.shape, a.dtype) for a in bufs] + [jax.ShapeDtypeStruct((8, SLAB), F32)],
        input_output_aliases={i: n_out_sem + i for i in range(nb)},
        compiler_params=pltpu.CompilerParams(has_side_effects=DATAFLOW),
    )(*operands)
    sems = [(outs[2 * k], outs[2 * k + 1]) for k in range(len(slots))]
    return sems, list(outs[n_out_sem:n_out_sem + nb]), outs[-1]


def _pair_sum(stacks, lands, place):
    n = len(stacks)

    def body(place_ref, *refs):
        k = pl.program_id(0)
        for m in range(n):
            mine, theirs, out, land = refs[m], refs[n + m], refs[2 * n + m], refs[3 * n + m]
            total = (mine[0, 0].astype(F32) + theirs[0].astype(F32)).astype(out.dtype)
            out[0] = total

            @pl.when(k == place_ref[1])
            def _():
                land[0] = total

    in_specs = [pl.BlockSpec((1, 1) + a.shape[1:], lambda k, place_ref: (k, place_ref[0], 0, 0)) for a in stacks]
    in_specs += [pl.BlockSpec((1,) + a.shape[1:], lambda k, place_ref: (k, 0, 0)) for a in lands]
    out_specs = [pl.BlockSpec((1,) + a.shape[1:], lambda k, place_ref: (k, 0, 0)) for a in lands]
    out_specs += [pl.BlockSpec((1,) + a.shape[1:], lambda k, place_ref: (place_ref[1], 0, 0)) for a in lands]
    outs = pl.pallas_call(
        body, name="pair_sum_" + "_".join(str(a.shape[1]) for a in stacks),
        grid_spec=pltpu.PrefetchScalarGridSpec(num_scalar_prefetch=1, grid=(N_DEV // 2,), in_specs=in_specs,
                                               out_specs=out_specs),
        out_shape=[pltpu.HBM(a.shape, a.dtype) for a in lands] * 2,
        compiler_params=_params(40, 1),
    )(place, *_in_hbm(*[a.reshape((N_DEV // 2, 2) + a.shape[1:]) for a in stacks], *lands))
    return list(outs[:n]), list(outs[n:])


def _reduce_adam(parts, w, m, v, name):
    rows, cols = w.shape
    n_parts = parts.shape[0]
    rb = rows
    for cand in (256, 176, 128):
        if rows % cand == 0 and rows > cand:
            rb = cand
            break

    def body(p_ref, w_ref, m_ref, v_ref, g_out, d_out, m_out, v_out):
        g = p_ref[0].astype(F32)
        for j in range(1, n_parts):
            g = g + p_ref[j].astype(F32)
        g_out[...] = g
        d_out[...], m_out[...], v_out[...] = _adam(g, w_ref[...], m_ref[...], v_ref[...])

    blk = pl.BlockSpec((rb, cols), lambda i: (i, 0))
    out = jax.ShapeDtypeStruct((rows, cols), F32)
    return pl.pallas_call(
        body, name=name, grid=(rows // rb,),
        in_specs=[pl.BlockSpec((n_parts, rb, cols), lambda i: (0, i, 0)), blk, blk, blk],
        out_specs=[blk] * 4, out_shape=[out] * 4,
        compiler_params=_params(40, 1),
    )(*_in_hbm(parts, w, m, v))


def _cols_from_stack(stack):
    n, r, c = stack.shape
    return jnp.transpose(stack, (1, 0, 2)).reshape(r, n * c)


def _block_diag(w):
    z = jnp.zeros((N_SLAB, 64, 64), w.dtype)
    pairs = w.reshape(N_SLAB, 2, 64, 64)
    top = jnp.concatenate([pairs[:, 0], z], axis=2)
    bottom = jnp.concatenate([z, pairs[:, 1]], axis=2)
    return jnp.concatenate([top, bottom], axis=1)


def _adam(g, w, m, v):
    m_new = ADAM_B1 * m + (1.0 - ADAM_B1) * g
    v_new = ADAM_B2 * v + (1.0 - ADAM_B2) * (g * g)
    m_hat = m_new / (1.0 - ADAM_B1 ** ADAM_STEP)
    v_hat = v_new / (1.0 - ADAM_B2 ** ADAM_STEP)
    return (-ADAM_LR) * (m_hat / (jnp.sqrt(v_hat) + ADAM_EPS) + ADAM_WD * w), m_new, v_new


WIDE = ("ln2_g", "lnf_g")
HALF = ("pool_scale", "conv_b", "b_a", "b_i", "lam", "gn_pool_g", "gn_lru_g")
VECTORS = [(k, D_MODEL) for k in WIDE] + [(k, 512) for k in HALF]
VECTOR_ROWS = sum(width // SLAB for _, width in VECTORS)
LOSS_ROW = -(-VECTOR_ROWS // 8) * 8
CONV_AT = LOSS_ROW + 8
CONV_LANES = LRU_WIDTH // SLAB
PACK_F_ROWS = CONV_AT + CONV_WIDTH * CONV_LANES
MATRIX_ROWS = N_SLAB * SLAB
HEAD = SLAB // 2
GATE_ROWS = N_SLAB * HEAD
PACK_B_ROWS = MATRIX_ROWS + 2 * GATE_ROWS


def _pack_small(vectors, pool_g, wa_g, wi_g, conv, sq):
    n_vec = len(vectors)

    def body(*refs):
        vec = refs[:n_vec]
        pw_ref, wa_ref, wi_ref, cw_ref, sq_ref, out, out_b = refs[n_vec:]
        out[...] = jnp.zeros_like(out)
        row = 0
        for ref, (_, width) in zip(vec, VECTORS):
            for k in range(width // SLAB):
                out[row:row + 1, :] = ref[:, k * SLAB:(k + 1) * SLAB]
                row += 1
        for tap in range(CONV_WIDTH):
            for k in range(CONV_LANES):
                at = CONV_AT + tap * CONV_LANES + k
                out[at:at + 1, :] = cw_ref[tap:tap + 1, k * SLAB:(k + 1) * SLAB]
        total = sq_ref[:, 0:SLAB]
        for k in range(1, D_MODEL // SLAB):
            total = total + sq_ref[:, k * SLAB:(k + 1) * SLAB]
        out[LOSS_ROW:LOSS_ROW + 1, :] = total
        left = lax.broadcasted_iota(jnp.int32, (HEAD, SLAB), 1) < HEAD
        for s in range(N_SLAB):
            out_b[s * SLAB:(s + 1) * SLAB, :] = pw_ref[s].astype(BF16)
            for i, ref in enumerate((wa_ref, wi_ref)):
                at = MATRIX_ROWS + i * GATE_ROWS + s * HEAD
                out_b[at:at + HEAD, :] = jnp.where(left, ref[s, 0:HEAD, :], ref[s, HEAD:SLAB, :]).astype(BF16)

    return pl.pallas_call(
        body, name="pack_small",
        out_shape=[jax.ShapeDtypeStruct((PACK_F_ROWS, SLAB), F32), jax.ShapeDtypeStruct((PACK_B_ROWS, SLAB), BF16)],
    )(*vectors, pool_g, wa_g, wi_g, conv, sq)


def _small_reduce_adam(parts, parts_b, vec_w, vec_m, vec_v, pool_wmv):
    n_vec = len(VECTORS)
    n_parts = parts.shape[0]

    def body(*refs):
        p_ref, pb_ref = refs[0], refs[1]
        w_refs, m_refs, v_refs = (refs[2 + k * n_vec:2 + (k + 1) * n_vec] for k in range(3))
        pw_w, pw_m, pw_v = refs[2 + 3 * n_vec:5 + 3 * n_vec]
        outs = refs[5 + 3 * n_vec:-1]
        total = refs[-1]
        total[...] = p_ref[0]
        for j in range(1, n_parts):
            total[...] += p_ref[j]
        row = 0
        for i, (_, width) in enumerate(VECTORS):
            n_rows = width // SLAB
            g = jnp.concatenate([total[row + k:row + k + 1, :] for k in range(n_rows)], axis=1)
            row += n_rows
            d, m_new, v_new = _adam(g, w_refs[i][...], m_refs[i][...], v_refs[i][...])
            for ref, val in zip(outs[4 * i:4 * i + 4], (g, d, m_new, v_new)):
                ref[...] = val
        tail = outs[4 * n_vec:]

        def summed(first, count):
            g = pb_ref[0, first:first + count, :].astype(F32)
            for j in range(1, n_parts):
                g = g + pb_ref[j, first:first + count, :].astype(F32)
            return g

        g = summed(0, MATRIX_ROWS)
        d, m_new, v_new = _adam(g, pw_w[...], pw_m[...], pw_v[...])
        for ref, val in zip(tail[0:4], (g, d, m_new, v_new)):
            ref[...] = val
        tail[4][...] = summed(MATRIX_ROWS, GATE_ROWS)
        tail[5][...] = summed(MATRIX_ROWS + GATE_ROWS, GATE_ROWS)
        for tap in range(CONV_WIDTH):
            at = CONV_AT + tap * CONV_LANES
            tail[6][tap:tap + 1, :] = jnp.concatenate([total[at + k:at + k + 1, :] for k in range(CONV_LANES)], axis=1)
        tail[7][...] = (0.5 / D_MODEL) * jnp.sum(total[LOSS_ROW:LOSS_ROW + 1, :], axis=1, keepdims=True)

    out_shape = []
    for _, width in VECTORS:
        out_shape += [jax.ShapeDtypeStruct((1, width), F32)] * 4
    out_shape += [jax.ShapeDtypeStruct((MATRIX_ROWS, SLAB), F32)] * 4 + [jax.ShapeDtypeStruct((GATE_ROWS, SLAB), F32)] * 2
    out_shape += [jax.ShapeDtypeStruct((CONV_WIDTH, LRU_WIDTH), F32), jax.ShapeDtypeStruct((1, 1), F32)]
    outs = pl.pallas_call(
        body, name="adam_small", out_shape=out_shape,
        scratch_shapes=[pltpu.VMEM((PACK_F_ROWS, SLAB), F32)],
        compiler_params=_params(40),
    )(parts, parts_b, *vec_w, *vec_m, *vec_v, *pool_wmv)
    vec_out = [tuple(outs[4 * i:4 * i + 4]) for i in range(n_vec)]
    tail = outs[4 * n_vec:]
    return vec_out, tuple(tail[0:4]), tail[4], tail[5], tail[6], tail[7]


def _plain_adam(grads, ws, ms, vs):
    n = len(grads)

    def body(*refs):
        ins, outs = refs[:4 * n], refs[4 * n:]
        for i in range(n):
            d, m_new, v_new = _adam(ins[i][...], ins[n + i][...], ins[2 * n + i][...], ins[3 * n + i][...])
            for ref, val in zip(outs[3 * i:3 * i + 3], (d, m_new, v_new)):
                ref[...] = val

    out_shape = []
    for g in grads:
        out_shape += [jax.ShapeDtypeStruct(g.shape, F32)] * 3
    outs = pl.pallas_call(body, name="adam_plain", out_shape=out_shape)(*grads, *ws, *ms, *vs)
    return [tuple(outs[3 * i:3 * i + 3]) for i in range(n)]


def kernel(x, ln1_g, w_in, pool_w, pool_scale, conv_w, conv_b, w_a, b_a, w_i, b_i, lam, gn_pool_g, gn_lru_g, w_out, ln2_g, w_ffn_gate, w_ffn_up, w_ffn_down, lnf_g, loss_target, m_ln1_g, m_w_in, m_pool_w, m_pool_scale, m_conv_w, m_conv_b, m_w_a, m_b_a, m_w_i, m_b_i, m_lam, m_gn_pool_g, m_gn_lru_g, m_w_out, m_ln2_g, m_w_ffn_gate, m_w_ffn_up, m_w_ffn_down, m_lnf_g, v_ln1_g, v_w_in, v_pool_w, v_pool_scale, v_conv_w, v_conv_b, v_w_a, v_b_a, v_w_i, v_b_i, v_lam, v_gn_pool_g, v_gn_lru_g, v_w_out, v_ln2_g, v_w_ffn_gate, v_w_ffn_up, v_w_ffn_down, v_lnf_g):
    weights = dict(ln1_g=ln1_g, w_in=w_in, pool_w=pool_w, pool_scale=pool_scale, conv_w=conv_w, conv_b=conv_b,
                   w_a=w_a, b_a=b_a, w_i=w_i, b_i=b_i, lam=lam, gn_pool_g=gn_pool_g, gn_lru_g=gn_lru_g,
                   w_out=w_out, ln2_g=ln2_g, w_ffn_gate=w_ffn_gate, w_ffn_up=w_ffn_up, w_ffn_down=w_ffn_down,
                   lnf_g=lnf_g)
    mom1 = dict(ln1_g=m_ln1_g, w_in=m_w_in, pool_w=m_pool_w, pool_scale=m_pool_scale, conv_w=m_conv_w,
                conv_b=m_conv_b, w_a=m_w_a, b_a=m_b_a, w_i=m_w_i, b_i=m_b_i, lam=m_lam, gn_pool_g=m_gn_pool_g,
                gn_lru_g=m_gn_lru_g, w_out=m_w_out, ln2_g=m_ln2_g, w_ffn_gate=m_w_ffn_gate,
                w_ffn_up=m_w_ffn_up, w_ffn_down=m_w_ffn_down, lnf_g=m_lnf_g)
    mom2 = dict(ln1_g=v_ln1_g, w_in=v_w_in, pool_w=v_pool_w, pool_scale=v_pool_scale, conv_w=v_conv_w,
                conv_b=v_conv_b, w_a=v_w_a, b_a=v_b_a, w_i=v_w_i, b_i=v_b_i, lam=v_lam, gn_pool_g=v_gn_pool_g,
                gn_lru_g=v_gn_lru_g, w_out=v_w_out, ln2_g=v_ln2_g, w_ffn_gate=v_w_ffn_gate,
                w_ffn_up=v_w_ffn_up, w_ffn_down=v_w_ffn_down, lnf_g=v_lnf_g)

    xs = x[0]
    target = loss_target[0]

    shard = dict(w_in=lambda a: a[0].T, w_ffn_gate=lambda a: a[0].T, w_ffn_up=lambda a: a[0].T,
                 w_out=lambda a: a[0], w_ffn_down=lambda a: a[0], conv_w=lambda a: a[0])
    unshard = dict(w_in=lambda a: a.T[None], w_ffn_gate=lambda a: a.T[None], w_ffn_up=lambda a: a.T[None],
                   w_out=lambda a: a[None], w_ffn_down=lambda a: a[None], conv_w=lambda a: a[None])

    gathered = ("w_in", "conv_w", "w_out", "w_ffn_gate", "w_ffn_up", "w_ffn_down")
    groups = ((0, 1), (2,), (3, 4, 5))
    sources = [shard[k](weights[k]) if k == "conv_w" else shard[k](weights[k]).astype(BF16) for k in gathered]
    my_index = 4 * lax.axis_index("x") + 2 * lax.axis_index("y") + lax.axis_index("c")
    lands = [lax.dynamic_update_index_in_dim(lax.empty((N_DEV,) + a.shape, a.dtype), a, my_index, 0) for a in sources]

    def first_hop(n, group=0):
        return _numbered(_fan_out(range(n), range(n, 2 * n), SAME_CORE + SIBLING, 0), group)

    start = []
    for g, members in enumerate(groups):
        start += _numbered(_fan_out(members, [6 + m for m in members], SAME_CORE + SIBLING, 0), g)
    sems, bufs, _ = _comm_call("gather_start", sources + lands,
                               start=(start, [sum(cp.group == g for cp in start) for g in range(len(groups))]))
    sources, lands = bufs[:6], bufs[6:]

    def gathered_group(tag, g, after):
        members = groups[g]
        n = len(members)
        relay = _numbered(_relay(range(n, 2 * n), SAME_CORE))
        relay_sems, bufs, _ = _comm_call("gather_relay_" + tag, [sources[m] for m in members] + [lands[m] for m in members],
                                         wait=(sems[g], first_hop(n)), start=(relay, (len(relay),)), after=after)
        _, bufs, _ = _comm_call("gather_wait_" + tag, bufs[n:], wait=(relay_sems[0], _numbered(_relay(range(n), SAME_CORE))))
        return bufs

    g_in, g_conv = gathered_group("in", 0, None)
    w_in_f = g_in.reshape(IN_WIDTH, D_MODEL)
    conv_w_f = _cols_from_stack(g_conv)

    wa_bd = _block_diag(w_a[0])
    wi_bd = _block_diag(w_i[0])
    lnf_row = lnf_g.reshape(1, D_MODEL)

    u_pool, u_lru, u_gate = _fwd_in(xs, ln1_g, w_in_f)
    y_pool, h, y_lru = _mixer_fwd(u_pool, u_lru, u_gate, pool_w[0], pool_scale, conv_w_f, conv_b,
                                  wa_bd, b_a, wi_bd, b_i, lam)
    (g_out,) = gathered_group("out", 1, y_pool)
    w_out_f = g_out.reshape(D_MODEL, D_MODEL)
    h1, n2 = _fwd_out(xs, y_pool, y_lru, gn_pool_g, gn_lru_g, w_out_f, ln2_g)
    g_gate, g_up, g_down = gathered_group("ffn", 2, n2)
    w_gate_f = g_gate.reshape(D_FF, D_MODEL)
    w_up_f = g_up.reshape(D_FF, D_MODEL)
    w_down_f = g_down.reshape(D_FF, D_MODEL)
    g_act, u_act, dh2, dh2b, d_lnf, sq = _ffn_fwd(h1, n2, target, lnf_row, w_gate_f, w_up_f, w_down_f)

    place = jnp.stack([lax.axis_index("c"), 2 * lax.axis_index("x") + lax.axis_index("y")]).astype(jnp.int32)

    d_gate, d_up, d_down, dn2 = _ffn_bwd(n2, dh2b, g_act, u_act, w_gate_f, w_up_f, w_down_f)
    ffn_stacks = [d.reshape(N_DEV, D_FF // N_DEV, D_MODEL) for d in (d_gate, d_up, d_down)]
    pair_lands = [lax.empty((N_DEV // 2,) + a.shape[1:], a.dtype) for a in ffn_stacks]
    pair_copies = _numbered(_to_sibling(range(3), range(3, 6)))
    sem, bufs, token = _comm_call("ffn_pair_start", ffn_stacks + pair_lands, start=(pair_copies, (len(pair_copies),)))
    dh1, dy_pool, dy_lru, d_out, d_ln2, d_gnp, d_gnl = _bwd_out(dn2, dh2, h1, y_pool, y_lru, gn_pool_g, gn_lru_g,
                                                                 w_out_f, ln2_g, token)
    _, bufs, _ = _comm_call("ffn_pair_wait", bufs, wait=(sem[0], pair_copies), after=dh1)
    ffn_sums, ffn_lands = _pair_sum(bufs[:3], bufs[3:], place)
    ffn_copies = _numbered(_to_chips(range(3), range(3, 6)))
    ffn_sem, ffn_bufs, token = _comm_call("ffn_chip_start", ffn_sums + ffn_lands, start=(ffn_copies, (len(ffn_copies),)))
    (du_pool, du_lru, du_gate, d_pw, d_ps, d_cw, d_cb, d_wa, d_ba, d_wi, d_bi, d_lam) = _mixer_bwd(
        u_pool, u_lru, u_gate, h, dy_pool, dy_lru, pool_w[0], pool_scale, conv_w_f, conv_b,
        wa_bd, b_a, wi_bd, b_i, lam, token)

    def direct(tag, stacks, wholes, after):
        sources = list(stacks) + list(wholes)
        n, n_st = len(sources), len(stacks)
        lands = [lax.dynamic_update_index_in_dim(
            lax.empty(a.shape if i < n_st else (N_DEV,) + a.shape, a.dtype),
            lax.dynamic_index_in_dim(a, my_index, 0, keepdims=False) if i < n_st else a, my_index, 0)
            for i, a in enumerate(sources)]
        copies = _numbered(_scatter(range(n_st), range(n, n + n_st), EVERYONE)
                           + _fan_out(range(n_st, n), range(n + n_st, 2 * n), EVERYONE, 0))
        sem, bufs, token = _comm_call(tag + "_start", sources + lands, start=(copies, (len(copies),)), after=after)
        return (tag, sem[0], bufs, copies), token

    def direct_finish(started, after):
        tag, sem, bufs, copies = started
        _, bufs, _ = _comm_call(tag + "_wait", bufs, wait=(sem, copies), after=after)
        return bufs[len(bufs) // 2:]

    vec_grads = dict(ln2_g=d_ln2, lnf_g=d_lnf, pool_scale=d_ps, conv_b=d_cb, b_a=d_ba, b_i=d_bi,
                     lam=d_lam, gn_pool_g=d_gnp, gn_lru_g=d_gnl)
    packed, packed_b = _pack_small([vec_grads[k] for k, _ in VECTORS], d_pw, d_wa, d_wi, d_cw, sq)
    small_started, token = direct("small", [d_out.reshape(N_DEV, D_MODEL // N_DEV, D_MODEL)], [packed, packed_b], None)
    grad_x, d_in, d_ln1 = _bwd_in(xs, dh1, du_pool, du_lru, du_gate, ln1_g, w_in_f, token)
    in_started, token = direct("in", [d_in.reshape(N_DEV, IN_WIDTH // N_DEV, D_MODEL)], [d_ln1], None)

    results = {}

    def reduce_adam(name, parts):
        outs = _reduce_adam(parts, shard[name](weights[name]), shard[name](mom1[name]), shard[name](mom2[name]),
                            "adam_" + name)
        results[name] = tuple(unshard[name](o) for o in outs)
        return outs[0]

    shard["ln1_g"] = unshard["ln1_g"] = lambda a: a
    _, ffn_bufs, _ = _comm_call("ffn_chip_wait", ffn_bufs, wait=(ffn_sem[0], ffn_copies), after=token)
    done = [reduce_adam(name, parts)
            for name, parts in zip(("w_ffn_gate", "w_ffn_up", "w_ffn_down"), ffn_bufs[3:])]
    r_out, r_small, r_small_b = direct_finish(small_started, done)
    done.append(reduce_adam("w_out", r_out))

    def as_row(a, width):
        return a.reshape(1, width)

    def as_matrix(a):
        return a.reshape(MATRIX_ROWS, SLAB)

    def as_heads(a):
        return a.reshape(2 * GATE_ROWS, HEAD)

    def heads_apart(g):
        return jnp.transpose(g.reshape(N_SLAB, HEAD, 2, HEAD), (0, 2, 1, 3)).reshape(2 * GATE_ROWS, HEAD)

    vec_out, pool_out, g_wa, g_wi, g_conv, loss_11 = _small_reduce_adam(
        r_small, r_small_b, [as_row(weights[k], w) for k, w in VECTORS], [as_row(mom1[k], w) for k, w in VECTORS],
        [as_row(mom2[k], w) for k, w in VECTORS], [as_matrix(t["pool_w"]) for t in (weights, mom1, mom2)])
    for (k, _), outs in zip(VECTORS, vec_out):
        results[k] = tuple(o.reshape(weights[k].shape) for o in outs)
    results["pool_w"] = tuple(o.reshape(pool_w.shape) for o in pool_out)
    my_columns = conv_w.shape[-1]
    plain_names = ("w_a", "w_i", "conv_w")
    plain_grads = [heads_apart(g) for g in (g_wa, g_wi)]
    plain_grads.append(lax.dynamic_slice_in_dim(g_conv, my_index * my_columns, my_columns, axis=1))
    views = (as_heads, as_heads, lambda a: a[0])
    plain_out = _plain_adam(plain_grads, *[[view(t[k]) for k, view in zip(plain_names, views)]
                                           for t in (weights, mom1, mom2)])
    for k, g, outs in zip(plain_names, plain_grads, plain_out):
        results[k] = tuple(o.reshape(weights[k].shape) for o in (g,) + outs)
    loss = loss_11[0, 0]
    r_in, r_ln1 = direct_finish(in_started, done + [plain_out[0][0], loss_11])
    reduce_adam("w_in", r_in)
    reduce_adam("ln1_g", r_ln1)

    order = ["ln1_g", "w_in", "pool_w", "pool_scale", "conv_w", "conv_b", "w_a", "b_a", "w_i", "b_i", "lam",
             "gn_pool_g", "gn_lru_g", "w_out", "ln2_g", "w_ffn_gate", "w_ffn_up", "w_ffn_down", "lnf_g"]
    return (loss, grad_x[None],
            *[results[k][0] for k in order], *[results[k][1] for k in order],
            *[results[k][2] for k in order], *[results[k][3] for k in order])
```

```python
from typing import Any, NamedTuple

import jax
import jax.numpy as jnp
from jax import lax
from jax.experimental import pallas as pl
from jax.experimental.pallas import tpu as pltpu

F32 = jnp.float32
BF16 = jnp.bfloat16

N_DEV = 8
D_MODEL = 1024
POOL_WIDTH = 512
LRU_WIDTH = 512
IN_WIDTH = 1536
D_FF = 2816
N_SLAB = 4
SLAB = 128
CONV_WIDTH = 4
LRU_C = 8.0
EPS = 1e-6
HALO = 16
FF_CHUNK = 256

ADAM_LR = 0.001
ADAM_B1 = 0.9
ADAM_B2 = 0.999
ADAM_EPS = 1e-08
ADAM_WD = 0.01
ADAM_STEP = 10

MIB = 1 << 20
MESH = pl.DeviceIdType.MESH


def _params(vmem_mib, n_axes=0):
    sem = ("arbitrary",) * n_axes if n_axes else None
    return pltpu.CompilerParams(dimension_semantics=sem, vmem_limit_bytes=vmem_mib * MIB)


def _in_hbm(*arrays):
    return [pltpu.with_memory_space_constraint(a, pltpu.HBM) for a in arrays]


def _mm(a, b):
    return jnp.dot(a, b, preferred_element_type=F32)


def _mm_nt(a, b):
    return lax.dot_general(a, b, (((1,), (1,)), ((), ())), preferred_element_type=F32)


def _mm_tn(a, b):
    return lax.dot_general(a, b, (((0,), (0,)), ((), ())), preferred_element_type=F32)


def _rms(x, g):
    rstd = lax.rsqrt(jnp.mean(x * x, axis=-1, keepdims=True) + EPS)
    xhat = x * rstd
    return xhat * g, xhat, rstd


def _rms_bwd(dy, xhat, rstd, g):
    gy = dy * g
    dx = rstd * (gy - xhat * jnp.mean(gy * xhat, axis=-1, keepdims=True))
    return dx, jnp.sum(dy * xhat, axis=0, keepdims=True)


def _gelu(z):
    t = jnp.tanh(0.7978845608028654 * (z + 0.044715 * z * z * z))
    return 0.5 * z * (1.0 + t), t


def _gelu_grad(z, t):
    return 0.5 * (1.0 + t) + 0.5 * z * (1.0 - t * t) * 0.7978845608028654 * (1.0 + 3.0 * 0.044715 * z * z)


def _softplus_neg(lam):
    x = -lam
    e = jnp.exp(-jnp.abs(x))
    u = 1.0 + e
    l1p = jnp.where(u == 1.0, e, jnp.log(u) * e / (u - 1.0))
    return jnp.maximum(x, 0.0) + l1p


def _expm1(x):
    p = x * (1.0 + x * (0.5 + x * (1.0 / 6.0 + x * (1.0 / 24.0 + x * (1.0 / 120.0)))))
    return jnp.where(jnp.abs(x) < 0.1, p, jnp.exp(x) - 1.0)


def _down(v, d):
    return pltpu.roll(v, d, 0)


def _up(v, d):
    return pltpu.roll(v, v.shape[0] - d, 0)


def _token_block(s):
    return 512 if s % 512 == 0 and s > 512 else 256


def _time_chunk(s):
    return 256 if s % 256 == 0 else s


def _fwd_in(x, ln1_g, w_in_t):
    s = x.shape[0]
    tb = _token_block(s)

    def body(x_ref, g_ref, w_ref, up_ref, ul_ref, ug_ref):
        n, _, _ = _rms(x_ref[...], g_ref[...])
        proj = _mm_nt(n.astype(BF16), w_ref[...])
        up_ref[...] = proj[:, :POOL_WIDTH]
        ul_ref[...] = proj[:, POOL_WIDTH:POOL_WIDTH + LRU_WIDTH]
        ug_ref[...] = proj[:, POOL_WIDTH + LRU_WIDTH:]

    out = pltpu.HBM((s, 512), F32)
    return pl.pallas_call(
        body, name="fwd_in", grid=(s // tb,),
        in_specs=[pl.BlockSpec((tb, D_MODEL), lambda i: (i, 0)),
                  pl.BlockSpec((1, D_MODEL), lambda i: (0, 0)),
                  pl.BlockSpec((IN_WIDTH, D_MODEL), lambda i: (0, 0))],
        out_specs=[pl.BlockSpec((tb, 512), lambda i: (i, 0))] * 3,
        out_shape=[out, out, out],
        compiler_params=_params(40, 1),
    )(*_in_hbm(x, ln1_g, w_in_t))


def _pool_denominator(t0, row, window):
    return jnp.minimum((t0 + row + 1).astype(F32), window)


def _causal_window(ext, deeper):
    s = ext + _down(ext, 1)
    s = s + deeper[0] * _down(s, 2)
    s = s + deeper[1] * _down(s, 4)
    s = s + deeper[2] * _down(s, 8)
    return s[HALO:]


def _anticausal_window(ext, deeper, rows):
    s = ext + _up(ext, 1)
    s = s + deeper[0] * _up(s, 2)
    s = s + deeper[1] * _up(s, 4)
    s = s + deeper[2] * _up(s, 8)
    return s[:rows]


def _conv_taps(ext):
    return [ext[HALO:], _down(ext, 1)[HALO:], _down(ext, 2)[HALO:], _down(ext, 3)[HALO:]]


def _conv(taps, cw, cb):
    return cw[3:4] * taps[0] + cw[2:3] * taps[1] + cw[1:2] * taps[2] + cw[0:1] * taps[3] + cb


def _lru_gates(xc, wa, ba, wi, bi, sp):
    xb = xc.astype(BF16)
    r = jax.nn.sigmoid(_mm(xb, wa) + ba)
    i = jax.nn.sigmoid(_mm(xb, wi) + bi)
    la = (-LRU_C) * r * sp
    a = jnp.exp(la)
    mult = jnp.sqrt(jnp.maximum(-_expm1(2.0 * la), 0.0))
    return xb, r, i, a, mult


SUBLANES = 8


def _scan_causal(a, b, h_prev, row, rows):
    d = 1
    while d < min(SUBLANES, rows):
        head = row < d
        a_sh = jnp.where(head, 1.0, _down(a, d))
        b_sh = jnp.where(head, 0.0, _down(b, d))
        b = a * b_sh + b
        a = a * a_sh
        d *= 2
    while d < rows:
        b = jnp.concatenate([b[:d], a[d:] * b[:-d] + b[d:]], axis=0)
        a = jnp.concatenate([a[:d], a[d:] * a[:-d]], axis=0)
        d *= 2
    return b + a * h_prev


def _scan_anticausal(a, b, l_next, row, rows):
    d = 1
    while d < min(SUBLANES, rows):
        tail = row >= rows - d
        a_sh = jnp.where(tail, 1.0, _up(a, d))
        b_sh = jnp.where(tail, 0.0, _up(b, d))
        b = a * b_sh + b
        a = a * a_sh
        d *= 2
    while d < rows:
        b = jnp.concatenate([a[:-d] * b[d:] + b[:-d], b[-d:]], axis=0)
        a = jnp.concatenate([a[:-d] * a[d:], a[-d:]], axis=0)
        d *= 2
    return b + a * l_next


def _slab_scalars():
    slab = pl.program_id(0)
    deeper = [jnp.where(slab > k, 1.0, 0.0).astype(F32) for k in range(N_SLAB - 1)]
    window = jnp.left_shift(jnp.int32(2), slab).astype(F32)
    inverse = jnp.where(slab == 0, 0.5, jnp.where(slab == 1, 0.25, jnp.where(slab == 2, 0.125, 0.0625))).astype(F32)
    return deeper, (window, inverse)


def _window_mean(total, t0, row, window, at_start):
    if at_start:
        return total / _pool_denominator(t0, row, window[0])
    return total * window[1]


def _slab_specs(s):
    seq = pl.BlockSpec((s, SLAB), lambda k: (0, k))
    mat = pl.BlockSpec((1, SLAB, SLAB), lambda k: (k, 0, 0))
    vec = pl.BlockSpec((1, SLAB), lambda k: (0, k))
    taps = pl.BlockSpec((CONV_WIDTH, SLAB), lambda k: (0, k))
    return seq, mat, vec, taps


def _mixer_fwd(u_pool, u_lru, u_gate, pool_w, pool_scale, conv_w, conv_b, wa_bd, b_a, wi_bd, b_i, lam):
    s = u_pool.shape[0]
    tc = _time_chunk(s)
    n_chunks = s // tc

    def body(up_ref, ul_ref, ug_ref, pw_ref, ps_ref, cw_ref, cb_ref, wa_ref, ba_ref, wi_ref, bi_ref, lam_ref,
             yp_ref, h_ref, yl_ref):
        deeper, window = _slab_scalars()
        pw = pw_ref[0].astype(BF16)
        wa = wa_ref[0].astype(BF16)
        wi = wi_ref[0].astype(BF16)
        ps, cw, cb, ba, bi = ps_ref[...], cw_ref[...], cb_ref[...], ba_ref[...], bi_ref[...]
        sp = _softplus_neg(lam_ref[...])
        row = lax.broadcasted_iota(jnp.int32, (tc, SLAB), 0)

        def chunk(t0, ext_p, ext_l, h_prev, at_start=False):
            rows = pl.ds(t0, tc)
            d = _window_mean(_causal_window(ext_p, deeper), t0, row, window, at_start) - ext_p[HALO:]
            yp_ref[rows, :] = _mm(d.astype(BF16), pw) * ps
            xc = _conv(_conv_taps(ext_l), cw, cb)
            _, _, i, a, mult = _lru_gates(xc, wa, ba, wi, bi, sp)
            h = _scan_causal(a, mult * (i * xc), h_prev, row, tc)
            h_ref[rows, :] = h
            yl_ref[rows, :] = h * _gelu(ug_ref[rows, :])[0]
            return h[tc - 1:tc, :]

        pad = jnp.zeros((HALO, SLAB), F32)
        h0 = chunk(0, jnp.concatenate([pad, up_ref[pl.ds(0, tc), :]], axis=0),
                   jnp.concatenate([pad, ul_ref[pl.ds(0, tc), :]], axis=0), jnp.zeros((1, SLAB), F32), at_start=True)

        def step(c, h_prev):
            t0 = pl.multiple_of(c * tc, tc)
            ext = pl.ds(pl.multiple_of(c * tc - HALO, HALO), tc + HALO)
            return chunk(t0, up_ref[ext, :], ul_ref[ext, :], h_prev)

        lax.fori_loop(1, n_chunks, step, h0)

    seq, mat, vec, taps = _slab_specs(s)
    out = pltpu.HBM((s, 512), F32)
    return pl.pallas_call(
        body, name="mixer_fwd", grid=(N_SLAB,),
        in_specs=[seq, seq, seq, mat, vec, taps, vec, mat, vec, mat, vec, vec],
        out_specs=[seq, seq, seq], out_shape=[out, out, out],
        compiler_params=_params(48, 1),
    )(*_in_hbm(u_pool, u_lru, u_gate, pool_w, pool_scale, conv_w, conv_b, wa_bd, b_a, wi_bd, b_i, lam))


def _fwd_out(x, y_pool, y_lru, gn_pool_g, gn_lru_g, w_out, ln2_g):
    s = x.shape[0]
    tb = _token_block(s)

    def body(x_ref, yp_ref, yl_ref, gp_ref, gl_ref, w_ref, g2_ref, h1_ref, n2_ref):
        mp, _, _ = _rms(yp_ref[...], gp_ref[...])
        ml, _, _ = _rms(yl_ref[...], gl_ref[...])
        h1 = x_ref[...] + _mm(mp.astype(BF16), w_ref[:POOL_WIDTH, :]) + _mm(ml.astype(BF16), w_ref[POOL_WIDTH:, :])
        h1_ref[...] = h1
        n2_ref[...] = _rms(h1, g2_ref[...])[0].astype(BF16)

    row = pl.BlockSpec((tb, D_MODEL), lambda i: (i, 0))
    half = pl.BlockSpec((tb, 512), lambda i: (i, 0))
    return pl.pallas_call(
        body, name="fwd_out", grid=(s // tb,),
        in_specs=[row, half, half, pl.BlockSpec((1, 512), lambda i: (0, 0)), pl.BlockSpec((1, 512), lambda i: (0, 0)),
                  pl.BlockSpec((D_MODEL, D_MODEL), lambda i: (0, 0)), pl.BlockSpec((1, D_MODEL), lambda i: (0, 0))],
        out_specs=[row, row],
        out_shape=[pltpu.HBM((s, D_MODEL), F32), pltpu.HBM((s, D_MODEL), BF16)],
        compiler_params=_params(40, 1),
    )(*_in_hbm(x, y_pool, y_lru, gn_pool_g, gn_lru_g, w_out, ln2_g))


def _ffn_fwd(h1, n2, target, lnf_g, w_gate, w_up, w_down):
    s = h1.shape[0]
    tb = 512
    sub = 256
    n_ff = D_FF // FF_CHUNK

    def body(h1_ref, n2_ref, t_ref, gf_ref, wg_hbm, wu_hbm, wd_hbm,
             g_ref, u_ref, dh_ref, dhb_ref, dgf_ref, sq_ref, wg, wu, wd, sem):
        @pl.when(pl.program_id(0) == 0)
        def _():
            loads = [pltpu.make_async_copy(src, dst, sem.at[k])
                     for k, (src, dst) in enumerate(((wg_hbm, wg), (wu_hbm, wu), (wd_hbm, wd)))]
            for cp in loads:
                cp.start()
            for cp in loads:
                cp.wait()
            dgf_ref[...] = jnp.zeros_like(dgf_ref)
            sq_ref[...] = jnp.zeros_like(sq_ref)

        n2v = n2_ref[...]
        acc = jnp.zeros((tb, D_MODEL), F32)
        for c in range(n_ff):
            cols = slice(c * FF_CHUNK, (c + 1) * FF_CHUNK)
            g = _mm_nt(n2v, wg[cols, :])
            u = _mm_nt(n2v, wu[cols, :])
            g_ref[:, cols] = g.astype(BF16)
            u_ref[:, cols] = u.astype(BF16)
            act = g * jax.nn.sigmoid(g) * u
            acc = acc + _mm(act.astype(BF16), wd[cols, :])
        gf = gf_ref[...]
        for r in range(tb // sub):
            rows = slice(r * sub, (r + 1) * sub)
            y, xhat, rstd = _rms(h1_ref[rows, :] + acc[rows, :], gf)
            err = y - t_ref[rows, :]
            sq_ref[...] += jnp.sum(err * err, axis=0, keepdims=True)
            dh2, dgf = _rms_bwd(err * (1.0 / D_MODEL), xhat, rstd, gf)
            dgf_ref[...] += dgf
            dh_ref[rows, :] = dh2
            dhb_ref[rows, :] = dh2.astype(BF16)

    row = pl.BlockSpec((tb, D_MODEL), lambda i: (i, 0))
    ff = pl.BlockSpec((tb, D_FF), lambda i: (i, 0))
    vec = pl.BlockSpec((1, D_MODEL), lambda i: (0, 0))
    anyspace = pl.BlockSpec(memory_space=pl.ANY)
    return pl.pallas_call(
        body, name="ffn_fwd", grid=(s // tb,),
        in_specs=[row, row, row, vec, anyspace, anyspace, anyspace],
        out_specs=[ff, ff, row, row, vec, vec],
        out_shape=[pltpu.HBM((s, D_FF), BF16), pltpu.HBM((s, D_FF), BF16),
                   pltpu.HBM((s, D_MODEL), F32), pltpu.HBM((s, D_MODEL), BF16),
                   jax.ShapeDtypeStruct((1, D_MODEL), F32), jax.ShapeDtypeStruct((1, D_MODEL), F32)],
        scratch_shapes=[pltpu.VMEM((D_FF, D_MODEL), BF16), pltpu.VMEM((D_FF, D_MODEL), BF16),
                        pltpu.VMEM((D_FF, D_MODEL), BF16), pltpu.SemaphoreType.DMA((3,))],
        compiler_params=_params(60, 1),
    )(*_in_hbm(h1, n2, target, lnf_g, w_gate, w_up, w_down))


def _ffn_bwd(n2, dh2b, g, u, w_gate_t, w_up_t, w_down):
    s = n2.shape[0]
    tb = min(1024, s)
    n_ff = D_FF // FF_CHUNK
    n_tb = s // tb

    def body(n2_ref, dh_ref, g_ref, u_ref, wg_ref, wu_ref, wd_ref, dwg_ref, dwu_ref, dwd_ref, dn2_ref,
             dn2_acc, acc_g, acc_u, acc_d):
        j = pl.program_id(0)
        t = pl.program_id(1)

        @pl.when(t == 0)
        def _():
            acc_g[...] = jnp.zeros_like(acc_g)
            acc_u[...] = jnp.zeros_like(acc_u)
            acc_d[...] = jnp.zeros_like(acc_d)

        n2v = n2_ref[...]
        dh = dh_ref[...]
        gv = g_ref[...].astype(F32)
        uv = u_ref[...].astype(F32)
        sg = jax.nn.sigmoid(gv)
        silu = gv * sg
        dact = _mm_nt(dh, wd_ref[...])
        dub = (dact * silu).astype(BF16)
        dgb = (dact * uv * (sg * (1.0 + gv * (1.0 - sg)))).astype(BF16)
        acc_d[...] += _mm_tn((silu * uv).astype(BF16), dh)
        acc_g[...] += _mm_tn(dgb, n2v)
        acc_u[...] += _mm_tn(dub, n2v)
        part = _mm(jnp.concatenate([dgb, dub], axis=1), jnp.concatenate([wg_ref[...], wu_ref[...]], axis=0))
        rows = pl.ds(pl.multiple_of(t * tb, tb), tb)

        @pl.when(t == n_tb - 1)
        def _():
            dwg_ref[...] = acc_g[...].astype(BF16)
            dwu_ref[...] = acc_u[...].astype(BF16)
            dwd_ref[...] = acc_d[...].astype(BF16)

        @pl.when(j == 0)
        def _():
            dn2_acc[rows, :] = part

        @pl.when(jnp.logical_and(j > 0, j < n_ff - 1))
        def _():
            dn2_acc[rows, :] += part

        @pl.when(j == n_ff - 1)
        def _():
            dn2_ref[...] = dn2_acc[rows, :] + part

    row = pl.BlockSpec((tb, D_MODEL), lambda j, t: (t, 0))
    act = pl.BlockSpec((tb, FF_CHUNK), lambda j, t: (t, j))
    w_row = pl.BlockSpec((FF_CHUNK, D_MODEL), lambda j, t: (j, 0))
    last = pl.BlockSpec((tb, D_MODEL), lambda j, t: (jnp.where(j == n_ff - 1, t, 0), 0))
    grad = pltpu.HBM((D_FF, D_MODEL), BF16)
    chunk_acc = pltpu.VMEM((FF_CHUNK, D_MODEL), F32)
    return pl.pallas_call(
        body, name="ffn_bwd", grid=(n_ff, n_tb),
        in_specs=[row, row, act, act, w_row, w_row, w_row],
        out_specs=[w_row, w_row, w_row, last],
        out_shape=[grad, grad, grad, pltpu.HBM((s, D_MODEL), F32)],
        scratch_shapes=[pltpu.VMEM((s, D_MODEL), F32), chunk_acc, chunk_acc, chunk_acc],
        compiler_params=_params(56, 2),
    )(*_in_hbm(n2, dh2b, g, u, w_gate_t, w_up_t, w_down))


def _bwd_out(dn2, dh2, h1, y_pool, y_lru, gn_pool_g, gn_lru_g, w_out, ln2_g, after):
    s = h1.shape[0]
    tb = _token_block(s)

    def body(dn2_ref, dh2_ref, h1_ref, yp_ref, yl_ref, gp_ref, gl_ref, w_ref, g2_ref, _after,
             dh1_ref, dyp_ref, dyl_ref, dwb_ref, dg2_ref, dgp_ref, dgl_ref, dw_ref):
        @pl.when(pl.program_id(0) == 0)
        def _():
            dw_ref[...] = jnp.zeros_like(dw_ref)
            dg2_ref[...] = jnp.zeros_like(dg2_ref)
            dgp_ref[...] = jnp.zeros_like(dgp_ref)
            dgl_ref[...] = jnp.zeros_like(dgl_ref)

        g2 = g2_ref[...]
        _, xhat2, rstd2 = _rms(h1_ref[...], g2)
        dres, dg2 = _rms_bwd(dn2_ref[...], xhat2, rstd2, g2)
        dg2_ref[...] += dg2
        dh1 = dh2_ref[...] + dres
        dh1_ref[...] = dh1
        dh1b = dh1.astype(BF16)
        gp, gl = gp_ref[...], gl_ref[...]
        mp, xhat_p, rstd_p = _rms(yp_ref[...], gp)
        ml, xhat_l, rstd_l = _rms(yl_ref[...], gl)
        dw_ref[:POOL_WIDTH, :] += _mm_tn(mp.astype(BF16), dh1b)
        dw_ref[POOL_WIDTH:, :] += _mm_tn(ml.astype(BF16), dh1b)
        dyp, dgp = _rms_bwd(_mm_nt(dh1b, w_ref[:POOL_WIDTH, :]), xhat_p, rstd_p, gp)
        dyl, dgl = _rms_bwd(_mm_nt(dh1b, w_ref[POOL_WIDTH:, :]), xhat_l, rstd_l, gl)
        dyp_ref[...] = dyp
        dyl_ref[...] = dyl
        dgp_ref[...] += dgp
        dgl_ref[...] += dgl

        @pl.when(pl.program_id(0) == s // tb - 1)
        def _():
            dwb_ref[...] = dw_ref[...].astype(BF16)

    row = pl.BlockSpec((tb, D_MODEL), lambda i: (i, 0))
    half = pl.BlockSpec((tb, 512), lambda i: (i, 0))
    vec = pl.BlockSpec((1, D_MODEL), lambda i: (0, 0))
    hvec = pl.BlockSpec((1, 512), lambda i: (0, 0))
    mat = pl.BlockSpec((D_MODEL, D_MODEL), lambda i: (0, 0))
    return pl.pallas_call(
        body, name="bwd_out", grid=(s // tb,),
        in_specs=[row, row, row, half, half, hvec, hvec, mat, vec, pl.BlockSpec(memory_space=pl.ANY)],
        out_specs=[row, half, half, mat, vec, hvec, hvec],
        out_shape=[pltpu.HBM((s, D_MODEL), F32), pltpu.HBM((s, 512), F32),
                   pltpu.HBM((s, 512), F32), pltpu.HBM((D_MODEL, D_MODEL), BF16),
                   jax.ShapeDtypeStruct((1, D_MODEL), F32), jax.ShapeDtypeStruct((1, 512), F32),
                   jax.ShapeDtypeStruct((1, 512), F32)],
        scratch_shapes=[pltpu.VMEM((D_MODEL, D_MODEL), F32)],
        compiler_params=_params(48, 1),
    )(*_in_hbm(dn2, dh2, h1, y_pool, y_lru, gn_pool_g, gn_lru_g, w_out, ln2_g), after)


def _mixer_bwd(u_pool, u_lru, u_gate, h, dy_pool, dy_lru,
               pool_w, pool_scale, conv_w, conv_b, wa_bd, b_a, wi_bd, b_i, lam, after):
    s = u_pool.shape[0]
    tc = _time_chunk(s)
    n_chunks = s // tc

    def body(up_ref, ul_ref, ug_ref, h_ref, dyp_ref, dyl_ref,
             pw_ref, ps_ref, cw_ref, cb_ref, wa_ref, ba_ref, wi_ref, bi_ref, lam_ref, _after,
             dup_ref, dul_ref, dug_ref, dpw_ref, dps_ref, dcw_ref, dcb_ref, dwa_ref, dba_ref, dwi_ref, dbi_ref, dlam_ref):
        deeper, window = _slab_scalars()
        pw = pw_ref[0].astype(BF16)
        wa = wa_ref[0].astype(BF16)
        wi = wi_ref[0].astype(BF16)
        ps, cw, cb, ba, bi = ps_ref[...], cw_ref[...], cb_ref[...], ba_ref[...], bi_ref[...]
        lam_v = lam_ref[...]
        sp = _softplus_neg(lam_v)
        row = lax.broadcasted_iota(jnp.int32, (tc, SLAB), 0)
        for ref in (dpw_ref, dps_ref, dcw_ref, dcb_ref, dwa_ref, dba_ref, dwi_ref, dbi_ref, dlam_ref):
            ref[...] = jnp.zeros_like(ref)

        def chunk(t0, ext_p, ext_l, ext_h, carry, at_start=False):
            l_next, a_next, dxc_next, ddn_next = carry
            rows = pl.ds(t0, tc)
            taps = _conv_taps(ext_l)
            xc = _conv(taps, cw, cb)
            xb, r, i, a, mult = _lru_gates(xc, wa, ba, wi, bi, sp)
            hv = ext_h[HALO:]
            h_before = _down(ext_h, 1)[HALO:]
            ug = ug_ref[rows, :]
            dyl = dyl_ref[rows, :]
            gel, th = _gelu(ug)
            dug_ref[rows, :] = dyl * hv * _gelu_grad(ug, th)
            a_after = jnp.where(row == tc - 1, a_next, _up(a, 1))
            l = _scan_anticausal(a_after, dyl * gel, l_next, row, tc)
            dmult = l * (i * xc)
            di = l * mult * xc
            dxc = l * mult * i
            dla = (l * h_before) * a - jnp.where(mult > 0.0, dmult * (a * a) / mult, 0.0)
            dlam_ref[...] += jnp.sum(dla * r, axis=0, keepdims=True)
            dpa = (dla * ((-LRU_C) * sp)) * (r * (1.0 - r))
            dpi = di * (i * (1.0 - i))
            dpab = dpa.astype(BF16)
            dpib = dpi.astype(BF16)
            dwa_ref[0] += _mm_tn(xb, dpab)
            dwi_ref[0] += _mm_tn(xb, dpib)
            dba_ref[...] += jnp.sum(dpa, axis=0, keepdims=True)
            dbi_ref[...] += jnp.sum(dpi, axis=0, keepdims=True)
            dxc = dxc + _mm_nt(dpab, wa) + _mm_nt(dpib, wi)
            ext_d = jnp.concatenate([dxc, dxc_next], axis=0)
            dul_ref[rows, :] = (cw[3:4] * dxc + cw[2:3] * _up(ext_d, 1)[:tc]
                                + cw[1:2] * _up(ext_d, 2)[:tc] + cw[0:1] * _up(ext_d, 3)[:tc])
            for k in range(CONV_WIDTH):
                dcw_ref[k:k + 1, :] += jnp.sum(dxc * taps[CONV_WIDTH - 1 - k], axis=0, keepdims=True)
            dcb_ref[...] += jnp.sum(dxc, axis=0, keepdims=True)
            db = (_window_mean(_causal_window(ext_p, deeper), t0, row, window, at_start) - ext_p[HALO:]).astype(BF16)
            dyp = dyp_ref[rows, :]
            dps_ref[...] += jnp.sum(dyp * _mm(db, pw), axis=0, keepdims=True)
            dys = (dyp * ps).astype(BF16)
            dpw_ref[0] += _mm_tn(db, dys)
            dd = _mm_nt(dys, pw)
            ddn = _window_mean(dd, t0, row, window, at_start)
            ext_q = jnp.concatenate([ddn, ddn_next], axis=0)
            dup_ref[rows, :] = _anticausal_window(ext_q, deeper, tc) - dd
            return l[0:1, :], a[0:1, :], dxc[0:8, :], ddn[0:HALO, :]

        def step(k, carry):
            c = n_chunks - 1 - k
            t0 = pl.multiple_of(c * tc, tc)
            ext = pl.ds(pl.multiple_of(c * tc - HALO, HALO), tc + HALO)
            return chunk(t0, up_ref[ext, :], ul_ref[ext, :], h_ref[ext, :], carry)

        carry = (jnp.zeros((1, SLAB), F32), jnp.zeros((1, SLAB), F32),
                 jnp.zeros((8, SLAB), F32), jnp.zeros((HALO, SLAB), F32))
        carry = lax.fori_loop(0, n_chunks - 1, step, carry)
        pad = jnp.zeros((HALO, SLAB), F32)
        first = pl.ds(0, tc)
        chunk(0, jnp.concatenate([pad, up_ref[first, :]], axis=0), jnp.concatenate([pad, ul_ref[first, :]], axis=0),
              jnp.concatenate([pad, h_ref[first, :]], axis=0), carry, at_start=True)
        dlam_ref[...] = dlam_ref[...] * (LRU_C * jax.nn.sigmoid(-lam_v))

    seq, mat, vec, taps = _slab_specs(s)
    full = pltpu.HBM((s, 512), F32)
    mats = jax.ShapeDtypeStruct((N_SLAB, SLAB, SLAB), F32)
    vecs = jax.ShapeDtypeStruct((1, 512), F32)
    return pl.pallas_call(
        body, name="mixer_bwd", grid=(N_SLAB,),
        in_specs=[seq] * 6 + [mat, vec, taps, vec, mat, vec, mat, vec, vec, pl.BlockSpec(memory_space=pl.ANY)],
        out_specs=[seq, seq, seq, mat, vec, taps, vec, mat, vec, mat, vec, vec],
        out_shape=[full, full, full, mats, vecs, jax.ShapeDtypeStruct((CONV_WIDTH, 512), F32), vecs,
                   mats, vecs, mats, vecs, vecs],
        compiler_params=_params(56, 1),
    )(*_in_hbm(u_pool, u_lru, u_gate, h, dy_pool, dy_lru, pool_w, pool_scale, conv_w, conv_b, wa_bd, b_a, wi_bd,
               b_i, lam), after)


def _bwd_in(x, dh1, du_pool, du_lru, du_gate, ln1_g, w_in_t, after):
    s = x.shape[0]
    tb = _token_block(s)

    def body(x_ref, dh1_ref, dup_ref, dul_ref, dug_ref, g_ref, w_ref, _after, dx_ref, dwb_ref, dg_ref, dw_ref):
        @pl.when(pl.program_id(0) == 0)
        def _():
            dw_ref[...] = jnp.zeros_like(dw_ref)
            dg_ref[...] = jnp.zeros_like(dg_ref)

        g1 = g_ref[...]
        n, xhat, rstd = _rms(x_ref[...], g1)
        nb = n.astype(BF16)
        dn = jnp.zeros((tb, D_MODEL), F32)
        for k, ref in enumerate((dup_ref, dul_ref, dug_ref)):
            rows = slice(k * 512, (k + 1) * 512)
            db = ref[...].astype(BF16)
            dw_ref[rows, :] += _mm_tn(db, nb)
            dn = dn + _mm(db, w_ref[rows, :])
        dx, dg1 = _rms_bwd(dn, xhat, rstd, g1)
        dx_ref[...] = dh1_ref[...] + dx
        dg_ref[...] += dg1

        @pl.when(pl.program_id(0) == s // tb - 1)
        def _():
            dwb_ref[...] = dw_ref[...].astype(BF16)

    row = pl.BlockSpec((tb, D_MODEL), lambda i: (i, 0))
    half = pl.BlockSpec((tb, 512), lambda i: (i, 0))
    vec = pl.BlockSpec((1, D_MODEL), lambda i: (0, 0))
    mat = pl.BlockSpec((IN_WIDTH, D_MODEL), lambda i: (0, 0))
    return pl.pallas_call(
        body, name="bwd_in", grid=(s // tb,),
        in_specs=[row, row, half, half, half, vec, mat, pl.BlockSpec(memory_space=pl.ANY)],
        out_specs=[row, mat, vec],
        out_shape=[pltpu.HBM((s, D_MODEL), F32), pltpu.HBM((IN_WIDTH, D_MODEL), BF16),
                   jax.ShapeDtypeStruct((1, D_MODEL), F32)],
        scratch_shapes=[pltpu.VMEM((IN_WIDTH, D_MODEL), F32)],
        compiler_params=_params(48, 1),
    )(*_in_hbm(x, dh1, du_pool, du_lru, du_gate, ln1_g, w_in_t), after)


def _mesh_position():
    x, y, c = lax.axis_index("x"), lax.axis_index("y"), lax.axis_index("c")
    return x, y, c, 4 * x + 2 * y + c


def _peer(x, y, c, p):
    px = 1 - x if p & 4 else x
    py = 1 - y if p & 2 else y
    pc = 1 - c if p & 1 else c
    return (px, py, pc), 4 * px + 2 * py + pc


HBM_SPEC = pl.BlockSpec(memory_space=pltpu.HBM)
SEM_SPEC = pl.BlockSpec(memory_space=pltpu.SEMAPHORE)
DATAFLOW = pltpu.SideEffectType.DATAFLOW_SIDE_EFFECTING


class Copy(NamedTuple):
    src: int
    src_at: Any
    dst: int
    dst_at: Any
    peer: int
    group: int
    slot: int


SIBLING = (1,)
SAME_CORE = (2, 4, 6)
EVERYONE = tuple(range(1, N_DEV))


def _same(index):
    return index


def _chip(index):
    return jnp.right_shift(index, 1)


def _fan_out(srcs, lands, peers, group):
    return [Copy(s, None, d, _same, p, group, N_DEV * i + p) for i, (s, d) in enumerate(zip(srcs, lands)) for p in peers]


def _scatter(stacks, lands, peers):
    return [Copy(s, lambda me, p=p: jnp.bitwise_xor(me, p), d, _same, p, 0, 0)
            for s, d in zip(stacks, lands) for p in peers]


def _numbered(copies, group=0):
    return [cp._replace(group=group, slot=i) for i, cp in enumerate(copies)]


def _relay(lands, peers):
    return [Copy(b, lambda s, q=q: jnp.bitwise_xor(s, q), b, lambda s, q=q: jnp.bitwise_xor(s, q), 1, 0, N_DEV * i + q)
            for i, b in enumerate(lands) for q in peers]


def _to_sibling(stacks, lands):
    return [Copy(s, lambda me, k=k: 2 * k + 1 - jnp.bitwise_and(me, 1), d, lambda me, k=k: k, 1, 0, 4 * i + k)
            for i, (s, d) in enumerate(zip(stacks, lands)) for k in range(N_DEV // 2)]


def _to_chips(sums, lands):
    return [Copy(s, lambda me, p=p: jnp.bitwise_xor(_chip(me), p // 2), d, _chip, p, 0, 4 * i + p // 2)
            for i, (s, d) in enumerate(zip(sums, lands)) for p in SAME_CORE]


def _comm_call(name, bufs, wait=None, start=None, after=()):
    nb = len(bufs)
    slots = list(start[1]) if start else []
    n_out_sem = 2 * len(slots)
    after = [a for a in (after if isinstance(after, (list, tuple)) else [after]) if a is not None]

    def body(*refs):
        b = refs[:nb]
        at = nb
        if wait:
            w_send, w_recv = refs[at], refs[at + 1]
            at += 2
        at += len(after)
        out_sems = refs[at:at + n_out_sem]
        token = refs[at + n_out_sem + nb]
        x, y, c, me = _mesh_position()

        def part(i, row_of, sender):
            return b[i] if row_of is None else b[i].at[row_of(sender)]

        if wait:
            for cp in wait[1]:
                peer, peer_index = _peer(x, y, c, cp.peer)
                arrival = pltpu.make_async_remote_copy(part(cp.src, cp.src_at, me), part(cp.dst, cp.dst_at, peer_index),
                                                       w_send.at[cp.slot], w_recv.at[cp.slot],
                                                       device_id=peer, device_id_type=MESH)
                arrival.wait_send()
                arrival.wait_recv()
        if start:
            for cp in start[0]:
                peer, _ = _peer(x, y, c, cp.peer)
                pltpu.make_async_remote_copy(part(cp.src, cp.src_at, me), part(cp.dst, cp.dst_at, me),
                                             out_sems[2 * cp.group].at[cp.slot], out_sems[2 * cp.group + 1].at[cp.slot],
                                             device_id=peer, device_id_type=MESH).start()
        token[...] = jnp.zeros_like(token)

    sem_shapes = []
    for n_slots in slots:
        sem_shapes += [pltpu.SemaphoreType.DMA((n_slots,))] * 2
    operands = [pltpu.with_memory_space_constraint(a, pltpu.HBM) for a in bufs]
    in_specs = [HBM_SPEC] * nb
    if wait:
        operands += list(wait[0])
        in_specs += [SEM_SPEC, SEM_SPEC]
    operands += after
    in_specs += [pl.BlockSpec(memory_space=pl.ANY)] * len(after)
    outs = pl.pallas_call(
        body, name=name, in_specs=in_specs,
        out_specs=[SEM_SPEC] * n_out_sem + [HBM_SPEC] * nb + [pl.BlockSpec(memory_space=pltpu.VMEM)],
        out_shape=sem_shapes + [pltpu.HBM(a.shape, a.dtype) for a in bufs] + [jax.ShapeDtypeStruct((8, SLAB), F32)],
        input_output_aliases={i: n_out_sem + i for i in range(nb)},
        compiler_params=pltpu.CompilerParams(has_side_effects=DATAFLOW),
    )(*operands)
    sems = [(outs[2 * k], outs[2 * k + 1]) for k in range(len(slots))]
    return sems, list(outs[n_out_sem:n_out_sem + nb]), outs[-1]


def _pair_sum(stacks, lands, place):
    n = len(stacks)

    def body(place_ref, *refs):
        k = pl.program_id(0)
        for m in range(n):
            mine, theirs, out, land = refs[m], refs[n + m], refs[2 * n + m], refs[3 * n + m]
            total = (mine[0, 0].astype(F32) + theirs[0].astype(F32)).astype(out.dtype)
            out[0] = total

            @pl.when(k == place_ref[1])
            def _():
                land[0] = total

    in_specs = [pl.BlockSpec((1, 1) + a.shape[1:], lambda k, place_ref: (k, place_ref[0], 0, 0)) for a in stacks]
    in_specs += [pl.BlockSpec((1,) + a.shape[1:], lambda k, place_ref: (k, 0, 0)) for a in lands]
    out_specs = [pl.BlockSpec((1,) + a.shape[1:], lambda k, place_ref: (k, 0, 0)) for a in lands]
    out_specs += [pl.BlockSpec((1,) + a.shape[1:], lambda k, place_ref: (place_ref[1], 0, 0)) for a in lands]
    outs = pl.pallas_call(
        body, name="pair_sum_" + "_".join(str(a.shape[1]) for a in stacks),
        grid_spec=pltpu.PrefetchScalarGridSpec(num_scalar_prefetch=1, grid=(N_DEV // 2,), in_specs=in_specs,
                                               out_specs=out_specs),
        out_shape=[pltpu.HBM(a.shape, a.dtype) for a in lands] * 2,
        compiler_params=_params(40, 1),
    )(place, *_in_hbm(*[a.reshape((N_DEV // 2, 2) + a.shape[1:]) for a in stacks], *lands))
    return list(outs[:n]), list(outs[n:])


def _reduce_adam(parts, w, m, v, name):
    rows, cols = w.shape
    n_parts = parts.shape[0]
    rb = rows
    for cand in (256, 176, 128):
        if rows % cand == 0 and rows > cand:
            rb = cand
            break

    def body(p_ref, w_ref, m_ref, v_ref, g_out, d_out, m_out, v_out):
        g = p_ref[0].astype(F32)
        for j in range(1, n_parts):
            g = g + p_ref[j].astype(F32)
        g_out[...] = g
        d_out[...], m_out[...], v_out[...] = _adam(g, w_ref[...], m_ref[...], v_ref[...])

    blk = pl.BlockSpec((rb, cols), lambda i: (i, 0))
    out = jax.ShapeDtypeStruct((rows, cols), F32)
    return pl.pallas_call(
        body, name=name, grid=(rows // rb,),
        in_specs=[pl.BlockSpec((n_parts, rb, cols), lambda i: (0, i, 0)), blk, blk, blk],
        out_specs=[blk] * 4, out_shape=[out] * 4,
        compiler_params=_params(40, 1),
    )(*_in_hbm(parts, w, m, v))


def _cols_from_stack(stack):
    n, r, c = stack.shape
    return jnp.transpose(stack, (1, 0, 2)).reshape(r, n * c)


def _block_diag(w):
    z = jnp.zeros((N_SLAB, 64, 64), w.dtype)
    pairs = w.reshape(N_SLAB, 2, 64, 64)
    top = jnp.concatenate([pairs[:, 0], z], axis=2)
    bottom = jnp.concatenate([z, pairs[:, 1]], axis=2)
    return jnp.concatenate([top, bottom], axis=1)


def _adam(g, w, m, v):
    m_new = ADAM_B1 * m + (1.0 - ADAM_B1) * g
    v_new = ADAM_B2 * v + (1.0 - ADAM_B2) * (g * g)
    m_hat = m_new / (1.0 - ADAM_B1 ** ADAM_STEP)
    v_hat = v_new / (1.0 - ADAM_B2 ** ADAM_STEP)
    return (-ADAM_LR) * (m_hat / (jnp.sqrt(v_hat) + ADAM_EPS) + ADAM_WD * w), m_new, v_new


WIDE = ("ln2_g", "lnf_g")
HALF = ("pool_scale", "conv_b", "b_a", "b_i", "lam", "gn_pool_g", "gn_lru_g")
VECTORS = [(k, D_MODEL) for k in WIDE] + [(k, 512) for k in HALF]
VECTOR_ROWS = sum(width // SLAB for _, width in VECTORS)
LOSS_ROW = -(-VECTOR_ROWS // 8) * 8
CONV_AT = LOSS_ROW + 8
CONV_LANES = LRU_WIDTH // SLAB
PACK_F_ROWS = CONV_AT + CONV_WIDTH * CONV_LANES
MATRIX_ROWS = N_SLAB * SLAB
HEAD = SLAB // 2
GATE_ROWS = N_SLAB * HEAD
PACK_B_ROWS = MATRIX_ROWS + 2 * GATE_ROWS


def _pack_small(vectors, pool_g, wa_g, wi_g, conv, sq):
    n_vec = len(vectors)

    def body(*refs):
        vec = refs[:n_vec]
        pw_ref, wa_ref, wi_ref, cw_ref, sq_ref, out, out_b = refs[n_vec:]
        out[...] = jnp.zeros_like(out)
        row = 0
        for ref, (_, width) in zip(vec, VECTORS):
            for k in range(width // SLAB):
                out[row:row + 1, :] = ref[:, k * SLAB:(k + 1) * SLAB]
                row += 1
        for tap in range(CONV_WIDTH):
            for k in range(CONV_LANES):
                at = CONV_AT + tap * CONV_LANES + k
                out[at:at + 1, :] = cw_ref[tap:tap + 1, k * SLAB:(k + 1) * SLAB]
        total = sq_ref[:, 0:SLAB]
        for k in range(1, D_MODEL // SLAB):
            total = total + sq_ref[:, k * SLAB:(k + 1) * SLAB]
        out[LOSS_ROW:LOSS_ROW + 1, :] = total
        left = lax.broadcasted_iota(jnp.int32, (HEAD, SLAB), 1) < HEAD
        for s in range(N_SLAB):
            out_b[s * SLAB:(s + 1) * SLAB, :] = pw_ref[s].astype(BF16)
            for i, ref in enumerate((wa_ref, wi_ref)):
                at = MATRIX_ROWS + i * GATE_ROWS + s * HEAD
                out_b[at:at + HEAD, :] = jnp.where(left, ref[s, 0:HEAD, :], ref[s, HEAD:SLAB, :]).astype(BF16)

    return pl.pallas_call(
        body, name="pack_small",
        out_shape=[jax.ShapeDtypeStruct((PACK_F_ROWS, SLAB), F32), jax.ShapeDtypeStruct((PACK_B_ROWS, SLAB), BF16)],
    )(*vectors, pool_g, wa_g, wi_g, conv, sq)


def _small_reduce_adam(parts, parts_b, vec_w, vec_m, vec_v, pool_wmv):
    n_vec = len(VECTORS)
    n_parts = parts.shape[0]

    def body(*refs):
        p_ref, pb_ref = refs[0], refs[1]
        w_refs, m_refs, v_refs = (refs[2 + k * n_vec:2 + (k + 1) * n_vec] for k in range(3))
        pw_w, pw_m, pw_v = refs[2 + 3 * n_vec:5 + 3 * n_vec]
        outs = refs[5 + 3 * n_vec:-1]
        total = refs[-1]
        total[...] = p_ref[0]
        for j in range(1, n_parts):
            total[...] += p_ref[j]
        row = 0
        for i, (_, width) in enumerate(VECTORS):
            n_rows = width // SLAB
            g = jnp.concatenate([total[row + k:row + k + 1, :] for k in range(n_rows)], axis=1)
            row += n_rows
            d, m_new, v_new = _adam(g, w_refs[i][...], m_refs[i][...], v_refs[i][...])
            for ref, val in zip(outs[4 * i:4 * i + 4], (g, d, m_new, v_new)):
                ref[...] = val
        tail = outs[4 * n_vec:]

        def summed(first, count):
            g = pb_ref[0, first:first + count, :].astype(F32)
            for j in range(1, n_parts):
                g = g + pb_ref[j, first:first + count, :].astype(F32)
            return g

        g = summed(0, MATRIX_ROWS)
        d, m_new, v_new = _adam(g, pw_w[...], pw_m[...], pw_v[...])
        for ref, val in zip(tail[0:4], (g, d, m_new, v_new)):
            ref[...] = val
        tail[4][...] = summed(MATRIX_ROWS, GATE_ROWS)
        tail[5][...] = summed(MATRIX_ROWS + GATE_ROWS, GATE_ROWS)
        for tap in range(CONV_WIDTH):
            at = CONV_AT + tap * CONV_LANES
            tail[6][tap:tap + 1, :] = jnp.concatenate([total[at + k:at + k + 1, :] for k in range(CONV_LANES)], axis=1)
        tail[7][...] = (0.5 / D_MODEL) * jnp.sum(total[LOSS_ROW:LOSS_ROW + 1, :], axis=1, keepdims=True)

    out_shape = []
    for _, width in VECTORS:
        out_shape += [jax.ShapeDtypeStruct((1, width), F32)] * 4
    out_shape += [jax.ShapeDtypeStruct((MATRIX_ROWS, SLAB), F32)] * 4 + [jax.ShapeDtypeStruct((GATE_ROWS, SLAB), F32)] * 2
    out_shape += [jax.ShapeDtypeStruct((CONV_WIDTH, LRU_WIDTH), F32), jax.ShapeDtypeStruct((1, 1), F32)]
    outs = pl.pallas_call(
        body, name="adam_small", out_shape=out_shape,
        scratch_shapes=[pltpu.VMEM((PACK_F_ROWS, SLAB), F32)],
        compiler_params=_params(40),
    )(parts, parts_b, *vec_w, *vec_m, *vec_v, *pool_wmv)
    vec_out = [tuple(outs[4 * i:4 * i + 4]) for i in range(n_vec)]
    tail = outs[4 * n_vec:]
    return vec_out, tuple(tail[0:4]), tail[4], tail[5], tail[6], tail[7]


def _plain_adam(grads, ws, ms, vs):
    n = len(grads)

    def body(*refs):
        ins, outs = refs[:4 * n], refs[4 * n:]
        for i in range(n):
            d, m_new, v_new = _adam(ins[i][...], ins[n + i][...], ins[2 * n + i][...], ins[3 * n + i][...])
            for ref, val in zip(outs[3 * i:3 * i + 3], (d, m_new, v_new)):
                ref[...] = val

    out_shape = []
    for g in grads:
        out_shape += [jax.ShapeDtypeStruct(g.shape, F32)] * 3
    outs = pl.pallas_call(body, name="adam_plain", out_shape=out_shape)(*grads, *ws, *ms, *vs)
    return [tuple(outs[3 * i:3 * i + 3]) for i in range(n)]


def kernel(x, ln1_g, w_in, pool_w, pool_scale, conv_w, conv_b, w_a, b_a, w_i, b_i, lam, gn_pool_g, gn_lru_g, w_out, ln2_g, w_ffn_gate, w_ffn_up, w_ffn_down, lnf_g, loss_target, m_ln1_g, m_w_in, m_pool_w, m_pool_scale, m_conv_w, m_conv_b, m_w_a, m_b_a, m_w_i, m_b_i, m_lam, m_gn_pool_g, m_gn_lru_g, m_w_out, m_ln2_g, m_w_ffn_gate, m_w_ffn_up, m_w_ffn_down, m_lnf_g, v_ln1_g, v_w_in, v_pool_w, v_pool_scale, v_conv_w, v_conv_b, v_w_a, v_b_a, v_w_i, v_b_i, v_lam, v_gn_pool_g, v_gn_lru_g, v_w_out, v_ln2_g, v_w_ffn_gate, v_w_ffn_up, v_w_ffn_down, v_lnf_g):
    weights = dict(ln1_g=ln1_g, w_in=w_in, pool_w=pool_w, pool_scale=pool_scale, conv_w=conv_w, conv_b=conv_b,
                   w_a=w_a, b_a=b_a, w_i=w_i, b_i=b_i, lam=lam, gn_pool_g=gn_pool_g, gn_lru_g=gn_lru_g,
                   w_out=w_out, ln2_g=ln2_g, w_ffn_gate=w_ffn_gate, w_ffn_up=w_ffn_up, w_ffn_down=w_ffn_down,
                   lnf_g=lnf_g)
    mom1 = dict(ln1_g=m_ln1_g, w_in=m_w_in, pool_w=m_pool_w, pool_scale=m_pool_scale, conv_w=m_conv_w,
                conv_b=m_conv_b, w_a=m_w_a, b_a=m_b_a, w_i=m_w_i, b_i=m_b_i, lam=m_lam, gn_pool_g=m_gn_pool_g,
                gn_lru_g=m_gn_lru_g, w_out=m_w_out, ln2_g=m_ln2_g, w_ffn_gate=m_w_ffn_gate,
                w_ffn_up=m_w_ffn_up, w_ffn_down=m_w_ffn_down, lnf_g=m_lnf_g)
    mom2 = dict(ln1_g=v_ln1_g, w_in=v_w_in, pool_w=v_pool_w, pool_scale=v_pool_scale, conv_w=v_conv_w,
                conv_b=v_conv_b, w_a=v_w_a, b_a=v_b_a, w_i=v_w_i, b_i=v_b_i, lam=v_lam, gn_pool_g=v_gn_pool_g,
                gn_lru_g=v_gn_lru_g, w_out=v_w_out, ln2_g=v_ln2_g, w_ffn_gate=v_w_ffn_gate,
                w_ffn_up=v_w_ffn_up, w_ffn_down=v_w_ffn_down, lnf_g=v_lnf_g)

    xs = x[0]
    target = loss_target[0]

    shard = dict(w_in=lambda a: a[0].T, w_ffn_gate=lambda a: a[0].T, w_ffn_up=lambda a: a[0].T,
                 w_out=lambda a: a[0], w_ffn_down=lambda a: a[0], conv_w=lambda a: a[0])
    unshard = dict(w_in=lambda a: a.T[None], w_ffn_gate=lambda a: a.T[None], w_ffn_up=lambda a: a.T[None],
                   w_out=lambda a: a[None], w_ffn_down=lambda a: a[None], conv_w=lambda a: a[None])

    gathered = ("w_in", "conv_w", "w_out", "w_ffn_gate", "w_ffn_up", "w_ffn_down")
    groups = ((0, 1), (2,), (3, 4, 5))
    sources = [shard[k](weights[k]) if k == "conv_w" else shard[k](weights[k]).astype(BF16) for k in gathered]
    my_index = 4 * lax.axis_index("x") + 2 * lax.axis_index("y") + lax.axis_index("c")
    lands = [lax.dynamic_update_index_in_dim(lax.empty((N_DEV,) + a.shape, a.dtype), a, my_index, 0) for a in sources]

    def first_hop(n, group=0):
        return _numbered(_fan_out(range(n), range(n, 2 * n), SAME_CORE + SIBLING, 0), group)

    start = []
    for g, members in enumerate(groups):
        start += _numbered(_fan_out(members, [6 + m for m in members], SAME_CORE + SIBLING, 0), g)
    sems, bufs, _ = _comm_call("gather_start", sources + lands,
                               start=(start, [sum(cp.group == g for cp in start) for g in range(len(groups))]))
    sources, lands = bufs[:6], bufs[6:]

    def gathered_group(tag, g, after):
        members = groups[g]
        n = len(members)
        relay = _numbered(_relay(range(n, 2 * n), SAME_CORE))
        relay_sems, bufs, _ = _comm_call("gather_relay_" + tag, [sources[m] for m in members] + [lands[m] for m in members],
                                         wait=(sems[g], first_hop(n)), start=(relay, (len(relay),)), after=after)
        _, bufs, _ = _comm_call("gather_wait_" + tag, bufs[n:], wait=(relay_sems[0], _numbered(_relay(range(n), SAME_CORE))))
        return bufs

    g_in, g_conv = gathered_group("in", 0, None)
    w_in_f = g_in.reshape(IN_WIDTH, D_MODEL)
    conv_w_f = _cols_from_stack(g_conv)

    wa_bd = _block_diag(w_a[0])
    wi_bd = _block_diag(w_i[0])
    lnf_row = lnf_g.reshape(1, D_MODEL)

    u_pool, u_lru, u_gate = _fwd_in(xs, ln1_g, w_in_f)
    y_pool, h, y_lru = _mixer_fwd(u_pool, u_lru, u_gate, pool_w[0], pool_scale, conv_w_f, conv_b,
                                  wa_bd, b_a, wi_bd, b_i, lam)
    (g_out,) = gathered_group("out", 1, y_pool)
    w_out_f = g_out.reshape(D_MODEL, D_MODEL)
    h1, n2 = _fwd_out(xs, y_pool, y_lru, gn_pool_g, gn_lru_g, w_out_f, ln2_g)
    g_gate, g_up, g_down = gathered_group("ffn", 2, n2)
    w_gate_f = g_gate.reshape(D_FF, D_MODEL)
    w_up_f = g_up.reshape(D_FF, D_MODEL)
    w_down_f = g_down.reshape(D_FF, D_MODEL)
    g_act, u_act, dh2, dh2b, d_lnf, sq = _ffn_fwd(h1, n2, target, lnf_row, w_gate_f, w_up_f, w_down_f)

    place = jnp.stack([lax.axis_index("c"), 2 * lax.axis_index("x") + lax.axis_index("y")]).astype(jnp.int32)

    d_gate, d_up, d_down, dn2 = _ffn_bwd(n2, dh2b, g_act, u_act, w_gate_f, w_up_f, w_down_f)
    ffn_stacks = [d.reshape(N_DEV, D_FF // N_DEV, D_MODEL) for d in (d_gate, d_up, d_down)]
    pair_lands = [lax.empty((N_DEV // 2,) + a.shape[1:], a.dtype) for a in ffn_stacks]
    pair_copies = _numbered(_to_sibling(range(3), range(3, 6)))
    sem, bufs, token = _comm_call("ffn_pair_start", ffn_stacks + pair_lands, start=(pair_copies, (len(pair_copies),)))
    dh1, dy_pool, dy_lru, d_out, d_ln2, d_gnp, d_gnl = _bwd_out(dn2, dh2, h1, y_pool, y_lru, gn_pool_g, gn_lru_g,
                                                                 w_out_f, ln2_g, token)
    _, bufs, _ = _comm_call("ffn_pair_wait", bufs, wait=(sem[0], pair_copies), after=dh1)
    ffn_sums, ffn_lands = _pair_sum(bufs[:3], bufs[3:], place)
    ffn_copies = _numbered(_to_chips(range(3), range(3, 6)))
    ffn_sem, ffn_bufs, token = _comm_call("ffn_chip_start", ffn_sums + ffn_lands, start=(ffn_copies, (len(ffn_copies),)))
    (du_pool, du_lru, du_gate, d_pw, d_ps, d_cw, d_cb, d_wa, d_ba, d_wi, d_bi, d_lam) = _mixer_bwd(
        u_pool, u_lru, u_gate, h, dy_pool, dy_lru, pool_w[0], pool_scale, conv_w_f, conv_b,
        wa_bd, b_a, wi_bd, b_i, lam, token)

    def direct(tag, stacks, wholes, after):
        sources = list(stacks) + list(wholes)
        n, n_st = len(sources), len(stacks)
        lands = [lax.dynamic_update_index_in_dim(
            lax.empty(a.shape if i < n_st else (N_DEV,) + a.shape, a.dtype),
            lax.dynamic_index_in_dim(a, my_index, 0, keepdims=False) if i < n_st else a, my_index, 0)
            for i, a in enumerate(sources)]
        copies = _numbered(_scatter(range(n_st), range(n, n + n_st), EVERYONE)
                           + _fan_out(range(n_st, n), range(n + n_st, 2 * n), EVERYONE, 0))
        sem, bufs, token = _comm_call(tag + "_start", sources + lands, start=(copies, (len(copies),)), after=after)
        return (tag, sem[0], bufs, copies), token

    def direct_finish(started, after):
        tag, sem, bufs, copies = started
        _, bufs, _ = _comm_call(tag + "_wait", bufs, wait=(sem, copies), after=after)
        return bufs[len(bufs) // 2:]

    vec_grads = dict(ln2_g=d_ln2, lnf_g=d_lnf, pool_scale=d_ps, conv_b=d_cb, b_a=d_ba, b_i=d_bi,
                     lam=d_lam, gn_pool_g=d_gnp, gn_lru_g=d_gnl)
    packed, packed_b = _pack_small([vec_grads[k] for k, _ in VECTORS], d_pw, d_wa, d_wi, d_cw, sq)
    small_started, token = direct("small", [d_out.reshape(N_DEV, D_MODEL // N_DEV, D_MODEL)], [packed, packed_b], None)
    grad_x, d_in, d_ln1 = _bwd_in(xs, dh1, du_pool, du_lru, du_gate, ln1_g, w_in_f, token)
    in_started, token = direct("in", [d_in.reshape(N_DEV, IN_WIDTH // N_DEV, D_MODEL)], [d_ln1], None)

    results = {}

    def reduce_adam(name, parts):
        outs = _reduce_adam(parts, shard[name](weights[name]), shard[name](mom1[name]), shard[name](mom2[name]),
                            "adam_" + name)
        results[name] = tuple(unshard[name](o) for o in outs)
        return outs[0]

    shard["ln1_g"] = unshard["ln1_g"] = lambda a: a
    _, ffn_bufs, _ = _comm_call("ffn_chip_wait", ffn_bufs, wait=(ffn_sem[0], ffn_copies), after=token)
    done = [reduce_adam(name, parts)
            for name, parts in zip(("w_ffn_gate", "w_ffn_up", "w_ffn_down"), ffn_bufs[3:])]
    r_out, r_small, r_small_b = direct_finish(small_started, done)
    done.append(reduce_adam("w_out", r_out))

    def as_row(a, width):
        return a.reshape(1, width)

    def as_matrix(a):
        return a.reshape(MATRIX_ROWS, SLAB)

    def as_heads(a):
        return a.reshape(2 * GATE_ROWS, HEAD)

    def heads_apart(g):
        return jnp.transpose(g.reshape(N_SLAB, HEAD, 2, HEAD), (0, 2, 1, 3)).reshape(2 * GATE_ROWS, HEAD)

    vec_out, pool_out, g_wa, g_wi, g_conv, loss_11 = _small_reduce_adam(
        r_small, r_small_b, [as_row(weights[k], w) for k, w in VECTORS], [as_row(mom1[k], w) for k, w in VECTORS],
        [as_row(mom2[k], w) for k, w in VECTORS], [as_matrix(t["pool_w"]) for t in (weights, mom1, mom2)])
    for (k, _), outs in zip(VECTORS, vec_out):
        results[k] = tuple(o.reshape(weights[k].shape) for o in outs)
    results["pool_w"] = tuple(o.reshape(pool_w.shape) for o in pool_out)
    my_columns = conv_w.shape[-1]
    plain_names = ("w_a", "w_i", "conv_w")
    plain_grads = [heads_apart(g) for g in (g_wa, g_wi)]
    plain_grads.append(lax.dynamic_slice_in_dim(g_conv, my_index * my_columns, my_columns, axis=1))
    views = (as_heads, as_heads, lambda a: a[0])
    plain_out = _plain_adam(plain_grads, *[[view(t[k]) for k, view in zip(plain_names, views)]
                                           for t in (weights, mom1, mom2)])
    for k, g, outs in zip(plain_names, plain_grads, plain_out):
        results[k] = tuple(o.reshape(weights[k].shape) for o in (g,) + outs)
    loss = loss_11[0, 0]
    r_in, r_ln1 = direct_finish(in_started, done + [plain_out[0][0], loss_11])
    reduce_adam("w_in", r_in)
    reduce_adam("ln1_g", r_ln1)

    order = ["ln1_g", "w_in", "pool_w", "pool_scale", "conv_w", "conv_b", "w_a", "b_a", "w_i", "b_i", "lam",
             "gn_pool_g", "gn_lru_g", "w_out", "ln2_g", "w_ffn_gate", "w_ffn_up", "w_ffn_down", "lnf_g"]
    return (loss, grad_x[None],
            *[results[k][0] for k in order], *[results[k][1] for k in order],
            *[results[k][2] for k in order], *[results[k][3] for k in order])
```

```python
from typing import Any, NamedTuple

import jax
import jax.numpy as jnp
from jax import lax
from jax.experimental import pallas as pl
from jax.experimental.pallas import tpu as pltpu

F32 = jnp.float32
BF16 = jnp.bfloat16

N_DEV = 8
D_MODEL = 1024
POOL_WIDTH = 512
LRU_WIDTH = 512
IN_WIDTH = 1536
D_FF = 2816
N_SLAB = 4
SLAB = 128
CONV_WIDTH = 4
LRU_C = 8.0
EPS = 1e-6
HALO = 16
FF_CHUNK = 256

ADAM_LR = 0.001
ADAM_B1 = 0.9
ADAM_B2 = 0.999
ADAM_EPS = 1e-08
ADAM_WD = 0.01
ADAM_STEP = 10

MIB = 1 << 20
MESH = pl.DeviceIdType.MESH


def _params(vmem_mib, n_axes=0):
    sem = ("arbitrary",) * n_axes if n_axes else None
    return pltpu.CompilerParams(dimension_semantics=sem, vmem_limit_bytes=vmem_mib * MIB)


def _in_hbm(*arrays):
    return [pltpu.with_memory_space_constraint(a, pltpu.HBM) for a in arrays]


def _mm(a, b):
    return jnp.dot(a, b, preferred_element_type=F32)


def _mm_nt(a, b):
    return lax.dot_general(a, b, (((1,), (1,)), ((), ())), preferred_element_type=F32)


def _mm_tn(a, b):
    return lax.dot_general(a, b, (((0,), (0,)), ((), ())), preferred_element_type=F32)


def _rms(x, g):
    rstd = lax.rsqrt(jnp.mean(x * x, axis=-1, keepdims=True) + EPS)
    xhat = x * rstd
    return xhat * g, xhat, rstd


def _rms_bwd(dy, xhat, rstd, g):
    gy = dy * g
    dx = rstd * (gy - xhat * jnp.mean(gy * xhat, axis=-1, keepdims=True))
    return dx, jnp.sum(dy * xhat, axis=0, keepdims=True)


def _gelu(z):
    t = jnp.tanh(0.7978845608028654 * (z + 0.044715 * z * z * z))
    return 0.5 * z * (1.0 + t), t


def _gelu_grad(z, t):
    return 0.5 * (1.0 + t) + 0.5 * z * (1.0 - t * t) * 0.7978845608028654 * (1.0 + 3.0 * 0.044715 * z * z)


def _softplus_neg(lam):
    x = -lam
    e = jnp.exp(-jnp.abs(x))
    u = 1.0 + e
    l1p = jnp.where(u == 1.0, e, jnp.log(u) * e / (u - 1.0))
    return jnp.maximum(x, 0.0) + l1p


def _expm1(x):
    p = x * (1.0 + x * (0.5 + x * (1.0 / 6.0 + x * (1.0 / 24.0 + x * (1.0 / 120.0)))))
    return jnp.where(jnp.abs(x) < 0.1, p, jnp.exp(x) - 1.0)


def _down(v, d):
    return pltpu.roll(v, d, 0)


def _up(v, d):
    return pltpu.roll(v, v.shape[0] - d, 0)


def _token_block(s):
    return 512 if s % 512 == 0 and s > 512 else 256


def _time_chunk(s):
    return 256 if s % 256 == 0 else s


def _fwd_in(x, ln1_g, w_in_t):
    s = x.shape[0]
    tb = _token_block(s)

    def body(x_ref, g_ref, w_ref, up_ref, ul_ref, ug_ref):
        n, _, _ = _rms(x_ref[...], g_ref[...])
        proj = _mm_nt(n.astype(BF16), w_ref[...])
        up_ref[...] = proj[:, :POOL_WIDTH]
        ul_ref[...] = proj[:, POOL_WIDTH:POOL_WIDTH + LRU_WIDTH]
        ug_ref[...] = proj[:, POOL_WIDTH + LRU_WIDTH:]

    out = pltpu.HBM((s, 512), F32)
    return pl.pallas_call(
        body, name="fwd_in", grid=(s // tb,),
        in_specs=[pl.BlockSpec((tb, D_MODEL), lambda i: (i, 0)),
                  pl.BlockSpec((1, D_MODEL), lambda i: (0, 0)),
                  pl.BlockSpec((IN_WIDTH, D_MODEL), lambda i: (0, 0))],
        out_specs=[pl.BlockSpec((tb, 512), lambda i: (i, 0))] * 3,
        out_shape=[out, out, out],
        compiler_params=_params(40, 1),
    )(*_in_hbm(x, ln1_g, w_in_t))


def _pool_denominator(t0, row, window):
    return jnp.minimum((t0 + row + 1).astype(F32), window)


def _causal_window(ext, deeper):
    s = ext + _down(ext, 1)
    s = s + deeper[0] * _down(s, 2)
    s = s + deeper[1] * _down(s, 4)
    s = s + deeper[2] * _down(s, 8)
    return s[HALO:]


def _anticausal_window(ext, deeper, rows):
    s = ext + _up(ext, 1)
    s = s + deeper[0] * _up(s, 2)
    s = s + deeper[1] * _up(s, 4)
    s = s + deeper[2] * _up(s, 8)
    return s[:rows]


def _conv_taps(ext):
    return [ext[HALO:], _down(ext, 1)[HALO:], _down(ext, 2)[HALO:], _down(ext, 3)[HALO:]]


def _conv(taps, cw, cb):
    return cw[3:4] * taps[0] + cw[2:3] * taps[1] + cw[1:2] * taps[2] + cw[0:1] * taps[3] + cb


def _lru_gates(xc, wa, ba, wi, bi, sp):
    xb = xc.astype(BF16)
    r = jax.nn.sigmoid(_mm(xb, wa) + ba)
    i = jax.nn.sigmoid(_mm(xb, wi) + bi)
    la = (-LRU_C) * r * sp
    a = jnp.exp(la)
    mult = jnp.sqrt(jnp.maximum(-_expm1(2.0 * la), 0.0))
    return xb, r, i, a, mult


SUBLANES = 8


def _scan_causal(a, b, h_prev, row, rows):
    d = 1
    while d < min(SUBLANES, rows):
        head = row < d
        a_sh = jnp.where(head, 1.0, _down(a, d))
        b_sh = jnp.where(head, 0.0, _down(b, d))
        b = a * b_sh + b
        a = a * a_sh
        d *= 2
    while d < rows:
        b = jnp.concatenate([b[:d], a[d:] * b[:-d] + b[d:]], axis=0)
        a = jnp.concatenate([a[:d], a[d:] * a[:-d]], axis=0)
        d *= 2
    return b + a * h_prev


def _scan_anticausal(a, b, l_next, row, rows):
    d = 1
    while d < min(SUBLANES, rows):
        tail = row >= rows - d
        a_sh = jnp.where(tail, 1.0, _up(a, d))
        b_sh = jnp.where(tail, 0.0, _up(b, d))
        b = a * b_sh + b
        a = a * a_sh
        d *= 2
    while d < rows:
        b = jnp.concatenate([a[:-d] * b[d:] + b[:-d], b[-d:]], axis=0)
        a = jnp.concatenate([a[:-d] * a[d:], a[-d:]], axis=0)
        d *= 2
    return b + a * l_next


def _slab_scalars():
    slab = pl.program_id(0)
    deeper = [jnp.where(slab > k, 1.0, 0.0).astype(F32) for k in range(N_SLAB - 1)]
    window = jnp.left_shift(jnp.int32(2), slab).astype(F32)
    inverse = jnp.where(slab == 0, 0.5, jnp.where(slab == 1, 0.25, jnp.where(slab == 2, 0.125, 0.0625))).astype(F32)
    return deeper, (window, inverse)


def _window_mean(total, t0, row, window, at_start):
    if at_start:
        return total / _pool_denominator(t0, row, window[0])
    return total * window[1]


def _slab_specs(s):
    seq = pl.BlockSpec((s, SLAB), lambda k: (0, k))
    mat = pl.BlockSpec((1, SLAB, SLAB), lambda k: (k, 0, 0))
    vec = pl.BlockSpec((1, SLAB), lambda k: (0, k))
    taps = pl.BlockSpec((CONV_WIDTH, SLAB), lambda k: (0, k))
    return seq, mat, vec, taps


def _mixer_fwd(u_pool, u_lru, u_gate, pool_w, pool_scale, conv_w, conv_b, wa_bd, b_a, wi_bd, b_i, lam):
    s = u_pool.shape[0]
    tc = _time_chunk(s)
    n_chunks = s // tc

    def body(up_ref, ul_ref, ug_ref, pw_ref, ps_ref, cw_ref, cb_ref, wa_ref, ba_ref, wi_ref, bi_ref, lam_ref,
             yp_ref, h_ref, yl_ref):
        deeper, window = _slab_scalars()
        pw = pw_ref[0].astype(BF16)
        wa = wa_ref[0].astype(BF16)
        wi = wi_ref[0].astype(BF16)
        ps, cw, cb, ba, bi = ps_ref[...], cw_ref[...], cb_ref[...], ba_ref[...], bi_ref[...]
        sp = _softplus_neg(lam_ref[...])
        row = lax.broadcasted_iota(jnp.int32, (tc, SLAB), 0)

        def chunk(t0, ext_p, ext_l, h_prev, at_start=False):
            rows = pl.ds(t0, tc)
            d = _window_mean(_causal_window(ext_p, deeper), t0, row, window, at_start) - ext_p[HALO:]
            yp_ref[rows, :] = _mm(d.astype(BF16), pw) * ps
            xc = _conv(_conv_taps(ext_l), cw, cb)
            _, _, i, a, mult = _lru_gates(xc, wa, ba, wi, bi, sp)
            h = _scan_causal(a, mult * (i * xc), h_prev, row, tc)
            h_ref[rows, :] = h
            yl_ref[rows, :] = h * _gelu(ug_ref[rows, :])[0]
            return h[tc - 1:tc, :]

        pad = jnp.zeros((HALO, SLAB), F32)
        h0 = chunk(0, jnp.concatenate([pad, up_ref[pl.ds(0, tc), :]], axis=0),
                   jnp.concatenate([pad, ul_ref[pl.ds(0, tc), :]], axis=0), jnp.zeros((1, SLAB), F32), at_start=True)

        def step(c, h_prev):
            t0 = pl.multiple_of(c * tc, tc)
            ext = pl.ds(pl.multiple_of(c * tc - HALO, HALO), tc + HALO)
            return chunk(t0, up_ref[ext, :], ul_ref[ext, :], h_prev)

        lax.fori_loop(1, n_chunks, step, h0)

    seq, mat, vec, taps = _slab_specs(s)
    out = pltpu.HBM((s, 512), F32)
    return pl.pallas_call(
        body, name="mixer_fwd", grid=(N_SLAB,),
        in_specs=[seq, seq, seq, mat, vec, taps, vec, mat, vec, mat, vec, vec],
        out_specs=[seq, seq, seq], out_shape=[out, out, out],
        compiler_params=_params(48, 1),
    )(*_in_hbm(u_pool, u_lru, u_gate, pool_w, pool_scale, conv_w, conv_b, wa_bd, b_a, wi_bd, b_i, lam))


def _fwd_out(x, y_pool, y_lru, gn_pool_g, gn_lru_g, w_out, ln2_g):
    s = x.shape[0]
    tb = _token_block(s)

    def body(x_ref, yp_ref, yl_ref, gp_ref, gl_ref, w_ref, g2_ref, h1_ref, n2_ref):
        mp, _, _ = _rms(yp_ref[...], gp_ref[...])
        ml, _, _ = _rms(yl_ref[...], gl_ref[...])
        h1 = x_ref[...] + _mm(mp.astype(BF16), w_ref[:POOL_WIDTH, :]) + _mm(ml.astype(BF16), w_ref[POOL_WIDTH:, :])
        h1_ref[...] = h1
        n2_ref[...] = _rms(h1, g2_ref[...])[0].astype(BF16)

    row = pl.BlockSpec((tb, D_MODEL), lambda i: (i, 0))
    half = pl.BlockSpec((tb, 512), lambda i: (i, 0))
    return pl.pallas_call(
        body, name="fwd_out", grid=(s // tb,),
        in_specs=[row, half, half, pl.BlockSpec((1, 512), lambda i: (0, 0)), pl.BlockSpec((1, 512), lambda i: (0, 0)),
                  pl.BlockSpec((D_MODEL, D_MODEL), lambda i: (0, 0)), pl.BlockSpec((1, D_MODEL), lambda i: (0, 0))],
        out_specs=[row, row],
        out_shape=[pltpu.HBM((s, D_MODEL), F32), pltpu.HBM((s, D_MODEL), BF16)],
        compiler_params=_params(40, 1),
    )(*_in_hbm(x, y_pool, y_lru, gn_pool_g, gn_lru_g, w_out, ln2_g))


def _ffn_fwd(h1, n2, target, lnf_g, w_gate, w_up, w_down):
    s = h1.shape[0]
    tb = 512
    sub = 256
    n_ff = D_FF // FF_CHUNK

    def body(h1_ref, n2_ref, t_ref, gf_ref, wg_hbm, wu_hbm, wd_hbm,
             g_ref, u_ref, dh_ref, dhb_ref, dgf_ref, sq_ref, wg, wu, wd, sem):
        @pl.when(pl.program_id(0) == 0)
        def _():
            loads = [pltpu.make_async_copy(src, dst, sem.at[k])
                     for k, (src, dst) in enumerate(((wg_hbm, wg), (wu_hbm, wu), (wd_hbm, wd)))]
            for cp in loads:
                cp.start()
            for cp in loads:
                cp.wait()
            dgf_ref[...] = jnp.zeros_like(dgf_ref)
            sq_ref[...] = jnp.zeros_like(sq_ref)

        n2v = n2_ref[...]
        acc = jnp.zeros((tb, D_MODEL), F32)
        for c in range(n_ff):
            cols = slice(c * FF_CHUNK, (c + 1) * FF_CHUNK)
            g = _mm_nt(n2v, wg[cols, :])
            u = _mm_nt(n2v, wu[cols, :])
            g_ref[:, cols] = g.astype(BF16)
            u_ref[:, cols] = u.astype(BF16)
            act = g * jax.nn.sigmoid(g) * u
            acc = acc + _mm(act.astype(BF16), wd[cols, :])
        gf = gf_ref[...]
        for r in range(tb // sub):
            rows = slice(r * sub, (r + 1) * sub)
            y, xhat, rstd = _rms(h1_ref[rows, :] + acc[rows, :], gf)
            err = y - t_ref[rows, :]
            sq_ref[...] += jnp.sum(err * err, axis=0, keepdims=True)
            dh2, dgf = _rms_bwd(err * (1.0 / D_MODEL), xhat, rstd, gf)
            dgf_ref[...] += dgf
            dh_ref[rows, :] = dh2
            dhb_ref[rows, :] = dh2.astype(BF16)

    row = pl.BlockSpec((tb, D_MODEL), lambda i: (i, 0))
    ff = pl.BlockSpec((tb, D_FF), lambda i: (i, 0))
    vec = pl.BlockSpec((1, D_MODEL), lambda i: (0, 0))
    anyspace = pl.BlockSpec(memory_space=pl.ANY)
    return pl.pallas_call(
        body, name="ffn_fwd", grid=(s // tb,),
        in_specs=[row, row, row, vec, anyspace, anyspace, anyspace],
        out_specs=[ff, ff, row, row, vec, vec],
        out_shape=[pltpu.HBM((s, D_FF), BF16), pltpu.HBM((s, D_FF), BF16),
                   pltpu.HBM((s, D_MODEL), F32), pltpu.HBM((s, D_MODEL), BF16),
                   jax.ShapeDtypeStruct((1, D_MODEL), F32), jax.ShapeDtypeStruct((1, D_MODEL), F32)],
        scratch_shapes=[pltpu.VMEM((D_FF, D_MODEL), BF16), pltpu.VMEM((D_FF, D_MODEL), BF16),
                        pltpu.VMEM((D_FF, D_MODEL), BF16), pltpu.SemaphoreType.DMA((3,))],
        compiler_params=_params(60, 1),
    )(*_in_hbm(h1, n2, target, lnf_g, w_gate, w_up, w_down))


def _ffn_bwd(n2, dh2b, g, u, w_gate_t, w_up_t, w_down):
    s = n2.shape[0]
    tb = min(1024, s)
    sub = 256
    n_ff = D_FF // FF_CHUNK
    n_tb = s // tb

    def body(n2_hbm, dh_hbm, g_ref, u_ref, wg_ref, wu_ref, wd_ref, dwg_ref, dwu_ref, dwd_ref, dn2_hbm,
             n2_v, dh_v, dn2_acc, dg_s, du_s, act_s, sem):
        j = pl.program_id(0)
        t = pl.program_id(1)

        @pl.when(jnp.logical_and(j == 0, t == 0))
        def _():
            loads = [pltpu.make_async_copy(n2_hbm, n2_v, sem.at[0]), pltpu.make_async_copy(dh_hbm, dh_v, sem.at[1])]
            for cp in loads:
                cp.start()
            dn2_acc[...] = jnp.zeros_like(dn2_acc)
            for cp in loads:
                cp.wait()

        wd = wd_ref[...]
        w_both = jnp.concatenate([wg_ref[...], wu_ref[...]], axis=0)
        for r in range(tb // sub):
            local = slice(r * sub, (r + 1) * sub)
            rows = pl.ds(pl.multiple_of(t * tb, tb) + r * sub, sub)
            gv = g_ref[local, :].astype(F32)
            uv = u_ref[local, :].astype(F32)
            sg = jax.nn.sigmoid(gv)
            silu = gv * sg
            dact = _mm_nt(dh_v[rows, :], wd)
            dub = (dact * silu).astype(BF16)
            dgb = (dact * uv * (sg * (1.0 + gv * (1.0 - sg)))).astype(BF16)
            du_s[rows, :] = dub
            dg_s[rows, :] = dgb
            act_s[rows, :] = (silu * uv).astype(BF16)
            dn2_acc[rows, :] += _mm(jnp.concatenate([dgb, dub], axis=1), w_both)

        @pl.when(t == n_tb - 1)
        def _():
            dwd_ref[...] = _mm_tn(act_s[...], dh_v[...]).astype(BF16)
            dwg_ref[...] = _mm_tn(dg_s[...], n2_v[...]).astype(BF16)
            dwu_ref[...] = _mm_tn(du_s[...], n2_v[...]).astype(BF16)

        def dn2_out(block):
            rows = pl.ds(block * tb, tb)
            return pltpu.make_async_copy(dn2_acc.at[rows, :], dn2_hbm.at[rows, :], sem.at[2 + block])

        for block in range(n_tb):
            @pl.when(jnp.logical_and(j == n_ff - 1, t == block))
            def _(block=block):
                dn2_out(block).start()

        @pl.when(jnp.logical_and(j == n_ff - 1, t == n_tb - 1))
        def _():
            for block in range(n_tb):
                dn2_out(block).wait()

    act = pl.BlockSpec((tb, FF_CHUNK), lambda j, t: (t, j))
    w_row = pl.BlockSpec((FF_CHUNK, D_MODEL), lambda j, t: (j, 0))
    anyspace = pl.BlockSpec(memory_space=pl.ANY)
    grad = pltpu.HBM((D_FF, D_MODEL), BF16)
    whole = pltpu.VMEM((s, D_MODEL), BF16)
    piece = pltpu.VMEM((s, FF_CHUNK), BF16)
    return pl.pallas_call(
        body, name="ffn_bwd", grid=(n_ff, n_tb),
        in_specs=[anyspace, anyspace, act, act, w_row, w_row, w_row],
        out_specs=[w_row, w_row, w_row, anyspace],
        out_shape=[grad, grad, grad, pltpu.HBM((s, D_MODEL), F32)],
        scratch_shapes=[whole, whole, pltpu.VMEM((s, D_MODEL), F32), piece, piece, piece,
                        pltpu.SemaphoreType.DMA((2 + n_tb,))],
        compiler_params=_params(60, 2),
    )(*_in_hbm(n2, dh2b, g, u, w_gate_t, w_up_t, w_down))


def _bwd_out(dn2, dh2, h1, y_pool, y_lru, gn_pool_g, gn_lru_g, w_out, ln2_g, after):
    s = h1.shape[0]
    tb = _token_block(s)

    def body(dn2_ref, dh2_ref, h1_ref, yp_ref, yl_ref, gp_ref, gl_ref, w_ref, g2_ref, _after,
             dh1_ref, dyp_ref, dyl_ref, dwb_ref, dg2_ref, dgp_ref, dgl_ref, dw_ref):
        @pl.when(pl.program_id(0) == 0)
        def _():
            dw_ref[...] = jnp.zeros_like(dw_ref)
            dg2_ref[...] = jnp.zeros_like(dg2_ref)
            dgp_ref[...] = jnp.zeros_like(dgp_ref)
            dgl_ref[...] = jnp.zeros_like(dgl_ref)

        g2 = g2_ref[...]
        _, xhat2, rstd2 = _rms(h1_ref[...], g2)
        dres, dg2 = _rms_bwd(dn2_ref[...], xhat2, rstd2, g2)
        dg2_ref[...] += dg2
        dh1 = dh2_ref[...] + dres
        dh1_ref[...] = dh1
        dh1b = dh1.astype(BF16)
        gp, gl = gp_ref[...], gl_ref[...]
        mp, xhat_p, rstd_p = _rms(yp_ref[...], gp)
        ml, xhat_l, rstd_l = _rms(yl_ref[...], gl)
        dw_ref[:POOL_WIDTH, :] += _mm_tn(mp.astype(BF16), dh1b)
        dw_ref[POOL_WIDTH:, :] += _mm_tn(ml.astype(BF16), dh1b)
        dyp, dgp = _rms_bwd(_mm_nt(dh1b, w_ref[:POOL_WIDTH, :]), xhat_p, rstd_p, gp)
        dyl, dgl = _rms_bwd(_mm_nt(dh1b, w_ref[POOL_WIDTH:, :]), xhat_l, rstd_l, gl)
        dyp_ref[...] = dyp
        dyl_ref[...] = dyl
        dgp_ref[...] += dgp
        dgl_ref[...] += dgl

        @pl.when(pl.program_id(0) == s // tb - 1)
        def _():
            dwb_ref[...] = dw_ref[...].astype(BF16)

    row = pl.BlockSpec((tb, D_MODEL), lambda i: (i, 0))
    half = pl.BlockSpec((tb, 512), lambda i: (i, 0))
    vec = pl.BlockSpec((1, D_MODEL), lambda i: (0, 0))
    hvec = pl.BlockSpec((1, 512), lambda i: (0, 0))
    mat = pl.BlockSpec((D_MODEL, D_MODEL), lambda i: (0, 0))
    return pl.pallas_call(
        body, name="bwd_out", grid=(s // tb,),
        in_specs=[row, row, row, half, half, hvec, hvec, mat, vec, pl.BlockSpec(memory_space=pl.ANY)],
        out_specs=[row, half, half, mat, vec, hvec, hvec],
        out_shape=[pltpu.HBM((s, D_MODEL), F32), pltpu.HBM((s, 512), F32),
                   pltpu.HBM((s, 512), F32), pltpu.HBM((D_MODEL, D_MODEL), BF16),
                   jax.ShapeDtypeStruct((1, D_MODEL), F32), jax.ShapeDtypeStruct((1, 512), F32),
                   jax.ShapeDtypeStruct((1, 512), F32)],
        scratch_shapes=[pltpu.VMEM((D_MODEL, D_MODEL), F32)],
        compiler_params=_params(48, 1),
    )(*_in_hbm(dn2, dh2, h1, y_pool, y_lru, gn_pool_g, gn_lru_g, w_out, ln2_g), after)


def _mixer_bwd(u_pool, u_lru, u_gate, h, dy_pool, dy_lru,
               pool_w, pool_scale, conv_w, conv_b, wa_bd, b_a, wi_bd, b_i, lam, after):
    s = u_pool.shape[0]
    tc = _time_chunk(s)
    n_chunks = s // tc

    def body(up_ref, ul_ref, ug_ref, h_ref, dyp_ref, dyl_ref,
             pw_ref, ps_ref, cw_ref, cb_ref, wa_ref, ba_ref, wi_ref, bi_ref, lam_ref, _after,
             dup_ref, dul_ref, dug_ref, dpw_ref, dps_ref, dcw_ref, dcb_ref, dwa_ref, dba_ref, dwi_ref, dbi_ref, dlam_ref):
        deeper, window = _slab_scalars()
        pw = pw_ref[0].astype(BF16)
        wa = wa_ref[0].astype(BF16)
        wi = wi_ref[0].astype(BF16)
        ps, cw, cb, ba, bi = ps_ref[...], cw_ref[...], cb_ref[...], ba_ref[...], bi_ref[...]
        lam_v = lam_ref[...]
        sp = _softplus_neg(lam_v)
        row = lax.broadcasted_iota(jnp.int32, (tc, SLAB), 0)
        for ref in (dpw_ref, dps_ref, dcw_ref, dcb_ref, dwa_ref, dba_ref, dwi_ref, dbi_ref, dlam_ref):
            ref[...] = jnp.zeros_like(ref)

        def chunk(t0, ext_p, ext_l, ext_h, carry, at_start=False):
            l_next, a_next, dxc_next, ddn_next = carry
            rows = pl.ds(t0, tc)
            taps = _conv_taps(ext_l)
            xc = _conv(taps, cw, cb)
            xb, r, i, a, mult = _lru_gates(xc, wa, ba, wi, bi, sp)
            hv = ext_h[HALO:]
            h_before = _down(ext_h, 1)[HALO:]
            ug = ug_ref[rows, :]
            dyl = dyl_ref[rows, :]
            gel, th = _gelu(ug)
            dug_ref[rows, :] = dyl * hv * _gelu_grad(ug, th)
            a_after = jnp.where(row == tc - 1, a_next, _up(a, 1))
            l = _scan_anticausal(a_after, dyl * gel, l_next, row, tc)
            dmult = l * (i * xc)
            di = l * mult * xc
            dxc = l * mult * i
            dla = (l * h_before) * a - jnp.where(mult > 0.0, dmult * (a * a) / mult, 0.0)
            dlam_ref[...] += jnp.sum(dla * r, axis=0, keepdims=True)
            dpa = (dla * ((-LRU_C) * sp)) * (r * (1.0 - r))
            dpi = di * (i * (1.0 - i))
            dpab = dpa.astype(BF16)
            dpib = dpi.astype(BF16)
            dwa_ref[0] += _mm_tn(xb, dpab)
            dwi_ref[0] += _mm_tn(xb, dpib)
            dba_ref[...] += jnp.sum(dpa, axis=0, keepdims=True)
            dbi_ref[...] += jnp.sum(dpi, axis=0, keepdims=True)
            dxc = dxc + _mm_nt(dpab, wa) + _mm_nt(dpib, wi)
            ext_d = jnp.concatenate([dxc, dxc_next], axis=0)
            dul_ref[rows, :] = (cw[3:4] * dxc + cw[2:3] * _up(ext_d, 1)[:tc]
                                + cw[1:2] * _up(ext_d, 2)[:tc] + cw[0:1] * _up(ext_d, 3)[:tc])
            for k in range(CONV_WIDTH):
                dcw_ref[k:k + 1, :] += jnp.sum(dxc * taps[CONV_WIDTH - 1 - k], axis=0, keepdims=True)
            dcb_ref[...] += jnp.sum(dxc, axis=0, keepdims=True)
            db = (_window_mean(_causal_window(ext_p, deeper), t0, row, window, at_start) - ext_p[HALO:]).astype(BF16)
            dyp = dyp_ref[rows, :]
            dps_ref[...] += jnp.sum(dyp * _mm(db, pw), axis=0, keepdims=True)
            dys = (dyp * ps).astype(BF16)
            dpw_ref[0] += _mm_tn(db, dys)
            dd = _mm_nt(dys, pw)
            ddn = _window_mean(dd, t0, row, window, at_start)
            ext_q = jnp.concatenate([ddn, ddn_next], axis=0)
            dup_ref[rows, :] = _anticausal_window(ext_q, deeper, tc) - dd
            return l[0:1, :], a[0:1, :], dxc[0:8, :], ddn[0:HALO, :]

        def step(k, carry):
            c = n_chunks - 1 - k
            t0 = pl.multiple_of(c * tc, tc)
            ext = pl.ds(pl.multiple_of(c * tc - HALO, HALO), tc + HALO)
            return chunk(t0, up_ref[ext, :], ul_ref[ext, :], h_ref[ext, :], carry)

        carry = (jnp.zeros((1, SLAB), F32), jnp.zeros((1, SLAB), F32),
                 jnp.zeros((8, SLAB), F32), jnp.zeros((HALO, SLAB), F32))
        carry = lax.fori_loop(0, n_chunks - 1, step, carry)
        pad = jnp.zeros((HALO, SLAB), F32)
        first = pl.ds(0, tc)
        chunk(0, jnp.concatenate([pad, up_ref[first, :]], axis=0), jnp.concatenate([pad, ul_ref[first, :]], axis=0),
              jnp.concatenate([pad, h_ref[first, :]], axis=0), carry, at_start=True)
        dlam_ref[...] = dlam_ref[...] * (LRU_C * jax.nn.sigmoid(-lam_v))

    seq, mat, vec, taps = _slab_specs(s)
    full = pltpu.HBM((s, 512), F32)
    mats = jax.ShapeDtypeStruct((N_SLAB, SLAB, SLAB), F32)
    vecs = jax.ShapeDtypeStruct((1, 512), F32)
    return pl.pallas_call(
        body, name="mixer_bwd", grid=(N_SLAB,),
        in_specs=[seq] * 6 + [mat, vec, taps, vec, mat, vec, mat, vec, vec, pl.BlockSpec(memory_space=pl.ANY)],
        out_specs=[seq, seq, seq, mat, vec, taps, vec, mat, vec, mat, vec, vec],
        out_shape=[full, full, full, mats, vecs, jax.ShapeDtypeStruct((CONV_WIDTH, 512), F32), vecs,
                   mats, vecs, mats, vecs, vecs],
        compiler_params=_params(56, 1),
    )(*_in_hbm(u_pool, u_lru, u_gate, h, dy_pool, dy_lru, pool_w, pool_scale, conv_w, conv_b, wa_bd, b_a, wi_bd,
               b_i, lam), after)


def _bwd_in(x, dh1, du_pool, du_lru, du_gate, ln1_g, w_in_t, after):
    s = x.shape[0]
    tb = _token_block(s)

    def body(x_ref, dh1_ref, dup_ref, dul_ref, dug_ref, g_ref, w_ref, _after, dx_ref, dwb_ref, dg_ref, dw_ref):
        @pl.when(pl.program_id(0) == 0)
        def _():
            dw_ref[...] = jnp.zeros_like(dw_ref)
            dg_ref[...] = jnp.zeros_like(dg_ref)

        g1 = g_ref[...]
        n, xhat, rstd = _rms(x_ref[...], g1)
        nb = n.astype(BF16)
        dn = jnp.zeros((tb, D_MODEL), F32)
        for k, ref in enumerate((dup_ref, dul_ref, dug_ref)):
            rows = slice(k * 512, (k + 1) * 512)
            db = ref[...].astype(BF16)
            dw_ref[rows, :] += _mm_tn(db, nb)
            dn = dn + _mm(db, w_ref[rows, :])
        dx, dg1 = _rms_bwd(dn, xhat, rstd, g1)
        dx_ref[...] = dh1_ref[...] + dx
        dg_ref[...] += dg1

        @pl.when(pl.program_id(0) == s // tb - 1)
        def _():
            dwb_ref[...] = dw_ref[...].astype(BF16)

    row = pl.BlockSpec((tb, D_MODEL), lambda i: (i, 0))
    half = pl.BlockSpec((tb, 512), lambda i: (i, 0))
    vec = pl.BlockSpec((1, D_MODEL), lambda i: (0, 0))
    mat = pl.BlockSpec((IN_WIDTH, D_MODEL), lambda i: (0, 0))
    return pl.pallas_call(
        body, name="bwd_in", grid=(s // tb,),
        in_specs=[row, row, half, half, half, vec, mat, pl.BlockSpec(memory_space=pl.ANY)],
        out_specs=[row, mat, vec],
        out_shape=[pltpu.HBM((s, D_MODEL), F32), pltpu.HBM((IN_WIDTH, D_MODEL), BF16),
                   jax.ShapeDtypeStruct((1, D_MODEL), F32)],
        scratch_shapes=[pltpu.VMEM((IN_WIDTH, D_MODEL), F32)],
        compiler_params=_params(48, 1),
    )(*_in_hbm(x, dh1, du_pool, du_lru, du_gate, ln1_g, w_in_t), after)


def _mesh_position():
    x, y, c = lax.axis_index("x"), lax.axis_index("y"), lax.axis_index("c")
    return x, y, c, 4 * x + 2 * y + c


def _peer(x, y, c, p):
    px = 1 - x if p & 4 else x
    py = 1 - y if p & 2 else y
    pc = 1 - c if p & 1 else c
    return (px, py, pc), 4 * px + 2 * py + pc


HBM_SPEC = pl.BlockSpec(memory_space=pltpu.HBM)
SEM_SPEC = pl.BlockSpec(memory_space=pltpu.SEMAPHORE)
DATAFLOW = pltpu.SideEffectType.DATAFLOW_SIDE_EFFECTING


class Copy(NamedTuple):
    src: int
    src_at: Any
    dst: int
    dst_at: Any
    peer: int
    group: int
    slot: int


SIBLING = (1,)
SAME_CORE = (2, 4, 6)
EVERYONE = tuple(range(1, N_DEV))


def _same(index):
    return index


def _chip(index):
    return jnp.right_shift(index, 1)


def _fan_out(srcs, lands, peers, group):
    return [Copy(s, None, d, _same, p, group, N_DEV * i + p) for i, (s, d) in enumerate(zip(srcs, lands)) for p in peers]


def _scatter(stacks, lands, peers):
    return [Copy(s, lambda me, p=p: jnp.bitwise_xor(me, p), d, _same, p, 0, 0)
            for s, d in zip(stacks, lands) for p in peers]


def _numbered(copies, group=0):
    return [cp._replace(group=group, slot=i) for i, cp in enumerate(copies)]


def _relay(lands, peers):
    return [Copy(b, lambda s, q=q: jnp.bitwise_xor(s, q), b, lambda s, q=q: jnp.bitwise_xor(s, q), 1, 0, N_DEV * i + q)
            for i, b in enumerate(lands) for q in peers]


def _to_sibling(stacks, lands):
    return [Copy(s, lambda me, k=k: 2 * k + 1 - jnp.bitwise_and(me, 1), d, lambda me, k=k: k, 1, 0, 4 * i + k)
            for i, (s, d) in enumerate(zip(stacks, lands)) for k in range(N_DEV // 2)]


def _to_chips(sums, lands):
    return [Copy(s, lambda me, p=p: jnp.bitwise_xor(_chip(me), p // 2), d, _chip, p, 0, 4 * i + p // 2)
            for i, (s, d) in enumerate(zip(sums, lands)) for p in SAME_CORE]


def _comm_call(name, bufs, wait=None, start=None, after=()):
    nb = len(bufs)
    slots = list(start[1]) if start else []
    n_out_sem = 2 * len(slots)
    after = [a for a in (after if isinstance(after, (list, tuple)) else [after]) if a is not None]

    def body(*refs):
        b = refs[:nb]
        at = nb
        if wait:
            w_send, w_recv = refs[at], refs[at + 1]
            at += 2
        at += len(after)
        out_sems = refs[at:at + n_out_sem]
        token = refs[at + n_out_sem + nb]
        x, y, c, me = _mesh_position()

        def part(i, row_of, sender):
            return b[i] if row_of is None else b[i].at[row_of(sender)]

        if wait:
            for cp in wait[1]:
                peer, peer_index = _peer(x, y, c, cp.peer)
                arrival = pltpu.make_async_remote_copy(part(cp.src, cp.src_at, me), part(cp.dst, cp.dst_at, peer_index),
                                                       w_send.at[cp.slot], w_recv.at[cp.slot],
                                                       device_id=peer, device_id_type=MESH)
                arrival.wait_send()
                arrival.wait_recv()
        if start:
            for cp in start[0]:
                peer, _ = _peer(x, y, c, cp.peer)
                pltpu.make_async_remote_copy(part(cp.src, cp.src_at, me), part(cp.dst, cp.dst_at, me),
                                             out_sems[2 * cp.group].at[cp.slot], out_sems[2 * cp.group + 1].at[cp.slot],
                                             device_id=peer, device_id_type=MESH).start()
        token[...] = jnp.zeros_like(token)

    sem_shapes = []
    for n_slots in slots:
        sem_shapes += [pltpu.SemaphoreType.DMA((n_slots,))] * 2
    operands = [pltpu.with_memory_space_constraint(a, pltpu.HBM) for a in bufs]
    in_specs = [HBM_SPEC] * nb
    if wait:
        operands += list(wait[0])
        in_specs += [SEM_SPEC, SEM_SPEC]
    operands += after
    in_specs += [pl.BlockSpec(memory_space=pl.ANY)] * len(after)
    outs = pl.pallas_call(
        body, name=name, in_specs=in_specs,
        out_specs=[SEM_SPEC] * n_out_sem + [HBM_SPEC] * nb + [pl.BlockSpec(memory_space=pltpu.VMEM)],
        out_shape=sem_shapes + [pltpu.HBM(a.shape, a.dtype) for a in bufs] + [jax.ShapeDtypeStruct((8, SLAB), F32)],
        input_output_aliases={i: n_out_sem + i for i in range(nb)},
        compiler_params=pltpu.CompilerParams(has_side_effects=DATAFLOW),
    )(*operands)
    sems = [(outs[2 * k], outs[2 * k + 1]) for k in range(len(slots))]
    return sems, list(outs[n_out_sem:n_out_sem + nb]), outs[-1]


def _pair_sum(stacks, lands, place):
    n = len(stacks)

    def body(place_ref, *refs):
        k = pl.program_id(0)
        for m in range(n):
            mine, theirs, out, land = refs[m], refs[n + m], refs[2 * n + m], refs[3 * n + m]
            total = (mine[0, 0].astype(F32) + theirs[0].astype(F32)).astype(out.dtype)
            out[0] = total

            @pl.when(k == place_ref[1])
            def _():
                land[0] = total

    in_specs = [pl.BlockSpec((1, 1) + a.shape[1:], lambda k, place_ref: (k, place_ref[0], 0, 0)) for a in stacks]
    in_specs += [pl.BlockSpec((1,) + a.shape[1:], lambda k, place_ref: (k, 0, 0)) for a in lands]
    out_specs = [pl.BlockSpec((1,) + a.shape[1:], lambda k, place_ref: (k, 0, 0)) for a in lands]
    out_specs += [pl.BlockSpec((1,) + a.shape[1:], lambda k, place_ref: (place_ref[1], 0, 0)) for a in lands]
    outs = pl.pallas_call(
        body, name="pair_sum_" + "_".join(str(a.shape[1]) for a in stacks),
        grid_spec=pltpu.PrefetchScalarGridSpec(num_scalar_prefetch=1, grid=(N_DEV // 2,), in_specs=in_specs,
                                               out_specs=out_specs),
        out_shape=[pltpu.HBM(a.shape, a.dtype) for a in lands] * 2,
        compiler_params=_params(40, 1),
    )(place, *_in_hbm(*[a.reshape((N_DEV // 2, 2) + a.shape[1:]) for a in stacks], *lands))
    return list(outs[:n]), list(outs[n:])


def _reduce_adam(parts, w, m, v, name):
    rows, cols = w.shape
    n_parts = parts.shape[0]
    rb = rows
    for cand in (256, 176, 128):
        if rows % cand == 0 and rows > cand:
            rb = cand
            break

    def body(p_ref, w_ref, m_ref, v_ref, g_out, d_out, m_out, v_out):
        g = p_ref[0].astype(F32)
        for j in range(1, n_parts):
            g = g + p_ref[j].astype(F32)
        g_out[...] = g
        d_out[...], m_out[...], v_out[...] = _adam(g, w_ref[...], m_ref[...], v_ref[...])

    blk = pl.BlockSpec((rb, cols), lambda i: (i, 0))
    out = jax.ShapeDtypeStruct((rows, cols), F32)
    return pl.pallas_call(
        body, name=name, grid=(rows // rb,),
        in_specs=[pl.BlockSpec((n_parts, rb, cols), lambda i: (0, i, 0)), blk, blk, blk],
        out_specs=[blk] * 4, out_shape=[out] * 4,
        compiler_params=_params(40, 1),
    )(*_in_hbm(parts, w, m, v))


def _cols_from_stack(stack):
    n, r, c = stack.shape
    return jnp.transpose(stack, (1, 0, 2)).reshape(r, n * c)


def _block_diag(w):
    z = jnp.zeros((N_SLAB, 64, 64), w.dtype)
    pairs = w.reshape(N_SLAB, 2, 64, 64)
    top = jnp.concatenate([pairs[:, 0], z], axis=2)
    bottom = jnp.concatenate([z, pairs[:, 1]], axis=2)
    return jnp.concatenate([top, bottom], axis=1)


def _adam(g, w, m, v):
    m_new = ADAM_B1 * m + (1.0 - ADAM_B1) * g
    v_new = ADAM_B2 * v + (1.0 - ADAM_B2) * (g * g)
    m_hat = m_new / (1.0 - ADAM_B1 ** ADAM_STEP)
    v_hat = v_new / (1.0 - ADAM_B2 ** ADAM_STEP)
    return (-ADAM_LR) * (m_hat / (jnp.sqrt(v_hat) + ADAM_EPS) + ADAM_WD * w), m_new, v_new


WIDE = ("ln2_g", "lnf_g")
HALF = ("pool_scale", "conv_b", "b_a", "b_i", "lam", "gn_pool_g", "gn_lru_g")
VECTORS = [(k, D_MODEL) for k in WIDE] + [(k, 512) for k in HALF]
VECTOR_ROWS = sum(width // SLAB for _, width in VECTORS)
LOSS_ROW = -(-VECTOR_ROWS // 8) * 8
CONV_AT = LOSS_ROW + 8
CONV_LANES = LRU_WIDTH // SLAB
PACK_F_ROWS = CONV_AT + CONV_WIDTH * CONV_LANES
MATRIX_ROWS = N_SLAB * SLAB
HEAD = SLAB // 2
GATE_ROWS = N_SLAB * HEAD
PACK_B_ROWS = MATRIX_ROWS + 2 * GATE_ROWS


def _pack_small(vectors, pool_g, wa_g, wi_g, conv, sq):
    n_vec = len(vectors)

    def body(*refs):
        vec = refs[:n_vec]
        pw_ref, wa_ref, wi_ref, cw_ref, sq_ref, out, out_b = refs[n_vec:]
        out[...] = jnp.zeros_like(out)
        row = 0
        for ref, (_, width) in zip(vec, VECTORS):
            for k in range(width // SLAB):
                out[row:row + 1, :] = ref[:, k * SLAB:(k + 1) * SLAB]
                row += 1
        for tap in range(CONV_WIDTH):
            for k in range(CONV_LANES):
                at = CONV_AT + tap * CONV_LANES + k
                out[at:at + 1, :] = cw_ref[tap:tap + 1, k * SLAB:(k + 1) * SLAB]
        total = sq_ref[:, 0:SLAB]
        for k in range(1, D_MODEL // SLAB):
            total = total + sq_ref[:, k * SLAB:(k + 1) * SLAB]
        out[LOSS_ROW:LOSS_ROW + 1, :] = total
        left = lax.broadcasted_iota(jnp.int32, (HEAD, SLAB), 1) < HEAD
        for s in range(N_SLAB):
            out_b[s * SLAB:(s + 1) * SLAB, :] = pw_ref[s].astype(BF16)
            for i, ref in enumerate((wa_ref, wi_ref)):
                at = MATRIX_ROWS + i * GATE_ROWS + s * HEAD
                out_b[at:at + HEAD, :] = jnp.where(left, ref[s, 0:HEAD, :], ref[s, HEAD:SLAB, :]).astype(BF16)

    return pl.pallas_call(
        body, name="pack_small",
        out_shape=[jax.ShapeDtypeStruct((PACK_F_ROWS, SLAB), F32), jax.ShapeDtypeStruct((PACK_B_ROWS, SLAB), BF16)],
    )(*vectors, pool_g, wa_g, wi_g, conv, sq)


def _small_reduce_adam(parts, parts_b, vec_w, vec_m, vec_v, pool_wmv):
    n_vec = len(VECTORS)
    n_parts = parts.shape[0]

    def body(*refs):
        p_ref, pb_ref = refs[0], refs[1]
        w_refs, m_refs, v_refs = (refs[2 + k * n_vec:2 + (k + 1) * n_vec] for k in range(3))
        pw_w, pw_m, pw_v = refs[2 + 3 * n_vec:5 + 3 * n_vec]
        outs = refs[5 + 3 * n_vec:-1]
        total = refs[-1]
        total[...] = p_ref[0]
        for j in range(1, n_parts):
            total[...] += p_ref[j]
        row = 0
        for i, (_, width) in enumerate(VECTORS):
            n_rows = width // SLAB
            g = jnp.concatenate([total[row + k:row + k + 1, :] for k in range(n_rows)], axis=1)
            row += n_rows
            d, m_new, v_new = _adam(g, w_refs[i][...], m_refs[i][...], v_refs[i][...])
            for ref, val in zip(outs[4 * i:4 * i + 4], (g, d, m_new, v_new)):
                ref[...] = val
        tail = outs[4 * n_vec:]

        def summed(first, count):
            g = pb_ref[0, first:first + count, :].astype(F32)
            for j in range(1, n_parts):
                g = g + pb_ref[j, first:first + count, :].astype(F32)
            return g

        g = summed(0, MATRIX_ROWS)
        d, m_new, v_new = _adam(g, pw_w[...], pw_m[...], pw_v[...])
        for ref, val in zip(tail[0:4], (g, d, m_new, v_new)):
            ref[...] = val
        tail[4][...] = summed(MATRIX_ROWS, GATE_ROWS)
        tail[5][...] = summed(MATRIX_ROWS + GATE_ROWS, GATE_ROWS)
        for tap in range(CONV_WIDTH):
            at = CONV_AT + tap * CONV_LANES
            tail[6][tap:tap + 1, :] = jnp.concatenate([total[at + k:at + k + 1, :] for k in range(CONV_LANES)], axis=1)
        tail[7][...] = (0.5 / D_MODEL) * jnp.sum(total[LOSS_ROW:LOSS_ROW + 1, :], axis=1, keepdims=True)

    out_shape = []
    for _, width in VECTORS:
        out_shape += [jax.ShapeDtypeStruct((1, width), F32)] * 4
    out_shape += [jax.ShapeDtypeStruct((MATRIX_ROWS, SLAB), F32)] * 4 + [jax.ShapeDtypeStruct((GATE_ROWS, SLAB), F32)] * 2
    out_shape += [jax.ShapeDtypeStruct((CONV_WIDTH, LRU_WIDTH), F32), jax.ShapeDtypeStruct((1, 1), F32)]
    outs = pl.pallas_call(
        body, name="adam_small", out_shape=out_shape,
        scratch_shapes=[pltpu.VMEM((PACK_F_ROWS, SLAB), F32)],
        compiler_params=_params(40),
    )(parts, parts_b, *vec_w, *vec_m, *vec_v, *pool_wmv)
    vec_out = [tuple(outs[4 * i:4 * i + 4]) for i in range(n_vec)]
    tail = outs[4 * n_vec:]
    return vec_out, tuple(tail[0:4]), tail[4], tail[5], tail[6], tail[7]


def _plain_adam(grads, ws, ms, vs):
    n = len(grads)

    def body(*refs):
        ins, outs = refs[:4 * n], refs[4 * n:]
        for i in range(n):
            d, m_new, v_new = _adam(ins[i][...], ins[n + i][...], ins[2 * n + i][...], ins[3 * n + i][...])
            for ref, val in zip(outs[3 * i:3 * i + 3], (d, m_new, v_new)):
                ref[...] = val

    out_shape = []
    for g in grads:
        out_shape += [jax.ShapeDtypeStruct(g.shape, F32)] * 3
    outs = pl.pallas_call(body, name="adam_plain", out_shape=out_shape)(*grads, *ws, *ms, *vs)
    return [tuple(outs[3 * i:3 * i + 3]) for i in range(n)]


def kernel(x, ln1_g, w_in, pool_w, pool_scale, conv_w, conv_b, w_a, b_a, w_i, b_i, lam, gn_pool_g, gn_lru_g, w_out, ln2_g, w_ffn_gate, w_ffn_up, w_ffn_down, lnf_g, loss_target, m_ln1_g, m_w_in, m_pool_w, m_pool_scale, m_conv_w, m_conv_b, m_w_a, m_b_a, m_w_i, m_b_i, m_lam, m_gn_pool_g, m_gn_lru_g, m_w_out, m_ln2_g, m_w_ffn_gate, m_w_ffn_up, m_w_ffn_down, m_lnf_g, v_ln1_g, v_w_in, v_pool_w, v_pool_scale, v_conv_w, v_conv_b, v_w_a, v_b_a, v_w_i, v_b_i, v_lam, v_gn_pool_g, v_gn_lru_g, v_w_out, v_ln2_g, v_w_ffn_gate, v_w_ffn_up, v_w_ffn_down, v_lnf_g):
    weights = dict(ln1_g=ln1_g, w_in=w_in, pool_w=pool_w, pool_scale=pool_scale, conv_w=conv_w, conv_b=conv_b,
                   w_a=w_a, b_a=b_a, w_i=w_i, b_i=b_i, lam=lam, gn_pool_g=gn_pool_g, gn_lru_g=gn_lru_g,
                   w_out=w_out, ln2_g=ln2_g, w_ffn_gate=w_ffn_gate, w_ffn_up=w_ffn_up, w_ffn_down=w_ffn_down,
                   lnf_g=lnf_g)
    mom1 = dict(ln1_g=m_ln1_g, w_in=m_w_in, pool_w=m_pool_w, pool_scale=m_pool_scale, conv_w=m_conv_w,
                conv_b=m_conv_b, w_a=m_w_a, b_a=m_b_a, w_i=m_w_i, b_i=m_b_i, lam=m_lam, gn_pool_g=m_gn_pool_g,
                gn_lru_g=m_gn_lru_g, w_out=m_w_out, ln2_g=m_ln2_g, w_ffn_gate=m_w_ffn_gate,
                w_ffn_up=m_w_ffn_up, w_ffn_down=m_w_ffn_down, lnf_g=m_lnf_g)
    mom2 = dict(ln1_g=v_ln1_g, w_in=v_w_in, pool_w=v_pool_w, pool_scale=v_pool_scale, conv_w=v_conv_w,
                conv_b=v_conv_b, w_a=v_w_a, b_a=v_b_a, w_i=v_w_i, b_i=v_b_i, lam=v_lam, gn_pool_g=v_gn_pool_g,
                gn_lru_g=v_gn_lru_g, w_out=v_w_out, ln2_g=v_ln2_g, w_ffn_gate=v_w_ffn_gate,
                w_ffn_up=v_w_ffn_up, w_ffn_down=v_w_ffn_down, lnf_g=v_lnf_g)

    xs = x[0]
    target = loss_target[0]

    shard = dict(w_in=lambda a: a[0].T, w_ffn_gate=lambda a: a[0].T, w_ffn_up=lambda a: a[0].T,
                 w_out=lambda a: a[0], w_ffn_down=lambda a: a[0], conv_w=lambda a: a[0])
    unshard = dict(w_in=lambda a: a.T[None], w_ffn_gate=lambda a: a.T[None], w_ffn_up=lambda a: a.T[None],
                   w_out=lambda a: a[None], w_ffn_down=lambda a: a[None], conv_w=lambda a: a[None])

    gathered = ("w_in", "conv_w", "w_out", "w_ffn_gate", "w_ffn_up", "w_ffn_down")
    groups = ((0, 1), (2,), (3, 4, 5))
    sources = [shard[k](weights[k]) if k == "conv_w" else shard[k](weights[k]).astype(BF16) for k in gathered]
    my_index = 4 * lax.axis_index("x") + 2 * lax.axis_index("y") + lax.axis_index("c")
    lands = [lax.dynamic_update_index_in_dim(lax.empty((N_DEV,) + a.shape, a.dtype), a, my_index, 0) for a in sources]

    def first_hop(n, group=0):
        return _numbered(_fan_out(range(n), range(n, 2 * n), SAME_CORE + SIBLING, 0), group)

    start = []
    for g, members in enumerate(groups):
        start += _numbered(_fan_out(members, [6 + m for m in members], SAME_CORE + SIBLING, 0), g)
    sems, bufs, _ = _comm_call("gather_start", sources + lands,
                               start=(start, [sum(cp.group == g for cp in start) for g in range(len(groups))]))
    sources, lands = bufs[:6], bufs[6:]

    def gathered_group(tag, g, after):
        members = groups[g]
        n = len(members)
        relay = _numbered(_relay(range(n, 2 * n), SAME_CORE))
        relay_sems, bufs, _ = _comm_call("gather_relay_" + tag, [sources[m] for m in members] + [lands[m] for m in members],
                                         wait=(sems[g], first_hop(n)), start=(relay, (len(relay),)), after=after)
        _, bufs, _ = _comm_call("gather_wait_" + tag, bufs[n:], wait=(relay_sems[0], _numbered(_relay(range(n), SAME_CORE))))
        return bufs

    g_in, g_conv = gathered_group("in", 0, None)
    w_in_f = g_in.reshape(IN_WIDTH, D_MODEL)
    conv_w_f = _cols_from_stack(g_conv)

    wa_bd = _block_diag(w_a[0])
    wi_bd = _block_diag(w_i[0])
    lnf_row = lnf_g.reshape(1, D_MODEL)

    u_pool, u_lru, u_gate = _fwd_in(xs, ln1_g, w_in_f)
    y_pool, h, y_lru = _mixer_fwd(u_pool, u_lru, u_gate, pool_w[0], pool_scale, conv_w_f, conv_b,
                                  wa_bd, b_a, wi_bd, b_i, lam)
    (g_out,) = gathered_group("out", 1, y_pool)
    w_out_f = g_out.reshape(D_MODEL, D_MODEL)
    h1, n2 = _fwd_out(xs, y_pool, y_lru, gn_pool_g, gn_lru_g, w_out_f, ln2_g)
    g_gate, g_up, g_down = gathered_group("ffn", 2, n2)
    w_gate_f = g_gate.reshape(D_FF, D_MODEL)
    w_up_f = g_up.reshape(D_FF, D_MODEL)
    w_down_f = g_down.reshape(D_FF, D_MODEL)
    g_act, u_act, dh2, dh2b, d_lnf, sq = _ffn_fwd(h1, n2, target, lnf_row, w_gate_f, w_up_f, w_down_f)

    place = jnp.stack([lax.axis_index("c"), 2 * lax.axis_index("x") + lax.axis_index("y")]).astype(jnp.int32)

    d_gate, d_up, d_down, dn2 = _ffn_bwd(n2, dh2b, g_act, u_act, w_gate_f, w_up_f, w_down_f)
    ffn_stacks = [d.reshape(N_DEV, D_FF // N_DEV, D_MODEL) for d in (d_gate, d_up, d_down)]
    pair_lands = [lax.empty((N_DEV // 2,) + a.shape[1:], a.dtype) for a in ffn_stacks]
    pair_copies = _numbered(_to_sibling(range(3), range(3, 6)))
    sem, bufs, token = _comm_call("ffn_pair_start", ffn_stacks + pair_lands, start=(pair_copies, (len(pair_copies),)))
    dh1, dy_pool, dy_lru, d_out, d_ln2, d_gnp, d_gnl = _bwd_out(dn2, dh2, h1, y_pool, y_lru, gn_pool_g, gn_lru_g,
                                                                 w_out_f, ln2_g, token)
    _, bufs, _ = _comm_call("ffn_pair_wait", bufs, wait=(sem[0], pair_copies), after=dh1)
    ffn_sums, ffn_lands = _pair_sum(bufs[:3], bufs[3:], place)
    ffn_copies = _numbered(_to_chips(range(3), range(3, 6)))
    ffn_sem, ffn_bufs, token = _comm_call("ffn_chip_start", ffn_sums + ffn_lands, start=(ffn_copies, (len(ffn_copies),)))
    (du_pool, du_lru, du_gate, d_pw, d_ps, d_cw, d_cb, d_wa, d_ba, d_wi, d_bi, d_lam) = _mixer_bwd(
        u_pool, u_lru, u_gate, h, dy_pool, dy_lru, pool_w[0], pool_scale, conv_w_f, conv_b,
        wa_bd, b_a, wi_bd, b_i, lam, token)

    def direct(tag, stacks, wholes, after):
        sources = list(stacks) + list(wholes)
        n, n_st = len(sources), len(stacks)
        lands = [lax.dynamic_update_index_in_dim(
            lax.empty(a.shape if i < n_st else (N_DEV,) + a.shape, a.dtype),
            lax.dynamic_index_in_dim(a, my_index, 0, keepdims=False) if i < n_st else a, my_index, 0)
            for i, a in enumerate(sources)]
        copies = _numbered(_scatter(range(n_st), range(n, n + n_st), EVERYONE)
                           + _fan_out(range(n_st, n), range(n + n_st, 2 * n), EVERYONE, 0))
        sem, bufs, token = _comm_call(tag + "_start", sources + lands, start=(copies, (len(copies),)), after=after)
        return (tag, sem[0], bufs, copies), token

    def direct_finish(started, after):
        tag, sem, bufs, copies = started
        _, bufs, _ = _comm_call(tag + "_wait", bufs, wait=(sem, copies), after=after)
        return bufs[len(bufs) // 2:]

    vec_grads = dict(ln2_g=d_ln2, lnf_g=d_lnf, pool_scale=d_ps, conv_b=d_cb, b_a=d_ba, b_i=d_bi,
                     lam=d_lam, gn_pool_g=d_gnp, gn_lru_g=d_gnl)
    packed, packed_b = _pack_small([vec_grads[k] for k, _ in VECTORS], d_pw, d_wa, d_wi, d_cw, sq)
    small_started, token = direct("small", [d_out.reshape(N_DEV, D_MODEL // N_DEV, D_MODEL)], [packed, packed_b], None)
    grad_x, d_in, d_ln1 = _bwd_in(xs, dh1, du_pool, du_lru, du_gate, ln1_g, w_in_f, token)
    in_started, token = direct("in", [d_in.reshape(N_DEV, IN_WIDTH // N_DEV, D_MODEL)], [d_ln1], None)

    results = {}

    def reduce_adam(name, parts):
        outs = _reduce_adam(parts, shard[name](weights[name]), shard[name](mom1[name]), shard[name](mom2[name]),
                            "adam_" + name)
        results[name] = tuple(unshard[name](o) for o in outs)
        return outs[0]

    shard["ln1_g"] = unshard["ln1_g"] = lambda a: a
    _, ffn_bufs, _ = _comm_call("ffn_chip_wait", ffn_bufs, wait=(ffn_sem[0], ffn_copies), after=token)
    done = [reduce_adam(name, parts)
            for name, parts in zip(("w_ffn_gate", "w_ffn_up", "w_ffn_down"), ffn_bufs[3:])]
    r_out, r_small, r_small_b = direct_finish(small_started, done)
    done.append(reduce_adam("w_out", r_out))

    def as_row(a, width):
        return a.reshape(1, width)

    def as_matrix(a):
        return a.reshape(MATRIX_ROWS, SLAB)

    def as_heads(a):
        return a.reshape(2 * GATE_ROWS, HEAD)

    def heads_apart(g):
        return jnp.transpose(g.reshape(N_SLAB, HEAD, 2, HEAD), (0, 2, 1, 3)).reshape(2 * GATE_ROWS, HEAD)

    vec_out, pool_out, g_wa, g_wi, g_conv, loss_11 = _small_reduce_adam(
        r_small, r_small_b, [as_row(weights[k], w) for k, w in VECTORS], [as_row(mom1[k], w) for k, w in VECTORS],
        [as_row(mom2[k], w) for k, w in VECTORS], [as_matrix(t["pool_w"]) for t in (weights, mom1, mom2)])
    for (k, _), outs in zip(VECTORS, vec_out):
        results[k] = tuple(o.reshape(weights[k].shape) for o in outs)
    results["pool_w"] = tuple(o.reshape(pool_w.shape) for o in pool_out)
    my_columns = conv_w.shape[-1]
    plain_names = ("w_a", "w_i", "conv_w")
    plain_grads = [heads_apart(g) for g in (g_wa, g_wi)]
    plain_grads.append(lax.dynamic_slice_in_dim(g_conv, my_index * my_columns, my_columns, axis=1))
    views = (as_heads, as_heads, lambda a: a[0])
    plain_out = _plain_adam(plain_grads, *[[view(t[k]) for k, view in zip(plain_names, views)]
                                           for t in (weights, mom1, mom2)])
    for k, g, outs in zip(plain_names, plain_grads, plain_out):
        results[k] = tuple(o.reshape(weights[k].shape) for o in (g,) + outs)
    loss = loss_11[0, 0]
    r_in, r_ln1 = direct_finish(in_started, done + [plain_out[0][0], loss_11])
    reduce_adam("w_in", r_in)
    reduce_adam("ln1_g", r_ln1)

    order = ["ln1_g", "w_in", "pool_w", "pool_scale", "conv_w", "conv_b", "w_a", "b_a", "w_i", "b_i", "lam",
             "gn_pool_g", "gn_lru_g", "w_out", "ln2_g", "w_ffn_gate", "w_ffn_up", "w_ffn_down", "lnf_g"]
    return (loss, grad_x[None],
            *[results[k][0] for k in order], *[results[k][1] for k in order],
            *[results[k][2] for k in order], *[results[k][3] for k in order])
```

```python
from typing import Any, NamedTuple

import jax
import jax.numpy as jnp
from jax import lax
from jax.experimental import pallas as pl
from jax.experimental.pallas import tpu as pltpu

F32 = jnp.float32
BF16 = jnp.bfloat16

N_DEV = 8
D_MODEL = 1024
POOL_WIDTH = 512
LRU_WIDTH = 512
IN_WIDTH = 1536
D_FF = 2816
N_SLAB = 4
SLAB = 128
CONV_WIDTH = 4
LRU_C = 8.0
EPS = 1e-6
HALO = 16
FF_CHUNK = 256

ADAM_LR = 0.001
ADAM_B1 = 0.9
ADAM_B2 = 0.999
ADAM_EPS = 1e-08
ADAM_WD = 0.01
ADAM_STEP = 10

MIB = 1 << 20
MESH = pl.DeviceIdType.MESH


def _params(vmem_mib, n_axes=0):
    sem = ("arbitrary",) * n_axes if n_axes else None
    return pltpu.CompilerParams(dimension_semantics=sem, vmem_limit_bytes=vmem_mib * MIB)


def _in_hbm(*arrays):
    return [pltpu.with_memory_space_constraint(a, pltpu.HBM) for a in arrays]


def _mm(a, b):
    return jnp.dot(a, b, preferred_element_type=F32)


def _mm_nt(a, b):
    return lax.dot_general(a, b, (((1,), (1,)), ((), ())), preferred_element_type=F32)


def _mm_tn(a, b):
    return lax.dot_general(a, b, (((0,), (0,)), ((), ())), preferred_element_type=F32)


def _rms(x, g):
    rstd = lax.rsqrt(jnp.mean(x * x, axis=-1, keepdims=True) + EPS)
    xhat = x * rstd
    return xhat * g, xhat, rstd


def _rms_bwd(dy, xhat, rstd, g):
    gy = dy * g
    dx = rstd * (gy - xhat * jnp.mean(gy * xhat, axis=-1, keepdims=True))
    return dx, jnp.sum(dy * xhat, axis=0, keepdims=True)


def _gelu(z):
    t = jnp.tanh(0.7978845608028654 * (z + 0.044715 * z * z * z))
    return 0.5 * z * (1.0 + t), t


def _gelu_grad(z, t):
    return 0.5 * (1.0 + t) + 0.5 * z * (1.0 - t * t) * 0.7978845608028654 * (1.0 + 3.0 * 0.044715 * z * z)


def _softplus_neg(lam):
    x = -lam
    e = jnp.exp(-jnp.abs(x))
    u = 1.0 + e
    l1p = jnp.where(u == 1.0, e, jnp.log(u) * e / (u - 1.0))
    return jnp.maximum(x, 0.0) + l1p


def _expm1(x):
    p = x * (1.0 + x * (0.5 + x * (1.0 / 6.0 + x * (1.0 / 24.0 + x * (1.0 / 120.0)))))
    return jnp.where(jnp.abs(x) < 0.1, p, jnp.exp(x) - 1.0)


def _down(v, d):
    return pltpu.roll(v, d, 0)


def _up(v, d):
    return pltpu.roll(v, v.shape[0] - d, 0)


def _token_block(s, most=512):
    for rows in (most, 512, 256):
        if rows <= most and s % rows == 0 and s > rows:
            return rows
    return s


def _time_chunk(s):
    return 512 if s % 512 == 0 and s > 512 else 256 if s % 256 == 0 else s


def _fwd_in(x, ln1_g, w_in_t):
    s = x.shape[0]
    tb = _token_block(s, 1024)

    def body(x_ref, g_ref, w_ref, up_ref, ul_ref, ug_ref):
        n, _, _ = _rms(x_ref[...], g_ref[...])
        proj = _mm_nt(n.astype(BF16), w_ref[...])
        up_ref[...] = proj[:, :POOL_WIDTH]
        ul_ref[...] = proj[:, POOL_WIDTH:POOL_WIDTH + LRU_WIDTH]
        ug_ref[...] = proj[:, POOL_WIDTH + LRU_WIDTH:]

    out = pltpu.HBM((s, 512), F32)
    return pl.pallas_call(
        body, name="fwd_in", grid=(s // tb,),
        in_specs=[pl.BlockSpec((tb, D_MODEL), lambda i: (i, 0)),
                  pl.BlockSpec((1, D_MODEL), lambda i: (0, 0)),
                  pl.BlockSpec((IN_WIDTH, D_MODEL), lambda i: (0, 0))],
        out_specs=[pl.BlockSpec((tb, 512), lambda i: (i, 0))] * 3,
        out_shape=[out, out, out],
        compiler_params=_params(40, 1),
    )(*_in_hbm(x, ln1_g, w_in_t))


def _pool_denominator(t0, row, window):
    return jnp.minimum((t0 + row + 1).astype(F32), window)


def _causal_window(ext, deeper):
    s = ext + _down(ext, 1)
    s = s + deeper[0] * _down(s, 2)
    s = s + deeper[1] * _down(s, 4)
    s = s + deeper[2] * _down(s, 8)
    return s[HALO:]


def _anticausal_window(ext, deeper, rows):
    s = ext + _up(ext, 1)
    s = s + deeper[0] * _up(s, 2)
    s = s + deeper[1] * _up(s, 4)
    s = s + deeper[2] * _up(s, 8)
    return s[:rows]


def _conv_taps(ext):
    return [ext[HALO:], _down(ext, 1)[HALO:], _down(ext, 2)[HALO:], _down(ext, 3)[HALO:]]


def _conv(taps, cw, cb):
    return cw[3:4] * taps[0] + cw[2:3] * taps[1] + cw[1:2] * taps[2] + cw[0:1] * taps[3] + cb


def _lru_gates(xc, wa, ba, wi, bi, sp):
    xb = xc.astype(BF16)
    r = jax.nn.sigmoid(_mm(xb, wa) + ba)
    i = jax.nn.sigmoid(_mm(xb, wi) + bi)
    la = (-LRU_C) * r * sp
    a = jnp.exp(la)
    mult = jnp.sqrt(jnp.maximum(-_expm1(2.0 * la), 0.0))
    return xb, r, i, a, mult


SUBLANES = 8


def _scan_causal(a, b, h_prev, row, rows):
    d = 1
    while d < min(SUBLANES, rows):
        head = row < d
        a_sh = jnp.where(head, 1.0, _down(a, d))
        b_sh = jnp.where(head, 0.0, _down(b, d))
        b = a * b_sh + b
        a = a * a_sh
        d *= 2
    while d < rows:
        b = jnp.concatenate([b[:d], a[d:] * b[:-d] + b[d:]], axis=0)
        a = jnp.concatenate([a[:d], a[d:] * a[:-d]], axis=0)
        d *= 2
    return b + a * h_prev


def _scan_anticausal(a, b, l_next, row, rows):
    d = 1
    while d < min(SUBLANES, rows):
        tail = row >= rows - d
        a_sh = jnp.where(tail, 1.0, _up(a, d))
        b_sh = jnp.where(tail, 0.0, _up(b, d))
        b = a * b_sh + b
        a = a * a_sh
        d *= 2
    while d < rows:
        b = jnp.concatenate([a[:-d] * b[d:] + b[:-d], b[-d:]], axis=0)
        a = jnp.concatenate([a[:-d] * a[d:], a[-d:]], axis=0)
        d *= 2
    return b + a * l_next


def _slab_scalars():
    slab = pl.program_id(0)
    deeper = [jnp.where(slab > k, 1.0, 0.0).astype(F32) for k in range(N_SLAB - 1)]
    window = jnp.left_shift(jnp.int32(2), slab).astype(F32)
    inverse = jnp.where(slab == 0, 0.5, jnp.where(slab == 1, 0.25, jnp.where(slab == 2, 0.125, 0.0625))).astype(F32)
    return deeper, (window, inverse)


def _window_mean(total, t0, row, window, at_start):
    if at_start:
        return total / _pool_denominator(t0, row, window[0])
    return total * window[1]


def _slab_specs(s):
    seq = pl.BlockSpec((s, SLAB), lambda k: (0, k))
    mat = pl.BlockSpec((1, SLAB, SLAB), lambda k: (k, 0, 0))
    vec = pl.BlockSpec((1, SLAB), lambda k: (0, k))
    taps = pl.BlockSpec((CONV_WIDTH, SLAB), lambda k: (0, k))
    return seq, mat, vec, taps


def _mixer_fwd(u_pool, u_lru, u_gate, pool_w, pool_scale, conv_w, conv_b, wa_bd, b_a, wi_bd, b_i, lam):
    s = u_pool.shape[0]
    tc = _time_chunk(s)
    n_chunks = s // tc

    def body(up_ref, ul_ref, ug_ref, pw_ref, ps_ref, cw_ref, cb_ref, wa_ref, ba_ref, wi_ref, bi_ref, lam_ref,
             yp_ref, h_ref, yl_ref):
        deeper, window = _slab_scalars()
        pw = pw_ref[0].astype(BF16)
        wa = wa_ref[0].astype(BF16)
        wi = wi_ref[0].astype(BF16)
        ps, cw, cb, ba, bi = ps_ref[...], cw_ref[...], cb_ref[...], ba_ref[...], bi_ref[...]
        sp = _softplus_neg(lam_ref[...])
        row = lax.broadcasted_iota(jnp.int32, (tc, SLAB), 0)

        def chunk(t0, ext_p, ext_l, h_prev, at_start=False):
            rows = pl.ds(t0, tc)
            d = _window_mean(_causal_window(ext_p, deeper), t0, row, window, at_start) - ext_p[HALO:]
            yp_ref[rows, :] = _mm(d.astype(BF16), pw) * ps
            xc = _conv(_conv_taps(ext_l), cw, cb)
            _, _, i, a, mult = _lru_gates(xc, wa, ba, wi, bi, sp)
            h = _scan_causal(a, mult * (i * xc), h_prev, row, tc)
            h_ref[rows, :] = h
            yl_ref[rows, :] = h * _gelu(ug_ref[rows, :])[0]
            return h[tc - 1:tc, :]

        pad = jnp.zeros((HALO, SLAB), F32)
        h0 = chunk(0, jnp.concatenate([pad, up_ref[pl.ds(0, tc), :]], axis=0),
                   jnp.concatenate([pad, ul_ref[pl.ds(0, tc), :]], axis=0), jnp.zeros((1, SLAB), F32), at_start=True)

        def step(c, h_prev):
            t0 = pl.multiple_of(c * tc, tc)
            ext = pl.ds(pl.multiple_of(c * tc - HALO, HALO), tc + HALO)
            return chunk(t0, up_ref[ext, :], ul_ref[ext, :], h_prev)

        lax.fori_loop(1, n_chunks, step, h0)

    seq, mat, vec, taps = _slab_specs(s)
    out = pltpu.HBM((s, 512), F32)
    return pl.pallas_call(
        body, name="mixer_fwd", grid=(N_SLAB,),
        in_specs=[seq, seq, seq, mat, vec, taps, vec, mat, vec, mat, vec, vec],
        out_specs=[seq, seq, seq], out_shape=[out, out, out],
        compiler_params=_params(48, 1),
    )(*_in_hbm(u_pool, u_lru, u_gate, pool_w, pool_scale, conv_w, conv_b, wa_bd, b_a, wi_bd, b_i, lam))


def _fwd_out(x, y_pool, y_lru, gn_pool_g, gn_lru_g, w_out, ln2_g):
    s = x.shape[0]
    tb = _token_block(s, 1024)

    def body(x_ref, yp_ref, yl_ref, gp_ref, gl_ref, w_ref, g2_ref, h1_ref, n2_ref):
        mp, _, _ = _rms(yp_ref[...], gp_ref[...])
        ml, _, _ = _rms(yl_ref[...], gl_ref[...])
        h1 = x_ref[...] + _mm(mp.astype(BF16), w_ref[:POOL_WIDTH, :]) + _mm(ml.astype(BF16), w_ref[POOL_WIDTH:, :])
        h1_ref[...] = h1
        n2_ref[...] = _rms(h1, g2_ref[...])[0].astype(BF16)

    row = pl.BlockSpec((tb, D_MODEL), lambda i: (i, 0))
    half = pl.BlockSpec((tb, 512), lambda i: (i, 0))
    return pl.pallas_call(
        body, name="fwd_out", grid=(s // tb,),
        in_specs=[row, half, half, pl.BlockSpec((1, 512), lambda i: (0, 0)), pl.BlockSpec((1, 512), lambda i: (0, 0)),
                  pl.BlockSpec((D_MODEL, D_MODEL), lambda i: (0, 0)), pl.BlockSpec((1, D_MODEL), lambda i: (0, 0))],
        out_specs=[row, row],
        out_shape=[pltpu.HBM((s, D_MODEL), F32), pltpu.HBM((s, D_MODEL), BF16)],
        compiler_params=_params(40, 1),
    )(*_in_hbm(x, y_pool, y_lru, gn_pool_g, gn_lru_g, w_out, ln2_g))


def _ffn_fwd(h1, n2, target, lnf_g, w_gate, w_up, w_down):
    s = h1.shape[0]
    tb = 512
    sub = 256
    n_ff = D_FF // FF_CHUNK

    def body(h1_ref, n2_ref, t_ref, gf_ref, wg_hbm, wu_hbm, wd_hbm,
             g_ref, u_ref, dh_ref, dhb_ref, dgf_ref, sq_ref, wg, wu, wd, sem):
        @pl.when(pl.program_id(0) == 0)
        def _():
            loads = [pltpu.make_async_copy(src, dst, sem.at[k])
                     for k, (src, dst) in enumerate(((wg_hbm, wg), (wu_hbm, wu), (wd_hbm, wd)))]
            for cp in loads:
                cp.start()
            for cp in loads:
                cp.wait()
            dgf_ref[...] = jnp.zeros_like(dgf_ref)
            sq_ref[...] = jnp.zeros_like(sq_ref)

        n2v = n2_ref[...]
        acc = jnp.zeros((tb, D_MODEL), F32)
        for c in range(n_ff):
            cols = slice(c * FF_CHUNK, (c + 1) * FF_CHUNK)
            g = _mm_nt(n2v, wg[cols, :])
            u = _mm_nt(n2v, wu[cols, :])
            g_ref[:, cols] = g.astype(BF16)
            u_ref[:, cols] = u.astype(BF16)
            act = g * jax.nn.sigmoid(g) * u
            acc = acc + _mm(act.astype(BF16), wd[cols, :])
        gf = gf_ref[...]
        for r in range(tb // sub):
            rows = slice(r * sub, (r + 1) * sub)
            y, xhat, rstd = _rms(h1_ref[rows, :] + acc[rows, :], gf)
            err = y - t_ref[rows, :]
            sq_ref[...] += jnp.sum(err * err, axis=0, keepdims=True)
            dh2, dgf = _rms_bwd(err * (1.0 / D_MODEL), xhat, rstd, gf)
            dgf_ref[...] += dgf
            dh_ref[rows, :] = dh2
            dhb_ref[rows, :] = dh2.astype(BF16)

    row = pl.BlockSpec((tb, D_MODEL), lambda i: (i, 0))
    ff = pl.BlockSpec((tb, D_FF), lambda i: (i, 0))
    vec = pl.BlockSpec((1, D_MODEL), lambda i: (0, 0))
    anyspace = pl.BlockSpec(memory_space=pl.ANY)
    return pl.pallas_call(
        body, name="ffn_fwd", grid=(s // tb,),
        in_specs=[row, row, row, vec, anyspace, anyspace, anyspace],
        out_specs=[ff, ff, row, row, vec, vec],
        out_shape=[pltpu.HBM((s, D_FF), BF16), pltpu.HBM((s, D_FF), BF16),
                   pltpu.HBM((s, D_MODEL), F32), pltpu.HBM((s, D_MODEL), BF16),
                   jax.ShapeDtypeStruct((1, D_MODEL), F32), jax.ShapeDtypeStruct((1, D_MODEL), F32)],
        scratch_shapes=[pltpu.VMEM((D_FF, D_MODEL), BF16), pltpu.VMEM((D_FF, D_MODEL), BF16),
                        pltpu.VMEM((D_FF, D_MODEL), BF16), pltpu.SemaphoreType.DMA((3,))],
        compiler_params=_params(60, 1),
    )(*_in_hbm(h1, n2, target, lnf_g, w_gate, w_up, w_down))


def _ffn_bwd(n2, dh2b, g, u, w_gate_t, w_up_t, w_down):
    s = n2.shape[0]
    tb = min(1024, s)
    n_ff = D_FF // FF_CHUNK
    n_tb = s // tb

    def body(n2_ref, dh_ref, g_ref, u_ref, wg_ref, wu_ref, wd_ref, dwg_ref, dwu_ref, dwd_ref, dn2_ref,
             dn2_acc, acc_g, acc_u, acc_d):
        j = pl.program_id(0)
        t = pl.program_id(1)

        @pl.when(t == 0)
        def _():
            acc_g[...] = jnp.zeros_like(acc_g)
            acc_u[...] = jnp.zeros_like(acc_u)
            acc_d[...] = jnp.zeros_like(acc_d)

        n2v = n2_ref[...]
        dh = dh_ref[...]
        gv = g_ref[...].astype(F32)
        uv = u_ref[...].astype(F32)
        sg = jax.nn.sigmoid(gv)
        silu = gv * sg
        dact = _mm_nt(dh, wd_ref[...])
        dub = (dact * silu).astype(BF16)
        dgb = (dact * uv * (sg * (1.0 + gv * (1.0 - sg)))).astype(BF16)
        acc_d[...] += _mm_tn((silu * uv).astype(BF16), dh)
        acc_g[...] += _mm_tn(dgb, n2v)
        acc_u[...] += _mm_tn(dub, n2v)
        part = _mm(jnp.concatenate([dgb, dub], axis=1), jnp.concatenate([wg_ref[...], wu_ref[...]], axis=0))
        rows = pl.ds(pl.multiple_of(t * tb, tb), tb)

        @pl.when(t == n_tb - 1)
        def _():
            dwg_ref[...] = acc_g[...].astype(BF16)
            dwu_ref[...] = acc_u[...].astype(BF16)
            dwd_ref[...] = acc_d[...].astype(BF16)

        @pl.when(j == 0)
        def _():
            dn2_acc[rows, :] = part

        @pl.when(jnp.logical_and(j > 0, j < n_ff - 1))
        def _():
            dn2_acc[rows, :] += part

        @pl.when(j == n_ff - 1)
        def _():
            dn2_ref[...] = dn2_acc[rows, :] + part

    row = pl.BlockSpec((tb, D_MODEL), lambda j, t: (t, 0))
    act = pl.BlockSpec((tb, FF_CHUNK), lambda j, t: (t, j))
    w_row = pl.BlockSpec((FF_CHUNK, D_MODEL), lambda j, t: (j, 0))
    last = pl.BlockSpec((tb, D_MODEL), lambda j, t: (jnp.where(j == n_ff - 1, t, 0), 0))
    grad = pltpu.HBM((D_FF, D_MODEL), BF16)
    chunk_acc = pltpu.VMEM((FF_CHUNK, D_MODEL), F32)
    return pl.pallas_call(
        body, name="ffn_bwd", grid=(n_ff, n_tb),
        in_specs=[row, row, act, act, w_row, w_row, w_row],
        out_specs=[w_row, w_row, w_row, last],
        out_shape=[grad, grad, grad, pltpu.HBM((s, D_MODEL), F32)],
        scratch_shapes=[pltpu.VMEM((s, D_MODEL), F32), chunk_acc, chunk_acc, chunk_acc],
        compiler_params=_params(56, 2),
    )(*_in_hbm(n2, dh2b, g, u, w_gate_t, w_up_t, w_down))


def _bwd_out(dn2, dh2, h1, y_pool, y_lru, gn_pool_g, gn_lru_g, w_out, ln2_g, after):
    s = h1.shape[0]
    tb = _token_block(s)

    def body(dn2_ref, dh2_ref, h1_ref, yp_ref, yl_ref, gp_ref, gl_ref, w_ref, g2_ref, _after,
             dh1_ref, dyp_ref, dyl_ref, dwb_ref, dg2_ref, dgp_ref, dgl_ref, dw_ref):
        @pl.when(pl.program_id(0) == 0)
        def _():
            dw_ref[...] = jnp.zeros_like(dw_ref)
            dg2_ref[...] = jnp.zeros_like(dg2_ref)
            dgp_ref[...] = jnp.zeros_like(dgp_ref)
            dgl_ref[...] = jnp.zeros_like(dgl_ref)

        g2 = g2_ref[...]
        _, xhat2, rstd2 = _rms(h1_ref[...], g2)
        dres, dg2 = _rms_bwd(dn2_ref[...], xhat2, rstd2, g2)
        dg2_ref[...] += dg2
        dh1 = dh2_ref[...] + dres
        dh1_ref[...] = dh1
        dh1b = dh1.astype(BF16)
        gp, gl = gp_ref[...], gl_ref[...]
        mp, xhat_p, rstd_p = _rms(yp_ref[...], gp)
        ml, xhat_l, rstd_l = _rms(yl_ref[...], gl)
        dw_ref[:POOL_WIDTH, :] += _mm_tn(mp.astype(BF16), dh1b)
        dw_ref[POOL_WIDTH:, :] += _mm_tn(ml.astype(BF16), dh1b)
        dyp, dgp = _rms_bwd(_mm_nt(dh1b, w_ref[:POOL_WIDTH, :]), xhat_p, rstd_p, gp)
        dyl, dgl = _rms_bwd(_mm_nt(dh1b, w_ref[POOL_WIDTH:, :]), xhat_l, rstd_l, gl)
        dyp_ref[...] = dyp
        dyl_ref[...] = dyl
        dgp_ref[...] += dgp
        dgl_ref[...] += dgl

        @pl.when(pl.program_id(0) == s // tb - 1)
        def _():
            dwb_ref[...] = dw_ref[...].astype(BF16)

    row = pl.BlockSpec((tb, D_MODEL), lambda i: (i, 0))
    half = pl.BlockSpec((tb, 512), lambda i: (i, 0))
    vec = pl.BlockSpec((1, D_MODEL), lambda i: (0, 0))
    hvec = pl.BlockSpec((1, 512), lambda i: (0, 0))
    mat = pl.BlockSpec((D_MODEL, D_MODEL), lambda i: (0, 0))
    return pl.pallas_call(
        body, name="bwd_out", grid=(s // tb,),
        in_specs=[row, row, row, half, half, hvec, hvec, mat, vec, pl.BlockSpec(memory_space=pl.ANY)],
        out_specs=[row, half, half, mat, vec, hvec, hvec],
        out_shape=[pltpu.HBM((s, D_MODEL), F32), pltpu.HBM((s, 512), F32),
                   pltpu.HBM((s, 512), F32), pltpu.HBM((D_MODEL, D_MODEL), BF16),
                   jax.ShapeDtypeStruct((1, D_MODEL), F32), jax.ShapeDtypeStruct((1, 512), F32),
                   jax.ShapeDtypeStruct((1, 512), F32)],
        scratch_shapes=[pltpu.VMEM((D_MODEL, D_MODEL), F32)],
        compiler_params=_params(48, 1),
    )(*_in_hbm(dn2, dh2, h1, y_pool, y_lru, gn_pool_g, gn_lru_g, w_out, ln2_g), after)


def _mixer_bwd(u_pool, u_lru, u_gate, h, dy_pool, dy_lru,
               pool_w, pool_scale, conv_w, conv_b, wa_bd, b_a, wi_bd, b_i, lam, after):
    s = u_pool.shape[0]
    tc = _time_chunk(s)
    n_chunks = s // tc

    def body(up_ref, ul_ref, ug_ref, h_ref, dyp_ref, dyl_ref,
             pw_ref, ps_ref, cw_ref, cb_ref, wa_ref, ba_ref, wi_ref, bi_ref, lam_ref, _after,
             dup_ref, dul_ref, dug_ref, dpw_ref, dps_ref, dcw_ref, dcb_ref, dwa_ref, dba_ref, dwi_ref, dbi_ref, dlam_ref):
        deeper, window = _slab_scalars()
        pw = pw_ref[0].astype(BF16)
        wa = wa_ref[0].astype(BF16)
        wi = wi_ref[0].astype(BF16)
        ps, cw, cb, ba, bi = ps_ref[...], cw_ref[...], cb_ref[...], ba_ref[...], bi_ref[...]
        lam_v = lam_ref[...]
        sp = _softplus_neg(lam_v)
        row = lax.broadcasted_iota(jnp.int32, (tc, SLAB), 0)
        for ref in (dpw_ref, dps_ref, dcw_ref, dcb_ref, dwa_ref, dba_ref, dwi_ref, dbi_ref, dlam_ref):
            ref[...] = jnp.zeros_like(ref)

        def chunk(t0, ext_p, ext_l, ext_h, carry, at_start=False):
            l_next, a_next, dxc_next, ddn_next = carry
            rows = pl.ds(t0, tc)
            taps = _conv_taps(ext_l)
            xc = _conv(taps, cw, cb)
            xb, r, i, a, mult = _lru_gates(xc, wa, ba, wi, bi, sp)
            hv = ext_h[HALO:]
            h_before = _down(ext_h, 1)[HALO:]
            ug = ug_ref[rows, :]
            dyl = dyl_ref[rows, :]
            gel, th = _gelu(ug)
            dug_ref[rows, :] = dyl * hv * _gelu_grad(ug, th)
            a_after = jnp.where(row == tc - 1, a_next, _up(a, 1))
            l = _scan_anticausal(a_after, dyl * gel, l_next, row, tc)
            dmult = l * (i * xc)
            di = l * mult * xc
            dxc = l * mult * i
            dla = (l * h_before) * a - jnp.where(mult > 0.0, dmult * (a * a) / mult, 0.0)
            dlam_ref[...] += jnp.sum(dla * r, axis=0, keepdims=True)
            dpa = (dla * ((-LRU_C) * sp)) * (r * (1.0 - r))
            dpi = di * (i * (1.0 - i))
            dpab = dpa.astype(BF16)
            dpib = dpi.astype(BF16)
            dwa_ref[0] += _mm_tn(xb, dpab)
            dwi_ref[0] += _mm_tn(xb, dpib)
            dba_ref[...] += jnp.sum(dpa, axis=0, keepdims=True)
            dbi_ref[...] += jnp.sum(dpi, axis=0, keepdims=True)
            dxc = dxc + _mm_nt(dpab, wa) + _mm_nt(dpib, wi)
            ext_d = jnp.concatenate([dxc, dxc_next], axis=0)
            dul_ref[rows, :] = (cw[3:4] * dxc + cw[2:3] * _up(ext_d, 1)[:tc]
                                + cw[1:2] * _up(ext_d, 2)[:tc] + cw[0:1] * _up(ext_d, 3)[:tc])
            for k in range(CONV_WIDTH):
                dcw_ref[k:k + 1, :] += jnp.sum(dxc * taps[CONV_WIDTH - 1 - k], axis=0, keepdims=True)
            dcb_ref[...] += jnp.sum(dxc, axis=0, keepdims=True)
            db = (_window_mean(_causal_window(ext_p, deeper), t0, row, window, at_start) - ext_p[HALO:]).astype(BF16)
            dyp = dyp_ref[rows, :]
            dps_ref[...] += jnp.sum(dyp * _mm(db, pw), axis=0, keepdims=True)
            dys = (dyp * ps).astype(BF16)
            dpw_ref[0] += _mm_tn(db, dys)
            dd = _mm_nt(dys, pw)
            ddn = _window_mean(dd, t0, row, window, at_start)
            ext_q = jnp.concatenate([ddn, ddn_next], axis=0)
            dup_ref[rows, :] = _anticausal_window(ext_q, deeper, tc) - dd
            return l[0:1, :], a[0:1, :], dxc[0:8, :], ddn[0:HALO, :]

        def step(k, carry):
            c = n_chunks - 1 - k
            t0 = pl.multiple_of(c * tc, tc)
            ext = pl.ds(pl.multiple_of(c * tc - HALO, HALO), tc + HALO)
            return chunk(t0, up_ref[ext, :], ul_ref[ext, :], h_ref[ext, :], carry)

        carry = (jnp.zeros((1, SLAB), F32), jnp.zeros((1, SLAB), F32),
                 jnp.zeros((8, SLAB), F32), jnp.zeros((HALO, SLAB), F32))
        carry = lax.fori_loop(0, n_chunks - 1, step, carry)
        pad = jnp.zeros((HALO, SLAB), F32)
        first = pl.ds(0, tc)
        chunk(0, jnp.concatenate([pad, up_ref[first, :]], axis=0), jnp.concatenate([pad, ul_ref[first, :]], axis=0),
              jnp.concatenate([pad, h_ref[first, :]], axis=0), carry, at_start=True)
        dlam_ref[...] = dlam_ref[...] * (LRU_C * jax.nn.sigmoid(-lam_v))

    seq, mat, vec, taps = _slab_specs(s)
    full = pltpu.HBM((s, 512), F32)
    mats = jax.ShapeDtypeStruct((N_SLAB, SLAB, SLAB), F32)
    vecs = jax.ShapeDtypeStruct((1, 512), F32)
    return pl.pallas_call(
        body, name="mixer_bwd", grid=(N_SLAB,),
        in_specs=[seq] * 6 + [mat, vec, taps, vec, mat, vec, mat, vec, vec, pl.BlockSpec(memory_space=pl.ANY)],
        out_specs=[seq, seq, seq, mat, vec, taps, vec, mat, vec, mat, vec, vec],
        out_shape=[full, full, full, mats, vecs, jax.ShapeDtypeStruct((CONV_WIDTH, 512), F32), vecs,
                   mats, vecs, mats, vecs, vecs],
        compiler_params=_params(56, 1),
    )(*_in_hbm(u_pool, u_lru, u_gate, h, dy_pool, dy_lru, pool_w, pool_scale, conv_w, conv_b, wa_bd, b_a, wi_bd,
               b_i, lam), after)


def _bwd_in(x, dh1, du_pool, du_lru, du_gate, ln1_g, w_in_t, after):
    s = x.shape[0]
    tb = _token_block(s)

    def body(x_ref, dh1_ref, dup_ref, dul_ref, dug_ref, g_ref, w_ref, _after, dx_ref, dwb_ref, dg_ref, dw_ref):
        @pl.when(pl.program_id(0) == 0)
        def _():
            dw_ref[...] = jnp.zeros_like(dw_ref)
            dg_ref[...] = jnp.zeros_like(dg_ref)

        g1 = g_ref[...]
        n, xhat, rstd = _rms(x_ref[...], g1)
        nb = n.astype(BF16)
        dn = jnp.zeros((tb, D_MODEL), F32)
        for k, ref in enumerate((dup_ref, dul_ref, dug_ref)):
            rows = slice(k * 512, (k + 1) * 512)
            db = ref[...].astype(BF16)
            dw_ref[rows, :] += _mm_tn(db, nb)
            dn = dn + _mm(db, w_ref[rows, :])
        dx, dg1 = _rms_bwd(dn, xhat, rstd, g1)
        dx_ref[...] = dh1_ref[...] + dx
        dg_ref[...] += dg1

        @pl.when(pl.program_id(0) == s // tb - 1)
        def _():
            dwb_ref[...] = dw_ref[...].astype(BF16)

    row = pl.BlockSpec((tb, D_MODEL), lambda i: (i, 0))
    half = pl.BlockSpec((tb, 512), lambda i: (i, 0))
    vec = pl.BlockSpec((1, D_MODEL), lambda i: (0, 0))
    mat = pl.BlockSpec((IN_WIDTH, D_MODEL), lambda i: (0, 0))
    return pl.pallas_call(
        body, name="bwd_in", grid=(s // tb,),
        in_specs=[row, row, half, half, half, vec, mat, pl.BlockSpec(memory_space=pl.ANY)],
        out_specs=[row, mat, vec],
        out_shape=[pltpu.HBM((s, D_MODEL), F32), pltpu.HBM((IN_WIDTH, D_MODEL), BF16),
                   jax.ShapeDtypeStruct((1, D_MODEL), F32)],
        scratch_shapes=[pltpu.VMEM((IN_WIDTH, D_MODEL), F32)],
        compiler_params=_params(48, 1),
    )(*_in_hbm(x, dh1, du_pool, du_lru, du_gate, ln1_g, w_in_t), after)


def _mesh_position():
    x, y, c = lax.axis_index("x"), lax.axis_index("y"), lax.axis_index("c")
    return x, y, c, 4 * x + 2 * y + c


def _peer(x, y, c, p):
    px = 1 - x if p & 4 else x
    py = 1 - y if p & 2 else y
    pc = 1 - c if p & 1 else c
    return (px, py, pc), 4 * px + 2 * py + pc


HBM_SPEC = pl.BlockSpec(memory_space=pltpu.HBM)
SEM_SPEC = pl.BlockSpec(memory_space=pltpu.SEMAPHORE)
DATAFLOW = pltpu.SideEffectType.DATAFLOW_SIDE_EFFECTING


class Copy(NamedTuple):
    src: int
    src_at: Any
    dst: int
    dst_at: Any
    peer: int
    group: int
    slot: int


SIBLING = (1,)
SAME_CORE = (2, 4, 6)
EVERYONE = tuple(range(1, N_DEV))


def _same(index):
    return index


def _chip(index):
    return jnp.right_shift(index, 1)


def _fan_out(srcs, lands, peers, group):
    return [Copy(s, None, d, _same, p, group, N_DEV * i + p) for i, (s, d) in enumerate(zip(srcs, lands)) for p in peers]


def _scatter(stacks, lands, peers):
    return [Copy(s, lambda me, p=p: jnp.bitwise_xor(me, p), d, _same, p, 0, 0)
            for s, d in zip(stacks, lands) for p in peers]


def _numbered(copies, group=0):
    return [cp._replace(group=group, slot=i) for i, cp in enumerate(copies)]


def _relay(lands, peers):
    return [Copy(b, lambda s, q=q: jnp.bitwise_xor(s, q), b, lambda s, q=q: jnp.bitwise_xor(s, q), 1, 0, N_DEV * i + q)
            for i, b in enumerate(lands) for q in peers]


def _to_sibling(stacks, lands):
    return [Copy(s, lambda me, k=k: 2 * k + 1 - jnp.bitwise_and(me, 1), d, lambda me, k=k: k, 1, 0, 4 * i + k)
            for i, (s, d) in enumerate(zip(stacks, lands)) for k in range(N_DEV // 2)]


def _to_chips(sums, lands):
    return [Copy(s, lambda me, p=p: jnp.bitwise_xor(_chip(me), p // 2), d, _chip, p, 0, 4 * i + p // 2)
            for i, (s, d) in enumerate(zip(sums, lands)) for p in SAME_CORE]


def _comm_call(name, bufs, wait=None, start=None, after=()):
    nb = len(bufs)
    slots = list(start[1]) if start else []
    n_out_sem = 2 * len(slots)
    after = [a for a in (after if isinstance(after, (list, tuple)) else [after]) if a is not None]

    def body(*refs):
        b = refs[:nb]
        at = nb
        if wait:
            w_send, w_recv = refs[at], refs[at + 1]
            at += 2
        at += len(after)
        out_sems = refs[at:at + n_out_sem]
        token = refs[at + n_out_sem + nb]
        x, y, c, me = _mesh_position()

        def part(i, row_of, sender):
            return b[i] if row_of is None else b[i].at[row_of(sender)]

        if wait:
            for cp in wait[1]:
                peer, peer_index = _peer(x, y, c, cp.peer)
                arrival = pltpu.make_async_remote_copy(part(cp.src, cp.src_at, me), part(cp.dst, cp.dst_at, peer_index),
                                                       w_send.at[cp.slot], w_recv.at[cp.slot],
                                                       device_id=peer, device_id_type=MESH)
                arrival.wait_send()
                arrival.wait_recv()
        if start:
            for cp in start[0]:
                peer, _ = _peer(x, y, c, cp.peer)
                pltpu.make_async_remote_copy(part(cp.src, cp.src_at, me), part(cp.dst, cp.dst_at, me),
                                             out_sems[2 * cp.group].at[cp.slot], out_sems[2 * cp.group + 1].at[cp.slot],
                                             device_id=peer, device_id_type=MESH).start()
        token[...] = jnp.zeros_like(token)

    sem_shapes = []
    for n_slots in slots:
        sem_shapes += [pltpu.SemaphoreType.DMA((n_slots,))] * 2
    operands = [pltpu.with_memory_space_constraint(a, pltpu.HBM) for a in bufs]
    in_specs = [HBM_SPEC] * nb
    if wait:
        operands += list(wait[0])
        in_specs += [SEM_SPEC, SEM_SPEC]
    operands += after
    in_specs += [pl.BlockSpec(memory_space=pl.ANY)] * len(after)
    outs = pl.pallas_call(
        body, name=name, in_specs=in_specs,
        out_specs=[SEM_SPEC] * n_out_sem + [HBM_SPEC] * nb + [pl.BlockSpec(memory_space=pltpu.VMEM)],
        out_shape=sem_shapes + [pltpu.HBM(a.shape, a.dtype) for a in bufs] + [jax.ShapeDtypeStruct((8, SLAB), F32)],
        input_output_aliases={i: n_out_sem + i for i in range(nb)},
        compiler_params=pltpu.CompilerParams(has_side_effects=DATAFLOW),
    )(*operands)
    sems = [(outs[2 * k], outs[2 * k + 1]) for k in range(len(slots))]
    return sems, list(outs[n_out_sem:n_out_sem + nb]), outs[-1]


def _pair_sum(stacks, lands, place):
    n = len(stacks)

    def body(place_ref, *refs):
        k = pl.program_id(0)
        for m in range(n):
            mine, theirs, out, land = refs[m], refs[n + m], refs[2 * n + m], refs[3 * n + m]
            total = (mine[0, 0].astype(F32) + theirs[0].astype(F32)).astype(out.dtype)
            out[0] = total

            @pl.when(k == place_ref[1])
            def _():
                land[0] = total

    in_specs = [pl.BlockSpec((1, 1) + a.shape[1:], lambda k, place_ref: (k, place_ref[0], 0, 0)) for a in stacks]
    in_specs += [pl.BlockSpec((1,) + a.shape[1:], lambda k, place_ref: (k, 0, 0)) for a in lands]
    out_specs = [pl.BlockSpec((1,) + a.shape[1:], lambda k, place_ref: (k, 0, 0)) for a in lands]
    out_specs += [pl.BlockSpec((1,) + a.shape[1:], lambda k, place_ref: (place_ref[1], 0, 0)) for a in lands]
    outs = pl.pallas_call(
        body, name="pair_sum_" + "_".join(str(a.shape[1]) for a in stacks),
        grid_spec=pltpu.PrefetchScalarGridSpec(num_scalar_prefetch=1, grid=(N_DEV // 2,), in_specs=in_specs,
                                               out_specs=out_specs),
        out_shape=[pltpu.HBM(a.shape, a.dtype) for a in lands] * 2,
        compiler_params=_params(40, 1),
    )(place, *_in_hbm(*[a.reshape((N_DEV // 2, 2) + a.shape[1:]) for a in stacks], *lands))
    return list(outs[:n]), list(outs[n:])


def _reduce_adam(parts, w, m, v, name):
    rows, cols = w.shape
    n_parts = parts.shape[0]
    rb = rows
    for cand in (256, 176, 128):
        if rows % cand == 0 and rows > cand:
            rb = cand
            break

    def body(p_ref, w_ref, m_ref, v_ref, g_out, d_out, m_out, v_out):
        g = p_ref[0].astype(F32)
        for j in range(1, n_parts):
            g = g + p_ref[j].astype(F32)
        g_out[...] = g
        d_out[...], m_out[...], v_out[...] = _adam(g, w_ref[...], m_ref[...], v_ref[...])

    blk = pl.BlockSpec((rb, cols), lambda i: (i, 0))
    out = jax.ShapeDtypeStruct((rows, cols), F32)
    return pl.pallas_call(
        body, name=name, grid=(rows // rb,),
        in_specs=[pl.BlockSpec((n_parts, rb, cols), lambda i: (0, i, 0)), blk, blk, blk],
        out_specs=[blk] * 4, out_shape=[out] * 4,
        compiler_params=_params(40, 1),
    )(*_in_hbm(parts, w, m, v))


def _cols_from_stack(stack):
    n, r, c = stack.shape
    return jnp.transpose(stack, (1, 0, 2)).reshape(r, n * c)


def _block_diag(w):
    z = jnp.zeros((N_SLAB, 64, 64), w.dtype)
    pairs = w.reshape(N_SLAB, 2, 64, 64)
    top = jnp.concatenate([pairs[:, 0], z], axis=2)
    bottom = jnp.concatenate([z, pairs[:, 1]], axis=2)
    return jnp.concatenate([top, bottom], axis=1)


def _adam(g, w, m, v):
    m_new = ADAM_B1 * m + (1.0 - ADAM_B1) * g
    v_new = ADAM_B2 * v + (1.0 - ADAM_B2) * (g * g)
    m_hat = m_new / (1.0 - ADAM_B1 ** ADAM_STEP)
    v_hat = v_new / (1.0 - ADAM_B2 ** ADAM_STEP)
    return (-ADAM_LR) * (m_hat / (jnp.sqrt(v_hat) + ADAM_EPS) + ADAM_WD * w), m_new, v_new


WIDE = ("ln2_g", "lnf_g")
HALF = ("pool_scale", "conv_b", "b_a", "b_i", "lam", "gn_pool_g", "gn_lru_g")
VECTORS = [(k, D_MODEL) for k in WIDE] + [(k, 512) for k in HALF]
VECTOR_ROWS = sum(width // SLAB for _, width in VECTORS)
LOSS_ROW = -(-VECTOR_ROWS // 8) * 8
CONV_AT = LOSS_ROW + 8
CONV_LANES = LRU_WIDTH // SLAB
PACK_F_ROWS = CONV_AT + CONV_WIDTH * CONV_LANES
MATRIX_ROWS = N_SLAB * SLAB
HEAD = SLAB // 2
GATE_ROWS = N_SLAB * HEAD
PACK_B_ROWS = MATRIX_ROWS + 2 * GATE_ROWS


def _pack_small(vectors, pool_g, wa_g, wi_g, conv, sq):
    n_vec = len(vectors)

    def body(*refs):
        vec = refs[:n_vec]
        pw_ref, wa_ref, wi_ref, cw_ref, sq_ref, out, out_b = refs[n_vec:]
        out[...] = jnp.zeros_like(out)
        row = 0
        for ref, (_, width) in zip(vec, VECTORS):
            for k in range(width // SLAB):
                out[row:row + 1, :] = ref[:, k * SLAB:(k + 1) * SLAB]
                row += 1
        for tap in range(CONV_WIDTH):
            for k in range(CONV_LANES):
                at = CONV_AT + tap * CONV_LANES + k
                out[at:at + 1, :] = cw_ref[tap:tap + 1, k * SLAB:(k + 1) * SLAB]
        total = sq_ref[:, 0:SLAB]
        for k in range(1, D_MODEL // SLAB):
            total = total + sq_ref[:, k * SLAB:(k + 1) * SLAB]
        out[LOSS_ROW:LOSS_ROW + 1, :] = total
        left = lax.broadcasted_iota(jnp.int32, (HEAD, SLAB), 1) < HEAD
        for s in range(N_SLAB):
            out_b[s * SLAB:(s + 1) * SLAB, :] = pw_ref[s].astype(BF16)
            for i, ref in enumerate((wa_ref, wi_ref)):
                at = MATRIX_ROWS + i * GATE_ROWS + s * HEAD
                out_b[at:at + HEAD, :] = jnp.where(left, ref[s, 0:HEAD, :], ref[s, HEAD:SLAB, :]).astype(BF16)

    return pl.pallas_call(
        body, name="pack_small",
        out_shape=[jax.ShapeDtypeStruct((PACK_F_ROWS, SLAB), F32), jax.ShapeDtypeStruct((PACK_B_ROWS, SLAB), BF16)],
    )(*vectors, pool_g, wa_g, wi_g, conv, sq)


def _small_reduce_adam(parts, parts_b, vec_w, vec_m, vec_v, pool_wmv):
    n_vec = len(VECTORS)
    n_parts = parts.shape[0]

    def body(*refs):
        p_ref, pb_ref = refs[0], refs[1]
        w_refs, m_refs, v_refs = (refs[2 + k * n_vec:2 + (k + 1) * n_vec] for k in range(3))
        pw_w, pw_m, pw_v = refs[2 + 3 * n_vec:5 + 3 * n_vec]
        outs = refs[5 + 3 * n_vec:-1]
        total = refs[-1]
        total[...] = p_ref[0]
        for j in range(1, n_parts):
            total[...] += p_ref[j]
        row = 0
        for i, (_, width) in enumerate(VECTORS):
            n_rows = width // SLAB
            g = jnp.concatenate([total[row + k:row + k + 1, :] for k in range(n_rows)], axis=1)
            row += n_rows
            d, m_new, v_new = _adam(g, w_refs[i][...], m_refs[i][...], v_refs[i][...])
            for ref, val in zip(outs[4 * i:4 * i + 4], (g, d, m_new, v_new)):
                ref[...] = val
        tail = outs[4 * n_vec:]

        def summed(first, count):
            g = pb_ref[0, first:first + count, :].astype(F32)
            for j in range(1, n_parts):
                g = g + pb_ref[j, first:first + count, :].astype(F32)
            return g

        g = summed(0, MATRIX_ROWS)
        d, m_new, v_new = _adam(g, pw_w[...], pw_m[...], pw_v[...])
        for ref, val in zip(tail[0:4], (g, d, m_new, v_new)):
            ref[...] = val
        tail[4][...] = summed(MATRIX_ROWS, GATE_ROWS)
        tail[5][...] = summed(MATRIX_ROWS + GATE_ROWS, GATE_ROWS)
        for tap in range(CONV_WIDTH):
            at = CONV_AT + tap * CONV_LANES
            tail[6][tap:tap + 1, :] = jnp.concatenate([total[at + k:at + k + 1, :] for k in range(CONV_LANES)], axis=1)
        tail[7][...] = (0.5 / D_MODEL) * jnp.sum(total[LOSS_ROW:LOSS_ROW + 1, :], axis=1, keepdims=True)

    out_shape = []
    for _, width in VECTORS:
        out_shape += [jax.ShapeDtypeStruct((1, width), F32)] * 4
    out_shape += [jax.ShapeDtypeStruct((MATRIX_ROWS, SLAB), F32)] * 4 + [jax.ShapeDtypeStruct((GATE_ROWS, SLAB), F32)] * 2
    out_shape += [jax.ShapeDtypeStruct((CONV_WIDTH, LRU_WIDTH), F32), jax.ShapeDtypeStruct((1, 1), F32)]
    outs = pl.pallas_call(
        body, name="adam_small", out_shape=out_shape,
        scratch_shapes=[pltpu.VMEM((PACK_F_ROWS, SLAB), F32)],
        compiler_params=_params(40),
    )(parts, parts_b, *vec_w, *vec_m, *vec_v, *pool_wmv)
    vec_out = [tuple(outs[4 * i:4 * i + 4]) for i in range(n_vec)]
    tail = outs[4 * n_vec:]
    return vec_out, tuple(tail[0:4]), tail[4], tail[5], tail[6], tail[7]


def _plain_adam(grads, ws, ms, vs):
    n = len(grads)

    def body(*refs):
        ins, outs = refs[:4 * n], refs[4 * n:]
        for i in range(n):
            d, m_new, v_new = _adam(ins[i][...], ins[n + i][...], ins[2 * n + i][...], ins[3 * n + i][...])
            for ref, val in zip(outs[3 * i:3 * i + 3], (d, m_new, v_new)):
                ref[...] = val

    out_shape = []
    for g in grads:
        out_shape += [jax.ShapeDtypeStruct(g.shape, F32)] * 3
    outs = pl.pallas_call(body, name="adam_plain", out_shape=out_shape)(*grads, *ws, *ms, *vs)
    return [tuple(outs[3 * i:3 * i + 3]) for i in range(n)]


def kernel(x, ln1_g, w_in, pool_w, pool_scale, conv_w, conv_b, w_a, b_a, w_i, b_i, lam, gn_pool_g, gn_lru_g, w_out, ln2_g, w_ffn_gate, w_ffn_up, w_ffn_down, lnf_g, loss_target, m_ln1_g, m_w_in, m_pool_w, m_pool_scale, m_conv_w, m_conv_b, m_w_a, m_b_a, m_w_i, m_b_i, m_lam, m_gn_pool_g, m_gn_lru_g, m_w_out, m_ln2_g, m_w_ffn_gate, m_w_ffn_up, m_w_ffn_down, m_lnf_g, v_ln1_g, v_w_in, v_pool_w, v_pool_scale, v_conv_w, v_conv_b, v_w_a, v_b_a, v_w_i, v_b_i, v_lam, v_gn_pool_g, v_gn_lru_g, v_w_out, v_ln2_g, v_w_ffn_gate, v_w_ffn_up, v_w_ffn_down, v_lnf_g):
    weights = dict(ln1_g=ln1_g, w_in=w_in, pool_w=pool_w, pool_scale=pool_scale, conv_w=conv_w, conv_b=conv_b,
                   w_a=w_a, b_a=b_a, w_i=w_i, b_i=b_i, lam=lam, gn_pool_g=gn_pool_g, gn_lru_g=gn_lru_g,
                   w_out=w_out, ln2_g=ln2_g, w_ffn_gate=w_ffn_gate, w_ffn_up=w_ffn_up, w_ffn_down=w_ffn_down,
                   lnf_g=lnf_g)
    mom1 = dict(ln1_g=m_ln1_g, w_in=m_w_in, pool_w=m_pool_w, pool_scale=m_pool_scale, conv_w=m_conv_w,
                conv_b=m_conv_b, w_a=m_w_a, b_a=m_b_a, w_i=m_w_i, b_i=m_b_i, lam=m_lam, gn_pool_g=m_gn_pool_g,
                gn_lru_g=m_gn_lru_g, w_out=m_w_out, ln2_g=m_ln2_g, w_ffn_gate=m_w_ffn_gate,
                w_ffn_up=m_w_ffn_up, w_ffn_down=m_w_ffn_down, lnf_g=m_lnf_g)
    mom2 = dict(ln1_g=v_ln1_g, w_in=v_w_in, pool_w=v_pool_w, pool_scale=v_pool_scale, conv_w=v_conv_w,
                conv_b=v_conv_b, w_a=v_w_a, b_a=v_b_a, w_i=v_w_i, b_i=v_b_i, lam=v_lam, gn_pool_g=v_gn_pool_g,
                gn_lru_g=v_gn_lru_g, w_out=v_w_out, ln2_g=v_ln2_g, w_ffn_gate=v_w_ffn_gate,
                w_ffn_up=v_w_ffn_up, w_ffn_down=v_w_ffn_down, lnf_g=v_lnf_g)

    xs = x[0]
    target = loss_target[0]

    shard = dict(w_in=lambda a: a[0].T, w_ffn_gate=lambda a: a[0].T, w_ffn_up=lambda a: a[0].T,
                 w_out=lambda a: a[0], w_ffn_down=lambda a: a[0], conv_w=lambda a: a[0])
    unshard = dict(w_in=lambda a: a.T[None], w_ffn_gate=lambda a: a.T[None], w_ffn_up=lambda a: a.T[None],
                   w_out=lambda a: a[None], w_ffn_down=lambda a: a[None], conv_w=lambda a: a[None])

    gathered = ("w_in", "conv_w", "w_out", "w_ffn_gate", "w_ffn_up", "w_ffn_down")
    groups = ((0, 1), (2,), (3, 4, 5))
    sources = [shard[k](weights[k]) if k == "conv_w" else shard[k](weights[k]).astype(BF16) for k in gathered]
    my_index = 4 * lax.axis_index("x") + 2 * lax.axis_index("y") + lax.axis_index("c")
    lands = [lax.dynamic_update_index_in_dim(lax.empty((N_DEV,) + a.shape, a.dtype), a, my_index, 0) for a in sources]

    def first_hop(n, group=0):
        return _numbered(_fan_out(range(n), range(n, 2 * n), SAME_CORE + SIBLING, 0), group)

    start = []
    for g, members in enumerate(groups):
        start += _numbered(_fan_out(members, [6 + m for m in members], SAME_CORE + SIBLING, 0), g)
    sems, bufs, _ = _comm_call("gather_start", sources + lands,
                               start=(start, [sum(cp.group == g for cp in start) for g in range(len(groups))]))
    sources, lands = bufs[:6], bufs[6:]

    def gathered_group(tag, g, after):
        members = groups[g]
        n = len(members)
        relay = _numbered(_relay(range(n, 2 * n), SAME_CORE))
        relay_sems, bufs, _ = _comm_call("gather_relay_" + tag, [sources[m] for m in members] + [lands[m] for m in members],
                                         wait=(sems[g], first_hop(n)), start=(relay, (len(relay),)), after=after)
        _, bufs, _ = _comm_call("gather_wait_" + tag, bufs[n:], wait=(relay_sems[0], _numbered(_relay(range(n), SAME_CORE))))
        return bufs

    g_in, g_conv = gathered_group("in", 0, None)
    w_in_f = g_in.reshape(IN_WIDTH, D_MODEL)
    conv_w_f = _cols_from_stack(g_conv)

    wa_bd = _block_diag(w_a[0])
    wi_bd = _block_diag(w_i[0])
    lnf_row = lnf_g.reshape(1, D_MODEL)

    u_pool, u_lru, u_gate = _fwd_in(xs, ln1_g, w_in_f)
    y_pool, h, y_lru = _mixer_fwd(u_pool, u_lru, u_gate, pool_w[0], pool_scale, conv_w_f, conv_b,
                                  wa_bd, b_a, wi_bd, b_i, lam)
    (g_out,) = gathered_group("out", 1, y_pool)
    w_out_f = g_out.reshape(D_MODEL, D_MODEL)
    h1, n2 = _fwd_out(xs, y_pool, y_lru, gn_pool_g, gn_lru_g, w_out_f, ln2_g)
    g_gate, g_up, g_down = gathered_group("ffn", 2, n2)
    w_gate_f = g_gate.reshape(D_FF, D_MODEL)
    w_up_f = g_up.reshape(D_FF, D_MODEL)
    w_down_f = g_down.reshape(D_FF, D_MODEL)
    g_act, u_act, dh2, dh2b, d_lnf, sq = _ffn_fwd(h1, n2, target, lnf_row, w_gate_f, w_up_f, w_down_f)

    place = jnp.stack([lax.axis_index("c"), 2 * lax.axis_index("x") + lax.axis_index("y")]).astype(jnp.int32)

    d_gate, d_up, d_down, dn2 = _ffn_bwd(n2, dh2b, g_act, u_act, w_gate_f, w_up_f, w_down_f)
    ffn_stacks = [d.reshape(N_DEV, D_FF // N_DEV, D_MODEL) for d in (d_gate, d_up, d_down)]
    pair_lands = [lax.empty((N_DEV // 2,) + a.shape[1:], a.dtype) for a in ffn_stacks]
    pair_copies = _numbered(_to_sibling(range(3), range(3, 6)))
    sem, bufs, token = _comm_call("ffn_pair_start", ffn_stacks + pair_lands, start=(pair_copies, (len(pair_copies),)))
    dh1, dy_pool, dy_lru, d_out, d_ln2, d_gnp, d_gnl = _bwd_out(dn2, dh2, h1, y_pool, y_lru, gn_pool_g, gn_lru_g,
                                                                 w_out_f, ln2_g, token)
    _, bufs, _ = _comm_call("ffn_pair_wait", bufs, wait=(sem[0], pair_copies), after=dh1)
    ffn_sums, ffn_lands = _pair_sum(bufs[:3], bufs[3:], place)
    ffn_copies = _numbered(_to_chips(range(3), range(3, 6)))
    ffn_sem, ffn_bufs, token = _comm_call("ffn_chip_start", ffn_sums + ffn_lands, start=(ffn_copies, (len(ffn_copies),)))
    (du_pool, du_lru, du_gate, d_pw, d_ps, d_cw, d_cb, d_wa, d_ba, d_wi, d_bi, d_lam) = _mixer_bwd(
        u_pool, u_lru, u_gate, h, dy_pool, dy_lru, pool_w[0], pool_scale, conv_w_f, conv_b,
        wa_bd, b_a, wi_bd, b_i, lam, token)

    def direct(tag, stacks, wholes, after):
        sources = list(stacks) + list(wholes)
        n, n_st = len(sources), len(stacks)
        lands = [lax.dynamic_update_index_in_dim(
            lax.empty(a.shape if i < n_st else (N_DEV,) + a.shape, a.dtype),
            lax.dynamic_index_in_dim(a, my_index, 0, keepdims=False) if i < n_st else a, my_index, 0)
            for i, a in enumerate(sources)]
        copies = _numbered(_scatter(range(n_st), range(n, n + n_st), EVERYONE)
                           + _fan_out(range(n_st, n), range(n + n_st, 2 * n), EVERYONE, 0))
        sem, bufs, token = _comm_call(tag + "_start", sources + lands, start=(copies, (len(copies),)), after=after)
        return (tag, sem[0], bufs, copies), token

    def direct_finish(started, after):
        tag, sem, bufs, copies = started
        _, bufs, _ = _comm_call(tag + "_wait", bufs, wait=(sem, copies), after=after)
        return bufs[len(bufs) // 2:]

    vec_grads = dict(ln2_g=d_ln2, lnf_g=d_lnf, pool_scale=d_ps, conv_b=d_cb, b_a=d_ba, b_i=d_bi,
                     lam=d_lam, gn_pool_g=d_gnp, gn_lru_g=d_gnl)
    packed, packed_b = _pack_small([vec_grads[k] for k, _ in VECTORS], d_pw, d_wa, d_wi, d_cw, sq)
    small_started, token = direct("small", [d_out.reshape(N_DEV, D_MODEL // N_DEV, D_MODEL)], [packed, packed_b], None)
    grad_x, d_in, d_ln1 = _bwd_in(xs, dh1, du_pool, du_lru, du_gate, ln1_g, w_in_f, token)
    in_started, token = direct("in", [d_in.reshape(N_DEV, IN_WIDTH // N_DEV, D_MODEL)], [d_ln1], None)

    results = {}

    def reduce_adam(name, parts):
        outs = _reduce_adam(parts, shard[name](weights[name]), shard[name](mom1[name]), shard[name](mom2[name]),
                            "adam_" + name)
        results[name] = tuple(unshard[name](o) for o in outs)
        return outs[0]

    shard["ln1_g"] = unshard["ln1_g"] = lambda a: a
    _, ffn_bufs, _ = _comm_call("ffn_chip_wait", ffn_bufs, wait=(ffn_sem[0], ffn_copies), after=token)
    done = [reduce_adam(name, parts)
            for name, parts in zip(("w_ffn_gate", "w_ffn_up", "w_ffn_down"), ffn_bufs[3:])]
    r_out, r_small, r_small_b = direct_finish(small_started, done)
    done.append(reduce_adam("w_out", r_out))

    def as_row(a, width):
        return a.reshape(1, width)

    def as_matrix(a):
        return a.reshape(MATRIX_ROWS, SLAB)

    def as_heads(a):
        return a.reshape(2 * GATE_ROWS, HEAD)

    def heads_apart(g):
        return jnp.transpose(g.reshape(N_SLAB, HEAD, 2, HEAD), (0, 2, 1, 3)).reshape(2 * GATE_ROWS, HEAD)

    vec_out, pool_out, g_wa, g_wi, g_conv, loss_11 = _small_reduce_adam(
        r_small, r_small_b, [as_row(weights[k], w) for k, w in VECTORS], [as_row(mom1[k], w) for k, w in VECTORS],
        [as_row(mom2[k], w) for k, w in VECTORS], [as_matrix(t["pool_w"]) for t in (weights, mom1, mom2)])
    for (k, _), outs in zip(VECTORS, vec_out):
        results[k] = tuple(o.reshape(weights[k].shape) for o in outs)
    results["pool_w"] = tuple(o.reshape(pool_w.shape) for o in pool_out)
    my_columns = conv_w.shape[-1]
    plain_names = ("w_a", "w_i", "conv_w")
    plain_grads = [heads_apart(g) for g in (g_wa, g_wi)]
    plain_grads.append(lax.dynamic_slice_in_dim(g_conv, my_index * my_columns, my_columns, axis=1))
    views = (as_heads, as_heads, lambda a: a[0])
    plain_out = _plain_adam(plain_grads, *[[view(t[k]) for k, view in zip(plain_names, views)]
                                           for t in (weights, mom1, mom2)])
    for k, g, outs in zip(plain_names, plain_grads, plain_out):
        results[k] = tuple(o.reshape(weights[k].shape) for o in (g,) + outs)
    loss = loss_11[0, 0]
    r_in, r_ln1 = direct_finish(in_started, done + [plain_out[0][0], loss_11])
    reduce_adam("w_in", r_in)
    reduce_adam("ln1_g", r_ln1)

    order = ["ln1_g", "w_in", "pool_w", "pool_scale", "conv_w", "conv_b", "w_a", "b_a", "w_i", "b_i", "lam",
             "gn_pool_g", "gn_lru_g", "w_out", "ln2_g", "w_ffn_gate", "w_ffn_up", "w_ffn_down", "lnf_g"]
    return (loss, grad_x[None],
            *[results[k][0] for k in order], *[results[k][1] for k in order],
            *[results[k][2] for k in order], *[results[k][3] for k in order])
```

```python
from typing import Any, NamedTuple

import jax
import jax.numpy as jnp
from jax import lax
from jax.experimental import pallas as pl
from jax.experimental.pallas import tpu as pltpu

F32 = jnp.float32
BF16 = jnp.bfloat16

N_DEV = 8
D_MODEL = 1024
POOL_WIDTH = 512
LRU_WIDTH = 512
IN_WIDTH = 1536
D_FF = 2816
N_SLAB = 4
SLAB = 128
CONV_WIDTH = 4
LRU_C = 8.0
EPS = 1e-6
HALO = 16
FF_CHUNK = 256

ADAM_LR = 0.001
ADAM_B1 = 0.9
ADAM_B2 = 0.999
ADAM_EPS = 1e-08
ADAM_WD = 0.01
ADAM_STEP = 10

MIB = 1 << 20
MESH = pl.DeviceIdType.MESH


def _params(vmem_mib, n_axes=0):
    sem = ("arbitrary",) * n_axes if n_axes else None
    return pltpu.CompilerParams(dimension_semantics=sem, vmem_limit_bytes=vmem_mib * MIB)


def _in_hbm(*arrays):
    return [pltpu.with_memory_space_constraint(a, pltpu.HBM) for a in arrays]


def _mm(a, b):
    return jnp.dot(a, b, preferred_element_type=F32)


def _mm_nt(a, b):
    return lax.dot_general(a, b, (((1,), (1,)), ((), ())), preferred_element_type=F32)


def _mm_tn(a, b):
    return lax.dot_general(a, b, (((0,), (0,)), ((), ())), preferred_element_type=F32)


def _rms(x, g):
    rstd = lax.rsqrt(jnp.mean(x * x, axis=-1, keepdims=True) + EPS)
    xhat = x * rstd
    return xhat * g, xhat, rstd


def _rms_bwd(dy, xhat, rstd, g):
    gy = dy * g
    dx = rstd * (gy - xhat * jnp.mean(gy * xhat, axis=-1, keepdims=True))
    return dx, jnp.sum(dy * xhat, axis=0, keepdims=True)


def _gelu(z):
    t = jnp.tanh(0.7978845608028654 * (z + 0.044715 * z * z * z))
    return 0.5 * z * (1.0 + t), t


def _gelu_grad(z, t):
    return 0.5 * (1.0 + t) + 0.5 * z * (1.0 - t * t) * 0.7978845608028654 * (1.0 + 3.0 * 0.044715 * z * z)


def _softplus_neg(lam):
    x = -lam
    e = jnp.exp(-jnp.abs(x))
    u = 1.0 + e
    l1p = jnp.where(u == 1.0, e, jnp.log(u) * e / (u - 1.0))
    return jnp.maximum(x, 0.0) + l1p


def _one_minus_square(la, a):
    x = 2.0 * la
    series = -x * (1.0 + x * (0.5 + x * (1.0 / 6.0 + x * (1.0 / 24.0))))
    return jnp.where(x > -0.06, series, 1.0 - a * a)


def _sigmoid(x):
    return 0.5 * jnp.tanh(0.5 * x) + 0.5


def _down(v, d):
    return pltpu.roll(v, d, 0)


def _up(v, d):
    return pltpu.roll(v, v.shape[0] - d, 0)


def _token_block(s, most=512):
    for rows in (most, 512, 256):
        if rows <= most and s % rows == 0 and s > rows:
            return rows
    return s


def _time_chunk(s):
    return 512 if s % 512 == 0 and s > 512 else 256 if s % 256 == 0 else s


def _fwd_in(x, ln1_g, w_in_t):
    s = x.shape[0]
    tb = _token_block(s, 1024)

    def body(x_ref, g_ref, w_ref, up_ref, ul_ref, ug_ref):
        n, _, _ = _rms(x_ref[...], g_ref[...])
        proj = _mm_nt(n.astype(BF16), w_ref[...])
        up_ref[...] = proj[:, :POOL_WIDTH]
        ul_ref[...] = proj[:, POOL_WIDTH:POOL_WIDTH + LRU_WIDTH]
        ug_ref[...] = proj[:, POOL_WIDTH + LRU_WIDTH:]

    out = pltpu.HBM((s, 512), F32)
    return pl.pallas_call(
        body, name="fwd_in", grid=(s // tb,),
        in_specs=[pl.BlockSpec((tb, D_MODEL), lambda i: (i, 0)),
                  pl.BlockSpec((1, D_MODEL), lambda i: (0, 0)),
                  pl.BlockSpec((IN_WIDTH, D_MODEL), lambda i: (0, 0))],
        out_specs=[pl.BlockSpec((tb, 512), lambda i: (i, 0))] * 3,
        out_shape=[out, out, out],
        compiler_params=_params(40, 1),
    )(*_in_hbm(x, ln1_g, w_in_t))


def _pool_denominator(t0, row, window):
    return jnp.minimum((t0 + row + 1).astype(F32), window)


def _causal_window(ext, deeper):
    s = ext + _down(ext, 1)
    s = s + deeper[0] * _down(s, 2)
    s = s + deeper[1] * _down(s, 4)
    s = s + deeper[2] * _down(s, 8)
    return s[HALO:]


def _anticausal_window(ext, deeper, rows):
    s = ext + _up(ext, 1)
    s = s + deeper[0] * _up(s, 2)
    s = s + deeper[1] * _up(s, 4)
    s = s + deeper[2] * _up(s, 8)
    return s[:rows]


def _conv_taps(ext):
    return [ext[HALO:], _down(ext, 1)[HALO:], _down(ext, 2)[HALO:], _down(ext, 3)[HALO:]]


def _conv(taps, cw, cb):
    return cw[3] * taps[0] + cw[2] * taps[1] + cw[1] * taps[2] + cw[0] * taps[3] + cb


def _lru_gates(xc, wa, ba, wi, bi, sp):
    xb = xc.astype(BF16)
    r = _sigmoid(_mm(xb, wa) + ba)
    i = _sigmoid(_mm(xb, wi) + bi)
    la = (-LRU_C) * r * sp
    a = jnp.exp(la)
    mult = jnp.sqrt(jnp.maximum(_one_minus_square(la, a), 0.0))
    return xb, r, i, a, mult


SUBLANES = 8


def _scan_pads(a_pad, b_pad, rows, causal):
    spare = slice(0, SUBLANES) if causal else slice(rows, rows + SUBLANES)
    a_pad[spare, :] = jnp.ones((SUBLANES, SLAB), F32)
    b_pad[spare, :] = jnp.zeros((SUBLANES, SLAB), F32)


def _scan_causal(a, b, h_prev, a_pad, b_pad, rows):
    d = 1
    while d < min(SUBLANES, rows):
        a_pad[SUBLANES:, :] = a
        b_pad[SUBLANES:, :] = b
        b = a * b_pad[SUBLANES - d:SUBLANES - d + rows, :] + b
        a = a * a_pad[SUBLANES - d:SUBLANES - d + rows, :]
        d *= 2
    while d < rows:
        b = jnp.concatenate([b[:d], a[d:] * b[:-d] + b[d:]], axis=0)
        a = jnp.concatenate([a[:d], a[d:] * a[:-d]], axis=0)
        d *= 2
    return b + a * h_prev


def _scan_anticausal(a, b, l_next, a_pad, b_pad, rows):
    d = 1
    while d < min(SUBLANES, rows):
        a_pad[:rows, :] = a
        b_pad[:rows, :] = b
        b = a * b_pad[d:d + rows, :] + b
        a = a * a_pad[d:d + rows, :]
        d *= 2
    while d < rows:
        b = jnp.concatenate([a[:-d] * b[d:] + b[:-d], b[-d:]], axis=0)
        a = jnp.concatenate([a[:-d] * a[d:], a[-d:]], axis=0)
        d *= 2
    return b + a * l_next


def _slab_scalars():
    slab = pl.program_id(0)
    deeper = [jnp.where(slab > k, 1.0, 0.0).astype(F32) for k in range(N_SLAB - 1)]
    window = jnp.left_shift(jnp.int32(2), slab).astype(F32)
    inverse = jnp.where(slab == 0, 0.5, jnp.where(slab == 1, 0.25, jnp.where(slab == 2, 0.125, 0.0625))).astype(F32)
    return deeper, (window, inverse)


def _window_mean(total, t0, row, window, at_start):
    if at_start:
        return total / _pool_denominator(t0, row, window[0])
    return total * window[1]


def _slab_specs(s):
    seq = pl.BlockSpec((s, SLAB), lambda k: (0, k))
    mat = pl.BlockSpec((1, SLAB, SLAB), lambda k: (k, 0, 0))
    vec = pl.BlockSpec((1, SLAB), lambda k: (0, k))
    taps = pl.BlockSpec((CONV_WIDTH, SLAB), lambda k: (0, k))
    return seq, mat, vec, taps


def _mixer_fwd(u_pool, u_lru, u_gate, pool_w, pool_scale, conv_w, conv_b, wa_bd, b_a, wi_bd, b_i, lam):
    s = u_pool.shape[0]
    tc = _time_chunk(s)
    n_chunks = s // tc

    def body(up_ref, ul_ref, ug_ref, pw_ref, ps_ref, cw_ref, cb_ref, wa_ref, ba_ref, wi_ref, bi_ref, lam_ref,
             yp_ref, h_ref, yl_ref, shift_ref, a_pad, b_pad):
        _scan_pads(a_pad, b_pad, tc, causal=True)
        deeper, window = _slab_scalars()
        pw = pw_ref[0].astype(BF16)
        wa = wa_ref[0].astype(BF16)
        wi = wi_ref[0].astype(BF16)
        ps, cb, ba, bi = ps_ref[...], cb_ref[...], ba_ref[...], bi_ref[...]
        cw = [cw_ref[k:k + 1, :] for k in range(CONV_WIDTH)]
        sp = _softplus_neg(lam_ref[...])
        row = lax.broadcasted_iota(jnp.int32, (tc, SLAB), 0)

        def chunk(t0, ext_p, ext_l, h_prev, at_start=False):
            rows = pl.ds(t0, tc)
            d = _window_mean(_causal_window(ext_p, deeper), t0, row, window, at_start) - ext_p[HALO:]
            yp_ref[rows, :] = _mm(d.astype(BF16), pw) * ps
            shift_ref[...] = ext_l
            xc = _conv([shift_ref[HALO - k:HALO - k + tc, :] for k in range(CONV_WIDTH)], cw, cb)
            _, _, i, a, mult = _lru_gates(xc, wa, ba, wi, bi, sp)
            h = _scan_causal(a, mult * (i * xc), h_prev, a_pad, b_pad, tc)
            h_ref[rows, :] = h
            yl_ref[rows, :] = h * _gelu(ug_ref[rows, :])[0]
            return h[tc - 1:tc, :]

        pad = jnp.zeros((HALO, SLAB), F32)
        h0 = chunk(0, jnp.concatenate([pad, up_ref[pl.ds(0, tc), :]], axis=0),
                   jnp.concatenate([pad, ul_ref[pl.ds(0, tc), :]], axis=0), jnp.zeros((1, SLAB), F32), at_start=True)

        def step(c, h_prev):
            t0 = pl.multiple_of(c * tc, tc)
            ext = pl.ds(pl.multiple_of(c * tc - HALO, HALO), tc + HALO)
            return chunk(t0, up_ref[ext, :], ul_ref[ext, :], h_prev)

        lax.fori_loop(1, n_chunks, step, h0)

    seq, mat, vec, taps = _slab_specs(s)
    out = pltpu.HBM((s, 512), F32)
    return pl.pallas_call(
        body, name="mixer_fwd", grid=(N_SLAB,),
        in_specs=[seq, seq, seq, mat, vec, taps, vec, mat, vec, mat, vec, vec],
        out_specs=[seq, seq, seq], out_shape=[out, out, out],
        scratch_shapes=[pltpu.VMEM((tc + HALO, SLAB), F32), pltpu.VMEM((tc + SUBLANES, SLAB), F32),
                        pltpu.VMEM((tc + SUBLANES, SLAB), F32)],
        compiler_params=_params(48, 1),
    )(*_in_hbm(u_pool, u_lru, u_gate, pool_w, pool_scale, conv_w, conv_b, wa_bd, b_a, wi_bd, b_i, lam))


def _fwd_out(x, y_pool, y_lru, gn_pool_g, gn_lru_g, w_out, ln2_g):
    s = x.shape[0]
    tb = _token_block(s, 1024)

    def body(x_ref, yp_ref, yl_ref, gp_ref, gl_ref, w_ref, g2_ref, h1_ref, n2_ref):
        mp, _, _ = _rms(yp_ref[...], gp_ref[...])
        ml, _, _ = _rms(yl_ref[...], gl_ref[...])
        h1 = x_ref[...] + _mm(mp.astype(BF16), w_ref[:POOL_WIDTH, :]) + _mm(ml.astype(BF16), w_ref[POOL_WIDTH:, :])
        h1_ref[...] = h1
        n2_ref[...] = _rms(h1, g2_ref[...])[0].astype(BF16)

    row = pl.BlockSpec((tb, D_MODEL), lambda i: (i, 0))
    half = pl.BlockSpec((tb, 512), lambda i: (i, 0))
    return pl.pallas_call(
        body, name="fwd_out", grid=(s // tb,),
        in_specs=[row, half, half, pl.BlockSpec((1, 512), lambda i: (0, 0)), pl.BlockSpec((1, 512), lambda i: (0, 0)),
                  pl.BlockSpec((D_MODEL, D_MODEL), lambda i: (0, 0)), pl.BlockSpec((1, D_MODEL), lambda i: (0, 0))],
        out_specs=[row, row],
        out_shape=[pltpu.HBM((s, D_MODEL), F32), pltpu.HBM((s, D_MODEL), BF16)],
        compiler_params=_params(40, 1),
    )(*_in_hbm(x, y_pool, y_lru, gn_pool_g, gn_lru_g, w_out, ln2_g))


def _ffn_fwd(h1, n2, target, lnf_g, w_gate, w_up, w_down):
    s = h1.shape[0]
    tb = 512
    sub = 256
    n_ff = D_FF // FF_CHUNK

    def body(h1_ref, n2_ref, t_ref, gf_ref, wg_hbm, wu_hbm, wd_hbm,
             g_ref, u_ref, dh_ref, dhb_ref, dgf_ref, sq_ref, wg, wu, wd, sem):
        @pl.when(pl.program_id(0) == 0)
        def _():
            loads = [pltpu.make_async_copy(src, dst, sem.at[k])
                     for k, (src, dst) in enumerate(((wg_hbm, wg), (wu_hbm, wu), (wd_hbm, wd)))]
            for cp in loads:
                cp.start()
            for cp in loads:
                cp.wait()
            dgf_ref[...] = jnp.zeros_like(dgf_ref)
            sq_ref[...] = jnp.zeros_like(sq_ref)

        n2v = n2_ref[...]
        acc = jnp.zeros((tb, D_MODEL), F32)
        for c in range(n_ff):
            cols = slice(c * FF_CHUNK, (c + 1) * FF_CHUNK)
            g = _mm_nt(n2v, wg[cols, :])
            u = _mm_nt(n2v, wu[cols, :])
            g_ref[:, cols] = g.astype(BF16)
            u_ref[:, cols] = u.astype(BF16)
            act = g * jax.nn.sigmoid(g) * u
            acc = acc + _mm(act.astype(BF16), wd[cols, :])
        gf = gf_ref[...]
        for r in range(tb // sub):
            rows = slice(r * sub, (r + 1) * sub)
            y, xhat, rstd = _rms(h1_ref[rows, :] + acc[rows, :], gf)
            err = y - t_ref[rows, :]
            sq_ref[...] += jnp.sum(err * err, axis=0, keepdims=True)
            dh2, dgf = _rms_bwd(err * (1.0 / D_MODEL), xhat, rstd, gf)
            dgf_ref[...] += dgf
            dh_ref[rows, :] = dh2
            dhb_ref[rows, :] = dh2.astype(BF16)

    row = pl.BlockSpec((tb, D_MODEL), lambda i: (i, 0))
    ff = pl.BlockSpec((tb, D_FF), lambda i: (i, 0))
    vec = pl.BlockSpec((1, D_MODEL), lambda i: (0, 0))
    anyspace = pl.BlockSpec(memory_space=pl.ANY)
    return pl.pallas_call(
        body, name="ffn_fwd", grid=(s // tb,),
        in_specs=[row, row, row, vec, anyspace, anyspace, anyspace],
        out_specs=[ff, ff, row, row, vec, vec],
        out_shape=[pltpu.HBM((s, D_FF), BF16), pltpu.HBM((s, D_FF), BF16),
                   pltpu.HBM((s, D_MODEL), F32), pltpu.HBM((s, D_MODEL), BF16),
                   jax.ShapeDtypeStruct((1, D_MODEL), F32), jax.ShapeDtypeStruct((1, D_MODEL), F32)],
        scratch_shapes=[pltpu.VMEM((D_FF, D_MODEL), BF16), pltpu.VMEM((D_FF, D_MODEL), BF16),
                        pltpu.VMEM((D_FF, D_MODEL), BF16), pltpu.SemaphoreType.DMA((3,))],
        compiler_params=_params(60, 1),
    )(*_in_hbm(h1, n2, target, lnf_g, w_gate, w_up, w_down))


def _ffn_bwd(n2, dh2b, g, u, w_gate_t, w_up_t, w_down):
    s = n2.shape[0]
    tb = min(1024, s)
    n_ff = D_FF // FF_CHUNK
    n_tb = s // tb

    def body(n2_ref, dh_ref, g_ref, u_ref, wg_ref, wu_ref, wd_ref, dwg_ref, dwu_ref, dwd_ref, dn2_ref,
             dn2_acc, acc_g, acc_u, acc_d):
        j = pl.program_id(0)
        t = pl.program_id(1)

        @pl.when(t == 0)
        def _():
            acc_g[...] = jnp.zeros_like(acc_g)
            acc_u[...] = jnp.zeros_like(acc_u)
            acc_d[...] = jnp.zeros_like(acc_d)

        n2v = n2_ref[...]
        dh = dh_ref[...]
        gv = g_ref[...].astype(F32)
        uv = u_ref[...].astype(F32)
        sg = jax.nn.sigmoid(gv)
        silu = gv * sg
        dact = _mm_nt(dh, wd_ref[...])
        dub = (dact * silu).astype(BF16)
        dgb = (dact * uv * (sg * (1.0 + gv * (1.0 - sg)))).astype(BF16)
        acc_d[...] += _mm_tn((silu * uv).astype(BF16), dh)
        acc_g[...] += _mm_tn(dgb, n2v)
        acc_u[...] += _mm_tn(dub, n2v)
        part = _mm(jnp.concatenate([dgb, dub], axis=1), jnp.concatenate([wg_ref[...], wu_ref[...]], axis=0))
        rows = pl.ds(pl.multiple_of(t * tb, tb), tb)

        @pl.when(t == n_tb - 1)
        def _():
            dwg_ref[...] = acc_g[...].astype(BF16)
            dwu_ref[...] = acc_u[...].astype(BF16)
            dwd_ref[...] = acc_d[...].astype(BF16)

        @pl.when(j == 0)
        def _():
            dn2_acc[rows, :] = part

        @pl.when(jnp.logical_and(j > 0, j < n_ff - 1))
        def _():
            dn2_acc[rows, :] += part

        @pl.when(j == n_ff - 1)
        def _():
            dn2_ref[...] = dn2_acc[rows, :] + part

    row = pl.BlockSpec((tb, D_MODEL), lambda j, t: (t, 0))
    act = pl.BlockSpec((tb, FF_CHUNK), lambda j, t: (t, j))
    w_row = pl.BlockSpec((FF_CHUNK, D_MODEL), lambda j, t: (j, 0))
    last = pl.BlockSpec((tb, D_MODEL), lambda j, t: (jnp.where(j == n_ff - 1, t, 0), 0))
    grad = pltpu.HBM((D_FF, D_MODEL), BF16)
    chunk_acc = pltpu.VMEM((FF_CHUNK, D_MODEL), F32)
    return pl.pallas_call(
        body, name="ffn_bwd", grid=(n_ff, n_tb),
        in_specs=[row, row, act, act, w_row, w_row, w_row],
        out_specs=[w_row, w_row, w_row, last],
        out_shape=[grad, grad, grad, pltpu.HBM((s, D_MODEL), F32)],
        scratch_shapes=[pltpu.VMEM((s, D_MODEL), F32), chunk_acc, chunk_acc, chunk_acc],
        compiler_params=_params(56, 2),
    )(*_in_hbm(n2, dh2b, g, u, w_gate_t, w_up_t, w_down))


def _bwd_out(dn2, dh2, h1, y_pool, y_lru, gn_pool_g, gn_lru_g, w_out, ln2_g, after):
    s = h1.shape[0]
    tb = _token_block(s)

    def body(dn2_ref, dh2_ref, h1_ref, yp_ref, yl_ref, gp_ref, gl_ref, w_ref, g2_ref, _after,
             dh1_ref, dyp_ref, dyl_ref, dwb_ref, dg2_ref, dgp_ref, dgl_ref, dw_ref):
        @pl.when(pl.program_id(0) == 0)
        def _():
            dw_ref[...] = jnp.zeros_like(dw_ref)
            dg2_ref[...] = jnp.zeros_like(dg2_ref)
            dgp_ref[...] = jnp.zeros_like(dgp_ref)
            dgl_ref[...] = jnp.zeros_like(dgl_ref)

        g2 = g2_ref[...]
        _, xhat2, rstd2 = _rms(h1_ref[...], g2)
        dres, dg2 = _rms_bwd(dn2_ref[...], xhat2, rstd2, g2)
        dg2_ref[...] += dg2
        dh1 = dh2_ref[...] + dres
        dh1_ref[...] = dh1
        dh1b = dh1.astype(BF16)
        gp, gl = gp_ref[...], gl_ref[...]
        mp, xhat_p, rstd_p = _rms(yp_ref[...], gp)
        ml, xhat_l, rstd_l = _rms(yl_ref[...], gl)
        dw_ref[:POOL_WIDTH, :] += _mm_tn(mp.astype(BF16), dh1b)
        dw_ref[POOL_WIDTH:, :] += _mm_tn(ml.astype(BF16), dh1b)
        dyp, dgp = _rms_bwd(_mm_nt(dh1b, w_ref[:POOL_WIDTH, :]), xhat_p, rstd_p, gp)
        dyl, dgl = _rms_bwd(_mm_nt(dh1b, w_ref[POOL_WIDTH:, :]), xhat_l, rstd_l, gl)
        dyp_ref[...] = dyp
        dyl_ref[...] = dyl
        dgp_ref[...] += dgp
        dgl_ref[...] += dgl

        @pl.when(pl.program_id(0) == s // tb - 1)
        def _():
            dwb_ref[...] = dw_ref[...].astype(BF16)

    row = pl.BlockSpec((tb, D_MODEL), lambda i: (i, 0))
    half = pl.BlockSpec((tb, 512), lambda i: (i, 0))
    vec = pl.BlockSpec((1, D_MODEL), lambda i: (0, 0))
    hvec = pl.BlockSpec((1, 512), lambda i: (0, 0))
    mat = pl.BlockSpec((D_MODEL, D_MODEL), lambda i: (0, 0))
    return pl.pallas_call(
        body, name="bwd_out", grid=(s // tb,),
        in_specs=[row, row, row, half, half, hvec, hvec, mat, vec, pl.BlockSpec(memory_space=pl.ANY)],
        out_specs=[row, half, half, mat, vec, hvec, hvec],
        out_shape=[pltpu.HBM((s, D_MODEL), F32), pltpu.HBM((s, 512), F32),
                   pltpu.HBM((s, 512), F32), pltpu.HBM((D_MODEL, D_MODEL), BF16),
                   jax.ShapeDtypeStruct((1, D_MODEL), F32), jax.ShapeDtypeStruct((1, 512), F32),
                   jax.ShapeDtypeStruct((1, 512), F32)],
        scratch_shapes=[pltpu.VMEM((D_MODEL, D_MODEL), F32)],
        compiler_params=_params(48, 1),
    )(*_in_hbm(dn2, dh2, h1, y_pool, y_lru, gn_pool_g, gn_lru_g, w_out, ln2_g), after)


def _mixer_bwd(u_pool, u_lru, u_gate, h, dy_pool, dy_lru,
               pool_w, pool_scale, conv_w, conv_b, wa_bd, b_a, wi_bd, b_i, lam, after):
    s = u_pool.shape[0]
    tc = _time_chunk(s)
    n_chunks = s // tc

    def body(up_ref, ul_ref, ug_ref, h_ref, dyp_ref, dyl_ref,
             pw_ref, ps_ref, cw_ref, cb_ref, wa_ref, ba_ref, wi_ref, bi_ref, lam_ref, _after,
             dup_ref, dul_ref, dug_ref, dpw_ref, dps_ref, dcw_ref, dcb_ref, dwa_ref, dba_ref, dwi_ref, dbi_ref, dlam_ref,
             a_pad, b_pad):
        _scan_pads(a_pad, b_pad, tc, causal=False)
        deeper, window = _slab_scalars()
        pw = pw_ref[0].astype(BF16)
        wa = wa_ref[0].astype(BF16)
        wi = wi_ref[0].astype(BF16)
        ps, cb, ba, bi = ps_ref[...], cb_ref[...], ba_ref[...], bi_ref[...]
        cw = [cw_ref[k:k + 1, :] for k in range(CONV_WIDTH)]
        lam_v = lam_ref[...]
        sp = _softplus_neg(lam_v)
        row = lax.broadcasted_iota(jnp.int32, (tc, SLAB), 0)
        for ref in (dpw_ref, dps_ref, dcw_ref, dcb_ref, dwa_ref, dba_ref, dwi_ref, dbi_ref, dlam_ref):
            ref[...] = jnp.zeros_like(ref)

        def chunk(t0, ext_p, ext_l, ext_h, carry, at_start=False):
            l_next, a_next, dxc_next, ddn_next = carry
            rows = pl.ds(t0, tc)
            taps = _conv_taps(ext_l)
            xc = _conv(taps, cw, cb)
            xb, r, i, a, mult = _lru_gates(xc, wa, ba, wi, bi, sp)
            hv = ext_h[HALO:]
            h_before = _down(ext_h, 1)[HALO:]
            ug = ug_ref[rows, :]
            dyl = dyl_ref[rows, :]
            gel, th = _gelu(ug)
            dug_ref[rows, :] = dyl * hv * _gelu_grad(ug, th)
            a_after = jnp.where(row == tc - 1, a_next, _up(a, 1))
            l = _scan_anticausal(a_after, dyl * gel, l_next, a_pad, b_pad, tc)
            dmult = l * (i * xc)
            di = l * mult * xc
            dxc = l * mult * i
            dla = (l * h_before) * a - jnp.where(mult > 0.0, dmult * (a * a) / mult, 0.0)
            dlam_ref[...] += jnp.sum(dla * r, axis=0, keepdims=True)
            dpa = (dla * ((-LRU_C) * sp)) * (r * (1.0 - r))
            dpi = di * (i * (1.0 - i))
            dpab = dpa.astype(BF16)
            dpib = dpi.astype(BF16)
            dwa_ref[0] += _mm_tn(xb, dpab)
            dwi_ref[0] += _mm_tn(xb, dpib)
            dba_ref[...] += jnp.sum(dpa, axis=0, keepdims=True)
            dbi_ref[...] += jnp.sum(dpi, axis=0, keepdims=True)
            dxc = dxc + _mm_nt(dpab, wa) + _mm_nt(dpib, wi)
            ext_d = jnp.concatenate([dxc, dxc_next], axis=0)
            dul_ref[rows, :] = (cw[3] * dxc + cw[2] * _up(ext_d, 1)[:tc]
                                + cw[1] * _up(ext_d, 2)[:tc] + cw[0] * _up(ext_d, 3)[:tc])
            for k in range(CONV_WIDTH):
                dcw_ref[k:k + 1, :] += jnp.sum(dxc * taps[CONV_WIDTH - 1 - k], axis=0, keepdims=True)
            dcb_ref[...] += jnp.sum(dxc, axis=0, keepdims=True)
            db = (_window_mean(_causal_window(ext_p, deeper), t0, row, window, at_start) - ext_p[HALO:]).astype(BF16)
            dyp = dyp_ref[rows, :]
            dps_ref[...] += jnp.sum(dyp * _mm(db, pw), axis=0, keepdims=True)
            dys = (dyp * ps).astype(BF16)
            dpw_ref[0] += _mm_tn(db, dys)
            dd = _mm_nt(dys, pw)
            ddn = _window_mean(dd, t0, row, window, at_start)
            ext_q = jnp.concatenate([ddn, ddn_next], axis=0)
            dup_ref[rows, :] = _anticausal_window(ext_q, deeper, tc) - dd
            return l[0:1, :], a[0:1, :], dxc[0:8, :], ddn[0:HALO, :]

        def step(k, carry):
            c = n_chunks - 1 - k
            t0 = pl.multiple_of(c * tc, tc)
            ext = pl.ds(pl.multiple_of(c * tc - HALO, HALO), tc + HALO)
            return chunk(t0, up_ref[ext, :], ul_ref[ext, :], h_ref[ext, :], carry)

        carry = (jnp.zeros((1, SLAB), F32), jnp.zeros((1, SLAB), F32),
                 jnp.zeros((8, SLAB), F32), jnp.zeros((HALO, SLAB), F32))
        carry = lax.fori_loop(0, n_chunks - 1, step, carry)
        pad = jnp.zeros((HALO, SLAB), F32)
        first = pl.ds(0, tc)
        chunk(0, jnp.concatenate([pad, up_ref[first, :]], axis=0), jnp.concatenate([pad, ul_ref[first, :]], axis=0),
              jnp.concatenate([pad, h_ref[first, :]], axis=0), carry, at_start=True)
        dlam_ref[...] = dlam_ref[...] * (LRU_C * jax.nn.sigmoid(-lam_v))

    seq, mat, vec, taps = _slab_specs(s)
    full = pltpu.HBM((s, 512), F32)
    mats = jax.ShapeDtypeStruct((N_SLAB, SLAB, SLAB), F32)
    vecs = jax.ShapeDtypeStruct((1, 512), F32)
    return pl.pallas_call(
        body, name="mixer_bwd", grid=(N_SLAB,),
        in_specs=[seq] * 6 + [mat, vec, taps, vec, mat, vec, mat, vec, vec, pl.BlockSpec(memory_space=pl.ANY)],
        out_specs=[seq, seq, seq, mat, vec, taps, vec, mat, vec, mat, vec, vec],
        out_shape=[full, full, full, mats, vecs, jax.ShapeDtypeStruct((CONV_WIDTH, 512), F32), vecs,
                   mats, vecs, mats, vecs, vecs],
        scratch_shapes=[pltpu.VMEM((tc + SUBLANES, SLAB), F32), pltpu.VMEM((tc + SUBLANES, SLAB), F32)],
        compiler_params=_params(56, 1),
    )(*_in_hbm(u_pool, u_lru, u_gate, h, dy_pool, dy_lru, pool_w, pool_scale, conv_w, conv_b, wa_bd, b_a, wi_bd,
               b_i, lam), after)


def _bwd_in(x, dh1, du_pool, du_lru, du_gate, ln1_g, w_in_t, after):
    s = x.shape[0]
    tb = _token_block(s)

    def body(x_ref, dh1_ref, dup_ref, dul_ref, dug_ref, g_ref, w_ref, _after, dx_ref, dwb_ref, dg_ref, dw_ref):
        @pl.when(pl.program_id(0) == 0)
        def _():
            dw_ref[...] = jnp.zeros_like(dw_ref)
            dg_ref[...] = jnp.zeros_like(dg_ref)

        g1 = g_ref[...]
        n, xhat, rstd = _rms(x_ref[...], g1)
        nb = n.astype(BF16)
        dn = jnp.zeros((tb, D_MODEL), F32)
        for k, ref in enumerate((dup_ref, dul_ref, dug_ref)):
            rows = slice(k * 512, (k + 1) * 512)
            db = ref[...].astype(BF16)
            dw_ref[rows, :] += _mm_tn(db, nb)
            dn = dn + _mm(db, w_ref[rows, :])
        dx, dg1 = _rms_bwd(dn, xhat, rstd, g1)
        dx_ref[...] = dh1_ref[...] + dx
        dg_ref[...] += dg1

        @pl.when(pl.program_id(0) == s // tb - 1)
        def _():
            dwb_ref[...] = dw_ref[...].astype(BF16)

    row = pl.BlockSpec((tb, D_MODEL), lambda i: (i, 0))
    half = pl.BlockSpec((tb, 512), lambda i: (i, 0))
    vec = pl.BlockSpec((1, D_MODEL), lambda i: (0, 0))
    mat = pl.BlockSpec((IN_WIDTH, D_MODEL), lambda i: (0, 0))
    return pl.pallas_call(
        body, name="bwd_in", grid=(s // tb,),
        in_specs=[row, row, half, half, half, vec, mat, pl.BlockSpec(memory_space=pl.ANY)],
        out_specs=[row, mat, vec],
        out_shape=[pltpu.HBM((s, D_MODEL), F32), pltpu.HBM((IN_WIDTH, D_MODEL), BF16),
                   jax.ShapeDtypeStruct((1, D_MODEL), F32)],
        scratch_shapes=[pltpu.VMEM((IN_WIDTH, D_MODEL), F32)],
        compiler_params=_params(48, 1),
    )(*_in_hbm(x, dh1, du_pool, du_lru, du_gate, ln1_g, w_in_t), after)


def _mesh_position():
    x, y, c = lax.axis_index("x"), lax.axis_index("y"), lax.axis_index("c")
    return x, y, c, 4 * x + 2 * y + c


def _peer(x, y, c, p):
    px = 1 - x if p & 4 else x
    py = 1 - y if p & 2 else y
    pc = 1 - c if p & 1 else c
    return (px, py, pc), 4 * px + 2 * py + pc


HBM_SPEC = pl.BlockSpec(memory_space=pltpu.HBM)
SEM_SPEC = pl.BlockSpec(memory_space=pltpu.SEMAPHORE)
DATAFLOW = pltpu.SideEffectType.DATAFLOW_SIDE_EFFECTING


class Copy(NamedTuple):
    src: int
    src_at: Any
    dst: int
    dst_at: Any
    peer: int
    group: int
    slot: int


SIBLING = (1,)
SAME_CORE = (2, 4, 6)
EVERYONE = tuple(range(1, N_DEV))


def _same(index):
    return index


def _chip(index):
    return jnp.right_shift(index, 1)


def _fan_out(srcs, lands, peers, group):
    return [Copy(s, None, d, _same, p, group, N_DEV * i + p) for i, (s, d) in enumerate(zip(srcs, lands)) for p in peers]


def _scatter(stacks, lands, peers):
    return [Copy(s, lambda me, p=p: jnp.bitwise_xor(me, p), d, _same, p, 0, 0)
            for s, d in zip(stacks, lands) for p in peers]


def _numbered(copies, group=0):
    return [cp._replace(group=group, slot=i) for i, cp in enumerate(copies)]


def _relay(lands, peers):
    return [Copy(b, lambda s, q=q: jnp.bitwise_xor(s, q), b, lambda s, q=q: jnp.bitwise_xor(s, q), 1, 0, N_DEV * i + q)
            for i, b in enumerate(lands) for q in peers]


def _to_sibling(stacks, lands):
    return [Copy(s, lambda me, k=k: 2 * k + 1 - jnp.bitwise_and(me, 1), d, lambda me, k=k: k, 1, 0, 4 * i + k)
            for i, (s, d) in enumerate(zip(stacks, lands)) for k in range(N_DEV // 2)]


def _to_chips(sums, lands):
    return [Copy(s, lambda me, p=p: jnp.bitwise_xor(_chip(me), p // 2), d, _chip, p, 0, 4 * i + p // 2)
            for i, (s, d) in enumerate(zip(sums, lands)) for p in SAME_CORE]


def _comm_call(name, bufs, wait=None, start=None, after=()):
    nb = len(bufs)
    slots = list(start[1]) if start else []
    n_out_sem = 2 * len(slots)
    after = [a for a in (after if isinstance(after, (list, tuple)) else [after]) if a is not None]

    def body(*refs):
        b = refs[:nb]
        at = nb
        if wait:
            w_send, w_recv = refs[at], refs[at + 1]
            at += 2
        at += len(after)
        out_sems = refs[at:at + n_out_sem]
        token = refs[at + n_out_sem + nb]
        x, y, c, me = _mesh_position()

        def part(i, row_of, sender):
            return b[i] if row_of is None else b[i].at[row_of(sender)]

        if wait:
            for cp in wait[1]:
                peer, peer_index = _peer(x, y, c, cp.peer)
                arrival = pltpu.make_async_remote_copy(part(cp.src, cp.src_at, me), part(cp.dst, cp.dst_at, peer_index),
                                                       w_send.at[cp.slot], w_recv.at[cp.slot],
                                                       device_id=peer, device_id_type=MESH)
                arrival.wait_send()
                arrival.wait_recv()
        if start:
            for cp in start[0]:
                peer, _ = _peer(x, y, c, cp.peer)
                pltpu.make_async_remote_copy(part(cp.src, cp.src_at, me), part(cp.dst, cp.dst_at, me),
                                             out_sems[2 * cp.group].at[cp.slot], out_sems[2 * cp.group + 1].at[cp.slot],
                                             device_id=peer, device_id_type=MESH).start()
        token[...] = jnp.zeros_like(token)

    sem_shapes = []
    for n_slots in slots:
        sem_shapes += [pltpu.SemaphoreType.DMA((n_slots,))] * 2
    operands = [pltpu.with_memory_space_constraint(a, pltpu.HBM) for a in bufs]
    in_specs = [HBM_SPEC] * nb
    if wait:
        operands += list(wait[0])
        in_specs += [SEM_SPEC, SEM_SPEC]
    operands += after
    in_specs += [pl.BlockSpec(memory_space=pl.ANY)] * len(after)
    outs = pl.pallas_call(
        body, name=name, in_specs=in_specs,
        out_specs=[SEM_SPEC] * n_out_sem + [HBM_SPEC] * nb + [pl.BlockSpec(memory_space=pltpu.VMEM)],
        out_shape=sem_shapes + [pltpu.HBM(a.shape, a.dtype) for a in bufs] + [jax.ShapeDtypeStruct((8, SLAB), F32)],
        input_output_aliases={i: n_out_sem + i for i in range(nb)},
        compiler_params=pltpu.CompilerParams(has_side_effects=DATAFLOW),
    )(*operands)
    sems = [(outs[2 * k], outs[2 * k + 1]) for k in range(len(slots))]
    return sems, list(outs[n_out_sem:n_out_sem + nb]), outs[-1]


def _pair_sum(stacks, lands, place):
    n = len(stacks)

    def body(place_ref, *refs):
        k = pl.program_id(0)
        for m in range(n):
            mine, theirs, out, land = refs[m], refs[n + m], refs[2 * n + m], refs[3 * n + m]
            total = (mine[0, 0].astype(F32) + theirs[0].astype(F32)).astype(out.dtype)
            out[0] = total

            @pl.when(k == place_ref[1])
            def _():
                land[0] = total

    in_specs = [pl.BlockSpec((1, 1) + a.shape[1:], lambda k, place_ref: (k, place_ref[0], 0, 0)) for a in stacks]
    in_specs += [pl.BlockSpec((1,) + a.shape[1:], lambda k, place_ref: (k, 0, 0)) for a in lands]
    out_specs = [pl.BlockSpec((1,) + a.shape[1:], lambda k, place_ref: (k, 0, 0)) for a in lands]
    out_specs += [pl.BlockSpec((1,) + a.shape[1:], lambda k, place_ref: (place_ref[1], 0, 0)) for a in lands]
    outs = pl.pallas_call(
        body, name="pair_sum_" + "_".join(str(a.shape[1]) for a in stacks),
        grid_spec=pltpu.PrefetchScalarGridSpec(num_scalar_prefetch=1, grid=(N_DEV // 2,), in_specs=in_specs,
                                               out_specs=out_specs),
        out_shape=[pltpu.HBM(a.shape, a.dtype) for a in lands] * 2,
        compiler_params=_params(40, 1),
    )(place, *_in_hbm(*[a.reshape((N_DEV // 2, 2) + a.shape[1:]) for a in stacks], *lands))
    return list(outs[:n]), list(outs[n:])


def _reduce_adam(parts, w, m, v, name):
    rows, cols = w.shape
    n_parts = parts.shape[0]
    rb = rows
    for cand in (256, 176, 128):
        if rows % cand == 0 and rows > cand:
            rb = cand
            break

    def body(p_ref, w_ref, m_ref, v_ref, g_out, d_out, m_out, v_out):
        g = p_ref[0].astype(F32)
        for j in range(1, n_parts):
            g = g + p_ref[j].astype(F32)
        g_out[...] = g
        d_out[...], m_out[...], v_out[...] = _adam(g, w_ref[...], m_ref[...], v_ref[...])

    blk = pl.BlockSpec((rb, cols), lambda i: (i, 0))
    out = jax.ShapeDtypeStruct((rows, cols), F32)
    return pl.pallas_call(
        body, name=name, grid=(rows // rb,),
        in_specs=[pl.BlockSpec((n_parts, rb, cols), lambda i: (0, i, 0)), blk, blk, blk],
        out_specs=[blk] * 4, out_shape=[out] * 4,
        compiler_params=_params(40, 1),
    )(*_in_hbm(parts, w, m, v))


def _cols_from_stack(stack):
    n, r, c = stack.shape
    return jnp.transpose(stack, (1, 0, 2)).reshape(r, n * c)


def _block_diag(w):
    z = jnp.zeros((N_SLAB, 64, 64), w.dtype)
    pairs = w.reshape(N_SLAB, 2, 64, 64)
    top = jnp.concatenate([pairs[:, 0], z], axis=2)
    bottom = jnp.concatenate([z, pairs[:, 1]], axis=2)
    return jnp.concatenate([top, bottom], axis=1)


def _adam(g, w, m, v):
    m_new = ADAM_B1 * m + (1.0 - ADAM_B1) * g
    v_new = ADAM_B2 * v + (1.0 - ADAM_B2) * (g * g)
    m_hat = m_new / (1.0 - ADAM_B1 ** ADAM_STEP)
    v_hat = v_new / (1.0 - ADAM_B2 ** ADAM_STEP)
    return (-ADAM_LR) * (m_hat / (jnp.sqrt(v_hat) + ADAM_EPS) + ADAM_WD * w), m_new, v_new


WIDE = ("ln2_g", "lnf_g")
HALF = ("pool_scale", "conv_b", "b_a", "b_i", "lam", "gn_pool_g", "gn_lru_g")
VECTORS = [(k, D_MODEL) for k in WIDE] + [(k, 512) for k in HALF]
VECTOR_ROWS = sum(width // SLAB for _, width in VECTORS)
LOSS_ROW = -(-VECTOR_ROWS // 8) * 8
CONV_AT = LOSS_ROW + 8
CONV_LANES = LRU_WIDTH // SLAB
PACK_F_ROWS = CONV_AT + CONV_WIDTH * CONV_LANES
MATRIX_ROWS = N_SLAB * SLAB
HEAD = SLAB // 2
GATE_ROWS = N_SLAB * HEAD
PACK_B_ROWS = MATRIX_ROWS + 2 * GATE_ROWS


def _pack_small(vectors, pool_g, wa_g, wi_g, conv, sq):
    n_vec = len(vectors)

    def body(*refs):
        vec = refs[:n_vec]
        pw_ref, wa_ref, wi_ref, cw_ref, sq_ref, out, out_b = refs[n_vec:]
        out[...] = jnp.zeros_like(out)
        row = 0
        for ref, (_, width) in zip(vec, VECTORS):
            for k in range(width // SLAB):
                out[row:row + 1, :] = ref[:, k * SLAB:(k + 1) * SLAB]
                row += 1
        for tap in range(CONV_WIDTH):
            for k in range(CONV_LANES):
                at = CONV_AT + tap * CONV_LANES + k
                out[at:at + 1, :] = cw_ref[tap:tap + 1, k * SLAB:(k + 1) * SLAB]
        total = sq_ref[:, 0:SLAB]
        for k in range(1, D_MODEL // SLAB):
            total = total + sq_ref[:, k * SLAB:(k + 1) * SLAB]
        out[LOSS_ROW:LOSS_ROW + 1, :] = total
        left = lax.broadcasted_iota(jnp.int32, (HEAD, SLAB), 1) < HEAD
        for s in range(N_SLAB):
            out_b[s * SLAB:(s + 1) * SLAB, :] = pw_ref[s].astype(BF16)
            for i, ref in enumerate((wa_ref, wi_ref)):
                at = MATRIX_ROWS + i * GATE_ROWS + s * HEAD
                out_b[at:at + HEAD, :] = jnp.where(left, ref[s, 0:HEAD, :], ref[s, HEAD:SLAB, :]).astype(BF16)

    return pl.pallas_call(
        body, name="pack_small",
        out_shape=[jax.ShapeDtypeStruct((PACK_F_ROWS, SLAB), F32), jax.ShapeDtypeStruct((PACK_B_ROWS, SLAB), BF16)],
    )(*vectors, pool_g, wa_g, wi_g, conv, sq)


def _small_reduce_adam(parts, parts_b, vec_w, vec_m, vec_v, pool_wmv):
    n_vec = len(VECTORS)
    n_parts = parts.shape[0]

    def body(*refs):
        p_ref, pb_ref = refs[0], refs[1]
        w_refs, m_refs, v_refs = (refs[2 + k * n_vec:2 + (k + 1) * n_vec] for k in range(3))
        pw_w, pw_m, pw_v = refs[2 + 3 * n_vec:5 + 3 * n_vec]
        outs = refs[5 + 3 * n_vec:-1]
        total = refs[-1]
        total[...] = p_ref[0]
        for j in range(1, n_parts):
            total[...] += p_ref[j]
        row = 0
        for i, (_, width) in enumerate(VECTORS):
            n_rows = width // SLAB
            g = jnp.concatenate([total[row + k:row + k + 1, :] for k in range(n_rows)], axis=1)
            row += n_rows
            d, m_new, v_new = _adam(g, w_refs[i][...], m_refs[i][...], v_refs[i][...])
            for ref, val in zip(outs[4 * i:4 * i + 4], (g, d, m_new, v_new)):
                ref[...] = val
        tail = outs[4 * n_vec:]

        def summed(first, count):
            g = pb_ref[0, first:first + count, :].astype(F32)
            for j in range(1, n_parts):
                g = g + pb_ref[j, first:first + count, :].astype(F32)
            return g

        g = summed(0, MATRIX_ROWS)
        d, m_new, v_new = _adam(g, pw_w[...], pw_m[...], pw_v[...])
        for ref, val in zip(tail[0:4], (g, d, m_new, v_new)):
            ref[...] = val
        tail[4][...] = summed(MATRIX_ROWS, GATE_ROWS)
        tail[5][...] = summed(MATRIX_ROWS + GATE_ROWS, GATE_ROWS)
        for tap in range(CONV_WIDTH):
            at = CONV_AT + tap * CONV_LANES
            tail[6][tap:tap + 1, :] = jnp.concatenate([total[at + k:at + k + 1, :] for k in range(CONV_LANES)], axis=1)
        tail[7][...] = (0.5 / D_MODEL) * jnp.sum(total[LOSS_ROW:LOSS_ROW + 1, :], axis=1, keepdims=True)

    out_shape = []
    for _, width in VECTORS:
        out_shape += [jax.ShapeDtypeStruct((1, width), F32)] * 4
    out_shape += [jax.ShapeDtypeStruct((MATRIX_ROWS, SLAB), F32)] * 4 + [jax.ShapeDtypeStruct((GATE_ROWS, SLAB), F32)] * 2
    out_shape += [jax.ShapeDtypeStruct((CONV_WIDTH, LRU_WIDTH), F32), jax.ShapeDtypeStruct((1, 1), F32)]
    outs = pl.pallas_call(
        body, name="adam_small", out_shape=out_shape,
        scratch_shapes=[pltpu.VMEM((PACK_F_ROWS, SLAB), F32)],
        compiler_params=_params(40),
    )(parts, parts_b, *vec_w, *vec_m, *vec_v, *pool_wmv)
    vec_out = [tuple(outs[4 * i:4 * i + 4]) for i in range(n_vec)]
    tail = outs[4 * n_vec:]
    return vec_out, tuple(tail[0:4]), tail[4], tail[5], tail[6], tail[7]


def _plain_adam(grads, ws, ms, vs):
    n = len(grads)

    def body(*refs):
        ins, outs = refs[:4 * n], refs[4 * n:]
        for i in range(n):
            d, m_new, v_new = _adam(ins[i][...], ins[n + i][...], ins[2 * n + i][...], ins[3 * n + i][...])
            for ref, val in zip(outs[3 * i:3 * i + 3], (d, m_new, v_new)):
                ref[...] = val

    out_shape = []
    for g in grads:
        out_shape += [jax.ShapeDtypeStruct(g.shape, F32)] * 3
    outs = pl.pallas_call(body, name="adam_plain", out_shape=out_shape)(*grads, *ws, *ms, *vs)
    return [tuple(outs[3 * i:3 * i + 3]) for i in range(n)]


def kernel(x, ln1_g, w_in, pool_w, pool_scale, conv_w, conv_b, w_a, b_a, w_i, b_i, lam, gn_pool_g, gn_lru_g, w_out, ln2_g, w_ffn_gate, w_ffn_up, w_ffn_down, lnf_g, loss_target, m_ln1_g, m_w_in, m_pool_w, m_pool_scale, m_conv_w, m_conv_b, m_w_a, m_b_a, m_w_i, m_b_i, m_lam, m_gn_pool_g, m_gn_lru_g, m_w_out, m_ln2_g, m_w_ffn_gate, m_w_ffn_up, m_w_ffn_down, m_lnf_g, v_ln1_g, v_w_in, v_pool_w, v_pool_scale, v_conv_w, v_conv_b, v_w_a, v_b_a, v_w_i, v_b_i, v_lam, v_gn_pool_g, v_gn_lru_g, v_w_out, v_ln2_g, v_w_ffn_gate, v_w_ffn_up, v_w_ffn_down, v_lnf_g):
    weights = dict(ln1_g=ln1_g, w_in=w_in, pool_w=pool_w, pool_scale=pool_scale, conv_w=conv_w, conv_b=conv_b,
                   w_a=w_a, b_a=b_a, w_i=w_i, b_i=b_i, lam=lam, gn_pool_g=gn_pool_g, gn_lru_g=gn_lru_g,
                   w_out=w_out, ln2_g=ln2_g, w_ffn_gate=w_ffn_gate, w_ffn_up=w_ffn_up, w_ffn_down=w_ffn_down,
                   lnf_g=lnf_g)
    mom1 = dict(ln1_g=m_ln1_g, w_in=m_w_in, pool_w=m_pool_w, pool_scale=m_pool_scale, conv_w=m_conv_w,
                conv_b=m_conv_b, w_a=m_w_a, b_a=m_b_a, w_i=m_w_i, b_i=m_b_i, lam=m_lam, gn_pool_g=m_gn_pool_g,
                gn_lru_g=m_gn_lru_g, w_out=m_w_out, ln2_g=m_ln2_g, w_ffn_gate=m_w_ffn_gate,
                w_ffn_up=m_w_ffn_up, w_ffn_down=m_w_ffn_down, lnf_g=m_lnf_g)
    mom2 = dict(ln1_g=v_ln1_g, w_in=v_w_in, pool_w=v_pool_w, pool_scale=v_pool_scale, conv_w=v_conv_w,
                conv_b=v_conv_b, w_a=v_w_a, b_a=v_b_a, w_i=v_w_i, b_i=v_b_i, lam=v_lam, gn_pool_g=v_gn_pool_g,
                gn_lru_g=v_gn_lru_g, w_out=v_w_out, ln2_g=v_ln2_g, w_ffn_gate=v_w_ffn_gate,
                w_ffn_up=v_w_ffn_up, w_ffn_down=v_w_ffn_down, lnf_g=v_lnf_g)

    xs = x[0]
    target = loss_target[0]

    shard = dict(w_in=lambda a: a[0].T, w_ffn_gate=lambda a: a[0].T, w_ffn_up=lambda a: a[0].T,
                 w_out=lambda a: a[0], w_ffn_down=lambda a: a[0], conv_w=lambda a: a[0])
    unshard = dict(w_in=lambda a: a.T[None], w_ffn_gate=lambda a: a.T[None], w_ffn_up=lambda a: a.T[None],
                   w_out=lambda a: a[None], w_ffn_down=lambda a: a[None], conv_w=lambda a: a[None])

    gathered = ("w_in", "conv_w", "w_out", "w_ffn_gate", "w_ffn_up", "w_ffn_down")
    groups = ((0, 1), (2,), (3, 4, 5))
    sources = [shard[k](weights[k]) if k == "conv_w" else shard[k](weights[k]).astype(BF16) for k in gathered]
    my_index = 4 * lax.axis_index("x") + 2 * lax.axis_index("y") + lax.axis_index("c")
    lands = [lax.dynamic_update_index_in_dim(lax.empty((N_DEV,) + a.shape, a.dtype), a, my_index, 0) for a in sources]

    def first_hop(n, group=0):
        return _numbered(_fan_out(range(n), range(n, 2 * n), SAME_CORE + SIBLING, 0), group)

    start = []
    for g, members in enumerate(groups):
        start += _numbered(_fan_out(members, [6 + m for m in members], SAME_CORE + SIBLING, 0), g)
    sems, bufs, _ = _comm_call("gather_start", sources + lands,
                               start=(start, [sum(cp.group == g for cp in start) for g in range(len(groups))]))
    sources, lands = bufs[:6], bufs[6:]

    def gathered_group(tag, g, after):
        members = groups[g]
        n = len(members)
        relay = _numbered(_relay(range(n, 2 * n), SAME_CORE))
        relay_sems, bufs, _ = _comm_call("gather_relay_" + tag, [sources[m] for m in members] + [lands[m] for m in members],
                                         wait=(sems[g], first_hop(n)), start=(relay, (len(relay),)), after=after)
        _, bufs, _ = _comm_call("gather_wait_" + tag, bufs[n:], wait=(relay_sems[0], _numbered(_relay(range(n), SAME_CORE))))
        return bufs

    g_in, g_conv = gathered_group("in", 0, None)
    w_in_f = g_in.reshape(IN_WIDTH, D_MODEL)
    conv_w_f = _cols_from_stack(g_conv)

    wa_bd = _block_diag(w_a[0])
    wi_bd = _block_diag(w_i[0])
    lnf_row = lnf_g.reshape(1, D_MODEL)

    u_pool, u_lru, u_gate = _fwd_in(xs, ln1_g, w_in_f)
    y_pool, h, y_lru = _mixer_fwd(u_pool, u_lru, u_gate, pool_w[0], pool_scale, conv_w_f, conv_b,
                                  wa_bd, b_a, wi_bd, b_i, lam)
    (g_out,) = gathered_group("out", 1, y_pool)
    w_out_f = g_out.reshape(D_MODEL, D_MODEL)
    h1, n2 = _fwd_out(xs, y_pool, y_lru, gn_pool_g, gn_lru_g, w_out_f, ln2_g)
    g_gate, g_up, g_down = gathered_group("ffn", 2, n2)
    w_gate_f = g_gate.reshape(D_FF, D_MODEL)
    w_up_f = g_up.reshape(D_FF, D_MODEL)
    w_down_f = g_down.reshape(D_FF, D_MODEL)
    g_act, u_act, dh2, dh2b, d_lnf, sq = _ffn_fwd(h1, n2, target, lnf_row, w_gate_f, w_up_f, w_down_f)

    place = jnp.stack([lax.axis_index("c"), 2 * lax.axis_index("x") + lax.axis_index("y")]).astype(jnp.int32)

    d_gate, d_up, d_down, dn2 = _ffn_bwd(n2, dh2b, g_act, u_act, w_gate_f, w_up_f, w_down_f)
    ffn_stacks = [d.reshape(N_DEV, D_FF // N_DEV, D_MODEL) for d in (d_gate, d_up, d_down)]
    pair_lands = [lax.empty((N_DEV // 2,) + a.shape[1:], a.dtype) for a in ffn_stacks]
    pair_copies = _numbered(_to_sibling(range(3), range(3, 6)))
    sem, bufs, token = _comm_call("ffn_pair_start", ffn_stacks + pair_lands, start=(pair_copies, (len(pair_copies),)))
    dh1, dy_pool, dy_lru, d_out, d_ln2, d_gnp, d_gnl = _bwd_out(dn2, dh2, h1, y_pool, y_lru, gn_pool_g, gn_lru_g,
                                                                 w_out_f, ln2_g, token)
    _, bufs, _ = _comm_call("ffn_pair_wait", bufs, wait=(sem[0], pair_copies), after=dh1)
    ffn_sums, ffn_lands = _pair_sum(bufs[:3], bufs[3:], place)
    ffn_copies = _numbered(_to_chips(range(3), range(3, 6)))
    ffn_sem, ffn_bufs, token = _comm_call("ffn_chip_start", ffn_sums + ffn_lands, start=(ffn_copies, (len(ffn_copies),)))
    (du_pool, du_lru, du_gate, d_pw, d_ps, d_cw, d_cb, d_wa, d_ba, d_wi, d_bi, d_lam) = _mixer_bwd(
        u_pool, u_lru, u_gate, h, dy_pool, dy_lru, pool_w[0], pool_scale, conv_w_f, conv_b,
        wa_bd, b_a, wi_bd, b_i, lam, token)

    def direct(tag, stacks, wholes, after):
        sources = list(stacks) + list(wholes)
        n, n_st = len(sources), len(stacks)
        lands = [lax.dynamic_update_index_in_dim(
            lax.empty(a.shape if i < n_st else (N_DEV,) + a.shape, a.dtype),
            lax.dynamic_index_in_dim(a, my_index, 0, keepdims=False) if i < n_st else a, my_index, 0)
            for i, a in enumerate(sources)]
        copies = _numbered(_scatter(range(n_st), range(n, n + n_st), EVERYONE)
                           + _fan_out(range(n_st, n), range(n + n_st, 2 * n), EVERYONE, 0))
        sem, bufs, token = _comm_call(tag + "_start", sources + lands, start=(copies, (len(copies),)), after=after)
        return (tag, sem[0], bufs, copies), token

    def direct_finish(started, after):
        tag, sem, bufs, copies = started
        _, bufs, _ = _comm_call(tag + "_wait", bufs, wait=(sem, copies), after=after)
        return bufs[len(bufs) // 2:]

    vec_grads = dict(ln2_g=d_ln2, lnf_g=d_lnf, pool_scale=d_ps, conv_b=d_cb, b_a=d_ba, b_i=d_bi,
                     lam=d_lam, gn_pool_g=d_gnp, gn_lru_g=d_gnl)
    packed, packed_b = _pack_small([vec_grads[k] for k, _ in VECTORS], d_pw, d_wa, d_wi, d_cw, sq)
    small_started, token = direct("small", [d_out.reshape(N_DEV, D_MODEL // N_DEV, D_MODEL)], [packed, packed_b], None)
    grad_x, d_in, d_ln1 = _bwd_in(xs, dh1, du_pool, du_lru, du_gate, ln1_g, w_in_f, token)
    in_started, token = direct("in", [d_in.reshape(N_DEV, IN_WIDTH // N_DEV, D_MODEL)], [d_ln1], None)

    results = {}

    def reduce_adam(name, parts):
        outs = _reduce_adam(parts, shard[name](weights[name]), shard[name](mom1[name]), shard[name](mom2[name]),
                            "adam_" + name)
        results[name] = tuple(unshard[name](o) for o in outs)
        return outs[0]

    shard["ln1_g"] = unshard["ln1_g"] = lambda a: a
    _, ffn_bufs, _ = _comm_call("ffn_chip_wait", ffn_bufs, wait=(ffn_sem[0], ffn_copies), after=token)
    done = [reduce_adam(name, parts)
            for name, parts in zip(("w_ffn_gate", "w_ffn_up", "w_ffn_down"), ffn_bufs[3:])]
    r_out, r_small, r_small_b = direct_finish(small_started, done)
    done.append(reduce_adam("w_out", r_out))

    def as_row(a, width):
        return a.reshape(1, width)

    def as_matrix(a):
        return a.reshape(MATRIX_ROWS, SLAB)

    def as_heads(a):
        return a.reshape(2 * GATE_ROWS, HEAD)

    def heads_apart(g):
        return jnp.transpose(g.reshape(N_SLAB, HEAD, 2, HEAD), (0, 2, 1, 3)).reshape(2 * GATE_ROWS, HEAD)

    vec_out, pool_out, g_wa, g_wi, g_conv, loss_11 = _small_reduce_adam(
        r_small, r_small_b, [as_row(weights[k], w) for k, w in VECTORS], [as_row(mom1[k], w) for k, w in VECTORS],
        [as_row(mom2[k], w) for k, w in VECTORS], [as_matrix(t["pool_w"]) for t in (weights, mom1, mom2)])
    for (k, _), outs in zip(VECTORS, vec_out):
        results[k] = tuple(o.reshape(weights[k].shape) for o in outs)
    results["pool_w"] = tuple(o.reshape(pool_w.shape) for o in pool_out)
    my_columns = conv_w.shape[-1]
    plain_names = ("w_a", "w_i", "conv_w")
    plain_grads = [heads_apart(g) for g in (g_wa, g_wi)]
    plain_grads.append(lax.dynamic_slice_in_dim(g_conv, my_index * my_columns, my_columns, axis=1))
    views = (as_heads, as_heads, lambda a: a[0])
    plain_out = _plain_adam(plain_grads, *[[view(t[k]) for k, view in zip(plain_names, views)]
                                           for t in (weights, mom1, mom2)])
    for k, g, outs in zip(plain_names, plain_grads, plain_out):
        results[k] = tuple(o.reshape(weights[k].shape) for o in (g,) + outs)
    loss = loss_11[0, 0]
    r_in, r_ln1 = direct_finish(in_started, done + [plain_out[0][0], loss_11])
    reduce_adam("w_in", r_in)
    reduce_adam("ln1_g", r_ln1)

    order = ["ln1_g", "w_in", "pool_w", "pool_scale", "conv_w", "conv_b", "w_a", "b_a", "w_i", "b_i", "lam",
             "gn_pool_g", "gn_lru_g", "w_out", "ln2_g", "w_ffn_gate", "w_ffn_up", "w_ffn_down", "lnf_g"]
    return (loss, grad_x[None],
            *[results[k][0] for k in order], *[results[k][1] for k in order],
            *[results[k][2] for k in order], *[results[k][3] for k in order])
```

```python
from typing import Any, NamedTuple

import jax
import jax.numpy as jnp
from jax import lax
from jax.experimental import pallas as pl
from jax.experimental.pallas import tpu as pltpu

F32 = jnp.float32
BF16 = jnp.bfloat16

N_DEV = 8
D_MODEL = 1024
POOL_WIDTH = 512
LRU_WIDTH = 512
IN_WIDTH = 1536
D_FF = 2816
N_SLAB = 4
SLAB = 128
CONV_WIDTH = 4
LRU_C = 8.0
EPS = 1e-6
HALO = 16
FF_CHUNK = 256

ADAM_LR = 0.001
ADAM_B1 = 0.9
ADAM_B2 = 0.999
ADAM_EPS = 1e-08
ADAM_WD = 0.01
ADAM_STEP = 10

MIB = 1 << 20
MESH = pl.DeviceIdType.MESH


def _params(vmem_mib, n_axes=0):
    sem = ("arbitrary",) * n_axes if n_axes else None
    return pltpu.CompilerParams(dimension_semantics=sem, vmem_limit_bytes=vmem_mib * MIB)


def _in_hbm(*arrays):
    return [pltpu.with_memory_space_constraint(a, pltpu.HBM) for a in arrays]


def _mm(a, b):
    return jnp.dot(a, b, preferred_element_type=F32)


def _mm_nt(a, b):
    return lax.dot_general(a, b, (((1,), (1,)), ((), ())), preferred_element_type=F32)


def _mm_tn(a, b):
    return lax.dot_general(a, b, (((0,), (0,)), ((), ())), preferred_element_type=F32)


def _rms(x, g):
    rstd = lax.rsqrt(jnp.mean(x * x, axis=-1, keepdims=True) + EPS)
    xhat = x * rstd
    return xhat * g, xhat, rstd


def _rms_bwd(dy, xhat, rstd, g):
    gy = dy * g
    dx = rstd * (gy - xhat * jnp.mean(gy * xhat, axis=-1, keepdims=True))
    return dx, jnp.sum(dy * xhat, axis=0, keepdims=True)


def _gelu(z):
    t = jnp.tanh(0.7978845608028654 * (z + 0.044715 * z * z * z))
    return 0.5 * z * (1.0 + t), t


def _gelu_grad(z, t):
    return 0.5 * (1.0 + t) + 0.5 * z * (1.0 - t * t) * 0.7978845608028654 * (1.0 + 3.0 * 0.044715 * z * z)


def _softplus_neg(lam):
    x = -lam
    e = jnp.exp(-jnp.abs(x))
    u = 1.0 + e
    l1p = jnp.where(u == 1.0, e, jnp.log(u) * e / (u - 1.0))
    return jnp.maximum(x, 0.0) + l1p


def _one_minus_square(la, a):
    x = 2.0 * la
    series = -x * (1.0 + x * (0.5 + x * (1.0 / 6.0 + x * (1.0 / 24.0))))
    return jnp.where(x > -0.06, series, 1.0 - a * a)


def _sigmoid(x):
    return 0.5 * jnp.tanh(0.5 * x) + 0.5


def _down(v, d):
    return pltpu.roll(v, d, 0)


def _up(v, d):
    return pltpu.roll(v, v.shape[0] - d, 0)


def _token_block(s, most=512):
    for rows in (most, 512, 256):
        if rows <= most and s % rows == 0 and s > rows:
            return rows
    return s


def _time_chunk(s):
    return 512 if s % 512 == 0 and s > 512 else 256 if s % 256 == 0 else s


def _fwd_in(x, ln1_g, w_in_t):
    s = x.shape[0]
    tb = _token_block(s, 1024)

    def body(x_ref, g_ref, w_ref, up_ref, ul_ref, ug_ref):
        n, _, _ = _rms(x_ref[...], g_ref[...])
        proj = _mm_nt(n.astype(BF16), w_ref[...])
        up_ref[...] = proj[:, :POOL_WIDTH]
        ul_ref[...] = proj[:, POOL_WIDTH:POOL_WIDTH + LRU_WIDTH]
        ug_ref[...] = proj[:, POOL_WIDTH + LRU_WIDTH:]

    out = pltpu.HBM((s, 512), F32)
    return pl.pallas_call(
        body, name="fwd_in", grid=(s // tb,),
        in_specs=[pl.BlockSpec((tb, D_MODEL), lambda i: (i, 0)),
                  pl.BlockSpec((1, D_MODEL), lambda i: (0, 0)),
                  pl.BlockSpec((IN_WIDTH, D_MODEL), lambda i: (0, 0))],
        out_specs=[pl.BlockSpec((tb, 512), lambda i: (i, 0))] * 3,
        out_shape=[out, out, out],
        compiler_params=_params(40, 1),
    )(*_in_hbm(x, ln1_g, w_in_t))


def _pool_denominator(t0, row, window):
    return jnp.minimum((t0 + row + 1).astype(F32), window)


def _causal_window(ext, deeper):
    s = ext + _down(ext, 1)
    s = s + deeper[0] * _down(s, 2)
    s = s + deeper[1] * _down(s, 4)
    s = s + deeper[2] * _down(s, 8)
    return s[HALO:]


def _anticausal_window(ext, deeper, rows):
    s = ext + _up(ext, 1)
    s = s + deeper[0] * _up(s, 2)
    s = s + deeper[1] * _up(s, 4)
    s = s + deeper[2] * _up(s, 8)
    return s[:rows]


def _conv_taps(ext):
    return [ext[HALO:], _down(ext, 1)[HALO:], _down(ext, 2)[HALO:], _down(ext, 3)[HALO:]]


def _conv(taps, cw, cb):
    return cw[3] * taps[0] + cw[2] * taps[1] + cw[1] * taps[2] + cw[0] * taps[3] + cb


def _lru_gates(xc, wa, ba, wi, bi, sp):
    xb = xc.astype(BF16)
    r = _sigmoid(_mm(xb, wa) + ba)
    i = _sigmoid(_mm(xb, wi) + bi)
    la = (-LRU_C) * r * sp
    a = jnp.exp(la)
    mult = jnp.sqrt(jnp.maximum(_one_minus_square(la, a), 0.0))
    return xb, r, i, a, mult


SUBLANES = 8


def _scan_pads(a_pad, b_pad, rows, causal):
    spare = slice(0, SUBLANES) if causal else slice(rows, rows + SUBLANES)
    a_pad[spare, :] = jnp.ones((SUBLANES, SLAB), F32)
    b_pad[spare, :] = jnp.zeros((SUBLANES, SLAB), F32)


def _scan_causal(a, b, h_prev, a_pad, b_pad, rows):
    d = 1
    while d < min(SUBLANES, rows):
        a_pad[SUBLANES:, :] = a
        b_pad[SUBLANES:, :] = b
        b = a * b_pad[SUBLANES - d:SUBLANES - d + rows, :] + b
        a = a * a_pad[SUBLANES - d:SUBLANES - d + rows, :]
        d *= 2
    while d < rows:
        b = jnp.concatenate([b[:d], a[d:] * b[:-d] + b[d:]], axis=0)
        a = jnp.concatenate([a[:d], a[d:] * a[:-d]], axis=0)
        d *= 2
    return b + a * h_prev


def _scan_anticausal(a, b, l_next, a_pad, b_pad, rows):
    d = 1
    while d < min(SUBLANES, rows):
        a_pad[:rows, :] = a
        b_pad[:rows, :] = b
        b = a * b_pad[d:d + rows, :] + b
        a = a * a_pad[d:d + rows, :]
        d *= 2
    while d < rows:
        b = jnp.concatenate([a[:-d] * b[d:] + b[:-d], b[-d:]], axis=0)
        a = jnp.concatenate([a[:-d] * a[d:], a[-d:]], axis=0)
        d *= 2
    return b + a * l_next


def _slab_scalars():
    slab = pl.program_id(0)
    deeper = [jnp.where(slab > k, 1.0, 0.0).astype(F32) for k in range(N_SLAB - 1)]
    window = jnp.left_shift(jnp.int32(2), slab).astype(F32)
    inverse = jnp.where(slab == 0, 0.5, jnp.where(slab == 1, 0.25, jnp.where(slab == 2, 0.125, 0.0625))).astype(F32)
    return deeper, (window, inverse)


def _window_mean(total, t0, row, window, at_start):
    if at_start:
        return total / _pool_denominator(t0, row, window[0])
    return total * window[1]


def _slab_specs(s):
    seq = pl.BlockSpec((s, SLAB), lambda k: (0, k))
    mat = pl.BlockSpec((1, SLAB, SLAB), lambda k: (k, 0, 0))
    vec = pl.BlockSpec((1, SLAB), lambda k: (0, k))
    taps = pl.BlockSpec((CONV_WIDTH, SLAB), lambda k: (0, k))
    return seq, mat, vec, taps


def _mixer_fwd(u_pool, u_lru, u_gate, pool_w, pool_scale, conv_w, conv_b, wa_bd, b_a, wi_bd, b_i, lam):
    s = u_pool.shape[0]
    tc = _time_chunk(s)
    n_chunks = s // tc

    def body(up_ref, ul_ref, ug_ref, pw_ref, ps_ref, cw_ref, cb_ref, wa_ref, ba_ref, wi_ref, bi_ref, lam_ref,
             yp_ref, h_ref, yl_ref, shift_ref, a_pad, b_pad):
        _scan_pads(a_pad, b_pad, tc, causal=True)
        deeper, window = _slab_scalars()
        pw = pw_ref[0].astype(BF16)
        wa = wa_ref[0].astype(BF16)
        wi = wi_ref[0].astype(BF16)
        ps, cb, ba, bi = ps_ref[...], cb_ref[...], ba_ref[...], bi_ref[...]
        cw = [cw_ref[k:k + 1, :] for k in range(CONV_WIDTH)]
        sp = _softplus_neg(lam_ref[...])
        row = lax.broadcasted_iota(jnp.int32, (tc, SLAB), 0)

        def chunk(t0, ext_p, ext_l, h_prev, at_start=False):
            rows = pl.ds(t0, tc)
            d = _window_mean(_causal_window(ext_p, deeper), t0, row, window, at_start) - ext_p[HALO:]
            yp_ref[rows, :] = _mm(d.astype(BF16), pw) * ps
            shift_ref[...] = ext_l
            xc = _conv([shift_ref[HALO - k:HALO - k + tc, :] for k in range(CONV_WIDTH)], cw, cb)
            _, _, i, a, mult = _lru_gates(xc, wa, ba, wi, bi, sp)
            h = _scan_causal(a, mult * (i * xc), h_prev, a_pad, b_pad, tc)
            h_ref[rows, :] = h
            yl_ref[rows, :] = h * _gelu(ug_ref[rows, :])[0]
            return h[tc - 1:tc, :]

        pad = jnp.zeros((HALO, SLAB), F32)
        h0 = chunk(0, jnp.concatenate([pad, up_ref[pl.ds(0, tc), :]], axis=0),
                   jnp.concatenate([pad, ul_ref[pl.ds(0, tc), :]], axis=0), jnp.zeros((1, SLAB), F32), at_start=True)

        def step(c, h_prev):
            t0 = pl.multiple_of(c * tc, tc)
            ext = pl.ds(pl.multiple_of(c * tc - HALO, HALO), tc + HALO)
            return chunk(t0, up_ref[ext, :], ul_ref[ext, :], h_prev)

        lax.fori_loop(1, n_chunks, step, h0)

    seq, mat, vec, taps = _slab_specs(s)
    out = pltpu.HBM((s, 512), F32)
    return pl.pallas_call(
        body, name="mixer_fwd", grid=(N_SLAB,),
        in_specs=[seq, seq, seq, mat, vec, taps, vec, mat, vec, mat, vec, vec],
        out_specs=[seq, seq, seq], out_shape=[out, out, out],
        scratch_shapes=[pltpu.VMEM((tc + HALO, SLAB), F32), pltpu.VMEM((tc + SUBLANES, SLAB), F32),
                        pltpu.VMEM((tc + SUBLANES, SLAB), F32)],
        compiler_params=_params(48, 1),
    )(*_in_hbm(u_pool, u_lru, u_gate, pool_w, pool_scale, conv_w, conv_b, wa_bd, b_a, wi_bd, b_i, lam))


def _fwd_out(x, y_pool, y_lru, gn_pool_g, gn_lru_g, w_out, ln2_g):
    s = x.shape[0]
    tb = _token_block(s, 1024)

    def body(x_ref, yp_ref, yl_ref, gp_ref, gl_ref, w_ref, g2_ref, h1_ref, n2_ref):
        mp, _, _ = _rms(yp_ref[...], gp_ref[...])
        ml, _, _ = _rms(yl_ref[...], gl_ref[...])
        h1 = x_ref[...] + _mm(mp.astype(BF16), w_ref[:POOL_WIDTH, :]) + _mm(ml.astype(BF16), w_ref[POOL_WIDTH:, :])
        h1_ref[...] = h1
        n2_ref[...] = _rms(h1, g2_ref[...])[0].astype(BF16)

    row = pl.BlockSpec((tb, D_MODEL), lambda i: (i, 0))
    half = pl.BlockSpec((tb, 512), lambda i: (i, 0))
    return pl.pallas_call(
        body, name="fwd_out", grid=(s // tb,),
        in_specs=[row, half, half, pl.BlockSpec((1, 512), lambda i: (0, 0)), pl.BlockSpec((1, 512), lambda i: (0, 0)),
                  pl.BlockSpec((D_MODEL, D_MODEL), lambda i: (0, 0)), pl.BlockSpec((1, D_MODEL), lambda i: (0, 0))],
        out_specs=[row, row],
        out_shape=[pltpu.HBM((s, D_MODEL), F32), pltpu.HBM((s, D_MODEL), BF16)],
        compiler_params=_params(40, 1),
    )(*_in_hbm(x, y_pool, y_lru, gn_pool_g, gn_lru_g, w_out, ln2_g))


def _ffn_fwd(h1, n2, target, lnf_g, w_gate, w_up, w_down):
    s = h1.shape[0]
    tb = 512
    sub = 256
    n_ff = D_FF // FF_CHUNK

    def body(h1_ref, n2_ref, t_ref, gf_ref, wg_hbm, wu_hbm, wd_hbm,
             g_ref, u_ref, dh_ref, dhb_ref, dgf_ref, sq_ref, wg, wu, wd, sem):
        @pl.when(pl.program_id(0) == 0)
        def _():
            loads = [pltpu.make_async_copy(src, dst, sem.at[k])
                     for k, (src, dst) in enumerate(((wg_hbm, wg), (wu_hbm, wu), (wd_hbm, wd)))]
            for cp in loads:
                cp.start()
            for cp in loads:
                cp.wait()
            dgf_ref[...] = jnp.zeros_like(dgf_ref)
            sq_ref[...] = jnp.zeros_like(sq_ref)

        n2v = n2_ref[...]
        acc = jnp.zeros((tb, D_MODEL), F32)
        for c in range(n_ff):
            cols = slice(c * FF_CHUNK, (c + 1) * FF_CHUNK)
            g = _mm_nt(n2v, wg[cols, :])
            u = _mm_nt(n2v, wu[cols, :])
            g_ref[:, cols] = g.astype(BF16)
            u_ref[:, cols] = u.astype(BF16)
            act = g * jax.nn.sigmoid(g) * u
            acc = acc + _mm(act.astype(BF16), wd[cols, :])
        gf = gf_ref[...]
        for r in range(tb // sub):
            rows = slice(r * sub, (r + 1) * sub)
            y, xhat, rstd = _rms(h1_ref[rows, :] + acc[rows, :], gf)
            err = y - t_ref[rows, :]
            sq_ref[...] += jnp.sum(err * err, axis=0, keepdims=True)
            dh2, dgf = _rms_bwd(err * (1.0 / D_MODEL), xhat, rstd, gf)
            dgf_ref[...] += dgf
            dh_ref[rows, :] = dh2
            dhb_ref[rows, :] = dh2.astype(BF16)

    row = pl.BlockSpec((tb, D_MODEL), lambda i: (i, 0))
    ff = pl.BlockSpec((tb, D_FF), lambda i: (i, 0))
    vec = pl.BlockSpec((1, D_MODEL), lambda i: (0, 0))
    anyspace = pl.BlockSpec(memory_space=pl.ANY)
    return pl.pallas_call(
        body, name="ffn_fwd", grid=(s // tb,),
        in_specs=[row, row, row, vec, anyspace, anyspace, anyspace],
        out_specs=[ff, ff, row, row, vec, vec],
        out_shape=[pltpu.HBM((s, D_FF), BF16), pltpu.HBM((s, D_FF), BF16),
                   pltpu.HBM((s, D_MODEL), F32), pltpu.HBM((s, D_MODEL), BF16),
                   jax.ShapeDtypeStruct((1, D_MODEL), F32), jax.ShapeDtypeStruct((1, D_MODEL), F32)],
        scratch_shapes=[pltpu.VMEM((D_FF, D_MODEL), BF16), pltpu.VMEM((D_FF, D_MODEL), BF16),
                        pltpu.VMEM((D_FF, D_MODEL), BF16), pltpu.SemaphoreType.DMA((3,))],
        compiler_params=_params(60, 1),
    )(*_in_hbm(h1, n2, target, lnf_g, w_gate, w_up, w_down))


def _ffn_bwd(n2, dh2b, g, u, w_gate_t, w_up_t, w_down):
    s = n2.shape[0]
    tb = min(1024, s)
    n_ff = D_FF // FF_CHUNK
    n_tb = s // tb

    def body(n2_ref, dh_ref, g_ref, u_ref, wg_ref, wu_ref, wd_ref, dwg_ref, dwu_ref, dwd_ref, dn2_ref,
             dn2_acc, acc_g, acc_u, acc_d):
        j = pl.program_id(0)
        t = pl.program_id(1)

        @pl.when(t == 0)
        def _():
            acc_g[...] = jnp.zeros_like(acc_g)
            acc_u[...] = jnp.zeros_like(acc_u)
            acc_d[...] = jnp.zeros_like(acc_d)

        n2v = n2_ref[...]
        dh = dh_ref[...]
        gv = g_ref[...].astype(F32)
        uv = u_ref[...].astype(F32)
        sg = jax.nn.sigmoid(gv)
        silu = gv * sg
        dact = _mm_nt(dh, wd_ref[...])
        dub = (dact * silu).astype(BF16)
        dgb = (dact * uv * (sg * (1.0 + gv * (1.0 - sg)))).astype(BF16)
        acc_d[...] += _mm_tn((silu * uv).astype(BF16), dh)
        acc_g[...] += _mm_tn(dgb, n2v)
        acc_u[...] += _mm_tn(dub, n2v)
        part = _mm(jnp.concatenate([dgb, dub], axis=1), jnp.concatenate([wg_ref[...], wu_ref[...]], axis=0))
        rows = pl.ds(pl.multiple_of(t * tb, tb), tb)

        @pl.when(t == n_tb - 1)
        def _():
            dwg_ref[...] = acc_g[...].astype(BF16)
            dwu_ref[...] = acc_u[...].astype(BF16)
            dwd_ref[...] = acc_d[...].astype(BF16)

        @pl.when(j == 0)
        def _():
            dn2_acc[rows, :] = part

        @pl.when(jnp.logical_and(j > 0, j < n_ff - 1))
        def _():
            dn2_acc[rows, :] += part

        @pl.when(j == n_ff - 1)
        def _():
            dn2_ref[...] = dn2_acc[rows, :] + part

    row = pl.BlockSpec((tb, D_MODEL), lambda j, t: (t, 0))
    act = pl.BlockSpec((tb, FF_CHUNK), lambda j, t: (t, j))
    w_row = pl.BlockSpec((FF_CHUNK, D_MODEL), lambda j, t: (j, 0))
    last = pl.BlockSpec((tb, D_MODEL), lambda j, t: (jnp.where(j == n_ff - 1, t, 0), 0))
    grad = pltpu.HBM((D_FF, D_MODEL), BF16)
    chunk_acc = pltpu.VMEM((FF_CHUNK, D_MODEL), F32)
    return pl.pallas_call(
        body, name="ffn_bwd", grid=(n_ff, n_tb),
        in_specs=[row, row, act, act, w_row, w_row, w_row],
        out_specs=[w_row, w_row, w_row, last],
        out_shape=[grad, grad, grad, pltpu.HBM((s, D_MODEL), F32)],
        scratch_shapes=[pltpu.VMEM((s, D_MODEL), F32), chunk_acc, chunk_acc, chunk_acc],
        compiler_params=_params(56, 2),
    )(*_in_hbm(n2, dh2b, g, u, w_gate_t, w_up_t, w_down))


def _bwd_out(dn2, dh2, h1, y_pool, y_lru, gn_pool_g, gn_lru_g, w_out, ln2_g, after):
    s = h1.shape[0]
    tb = _token_block(s)

    def body(dn2_ref, dh2_ref, h1_ref, yp_ref, yl_ref, gp_ref, gl_ref, w_ref, g2_ref, _after,
             dh1_ref, dyp_ref, dyl_ref, dwb_ref, dg2_ref, dgp_ref, dgl_ref, dw_ref):
        @pl.when(pl.program_id(0) == 0)
        def _():
            dw_ref[...] = jnp.zeros_like(dw_ref)
            dg2_ref[...] = jnp.zeros_like(dg2_ref)
            dgp_ref[...] = jnp.zeros_like(dgp_ref)
            dgl_ref[...] = jnp.zeros_like(dgl_ref)

        g2 = g2_ref[...]
        _, xhat2, rstd2 = _rms(h1_ref[...], g2)
        dres, dg2 = _rms_bwd(dn2_ref[...], xhat2, rstd2, g2)
        dg2_ref[...] += dg2
        dh1 = dh2_ref[...] + dres
        dh1_ref[...] = dh1
        dh1b = dh1.astype(BF16)
        gp, gl = gp_ref[...], gl_ref[...]
        mp, xhat_p, rstd_p = _rms(yp_ref[...], gp)
        ml, xhat_l, rstd_l = _rms(yl_ref[...], gl)
        dw_ref[:POOL_WIDTH, :] += _mm_tn(mp.astype(BF16), dh1b)
        dw_ref[POOL_WIDTH:, :] += _mm_tn(ml.astype(BF16), dh1b)
        dyp, dgp = _rms_bwd(_mm_nt(dh1b, w_ref[:POOL_WIDTH, :]), xhat_p, rstd_p, gp)
        dyl, dgl = _rms_bwd(_mm_nt(dh1b, w_ref[POOL_WIDTH:, :]), xhat_l, rstd_l, gl)
        dyp_ref[...] = dyp
        dyl_ref[...] = dyl
        dgp_ref[...] += dgp
        dgl_ref[...] += dgl

        @pl.when(pl.program_id(0) == s // tb - 1)
        def _():
            dwb_ref[...] = dw_ref[...].astype(BF16)

    row = pl.BlockSpec((tb, D_MODEL), lambda i: (i, 0))
    half = pl.BlockSpec((tb, 512), lambda i: (i, 0))
    vec = pl.BlockSpec((1, D_MODEL), lambda i: (0, 0))
    hvec = pl.BlockSpec((1, 512), lambda i: (0, 0))
    mat = pl.BlockSpec((D_MODEL, D_MODEL), lambda i: (0, 0))
    return pl.pallas_call(
        body, name="bwd_out", grid=(s // tb,),
        in_specs=[row, row, row, half, half, hvec, hvec, mat, vec, pl.BlockSpec(memory_space=pl.ANY)],
        out_specs=[row, half, half, mat, vec, hvec, hvec],
        out_shape=[pltpu.HBM((s, D_MODEL), F32), pltpu.HBM((s, 512), F32),
                   pltpu.HBM((s, 512), F32), pltpu.HBM((D_MODEL, D_MODEL), BF16),
                   jax.ShapeDtypeStruct((1, D_MODEL), F32), jax.ShapeDtypeStruct((1, 512), F32),
                   jax.ShapeDtypeStruct((1, 512), F32)],
        scratch_shapes=[pltpu.VMEM((D_MODEL, D_MODEL), F32)],
        compiler_params=_params(48, 1),
    )(*_in_hbm(dn2, dh2, h1, y_pool, y_lru, gn_pool_g, gn_lru_g, w_out, ln2_g), after)


def _mixer_bwd(u_pool, u_lru, u_gate, h, dy_pool, dy_lru,
               pool_w, pool_scale, conv_w, conv_b, wa_bd, b_a, wi_bd, b_i, lam, after):
    s = u_pool.shape[0]
    tc = _time_chunk(s)
    n_chunks = s // tc

    def body(up_ref, ul_ref, ug_ref, h_ref, dyp_ref, dyl_ref,
             pw_ref, ps_ref, cw_ref, cb_ref, wa_ref, ba_ref, wi_ref, bi_ref, lam_ref, _after,
             dup_ref, dul_ref, dug_ref, dpw_ref, dps_ref, dcw_ref, dcb_ref, dwa_ref, dba_ref, dwi_ref, dbi_ref, dlam_ref,
             a_pad, b_pad):
        _scan_pads(a_pad, b_pad, tc, causal=False)
        deeper, window = _slab_scalars()
        pw = pw_ref[0].astype(BF16)
        wa = wa_ref[0].astype(BF16)
        wi = wi_ref[0].astype(BF16)
        ps, cb, ba, bi = ps_ref[...], cb_ref[...], ba_ref[...], bi_ref[...]
        cw = [cw_ref[k:k + 1, :] for k in range(CONV_WIDTH)]
        lam_v = lam_ref[...]
        sp = _softplus_neg(lam_v)
        row = lax.broadcasted_iota(jnp.int32, (tc, SLAB), 0)
        for ref in (dpw_ref, dps_ref, dcw_ref, dcb_ref, dwa_ref, dba_ref, dwi_ref, dbi_ref, dlam_ref):
            ref[...] = jnp.zeros_like(ref)

        def chunk(t0, ext_p, ext_l, ext_h, carry, at_start=False):
            l_next, a_next, dxc_next, ddn_next = carry
            rows = pl.ds(t0, tc)
            taps = _conv_taps(ext_l)
            xc = _conv(taps, cw, cb)
            xb, r, i, a, mult = _lru_gates(xc, wa, ba, wi, bi, sp)
            hv = ext_h[HALO:]
            h_before = _down(ext_h, 1)[HALO:]
            ug = ug_ref[rows, :]
            dyl = dyl_ref[rows, :]
            gel, th = _gelu(ug)
            dug_ref[rows, :] = dyl * hv * _gelu_grad(ug, th)
            a_after = jnp.where(row == tc - 1, a_next, _up(a, 1))
            l = _scan_anticausal(a_after, dyl * gel, l_next, a_pad, b_pad, tc)
            dmult = l * (i * xc)
            di = l * mult * xc
            dxc = l * mult * i
            dla = (l * h_before) * a - jnp.where(mult > 0.0, dmult * (a * a) / mult, 0.0)
            dlam_ref[...] += jnp.sum(dla * r, axis=0, keepdims=True)
            dpa = (dla * ((-LRU_C) * sp)) * (r * (1.0 - r))
            dpi = di * (i * (1.0 - i))
            dpab = dpa.astype(BF16)
            dpib = dpi.astype(BF16)
            dwa_ref[0] += _mm_tn(xb, dpab)
            dwi_ref[0] += _mm_tn(xb, dpib)
            dba_ref[...] += jnp.sum(dpa, axis=0, keepdims=True)
            dbi_ref[...] += jnp.sum(dpi, axis=0, keepdims=True)
            dxc = dxc + _mm_nt(dpab, wa) + _mm_nt(dpib, wi)
            ext_d = jnp.concatenate([dxc, dxc_next], axis=0)
            dul_ref[rows, :] = (cw[3] * dxc + cw[2] * _up(ext_d, 1)[:tc]
                                + cw[1] * _up(ext_d, 2)[:tc] + cw[0] * _up(ext_d, 3)[:tc])
            for k in range(CONV_WIDTH):
                dcw_ref[k:k + 1, :] += jnp.sum(dxc * taps[CONV_WIDTH - 1 - k], axis=0, keepdims=True)
            dcb_ref[...] += jnp.sum(dxc, axis=0, keepdims=True)
            db = (_window_mean(_causal_window(ext_p, deeper), t0, row, window, at_start) - ext_p[HALO:]).astype(BF16)
            dyp = dyp_ref[rows, :]
            dps_ref[...] += jnp.sum(dyp * _mm(db, pw), axis=0, keepdims=True)
            dys = (dyp * ps).astype(BF16)
            dpw_ref[0] += _mm_tn(db, dys)
            dd = _mm_nt(dys, pw)
            ddn = _window_mean(dd, t0, row, window, at_start)
            ext_q = jnp.concatenate([ddn, ddn_next], axis=0)
            dup_ref[rows, :] = _anticausal_window(ext_q, deeper, tc) - dd
            return l[0:1, :], a[0:1, :], dxc[0:8, :], ddn[0:HALO, :]

        def step(k, carry):
            c = n_chunks - 1 - k
            t0 = pl.multiple_of(c * tc, tc)
            ext = pl.ds(pl.multiple_of(c * tc - HALO, HALO), tc + HALO)
            return chunk(t0, up_ref[ext, :], ul_ref[ext, :], h_ref[ext, :], carry)

        carry = (jnp.zeros((1, SLAB), F32), jnp.zeros((1, SLAB), F32),
                 jnp.zeros((8, SLAB), F32), jnp.zeros((HALO, SLAB), F32))
        carry = lax.fori_loop(0, n_chunks - 1, step, carry)
        pad = jnp.zeros((HALO, SLAB), F32)
        first = pl.ds(0, tc)
        chunk(0, jnp.concatenate([pad, up_ref[first, :]], axis=0), jnp.concatenate([pad, ul_ref[first, :]], axis=0),
              jnp.concatenate([pad, h_ref[first, :]], axis=0), carry, at_start=True)
        dlam_ref[...] = dlam_ref[...] * (LRU_C * jax.nn.sigmoid(-lam_v))

    seq, mat, vec, taps = _slab_specs(s)
    full = pltpu.HBM((s, 512), F32)
    mats = jax.ShapeDtypeStruct((N_SLAB, SLAB, SLAB), F32)
    vecs = jax.ShapeDtypeStruct((1, 512), F32)
    return pl.pallas_call(
        body, name="mixer_bwd", grid=(N_SLAB,),
        in_specs=[seq] * 6 + [mat, vec, taps, vec, mat, vec, mat, vec, vec, pl.BlockSpec(memory_space=pl.ANY)],
        out_specs=[seq, seq, seq, mat, vec, taps, vec, mat, vec, mat, vec, vec],
        out_shape=[full, full, full, mats, vecs, jax.ShapeDtypeStruct((CONV_WIDTH, 512), F32), vecs,
                   mats, vecs, mats, vecs, vecs],
        scratch_shapes=[pltpu.VMEM((tc + SUBLANES, SLAB), F32), pltpu.VMEM((tc + SUBLANES, SLAB), F32)],
        compiler_params=_params(56, 1),
    )(*_in_hbm(u_pool, u_lru, u_gate, h, dy_pool, dy_lru, pool_w, pool_scale, conv_w, conv_b, wa_bd, b_a, wi_bd,
               b_i, lam), after)


def _bwd_in(x, dh1, du_pool, du_lru, du_gate, ln1_g, w_in_t, after):
    s = x.shape[0]
    tb = _token_block(s)

    def body(x_ref, dh1_ref, dup_ref, dul_ref, dug_ref, g_ref, w_ref, _after, dx_ref, dwb_ref, dg_ref, dw_ref):
        @pl.when(pl.program_id(0) == 0)
        def _():
            dw_ref[...] = jnp.zeros_like(dw_ref)
            dg_ref[...] = jnp.zeros_like(dg_ref)

        g1 = g_ref[...]
        n, xhat, rstd = _rms(x_ref[...], g1)
        nb = n.astype(BF16)
        dn = jnp.zeros((tb, D_MODEL), F32)
        for k, ref in enumerate((dup_ref, dul_ref, dug_ref)):
            rows = slice(k * 512, (k + 1) * 512)
            db = ref[...].astype(BF16)
            dw_ref[rows, :] += _mm_tn(db, nb)
            dn = dn + _mm(db, w_ref[rows, :])
        dx, dg1 = _rms_bwd(dn, xhat, rstd, g1)
        dx_ref[...] = dh1_ref[...] + dx
        dg_ref[...] += dg1

        @pl.when(pl.program_id(0) == s // tb - 1)
        def _():
            dwb_ref[...] = dw_ref[...].astype(BF16)

    row = pl.BlockSpec((tb, D_MODEL), lambda i: (i, 0))
    half = pl.BlockSpec((tb, 512), lambda i: (i, 0))
    vec = pl.BlockSpec((1, D_MODEL), lambda i: (0, 0))
    mat = pl.BlockSpec((IN_WIDTH, D_MODEL), lambda i: (0, 0))
    return pl.pallas_call(
        body, name="bwd_in", grid=(s // tb,),
        in_specs=[row, row, half, half, half, vec, mat, pl.BlockSpec(memory_space=pl.ANY)],
        out_specs=[row, mat, vec],
        out_shape=[pltpu.HBM((s, D_MODEL), F32), pltpu.HBM((IN_WIDTH, D_MODEL), BF16),
                   jax.ShapeDtypeStruct((1, D_MODEL), F32)],
        scratch_shapes=[pltpu.VMEM((IN_WIDTH, D_MODEL), F32)],
        compiler_params=_params(48, 1),
    )(*_in_hbm(x, dh1, du_pool, du_lru, du_gate, ln1_g, w_in_t), after)


def _mesh_position():
    x, y, c = lax.axis_index("x"), lax.axis_index("y"), lax.axis_index("c")
    return x, y, c, 4 * x + 2 * y + c


def _peer(x, y, c, p):
    px = 1 - x if p & 4 else x
    py = 1 - y if p & 2 else y
    pc = 1 - c if p & 1 else c
    return (px, py, pc), 4 * px + 2 * py + pc


HBM_SPEC = pl.BlockSpec(memory_space=pltpu.HBM)
SEM_SPEC = pl.BlockSpec(memory_space=pltpu.SEMAPHORE)
DATAFLOW = pltpu.SideEffectType.DATAFLOW_SIDE_EFFECTING


class Copy(NamedTuple):
    src: int
    src_at: Any
    dst: int
    dst_at: Any
    peer: int
    group: int
    slot: int


SIBLING = (1,)
SAME_CORE = (2, 4, 6)
EVERYONE = tuple(range(1, N_DEV))
MYSELF = (0,)


def _same(index):
    return index


def _chip(index):
    return jnp.right_shift(index, 1)


def _fan_out(srcs, lands, peers, group):
    return [Copy(s, None, d, _same, p, group, N_DEV * i + p) for i, (s, d) in enumerate(zip(srcs, lands)) for p in peers]


def _scatter(stacks, lands, peers):
    return [Copy(s, lambda me, p=p: jnp.bitwise_xor(me, p), d, _same, p, 0, 0)
            for s, d in zip(stacks, lands) for p in peers]


def _numbered(copies, group=0):
    return [cp._replace(group=group, slot=i) for i, cp in enumerate(copies)]


def _relay(lands, peers):
    return [Copy(b, lambda s, q=q: jnp.bitwise_xor(s, q), b, lambda s, q=q: jnp.bitwise_xor(s, q), 1, 0, N_DEV * i + q)
            for i, b in enumerate(lands) for q in peers]


def _to_sibling(stacks, lands):
    return [Copy(s, lambda me, k=k: 2 * k + 1 - jnp.bitwise_and(me, 1), d, lambda me, k=k: k, 1, 0, 4 * i + k)
            for i, (s, d) in enumerate(zip(stacks, lands)) for k in range(N_DEV // 2)]


def _to_chips(sums, lands):
    return [Copy(s, lambda me, p=p: jnp.bitwise_xor(_chip(me), p // 2), d, _chip, p, 0, 4 * i + p // 2)
            for i, (s, d) in enumerate(zip(sums, lands)) for p in SAME_CORE]


def _comm_call(name, bufs, wait=None, start=None, after=()):
    nb = len(bufs)
    slots = list(start[1]) if start else []
    n_out_sem = 2 * len(slots)
    after = [a for a in (after if isinstance(after, (list, tuple)) else [after]) if a is not None]

    def body(*refs):
        b = refs[:nb]
        at = nb
        if wait:
            w_send, w_recv = refs[at], refs[at + 1]
            at += 2
        at += len(after)
        out_sems = refs[at:at + n_out_sem]
        token = refs[at + n_out_sem + nb]
        x, y, c, me = _mesh_position()

        def part(i, row_of, sender):
            return b[i] if row_of is None else b[i].at[row_of(sender)]

        if wait:
            for cp in wait[1]:
                peer, peer_index = _peer(x, y, c, cp.peer)
                arrival = pltpu.make_async_remote_copy(part(cp.src, cp.src_at, me), part(cp.dst, cp.dst_at, peer_index),
                                                       w_send.at[cp.slot], w_recv.at[cp.slot],
                                                       device_id=peer, device_id_type=MESH)
                arrival.wait_send()
                arrival.wait_recv()
        if start:
            for cp in start[0]:
                peer, _ = _peer(x, y, c, cp.peer)
                pltpu.make_async_remote_copy(part(cp.src, cp.src_at, me), part(cp.dst, cp.dst_at, me),
                                             out_sems[2 * cp.group].at[cp.slot], out_sems[2 * cp.group + 1].at[cp.slot],
                                             device_id=peer, device_id_type=MESH).start()
        token[...] = jnp.zeros_like(token)

    sem_shapes = []
    for n_slots in slots:
        sem_shapes += [pltpu.SemaphoreType.DMA((n_slots,))] * 2
    operands = [pltpu.with_memory_space_constraint(a, pltpu.HBM) for a in bufs]
    in_specs = [HBM_SPEC] * nb
    if wait:
        operands += list(wait[0])
        in_specs += [SEM_SPEC, SEM_SPEC]
    operands += after
    in_specs += [pl.BlockSpec(memory_space=pl.ANY)] * len(after)
    outs = pl.pallas_call(
        body, name=name, in_specs=in_specs,
        out_specs=[SEM_SPEC] * n_out_sem + [HBM_SPEC] * nb + [pl.BlockSpec(memory_space=pltpu.VMEM)],
        out_shape=sem_shapes + [pltpu.HBM(a.shape, a.dtype) for a in bufs] + [jax.ShapeDtypeStruct((8, SLAB), F32)],
        input_output_aliases={i: n_out_sem + i for i in range(nb)},
        compiler_params=pltpu.CompilerParams(has_side_effects=DATAFLOW),
    )(*operands)
    sems = [(outs[2 * k], outs[2 * k + 1]) for k in range(len(slots))]
    return sems, list(outs[n_out_sem:n_out_sem + nb]), outs[-1]


def _pair_sum(stacks, lands, place):
    n = len(stacks)

    def body(place_ref, *refs):
        k = pl.program_id(0)
        for m in range(n):
            mine, theirs, out, land = refs[m], refs[n + m], refs[2 * n + m], refs[3 * n + m]
            total = (mine[0, 0].astype(F32) + theirs[0].astype(F32)).astype(out.dtype)
            out[0] = total

            @pl.when(k == place_ref[1])
            def _():
                land[0] = total

    in_specs = [pl.BlockSpec((1, 1) + a.shape[1:], lambda k, place_ref: (k, place_ref[0], 0, 0)) for a in stacks]
    in_specs += [pl.BlockSpec((1,) + a.shape[1:], lambda k, place_ref: (k, 0, 0)) for a in lands]
    out_specs = [pl.BlockSpec((1,) + a.shape[1:], lambda k, place_ref: (k, 0, 0)) for a in lands]
    out_specs += [pl.BlockSpec((1,) + a.shape[1:], lambda k, place_ref: (place_ref[1], 0, 0)) for a in lands]
    outs = pl.pallas_call(
        body, name="pair_sum_" + "_".join(str(a.shape[1]) for a in stacks),
        grid_spec=pltpu.PrefetchScalarGridSpec(num_scalar_prefetch=1, grid=(N_DEV // 2,), in_specs=in_specs,
                                               out_specs=out_specs),
        out_shape=[pltpu.HBM(a.shape, a.dtype) for a in lands] * 2,
        compiler_params=_params(40, 1),
    )(place, *_in_hbm(*[a.reshape((N_DEV // 2, 2) + a.shape[1:]) for a in stacks], *lands))
    return list(outs[:n]), list(outs[n:])


def _reduce_adam(parts, w, m, v, name):
    rows, cols = w.shape
    n_parts = parts.shape[0]
    rb = rows
    for cand in (256, 176, 128):
        if rows % cand == 0 and rows > cand:
            rb = cand
            break

    def body(p_ref, w_ref, m_ref, v_ref, g_out, d_out, m_out, v_out):
        g = p_ref[0].astype(F32)
        for j in range(1, n_parts):
            g = g + p_ref[j].astype(F32)
        g_out[...] = g
        d_out[...], m_out[...], v_out[...] = _adam(g, w_ref[...], m_ref[...], v_ref[...])

    blk = pl.BlockSpec((rb, cols), lambda i: (i, 0))
    out = jax.ShapeDtypeStruct((rows, cols), F32)
    return pl.pallas_call(
        body, name=name, grid=(rows // rb,),
        in_specs=[pl.BlockSpec((n_parts, rb, cols), lambda i: (0, i, 0)), blk, blk, blk],
        out_specs=[blk] * 4, out_shape=[out] * 4,
        compiler_params=_params(40, 1),
    )(*_in_hbm(parts, w, m, v))


def _cols_from_stack(stack):
    n, r, c = stack.shape
    return jnp.transpose(stack, (1, 0, 2)).reshape(r, n * c)


def _block_diag(w):
    z = jnp.zeros((N_SLAB, 64, 64), w.dtype)
    pairs = w.reshape(N_SLAB, 2, 64, 64)
    top = jnp.concatenate([pairs[:, 0], z], axis=2)
    bottom = jnp.concatenate([z, pairs[:, 1]], axis=2)
    return jnp.concatenate([top, bottom], axis=1)


def _adam(g, w, m, v):
    m_new = ADAM_B1 * m + (1.0 - ADAM_B1) * g
    v_new = ADAM_B2 * v + (1.0 - ADAM_B2) * (g * g)
    m_hat = m_new / (1.0 - ADAM_B1 ** ADAM_STEP)
    v_hat = v_new / (1.0 - ADAM_B2 ** ADAM_STEP)
    return (-ADAM_LR) * (m_hat / (jnp.sqrt(v_hat) + ADAM_EPS) + ADAM_WD * w), m_new, v_new


WIDE = ("ln2_g", "lnf_g")
HALF = ("pool_scale", "conv_b", "b_a", "b_i", "lam", "gn_pool_g", "gn_lru_g")
VECTORS = [(k, D_MODEL) for k in WIDE] + [(k, 512) for k in HALF]
VECTOR_ROWS = sum(width // SLAB for _, width in VECTORS)
LOSS_ROW = -(-VECTOR_ROWS // 8) * 8
CONV_AT = LOSS_ROW + 8
CONV_LANES = LRU_WIDTH // SLAB
PACK_F_ROWS = CONV_AT + CONV_WIDTH * CONV_LANES
MATRIX_ROWS = N_SLAB * SLAB
HEAD = SLAB // 2
GATE_ROWS = N_SLAB * HEAD
PACK_B_ROWS = MATRIX_ROWS + 2 * GATE_ROWS


def _pack_small(vectors, pool_g, wa_g, wi_g, conv, sq):
    n_vec = len(vectors)

    def body(*refs):
        vec = refs[:n_vec]
        pw_ref, wa_ref, wi_ref, cw_ref, sq_ref, out, out_b = refs[n_vec:]
        out[...] = jnp.zeros_like(out)
        row = 0
        for ref, (_, width) in zip(vec, VECTORS):
            for k in range(width // SLAB):
                out[row:row + 1, :] = ref[:, k * SLAB:(k + 1) * SLAB]
                row += 1
        for tap in range(CONV_WIDTH):
            for k in range(CONV_LANES):
                at = CONV_AT + tap * CONV_LANES + k
                out[at:at + 1, :] = cw_ref[tap:tap + 1, k * SLAB:(k + 1) * SLAB]
        total = sq_ref[:, 0:SLAB]
        for k in range(1, D_MODEL // SLAB):
            total = total + sq_ref[:, k * SLAB:(k + 1) * SLAB]
        out[LOSS_ROW:LOSS_ROW + 1, :] = total
        left = lax.broadcasted_iota(jnp.int32, (HEAD, SLAB), 1) < HEAD
        for s in range(N_SLAB):
            out_b[s * SLAB:(s + 1) * SLAB, :] = pw_ref[s].astype(BF16)
            for i, ref in enumerate((wa_ref, wi_ref)):
                at = MATRIX_ROWS + i * GATE_ROWS + s * HEAD
                out_b[at:at + HEAD, :] = jnp.where(left, ref[s, 0:HEAD, :], ref[s, HEAD:SLAB, :]).astype(BF16)

    return pl.pallas_call(
        body, name="pack_small",
        out_shape=[jax.ShapeDtypeStruct((PACK_F_ROWS, SLAB), F32), jax.ShapeDtypeStruct((PACK_B_ROWS, SLAB), BF16)],
    )(*vectors, pool_g, wa_g, wi_g, conv, sq)


def _small_reduce_adam(parts, parts_b, vec_w, vec_m, vec_v, pool_wmv):
    n_vec = len(VECTORS)
    n_parts = parts.shape[0]

    def body(*refs):
        p_ref, pb_ref = refs[0], refs[1]
        w_refs, m_refs, v_refs = (refs[2 + k * n_vec:2 + (k + 1) * n_vec] for k in range(3))
        pw_w, pw_m, pw_v = refs[2 + 3 * n_vec:5 + 3 * n_vec]
        outs = refs[5 + 3 * n_vec:-1]
        total = refs[-1]
        total[...] = p_ref[0]
        for j in range(1, n_parts):
            total[...] += p_ref[j]
        row = 0
        for i, (_, width) in enumerate(VECTORS):
            n_rows = width // SLAB
            g = jnp.concatenate([total[row + k:row + k + 1, :] for k in range(n_rows)], axis=1)
            row += n_rows
            d, m_new, v_new = _adam(g, w_refs[i][...], m_refs[i][...], v_refs[i][...])
            for ref, val in zip(outs[4 * i:4 * i + 4], (g, d, m_new, v_new)):
                ref[...] = val
        tail = outs[4 * n_vec:]

        def summed(first, count):
            g = pb_ref[0, first:first + count, :].astype(F32)
            for j in range(1, n_parts):
                g = g + pb_ref[j, first:first + count, :].astype(F32)
            return g

        g = summed(0, MATRIX_ROWS)
        d, m_new, v_new = _adam(g, pw_w[...], pw_m[...], pw_v[...])
        for ref, val in zip(tail[0:4], (g, d, m_new, v_new)):
            ref[...] = val
        tail[4][...] = summed(MATRIX_ROWS, GATE_ROWS)
        tail[5][...] = summed(MATRIX_ROWS + GATE_ROWS, GATE_ROWS)
        for tap in range(CONV_WIDTH):
            at = CONV_AT + tap * CONV_LANES
            tail[6][tap:tap + 1, :] = jnp.concatenate([total[at + k:at + k + 1, :] for k in range(CONV_LANES)], axis=1)
        tail[7][...] = (0.5 / D_MODEL) * jnp.sum(total[LOSS_ROW:LOSS_ROW + 1, :], axis=1, keepdims=True)

    out_shape = []
    for _, width in VECTORS:
        out_shape += [jax.ShapeDtypeStruct((1, width), F32)] * 4
    out_shape += [jax.ShapeDtypeStruct((MATRIX_ROWS, SLAB), F32)] * 4 + [jax.ShapeDtypeStruct((GATE_ROWS, SLAB), F32)] * 2
    out_shape += [jax.ShapeDtypeStruct((CONV_WIDTH, LRU_WIDTH), F32), jax.ShapeDtypeStruct((1, 1), F32)]
    outs = pl.pallas_call(
        body, name="adam_small", out_shape=out_shape,
        scratch_shapes=[pltpu.VMEM((PACK_F_ROWS, SLAB), F32)],
        compiler_params=_params(40),
    )(parts, parts_b, *vec_w, *vec_m, *vec_v, *pool_wmv)
    vec_out = [tuple(outs[4 * i:4 * i + 4]) for i in range(n_vec)]
    tail = outs[4 * n_vec:]
    return vec_out, tuple(tail[0:4]), tail[4], tail[5], tail[6], tail[7]


def _plain_adam(grads, ws, ms, vs):
    n = len(grads)

    def body(*refs):
        ins, outs = refs[:4 * n], refs[4 * n:]
        for i in range(n):
            d, m_new, v_new = _adam(ins[i][...], ins[n + i][...], ins[2 * n + i][...], ins[3 * n + i][...])
            for ref, val in zip(outs[3 * i:3 * i + 3], (d, m_new, v_new)):
                ref[...] = val

    out_shape = []
    for g in grads:
        out_shape += [jax.ShapeDtypeStruct(g.shape, F32)] * 3
    outs = pl.pallas_call(body, name="adam_plain", out_shape=out_shape)(*grads, *ws, *ms, *vs)
    return [tuple(outs[3 * i:3 * i + 3]) for i in range(n)]


def kernel(x, ln1_g, w_in, pool_w, pool_scale, conv_w, conv_b, w_a, b_a, w_i, b_i, lam, gn_pool_g, gn_lru_g, w_out, ln2_g, w_ffn_gate, w_ffn_up, w_ffn_down, lnf_g, loss_target, m_ln1_g, m_w_in, m_pool_w, m_pool_scale, m_conv_w, m_conv_b, m_w_a, m_b_a, m_w_i, m_b_i, m_lam, m_gn_pool_g, m_gn_lru_g, m_w_out, m_ln2_g, m_w_ffn_gate, m_w_ffn_up, m_w_ffn_down, m_lnf_g, v_ln1_g, v_w_in, v_pool_w, v_pool_scale, v_conv_w, v_conv_b, v_w_a, v_b_a, v_w_i, v_b_i, v_lam, v_gn_pool_g, v_gn_lru_g, v_w_out, v_ln2_g, v_w_ffn_gate, v_w_ffn_up, v_w_ffn_down, v_lnf_g):
    weights = dict(ln1_g=ln1_g, w_in=w_in, pool_w=pool_w, pool_scale=pool_scale, conv_w=conv_w, conv_b=conv_b,
                   w_a=w_a, b_a=b_a, w_i=w_i, b_i=b_i, lam=lam, gn_pool_g=gn_pool_g, gn_lru_g=gn_lru_g,
                   w_out=w_out, ln2_g=ln2_g, w_ffn_gate=w_ffn_gate, w_ffn_up=w_ffn_up, w_ffn_down=w_ffn_down,
                   lnf_g=lnf_g)
    mom1 = dict(ln1_g=m_ln1_g, w_in=m_w_in, pool_w=m_pool_w, pool_scale=m_pool_scale, conv_w=m_conv_w,
                conv_b=m_conv_b, w_a=m_w_a, b_a=m_b_a, w_i=m_w_i, b_i=m_b_i, lam=m_lam, gn_pool_g=m_gn_pool_g,
                gn_lru_g=m_gn_lru_g, w_out=m_w_out, ln2_g=m_ln2_g, w_ffn_gate=m_w_ffn_gate,
                w_ffn_up=m_w_ffn_up, w_ffn_down=m_w_ffn_down, lnf_g=m_lnf_g)
    mom2 = dict(ln1_g=v_ln1_g, w_in=v_w_in, pool_w=v_pool_w, pool_scale=v_pool_scale, conv_w=v_conv_w,
                conv_b=v_conv_b, w_a=v_w_a, b_a=v_b_a, w_i=v_w_i, b_i=v_b_i, lam=v_lam, gn_pool_g=v_gn_pool_g,
                gn_lru_g=v_gn_lru_g, w_out=v_w_out, ln2_g=v_ln2_g, w_ffn_gate=v_w_ffn_gate,
                w_ffn_up=v_w_ffn_up, w_ffn_down=v_w_ffn_down, lnf_g=v_lnf_g)

    xs = x[0]
    target = loss_target[0]

    shard = dict(w_in=lambda a: a[0].T, w_ffn_gate=lambda a: a[0].T, w_ffn_up=lambda a: a[0].T,
                 w_out=lambda a: a[0], w_ffn_down=lambda a: a[0], conv_w=lambda a: a[0])
    unshard = dict(w_in=lambda a: a.T[None], w_ffn_gate=lambda a: a.T[None], w_ffn_up=lambda a: a.T[None],
                   w_out=lambda a: a[None], w_ffn_down=lambda a: a[None], conv_w=lambda a: a[None])

    gathered = ("w_in", "conv_w", "w_out", "w_ffn_gate", "w_ffn_up", "w_ffn_down")
    groups = ((0, 1), (2,), (3, 4, 5))
    sources = [shard[k](weights[k]) if k == "conv_w" else shard[k](weights[k]).astype(BF16) for k in gathered]
    my_index = 4 * lax.axis_index("x") + 2 * lax.axis_index("y") + lax.axis_index("c")
    lands = [lax.empty((N_DEV,) + a.shape, a.dtype) for a in sources]

    def first_hop(n, group=0):
        return _numbered(_fan_out(range(n), range(n, 2 * n), SAME_CORE + SIBLING + MYSELF, 0), group)

    start = []
    for g, members in enumerate(groups):
        start += _numbered(_fan_out(members, [6 + m for m in members], SAME_CORE + SIBLING + MYSELF, 0), g)
    sems, bufs, _ = _comm_call("gather_start", sources + lands,
                               start=(start, [sum(cp.group == g for cp in start) for g in range(len(groups))]))
    sources, lands = bufs[:6], bufs[6:]

    def gathered_group(tag, g, after):
        members = groups[g]
        n = len(members)
        relay = _numbered(_relay(range(n, 2 * n), SAME_CORE))
        relay_sems, bufs, _ = _comm_call("gather_relay_" + tag, [sources[m] for m in members] + [lands[m] for m in members],
                                         wait=(sems[g], first_hop(n)), start=(relay, (len(relay),)), after=after)
        _, bufs, _ = _comm_call("gather_wait_" + tag, bufs[n:], wait=(relay_sems[0], _numbered(_relay(range(n), SAME_CORE))))
        return bufs

    g_in, g_conv = gathered_group("in", 0, None)
    w_in_f = g_in.reshape(IN_WIDTH, D_MODEL)
    conv_w_f = _cols_from_stack(g_conv)

    wa_bd = _block_diag(w_a[0])
    wi_bd = _block_diag(w_i[0])
    lnf_row = lnf_g.reshape(1, D_MODEL)

    u_pool, u_lru, u_gate = _fwd_in(xs, ln1_g, w_in_f)
    y_pool, h, y_lru = _mixer_fwd(u_pool, u_lru, u_gate, pool_w[0], pool_scale, conv_w_f, conv_b,
                                  wa_bd, b_a, wi_bd, b_i, lam)
    (g_out,) = gathered_group("out", 1, y_pool)
    w_out_f = g_out.reshape(D_MODEL, D_MODEL)
    h1, n2 = _fwd_out(xs, y_pool, y_lru, gn_pool_g, gn_lru_g, w_out_f, ln2_g)
    g_gate, g_up, g_down = gathered_group("ffn", 2, n2)
    w_gate_f = g_gate.reshape(D_FF, D_MODEL)
    w_up_f = g_up.reshape(D_FF, D_MODEL)
    w_down_f = g_down.reshape(D_FF, D_MODEL)
    g_act, u_act, dh2, dh2b, d_lnf, sq = _ffn_fwd(h1, n2, target, lnf_row, w_gate_f, w_up_f, w_down_f)

    place = jnp.stack([lax.axis_index("c"), 2 * lax.axis_index("x") + lax.axis_index("y")]).astype(jnp.int32)

    d_gate, d_up, d_down, dn2 = _ffn_bwd(n2, dh2b, g_act, u_act, w_gate_f, w_up_f, w_down_f)
    ffn_stacks = [d.reshape(N_DEV, D_FF // N_DEV, D_MODEL) for d in (d_gate, d_up, d_down)]
    pair_lands = [lax.empty((N_DEV // 2,) + a.shape[1:], a.dtype) for a in ffn_stacks]
    pair_copies = _numbered(_to_sibling(range(3), range(3, 6)))
    sem, bufs, token = _comm_call("ffn_pair_start", ffn_stacks + pair_lands, start=(pair_copies, (len(pair_copies),)))
    dh1, dy_pool, dy_lru, d_out, d_ln2, d_gnp, d_gnl = _bwd_out(dn2, dh2, h1, y_pool, y_lru, gn_pool_g, gn_lru_g,
                                                                 w_out_f, ln2_g, token)
    _, bufs, _ = _comm_call("ffn_pair_wait", bufs, wait=(sem[0], pair_copies), after=dh1)
    ffn_sums, ffn_lands = _pair_sum(bufs[:3], bufs[3:], place)
    ffn_copies = _numbered(_to_chips(range(3), range(3, 6)))
    ffn_sem, ffn_bufs, token = _comm_call("ffn_chip_start", ffn_sums + ffn_lands, start=(ffn_copies, (len(ffn_copies),)))
    (du_pool, du_lru, du_gate, d_pw, d_ps, d_cw, d_cb, d_wa, d_ba, d_wi, d_bi, d_lam) = _mixer_bwd(
        u_pool, u_lru, u_gate, h, dy_pool, dy_lru, pool_w[0], pool_scale, conv_w_f, conv_b,
        wa_bd, b_a, wi_bd, b_i, lam, token)

    def direct(tag, stacks, wholes, after):
        sources = list(stacks) + list(wholes)
        n, n_st = len(sources), len(stacks)
        lands = [lax.empty(a.shape if i < n_st else (N_DEV,) + a.shape, a.dtype) for i, a in enumerate(sources)]
        copies = _numbered(_scatter(range(n_st), range(n, n + n_st), EVERYONE + MYSELF)
                           + _fan_out(range(n_st, n), range(n + n_st, 2 * n), EVERYONE + MYSELF, 0))
        sem, bufs, token = _comm_call(tag + "_start", sources + lands, start=(copies, (len(copies),)), after=after)
        return (tag, sem[0], bufs, copies), token

    def direct_finish(started, after):
        tag, sem, bufs, copies = started
        _, bufs, _ = _comm_call(tag + "_wait", bufs, wait=(sem, copies), after=after)
        return bufs[len(bufs) // 2:]

    vec_grads = dict(ln2_g=d_ln2, lnf_g=d_lnf, pool_scale=d_ps, conv_b=d_cb, b_a=d_ba, b_i=d_bi,
                     lam=d_lam, gn_pool_g=d_gnp, gn_lru_g=d_gnl)
    packed, packed_b = _pack_small([vec_grads[k] for k, _ in VECTORS], d_pw, d_wa, d_wi, d_cw, sq)
    small_started, token = direct("small", [d_out.reshape(N_DEV, D_MODEL // N_DEV, D_MODEL)], [packed, packed_b], None)
    grad_x, d_in, d_ln1 = _bwd_in(xs, dh1, du_pool, du_lru, du_gate, ln1_g, w_in_f, token)
    in_started, token = direct("in", [d_in.reshape(N_DEV, IN_WIDTH // N_DEV, D_MODEL)], [d_ln1], None)

    results = {}

    def reduce_adam(name, parts):
        outs = _reduce_adam(parts, shard[name](weights[name]), shard[name](mom1[name]), shard[name](mom2[name]),
                            "adam_" + name)
        results[name] = tuple(unshard[name](o) for o in outs)
        return outs[0]

    shard["ln1_g"] = unshard["ln1_g"] = lambda a: a
    _, ffn_bufs, _ = _comm_call("ffn_chip_wait", ffn_bufs, wait=(ffn_sem[0], ffn_copies), after=token)
    done = [reduce_adam(name, parts)
            for name, parts in zip(("w_ffn_gate", "w_ffn_up", "w_ffn_down"), ffn_bufs[3:])]
    r_out, r_small, r_small_b = direct_finish(small_started, done)
    done.append(reduce_adam("w_out", r_out))

    def as_row(a, width):
        return a.reshape(1, width)

    def as_matrix(a):
        return a.reshape(MATRIX_ROWS, SLAB)

    def as_heads(a):
        return a.reshape(2 * GATE_ROWS, HEAD)

    def heads_apart(g):
        return jnp.transpose(g.reshape(N_SLAB, HEAD, 2, HEAD), (0, 2, 1, 3)).reshape(2 * GATE_ROWS, HEAD)

    vec_out, pool_out, g_wa, g_wi, g_conv, loss_11 = _small_reduce_adam(
        r_small, r_small_b, [as_row(weights[k], w) for k, w in VECTORS], [as_row(mom1[k], w) for k, w in VECTORS],
        [as_row(mom2[k], w) for k, w in VECTORS], [as_matrix(t["pool_w"]) for t in (weights, mom1, mom2)])
    for (k, _), outs in zip(VECTORS, vec_out):
        results[k] = tuple(o.reshape(weights[k].shape) for o in outs)
    results["pool_w"] = tuple(o.reshape(pool_w.shape) for o in pool_out)
    my_columns = conv_w.shape[-1]
    plain_names = ("w_a", "w_i", "conv_w")
    plain_grads = [heads_apart(g) for g in (g_wa, g_wi)]
    plain_grads.append(lax.dynamic_slice_in_dim(g_conv, my_index * my_columns, my_columns, axis=1))
    views = (as_heads, as_heads, lambda a: a[0])
    plain_out = _plain_adam(plain_grads, *[[view(t[k]) for k, view in zip(plain_names, views)]
                                           for t in (weights, mom1, mom2)])
    for k, g, outs in zip(plain_names, plain_grads, plain_out):
        results[k] = tuple(o.reshape(weights[k].shape) for o in (g,) + outs)
    loss = loss_11[0, 0]
    r_in, r_ln1 = direct_finish(in_started, done + [plain_out[0][0], loss_11])
    reduce_adam("w_in", r_in)
    reduce_adam("ln1_g", r_ln1)

    order = ["ln1_g", "w_in", "pool_w", "pool_scale", "conv_w", "conv_b", "w_a", "b_a", "w_i", "b_i", "lam",
             "gn_pool_g", "gn_lru_g", "w_out", "ln2_g", "w_ffn_gate", "w_ffn_up", "w_ffn_down", "lnf_g"]
    return (loss, grad_x[None],
            *[results[k][0] for k in order], *[results[k][1] for k in order],
            *[results[k][2] for k in order], *[results[k][3] for k in order])
```

```python
from typing import Any, NamedTuple

import jax
import jax.numpy as jnp
from jax import lax
from jax.experimental import pallas as pl
from jax.experimental.pallas import tpu as pltpu

F32 = jnp.float32
BF16 = jnp.bfloat16

N_DEV = 8
D_MODEL = 1024
POOL_WIDTH = 512
LRU_WIDTH = 512
IN_WIDTH = 1536
D_FF = 2816
N_SLAB = 4
SLAB = 128
CONV_WIDTH = 4
LRU_C = 8.0
EPS = 1e-6
HALO = 16
FF_CHUNK = 256

ADAM_LR = 0.001
ADAM_B1 = 0.9
ADAM_B2 = 0.999
ADAM_EPS = 1e-08
ADAM_WD = 0.01
ADAM_STEP = 10

MIB = 1 << 20
MESH = pl.DeviceIdType.MESH


def _params(vmem_mib, n_axes=0):
    sem = ("arbitrary",) * n_axes if n_axes else None
    return pltpu.CompilerParams(dimension_semantics=sem, vmem_limit_bytes=vmem_mib * MIB)


def _in_hbm(*arrays):
    return [pltpu.with_memory_space_constraint(a, pltpu.HBM) for a in arrays]


def _mm(a, b):
    return jnp.dot(a, b, preferred_element_type=F32)


def _mm_nt(a, b):
    return lax.dot_general(a, b, (((1,), (1,)), ((), ())), preferred_element_type=F32)


def _mm_tn(a, b):
    return lax.dot_general(a, b, (((0,), (0,)), ((), ())), preferred_element_type=F32)


def _rms(x, g):
    rstd = lax.rsqrt(jnp.mean(x * x, axis=-1, keepdims=True) + EPS)
    xhat = x * rstd
    return xhat * g, xhat, rstd


def _rms_bwd(dy, xhat, rstd, g):
    gy = dy * g
    dx = rstd * (gy - xhat * jnp.mean(gy * xhat, axis=-1, keepdims=True))
    return dx, jnp.sum(dy * xhat, axis=0, keepdims=True)


def _gelu(z):
    t = jnp.tanh(0.7978845608028654 * (z + 0.044715 * z * z * z))
    return 0.5 * z * (1.0 + t), t


def _gelu_grad(z, t):
    return 0.5 * (1.0 + t) + 0.5 * z * (1.0 - t * t) * 0.7978845608028654 * (1.0 + 3.0 * 0.044715 * z * z)


def _softplus_neg(lam):
    x = -lam
    e = jnp.exp(-jnp.abs(x))
    u = 1.0 + e
    l1p = jnp.where(u == 1.0, e, jnp.log(u) * e / (u - 1.0))
    return jnp.maximum(x, 0.0) + l1p


def _one_minus_square(la, a):
    x = 2.0 * la
    series = -x * (1.0 + x * (0.5 + x * (1.0 / 6.0 + x * (1.0 / 24.0))))
    return jnp.where(x > -0.06, series, 1.0 - a * a)


def _sigmoid(x):
    return 0.5 * jnp.tanh(0.5 * x) + 0.5


def _down(v, d):
    return pltpu.roll(v, d, 0)


def _up(v, d):
    return pltpu.roll(v, v.shape[0] - d, 0)


def _token_block(s, most=512):
    for rows in (most, 512, 256):
        if rows <= most and s % rows == 0 and s > rows:
            return rows
    return s


def _time_chunk(s):
    return 512 if s % 512 == 0 and s > 512 else 256 if s % 256 == 0 else s


def _fwd_in(x, ln1_g, w_in_t):
    s = x.shape[0]
    tb = _token_block(s, 1024)

    def body(x_ref, g_ref, w_ref, up_ref, ul_ref, ug_ref):
        n, _, _ = _rms(x_ref[...], g_ref[...])
        proj = _mm_nt(n.astype(BF16), w_ref[...])
        up_ref[...] = proj[:, :POOL_WIDTH]
        ul_ref[...] = proj[:, POOL_WIDTH:POOL_WIDTH + LRU_WIDTH]
        ug_ref[...] = proj[:, POOL_WIDTH + LRU_WIDTH:]

    out = pltpu.HBM((s, 512), F32)
    return pl.pallas_call(
        body, name="fwd_in", grid=(s // tb,),
        in_specs=[pl.BlockSpec((tb, D_MODEL), lambda i: (i, 0)),
                  pl.BlockSpec((1, D_MODEL), lambda i: (0, 0)),
                  pl.BlockSpec((IN_WIDTH, D_MODEL), lambda i: (0, 0))],
        out_specs=[pl.BlockSpec((tb, 512), lambda i: (i, 0))] * 3,
        out_shape=[out, out, out],
        compiler_params=_params(40, 1),
    )(*_in_hbm(x, ln1_g, w_in_t))


def _pool_denominator(t0, row, window):
    return jnp.minimum((t0 + row + 1).astype(F32), window)


def _causal_window(ext, deeper):
    s = ext + _down(ext, 1)
    s = s + deeper[0] * _down(s, 2)
    s = s + deeper[1] * _down(s, 4)
    s = s + deeper[2] * _down(s, 8)
    return s[HALO:]


def _anticausal_window(ext, deeper, rows):
    s = ext + _up(ext, 1)
    s = s + deeper[0] * _up(s, 2)
    s = s + deeper[1] * _up(s, 4)
    s = s + deeper[2] * _up(s, 8)
    return s[:rows]


def _conv(taps, cw, cb):
    return cw[3] * taps[0] + cw[2] * taps[1] + cw[1] * taps[2] + cw[0] * taps[3] + cb


def _lru_gates(xc, wa, ba, wi, bi, sp):
    xb = xc.astype(BF16)
    r = _sigmoid(_mm(xb, wa) + ba)
    i = _sigmoid(_mm(xb, wi) + bi)
    la = (-LRU_C) * r * sp
    a = jnp.exp(la)
    mult = jnp.sqrt(jnp.maximum(_one_minus_square(la, a), 0.0))
    return xb, r, i, a, mult


SUBLANES = 8


def _scan_pads(a_pad, b_pad, rows, causal):
    spare = slice(0, SUBLANES) if causal else slice(rows, rows + SUBLANES)
    a_pad[spare, :] = jnp.ones((SUBLANES, SLAB), F32)
    b_pad[spare, :] = jnp.zeros((SUBLANES, SLAB), F32)


def _scan_causal(a, b, h_prev, a_pad, b_pad, rows):
    d = 1
    while d < min(SUBLANES, rows):
        a_pad[SUBLANES:, :] = a
        b_pad[SUBLANES:, :] = b
        b = a * b_pad[SUBLANES - d:SUBLANES - d + rows, :] + b
        a = a * a_pad[SUBLANES - d:SUBLANES - d + rows, :]
        d *= 2
    while d < rows:
        b = jnp.concatenate([b[:d], a[d:] * b[:-d] + b[d:]], axis=0)
        a = jnp.concatenate([a[:d], a[d:] * a[:-d]], axis=0)
        d *= 2
    return b + a * h_prev


def _scan_anticausal(a, b, l_next, a_pad, b_pad, rows):
    d = 1
    while d < min(SUBLANES, rows):
        a_pad[:rows, :] = a
        b_pad[:rows, :] = b
        b = a * b_pad[d:d + rows, :] + b
        a = a * a_pad[d:d + rows, :]
        d *= 2
    while d < rows:
        b = jnp.concatenate([a[:-d] * b[d:] + b[:-d], b[-d:]], axis=0)
        a = jnp.concatenate([a[:-d] * a[d:], a[-d:]], axis=0)
        d *= 2
    return b + a * l_next


def _slab_scalars():
    slab = pl.program_id(0)
    deeper = [jnp.where(slab > k, 1.0, 0.0).astype(F32) for k in range(N_SLAB - 1)]
    window = jnp.left_shift(jnp.int32(2), slab).astype(F32)
    inverse = jnp.where(slab == 0, 0.5, jnp.where(slab == 1, 0.25, jnp.where(slab == 2, 0.125, 0.0625))).astype(F32)
    return deeper, (window, inverse)


def _window_mean(total, t0, row, window, at_start):
    if at_start:
        return total / _pool_denominator(t0, row, window[0])
    return total * window[1]


def _slab_specs(s):
    seq = pl.BlockSpec((s, SLAB), lambda k: (0, k))
    mat = pl.BlockSpec((1, SLAB, SLAB), lambda k: (k, 0, 0))
    vec = pl.BlockSpec((1, SLAB), lambda k: (0, k))
    taps = pl.BlockSpec((CONV_WIDTH, SLAB), lambda k: (0, k))
    return seq, mat, vec, taps


def _mixer_fwd(u_pool, u_lru, u_gate, pool_w, pool_scale, conv_w, conv_b, wa_bd, b_a, wi_bd, b_i, lam):
    s = u_pool.shape[0]
    tc = _time_chunk(s)
    n_chunks = s // tc

    def body(up_ref, ul_ref, ug_ref, pw_ref, ps_ref, cw_ref, cb_ref, wa_ref, ba_ref, wi_ref, bi_ref, lam_ref,
             yp_ref, h_ref, yl_ref, shift_ref, a_pad, b_pad):
        _scan_pads(a_pad, b_pad, tc, causal=True)
        deeper, window = _slab_scalars()
        pw = pw_ref[0].astype(BF16)
        wa = wa_ref[0].astype(BF16)
        wi = wi_ref[0].astype(BF16)
        ps, cb, ba, bi = ps_ref[...], cb_ref[...], ba_ref[...], bi_ref[...]
        cw = [cw_ref[k:k + 1, :] for k in range(CONV_WIDTH)]
        sp = _softplus_neg(lam_ref[...])
        row = lax.broadcasted_iota(jnp.int32, (tc, SLAB), 0)

        def chunk(t0, ext_p, ext_l, h_prev, at_start=False):
            rows = pl.ds(t0, tc)
            d = _window_mean(_causal_window(ext_p, deeper), t0, row, window, at_start) - ext_p[HALO:]
            yp_ref[rows, :] = _mm(d.astype(BF16), pw) * ps
            shift_ref[...] = ext_l
            xc = _conv([shift_ref[HALO - k:HALO - k + tc, :] for k in range(CONV_WIDTH)], cw, cb)
            _, _, i, a, mult = _lru_gates(xc, wa, ba, wi, bi, sp)
            h = _scan_causal(a, mult * (i * xc), h_prev, a_pad, b_pad, tc)
            h_ref[rows, :] = h
            yl_ref[rows, :] = h * _gelu(ug_ref[rows, :])[0]
            return h[tc - 1:tc, :]

        pad = jnp.zeros((HALO, SLAB), F32)
        h0 = chunk(0, jnp.concatenate([pad, up_ref[pl.ds(0, tc), :]], axis=0),
                   jnp.concatenate([pad, ul_ref[pl.ds(0, tc), :]], axis=0), jnp.zeros((1, SLAB), F32), at_start=True)

        def step(c, h_prev):
            t0 = pl.multiple_of(c * tc, tc)
            ext = pl.ds(pl.multiple_of(c * tc - HALO, HALO), tc + HALO)
            return chunk(t0, up_ref[ext, :], ul_ref[ext, :], h_prev)

        lax.fori_loop(1, n_chunks, step, h0)

    seq, mat, vec, taps = _slab_specs(s)
    out = pltpu.HBM((s, 512), F32)
    return pl.pallas_call(
        body, name="mixer_fwd", grid=(N_SLAB,),
        in_specs=[seq, seq, seq, mat, vec, taps, vec, mat, vec, mat, vec, vec],
        out_specs=[seq, seq, seq], out_shape=[out, out, out],
        scratch_shapes=[pltpu.VMEM((tc + HALO, SLAB), F32), pltpu.VMEM((tc + SUBLANES, SLAB), F32),
                        pltpu.VMEM((tc + SUBLANES, SLAB), F32)],
        compiler_params=_params(48, 1),
    )(*_in_hbm(u_pool, u_lru, u_gate, pool_w, pool_scale, conv_w, conv_b, wa_bd, b_a, wi_bd, b_i, lam))


def _fwd_out(x, y_pool, y_lru, gn_pool_g, gn_lru_g, w_out, ln2_g):
    s = x.shape[0]
    tb = _token_block(s, 1024)

    def body(x_ref, yp_ref, yl_ref, gp_ref, gl_ref, w_ref, g2_ref, h1_ref, n2_ref):
        mp, _, _ = _rms(yp_ref[...], gp_ref[...])
        ml, _, _ = _rms(yl_ref[...], gl_ref[...])
        h1 = x_ref[...] + _mm(mp.astype(BF16), w_ref[:POOL_WIDTH, :]) + _mm(ml.astype(BF16), w_ref[POOL_WIDTH:, :])
        h1_ref[...] = h1
        n2_ref[...] = _rms(h1, g2_ref[...])[0].astype(BF16)

    row = pl.BlockSpec((tb, D_MODEL), lambda i: (i, 0))
    half = pl.BlockSpec((tb, 512), lambda i: (i, 0))
    return pl.pallas_call(
        body, name="fwd_out", grid=(s // tb,),
        in_specs=[row, half, half, pl.BlockSpec((1, 512), lambda i: (0, 0)), pl.BlockSpec((1, 512), lambda i: (0, 0)),
                  pl.BlockSpec((D_MODEL, D_MODEL), lambda i: (0, 0)), pl.BlockSpec((1, D_MODEL), lambda i: (0, 0))],
        out_specs=[row, row],
        out_shape=[pltpu.HBM((s, D_MODEL), F32), pltpu.HBM((s, D_MODEL), BF16)],
        compiler_params=_params(40, 1),
    )(*_in_hbm(x, y_pool, y_lru, gn_pool_g, gn_lru_g, w_out, ln2_g))


def _ffn_fwd(h1, n2, target, lnf_g, w_gate, w_up, w_down):
    s = h1.shape[0]
    tb = 512
    sub = 256
    n_ff = D_FF // FF_CHUNK

    def body(h1_ref, n2_ref, t_ref, gf_ref, wg_hbm, wu_hbm, wd_hbm,
             g_ref, u_ref, dh_ref, dhb_ref, dgf_ref, sq_ref, wg, wu, wd, sem):
        @pl.when(pl.program_id(0) == 0)
        def _():
            loads = [pltpu.make_async_copy(src, dst, sem.at[k])
                     for k, (src, dst) in enumerate(((wg_hbm, wg), (wu_hbm, wu), (wd_hbm, wd)))]
            for cp in loads:
                cp.start()
            for cp in loads:
                cp.wait()
            dgf_ref[...] = jnp.zeros_like(dgf_ref)
            sq_ref[...] = jnp.zeros_like(sq_ref)

        n2v = n2_ref[...]
        acc = jnp.zeros((tb, D_MODEL), F32)
        for c in range(n_ff):
            cols = slice(c * FF_CHUNK, (c + 1) * FF_CHUNK)
            g = _mm_nt(n2v, wg[cols, :])
            u = _mm_nt(n2v, wu[cols, :])
            g_ref[:, cols] = g.astype(BF16)
            u_ref[:, cols] = u.astype(BF16)
            act = g * jax.nn.sigmoid(g) * u
            acc = acc + _mm(act.astype(BF16), wd[cols, :])
        gf = gf_ref[...]
        for r in range(tb // sub):
            rows = slice(r * sub, (r + 1) * sub)
            y, xhat, rstd = _rms(h1_ref[rows, :] + acc[rows, :], gf)
            err = y - t_ref[rows, :]
            sq_ref[...] += jnp.sum(err * err, axis=0, keepdims=True)
            dh2, dgf = _rms_bwd(err * (1.0 / D_MODEL), xhat, rstd, gf)
            dgf_ref[...] += dgf
            dh_ref[rows, :] = dh2
            dhb_ref[rows, :] = dh2.astype(BF16)

    row = pl.BlockSpec((tb, D_MODEL), lambda i: (i, 0))
    ff = pl.BlockSpec((tb, D_FF), lambda i: (i, 0))
    vec = pl.BlockSpec((1, D_MODEL), lambda i: (0, 0))
    anyspace = pl.BlockSpec(memory_space=pl.ANY)
    return pl.pallas_call(
        body, name="ffn_fwd", grid=(s // tb,),
        in_specs=[row, row, row, vec, anyspace, anyspace, anyspace],
        out_specs=[ff, ff, row, row, vec, vec],
        out_shape=[pltpu.HBM((s, D_FF), BF16), pltpu.HBM((s, D_FF), BF16),
                   pltpu.HBM((s, D_MODEL), F32), pltpu.HBM((s, D_MODEL), BF16),
                   jax.ShapeDtypeStruct((1, D_MODEL), F32), jax.ShapeDtypeStruct((1, D_MODEL), F32)],
        scratch_shapes=[pltpu.VMEM((D_FF, D_MODEL), BF16), pltpu.VMEM((D_FF, D_MODEL), BF16),
                        pltpu.VMEM((D_FF, D_MODEL), BF16), pltpu.SemaphoreType.DMA((3,))],
        compiler_params=_params(60, 1),
    )(*_in_hbm(h1, n2, target, lnf_g, w_gate, w_up, w_down))


def _ffn_bwd(n2, dh2b, g, u, w_gate_t, w_up_t, w_down):
    s = n2.shape[0]
    tb = min(1024, s)
    n_ff = D_FF // FF_CHUNK
    n_tb = s // tb

    def body(n2_ref, dh_ref, g_ref, u_ref, wg_ref, wu_ref, wd_ref, dwg_ref, dwu_ref, dwd_ref, dn2_ref,
             dn2_acc, acc_g, acc_u, acc_d):
        j = pl.program_id(0)
        t = pl.program_id(1)

        @pl.when(t == 0)
        def _():
            acc_g[...] = jnp.zeros_like(acc_g)
            acc_u[...] = jnp.zeros_like(acc_u)
            acc_d[...] = jnp.zeros_like(acc_d)

        n2v = n2_ref[...]
        dh = dh_ref[...]
        gv = g_ref[...].astype(F32)
        uv = u_ref[...].astype(F32)
        sg = jax.nn.sigmoid(gv)
        silu = gv * sg
        dact = _mm_nt(dh, wd_ref[...])
        dub = (dact * silu).astype(BF16)
        dgb = (dact * uv * (sg * (1.0 + gv * (1.0 - sg)))).astype(BF16)
        acc_d[...] += _mm_tn((silu * uv).astype(BF16), dh)
        acc_g[...] += _mm_tn(dgb, n2v)
        acc_u[...] += _mm_tn(dub, n2v)
        part = _mm(jnp.concatenate([dgb, dub], axis=1), jnp.concatenate([wg_ref[...], wu_ref[...]], axis=0))
        rows = pl.ds(pl.multiple_of(t * tb, tb), tb)

        @pl.when(t == n_tb - 1)
        def _():
            dwg_ref[...] = acc_g[...].astype(BF16)
            dwu_ref[...] = acc_u[...].astype(BF16)
            dwd_ref[...] = acc_d[...].astype(BF16)

        @pl.when(j == 0)
        def _():
            dn2_acc[rows, :] = part

        @pl.when(jnp.logical_and(j > 0, j < n_ff - 1))
        def _():
            dn2_acc[rows, :] += part

        @pl.when(j == n_ff - 1)
        def _():
            dn2_ref[...] = dn2_acc[rows, :] + part

    row = pl.BlockSpec((tb, D_MODEL), lambda j, t: (t, 0))
    act = pl.BlockSpec((tb, FF_CHUNK), lambda j, t: (t, j))
    w_row = pl.BlockSpec((FF_CHUNK, D_MODEL), lambda j, t: (j, 0))
    last = pl.BlockSpec((tb, D_MODEL), lambda j, t: (jnp.where(j == n_ff - 1, t, 0), 0))
    grad = pltpu.HBM((D_FF, D_MODEL), BF16)
    chunk_acc = pltpu.VMEM((FF_CHUNK, D_MODEL), F32)
    return pl.pallas_call(
        body, name="ffn_bwd", grid=(n_ff, n_tb),
        in_specs=[row, row, act, act, w_row, w_row, w_row],
        out_specs=[w_row, w_row, w_row, last],
        out_shape=[grad, grad, grad, pltpu.HBM((s, D_MODEL), F32)],
        scratch_shapes=[pltpu.VMEM((s, D_MODEL), F32), chunk_acc, chunk_acc, chunk_acc],
        compiler_params=_params(56, 2),
    )(*_in_hbm(n2, dh2b, g, u, w_gate_t, w_up_t, w_down))


def _bwd_out(dn2, dh2, h1, y_pool, y_lru, gn_pool_g, gn_lru_g, w_out, ln2_g, after):
    s = h1.shape[0]
    tb = _token_block(s)

    def body(dn2_ref, dh2_ref, h1_ref, yp_ref, yl_ref, gp_ref, gl_ref, w_ref, g2_ref, _after,
             dh1_ref, dyp_ref, dyl_ref, dwb_ref, dg2_ref, dgp_ref, dgl_ref, dw_ref):
        @pl.when(pl.program_id(0) == 0)
        def _():
            dw_ref[...] = jnp.zeros_like(dw_ref)
            dg2_ref[...] = jnp.zeros_like(dg2_ref)
            dgp_ref[...] = jnp.zeros_like(dgp_ref)
            dgl_ref[...] = jnp.zeros_like(dgl_ref)

        g2 = g2_ref[...]
        _, xhat2, rstd2 = _rms(h1_ref[...], g2)
        dres, dg2 = _rms_bwd(dn2_ref[...], xhat2, rstd2, g2)
        dg2_ref[...] += dg2
        dh1 = dh2_ref[...] + dres
        dh1_ref[...] = dh1
        dh1b = dh1.astype(BF16)
        gp, gl = gp_ref[...], gl_ref[...]
        mp, xhat_p, rstd_p = _rms(yp_ref[...], gp)
        ml, xhat_l, rstd_l = _rms(yl_ref[...], gl)
        dw_ref[:POOL_WIDTH, :] += _mm_tn(mp.astype(BF16), dh1b)
        dw_ref[POOL_WIDTH:, :] += _mm_tn(ml.astype(BF16), dh1b)
        dyp, dgp = _rms_bwd(_mm_nt(dh1b, w_ref[:POOL_WIDTH, :]), xhat_p, rstd_p, gp)
        dyl, dgl = _rms_bwd(_mm_nt(dh1b, w_ref[POOL_WIDTH:, :]), xhat_l, rstd_l, gl)
        dyp_ref[...] = dyp
        dyl_ref[...] = dyl
        dgp_ref[...] += dgp
        dgl_ref[...] += dgl

        @pl.when(pl.program_id(0) == s // tb - 1)
        def _():
            dwb_ref[...] = dw_ref[...].astype(BF16)

    row = pl.BlockSpec((tb, D_MODEL), lambda i: (i, 0))
    half = pl.BlockSpec((tb, 512), lambda i: (i, 0))
    vec = pl.BlockSpec((1, D_MODEL), lambda i: (0, 0))
    hvec = pl.BlockSpec((1, 512), lambda i: (0, 0))
    mat = pl.BlockSpec((D_MODEL, D_MODEL), lambda i: (0, 0))
    return pl.pallas_call(
        body, name="bwd_out", grid=(s // tb,),
        in_specs=[row, row, row, half, half, hvec, hvec, mat, vec, pl.BlockSpec(memory_space=pl.ANY)],
        out_specs=[row, half, half, mat, vec, hvec, hvec],
        out_shape=[pltpu.HBM((s, D_MODEL), F32), pltpu.HBM((s, 512), F32),
                   pltpu.HBM((s, 512), F32), pltpu.HBM((D_MODEL, D_MODEL), BF16),
                   jax.ShapeDtypeStruct((1, D_MODEL), F32), jax.ShapeDtypeStruct((1, 512), F32),
                   jax.ShapeDtypeStruct((1, 512), F32)],
        scratch_shapes=[pltpu.VMEM((D_MODEL, D_MODEL), F32)],
        compiler_params=_params(48, 1),
    )(*_in_hbm(dn2, dh2, h1, y_pool, y_lru, gn_pool_g, gn_lru_g, w_out, ln2_g), after)


def _mixer_bwd(u_pool, u_lru, u_gate, h, dy_pool, dy_lru,
               pool_w, pool_scale, conv_w, conv_b, wa_bd, b_a, wi_bd, b_i, lam, after):
    s = u_pool.shape[0]
    tc = _time_chunk(s)
    n_chunks = s // tc

    def body(up_ref, ul_ref, ug_ref, h_ref, dyp_ref, dyl_ref,
             pw_ref, ps_ref, cw_ref, cb_ref, wa_ref, ba_ref, wi_ref, bi_ref, lam_ref, _after,
             dup_ref, dul_ref, dug_ref, dpw_ref, dps_ref, dcw_ref, dcb_ref, dwa_ref, dba_ref, dwi_ref, dbi_ref, dlam_ref,
             a_pad, b_pad, u_shift, h_shift, a_shift, d_shift):
        _scan_pads(a_pad, b_pad, tc, causal=False)
        deeper, window = _slab_scalars()
        pw = pw_ref[0].astype(BF16)
        wa = wa_ref[0].astype(BF16)
        wi = wi_ref[0].astype(BF16)
        ps, cb, ba, bi = ps_ref[...], cb_ref[...], ba_ref[...], bi_ref[...]
        cw = [cw_ref[k:k + 1, :] for k in range(CONV_WIDTH)]
        lam_v = lam_ref[...]
        sp = _softplus_neg(lam_v)
        row = lax.broadcasted_iota(jnp.int32, (tc, SLAB), 0)
        for ref in (dpw_ref, dps_ref, dcw_ref, dcb_ref, dwa_ref, dba_ref, dwi_ref, dbi_ref, dlam_ref):
            ref[...] = jnp.zeros_like(ref)

        def chunk(t0, ext_p, ext_l, ext_h, carry, at_start=False):
            l_next, a_next, dxc_next, ddn_next = carry
            rows = pl.ds(t0, tc)
            u_shift[...] = ext_l
            taps = [u_shift[HALO - k:HALO - k + tc, :] for k in range(CONV_WIDTH)]
            xc = _conv(taps, cw, cb)
            xb, r, i, a, mult = _lru_gates(xc, wa, ba, wi, bi, sp)
            hv = ext_h[HALO:]
            h_shift[...] = ext_h
            h_before = h_shift[HALO - 1:HALO - 1 + tc, :]
            ug = ug_ref[rows, :]
            dyl = dyl_ref[rows, :]
            gel, th = _gelu(ug)
            dug_ref[rows, :] = dyl * hv * _gelu_grad(ug, th)
            a_shift[:tc, :] = a
            a_shift[tc:, :] = jnp.broadcast_to(a_next, (SUBLANES, SLAB))
            a_after = a_shift[1:1 + tc, :]
            l = _scan_anticausal(a_after, dyl * gel, l_next, a_pad, b_pad, tc)
            dmult = l * (i * xc)
            di = l * mult * xc
            dxc = l * mult * i
            dla = (l * h_before) * a - jnp.where(mult > 0.0, dmult * (a * a) / mult, 0.0)
            dlam_ref[...] += jnp.sum(dla * r, axis=0, keepdims=True)
            dpa = (dla * ((-LRU_C) * sp)) * (r * (1.0 - r))
            dpi = di * (i * (1.0 - i))
            dpab = dpa.astype(BF16)
            dpib = dpi.astype(BF16)
            dwa_ref[0] += _mm_tn(xb, dpab)
            dwi_ref[0] += _mm_tn(xb, dpib)
            dba_ref[...] += jnp.sum(dpa, axis=0, keepdims=True)
            dbi_ref[...] += jnp.sum(dpi, axis=0, keepdims=True)
            dxc = dxc + _mm_nt(dpab, wa) + _mm_nt(dpib, wi)
            d_shift[:tc, :] = dxc
            d_shift[tc:, :] = dxc_next
            dul_ref[rows, :] = (cw[3] * dxc + cw[2] * d_shift[1:1 + tc, :]
                                + cw[1] * d_shift[2:2 + tc, :] + cw[0] * d_shift[3:3 + tc, :])
            for k in range(CONV_WIDTH):
                dcw_ref[k:k + 1, :] += jnp.sum(dxc * taps[CONV_WIDTH - 1 - k], axis=0, keepdims=True)
            dcb_ref[...] += jnp.sum(dxc, axis=0, keepdims=True)
            db = (_window_mean(_causal_window(ext_p, deeper), t0, row, window, at_start) - ext_p[HALO:]).astype(BF16)
            dyp = dyp_ref[rows, :]
            dps_ref[...] += jnp.sum(dyp * _mm(db, pw), axis=0, keepdims=True)
            dys = (dyp * ps).astype(BF16)
            dpw_ref[0] += _mm_tn(db, dys)
            dd = _mm_nt(dys, pw)
            ddn = _window_mean(dd, t0, row, window, at_start)
            ext_q = jnp.concatenate([ddn, ddn_next], axis=0)
            dup_ref[rows, :] = _anticausal_window(ext_q, deeper, tc) - dd
            return l[0:1, :], a[0:1, :], dxc[0:8, :], ddn[0:HALO, :]

        def step(k, carry):
            c = n_chunks - 1 - k
            t0 = pl.multiple_of(c * tc, tc)
            ext = pl.ds(pl.multiple_of(c * tc - HALO, HALO), tc + HALO)
            return chunk(t0, up_ref[ext, :], ul_ref[ext, :], h_ref[ext, :], carry)

        carry = (jnp.zeros((1, SLAB), F32), jnp.zeros((1, SLAB), F32),
                 jnp.zeros((8, SLAB), F32), jnp.zeros((HALO, SLAB), F32))
        carry = lax.fori_loop(0, n_chunks - 1, step, carry)
        pad = jnp.zeros((HALO, SLAB), F32)
        first = pl.ds(0, tc)
        chunk(0, jnp.concatenate([pad, up_ref[first, :]], axis=0), jnp.concatenate([pad, ul_ref[first, :]], axis=0),
              jnp.concatenate([pad, h_ref[first, :]], axis=0), carry, at_start=True)
        dlam_ref[...] = dlam_ref[...] * (LRU_C * jax.nn.sigmoid(-lam_v))

    seq, mat, vec, taps = _slab_specs(s)
    full = pltpu.HBM((s, 512), F32)
    mats = jax.ShapeDtypeStruct((N_SLAB, SLAB, SLAB), F32)
    vecs = jax.ShapeDtypeStruct((1, 512), F32)
    return pl.pallas_call(
        body, name="mixer_bwd", grid=(N_SLAB,),
        in_specs=[seq] * 6 + [mat, vec, taps, vec, mat, vec, mat, vec, vec, pl.BlockSpec(memory_space=pl.ANY)],
        out_specs=[seq, seq, seq, mat, vec, taps, vec, mat, vec, mat, vec, vec],
        out_shape=[full, full, full, mats, vecs, jax.ShapeDtypeStruct((CONV_WIDTH, 512), F32), vecs,
                   mats, vecs, mats, vecs, vecs],
        scratch_shapes=[pltpu.VMEM((tc + SUBLANES, SLAB), F32), pltpu.VMEM((tc + SUBLANES, SLAB), F32),
                        pltpu.VMEM((tc + HALO, SLAB), F32), pltpu.VMEM((tc + HALO, SLAB), F32),
                        pltpu.VMEM((tc + SUBLANES, SLAB), F32), pltpu.VMEM((tc + SUBLANES, SLAB), F32)],
        compiler_params=_params(56, 1),
    )(*_in_hbm(u_pool, u_lru, u_gate, h, dy_pool, dy_lru, pool_w, pool_scale, conv_w, conv_b, wa_bd, b_a, wi_bd,
               b_i, lam), after)


def _bwd_in(x, dh1, du_pool, du_lru, du_gate, ln1_g, w_in_t, after):
    s = x.shape[0]
    tb = _token_block(s)

    def body(x_ref, dh1_ref, dup_ref, dul_ref, dug_ref, g_ref, w_ref, _after, dx_ref, dwb_ref, dg_ref, dw_ref):
        @pl.when(pl.program_id(0) == 0)
        def _():
            dw_ref[...] = jnp.zeros_like(dw_ref)
            dg_ref[...] = jnp.zeros_like(dg_ref)

        g1 = g_ref[...]
        n, xhat, rstd = _rms(x_ref[...], g1)
        nb = n.astype(BF16)
        db = jnp.concatenate([ref[...].astype(BF16) for ref in (dup_ref, dul_ref, dug_ref)], axis=1)
        dw_ref[...] += _mm_tn(db, nb)
        dx, dg1 = _rms_bwd(_mm(db, w_ref[...]), xhat, rstd, g1)
        dx_ref[...] = dh1_ref[...] + dx
        dg_ref[...] += dg1

        @pl.when(pl.program_id(0) == s // tb - 1)
        def _():
            dwb_ref[...] = dw_ref[...].astype(BF16)

    row = pl.BlockSpec((tb, D_MODEL), lambda i: (i, 0))
    half = pl.BlockSpec((tb, 512), lambda i: (i, 0))
    vec = pl.BlockSpec((1, D_MODEL), lambda i: (0, 0))
    mat = pl.BlockSpec((IN_WIDTH, D_MODEL), lambda i: (0, 0))
    return pl.pallas_call(
        body, name="bwd_in", grid=(s // tb,),
        in_specs=[row, row, half, half, half, vec, mat, pl.BlockSpec(memory_space=pl.ANY)],
        out_specs=[row, mat, vec],
        out_shape=[pltpu.HBM((s, D_MODEL), F32), pltpu.HBM((IN_WIDTH, D_MODEL), BF16),
                   jax.ShapeDtypeStruct((1, D_MODEL), F32)],
        scratch_shapes=[pltpu.VMEM((IN_WIDTH, D_MODEL), F32)],
        compiler_params=_params(48, 1),
    )(*_in_hbm(x, dh1, du_pool, du_lru, du_gate, ln1_g, w_in_t), after)


def _mesh_position():
    x, y, c = lax.axis_index("x"), lax.axis_index("y"), lax.axis_index("c")
    return x, y, c, 4 * x + 2 * y + c


def _peer(x, y, c, p):
    px = 1 - x if p & 4 else x
    py = 1 - y if p & 2 else y
    pc = 1 - c if p & 1 else c
    return (px, py, pc), 4 * px + 2 * py + pc


HBM_SPEC = pl.BlockSpec(memory_space=pltpu.HBM)
SEM_SPEC = pl.BlockSpec(memory_space=pltpu.SEMAPHORE)
DATAFLOW = pltpu.SideEffectType.DATAFLOW_SIDE_EFFECTING


class Copy(NamedTuple):
    src: int
    src_at: Any
    dst: int
    dst_at: Any
    peer: int
    group: int
    slot: int


SIBLING = (1,)
SAME_CORE = (2, 4, 6)
EVERYONE = tuple(range(1, N_DEV))
MYSELF = (0,)


def _same(index):
    return index


def _chip(index):
    return jnp.right_shift(index, 1)


def _fan_out(srcs, lands, peers, group):
    return [Copy(s, None, d, _same, p, group, N_DEV * i + p) for i, (s, d) in enumerate(zip(srcs, lands)) for p in peers]


def _scatter(stacks, lands, peers):
    return [Copy(s, lambda me, p=p: jnp.bitwise_xor(me, p), d, _same, p, 0, 0)
            for s, d in zip(stacks, lands) for p in peers]


def _numbered(copies, group=0):
    return [cp._replace(group=group, slot=i) for i, cp in enumerate(copies)]


def _relay(lands, peers):
    return [Copy(b, lambda s, q=q: jnp.bitwise_xor(s, q), b, lambda s, q=q: jnp.bitwise_xor(s, q), 1, 0, N_DEV * i + q)
            for i, b in enumerate(lands) for q in peers]


def _to_sibling(stacks, lands):
    return [Copy(s, lambda me, k=k: 2 * k + 1 - jnp.bitwise_and(me, 1), d, lambda me, k=k: k, 1, 0, 4 * i + k)
            for i, (s, d) in enumerate(zip(stacks, lands)) for k in range(N_DEV // 2)]


def _to_chips(sums, lands):
    return [Copy(s, lambda me, p=p: jnp.bitwise_xor(_chip(me), p // 2), d, _chip, p, 0, 4 * i + p // 2)
            for i, (s, d) in enumerate(zip(sums, lands)) for p in SAME_CORE]


def _comm_call(name, bufs, wait=None, start=None, after=()):
    nb = len(bufs)
    slots = list(start[1]) if start else []
    n_out_sem = 2 * len(slots)
    after = [a for a in (after if isinstance(after, (list, tuple)) else [after]) if a is not None]

    def body(*refs):
        b = refs[:nb]
        at = nb
        if wait:
            w_send, w_recv = refs[at], refs[at + 1]
            at += 2
        at += len(after)
        out_sems = refs[at:at + n_out_sem]
        token = refs[at + n_out_sem + nb]
        x, y, c, me = _mesh_position()

        def part(i, row_of, sender):
            return b[i] if row_of is None else b[i].at[row_of(sender)]

        if wait:
            for cp in wait[1]:
                peer, peer_index = _peer(x, y, c, cp.peer)
                arrival = pltpu.make_async_remote_copy(part(cp.src, cp.src_at, me), part(cp.dst, cp.dst_at, peer_index),
                                                       w_send.at[cp.slot], w_recv.at[cp.slot],
                                                       device_id=peer, device_id_type=MESH)
                arrival.wait_send()
                arrival.wait_recv()
        if start:
            for cp in start[0]:
                peer, _ = _peer(x, y, c, cp.peer)
                pltpu.make_async_remote_copy(part(cp.src, cp.src_at, me), part(cp.dst, cp.dst_at, me),
                                             out_sems[2 * cp.group].at[cp.slot], out_sems[2 * cp.group + 1].at[cp.slot],
                                             device_id=peer, device_id_type=MESH).start()
        token[...] = jnp.zeros_like(token)

    sem_shapes = []
    for n_slots in slots:
        sem_shapes += [pltpu.SemaphoreType.DMA((n_slots,))] * 2
    operands = [pltpu.with_memory_space_constraint(a, pltpu.HBM) for a in bufs]
    in_specs = [HBM_SPEC] * nb
    if wait:
        operands += list(wait[0])
        in_specs += [SEM_SPEC, SEM_SPEC]
    operands += after
    in_specs += [pl.BlockSpec(memory_space=pl.ANY)] * len(after)
    outs = pl.pallas_call(
        body, name=name, in_specs=in_specs,
        out_specs=[SEM_SPEC] * n_out_sem + [HBM_SPEC] * nb + [pl.BlockSpec(memory_space=pltpu.VMEM)],
        out_shape=sem_shapes + [pltpu.HBM(a.shape, a.dtype) for a in bufs] + [jax.ShapeDtypeStruct((8, SLAB), F32)],
        input_output_aliases={i: n_out_sem + i for i in range(nb)},
        compiler_params=pltpu.CompilerParams(has_side_effects=DATAFLOW),
    )(*operands)
    sems = [(outs[2 * k], outs[2 * k + 1]) for k in range(len(slots))]
    return sems, list(outs[n_out_sem:n_out_sem + nb]), outs[-1]


def _pair_sum(stacks, lands, place):
    n = len(stacks)

    def body(place_ref, *refs):
        k = pl.program_id(0)
        for m in range(n):
            mine, theirs, out, land = refs[m], refs[n + m], refs[2 * n + m], refs[3 * n + m]
            total = (mine[0, 0].astype(F32) + theirs[0].astype(F32)).astype(out.dtype)
            out[0] = total

            @pl.when(k == place_ref[1])
            def _():
                land[0] = total

    in_specs = [pl.BlockSpec((1, 1) + a.shape[1:], lambda k, place_ref: (k, place_ref[0], 0, 0)) for a in stacks]
    in_specs += [pl.BlockSpec((1,) + a.shape[1:], lambda k, place_ref: (k, 0, 0)) for a in lands]
    out_specs = [pl.BlockSpec((1,) + a.shape[1:], lambda k, place_ref: (k, 0, 0)) for a in lands]
    out_specs += [pl.BlockSpec((1,) + a.shape[1:], lambda k, place_ref: (place_ref[1], 0, 0)) for a in lands]
    outs = pl.pallas_call(
        body, name="pair_sum_" + "_".join(str(a.shape[1]) for a in stacks),
        grid_spec=pltpu.PrefetchScalarGridSpec(num_scalar_prefetch=1, grid=(N_DEV // 2,), in_specs=in_specs,
                                               out_specs=out_specs),
        out_shape=[pltpu.HBM(a.shape, a.dtype) for a in lands] * 2,
        compiler_params=_params(40, 1),
    )(place, *_in_hbm(*[a.reshape((N_DEV // 2, 2) + a.shape[1:]) for a in stacks], *lands))
    return list(outs[:n]), list(outs[n:])


def _reduce_adam(parts, w, m, v, name):
    rows, cols = w.shape
    n_parts = parts.shape[0]
    rb = rows
    for cand in (256, 176, 128):
        if rows % cand == 0 and rows > cand:
            rb = cand
            break

    def body(p_ref, w_ref, m_ref, v_ref, g_out, d_out, m_out, v_out):
        g = p_ref[0].astype(F32)
        for j in range(1, n_parts):
            g = g + p_ref[j].astype(F32)
        g_out[...] = g
        d_out[...], m_out[...], v_out[...] = _adam(g, w_ref[...], m_ref[...], v_ref[...])

    blk = pl.BlockSpec((rb, cols), lambda i: (i, 0))
    out = jax.ShapeDtypeStruct((rows, cols), F32)
    return pl.pallas_call(
        body, name=name, grid=(rows // rb,),
        in_specs=[pl.BlockSpec((n_parts, rb, cols), lambda i: (0, i, 0)), blk, blk, blk],
        out_specs=[blk] * 4, out_shape=[out] * 4,
        compiler_params=_params(40, 1),
    )(*_in_hbm(parts, w, m, v))


def _cols_from_stack(stack):
    n, r, c = stack.shape
    return jnp.transpose(stack, (1, 0, 2)).reshape(r, n * c)


def _block_diag(w):
    z = jnp.zeros((N_SLAB, 64, 64), w.dtype)
    pairs = w.reshape(N_SLAB, 2, 64, 64)
    top = jnp.concatenate([pairs[:, 0], z], axis=2)
    bottom = jnp.concatenate([z, pairs[:, 1]], axis=2)
    return jnp.concatenate([top, bottom], axis=1)


def _adam(g, w, m, v):
    m_new = ADAM_B1 * m + (1.0 - ADAM_B1) * g
    v_new = ADAM_B2 * v + (1.0 - ADAM_B2) * (g * g)
    m_hat = m_new / (1.0 - ADAM_B1 ** ADAM_STEP)
    v_hat = v_new / (1.0 - ADAM_B2 ** ADAM_STEP)
    return (-ADAM_LR) * (m_hat / (jnp.sqrt(v_hat) + ADAM_EPS) + ADAM_WD * w), m_new, v_new


WIDE = ("ln2_g", "lnf_g")
HALF = ("pool_scale", "conv_b", "b_a", "b_i", "lam", "gn_pool_g", "gn_lru_g")
VECTORS = [(k, D_MODEL) for k in WIDE] + [(k, 512) for k in HALF]
VECTOR_ROWS = sum(width // SLAB for _, width in VECTORS)
LOSS_ROW = -(-VECTOR_ROWS // 8) * 8
CONV_AT = LOSS_ROW + 8
CONV_LANES = LRU_WIDTH // SLAB
PACK_F_ROWS = CONV_AT + CONV_WIDTH * CONV_LANES
MATRIX_ROWS = N_SLAB * SLAB
HEAD = SLAB // 2
GATE_ROWS = N_SLAB * HEAD
PACK_B_ROWS = MATRIX_ROWS + 2 * GATE_ROWS


def _pack_small(vectors, pool_g, wa_g, wi_g, conv, sq):
    n_vec = len(vectors)

    def body(*refs):
        vec = refs[:n_vec]
        pw_ref, wa_ref, wi_ref, cw_ref, sq_ref, out, out_b = refs[n_vec:]
        out[...] = jnp.zeros_like(out)
        row = 0
        for ref, (_, width) in zip(vec, VECTORS):
            for k in range(width // SLAB):
                out[row:row + 1, :] = ref[:, k * SLAB:(k + 1) * SLAB]
                row += 1
        for tap in range(CONV_WIDTH):
            for k in range(CONV_LANES):
                at = CONV_AT + tap * CONV_LANES + k
                out[at:at + 1, :] = cw_ref[tap:tap + 1, k * SLAB:(k + 1) * SLAB]
        total = sq_ref[:, 0:SLAB]
        for k in range(1, D_MODEL // SLAB):
            total = total + sq_ref[:, k * SLAB:(k + 1) * SLAB]
        out[LOSS_ROW:LOSS_ROW + 1, :] = total
        left = lax.broadcasted_iota(jnp.int32, (HEAD, SLAB), 1) < HEAD
        for s in range(N_SLAB):
            out_b[s * SLAB:(s + 1) * SLAB, :] = pw_ref[s].astype(BF16)
            for i, ref in enumerate((wa_ref, wi_ref)):
                at = MATRIX_ROWS + i * GATE_ROWS + s * HEAD
                out_b[at:at + HEAD, :] = jnp.where(left, ref[s, 0:HEAD, :], ref[s, HEAD:SLAB, :]).astype(BF16)

    return pl.pallas_call(
        body, name="pack_small",
        out_shape=[jax.ShapeDtypeStruct((PACK_F_ROWS, SLAB), F32), jax.ShapeDtypeStruct((PACK_B_ROWS, SLAB), BF16)],
    )(*vectors, pool_g, wa_g, wi_g, conv, sq)


def _small_reduce_adam(parts, parts_b, vec_w, vec_m, vec_v, pool_wmv):
    n_vec = len(VECTORS)
    n_parts = parts.shape[0]

    def body(*refs):
        p_ref, pb_ref = refs[0], refs[1]
        w_refs, m_refs, v_refs = (refs[2 + k * n_vec:2 + (k + 1) * n_vec] for k in range(3))
        pw_w, pw_m, pw_v = refs[2 + 3 * n_vec:5 + 3 * n_vec]
        outs = refs[5 + 3 * n_vec:-1]
        total = refs[-1]
        total[...] = p_ref[0]
        for j in range(1, n_parts):
            total[...] += p_ref[j]
        row = 0
        for i, (_, width) in enumerate(VECTORS):
            n_rows = width // SLAB
            g = jnp.concatenate([total[row + k:row + k + 1, :] for k in range(n_rows)], axis=1)
            row += n_rows
            d, m_new, v_new = _adam(g, w_refs[i][...], m_refs[i][...], v_refs[i][...])
            for ref, val in zip(outs[4 * i:4 * i + 4], (g, d, m_new, v_new)):
                ref[...] = val
        tail = outs[4 * n_vec:]

        def summed(first, count):
            g = pb_ref[0, first:first + count, :].astype(F32)
            for j in range(1, n_parts):
                g = g + pb_ref[j, first:first + count, :].astype(F32)
            return g

        g = summed(0, MATRIX_ROWS)
        d, m_new, v_new = _adam(g, pw_w[...], pw_m[...], pw_v[...])
        for ref, val in zip(tail[0:4], (g, d, m_new, v_new)):
            ref[...] = val
        tail[4][...] = summed(MATRIX_ROWS, GATE_ROWS)
        tail[5][...] = summed(MATRIX_ROWS + GATE_ROWS, GATE_ROWS)
        for tap in range(CONV_WIDTH):
            at = CONV_AT + tap * CONV_LANES
            tail[6][tap:tap + 1, :] = jnp.concatenate([total[at + k:at + k + 1, :] for k in range(CONV_LANES)], axis=1)
        tail[7][...] = (0.5 / D_MODEL) * jnp.sum(total[LOSS_ROW:LOSS_ROW + 1, :], axis=1, keepdims=True)

    out_shape = []
    for _, width in VECTORS:
        out_shape += [jax.ShapeDtypeStruct((1, width), F32)] * 4
    out_shape += [jax.ShapeDtypeStruct((MATRIX_ROWS, SLAB), F32)] * 4 + [jax.ShapeDtypeStruct((GATE_ROWS, SLAB), F32)] * 2
    out_shape += [jax.ShapeDtypeStruct((CONV_WIDTH, LRU_WIDTH), F32), jax.ShapeDtypeStruct((1, 1), F32)]
    outs = pl.pallas_call(
        body, name="adam_small", out_shape=out_shape,
        scratch_shapes=[pltpu.VMEM((PACK_F_ROWS, SLAB), F32)],
        compiler_params=_params(40),
    )(parts, parts_b, *vec_w, *vec_m, *vec_v, *pool_wmv)
    vec_out = [tuple(outs[4 * i:4 * i + 4]) for i in range(n_vec)]
    tail = outs[4 * n_vec:]
    return vec_out, tuple(tail[0:4]), tail[4], tail[5], tail[6], tail[7]


def _plain_adam(grads, ws, ms, vs):
    n = len(grads)

    def body(*refs):
        ins, outs = refs[:4 * n], refs[4 * n:]
        for i in range(n):
            d, m_new, v_new = _adam(ins[i][...], ins[n + i][...], ins[2 * n + i][...], ins[3 * n + i][...])
            for ref, val in zip(outs[3 * i:3 * i + 3], (d, m_new, v_new)):
                ref[...] = val

    out_shape = []
    for g in grads:
        out_shape += [jax.ShapeDtypeStruct(g.shape, F32)] * 3
    outs = pl.pallas_call(body, name="adam_plain", out_shape=out_shape)(*grads, *ws, *ms, *vs)
    return [tuple(outs[3 * i:3 * i + 3]) for i in range(n)]


def kernel(x, ln1_g, w_in, pool_w, pool_scale, conv_w, conv_b, w_a, b_a, w_i, b_i, lam, gn_pool_g, gn_lru_g, w_out, ln2_g, w_ffn_gate, w_ffn_up, w_ffn_down, lnf_g, loss_target, m_ln1_g, m_w_in, m_pool_w, m_pool_scale, m_conv_w, m_conv_b, m_w_a, m_b_a, m_w_i, m_b_i, m_lam, m_gn_pool_g, m_gn_lru_g, m_w_out, m_ln2_g, m_w_ffn_gate, m_w_ffn_up, m_w_ffn_down, m_lnf_g, v_ln1_g, v_w_in, v_pool_w, v_pool_scale, v_conv_w, v_conv_b, v_w_a, v_b_a, v_w_i, v_b_i, v_lam, v_gn_pool_g, v_gn_lru_g, v_w_out, v_ln2_g, v_w_ffn_gate, v_w_ffn_up, v_w_ffn_down, v_lnf_g):
    weights = dict(ln1_g=ln1_g, w_in=w_in, pool_w=pool_w, pool_scale=pool_scale, conv_w=conv_w, conv_b=conv_b,
                   w_a=w_a, b_a=b_a, w_i=w_i, b_i=b_i, lam=lam, gn_pool_g=gn_pool_g, gn_lru_g=gn_lru_g,
                   w_out=w_out, ln2_g=ln2_g, w_ffn_gate=w_ffn_gate, w_ffn_up=w_ffn_up, w_ffn_down=w_ffn_down,
                   lnf_g=lnf_g)
    mom1 = dict(ln1_g=m_ln1_g, w_in=m_w_in, pool_w=m_pool_w, pool_scale=m_pool_scale, conv_w=m_conv_w,
                conv_b=m_conv_b, w_a=m_w_a, b_a=m_b_a, w_i=m_w_i, b_i=m_b_i, lam=m_lam, gn_pool_g=m_gn_pool_g,
                gn_lru_g=m_gn_lru_g, w_out=m_w_out, ln2_g=m_ln2_g, w_ffn_gate=m_w_ffn_gate,
                w_ffn_up=m_w_ffn_up, w_ffn_down=m_w_ffn_down, lnf_g=m_lnf_g)
    mom2 = dict(ln1_g=v_ln1_g, w_in=v_w_in, pool_w=v_pool_w, pool_scale=v_pool_scale, conv_w=v_conv_w,
                conv_b=v_conv_b, w_a=v_w_a, b_a=v_b_a, w_i=v_w_i, b_i=v_b_i, lam=v_lam, gn_pool_g=v_gn_pool_g,
                gn_lru_g=v_gn_lru_g, w_out=v_w_out, ln2_g=v_ln2_g, w_ffn_gate=v_w_ffn_gate,
                w_ffn_up=v_w_ffn_up, w_ffn_down=v_w_ffn_down, lnf_g=v_lnf_g)

    xs = x[0]
    target = loss_target[0]

    shard = dict(w_in=lambda a: a[0].T, w_ffn_gate=lambda a: a[0].T, w_ffn_up=lambda a: a[0].T,
                 w_out=lambda a: a[0], w_ffn_down=lambda a: a[0], conv_w=lambda a: a[0])
    unshard = dict(w_in=lambda a: a.T[None], w_ffn_gate=lambda a: a.T[None], w_ffn_up=lambda a: a.T[None],
                   w_out=lambda a: a[None], w_ffn_down=lambda a: a[None], conv_w=lambda a: a[None])

    gathered = ("w_in", "conv_w", "w_out", "w_ffn_gate", "w_ffn_up", "w_ffn_down")
    groups = ((0, 1), (2,), (3, 4, 5))
    sources = [shard[k](weights[k]) if k == "conv_w" else shard[k](weights[k]).astype(BF16) for k in gathered]
    my_index = 4 * lax.axis_index("x") + 2 * lax.axis_index("y") + lax.axis_index("c")
    lands = [lax.empty((N_DEV,) + a.shape, a.dtype) for a in sources]

    def first_hop(n, group=0):
        return _numbered(_fan_out(range(n), range(n, 2 * n), SAME_CORE + SIBLING + MYSELF, 0), group)

    start = []
    for g, members in enumerate(groups):
        start += _numbered(_fan_out(members, [6 + m for m in members], SAME_CORE + SIBLING + MYSELF, 0), g)
    sems, bufs, _ = _comm_call("gather_start", sources + lands,
                               start=(start, [sum(cp.group == g for cp in start) for g in range(len(groups))]))
    sources, lands = bufs[:6], bufs[6:]

    def gathered_group(tag, g, after):
        members = groups[g]
        n = len(members)
        relay = _numbered(_relay(range(n, 2 * n), SAME_CORE))
        relay_sems, bufs, _ = _comm_call("gather_relay_" + tag, [sources[m] for m in members] + [lands[m] for m in members],
                                         wait=(sems[g], first_hop(n)), start=(relay, (len(relay),)), after=after)
        _, bufs, _ = _comm_call("gather_wait_" + tag, bufs[n:], wait=(relay_sems[0], _numbered(_relay(range(n), SAME_CORE))))
        return bufs

    g_in, g_conv = gathered_group("in", 0, None)
    w_in_f = g_in.reshape(IN_WIDTH, D_MODEL)
    conv_w_f = _cols_from_stack(g_conv)

    wa_bd = _block_diag(w_a[0])
    wi_bd = _block_diag(w_i[0])
    lnf_row = lnf_g.reshape(1, D_MODEL)

    u_pool, u_lru, u_gate = _fwd_in(xs, ln1_g, w_in_f)
    y_pool, h, y_lru = _mixer_fwd(u_pool, u_lru, u_gate, pool_w[0], pool_scale, conv_w_f, conv_b,
                                  wa_bd, b_a, wi_bd, b_i, lam)
    (g_out,) = gathered_group("out", 1, y_pool)
    w_out_f = g_out.reshape(D_MODEL, D_MODEL)
    h1, n2 = _fwd_out(xs, y_pool, y_lru, gn_pool_g, gn_lru_g, w_out_f, ln2_g)
    g_gate, g_up, g_down = gathered_group("ffn", 2, n2)
    w_gate_f = g_gate.reshape(D_FF, D_MODEL)
    w_up_f = g_up.reshape(D_FF, D_MODEL)
    w_down_f = g_down.reshape(D_FF, D_MODEL)
    g_act, u_act, dh2, dh2b, d_lnf, sq = _ffn_fwd(h1, n2, target, lnf_row, w_gate_f, w_up_f, w_down_f)

    place = jnp.stack([lax.axis_index("c"), 2 * lax.axis_index("x") + lax.axis_index("y")]).astype(jnp.int32)

    d_gate, d_up, d_down, dn2 = _ffn_bwd(n2, dh2b, g_act, u_act, w_gate_f, w_up_f, w_down_f)
    ffn_stacks = [d.reshape(N_DEV, D_FF // N_DEV, D_MODEL) for d in (d_gate, d_up, d_down)]
    pair_lands = [lax.empty((N_DEV // 2,) + a.shape[1:], a.dtype) for a in ffn_stacks]
    pair_copies = _numbered(_to_sibling(range(3), range(3, 6)))
    sem, bufs, token = _comm_call("ffn_pair_start", ffn_stacks + pair_lands, start=(pair_copies, (len(pair_copies),)))
    dh1, dy_pool, dy_lru, d_out, d_ln2, d_gnp, d_gnl = _bwd_out(dn2, dh2, h1, y_pool, y_lru, gn_pool_g, gn_lru_g,
                                                                 w_out_f, ln2_g, token)
    _, bufs, _ = _comm_call("ffn_pair_wait", bufs, wait=(sem[0], pair_copies), after=dh1)
    ffn_sums, ffn_lands = _pair_sum(bufs[:3], bufs[3:], place)
    ffn_copies = _numbered(_to_chips(range(3), range(3, 6)))
    ffn_sem, ffn_bufs, token = _comm_call("ffn_chip_start", ffn_sums + ffn_lands, start=(ffn_copies, (len(ffn_copies),)))
    (du_pool, du_lru, du_gate, d_pw, d_ps, d_cw, d_cb, d_wa, d_ba, d_wi, d_bi, d_lam) = _mixer_bwd(
        u_pool, u_lru, u_gate, h, dy_pool, dy_lru, pool_w[0], pool_scale, conv_w_f, conv_b,
        wa_bd, b_a, wi_bd, b_i, lam, token)

    def direct(tag, stacks, wholes, after):
        sources = list(stacks) + list(wholes)
        n, n_st = len(sources), len(stacks)
        lands = [lax.empty(a.shape if i < n_st else (N_DEV,) + a.shape, a.dtype) for i, a in enumerate(sources)]
        copies = _numbered(_scatter(range(n_st), range(n, n + n_st), EVERYONE + MYSELF)
                           + _fan_out(range(n_st, n), range(n + n_st, 2 * n), EVERYONE + MYSELF, 0))
        sem, bufs, token = _comm_call(tag + "_start", sources + lands, start=(copies, (len(copies),)), after=after)
        return (tag, sem[0], bufs, copies), token

    def direct_finish(started, after):
        tag, sem, bufs, copies = started
        _, bufs, _ = _comm_call(tag + "_wait", bufs, wait=(sem, copies), after=after)
        return bufs[len(bufs) // 2:]

    vec_grads = dict(ln2_g=d_ln2, lnf_g=d_lnf, pool_scale=d_ps, conv_b=d_cb, b_a=d_ba, b_i=d_bi,
                     lam=d_lam, gn_pool_g=d_gnp, gn_lru_g=d_gnl)
    packed, packed_b = _pack_small([vec_grads[k] for k, _ in VECTORS], d_pw, d_wa, d_wi, d_cw, sq)
    small_started, token = direct("small", [d_out.reshape(N_DEV, D_MODEL // N_DEV, D_MODEL)], [packed, packed_b], None)
    grad_x, d_in, d_ln1 = _bwd_in(xs, dh1, du_pool, du_lru, du_gate, ln1_g, w_in_f, token)
    in_started, token = direct("in", [d_in.reshape(N_DEV, IN_WIDTH // N_DEV, D_MODEL)], [d_ln1], None)

    results = {}

    def reduce_adam(name, parts):
        outs = _reduce_adam(parts, shard[name](weights[name]), shard[name](mom1[name]), shard[name](mom2[name]),
                            "adam_" + name)
        results[name] = tuple(unshard[name](o) for o in outs)
        return outs[0]

    shard["ln1_g"] = unshard["ln1_g"] = lambda a: a
    _, ffn_bufs, _ = _comm_call("ffn_chip_wait", ffn_bufs, wait=(ffn_sem[0], ffn_copies), after=token)
    done = [reduce_adam(name, parts)
            for name, parts in zip(("w_ffn_gate", "w_ffn_up", "w_ffn_down"), ffn_bufs[3:])]
    r_out, r_small, r_small_b = direct_finish(small_started, done)
    done.append(reduce_adam("w_out", r_out))

    def as_row(a, width):
        return a.reshape(1, width)

    def as_matrix(a):
        return a.reshape(MATRIX_ROWS, SLAB)

    def as_heads(a):
        return a.reshape(2 * GATE_ROWS, HEAD)

    def heads_apart(g):
        return jnp.transpose(g.reshape(N_SLAB, HEAD, 2, HEAD), (0, 2, 1, 3)).reshape(2 * GATE_ROWS, HEAD)

    vec_out, pool_out, g_wa, g_wi, g_conv, loss_11 = _small_reduce_adam(
        r_small, r_small_b, [as_row(weights[k], w) for k, w in VECTORS], [as_row(mom1[k], w) for k, w in VECTORS],
        [as_row(mom2[k], w) for k, w in VECTORS], [as_matrix(t["pool_w"]) for t in (weights, mom1, mom2)])
    for (k, _), outs in zip(VECTORS, vec_out):
        results[k] = tuple(o.reshape(weights[k].shape) for o in outs)
    results["pool_w"] = tuple(o.reshape(pool_w.shape) for o in pool_out)
    my_columns = conv_w.shape[-1]
    plain_names = ("w_a", "w_i", "conv_w")
    plain_grads = [heads_apart(g) for g in (g_wa, g_wi)]
    plain_grads.append(lax.dynamic_slice_in_dim(g_conv, my_index * my_columns, my_columns, axis=1))
    views = (as_heads, as_heads, lambda a: a[0])
    plain_out = _plain_adam(plain_grads, *[[view(t[k]) for k, view in zip(plain_names, views)]
                                           for t in (weights, mom1, mom2)])
    for k, g, outs in zip(plain_names, plain_grads, plain_out):
        results[k] = tuple(o.reshape(weights[k].shape) for o in (g,) + outs)
    loss = loss_11[0, 0]
    r_in, r_ln1 = direct_finish(in_started, done + [plain_out[0][0], loss_11])
    reduce_adam("w_in", r_in)
    reduce_adam("ln1_g", r_ln1)

    order = ["ln1_g", "w_in", "pool_w", "pool_scale", "conv_w", "conv_b", "w_a", "b_a", "w_i", "b_i", "lam",
             "gn_pool_g", "gn_lru_g", "w_out", "ln2_g", "w_ffn_gate", "w_ffn_up", "w_ffn_down", "lnf_g"]
    return (loss, grad_x[None],
            *[results[k][0] for k in order], *[results[k][1] for k in order],
            *[results[k][2] for k in order], *[results[k][3] for k in order])
```

```python
from typing import Any, NamedTuple

import jax
import jax.numpy as jnp
from jax import lax
from jax.experimental import pallas as pl
from jax.experimental.pallas import tpu as pltpu

F32 = jnp.float32
BF16 = jnp.bfloat16

N_DEV = 8
D_MODEL = 1024
POOL_WIDTH = 512
LRU_WIDTH = 512
IN_WIDTH = 1536
D_FF = 2816
N_SLAB = 4
SLAB = 128
CONV_WIDTH = 4
LRU_C = 8.0
EPS = 1e-6
HALO = 16
FF_CHUNK = 256

ADAM_LR = 0.001
ADAM_B1 = 0.9
ADAM_B2 = 0.999
ADAM_EPS = 1e-08
ADAM_WD = 0.01
ADAM_STEP = 10

MIB = 1 << 20
MESH = pl.DeviceIdType.MESH


def _params(vmem_mib, n_axes=0):
    sem = ("arbitrary",) * n_axes if n_axes else None
    return pltpu.CompilerParams(dimension_semantics=sem, vmem_limit_bytes=vmem_mib * MIB)


def _in_hbm(*arrays):
    return [pltpu.with_memory_space_constraint(a, pltpu.HBM) for a in arrays]


def _mm(a, b):
    return jnp.dot(a, b, preferred_element_type=F32)


def _mm_nt(a, b):
    return lax.dot_general(a, b, (((1,), (1,)), ((), ())), preferred_element_type=F32)


def _mm_tn(a, b):
    return lax.dot_general(a, b, (((0,), (0,)), ((), ())), preferred_element_type=F32)


def _rms(x, g):
    rstd = lax.rsqrt(jnp.mean(x * x, axis=-1, keepdims=True) + EPS)
    xhat = x * rstd
    return xhat * g, xhat, rstd


def _rms_bwd(dy, xhat, rstd, g):
    gy = dy * g
    dx = rstd * (gy - xhat * jnp.mean(gy * xhat, axis=-1, keepdims=True))
    return dx, jnp.sum(dy * xhat, axis=0, keepdims=True)


def _gelu(z):
    t = jnp.tanh(0.7978845608028654 * (z + 0.044715 * z * z * z))
    return 0.5 * z * (1.0 + t), t


def _gelu_grad(z, t):
    return 0.5 * (1.0 + t) + 0.5 * z * (1.0 - t * t) * 0.7978845608028654 * (1.0 + 3.0 * 0.044715 * z * z)


def _softplus_neg(lam):
    x = -lam
    e = jnp.exp(-jnp.abs(x))
    u = 1.0 + e
    l1p = jnp.where(u == 1.0, e, jnp.log(u) * e / (u - 1.0))
    return jnp.maximum(x, 0.0) + l1p


def _one_minus_square(la, a):
    x = 2.0 * la
    series = -x * (1.0 + x * (0.5 + x * (1.0 / 6.0 + x * (1.0 / 24.0))))
    return jnp.where(x > -0.06, series, 1.0 - a * a)


def _sigmoid(x):
    return 0.5 * jnp.tanh(0.5 * x) + 0.5


def _down(v, d):
    return pltpu.roll(v, d, 0)


def _up(v, d):
    return pltpu.roll(v, v.shape[0] - d, 0)


def _slab_major(s):
    return pltpu.HBM((N_SLAB, s, SLAB), F32)


def _slabs_spec(tb):
    return pl.BlockSpec((N_SLAB, tb, SLAB), lambda i: (0, i, 0))


def _read_slabs(ref):
    return jnp.concatenate([ref[k] for k in range(N_SLAB)], axis=1)


def _write_slabs(ref, value):
    for k in range(N_SLAB):
        ref[k] = value[:, k * SLAB:(k + 1) * SLAB]


def _token_block(s, most=512):
    for rows in (most, 512, 256):
        if rows <= most and s % rows == 0 and s > rows:
            return rows
    return s


def _time_chunk(s):
    return 512 if s % 512 == 0 and s > 512 else 256 if s % 256 == 0 else s


def _fwd_in(x, ln1_g, w_in_t):
    s = x.shape[0]
    tb = _token_block(s, 1024)

    def body(x_ref, g_ref, w_ref, up_ref, ul_ref, ug_ref):
        n, _, _ = _rms(x_ref[...], g_ref[...])
        proj = _mm_nt(n.astype(BF16), w_ref[...])
        _write_slabs(up_ref, proj[:, :POOL_WIDTH])
        _write_slabs(ul_ref, proj[:, POOL_WIDTH:POOL_WIDTH + LRU_WIDTH])
        _write_slabs(ug_ref, proj[:, POOL_WIDTH + LRU_WIDTH:])

    out = _slab_major(s)
    return pl.pallas_call(
        body, name="fwd_in", grid=(s // tb,),
        in_specs=[pl.BlockSpec((tb, D_MODEL), lambda i: (i, 0)),
                  pl.BlockSpec((1, D_MODEL), lambda i: (0, 0)),
                  pl.BlockSpec((IN_WIDTH, D_MODEL), lambda i: (0, 0))],
        out_specs=[_slabs_spec(tb)] * 3,
        out_shape=[out, out, out],
        compiler_params=_params(40, 1),
    )(*_in_hbm(x, ln1_g, w_in_t))


def _pool_denominator(t0, row, window):
    return jnp.minimum((t0 + row + 1).astype(F32), window)


def _causal_window(ext, deeper):
    s = ext + _down(ext, 1)
    s = s + deeper[0] * _down(s, 2)
    s = s + deeper[1] * _down(s, 4)
    s = s + deeper[2] * _down(s, 8)
    return s[HALO:]


def _anticausal_window(ext, deeper, rows):
    s = ext + _up(ext, 1)
    s = s + deeper[0] * _up(s, 2)
    s = s + deeper[1] * _up(s, 4)
    s = s + deeper[2] * _up(s, 8)
    return s[:rows]


def _conv(taps, cw, cb):
    return cw[3] * taps[0] + cw[2] * taps[1] + cw[1] * taps[2] + cw[0] * taps[3] + cb


def _lru_gates(xc, wa, ba, wi, bi, sp):
    xb = xc.astype(BF16)
    r = _sigmoid(_mm(xb, wa) + ba)
    i = _sigmoid(_mm(xb, wi) + bi)
    la = (-LRU_C) * r * sp
    a = jnp.exp(la)
    mult = jnp.sqrt(jnp.maximum(_one_minus_square(la, a), 0.0))
    return xb, r, i, a, mult


SUBLANES = 8


def _scan_pads(a_pad, b_pad, rows, causal):
    spare = slice(0, SUBLANES) if causal else slice(rows, rows + SUBLANES)
    a_pad[spare, :] = jnp.ones((SUBLANES, SLAB), F32)
    b_pad[spare, :] = jnp.zeros((SUBLANES, SLAB), F32)


def _scan_causal(a, b, h_prev, a_pad, b_pad, rows):
    d = 1
    while d < min(SUBLANES, rows):
        a_pad[SUBLANES:, :] = a
        b_pad[SUBLANES:, :] = b
        b = a * b_pad[SUBLANES - d:SUBLANES - d + rows, :] + b
        a = a * a_pad[SUBLANES - d:SUBLANES - d + rows, :]
        d *= 2
    while d < rows:
        b = jnp.concatenate([b[:d], a[d:] * b[:-d] + b[d:]], axis=0)
        a = jnp.concatenate([a[:d], a[d:] * a[:-d]], axis=0)
        d *= 2
    return b + a * h_prev


def _scan_anticausal(a, b, l_next, a_pad, b_pad, rows):
    d = 1
    while d < min(SUBLANES, rows):
        a_pad[:rows, :] = a
        b_pad[:rows, :] = b
        b = a * b_pad[d:d + rows, :] + b
        a = a * a_pad[d:d + rows, :]
        d *= 2
    while d < rows:
        b = jnp.concatenate([a[:-d] * b[d:] + b[:-d], b[-d:]], axis=0)
        a = jnp.concatenate([a[:-d] * a[d:], a[-d:]], axis=0)
        d *= 2
    return b + a * l_next


def _slab_scalars():
    slab = pl.program_id(0)
    deeper = [jnp.where(slab > k, 1.0, 0.0).astype(F32) for k in range(N_SLAB - 1)]
    window = jnp.left_shift(jnp.int32(2), slab).astype(F32)
    inverse = jnp.where(slab == 0, 0.5, jnp.where(slab == 1, 0.25, jnp.where(slab == 2, 0.125, 0.0625))).astype(F32)
    return deeper, (window, inverse)


def _window_mean(total, t0, row, window, at_start):
    if at_start:
        return total / _pool_denominator(t0, row, window[0])
    return total * window[1]


def _slab_specs(s):
    seq = pl.BlockSpec((None, s, SLAB), lambda k: (k, 0, 0))
    mat = pl.BlockSpec((1, SLAB, SLAB), lambda k: (k, 0, 0))
    vec = pl.BlockSpec((1, SLAB), lambda k: (0, k))
    taps = pl.BlockSpec((CONV_WIDTH, SLAB), lambda k: (0, k))
    return seq, mat, vec, taps


def _mixer_fwd(u_pool, u_lru, u_gate, pool_w, pool_scale, conv_w, conv_b, wa_bd, b_a, wi_bd, b_i, lam):
    s = u_pool.shape[1]
    tc = _time_chunk(s)
    n_chunks = s // tc

    def body(up_ref, ul_ref, ug_ref, pw_ref, ps_ref, cw_ref, cb_ref, wa_ref, ba_ref, wi_ref, bi_ref, lam_ref,
             yp_ref, h_ref, yl_ref, shift_ref, a_pad, b_pad):
        _scan_pads(a_pad, b_pad, tc, causal=True)
        deeper, window = _slab_scalars()
        pw = pw_ref[0].astype(BF16)
        wa = wa_ref[0].astype(BF16)
        wi = wi_ref[0].astype(BF16)
        ps, cb, ba, bi = ps_ref[...], cb_ref[...], ba_ref[...], bi_ref[...]
        cw = [cw_ref[k:k + 1, :] for k in range(CONV_WIDTH)]
        sp = _softplus_neg(lam_ref[...])
        row = lax.broadcasted_iota(jnp.int32, (tc, SLAB), 0)

        def chunk(t0, ext_p, ext_l, h_prev, at_start=False):
            rows = pl.ds(t0, tc)
            d = _window_mean(_causal_window(ext_p, deeper), t0, row, window, at_start) - ext_p[HALO:]
            yp_ref[rows, :] = _mm(d.astype(BF16), pw) * ps
            shift_ref[...] = ext_l
            xc = _conv([shift_ref[HALO - k:HALO - k + tc, :] for k in range(CONV_WIDTH)], cw, cb)
            _, _, i, a, mult = _lru_gates(xc, wa, ba, wi, bi, sp)
            h = _scan_causal(a, mult * (i * xc), h_prev, a_pad, b_pad, tc)
            h_ref[rows, :] = h
            yl_ref[rows, :] = h * _gelu(ug_ref[rows, :])[0]
            return h[tc - 1:tc, :]

        pad = jnp.zeros((HALO, SLAB), F32)
        h0 = chunk(0, jnp.concatenate([pad, up_ref[pl.ds(0, tc), :]], axis=0),
                   jnp.concatenate([pad, ul_ref[pl.ds(0, tc), :]], axis=0), jnp.zeros((1, SLAB), F32), at_start=True)

        def step(c, h_prev):
            t0 = pl.multiple_of(c * tc, tc)
            ext = pl.ds(pl.multiple_of(c * tc - HALO, HALO), tc + HALO)
            return chunk(t0, up_ref[ext, :], ul_ref[ext, :], h_prev)

        lax.fori_loop(1, n_chunks, step, h0)

    seq, mat, vec, taps = _slab_specs(s)
    out = _slab_major(s)
    return pl.pallas_call(
        body, name="mixer_fwd", grid=(N_SLAB,),
        in_specs=[seq, seq, seq, mat, vec, taps, vec, mat, vec, mat, vec, vec],
        out_specs=[seq, seq, seq], out_shape=[out, out, out],
        scratch_shapes=[pltpu.VMEM((tc + HALO, SLAB), F32), pltpu.VMEM((tc + SUBLANES, SLAB), F32),
                        pltpu.VMEM((tc + SUBLANES, SLAB), F32)],
        compiler_params=_params(48, 1),
    )(*_in_hbm(u_pool, u_lru, u_gate, pool_w, pool_scale, conv_w, conv_b, wa_bd, b_a, wi_bd, b_i, lam))


def _fwd_out(x, y_pool, y_lru, gn_pool_g, gn_lru_g, w_out, ln2_g):
    s = x.shape[0]
    tb = _token_block(s, 1024)

    def body(x_ref, yp_ref, yl_ref, gp_ref, gl_ref, w_ref, g2_ref, h1_ref, n2_ref):
        mp, _, _ = _rms(_read_slabs(yp_ref), gp_ref[...])
        ml, _, _ = _rms(_read_slabs(yl_ref), gl_ref[...])
        h1 = x_ref[...] + _mm(mp.astype(BF16), w_ref[:POOL_WIDTH, :]) + _mm(ml.astype(BF16), w_ref[POOL_WIDTH:, :])
        h1_ref[...] = h1
        n2_ref[...] = _rms(h1, g2_ref[...])[0].astype(BF16)

    row = pl.BlockSpec((tb, D_MODEL), lambda i: (i, 0))
    half = _slabs_spec(tb)
    return pl.pallas_call(
        body, name="fwd_out", grid=(s // tb,),
        in_specs=[row, half, half, pl.BlockSpec((1, 512), lambda i: (0, 0)), pl.BlockSpec((1, 512), lambda i: (0, 0)),
                  pl.BlockSpec((D_MODEL, D_MODEL), lambda i: (0, 0)), pl.BlockSpec((1, D_MODEL), lambda i: (0, 0))],
        out_specs=[row, row],
        out_shape=[pltpu.HBM((s, D_MODEL), F32), pltpu.HBM((s, D_MODEL), BF16)],
        compiler_params=_params(40, 1),
    )(*_in_hbm(x, y_pool, y_lru, gn_pool_g, gn_lru_g, w_out, ln2_g))


def _ffn_fwd(h1, n2, target, lnf_g, w_gate, w_up, w_down):
    s = h1.shape[0]
    tb = 512
    sub = 256
    n_ff = D_FF // FF_CHUNK

    def body(h1_ref, n2_ref, t_ref, gf_ref, wg_hbm, wu_hbm, wd_hbm,
             g_ref, u_ref, dh_ref, dhb_ref, dgf_ref, sq_ref, wg, wu, wd, sem):
        @pl.when(pl.program_id(0) == 0)
        def _():
            loads = [pltpu.make_async_copy(src, dst, sem.at[k])
                     for k, (src, dst) in enumerate(((wg_hbm, wg), (wu_hbm, wu), (wd_hbm, wd)))]
            for cp in loads:
                cp.start()
            for cp in loads:
                cp.wait()
            dgf_ref[...] = jnp.zeros_like(dgf_ref)
            sq_ref[...] = jnp.zeros_like(sq_ref)

        n2v = n2_ref[...]
        acc = jnp.zeros((tb, D_MODEL), F32)
        for c in range(n_ff):
            cols = slice(c * FF_CHUNK, (c + 1) * FF_CHUNK)
            g = _mm_nt(n2v, wg[cols, :])
            u = _mm_nt(n2v, wu[cols, :])
            g_ref[:, cols] = g.astype(BF16)
            u_ref[:, cols] = u.astype(BF16)
            act = g * jax.nn.sigmoid(g) * u
            acc = acc + _mm(act.astype(BF16), wd[cols, :])
        gf = gf_ref[...]
        for r in range(tb // sub):
            rows = slice(r * sub, (r + 1) * sub)
            y, xhat, rstd = _rms(h1_ref[rows, :] + acc[rows, :], gf)
            err = y - t_ref[rows, :]
            sq_ref[...] += jnp.sum(err * err, axis=0, keepdims=True)
            dh2, dgf = _rms_bwd(err * (1.0 / D_MODEL), xhat, rstd, gf)
            dgf_ref[...] += dgf
            dh_ref[rows, :] = dh2
            dhb_ref[rows, :] = dh2.astype(BF16)

    row = pl.BlockSpec((tb, D_MODEL), lambda i: (i, 0))
    ff = pl.BlockSpec((tb, D_FF), lambda i: (i, 0))
    vec = pl.BlockSpec((1, D_MODEL), lambda i: (0, 0))
    anyspace = pl.BlockSpec(memory_space=pl.ANY)
    return pl.pallas_call(
        body, name="ffn_fwd", grid=(s // tb,),
        in_specs=[row, row, row, vec, anyspace, anyspace, anyspace],
        out_specs=[ff, ff, row, row, vec, vec],
        out_shape=[pltpu.HBM((s, D_FF), BF16), pltpu.HBM((s, D_FF), BF16),
                   pltpu.HBM((s, D_MODEL), F32), pltpu.HBM((s, D_MODEL), BF16),
                   jax.ShapeDtypeStruct((1, D_MODEL), F32), jax.ShapeDtypeStruct((1, D_MODEL), F32)],
        scratch_shapes=[pltpu.VMEM((D_FF, D_MODEL), BF16), pltpu.VMEM((D_FF, D_MODEL), BF16),
                        pltpu.VMEM((D_FF, D_MODEL), BF16), pltpu.SemaphoreType.DMA((3,))],
        compiler_params=_params(60, 1),
    )(*_in_hbm(h1, n2, target, lnf_g, w_gate, w_up, w_down))


def _ffn_bwd(n2, dh2b, g, u, w_gate_t, w_up_t, w_down):
    s = n2.shape[0]
    tb = min(1024, s)
    n_ff = D_FF // FF_CHUNK
    n_tb = s // tb

    def body(n2_ref, dh_ref, g_ref, u_ref, wg_ref, wu_ref, wd_ref, dwg_ref, dwu_ref, dwd_ref, dn2_ref,
             dn2_acc, acc_g, acc_u, acc_d):
        j = pl.program_id(0)
        t = pl.program_id(1)

        @pl.when(t == 0)
        def _():
            acc_g[...] = jnp.zeros_like(acc_g)
            acc_u[...] = jnp.zeros_like(acc_u)
            acc_d[...] = jnp.zeros_like(acc_d)

        n2v = n2_ref[...]
        dh = dh_ref[...]
        gv = g_ref[...].astype(F32)
        uv = u_ref[...].astype(F32)
        sg = jax.nn.sigmoid(gv)
        silu = gv * sg
        dact = _mm_nt(dh, wd_ref[...])
        dub = (dact * silu).astype(BF16)
        dgb = (dact * uv * (sg * (1.0 + gv * (1.0 - sg)))).astype(BF16)
        acc_d[...] += _mm_tn((silu * uv).astype(BF16), dh)
        acc_g[...] += _mm_tn(dgb, n2v)
        acc_u[...] += _mm_tn(dub, n2v)
        part = _mm(jnp.concatenate([dgb, dub], axis=1), jnp.concatenate([wg_ref[...], wu_ref[...]], axis=0))
        rows = pl.ds(pl.multiple_of(t * tb, tb), tb)

        @pl.when(t == n_tb - 1)
        def _():
            dwg_ref[...] = acc_g[...].astype(BF16)
            dwu_ref[...] = acc_u[...].astype(BF16)
            dwd_ref[...] = acc_d[...].astype(BF16)

        @pl.when(j == 0)
        def _():
            dn2_acc[rows, :] = part

        @pl.when(jnp.logical_and(j > 0, j < n_ff - 1))
        def _():
            dn2_acc[rows, :] += part

        @pl.when(j == n_ff - 1)
        def _():
            dn2_ref[...] = dn2_acc[rows, :] + part

    row = pl.BlockSpec((tb, D_MODEL), lambda j, t: (t, 0))
    act = pl.BlockSpec((tb, FF_CHUNK), lambda j, t: (t, j))
    w_row = pl.BlockSpec((FF_CHUNK, D_MODEL), lambda j, t: (j, 0))
    last = pl.BlockSpec((tb, D_MODEL), lambda j, t: (jnp.where(j == n_ff - 1, t, 0), 0))
    grad = pltpu.HBM((D_FF, D_MODEL), BF16)
    chunk_acc = pltpu.VMEM((FF_CHUNK, D_MODEL), F32)
    return pl.pallas_call(
        body, name="ffn_bwd", grid=(n_ff, n_tb),
        in_specs=[row, row, act, act, w_row, w_row, w_row],
        out_specs=[w_row, w_row, w_row, last],
        out_shape=[grad, grad, grad, pltpu.HBM((s, D_MODEL), F32)],
        scratch_shapes=[pltpu.VMEM((s, D_MODEL), F32), chunk_acc, chunk_acc, chunk_acc],
        compiler_params=_params(56, 2),
    )(*_in_hbm(n2, dh2b, g, u, w_gate_t, w_up_t, w_down))


def _bwd_out(dn2, dh2, h1, y_pool, y_lru, gn_pool_g, gn_lru_g, w_out, ln2_g, after):
    s = h1.shape[0]
    tb = _token_block(s)

    def body(dn2_ref, dh2_ref, h1_ref, yp_ref, yl_ref, gp_ref, gl_ref, w_ref, g2_ref, _after,
             dh1_ref, dyp_ref, dyl_ref, dwb_ref, dg2_ref, dgp_ref, dgl_ref, dw_ref):
        @pl.when(pl.program_id(0) == 0)
        def _():
            dw_ref[...] = jnp.zeros_like(dw_ref)
            dg2_ref[...] = jnp.zeros_like(dg2_ref)
            dgp_ref[...] = jnp.zeros_like(dgp_ref)
            dgl_ref[...] = jnp.zeros_like(dgl_ref)

        g2 = g2_ref[...]
        _, xhat2, rstd2 = _rms(h1_ref[...], g2)
        dres, dg2 = _rms_bwd(dn2_ref[...], xhat2, rstd2, g2)
        dg2_ref[...] += dg2
        dh1 = dh2_ref[...] + dres
        dh1_ref[...] = dh1
        dh1b = dh1.astype(BF16)
        gp, gl = gp_ref[...], gl_ref[...]
        mp, xhat_p, rstd_p = _rms(_read_slabs(yp_ref), gp)
        ml, xhat_l, rstd_l = _rms(_read_slabs(yl_ref), gl)
        dw_ref[:POOL_WIDTH, :] += _mm_tn(mp.astype(BF16), dh1b)
        dw_ref[POOL_WIDTH:, :] += _mm_tn(ml.astype(BF16), dh1b)
        dyp, dgp = _rms_bwd(_mm_nt(dh1b, w_ref[:POOL_WIDTH, :]), xhat_p, rstd_p, gp)
        dyl, dgl = _rms_bwd(_mm_nt(dh1b, w_ref[POOL_WIDTH:, :]), xhat_l, rstd_l, gl)
        _write_slabs(dyp_ref, dyp)
        _write_slabs(dyl_ref, dyl)
        dgp_ref[...] += dgp
        dgl_ref[...] += dgl

        @pl.when(pl.program_id(0) == s // tb - 1)
        def _():
            dwb_ref[...] = dw_ref[...].astype(BF16)

    row = pl.BlockSpec((tb, D_MODEL), lambda i: (i, 0))
    half = _slabs_spec(tb)
    vec = pl.BlockSpec((1, D_MODEL), lambda i: (0, 0))
    hvec = pl.BlockSpec((1, 512), lambda i: (0, 0))
    mat = pl.BlockSpec((D_MODEL, D_MODEL), lambda i: (0, 0))
    return pl.pallas_call(
        body, name="bwd_out", grid=(s // tb,),
        in_specs=[row, row, row, half, half, hvec, hvec, mat, vec, pl.BlockSpec(memory_space=pl.ANY)],
        out_specs=[row, half, half, mat, vec, hvec, hvec],
        out_shape=[pltpu.HBM((s, D_MODEL), F32), _slab_major(s),
                   _slab_major(s), pltpu.HBM((D_MODEL, D_MODEL), BF16),
                   jax.ShapeDtypeStruct((1, D_MODEL), F32), jax.ShapeDtypeStruct((1, 512), F32),
                   jax.ShapeDtypeStruct((1, 512), F32)],
        scratch_shapes=[pltpu.VMEM((D_MODEL, D_MODEL), F32)],
        compiler_params=_params(48, 1),
    )(*_in_hbm(dn2, dh2, h1, y_pool, y_lru, gn_pool_g, gn_lru_g, w_out, ln2_g), after)


def _mixer_bwd(u_pool, u_lru, u_gate, h, dy_pool, dy_lru,
               pool_w, pool_scale, conv_w, conv_b, wa_bd, b_a, wi_bd, b_i, lam, after):
    s = u_pool.shape[1]
    tc = _time_chunk(s)
    n_chunks = s // tc

    def body(up_ref, ul_ref, ug_ref, h_ref, dyp_ref, dyl_ref,
             pw_ref, ps_ref, cw_ref, cb_ref, wa_ref, ba_ref, wi_ref, bi_ref, lam_ref, _after,
             dup_ref, dul_ref, dug_ref, dpw_ref, dps_ref, dcw_ref, dcb_ref, dwa_ref, dba_ref, dwi_ref, dbi_ref, dlam_ref,
             a_pad, b_pad, u_shift, h_shift, a_shift, d_shift):
        _scan_pads(a_pad, b_pad, tc, causal=False)
        deeper, window = _slab_scalars()
        pw = pw_ref[0].astype(BF16)
        wa = wa_ref[0].astype(BF16)
        wi = wi_ref[0].astype(BF16)
        ps, cb, ba, bi = ps_ref[...], cb_ref[...], ba_ref[...], bi_ref[...]
        cw = [cw_ref[k:k + 1, :] for k in range(CONV_WIDTH)]
        lam_v = lam_ref[...]
        sp = _softplus_neg(lam_v)
        row = lax.broadcasted_iota(jnp.int32, (tc, SLAB), 0)
        for ref in (dpw_ref, dps_ref, dcw_ref, dcb_ref, dwa_ref, dba_ref, dwi_ref, dbi_ref, dlam_ref):
            ref[...] = jnp.zeros_like(ref)

        def chunk(t0, ext_p, ext_l, ext_h, carry, at_start=False):
            l_next, a_next, dxc_next, ddn_next = carry
            rows = pl.ds(t0, tc)
            u_shift[...] = ext_l
            taps = [u_shift[HALO - k:HALO - k + tc, :] for k in range(CONV_WIDTH)]
            xc = _conv(taps, cw, cb)
            xb, r, i, a, mult = _lru_gates(xc, wa, ba, wi, bi, sp)
            hv = ext_h[HALO:]
            h_shift[...] = ext_h
            h_before = h_shift[HALO - 1:HALO - 1 + tc, :]
            ug = ug_ref[rows, :]
            dyl = dyl_ref[rows, :]
            gel, th = _gelu(ug)
            dug_ref[rows, :] = dyl * hv * _gelu_grad(ug, th)
            a_shift[:tc, :] = a
            a_shift[tc:, :] = jnp.broadcast_to(a_next, (SUBLANES, SLAB))
            a_after = a_shift[1:1 + tc, :]
            l = _scan_anticausal(a_after, dyl * gel, l_next, a_pad, b_pad, tc)
            dmult = l * (i * xc)
            di = l * mult * xc
            dxc = l * mult * i
            dla = (l * h_before) * a - jnp.where(mult > 0.0, dmult * (a * a) / mult, 0.0)
            dlam_ref[...] += jnp.sum(dla * r, axis=0, keepdims=True)
            dpa = (dla * ((-LRU_C) * sp)) * (r * (1.0 - r))
            dpi = di * (i * (1.0 - i))
            dpab = dpa.astype(BF16)
            dpib = dpi.astype(BF16)
            dwa_ref[0] += _mm_tn(xb, dpab)
            dwi_ref[0] += _mm_tn(xb, dpib)
            dba_ref[...] += jnp.sum(dpa, axis=0, keepdims=True)
            dbi_ref[...] += jnp.sum(dpi, axis=0, keepdims=True)
            dxc = dxc + _mm_nt(dpab, wa) + _mm_nt(dpib, wi)
            d_shift[:tc, :] = dxc
            d_shift[tc:, :] = dxc_next
            dul_ref[rows, :] = (cw[3] * dxc + cw[2] * d_shift[1:1 + tc, :]
                                + cw[1] * d_shift[2:2 + tc, :] + cw[0] * d_shift[3:3 + tc, :])
            for k in range(CONV_WIDTH):
                dcw_ref[k:k + 1, :] += jnp.sum(dxc * taps[CONV_WIDTH - 1 - k], axis=0, keepdims=True)
            dcb_ref[...] += jnp.sum(dxc, axis=0, keepdims=True)
            db = (_window_mean(_causal_window(ext_p, deeper), t0, row, window, at_start) - ext_p[HALO:]).astype(BF16)
            dyp = dyp_ref[rows, :]
            dps_ref[...] += jnp.sum(dyp * _mm(db, pw), axis=0, keepdims=True)
            dys = (dyp * ps).astype(BF16)
            dpw_ref[0] += _mm_tn(db, dys)
            dd = _mm_nt(dys, pw)
            ddn = _window_mean(dd, t0, row, window, at_start)
            ext_q = jnp.concatenate([ddn, ddn_next], axis=0)
            dup_ref[rows, :] = _anticausal_window(ext_q, deeper, tc) - dd
            return l[0:1, :], a[0:1, :], dxc[0:8, :], ddn[0:HALO, :]

        def step(k, carry):
            c = n_chunks - 1 - k
            t0 = pl.multiple_of(c * tc, tc)
            ext = pl.ds(pl.multiple_of(c * tc - HALO, HALO), tc + HALO)
            return chunk(t0, up_ref[ext, :], ul_ref[ext, :], h_ref[ext, :], carry)

        carry = (jnp.zeros((1, SLAB), F32), jnp.zeros((1, SLAB), F32),
                 jnp.zeros((8, SLAB), F32), jnp.zeros((HALO, SLAB), F32))
        carry = lax.fori_loop(0, n_chunks - 1, step, carry)
        pad = jnp.zeros((HALO, SLAB), F32)
        first = pl.ds(0, tc)
        chunk(0, jnp.concatenate([pad, up_ref[first, :]], axis=0), jnp.concatenate([pad, ul_ref[first, :]], axis=0),
              jnp.concatenate([pad, h_ref[first, :]], axis=0), carry, at_start=True)
        dlam_ref[...] = dlam_ref[...] * (LRU_C * jax.nn.sigmoid(-lam_v))

    seq, mat, vec, taps = _slab_specs(s)
    full = _slab_major(s)
    mats = jax.ShapeDtypeStruct((N_SLAB, SLAB, SLAB), F32)
    vecs = jax.ShapeDtypeStruct((1, 512), F32)
    return pl.pallas_call(
        body, name="mixer_bwd", grid=(N_SLAB,),
        in_specs=[seq] * 6 + [mat, vec, taps, vec, mat, vec, mat, vec, vec, pl.BlockSpec(memory_space=pl.ANY)],
        out_specs=[seq, seq, seq, mat, vec, taps, vec, mat, vec, mat, vec, vec],
        out_shape=[full, full, full, mats, vecs, jax.ShapeDtypeStruct((CONV_WIDTH, 512), F32), vecs,
                   mats, vecs, mats, vecs, vecs],
        scratch_shapes=[pltpu.VMEM((tc + SUBLANES, SLAB), F32), pltpu.VMEM((tc + SUBLANES, SLAB), F32),
                        pltpu.VMEM((tc + HALO, SLAB), F32), pltpu.VMEM((tc + HALO, SLAB), F32),
                        pltpu.VMEM((tc + SUBLANES, SLAB), F32), pltpu.VMEM((tc + SUBLANES, SLAB), F32)],
        compiler_params=_params(56, 1),
    )(*_in_hbm(u_pool, u_lru, u_gate, h, dy_pool, dy_lru, pool_w, pool_scale, conv_w, conv_b, wa_bd, b_a, wi_bd,
               b_i, lam), after)


def _bwd_in(x, dh1, du_pool, du_lru, du_gate, ln1_g, w_in_t, after):
    s = x.shape[0]
    tb = _token_block(s)

    def body(x_ref, dh1_ref, dup_ref, dul_ref, dug_ref, g_ref, w_ref, _after, dx_ref, dwb_ref, dg_ref, dw_ref):
        @pl.when(pl.program_id(0) == 0)
        def _():
            dw_ref[...] = jnp.zeros_like(dw_ref)
            dg_ref[...] = jnp.zeros_like(dg_ref)

        g1 = g_ref[...]
        n, xhat, rstd = _rms(x_ref[...], g1)
        nb = n.astype(BF16)
        db = jnp.concatenate([_read_slabs(ref).astype(BF16) for ref in (dup_ref, dul_ref, dug_ref)], axis=1)
        dw_ref[...] += _mm_tn(db, nb)
        dx, dg1 = _rms_bwd(_mm(db, w_ref[...]), xhat, rstd, g1)
        dx_ref[...] = dh1_ref[...] + dx
        dg_ref[...] += dg1

        @pl.when(pl.program_id(0) == s // tb - 1)
        def _():
            dwb_ref[...] = dw_ref[...].astype(BF16)

    row = pl.BlockSpec((tb, D_MODEL), lambda i: (i, 0))
    half = _slabs_spec(tb)
    vec = pl.BlockSpec((1, D_MODEL), lambda i: (0, 0))
    mat = pl.BlockSpec((IN_WIDTH, D_MODEL), lambda i: (0, 0))
    return pl.pallas_call(
        body, name="bwd_in", grid=(s // tb,),
        in_specs=[row, row, half, half, half, vec, mat, pl.BlockSpec(memory_space=pl.ANY)],
        out_specs=[row, mat, vec],
        out_shape=[pltpu.HBM((s, D_MODEL), F32), pltpu.HBM((IN_WIDTH, D_MODEL), BF16),
                   jax.ShapeDtypeStruct((1, D_MODEL), F32)],
        scratch_shapes=[pltpu.VMEM((IN_WIDTH, D_MODEL), F32)],
        compiler_params=_params(48, 1),
    )(*_in_hbm(x, dh1, du_pool, du_lru, du_gate, ln1_g, w_in_t), after)


def _mesh_position():
    x, y, c = lax.axis_index("x"), lax.axis_index("y"), lax.axis_index("c")
    return x, y, c, 4 * x + 2 * y + c


def _peer(x, y, c, p):
    px = 1 - x if p & 4 else x
    py = 1 - y if p & 2 else y
    pc = 1 - c if p & 1 else c
    return (px, py, pc), 4 * px + 2 * py + pc


HBM_SPEC = pl.BlockSpec(memory_space=pltpu.HBM)
SEM_SPEC = pl.BlockSpec(memory_space=pltpu.SEMAPHORE)
DATAFLOW = pltpu.SideEffectType.DATAFLOW_SIDE_EFFECTING


class Copy(NamedTuple):
    src: int
    src_at: Any
    dst: int
    dst_at: Any
    peer: int
    group: int
    slot: int


SIBLING = (1,)
SAME_CORE = (2, 4, 6)
EVERYONE = tuple(range(1, N_DEV))
MYSELF = (0,)


def _same(index):
    return index


def _chip(index):
    return jnp.right_shift(index, 1)


def _fan_out(srcs, lands, peers, group):
    return [Copy(s, None, d, _same, p, group, N_DEV * i + p) for i, (s, d) in enumerate(zip(srcs, lands)) for p in peers]


def _scatter(stacks, lands, peers):
    return [Copy(s, lambda me, p=p: jnp.bitwise_xor(me, p), d, _same, p, 0, 0)
            for s, d in zip(stacks, lands) for p in peers]


def _numbered(copies, group=0):
    return [cp._replace(group=group, slot=i) for i, cp in enumerate(copies)]


def _relay(lands, peers):
    return [Copy(b, lambda s, q=q: jnp.bitwise_xor(s, q), b, lambda s, q=q: jnp.bitwise_xor(s, q), 1, 0, N_DEV * i + q)
            for i, b in enumerate(lands) for q in peers]


def _to_sibling(stacks, lands):
    return [Copy(s, lambda me, k=k: 2 * k + 1 - jnp.bitwise_and(me, 1), d, lambda me, k=k: k, 1, 0, 4 * i + k)
            for i, (s, d) in enumerate(zip(stacks, lands)) for k in range(N_DEV // 2)]


def _to_chips(sums, lands):
    return [Copy(s, lambda me, p=p: jnp.bitwise_xor(_chip(me), p // 2), d, _chip, p, 0, 4 * i + p // 2)
            for i, (s, d) in enumerate(zip(sums, lands)) for p in SAME_CORE]


def _comm_call(name, bufs, wait=None, start=None, after=()):
    nb = len(bufs)
    slots = list(start[1]) if start else []
    n_out_sem = 2 * len(slots)
    after = [a for a in (after if isinstance(after, (list, tuple)) else [after]) if a is not None]

    def body(*refs):
        b = refs[:nb]
        at = nb
        if wait:
            w_send, w_recv = refs[at], refs[at + 1]
            at += 2
        at += len(after)
        out_sems = refs[at:at + n_out_sem]
        token = refs[at + n_out_sem + nb]
        x, y, c, me = _mesh_position()

        def part(i, row_of, sender):
            return b[i] if row_of is None else b[i].at[row_of(sender)]

        if wait:
            for cp in wait[1]:
                peer, peer_index = _peer(x, y, c, cp.peer)
                arrival = pltpu.make_async_remote_copy(part(cp.src, cp.src_at, me), part(cp.dst, cp.dst_at, peer_index),
                                                       w_send.at[cp.slot], w_recv.at[cp.slot],
                                                       device_id=peer, device_id_type=MESH)
                arrival.wait_send()
                arrival.wait_recv()
        if start:
            for cp in start[0]:
                peer, _ = _peer(x, y, c, cp.peer)
                pltpu.make_async_remote_copy(part(cp.src, cp.src_at, me), part(cp.dst, cp.dst_at, me),
                                             out_sems[2 * cp.group].at[cp.slot], out_sems[2 * cp.group + 1].at[cp.slot],
                                             device_id=peer, device_id_type=MESH).start()
        token[...] = jnp.zeros_like(token)

    sem_shapes = []
    for n_slots in slots:
        sem_shapes += [pltpu.SemaphoreType.DMA((n_slots,))] * 2
    operands = [pltpu.with_memory_space_constraint(a, pltpu.HBM) for a in bufs]
    in_specs = [HBM_SPEC] * nb
    if wait:
        operands += list(wait[0])
        in_specs += [SEM_SPEC, SEM_SPEC]
    operands += after
    in_specs += [pl.BlockSpec(memory_space=pl.ANY)] * len(after)
    outs = pl.pallas_call(
        body, name=name, in_specs=in_specs,
        out_specs=[SEM_SPEC] * n_out_sem + [HBM_SPEC] * nb + [pl.BlockSpec(memory_space=pltpu.VMEM)],
        out_shape=sem_shapes + [pltpu.HBM(a.shape, a.dtype) for a in bufs] + [jax.ShapeDtypeStruct((8, SLAB), F32)],
        input_output_aliases={i: n_out_sem + i for i in range(nb)},
        compiler_params=pltpu.CompilerParams(has_side_effects=DATAFLOW),
    )(*operands)
    sems = [(outs[2 * k], outs[2 * k + 1]) for k in range(len(slots))]
    return sems, list(outs[n_out_sem:n_out_sem + nb]), outs[-1]


def _pair_sum(stacks, lands, place):
    n = len(stacks)

    def body(place_ref, *refs):
        k = pl.program_id(0)
        for m in range(n):
            mine, theirs, out, land = refs[m], refs[n + m], refs[2 * n + m], refs[3 * n + m]
            total = (mine[0, 0].astype(F32) + theirs[0].astype(F32)).astype(out.dtype)
            out[0] = total

            @pl.when(k == place_ref[1])
            def _():
                land[0] = total

    in_specs = [pl.BlockSpec((1, 1) + a.shape[1:], lambda k, place_ref: (k, place_ref[0], 0, 0)) for a in stacks]
    in_specs += [pl.BlockSpec((1,) + a.shape[1:], lambda k, place_ref: (k, 0, 0)) for a in lands]
    out_specs = [pl.BlockSpec((1,) + a.shape[1:], lambda k, place_ref: (k, 0, 0)) for a in lands]
    out_specs += [pl.BlockSpec((1,) + a.shape[1:], lambda k, place_ref: (place_ref[1], 0, 0)) for a in lands]
    outs = pl.pallas_call(
        body, name="pair_sum_" + "_".join(str(a.shape[1]) for a in stacks),
        grid_spec=pltpu.PrefetchScalarGridSpec(num_scalar_prefetch=1, grid=(N_DEV // 2,), in_specs=in_specs,
                                               out_specs=out_specs),
        out_shape=[pltpu.HBM(a.shape, a.dtype) for a in lands] * 2,
        compiler_params=_params(40, 1),
    )(place, *_in_hbm(*[a.reshape((N_DEV // 2, 2) + a.shape[1:]) for a in stacks], *lands))
    return list(outs[:n]), list(outs[n:])


def _reduce_adam(parts, w, m, v, name):
    rows, cols = w.shape
    n_parts = parts.shape[0]
    rb = rows
    for cand in (256, 176, 128):
        if rows % cand == 0 and rows > cand:
            rb = cand
            break

    def body(p_ref, w_ref, m_ref, v_ref, g_out, d_out, m_out, v_out):
        g = p_ref[0].astype(F32)
        for j in range(1, n_parts):
            g = g + p_ref[j].astype(F32)
        g_out[...] = g
        d_out[...], m_out[...], v_out[...] = _adam(g, w_ref[...], m_ref[...], v_ref[...])

    blk = pl.BlockSpec((rb, cols), lambda i: (i, 0))
    out = jax.ShapeDtypeStruct((rows, cols), F32)
    return pl.pallas_call(
        body, name=name, grid=(rows // rb,),
        in_specs=[pl.BlockSpec((n_parts, rb, cols), lambda i: (0, i, 0)), blk, blk, blk],
        out_specs=[blk] * 4, out_shape=[out] * 4,
        compiler_params=_params(40, 1),
    )(*_in_hbm(parts, w, m, v))


def _cols_from_stack(stack):
    n, r, c = stack.shape
    return jnp.transpose(stack, (1, 0, 2)).reshape(r, n * c)


def _block_diag(w):
    z = jnp.zeros((N_SLAB, 64, 64), w.dtype)
    pairs = w.reshape(N_SLAB, 2, 64, 64)
    top = jnp.concatenate([pairs[:, 0], z], axis=2)
    bottom = jnp.concatenate([z, pairs[:, 1]], axis=2)
    return jnp.concatenate([top, bottom], axis=1)


def _adam(g, w, m, v):
    m_new = ADAM_B1 * m + (1.0 - ADAM_B1) * g
    v_new = ADAM_B2 * v + (1.0 - ADAM_B2) * (g * g)
    m_hat = m_new / (1.0 - ADAM_B1 ** ADAM_STEP)
    v_hat = v_new / (1.0 - ADAM_B2 ** ADAM_STEP)
    return (-ADAM_LR) * (m_hat / (jnp.sqrt(v_hat) + ADAM_EPS) + ADAM_WD * w), m_new, v_new


WIDE = ("ln2_g", "lnf_g")
HALF = ("pool_scale", "conv_b", "b_a", "b_i", "lam", "gn_pool_g", "gn_lru_g")
VECTORS = [(k, D_MODEL) for k in WIDE] + [(k, 512) for k in HALF]
VECTOR_ROWS = sum(width // SLAB for _, width in VECTORS)
LOSS_ROW = -(-VECTOR_ROWS // 8) * 8
CONV_AT = LOSS_ROW + 8
CONV_LANES = LRU_WIDTH // SLAB
PACK_F_ROWS = CONV_AT + CONV_WIDTH * CONV_LANES
MATRIX_ROWS = N_SLAB * SLAB
HEAD = SLAB // 2
GATE_ROWS = N_SLAB * HEAD
PACK_B_ROWS = MATRIX_ROWS + 2 * GATE_ROWS


def _pack_small(vectors, pool_g, wa_g, wi_g, conv, sq):
    n_vec = len(vectors)

    def body(*refs):
        vec = refs[:n_vec]
        pw_ref, wa_ref, wi_ref, cw_ref, sq_ref, out, out_b = refs[n_vec:]
        out[...] = jnp.zeros_like(out)
        row = 0
        for ref, (_, width) in zip(vec, VECTORS):
            for k in range(width // SLAB):
                out[row:row + 1, :] = ref[:, k * SLAB:(k + 1) * SLAB]
                row += 1
        for tap in range(CONV_WIDTH):
            for k in range(CONV_LANES):
                at = CONV_AT + tap * CONV_LANES + k
                out[at:at + 1, :] = cw_ref[tap:tap + 1, k * SLAB:(k + 1) * SLAB]
        total = sq_ref[:, 0:SLAB]
        for k in range(1, D_MODEL // SLAB):
            total = total + sq_ref[:, k * SLAB:(k + 1) * SLAB]
        out[LOSS_ROW:LOSS_ROW + 1, :] = total
        left = lax.broadcasted_iota(jnp.int32, (HEAD, SLAB), 1) < HEAD
        for s in range(N_SLAB):
            out_b[s * SLAB:(s + 1) * SLAB, :] = pw_ref[s].astype(BF16)
            for i, ref in enumerate((wa_ref, wi_ref)):
                at = MATRIX_ROWS + i * GATE_ROWS + s * HEAD
                out_b[at:at + HEAD, :] = jnp.where(left, ref[s, 0:HEAD, :], ref[s, HEAD:SLAB, :]).astype(BF16)

    return pl.pallas_call(
        body, name="pack_small",
        out_shape=[jax.ShapeDtypeStruct((PACK_F_ROWS, SLAB), F32), jax.ShapeDtypeStruct((PACK_B_ROWS, SLAB), BF16)],
    )(*vectors, pool_g, wa_g, wi_g, conv, sq)


def _small_reduce_adam(parts, parts_b, vec_w, vec_m, vec_v, pool_wmv):
    n_vec = len(VECTORS)
    n_parts = parts.shape[0]

    def body(*refs):
        p_ref, pb_ref = refs[0], refs[1]
        w_refs, m_refs, v_refs = (refs[2 + k * n_vec:2 + (k + 1) * n_vec] for k in range(3))
        pw_w, pw_m, pw_v = refs[2 + 3 * n_vec:5 + 3 * n_vec]
        outs = refs[5 + 3 * n_vec:-1]
        total = refs[-1]
        total[...] = p_ref[0]
        for j in range(1, n_parts):
            total[...] += p_ref[j]
        row = 0
        for i, (_, width) in enumerate(VECTORS):
            n_rows = width // SLAB
            g = jnp.concatenate([total[row + k:row + k + 1, :] for k in range(n_rows)], axis=1)
            row += n_rows
            d, m_new, v_new = _adam(g, w_refs[i][...], m_refs[i][...], v_refs[i][...])
            for ref, val in zip(outs[4 * i:4 * i + 4], (g, d, m_new, v_new)):
                ref[...] = val
        tail = outs[4 * n_vec:]

        def summed(first, count):
            g = pb_ref[0, first:first + count, :].astype(F32)
            for j in range(1, n_parts):
                g = g + pb_ref[j, first:first + count, :].astype(F32)
            return g

        g = summed(0, MATRIX_ROWS)
        d, m_new, v_new = _adam(g, pw_w[...], pw_m[...], pw_v[...])
        for ref, val in zip(tail[0:4], (g, d, m_new, v_new)):
            ref[...] = val
        tail[4][...] = summed(MATRIX_ROWS, GATE_ROWS)
        tail[5][...] = summed(MATRIX_ROWS + GATE_ROWS, GATE_ROWS)
        for tap in range(CONV_WIDTH):
            at = CONV_AT + tap * CONV_LANES
            tail[6][tap:tap + 1, :] = jnp.concatenate([total[at + k:at + k + 1, :] for k in range(CONV_LANES)], axis=1)
        tail[7][...] = (0.5 / D_MODEL) * jnp.sum(total[LOSS_ROW:LOSS_ROW + 1, :], axis=1, keepdims=True)

    out_shape = []
    for _, width in VECTORS:
        out_shape += [jax.ShapeDtypeStruct((1, width), F32)] * 4
    out_shape += [jax.ShapeDtypeStruct((MATRIX_ROWS, SLAB), F32)] * 4 + [jax.ShapeDtypeStruct((GATE_ROWS, SLAB), F32)] * 2
    out_shape += [jax.ShapeDtypeStruct((CONV_WIDTH, LRU_WIDTH), F32), jax.ShapeDtypeStruct((1, 1), F32)]
    outs = pl.pallas_call(
        body, name="adam_small", out_shape=out_shape,
        scratch_shapes=[pltpu.VMEM((PACK_F_ROWS, SLAB), F32)],
        compiler_params=_params(40),
    )(parts, parts_b, *vec_w, *vec_m, *vec_v, *pool_wmv)
    vec_out = [tuple(outs[4 * i:4 * i + 4]) for i in range(n_vec)]
    tail = outs[4 * n_vec:]
    return vec_out, tuple(tail[0:4]), tail[4], tail[5], tail[6], tail[7]


def _plain_adam(grads, ws, ms, vs):
    n = len(grads)

    def body(*refs):
        ins, outs = refs[:4 * n], refs[4 * n:]
        for i in range(n):
            d, m_new, v_new = _adam(ins[i][...], ins[n + i][...], ins[2 * n + i][...], ins[3 * n + i][...])
            for ref, val in zip(outs[3 * i:3 * i + 3], (d, m_new, v_new)):
                ref[...] = val

    out_shape = []
    for g in grads:
        out_shape += [jax.ShapeDtypeStruct(g.shape, F32)] * 3
    outs = pl.pallas_call(body, name="adam_plain", out_shape=out_shape)(*grads, *ws, *ms, *vs)
    return [tuple(outs[3 * i:3 * i + 3]) for i in range(n)]


def kernel(x, ln1_g, w_in, pool_w, pool_scale, conv_w, conv_b, w_a, b_a, w_i, b_i, lam, gn_pool_g, gn_lru_g, w_out, ln2_g, w_ffn_gate, w_ffn_up, w_ffn_down, lnf_g, loss_target, m_ln1_g, m_w_in, m_pool_w, m_pool_scale, m_conv_w, m_conv_b, m_w_a, m_b_a, m_w_i, m_b_i, m_lam, m_gn_pool_g, m_gn_lru_g, m_w_out, m_ln2_g, m_w_ffn_gate, m_w_ffn_up, m_w_ffn_down, m_lnf_g, v_ln1_g, v_w_in, v_pool_w, v_pool_scale, v_conv_w, v_conv_b, v_w_a, v_b_a, v_w_i, v_b_i, v_lam, v_gn_pool_g, v_gn_lru_g, v_w_out, v_ln2_g, v_w_ffn_gate, v_w_ffn_up, v_w_ffn_down, v_lnf_g):
    weights = dict(ln1_g=ln1_g, w_in=w_in, pool_w=pool_w, pool_scale=pool_scale, conv_w=conv_w, conv_b=conv_b,
                   w_a=w_a, b_a=b_a, w_i=w_i, b_i=b_i, lam=lam, gn_pool_g=gn_pool_g, gn_lru_g=gn_lru_g,
                   w_out=w_out, ln2_g=ln2_g, w_ffn_gate=w_ffn_gate, w_ffn_up=w_ffn_up, w_ffn_down=w_ffn_down,
                   lnf_g=lnf_g)
    mom1 = dict(ln1_g=m_ln1_g, w_in=m_w_in, pool_w=m_pool_w, pool_scale=m_pool_scale, conv_w=m_conv_w,
                conv_b=m_conv_b, w_a=m_w_a, b_a=m_b_a, w_i=m_w_i, b_i=m_b_i, lam=m_lam, gn_pool_g=m_gn_pool_g,
                gn_lru_g=m_gn_lru_g, w_out=m_w_out, ln2_g=m_ln2_g, w_ffn_gate=m_w_ffn_gate,
                w_ffn_up=m_w_ffn_up, w_ffn_down=m_w_ffn_down, lnf_g=m_lnf_g)
    mom2 = dict(ln1_g=v_ln1_g, w_in=v_w_in, pool_w=v_pool_w, pool_scale=v_pool_scale, conv_w=v_conv_w,
                conv_b=v_conv_b, w_a=v_w_a, b_a=v_b_a, w_i=v_w_i, b_i=v_b_i, lam=v_lam, gn_pool_g=v_gn_pool_g,
                gn_lru_g=v_gn_lru_g, w_out=v_w_out, ln2_g=v_ln2_g, w_ffn_gate=v_w_ffn_gate,
                w_ffn_up=v_w_ffn_up, w_ffn_down=v_w_ffn_down, lnf_g=v_lnf_g)

    xs = x[0]
    target = loss_target[0]

    shard = dict(w_in=lambda a: a[0].T, w_ffn_gate=lambda a: a[0].T, w_ffn_up=lambda a: a[0].T,
                 w_out=lambda a: a[0], w_ffn_down=lambda a: a[0], conv_w=lambda a: a[0])
    unshard = dict(w_in=lambda a: a.T[None], w_ffn_gate=lambda a: a.T[None], w_ffn_up=lambda a: a.T[None],
                   w_out=lambda a: a[None], w_ffn_down=lambda a: a[None], conv_w=lambda a: a[None])

    gathered = ("w_in", "conv_w", "w_out", "w_ffn_gate", "w_ffn_up", "w_ffn_down")
    groups = ((0, 1), (2,), (3, 4, 5))
    sources = [shard[k](weights[k]) if k == "conv_w" else shard[k](weights[k]).astype(BF16) for k in gathered]
    my_index = 4 * lax.axis_index("x") + 2 * lax.axis_index("y") + lax.axis_index("c")
    lands = [lax.empty((N_DEV,) + a.shape, a.dtype) for a in sources]

    def first_hop(n, group=0):
        return _numbered(_fan_out(range(n), range(n, 2 * n), SAME_CORE + SIBLING + MYSELF, 0), group)

    start = []
    for g, members in enumerate(groups):
        start += _numbered(_fan_out(members, [6 + m for m in members], SAME_CORE + SIBLING + MYSELF, 0), g)
    sems, bufs, _ = _comm_call("gather_start", sources + lands,
                               start=(start, [sum(cp.group == g for cp in start) for g in range(len(groups))]))
    sources, lands = bufs[:6], bufs[6:]

    def gathered_group(tag, g, after):
        members = groups[g]
        n = len(members)
        relay = _numbered(_relay(range(n, 2 * n), SAME_CORE))
        relay_sems, bufs, _ = _comm_call("gather_relay_" + tag, [sources[m] for m in members] + [lands[m] for m in members],
                                         wait=(sems[g], first_hop(n)), start=(relay, (len(relay),)), after=after)
        _, bufs, _ = _comm_call("gather_wait_" + tag, bufs[n:], wait=(relay_sems[0], _numbered(_relay(range(n), SAME_CORE))))
        return bufs

    g_in, g_conv = gathered_group("in", 0, None)
    w_in_f = g_in.reshape(IN_WIDTH, D_MODEL)
    conv_w_f = _cols_from_stack(g_conv)

    wa_bd = _block_diag(w_a[0])
    wi_bd = _block_diag(w_i[0])
    lnf_row = lnf_g.reshape(1, D_MODEL)

    u_pool, u_lru, u_gate = _fwd_in(xs, ln1_g, w_in_f)
    y_pool, h, y_lru = _mixer_fwd(u_pool, u_lru, u_gate, pool_w[0], pool_scale, conv_w_f, conv_b,
                                  wa_bd, b_a, wi_bd, b_i, lam)
    (g_out,) = gathered_group("out", 1, y_pool)
    w_out_f = g_out.reshape(D_MODEL, D_MODEL)
    h1, n2 = _fwd_out(xs, y_pool, y_lru, gn_pool_g, gn_lru_g, w_out_f, ln2_g)
    g_gate, g_up, g_down = gathered_group("ffn", 2, n2)
    w_gate_f = g_gate.reshape(D_FF, D_MODEL)
    w_up_f = g_up.reshape(D_FF, D_MODEL)
    w_down_f = g_down.reshape(D_FF, D_MODEL)
    g_act, u_act, dh2, dh2b, d_lnf, sq = _ffn_fwd(h1, n2, target, lnf_row, w_gate_f, w_up_f, w_down_f)

    place = jnp.stack([lax.axis_index("c"), 2 * lax.axis_index("x") + lax.axis_index("y")]).astype(jnp.int32)

    d_gate, d_up, d_down, dn2 = _ffn_bwd(n2, dh2b, g_act, u_act, w_gate_f, w_up_f, w_down_f)
    ffn_stacks = [d.reshape(N_DEV, D_FF // N_DEV, D_MODEL) for d in (d_gate, d_up, d_down)]
    pair_lands = [lax.empty((N_DEV // 2,) + a.shape[1:], a.dtype) for a in ffn_stacks]
    pair_copies = _numbered(_to_sibling(range(3), range(3, 6)))
    sem, bufs, token = _comm_call("ffn_pair_start", ffn_stacks + pair_lands, start=(pair_copies, (len(pair_copies),)))
    dh1, dy_pool, dy_lru, d_out, d_ln2, d_gnp, d_gnl = _bwd_out(dn2, dh2, h1, y_pool, y_lru, gn_pool_g, gn_lru_g,
                                                                 w_out_f, ln2_g, token)
    _, bufs, _ = _comm_call("ffn_pair_wait", bufs, wait=(sem[0], pair_copies), after=dh1)
    ffn_sums, ffn_lands = _pair_sum(bufs[:3], bufs[3:], place)
    ffn_copies = _numbered(_to_chips(range(3), range(3, 6)))
    ffn_sem, ffn_bufs, token = _comm_call("ffn_chip_start", ffn_sums + ffn_lands, start=(ffn_copies, (len(ffn_copies),)))
    (du_pool, du_lru, du_gate, d_pw, d_ps, d_cw, d_cb, d_wa, d_ba, d_wi, d_bi, d_lam) = _mixer_bwd(
        u_pool, u_lru, u_gate, h, dy_pool, dy_lru, pool_w[0], pool_scale, conv_w_f, conv_b,
        wa_bd, b_a, wi_bd, b_i, lam, token)

    def direct(tag, stacks, wholes, after):
        sources = list(stacks) + list(wholes)
        n, n_st = len(sources), len(stacks)
        lands = [lax.empty(a.shape if i < n_st else (N_DEV,) + a.shape, a.dtype) for i, a in enumerate(sources)]
        copies = _numbered(_scatter(range(n_st), range(n, n + n_st), EVERYONE + MYSELF)
                           + _fan_out(range(n_st, n), range(n + n_st, 2 * n), EVERYONE + MYSELF, 0))
        sem, bufs, token = _comm_call(tag + "_start", sources + lands, start=(copies, (len(copies),)), after=after)
        return (tag, sem[0], bufs, copies), token

    def direct_finish(started, after):
        tag, sem, bufs, copies = started
        _, bufs, _ = _comm_call(tag + "_wait", bufs, wait=(sem, copies), after=after)
        return bufs[len(bufs) // 2:]

    vec_grads = dict(ln2_g=d_ln2, lnf_g=d_lnf, pool_scale=d_ps, conv_b=d_cb, b_a=d_ba, b_i=d_bi,
                     lam=d_lam, gn_pool_g=d_gnp, gn_lru_g=d_gnl)
    packed, packed_b = _pack_small([vec_grads[k] for k, _ in VECTORS], d_pw, d_wa, d_wi, d_cw, sq)
    small_started, token = direct("small", [d_out.reshape(N_DEV, D_MODEL // N_DEV, D_MODEL)], [packed, packed_b], None)
    grad_x, d_in, d_ln1 = _bwd_in(xs, dh1, du_pool, du_lru, du_gate, ln1_g, w_in_f, token)
    in_started, token = direct("in", [d_in.reshape(N_DEV, IN_WIDTH // N_DEV, D_MODEL)], [d_ln1], None)

    results = {}

    def reduce_adam(name, parts):
        outs = _reduce_adam(parts, shard[name](weights[name]), shard[name](mom1[name]), shard[name](mom2[name]),
                            "adam_" + name)
        results[name] = tuple(unshard[name](o) for o in outs)
        return outs[0]

    shard["ln1_g"] = unshard["ln1_g"] = lambda a: a
    _, ffn_bufs, _ = _comm_call("ffn_chip_wait", ffn_bufs, wait=(ffn_sem[0], ffn_copies), after=token)
    done = [reduce_adam(name, parts)
            for name, parts in zip(("w_ffn_gate", "w_ffn_up", "w_ffn_down"), ffn_bufs[3:])]
    r_out, r_small, r_small_b = direct_finish(small_started, done)
    done.append(reduce_adam("w_out", r_out))

    def as_row(a, width):
        return a.reshape(1, width)

    def as_matrix(a):
        return a.reshape(MATRIX_ROWS, SLAB)

    def as_heads(a):
        return a.reshape(2 * GATE_ROWS, HEAD)

    def heads_apart(g):
        return jnp.transpose(g.reshape(N_SLAB, HEAD, 2, HEAD), (0, 2, 1, 3)).reshape(2 * GATE_ROWS, HEAD)

    vec_out, pool_out, g_wa, g_wi, g_conv, loss_11 = _small_reduce_adam(
        r_small, r_small_b, [as_row(weights[k], w) for k, w in VECTORS], [as_row(mom1[k], w) for k, w in VECTORS],
        [as_row(mom2[k], w) for k, w in VECTORS], [as_matrix(t["pool_w"]) for t in (weights, mom1, mom2)])
    for (k, _), outs in zip(VECTORS, vec_out):
        results[k] = tuple(o.reshape(weights[k].shape) for o in outs)
    results["pool_w"] = tuple(o.reshape(pool_w.shape) for o in pool_out)
    my_columns = conv_w.shape[-1]
    plain_names = ("w_a", "w_i", "conv_w")
    plain_grads = [heads_apart(g) for g in (g_wa, g_wi)]
    plain_grads.append(lax.dynamic_slice_in_dim(g_conv, my_index * my_columns, my_columns, axis=1))
    views = (as_heads, as_heads, lambda a: a[0])
    plain_out = _plain_adam(plain_grads, *[[view(t[k]) for k, view in zip(plain_names, views)]
                                           for t in (weights, mom1, mom2)])
    for k, g, outs in zip(plain_names, plain_grads, plain_out):
        results[k] = tuple(o.reshape(weights[k].shape) for o in (g,) + outs)
    loss = loss_11[0, 0]
    r_in, r_ln1 = direct_finish(in_started, done + [plain_out[0][0], loss_11])
    reduce_adam("w_in", r_in)
    reduce_adam("ln1_g", r_ln1)

    order = ["ln1_g", "w_in", "pool_w", "pool_scale", "conv_w", "conv_b", "w_a", "b_a", "w_i", "b_i", "lam",
             "gn_pool_g", "gn_lru_g", "w_out", "ln2_g", "w_ffn_gate", "w_ffn_up", "w_ffn_down", "lnf_g"]
    return (loss, grad_x[None],
            *[results[k][0] for k in order], *[results[k][1] for k in order],
            *[results[k][2] for k in order], *[results[k][3] for k in order])
```

```python
from typing import Any, NamedTuple

import jax
import jax.numpy as jnp
from jax import lax
from jax.experimental import pallas as pl
from jax.experimental.pallas import tpu as pltpu

F32 = jnp.float32
BF16 = jnp.bfloat16

N_DEV = 8
D_MODEL = 1024
POOL_WIDTH = 512
LRU_WIDTH = 512
IN_WIDTH = 1536
D_FF = 2816
N_SLAB = 4
SLAB = 128
CONV_WIDTH = 4
LRU_C = 8.0
EPS = 1e-6
HALO = 16
FF_CHUNK = 256

ADAM_LR = 0.001
ADAM_B1 = 0.9
ADAM_B2 = 0.999
ADAM_EPS = 1e-08
ADAM_WD = 0.01
ADAM_STEP = 10

MIB = 1 << 20
MESH = pl.DeviceIdType.MESH


def _params(vmem_mib, n_axes=0):
    sem = ("arbitrary",) * n_axes if n_axes else None
    return pltpu.CompilerParams(dimension_semantics=sem, vmem_limit_bytes=vmem_mib * MIB)


def _in_hbm(*arrays):
    return [pltpu.with_memory_space_constraint(a, pltpu.HBM) for a in arrays]


def _mm(a, b):
    return jnp.dot(a, b, preferred_element_type=F32)


def _mm_nt(a, b):
    return lax.dot_general(a, b, (((1,), (1,)), ((), ())), preferred_element_type=F32)


def _mm_tn(a, b):
    return lax.dot_general(a, b, (((0,), (0,)), ((), ())), preferred_element_type=F32)


def _rms(x, g):
    rstd = lax.rsqrt(jnp.mean(x * x, axis=-1, keepdims=True) + EPS)
    xhat = x * rstd
    return xhat * g, xhat, rstd


def _rms_bwd(dy, xhat, rstd, g):
    gy = dy * g
    dx = rstd * (gy - xhat * jnp.mean(gy * xhat, axis=-1, keepdims=True))
    return dx, jnp.sum(dy * xhat, axis=0, keepdims=True)


def _gelu(z):
    t = jnp.tanh(0.7978845608028654 * (z + 0.044715 * z * z * z))
    return 0.5 * z * (1.0 + t), t


def _gelu_grad(z, t):
    return 0.5 * (1.0 + t) + 0.5 * z * (1.0 - t * t) * 0.7978845608028654 * (1.0 + 3.0 * 0.044715 * z * z)


def _softplus_neg(lam):
    x = -lam
    e = jnp.exp(-jnp.abs(x))
    u = 1.0 + e
    l1p = jnp.where(u == 1.0, e, jnp.log(u) * e / (u - 1.0))
    return jnp.maximum(x, 0.0) + l1p


def _one_minus_square(la, a):
    x = 2.0 * la
    series = -x * (1.0 + x * (0.5 + x * (1.0 / 6.0 + x * (1.0 / 24.0))))
    return jnp.where(x > -0.06, series, 1.0 - a * a)


def _sigmoid(x):
    return 0.5 * jnp.tanh(0.5 * x) + 0.5


def _down(v, d):
    return pltpu.roll(v, d, 0)


def _up(v, d):
    return pltpu.roll(v, v.shape[0] - d, 0)


def _slab_major(s, dtype=F32):
    return pltpu.HBM((N_SLAB, s, SLAB), dtype)


def _slabs_spec(tb):
    return pl.BlockSpec((N_SLAB, tb, SLAB), lambda i: (0, i, 0))


def _read_slabs(ref):
    return jnp.concatenate([ref[k] for k in range(N_SLAB)], axis=1)


def _write_slabs(ref, value):
    for k in range(N_SLAB):
        ref[k] = value[:, k * SLAB:(k + 1) * SLAB]


def _token_block(s, most=512):
    for rows in (most, 512, 256):
        if rows <= most and s % rows == 0 and s > rows:
            return rows
    return s


def _time_chunk(s):
    return 512 if s % 512 == 0 and s > 512 else 256 if s % 256 == 0 else s


def _fwd_in(x, ln1_g, w_in_t):
    s = x.shape[0]
    tb = _token_block(s, 1024)

    def body(x_ref, g_ref, w_ref, up_ref, ul_ref, ug_ref):
        n, _, _ = _rms(x_ref[...], g_ref[...])
        proj = _mm_nt(n.astype(BF16), w_ref[...])
        _write_slabs(up_ref, proj[:, :POOL_WIDTH])
        _write_slabs(ul_ref, proj[:, POOL_WIDTH:POOL_WIDTH + LRU_WIDTH])
        _write_slabs(ug_ref, proj[:, POOL_WIDTH + LRU_WIDTH:])

    out = _slab_major(s)
    return pl.pallas_call(
        body, name="fwd_in", grid=(s // tb,),
        in_specs=[pl.BlockSpec((tb, D_MODEL), lambda i: (i, 0)),
                  pl.BlockSpec((1, D_MODEL), lambda i: (0, 0)),
                  pl.BlockSpec((IN_WIDTH, D_MODEL), lambda i: (0, 0))],
        out_specs=[_slabs_spec(tb)] * 3,
        out_shape=[out, out, out],
        compiler_params=_params(40, 1),
    )(*_in_hbm(x, ln1_g, w_in_t))


def _pool_denominator(t0, row, window):
    return jnp.minimum((t0 + row + 1).astype(F32), window)


def _causal_window(ext, deeper):
    s = ext + _down(ext, 1)
    s = s + deeper[0] * _down(s, 2)
    s = s + deeper[1] * _down(s, 4)
    s = s + deeper[2] * _down(s, 8)
    return s[HALO:]


def _anticausal_window(ext, deeper, rows):
    s = ext + _up(ext, 1)
    s = s + deeper[0] * _up(s, 2)
    s = s + deeper[1] * _up(s, 4)
    s = s + deeper[2] * _up(s, 8)
    return s[:rows]


def _conv(taps, cw, cb):
    return cw[3] * taps[0] + cw[2] * taps[1] + cw[1] * taps[2] + cw[0] * taps[3] + cb


def _lru_gates(xc, wa, ba, wi, bi, sp):
    xb = xc.astype(BF16)
    r = _sigmoid(_mm(xb, wa) + ba)
    i = _sigmoid(_mm(xb, wi) + bi)
    la = (-LRU_C) * r * sp
    a = jnp.exp(la)
    mult = jnp.sqrt(jnp.maximum(_one_minus_square(la, a), 0.0))
    return xb, r, i, a, mult


SUBLANES = 8


def _scan_pads(a_pad, b_pad, rows, causal):
    spare = slice(0, SUBLANES) if causal else slice(rows, rows + SUBLANES)
    a_pad[spare, :] = jnp.ones((SUBLANES, SLAB), F32)
    b_pad[spare, :] = jnp.zeros((SUBLANES, SLAB), F32)


def _scan_causal(a, b, h_prev, a_pad, b_pad, rows):
    d = 1
    while d < min(SUBLANES, rows):
        a_pad[SUBLANES:, :] = a
        b_pad[SUBLANES:, :] = b
        b = a * b_pad[SUBLANES - d:SUBLANES - d + rows, :] + b
        a = a * a_pad[SUBLANES - d:SUBLANES - d + rows, :]
        d *= 2
    while d < rows:
        b = jnp.concatenate([b[:d], a[d:] * b[:-d] + b[d:]], axis=0)
        a = jnp.concatenate([a[:d], a[d:] * a[:-d]], axis=0)
        d *= 2
    return b + a * h_prev


def _scan_anticausal(a, b, l_next, a_pad, b_pad, rows):
    d = 1
    while d < min(SUBLANES, rows):
        a_pad[:rows, :] = a
        b_pad[:rows, :] = b
        b = a * b_pad[d:d + rows, :] + b
        a = a * a_pad[d:d + rows, :]
        d *= 2
    while d < rows:
        b = jnp.concatenate([a[:-d] * b[d:] + b[:-d], b[-d:]], axis=0)
        a = jnp.concatenate([a[:-d] * a[d:], a[-d:]], axis=0)
        d *= 2
    return b + a * l_next


def _slab_scalars():
    slab = pl.program_id(0)
    deeper = [jnp.where(slab > k, 1.0, 0.0).astype(F32) for k in range(N_SLAB - 1)]
    window = jnp.left_shift(jnp.int32(2), slab).astype(F32)
    inverse = jnp.where(slab == 0, 0.5, jnp.where(slab == 1, 0.25, jnp.where(slab == 2, 0.125, 0.0625))).astype(F32)
    return deeper, (window, inverse)


def _window_mean(total, t0, row, window, at_start):
    if at_start:
        return total / _pool_denominator(t0, row, window[0])
    return total * window[1]


def _slab_specs(s):
    seq = pl.BlockSpec((None, s, SLAB), lambda k: (k, 0, 0))
    mat = pl.BlockSpec((1, SLAB, SLAB), lambda k: (k, 0, 0))
    vec = pl.BlockSpec((1, SLAB), lambda k: (0, k))
    taps = pl.BlockSpec((CONV_WIDTH, SLAB), lambda k: (0, k))
    return seq, mat, vec, taps


def _mixer_fwd(u_pool, u_lru, u_gate, pool_w, pool_scale, conv_w, conv_b, wa_bd, b_a, wi_bd, b_i, lam):
    s = u_pool.shape[1]
    tc = _time_chunk(s)
    n_chunks = s // tc

    def body(up_ref, ul_ref, ug_ref, pw_ref, ps_ref, cw_ref, cb_ref, wa_ref, ba_ref, wi_ref, bi_ref, lam_ref,
             yp_ref, h_ref, yl_ref, shift_ref, a_pad, b_pad):
        _scan_pads(a_pad, b_pad, tc, causal=True)
        deeper, window = _slab_scalars()
        pw = pw_ref[0].astype(BF16)
        wa = wa_ref[0].astype(BF16)
        wi = wi_ref[0].astype(BF16)
        ps, cb, ba, bi = ps_ref[...], cb_ref[...], ba_ref[...], bi_ref[...]
        cw = [cw_ref[k:k + 1, :] for k in range(CONV_WIDTH)]
        sp = _softplus_neg(lam_ref[...])
        row = lax.broadcasted_iota(jnp.int32, (tc, SLAB), 0)

        def chunk(t0, ext_p, ext_l, h_prev, at_start=False):
            rows = pl.ds(t0, tc)
            d = _window_mean(_causal_window(ext_p, deeper), t0, row, window, at_start) - ext_p[HALO:]
            yp_ref[rows, :] = _mm(d.astype(BF16), pw) * ps
            shift_ref[...] = ext_l
            xc = _conv([shift_ref[HALO - k:HALO - k + tc, :] for k in range(CONV_WIDTH)], cw, cb)
            _, _, i, a, mult = _lru_gates(xc, wa, ba, wi, bi, sp)
            h = _scan_causal(a, mult * (i * xc), h_prev, a_pad, b_pad, tc)
            h_ref[rows, :] = h
            yl_ref[rows, :] = h * _gelu(ug_ref[rows, :])[0]
            return h[tc - 1:tc, :]

        pad = jnp.zeros((HALO, SLAB), F32)
        h0 = chunk(0, jnp.concatenate([pad, up_ref[pl.ds(0, tc), :]], axis=0),
                   jnp.concatenate([pad, ul_ref[pl.ds(0, tc), :]], axis=0), jnp.zeros((1, SLAB), F32), at_start=True)

        def step(c, h_prev):
            t0 = pl.multiple_of(c * tc, tc)
            ext = pl.ds(pl.multiple_of(c * tc - HALO, HALO), tc + HALO)
            return chunk(t0, up_ref[ext, :], ul_ref[ext, :], h_prev)

        lax.fori_loop(1, n_chunks, step, h0)

    seq, mat, vec, taps = _slab_specs(s)
    out = _slab_major(s)
    return pl.pallas_call(
        body, name="mixer_fwd", grid=(N_SLAB,),
        in_specs=[seq, seq, seq, mat, vec, taps, vec, mat, vec, mat, vec, vec],
        out_specs=[seq, seq, seq], out_shape=[out, out, out],
        scratch_shapes=[pltpu.VMEM((tc + HALO, SLAB), F32), pltpu.VMEM((tc + SUBLANES, SLAB), F32),
                        pltpu.VMEM((tc + SUBLANES, SLAB), F32)],
        compiler_params=_params(48, 1),
    )(*_in_hbm(u_pool, u_lru, u_gate, pool_w, pool_scale, conv_w, conv_b, wa_bd, b_a, wi_bd, b_i, lam))


def _fwd_out(x, y_pool, y_lru, gn_pool_g, gn_lru_g, w_out, ln2_g):
    s = x.shape[0]
    tb = _token_block(s, 1024)

    def body(x_ref, yp_ref, yl_ref, gp_ref, gl_ref, w_ref, g2_ref, h1_ref, n2_ref):
        mp, _, _ = _rms(_read_slabs(yp_ref), gp_ref[...])
        ml, _, _ = _rms(_read_slabs(yl_ref), gl_ref[...])
        h1 = x_ref[...] + _mm(mp.astype(BF16), w_ref[:POOL_WIDTH, :]) + _mm(ml.astype(BF16), w_ref[POOL_WIDTH:, :])
        h1_ref[...] = h1
        n2_ref[...] = _rms(h1, g2_ref[...])[0].astype(BF16)

    row = pl.BlockSpec((tb, D_MODEL), lambda i: (i, 0))
    half = _slabs_spec(tb)
    return pl.pallas_call(
        body, name="fwd_out", grid=(s // tb,),
        in_specs=[row, half, half, pl.BlockSpec((1, 512), lambda i: (0, 0)), pl.BlockSpec((1, 512), lambda i: (0, 0)),
                  pl.BlockSpec((D_MODEL, D_MODEL), lambda i: (0, 0)), pl.BlockSpec((1, D_MODEL), lambda i: (0, 0))],
        out_specs=[row, row],
        out_shape=[pltpu.HBM((s, D_MODEL), F32), pltpu.HBM((s, D_MODEL), BF16)],
        compiler_params=_params(40, 1),
    )(*_in_hbm(x, y_pool, y_lru, gn_pool_g, gn_lru_g, w_out, ln2_g))


def _ffn_fwd(h1, n2, target, lnf_g, w_gate, w_up, w_down):
    s = h1.shape[0]
    tb = 512
    sub = 256
    n_ff = D_FF // FF_CHUNK

    def body(h1_ref, n2_ref, t_ref, gf_ref, wg_hbm, wu_hbm, wd_hbm,
             g_ref, u_ref, dh_ref, dhb_ref, dgf_ref, sq_ref, wg, wu, wd, sem):
        @pl.when(pl.program_id(0) == 0)
        def _():
            loads = [pltpu.make_async_copy(src, dst, sem.at[k])
                     for k, (src, dst) in enumerate(((wg_hbm, wg), (wu_hbm, wu), (wd_hbm, wd)))]
            for cp in loads:
                cp.start()
            for cp in loads:
                cp.wait()
            dgf_ref[...] = jnp.zeros_like(dgf_ref)
            sq_ref[...] = jnp.zeros_like(sq_ref)

        n2v = n2_ref[...]
        acc = jnp.zeros((tb, D_MODEL), F32)
        for c in range(n_ff):
            cols = slice(c * FF_CHUNK, (c + 1) * FF_CHUNK)
            g = _mm_nt(n2v, wg[cols, :])
            u = _mm_nt(n2v, wu[cols, :])
            g_ref[:, cols] = g.astype(BF16)
            u_ref[:, cols] = u.astype(BF16)
            act = g * jax.nn.sigmoid(g) * u
            acc = acc + _mm(act.astype(BF16), wd[cols, :])
        gf = gf_ref[...]
        for r in range(tb // sub):
            rows = slice(r * sub, (r + 1) * sub)
            y, xhat, rstd = _rms(h1_ref[rows, :] + acc[rows, :], gf)
            err = y - t_ref[rows, :]
            sq_ref[...] += jnp.sum(err * err, axis=0, keepdims=True)
            dh2, dgf = _rms_bwd(err * (1.0 / D_MODEL), xhat, rstd, gf)
            dgf_ref[...] += dgf
            dh_ref[rows, :] = dh2
            dhb_ref[rows, :] = dh2.astype(BF16)

    row = pl.BlockSpec((tb, D_MODEL), lambda i: (i, 0))
    ff = pl.BlockSpec((tb, D_FF), lambda i: (i, 0))
    vec = pl.BlockSpec((1, D_MODEL), lambda i: (0, 0))
    anyspace = pl.BlockSpec(memory_space=pl.ANY)
    return pl.pallas_call(
        body, name="ffn_fwd", grid=(s // tb,),
        in_specs=[row, row, row, vec, anyspace, anyspace, anyspace],
        out_specs=[ff, ff, row, row, vec, vec],
        out_shape=[pltpu.HBM((s, D_FF), BF16), pltpu.HBM((s, D_FF), BF16),
                   pltpu.HBM((s, D_MODEL), F32), pltpu.HBM((s, D_MODEL), BF16),
                   jax.ShapeDtypeStruct((1, D_MODEL), F32), jax.ShapeDtypeStruct((1, D_MODEL), F32)],
        scratch_shapes=[pltpu.VMEM((D_FF, D_MODEL), BF16), pltpu.VMEM((D_FF, D_MODEL), BF16),
                        pltpu.VMEM((D_FF, D_MODEL), BF16), pltpu.SemaphoreType.DMA((3,))],
        compiler_params=_params(60, 1),
    )(*_in_hbm(h1, n2, target, lnf_g, w_gate, w_up, w_down))


def _ffn_bwd(n2, dh2b, g, u, w_gate_t, w_up_t, w_down):
    s = n2.shape[0]
    tb = min(1024, s)
    n_ff = D_FF // FF_CHUNK
    n_tb = s // tb

    def body(n2_ref, dh_ref, g_ref, u_ref, wg_ref, wu_ref, wd_ref, dwg_ref, dwu_ref, dwd_ref, dn2_ref,
             dn2_acc, acc_g, acc_u, acc_d):
        j = pl.program_id(0)
        t = pl.program_id(1)

        @pl.when(t == 0)
        def _():
            acc_g[...] = jnp.zeros_like(acc_g)
            acc_u[...] = jnp.zeros_like(acc_u)
            acc_d[...] = jnp.zeros_like(acc_d)

        n2v = n2_ref[...]
        dh = dh_ref[...]
        gv = g_ref[...].astype(F32)
        uv = u_ref[...].astype(F32)
        sg = jax.nn.sigmoid(gv)
        silu = gv * sg
        dact = _mm_nt(dh, wd_ref[...])
        dub = (dact * silu).astype(BF16)
        dgb = (dact * uv * (sg * (1.0 + gv * (1.0 - sg)))).astype(BF16)
        acc_d[...] += _mm_tn((silu * uv).astype(BF16), dh)
        acc_g[...] += _mm_tn(dgb, n2v)
        acc_u[...] += _mm_tn(dub, n2v)
        part = _mm(jnp.concatenate([dgb, dub], axis=1), jnp.concatenate([wg_ref[...], wu_ref[...]], axis=0))
        rows = pl.ds(pl.multiple_of(t * tb, tb), tb)

        @pl.when(t == n_tb - 1)
        def _():
            dwg_ref[...] = acc_g[...].astype(BF16)
            dwu_ref[...] = acc_u[...].astype(BF16)
            dwd_ref[...] = acc_d[...].astype(BF16)

        @pl.when(j == 0)
        def _():
            dn2_acc[rows, :] = part

        @pl.when(jnp.logical_and(j > 0, j < n_ff - 1))
        def _():
            dn2_acc[rows, :] += part

        @pl.when(j == n_ff - 1)
        def _():
            dn2_ref[...] = dn2_acc[rows, :] + part

    row = pl.BlockSpec((tb, D_MODEL), lambda j, t: (t, 0))
    act = pl.BlockSpec((tb, FF_CHUNK), lambda j, t: (t, j))
    w_row = pl.BlockSpec((FF_CHUNK, D_MODEL), lambda j, t: (j, 0))
    last = pl.BlockSpec((tb, D_MODEL), lambda j, t: (jnp.where(j == n_ff - 1, t, 0), 0))
    grad = pltpu.HBM((D_FF, D_MODEL), BF16)
    chunk_acc = pltpu.VMEM((FF_CHUNK, D_MODEL), F32)
    return pl.pallas_call(
        body, name="ffn_bwd", grid=(n_ff, n_tb),
        in_specs=[row, row, act, act, w_row, w_row, w_row],
        out_specs=[w_row, w_row, w_row, last],
        out_shape=[grad, grad, grad, pltpu.HBM((s, D_MODEL), F32)],
        scratch_shapes=[pltpu.VMEM((s, D_MODEL), F32), chunk_acc, chunk_acc, chunk_acc],
        compiler_params=_params(56, 2),
    )(*_in_hbm(n2, dh2b, g, u, w_gate_t, w_up_t, w_down))


def _bwd_out(dn2, dh2, h1, y_pool, y_lru, gn_pool_g, gn_lru_g, w_out, ln2_g, after):
    s = h1.shape[0]
    tb = _token_block(s)

    def body(dn2_ref, dh2_ref, h1_ref, yp_ref, yl_ref, gp_ref, gl_ref, w_ref, g2_ref, _after,
             dh1_ref, dyp_ref, dyl_ref, dwb_ref, dg2_ref, dgp_ref, dgl_ref, dw_ref):
        @pl.when(pl.program_id(0) == 0)
        def _():
            dw_ref[...] = jnp.zeros_like(dw_ref)
            dg2_ref[...] = jnp.zeros_like(dg2_ref)
            dgp_ref[...] = jnp.zeros_like(dgp_ref)
            dgl_ref[...] = jnp.zeros_like(dgl_ref)

        g2 = g2_ref[...]
        _, xhat2, rstd2 = _rms(h1_ref[...], g2)
        dres, dg2 = _rms_bwd(dn2_ref[...], xhat2, rstd2, g2)
        dg2_ref[...] += dg2
        dh1 = dh2_ref[...] + dres
        dh1_ref[...] = dh1
        dh1b = dh1.astype(BF16)
        gp, gl = gp_ref[...], gl_ref[...]
        mp, xhat_p, rstd_p = _rms(_read_slabs(yp_ref), gp)
        ml, xhat_l, rstd_l = _rms(_read_slabs(yl_ref), gl)
        dw_ref[:POOL_WIDTH, :] += _mm_tn(mp.astype(BF16), dh1b)
        dw_ref[POOL_WIDTH:, :] += _mm_tn(ml.astype(BF16), dh1b)
        dyp, dgp = _rms_bwd(_mm_nt(dh1b, w_ref[:POOL_WIDTH, :]), xhat_p, rstd_p, gp)
        dyl, dgl = _rms_bwd(_mm_nt(dh1b, w_ref[POOL_WIDTH:, :]), xhat_l, rstd_l, gl)
        _write_slabs(dyp_ref, dyp)
        _write_slabs(dyl_ref, dyl)
        dgp_ref[...] += dgp
        dgl_ref[...] += dgl

        @pl.when(pl.program_id(0) == s // tb - 1)
        def _():
            dwb_ref[...] = dw_ref[...].astype(BF16)

    row = pl.BlockSpec((tb, D_MODEL), lambda i: (i, 0))
    half = _slabs_spec(tb)
    vec = pl.BlockSpec((1, D_MODEL), lambda i: (0, 0))
    hvec = pl.BlockSpec((1, 512), lambda i: (0, 0))
    mat = pl.BlockSpec((D_MODEL, D_MODEL), lambda i: (0, 0))
    return pl.pallas_call(
        body, name="bwd_out", grid=(s // tb,),
        in_specs=[row, row, row, half, half, hvec, hvec, mat, vec, pl.BlockSpec(memory_space=pl.ANY)],
        out_specs=[row, half, half, mat, vec, hvec, hvec],
        out_shape=[pltpu.HBM((s, D_MODEL), F32), _slab_major(s),
                   _slab_major(s), pltpu.HBM((D_MODEL, D_MODEL), BF16),
                   jax.ShapeDtypeStruct((1, D_MODEL), F32), jax.ShapeDtypeStruct((1, 512), F32),
                   jax.ShapeDtypeStruct((1, 512), F32)],
        scratch_shapes=[pltpu.VMEM((D_MODEL, D_MODEL), F32)],
        compiler_params=_params(48, 1),
    )(*_in_hbm(dn2, dh2, h1, y_pool, y_lru, gn_pool_g, gn_lru_g, w_out, ln2_g), after)


def _mixer_bwd(u_pool, u_lru, u_gate, h, dy_pool, dy_lru,
               pool_w, pool_scale, conv_w, conv_b, wa_bd, b_a, wi_bd, b_i, lam, after):
    s = u_pool.shape[1]
    tc = _time_chunk(s)
    n_chunks = s // tc

    def body(up_ref, ul_ref, ug_ref, h_ref, dyp_ref, dyl_ref,
             pw_ref, ps_ref, cw_ref, cb_ref, wa_ref, ba_ref, wi_ref, bi_ref, lam_ref, _after,
             dup_ref, dul_ref, dug_ref, dpw_ref, dps_ref, dcw_ref, dcb_ref, dwa_ref, dba_ref, dwi_ref, dbi_ref, dlam_ref,
             a_pad, b_pad, u_shift, h_shift, a_shift, d_shift):
        _scan_pads(a_pad, b_pad, tc, causal=False)
        deeper, window = _slab_scalars()
        pw = pw_ref[0].astype(BF16)
        wa = wa_ref[0].astype(BF16)
        wi = wi_ref[0].astype(BF16)
        ps, cb, ba, bi = ps_ref[...], cb_ref[...], ba_ref[...], bi_ref[...]
        cw = [cw_ref[k:k + 1, :] for k in range(CONV_WIDTH)]
        lam_v = lam_ref[...]
        sp = _softplus_neg(lam_v)
        row = lax.broadcasted_iota(jnp.int32, (tc, SLAB), 0)
        for ref in (dpw_ref, dps_ref, dcw_ref, dcb_ref, dwa_ref, dba_ref, dwi_ref, dbi_ref, dlam_ref):
            ref[...] = jnp.zeros_like(ref)

        def chunk(t0, ext_p, ext_l, ext_h, carry, at_start=False):
            l_next, a_next, dxc_next, ddn_next = carry
            rows = pl.ds(t0, tc)
            u_shift[...] = ext_l
            taps = [u_shift[HALO - k:HALO - k + tc, :] for k in range(CONV_WIDTH)]
            xc = _conv(taps, cw, cb)
            xb, r, i, a, mult = _lru_gates(xc, wa, ba, wi, bi, sp)
            hv = ext_h[HALO:]
            h_shift[...] = ext_h
            h_before = h_shift[HALO - 1:HALO - 1 + tc, :]
            ug = ug_ref[rows, :]
            dyl = dyl_ref[rows, :]
            gel, th = _gelu(ug)
            dug_ref[rows, :] = (dyl * hv * _gelu_grad(ug, th)).astype(BF16)
            a_shift[:tc, :] = a
            a_shift[tc:, :] = jnp.broadcast_to(a_next, (SUBLANES, SLAB))
            a_after = a_shift[1:1 + tc, :]
            l = _scan_anticausal(a_after, dyl * gel, l_next, a_pad, b_pad, tc)
            dmult = l * (i * xc)
            di = l * mult * xc
            dxc = l * mult * i
            dla = (l * h_before) * a - jnp.where(mult > 0.0, dmult * (a * a) / mult, 0.0)
            dlam_ref[...] += jnp.sum(dla * r, axis=0, keepdims=True)
            dpa = (dla * ((-LRU_C) * sp)) * (r * (1.0 - r))
            dpi = di * (i * (1.0 - i))
            dpab = dpa.astype(BF16)
            dpib = dpi.astype(BF16)
            dwa_ref[0] += _mm_tn(xb, dpab)
            dwi_ref[0] += _mm_tn(xb, dpib)
            dba_ref[...] += jnp.sum(dpa, axis=0, keepdims=True)
            dbi_ref[...] += jnp.sum(dpi, axis=0, keepdims=True)
            dxc = dxc + _mm_nt(dpab, wa) + _mm_nt(dpib, wi)
            d_shift[:tc, :] = dxc
            d_shift[tc:, :] = dxc_next
            dul_ref[rows, :] = (cw[3] * dxc + cw[2] * d_shift[1:1 + tc, :]
                                + cw[1] * d_shift[2:2 + tc, :] + cw[0] * d_shift[3:3 + tc, :]).astype(BF16)
            for k in range(CONV_WIDTH):
                dcw_ref[k:k + 1, :] += jnp.sum(dxc * taps[CONV_WIDTH - 1 - k], axis=0, keepdims=True)
            dcb_ref[...] += jnp.sum(dxc, axis=0, keepdims=True)
            db = (_window_mean(_causal_window(ext_p, deeper), t0, row, window, at_start) - ext_p[HALO:]).astype(BF16)
            dyp = dyp_ref[rows, :]
            dps_ref[...] += jnp.sum(dyp * _mm(db, pw), axis=0, keepdims=True)
            dys = (dyp * ps).astype(BF16)
            dpw_ref[0] += _mm_tn(db, dys)
            dd = _mm_nt(dys, pw)
            ddn = _window_mean(dd, t0, row, window, at_start)
            ext_q = jnp.concatenate([ddn, ddn_next], axis=0)
            dup_ref[rows, :] = (_anticausal_window(ext_q, deeper, tc) - dd).astype(BF16)
            return l[0:1, :], a[0:1, :], dxc[0:8, :], ddn[0:HALO, :]

        def step(k, carry):
            c = n_chunks - 1 - k
            t0 = pl.multiple_of(c * tc, tc)
            ext = pl.ds(pl.multiple_of(c * tc - HALO, HALO), tc + HALO)
            return chunk(t0, up_ref[ext, :], ul_ref[ext, :], h_ref[ext, :], carry)

        carry = (jnp.zeros((1, SLAB), F32), jnp.zeros((1, SLAB), F32),
                 jnp.zeros((8, SLAB), F32), jnp.zeros((HALO, SLAB), F32))
        carry = lax.fori_loop(0, n_chunks - 1, step, carry)
        pad = jnp.zeros((HALO, SLAB), F32)
        first = pl.ds(0, tc)
        chunk(0, jnp.concatenate([pad, up_ref[first, :]], axis=0), jnp.concatenate([pad, ul_ref[first, :]], axis=0),
              jnp.concatenate([pad, h_ref[first, :]], axis=0), carry, at_start=True)
        dlam_ref[...] = dlam_ref[...] * (LRU_C * jax.nn.sigmoid(-lam_v))

    seq, mat, vec, taps = _slab_specs(s)
    grad = _slab_major(s, BF16)
    mats = jax.ShapeDtypeStruct((N_SLAB, SLAB, SLAB), F32)
    vecs = jax.ShapeDtypeStruct((1, 512), F32)
    return pl.pallas_call(
        body, name="mixer_bwd", grid=(N_SLAB,),
        in_specs=[seq] * 6 + [mat, vec, taps, vec, mat, vec, mat, vec, vec, pl.BlockSpec(memory_space=pl.ANY)],
        out_specs=[seq, seq, seq, mat, vec, taps, vec, mat, vec, mat, vec, vec],
        out_shape=[grad, grad, grad, mats, vecs, jax.ShapeDtypeStruct((CONV_WIDTH, 512), F32), vecs,
                   mats, vecs, mats, vecs, vecs],
        scratch_shapes=[pltpu.VMEM((tc + SUBLANES, SLAB), F32), pltpu.VMEM((tc + SUBLANES, SLAB), F32),
                        pltpu.VMEM((tc + HALO, SLAB), F32), pltpu.VMEM((tc + HALO, SLAB), F32),
                        pltpu.VMEM((tc + SUBLANES, SLAB), F32), pltpu.VMEM((tc + SUBLANES, SLAB), F32)],
        compiler_params=_params(56, 1),
    )(*_in_hbm(u_pool, u_lru, u_gate, h, dy_pool, dy_lru, pool_w, pool_scale, conv_w, conv_b, wa_bd, b_a, wi_bd,
               b_i, lam), after)


def _bwd_in(x, dh1, du_pool, du_lru, du_gate, ln1_g, w_in_t, after):
    s = x.shape[0]
    tb = _token_block(s)

    def body(x_ref, dh1_ref, dup_ref, dul_ref, dug_ref, g_ref, w_ref, _after, dx_ref, dwb_ref, dg_ref, dw_ref):
        @pl.when(pl.program_id(0) == 0)
        def _():
            dw_ref[...] = jnp.zeros_like(dw_ref)
            dg_ref[...] = jnp.zeros_like(dg_ref)

        g1 = g_ref[...]
        n, xhat, rstd = _rms(x_ref[...], g1)
        nb = n.astype(BF16)
        db = jnp.concatenate([_read_slabs(ref) for ref in (dup_ref, dul_ref, dug_ref)], axis=1)
        dw_ref[...] += _mm_tn(db, nb)
        dx, dg1 = _rms_bwd(_mm(db, w_ref[...]), xhat, rstd, g1)
        dx_ref[...] = dh1_ref[...] + dx
        dg_ref[...] += dg1

        @pl.when(pl.program_id(0) == s // tb - 1)
        def _():
            dwb_ref[...] = dw_ref[...].astype(BF16)

    row = pl.BlockSpec((tb, D_MODEL), lambda i: (i, 0))
    half = _slabs_spec(tb)
    vec = pl.BlockSpec((1, D_MODEL), lambda i: (0, 0))
    mat = pl.BlockSpec((IN_WIDTH, D_MODEL), lambda i: (0, 0))
    return pl.pallas_call(
        body, name="bwd_in", grid=(s // tb,),
        in_specs=[row, row, half, half, half, vec, mat, pl.BlockSpec(memory_space=pl.ANY)],
        out_specs=[row, mat, vec],
        out_shape=[pltpu.HBM((s, D_MODEL), F32), pltpu.HBM((IN_WIDTH, D_MODEL), BF16),
                   jax.ShapeDtypeStruct((1, D_MODEL), F32)],
        scratch_shapes=[pltpu.VMEM((IN_WIDTH, D_MODEL), F32)],
        compiler_params=_params(48, 1),
    )(*_in_hbm(x, dh1, du_pool, du_lru, du_gate, ln1_g, w_in_t), after)


def _mesh_position():
    x, y, c = lax.axis_index("x"), lax.axis_index("y"), lax.axis_index("c")
    return x, y, c, 4 * x + 2 * y + c


def _peer(x, y, c, p):
    px = 1 - x if p & 4 else x
    py = 1 - y if p & 2 else y
    pc = 1 - c if p & 1 else c
    return (px, py, pc), 4 * px + 2 * py + pc


HBM_SPEC = pl.BlockSpec(memory_space=pltpu.HBM)
SEM_SPEC = pl.BlockSpec(memory_space=pltpu.SEMAPHORE)
DATAFLOW = pltpu.SideEffectType.DATAFLOW_SIDE_EFFECTING


class Copy(NamedTuple):
    src: int
    src_at: Any
    dst: int
    dst_at: Any
    peer: int
    group: int
    slot: int


SIBLING = (1,)
SAME_CORE = (2, 4, 6)
EVERYONE = tuple(range(1, N_DEV))
MYSELF = (0,)


def _same(index):
    return index


def _chip(index):
    return jnp.right_shift(index, 1)


def _fan_out(srcs, lands, peers, group):
    return [Copy(s, None, d, _same, p, group, N_DEV * i + p) for i, (s, d) in enumerate(zip(srcs, lands)) for p in peers]


def _scatter(stacks, lands, peers):
    return [Copy(s, lambda me, p=p: jnp.bitwise_xor(me, p), d, _same, p, 0, 0)
            for s, d in zip(stacks, lands) for p in peers]


def _numbered(copies, group=0):
    return [cp._replace(group=group, slot=i) for i, cp in enumerate(copies)]


def _relay(lands, peers):
    return [Copy(b, lambda s, q=q: jnp.bitwise_xor(s, q), b, lambda s, q=q: jnp.bitwise_xor(s, q), 1, 0, N_DEV * i + q)
            for i, b in enumerate(lands) for q in peers]


def _to_sibling(stacks, lands):
    return [Copy(s, lambda me, k=k: 2 * k + 1 - jnp.bitwise_and(me, 1), d, lambda me, k=k: k, 1, 0, 4 * i + k)
            for i, (s, d) in enumerate(zip(stacks, lands)) for k in range(N_DEV // 2)]


def _to_chips(sums, lands):
    return [Copy(s, lambda me, p=p: jnp.bitwise_xor(_chip(me), p // 2), d, _chip, p, 0, 4 * i + p // 2)
            for i, (s, d) in enumerate(zip(sums, lands)) for p in SAME_CORE]


def _comm_call(name, bufs, wait=None, start=None, after=()):
    nb = len(bufs)
    slots = list(start[1]) if start else []
    n_out_sem = 2 * len(slots)
    after = [a for a in (after if isinstance(after, (list, tuple)) else [after]) if a is not None]

    def body(*refs):
        b = refs[:nb]
        at = nb
        if wait:
            w_send, w_recv = refs[at], refs[at + 1]
            at += 2
        at += len(after)
        out_sems = refs[at:at + n_out_sem]
        token = refs[at + n_out_sem + nb]
        x, y, c, me = _mesh_position()

        def part(i, row_of, sender):
            return b[i] if row_of is None else b[i].at[row_of(sender)]

        if wait:
            for cp in wait[1]:
                peer, peer_index = _peer(x, y, c, cp.peer)
                arrival = pltpu.make_async_remote_copy(part(cp.src, cp.src_at, me), part(cp.dst, cp.dst_at, peer_index),
                                                       w_send.at[cp.slot], w_recv.at[cp.slot],
                                                       device_id=peer, device_id_type=MESH)
                arrival.wait_send()
                arrival.wait_recv()
        if start:
            for cp in start[0]:
                peer, _ = _peer(x, y, c, cp.peer)
                pltpu.make_async_remote_copy(part(cp.src, cp.src_at, me), part(cp.dst, cp.dst_at, me),
                                             out_sems[2 * cp.group].at[cp.slot], out_sems[2 * cp.group + 1].at[cp.slot],
                                             device_id=peer, device_id_type=MESH).start()
        token[...] = jnp.zeros_like(token)

    sem_shapes = []
    for n_slots in slots:
        sem_shapes += [pltpu.SemaphoreType.DMA((n_slots,))] * 2
    operands = [pltpu.with_memory_space_constraint(a, pltpu.HBM) for a in bufs]
    in_specs = [HBM_SPEC] * nb
    if wait:
        operands += list(wait[0])
        in_specs += [SEM_SPEC, SEM_SPEC]
    operands += after
    in_specs += [pl.BlockSpec(memory_space=pl.ANY)] * len(after)
    outs = pl.pallas_call(
        body, name=name, in_specs=in_specs,
        out_specs=[SEM_SPEC] * n_out_sem + [HBM_SPEC] * nb + [pl.BlockSpec(memory_space=pltpu.VMEM)],
        out_shape=sem_shapes + [pltpu.HBM(a.shape, a.dtype) for a in bufs] + [jax.ShapeDtypeStruct((8, SLAB), F32)],
        input_output_aliases={i: n_out_sem + i for i in range(nb)},
        compiler_params=pltpu.CompilerParams(has_side_effects=DATAFLOW),
    )(*operands)
    sems = [(outs[2 * k], outs[2 * k + 1]) for k in range(len(slots))]
    return sems, list(outs[n_out_sem:n_out_sem + nb]), outs[-1]


def _pair_sum(stacks, lands, place):
    n = len(stacks)

    def body(place_ref, *refs):
        k = pl.program_id(0)
        for m in range(n):
            mine, theirs, out, land = refs[m], refs[n + m], refs[2 * n + m], refs[3 * n + m]
            total = (mine[0, 0].astype(F32) + theirs[0].astype(F32)).astype(out.dtype)
            out[0] = total

            @pl.when(k == place_ref[1])
            def _():
                land[0] = total

    in_specs = [pl.BlockSpec((1, 1) + a.shape[1:], lambda k, place_ref: (k, place_ref[0], 0, 0)) for a in stacks]
    in_specs += [pl.BlockSpec((1,) + a.shape[1:], lambda k, place_ref: (k, 0, 0)) for a in lands]
    out_specs = [pl.BlockSpec((1,) + a.shape[1:], lambda k, place_ref: (k, 0, 0)) for a in lands]
    out_specs += [pl.BlockSpec((1,) + a.shape[1:], lambda k, place_ref: (place_ref[1], 0, 0)) for a in lands]
    outs = pl.pallas_call(
        body, name="pair_sum_" + "_".join(str(a.shape[1]) for a in stacks),
        grid_spec=pltpu.PrefetchScalarGridSpec(num_scalar_prefetch=1, grid=(N_DEV // 2,), in_specs=in_specs,
                                               out_specs=out_specs),
        out_shape=[pltpu.HBM(a.shape, a.dtype) for a in lands] * 2,
        compiler_params=_params(40, 1),
    )(place, *_in_hbm(*[a.reshape((N_DEV // 2, 2) + a.shape[1:]) for a in stacks], *lands))
    return list(outs[:n]), list(outs[n:])


def _reduce_adam(parts, w, m, v, name):
    rows, cols = w.shape
    n_parts = parts.shape[0]
    rb = rows
    for cand in (256, 176, 128):
        if rows % cand == 0 and rows > cand:
            rb = cand
            break

    def body(p_ref, w_ref, m_ref, v_ref, g_out, d_out, m_out, v_out):
        g = p_ref[0].astype(F32)
        for j in range(1, n_parts):
            g = g + p_ref[j].astype(F32)
        g_out[...] = g
        d_out[...], m_out[...], v_out[...] = _adam(g, w_ref[...], m_ref[...], v_ref[...])

    blk = pl.BlockSpec((rb, cols), lambda i: (i, 0))
    out = jax.ShapeDtypeStruct((rows, cols), F32)
    return pl.pallas_call(
        body, name=name, grid=(rows // rb,),
        in_specs=[pl.BlockSpec((n_parts, rb, cols), lambda i: (0, i, 0)), blk, blk, blk],
        out_specs=[blk] * 4, out_shape=[out] * 4,
        compiler_params=_params(40, 1),
    )(*_in_hbm(parts, w, m, v))


def _cols_from_stack(stack):
    n, r, c = stack.shape
    return jnp.transpose(stack, (1, 0, 2)).reshape(r, n * c)


def _block_diag(w):
    z = jnp.zeros((N_SLAB, 64, 64), w.dtype)
    pairs = w.reshape(N_SLAB, 2, 64, 64)
    top = jnp.concatenate([pairs[:, 0], z], axis=2)
    bottom = jnp.concatenate([z, pairs[:, 1]], axis=2)
    return jnp.concatenate([top, bottom], axis=1)


def _adam(g, w, m, v):
    m_new = ADAM_B1 * m + (1.0 - ADAM_B1) * g
    v_new = ADAM_B2 * v + (1.0 - ADAM_B2) * (g * g)
    m_hat = m_new / (1.0 - ADAM_B1 ** ADAM_STEP)
    v_hat = v_new / (1.0 - ADAM_B2 ** ADAM_STEP)
    return (-ADAM_LR) * (m_hat / (jnp.sqrt(v_hat) + ADAM_EPS) + ADAM_WD * w), m_new, v_new


WIDE = ("ln2_g", "lnf_g")
HALF = ("pool_scale", "conv_b", "b_a", "b_i", "lam", "gn_pool_g", "gn_lru_g")
VECTORS = [(k, D_MODEL) for k in WIDE] + [(k, 512) for k in HALF]
VECTOR_ROWS = sum(width // SLAB for _, width in VECTORS)
LOSS_ROW = -(-VECTOR_ROWS // 8) * 8
CONV_AT = LOSS_ROW + 8
CONV_LANES = LRU_WIDTH // SLAB
PACK_F_ROWS = CONV_AT + CONV_WIDTH * CONV_LANES
MATRIX_ROWS = N_SLAB * SLAB
HEAD = SLAB // 2
GATE_ROWS = N_SLAB * HEAD
PACK_B_ROWS = MATRIX_ROWS + 2 * GATE_ROWS


def _pack_small(vectors, pool_g, wa_g, wi_g, conv, sq):
    n_vec = len(vectors)

    def body(*refs):
        vec = refs[:n_vec]
        pw_ref, wa_ref, wi_ref, cw_ref, sq_ref, out, out_b = refs[n_vec:]
        out[...] = jnp.zeros_like(out)
        row = 0
        for ref, (_, width) in zip(vec, VECTORS):
            for k in range(width // SLAB):
                out[row:row + 1, :] = ref[:, k * SLAB:(k + 1) * SLAB]
                row += 1
        for tap in range(CONV_WIDTH):
            for k in range(CONV_LANES):
                at = CONV_AT + tap * CONV_LANES + k
                out[at:at + 1, :] = cw_ref[tap:tap + 1, k * SLAB:(k + 1) * SLAB]
        total = sq_ref[:, 0:SLAB]
        for k in range(1, D_MODEL // SLAB):
            total = total + sq_ref[:, k * SLAB:(k + 1) * SLAB]
        out[LOSS_ROW:LOSS_ROW + 1, :] = total
        left = lax.broadcasted_iota(jnp.int32, (HEAD, SLAB), 1) < HEAD
        for s in range(N_SLAB):
            out_b[s * SLAB:(s + 1) * SLAB, :] = pw_ref[s].astype(BF16)
            for i, ref in enumerate((wa_ref, wi_ref)):
                at = MATRIX_ROWS + i * GATE_ROWS + s * HEAD
                out_b[at:at + HEAD, :] = jnp.where(left, ref[s, 0:HEAD, :], ref[s, HEAD:SLAB, :]).astype(BF16)

    return pl.pallas_call(
        body, name="pack_small",
        out_shape=[jax.ShapeDtypeStruct((PACK_F_ROWS, SLAB), F32), jax.ShapeDtypeStruct((PACK_B_ROWS, SLAB), BF16)],
    )(*vectors, pool_g, wa_g, wi_g, conv, sq)


def _small_reduce_adam(parts, parts_b, vec_w, vec_m, vec_v, pool_wmv):
    n_vec = len(VECTORS)
    n_parts = parts.shape[0]

    def body(*refs):
        p_ref, pb_ref = refs[0], refs[1]
        w_refs, m_refs, v_refs = (refs[2 + k * n_vec:2 + (k + 1) * n_vec] for k in range(3))
        pw_w, pw_m, pw_v = refs[2 + 3 * n_vec:5 + 3 * n_vec]
        outs = refs[5 + 3 * n_vec:-1]
        total = refs[-1]
        total[...] = p_ref[0]
        for j in range(1, n_parts):
            total[...] += p_ref[j]
        row = 0
        for i, (_, width) in enumerate(VECTORS):
            n_rows = width // SLAB
            g = jnp.concatenate([total[row + k:row + k + 1, :] for k in range(n_rows)], axis=1)
            row += n_rows
            d, m_new, v_new = _adam(g, w_refs[i][...], m_refs[i][...], v_refs[i][...])
            for ref, val in zip(outs[4 * i:4 * i + 4], (g, d, m_new, v_new)):
                ref[...] = val
        tail = outs[4 * n_vec:]

        def summed(first, count):
            g = pb_ref[0, first:first + count, :].astype(F32)
            for j in range(1, n_parts):
                g = g + pb_ref[j, first:first + count, :].astype(F32)
            return g

        g = summed(0, MATRIX_ROWS)
        d, m_new, v_new = _adam(g, pw_w[...], pw_m[...], pw_v[...])
        for ref, val in zip(tail[0:4], (g, d, m_new, v_new)):
            ref[...] = val
        tail[4][...] = summed(MATRIX_ROWS, GATE_ROWS)
        tail[5][...] = summed(MATRIX_ROWS + GATE_ROWS, GATE_ROWS)
        for tap in range(CONV_WIDTH):
            at = CONV_AT + tap * CONV_LANES
            tail[6][tap:tap + 1, :] = jnp.concatenate([total[at + k:at + k + 1, :] for k in range(CONV_LANES)], axis=1)
        tail[7][...] = (0.5 / D_MODEL) * jnp.sum(total[LOSS_ROW:LOSS_ROW + 1, :], axis=1, keepdims=True)

    out_shape = []
    for _, width in VECTORS:
        out_shape += [jax.ShapeDtypeStruct((1, width), F32)] * 4
    out_shape += [jax.ShapeDtypeStruct((MATRIX_ROWS, SLAB), F32)] * 4 + [jax.ShapeDtypeStruct((GATE_ROWS, SLAB), F32)] * 2
    out_shape += [jax.ShapeDtypeStruct((CONV_WIDTH, LRU_WIDTH), F32), jax.ShapeDtypeStruct((1, 1), F32)]
    outs = pl.pallas_call(
        body, name="adam_small", out_shape=out_shape,
        scratch_shapes=[pltpu.VMEM((PACK_F_ROWS, SLAB), F32)],
        compiler_params=_params(40),
    )(parts, parts_b, *vec_w, *vec_m, *vec_v, *pool_wmv)
    vec_out = [tuple(outs[4 * i:4 * i + 4]) for i in range(n_vec)]
    tail = outs[4 * n_vec:]
    return vec_out, tuple(tail[0:4]), tail[4], tail[5], tail[6], tail[7]


def _plain_adam(grads, ws, ms, vs):
    n = len(grads)

    def body(*refs):
        ins, outs = refs[:4 * n], refs[4 * n:]
        for i in range(n):
            d, m_new, v_new = _adam(ins[i][...], ins[n + i][...], ins[2 * n + i][...], ins[3 * n + i][...])
            for ref, val in zip(outs[3 * i:3 * i + 3], (d, m_new, v_new)):
                ref[...] = val

    out_shape = []
    for g in grads:
        out_shape += [jax.ShapeDtypeStruct(g.shape, F32)] * 3
    outs = pl.pallas_call(body, name="adam_plain", out_shape=out_shape)(*grads, *ws, *ms, *vs)
    return [tuple(outs[3 * i:3 * i + 3]) for i in range(n)]


def kernel(x, ln1_g, w_in, pool_w, pool_scale, conv_w, conv_b, w_a, b_a, w_i, b_i, lam, gn_pool_g, gn_lru_g, w_out, ln2_g, w_ffn_gate, w_ffn_up, w_ffn_down, lnf_g, loss_target, m_ln1_g, m_w_in, m_pool_w, m_pool_scale, m_conv_w, m_conv_b, m_w_a, m_b_a, m_w_i, m_b_i, m_lam, m_gn_pool_g, m_gn_lru_g, m_w_out, m_ln2_g, m_w_ffn_gate, m_w_ffn_up, m_w_ffn_down, m_lnf_g, v_ln1_g, v_w_in, v_pool_w, v_pool_scale, v_conv_w, v_conv_b, v_w_a, v_b_a, v_w_i, v_b_i, v_lam, v_gn_pool_g, v_gn_lru_g, v_w_out, v_ln2_g, v_w_ffn_gate, v_w_ffn_up, v_w_ffn_down, v_lnf_g):
    weights = dict(ln1_g=ln1_g, w_in=w_in, pool_w=pool_w, pool_scale=pool_scale, conv_w=conv_w, conv_b=conv_b,
                   w_a=w_a, b_a=b_a, w_i=w_i, b_i=b_i, lam=lam, gn_pool_g=gn_pool_g, gn_lru_g=gn_lru_g,
                   w_out=w_out, ln2_g=ln2_g, w_ffn_gate=w_ffn_gate, w_ffn_up=w_ffn_up, w_ffn_down=w_ffn_down,
                   lnf_g=lnf_g)
    mom1 = dict(ln1_g=m_ln1_g, w_in=m_w_in, pool_w=m_pool_w, pool_scale=m_pool_scale, conv_w=m_conv_w,
                conv_b=m_conv_b, w_a=m_w_a, b_a=m_b_a, w_i=m_w_i, b_i=m_b_i, lam=m_lam, gn_pool_g=m_gn_pool_g,
                gn_lru_g=m_gn_lru_g, w_out=m_w_out, ln2_g=m_ln2_g, w_ffn_gate=m_w_ffn_gate,
                w_ffn_up=m_w_ffn_up, w_ffn_down=m_w_ffn_down, lnf_g=m_lnf_g)
    mom2 = dict(ln1_g=v_ln1_g, w_in=v_w_in, pool_w=v_pool_w, pool_scale=v_pool_scale, conv_w=v_conv_w,
                conv_b=v_conv_b, w_a=v_w_a, b_a=v_b_a, w_i=v_w_i, b_i=v_b_i, lam=v_lam, gn_pool_g=v_gn_pool_g,
                gn_lru_g=v_gn_lru_g, w_out=v_w_out, ln2_g=v_ln2_g, w_ffn_gate=v_w_ffn_gate,
                w_ffn_up=v_w_ffn_up, w_ffn_down=v_w_ffn_down, lnf_g=v_lnf_g)

    xs = x[0]
    target = loss_target[0]

    shard = dict(w_in=lambda a: a[0].T, w_ffn_gate=lambda a: a[0].T, w_ffn_up=lambda a: a[0].T,
                 w_out=lambda a: a[0], w_ffn_down=lambda a: a[0], conv_w=lambda a: a[0])
    unshard = dict(w_in=lambda a: a.T[None], w_ffn_gate=lambda a: a.T[None], w_ffn_up=lambda a: a.T[None],
                   w_out=lambda a: a[None], w_ffn_down=lambda a: a[None], conv_w=lambda a: a[None])

    gathered = ("w_in", "conv_w", "w_out", "w_ffn_gate", "w_ffn_up", "w_ffn_down")
    groups = ((0, 1), (2,), (3, 4, 5))
    sources = [shard[k](weights[k]) if k == "conv_w" else shard[k](weights[k]).astype(BF16) for k in gathered]
    my_index = 4 * lax.axis_index("x") + 2 * lax.axis_index("y") + lax.axis_index("c")
    lands = [lax.empty((N_DEV,) + a.shape, a.dtype) for a in sources]

    def first_hop(n, group=0):
        return _numbered(_fan_out(range(n), range(n, 2 * n), SAME_CORE + SIBLING + MYSELF, 0), group)

    start = []
    for g, members in enumerate(groups):
        start += _numbered(_fan_out(members, [6 + m for m in members], SAME_CORE + SIBLING + MYSELF, 0), g)
    sems, bufs, _ = _comm_call("gather_start", sources + lands,
                               start=(start, [sum(cp.group == g for cp in start) for g in range(len(groups))]))
    sources, lands = bufs[:6], bufs[6:]

    def gathered_group(tag, g, after):
        members = groups[g]
        n = len(members)
        relay = _numbered(_relay(range(n, 2 * n), SAME_CORE))
        relay_sems, bufs, _ = _comm_call("gather_relay_" + tag, [sources[m] for m in members] + [lands[m] for m in members],
                                         wait=(sems[g], first_hop(n)), start=(relay, (len(relay),)), after=after)
        _, bufs, _ = _comm_call("gather_wait_" + tag, bufs[n:], wait=(relay_sems[0], _numbered(_relay(range(n), SAME_CORE))))
        return bufs

    g_in, g_conv = gathered_group("in", 0, None)
    w_in_f = g_in.reshape(IN_WIDTH, D_MODEL)
    conv_w_f = _cols_from_stack(g_conv)

    wa_bd = _block_diag(w_a[0])
    wi_bd = _block_diag(w_i[0])
    lnf_row = lnf_g.reshape(1, D_MODEL)

    u_pool, u_lru, u_gate = _fwd_in(xs, ln1_g, w_in_f)
    y_pool, h, y_lru = _mixer_fwd(u_pool, u_lru, u_gate, pool_w[0], pool_scale, conv_w_f, conv_b,
                                  wa_bd, b_a, wi_bd, b_i, lam)
    (g_out,) = gathered_group("out", 1, y_pool)
    w_out_f = g_out.reshape(D_MODEL, D_MODEL)
    h1, n2 = _fwd_out(xs, y_pool, y_lru, gn_pool_g, gn_lru_g, w_out_f, ln2_g)
    g_gate, g_up, g_down = gathered_group("ffn", 2, n2)
    w_gate_f = g_gate.reshape(D_FF, D_MODEL)
    w_up_f = g_up.reshape(D_FF, D_MODEL)
    w_down_f = g_down.reshape(D_FF, D_MODEL)
    g_act, u_act, dh2, dh2b, d_lnf, sq = _ffn_fwd(h1, n2, target, lnf_row, w_gate_f, w_up_f, w_down_f)

    place = jnp.stack([lax.axis_index("c"), 2 * lax.axis_index("x") + lax.axis_index("y")]).astype(jnp.int32)

    d_gate, d_up, d_down, dn2 = _ffn_bwd(n2, dh2b, g_act, u_act, w_gate_f, w_up_f, w_down_f)
    ffn_stacks = [d.reshape(N_DEV, D_FF // N_DEV, D_MODEL) for d in (d_gate, d_up, d_down)]
    pair_lands = [lax.empty((N_DEV // 2,) + a.shape[1:], a.dtype) for a in ffn_stacks]
    pair_copies = _numbered(_to_sibling(range(3), range(3, 6)))
    sem, bufs, token = _comm_call("ffn_pair_start", ffn_stacks + pair_lands, start=(pair_copies, (len(pair_copies),)))
    dh1, dy_pool, dy_lru, d_out, d_ln2, d_gnp, d_gnl = _bwd_out(dn2, dh2, h1, y_pool, y_lru, gn_pool_g, gn_lru_g,
                                                                 w_out_f, ln2_g, token)
    _, bufs, _ = _comm_call("ffn_pair_wait", bufs, wait=(sem[0], pair_copies), after=dh1)
    ffn_sums, ffn_lands = _pair_sum(bufs[:3], bufs[3:], place)
    ffn_copies = _numbered(_to_chips(range(3), range(3, 6)))
    ffn_sem, ffn_bufs, token = _comm_call("ffn_chip_start", ffn_sums + ffn_lands, start=(ffn_copies, (len(ffn_copies),)))
    (du_pool, du_lru, du_gate, d_pw, d_ps, d_cw, d_cb, d_wa, d_ba, d_wi, d_bi, d_lam) = _mixer_bwd(
        u_pool, u_lru, u_gate, h, dy_pool, dy_lru, pool_w[0], pool_scale, conv_w_f, conv_b,
        wa_bd, b_a, wi_bd, b_i, lam, token)

    def direct(tag, stacks, wholes, after):
        sources = list(stacks) + list(wholes)
        n, n_st = len(sources), len(stacks)
        lands = [lax.empty(a.shape if i < n_st else (N_DEV,) + a.shape, a.dtype) for i, a in enumerate(sources)]
        copies = _numbered(_scatter(range(n_st), range(n, n + n_st), EVERYONE + MYSELF)
                           + _fan_out(range(n_st, n), range(n + n_st, 2 * n), EVERYONE + MYSELF, 0))
        sem, bufs, token = _comm_call(tag + "_start", sources + lands, start=(copies, (len(copies),)), after=after)
        return (tag, sem[0], bufs, copies), token

    def direct_finish(started, after):
        tag, sem, bufs, copies = started
        _, bufs, _ = _comm_call(tag + "_wait", bufs, wait=(sem, copies), after=after)
        return bufs[len(bufs) // 2:]

    vec_grads = dict(ln2_g=d_ln2, lnf_g=d_lnf, pool_scale=d_ps, conv_b=d_cb, b_a=d_ba, b_i=d_bi,
                     lam=d_lam, gn_pool_g=d_gnp, gn_lru_g=d_gnl)
    packed, packed_b = _pack_small([vec_grads[k] for k, _ in VECTORS], d_pw, d_wa, d_wi, d_cw, sq)
    small_started, token = direct("small", [d_out.reshape(N_DEV, D_MODEL // N_DEV, D_MODEL)], [packed, packed_b], None)
    grad_x, d_in, d_ln1 = _bwd_in(xs, dh1, du_pool, du_lru, du_gate, ln1_g, w_in_f, token)
    in_started, token = direct("in", [d_in.reshape(N_DEV, IN_WIDTH // N_DEV, D_MODEL)], [d_ln1], None)

    results = {}

    def reduce_adam(name, parts):
        outs = _reduce_adam(parts, shard[name](weights[name]), shard[name](mom1[name]), shard[name](mom2[name]),
                            "adam_" + name)
        results[name] = tuple(unshard[name](o) for o in outs)
        return outs[0]

    shard["ln1_g"] = unshard["ln1_g"] = lambda a: a
    _, ffn_bufs, _ = _comm_call("ffn_chip_wait", ffn_bufs, wait=(ffn_sem[0], ffn_copies), after=token)
    done = [reduce_adam(name, parts)
            for name, parts in zip(("w_ffn_gate", "w_ffn_up", "w_ffn_down"), ffn_bufs[3:])]
    r_out, r_small, r_small_b = direct_finish(small_started, done)
    done.append(reduce_adam("w_out", r_out))

    def as_row(a, width):
        return a.reshape(1, width)

    def as_matrix(a):
        return a.reshape(MATRIX_ROWS, SLAB)

    def as_heads(a):
        return a.reshape(2 * GATE_ROWS, HEAD)

    def heads_apart(g):
        return jnp.transpose(g.reshape(N_SLAB, HEAD, 2, HEAD), (0, 2, 1, 3)).reshape(2 * GATE_ROWS, HEAD)

    vec_out, pool_out, g_wa, g_wi, g_conv, loss_11 = _small_reduce_adam(
        r_small, r_small_b, [as_row(weights[k], w) for k, w in VECTORS], [as_row(mom1[k], w) for k, w in VECTORS],
        [as_row(mom2[k], w) for k, w in VECTORS], [as_matrix(t["pool_w"]) for t in (weights, mom1, mom2)])
    for (k, _), outs in zip(VECTORS, vec_out):
        results[k] = tuple(o.reshape(weights[k].shape) for o in outs)
    results["pool_w"] = tuple(o.reshape(pool_w.shape) for o in pool_out)
    my_columns = conv_w.shape[-1]
    plain_names = ("w_a", "w_i", "conv_w")
    plain_grads = [heads_apart(g) for g in (g_wa, g_wi)]
    plain_grads.append(lax.dynamic_slice_in_dim(g_conv, my_index * my_columns, my_columns, axis=1))
    views = (as_heads, as_heads, lambda a: a[0])
    plain_out = _plain_adam(plain_grads, *[[view(t[k]) for k, view in zip(plain_names, views)]
                                           for t in (weights, mom1, mom2)])
    for k, g, outs in zip(plain_names, plain_grads, plain_out):
        results[k] = tuple(o.reshape(weights[k].shape) for o in (g,) + outs)
    loss = loss_11[0, 0]
    r_in, r_ln1 = direct_finish(in_started, done + [plain_out[0][0], loss_11])
    reduce_adam("w_in", r_in)
    reduce_adam("ln1_g", r_ln1)

    order = ["ln1_g", "w_in", "pool_w", "pool_scale", "conv_w", "conv_b", "w_a", "b_a", "w_i", "b_i", "lam",
             "gn_pool_g", "gn_lru_g", "w_out", "ln2_g", "w_ffn_gate", "w_ffn_up", "w_ffn_down", "lnf_g"]
    return (loss, grad_x[None],
            *[results[k][0] for k in order], *[results[k][1] for k in order],
            *[results[k][2] for k in order], *[results[k][3] for k in order])
```

```python
from typing import Any, NamedTuple

import jax
import jax.numpy as jnp
from jax import lax
from jax.experimental import pallas as pl
from jax.experimental.pallas import tpu as pltpu

F32 = jnp.float32
BF16 = jnp.bfloat16

N_DEV = 8
D_MODEL = 1024
POOL_WIDTH = 512
LRU_WIDTH = 512
IN_WIDTH = 1536
D_FF = 2816
N_SLAB = 4
SLAB = 128
CONV_WIDTH = 4
LRU_C = 8.0
EPS = 1e-6
HALO = 16
FF_CHUNK = 256

ADAM_LR = 0.001
ADAM_B1 = 0.9
ADAM_B2 = 0.999
ADAM_EPS = 1e-08
ADAM_WD = 0.01
ADAM_STEP = 10

MIB = 1 << 20
MESH = pl.DeviceIdType.MESH


def _params(vmem_mib, n_axes=0):
    sem = ("arbitrary",) * n_axes if n_axes else None
    return pltpu.CompilerParams(dimension_semantics=sem, vmem_limit_bytes=vmem_mib * MIB)


def _in_hbm(*arrays):
    return [pltpu.with_memory_space_constraint(a, pltpu.HBM) for a in arrays]


def _mm(a, b):
    return jnp.dot(a, b, preferred_element_type=F32)


def _mm_nt(a, b):
    return lax.dot_general(a, b, (((1,), (1,)), ((), ())), preferred_element_type=F32)


def _mm_tn(a, b):
    return lax.dot_general(a, b, (((0,), (0,)), ((), ())), preferred_element_type=F32)


def _rms(x, g):
    rstd = lax.rsqrt(jnp.mean(x * x, axis=-1, keepdims=True) + EPS)
    xhat = x * rstd
    return xhat * g, xhat, rstd


def _rms_bwd(dy, xhat, rstd, g):
    gy = dy * g
    dx = rstd * (gy - xhat * jnp.mean(gy * xhat, axis=-1, keepdims=True))
    return dx, jnp.sum(dy * xhat, axis=0, keepdims=True)


def _gelu(z):
    t = jnp.tanh(0.7978845608028654 * (z + 0.044715 * z * z * z))
    return 0.5 * z * (1.0 + t), t


def _gelu_grad(z, t):
    return 0.5 * (1.0 + t) + 0.5 * z * (1.0 - t * t) * 0.7978845608028654 * (1.0 + 3.0 * 0.044715 * z * z)


def _softplus_neg(lam):
    x = -lam
    e = jnp.exp(-jnp.abs(x))
    u = 1.0 + e
    l1p = jnp.where(u == 1.0, e, jnp.log(u) * e / (u - 1.0))
    return jnp.maximum(x, 0.0) + l1p


def _one_minus_square(la, a):
    x = 2.0 * la
    series = -x * (1.0 + x * (0.5 + x * (1.0 / 6.0 + x * (1.0 / 24.0))))
    return jnp.where(x > -0.06, series, 1.0 - a * a)


def _sigmoid(x):
    return 0.5 * jnp.tanh(0.5 * x) + 0.5


def _down(v, d):
    return pltpu.roll(v, d, 0)


def _up(v, d):
    return pltpu.roll(v, v.shape[0] - d, 0)


def _slab_major(s, dtype=F32):
    return pltpu.HBM((N_SLAB, s, SLAB), dtype)


def _slabs_spec(tb):
    return pl.BlockSpec((N_SLAB, tb, SLAB), lambda i: (0, i, 0))


def _read_slabs(ref):
    return jnp.concatenate([ref[k] for k in range(N_SLAB)], axis=1)


def _write_slabs(ref, value):
    for k in range(N_SLAB):
        ref[k] = value[:, k * SLAB:(k + 1) * SLAB]


def _token_block(s, most=512):
    for rows in (most, 512, 256):
        if rows <= most and s % rows == 0 and s > rows:
            return rows
    return s


def _time_chunk(s):
    return 512 if s % 512 == 0 and s > 512 else 256 if s % 256 == 0 else s


def _fwd_in(x, ln1_g, w_in_t):
    s = x.shape[0]
    tb = _token_block(s, 1024)

    def body(x_ref, g_ref, w_ref, up_ref, ul_ref, ug_ref):
        n, _, _ = _rms(x_ref[...], g_ref[...])
        proj = _mm_nt(n.astype(BF16), w_ref[...])
        _write_slabs(up_ref, proj[:, :POOL_WIDTH])
        _write_slabs(ul_ref, proj[:, POOL_WIDTH:POOL_WIDTH + LRU_WIDTH])
        _write_slabs(ug_ref, proj[:, POOL_WIDTH + LRU_WIDTH:])

    out = _slab_major(s)
    return pl.pallas_call(
        body, name="fwd_in", grid=(s // tb,),
        in_specs=[pl.BlockSpec((tb, D_MODEL), lambda i: (i, 0)),
                  pl.BlockSpec((1, D_MODEL), lambda i: (0, 0)),
                  pl.BlockSpec((IN_WIDTH, D_MODEL), lambda i: (0, 0))],
        out_specs=[_slabs_spec(tb)] * 3,
        out_shape=[out, out, out],
        compiler_params=_params(40, 1),
    )(*_in_hbm(x, ln1_g, w_in_t))


def _pool_denominator(t0, row, window):
    return jnp.minimum((t0 + row + 1).astype(F32), window)


def _causal_window(ext, deeper):
    s = ext + _down(ext, 1)
    s = s + deeper[0] * _down(s, 2)
    s = s + deeper[1] * _down(s, 4)
    s = s + deeper[2] * _down(s, 8)
    return s[HALO:]


def _anticausal_window(ext, deeper, rows):
    s = ext + _up(ext, 1)
    s = s + deeper[0] * _up(s, 2)
    s = s + deeper[1] * _up(s, 4)
    s = s + deeper[2] * _up(s, 8)
    return s[:rows]


def _conv(taps, cw, cb):
    return cw[3] * taps[0] + cw[2] * taps[1] + cw[1] * taps[2] + cw[0] * taps[3] + cb


def _lru_gates(xc, wa, ba, wi, bi, sp):
    xb = xc.astype(BF16)
    r = _sigmoid(_mm(xb, wa) + ba)
    i = _sigmoid(_mm(xb, wi) + bi)
    la = (-LRU_C) * r * sp
    a = jnp.exp(la)
    mult = jnp.sqrt(jnp.maximum(_one_minus_square(la, a), 0.0))
    return xb, r, i, a, mult


SUBLANES = 8


def _scan_pads(a_pad, b_pad, rows, causal):
    spare = slice(0, SUBLANES) if causal else slice(rows, rows + SUBLANES)
    a_pad[spare, :] = jnp.ones((SUBLANES, SLAB), F32)
    b_pad[spare, :] = jnp.zeros((SUBLANES, SLAB), F32)


def _scan_causal(a, b, h_prev, a_pad, b_pad, rows):
    d = 1
    while d < min(SUBLANES, rows):
        a_pad[SUBLANES:, :] = a
        b_pad[SUBLANES:, :] = b
        b = a * b_pad[SUBLANES - d:SUBLANES - d + rows, :] + b
        a = a * a_pad[SUBLANES - d:SUBLANES - d + rows, :]
        d *= 2
    while d < rows:
        b = jnp.concatenate([b[:d], a[d:] * b[:-d] + b[d:]], axis=0)
        a = jnp.concatenate([a[:d], a[d:] * a[:-d]], axis=0)
        d *= 2
    return b + a * h_prev


def _scan_anticausal(a, b, l_next, a_pad, b_pad, rows):
    d = 1
    while d < min(SUBLANES, rows):
        a_pad[:rows, :] = a
        b_pad[:rows, :] = b
        b = a * b_pad[d:d + rows, :] + b
        a = a * a_pad[d:d + rows, :]
        d *= 2
    while d < rows:
        b = jnp.concatenate([a[:-d] * b[d:] + b[:-d], b[-d:]], axis=0)
        a = jnp.concatenate([a[:-d] * a[d:], a[-d:]], axis=0)
        d *= 2
    return b + a * l_next


def _slab_scalars():
    slab = pl.program_id(0)
    deeper = [jnp.where(slab > k, 1.0, 0.0).astype(F32) for k in range(N_SLAB - 1)]
    window = jnp.left_shift(jnp.int32(2), slab).astype(F32)
    inverse = jnp.where(slab == 0, 0.5, jnp.where(slab == 1, 0.25, jnp.where(slab == 2, 0.125, 0.0625))).astype(F32)
    return deeper, (window, inverse)


def _window_mean(total, t0, row, window, at_start):
    if at_start:
        return total / _pool_denominator(t0, row, window[0])
    return total * window[1]


def _slab_specs(s):
    seq = pl.BlockSpec((None, s, SLAB), lambda k: (k, 0, 0))
    mat = pl.BlockSpec((1, SLAB, SLAB), lambda k: (k, 0, 0))
    vec = pl.BlockSpec((1, SLAB), lambda k: (0, k))
    taps = pl.BlockSpec((CONV_WIDTH, SLAB), lambda k: (0, k))
    return seq, mat, vec, taps


def _mixer_fwd(u_pool, u_lru, u_gate, pool_w, pool_scale, conv_w, conv_b, wa_bd, b_a, wi_bd, b_i, lam):
    s = u_pool.shape[1]
    tc = _time_chunk(s)
    n_chunks = s // tc

    def body(up_ref, ul_ref, ug_ref, pw_ref, ps_ref, cw_ref, cb_ref, wa_ref, ba_ref, wi_ref, bi_ref, lam_ref,
             yp_ref, h_ref, yl_ref, shift_ref, a_pad, b_pad):
        _scan_pads(a_pad, b_pad, tc, causal=True)
        deeper, window = _slab_scalars()
        pw = pw_ref[0].astype(BF16)
        wa = wa_ref[0].astype(BF16)
        wi = wi_ref[0].astype(BF16)
        ps, cb, ba, bi = ps_ref[...], cb_ref[...], ba_ref[...], bi_ref[...]
        cw = [cw_ref[k:k + 1, :] for k in range(CONV_WIDTH)]
        sp = _softplus_neg(lam_ref[...])
        row = lax.broadcasted_iota(jnp.int32, (tc, SLAB), 0)

        def chunk(t0, ext_p, ext_l, h_prev, at_start=False):
            rows = pl.ds(t0, tc)
            d = _window_mean(_causal_window(ext_p, deeper), t0, row, window, at_start) - ext_p[HALO:]
            yp_ref[rows, :] = _mm(d.astype(BF16), pw) * ps
            shift_ref[...] = ext_l
            xc = _conv([shift_ref[HALO - k:HALO - k + tc, :] for k in range(CONV_WIDTH)], cw, cb)
            _, _, i, a, mult = _lru_gates(xc, wa, ba, wi, bi, sp)
            h = _scan_causal(a, mult * (i * xc), h_prev, a_pad, b_pad, tc)
            h_ref[rows, :] = h
            yl_ref[rows, :] = h * _gelu(ug_ref[rows, :])[0]
            return h[tc - 1:tc, :]

        pad = jnp.zeros((HALO, SLAB), F32)
        h0 = chunk(0, jnp.concatenate([pad, up_ref[pl.ds(0, tc), :]], axis=0),
                   jnp.concatenate([pad, ul_ref[pl.ds(0, tc), :]], axis=0), jnp.zeros((1, SLAB), F32), at_start=True)

        def step(c, h_prev):
            t0 = pl.multiple_of(c * tc, tc)
            ext = pl.ds(pl.multiple_of(c * tc - HALO, HALO), tc + HALO)
            return chunk(t0, up_ref[ext, :], ul_ref[ext, :], h_prev)

        lax.fori_loop(1, n_chunks, step, h0)

    seq, mat, vec, taps = _slab_specs(s)
    out = _slab_major(s)
    return pl.pallas_call(
        body, name="mixer_fwd", grid=(N_SLAB,),
        in_specs=[seq, seq, seq, mat, vec, taps, vec, mat, vec, mat, vec, vec],
        out_specs=[seq, seq, seq], out_shape=[out, out, out],
        scratch_shapes=[pltpu.VMEM((tc + HALO, SLAB), F32), pltpu.VMEM((tc + SUBLANES, SLAB), F32),
                        pltpu.VMEM((tc + SUBLANES, SLAB), F32)],
        compiler_params=_params(48, 1),
    )(*_in_hbm(u_pool, u_lru, u_gate, pool_w, pool_scale, conv_w, conv_b, wa_bd, b_a, wi_bd, b_i, lam))


def _fwd_out(x, y_pool, y_lru, gn_pool_g, gn_lru_g, w_out, ln2_g):
    s = x.shape[0]
    tb = _token_block(s, 1024)

    def body(x_ref, yp_ref, yl_ref, gp_ref, gl_ref, w_ref, g2_ref, h1_ref, n2_ref):
        mp, _, _ = _rms(_read_slabs(yp_ref), gp_ref[...])
        ml, _, _ = _rms(_read_slabs(yl_ref), gl_ref[...])
        h1 = x_ref[...] + _mm(mp.astype(BF16), w_ref[:POOL_WIDTH, :]) + _mm(ml.astype(BF16), w_ref[POOL_WIDTH:, :])
        h1_ref[...] = h1
        n2_ref[...] = _rms(h1, g2_ref[...])[0].astype(BF16)

    row = pl.BlockSpec((tb, D_MODEL), lambda i: (i, 0))
    half = _slabs_spec(tb)
    return pl.pallas_call(
        body, name="fwd_out", grid=(s // tb,),
        in_specs=[row, half, half, pl.BlockSpec((1, 512), lambda i: (0, 0)), pl.BlockSpec((1, 512), lambda i: (0, 0)),
                  pl.BlockSpec((D_MODEL, D_MODEL), lambda i: (0, 0)), pl.BlockSpec((1, D_MODEL), lambda i: (0, 0))],
        out_specs=[row, row],
        out_shape=[pltpu.HBM((s, D_MODEL), F32), pltpu.HBM((s, D_MODEL), BF16)],
        compiler_params=_params(40, 1),
    )(*_in_hbm(x, y_pool, y_lru, gn_pool_g, gn_lru_g, w_out, ln2_g))


def _ffn_fwd(h1, n2, target, lnf_g, w_gate, w_up, w_down):
    s = h1.shape[0]
    tb = 512
    sub = 256
    n_ff = D_FF // FF_CHUNK

    def body(h1_ref, n2_ref, t_ref, gf_ref, wg_hbm, wu_hbm, wd_hbm,
             g_ref, u_ref, dh_ref, dhb_ref, dgf_ref, sq_ref, wg, wu, wd, sem):
        @pl.when(pl.program_id(0) == 0)
        def _():
            loads = [pltpu.make_async_copy(src, dst, sem.at[k])
                     for k, (src, dst) in enumerate(((wg_hbm, wg), (wu_hbm, wu), (wd_hbm, wd)))]
            for cp in loads:
                cp.start()
            for cp in loads:
                cp.wait()
            dgf_ref[...] = jnp.zeros_like(dgf_ref)
            sq_ref[...] = jnp.zeros_like(sq_ref)

        n2v = n2_ref[...]
        acc = jnp.zeros((tb, D_MODEL), F32)
        for c in range(n_ff):
            cols = slice(c * FF_CHUNK, (c + 1) * FF_CHUNK)
            g = _mm_nt(n2v, wg[cols, :])
            u = _mm_nt(n2v, wu[cols, :])
            g_ref[:, cols] = g.astype(BF16)
            u_ref[:, cols] = u.astype(BF16)
            act = g * jax.nn.sigmoid(g) * u
            acc = acc + _mm(act.astype(BF16), wd[cols, :])
        gf = gf_ref[...]
        for r in range(tb // sub):
            rows = slice(r * sub, (r + 1) * sub)
            y, xhat, rstd = _rms(h1_ref[rows, :] + acc[rows, :], gf)
            err = y - t_ref[rows, :]
            sq_ref[...] += jnp.sum(err * err, axis=0, keepdims=True)
            dh2, dgf = _rms_bwd(err * (1.0 / D_MODEL), xhat, rstd, gf)
            dgf_ref[...] += dgf
            dh_ref[rows, :] = dh2
            dhb_ref[rows, :] = dh2.astype(BF16)

    row = pl.BlockSpec((tb, D_MODEL), lambda i: (i, 0))
    ff = pl.BlockSpec((tb, D_FF), lambda i: (i, 0))
    vec = pl.BlockSpec((1, D_MODEL), lambda i: (0, 0))
    anyspace = pl.BlockSpec(memory_space=pl.ANY)
    return pl.pallas_call(
        body, name="ffn_fwd", grid=(s // tb,),
        in_specs=[row, row, row, vec, anyspace, anyspace, anyspace],
        out_specs=[ff, ff, row, row, vec, vec],
        out_shape=[pltpu.HBM((s, D_FF), BF16), pltpu.HBM((s, D_FF), BF16),
                   pltpu.HBM((s, D_MODEL), F32), pltpu.HBM((s, D_MODEL), BF16),
                   jax.ShapeDtypeStruct((1, D_MODEL), F32), jax.ShapeDtypeStruct((1, D_MODEL), F32)],
        scratch_shapes=[pltpu.VMEM((D_FF, D_MODEL), BF16), pltpu.VMEM((D_FF, D_MODEL), BF16),
                        pltpu.VMEM((D_FF, D_MODEL), BF16), pltpu.SemaphoreType.DMA((3,))],
        compiler_params=_params(60, 1),
    )(*_in_hbm(h1, n2, target, lnf_g, w_gate, w_up, w_down))


def _ffn_bwd(n2, dh2b, g, u, w_gate_t, w_up_t, w_down):
    s = n2.shape[0]
    tb = min(1024, s)
    n_ff = D_FF // FF_CHUNK
    n_tb = s // tb

    def body(n2_ref, dh_ref, g_ref, u_ref, wg_ref, wu_ref, wd_ref, dwg_ref, dwu_ref, dwd_ref, dn2_ref,
             dn2_acc, acc_g, acc_u, acc_d):
        j = pl.program_id(0)
        t = pl.program_id(1)

        @pl.when(t == 0)
        def _():
            acc_g[...] = jnp.zeros_like(acc_g)
            acc_u[...] = jnp.zeros_like(acc_u)
            acc_d[...] = jnp.zeros_like(acc_d)

        n2v = n2_ref[...]
        dh = dh_ref[...]
        gv = g_ref[...].astype(F32)
        uv = u_ref[...].astype(F32)
        sg = jax.nn.sigmoid(gv)
        silu = gv * sg
        dact = _mm_nt(dh, wd_ref[...])
        dub = (dact * silu).astype(BF16)
        dgb = (dact * uv * (sg * (1.0 + gv * (1.0 - sg)))).astype(BF16)
        acc_d[...] += _mm_tn((silu * uv).astype(BF16), dh)
        acc_g[...] += _mm_tn(dgb, n2v)
        acc_u[...] += _mm_tn(dub, n2v)
        part = _mm(jnp.concatenate([dgb, dub], axis=1), jnp.concatenate([wg_ref[...], wu_ref[...]], axis=0))
        rows = pl.ds(pl.multiple_of(t * tb, tb), tb)

        @pl.when(t == n_tb - 1)
        def _():
            dwg_ref[...] = acc_g[...].astype(BF16)
            dwu_ref[...] = acc_u[...].astype(BF16)
            dwd_ref[...] = acc_d[...].astype(BF16)

        @pl.when(j == 0)
        def _():
            dn2_acc[rows, :] = part

        @pl.when(jnp.logical_and(j > 0, j < n_ff - 1))
        def _():
            dn2_acc[rows, :] += part

        @pl.when(j == n_ff - 1)
        def _():
            dn2_ref[...] = dn2_acc[rows, :] + part

    row = pl.BlockSpec((tb, D_MODEL), lambda j, t: (t, 0))
    act = pl.BlockSpec((tb, FF_CHUNK), lambda j, t: (t, j))
    w_row = pl.BlockSpec((FF_CHUNK, D_MODEL), lambda j, t: (j, 0))
    last = pl.BlockSpec((tb, D_MODEL), lambda j, t: (jnp.where(j == n_ff - 1, t, 0), 0))
    grad = pltpu.HBM((D_FF, D_MODEL), BF16)
    chunk_acc = pltpu.VMEM((FF_CHUNK, D_MODEL), F32)
    return pl.pallas_call(
        body, name="ffn_bwd", grid=(n_ff, n_tb),
        in_specs=[row, row, act, act, w_row, w_row, w_row],
        out_specs=[w_row, w_row, w_row, last],
        out_shape=[grad, grad, grad, pltpu.HBM((s, D_MODEL), F32)],
        scratch_shapes=[pltpu.VMEM((s, D_MODEL), F32), chunk_acc, chunk_acc, chunk_acc],
        compiler_params=_params(56, 2),
    )(*_in_hbm(n2, dh2b, g, u, w_gate_t, w_up_t, w_down))


def _bwd_out(dn2, dh2, h1, y_pool, y_lru, gn_pool_g, gn_lru_g, w_out, ln2_g, after):
    s = h1.shape[0]
    tb = _token_block(s)

    def body(dn2_ref, dh2_ref, h1_ref, yp_ref, yl_ref, gp_ref, gl_ref, w_ref, g2_ref, _after,
             dh1_ref, dyp_ref, dyl_ref, dwb_ref, dg2_ref, dgp_ref, dgl_ref, dw_ref):
        @pl.when(pl.program_id(0) == 0)
        def _():
            dw_ref[...] = jnp.zeros_like(dw_ref)
            dg2_ref[...] = jnp.zeros_like(dg2_ref)
            dgp_ref[...] = jnp.zeros_like(dgp_ref)
            dgl_ref[...] = jnp.zeros_like(dgl_ref)

        g2 = g2_ref[...]
        _, xhat2, rstd2 = _rms(h1_ref[...], g2)
        dres, dg2 = _rms_bwd(dn2_ref[...], xhat2, rstd2, g2)
        dg2_ref[...] += dg2
        dh1 = dh2_ref[...] + dres
        dh1_ref[...] = dh1
        dh1b = dh1.astype(BF16)
        gp, gl = gp_ref[...], gl_ref[...]
        mp, xhat_p, rstd_p = _rms(_read_slabs(yp_ref), gp)
        ml, xhat_l, rstd_l = _rms(_read_slabs(yl_ref), gl)
        dw_ref[:POOL_WIDTH, :] += _mm_tn(mp.astype(BF16), dh1b)
        dw_ref[POOL_WIDTH:, :] += _mm_tn(ml.astype(BF16), dh1b)
        dyp, dgp = _rms_bwd(_mm_nt(dh1b, w_ref[:POOL_WIDTH, :]), xhat_p, rstd_p, gp)
        dyl, dgl = _rms_bwd(_mm_nt(dh1b, w_ref[POOL_WIDTH:, :]), xhat_l, rstd_l, gl)
        _write_slabs(dyp_ref, dyp)
        _write_slabs(dyl_ref, dyl)
        dgp_ref[...] += dgp
        dgl_ref[...] += dgl

        @pl.when(pl.program_id(0) == s // tb - 1)
        def _():
            dwb_ref[...] = dw_ref[...].astype(BF16)

    row = pl.BlockSpec((tb, D_MODEL), lambda i: (i, 0))
    half = _slabs_spec(tb)
    vec = pl.BlockSpec((1, D_MODEL), lambda i: (0, 0))
    hvec = pl.BlockSpec((1, 512), lambda i: (0, 0))
    mat = pl.BlockSpec((D_MODEL, D_MODEL), lambda i: (0, 0))
    return pl.pallas_call(
        body, name="bwd_out", grid=(s // tb,),
        in_specs=[row, row, row, half, half, hvec, hvec, mat, vec, pl.BlockSpec(memory_space=pl.ANY)],
        out_specs=[row, half, half, mat, vec, hvec, hvec],
        out_shape=[pltpu.HBM((s, D_MODEL), F32), _slab_major(s),
                   _slab_major(s), pltpu.HBM((D_MODEL, D_MODEL), BF16),
                   jax.ShapeDtypeStruct((1, D_MODEL), F32), jax.ShapeDtypeStruct((1, 512), F32),
                   jax.ShapeDtypeStruct((1, 512), F32)],
        scratch_shapes=[pltpu.VMEM((D_MODEL, D_MODEL), F32)],
        compiler_params=_params(48, 1),
    )(*_in_hbm(dn2, dh2, h1, y_pool, y_lru, gn_pool_g, gn_lru_g, w_out, ln2_g), after)


def _mixer_bwd(u_pool, u_lru, u_gate, h, dy_pool, dy_lru,
               pool_w, pool_scale, conv_w, conv_b, wa_bd, b_a, wi_bd, b_i, lam, after):
    s = u_pool.shape[1]
    tc = _time_chunk(s)
    n_chunks = s // tc

    def body(up_ref, ul_ref, ug_ref, h_ref, dyp_ref, dyl_ref,
             pw_ref, ps_ref, cw_ref, cb_ref, wa_ref, ba_ref, wi_ref, bi_ref, lam_ref, _after,
             dup_ref, dul_ref, dug_ref, dpw_ref, dps_ref, dcw_ref, dcb_ref, dwa_ref, dba_ref, dwi_ref, dbi_ref, dlam_ref,
             a_pad, b_pad, u_shift, h_shift, a_shift, d_shift):
        _scan_pads(a_pad, b_pad, tc, causal=False)
        deeper, window = _slab_scalars()
        pw = pw_ref[0].astype(BF16)
        wa = wa_ref[0].astype(BF16)
        wi = wi_ref[0].astype(BF16)
        ps, cb, ba, bi = ps_ref[...], cb_ref[...], ba_ref[...], bi_ref[...]
        cw = [cw_ref[k:k + 1, :] for k in range(CONV_WIDTH)]
        lam_v = lam_ref[...]
        sp = _softplus_neg(lam_v)
        row = lax.broadcasted_iota(jnp.int32, (tc, SLAB), 0)
        for ref in (dpw_ref, dps_ref, dcw_ref, dcb_ref, dwa_ref, dba_ref, dwi_ref, dbi_ref, dlam_ref):
            ref[...] = jnp.zeros_like(ref)

        def chunk(t0, ext_p, ext_l, ext_h, carry, at_start=False):
            l_next, a_next, dxc_next, ddn_next = carry
            rows = pl.ds(t0, tc)
            u_shift[...] = ext_l
            taps = [u_shift[HALO - k:HALO - k + tc, :] for k in range(CONV_WIDTH)]
            xc = _conv(taps, cw, cb)
            xb, r, i, a, mult = _lru_gates(xc, wa, ba, wi, bi, sp)
            hv = ext_h[HALO:]
            h_shift[...] = ext_h
            h_before = h_shift[HALO - 1:HALO - 1 + tc, :]
            ug = ug_ref[rows, :]
            dyl = dyl_ref[rows, :]
            gel, th = _gelu(ug)
            dug_ref[rows, :] = (dyl * hv * _gelu_grad(ug, th)).astype(BF16)
            a_shift[:tc, :] = a
            a_shift[tc:, :] = jnp.broadcast_to(a_next, (SUBLANES, SLAB))
            a_after = a_shift[1:1 + tc, :]
            l = _scan_anticausal(a_after, dyl * gel, l_next, a_pad, b_pad, tc)
            dmult = l * (i * xc)
            di = l * mult * xc
            dxc = l * mult * i
            dla = (l * h_before) * a - jnp.where(mult > 0.0, dmult * (a * a) / mult, 0.0)
            dlam_ref[...] += jnp.sum(dla * r, axis=0, keepdims=True)
            dpa = (dla * ((-LRU_C) * sp)) * (r * (1.0 - r))
            dpi = di * (i * (1.0 - i))
            dpab = dpa.astype(BF16)
            dpib = dpi.astype(BF16)
            dwa_ref[0] += _mm_tn(xb, dpab)
            dwi_ref[0] += _mm_tn(xb, dpib)
            dba_ref[...] += jnp.sum(dpa, axis=0, keepdims=True)
            dbi_ref[...] += jnp.sum(dpi, axis=0, keepdims=True)
            dxc = dxc + _mm_nt(dpab, wa) + _mm_nt(dpib, wi)
            d_shift[:tc, :] = dxc
            d_shift[tc:, :] = dxc_next
            dul_ref[rows, :] = (cw[3] * dxc + cw[2] * d_shift[1:1 + tc, :]
                                + cw[1] * d_shift[2:2 + tc, :] + cw[0] * d_shift[3:3 + tc, :]).astype(BF16)
            for k in range(CONV_WIDTH):
                dcw_ref[k:k + 1, :] += jnp.sum(dxc * taps[CONV_WIDTH - 1 - k], axis=0, keepdims=True)
            dcb_ref[...] += jnp.sum(dxc, axis=0, keepdims=True)
            db = (_window_mean(_causal_window(ext_p, deeper), t0, row, window, at_start) - ext_p[HALO:]).astype(BF16)
            dyp = dyp_ref[rows, :]
            dps_ref[...] += jnp.sum(dyp * _mm(db, pw), axis=0, keepdims=True)
            dys = (dyp * ps).astype(BF16)
            dpw_ref[0] += _mm_tn(db, dys)
            dd = _mm_nt(dys, pw)
            ddn = _window_mean(dd, t0, row, window, at_start)
            ext_q = jnp.concatenate([ddn, ddn_next], axis=0)
            dup_ref[rows, :] = (_anticausal_window(ext_q, deeper, tc) - dd).astype(BF16)
            return l[0:1, :], a[0:1, :], dxc[0:8, :], ddn[0:HALO, :]

        def step(k, carry):
            c = n_chunks - 1 - k
            t0 = pl.multiple_of(c * tc, tc)
            ext = pl.ds(pl.multiple_of(c * tc - HALO, HALO), tc + HALO)
            return chunk(t0, up_ref[ext, :], ul_ref[ext, :], h_ref[ext, :], carry)

        carry = (jnp.zeros((1, SLAB), F32), jnp.zeros((1, SLAB), F32),
                 jnp.zeros((8, SLAB), F32), jnp.zeros((HALO, SLAB), F32))
        carry = lax.fori_loop(0, n_chunks - 1, step, carry)
        pad = jnp.zeros((HALO, SLAB), F32)
        first = pl.ds(0, tc)
        chunk(0, jnp.concatenate([pad, up_ref[first, :]], axis=0), jnp.concatenate([pad, ul_ref[first, :]], axis=0),
              jnp.concatenate([pad, h_ref[first, :]], axis=0), carry, at_start=True)
        dlam_ref[...] = dlam_ref[...] * (LRU_C * jax.nn.sigmoid(-lam_v))

    seq, mat, vec, taps = _slab_specs(s)
    grad = _slab_major(s, BF16)
    mats = jax.ShapeDtypeStruct((N_SLAB, SLAB, SLAB), F32)
    vecs = jax.ShapeDtypeStruct((1, 512), F32)
    return pl.pallas_call(
        body, name="mixer_bwd", grid=(N_SLAB,),
        in_specs=[seq] * 6 + [mat, vec, taps, vec, mat, vec, mat, vec, vec, pl.BlockSpec(memory_space=pl.ANY)],
        out_specs=[seq, seq, seq, mat, vec, taps, vec, mat, vec, mat, vec, vec],
        out_shape=[grad, grad, grad, mats, vecs, jax.ShapeDtypeStruct((CONV_WIDTH, 512), F32), vecs,
                   mats, vecs, mats, vecs, vecs],
        scratch_shapes=[pltpu.VMEM((tc + SUBLANES, SLAB), F32), pltpu.VMEM((tc + SUBLANES, SLAB), F32),
                        pltpu.VMEM((tc + HALO, SLAB), F32), pltpu.VMEM((tc + HALO, SLAB), F32),
                        pltpu.VMEM((tc + SUBLANES, SLAB), F32), pltpu.VMEM((tc + SUBLANES, SLAB), F32)],
        compiler_params=_params(56, 1),
    )(*_in_hbm(u_pool, u_lru, u_gate, h, dy_pool, dy_lru, pool_w, pool_scale, conv_w, conv_b, wa_bd, b_a, wi_bd,
               b_i, lam), after)


def _bwd_in(x, dh1, du_pool, du_lru, du_gate, ln1_g, w_in_t, after):
    s = x.shape[0]
    tb = _token_block(s)

    def body(x_ref, dh1_ref, dup_ref, dul_ref, dug_ref, g_ref, w_ref, _after, dx_ref, dwb_ref, dg_ref, dw_ref):
        @pl.when(pl.program_id(0) == 0)
        def _():
            dw_ref[...] = jnp.zeros_like(dw_ref)
            dg_ref[...] = jnp.zeros_like(dg_ref)

        g1 = g_ref[...]
        n, xhat, rstd = _rms(x_ref[...], g1)
        nb = n.astype(BF16)
        db = jnp.concatenate([_read_slabs(ref) for ref in (dup_ref, dul_ref, dug_ref)], axis=1)
        dw_ref[...] += _mm_tn(db, nb)
        dx, dg1 = _rms_bwd(_mm(db, w_ref[...]), xhat, rstd, g1)
        dx_ref[...] = dh1_ref[...] + dx
        dg_ref[...] += dg1

        @pl.when(pl.program_id(0) == s // tb - 1)
        def _():
            dwb_ref[...] = dw_ref[...].astype(BF16)

    row = pl.BlockSpec((tb, D_MODEL), lambda i: (i, 0))
    half = _slabs_spec(tb)
    vec = pl.BlockSpec((1, D_MODEL), lambda i: (0, 0))
    mat = pl.BlockSpec((IN_WIDTH, D_MODEL), lambda i: (0, 0))
    return pl.pallas_call(
        body, name="bwd_in", grid=(s // tb,),
        in_specs=[row, row, half, half, half, vec, mat, pl.BlockSpec(memory_space=pl.ANY)],
        out_specs=[row, mat, vec],
        out_shape=[pltpu.HBM((s, D_MODEL), F32), pltpu.HBM((IN_WIDTH, D_MODEL), BF16),
                   jax.ShapeDtypeStruct((1, D_MODEL), F32)],
        scratch_shapes=[pltpu.VMEM((IN_WIDTH, D_MODEL), F32)],
        compiler_params=_params(48, 1),
    )(*_in_hbm(x, dh1, du_pool, du_lru, du_gate, ln1_g, w_in_t), after)


def _mesh_position():
    x, y, c = lax.axis_index("x"), lax.axis_index("y"), lax.axis_index("c")
    return x, y, c, 4 * x + 2 * y + c


def _peer(x, y, c, p):
    px = 1 - x if p & 4 else x
    py = 1 - y if p & 2 else y
    pc = 1 - c if p & 1 else c
    return (px, py, pc), 4 * px + 2 * py + pc


HBM_SPEC = pl.BlockSpec(memory_space=pltpu.HBM)
SEM_SPEC = pl.BlockSpec(memory_space=pltpu.SEMAPHORE)
DATAFLOW = pltpu.SideEffectType.DATAFLOW_SIDE_EFFECTING


class Copy(NamedTuple):
    src: int
    src_at: Any
    dst: int
    dst_at: Any
    peer: int
    group: int
    slot: int


SIBLING = (1,)
SAME_CORE = (2, 4, 6)
EVERYONE = tuple(range(1, N_DEV))
MYSELF = (0,)


def _same(index):
    return index


def _chip(index):
    return jnp.right_shift(index, 1)


def _fan_out(srcs, lands, peers, group):
    return [Copy(s, None, d, _same, p, group, N_DEV * i + p) for i, (s, d) in enumerate(zip(srcs, lands)) for p in peers]


def _scatter(stacks, lands, peers):
    return [Copy(s, lambda me, p=p: jnp.bitwise_xor(me, p), d, _same, p, 0, 0)
            for s, d in zip(stacks, lands) for p in peers]


def _numbered(copies, group=0):
    return [cp._replace(group=group, slot=i) for i, cp in enumerate(copies)]


def _relay(lands, peers):
    return [Copy(b, lambda s, q=q: jnp.bitwise_xor(s, q), b, lambda s, q=q: jnp.bitwise_xor(s, q), 1, 0, N_DEV * i + q)
            for i, b in enumerate(lands) for q in peers]


def _to_sibling(stacks, lands):
    return [Copy(s, lambda me, k=k: 2 * k + 1 - jnp.bitwise_and(me, 1), d, lambda me, k=k: k, 1, 0, 4 * i + k)
            for i, (s, d) in enumerate(zip(stacks, lands)) for k in range(N_DEV // 2)]


def _to_chips(sums, lands):
    return [Copy(s, lambda me, p=p: jnp.bitwise_xor(_chip(me), p // 2), d, _chip, p, 0, 4 * i + p // 2)
            for i, (s, d) in enumerate(zip(sums, lands)) for p in SAME_CORE]


def _comm_call(name, bufs, wait=None, start=None, after=()):
    nb = len(bufs)
    slots = list(start[1]) if start else []
    n_out_sem = 2 * len(slots)
    after = [a for a in (after if isinstance(after, (list, tuple)) else [after]) if a is not None]

    def body(*refs):
        b = refs[:nb]
        at = nb
        if wait:
            w_send, w_recv = refs[at], refs[at + 1]
            at += 2
        at += len(after)
        out_sems = refs[at:at + n_out_sem]
        token = refs[at + n_out_sem + nb]
        x, y, c, me = _mesh_position()

        def part(i, row_of, sender):
            return b[i] if row_of is None else b[i].at[row_of(sender)]

        if wait:
            for cp in wait[1]:
                peer, peer_index = _peer(x, y, c, cp.peer)
                arrival = pltpu.make_async_remote_copy(part(cp.src, cp.src_at, me), part(cp.dst, cp.dst_at, peer_index),
                                                       w_send.at[cp.slot], w_recv.at[cp.slot],
                                                       device_id=peer, device_id_type=MESH)
                arrival.wait_send()
                arrival.wait_recv()
        if start:
            for cp in start[0]:
                peer, _ = _peer(x, y, c, cp.peer)
                pltpu.make_async_remote_copy(part(cp.src, cp.src_at, me), part(cp.dst, cp.dst_at, me),
                                             out_sems[2 * cp.group].at[cp.slot], out_sems[2 * cp.group + 1].at[cp.slot],
                                             device_id=peer, device_id_type=MESH).start()
        token[...] = jnp.zeros_like(token)

    sem_shapes = []
    for n_slots in slots:
        sem_shapes += [pltpu.SemaphoreType.DMA((n_slots,))] * 2
    operands = [pltpu.with_memory_space_constraint(a, pltpu.HBM) for a in bufs]
    in_specs = [HBM_SPEC] * nb
    if wait:
        operands += list(wait[0])
        in_specs += [SEM_SPEC, SEM_SPEC]
    operands += after
    in_specs += [pl.BlockSpec(memory_space=pl.ANY)] * len(after)
    outs = pl.pallas_call(
        body, name=name, in_specs=in_specs,
        out_specs=[SEM_SPEC] * n_out_sem + [HBM_SPEC] * nb + [pl.BlockSpec(memory_space=pltpu.VMEM)],
        out_shape=sem_shapes + [pltpu.HBM(a.shape, a.dtype) for a in bufs] + [jax.ShapeDtypeStruct((8, SLAB), F32)],
        input_output_aliases={i: n_out_sem + i for i in range(nb)},
        compiler_params=pltpu.CompilerParams(has_side_effects=DATAFLOW),
    )(*operands)
    sems = [(outs[2 * k], outs[2 * k + 1]) for k in range(len(slots))]
    return sems, list(outs[n_out_sem:n_out_sem + nb]), outs[-1]


def _pair_sum(stacks, lands, place):
    n = len(stacks)

    def body(place_ref, *refs):
        k = pl.program_id(0)
        for m in range(n):
            mine, theirs, out, land = refs[m], refs[n + m], refs[2 * n + m], refs[3 * n + m]
            total = (mine[0, 0].astype(F32) + theirs[0].astype(F32)).astype(out.dtype)
            out[0] = total

            @pl.when(k == place_ref[1])
            def _():
                land[0] = total

    in_specs = [pl.BlockSpec((1, 1) + a.shape[1:], lambda k, place_ref: (k, place_ref[0], 0, 0)) for a in stacks]
    in_specs += [pl.BlockSpec((1,) + a.shape[1:], lambda k, place_ref: (k, 0, 0)) for a in lands]
    out_specs = [pl.BlockSpec((1,) + a.shape[1:], lambda k, place_ref: (k, 0, 0)) for a in lands]
    out_specs += [pl.BlockSpec((1,) + a.shape[1:], lambda k, place_ref: (place_ref[1], 0, 0)) for a in lands]
    outs = pl.pallas_call(
        body, name="pair_sum_" + "_".join(str(a.shape[1]) for a in stacks),
        grid_spec=pltpu.PrefetchScalarGridSpec(num_scalar_prefetch=1, grid=(N_DEV // 2,), in_specs=in_specs,
                                               out_specs=out_specs),
        out_shape=[pltpu.HBM(a.shape, a.dtype) for a in lands] * 2,
        compiler_params=_params(40, 1),
    )(place, *_in_hbm(*[a.reshape((N_DEV // 2, 2) + a.shape[1:]) for a in stacks], *lands))
    return list(outs[:n]), list(outs[n:])


def _reduce_adam(parts, w, m, v, name):
    rows, cols = w.shape
    n_parts = parts.shape[0]
    rb = rows
    for cand in (256, 176, 128):
        if rows % cand == 0 and rows > cand:
            rb = cand
            break

    def body(p_ref, w_ref, m_ref, v_ref, g_out, d_out, m_out, v_out):
        g = p_ref[0].astype(F32)
        for j in range(1, n_parts):
            g = g + p_ref[j].astype(F32)
        g_out[...] = g
        d_out[...], m_out[...], v_out[...] = _adam(g, w_ref[...], m_ref[...], v_ref[...])

    blk = pl.BlockSpec((rb, cols), lambda i: (i, 0))
    out = jax.ShapeDtypeStruct((rows, cols), F32)
    return pl.pallas_call(
        body, name=name, grid=(rows // rb,),
        in_specs=[pl.BlockSpec((n_parts, rb, cols), lambda i: (0, i, 0)), blk, blk, blk],
        out_specs=[blk] * 4, out_shape=[out] * 4,
        compiler_params=_params(40, 1),
    )(*_in_hbm(parts, w, m, v))


def _cols_from_stack(stack):
    n, r, c = stack.shape
    return jnp.transpose(stack, (1, 0, 2)).reshape(r, n * c)


def _block_diag(w):
    z = jnp.zeros((N_SLAB, 64, 64), w.dtype)
    pairs = w.reshape(N_SLAB, 2, 64, 64)
    top = jnp.concatenate([pairs[:, 0], z], axis=2)
    bottom = jnp.concatenate([z, pairs[:, 1]], axis=2)
    return jnp.concatenate([top, bottom], axis=1)


def _adam(g, w, m, v):
    m_new = ADAM_B1 * m + (1.0 - ADAM_B1) * g
    v_new = ADAM_B2 * v + (1.0 - ADAM_B2) * (g * g)
    m_hat = m_new / (1.0 - ADAM_B1 ** ADAM_STEP)
    v_hat = v_new / (1.0 - ADAM_B2 ** ADAM_STEP)
    return (-ADAM_LR) * (m_hat / (jnp.sqrt(v_hat) + ADAM_EPS) + ADAM_WD * w), m_new, v_new


WIDE = ("ln2_g", "lnf_g")
HALF = ("pool_scale", "conv_b", "b_a", "b_i", "lam", "gn_pool_g", "gn_lru_g")
VECTORS = [(k, D_MODEL) for k in WIDE] + [(k, 512) for k in HALF]
VECTOR_ROWS = sum(width // SLAB for _, width in VECTORS)
LOSS_ROW = -(-VECTOR_ROWS // 8) * 8
CONV_AT = LOSS_ROW + 8
CONV_LANES = LRU_WIDTH // SLAB
PACK_F_ROWS = CONV_AT + CONV_WIDTH * CONV_LANES
MATRIX_ROWS = N_SLAB * SLAB
HEAD = SLAB // 2
GATE_ROWS = N_SLAB * HEAD
PACK_B_ROWS = MATRIX_ROWS + 2 * GATE_ROWS


def _pack_small(vectors, pool_g, wa_g, wi_g, conv, sq):
    n_vec = len(vectors)

    def body(*refs):
        vec = refs[:n_vec]
        pw_ref, wa_ref, wi_ref, cw_ref, sq_ref, out, out_b = refs[n_vec:]
        out[...] = jnp.zeros_like(out)
        row = 0
        for ref, (_, width) in zip(vec, VECTORS):
            for k in range(width // SLAB):
                out[row:row + 1, :] = ref[:, k * SLAB:(k + 1) * SLAB]
                row += 1
        for tap in range(CONV_WIDTH):
            for k in range(CONV_LANES):
                at = CONV_AT + tap * CONV_LANES + k
                out[at:at + 1, :] = cw_ref[tap:tap + 1, k * SLAB:(k + 1) * SLAB]
        total = sq_ref[:, 0:SLAB]
        for k in range(1, D_MODEL // SLAB):
            total = total + sq_ref[:, k * SLAB:(k + 1) * SLAB]
        out[LOSS_ROW:LOSS_ROW + 1, :] = total
        left = lax.broadcasted_iota(jnp.int32, (HEAD, SLAB), 1) < HEAD
        for s in range(N_SLAB):
            out_b[s * SLAB:(s + 1) * SLAB, :] = pw_ref[s].astype(BF16)
            for i, ref in enumerate((wa_ref, wi_ref)):
                at = MATRIX_ROWS + i * GATE_ROWS + s * HEAD
                out_b[at:at + HEAD, :] = jnp.where(left, ref[s, 0:HEAD, :], ref[s, HEAD:SLAB, :]).astype(BF16)

    return pl.pallas_call(
        body, name="pack_small",
        out_shape=[jax.ShapeDtypeStruct((PACK_F_ROWS, SLAB), F32), jax.ShapeDtypeStruct((PACK_B_ROWS, SLAB), BF16)],
    )(*vectors, pool_g, wa_g, wi_g, conv, sq)


def _small_reduce_adam(parts, parts_b, vec_w, vec_m, vec_v, pool_wmv):
    n_vec = len(VECTORS)
    n_parts = parts.shape[0]

    def body(*refs):
        p_ref, pb_ref = refs[0], refs[1]
        w_refs, m_refs, v_refs = (refs[2 + k * n_vec:2 + (k + 1) * n_vec] for k in range(3))
        pw_w, pw_m, pw_v = refs[2 + 3 * n_vec:5 + 3 * n_vec]
        outs = refs[5 + 3 * n_vec:-1]
        total = refs[-1]
        total[...] = p_ref[0]
        for j in range(1, n_parts):
            total[...] += p_ref[j]
        row = 0
        for i, (_, width) in enumerate(VECTORS):
            n_rows = width // SLAB
            g = jnp.concatenate([total[row + k:row + k + 1, :] for k in range(n_rows)], axis=1)
            row += n_rows
            d, m_new, v_new = _adam(g, w_refs[i][...], m_refs[i][...], v_refs[i][...])
            for ref, val in zip(outs[4 * i:4 * i + 4], (g, d, m_new, v_new)):
                ref[...] = val
        tail = outs[4 * n_vec:]

        def summed(first, count):
            g = pb_ref[0, first:first + count, :].astype(F32)
            for j in range(1, n_parts):
                g = g + pb_ref[j, first:first + count, :].astype(F32)
            return g

        g = summed(0, MATRIX_ROWS)
        d, m_new, v_new = _adam(g, pw_w[...], pw_m[...], pw_v[...])
        for ref, val in zip(tail[0:4], (g, d, m_new, v_new)):
            ref[...] = val
        tail[4][...] = summed(MATRIX_ROWS, GATE_ROWS)
        tail[5][...] = summed(MATRIX_ROWS + GATE_ROWS, GATE_ROWS)
        for tap in range(CONV_WIDTH):
            at = CONV_AT + tap * CONV_LANES
            tail[6][tap:tap + 1, :] = jnp.concatenate([total[at + k:at + k + 1, :] for k in range(CONV_LANES)], axis=1)
        tail[7][...] = (0.5 / D_MODEL) * jnp.sum(total[LOSS_ROW:LOSS_ROW + 1, :], axis=1, keepdims=True)

    out_shape = []
    for _, width in VECTORS:
        out_shape += [jax.ShapeDtypeStruct((1, width), F32)] * 4
    out_shape += [jax.ShapeDtypeStruct((MATRIX_ROWS, SLAB), F32)] * 4 + [jax.ShapeDtypeStruct((GATE_ROWS, SLAB), F32)] * 2
    out_shape += [jax.ShapeDtypeStruct((CONV_WIDTH, LRU_WIDTH), F32), jax.ShapeDtypeStruct((1, 1), F32)]
    outs = pl.pallas_call(
        body, name="adam_small", out_shape=out_shape,
        scratch_shapes=[pltpu.VMEM((PACK_F_ROWS, SLAB), F32)],
        compiler_params=_params(40),
    )(parts, parts_b, *vec_w, *vec_m, *vec_v, *pool_wmv)
    vec_out = [tuple(outs[4 * i:4 * i + 4]) for i in range(n_vec)]
    tail = outs[4 * n_vec:]
    return vec_out, tuple(tail[0:4]), tail[4], tail[5], tail[6], tail[7]


def _plain_adam(grads, ws, ms, vs):
    n = len(grads)

    def body(*refs):
        ins, outs = refs[:4 * n], refs[4 * n:]
        for i in range(n):
            d, m_new, v_new = _adam(ins[i][...], ins[n + i][...], ins[2 * n + i][...], ins[3 * n + i][...])
            for ref, val in zip(outs[3 * i:3 * i + 3], (d, m_new, v_new)):
                ref[...] = val

    out_shape = []
    for g in grads:
        out_shape += [jax.ShapeDtypeStruct(g.shape, F32)] * 3
    outs = pl.pallas_call(body, name="adam_plain", out_shape=out_shape)(*grads, *ws, *ms, *vs)
    return [tuple(outs[3 * i:3 * i + 3]) for i in range(n)]


def kernel(x, ln1_g, w_in, pool_w, pool_scale, conv_w, conv_b, w_a, b_a, w_i, b_i, lam, gn_pool_g, gn_lru_g, w_out, ln2_g, w_ffn_gate, w_ffn_up, w_ffn_down, lnf_g, loss_target, m_ln1_g, m_w_in, m_pool_w, m_pool_scale, m_conv_w, m_conv_b, m_w_a, m_b_a, m_w_i, m_b_i, m_lam, m_gn_pool_g, m_gn_lru_g, m_w_out, m_ln2_g, m_w_ffn_gate, m_w_ffn_up, m_w_ffn_down, m_lnf_g, v_ln1_g, v_w_in, v_pool_w, v_pool_scale, v_conv_w, v_conv_b, v_w_a, v_b_a, v_w_i, v_b_i, v_lam, v_gn_pool_g, v_gn_lru_g, v_w_out, v_ln2_g, v_w_ffn_gate, v_w_ffn_up, v_w_ffn_down, v_lnf_g):
    weights = dict(ln1_g=ln1_g, w_in=w_in, pool_w=pool_w, pool_scale=pool_scale, conv_w=conv_w, conv_b=conv_b,
                   w_a=w_a, b_a=b_a, w_i=w_i, b_i=b_i, lam=lam, gn_pool_g=gn_pool_g, gn_lru_g=gn_lru_g,
                   w_out=w_out, ln2_g=ln2_g, w_ffn_gate=w_ffn_gate, w_ffn_up=w_ffn_up, w_ffn_down=w_ffn_down,
                   lnf_g=lnf_g)
    mom1 = dict(ln1_g=m_ln1_g, w_in=m_w_in, pool_w=m_pool_w, pool_scale=m_pool_scale, conv_w=m_conv_w,
                conv_b=m_conv_b, w_a=m_w_a, b_a=m_b_a, w_i=m_w_i, b_i=m_b_i, lam=m_lam, gn_pool_g=m_gn_pool_g,
                gn_lru_g=m_gn_lru_g, w_out=m_w_out, ln2_g=m_ln2_g, w_ffn_gate=m_w_ffn_gate,
                w_ffn_up=m_w_ffn_up, w_ffn_down=m_w_ffn_down, lnf_g=m_lnf_g)
    mom2 = dict(ln1_g=v_ln1_g, w_in=v_w_in, pool_w=v_pool_w, pool_scale=v_pool_scale, conv_w=v_conv_w,
                conv_b=v_conv_b, w_a=v_w_a, b_a=v_b_a, w_i=v_w_i, b_i=v_b_i, lam=v_lam, gn_pool_g=v_gn_pool_g,
                gn_lru_g=v_gn_lru_g, w_out=v_w_out, ln2_g=v_ln2_g, w_ffn_gate=v_w_ffn_gate,
                w_ffn_up=v_w_ffn_up, w_ffn_down=v_w_ffn_down, lnf_g=v_lnf_g)

    xs = x[0]
    target = loss_target[0]

    shard = dict(w_in=lambda a: a[0].T, w_ffn_gate=lambda a: a[0].T, w_ffn_up=lambda a: a[0].T,
                 w_out=lambda a: a[0], w_ffn_down=lambda a: a[0], conv_w=lambda a: a[0])
    unshard = dict(w_in=lambda a: a.T[None], w_ffn_gate=lambda a: a.T[None], w_ffn_up=lambda a: a.T[None],
                   w_out=lambda a: a[None], w_ffn_down=lambda a: a[None], conv_w=lambda a: a[None])

    gathered = ("w_in", "conv_w", "w_out", "w_ffn_gate", "w_ffn_up", "w_ffn_down")
    groups = ((0, 1), (2,), (3, 4, 5))
    sources = [shard[k](weights[k]) if k == "conv_w" else shard[k](weights[k]).astype(BF16) for k in gathered]
    my_index = 4 * lax.axis_index("x") + 2 * lax.axis_index("y") + lax.axis_index("c")
    lands = [lax.empty((N_DEV,) + a.shape, a.dtype) for a in sources]

    def first_hop(n, group=0):
        return _numbered(_fan_out(range(n), range(n, 2 * n), SAME_CORE + SIBLING + MYSELF, 0), group)

    start = []
    for g, members in enumerate(groups):
        start += _numbered(_fan_out(members, [6 + m for m in members], SAME_CORE + SIBLING + MYSELF, 0), g)
    sems, bufs, _ = _comm_call("gather_start", sources + lands,
                               start=(start, [sum(cp.group == g for cp in start) for g in range(len(groups))]))
    sources, lands = bufs[:6], bufs[6:]

    def gathered_group(tag, g, after):
        members = groups[g]
        n = len(members)
        relay = _numbered(_relay(range(n, 2 * n), SAME_CORE))
        relay_sems, bufs, _ = _comm_call("gather_relay_" + tag, [sources[m] for m in members] + [lands[m] for m in members],
                                         wait=(sems[g], first_hop(n)), start=(relay, (len(relay),)), after=after)
        _, bufs, _ = _comm_call("gather_wait_" + tag, bufs[n:], wait=(relay_sems[0], _numbered(_relay(range(n), SAME_CORE))))
        return bufs

    g_in, g_conv = gathered_group("in", 0, None)
    w_in_f = g_in.reshape(IN_WIDTH, D_MODEL)
    conv_w_f = _cols_from_stack(g_conv)

    wa_bd = _block_diag(w_a[0])
    wi_bd = _block_diag(w_i[0])
    lnf_row = lnf_g.reshape(1, D_MODEL)

    u_pool, u_lru, u_gate = _fwd_in(xs, ln1_g, w_in_f)
    y_pool, h, y_lru = _mixer_fwd(u_pool, u_lru, u_gate, pool_w[0], pool_scale, conv_w_f, conv_b,
                                  wa_bd, b_a, wi_bd, b_i, lam)
    (g_out,) = gathered_group("out", 1, y_pool)
    w_out_f = g_out.reshape(D_MODEL, D_MODEL)
    h1, n2 = _fwd_out(xs, y_pool, y_lru, gn_pool_g, gn_lru_g, w_out_f, ln2_g)
    g_gate, g_up, g_down = gathered_group("ffn", 2, n2)
    w_gate_f = g_gate.reshape(D_FF, D_MODEL)
    w_up_f = g_up.reshape(D_FF, D_MODEL)
    w_down_f = g_down.reshape(D_FF, D_MODEL)
    g_act, u_act, dh2, dh2b, d_lnf, sq = _ffn_fwd(h1, n2, target, lnf_row, w_gate_f, w_up_f, w_down_f)

    def direct(tag, stacks, wholes, after):
        sources = list(stacks) + list(wholes)
        n, n_st = len(sources), len(stacks)
        lands = [lax.empty(a.shape if i < n_st else (N_DEV,) + a.shape, a.dtype) for i, a in enumerate(sources)]
        copies = _numbered(_scatter(range(n_st), range(n, n + n_st), EVERYONE + MYSELF)
                           + _fan_out(range(n_st, n), range(n + n_st, 2 * n), EVERYONE + MYSELF, 0))
        sem, bufs, token = _comm_call(tag + "_start", sources + lands, start=(copies, (len(copies),)), after=after)
        return (tag, sem[0], bufs, copies), token

    def direct_finish(started, after):
        tag, sem, bufs, copies = started
        _, bufs, _ = _comm_call(tag + "_wait", bufs, wait=(sem, copies), after=after)
        return bufs[len(bufs) // 2:]

    place = jnp.stack([lax.axis_index("c"), 2 * lax.axis_index("x") + lax.axis_index("y")]).astype(jnp.int32)

    d_gate, d_up, d_down, dn2 = _ffn_bwd(n2, dh2b, g_act, u_act, w_gate_f, w_up_f, w_down_f)
    down_started, token = direct("down", [d_down.reshape(N_DEV, D_FF // N_DEV, D_MODEL)], [], None)
    ffn_stacks = [d.reshape(N_DEV, D_FF // N_DEV, D_MODEL) for d in (d_gate, d_up)]
    pair_lands = [lax.empty((N_DEV // 2,) + a.shape[1:], a.dtype) for a in ffn_stacks]
    pair_copies = _numbered(_to_sibling(range(2), range(2, 4)))
    sem, bufs, token = _comm_call("ffn_pair_start", ffn_stacks + pair_lands,
                                  start=(pair_copies, (len(pair_copies),)), after=token)
    dh1, dy_pool, dy_lru, d_out, d_ln2, d_gnp, d_gnl = _bwd_out(dn2, dh2, h1, y_pool, y_lru, gn_pool_g, gn_lru_g,
                                                                 w_out_f, ln2_g, token)
    _, bufs, _ = _comm_call("ffn_pair_wait", bufs, wait=(sem[0], pair_copies), after=dh1)
    ffn_sums, ffn_lands = _pair_sum(bufs[:2], bufs[2:], place)
    ffn_copies = _numbered(_to_chips(range(2), range(2, 4)))
    ffn_sem, ffn_bufs, token = _comm_call("ffn_chip_start", ffn_sums + ffn_lands, start=(ffn_copies, (len(ffn_copies),)))
    (du_pool, du_lru, du_gate, d_pw, d_ps, d_cw, d_cb, d_wa, d_ba, d_wi, d_bi, d_lam) = _mixer_bwd(
        u_pool, u_lru, u_gate, h, dy_pool, dy_lru, pool_w[0], pool_scale, conv_w_f, conv_b,
        wa_bd, b_a, wi_bd, b_i, lam, token)

    vec_grads = dict(ln2_g=d_ln2, lnf_g=d_lnf, pool_scale=d_ps, conv_b=d_cb, b_a=d_ba, b_i=d_bi,
                     lam=d_lam, gn_pool_g=d_gnp, gn_lru_g=d_gnl)
    packed, packed_b = _pack_small([vec_grads[k] for k, _ in VECTORS], d_pw, d_wa, d_wi, d_cw, sq)
    small_started, token = direct("small", [d_out.reshape(N_DEV, D_MODEL // N_DEV, D_MODEL)], [packed, packed_b], None)
    grad_x, d_in, d_ln1 = _bwd_in(xs, dh1, du_pool, du_lru, du_gate, ln1_g, w_in_f, token)
    in_started, token = direct("in", [d_in.reshape(N_DEV, IN_WIDTH // N_DEV, D_MODEL)], [d_ln1], None)

    results = {}

    def reduce_adam(name, parts):
        outs = _reduce_adam(parts, shard[name](weights[name]), shard[name](mom1[name]), shard[name](mom2[name]),
                            "adam_" + name)
        results[name] = tuple(unshard[name](o) for o in outs)
        return outs[0]

    shard["ln1_g"] = unshard["ln1_g"] = lambda a: a
    _, ffn_bufs, _ = _comm_call("ffn_chip_wait", ffn_bufs, wait=(ffn_sem[0], ffn_copies), after=token)
    done = [reduce_adam(name, parts) for name, parts in zip(("w_ffn_gate", "w_ffn_up"), ffn_bufs[2:])]
    (r_down,) = direct_finish(down_started, done)
    done.append(reduce_adam("w_ffn_down", r_down))
    r_out, r_small, r_small_b = direct_finish(small_started, done)
    done.append(reduce_adam("w_out", r_out))

    def as_row(a, width):
        return a.reshape(1, width)

    def as_matrix(a):
        return a.reshape(MATRIX_ROWS, SLAB)

    def as_heads(a):
        return a.reshape(2 * GATE_ROWS, HEAD)

    def heads_apart(g):
        return jnp.transpose(g.reshape(N_SLAB, HEAD, 2, HEAD), (0, 2, 1, 3)).reshape(2 * GATE_ROWS, HEAD)

    vec_out, pool_out, g_wa, g_wi, g_conv, loss_11 = _small_reduce_adam(
        r_small, r_small_b, [as_row(weights[k], w) for k, w in VECTORS], [as_row(mom1[k], w) for k, w in VECTORS],
        [as_row(mom2[k], w) for k, w in VECTORS], [as_matrix(t["pool_w"]) for t in (weights, mom1, mom2)])
    for (k, _), outs in zip(VECTORS, vec_out):
        results[k] = tuple(o.reshape(weights[k].shape) for o in outs)
    results["pool_w"] = tuple(o.reshape(pool_w.shape) for o in pool_out)
    my_columns = conv_w.shape[-1]
    plain_names = ("w_a", "w_i", "conv_w")
    plain_grads = [heads_apart(g) for g in (g_wa, g_wi)]
    plain_grads.append(lax.dynamic_slice_in_dim(g_conv, my_index * my_columns, my_columns, axis=1))
    views = (as_heads, as_heads, lambda a: a[0])
    plain_out = _plain_adam(plain_grads, *[[view(t[k]) for k, view in zip(plain_names, views)]
                                           for t in (weights, mom1, mom2)])
    for k, g, outs in zip(plain_names, plain_grads, plain_out):
        results[k] = tuple(o.reshape(weights[k].shape) for o in (g,) + outs)
    loss = loss_11[0, 0]
    r_in, r_ln1 = direct_finish(in_started, done + [plain_out[0][0], loss_11])
    reduce_adam("w_in", r_in)
    reduce_adam("ln1_g", r_ln1)

    order = ["ln1_g", "w_in", "pool_w", "pool_scale", "conv_w", "conv_b", "w_a", "b_a", "w_i", "b_i", "lam",
             "gn_pool_g", "gn_lru_g", "w_out", "ln2_g", "w_ffn_gate", "w_ffn_up", "w_ffn_down", "lnf_g"]
    return (loss, grad_x[None],
            *[results[k][0] for k in order], *[results[k][1] for k in order],
            *[results[k][2] for k in order], *[results[k][3] for k in order])
```

```python
from typing import Any, NamedTuple

import jax
import jax.numpy as jnp
from jax import lax
from jax.experimental import pallas as pl
from jax.experimental.pallas import tpu as pltpu

F32 = jnp.float32
BF16 = jnp.bfloat16

N_DEV = 8
D_MODEL = 1024
POOL_WIDTH = 512
LRU_WIDTH = 512
IN_WIDTH = 1536
D_FF = 2816
N_SLAB = 4
SLAB = 128
CONV_WIDTH = 4
LRU_C = 8.0
EPS = 1e-6
HALO = 16
FF_CHUNK = 256

ADAM_LR = 0.001
ADAM_B1 = 0.9
ADAM_B2 = 0.999
ADAM_EPS = 1e-08
ADAM_WD = 0.01
ADAM_STEP = 10

MIB = 1 << 20
MESH = pl.DeviceIdType.MESH


def _params(vmem_mib, n_axes=0):
    sem = ("arbitrary",) * n_axes if n_axes else None
    return pltpu.CompilerParams(dimension_semantics=sem, vmem_limit_bytes=vmem_mib * MIB)


def _in_hbm(*arrays):
    return [pltpu.with_memory_space_constraint(a, pltpu.HBM) for a in arrays]


def _mm(a, b):
    return jnp.dot(a, b, preferred_element_type=F32)


def _mm_nt(a, b):
    return lax.dot_general(a, b, (((1,), (1,)), ((), ())), preferred_element_type=F32)


def _mm_tn(a, b):
    return lax.dot_general(a, b, (((0,), (0,)), ((), ())), preferred_element_type=F32)


def _rms(x, g):
    rstd = lax.rsqrt(jnp.mean(x * x, axis=-1, keepdims=True) + EPS)
    xhat = x * rstd
    return xhat * g, xhat, rstd


def _rms_bwd(dy, xhat, rstd, g):
    gy = dy * g
    dx = rstd * (gy - xhat * jnp.mean(gy * xhat, axis=-1, keepdims=True))
    return dx, jnp.sum(dy * xhat, axis=0, keepdims=True)


def _gelu(z):
    t = jnp.tanh(0.7978845608028654 * (z + 0.044715 * z * z * z))
    return 0.5 * z * (1.0 + t), t


def _gelu_grad(z, t):
    return 0.5 * (1.0 + t) + 0.5 * z * (1.0 - t * t) * 0.7978845608028654 * (1.0 + 3.0 * 0.044715 * z * z)


def _softplus_neg(lam):
    x = -lam
    e = jnp.exp(-jnp.abs(x))
    u = 1.0 + e
    l1p = jnp.where(u == 1.0, e, jnp.log(u) * e / (u - 1.0))
    return jnp.maximum(x, 0.0) + l1p


def _one_minus_square(la, a):
    x = 2.0 * la
    series = -x * (1.0 + x * (0.5 + x * (1.0 / 6.0 + x * (1.0 / 24.0))))
    return jnp.where(x > -0.06, series, 1.0 - a * a)


def _sigmoid(x):
    return 0.5 * jnp.tanh(0.5 * x) + 0.5


def _down(v, d):
    return pltpu.roll(v, d, 0)


def _up(v, d):
    return pltpu.roll(v, v.shape[0] - d, 0)


def _slab_major(s, dtype=F32):
    return pltpu.HBM((N_SLAB, s, SLAB), dtype)


def _slabs_spec(tb):
    return pl.BlockSpec((N_SLAB, tb, SLAB), lambda i: (0, i, 0))


def _read_slabs(ref):
    return jnp.concatenate([ref[k] for k in range(N_SLAB)], axis=1)


def _write_slabs(ref, value):
    for k in range(N_SLAB):
        ref[k] = value[:, k * SLAB:(k + 1) * SLAB]


def _token_block(s, most=512):
    for rows in (most, 512, 256):
        if rows <= most and s % rows == 0 and s > rows:
            return rows
    return s


def _time_chunk(s):
    return 512 if s % 512 == 0 and s > 512 else 256 if s % 256 == 0 else s


def _fwd_in(x, ln1_g, w_in_t):
    s = x.shape[0]
    tb = _token_block(s, 1024)

    def body(x_ref, g_ref, w_ref, up_ref, ul_ref, ug_ref):
        n, _, _ = _rms(x_ref[...], g_ref[...])
        proj = _mm_nt(n.astype(BF16), w_ref[...])
        _write_slabs(up_ref, proj[:, :POOL_WIDTH])
        _write_slabs(ul_ref, proj[:, POOL_WIDTH:POOL_WIDTH + LRU_WIDTH])
        _write_slabs(ug_ref, proj[:, POOL_WIDTH + LRU_WIDTH:])

    out = _slab_major(s)
    return pl.pallas_call(
        body, name="fwd_in", grid=(s // tb,),
        in_specs=[pl.BlockSpec((tb, D_MODEL), lambda i: (i, 0)),
                  pl.BlockSpec((1, D_MODEL), lambda i: (0, 0)),
                  pl.BlockSpec((IN_WIDTH, D_MODEL), lambda i: (0, 0))],
        out_specs=[_slabs_spec(tb)] * 3,
        out_shape=[out, out, out],
        compiler_params=_params(40, 1),
    )(*_in_hbm(x, ln1_g, w_in_t))


def _pool_denominator(t0, row, window):
    return jnp.minimum((t0 + row + 1).astype(F32), window)


def _causal_window(ext, deeper):
    s = ext + _down(ext, 1)
    s = s + deeper[0] * _down(s, 2)
    s = s + deeper[1] * _down(s, 4)
    s = s + deeper[2] * _down(s, 8)
    return s[HALO:]


def _anticausal_window(ext, deeper, rows):
    s = ext + _up(ext, 1)
    s = s + deeper[0] * _up(s, 2)
    s = s + deeper[1] * _up(s, 4)
    s = s + deeper[2] * _up(s, 8)
    return s[:rows]


def _conv(taps, cw, cb):
    return cw[3] * taps[0] + cw[2] * taps[1] + cw[1] * taps[2] + cw[0] * taps[3] + cb


def _lru_gates(xc, wa, ba, wi, bi, sp):
    xb = xc.astype(BF16)
    r = _sigmoid(_mm(xb, wa) + ba)
    i = _sigmoid(_mm(xb, wi) + bi)
    la = (-LRU_C) * r * sp
    a = jnp.exp(la)
    mult = jnp.sqrt(jnp.maximum(_one_minus_square(la, a), 0.0))
    return xb, r, i, a, mult


SUBLANES = 8


def _scan_pads(a_pad, b_pad, rows, causal):
    spare = slice(0, SUBLANES) if causal else slice(rows, rows + SUBLANES)
    a_pad[spare, :] = jnp.ones((SUBLANES, SLAB), F32)
    b_pad[spare, :] = jnp.zeros((SUBLANES, SLAB), F32)


def _scan_causal(a, b, h_prev, a_pad, b_pad, rows):
    d = 1
    while d < min(SUBLANES, rows):
        a_pad[SUBLANES:, :] = a
        b_pad[SUBLANES:, :] = b
        b = a * b_pad[SUBLANES - d:SUBLANES - d + rows, :] + b
        a = a * a_pad[SUBLANES - d:SUBLANES - d + rows, :]
        d *= 2
    while d < rows:
        b = jnp.concatenate([b[:d], a[d:] * b[:-d] + b[d:]], axis=0)
        a = jnp.concatenate([a[:d], a[d:] * a[:-d]], axis=0)
        d *= 2
    return b + a * h_prev


def _scan_anticausal(a, b, l_next, a_pad, b_pad, rows):
    d = 1
    while d < min(SUBLANES, rows):
        a_pad[:rows, :] = a
        b_pad[:rows, :] = b
        b = a * b_pad[d:d + rows, :] + b
        a = a * a_pad[d:d + rows, :]
        d *= 2
    while d < rows:
        b = jnp.concatenate([a[:-d] * b[d:] + b[:-d], b[-d:]], axis=0)
        a = jnp.concatenate([a[:-d] * a[d:], a[-d:]], axis=0)
        d *= 2
    return b + a * l_next


def _slab_scalars():
    slab = pl.program_id(0)
    deeper = [jnp.where(slab > k, 1.0, 0.0).astype(F32) for k in range(N_SLAB - 1)]
    window = jnp.left_shift(jnp.int32(2), slab).astype(F32)
    inverse = jnp.where(slab == 0, 0.5, jnp.where(slab == 1, 0.25, jnp.where(slab == 2, 0.125, 0.0625))).astype(F32)
    return deeper, (window, inverse)


def _window_mean(total, t0, row, window, at_start):
    if at_start:
        return total / _pool_denominator(t0, row, window[0])
    return total * window[1]


def _slab_specs(s):
    seq = pl.BlockSpec((None, s, SLAB), lambda k: (k, 0, 0))
    mat = pl.BlockSpec((1, SLAB, SLAB), lambda k: (k, 0, 0))
    vec = pl.BlockSpec((1, SLAB), lambda k: (0, k))
    taps = pl.BlockSpec((CONV_WIDTH, SLAB), lambda k: (0, k))
    return seq, mat, vec, taps


def _mixer_fwd(u_pool, u_lru, u_gate, pool_w, pool_scale, conv_w, conv_b, wa_bd, b_a, wi_bd, b_i, lam):
    s = u_pool.shape[1]
    tc = _time_chunk(s)
    n_chunks = s // tc

    def body(up_ref, ul_ref, ug_ref, pw_ref, ps_ref, cw_ref, cb_ref, wa_ref, ba_ref, wi_ref, bi_ref, lam_ref,
             yp_ref, h_ref, yl_ref, shift_ref, a_pad, b_pad):
        _scan_pads(a_pad, b_pad, tc, causal=True)
        deeper, window = _slab_scalars()
        pw = pw_ref[0].astype(BF16)
        wa = wa_ref[0].astype(BF16)
        wi = wi_ref[0].astype(BF16)
        ps, cb, ba, bi = ps_ref[...], cb_ref[...], ba_ref[...], bi_ref[...]
        cw = [cw_ref[k:k + 1, :] for k in range(CONV_WIDTH)]
        sp = _softplus_neg(lam_ref[...])
        row = lax.broadcasted_iota(jnp.int32, (tc, SLAB), 0)

        def chunk(t0, ext_p, ext_l, h_prev, at_start=False):
            rows = pl.ds(t0, tc)
            d = _window_mean(_causal_window(ext_p, deeper), t0, row, window, at_start) - ext_p[HALO:]
            yp_ref[rows, :] = _mm(d.astype(BF16), pw) * ps
            shift_ref[...] = ext_l
            xc = _conv([shift_ref[HALO - k:HALO - k + tc, :] for k in range(CONV_WIDTH)], cw, cb)
            _, _, i, a, mult = _lru_gates(xc, wa, ba, wi, bi, sp)
            h = _scan_causal(a, mult * (i * xc), h_prev, a_pad, b_pad, tc)
            h_ref[rows, :] = h
            yl_ref[rows, :] = h * _gelu(ug_ref[rows, :])[0]
            return h[tc - 1:tc, :]

        pad = jnp.zeros((HALO, SLAB), F32)
        h0 = chunk(0, jnp.concatenate([pad, up_ref[pl.ds(0, tc), :]], axis=0),
                   jnp.concatenate([pad, ul_ref[pl.ds(0, tc), :]], axis=0), jnp.zeros((1, SLAB), F32), at_start=True)

        def step(c, h_prev):
            t0 = pl.multiple_of(c * tc, tc)
            ext = pl.ds(pl.multiple_of(c * tc - HALO, HALO), tc + HALO)
            return chunk(t0, up_ref[ext, :], ul_ref[ext, :], h_prev)

        lax.fori_loop(1, n_chunks, step, h0)

    seq, mat, vec, taps = _slab_specs(s)
    out = _slab_major(s)
    return pl.pallas_call(
        body, name="mixer_fwd", grid=(N_SLAB,),
        in_specs=[seq, seq, seq, mat, vec, taps, vec, mat, vec, mat, vec, vec],
        out_specs=[seq, seq, seq], out_shape=[out, out, out],
        scratch_shapes=[pltpu.VMEM((tc + HALO, SLAB), F32), pltpu.VMEM((tc + SUBLANES, SLAB), F32),
                        pltpu.VMEM((tc + SUBLANES, SLAB), F32)],
        compiler_params=_params(48, 1),
    )(*_in_hbm(u_pool, u_lru, u_gate, pool_w, pool_scale, conv_w, conv_b, wa_bd, b_a, wi_bd, b_i, lam))


def _fwd_out(x, y_pool, y_lru, gn_pool_g, gn_lru_g, w_out, ln2_g):
    s = x.shape[0]
    tb = _token_block(s, 1024)

    def body(x_ref, yp_ref, yl_ref, gp_ref, gl_ref, w_ref, g2_ref, h1_ref, n2_ref):
        mp, _, _ = _rms(_read_slabs(yp_ref), gp_ref[...])
        ml, _, _ = _rms(_read_slabs(yl_ref), gl_ref[...])
        h1 = x_ref[...] + _mm(mp.astype(BF16), w_ref[:POOL_WIDTH, :]) + _mm(ml.astype(BF16), w_ref[POOL_WIDTH:, :])
        h1_ref[...] = h1
        n2_ref[...] = _rms(h1, g2_ref[...])[0].astype(BF16)

    row = pl.BlockSpec((tb, D_MODEL), lambda i: (i, 0))
    half = _slabs_spec(tb)
    return pl.pallas_call(
        body, name="fwd_out", grid=(s // tb,),
        in_specs=[row, half, half, pl.BlockSpec((1, 512), lambda i: (0, 0)), pl.BlockSpec((1, 512), lambda i: (0, 0)),
                  pl.BlockSpec((D_MODEL, D_MODEL), lambda i: (0, 0)), pl.BlockSpec((1, D_MODEL), lambda i: (0, 0))],
        out_specs=[row, row],
        out_shape=[pltpu.HBM((s, D_MODEL), F32), pltpu.HBM((s, D_MODEL), BF16)],
        compiler_params=_params(40, 1),
    )(*_in_hbm(x, y_pool, y_lru, gn_pool_g, gn_lru_g, w_out, ln2_g))


def _ffn_fwd(h1, n2, target, lnf_g, w_gate, w_up, w_down):
    s = h1.shape[0]
    tb = 512
    sub = 256
    n_ff = D_FF // FF_CHUNK

    def body(h1_ref, n2_ref, t_ref, gf_ref, wg_hbm, wu_hbm, wd_hbm,
             g_ref, u_ref, dh_ref, dhb_ref, dgf_ref, sq_ref, wg, wu, wd, sem):
        def token_block(loads):
            n2v = n2_ref[...]
            acc = jnp.zeros((tb, D_MODEL), F32)
            for c in range(n_ff):
                cols = slice(c * FF_CHUNK, (c + 1) * FF_CHUNK)
                for cp in loads[c]:
                    cp.wait()
                g = _mm_nt(n2v, wg[cols, :])
                u = _mm_nt(n2v, wu[cols, :])
                g_ref[:, cols] = g.astype(BF16)
                u_ref[:, cols] = u.astype(BF16)
                act = g * jax.nn.sigmoid(g) * u
                acc = acc + _mm(act.astype(BF16), wd[cols, :])
            gf = gf_ref[...]
            for r in range(tb // sub):
                rows = slice(r * sub, (r + 1) * sub)
                y, xhat, rstd = _rms(h1_ref[rows, :] + acc[rows, :], gf)
                err = y - t_ref[rows, :]
                sq_ref[...] += jnp.sum(err * err, axis=0, keepdims=True)
                dh2, dgf = _rms_bwd(err * (1.0 / D_MODEL), xhat, rstd, gf)
                dgf_ref[...] += dgf
                dh_ref[rows, :] = dh2
                dhb_ref[rows, :] = dh2.astype(BF16)

        first = pl.program_id(0) == 0

        @pl.when(first)
        def _():
            loads = [[pltpu.make_async_copy(src.at[pl.ds(c * FF_CHUNK, FF_CHUNK)], dst.at[pl.ds(c * FF_CHUNK, FF_CHUNK)],
                                            sem.at[3 * c + k])
                      for k, (src, dst) in enumerate(((wg_hbm, wg), (wu_hbm, wu), (wd_hbm, wd)))]
                     for c in range(n_ff)]
            for chunk in loads:
                for cp in chunk:
                    cp.start()
            dgf_ref[...] = jnp.zeros_like(dgf_ref)
            sq_ref[...] = jnp.zeros_like(sq_ref)
            token_block(loads)

        @pl.when(jnp.logical_not(first))
        def _():
            token_block([()] * n_ff)

    row = pl.BlockSpec((tb, D_MODEL), lambda i: (i, 0))
    ff = pl.BlockSpec((tb, D_FF), lambda i: (i, 0))
    vec = pl.BlockSpec((1, D_MODEL), lambda i: (0, 0))
    anyspace = pl.BlockSpec(memory_space=pl.ANY)
    return pl.pallas_call(
        body, name="ffn_fwd", grid=(s // tb,),
        in_specs=[row, row, row, vec, anyspace, anyspace, anyspace],
        out_specs=[ff, ff, row, row, vec, vec],
        out_shape=[pltpu.HBM((s, D_FF), BF16), pltpu.HBM((s, D_FF), BF16),
                   pltpu.HBM((s, D_MODEL), F32), pltpu.HBM((s, D_MODEL), BF16),
                   jax.ShapeDtypeStruct((1, D_MODEL), F32), jax.ShapeDtypeStruct((1, D_MODEL), F32)],
        scratch_shapes=[pltpu.VMEM((D_FF, D_MODEL), BF16), pltpu.VMEM((D_FF, D_MODEL), BF16),
                        pltpu.VMEM((D_FF, D_MODEL), BF16), pltpu.SemaphoreType.DMA((3 * n_ff,))],
        compiler_params=_params(60, 1),
    )(*_in_hbm(h1, n2, target, lnf_g, w_gate, w_up, w_down))


def _ffn_bwd(n2, dh2b, g, u, w_gate_t, w_up_t, w_down):
    s = n2.shape[0]
    tb = min(1024, s)
    n_ff = D_FF // FF_CHUNK
    n_tb = s // tb

    def body(n2_ref, dh_ref, g_ref, u_ref, wg_ref, wu_ref, wd_ref, dwg_ref, dwu_ref, dwd_ref, dn2_ref,
             dn2_acc, acc_g, acc_u, acc_d):
        j = pl.program_id(0)
        t = pl.program_id(1)

        @pl.when(t == 0)
        def _():
            acc_g[...] = jnp.zeros_like(acc_g)
            acc_u[...] = jnp.zeros_like(acc_u)
            acc_d[...] = jnp.zeros_like(acc_d)

        n2v = n2_ref[...]
        dh = dh_ref[...]
        gv = g_ref[...].astype(F32)
        uv = u_ref[...].astype(F32)
        sg = jax.nn.sigmoid(gv)
        silu = gv * sg
        dact = _mm_nt(dh, wd_ref[...])
        dub = (dact * silu).astype(BF16)
        dgb = (dact * uv * (sg * (1.0 + gv * (1.0 - sg)))).astype(BF16)
        acc_d[...] += _mm_tn((silu * uv).astype(BF16), dh)
        acc_g[...] += _mm_tn(dgb, n2v)
        acc_u[...] += _mm_tn(dub, n2v)
        part = _mm(jnp.concatenate([dgb, dub], axis=1), jnp.concatenate([wg_ref[...], wu_ref[...]], axis=0))
        rows = pl.ds(pl.multiple_of(t * tb, tb), tb)

        @pl.when(t == n_tb - 1)
        def _():
            dwg_ref[...] = acc_g[...].astype(BF16)
            dwu_ref[...] = acc_u[...].astype(BF16)
            dwd_ref[...] = acc_d[...].astype(BF16)

        @pl.when(j == 0)
        def _():
            dn2_acc[rows, :] = part

        @pl.when(jnp.logical_and(j > 0, j < n_ff - 1))
        def _():
            dn2_acc[rows, :] += part

        @pl.when(j == n_ff - 1)
        def _():
            dn2_ref[...] = dn2_acc[rows, :] + part

    row = pl.BlockSpec((tb, D_MODEL), lambda j, t: (t, 0))
    act = pl.BlockSpec((tb, FF_CHUNK), lambda j, t: (t, j))
    w_row = pl.BlockSpec((FF_CHUNK, D_MODEL), lambda j, t: (j, 0))
    last = pl.BlockSpec((tb, D_MODEL), lambda j, t: (jnp.where(j == n_ff - 1, t, 0), 0))
    grad = pltpu.HBM((D_FF, D_MODEL), BF16)
    chunk_acc = pltpu.VMEM((FF_CHUNK, D_MODEL), F32)
    return pl.pallas_call(
        body, name="ffn_bwd", grid=(n_ff, n_tb),
        in_specs=[row, row, act, act, w_row, w_row, w_row],
        out_specs=[w_row, w_row, w_row, last],
        out_shape=[grad, grad, grad, pltpu.HBM((s, D_MODEL), F32)],
        scratch_shapes=[pltpu.VMEM((s, D_MODEL), F32), chunk_acc, chunk_acc, chunk_acc],
        compiler_params=_params(56, 2),
    )(*_in_hbm(n2, dh2b, g, u, w_gate_t, w_up_t, w_down))


def _bwd_out(dn2, dh2, h1, y_pool, y_lru, gn_pool_g, gn_lru_g, w_out, ln2_g, after):
    s = h1.shape[0]
    tb = _token_block(s)

    def body(dn2_ref, dh2_ref, h1_ref, yp_ref, yl_ref, gp_ref, gl_ref, w_ref, g2_ref, _after,
             dh1_ref, dyp_ref, dyl_ref, dwb_ref, dg2_ref, dgp_ref, dgl_ref, dw_ref):
        @pl.when(pl.program_id(0) == 0)
        def _():
            dw_ref[...] = jnp.zeros_like(dw_ref)
            dg2_ref[...] = jnp.zeros_like(dg2_ref)
            dgp_ref[...] = jnp.zeros_like(dgp_ref)
            dgl_ref[...] = jnp.zeros_like(dgl_ref)

        g2 = g2_ref[...]
        _, xhat2, rstd2 = _rms(h1_ref[...], g2)
        dres, dg2 = _rms_bwd(dn2_ref[...], xhat2, rstd2, g2)
        dg2_ref[...] += dg2
        dh1 = dh2_ref[...] + dres
        dh1_ref[...] = dh1
        dh1b = dh1.astype(BF16)
        gp, gl = gp_ref[...], gl_ref[...]
        mp, xhat_p, rstd_p = _rms(_read_slabs(yp_ref), gp)
        ml, xhat_l, rstd_l = _rms(_read_slabs(yl_ref), gl)
        dw_ref[:POOL_WIDTH, :] += _mm_tn(mp.astype(BF16), dh1b)
        dw_ref[POOL_WIDTH:, :] += _mm_tn(ml.astype(BF16), dh1b)
        dyp, dgp = _rms_bwd(_mm_nt(dh1b, w_ref[:POOL_WIDTH, :]), xhat_p, rstd_p, gp)
        dyl, dgl = _rms_bwd(_mm_nt(dh1b, w_ref[POOL_WIDTH:, :]), xhat_l, rstd_l, gl)
        _write_slabs(dyp_ref, dyp)
        _write_slabs(dyl_ref, dyl)
        dgp_ref[...] += dgp
        dgl_ref[...] += dgl

        @pl.when(pl.program_id(0) == s // tb - 1)
        def _():
            dwb_ref[...] = dw_ref[...].astype(BF16)

    row = pl.BlockSpec((tb, D_MODEL), lambda i: (i, 0))
    half = _slabs_spec(tb)
    vec = pl.BlockSpec((1, D_MODEL), lambda i: (0, 0))
    hvec = pl.BlockSpec((1, 512), lambda i: (0, 0))
    mat = pl.BlockSpec((D_MODEL, D_MODEL), lambda i: (0, 0))
    return pl.pallas_call(
        body, name="bwd_out", grid=(s // tb,),
        in_specs=[row, row, row, half, half, hvec, hvec, mat, vec, pl.BlockSpec(memory_space=pl.ANY)],
        out_specs=[row, half, half, mat, vec, hvec, hvec],
        out_shape=[pltpu.HBM((s, D_MODEL), F32), _slab_major(s),
                   _slab_major(s), pltpu.HBM((D_MODEL, D_MODEL), BF16),
                   jax.ShapeDtypeStruct((1, D_MODEL), F32), jax.ShapeDtypeStruct((1, 512), F32),
                   jax.ShapeDtypeStruct((1, 512), F32)],
        scratch_shapes=[pltpu.VMEM((D_MODEL, D_MODEL), F32)],
        compiler_params=_params(48, 1),
    )(*_in_hbm(dn2, dh2, h1, y_pool, y_lru, gn_pool_g, gn_lru_g, w_out, ln2_g), after)


def _mixer_bwd(u_pool, u_lru, u_gate, h, dy_pool, dy_lru,
               pool_w, pool_scale, conv_w, conv_b, wa_bd, b_a, wi_bd, b_i, lam, after):
    s = u_pool.shape[1]
    tc = _time_chunk(s)
    n_chunks = s // tc

    def body(up_ref, ul_ref, ug_ref, h_ref, dyp_ref, dyl_ref,
             pw_ref, ps_ref, cw_ref, cb_ref, wa_ref, ba_ref, wi_ref, bi_ref, lam_ref, _after,
             dup_ref, dul_ref, dug_ref, dpw_ref, dps_ref, dcw_ref, dcb_ref, dwa_ref, dba_ref, dwi_ref, dbi_ref, dlam_ref,
             a_pad, b_pad, u_shift, h_shift, a_shift, d_shift):
        _scan_pads(a_pad, b_pad, tc, causal=False)
        deeper, window = _slab_scalars()
        pw = pw_ref[0].astype(BF16)
        wa = wa_ref[0].astype(BF16)
        wi = wi_ref[0].astype(BF16)
        ps, cb, ba, bi = ps_ref[...], cb_ref[...], ba_ref[...], bi_ref[...]
        cw = [cw_ref[k:k + 1, :] for k in range(CONV_WIDTH)]
        lam_v = lam_ref[...]
        sp = _softplus_neg(lam_v)
        row = lax.broadcasted_iota(jnp.int32, (tc, SLAB), 0)
        for ref in (dpw_ref, dps_ref, dcw_ref, dcb_ref, dwa_ref, dba_ref, dwi_ref, dbi_ref, dlam_ref):
            ref[...] = jnp.zeros_like(ref)

        def chunk(t0, ext_p, ext_l, ext_h, carry, at_start=False):
            l_next, a_next, dxc_next, ddn_next = carry
            rows = pl.ds(t0, tc)
            u_shift[...] = ext_l
            taps = [u_shift[HALO - k:HALO - k + tc, :] for k in range(CONV_WIDTH)]
            xc = _conv(taps, cw, cb)
            xb, r, i, a, mult = _lru_gates(xc, wa, ba, wi, bi, sp)
            hv = ext_h[HALO:]
            h_shift[...] = ext_h
            h_before = h_shift[HALO - 1:HALO - 1 + tc, :]
            ug = ug_ref[rows, :]
            dyl = dyl_ref[rows, :]
            gel, th = _gelu(ug)
            dug_ref[rows, :] = (dyl * hv * _gelu_grad(ug, th)).astype(BF16)
            a_shift[:tc, :] = a
            a_shift[tc:, :] = jnp.broadcast_to(a_next, (SUBLANES, SLAB))
            a_after = a_shift[1:1 + tc, :]
            l = _scan_anticausal(a_after, dyl * gel, l_next, a_pad, b_pad, tc)
            dmult = l * (i * xc)
            di = l * mult * xc
            dxc = l * mult * i
            dla = (l * h_before) * a - jnp.where(mult > 0.0, dmult * (a * a) / mult, 0.0)
            dlam_ref[...] += jnp.sum(dla * r, axis=0, keepdims=True)
            dpa = (dla * ((-LRU_C) * sp)) * (r * (1.0 - r))
            dpi = di * (i * (1.0 - i))
            dpab = dpa.astype(BF16)
            dpib = dpi.astype(BF16)
            dwa_ref[0] += _mm_tn(xb, dpab)
            dwi_ref[0] += _mm_tn(xb, dpib)
            dba_ref[...] += jnp.sum(dpa, axis=0, keepdims=True)
            dbi_ref[...] += jnp.sum(dpi, axis=0, keepdims=True)
            dxc = dxc + _mm_nt(dpab, wa) + _mm_nt(dpib, wi)
            d_shift[:tc, :] = dxc
            d_shift[tc:, :] = dxc_next
            dul_ref[rows, :] = (cw[3] * dxc + cw[2] * d_shift[1:1 + tc, :]
                                + cw[1] * d_shift[2:2 + tc, :] + cw[0] * d_shift[3:3 + tc, :]).astype(BF16)
            for k in range(CONV_WIDTH):
                dcw_ref[k:k + 1, :] += jnp.sum(dxc * taps[CONV_WIDTH - 1 - k], axis=0, keepdims=True)
            dcb_ref[...] += jnp.sum(dxc, axis=0, keepdims=True)
            db = (_window_mean(_causal_window(ext_p, deeper), t0, row, window, at_start) - ext_p[HALO:]).astype(BF16)
            dyp = dyp_ref[rows, :]
            dps_ref[...] += jnp.sum(dyp * _mm(db, pw), axis=0, keepdims=True)
            dys = (dyp * ps).astype(BF16)
            dpw_ref[0] += _mm_tn(db, dys)
            dd = _mm_nt(dys, pw)
            ddn = _window_mean(dd, t0, row, window, at_start)
            ext_q = jnp.concatenate([ddn, ddn_next], axis=0)
            dup_ref[rows, :] = (_anticausal_window(ext_q, deeper, tc) - dd).astype(BF16)
            return l[0:1, :], a[0:1, :], dxc[0:8, :], ddn[0:HALO, :]

        def step(k, carry):
            c = n_chunks - 1 - k
            t0 = pl.multiple_of(c * tc, tc)
            ext = pl.ds(pl.multiple_of(c * tc - HALO, HALO), tc + HALO)
            return chunk(t0, up_ref[ext, :], ul_ref[ext, :], h_ref[ext, :], carry)

        carry = (jnp.zeros((1, SLAB), F32), jnp.zeros((1, SLAB), F32),
                 jnp.zeros((8, SLAB), F32), jnp.zeros((HALO, SLAB), F32))
        carry = lax.fori_loop(0, n_chunks - 1, step, carry)
        pad = jnp.zeros((HALO, SLAB), F32)
        first = pl.ds(0, tc)
        chunk(0, jnp.concatenate([pad, up_ref[first, :]], axis=0), jnp.concatenate([pad, ul_ref[first, :]], axis=0),
              jnp.concatenate([pad, h_ref[first, :]], axis=0), carry, at_start=True)
        dlam_ref[...] = dlam_ref[...] * (LRU_C * jax.nn.sigmoid(-lam_v))

    seq, mat, vec, taps = _slab_specs(s)
    grad = _slab_major(s, BF16)
    mats = jax.ShapeDtypeStruct((N_SLAB, SLAB, SLAB), F32)
    vecs = jax.ShapeDtypeStruct((1, 512), F32)
    return pl.pallas_call(
        body, name="mixer_bwd", grid=(N_SLAB,),
        in_specs=[seq] * 6 + [mat, vec, taps, vec, mat, vec, mat, vec, vec, pl.BlockSpec(memory_space=pl.ANY)],
        out_specs=[seq, seq, seq, mat, vec, taps, vec, mat, vec, mat, vec, vec],
        out_shape=[grad, grad, grad, mats, vecs, jax.ShapeDtypeStruct((CONV_WIDTH, 512), F32), vecs,
                   mats, vecs, mats, vecs, vecs],
        scratch_shapes=[pltpu.VMEM((tc + SUBLANES, SLAB), F32), pltpu.VMEM((tc + SUBLANES, SLAB), F32),
                        pltpu.VMEM((tc + HALO, SLAB), F32), pltpu.VMEM((tc + HALO, SLAB), F32),
                        pltpu.VMEM((tc + SUBLANES, SLAB), F32), pltpu.VMEM((tc + SUBLANES, SLAB), F32)],
        compiler_params=_params(56, 1),
    )(*_in_hbm(u_pool, u_lru, u_gate, h, dy_pool, dy_lru, pool_w, pool_scale, conv_w, conv_b, wa_bd, b_a, wi_bd,
               b_i, lam), after)


def _bwd_in(x, dh1, du_pool, du_lru, du_gate, ln1_g, w_in_t, after):
    s = x.shape[0]
    tb = _token_block(s)

    def body(x_ref, dh1_ref, dup_ref, dul_ref, dug_ref, g_ref, w_ref, _after, dx_ref, dwb_ref, dg_ref, dw_ref):
        @pl.when(pl.program_id(0) == 0)
        def _():
            dw_ref[...] = jnp.zeros_like(dw_ref)
            dg_ref[...] = jnp.zeros_like(dg_ref)

        g1 = g_ref[...]
        n, xhat, rstd = _rms(x_ref[...], g1)
        nb = n.astype(BF16)
        db = jnp.concatenate([_read_slabs(ref) for ref in (dup_ref, dul_ref, dug_ref)], axis=1)
        dw_ref[...] += _mm_tn(db, nb)
        dx, dg1 = _rms_bwd(_mm(db, w_ref[...]), xhat, rstd, g1)
        dx_ref[...] = dh1_ref[...] + dx
        dg_ref[...] += dg1

        @pl.when(pl.program_id(0) == s // tb - 1)
        def _():
            dwb_ref[...] = dw_ref[...].astype(BF16)

    row = pl.BlockSpec((tb, D_MODEL), lambda i: (i, 0))
    half = _slabs_spec(tb)
    vec = pl.BlockSpec((1, D_MODEL), lambda i: (0, 0))
    mat = pl.BlockSpec((IN_WIDTH, D_MODEL), lambda i: (0, 0))
    return pl.pallas_call(
        body, name="bwd_in", grid=(s // tb,),
        in_specs=[row, row, half, half, half, vec, mat, pl.BlockSpec(memory_space=pl.ANY)],
        out_specs=[row, mat, vec],
        out_shape=[pltpu.HBM((s, D_MODEL), F32), pltpu.HBM((IN_WIDTH, D_MODEL), BF16),
                   jax.ShapeDtypeStruct((1, D_MODEL), F32)],
        scratch_shapes=[pltpu.VMEM((IN_WIDTH, D_MODEL), F32)],
        compiler_params=_params(48, 1),
    )(*_in_hbm(x, dh1, du_pool, du_lru, du_gate, ln1_g, w_in_t), after)


def _mesh_position():
    x, y, c = lax.axis_index("x"), lax.axis_index("y"), lax.axis_index("c")
    return x, y, c, 4 * x + 2 * y + c


def _peer(x, y, c, p):
    px = 1 - x if p & 4 else x
    py = 1 - y if p & 2 else y
    pc = 1 - c if p & 1 else c
    return (px, py, pc), 4 * px + 2 * py + pc


HBM_SPEC = pl.BlockSpec(memory_space=pltpu.HBM)
SEM_SPEC = pl.BlockSpec(memory_space=pltpu.SEMAPHORE)
DATAFLOW = pltpu.SideEffectType.DATAFLOW_SIDE_EFFECTING


class Copy(NamedTuple):
    src: int
    src_at: Any
    dst: int
    dst_at: Any
    peer: int
    group: int
    slot: int


SIBLING = (1,)
SAME_CORE = (2, 4, 6)
EVERYONE = tuple(range(1, N_DEV))
MYSELF = (0,)


def _same(index):
    return index


def _chip(index):
    return jnp.right_shift(index, 1)


def _fan_out(srcs, lands, peers, group):
    return [Copy(s, None, d, _same, p, group, N_DEV * i + p) for i, (s, d) in enumerate(zip(srcs, lands)) for p in peers]


def _scatter(stacks, lands, peers):
    return [Copy(s, lambda me, p=p: jnp.bitwise_xor(me, p), d, _same, p, 0, 0)
            for s, d in zip(stacks, lands) for p in peers]


def _numbered(copies, group=0):
    return [cp._replace(group=group, slot=i) for i, cp in enumerate(copies)]


def _relay(lands, peers):
    return [Copy(b, lambda s, q=q: jnp.bitwise_xor(s, q), b, lambda s, q=q: jnp.bitwise_xor(s, q), 1, 0, N_DEV * i + q)
            for i, b in enumerate(lands) for q in peers]


def _to_sibling(stacks, lands):
    return [Copy(s, lambda me, k=k: 2 * k + 1 - jnp.bitwise_and(me, 1), d, lambda me, k=k: k, 1, 0, 4 * i + k)
            for i, (s, d) in enumerate(zip(stacks, lands)) for k in range(N_DEV // 2)]


def _to_chips(sums, lands):
    return [Copy(s, lambda me, p=p: jnp.bitwise_xor(_chip(me), p // 2), d, _chip, p, 0, 4 * i + p // 2)
            for i, (s, d) in enumerate(zip(sums, lands)) for p in SAME_CORE]


def _comm_call(name, bufs, wait=None, start=None, after=()):
    nb = len(bufs)
    slots = list(start[1]) if start else []
    n_out_sem = 2 * len(slots)
    after = [a for a in (after if isinstance(after, (list, tuple)) else [after]) if a is not None]

    def body(*refs):
        b = refs[:nb]
        at = nb
        if wait:
            w_send, w_recv = refs[at], refs[at + 1]
            at += 2
        at += len(after)
        out_sems = refs[at:at + n_out_sem]
        token = refs[at + n_out_sem + nb]
        x, y, c, me = _mesh_position()

        def part(i, row_of, sender):
            return b[i] if row_of is None else b[i].at[row_of(sender)]

        if wait:
            for cp in wait[1]:
                peer, peer_index = _peer(x, y, c, cp.peer)
                arrival = pltpu.make_async_remote_copy(part(cp.src, cp.src_at, me), part(cp.dst, cp.dst_at, peer_index),
                                                       w_send.at[cp.slot], w_recv.at[cp.slot],
                                                       device_id=peer, device_id_type=MESH)
                arrival.wait_send()
                arrival.wait_recv()
        if start:
            for cp in start[0]:
                peer, _ = _peer(x, y, c, cp.peer)
                pltpu.make_async_remote_copy(part(cp.src, cp.src_at, me), part(cp.dst, cp.dst_at, me),
                                             out_sems[2 * cp.group].at[cp.slot], out_sems[2 * cp.group + 1].at[cp.slot],
                                             device_id=peer, device_id_type=MESH).start()
        token[...] = jnp.zeros_like(token)

    sem_shapes = []
    for n_slots in slots:
        sem_shapes += [pltpu.SemaphoreType.DMA((n_slots,))] * 2
    operands = [pltpu.with_memory_space_constraint(a, pltpu.HBM) for a in bufs]
    in_specs = [HBM_SPEC] * nb
    if wait:
        operands += list(wait[0])
        in_specs += [SEM_SPEC, SEM_SPEC]
    operands += after
    in_specs += [pl.BlockSpec(memory_space=pl.ANY)] * len(after)
    outs = pl.pallas_call(
        body, name=name, in_specs=in_specs,
        out_specs=[SEM_SPEC] * n_out_sem + [HBM_SPEC] * nb + [pl.BlockSpec(memory_space=pltpu.VMEM)],
        out_shape=sem_shapes + [pltpu.HBM(a.shape, a.dtype) for a in bufs] + [jax.ShapeDtypeStruct((8, SLAB), F32)],
        input_output_aliases={i: n_out_sem + i for i in range(nb)},
        compiler_params=pltpu.CompilerParams(has_side_effects=DATAFLOW),
    )(*operands)
    sems = [(outs[2 * k], outs[2 * k + 1]) for k in range(len(slots))]
    return sems, list(outs[n_out_sem:n_out_sem + nb]), outs[-1]


def _pair_sum(stacks, lands, place):
    n = len(stacks)

    def body(place_ref, *refs):
        k = pl.program_id(0)
        for m in range(n):
            mine, theirs, out, land = refs[m], refs[n + m], refs[2 * n + m], refs[3 * n + m]
            total = (mine[0, 0].astype(F32) + theirs[0].astype(F32)).astype(out.dtype)
            out[0] = total

            @pl.when(k == place_ref[1])
            def _():
                land[0] = total

    in_specs = [pl.BlockSpec((1, 1) + a.shape[1:], lambda k, place_ref: (k, place_ref[0], 0, 0)) for a in stacks]
    in_specs += [pl.BlockSpec((1,) + a.shape[1:], lambda k, place_ref: (k, 0, 0)) for a in lands]
    out_specs = [pl.BlockSpec((1,) + a.shape[1:], lambda k, place_ref: (k, 0, 0)) for a in lands]
    out_specs += [pl.BlockSpec((1,) + a.shape[1:], lambda k, place_ref: (place_ref[1], 0, 0)) for a in lands]
    outs = pl.pallas_call(
        body, name="pair_sum_" + "_".join(str(a.shape[1]) for a in stacks),
        grid_spec=pltpu.PrefetchScalarGridSpec(num_scalar_prefetch=1, grid=(N_DEV // 2,), in_specs=in_specs,
                                               out_specs=out_specs),
        out_shape=[pltpu.HBM(a.shape, a.dtype) for a in lands] * 2,
        compiler_params=_params(40, 1),
    )(place, *_in_hbm(*[a.reshape((N_DEV // 2, 2) + a.shape[1:]) for a in stacks], *lands))
    return list(outs[:n]), list(outs[n:])


def _reduce_adam(parts, w, m, v, name):
    rows, cols = w.shape
    n_parts = parts.shape[0]
    rb = rows
    for cand in (256, 176, 128):
        if rows % cand == 0 and rows > cand:
            rb = cand
            break

    def body(p_ref, w_ref, m_ref, v_ref, g_out, d_out, m_out, v_out):
        g = p_ref[0].astype(F32)
        for j in range(1, n_parts):
            g = g + p_ref[j].astype(F32)
        g_out[...] = g
        d_out[...], m_out[...], v_out[...] = _adam(g, w_ref[...], m_ref[...], v_ref[...])

    blk = pl.BlockSpec((rb, cols), lambda i: (i, 0))
    out = jax.ShapeDtypeStruct((rows, cols), F32)
    return pl.pallas_call(
        body, name=name, grid=(rows // rb,),
        in_specs=[pl.BlockSpec((n_parts, rb, cols), lambda i: (0, i, 0)), blk, blk, blk],
        out_specs=[blk] * 4, out_shape=[out] * 4,
        compiler_params=_params(40, 1),
    )(*_in_hbm(parts, w, m, v))


def _cols_from_stack(stack):
    n, r, c = stack.shape
    return jnp.transpose(stack, (1, 0, 2)).reshape(r, n * c)


def _block_diag(w):
    z = jnp.zeros((N_SLAB, 64, 64), w.dtype)
    pairs = w.reshape(N_SLAB, 2, 64, 64)
    top = jnp.concatenate([pairs[:, 0], z], axis=2)
    bottom = jnp.concatenate([z, pairs[:, 1]], axis=2)
    return jnp.concatenate([top, bottom], axis=1)


def _adam(g, w, m, v):
    m_new = ADAM_B1 * m + (1.0 - ADAM_B1) * g
    v_new = ADAM_B2 * v + (1.0 - ADAM_B2) * (g * g)
    m_hat = m_new / (1.0 - ADAM_B1 ** ADAM_STEP)
    v_hat = v_new / (1.0 - ADAM_B2 ** ADAM_STEP)
    return (-ADAM_LR) * (m_hat / (jnp.sqrt(v_hat) + ADAM_EPS) + ADAM_WD * w), m_new, v_new


WIDE = ("ln2_g", "lnf_g")
HALF = ("pool_scale", "conv_b", "b_a", "b_i", "lam", "gn_pool_g", "gn_lru_g")
VECTORS = [(k, D_MODEL) for k in WIDE] + [(k, 512) for k in HALF]
VECTOR_ROWS = sum(width // SLAB for _, width in VECTORS)
LOSS_ROW = -(-VECTOR_ROWS // 8) * 8
CONV_AT = LOSS_ROW + 8
CONV_LANES = LRU_WIDTH // SLAB
PACK_F_ROWS = CONV_AT + CONV_WIDTH * CONV_LANES
MATRIX_ROWS = N_SLAB * SLAB
HEAD = SLAB // 2
GATE_ROWS = N_SLAB * HEAD
PACK_B_ROWS = MATRIX_ROWS + 2 * GATE_ROWS


def _pack_small(vectors, pool_g, wa_g, wi_g, conv, sq):
    n_vec = len(vectors)

    def body(*refs):
        vec = refs[:n_vec]
        pw_ref, wa_ref, wi_ref, cw_ref, sq_ref, out, out_b = refs[n_vec:]
        out[...] = jnp.zeros_like(out)
        row = 0
        for ref, (_, width) in zip(vec, VECTORS):
            for k in range(width // SLAB):
                out[row:row + 1, :] = ref[:, k * SLAB:(k + 1) * SLAB]
                row += 1
        for tap in range(CONV_WIDTH):
            for k in range(CONV_LANES):
                at = CONV_AT + tap * CONV_LANES + k
                out[at:at + 1, :] = cw_ref[tap:tap + 1, k * SLAB:(k + 1) * SLAB]
        total = sq_ref[:, 0:SLAB]
        for k in range(1, D_MODEL // SLAB):
            total = total + sq_ref[:, k * SLAB:(k + 1) * SLAB]
        out[LOSS_ROW:LOSS_ROW + 1, :] = total
        left = lax.broadcasted_iota(jnp.int32, (HEAD, SLAB), 1) < HEAD
        for s in range(N_SLAB):
            out_b[s * SLAB:(s + 1) * SLAB, :] = pw_ref[s].astype(BF16)
            for i, ref in enumerate((wa_ref, wi_ref)):
                at = MATRIX_ROWS + i * GATE_ROWS + s * HEAD
                out_b[at:at + HEAD, :] = jnp.where(left, ref[s, 0:HEAD, :], ref[s, HEAD:SLAB, :]).astype(BF16)

    return pl.pallas_call(
        body, name="pack_small",
        out_shape=[jax.ShapeDtypeStruct((PACK_F_ROWS, SLAB), F32), jax.ShapeDtypeStruct((PACK_B_ROWS, SLAB), BF16)],
    )(*vectors, pool_g, wa_g, wi_g, conv, sq)


def _small_reduce_adam(parts, parts_b, vec_w, vec_m, vec_v, pool_wmv):
    n_vec = len(VECTORS)
    n_parts = parts.shape[0]

    def body(*refs):
        p_ref, pb_ref = refs[0], refs[1]
        w_refs, m_refs, v_refs = (refs[2 + k * n_vec:2 + (k + 1) * n_vec] for k in range(3))
        pw_w, pw_m, pw_v = refs[2 + 3 * n_vec:5 + 3 * n_vec]
        outs = refs[5 + 3 * n_vec:-1]
        total = refs[-1]
        total[...] = p_ref[0]
        for j in range(1, n_parts):
            total[...] += p_ref[j]
        row = 0
        for i, (_, width) in enumerate(VECTORS):
            n_rows = width // SLAB
            g = jnp.concatenate([total[row + k:row + k + 1, :] for k in range(n_rows)], axis=1)
            row += n_rows
            d, m_new, v_new = _adam(g, w_refs[i][...], m_refs[i][...], v_refs[i][...])
            for ref, val in zip(outs[4 * i:4 * i + 4], (g, d, m_new, v_new)):
                ref[...] = val
        tail = outs[4 * n_vec:]

        def summed(first, count):
            g = pb_ref[0, first:first + count, :].astype(F32)
            for j in range(1, n_parts):
                g = g + pb_ref[j, first:first + count, :].astype(F32)
            return g

        g = summed(0, MATRIX_ROWS)
        d, m_new, v_new = _adam(g, pw_w[...], pw_m[...], pw_v[...])
        for ref, val in zip(tail[0:4], (g, d, m_new, v_new)):
            ref[...] = val
        tail[4][...] = summed(MATRIX_ROWS, GATE_ROWS)
        tail[5][...] = summed(MATRIX_ROWS + GATE_ROWS, GATE_ROWS)
        for tap in range(CONV_WIDTH):
            at = CONV_AT + tap * CONV_LANES
            tail[6][tap:tap + 1, :] = jnp.concatenate([total[at + k:at + k + 1, :] for k in range(CONV_LANES)], axis=1)
        tail[7][...] = (0.5 / D_MODEL) * jnp.sum(total[LOSS_ROW:LOSS_ROW + 1, :], axis=1, keepdims=True)

    out_shape = []
    for _, width in VECTORS:
        out_shape += [jax.ShapeDtypeStruct((1, width), F32)] * 4
    out_shape += [jax.ShapeDtypeStruct((MATRIX_ROWS, SLAB), F32)] * 4 + [jax.ShapeDtypeStruct((GATE_ROWS, SLAB), F32)] * 2
    out_shape += [jax.ShapeDtypeStruct((CONV_WIDTH, LRU_WIDTH), F32), jax.ShapeDtypeStruct((1, 1), F32)]
    outs = pl.pallas_call(
        body, name="adam_small", out_shape=out_shape,
        scratch_shapes=[pltpu.VMEM((PACK_F_ROWS, SLAB), F32)],
        compiler_params=_params(40),
    )(parts, parts_b, *vec_w, *vec_m, *vec_v, *pool_wmv)
    vec_out = [tuple(outs[4 * i:4 * i + 4]) for i in range(n_vec)]
    tail = outs[4 * n_vec:]
    return vec_out, tuple(tail[0:4]), tail[4], tail[5], tail[6], tail[7]


def _plain_adam(grads, ws, ms, vs):
    n = len(grads)

    def body(*refs):
        ins, outs = refs[:4 * n], refs[4 * n:]
        for i in range(n):
            d, m_new, v_new = _adam(ins[i][...], ins[n + i][...], ins[2 * n + i][...], ins[3 * n + i][...])
            for ref, val in zip(outs[3 * i:3 * i + 3], (d, m_new, v_new)):
                ref[...] = val

    out_shape = []
    for g in grads:
        out_shape += [jax.ShapeDtypeStruct(g.shape, F32)] * 3
    outs = pl.pallas_call(body, name="adam_plain", out_shape=out_shape)(*grads, *ws, *ms, *vs)
    return [tuple(outs[3 * i:3 * i + 3]) for i in range(n)]


def kernel(x, ln1_g, w_in, pool_w, pool_scale, conv_w, conv_b, w_a, b_a, w_i, b_i, lam, gn_pool_g, gn_lru_g, w_out, ln2_g, w_ffn_gate, w_ffn_up, w_ffn_down, lnf_g, loss_target, m_ln1_g, m_w_in, m_pool_w, m_pool_scale, m_conv_w, m_conv_b, m_w_a, m_b_a, m_w_i, m_b_i, m_lam, m_gn_pool_g, m_gn_lru_g, m_w_out, m_ln2_g, m_w_ffn_gate, m_w_ffn_up, m_w_ffn_down, m_lnf_g, v_ln1_g, v_w_in, v_pool_w, v_pool_scale, v_conv_w, v_conv_b, v_w_a, v_b_a, v_w_i, v_b_i, v_lam, v_gn_pool_g, v_gn_lru_g, v_w_out, v_ln2_g, v_w_ffn_gate, v_w_ffn_up, v_w_ffn_down, v_lnf_g):
    weights = dict(ln1_g=ln1_g, w_in=w_in, pool_w=pool_w, pool_scale=pool_scale, conv_w=conv_w, conv_b=conv_b,
                   w_a=w_a, b_a=b_a, w_i=w_i, b_i=b_i, lam=lam, gn_pool_g=gn_pool_g, gn_lru_g=gn_lru_g,
                   w_out=w_out, ln2_g=ln2_g, w_ffn_gate=w_ffn_gate, w_ffn_up=w_ffn_up, w_ffn_down=w_ffn_down,
                   lnf_g=lnf_g)
    mom1 = dict(ln1_g=m_ln1_g, w_in=m_w_in, pool_w=m_pool_w, pool_scale=m_pool_scale, conv_w=m_conv_w,
                conv_b=m_conv_b, w_a=m_w_a, b_a=m_b_a, w_i=m_w_i, b_i=m_b_i, lam=m_lam, gn_pool_g=m_gn_pool_g,
                gn_lru_g=m_gn_lru_g, w_out=m_w_out, ln2_g=m_ln2_g, w_ffn_gate=m_w_ffn_gate,
                w_ffn_up=m_w_ffn_up, w_ffn_down=m_w_ffn_down, lnf_g=m_lnf_g)
    mom2 = dict(ln1_g=v_ln1_g, w_in=v_w_in, pool_w=v_pool_w, pool_scale=v_pool_scale, conv_w=v_conv_w,
                conv_b=v_conv_b, w_a=v_w_a, b_a=v_b_a, w_i=v_w_i, b_i=v_b_i, lam=v_lam, gn_pool_g=v_gn_pool_g,
                gn_lru_g=v_gn_lru_g, w_out=v_w_out, ln2_g=v_ln2_g, w_ffn_gate=v_w_ffn_gate,
                w_ffn_up=v_w_ffn_up, w_ffn_down=v_w_ffn_down, lnf_g=v_lnf_g)

    xs = x[0]
    target = loss_target[0]

    shard = dict(w_in=lambda a: a[0].T, w_ffn_gate=lambda a: a[0].T, w_ffn_up=lambda a: a[0].T,
                 w_out=lambda a: a[0], w_ffn_down=lambda a: a[0], conv_w=lambda a: a[0])
    unshard = dict(w_in=lambda a: a.T[None], w_ffn_gate=lambda a: a.T[None], w_ffn_up=lambda a: a.T[None],
                   w_out=lambda a: a[None], w_ffn_down=lambda a: a[None], conv_w=lambda a: a[None])

    gathered = ("w_in", "conv_w", "w_out", "w_ffn_gate", "w_ffn_up", "w_ffn_down")
    groups = ((0, 1), (2,), (3, 4, 5))
    sources = [shard[k](weights[k]) if k == "conv_w" else shard[k](weights[k]).astype(BF16) for k in gathered]
    my_index = 4 * lax.axis_index("x") + 2 * lax.axis_index("y") + lax.axis_index("c")
    lands = [lax.empty((N_DEV,) + a.shape, a.dtype) for a in sources]

    def first_hop(n, group=0):
        return _numbered(_fan_out(range(n), range(n, 2 * n), SAME_CORE + SIBLING + MYSELF, 0), group)

    start = []
    for g, members in enumerate(groups):
        start += _numbered(_fan_out(members, [6 + m for m in members], SAME_CORE + SIBLING + MYSELF, 0), g)
    sems, bufs, _ = _comm_call("gather_start", sources + lands,
                               start=(start, [sum(cp.group == g for cp in start) for g in range(len(groups))]))
    sources, lands = bufs[:6], bufs[6:]

    def gathered_group(tag, g, after):
        members = groups[g]
        n = len(members)
        relay = _numbered(_relay(range(n, 2 * n), SAME_CORE))
        relay_sems, bufs, _ = _comm_call("gather_relay_" + tag, [sources[m] for m in members] + [lands[m] for m in members],
                                         wait=(sems[g], first_hop(n)), start=(relay, (len(relay),)), after=after)
        _, bufs, _ = _comm_call("gather_wait_" + tag, bufs[n:], wait=(relay_sems[0], _numbered(_relay(range(n), SAME_CORE))))
        return bufs

    g_in, g_conv = gathered_group("in", 0, None)
    w_in_f = g_in.reshape(IN_WIDTH, D_MODEL)
    conv_w_f = _cols_from_stack(g_conv)

    wa_bd = _block_diag(w_a[0])
    wi_bd = _block_diag(w_i[0])
    lnf_row = lnf_g.reshape(1, D_MODEL)

    u_pool, u_lru, u_gate = _fwd_in(xs, ln1_g, w_in_f)
    y_pool, h, y_lru = _mixer_fwd(u_pool, u_lru, u_gate, pool_w[0], pool_scale, conv_w_f, conv_b,
                                  wa_bd, b_a, wi_bd, b_i, lam)
    (g_out,) = gathered_group("out", 1, y_pool)
    w_out_f = g_out.reshape(D_MODEL, D_MODEL)
    h1, n2 = _fwd_out(xs, y_pool, y_lru, gn_pool_g, gn_lru_g, w_out_f, ln2_g)
    g_gate, g_up, g_down = gathered_group("ffn", 2, n2)
    w_gate_f = g_gate.reshape(D_FF, D_MODEL)
    w_up_f = g_up.reshape(D_FF, D_MODEL)
    w_down_f = g_down.reshape(D_FF, D_MODEL)
    g_act, u_act, dh2, dh2b, d_lnf, sq = _ffn_fwd(h1, n2, target, lnf_row, w_gate_f, w_up_f, w_down_f)

    def direct(tag, stacks, wholes, after):
        sources = list(stacks) + list(wholes)
        n, n_st = len(sources), len(stacks)
        lands = [lax.empty(a.shape if i < n_st else (N_DEV,) + a.shape, a.dtype) for i, a in enumerate(sources)]
        copies = _numbered(_scatter(range(n_st), range(n, n + n_st), EVERYONE + MYSELF)
                           + _fan_out(range(n_st, n), range(n + n_st, 2 * n), EVERYONE + MYSELF, 0))
        sem, bufs, token = _comm_call(tag + "_start", sources + lands, start=(copies, (len(copies),)), after=after)
        return (tag, sem[0], bufs, copies), token

    def direct_finish(started, after):
        tag, sem, bufs, copies = started
        _, bufs, _ = _comm_call(tag + "_wait", bufs, wait=(sem, copies), after=after)
        return bufs[len(bufs) // 2:]

    place = jnp.stack([lax.axis_index("c"), 2 * lax.axis_index("x") + lax.axis_index("y")]).astype(jnp.int32)

    d_gate, d_up, d_down, dn2 = _ffn_bwd(n2, dh2b, g_act, u_act, w_gate_f, w_up_f, w_down_f)
    down_started, token = direct("down", [d_down.reshape(N_DEV, D_FF // N_DEV, D_MODEL)], [], None)
    ffn_stacks = [d.reshape(N_DEV, D_FF // N_DEV, D_MODEL) for d in (d_gate, d_up)]
    pair_lands = [lax.empty((N_DEV // 2,) + a.shape[1:], a.dtype) for a in ffn_stacks]
    pair_copies = _numbered(_to_sibling(range(2), range(2, 4)))
    sem, bufs, token = _comm_call("ffn_pair_start", ffn_stacks + pair_lands,
                                  start=(pair_copies, (len(pair_copies),)), after=token)
    dh1, dy_pool, dy_lru, d_out, d_ln2, d_gnp, d_gnl = _bwd_out(dn2, dh2, h1, y_pool, y_lru, gn_pool_g, gn_lru_g,
                                                                 w_out_f, ln2_g, token)
    _, bufs, _ = _comm_call("ffn_pair_wait", bufs, wait=(sem[0], pair_copies), after=dh1)
    ffn_sums, ffn_lands = _pair_sum(bufs[:2], bufs[2:], place)
    ffn_copies = _numbered(_to_chips(range(2), range(2, 4)))
    ffn_sem, ffn_bufs, token = _comm_call("ffn_chip_start", ffn_sums + ffn_lands, start=(ffn_copies, (len(ffn_copies),)))
    (du_pool, du_lru, du_gate, d_pw, d_ps, d_cw, d_cb, d_wa, d_ba, d_wi, d_bi, d_lam) = _mixer_bwd(
        u_pool, u_lru, u_gate, h, dy_pool, dy_lru, pool_w[0], pool_scale, conv_w_f, conv_b,
        wa_bd, b_a, wi_bd, b_i, lam, token)

    vec_grads = dict(ln2_g=d_ln2, lnf_g=d_lnf, pool_scale=d_ps, conv_b=d_cb, b_a=d_ba, b_i=d_bi,
                     lam=d_lam, gn_pool_g=d_gnp, gn_lru_g=d_gnl)
    packed, packed_b = _pack_small([vec_grads[k] for k, _ in VECTORS], d_pw, d_wa, d_wi, d_cw, sq)
    small_started, token = direct("small", [d_out.reshape(N_DEV, D_MODEL // N_DEV, D_MODEL)], [packed, packed_b], None)
    grad_x, d_in, d_ln1 = _bwd_in(xs, dh1, du_pool, du_lru, du_gate, ln1_g, w_in_f, token)
    in_started, token = direct("in", [d_in.reshape(N_DEV, IN_WIDTH // N_DEV, D_MODEL)], [d_ln1], None)

    results = {}

    def reduce_adam(name, parts):
        outs = _reduce_adam(parts, shard[name](weights[name]), shard[name](mom1[name]), shard[name](mom2[name]),
                            "adam_" + name)
        results[name] = tuple(unshard[name](o) for o in outs)
        return outs[0]

    shard["ln1_g"] = unshard["ln1_g"] = lambda a: a
    _, ffn_bufs, _ = _comm_call("ffn_chip_wait", ffn_bufs, wait=(ffn_sem[0], ffn_copies), after=token)
    done = [reduce_adam(name, parts) for name, parts in zip(("w_ffn_gate", "w_ffn_up"), ffn_bufs[2:])]
    (r_down,) = direct_finish(down_started, done)
    done.append(reduce_adam("w_ffn_down", r_down))
    r_out, r_small, r_small_b = direct_finish(small_started, done)
    done.append(reduce_adam("w_out", r_out))

    def as_row(a, width):
        return a.reshape(1, width)

    def as_matrix(a):
        return a.reshape(MATRIX_ROWS, SLAB)

    def as_heads(a):
        return a.reshape(2 * GATE_ROWS, HEAD)

    def heads_apart(g):
        return jnp.transpose(g.reshape(N_SLAB, HEAD, 2, HEAD), (0, 2, 1, 3)).reshape(2 * GATE_ROWS, HEAD)

    vec_out, pool_out, g_wa, g_wi, g_conv, loss_11 = _small_reduce_adam(
        r_small, r_small_b, [as_row(weights[k], w) for k, w in VECTORS], [as_row(mom1[k], w) for k, w in VECTORS],
        [as_row(mom2[k], w) for k, w in VECTORS], [as_matrix(t["pool_w"]) for t in (weights, mom1, mom2)])
    for (k, _), outs in zip(VECTORS, vec_out):
        results[k] = tuple(o.reshape(weights[k].shape) for o in outs)
    results["pool_w"] = tuple(o.reshape(pool_w.shape) for o in pool_out)
    my_columns = conv_w.shape[-1]
    plain_names = ("w_a", "w_i", "conv_w")
    plain_grads = [heads_apart(g) for g in (g_wa, g_wi)]
    plain_grads.append(lax.dynamic_slice_in_dim(g_conv, my_index * my_columns, my_columns, axis=1))
    views = (as_heads, as_heads, lambda a: a[0])
    plain_out = _plain_adam(plain_grads, *[[view(t[k]) for k, view in zip(plain_names, views)]
                                           for t in (weights, mom1, mom2)])
    for k, g, outs in zip(plain_names, plain_grads, plain_out):
        results[k] = tuple(o.reshape(weights[k].shape) for o in (g,) + outs)
    loss = loss_11[0, 0]
    r_in, r_ln1 = direct_finish(in_started, done + [plain_out[0][0], loss_11])
    reduce_adam("w_in", r_in)
    reduce_adam("ln1_g", r_ln1)

    order = ["ln1_g", "w_in", "pool_w", "pool_scale", "conv_w", "conv_b", "w_a", "b_a", "w_i", "b_i", "lam",
             "gn_pool_g", "gn_lru_g", "w_out", "ln2_g", "w_ffn_gate", "w_ffn_up", "w_ffn_down", "lnf_g"]
    return (loss, grad_x[None],
            *[results[k][0] for k in order], *[results[k][1] for k in order],
            *[results[k][2] for k in order], *[results[k][3] for k in order])
```

```python
from typing import Any, NamedTuple

import jax
import jax.numpy as jnp
from jax import lax
from jax.experimental import pallas as pl
from jax.experimental.pallas import tpu as pltpu

F32 = jnp.float32
BF16 = jnp.bfloat16

N_DEV = 8
D_MODEL = 1024
POOL_WIDTH = 512
LRU_WIDTH = 512
IN_WIDTH = 1536
D_FF = 2816
N_SLAB = 4
SLAB = 128
CONV_WIDTH = 4
LRU_C = 8.0
EPS = 1e-6
HALO = 16
FF_CHUNK = 256

ADAM_LR = 0.001
ADAM_B1 = 0.9
ADAM_B2 = 0.999
ADAM_EPS = 1e-08
ADAM_WD = 0.01
ADAM_STEP = 10

MIB = 1 << 20
MESH = pl.DeviceIdType.MESH


def _params(vmem_mib, n_axes=0):
    sem = ("arbitrary",) * n_axes if n_axes else None
    return pltpu.CompilerParams(dimension_semantics=sem, vmem_limit_bytes=vmem_mib * MIB)


def _in_hbm(*arrays):
    return [pltpu.with_memory_space_constraint(a, pltpu.HBM) for a in arrays]


def _mm(a, b):
    return jnp.dot(a, b, preferred_element_type=F32)


def _mm_nt(a, b):
    return lax.dot_general(a, b, (((1,), (1,)), ((), ())), preferred_element_type=F32)


def _mm_tn(a, b):
    return lax.dot_general(a, b, (((0,), (0,)), ((), ())), preferred_element_type=F32)


def _rms(x, g):
    rstd = lax.rsqrt(jnp.mean(x * x, axis=-1, keepdims=True) + EPS)
    xhat = x * rstd
    return xhat * g, xhat, rstd


def _rms_bwd(dy, xhat, rstd, g):
    gy = dy * g
    dx = rstd * (gy - xhat * jnp.mean(gy * xhat, axis=-1, keepdims=True))
    return dx, jnp.sum(dy * xhat, axis=0, keepdims=True)


def _gelu(z):
    t = jnp.tanh(0.7978845608028654 * (z + 0.044715 * z * z * z))
    return 0.5 * z * (1.0 + t), t


def _gelu_grad(z, t):
    return 0.5 * (1.0 + t) + 0.5 * z * (1.0 - t * t) * 0.7978845608028654 * (1.0 + 3.0 * 0.044715 * z * z)


def _softplus_neg(lam):
    x = -lam
    e = jnp.exp(-jnp.abs(x))
    u = 1.0 + e
    l1p = jnp.where(u == 1.0, e, jnp.log(u) * e / (u - 1.0))
    return jnp.maximum(x, 0.0) + l1p


def _one_minus_square(la, a):
    x = 2.0 * la
    series = -x * (1.0 + x * (0.5 + x * (1.0 / 6.0 + x * (1.0 / 24.0))))
    return jnp.where(x > -0.06, series, 1.0 - a * a)


def _sigmoid(x):
    return 0.5 * jnp.tanh(0.5 * x) + 0.5


def _down(v, d):
    return pltpu.roll(v, d, 0)


def _up(v, d):
    return pltpu.roll(v, v.shape[0] - d, 0)


def _slab_major(s, dtype=F32):
    return pltpu.HBM((N_SLAB, s, SLAB), dtype)


def _slabs_spec(tb):
    return pl.BlockSpec((N_SLAB, tb, SLAB), lambda i: (0, i, 0))


def _read_slabs(ref):
    return jnp.concatenate([ref[k] for k in range(N_SLAB)], axis=1)


def _write_slabs(ref, value):
    for k in range(N_SLAB):
        ref[k] = value[:, k * SLAB:(k + 1) * SLAB]


def _token_block(s, most=512):
    for rows in (most, 512, 256):
        if rows <= most and s % rows == 0 and s > rows:
            return rows
    return s


def _time_chunk(s):
    return 512 if s % 512 == 0 and s > 512 else 256 if s % 256 == 0 else s


def _fwd_in(x, ln1_g, w_in_t):
    s = x.shape[0]
    tb = _token_block(s, 1024)

    def body(x_ref, g_ref, w_ref, up_ref, ul_ref, ug_ref):
        n, _, _ = _rms(x_ref[...], g_ref[...])
        proj = _mm_nt(n.astype(BF16), w_ref[...])
        _write_slabs(up_ref, proj[:, :POOL_WIDTH])
        _write_slabs(ul_ref, proj[:, POOL_WIDTH:POOL_WIDTH + LRU_WIDTH])
        _write_slabs(ug_ref, proj[:, POOL_WIDTH + LRU_WIDTH:])

    out = _slab_major(s)
    return pl.pallas_call(
        body, name="fwd_in", grid=(s // tb,),
        in_specs=[pl.BlockSpec((tb, D_MODEL), lambda i: (i, 0)),
                  pl.BlockSpec((1, D_MODEL), lambda i: (0, 0)),
                  pl.BlockSpec((IN_WIDTH, D_MODEL), lambda i: (0, 0))],
        out_specs=[_slabs_spec(tb)] * 3,
        out_shape=[out, out, out],
        compiler_params=_params(40, 1),
    )(*_in_hbm(x, ln1_g, w_in_t))


def _pool_denominator(t0, row, window):
    return jnp.minimum((t0 + row + 1).astype(F32), window)


def _causal_window(ext, deeper):
    s = ext + _down(ext, 1)
    s = s + deeper[0] * _down(s, 2)
    s = s + deeper[1] * _down(s, 4)
    s = s + deeper[2] * _down(s, 8)
    return s[HALO:]


def _anticausal_window(ext, deeper, rows):
    s = ext + _up(ext, 1)
    s = s + deeper[0] * _up(s, 2)
    s = s + deeper[1] * _up(s, 4)
    s = s + deeper[2] * _up(s, 8)
    return s[:rows]


def _conv(taps, cw, cb):
    return cw[3] * taps[0] + cw[2] * taps[1] + cw[1] * taps[2] + cw[0] * taps[3] + cb


def _lru_gates(xc, wa, ba, wi, bi, sp):
    xb = xc.astype(BF16)
    r = _sigmoid(_mm(xb, wa) + ba)
    i = _sigmoid(_mm(xb, wi) + bi)
    la = (-LRU_C) * r * sp
    a = jnp.exp(la)
    mult = jnp.sqrt(jnp.maximum(_one_minus_square(la, a), 0.0))
    return xb, r, i, a, mult


SUBLANES = 8


def _scan_pads(a_pad, b_pad, rows, causal):
    spare = slice(0, SUBLANES) if causal else slice(rows, rows + SUBLANES)
    a_pad[spare, :] = jnp.ones((SUBLANES, SLAB), F32)
    b_pad[spare, :] = jnp.zeros((SUBLANES, SLAB), F32)


def _scan_causal(a, b, h_prev, a_pad, b_pad, rows):
    d = 1
    while d < min(SUBLANES, rows):
        a_pad[SUBLANES:, :] = a
        b_pad[SUBLANES:, :] = b
        b = a * b_pad[SUBLANES - d:SUBLANES - d + rows, :] + b
        a = a * a_pad[SUBLANES - d:SUBLANES - d + rows, :]
        d *= 2
    while d < rows:
        b = jnp.concatenate([b[:d], a[d:] * b[:-d] + b[d:]], axis=0)
        a = jnp.concatenate([a[:d], a[d:] * a[:-d]], axis=0)
        d *= 2
    return b + a * h_prev


def _scan_anticausal(a, b, l_next, a_pad, b_pad, rows):
    d = 1
    while d < min(SUBLANES, rows):
        a_pad[:rows, :] = a
        b_pad[:rows, :] = b
        b = a * b_pad[d:d + rows, :] + b
        a = a * a_pad[d:d + rows, :]
        d *= 2
    while d < rows:
        b = jnp.concatenate([a[:-d] * b[d:] + b[:-d], b[-d:]], axis=0)
        a = jnp.concatenate([a[:-d] * a[d:], a[-d:]], axis=0)
        d *= 2
    return b + a * l_next


def _slab_scalars():
    slab = pl.program_id(0)
    deeper = [jnp.where(slab > k, 1.0, 0.0).astype(F32) for k in range(N_SLAB - 1)]
    window = jnp.left_shift(jnp.int32(2), slab).astype(F32)
    inverse = jnp.where(slab == 0, 0.5, jnp.where(slab == 1, 0.25, jnp.where(slab == 2, 0.125, 0.0625))).astype(F32)
    return deeper, (window, inverse)


def _window_mean(total, t0, row, window, at_start):
    if at_start:
        return total / _pool_denominator(t0, row, window[0])
    return total * window[1]


def _slab_specs(s):
    seq = pl.BlockSpec((None, s, SLAB), lambda k: (k, 0, 0))
    mat = pl.BlockSpec((1, SLAB, SLAB), lambda k: (k, 0, 0))
    vec = pl.BlockSpec((1, SLAB), lambda k: (0, k))
    taps = pl.BlockSpec((CONV_WIDTH, SLAB), lambda k: (0, k))
    return seq, mat, vec, taps


def _mixer_fwd(u_pool, u_lru, u_gate, pool_w, pool_scale, conv_w, conv_b, wa_bd, b_a, wi_bd, b_i, lam):
    s = u_pool.shape[1]
    tc = _time_chunk(s)
    n_chunks = s // tc

    def body(up_ref, ul_ref, ug_ref, pw_ref, ps_ref, cw_ref, cb_ref, wa_ref, ba_ref, wi_ref, bi_ref, lam_ref,
             yp_ref, h_ref, yl_ref, shift_ref, a_pad, b_pad):
        _scan_pads(a_pad, b_pad, tc, causal=True)
        deeper, window = _slab_scalars()
        pw = pw_ref[0].astype(BF16)
        wa = wa_ref[0].astype(BF16)
        wi = wi_ref[0].astype(BF16)
        ps, cb, ba, bi = ps_ref[...], cb_ref[...], ba_ref[...], bi_ref[...]
        cw = [cw_ref[k:k + 1, :] for k in range(CONV_WIDTH)]
        sp = _softplus_neg(lam_ref[...])
        row = lax.broadcasted_iota(jnp.int32, (tc, SLAB), 0)

        def chunk(t0, ext_p, ext_l, h_prev, at_start=False):
            rows = pl.ds(t0, tc)
            d = _window_mean(_causal_window(ext_p, deeper), t0, row, window, at_start) - ext_p[HALO:]
            yp_ref[rows, :] = _mm(d.astype(BF16), pw) * ps
            shift_ref[...] = ext_l
            xc = _conv([shift_ref[HALO - k:HALO - k + tc, :] for k in range(CONV_WIDTH)], cw, cb)
            _, _, i, a, mult = _lru_gates(xc, wa, ba, wi, bi, sp)
            h = _scan_causal(a, mult * (i * xc), h_prev, a_pad, b_pad, tc)
            h_ref[rows, :] = h
            yl_ref[rows, :] = h * _gelu(ug_ref[rows, :])[0]
            return h[tc - 1:tc, :]

        pad = jnp.zeros((HALO, SLAB), F32)
        h0 = chunk(0, jnp.concatenate([pad, up_ref[pl.ds(0, tc), :]], axis=0),
                   jnp.concatenate([pad, ul_ref[pl.ds(0, tc), :]], axis=0), jnp.zeros((1, SLAB), F32), at_start=True)

        def step(c, h_prev):
            t0 = pl.multiple_of(c * tc, tc)
            ext = pl.ds(pl.multiple_of(c * tc - HALO, HALO), tc + HALO)
            return chunk(t0, up_ref[ext, :], ul_ref[ext, :], h_prev)

        lax.fori_loop(1, n_chunks, step, h0)

    seq, mat, vec, taps = _slab_specs(s)
    out = _slab_major(s)
    return pl.pallas_call(
        body, name="mixer_fwd", grid=(N_SLAB,),
        in_specs=[seq, seq, seq, mat, vec, taps, vec, mat, vec, mat, vec, vec],
        out_specs=[seq, seq, seq], out_shape=[out, out, out],
        scratch_shapes=[pltpu.VMEM((tc + HALO, SLAB), F32), pltpu.VMEM((tc + SUBLANES, SLAB), F32),
                        pltpu.VMEM((tc + SUBLANES, SLAB), F32)],
        compiler_params=_params(48, 1),
    )(*_in_hbm(u_pool, u_lru, u_gate, pool_w, pool_scale, conv_w, conv_b, wa_bd, b_a, wi_bd, b_i, lam))


def _fwd_out(x, y_pool, y_lru, gn_pool_g, gn_lru_g, w_out, ln2_g):
    s = x.shape[0]
    tb = _token_block(s, 1024)

    def body(x_ref, yp_ref, yl_ref, gp_ref, gl_ref, w_ref, g2_ref, h1_ref, n2_ref):
        mp, _, _ = _rms(_read_slabs(yp_ref), gp_ref[...])
        ml, _, _ = _rms(_read_slabs(yl_ref), gl_ref[...])
        h1 = x_ref[...] + _mm(mp.astype(BF16), w_ref[:POOL_WIDTH, :]) + _mm(ml.astype(BF16), w_ref[POOL_WIDTH:, :])
        h1_ref[...] = h1
        n2_ref[...] = _rms(h1, g2_ref[...])[0].astype(BF16)

    row = pl.BlockSpec((tb, D_MODEL), lambda i: (i, 0))
    half = _slabs_spec(tb)
    return pl.pallas_call(
        body, name="fwd_out", grid=(s // tb,),
        in_specs=[row, half, half, pl.BlockSpec((1, 512), lambda i: (0, 0)), pl.BlockSpec((1, 512), lambda i: (0, 0)),
                  pl.BlockSpec((D_MODEL, D_MODEL), lambda i: (0, 0)), pl.BlockSpec((1, D_MODEL), lambda i: (0, 0))],
        out_specs=[row, row],
        out_shape=[pltpu.HBM((s, D_MODEL), F32), pltpu.HBM((s, D_MODEL), BF16)],
        compiler_params=_params(40, 1),
    )(*_in_hbm(x, y_pool, y_lru, gn_pool_g, gn_lru_g, w_out, ln2_g))


def _ffn_fwd(h1, n2, target, lnf_g, w_gate, w_up, w_down):
    s = h1.shape[0]
    tb = 512
    sub = 256
    n_ff = D_FF // FF_CHUNK

    def body(h1_ref, n2_ref, t_ref, gf_ref, wg_hbm, wu_hbm, wd_hbm,
             g_ref, u_ref, dh_ref, dhb_ref, dgf_ref, sq_ref, wg, wu, wd, sem):
        @pl.when(pl.program_id(0) == 0)
        def _():
            loads = [pltpu.make_async_copy(src, dst, sem.at[k])
                     for k, (src, dst) in enumerate(((wg_hbm, wg), (wu_hbm, wu), (wd_hbm, wd)))]
            for cp in loads:
                cp.start()
            for cp in loads:
                cp.wait()
            dgf_ref[...] = jnp.zeros_like(dgf_ref)
            sq_ref[...] = jnp.zeros_like(sq_ref)

        n2v = n2_ref[...]
        acc = jnp.zeros((tb, D_MODEL), F32)
        for c in range(n_ff):
            cols = slice(c * FF_CHUNK, (c + 1) * FF_CHUNK)
            g = _mm_nt(n2v, wg[cols, :])
            u = _mm_nt(n2v, wu[cols, :])
            g_ref[:, cols] = g.astype(BF16)
            u_ref[:, cols] = u.astype(BF16)
            act = g * jax.nn.sigmoid(g) * u
            acc = acc + _mm(act.astype(BF16), wd[cols, :])
        gf = gf_ref[...]
        for r in range(tb // sub):
            rows = slice(r * sub, (r + 1) * sub)
            y, xhat, rstd = _rms(h1_ref[rows, :] + acc[rows, :], gf)
            err = y - t_ref[rows, :]
            sq_ref[...] += jnp.sum(err * err, axis=0, keepdims=True)
            dh2, dgf = _rms_bwd(err * (1.0 / D_MODEL), xhat, rstd, gf)
            dgf_ref[...] += dgf
            dh_ref[rows, :] = dh2
            dhb_ref[rows, :] = dh2.astype(BF16)

    row = pl.BlockSpec((tb, D_MODEL), lambda i: (i, 0))
    ff = pl.BlockSpec((tb, D_FF), lambda i: (i, 0))
    vec = pl.BlockSpec((1, D_MODEL), lambda i: (0, 0))
    anyspace = pl.BlockSpec(memory_space=pl.ANY)
    return pl.pallas_call(
        body, name="ffn_fwd", grid=(s // tb,),
        in_specs=[row, row, row, vec, anyspace, anyspace, anyspace],
        out_specs=[ff, ff, row, row, vec, vec],
        out_shape=[pltpu.HBM((s, D_FF), BF16), pltpu.HBM((s, D_FF), BF16),
                   pltpu.HBM((s, D_MODEL), F32), pltpu.HBM((s, D_MODEL), BF16),
                   jax.ShapeDtypeStruct((1, D_MODEL), F32), jax.ShapeDtypeStruct((1, D_MODEL), F32)],
        scratch_shapes=[pltpu.VMEM((D_FF, D_MODEL), BF16), pltpu.VMEM((D_FF, D_MODEL), BF16),
                        pltpu.VMEM((D_FF, D_MODEL), BF16), pltpu.SemaphoreType.DMA((3,))],
        compiler_params=_params(60, 1),
    )(*_in_hbm(h1, n2, target, lnf_g, w_gate, w_up, w_down))


def _ffn_bwd(n2, dh2b, g, u, w_gate_t, w_up_t, w_down):
    s = n2.shape[0]
    tb = min(1024, s)
    n_ff = D_FF // FF_CHUNK
    n_tb = s // tb

    def body(n2_ref, dh_ref, g_ref, u_ref, wg_ref, wu_ref, wd_ref, dwg_ref, dwu_ref, dwd_ref, dn2_ref,
             dn2_acc, acc_g, acc_u, acc_d):
        j = pl.program_id(0)
        t = pl.program_id(1)

        @pl.when(t == 0)
        def _():
            acc_g[...] = jnp.zeros_like(acc_g)
            acc_u[...] = jnp.zeros_like(acc_u)
            acc_d[...] = jnp.zeros_like(acc_d)

        n2v = n2_ref[...]
        dh = dh_ref[...]
        gv = g_ref[...].astype(F32)
        uv = u_ref[...].astype(F32)
        sg = jax.nn.sigmoid(gv)
        silu = gv * sg
        dact = _mm_nt(dh, wd_ref[...])
        dub = (dact * silu).astype(BF16)
        dgb = (dact * uv * (sg * (1.0 + gv * (1.0 - sg)))).astype(BF16)
        acc_d[...] += _mm_tn((silu * uv).astype(BF16), dh)
        acc_g[...] += _mm_tn(dgb, n2v)
        acc_u[...] += _mm_tn(dub, n2v)
        part = _mm(jnp.concatenate([dgb, dub], axis=1), jnp.concatenate([wg_ref[...], wu_ref[...]], axis=0))
        rows = pl.ds(pl.multiple_of(t * tb, tb), tb)

        @pl.when(t == n_tb - 1)
        def _():
            dwg_ref[...] = acc_g[...].astype(BF16)
            dwu_ref[...] = acc_u[...].astype(BF16)
            dwd_ref[...] = acc_d[...].astype(BF16)

        @pl.when(j == 0)
        def _():
            dn2_acc[rows, :] = part

        @pl.when(jnp.logical_and(j > 0, j < n_ff - 1))
        def _():
            dn2_acc[rows, :] += part

        @pl.when(j == n_ff - 1)
        def _():
            dn2_ref[...] = dn2_acc[rows, :] + part

    row = pl.BlockSpec((tb, D_MODEL), lambda j, t: (t, 0))
    act = pl.BlockSpec((tb, FF_CHUNK), lambda j, t: (t, j))
    w_row = pl.BlockSpec((FF_CHUNK, D_MODEL), lambda j, t: (j, 0))
    last = pl.BlockSpec((tb, D_MODEL), lambda j, t: (jnp.where(j == n_ff - 1, t, 0), 0))
    grad = pltpu.HBM((D_FF, D_MODEL), BF16)
    chunk_acc = pltpu.VMEM((FF_CHUNK, D_MODEL), F32)
    return pl.pallas_call(
        body, name="ffn_bwd", grid=(n_ff, n_tb),
        in_specs=[row, row, act, act, w_row, w_row, w_row],
        out_specs=[w_row, w_row, w_row, last],
        out_shape=[grad, grad, grad, pltpu.HBM((s, D_MODEL), F32)],
        scratch_shapes=[pltpu.VMEM((s, D_MODEL), F32), chunk_acc, chunk_acc, chunk_acc],
        compiler_params=_params(56, 2),
    )(*_in_hbm(n2, dh2b, g, u, w_gate_t, w_up_t, w_down))


def _bwd_out(dn2, dh2, h1, y_pool, y_lru, gn_pool_g, gn_lru_g, w_out, ln2_g, after):
    s = h1.shape[0]
    tb = _token_block(s)

    def body(dn2_ref, dh2_ref, h1_ref, yp_ref, yl_ref, gp_ref, gl_ref, w_ref, g2_ref, _after,
             dh1_ref, dyp_ref, dyl_ref, dwb_ref, dg2_ref, dgp_ref, dgl_ref, dw_ref):
        @pl.when(pl.program_id(0) == 0)
        def _():
            dw_ref[...] = jnp.zeros_like(dw_ref)
            dg2_ref[...] = jnp.zeros_like(dg2_ref)
            dgp_ref[...] = jnp.zeros_like(dgp_ref)
            dgl_ref[...] = jnp.zeros_like(dgl_ref)

        g2 = g2_ref[...]
        _, xhat2, rstd2 = _rms(h1_ref[...], g2)
        dres, dg2 = _rms_bwd(dn2_ref[...], xhat2, rstd2, g2)
        dg2_ref[...] += dg2
        dh1 = dh2_ref[...] + dres
        dh1_ref[...] = dh1
        dh1b = dh1.astype(BF16)
        gp, gl = gp_ref[...], gl_ref[...]
        mp, xhat_p, rstd_p = _rms(_read_slabs(yp_ref), gp)
        ml, xhat_l, rstd_l = _rms(_read_slabs(yl_ref), gl)
        dw_ref[:POOL_WIDTH, :] += _mm_tn(mp.astype(BF16), dh1b)
        dw_ref[POOL_WIDTH:, :] += _mm_tn(ml.astype(BF16), dh1b)
        dyp, dgp = _rms_bwd(_mm_nt(dh1b, w_ref[:POOL_WIDTH, :]), xhat_p, rstd_p, gp)
        dyl, dgl = _rms_bwd(_mm_nt(dh1b, w_ref[POOL_WIDTH:, :]), xhat_l, rstd_l, gl)
        _write_slabs(dyp_ref, dyp)
        _write_slabs(dyl_ref, dyl)
        dgp_ref[...] += dgp
        dgl_ref[...] += dgl

        @pl.when(pl.program_id(0) == s // tb - 1)
        def _():
            dwb_ref[...] = dw_ref[...].astype(BF16)

    row = pl.BlockSpec((tb, D_MODEL), lambda i: (i, 0))
    half = _slabs_spec(tb)
    vec = pl.BlockSpec((1, D_MODEL), lambda i: (0, 0))
    hvec = pl.BlockSpec((1, 512), lambda i: (0, 0))
    mat = pl.BlockSpec((D_MODEL, D_MODEL), lambda i: (0, 0))
    return pl.pallas_call(
        body, name="bwd_out", grid=(s // tb,),
        in_specs=[row, row, row, half, half, hvec, hvec, mat, vec, pl.BlockSpec(memory_space=pl.ANY)],
        out_specs=[row, half, half, mat, vec, hvec, hvec],
        out_shape=[pltpu.HBM((s, D_MODEL), F32), _slab_major(s),
                   _slab_major(s), pltpu.HBM((D_MODEL, D_MODEL), BF16),
                   jax.ShapeDtypeStruct((1, D_MODEL), F32), jax.ShapeDtypeStruct((1, 512), F32),
                   jax.ShapeDtypeStruct((1, 512), F32)],
        scratch_shapes=[pltpu.VMEM((D_MODEL, D_MODEL), F32)],
        compiler_params=_params(48, 1),
    )(*_in_hbm(dn2, dh2, h1, y_pool, y_lru, gn_pool_g, gn_lru_g, w_out, ln2_g), after)


def _mixer_bwd(u_pool, u_lru, u_gate, h, dy_pool, dy_lru,
               pool_w, pool_scale, conv_w, conv_b, wa_bd, b_a, wi_bd, b_i, lam, after):
    s = u_pool.shape[1]
    tc = _time_chunk(s)
    n_chunks = s // tc

    def body(up_ref, ul_ref, ug_ref, h_ref, dyp_ref, dyl_ref,
             pw_ref, ps_ref, cw_ref, cb_ref, wa_ref, ba_ref, wi_ref, bi_ref, lam_ref, _after,
             dup_ref, dul_ref, dug_ref, dpw_ref, dps_ref, dcw_ref, dcb_ref, dwa_ref, dba_ref, dwi_ref, dbi_ref, dlam_ref,
             a_pad, b_pad, u_shift, h_shift, a_shift, d_shift):
        _scan_pads(a_pad, b_pad, tc, causal=False)
        deeper, window = _slab_scalars()
        pw = pw_ref[0].astype(BF16)
        wa = wa_ref[0].astype(BF16)
        wi = wi_ref[0].astype(BF16)
        ps, cb, ba, bi = ps_ref[...], cb_ref[...], ba_ref[...], bi_ref[...]
        cw = [cw_ref[k:k + 1, :] for k in range(CONV_WIDTH)]
        lam_v = lam_ref[...]
        sp = _softplus_neg(lam_v)
        row = lax.broadcasted_iota(jnp.int32, (tc, SLAB), 0)
        for ref in (dpw_ref, dps_ref, dcw_ref, dcb_ref, dwa_ref, dba_ref, dwi_ref, dbi_ref, dlam_ref):
            ref[...] = jnp.zeros_like(ref)

        def chunk(t0, ext_p, ext_l, ext_h, carry, at_start=False):
            l_next, a_next, dxc_next, ddn_next = carry
            rows = pl.ds(t0, tc)
            u_shift[...] = ext_l
            taps = [u_shift[HALO - k:HALO - k + tc, :] for k in range(CONV_WIDTH)]
            xc = _conv(taps, cw, cb)
            xb, r, i, a, mult = _lru_gates(xc, wa, ba, wi, bi, sp)
            hv = ext_h[HALO:]
            h_shift[...] = ext_h
            h_before = h_shift[HALO - 1:HALO - 1 + tc, :]
            ug = ug_ref[rows, :]
            dyl = dyl_ref[rows, :]
            gel, th = _gelu(ug)
            dug_ref[rows, :] = (dyl * hv * _gelu_grad(ug, th)).astype(BF16)
            a_shift[:tc, :] = a
            a_shift[tc:, :] = jnp.broadcast_to(a_next, (SUBLANES, SLAB))
            a_after = a_shift[1:1 + tc, :]
            l = _scan_anticausal(a_after, dyl * gel, l_next, a_pad, b_pad, tc)
            dmult = l * (i * xc)
            di = l * mult * xc
            dxc = l * mult * i
            dla = (l * h_before) * a - jnp.where(mult > 0.0, dmult * (a * a) / mult, 0.0)
            dlam_ref[...] += jnp.sum(dla * r, axis=0, keepdims=True)
            dpa = (dla * ((-LRU_C) * sp)) * (r * (1.0 - r))
            dpi = di * (i * (1.0 - i))
            dpab = dpa.astype(BF16)
            dpib = dpi.astype(BF16)
            dwa_ref[0] += _mm_tn(xb, dpab)
            dwi_ref[0] += _mm_tn(xb, dpib)
            dba_ref[...] += jnp.sum(dpa, axis=0, keepdims=True)
            dbi_ref[...] += jnp.sum(dpi, axis=0, keepdims=True)
            dxc = dxc + _mm_nt(dpab, wa) + _mm_nt(dpib, wi)
            d_shift[:tc, :] = dxc
            d_shift[tc:, :] = dxc_next
            dul_ref[rows, :] = (cw[3] * dxc + cw[2] * d_shift[1:1 + tc, :]
                                + cw[1] * d_shift[2:2 + tc, :] + cw[0] * d_shift[3:3 + tc, :]).astype(BF16)
            for k in range(CONV_WIDTH):
                dcw_ref[k:k + 1, :] += jnp.sum(dxc * taps[CONV_WIDTH - 1 - k], axis=0, keepdims=True)
            dcb_ref[...] += jnp.sum(dxc, axis=0, keepdims=True)
            db = (_window_mean(_causal_window(ext_p, deeper), t0, row, window, at_start) - ext_p[HALO:]).astype(BF16)
            dyp = dyp_ref[rows, :]
            dps_ref[...] += jnp.sum(dyp * _mm(db, pw), axis=0, keepdims=True)
            dys = (dyp * ps).astype(BF16)
            dpw_ref[0] += _mm_tn(db, dys)
            dd = _mm_nt(dys, pw)
            ddn = _window_mean(dd, t0, row, window, at_start)
            ext_q = jnp.concatenate([ddn, ddn_next], axis=0)
            dup_ref[rows, :] = (_anticausal_window(ext_q, deeper, tc) - dd).astype(BF16)
            return l[0:1, :], a[0:1, :], dxc[0:8, :], ddn[0:HALO, :]

        def step(k, carry):
            c = n_chunks - 1 - k
            t0 = pl.multiple_of(c * tc, tc)
            ext = pl.ds(pl.multiple_of(c * tc - HALO, HALO), tc + HALO)
            return chunk(t0, up_ref[ext, :], ul_ref[ext, :], h_ref[ext, :], carry)

        carry = (jnp.zeros((1, SLAB), F32), jnp.zeros((1, SLAB), F32),
                 jnp.zeros((8, SLAB), F32), jnp.zeros((HALO, SLAB), F32))
        carry = lax.fori_loop(0, n_chunks - 1, step, carry)
        pad = jnp.zeros((HALO, SLAB), F32)
        first = pl.ds(0, tc)
        chunk(0, jnp.concatenate([pad, up_ref[first, :]], axis=0), jnp.concatenate([pad, ul_ref[first, :]], axis=0),
              jnp.concatenate([pad, h_ref[first, :]], axis=0), carry, at_start=True)
        dlam_ref[...] = dlam_ref[...] * (LRU_C * jax.nn.sigmoid(-lam_v))

    seq, mat, vec, taps = _slab_specs(s)
    grad = _slab_major(s, BF16)
    mats = jax.ShapeDtypeStruct((N_SLAB, SLAB, SLAB), F32)
    vecs = jax.ShapeDtypeStruct((1, 512), F32)
    return pl.pallas_call(
        body, name="mixer_bwd", grid=(N_SLAB,),
        in_specs=[seq] * 6 + [mat, vec, taps, vec, mat, vec, mat, vec, vec, pl.BlockSpec(memory_space=pl.ANY)],
        out_specs=[seq, seq, seq, mat, vec, taps, vec, mat, vec, mat, vec, vec],
        out_shape=[grad, grad, grad, mats, vecs, jax.ShapeDtypeStruct((CONV_WIDTH, 512), F32), vecs,
                   mats, vecs, mats, vecs, vecs],
        scratch_shapes=[pltpu.VMEM((tc + SUBLANES, SLAB), F32), pltpu.VMEM((tc + SUBLANES, SLAB), F32),
                        pltpu.VMEM((tc + HALO, SLAB), F32), pltpu.VMEM((tc + HALO, SLAB), F32),
                        pltpu.VMEM((tc + SUBLANES, SLAB), F32), pltpu.VMEM((tc + SUBLANES, SLAB), F32)],
        compiler_params=_params(56, 1),
    )(*_in_hbm(u_pool, u_lru, u_gate, h, dy_pool, dy_lru, pool_w, pool_scale, conv_w, conv_b, wa_bd, b_a, wi_bd,
               b_i, lam), after)


def _bwd_in(x, dh1, du_pool, du_lru, du_gate, ln1_g, w_in_t, after):
    s = x.shape[0]
    tb = _token_block(s)

    def body(x_ref, dh1_ref, dup_ref, dul_ref, dug_ref, g_ref, w_ref, _after, dx_ref, dwb_ref, dg_ref, dw_ref):
        @pl.when(pl.program_id(0) == 0)
        def _():
            dw_ref[...] = jnp.zeros_like(dw_ref)
            dg_ref[...] = jnp.zeros_like(dg_ref)

        g1 = g_ref[...]
        n, xhat, rstd = _rms(x_ref[...], g1)
        nb = n.astype(BF16)
        db = jnp.concatenate([_read_slabs(ref) for ref in (dup_ref, dul_ref, dug_ref)], axis=1)
        dw_ref[...] += _mm_tn(db, nb)
        dx, dg1 = _rms_bwd(_mm(db, w_ref[...]), xhat, rstd, g1)
        dx_ref[...] = dh1_ref[...] + dx
        dg_ref[...] += dg1

        @pl.when(pl.program_id(0) == s // tb - 1)
        def _():
            dwb_ref[...] = dw_ref[...].astype(BF16)

    row = pl.BlockSpec((tb, D_MODEL), lambda i: (i, 0))
    half = _slabs_spec(tb)
    vec = pl.BlockSpec((1, D_MODEL), lambda i: (0, 0))
    mat = pl.BlockSpec((IN_WIDTH, D_MODEL), lambda i: (0, 0))
    return pl.pallas_call(
        body, name="bwd_in", grid=(s // tb,),
        in_specs=[row, row, half, half, half, vec, mat, pl.BlockSpec(memory_space=pl.ANY)],
        out_specs=[row, mat, vec],
        out_shape=[pltpu.HBM((s, D_MODEL), F32), pltpu.HBM((IN_WIDTH, D_MODEL), BF16),
                   jax.ShapeDtypeStruct((1, D_MODEL), F32)],
        scratch_shapes=[pltpu.VMEM((IN_WIDTH, D_MODEL), F32)],
        compiler_params=_params(48, 1),
    )(*_in_hbm(x, dh1, du_pool, du_lru, du_gate, ln1_g, w_in_t), after)


def _mesh_position():
    x, y, c = lax.axis_index("x"), lax.axis_index("y"), lax.axis_index("c")
    return x, y, c, 4 * x + 2 * y + c


def _peer(x, y, c, p):
    px = 1 - x if p & 4 else x
    py = 1 - y if p & 2 else y
    pc = 1 - c if p & 1 else c
    return (px, py, pc), 4 * px + 2 * py + pc


HBM_SPEC = pl.BlockSpec(memory_space=pltpu.HBM)
SEM_SPEC = pl.BlockSpec(memory_space=pltpu.SEMAPHORE)
DATAFLOW = pltpu.SideEffectType.DATAFLOW_SIDE_EFFECTING


class Copy(NamedTuple):
    src: int
    src_at: Any
    dst: int
    dst_at: Any
    peer: int
    group: int
    slot: int


SIBLING = (1,)
SAME_CORE = (2, 4, 6)
EVERYONE = tuple(range(1, N_DEV))
MYSELF = (0,)


def _same(index):
    return index


def _chip(index):
    return jnp.right_shift(index, 1)


def _fan_out(srcs, lands, peers, group):
    return [Copy(s, None, d, _same, p, group, N_DEV * i + p) for i, (s, d) in enumerate(zip(srcs, lands)) for p in peers]


def _scatter(stacks, lands, peers):
    return [Copy(s, lambda me, p=p: jnp.bitwise_xor(me, p), d, _same, p, 0, 0)
            for s, d in zip(stacks, lands) for p in peers]


def _numbered(copies, group=0):
    return [cp._replace(group=group, slot=i) for i, cp in enumerate(copies)]


def _relay(lands, peers):
    return [Copy(b, lambda s, q=q: jnp.bitwise_xor(s, q), b, lambda s, q=q: jnp.bitwise_xor(s, q), 1, 0, N_DEV * i + q)
            for i, b in enumerate(lands) for q in peers]


def _to_sibling(stacks, lands):
    return [Copy(s, lambda me, k=k: 2 * k + 1 - jnp.bitwise_and(me, 1), d, lambda me, k=k: k, 1, 0, 4 * i + k)
            for i, (s, d) in enumerate(zip(stacks, lands)) for k in range(N_DEV // 2)]


def _to_chips(sums, lands):
    return [Copy(s, lambda me, p=p: jnp.bitwise_xor(_chip(me), p // 2), d, _chip, p, 0, 4 * i + p // 2)
            for i, (s, d) in enumerate(zip(sums, lands)) for p in SAME_CORE]


def _comm_call(name, bufs, wait=None, start=None, after=()):
    nb = len(bufs)
    waits = [wait] if isinstance(wait, tuple) else list(wait or [])
    slots = list(start[1]) if start else []
    n_out_sem = 2 * len(slots)
    after = [a for a in (after if isinstance(after, (list, tuple)) else [after]) if a is not None]

    def body(*refs):
        b = refs[:nb]
        at = nb + 2 * len(waits) + len(after)
        out_sems = refs[at:at + n_out_sem]
        token = refs[at + n_out_sem + nb]
        x, y, c, me = _mesh_position()

        def part(i, row_of, sender):
            return b[i] if row_of is None else b[i].at[row_of(sender)]

        for k, (_, copies) in enumerate(waits):
            w_send, w_recv = refs[nb + 2 * k], refs[nb + 2 * k + 1]
            for cp in copies:
                peer, peer_index = _peer(x, y, c, cp.peer)
                arrival = pltpu.make_async_remote_copy(part(cp.src, cp.src_at, me), part(cp.dst, cp.dst_at, peer_index),
                                                       w_send.at[cp.slot], w_recv.at[cp.slot],
                                                       device_id=peer, device_id_type=MESH)
                arrival.wait_send()
                arrival.wait_recv()
        if start:
            for cp in start[0]:
                peer, _ = _peer(x, y, c, cp.peer)
                pltpu.make_async_remote_copy(part(cp.src, cp.src_at, me), part(cp.dst, cp.dst_at, me),
                                             out_sems[2 * cp.group].at[cp.slot], out_sems[2 * cp.group + 1].at[cp.slot],
                                             device_id=peer, device_id_type=MESH).start()
        token[...] = jnp.zeros_like(token)

    sem_shapes = []
    for n_slots in slots:
        sem_shapes += [pltpu.SemaphoreType.DMA((n_slots,))] * 2
    operands = [pltpu.with_memory_space_constraint(a, pltpu.HBM) for a in bufs]
    in_specs = [HBM_SPEC] * nb
    for sems, _ in waits:
        operands += list(sems)
        in_specs += [SEM_SPEC, SEM_SPEC]
    operands += after
    in_specs += [pl.BlockSpec(memory_space=pl.ANY)] * len(after)
    outs = pl.pallas_call(
        body, name=name, in_specs=in_specs,
        out_specs=[SEM_SPEC] * n_out_sem + [HBM_SPEC] * nb + [pl.BlockSpec(memory_space=pltpu.VMEM)],
        out_shape=sem_shapes + [pltpu.HBM(a.shape, a.dtype) for a in bufs] + [jax.ShapeDtypeStruct((8, SLAB), F32)],
        input_output_aliases={i: n_out_sem + i for i in range(nb)},
        compiler_params=pltpu.CompilerParams(has_side_effects=DATAFLOW),
    )(*operands)
    sems = [(outs[2 * k], outs[2 * k + 1]) for k in range(len(slots))]
    return sems, list(outs[n_out_sem:n_out_sem + nb]), outs[-1]


def _pair_sum(stacks, lands, place):
    n = len(stacks)

    def body(place_ref, *refs):
        k = pl.program_id(0)
        for m in range(n):
            mine, theirs, out, land = refs[m], refs[n + m], refs[2 * n + m], refs[3 * n + m]
            total = (mine[0, 0].astype(F32) + theirs[0].astype(F32)).astype(out.dtype)
            out[0] = total

            @pl.when(k == place_ref[1])
            def _():
                land[0] = total

    in_specs = [pl.BlockSpec((1, 1) + a.shape[1:], lambda k, place_ref: (k, place_ref[0], 0, 0)) for a in stacks]
    in_specs += [pl.BlockSpec((1,) + a.shape[1:], lambda k, place_ref: (k, 0, 0)) for a in lands]
    out_specs = [pl.BlockSpec((1,) + a.shape[1:], lambda k, place_ref: (k, 0, 0)) for a in lands]
    out_specs += [pl.BlockSpec((1,) + a.shape[1:], lambda k, place_ref: (place_ref[1], 0, 0)) for a in lands]
    outs = pl.pallas_call(
        body, name="pair_sum_" + "_".join(str(a.shape[1]) for a in stacks),
        grid_spec=pltpu.PrefetchScalarGridSpec(num_scalar_prefetch=1, grid=(N_DEV // 2,), in_specs=in_specs,
                                               out_specs=out_specs),
        out_shape=[pltpu.HBM(a.shape, a.dtype) for a in lands] * 2,
        compiler_params=_params(40, 1),
    )(place, *_in_hbm(*[a.reshape((N_DEV // 2, 2) + a.shape[1:]) for a in stacks], *lands))
    return list(outs[:n]), list(outs[n:])


def _reduce_adam(parts, w, m, v, name):
    rows, cols = w.shape
    n_parts = parts.shape[0]
    rb = rows
    for cand in (256, 176, 128):
        if rows % cand == 0 and rows > cand:
            rb = cand
            break

    def body(p_ref, w_ref, m_ref, v_ref, g_out, d_out, m_out, v_out):
        g = p_ref[0].astype(F32)
        for j in range(1, n_parts):
            g = g + p_ref[j].astype(F32)
        g_out[...] = g
        d_out[...], m_out[...], v_out[...] = _adam(g, w_ref[...], m_ref[...], v_ref[...])

    blk = pl.BlockSpec((rb, cols), lambda i: (i, 0))
    out = jax.ShapeDtypeStruct((rows, cols), F32)
    return pl.pallas_call(
        body, name=name, grid=(rows // rb,),
        in_specs=[pl.BlockSpec((n_parts, rb, cols), lambda i: (0, i, 0)), blk, blk, blk],
        out_specs=[blk] * 4, out_shape=[out] * 4,
        compiler_params=_params(40, 1),
    )(*_in_hbm(parts, w, m, v))


def _cols_from_stack(stack):
    n, r, c = stack.shape
    return jnp.transpose(stack, (1, 0, 2)).reshape(r, n * c)


def _block_diag(w):
    z = jnp.zeros((N_SLAB, 64, 64), w.dtype)
    pairs = w.reshape(N_SLAB, 2, 64, 64)
    top = jnp.concatenate([pairs[:, 0], z], axis=2)
    bottom = jnp.concatenate([z, pairs[:, 1]], axis=2)
    return jnp.concatenate([top, bottom], axis=1)


def _adam(g, w, m, v):
    m_new = ADAM_B1 * m + (1.0 - ADAM_B1) * g
    v_new = ADAM_B2 * v + (1.0 - ADAM_B2) * (g * g)
    m_hat = m_new / (1.0 - ADAM_B1 ** ADAM_STEP)
    v_hat = v_new / (1.0 - ADAM_B2 ** ADAM_STEP)
    return (-ADAM_LR) * (m_hat / (jnp.sqrt(v_hat) + ADAM_EPS) + ADAM_WD * w), m_new, v_new


WIDE = ("ln2_g", "lnf_g")
HALF = ("pool_scale", "conv_b", "b_a", "b_i", "lam", "gn_pool_g", "gn_lru_g")
VECTORS = [(k, D_MODEL) for k in WIDE] + [(k, 512) for k in HALF]
VECTOR_ROWS = sum(width // SLAB for _, width in VECTORS)
LOSS_ROW = -(-VECTOR_ROWS // 8) * 8
CONV_AT = LOSS_ROW + 8
CONV_LANES = LRU_WIDTH // SLAB
PACK_F_ROWS = CONV_AT + CONV_WIDTH * CONV_LANES
MATRIX_ROWS = N_SLAB * SLAB
HEAD = SLAB // 2
GATE_ROWS = N_SLAB * HEAD
PACK_B_ROWS = MATRIX_ROWS + 2 * GATE_ROWS


def _pack_small(vectors, pool_g, wa_g, wi_g, conv, sq):
    n_vec = len(vectors)

    def body(*refs):
        vec = refs[:n_vec]
        pw_ref, wa_ref, wi_ref, cw_ref, sq_ref, out, out_b = refs[n_vec:]
        out[...] = jnp.zeros_like(out)
        row = 0
        for ref, (_, width) in zip(vec, VECTORS):
            for k in range(width // SLAB):
                out[row:row + 1, :] = ref[:, k * SLAB:(k + 1) * SLAB]
                row += 1
        for tap in range(CONV_WIDTH):
            for k in range(CONV_LANES):
                at = CONV_AT + tap * CONV_LANES + k
                out[at:at + 1, :] = cw_ref[tap:tap + 1, k * SLAB:(k + 1) * SLAB]
        total = sq_ref[:, 0:SLAB]
        for k in range(1, D_MODEL // SLAB):
            total = total + sq_ref[:, k * SLAB:(k + 1) * SLAB]
        out[LOSS_ROW:LOSS_ROW + 1, :] = total
        left = lax.broadcasted_iota(jnp.int32, (HEAD, SLAB), 1) < HEAD
        for s in range(N_SLAB):
            out_b[s * SLAB:(s + 1) * SLAB, :] = pw_ref[s].astype(BF16)
            for i, ref in enumerate((wa_ref, wi_ref)):
                at = MATRIX_ROWS + i * GATE_ROWS + s * HEAD
                out_b[at:at + HEAD, :] = jnp.where(left, ref[s, 0:HEAD, :], ref[s, HEAD:SLAB, :]).astype(BF16)

    return pl.pallas_call(
        body, name="pack_small",
        out_shape=[jax.ShapeDtypeStruct((PACK_F_ROWS, SLAB), F32), jax.ShapeDtypeStruct((PACK_B_ROWS, SLAB), BF16)],
    )(*vectors, pool_g, wa_g, wi_g, conv, sq)


def _small_reduce_adam(parts, parts_b, vec_w, vec_m, vec_v, pool_wmv):
    n_vec = len(VECTORS)
    n_parts = parts.shape[0]

    def body(*refs):
        p_ref, pb_ref = refs[0], refs[1]
        w_refs, m_refs, v_refs = (refs[2 + k * n_vec:2 + (k + 1) * n_vec] for k in range(3))
        pw_w, pw_m, pw_v = refs[2 + 3 * n_vec:5 + 3 * n_vec]
        outs = refs[5 + 3 * n_vec:-1]
        total = refs[-1]
        total[...] = p_ref[0]
        for j in range(1, n_parts):
            total[...] += p_ref[j]
        row = 0
        for i, (_, width) in enumerate(VECTORS):
            n_rows = width // SLAB
            g = jnp.concatenate([total[row + k:row + k + 1, :] for k in range(n_rows)], axis=1)
            row += n_rows
            d, m_new, v_new = _adam(g, w_refs[i][...], m_refs[i][...], v_refs[i][...])
            for ref, val in zip(outs[4 * i:4 * i + 4], (g, d, m_new, v_new)):
                ref[...] = val
        tail = outs[4 * n_vec:]

        def summed(first, count):
            g = pb_ref[0, first:first + count, :].astype(F32)
            for j in range(1, n_parts):
                g = g + pb_ref[j, first:first + count, :].astype(F32)
            return g

        g = summed(0, MATRIX_ROWS)
        d, m_new, v_new = _adam(g, pw_w[...], pw_m[...], pw_v[...])
        for ref, val in zip(tail[0:4], (g, d, m_new, v_new)):
            ref[...] = val
        tail[4][...] = summed(MATRIX_ROWS, GATE_ROWS)
        tail[5][...] = summed(MATRIX_ROWS + GATE_ROWS, GATE_ROWS)
        for tap in range(CONV_WIDTH):
            at = CONV_AT + tap * CONV_LANES
            tail[6][tap:tap + 1, :] = jnp.concatenate([total[at + k:at + k + 1, :] for k in range(CONV_LANES)], axis=1)
        tail[7][...] = (0.5 / D_MODEL) * jnp.sum(total[LOSS_ROW:LOSS_ROW + 1, :], axis=1, keepdims=True)

    out_shape = []
    for _, width in VECTORS:
        out_shape += [jax.ShapeDtypeStruct((1, width), F32)] * 4
    out_shape += [jax.ShapeDtypeStruct((MATRIX_ROWS, SLAB), F32)] * 4 + [jax.ShapeDtypeStruct((GATE_ROWS, SLAB), F32)] * 2
    out_shape += [jax.ShapeDtypeStruct((CONV_WIDTH, LRU_WIDTH), F32), jax.ShapeDtypeStruct((1, 1), F32)]
    outs = pl.pallas_call(
        body, name="adam_small", out_shape=out_shape,
        scratch_shapes=[pltpu.VMEM((PACK_F_ROWS, SLAB), F32)],
        compiler_params=_params(40),
    )(parts, parts_b, *vec_w, *vec_m, *vec_v, *pool_wmv)
    vec_out = [tuple(outs[4 * i:4 * i + 4]) for i in range(n_vec)]
    tail = outs[4 * n_vec:]
    return vec_out, tuple(tail[0:4]), tail[4], tail[5], tail[6], tail[7]


def _plain_adam(grads, ws, ms, vs):
    n = len(grads)

    def body(*refs):
        ins, outs = refs[:4 * n], refs[4 * n:]
        for i in range(n):
            d, m_new, v_new = _adam(ins[i][...], ins[n + i][...], ins[2 * n + i][...], ins[3 * n + i][...])
            for ref, val in zip(outs[3 * i:3 * i + 3], (d, m_new, v_new)):
                ref[...] = val

    out_shape = []
    for g in grads:
        out_shape += [jax.ShapeDtypeStruct(g.shape, F32)] * 3
    outs = pl.pallas_call(body, name="adam_plain", out_shape=out_shape)(*grads, *ws, *ms, *vs)
    return [tuple(outs[3 * i:3 * i + 3]) for i in range(n)]


def kernel(x, ln1_g, w_in, pool_w, pool_scale, conv_w, conv_b, w_a, b_a, w_i, b_i, lam, gn_pool_g, gn_lru_g, w_out, ln2_g, w_ffn_gate, w_ffn_up, w_ffn_down, lnf_g, loss_target, m_ln1_g, m_w_in, m_pool_w, m_pool_scale, m_conv_w, m_conv_b, m_w_a, m_b_a, m_w_i, m_b_i, m_lam, m_gn_pool_g, m_gn_lru_g, m_w_out, m_ln2_g, m_w_ffn_gate, m_w_ffn_up, m_w_ffn_down, m_lnf_g, v_ln1_g, v_w_in, v_pool_w, v_pool_scale, v_conv_w, v_conv_b, v_w_a, v_b_a, v_w_i, v_b_i, v_lam, v_gn_pool_g, v_gn_lru_g, v_w_out, v_ln2_g, v_w_ffn_gate, v_w_ffn_up, v_w_ffn_down, v_lnf_g):
    weights = dict(ln1_g=ln1_g, w_in=w_in, pool_w=pool_w, pool_scale=pool_scale, conv_w=conv_w, conv_b=conv_b,
                   w_a=w_a, b_a=b_a, w_i=w_i, b_i=b_i, lam=lam, gn_pool_g=gn_pool_g, gn_lru_g=gn_lru_g,
                   w_out=w_out, ln2_g=ln2_g, w_ffn_gate=w_ffn_gate, w_ffn_up=w_ffn_up, w_ffn_down=w_ffn_down,
                   lnf_g=lnf_g)
    mom1 = dict(ln1_g=m_ln1_g, w_in=m_w_in, pool_w=m_pool_w, pool_scale=m_pool_scale, conv_w=m_conv_w,
                conv_b=m_conv_b, w_a=m_w_a, b_a=m_b_a, w_i=m_w_i, b_i=m_b_i, lam=m_lam, gn_pool_g=m_gn_pool_g,
                gn_lru_g=m_gn_lru_g, w_out=m_w_out, ln2_g=m_ln2_g, w_ffn_gate=m_w_ffn_gate,
                w_ffn_up=m_w_ffn_up, w_ffn_down=m_w_ffn_down, lnf_g=m_lnf_g)
    mom2 = dict(ln1_g=v_ln1_g, w_in=v_w_in, pool_w=v_pool_w, pool_scale=v_pool_scale, conv_w=v_conv_w,
                conv_b=v_conv_b, w_a=v_w_a, b_a=v_b_a, w_i=v_w_i, b_i=v_b_i, lam=v_lam, gn_pool_g=v_gn_pool_g,
                gn_lru_g=v_gn_lru_g, w_out=v_w_out, ln2_g=v_ln2_g, w_ffn_gate=v_w_ffn_gate,
                w_ffn_up=v_w_ffn_up, w_ffn_down=v_w_ffn_down, lnf_g=v_lnf_g)

    xs = x[0]
    target = loss_target[0]

    shard = dict(w_in=lambda a: a[0].T, w_ffn_gate=lambda a: a[0].T, w_ffn_up=lambda a: a[0].T,
                 w_out=lambda a: a[0], w_ffn_down=lambda a: a[0], conv_w=lambda a: a[0])
    unshard = dict(w_in=lambda a: a.T[None], w_ffn_gate=lambda a: a.T[None], w_ffn_up=lambda a: a.T[None],
                   w_out=lambda a: a[None], w_ffn_down=lambda a: a[None], conv_w=lambda a: a[None])

    gathered = ("w_in", "conv_w", "w_out", "w_ffn_gate", "w_ffn_up", "w_ffn_down")
    groups = ((0, 1), (2,), (3, 4), (5,))
    sources = [shard[k](weights[k]) if k == "conv_w" else shard[k](weights[k]).astype(BF16) for k in gathered]
    my_index = 4 * lax.axis_index("x") + 2 * lax.axis_index("y") + lax.axis_index("c")
    lands = [lax.empty((N_DEV,) + a.shape, a.dtype) for a in sources]

    def first_hop(n, at=0):
        return _numbered(_fan_out(range(at, at + n), range(at + n, at + 2 * n), SAME_CORE + SIBLING + MYSELF, 0))

    def second_hop(n, at=0):
        return _numbered(_relay(range(at, at + n), SAME_CORE))

    start = []
    for g, members in enumerate(groups):
        start += _numbered(_fan_out(members, [6 + m for m in members], SAME_CORE + SIBLING + MYSELF, 0), g)
    sems, bufs, _ = _comm_call("gather_start", sources + lands,
                               start=(start, [sum(cp.group == g for cp in start) for g in range(len(groups))]))
    sources, lands = bufs[:6], bufs[6:]

    def group_buffers(g):
        return [sources[m] for m in groups[g]] + [lands[m] for m in groups[g]]

    def relayed(tag, g, after):
        n = len(groups[g])
        relay = second_hop(n, n)
        relay_sems, bufs, _ = _comm_call("gather_relay_" + tag, group_buffers(g), wait=(sems[g], first_hop(n)),
                                         start=(relay, (len(relay),)), after=after)
        return relay_sems[0], bufs[n:]

    relay_sem, bufs = relayed("in", 0, None)
    g_in, g_conv = _comm_call("gather_wait_in", bufs, wait=(relay_sem, second_hop(2)))[1]
    w_in_f = g_in.reshape(IN_WIDTH, D_MODEL)
    conv_w_f = _cols_from_stack(g_conv)

    wa_bd = _block_diag(w_a[0])
    wi_bd = _block_diag(w_i[0])
    lnf_row = lnf_g.reshape(1, D_MODEL)

    u_pool, u_lru, u_gate = _fwd_in(xs, ln1_g, w_in_f)
    y_pool, h, y_lru = _mixer_fwd(u_pool, u_lru, u_gate, pool_w[0], pool_scale, conv_w_f, conv_b,
                                  wa_bd, b_a, wi_bd, b_i, lam)
    relay_sem, bufs = relayed("out", 1, y_pool)
    gate_up = second_hop(2, 3)
    (gate_up_sem,), bufs, _ = _comm_call("gather_wait_out", bufs + group_buffers(2),
                                         wait=[(relay_sem, second_hop(1)), (sems[2], first_hop(2, 1))],
                                         start=(gate_up, (len(gate_up),)))
    g_out, gate_up_lands = bufs[0], bufs[3:]
    w_out_f = g_out.reshape(D_MODEL, D_MODEL)
    h1, n2 = _fwd_out(xs, y_pool, y_lru, gn_pool_g, gn_lru_g, w_out_f, ln2_g)
    relay_sem, bufs = relayed("down", 3, n2)
    g_gate, g_up, g_down = _comm_call("gather_wait_ffn", gate_up_lands + bufs,
                                      wait=[(gate_up_sem, second_hop(2)), (relay_sem, second_hop(1, 2))])[1]
    w_gate_f = g_gate.reshape(D_FF, D_MODEL)
    w_up_f = g_up.reshape(D_FF, D_MODEL)
    w_down_f = g_down.reshape(D_FF, D_MODEL)
    g_act, u_act, dh2, dh2b, d_lnf, sq = _ffn_fwd(h1, n2, target, lnf_row, w_gate_f, w_up_f, w_down_f)

    def direct(tag, stacks, wholes, after):
        sources = list(stacks) + list(wholes)
        n, n_st = len(sources), len(stacks)
        lands = [lax.empty(a.shape if i < n_st else (N_DEV,) + a.shape, a.dtype) for i, a in enumerate(sources)]
        copies = _numbered(_scatter(range(n_st), range(n, n + n_st), EVERYONE + MYSELF)
                           + _fan_out(range(n_st, n), range(n + n_st, 2 * n), EVERYONE + MYSELF, 0))
        sem, bufs, token = _comm_call(tag + "_start", sources + lands, start=(copies, (len(copies),)), after=after)
        return (tag, sem[0], bufs, copies), token

    def direct_finish(started, after):
        tag, sem, bufs, copies = started
        _, bufs, _ = _comm_call(tag + "_wait", bufs, wait=(sem, copies), after=after)
        return bufs[len(bufs) // 2:]

    place = jnp.stack([lax.axis_index("c"), 2 * lax.axis_index("x") + lax.axis_index("y")]).astype(jnp.int32)

    d_gate, d_up, d_down, dn2 = _ffn_bwd(n2, dh2b, g_act, u_act, w_gate_f, w_up_f, w_down_f)
    down_stack, *ffn_stacks = [d.reshape(N_DEV, D_FF // N_DEV, D_MODEL) for d in (d_down, d_gate, d_up)]
    pair_lands = [lax.empty((N_DEV // 2,) + a.shape[1:], a.dtype) for a in ffn_stacks]
    pair_copies = _numbered(_to_sibling(range(2), range(2, 4)))
    down_copies = _numbered(_scatter([4], [5], EVERYONE + MYSELF), 1)
    (sem, down_sem), bufs, token = _comm_call(
        "ffn_grads_start", ffn_stacks + pair_lands + [down_stack, lax.empty(down_stack.shape, down_stack.dtype)],
        start=(pair_copies + down_copies, (len(pair_copies), len(down_copies))))
    bufs, down_bufs = bufs[:4], bufs[4:]
    dh1, dy_pool, dy_lru, d_out, d_ln2, d_gnp, d_gnl = _bwd_out(dn2, dh2, h1, y_pool, y_lru, gn_pool_g, gn_lru_g,
                                                                 w_out_f, ln2_g, token)
    _, bufs, _ = _comm_call("ffn_pair_wait", bufs, wait=(sem, pair_copies), after=dh1)
    ffn_sums, ffn_lands = _pair_sum(bufs[:2], bufs[2:], place)
    ffn_copies = _numbered(_to_chips(range(2), range(2, 4)))
    ffn_sem, ffn_bufs, token = _comm_call("ffn_chip_start", ffn_sums + ffn_lands, start=(ffn_copies, (len(ffn_copies),)))
    (du_pool, du_lru, du_gate, d_pw, d_ps, d_cw, d_cb, d_wa, d_ba, d_wi, d_bi, d_lam) = _mixer_bwd(
        u_pool, u_lru, u_gate, h, dy_pool, dy_lru, pool_w[0], pool_scale, conv_w_f, conv_b,
        wa_bd, b_a, wi_bd, b_i, lam, token)

    vec_grads = dict(ln2_g=d_ln2, lnf_g=d_lnf, pool_scale=d_ps, conv_b=d_cb, b_a=d_ba, b_i=d_bi,
                     lam=d_lam, gn_pool_g=d_gnp, gn_lru_g=d_gnl)
    packed, packed_b = _pack_small([vec_grads[k] for k, _ in VECTORS], d_pw, d_wa, d_wi, d_cw, sq)
    small_started, token = direct("small", [d_out.reshape(N_DEV, D_MODEL // N_DEV, D_MODEL)], [packed, packed_b], None)
    grad_x, d_in, d_ln1 = _bwd_in(xs, dh1, du_pool, du_lru, du_gate, ln1_g, w_in_f, token)
    in_started, token = direct("in", [d_in.reshape(N_DEV, IN_WIDTH // N_DEV, D_MODEL)], [d_ln1], None)

    results = {}

    def reduce_adam(name, parts):
        outs = _reduce_adam(parts, shard[name](weights[name]), shard[name](mom1[name]), shard[name](mom2[name]),
                            "adam_" + name)
        results[name] = tuple(unshard[name](o) for o in outs)
        return outs[0]

    shard["ln1_g"] = unshard["ln1_g"] = lambda a: a
    _, ffn_bufs, _ = _comm_call("ffn_chip_wait", ffn_bufs, wait=(ffn_sem[0], ffn_copies), after=token)
    done = [reduce_adam(name, parts) for name, parts in zip(("w_ffn_gate", "w_ffn_up"), ffn_bufs[2:])]
    _, (_, r_down), _ = _comm_call("down_wait", down_bufs, after=done,
                                   wait=(down_sem, _numbered(_scatter([0], [1], EVERYONE + MYSELF))))
    done.append(reduce_adam("w_ffn_down", r_down))
    r_out, r_small, r_small_b = direct_finish(small_started, done)
    done.append(reduce_adam("w_out", r_out))

    def as_row(a, width):
        return a.reshape(1, width)

    def as_matrix(a):
        return a.reshape(MATRIX_ROWS, SLAB)

    def as_heads(a):
        return a.reshape(2 * GATE_ROWS, HEAD)

    def heads_apart(g):
        return jnp.transpose(g.reshape(N_SLAB, HEAD, 2, HEAD), (0, 2, 1, 3)).reshape(2 * GATE_ROWS, HEAD)

    vec_out, pool_out, g_wa, g_wi, g_conv, loss_11 = _small_reduce_adam(
        r_small, r_small_b, [as_row(weights[k], w) for k, w in VECTORS], [as_row(mom1[k], w) for k, w in VECTORS],
        [as_row(mom2[k], w) for k, w in VECTORS], [as_matrix(t["pool_w"]) for t in (weights, mom1, mom2)])
    for (k, _), outs in zip(VECTORS, vec_out):
        results[k] = tuple(o.reshape(weights[k].shape) for o in outs)
    results["pool_w"] = tuple(o.reshape(pool_w.shape) for o in pool_out)
    my_columns = conv_w.shape[-1]
    plain_names = ("w_a", "w_i", "conv_w")
    plain_grads = [heads_apart(g) for g in (g_wa, g_wi)]
    plain_grads.append(lax.dynamic_slice_in_dim(g_conv, my_index * my_columns, my_columns, axis=1))
    views = (as_heads, as_heads, lambda a: a[0])
    plain_out = _plain_adam(plain_grads, *[[view(t[k]) for k, view in zip(plain_names, views)]
                                           for t in (weights, mom1, mom2)])
    for k, g, outs in zip(plain_names, plain_grads, plain_out):
        results[k] = tuple(o.reshape(weights[k].shape) for o in (g,) + outs)
    loss = loss_11[0, 0]
    r_in, r_ln1 = direct_finish(in_started, done + [plain_out[0][0], loss_11])
    reduce_adam("w_in", r_in)
    reduce_adam("ln1_g", r_ln1)

    order = ["ln1_g", "w_in", "pool_w", "pool_scale", "conv_w", "conv_b", "w_a", "b_a", "w_i", "b_i", "lam",
             "gn_pool_g", "gn_lru_g", "w_out", "ln2_g", "w_ffn_gate", "w_ffn_up", "w_ffn_down", "lnf_g"]
    return (loss, grad_x[None],
            *[results[k][0] for k in order], *[results[k][1] for k in order],
            *[results[k][2] for k in order], *[results[k][3] for k in order])
```

```python
from typing import Any, NamedTuple

import jax
import jax.numpy as jnp
from jax import lax
from jax.experimental import pallas as pl
from jax.experimental.pallas import tpu as pltpu

F32 = jnp.float32
BF16 = jnp.bfloat16

N_DEV = 8
D_MODEL = 1024
POOL_WIDTH = 512
LRU_WIDTH = 512
IN_WIDTH = 1536
D_FF = 2816
N_SLAB = 4
SLAB = 128
CONV_WIDTH = 4
LRU_C = 8.0
EPS = 1e-6
HALO = 16
FF_CHUNK = 256

ADAM_LR = 0.001
ADAM_B1 = 0.9
ADAM_B2 = 0.999
ADAM_EPS = 1e-08
ADAM_WD = 0.01
ADAM_STEP = 10

MIB = 1 << 20
MESH = pl.DeviceIdType.MESH


def _params(vmem_mib, n_axes=0):
    sem = ("arbitrary",) * n_axes if n_axes else None
    return pltpu.CompilerParams(dimension_semantics=sem, vmem_limit_bytes=vmem_mib * MIB)


def _in_hbm(*arrays):
    return [pltpu.with_memory_space_constraint(a, pltpu.HBM) for a in arrays]


def _mm(a, b):
    return jnp.dot(a, b, preferred_element_type=F32)


def _mm_nt(a, b):
    return lax.dot_general(a, b, (((1,), (1,)), ((), ())), preferred_element_type=F32)


def _mm_tn(a, b):
    return lax.dot_general(a, b, (((0,), (0,)), ((), ())), preferred_element_type=F32)


def _rms(x, g):
    rstd = lax.rsqrt(jnp.mean(x * x, axis=-1, keepdims=True) + EPS)
    xhat = x * rstd
    return xhat * g, xhat, rstd


def _rms_bwd(dy, xhat, rstd, g):
    gy = dy * g
    dx = rstd * (gy - xhat * jnp.mean(gy * xhat, axis=-1, keepdims=True))
    return dx, jnp.sum(dy * xhat, axis=0, keepdims=True)


def _gelu(z):
    t = jnp.tanh(0.7978845608028654 * (z + 0.044715 * z * z * z))
    return 0.5 * z * (1.0 + t), t


def _gelu_grad(z, t):
    return 0.5 * (1.0 + t) + 0.5 * z * (1.0 - t * t) * 0.7978845608028654 * (1.0 + 3.0 * 0.044715 * z * z)


def _softplus_neg(lam):
    x = -lam
    e = jnp.exp(-jnp.abs(x))
    u = 1.0 + e
    l1p = jnp.where(u == 1.0, e, jnp.log(u) * e / (u - 1.0))
    return jnp.maximum(x, 0.0) + l1p


def _one_minus_square(la, a):
    x = 2.0 * la
    series = -x * (1.0 + x * (0.5 + x * (1.0 / 6.0 + x * (1.0 / 24.0))))
    return jnp.where(x > -0.06, series, 1.0 - a * a)


def _sigmoid(x):
    return 0.5 * jnp.tanh(0.5 * x) + 0.5


def _down(v, d):
    return pltpu.roll(v, d, 0)


def _up(v, d):
    return pltpu.roll(v, v.shape[0] - d, 0)


def _slab_major(s, dtype=F32):
    return pltpu.HBM((N_SLAB, s, SLAB), dtype)


def _slabs_spec(tb):
    return pl.BlockSpec((N_SLAB, tb, SLAB), lambda i: (0, i, 0))


def _read_slabs(ref):
    return jnp.concatenate([ref[k] for k in range(N_SLAB)], axis=1)


def _write_slabs(ref, value):
    for k in range(N_SLAB):
        ref[k] = value[:, k * SLAB:(k + 1) * SLAB]


def _token_block(s, most=512):
    for rows in (most, 512, 256):
        if rows <= most and s % rows == 0 and s > rows:
            return rows
    return s


def _time_chunk(s):
    return 512 if s % 512 == 0 and s > 512 else 256 if s % 256 == 0 else s


def _fwd_in(x, ln1_g, w_in_t):
    s = x.shape[0]
    tb = _token_block(s, 1024)

    def body(x_ref, g_ref, w_ref, up_ref, ul_ref, ug_ref):
        n, _, _ = _rms(x_ref[...], g_ref[...])
        proj = _mm_nt(n.astype(BF16), w_ref[...])
        _write_slabs(up_ref, proj[:, :POOL_WIDTH])
        _write_slabs(ul_ref, proj[:, POOL_WIDTH:POOL_WIDTH + LRU_WIDTH])
        _write_slabs(ug_ref, proj[:, POOL_WIDTH + LRU_WIDTH:])

    out = _slab_major(s)
    return pl.pallas_call(
        body, name="fwd_in", grid=(s // tb,),
        in_specs=[pl.BlockSpec((tb, D_MODEL), lambda i: (i, 0)),
                  pl.BlockSpec((1, D_MODEL), lambda i: (0, 0)),
                  pl.BlockSpec((IN_WIDTH, D_MODEL), lambda i: (0, 0))],
        out_specs=[_slabs_spec(tb)] * 3,
        out_shape=[out, out, out],
        compiler_params=_params(40, 1),
    )(*_in_hbm(x, ln1_g, w_in_t))


def _pool_denominator(t0, row, window):
    return jnp.minimum((t0 + row + 1).astype(F32), window)


def _causal_window(ext, deeper):
    s = ext + _down(ext, 1)
    s = s + deeper[0] * _down(s, 2)
    s = s + deeper[1] * _down(s, 4)
    s = s + deeper[2] * _down(s, 8)
    return s[HALO:]


def _anticausal_window(ext, deeper, rows):
    s = ext + _up(ext, 1)
    s = s + deeper[0] * _up(s, 2)
    s = s + deeper[1] * _up(s, 4)
    s = s + deeper[2] * _up(s, 8)
    return s[:rows]


def _conv(taps, cw, cb):
    return cw[3] * taps[0] + cw[2] * taps[1] + cw[1] * taps[2] + cw[0] * taps[3] + cb


def _lru_gates(xc, wa, ba, wi, bi, sp):
    xb = xc.astype(BF16)
    r = _sigmoid(_mm(xb, wa) + ba)
    i = _sigmoid(_mm(xb, wi) + bi)
    la = (-LRU_C) * r * sp
    a = jnp.exp(la)
    mult = jnp.sqrt(jnp.maximum(_one_minus_square(la, a), 0.0))
    return xb, r, i, a, mult


SUBLANES = 8


def _scan_pads(a_pad, b_pad, rows, causal):
    spare = slice(0, SUBLANES) if causal else slice(rows, rows + SUBLANES)
    a_pad[spare, :] = jnp.ones((SUBLANES, SLAB), F32)
    b_pad[spare, :] = jnp.zeros((SUBLANES, SLAB), F32)


def _scan_causal(a, b, h_prev, a_pad, b_pad, rows):
    d = 1
    while d < min(SUBLANES, rows):
        a_pad[SUBLANES:, :] = a
        b_pad[SUBLANES:, :] = b
        b = a * b_pad[SUBLANES - d:SUBLANES - d + rows, :] + b
        a = a * a_pad[SUBLANES - d:SUBLANES - d + rows, :]
        d *= 2
    while d < rows:
        b = jnp.concatenate([b[:d], a[d:] * b[:-d] + b[d:]], axis=0)
        a = jnp.concatenate([a[:d], a[d:] * a[:-d]], axis=0)
        d *= 2
    return b + a * h_prev


def _scan_anticausal(a, b, l_next, a_pad, b_pad, rows):
    d = 1
    while d < min(SUBLANES, rows):
        a_pad[:rows, :] = a
        b_pad[:rows, :] = b
        b = a * b_pad[d:d + rows, :] + b
        a = a * a_pad[d:d + rows, :]
        d *= 2
    while d < rows:
        b = jnp.concatenate([a[:-d] * b[d:] + b[:-d], b[-d:]], axis=0)
        a = jnp.concatenate([a[:-d] * a[d:], a[-d:]], axis=0)
        d *= 2
    return b + a * l_next


def _slab_scalars():
    slab = pl.program_id(0)
    deeper = [jnp.where(slab > k, 1.0, 0.0).astype(F32) for k in range(N_SLAB - 1)]
    window = jnp.left_shift(jnp.int32(2), slab).astype(F32)
    inverse = jnp.where(slab == 0, 0.5, jnp.where(slab == 1, 0.25, jnp.where(slab == 2, 0.125, 0.0625))).astype(F32)
    return deeper, (window, inverse)


def _window_mean(total, t0, row, window, at_start):
    if at_start:
        return total / _pool_denominator(t0, row, window[0])
    return total * window[1]


def _slab_specs(s):
    seq = pl.BlockSpec((None, s, SLAB), lambda k: (k, 0, 0))
    mat = pl.BlockSpec((1, SLAB, SLAB), lambda k: (k, 0, 0))
    vec = pl.BlockSpec((1, SLAB), lambda k: (0, k))
    taps = pl.BlockSpec((CONV_WIDTH, SLAB), lambda k: (0, k))
    return seq, mat, vec, taps


def _mixer_fwd(u_pool, u_lru, u_gate, pool_w, pool_scale, conv_w, conv_b, wa_bd, b_a, wi_bd, b_i, lam):
    s = u_pool.shape[1]
    tc = _time_chunk(s)
    n_chunks = s // tc

    def body(up_ref, ul_ref, ug_ref, pw_ref, ps_ref, cw_ref, cb_ref, wa_ref, ba_ref, wi_ref, bi_ref, lam_ref,
             yp_ref, h_ref, yl_ref, shift_ref, a_pad, b_pad):
        _scan_pads(a_pad, b_pad, tc, causal=True)
        deeper, window = _slab_scalars()
        pw = pw_ref[0].astype(BF16)
        wa = wa_ref[0].astype(BF16)
        wi = wi_ref[0].astype(BF16)
        ps, cb, ba, bi = ps_ref[...], cb_ref[...], ba_ref[...], bi_ref[...]
        cw = [cw_ref[k:k + 1, :] for k in range(CONV_WIDTH)]
        sp = _softplus_neg(lam_ref[...])
        row = lax.broadcasted_iota(jnp.int32, (tc, SLAB), 0)

        def chunk(t0, ext_p, ext_l, h_prev, at_start=False):
            rows = pl.ds(t0, tc)
            d = _window_mean(_causal_window(ext_p, deeper), t0, row, window, at_start) - ext_p[HALO:]
            yp_ref[rows, :] = _mm(d.astype(BF16), pw) * ps
            shift_ref[...] = ext_l
            xc = _conv([shift_ref[HALO - k:HALO - k + tc, :] for k in range(CONV_WIDTH)], cw, cb)
            _, _, i, a, mult = _lru_gates(xc, wa, ba, wi, bi, sp)
            h = _scan_causal(a, mult * (i * xc), h_prev, a_pad, b_pad, tc)
            h_ref[rows, :] = h
            yl_ref[rows, :] = h * _gelu(ug_ref[rows, :])[0]
            return h[tc - 1:tc, :]

        pad = jnp.zeros((HALO, SLAB), F32)
        h0 = chunk(0, jnp.concatenate([pad, up_ref[pl.ds(0, tc), :]], axis=0),
                   jnp.concatenate([pad, ul_ref[pl.ds(0, tc), :]], axis=0), jnp.zeros((1, SLAB), F32), at_start=True)

        def step(c, h_prev):
            t0 = pl.multiple_of(c * tc, tc)
            ext = pl.ds(pl.multiple_of(c * tc - HALO, HALO), tc + HALO)
            return chunk(t0, up_ref[ext, :], ul_ref[ext, :], h_prev)

        lax.fori_loop(1, n_chunks, step, h0)

    seq, mat, vec, taps = _slab_specs(s)
    out = _slab_major(s)
    return pl.pallas_call(
        body, name="mixer_fwd", grid=(N_SLAB,),
        in_specs=[seq, seq, seq, mat, vec, taps, vec, mat, vec, mat, vec, vec],
        out_specs=[seq, seq, seq], out_shape=[out, out, out],
        scratch_shapes=[pltpu.VMEM((tc + HALO, SLAB), F32), pltpu.VMEM((tc + SUBLANES, SLAB), F32),
                        pltpu.VMEM((tc + SUBLANES, SLAB), F32)],
        compiler_params=_params(48, 1),
    )(*_in_hbm(u_pool, u_lru, u_gate, pool_w, pool_scale, conv_w, conv_b, wa_bd, b_a, wi_bd, b_i, lam))


def _fwd_out(x, y_pool, y_lru, gn_pool_g, gn_lru_g, w_out, ln2_g):
    s = x.shape[0]
    tb = _token_block(s, 1024)

    def body(x_ref, yp_ref, yl_ref, gp_ref, gl_ref, w_ref, g2_ref, h1_ref, n2_ref):
        mp, _, _ = _rms(_read_slabs(yp_ref), gp_ref[...])
        ml, _, _ = _rms(_read_slabs(yl_ref), gl_ref[...])
        h1 = x_ref[...] + _mm(mp.astype(BF16), w_ref[:POOL_WIDTH, :]) + _mm(ml.astype(BF16), w_ref[POOL_WIDTH:, :])
        h1_ref[...] = h1
        n2_ref[...] = _rms(h1, g2_ref[...])[0].astype(BF16)

    row = pl.BlockSpec((tb, D_MODEL), lambda i: (i, 0))
    half = _slabs_spec(tb)
    return pl.pallas_call(
        body, name="fwd_out", grid=(s // tb,),
        in_specs=[row, half, half, pl.BlockSpec((1, 512), lambda i: (0, 0)), pl.BlockSpec((1, 512), lambda i: (0, 0)),
                  pl.BlockSpec((D_MODEL, D_MODEL), lambda i: (0, 0)), pl.BlockSpec((1, D_MODEL), lambda i: (0, 0))],
        out_specs=[row, row],
        out_shape=[pltpu.HBM((s, D_MODEL), F32), pltpu.HBM((s, D_MODEL), BF16)],
        compiler_params=_params(40, 1),
    )(*_in_hbm(x, y_pool, y_lru, gn_pool_g, gn_lru_g, w_out, ln2_g))


def _ffn_fwd(h1, n2, target, lnf_g, w_gate, w_up, w_down):
    s = h1.shape[0]
    tb = 512
    sub = 256
    n_ff = D_FF // FF_CHUNK

    def body(h1_ref, n2_ref, t_ref, gf_ref, wg_hbm, wu_hbm, wd_hbm,
             g_ref, u_ref, dh_ref, dhb_ref, dgf_ref, sq_ref, wg, wu, wd, sem):
        @pl.when(pl.program_id(0) == 0)
        def _():
            loads = [pltpu.make_async_copy(src, dst, sem.at[k])
                     for k, (src, dst) in enumerate(((wg_hbm, wg), (wu_hbm, wu), (wd_hbm, wd)))]
            for cp in loads:
                cp.start()
            for cp in loads:
                cp.wait()
            dgf_ref[...] = jnp.zeros_like(dgf_ref)
            sq_ref[...] = jnp.zeros_like(sq_ref)

        n2v = n2_ref[...]
        acc = jnp.zeros((tb, D_MODEL), F32)
        for c in range(n_ff):
            cols = slice(c * FF_CHUNK, (c + 1) * FF_CHUNK)
            g = _mm_nt(n2v, wg[cols, :])
            u = _mm_nt(n2v, wu[cols, :])
            g_ref[:, cols] = g.astype(BF16)
            u_ref[:, cols] = u.astype(BF16)
            act = g * jax.nn.sigmoid(g) * u
            acc = acc + _mm(act.astype(BF16), wd[cols, :])
        gf = gf_ref[...]
        for r in range(tb // sub):
            rows = slice(r * sub, (r + 1) * sub)
            y, xhat, rstd = _rms(h1_ref[rows, :] + acc[rows, :], gf)
            err = y - t_ref[rows, :]
            sq_ref[...] += jnp.sum(err * err, axis=0, keepdims=True)
            dh2, dgf = _rms_bwd(err * (1.0 / D_MODEL), xhat, rstd, gf)
            dgf_ref[...] += dgf
            dh_ref[rows, :] = dh2
            dhb_ref[rows, :] = dh2.astype(BF16)

    row = pl.BlockSpec((tb, D_MODEL), lambda i: (i, 0))
    ff = pl.BlockSpec((tb, D_FF), lambda i: (i, 0))
    vec = pl.BlockSpec((1, D_MODEL), lambda i: (0, 0))
    anyspace = pl.BlockSpec(memory_space=pl.ANY)
    return pl.pallas_call(
        body, name="ffn_fwd", grid=(s // tb,),
        in_specs=[row, row, row, vec, anyspace, anyspace, anyspace],
        out_specs=[ff, ff, row, row, vec, vec],
        out_shape=[pltpu.HBM((s, D_FF), BF16), pltpu.HBM((s, D_FF), BF16),
                   pltpu.HBM((s, D_MODEL), F32), pltpu.HBM((s, D_MODEL), BF16),
                   jax.ShapeDtypeStruct((1, D_MODEL), F32), jax.ShapeDtypeStruct((1, D_MODEL), F32)],
        scratch_shapes=[pltpu.VMEM((D_FF, D_MODEL), BF16), pltpu.VMEM((D_FF, D_MODEL), BF16),
                        pltpu.VMEM((D_FF, D_MODEL), BF16), pltpu.SemaphoreType.DMA((3,))],
        compiler_params=_params(60, 1),
    )(*_in_hbm(h1, n2, target, lnf_g, w_gate, w_up, w_down))


def _ffn_bwd(n2, dh2b, g, u, w_gate_t, w_up_t, w_down):
    s = n2.shape[0]
    tb = min(1024, s)
    n_ff = D_FF // FF_CHUNK
    n_tb = s // tb

    def body(n2_ref, dh_ref, g_ref, u_ref, wg_ref, wu_ref, wd_ref, dwg_ref, dwu_ref, dwd_ref, dn2_ref,
             dn2_acc, acc_g, acc_u, acc_d):
        j = pl.program_id(0)
        t = pl.program_id(1)
        rows = pl.ds(pl.multiple_of(t * tb, tb), tb)

        @pl.when(t == 0)
        def _():
            acc_g[...] = jnp.zeros_like(acc_g)
            acc_u[...] = jnp.zeros_like(acc_u)
            acc_d[...] = jnp.zeros_like(acc_d)

        @pl.when(j == 0)
        def _():
            dn2_acc[rows, :] = jnp.zeros((tb, D_MODEL), F32)

        n2v = n2_ref[...]
        dh = dh_ref[...]
        gv = g_ref[...].astype(F32)
        uv = u_ref[...].astype(F32)
        sg = jax.nn.sigmoid(gv)
        silu = gv * sg
        dact = _mm_nt(dh, wd_ref[...])
        dub = (dact * silu).astype(BF16)
        dgb = (dact * uv * (sg * (1.0 + gv * (1.0 - sg)))).astype(BF16)
        acc_d[...] += _mm_tn((silu * uv).astype(BF16), dh)
        acc_g[...] += _mm_tn(dgb, n2v)
        acc_u[...] += _mm_tn(dub, n2v)
        dn2_acc[rows, :] += _mm(jnp.concatenate([dgb, dub], axis=1),
                                jnp.concatenate([wg_ref[...], wu_ref[...]], axis=0))

        @pl.when(t == n_tb - 1)
        def _():
            dwg_ref[...] = acc_g[...].astype(BF16)
            dwu_ref[...] = acc_u[...].astype(BF16)
            dwd_ref[...] = acc_d[...].astype(BF16)

        @pl.when(j == n_ff - 1)
        def _():
            dn2_ref[...] = dn2_acc[rows, :]

    row = pl.BlockSpec((tb, D_MODEL), lambda j, t: (t, 0))
    act = pl.BlockSpec((tb, FF_CHUNK), lambda j, t: (t, j))
    w_row = pl.BlockSpec((FF_CHUNK, D_MODEL), lambda j, t: (j, 0))
    last = pl.BlockSpec((tb, D_MODEL), lambda j, t: (jnp.where(j == n_ff - 1, t, 0), 0))
    grad = pltpu.HBM((D_FF, D_MODEL), BF16)
    chunk_acc = pltpu.VMEM((FF_CHUNK, D_MODEL), F32)
    return pl.pallas_call(
        body, name="ffn_bwd", grid=(n_ff, n_tb),
        in_specs=[row, row, act, act, w_row, w_row, w_row],
        out_specs=[w_row, w_row, w_row, last],
        out_shape=[grad, grad, grad, pltpu.HBM((s, D_MODEL), F32)],
        scratch_shapes=[pltpu.VMEM((s, D_MODEL), F32), chunk_acc, chunk_acc, chunk_acc],
        compiler_params=_params(56, 2),
    )(*_in_hbm(n2, dh2b, g, u, w_gate_t, w_up_t, w_down))


def _bwd_out(dn2, dh2, h1, y_pool, y_lru, gn_pool_g, gn_lru_g, w_out, ln2_g, after):
    s = h1.shape[0]
    tb = _token_block(s)

    def body(dn2_ref, dh2_ref, h1_ref, yp_ref, yl_ref, gp_ref, gl_ref, w_ref, g2_ref, _after,
             dh1_ref, dyp_ref, dyl_ref, dwb_ref, dg2_ref, dgp_ref, dgl_ref, dw_ref):
        @pl.when(pl.program_id(0) == 0)
        def _():
            dw_ref[...] = jnp.zeros_like(dw_ref)
            dg2_ref[...] = jnp.zeros_like(dg2_ref)
            dgp_ref[...] = jnp.zeros_like(dgp_ref)
            dgl_ref[...] = jnp.zeros_like(dgl_ref)

        g2 = g2_ref[...]
        _, xhat2, rstd2 = _rms(h1_ref[...], g2)
        dres, dg2 = _rms_bwd(dn2_ref[...], xhat2, rstd2, g2)
        dg2_ref[...] += dg2
        dh1 = dh2_ref[...] + dres
        dh1_ref[...] = dh1
        dh1b = dh1.astype(BF16)
        gp, gl = gp_ref[...], gl_ref[...]
        mp, xhat_p, rstd_p = _rms(_read_slabs(yp_ref), gp)
        ml, xhat_l, rstd_l = _rms(_read_slabs(yl_ref), gl)
        dw_ref[:POOL_WIDTH, :] += _mm_tn(mp.astype(BF16), dh1b)
        dw_ref[POOL_WIDTH:, :] += _mm_tn(ml.astype(BF16), dh1b)
        dyp, dgp = _rms_bwd(_mm_nt(dh1b, w_ref[:POOL_WIDTH, :]), xhat_p, rstd_p, gp)
        dyl, dgl = _rms_bwd(_mm_nt(dh1b, w_ref[POOL_WIDTH:, :]), xhat_l, rstd_l, gl)
        _write_slabs(dyp_ref, dyp)
        _write_slabs(dyl_ref, dyl)
        dgp_ref[...] += dgp
        dgl_ref[...] += dgl

        @pl.when(pl.program_id(0) == s // tb - 1)
        def _():
            dwb_ref[...] = dw_ref[...].astype(BF16)

    row = pl.BlockSpec((tb, D_MODEL), lambda i: (i, 0))
    half = _slabs_spec(tb)
    vec = pl.BlockSpec((1, D_MODEL), lambda i: (0, 0))
    hvec = pl.BlockSpec((1, 512), lambda i: (0, 0))
    mat = pl.BlockSpec((D_MODEL, D_MODEL), lambda i: (0, 0))
    return pl.pallas_call(
        body, name="bwd_out", grid=(s // tb,),
        in_specs=[row, row, row, half, half, hvec, hvec, mat, vec, pl.BlockSpec(memory_space=pl.ANY)],
        out_specs=[row, half, half, mat, vec, hvec, hvec],
        out_shape=[pltpu.HBM((s, D_MODEL), F32), _slab_major(s),
                   _slab_major(s), pltpu.HBM((D_MODEL, D_MODEL), BF16),
                   jax.ShapeDtypeStruct((1, D_MODEL), F32), jax.ShapeDtypeStruct((1, 512), F32),
                   jax.ShapeDtypeStruct((1, 512), F32)],
        scratch_shapes=[pltpu.VMEM((D_MODEL, D_MODEL), F32)],
        compiler_params=_params(48, 1),
    )(*_in_hbm(dn2, dh2, h1, y_pool, y_lru, gn_pool_g, gn_lru_g, w_out, ln2_g), after)


def _mixer_bwd(u_pool, u_lru, u_gate, h, dy_pool, dy_lru,
               pool_w, pool_scale, conv_w, conv_b, wa_bd, b_a, wi_bd, b_i, lam, after):
    s = u_pool.shape[1]
    tc = _time_chunk(s)
    n_chunks = s // tc

    def body(up_ref, ul_ref, ug_ref, h_ref, dyp_ref, dyl_ref,
             pw_ref, ps_ref, cw_ref, cb_ref, wa_ref, ba_ref, wi_ref, bi_ref, lam_ref, _after,
             dup_ref, dul_ref, dug_ref, dpw_ref, dps_ref, dcw_ref, dcb_ref, dwa_ref, dba_ref, dwi_ref, dbi_ref, dlam_ref,
             a_pad, b_pad, u_shift, h_shift, a_shift, d_shift):
        _scan_pads(a_pad, b_pad, tc, causal=False)
        deeper, window = _slab_scalars()
        pw = pw_ref[0].astype(BF16)
        wa = wa_ref[0].astype(BF16)
        wi = wi_ref[0].astype(BF16)
        ps, cb, ba, bi = ps_ref[...], cb_ref[...], ba_ref[...], bi_ref[...]
        cw = [cw_ref[k:k + 1, :] for k in range(CONV_WIDTH)]
        lam_v = lam_ref[...]
        sp = _softplus_neg(lam_v)
        row = lax.broadcasted_iota(jnp.int32, (tc, SLAB), 0)
        for ref in (dpw_ref, dps_ref, dcw_ref, dcb_ref, dwa_ref, dba_ref, dwi_ref, dbi_ref, dlam_ref):
            ref[...] = jnp.zeros_like(ref)

        def chunk(t0, ext_p, ext_l, ext_h, carry, at_start=False):
            l_next, a_next, dxc_next, ddn_next = carry
            rows = pl.ds(t0, tc)
            u_shift[...] = ext_l
            taps = [u_shift[HALO - k:HALO - k + tc, :] for k in range(CONV_WIDTH)]
            xc = _conv(taps, cw, cb)
            xb, r, i, a, mult = _lru_gates(xc, wa, ba, wi, bi, sp)
            hv = ext_h[HALO:]
            h_shift[...] = ext_h
            h_before = h_shift[HALO - 1:HALO - 1 + tc, :]
            ug = ug_ref[rows, :]
            dyl = dyl_ref[rows, :]
            gel, th = _gelu(ug)
            dug_ref[rows, :] = (dyl * hv * _gelu_grad(ug, th)).astype(BF16)
            a_shift[:tc, :] = a
            a_shift[tc:, :] = jnp.broadcast_to(a_next, (SUBLANES, SLAB))
            a_after = a_shift[1:1 + tc, :]
            l = _scan_anticausal(a_after, dyl * gel, l_next, a_pad, b_pad, tc)
            dmult = l * (i * xc)
            di = l * mult * xc
            dxc = l * mult * i
            dla = (l * h_before) * a - jnp.where(mult > 0.0, dmult * (a * a) / mult, 0.0)
            dlam_ref[...] += jnp.sum(dla * r, axis=0, keepdims=True)
            dpa = (dla * ((-LRU_C) * sp)) * (r * (1.0 - r))
            dpi = di * (i * (1.0 - i))
            dpab = dpa.astype(BF16)
            dpib = dpi.astype(BF16)
            dwa_ref[0] += _mm_tn(xb, dpab)
            dwi_ref[0] += _mm_tn(xb, dpib)
            dba_ref[...] += jnp.sum(dpa, axis=0, keepdims=True)
            dbi_ref[...] += jnp.sum(dpi, axis=0, keepdims=True)
            dxc = dxc + _mm_nt(dpab, wa) + _mm_nt(dpib, wi)
            d_shift[:tc, :] = dxc
            d_shift[tc:, :] = dxc_next
            dul_ref[rows, :] = (cw[3] * dxc + cw[2] * d_shift[1:1 + tc, :]
                                + cw[1] * d_shift[2:2 + tc, :] + cw[0] * d_shift[3:3 + tc, :]).astype(BF16)
            for k in range(CONV_WIDTH):
                dcw_ref[k:k + 1, :] += jnp.sum(dxc * taps[CONV_WIDTH - 1 - k], axis=0, keepdims=True)
            dcb_ref[...] += jnp.sum(dxc, axis=0, keepdims=True)
            db = (_window_mean(_causal_window(ext_p, deeper), t0, row, window, at_start) - ext_p[HALO:]).astype(BF16)
            dyp = dyp_ref[rows, :]
            dps_ref[...] += jnp.sum(dyp * _mm(db, pw), axis=0, keepdims=True)
            dys = (dyp * ps).astype(BF16)
            dpw_ref[0] += _mm_tn(db, dys)
            dd = _mm_nt(dys, pw)
            ddn = _window_mean(dd, t0, row, window, at_start)
            ext_q = jnp.concatenate([ddn, ddn_next], axis=0)
            dup_ref[rows, :] = (_anticausal_window(ext_q, deeper, tc) - dd).astype(BF16)
            return l[0:1, :], a[0:1, :], dxc[0:8, :], ddn[0:HALO, :]

        def step(k, carry):
            c = n_chunks - 1 - k
            t0 = pl.multiple_of(c * tc, tc)
            ext = pl.ds(pl.multiple_of(c * tc - HALO, HALO), tc + HALO)
            return chunk(t0, up_ref[ext, :], ul_ref[ext, :], h_ref[ext, :], carry)

        carry = (jnp.zeros((1, SLAB), F32), jnp.zeros((1, SLAB), F32),
                 jnp.zeros((8, SLAB), F32), jnp.zeros((HALO, SLAB), F32))
        carry = lax.fori_loop(0, n_chunks - 1, step, carry)
        pad = jnp.zeros((HALO, SLAB), F32)
        first = pl.ds(0, tc)
        chunk(0, jnp.concatenate([pad, up_ref[first, :]], axis=0), jnp.concatenate([pad, ul_ref[first, :]], axis=0),
              jnp.concatenate([pad, h_ref[first, :]], axis=0), carry, at_start=True)
        dlam_ref[...] = dlam_ref[...] * (LRU_C * jax.nn.sigmoid(-lam_v))

    seq, mat, vec, taps = _slab_specs(s)
    grad = _slab_major(s, BF16)
    mats = jax.ShapeDtypeStruct((N_SLAB, SLAB, SLAB), F32)
    vecs = jax.ShapeDtypeStruct((1, 512), F32)
    return pl.pallas_call(
        body, name="mixer_bwd", grid=(N_SLAB,),
        in_specs=[seq] * 6 + [mat, vec, taps, vec, mat, vec, mat, vec, vec, pl.BlockSpec(memory_space=pl.ANY)],
        out_specs=[seq, seq, seq, mat, vec, taps, vec, mat, vec, mat, vec, vec],
        out_shape=[grad, grad, grad, mats, vecs, jax.ShapeDtypeStruct((CONV_WIDTH, 512), F32), vecs,
                   mats, vecs, mats, vecs, vecs],
        scratch_shapes=[pltpu.VMEM((tc + SUBLANES, SLAB), F32), pltpu.VMEM((tc + SUBLANES, SLAB), F32),
                        pltpu.VMEM((tc + HALO, SLAB), F32), pltpu.VMEM((tc + HALO, SLAB), F32),
                        pltpu.VMEM((tc + SUBLANES, SLAB), F32), pltpu.VMEM((tc + SUBLANES, SLAB), F32)],
        compiler_params=_params(56, 1),
    )(*_in_hbm(u_pool, u_lru, u_gate, h, dy_pool, dy_lru, pool_w, pool_scale, conv_w, conv_b, wa_bd, b_a, wi_bd,
               b_i, lam), after)


def _bwd_in(x, dh1, du_pool, du_lru, du_gate, ln1_g, w_in_t, after):
    s = x.shape[0]
    tb = _token_block(s)

    def body(x_ref, dh1_ref, dup_ref, dul_ref, dug_ref, g_ref, w_ref, _after, dx_ref, dwb_ref, dg_ref, dw_ref):
        @pl.when(pl.program_id(0) == 0)
        def _():
            dw_ref[...] = jnp.zeros_like(dw_ref)
            dg_ref[...] = jnp.zeros_like(dg_ref)

        g1 = g_ref[...]
        n, xhat, rstd = _rms(x_ref[...], g1)
        nb = n.astype(BF16)
        db = jnp.concatenate([_read_slabs(ref) for ref in (dup_ref, dul_ref, dug_ref)], axis=1)
        dw_ref[...] += _mm_tn(db, nb)
        dx, dg1 = _rms_bwd(_mm(db, w_ref[...]), xhat, rstd, g1)
        dx_ref[...] = dh1_ref[...] + dx
        dg_ref[...] += dg1

        @pl.when(pl.program_id(0) == s // tb - 1)
        def _():
            dwb_ref[...] = dw_ref[...].astype(BF16)

    row = pl.BlockSpec((tb, D_MODEL), lambda i: (i, 0))
    half = _slabs_spec(tb)
    vec = pl.BlockSpec((1, D_MODEL), lambda i: (0, 0))
    mat = pl.BlockSpec((IN_WIDTH, D_MODEL), lambda i: (0, 0))
    return pl.pallas_call(
        body, name="bwd_in", grid=(s // tb,),
        in_specs=[row, row, half, half, half, vec, mat, pl.BlockSpec(memory_space=pl.ANY)],
        out_specs=[row, mat, vec],
        out_shape=[pltpu.HBM((s, D_MODEL), F32), pltpu.HBM((IN_WIDTH, D_MODEL), BF16),
                   jax.ShapeDtypeStruct((1, D_MODEL), F32)],
        scratch_shapes=[pltpu.VMEM((IN_WIDTH, D_MODEL), F32)],
        compiler_params=_params(48, 1),
    )(*_in_hbm(x, dh1, du_pool, du_lru, du_gate, ln1_g, w_in_t), after)


def _mesh_position():
    x, y, c = lax.axis_index("x"), lax.axis_index("y"), lax.axis_index("c")
    return x, y, c, 4 * x + 2 * y + c


def _peer(x, y, c, p):
    px = 1 - x if p & 4 else x
    py = 1 - y if p & 2 else y
    pc = 1 - c if p & 1 else c
    return (px, py, pc), 4 * px + 2 * py + pc


HBM_SPEC = pl.BlockSpec(memory_space=pltpu.HBM)
SEM_SPEC = pl.BlockSpec(memory_space=pltpu.SEMAPHORE)
DATAFLOW = pltpu.SideEffectType.DATAFLOW_SIDE_EFFECTING


class Copy(NamedTuple):
    src: int
    src_at: Any
    dst: int
    dst_at: Any
    peer: int
    group: int
    slot: int


SIBLING = (1,)
SAME_CORE = (2, 4, 6)
EVERYONE = tuple(range(1, N_DEV))
MYSELF = (0,)


def _same(index):
    return index


def _chip(index):
    return jnp.right_shift(index, 1)


def _fan_out(srcs, lands, peers, group):
    return [Copy(s, None, d, _same, p, group, N_DEV * i + p) for i, (s, d) in enumerate(zip(srcs, lands)) for p in peers]


def _scatter(stacks, lands, peers):
    return [Copy(s, lambda me, p=p: jnp.bitwise_xor(me, p), d, _same, p, 0, 0)
            for s, d in zip(stacks, lands) for p in peers]


def _numbered(copies, group=0):
    return [cp._replace(group=group, slot=i) for i, cp in enumerate(copies)]


def _relay(lands, peers):
    return [Copy(b, lambda s, q=q: jnp.bitwise_xor(s, q), b, lambda s, q=q: jnp.bitwise_xor(s, q), 1, 0, N_DEV * i + q)
            for i, b in enumerate(lands) for q in peers]


def _to_sibling(stacks, lands):
    return [Copy(s, lambda me, k=k: 2 * k + 1 - jnp.bitwise_and(me, 1), d, lambda me, k=k: k, 1, 0, 4 * i + k)
            for i, (s, d) in enumerate(zip(stacks, lands)) for k in range(N_DEV // 2)]


def _to_chips(sums, lands):
    return [Copy(s, lambda me, p=p: jnp.bitwise_xor(_chip(me), p // 2), d, _chip, p, 0, 4 * i + p // 2)
            for i, (s, d) in enumerate(zip(sums, lands)) for p in SAME_CORE]


def _comm_call(name, bufs, wait=None, start=None, after=()):
    nb = len(bufs)
    waits = [wait] if isinstance(wait, tuple) else list(wait or [])
    slots = list(start[1]) if start else []
    n_out_sem = 2 * len(slots)
    after = [a for a in (after if isinstance(after, (list, tuple)) else [after]) if a is not None]

    def body(*refs):
        b = refs[:nb]
        at = nb + 2 * len(waits) + len(after)
        out_sems = refs[at:at + n_out_sem]
        token = refs[at + n_out_sem + nb]
        x, y, c, me = _mesh_position()

        def part(i, row_of, sender):
            return b[i] if row_of is None else b[i].at[row_of(sender)]

        for k, (_, copies) in enumerate(waits):
            w_send, w_recv = refs[nb + 2 * k], refs[nb + 2 * k + 1]
            for cp in copies:
                peer, peer_index = _peer(x, y, c, cp.peer)
                arrival = pltpu.make_async_remote_copy(part(cp.src, cp.src_at, me), part(cp.dst, cp.dst_at, peer_index),
                                                       w_send.at[cp.slot], w_recv.at[cp.slot],
                                                       device_id=peer, device_id_type=MESH)
                arrival.wait_send()
                arrival.wait_recv()
        if start:
            for cp in start[0]:
                peer, _ = _peer(x, y, c, cp.peer)
                pltpu.make_async_remote_copy(part(cp.src, cp.src_at, me), part(cp.dst, cp.dst_at, me),
                                             out_sems[2 * cp.group].at[cp.slot], out_sems[2 * cp.group + 1].at[cp.slot],
                                             device_id=peer, device_id_type=MESH).start()
        token[...] = jnp.zeros_like(token)

    sem_shapes = []
    for n_slots in slots:
        sem_shapes += [pltpu.SemaphoreType.DMA((n_slots,))] * 2
    operands = [pltpu.with_memory_space_constraint(a, pltpu.HBM) for a in bufs]
    in_specs = [HBM_SPEC] * nb
    for sems, _ in waits:
        operands += list(sems)
        in_specs += [SEM_SPEC, SEM_SPEC]
    operands += after
    in_specs += [pl.BlockSpec(memory_space=pl.ANY)] * len(after)
    outs = pl.pallas_call(
        body, name=name, in_specs=in_specs,
        out_specs=[SEM_SPEC] * n_out_sem + [HBM_SPEC] * nb + [pl.BlockSpec(memory_space=pltpu.VMEM)],
        out_shape=sem_shapes + [pltpu.HBM(a.shape, a.dtype) for a in bufs] + [jax.ShapeDtypeStruct((8, SLAB), F32)],
        input_output_aliases={i: n_out_sem + i for i in range(nb)},
        compiler_params=pltpu.CompilerParams(has_side_effects=DATAFLOW),
    )(*operands)
    sems = [(outs[2 * k], outs[2 * k + 1]) for k in range(len(slots))]
    return sems, list(outs[n_out_sem:n_out_sem + nb]), outs[-1]


def _pair_sum(stacks, lands, place):
    n = len(stacks)

    def body(place_ref, *refs):
        k = pl.program_id(0)
        for m in range(n):
            mine, theirs, out, land = refs[m], refs[n + m], refs[2 * n + m], refs[3 * n + m]
            total = (mine[0, 0].astype(F32) + theirs[0].astype(F32)).astype(out.dtype)
            out[0] = total

            @pl.when(k == place_ref[1])
            def _():
                land[0] = total

    in_specs = [pl.BlockSpec((1, 1) + a.shape[1:], lambda k, place_ref: (k, place_ref[0], 0, 0)) for a in stacks]
    in_specs += [pl.BlockSpec((1,) + a.shape[1:], lambda k, place_ref: (k, 0, 0)) for a in lands]
    out_specs = [pl.BlockSpec((1,) + a.shape[1:], lambda k, place_ref: (k, 0, 0)) for a in lands]
    out_specs += [pl.BlockSpec((1,) + a.shape[1:], lambda k, place_ref: (place_ref[1], 0, 0)) for a in lands]
    outs = pl.pallas_call(
        body, name="pair_sum_" + "_".join(str(a.shape[1]) for a in stacks),
        grid_spec=pltpu.PrefetchScalarGridSpec(num_scalar_prefetch=1, grid=(N_DEV // 2,), in_specs=in_specs,
                                               out_specs=out_specs),
        out_shape=[pltpu.HBM(a.shape, a.dtype) for a in lands] * 2,
        compiler_params=_params(40, 1),
    )(place, *_in_hbm(*[a.reshape((N_DEV // 2, 2) + a.shape[1:]) for a in stacks], *lands))
    return list(outs[:n]), list(outs[n:])


def _reduce_adam(parts, w, m, v, name):
    rows, cols = w.shape
    n_parts = parts.shape[0]
    rb = rows
    for cand in (256, 176, 128):
        if rows % cand == 0 and rows > cand:
            rb = cand
            break

    def body(p_ref, w_ref, m_ref, v_ref, g_out, d_out, m_out, v_out):
        g = p_ref[0].astype(F32)
        for j in range(1, n_parts):
            g = g + p_ref[j].astype(F32)
        g_out[...] = g
        d_out[...], m_out[...], v_out[...] = _adam(g, w_ref[...], m_ref[...], v_ref[...])

    blk = pl.BlockSpec((rb, cols), lambda i: (i, 0))
    out = jax.ShapeDtypeStruct((rows, cols), F32)
    return pl.pallas_call(
        body, name=name, grid=(rows // rb,),
        in_specs=[pl.BlockSpec((n_parts, rb, cols), lambda i: (0, i, 0)), blk, blk, blk],
        out_specs=[blk] * 4, out_shape=[out] * 4,
        compiler_params=_params(40, 1),
    )(*_in_hbm(parts, w, m, v))


def _cols_from_stack(stack):
    n, r, c = stack.shape
    return jnp.transpose(stack, (1, 0, 2)).reshape(r, n * c)


def _block_diag(w):
    z = jnp.zeros((N_SLAB, 64, 64), w.dtype)
    pairs = w.reshape(N_SLAB, 2, 64, 64)
    top = jnp.concatenate([pairs[:, 0], z], axis=2)
    bottom = jnp.concatenate([z, pairs[:, 1]], axis=2)
    return jnp.concatenate([top, bottom], axis=1)


def _adam(g, w, m, v):
    m_new = ADAM_B1 * m + (1.0 - ADAM_B1) * g
    v_new = ADAM_B2 * v + (1.0 - ADAM_B2) * (g * g)
    m_hat = m_new / (1.0 - ADAM_B1 ** ADAM_STEP)
    v_hat = v_new / (1.0 - ADAM_B2 ** ADAM_STEP)
    return (-ADAM_LR) * (m_hat / (jnp.sqrt(v_hat) + ADAM_EPS) + ADAM_WD * w), m_new, v_new


WIDE = ("ln2_g", "lnf_g")
HALF = ("pool_scale", "conv_b", "b_a", "b_i", "lam", "gn_pool_g", "gn_lru_g")
VECTORS = [(k, D_MODEL) for k in WIDE] + [(k, 512) for k in HALF]
VECTOR_ROWS = sum(width // SLAB for _, width in VECTORS)
LOSS_ROW = -(-VECTOR_ROWS // 8) * 8
CONV_AT = LOSS_ROW + 8
CONV_LANES = LRU_WIDTH // SLAB
PACK_F_ROWS = CONV_AT + CONV_WIDTH * CONV_LANES
MATRIX_ROWS = N_SLAB * SLAB
HEAD = SLAB // 2
GATE_ROWS = N_SLAB * HEAD
PACK_B_ROWS = MATRIX_ROWS + 2 * GATE_ROWS


def _pack_small(vectors, pool_g, wa_g, wi_g, conv, sq):
    n_vec = len(vectors)

    def body(*refs):
        vec = refs[:n_vec]
        pw_ref, wa_ref, wi_ref, cw_ref, sq_ref, out, out_b = refs[n_vec:]
        out[...] = jnp.zeros_like(out)
        row = 0
        for ref, (_, width) in zip(vec, VECTORS):
            for k in range(width // SLAB):
                out[row:row + 1, :] = ref[:, k * SLAB:(k + 1) * SLAB]
                row += 1
        for tap in range(CONV_WIDTH):
            for k in range(CONV_LANES):
                at = CONV_AT + tap * CONV_LANES + k
                out[at:at + 1, :] = cw_ref[tap:tap + 1, k * SLAB:(k + 1) * SLAB]
        total = sq_ref[:, 0:SLAB]
        for k in range(1, D_MODEL // SLAB):
            total = total + sq_ref[:, k * SLAB:(k + 1) * SLAB]
        out[LOSS_ROW:LOSS_ROW + 1, :] = total
        left = lax.broadcasted_iota(jnp.int32, (HEAD, SLAB), 1) < HEAD
        for s in range(N_SLAB):
            out_b[s * SLAB:(s + 1) * SLAB, :] = pw_ref[s].astype(BF16)
            for i, ref in enumerate((wa_ref, wi_ref)):
                at = MATRIX_ROWS + i * GATE_ROWS + s * HEAD
                out_b[at:at + HEAD, :] = jnp.where(left, ref[s, 0:HEAD, :], ref[s, HEAD:SLAB, :]).astype(BF16)

    return pl.pallas_call(
        body, name="pack_small",
        out_shape=[jax.ShapeDtypeStruct((PACK_F_ROWS, SLAB), F32), jax.ShapeDtypeStruct((PACK_B_ROWS, SLAB), BF16)],
    )(*vectors, pool_g, wa_g, wi_g, conv, sq)


def _small_reduce_adam(parts, parts_b, vec_w, vec_m, vec_v, pool_wmv):
    n_vec = len(VECTORS)
    n_parts = parts.shape[0]

    def body(*refs):
        p_ref, pb_ref = refs[0], refs[1]
        w_refs, m_refs, v_refs = (refs[2 + k * n_vec:2 + (k + 1) * n_vec] for k in range(3))
        pw_w, pw_m, pw_v = refs[2 + 3 * n_vec:5 + 3 * n_vec]
        outs = refs[5 + 3 * n_vec:-1]
        total = refs[-1]
        total[...] = p_ref[0]
        for j in range(1, n_parts):
            total[...] += p_ref[j]
        row = 0
        for i, (_, width) in enumerate(VECTORS):
            n_rows = width // SLAB
            g = jnp.concatenate([total[row + k:row + k + 1, :] for k in range(n_rows)], axis=1)
            row += n_rows
            d, m_new, v_new = _adam(g, w_refs[i][...], m_refs[i][...], v_refs[i][...])
            for ref, val in zip(outs[4 * i:4 * i + 4], (g, d, m_new, v_new)):
                ref[...] = val
        tail = outs[4 * n_vec:]

        def summed(first, count):
            g = pb_ref[0, first:first + count, :].astype(F32)
            for j in range(1, n_parts):
                g = g + pb_ref[j, first:first + count, :].astype(F32)
            return g

        g = summed(0, MATRIX_ROWS)
        d, m_new, v_new = _adam(g, pw_w[...], pw_m[...], pw_v[...])
        for ref, val in zip(tail[0:4], (g, d, m_new, v_new)):
            ref[...] = val
        tail[4][...] = summed(MATRIX_ROWS, GATE_ROWS)
        tail[5][...] = summed(MATRIX_ROWS + GATE_ROWS, GATE_ROWS)
        for tap in range(CONV_WIDTH):
            at = CONV_AT + tap * CONV_LANES
            tail[6][tap:tap + 1, :] = jnp.concatenate([total[at + k:at + k + 1, :] for k in range(CONV_LANES)], axis=1)
        tail[7][...] = (0.5 / D_MODEL) * jnp.sum(total[LOSS_ROW:LOSS_ROW + 1, :], axis=1, keepdims=True)

    out_shape = []
    for _, width in VECTORS:
        out_shape += [jax.ShapeDtypeStruct((1, width), F32)] * 4
    out_shape += [jax.ShapeDtypeStruct((MATRIX_ROWS, SLAB), F32)] * 4 + [jax.ShapeDtypeStruct((GATE_ROWS, SLAB), F32)] * 2
    out_shape += [jax.ShapeDtypeStruct((CONV_WIDTH, LRU_WIDTH), F32), jax.ShapeDtypeStruct((1, 1), F32)]
    outs = pl.pallas_call(
        body, name="adam_small", out_shape=out_shape,
        scratch_shapes=[pltpu.VMEM((PACK_F_ROWS, SLAB), F32)],
        compiler_params=_params(40),
    )(parts, parts_b, *vec_w, *vec_m, *vec_v, *pool_wmv)
    vec_out = [tuple(outs[4 * i:4 * i + 4]) for i in range(n_vec)]
    tail = outs[4 * n_vec:]
    return vec_out, tuple(tail[0:4]), tail[4], tail[5], tail[6], tail[7]


def _plain_adam(grads, ws, ms, vs):
    n = len(grads)

    def body(*refs):
        ins, outs = refs[:4 * n], refs[4 * n:]
        for i in range(n):
            d, m_new, v_new = _adam(ins[i][...], ins[n + i][...], ins[2 * n + i][...], ins[3 * n + i][...])
            for ref, val in zip(outs[3 * i:3 * i + 3], (d, m_new, v_new)):
                ref[...] = val

    out_shape = []
    for g in grads:
        out_shape += [jax.ShapeDtypeStruct(g.shape, F32)] * 3
    outs = pl.pallas_call(body, name="adam_plain", out_shape=out_shape)(*grads, *ws, *ms, *vs)
    return [tuple(outs[3 * i:3 * i + 3]) for i in range(n)]


def kernel(x, ln1_g, w_in, pool_w, pool_scale, conv_w, conv_b, w_a, b_a, w_i, b_i, lam, gn_pool_g, gn_lru_g, w_out, ln2_g, w_ffn_gate, w_ffn_up, w_ffn_down, lnf_g, loss_target, m_ln1_g, m_w_in, m_pool_w, m_pool_scale, m_conv_w, m_conv_b, m_w_a, m_b_a, m_w_i, m_b_i, m_lam, m_gn_pool_g, m_gn_lru_g, m_w_out, m_ln2_g, m_w_ffn_gate, m_w_ffn_up, m_w_ffn_down, m_lnf_g, v_ln1_g, v_w_in, v_pool_w, v_pool_scale, v_conv_w, v_conv_b, v_w_a, v_b_a, v_w_i, v_b_i, v_lam, v_gn_pool_g, v_gn_lru_g, v_w_out, v_ln2_g, v_w_ffn_gate, v_w_ffn_up, v_w_ffn_down, v_lnf_g):
    weights = dict(ln1_g=ln1_g, w_in=w_in, pool_w=pool_w, pool_scale=pool_scale, conv_w=conv_w, conv_b=conv_b,
                   w_a=w_a, b_a=b_a, w_i=w_i, b_i=b_i, lam=lam, gn_pool_g=gn_pool_g, gn_lru_g=gn_lru_g,
                   w_out=w_out, ln2_g=ln2_g, w_ffn_gate=w_ffn_gate, w_ffn_up=w_ffn_up, w_ffn_down=w_ffn_down,
                   lnf_g=lnf_g)
    mom1 = dict(ln1_g=m_ln1_g, w_in=m_w_in, pool_w=m_pool_w, pool_scale=m_pool_scale, conv_w=m_conv_w,
                conv_b=m_conv_b, w_a=m_w_a, b_a=m_b_a, w_i=m_w_i, b_i=m_b_i, lam=m_lam, gn_pool_g=m_gn_pool_g,
                gn_lru_g=m_gn_lru_g, w_out=m_w_out, ln2_g=m_ln2_g, w_ffn_gate=m_w_ffn_gate,
                w_ffn_up=m_w_ffn_up, w_ffn_down=m_w_ffn_down, lnf_g=m_lnf_g)
    mom2 = dict(ln1_g=v_ln1_g, w_in=v_w_in, pool_w=v_pool_w, pool_scale=v_pool_scale, conv_w=v_conv_w,
                conv_b=v_conv_b, w_a=v_w_a, b_a=v_b_a, w_i=v_w_i, b_i=v_b_i, lam=v_lam, gn_pool_g=v_gn_pool_g,
                gn_lru_g=v_gn_lru_g, w_out=v_w_out, ln2_g=v_ln2_g, w_ffn_gate=v_w_ffn_gate,
                w_ffn_up=v_w_ffn_up, w_ffn_down=v_w_ffn_down, lnf_g=v_lnf_g)

    xs = x[0]
    target = loss_target[0]

    shard = dict(w_in=lambda a: a[0].T, w_ffn_gate=lambda a: a[0].T, w_ffn_up=lambda a: a[0].T,
                 w_out=lambda a: a[0], w_ffn_down=lambda a: a[0], conv_w=lambda a: a[0])
    unshard = dict(w_in=lambda a: a.T[None], w_ffn_gate=lambda a: a.T[None], w_ffn_up=lambda a: a.T[None],
                   w_out=lambda a: a[None], w_ffn_down=lambda a: a[None], conv_w=lambda a: a[None])

    gathered = ("w_in", "conv_w", "w_out", "w_ffn_gate", "w_ffn_up", "w_ffn_down")
    groups = ((0, 1), (2,), (3, 4), (5,))
    sources = [shard[k](weights[k]) if k == "conv_w" else shard[k](weights[k]).astype(BF16) for k in gathered]
    my_index = 4 * lax.axis_index("x") + 2 * lax.axis_index("y") + lax.axis_index("c")
    lands = [lax.empty((N_DEV,) + a.shape, a.dtype) for a in sources]

    def first_hop(n, at=0):
        return _numbered(_fan_out(range(at, at + n), range(at + n, at + 2 * n), SAME_CORE + SIBLING + MYSELF, 0))

    def second_hop(n, at=0):
        return _numbered(_relay(range(at, at + n), SAME_CORE))

    start = []
    for g, members in enumerate(groups):
        start += _numbered(_fan_out(members, [6 + m for m in members], SAME_CORE + SIBLING + MYSELF, 0), g)
    sems, bufs, _ = _comm_call("gather_start", sources + lands,
                               start=(start, [sum(cp.group == g for cp in start) for g in range(len(groups))]))
    sources, lands = bufs[:6], bufs[6:]

    def group_buffers(g):
        return [sources[m] for m in groups[g]] + [lands[m] for m in groups[g]]

    def relayed(tag, g, after):
        n = len(groups[g])
        relay = second_hop(n, n)
        relay_sems, bufs, _ = _comm_call("gather_relay_" + tag, group_buffers(g), wait=(sems[g], first_hop(n)),
                                         start=(relay, (len(relay),)), after=after)
        return relay_sems[0], bufs[n:]

    relay_sem, bufs = relayed("in", 0, None)
    g_in, g_conv = _comm_call("gather_wait_in", bufs, wait=(relay_sem, second_hop(2)))[1]
    w_in_f = g_in.reshape(IN_WIDTH, D_MODEL)
    conv_w_f = _cols_from_stack(g_conv)

    wa_bd = _block_diag(w_a[0])
    wi_bd = _block_diag(w_i[0])
    lnf_row = lnf_g.reshape(1, D_MODEL)

    u_pool, u_lru, u_gate = _fwd_in(xs, ln1_g, w_in_f)
    y_pool, h, y_lru = _mixer_fwd(u_pool, u_lru, u_gate, pool_w[0], pool_scale, conv_w_f, conv_b,
                                  wa_bd, b_a, wi_bd, b_i, lam)
    relay_sem, bufs = relayed("out", 1, y_pool)
    gate_up = second_hop(2, 3)
    (gate_up_sem,), bufs, _ = _comm_call("gather_wait_out", bufs + group_buffers(2),
                                         wait=[(relay_sem, second_hop(1)), (sems[2], first_hop(2, 1))],
                                         start=(gate_up, (len(gate_up),)))
    g_out, gate_up_lands = bufs[0], bufs[3:]
    w_out_f = g_out.reshape(D_MODEL, D_MODEL)
    h1, n2 = _fwd_out(xs, y_pool, y_lru, gn_pool_g, gn_lru_g, w_out_f, ln2_g)
    relay_sem, bufs = relayed("down", 3, n2)
    g_gate, g_up, g_down = _comm_call("gather_wait_ffn", gate_up_lands + bufs,
                                      wait=[(gate_up_sem, second_hop(2)), (relay_sem, second_hop(1, 2))])[1]
    w_gate_f = g_gate.reshape(D_FF, D_MODEL)
    w_up_f = g_up.reshape(D_FF, D_MODEL)
    w_down_f = g_down.reshape(D_FF, D_MODEL)
    g_act, u_act, dh2, dh2b, d_lnf, sq = _ffn_fwd(h1, n2, target, lnf_row, w_gate_f, w_up_f, w_down_f)

    def direct(tag, stacks, wholes, after):
        sources = list(stacks) + list(wholes)
        n, n_st = len(sources), len(stacks)
        lands = [lax.empty(a.shape if i < n_st else (N_DEV,) + a.shape, a.dtype) for i, a in enumerate(sources)]
        copies = _numbered(_scatter(range(n_st), range(n, n + n_st), EVERYONE + MYSELF)
                           + _fan_out(range(n_st, n), range(n + n_st, 2 * n), EVERYONE + MYSELF, 0))
        sem, bufs, token = _comm_call(tag + "_start", sources + lands, start=(copies, (len(copies),)), after=after)
        return (tag, sem[0], bufs, copies), token

    def direct_finish(started, after):
        tag, sem, bufs, copies = started
        _, bufs, _ = _comm_call(tag + "_wait", bufs, wait=(sem, copies), after=after)
        return bufs[len(bufs) // 2:]

    place = jnp.stack([lax.axis_index("c"), 2 * lax.axis_index("x") + lax.axis_index("y")]).astype(jnp.int32)

    d_gate, d_up, d_down, dn2 = _ffn_bwd(n2, dh2b, g_act, u_act, w_gate_f, w_up_f, w_down_f)
    down_stack, *ffn_stacks = [d.reshape(N_DEV, D_FF // N_DEV, D_MODEL) for d in (d_down, d_gate, d_up)]
    pair_lands = [lax.empty((N_DEV // 2,) + a.shape[1:], a.dtype) for a in ffn_stacks]
    pair_copies = _numbered(_to_sibling(range(2), range(2, 4)))
    down_copies = _numbered(_scatter([4], [5], EVERYONE + MYSELF), 1)
    (sem, down_sem), bufs, token = _comm_call(
        "ffn_grads_start", ffn_stacks + pair_lands + [down_stack, lax.empty(down_stack.shape, down_stack.dtype)],
        start=(pair_copies + down_copies, (len(pair_copies), len(down_copies))))
    bufs, down_bufs = bufs[:4], bufs[4:]
    dh1, dy_pool, dy_lru, d_out, d_ln2, d_gnp, d_gnl = _bwd_out(dn2, dh2, h1, y_pool, y_lru, gn_pool_g, gn_lru_g,
                                                                 w_out_f, ln2_g, token)
    _, bufs, _ = _comm_call("ffn_pair_wait", bufs, wait=(sem, pair_copies), after=dh1)
    ffn_sums, ffn_lands = _pair_sum(bufs[:2], bufs[2:], place)
    ffn_copies = _numbered(_to_chips(range(2), range(2, 4)))
    ffn_sem, ffn_bufs, token = _comm_call("ffn_chip_start", ffn_sums + ffn_lands, start=(ffn_copies, (len(ffn_copies),)))
    (du_pool, du_lru, du_gate, d_pw, d_ps, d_cw, d_cb, d_wa, d_ba, d_wi, d_bi, d_lam) = _mixer_bwd(
        u_pool, u_lru, u_gate, h, dy_pool, dy_lru, pool_w[0], pool_scale, conv_w_f, conv_b,
        wa_bd, b_a, wi_bd, b_i, lam, token)

    vec_grads = dict(ln2_g=d_ln2, lnf_g=d_lnf, pool_scale=d_ps, conv_b=d_cb, b_a=d_ba, b_i=d_bi,
                     lam=d_lam, gn_pool_g=d_gnp, gn_lru_g=d_gnl)
    packed, packed_b = _pack_small([vec_grads[k] for k, _ in VECTORS], d_pw, d_wa, d_wi, d_cw, sq)
    small_started, token = direct("small", [d_out.reshape(N_DEV, D_MODEL // N_DEV, D_MODEL)], [packed, packed_b], None)
    grad_x, d_in, d_ln1 = _bwd_in(xs, dh1, du_pool, du_lru, du_gate, ln1_g, w_in_f, token)
    in_started, token = direct("in", [d_in.reshape(N_DEV, IN_WIDTH // N_DEV, D_MODEL)], [d_ln1], None)

    results = {}

    def reduce_adam(name, parts):
        outs = _reduce_adam(parts, shard[name](weights[name]), shard[name](mom1[name]), shard[name](mom2[name]),
                            "adam_" + name)
        results[name] = tuple(unshard[name](o) for o in outs)
        return outs[0]

    shard["ln1_g"] = unshard["ln1_g"] = lambda a: a
    _, ffn_bufs, _ = _comm_call("ffn_chip_wait", ffn_bufs, wait=(ffn_sem[0], ffn_copies), after=token)
    done = [reduce_adam(name, parts) for name, parts in zip(("w_ffn_gate", "w_ffn_up"), ffn_bufs[2:])]
    _, (_, r_down), _ = _comm_call("down_wait", down_bufs, after=done,
                                   wait=(down_sem, _numbered(_scatter([0], [1], EVERYONE + MYSELF))))
    done.append(reduce_adam("w_ffn_down", r_down))
    r_out, r_small, r_small_b = direct_finish(small_started, done)
    done.append(reduce_adam("w_out", r_out))

    def as_row(a, width):
        return a.reshape(1, width)

    def as_matrix(a):
        return a.reshape(MATRIX_ROWS, SLAB)

    def as_heads(a):
        return a.reshape(2 * GATE_ROWS, HEAD)

    def heads_apart(g):
        return jnp.transpose(g.reshape(N_SLAB, HEAD, 2, HEAD), (0, 2, 1, 3)).reshape(2 * GATE_ROWS, HEAD)

    vec_out, pool_out, g_wa, g_wi, g_conv, loss_11 = _small_reduce_adam(
        r_small, r_small_b, [as_row(weights[k], w) for k, w in VECTORS], [as_row(mom1[k], w) for k, w in VECTORS],
        [as_row(mom2[k], w) for k, w in VECTORS], [as_matrix(t["pool_w"]) for t in (weights, mom1, mom2)])
    for (k, _), outs in zip(VECTORS, vec_out):
        results[k] = tuple(o.reshape(weights[k].shape) for o in outs)
    results["pool_w"] = tuple(o.reshape(pool_w.shape) for o in pool_out)
    my_columns = conv_w.shape[-1]
    plain_names = ("w_a", "w_i", "conv_w")
    plain_grads = [heads_apart(g) for g in (g_wa, g_wi)]
    plain_grads.append(lax.dynamic_slice_in_dim(g_conv, my_index * my_columns, my_columns, axis=1))
    views = (as_heads, as_heads, lambda a: a[0])
    plain_out = _plain_adam(plain_grads, *[[view(t[k]) for k, view in zip(plain_names, views)]
                                           for t in (weights, mom1, mom2)])
    for k, g, outs in zip(plain_names, plain_grads, plain_out):
        results[k] = tuple(o.reshape(weights[k].shape) for o in (g,) + outs)
    loss = loss_11[0, 0]
    r_in, r_ln1 = direct_finish(in_started, done + [plain_out[0][0], loss_11])
    reduce_adam("w_in", r_in)
    reduce_adam("ln1_g", r_ln1)

    order = ["ln1_g", "w_in", "pool_w", "pool_scale", "conv_w", "conv_b", "w_a", "b_a", "w_i", "b_i", "lam",
             "gn_pool_g", "gn_lru_g", "w_out", "ln2_g", "w_ffn_gate", "w_ffn_up", "w_ffn_down", "lnf_g"]
    return (loss, grad_x[None],
            *[results[k][0] for k in order], *[results[k][1] for k in order],
            *[results[k][2] for k in order], *[results[k][3] for k in order])
```

```python
from typing import Any, NamedTuple

import jax
import jax.numpy as jnp
from jax import lax
from jax.experimental import pallas as pl
from jax.experimental.pallas import tpu as pltpu

F32 = jnp.float32
BF16 = jnp.bfloat16

N_DEV = 8
D_MODEL = 1024
POOL_WIDTH = 512
LRU_WIDTH = 512
IN_WIDTH = 1536
D_FF = 2816
N_SLAB = 4
SLAB = 128
CONV_WIDTH = 4
LRU_C = 8.0
EPS = 1e-6
HALO = 16
FF_CHUNK = 256

ADAM_LR = 0.001
ADAM_B1 = 0.9
ADAM_B2 = 0.999
ADAM_EPS = 1e-08
ADAM_WD = 0.01
ADAM_STEP = 10

MIB = 1 << 20
MESH = pl.DeviceIdType.MESH


def _params(vmem_mib, n_axes=0):
    sem = ("arbitrary",) * n_axes if n_axes else None
    return pltpu.CompilerParams(dimension_semantics=sem, vmem_limit_bytes=vmem_mib * MIB)


def _in_hbm(*arrays):
    return [pltpu.with_memory_space_constraint(a, pltpu.HBM) for a in arrays]


def _mm(a, b):
    return jnp.dot(a, b, preferred_element_type=F32)


def _mm_nt(a, b):
    return lax.dot_general(a, b, (((1,), (1,)), ((), ())), preferred_element_type=F32)


def _mm_tn(a, b):
    return lax.dot_general(a, b, (((0,), (0,)), ((), ())), preferred_element_type=F32)


def _rms(x, g):
    rstd = lax.rsqrt(jnp.mean(x * x, axis=-1, keepdims=True) + EPS)
    xhat = x * rstd
    return xhat * g, xhat, rstd


def _rms_bwd(dy, xhat, rstd, g):
    gy = dy * g
    dx = rstd * (gy - xhat * jnp.mean(gy * xhat, axis=-1, keepdims=True))
    return dx, jnp.sum(dy * xhat, axis=0, keepdims=True)


def _gelu(z):
    t = jnp.tanh(0.7978845608028654 * (z + 0.044715 * z * z * z))
    return 0.5 * z * (1.0 + t), t


def _gelu_grad(z, t):
    return 0.5 * (1.0 + t) + 0.5 * z * (1.0 - t * t) * 0.7978845608028654 * (1.0 + 3.0 * 0.044715 * z * z)


def _softplus_neg(lam):
    x = -lam
    e = jnp.exp(-jnp.abs(x))
    u = 1.0 + e
    l1p = jnp.where(u == 1.0, e, jnp.log(u) * e / (u - 1.0))
    return jnp.maximum(x, 0.0) + l1p


def _one_minus_square(la, a):
    x = 2.0 * la
    series = -x * (1.0 + x * (0.5 + x * (1.0 / 6.0 + x * (1.0 / 24.0))))
    return jnp.where(x > -0.06, series, 1.0 - a * a)


def _sigmoid(x):
    return 0.5 * jnp.tanh(0.5 * x) + 0.5


def _down(v, d):
    return pltpu.roll(v, d, 0)


def _up(v, d):
    return pltpu.roll(v, v.shape[0] - d, 0)


def _slab_major(s, dtype=F32):
    return pltpu.HBM((N_SLAB, s, SLAB), dtype)


def _slabs_spec(tb):
    return pl.BlockSpec((N_SLAB, tb, SLAB), lambda i: (0, i, 0))


def _read_slabs(ref):
    return jnp.concatenate([ref[k] for k in range(N_SLAB)], axis=1)


def _write_slabs(ref, value):
    for k in range(N_SLAB):
        ref[k] = value[:, k * SLAB:(k + 1) * SLAB]


def _token_block(s, most=512):
    for rows in (most, 512, 256):
        if rows <= most and s % rows == 0 and s > rows:
            return rows
    return s


def _time_chunk(s):
    return 512 if s % 512 == 0 and s > 512 else 256 if s % 256 == 0 else s


def _fwd_in(x, ln1_g, w_in_t):
    s = x.shape[0]
    tb = _token_block(s)

    def body(x_ref, g_ref, w_ref, up_ref, ul_ref, ug_ref):
        n, _, _ = _rms(x_ref[...], g_ref[...])
        proj = _mm_nt(n.astype(BF16), w_ref[...])
        _write_slabs(up_ref, proj[:, :POOL_WIDTH])
        _write_slabs(ul_ref, proj[:, POOL_WIDTH:POOL_WIDTH + LRU_WIDTH])
        _write_slabs(ug_ref, proj[:, POOL_WIDTH + LRU_WIDTH:])

    out = _slab_major(s)
    return pl.pallas_call(
        body, name="fwd_in", grid=(s // tb,),
        in_specs=[pl.BlockSpec((tb, D_MODEL), lambda i: (i, 0)),
                  pl.BlockSpec((1, D_MODEL), lambda i: (0, 0)),
                  pl.BlockSpec((IN_WIDTH, D_MODEL), lambda i: (0, 0))],
        out_specs=[_slabs_spec(tb)] * 3,
        out_shape=[out, out, out],
        compiler_params=_params(40, 1),
    )(*_in_hbm(x, ln1_g, w_in_t))


def _pool_denominator(t0, row, window):
    return jnp.minimum((t0 + row + 1).astype(F32), window)


def _causal_window(ext, deeper):
    s = ext + _down(ext, 1)
    s = s + deeper[0] * _down(s, 2)
    s = s + deeper[1] * _down(s, 4)
    s = s + deeper[2] * _down(s, 8)
    return s[HALO:]


def _anticausal_window(ext, deeper, rows):
    s = ext + _up(ext, 1)
    s = s + deeper[0] * _up(s, 2)
    s = s + deeper[1] * _up(s, 4)
    s = s + deeper[2] * _up(s, 8)
    return s[:rows]


def _conv(taps, cw, cb):
    return cw[3] * taps[0] + cw[2] * taps[1] + cw[1] * taps[2] + cw[0] * taps[3] + cb


def _lru_gates(xc, wa, ba, wi, bi, sp):
    xb = xc.astype(BF16)
    r = _sigmoid(_mm(xb, wa) + ba)
    i = _sigmoid(_mm(xb, wi) + bi)
    la = (-LRU_C) * r * sp
    a = jnp.exp(la)
    mult = jnp.sqrt(jnp.maximum(_one_minus_square(la, a), 0.0))
    return xb, r, i, a, mult


SUBLANES = 8


def _scan_pads(a_pad, b_pad, rows, causal):
    spare = slice(0, SUBLANES) if causal else slice(rows, rows + SUBLANES)
    a_pad[spare, :] = jnp.ones((SUBLANES, SLAB), F32)
    b_pad[spare, :] = jnp.zeros((SUBLANES, SLAB), F32)


def _scan_causal(a, b, h_prev, a_pad, b_pad, rows):
    d = 1
    while d < min(SUBLANES, rows):
        a_pad[SUBLANES:, :] = a
        b_pad[SUBLANES:, :] = b
        b = a * b_pad[SUBLANES - d:SUBLANES - d + rows, :] + b
        a = a * a_pad[SUBLANES - d:SUBLANES - d + rows, :]
        d *= 2
    while d < rows:
        b = jnp.concatenate([b[:d], a[d:] * b[:-d] + b[d:]], axis=0)
        a = jnp.concatenate([a[:d], a[d:] * a[:-d]], axis=0)
        d *= 2
    return b + a * h_prev


def _scan_anticausal(a, b, l_next, a_pad, b_pad, rows):
    d = 1
    while d < min(SUBLANES, rows):
        a_pad[:rows, :] = a
        b_pad[:rows, :] = b
        b = a * b_pad[d:d + rows, :] + b
        a = a * a_pad[d:d + rows, :]
        d *= 2
    while d < rows:
        b = jnp.concatenate([a[:-d] * b[d:] + b[:-d], b[-d:]], axis=0)
        a = jnp.concatenate([a[:-d] * a[d:], a[-d:]], axis=0)
        d *= 2
    return b + a * l_next


def _slab_scalars():
    slab = pl.program_id(0)
    deeper = [jnp.where(slab > k, 1.0, 0.0).astype(F32) for k in range(N_SLAB - 1)]
    window = jnp.left_shift(jnp.int32(2), slab).astype(F32)
    inverse = jnp.where(slab == 0, 0.5, jnp.where(slab == 1, 0.25, jnp.where(slab == 2, 0.125, 0.0625))).astype(F32)
    return deeper, (window, inverse)


def _window_mean(total, t0, row, window, at_start):
    if at_start:
        return total / _pool_denominator(t0, row, window[0])
    return total * window[1]


def _slab_specs(s):
    seq = pl.BlockSpec((None, s, SLAB), lambda k: (k, 0, 0))
    mat = pl.BlockSpec((1, SLAB, SLAB), lambda k: (k, 0, 0))
    vec = pl.BlockSpec((1, SLAB), lambda k: (0, k))
    taps = pl.BlockSpec((CONV_WIDTH, SLAB), lambda k: (0, k))
    return seq, mat, vec, taps


def _mixer_fwd(u_pool, u_lru, u_gate, pool_w, pool_scale, conv_w, conv_b, wa_bd, b_a, wi_bd, b_i, lam):
    s = u_pool.shape[1]
    tc = _time_chunk(s)
    n_chunks = s // tc

    def body(up_ref, ul_ref, ug_ref, pw_ref, ps_ref, cw_ref, cb_ref, wa_ref, ba_ref, wi_ref, bi_ref, lam_ref,
             yp_ref, h_ref, yl_ref, shift_ref, a_pad, b_pad):
        _scan_pads(a_pad, b_pad, tc, causal=True)
        deeper, window = _slab_scalars()
        pw = pw_ref[0].astype(BF16)
        wa = wa_ref[0].astype(BF16)
        wi = wi_ref[0].astype(BF16)
        ps, cb, ba, bi = ps_ref[...], cb_ref[...], ba_ref[...], bi_ref[...]
        cw = [cw_ref[k:k + 1, :] for k in range(CONV_WIDTH)]
        sp = _softplus_neg(lam_ref[...])
        row = lax.broadcasted_iota(jnp.int32, (tc, SLAB), 0)

        def chunk(t0, ext_p, ext_l, h_prev, at_start=False):
            rows = pl.ds(t0, tc)
            d = _window_mean(_causal_window(ext_p, deeper), t0, row, window, at_start) - ext_p[HALO:]
            yp_ref[rows, :] = _mm(d.astype(BF16), pw) * ps
            shift_ref[...] = ext_l
            xc = _conv([shift_ref[HALO - k:HALO - k + tc, :] for k in range(CONV_WIDTH)], cw, cb)
            _, _, i, a, mult = _lru_gates(xc, wa, ba, wi, bi, sp)
            h = _scan_causal(a, mult * (i * xc), h_prev, a_pad, b_pad, tc)
            h_ref[rows, :] = h
            yl_ref[rows, :] = h * _gelu(ug_ref[rows, :])[0]
            return h[tc - 1:tc, :]

        pad = jnp.zeros((HALO, SLAB), F32)
        h0 = chunk(0, jnp.concatenate([pad, up_ref[pl.ds(0, tc), :]], axis=0),
                   jnp.concatenate([pad, ul_ref[pl.ds(0, tc), :]], axis=0), jnp.zeros((1, SLAB), F32), at_start=True)

        def step(c, h_prev):
            t0 = pl.multiple_of(c * tc, tc)
            ext = pl.ds(pl.multiple_of(c * tc - HALO, HALO), tc + HALO)
            return chunk(t0, up_ref[ext, :], ul_ref[ext, :], h_prev)

        lax.fori_loop(1, n_chunks, step, h0)

    seq, mat, vec, taps = _slab_specs(s)
    out = _slab_major(s)
    return pl.pallas_call(
        body, name="mixer_fwd", grid=(N_SLAB,),
        in_specs=[seq, seq, seq, mat, vec, taps, vec, mat, vec, mat, vec, vec],
        out_specs=[seq, seq, seq], out_shape=[out, out, out],
        scratch_shapes=[pltpu.VMEM((tc + HALO, SLAB), F32), pltpu.VMEM((tc + SUBLANES, SLAB), F32),
                        pltpu.VMEM((tc + SUBLANES, SLAB), F32)],
        compiler_params=_params(48, 1),
    )(*_in_hbm(u_pool, u_lru, u_gate, pool_w, pool_scale, conv_w, conv_b, wa_bd, b_a, wi_bd, b_i, lam))


def _fwd_out(x, y_pool, y_lru, gn_pool_g, gn_lru_g, w_out, ln2_g):
    s = x.shape[0]
    tb = _token_block(s)

    def body(x_ref, yp_ref, yl_ref, gp_ref, gl_ref, w_ref, g2_ref, h1_ref, n2_ref):
        mp, _, _ = _rms(_read_slabs(yp_ref), gp_ref[...])
        ml, _, _ = _rms(_read_slabs(yl_ref), gl_ref[...])
        h1 = x_ref[...] + _mm(mp.astype(BF16), w_ref[:POOL_WIDTH, :]) + _mm(ml.astype(BF16), w_ref[POOL_WIDTH:, :])
        h1_ref[...] = h1
        n2_ref[...] = _rms(h1, g2_ref[...])[0].astype(BF16)

    row = pl.BlockSpec((tb, D_MODEL), lambda i: (i, 0))
    half = _slabs_spec(tb)
    return pl.pallas_call(
        body, name="fwd_out", grid=(s // tb,),
        in_specs=[row, half, half, pl.BlockSpec((1, 512), lambda i: (0, 0)), pl.BlockSpec((1, 512), lambda i: (0, 0)),
                  pl.BlockSpec((D_MODEL, D_MODEL), lambda i: (0, 0)), pl.BlockSpec((1, D_MODEL), lambda i: (0, 0))],
        out_specs=[row, row],
        out_shape=[pltpu.HBM((s, D_MODEL), F32), pltpu.HBM((s, D_MODEL), BF16)],
        compiler_params=_params(40, 1),
    )(*_in_hbm(x, y_pool, y_lru, gn_pool_g, gn_lru_g, w_out, ln2_g))


def _ffn_fwd(h1, n2, target, lnf_g, w_gate, w_up, w_down):
    s = h1.shape[0]
    tb = 512
    sub = 256
    n_ff = D_FF // FF_CHUNK

    def body(h1_ref, n2_ref, t_ref, gf_ref, wg_hbm, wu_hbm, wd_hbm,
             g_ref, u_ref, dh_ref, dhb_ref, dgf_ref, sq_ref, wg, wu, wd, sem):
        @pl.when(pl.program_id(0) == 0)
        def _():
            loads = [pltpu.make_async_copy(src, dst, sem.at[k])
                     for k, (src, dst) in enumerate(((wg_hbm, wg), (wu_hbm, wu), (wd_hbm, wd)))]
            for cp in loads:
                cp.start()
            for cp in loads:
                cp.wait()
            dgf_ref[...] = jnp.zeros_like(dgf_ref)
            sq_ref[...] = jnp.zeros_like(sq_ref)

        n2v = n2_ref[...]
        acc = jnp.zeros((tb, D_MODEL), F32)
        for c in range(n_ff):
            cols = slice(c * FF_CHUNK, (c + 1) * FF_CHUNK)
            g = _mm_nt(n2v, wg[cols, :])
            u = _mm_nt(n2v, wu[cols, :])
            g_ref[:, cols] = g.astype(BF16)
            u_ref[:, cols] = u.astype(BF16)
            act = g * jax.nn.sigmoid(g) * u
            acc = acc + _mm(act.astype(BF16), wd[cols, :])
        gf = gf_ref[...]
        for r in range(tb // sub):
            rows = slice(r * sub, (r + 1) * sub)
            y, xhat, rstd = _rms(h1_ref[rows, :] + acc[rows, :], gf)
            err = y - t_ref[rows, :]
            sq_ref[...] += jnp.sum(err * err, axis=0, keepdims=True)
            dh2, dgf = _rms_bwd(err * (1.0 / D_MODEL), xhat, rstd, gf)
            dgf_ref[...] += dgf
            dh_ref[rows, :] = dh2
            dhb_ref[rows, :] = dh2.astype(BF16)

    row = pl.BlockSpec((tb, D_MODEL), lambda i: (i, 0))
    ff = pl.BlockSpec((tb, D_FF), lambda i: (i, 0))
    vec = pl.BlockSpec((1, D_MODEL), lambda i: (0, 0))
    anyspace = pl.BlockSpec(memory_space=pl.ANY)
    return pl.pallas_call(
        body, name="ffn_fwd", grid=(s // tb,),
        in_specs=[row, row, row, vec, anyspace, anyspace, anyspace],
        out_specs=[ff, ff, row, row, vec, vec],
        out_shape=[pltpu.HBM((s, D_FF), BF16), pltpu.HBM((s, D_FF), BF16),
                   pltpu.HBM((s, D_MODEL), F32), pltpu.HBM((s, D_MODEL), BF16),
                   jax.ShapeDtypeStruct((1, D_MODEL), F32), jax.ShapeDtypeStruct((1, D_MODEL), F32)],
        scratch_shapes=[pltpu.VMEM((D_FF, D_MODEL), BF16), pltpu.VMEM((D_FF, D_MODEL), BF16),
                        pltpu.VMEM((D_FF, D_MODEL), BF16), pltpu.SemaphoreType.DMA((3,))],
        compiler_params=_params(60, 1),
    )(*_in_hbm(h1, n2, target, lnf_g, w_gate, w_up, w_down))


def _ffn_bwd(n2, dh2b, g, u, w_gate_t, w_up_t, w_down):
    s = n2.shape[0]
    tb = min(1024, s)
    n_ff = D_FF // FF_CHUNK
    n_tb = s // tb

    def body(n2_ref, dh_ref, g_ref, u_ref, wg_ref, wu_ref, wd_ref, dwg_ref, dwu_ref, dwd_ref, dn2_ref,
             dn2_acc, acc_g, acc_u, acc_d):
        j = pl.program_id(0)
        t = pl.program_id(1)
        rows = pl.ds(pl.multiple_of(t * tb, tb), tb)

        @pl.when(t == 0)
        def _():
            acc_g[...] = jnp.zeros_like(acc_g)
            acc_u[...] = jnp.zeros_like(acc_u)
            acc_d[...] = jnp.zeros_like(acc_d)

        @pl.when(j == 0)
        def _():
            dn2_acc[rows, :] = jnp.zeros((tb, D_MODEL), F32)

        n2v = n2_ref[...]
        dh = dh_ref[...]
        gv = g_ref[...].astype(F32)
        uv = u_ref[...].astype(F32)
        sg = jax.nn.sigmoid(gv)
        silu = gv * sg
        dact = _mm_nt(dh, wd_ref[...])
        dub = (dact * silu).astype(BF16)
        dgb = (dact * uv * (sg * (1.0 + gv * (1.0 - sg)))).astype(BF16)
        acc_d[...] += _mm_tn((silu * uv).astype(BF16), dh)
        acc_g[...] += _mm_tn(dgb, n2v)
        acc_u[...] += _mm_tn(dub, n2v)
        dn2_acc[rows, :] += _mm(jnp.concatenate([dgb, dub], axis=1),
                                jnp.concatenate([wg_ref[...], wu_ref[...]], axis=0))

        @pl.when(t == n_tb - 1)
        def _():
            dwg_ref[...] = acc_g[...].astype(BF16)
            dwu_ref[...] = acc_u[...].astype(BF16)
            dwd_ref[...] = acc_d[...].astype(BF16)

        @pl.when(j == n_ff - 1)
        def _():
            dn2_ref[...] = dn2_acc[rows, :]

    row = pl.BlockSpec((tb, D_MODEL), lambda j, t: (t, 0))
    act = pl.BlockSpec((tb, FF_CHUNK), lambda j, t: (t, j))
    w_row = pl.BlockSpec((FF_CHUNK, D_MODEL), lambda j, t: (j, 0))
    last = pl.BlockSpec((tb, D_MODEL), lambda j, t: (jnp.where(j == n_ff - 1, t, 0), 0))
    grad = pltpu.HBM((D_FF, D_MODEL), BF16)
    chunk_acc = pltpu.VMEM((FF_CHUNK, D_MODEL), F32)
    return pl.pallas_call(
        body, name="ffn_bwd", grid=(n_ff, n_tb),
        in_specs=[row, row, act, act, w_row, w_row, w_row],
        out_specs=[w_row, w_row, w_row, last],
        out_shape=[grad, grad, grad, pltpu.HBM((s, D_MODEL), F32)],
        scratch_shapes=[pltpu.VMEM((s, D_MODEL), F32), chunk_acc, chunk_acc, chunk_acc],
        compiler_params=_params(56, 2),
    )(*_in_hbm(n2, dh2b, g, u, w_gate_t, w_up_t, w_down))


def _bwd_out(dn2, dh2, h1, y_pool, y_lru, gn_pool_g, gn_lru_g, w_out, ln2_g, after):
    s = h1.shape[0]
    tb = _token_block(s)

    def body(dn2_ref, dh2_ref, h1_ref, yp_ref, yl_ref, gp_ref, gl_ref, w_ref, g2_ref, _after,
             dh1_ref, dyp_ref, dyl_ref, dwb_ref, dg2_ref, dgp_ref, dgl_ref, dw_ref):
        @pl.when(pl.program_id(0) == 0)
        def _():
            dw_ref[...] = jnp.zeros_like(dw_ref)
            dg2_ref[...] = jnp.zeros_like(dg2_ref)
            dgp_ref[...] = jnp.zeros_like(dgp_ref)
            dgl_ref[...] = jnp.zeros_like(dgl_ref)

        g2 = g2_ref[...]
        _, xhat2, rstd2 = _rms(h1_ref[...], g2)
        dres, dg2 = _rms_bwd(dn2_ref[...], xhat2, rstd2, g2)
        dg2_ref[...] += dg2
        dh1 = dh2_ref[...] + dres
        dh1_ref[...] = dh1
        dh1b = dh1.astype(BF16)
        gp, gl = gp_ref[...], gl_ref[...]
        mp, xhat_p, rstd_p = _rms(_read_slabs(yp_ref), gp)
        ml, xhat_l, rstd_l = _rms(_read_slabs(yl_ref), gl)
        dw_ref[:POOL_WIDTH, :] += _mm_tn(mp.astype(BF16), dh1b)
        dw_ref[POOL_WIDTH:, :] += _mm_tn(ml.astype(BF16), dh1b)
        dyp, dgp = _rms_bwd(_mm_nt(dh1b, w_ref[:POOL_WIDTH, :]), xhat_p, rstd_p, gp)
        dyl, dgl = _rms_bwd(_mm_nt(dh1b, w_ref[POOL_WIDTH:, :]), xhat_l, rstd_l, gl)
        _write_slabs(dyp_ref, dyp)
        _write_slabs(dyl_ref, dyl)
        dgp_ref[...] += dgp
        dgl_ref[...] += dgl

        @pl.when(pl.program_id(0) == s // tb - 1)
        def _():
            dwb_ref[...] = dw_ref[...].astype(BF16)

    row = pl.BlockSpec((tb, D_MODEL), lambda i: (i, 0))
    half = _slabs_spec(tb)
    vec = pl.BlockSpec((1, D_MODEL), lambda i: (0, 0))
    hvec = pl.BlockSpec((1, 512), lambda i: (0, 0))
    mat = pl.BlockSpec((D_MODEL, D_MODEL), lambda i: (0, 0))
    return pl.pallas_call(
        body, name="bwd_out", grid=(s // tb,),
        in_specs=[row, row, row, half, half, hvec, hvec, mat, vec, pl.BlockSpec(memory_space=pl.ANY)],
        out_specs=[row, half, half, mat, vec, hvec, hvec],
        out_shape=[pltpu.HBM((s, D_MODEL), F32), _slab_major(s),
                   _slab_major(s), pltpu.HBM((D_MODEL, D_MODEL), BF16),
                   jax.ShapeDtypeStruct((1, D_MODEL), F32), jax.ShapeDtypeStruct((1, 512), F32),
                   jax.ShapeDtypeStruct((1, 512), F32)],
        scratch_shapes=[pltpu.VMEM((D_MODEL, D_MODEL), F32)],
        compiler_params=_params(48, 1),
    )(*_in_hbm(dn2, dh2, h1, y_pool, y_lru, gn_pool_g, gn_lru_g, w_out, ln2_g), after)


def _mixer_bwd(u_pool, u_lru, u_gate, h, dy_pool, dy_lru,
               pool_w, pool_scale, conv_w, conv_b, wa_bd, b_a, wi_bd, b_i, lam, after):
    s = u_pool.shape[1]
    tc = _time_chunk(s)
    n_chunks = s // tc

    def body(up_ref, ul_ref, ug_ref, h_ref, dyp_ref, dyl_ref,
             pw_ref, ps_ref, cw_ref, cb_ref, wa_ref, ba_ref, wi_ref, bi_ref, lam_ref, _after,
             dup_ref, dul_ref, dug_ref, dpw_ref, dps_ref, dcw_ref, dcb_ref, dwa_ref, dba_ref, dwi_ref, dbi_ref, dlam_ref,
             a_pad, b_pad, u_shift, h_shift, a_shift, d_shift):
        _scan_pads(a_pad, b_pad, tc, causal=False)
        deeper, window = _slab_scalars()
        pw = pw_ref[0].astype(BF16)
        wa = wa_ref[0].astype(BF16)
        wi = wi_ref[0].astype(BF16)
        ps, cb, ba, bi = ps_ref[...], cb_ref[...], ba_ref[...], bi_ref[...]
        cw = [cw_ref[k:k + 1, :] for k in range(CONV_WIDTH)]
        lam_v = lam_ref[...]
        sp = _softplus_neg(lam_v)
        row = lax.broadcasted_iota(jnp.int32, (tc, SLAB), 0)
        for ref in (dpw_ref, dps_ref, dcw_ref, dcb_ref, dwa_ref, dba_ref, dwi_ref, dbi_ref, dlam_ref):
            ref[...] = jnp.zeros_like(ref)

        def chunk(t0, ext_p, ext_l, ext_h, carry, at_start=False):
            l_next, a_next, dxc_next, ddn_next = carry
            rows = pl.ds(t0, tc)
            u_shift[...] = ext_l
            taps = [u_shift[HALO - k:HALO - k + tc, :] for k in range(CONV_WIDTH)]
            xc = _conv(taps, cw, cb)
            xb, r, i, a, mult = _lru_gates(xc, wa, ba, wi, bi, sp)
            hv = ext_h[HALO:]
            h_shift[...] = ext_h
            h_before = h_shift[HALO - 1:HALO - 1 + tc, :]
            ug = ug_ref[rows, :]
            dyl = dyl_ref[rows, :]
            gel, th = _gelu(ug)
            dug_ref[rows, :] = (dyl * hv * _gelu_grad(ug, th)).astype(BF16)
            a_shift[:tc, :] = a
            a_shift[tc:, :] = jnp.broadcast_to(a_next, (SUBLANES, SLAB))
            a_after = a_shift[1:1 + tc, :]
            l = _scan_anticausal(a_after, dyl * gel, l_next, a_pad, b_pad, tc)
            dmult = l * (i * xc)
            di = l * mult * xc
            dxc = l * mult * i
            dla = (l * h_before) * a - jnp.where(mult > 0.0, dmult * (a * a) / mult, 0.0)
            dlam_ref[...] += jnp.sum(dla * r, axis=0, keepdims=True)
            dpa = (dla * ((-LRU_C) * sp)) * (r * (1.0 - r))
            dpi = di * (i * (1.0 - i))
            dpab = dpa.astype(BF16)
            dpib = dpi.astype(BF16)
            dwa_ref[0] += _mm_tn(xb, dpab)
            dwi_ref[0] += _mm_tn(xb, dpib)
            dba_ref[...] += jnp.sum(dpa, axis=0, keepdims=True)
            dbi_ref[...] += jnp.sum(dpi, axis=0, keepdims=True)
            dxc = dxc + _mm_nt(dpab, wa) + _mm_nt(dpib, wi)
            d_shift[:tc, :] = dxc
            d_shift[tc:, :] = dxc_next
            dul_ref[rows, :] = (cw[3] * dxc + cw[2] * d_shift[1:1 + tc, :]
                                + cw[1] * d_shift[2:2 + tc, :] + cw[0] * d_shift[3:3 + tc, :]).astype(BF16)
            for k in range(CONV_WIDTH):
                dcw_ref[k:k + 1, :] += jnp.sum(dxc * taps[CONV_WIDTH - 1 - k], axis=0, keepdims=True)
            dcb_ref[...] += jnp.sum(dxc, axis=0, keepdims=True)
            db = (_window_mean(_causal_window(ext_p, deeper), t0, row, window, at_start) - ext_p[HALO:]).astype(BF16)
            dyp = dyp_ref[rows, :]
            dps_ref[...] += jnp.sum(dyp * _mm(db, pw), axis=0, keepdims=True)
            dys = (dyp * ps).astype(BF16)
            dpw_ref[0] += _mm_tn(db, dys)
            dd = _mm_nt(dys, pw)
            ddn = _window_mean(dd, t0, row, window, at_start)
            ext_q = jnp.concatenate([ddn, ddn_next], axis=0)
            dup_ref[rows, :] = (_anticausal_window(ext_q, deeper, tc) - dd).astype(BF16)
            return l[0:1, :], a[0:1, :], dxc[0:8, :], ddn[0:HALO, :]

        def step(k, carry):
            c = n_chunks - 1 - k
            t0 = pl.multiple_of(c * tc, tc)
            ext = pl.ds(pl.multiple_of(c * tc - HALO, HALO), tc + HALO)
            return chunk(t0, up_ref[ext, :], ul_ref[ext, :], h_ref[ext, :], carry)

        carry = (jnp.zeros((1, SLAB), F32), jnp.zeros((1, SLAB), F32),
                 jnp.zeros((8, SLAB), F32), jnp.zeros((HALO, SLAB), F32))
        carry = lax.fori_loop(0, n_chunks - 1, step, carry)
        pad = jnp.zeros((HALO, SLAB), F32)
        first = pl.ds(0, tc)
        chunk(0, jnp.concatenate([pad, up_ref[first, :]], axis=0), jnp.concatenate([pad, ul_ref[first, :]], axis=0),
              jnp.concatenate([pad, h_ref[first, :]], axis=0), carry, at_start=True)
        dlam_ref[...] = dlam_ref[...] * (LRU_C * jax.nn.sigmoid(-lam_v))

    seq, mat, vec, taps = _slab_specs(s)
    grad = _slab_major(s, BF16)
    mats = jax.ShapeDtypeStruct((N_SLAB, SLAB, SLAB), F32)
    vecs = jax.ShapeDtypeStruct((1, 512), F32)
    return pl.pallas_call(
        body, name="mixer_bwd", grid=(N_SLAB,),
        in_specs=[seq] * 6 + [mat, vec, taps, vec, mat, vec, mat, vec, vec, pl.BlockSpec(memory_space=pl.ANY)],
        out_specs=[seq, seq, seq, mat, vec, taps, vec, mat, vec, mat, vec, vec],
        out_shape=[grad, grad, grad, mats, vecs, jax.ShapeDtypeStruct((CONV_WIDTH, 512), F32), vecs,
                   mats, vecs, mats, vecs, vecs],
        scratch_shapes=[pltpu.VMEM((tc + SUBLANES, SLAB), F32), pltpu.VMEM((tc + SUBLANES, SLAB), F32),
                        pltpu.VMEM((tc + HALO, SLAB), F32), pltpu.VMEM((tc + HALO, SLAB), F32),
                        pltpu.VMEM((tc + SUBLANES, SLAB), F32), pltpu.VMEM((tc + SUBLANES, SLAB), F32)],
        compiler_params=_params(56, 1),
    )(*_in_hbm(u_pool, u_lru, u_gate, h, dy_pool, dy_lru, pool_w, pool_scale, conv_w, conv_b, wa_bd, b_a, wi_bd,
               b_i, lam), after)


def _bwd_in(x, dh1, du_pool, du_lru, du_gate, ln1_g, w_in_t, after):
    s = x.shape[0]
    tb = _token_block(s)

    def body(x_ref, dh1_ref, dup_ref, dul_ref, dug_ref, g_ref, w_ref, _after, dx_ref, dwb_ref, dg_ref, dw_ref):
        @pl.when(pl.program_id(0) == 0)
        def _():
            dw_ref[...] = jnp.zeros_like(dw_ref)
            dg_ref[...] = jnp.zeros_like(dg_ref)

        g1 = g_ref[...]
        n, xhat, rstd = _rms(x_ref[...], g1)
        nb = n.astype(BF16)
        db = jnp.concatenate([_read_slabs(ref) for ref in (dup_ref, dul_ref, dug_ref)], axis=1)
        dw_ref[...] += _mm_tn(db, nb)
        dx, dg1 = _rms_bwd(_mm(db, w_ref[...]), xhat, rstd, g1)
        dx_ref[...] = dh1_ref[...] + dx
        dg_ref[...] += dg1

        @pl.when(pl.program_id(0) == s // tb - 1)
        def _():
            dwb_ref[...] = dw_ref[...].astype(BF16)

    row = pl.BlockSpec((tb, D_MODEL), lambda i: (i, 0))
    half = _slabs_spec(tb)
    vec = pl.BlockSpec((1, D_MODEL), lambda i: (0, 0))
    mat = pl.BlockSpec((IN_WIDTH, D_MODEL), lambda i: (0, 0))
    return pl.pallas_call(
        body, name="bwd_in", grid=(s // tb,),
        in_specs=[row, row, half, half, half, vec, mat, pl.BlockSpec(memory_space=pl.ANY)],
        out_specs=[row, mat, vec],
        out_shape=[pltpu.HBM((s, D_MODEL), F32), pltpu.HBM((IN_WIDTH, D_MODEL), BF16),
                   jax.ShapeDtypeStruct((1, D_MODEL), F32)],
        scratch_shapes=[pltpu.VMEM((IN_WIDTH, D_MODEL), F32)],
        compiler_params=_params(48, 1),
    )(*_in_hbm(x, dh1, du_pool, du_lru, du_gate, ln1_g, w_in_t), after)


def _mesh_position():
    x, y, c = lax.axis_index("x"), lax.axis_index("y"), lax.axis_index("c")
    return x, y, c, 4 * x + 2 * y + c


def _peer(x, y, c, p):
    px = 1 - x if p & 4 else x
    py = 1 - y if p & 2 else y
    pc = 1 - c if p & 1 else c
    return (px, py, pc), 4 * px + 2 * py + pc


HBM_SPEC = pl.BlockSpec(memory_space=pltpu.HBM)
SEM_SPEC = pl.BlockSpec(memory_space=pltpu.SEMAPHORE)
DATAFLOW = pltpu.SideEffectType.DATAFLOW_SIDE_EFFECTING


class Copy(NamedTuple):
    src: int
    src_at: Any
    dst: int
    dst_at: Any
    peer: int
    group: int
    slot: int


SIBLING = (1,)
SAME_CORE = (2, 4, 6)
EVERYONE = tuple(range(1, N_DEV))
MYSELF = (0,)


def _same(index):
    return index


def _chip(index):
    return jnp.right_shift(index, 1)


def _fan_out(srcs, lands, peers, group):
    return [Copy(s, None, d, _same, p, group, N_DEV * i + p) for i, (s, d) in enumerate(zip(srcs, lands)) for p in peers]


def _scatter(stacks, lands, peers):
    return [Copy(s, lambda me, p=p: jnp.bitwise_xor(me, p), d, _same, p, 0, 0)
            for s, d in zip(stacks, lands) for p in peers]


def _numbered(copies, group=0):
    return [cp._replace(group=group, slot=i) for i, cp in enumerate(copies)]


def _relay(lands, peers):
    return [Copy(b, lambda s, q=q: jnp.bitwise_xor(s, q), b, lambda s, q=q: jnp.bitwise_xor(s, q), 1, 0, N_DEV * i + q)
            for i, b in enumerate(lands) for q in peers]


def _to_sibling(stacks, lands):
    return [Copy(s, lambda me, k=k: 2 * k + 1 - jnp.bitwise_and(me, 1), d, lambda me, k=k: k, 1, 0, 4 * i + k)
            for i, (s, d) in enumerate(zip(stacks, lands)) for k in range(N_DEV // 2)]


def _to_chips(sums, lands):
    return [Copy(s, lambda me, p=p: jnp.bitwise_xor(_chip(me), p // 2), d, _chip, p, 0, 4 * i + p // 2)
            for i, (s, d) in enumerate(zip(sums, lands)) for p in SAME_CORE]


def _comm_call(name, bufs, wait=None, start=None, after=()):
    nb = len(bufs)
    waits = [wait] if isinstance(wait, tuple) else list(wait or [])
    slots = list(start[1]) if start else []
    n_out_sem = 2 * len(slots)
    after = [a for a in (after if isinstance(after, (list, tuple)) else [after]) if a is not None]

    def body(*refs):
        b = refs[:nb]
        at = nb + 2 * len(waits) + len(after)
        out_sems = refs[at:at + n_out_sem]
        token = refs[at + n_out_sem + nb]
        x, y, c, me = _mesh_position()

        def part(i, row_of, sender):
            return b[i] if row_of is None else b[i].at[row_of(sender)]

        for k, (_, copies) in enumerate(waits):
            w_send, w_recv = refs[nb + 2 * k], refs[nb + 2 * k + 1]
            for cp in copies:
                peer, peer_index = _peer(x, y, c, cp.peer)
                arrival = pltpu.make_async_remote_copy(part(cp.src, cp.src_at, me), part(cp.dst, cp.dst_at, peer_index),
                                                       w_send.at[cp.slot], w_recv.at[cp.slot],
                                                       device_id=peer, device_id_type=MESH)
                arrival.wait_send()
                arrival.wait_recv()
        if start:
            for cp in start[0]:
                peer, _ = _peer(x, y, c, cp.peer)
                pltpu.make_async_remote_copy(part(cp.src, cp.src_at, me), part(cp.dst, cp.dst_at, me),
                                             out_sems[2 * cp.group].at[cp.slot], out_sems[2 * cp.group + 1].at[cp.slot],
                                             device_id=peer, device_id_type=MESH).start()
        token[...] = jnp.zeros_like(token)

    sem_shapes = []
    for n_slots in slots:
        sem_shapes += [pltpu.SemaphoreType.DMA((n_slots,))] * 2
    operands = [pltpu.with_memory_space_constraint(a, pltpu.HBM) for a in bufs]
    in_specs = [HBM_SPEC] * nb
    for sems, _ in waits:
        operands += list(sems)
        in_specs += [SEM_SPEC, SEM_SPEC]
    operands += after
    in_specs += [pl.BlockSpec(memory_space=pl.ANY)] * len(after)
    outs = pl.pallas_call(
        body, name=name, in_specs=in_specs,
        out_specs=[SEM_SPEC] * n_out_sem + [HBM_SPEC] * nb + [pl.BlockSpec(memory_space=pltpu.VMEM)],
        out_shape=sem_shapes + [pltpu.HBM(a.shape, a.dtype) for a in bufs] + [jax.ShapeDtypeStruct((8, SLAB), F32)],
        input_output_aliases={i: n_out_sem + i for i in range(nb)},
        compiler_params=pltpu.CompilerParams(has_side_effects=DATAFLOW),
    )(*operands)
    sems = [(outs[2 * k], outs[2 * k + 1]) for k in range(len(slots))]
    return sems, list(outs[n_out_sem:n_out_sem + nb]), outs[-1]


def _pair_sum(stacks, lands, place):
    n = len(stacks)

    def body(place_ref, *refs):
        k = pl.program_id(0)
        for m in range(n):
            mine, theirs, out, land = refs[m], refs[n + m], refs[2 * n + m], refs[3 * n + m]
            total = (mine[0, 0].astype(F32) + theirs[0].astype(F32)).astype(out.dtype)
            out[0] = total

            @pl.when(k == place_ref[1])
            def _():
                land[0] = total

    in_specs = [pl.BlockSpec((1, 1) + a.shape[1:], lambda k, place_ref: (k, place_ref[0], 0, 0)) for a in stacks]
    in_specs += [pl.BlockSpec((1,) + a.shape[1:], lambda k, place_ref: (k, 0, 0)) for a in lands]
    out_specs = [pl.BlockSpec((1,) + a.shape[1:], lambda k, place_ref: (k, 0, 0)) for a in lands]
    out_specs += [pl.BlockSpec((1,) + a.shape[1:], lambda k, place_ref: (place_ref[1], 0, 0)) for a in lands]
    outs = pl.pallas_call(
        body, name="pair_sum_" + "_".join(str(a.shape[1]) for a in stacks),
        grid_spec=pltpu.PrefetchScalarGridSpec(num_scalar_prefetch=1, grid=(N_DEV // 2,), in_specs=in_specs,
                                               out_specs=out_specs),
        out_shape=[pltpu.HBM(a.shape, a.dtype) for a in lands] * 2,
        compiler_params=_params(40, 1),
    )(place, *_in_hbm(*[a.reshape((N_DEV // 2, 2) + a.shape[1:]) for a in stacks], *lands))
    return list(outs[:n]), list(outs[n:])


def _reduce_adam(parts, w, m, v, name):
    rows, cols = w.shape
    n_parts = parts.shape[0]
    rb = rows
    for cand in (176, 48, 32):
        if rows % cand == 0 and rows > cand:
            rb = cand
            break

    def body(p_ref, w_ref, m_ref, v_ref, g_out, d_out, m_out, v_out):
        g = p_ref[0].astype(F32)
        for j in range(1, n_parts):
            g = g + p_ref[j].astype(F32)
        g_out[...] = g
        d_out[...], m_out[...], v_out[...] = _adam(g, w_ref[...], m_ref[...], v_ref[...])

    blk = pl.BlockSpec((rb, cols), lambda i: (i, 0))
    out = jax.ShapeDtypeStruct((rows, cols), F32)
    return pl.pallas_call(
        body, name=name, grid=(rows // rb,),
        in_specs=[pl.BlockSpec((n_parts, rb, cols), lambda i: (0, i, 0)), blk, blk, blk],
        out_specs=[blk] * 4, out_shape=[out] * 4,
        compiler_params=_params(40, 1),
    )(*_in_hbm(parts, w, m, v))


def _cols_from_stack(stack):
    n, r, c = stack.shape
    return jnp.transpose(stack, (1, 0, 2)).reshape(r, n * c)


def _block_diag(w):
    z = jnp.zeros((N_SLAB, 64, 64), w.dtype)
    pairs = w.reshape(N_SLAB, 2, 64, 64)
    top = jnp.concatenate([pairs[:, 0], z], axis=2)
    bottom = jnp.concatenate([z, pairs[:, 1]], axis=2)
    return jnp.concatenate([top, bottom], axis=1)


def _adam(g, w, m, v):
    m_new = ADAM_B1 * m + (1.0 - ADAM_B1) * g
    v_new = ADAM_B2 * v + (1.0 - ADAM_B2) * (g * g)
    m_hat = m_new / (1.0 - ADAM_B1 ** ADAM_STEP)
    v_hat = v_new / (1.0 - ADAM_B2 ** ADAM_STEP)
    return (-ADAM_LR) * (m_hat / (jnp.sqrt(v_hat) + ADAM_EPS) + ADAM_WD * w), m_new, v_new


WIDE = ("ln2_g", "lnf_g")
HALF = ("pool_scale", "conv_b", "b_a", "b_i", "lam", "gn_pool_g", "gn_lru_g")
VECTORS = [(k, D_MODEL) for k in WIDE] + [(k, 512) for k in HALF]
VECTOR_ROWS = sum(width // SLAB for _, width in VECTORS)
LOSS_ROW = -(-VECTOR_ROWS // 8) * 8
CONV_AT = LOSS_ROW + 8
CONV_LANES = LRU_WIDTH // SLAB
PACK_F_ROWS = CONV_AT + CONV_WIDTH * CONV_LANES
MATRIX_ROWS = N_SLAB * SLAB
HEAD = SLAB // 2
GATE_ROWS = N_SLAB * HEAD
PACK_B_ROWS = MATRIX_ROWS + 2 * GATE_ROWS


def _pack_small(vectors, pool_g, wa_g, wi_g, conv, sq):
    n_vec = len(vectors)

    def body(*refs):
        vec = refs[:n_vec]
        pw_ref, wa_ref, wi_ref, cw_ref, sq_ref, out, out_b = refs[n_vec:]
        out[...] = jnp.zeros_like(out)
        row = 0
        for ref, (_, width) in zip(vec, VECTORS):
            for k in range(width // SLAB):
                out[row:row + 1, :] = ref[:, k * SLAB:(k + 1) * SLAB]
                row += 1
        for tap in range(CONV_WIDTH):
            for k in range(CONV_LANES):
                at = CONV_AT + tap * CONV_LANES + k
                out[at:at + 1, :] = cw_ref[tap:tap + 1, k * SLAB:(k + 1) * SLAB]
        total = sq_ref[:, 0:SLAB]
        for k in range(1, D_MODEL // SLAB):
            total = total + sq_ref[:, k * SLAB:(k + 1) * SLAB]
        out[LOSS_ROW:LOSS_ROW + 1, :] = total
        left = lax.broadcasted_iota(jnp.int32, (HEAD, SLAB), 1) < HEAD
        for s in range(N_SLAB):
            out_b[s * SLAB:(s + 1) * SLAB, :] = pw_ref[s].astype(BF16)
            for i, ref in enumerate((wa_ref, wi_ref)):
                at = MATRIX_ROWS + i * GATE_ROWS + s * HEAD
                out_b[at:at + HEAD, :] = jnp.where(left, ref[s, 0:HEAD, :], ref[s, HEAD:SLAB, :]).astype(BF16)

    return pl.pallas_call(
        body, name="pack_small",
        out_shape=[jax.ShapeDtypeStruct((PACK_F_ROWS, SLAB), F32), jax.ShapeDtypeStruct((PACK_B_ROWS, SLAB), BF16)],
    )(*vectors, pool_g, wa_g, wi_g, conv, sq)


def _small_reduce_adam(parts, parts_b, vec_w, vec_m, vec_v, pool_wmv):
    n_vec = len(VECTORS)
    n_parts = parts.shape[0]

    def body(*refs):
        p_ref, pb_ref = refs[0], refs[1]
        w_refs, m_refs, v_refs = (refs[2 + k * n_vec:2 + (k + 1) * n_vec] for k in range(3))
        pw_w, pw_m, pw_v = refs[2 + 3 * n_vec:5 + 3 * n_vec]
        outs = refs[5 + 3 * n_vec:-1]
        total = refs[-1]
        total[...] = p_ref[0]
        for j in range(1, n_parts):
            total[...] += p_ref[j]
        row = 0
        for i, (_, width) in enumerate(VECTORS):
            n_rows = width // SLAB
            g = jnp.concatenate([total[row + k:row + k + 1, :] for k in range(n_rows)], axis=1)
            row += n_rows
            d, m_new, v_new = _adam(g, w_refs[i][...], m_refs[i][...], v_refs[i][...])
            for ref, val in zip(outs[4 * i:4 * i + 4], (g, d, m_new, v_new)):
                ref[...] = val
        tail = outs[4 * n_vec:]

        def summed(first, count):
            g = pb_ref[0, first:first + count, :].astype(F32)
            for j in range(1, n_parts):
                g = g + pb_ref[j, first:first + count, :].astype(F32)
            return g

        g = summed(0, MATRIX_ROWS)
        d, m_new, v_new = _adam(g, pw_w[...], pw_m[...], pw_v[...])
        for ref, val in zip(tail[0:4], (g, d, m_new, v_new)):
            ref[...] = val
        tail[4][...] = summed(MATRIX_ROWS, GATE_ROWS)
        tail[5][...] = summed(MATRIX_ROWS + GATE_ROWS, GATE_ROWS)
        for tap in range(CONV_WIDTH):
            at = CONV_AT + tap * CONV_LANES
            tail[6][tap:tap + 1, :] = jnp.concatenate([total[at + k:at + k + 1, :] for k in range(CONV_LANES)], axis=1)
        tail[7][...] = (0.5 / D_MODEL) * jnp.sum(total[LOSS_ROW:LOSS_ROW + 1, :], axis=1, keepdims=True)

    out_shape = []
    for _, width in VECTORS:
        out_shape += [jax.ShapeDtypeStruct((1, width), F32)] * 4
    out_shape += [jax.ShapeDtypeStruct((MATRIX_ROWS, SLAB), F32)] * 4 + [jax.ShapeDtypeStruct((GATE_ROWS, SLAB), F32)] * 2
    out_shape += [jax.ShapeDtypeStruct((CONV_WIDTH, LRU_WIDTH), F32), jax.ShapeDtypeStruct((1, 1), F32)]
    outs = pl.pallas_call(
        body, name="adam_small", out_shape=out_shape,
        scratch_shapes=[pltpu.VMEM((PACK_F_ROWS, SLAB), F32)],
        compiler_params=_params(40),
    )(parts, parts_b, *vec_w, *vec_m, *vec_v, *pool_wmv)
    vec_out = [tuple(outs[4 * i:4 * i + 4]) for i in range(n_vec)]
    tail = outs[4 * n_vec:]
    return vec_out, tuple(tail[0:4]), tail[4], tail[5], tail[6], tail[7]


def _plain_adam(grads, ws, ms, vs):
    n = len(grads)

    def body(*refs):
        ins, outs = refs[:4 * n], refs[4 * n:]
        for i in range(n):
            d, m_new, v_new = _adam(ins[i][...], ins[n + i][...], ins[2 * n + i][...], ins[3 * n + i][...])
            for ref, val in zip(outs[3 * i:3 * i + 3], (d, m_new, v_new)):
                ref[...] = val

    out_shape = []
    for g in grads:
        out_shape += [jax.ShapeDtypeStruct(g.shape, F32)] * 3
    outs = pl.pallas_call(body, name="adam_plain", out_shape=out_shape)(*grads, *ws, *ms, *vs)
    return [tuple(outs[3 * i:3 * i + 3]) for i in range(n)]


def kernel(x, ln1_g, w_in, pool_w, pool_scale, conv_w, conv_b, w_a, b_a, w_i, b_i, lam, gn_pool_g, gn_lru_g, w_out, ln2_g, w_ffn_gate, w_ffn_up, w_ffn_down, lnf_g, loss_target, m_ln1_g, m_w_in, m_pool_w, m_pool_scale, m_conv_w, m_conv_b, m_w_a, m_b_a, m_w_i, m_b_i, m_lam, m_gn_pool_g, m_gn_lru_g, m_w_out, m_ln2_g, m_w_ffn_gate, m_w_ffn_up, m_w_ffn_down, m_lnf_g, v_ln1_g, v_w_in, v_pool_w, v_pool_scale, v_conv_w, v_conv_b, v_w_a, v_b_a, v_w_i, v_b_i, v_lam, v_gn_pool_g, v_gn_lru_g, v_w_out, v_ln2_g, v_w_ffn_gate, v_w_ffn_up, v_w_ffn_down, v_lnf_g):
    weights = dict(ln1_g=ln1_g, w_in=w_in, pool_w=pool_w, pool_scale=pool_scale, conv_w=conv_w, conv_b=conv_b,
                   w_a=w_a, b_a=b_a, w_i=w_i, b_i=b_i, lam=lam, gn_pool_g=gn_pool_g, gn_lru_g=gn_lru_g,
                   w_out=w_out, ln2_g=ln2_g, w_ffn_gate=w_ffn_gate, w_ffn_up=w_ffn_up, w_ffn_down=w_ffn_down,
                   lnf_g=lnf_g)
    mom1 = dict(ln1_g=m_ln1_g, w_in=m_w_in, pool_w=m_pool_w, pool_scale=m_pool_scale, conv_w=m_conv_w,
                conv_b=m_conv_b, w_a=m_w_a, b_a=m_b_a, w_i=m_w_i, b_i=m_b_i, lam=m_lam, gn_pool_g=m_gn_pool_g,
                gn_lru_g=m_gn_lru_g, w_out=m_w_out, ln2_g=m_ln2_g, w_ffn_gate=m_w_ffn_gate,
                w_ffn_up=m_w_ffn_up, w_ffn_down=m_w_ffn_down, lnf_g=m_lnf_g)
    mom2 = dict(ln1_g=v_ln1_g, w_in=v_w_in, pool_w=v_pool_w, pool_scale=v_pool_scale, conv_w=v_conv_w,
                conv_b=v_conv_b, w_a=v_w_a, b_a=v_b_a, w_i=v_w_i, b_i=v_b_i, lam=v_lam, gn_pool_g=v_gn_pool_g,
                gn_lru_g=v_gn_lru_g, w_out=v_w_out, ln2_g=v_ln2_g, w_ffn_gate=v_w_ffn_gate,
                w_ffn_up=v_w_ffn_up, w_ffn_down=v_w_ffn_down, lnf_g=v_lnf_g)

    xs = x[0]
    target = loss_target[0]

    shard = dict(w_in=lambda a: a[0].T, w_ffn_gate=lambda a: a[0].T, w_ffn_up=lambda a: a[0].T,
                 w_out=lambda a: a[0], w_ffn_down=lambda a: a[0], conv_w=lambda a: a[0])
    unshard = dict(w_in=lambda a: a.T[None], w_ffn_gate=lambda a: a.T[None], w_ffn_up=lambda a: a.T[None],
                   w_out=lambda a: a[None], w_ffn_down=lambda a: a[None], conv_w=lambda a: a[None])

    gathered = ("w_in", "conv_w", "w_out", "w_ffn_gate", "w_ffn_up", "w_ffn_down")
    groups = ((0, 1), (2,), (3, 4), (5,))
    sources = [shard[k](weights[k]) if k == "conv_w" else shard[k](weights[k]).astype(BF16) for k in gathered]
    my_index = 4 * lax.axis_index("x") + 2 * lax.axis_index("y") + lax.axis_index("c")
    lands = [lax.empty((N_DEV,) + a.shape, a.dtype) for a in sources]

    def first_hop(n, at=0):
        return _numbered(_fan_out(range(at, at + n), range(at + n, at + 2 * n), SAME_CORE + SIBLING + MYSELF, 0))

    def second_hop(n, at=0):
        return _numbered(_relay(range(at, at + n), SAME_CORE))

    start = []
    for g, members in enumerate(groups):
        start += _numbered(_fan_out(members, [6 + m for m in members], SAME_CORE + SIBLING + MYSELF, 0), g)
    sems, bufs, _ = _comm_call("gather_start", sources + lands,
                               start=(start, [sum(cp.group == g for cp in start) for g in range(len(groups))]))
    sources, lands = bufs[:6], bufs[6:]

    def group_buffers(g):
        return [sources[m] for m in groups[g]] + [lands[m] for m in groups[g]]

    def relayed(tag, g, after):
        n = len(groups[g])
        relay = second_hop(n, n)
        relay_sems, bufs, _ = _comm_call("gather_relay_" + tag, group_buffers(g), wait=(sems[g], first_hop(n)),
                                         start=(relay, (len(relay),)), after=after)
        return relay_sems[0], bufs[n:]

    relay_sem, bufs = relayed("in", 0, None)
    g_in, g_conv = _comm_call("gather_wait_in", bufs, wait=(relay_sem, second_hop(2)))[1]
    w_in_f = g_in.reshape(IN_WIDTH, D_MODEL)
    conv_w_f = _cols_from_stack(g_conv)

    wa_bd = _block_diag(w_a[0])
    wi_bd = _block_diag(w_i[0])
    lnf_row = lnf_g.reshape(1, D_MODEL)

    u_pool, u_lru, u_gate = _fwd_in(xs, ln1_g, w_in_f)
    y_pool, h, y_lru = _mixer_fwd(u_pool, u_lru, u_gate, pool_w[0], pool_scale, conv_w_f, conv_b,
                                  wa_bd, b_a, wi_bd, b_i, lam)
    relay_sem, bufs = relayed("out", 1, y_pool)
    gate_up = second_hop(2, 3)
    (gate_up_sem,), bufs, _ = _comm_call("gather_wait_out", bufs + group_buffers(2),
                                         wait=[(relay_sem, second_hop(1)), (sems[2], first_hop(2, 1))],
                                         start=(gate_up, (len(gate_up),)))
    g_out, gate_up_lands = bufs[0], bufs[3:]
    w_out_f = g_out.reshape(D_MODEL, D_MODEL)
    h1, n2 = _fwd_out(xs, y_pool, y_lru, gn_pool_g, gn_lru_g, w_out_f, ln2_g)
    relay_sem, bufs = relayed("down", 3, n2)
    g_gate, g_up, g_down = _comm_call("gather_wait_ffn", gate_up_lands + bufs,
                                      wait=[(gate_up_sem, second_hop(2)), (relay_sem, second_hop(1, 2))])[1]
    w_gate_f = g_gate.reshape(D_FF, D_MODEL)
    w_up_f = g_up.reshape(D_FF, D_MODEL)
    w_down_f = g_down.reshape(D_FF, D_MODEL)
    g_act, u_act, dh2, dh2b, d_lnf, sq = _ffn_fwd(h1, n2, target, lnf_row, w_gate_f, w_up_f, w_down_f)

    def direct(tag, stacks, wholes, after):
        sources = list(stacks) + list(wholes)
        n, n_st = len(sources), len(stacks)
        lands = [lax.empty(a.shape if i < n_st else (N_DEV,) + a.shape, a.dtype) for i, a in enumerate(sources)]
        copies = _numbered(_scatter(range(n_st), range(n, n + n_st), EVERYONE + MYSELF)
                           + _fan_out(range(n_st, n), range(n + n_st, 2 * n), EVERYONE + MYSELF, 0))
        sem, bufs, token = _comm_call(tag + "_start", sources + lands, start=(copies, (len(copies),)), after=after)
        return (tag, sem[0], bufs, copies), token

    def direct_finish(started, after):
        tag, sem, bufs, copies = started
        _, bufs, _ = _comm_call(tag + "_wait", bufs, wait=(sem, copies), after=after)
        return bufs[len(bufs) // 2:]

    place = jnp.stack([lax.axis_index("c"), 2 * lax.axis_index("x") + lax.axis_index("y")]).astype(jnp.int32)

    d_gate, d_up, d_down, dn2 = _ffn_bwd(n2, dh2b, g_act, u_act, w_gate_f, w_up_f, w_down_f)
    down_stack, *ffn_stacks = [d.reshape(N_DEV, D_FF // N_DEV, D_MODEL) for d in (d_down, d_gate, d_up)]
    pair_lands = [lax.empty((N_DEV // 2,) + a.shape[1:], a.dtype) for a in ffn_stacks]
    pair_copies = _numbered(_to_sibling(range(2), range(2, 4)))
    down_copies = _numbered(_scatter([4], [5], EVERYONE + MYSELF), 1)
    (sem, down_sem), bufs, token = _comm_call(
        "ffn_grads_start", ffn_stacks + pair_lands + [down_stack, lax.empty(down_stack.shape, down_stack.dtype)],
        start=(pair_copies + down_copies, (len(pair_copies), len(down_copies))))
    bufs, down_bufs = bufs[:4], bufs[4:]
    dh1, dy_pool, dy_lru, d_out, d_ln2, d_gnp, d_gnl = _bwd_out(dn2, dh2, h1, y_pool, y_lru, gn_pool_g, gn_lru_g,
                                                                 w_out_f, ln2_g, token)
    _, bufs, _ = _comm_call("ffn_pair_wait", bufs, wait=(sem, pair_copies), after=dh1)
    ffn_sums, ffn_lands = _pair_sum(bufs[:2], bufs[2:], place)
    ffn_copies = _numbered(_to_chips(range(2), range(2, 4)))
    ffn_sem, ffn_bufs, token = _comm_call("ffn_chip_start", ffn_sums + ffn_lands, start=(ffn_copies, (len(ffn_copies),)))
    (du_pool, du_lru, du_gate, d_pw, d_ps, d_cw, d_cb, d_wa, d_ba, d_wi, d_bi, d_lam) = _mixer_bwd(
        u_pool, u_lru, u_gate, h, dy_pool, dy_lru, pool_w[0], pool_scale, conv_w_f, conv_b,
        wa_bd, b_a, wi_bd, b_i, lam, token)

    vec_grads = dict(ln2_g=d_ln2, lnf_g=d_lnf, pool_scale=d_ps, conv_b=d_cb, b_a=d_ba, b_i=d_bi,
                     lam=d_lam, gn_pool_g=d_gnp, gn_lru_g=d_gnl)
    packed, packed_b = _pack_small([vec_grads[k] for k, _ in VECTORS], d_pw, d_wa, d_wi, d_cw, sq)
    small_started, token = direct("small", [d_out.reshape(N_DEV, D_MODEL // N_DEV, D_MODEL)], [packed, packed_b], None)
    grad_x, d_in, d_ln1 = _bwd_in(xs, dh1, du_pool, du_lru, du_gate, ln1_g, w_in_f, token)
    in_started, token = direct("in", [d_in.reshape(N_DEV, IN_WIDTH // N_DEV, D_MODEL)], [d_ln1], None)

    results = {}

    def reduce_adam(name, parts):
        outs = _reduce_adam(parts, shard[name](weights[name]), shard[name](mom1[name]), shard[name](mom2[name]),
                            "adam_" + name)
        results[name] = tuple(unshard[name](o) for o in outs)
        return outs[0]

    shard["ln1_g"] = unshard["ln1_g"] = lambda a: a
    _, ffn_bufs, _ = _comm_call("ffn_chip_wait", ffn_bufs, wait=(ffn_sem[0], ffn_copies), after=token)
    done = [reduce_adam(name, parts) for name, parts in zip(("w_ffn_gate", "w_ffn_up"), ffn_bufs[2:])]
    _, (_, r_down), _ = _comm_call("down_wait", down_bufs, after=done,
                                   wait=(down_sem, _numbered(_scatter([0], [1], EVERYONE + MYSELF))))
    done.append(reduce_adam("w_ffn_down", r_down))
    r_out, r_small, r_small_b = direct_finish(small_started, done)
    done.append(reduce_adam("w_out", r_out))

    def as_row(a, width):
        return a.reshape(1, width)

    def as_matrix(a):
        return a.reshape(MATRIX_ROWS, SLAB)

    def as_heads(a):
        return a.reshape(2 * GATE_ROWS, HEAD)

    def heads_apart(g):
        return jnp.transpose(g.reshape(N_SLAB, HEAD, 2, HEAD), (0, 2, 1, 3)).reshape(2 * GATE_ROWS, HEAD)

    vec_out, pool_out, g_wa, g_wi, g_conv, loss_11 = _small_reduce_adam(
        r_small, r_small_b, [as_row(weights[k], w) for k, w in VECTORS], [as_row(mom1[k], w) for k, w in VECTORS],
        [as_row(mom2[k], w) for k, w in VECTORS], [as_matrix(t["pool_w"]) for t in (weights, mom1, mom2)])
    for (k, _), outs in zip(VECTORS, vec_out):
        results[k] = tuple(o.reshape(weights[k].shape) for o in outs)
    results["pool_w"] = tuple(o.reshape(pool_w.shape) for o in pool_out)
    my_columns = conv_w.shape[-1]
    plain_names = ("w_a", "w_i", "conv_w")
    plain_grads = [heads_apart(g) for g in (g_wa, g_wi)]
    plain_grads.append(lax.dynamic_slice_in_dim(g_conv, my_index * my_columns, my_columns, axis=1))
    views = (as_heads, as_heads, lambda a: a[0])
    plain_out = _plain_adam(plain_grads, *[[view(t[k]) for k, view in zip(plain_names, views)]
                                           for t in (weights, mom1, mom2)])
    for k, g, outs in zip(plain_names, plain_grads, plain_out):
        results[k] = tuple(o.reshape(weights[k].shape) for o in (g,) + outs)
    loss = loss_11[0, 0]
    r_in, r_ln1 = direct_finish(in_started, done + [plain_out[0][0], loss_11])
    reduce_adam("w_in", r_in)
    reduce_adam("ln1_g", r_ln1)

    order = ["ln1_g", "w_in", "pool_w", "pool_scale", "conv_w", "conv_b", "w_a", "b_a", "w_i", "b_i", "lam",
             "gn_pool_g", "gn_lru_g", "w_out", "ln2_g", "w_ffn_gate", "w_ffn_up", "w_ffn_down", "lnf_g"]
    return (loss, grad_x[None],
            *[results[k][0] for k in order], *[results[k][1] for k in order],
            *[results[k][2] for k in order], *[results[k][3] for k in order])
```

```python
from typing import Any, NamedTuple

import jax
import jax.numpy as jnp
from jax import lax
from jax.experimental import pallas as pl
from jax.experimental.pallas import tpu as pltpu

F32 = jnp.float32
BF16 = jnp.bfloat16

N_DEV = 8
D_MODEL = 1024
POOL_WIDTH = 512
LRU_WIDTH = 512
IN_WIDTH = 1536
D_FF = 2816
N_SLAB = 4
SLAB = 128
CONV_WIDTH = 4
LRU_C = 8.0
EPS = 1e-6
HALO = 16
FF_CHUNK = 256

ADAM_LR = 0.001
ADAM_B1 = 0.9
ADAM_B2 = 0.999
ADAM_EPS = 1e-08
ADAM_WD = 0.01
ADAM_STEP = 10

MIB = 1 << 20
MESH = pl.DeviceIdType.MESH


def _params(vmem_mib, n_axes=0):
    sem = ("arbitrary",) * n_axes if n_axes else None
    return pltpu.CompilerParams(dimension_semantics=sem, vmem_limit_bytes=vmem_mib * MIB)


def _in_hbm(*arrays):
    return [pltpu.with_memory_space_constraint(a, pltpu.HBM) for a in arrays]


def _mm(a, b):
    return jnp.dot(a, b, preferred_element_type=F32)


def _mm_nt(a, b):
    return lax.dot_general(a, b, (((1,), (1,)), ((), ())), preferred_element_type=F32)


def _mm_tn(a, b):
    return lax.dot_general(a, b, (((0,), (0,)), ((), ())), preferred_element_type=F32)


def _rms(x, g):
    rstd = lax.rsqrt(jnp.mean(x * x, axis=-1, keepdims=True) + EPS)
    xhat = x * rstd
    return xhat * g, xhat, rstd


def _rms_bwd(dy, xhat, rstd, g):
    gy = dy * g
    dx = rstd * (gy - xhat * jnp.mean(gy * xhat, axis=-1, keepdims=True))
    return dx, jnp.sum(dy * xhat, axis=0, keepdims=True)


def _gelu(z):
    t = jnp.tanh(0.7978845608028654 * (z + 0.044715 * z * z * z))
    return 0.5 * z * (1.0 + t), t


def _gelu_grad(z, t):
    return 0.5 * (1.0 + t) + 0.5 * z * (1.0 - t * t) * 0.7978845608028654 * (1.0 + 3.0 * 0.044715 * z * z)


def _softplus_neg(lam):
    x = -lam
    e = jnp.exp(-jnp.abs(x))
    u = 1.0 + e
    l1p = jnp.where(u == 1.0, e, jnp.log(u) * e / (u - 1.0))
    return jnp.maximum(x, 0.0) + l1p


def _one_minus_square(la, a):
    x = 2.0 * la
    series = -x * (1.0 + x * (0.5 + x * (1.0 / 6.0 + x * (1.0 / 24.0))))
    return jnp.where(x > -0.06, series, 1.0 - a * a)


def _sigmoid(x):
    return 0.5 * jnp.tanh(0.5 * x) + 0.5


def _down(v, d):
    return pltpu.roll(v, d, 0)


def _up(v, d):
    return pltpu.roll(v, v.shape[0] - d, 0)


def _slab_major(s, dtype=F32):
    return pltpu.HBM((N_SLAB, s, SLAB), dtype)


def _slabs_spec(tb):
    return pl.BlockSpec((N_SLAB, tb, SLAB), lambda i: (0, i, 0))


def _read_slabs(ref):
    return jnp.concatenate([ref[k] for k in range(N_SLAB)], axis=1)


def _write_slabs(ref, value):
    for k in range(N_SLAB):
        ref[k] = value[:, k * SLAB:(k + 1) * SLAB]


def _token_block(s, most=512):
    for rows in (most, 512, 256):
        if rows <= most and s % rows == 0 and s > rows:
            return rows
    return s


def _time_chunk(s):
    return 512 if s % 512 == 0 and s > 512 else 256 if s % 256 == 0 else s


def _fwd_in(x, ln1_g, w_in_t):
    s = x.shape[0]
    tb = _token_block(s, 1024)

    def body(x_ref, g_ref, w_ref, up_ref, ul_ref, ug_ref):
        n, _, _ = _rms(x_ref[...], g_ref[...])
        proj = _mm_nt(n.astype(BF16), w_ref[...])
        _write_slabs(up_ref, proj[:, :POOL_WIDTH])
        _write_slabs(ul_ref, proj[:, POOL_WIDTH:POOL_WIDTH + LRU_WIDTH])
        _write_slabs(ug_ref, proj[:, POOL_WIDTH + LRU_WIDTH:])

    out = _slab_major(s)
    return pl.pallas_call(
        body, name="fwd_in", grid=(s // tb,),
        in_specs=[pl.BlockSpec((tb, D_MODEL), lambda i: (i, 0)),
                  pl.BlockSpec((1, D_MODEL), lambda i: (0, 0)),
                  pl.BlockSpec((IN_WIDTH, D_MODEL), lambda i: (0, 0))],
        out_specs=[_slabs_spec(tb)] * 3,
        out_shape=[out, out, out],
        compiler_params=_params(40, 1),
    )(*_in_hbm(x, ln1_g, w_in_t))


def _pool_denominator(t0, row, window):
    return jnp.minimum((t0 + row + 1).astype(F32), window)


def _causal_window(ext, deeper):
    s = ext + _down(ext, 1)
    s = s + deeper[0] * _down(s, 2)
    s = s + deeper[1] * _down(s, 4)
    s = s + deeper[2] * _down(s, 8)
    return s[HALO:]


def _anticausal_window(ext, deeper, rows):
    s = ext + _up(ext, 1)
    s = s + deeper[0] * _up(s, 2)
    s = s + deeper[1] * _up(s, 4)
    s = s + deeper[2] * _up(s, 8)
    return s[:rows]


def _conv(taps, cw, cb):
    return cw[3] * taps[0] + cw[2] * taps[1] + cw[1] * taps[2] + cw[0] * taps[3] + cb


def _lru_gates(xc, wa, ba, wi, bi, sp):
    xb = xc.astype(BF16)
    r = _sigmoid(_mm(xb, wa) + ba)
    i = _sigmoid(_mm(xb, wi) + bi)
    la = (-LRU_C) * r * sp
    a = jnp.exp(la)
    mult = jnp.sqrt(jnp.maximum(_one_minus_square(la, a), 0.0))
    return xb, r, i, a, mult


SUBLANES = 8


def _scan_pads(a_pad, b_pad, rows, causal):
    spare = slice(0, SUBLANES) if causal else slice(rows, rows + SUBLANES)
    a_pad[spare, :] = jnp.ones((SUBLANES, SLAB), F32)
    b_pad[spare, :] = jnp.zeros((SUBLANES, SLAB), F32)


def _scan_causal(a, b, h_prev, a_pad, b_pad, rows):
    d = 1
    while d < min(SUBLANES, rows):
        a_pad[SUBLANES:, :] = a
        b_pad[SUBLANES:, :] = b
        b = a * b_pad[SUBLANES - d:SUBLANES - d + rows, :] + b
        a = a * a_pad[SUBLANES - d:SUBLANES - d + rows, :]
        d *= 2
    while d < rows:
        b = jnp.concatenate([b[:d], a[d:] * b[:-d] + b[d:]], axis=0)
        a = jnp.concatenate([a[:d], a[d:] * a[:-d]], axis=0)
        d *= 2
    return b + a * h_prev


def _scan_anticausal(a, b, l_next, a_pad, b_pad, rows):
    d = 1
    while d < min(SUBLANES, rows):
        a_pad[:rows, :] = a
        b_pad[:rows, :] = b
        b = a * b_pad[d:d + rows, :] + b
        a = a * a_pad[d:d + rows, :]
        d *= 2
    while d < rows:
        b = jnp.concatenate([a[:-d] * b[d:] + b[:-d], b[-d:]], axis=0)
        a = jnp.concatenate([a[:-d] * a[d:], a[-d:]], axis=0)
        d *= 2
    return b + a * l_next


def _slab_scalars():
    slab = pl.program_id(0)
    deeper = [jnp.where(slab > k, 1.0, 0.0).astype(F32) for k in range(N_SLAB - 1)]
    window = jnp.left_shift(jnp.int32(2), slab).astype(F32)
    inverse = jnp.where(slab == 0, 0.5, jnp.where(slab == 1, 0.25, jnp.where(slab == 2, 0.125, 0.0625))).astype(F32)
    return deeper, (window, inverse)


def _window_mean(total, t0, row, window, at_start):
    if at_start:
        return total / _pool_denominator(t0, row, window[0])
    return total * window[1]


def _slab_specs(s):
    seq = pl.BlockSpec((None, s, SLAB), lambda k: (k, 0, 0))
    mat = pl.BlockSpec((1, SLAB, SLAB), lambda k: (k, 0, 0))
    vec = pl.BlockSpec((1, SLAB), lambda k: (0, k))
    taps = pl.BlockSpec((CONV_WIDTH, SLAB), lambda k: (0, k))
    return seq, mat, vec, taps


def _mixer_fwd(u_pool, u_lru, u_gate, pool_w, pool_scale, conv_w, conv_b, wa_bd, b_a, wi_bd, b_i, lam):
    s = u_pool.shape[1]
    tc = _time_chunk(s)
    n_chunks = s // tc

    def body(up_ref, ul_ref, ug_ref, pw_ref, ps_ref, cw_ref, cb_ref, wa_ref, ba_ref, wi_ref, bi_ref, lam_ref,
             yp_ref, h_ref, yl_ref, shift_ref, a_pad, b_pad):
        _scan_pads(a_pad, b_pad, tc, causal=True)
        deeper, window = _slab_scalars()
        pw = pw_ref[0].astype(BF16)
        wa = wa_ref[0].astype(BF16)
        wi = wi_ref[0].astype(BF16)
        ps, cb, ba, bi = ps_ref[...], cb_ref[...], ba_ref[...], bi_ref[...]
        cw = [cw_ref[k:k + 1, :] for k in range(CONV_WIDTH)]
        sp = _softplus_neg(lam_ref[...])
        row = lax.broadcasted_iota(jnp.int32, (tc, SLAB), 0)

        def chunk(t0, ext_p, ext_l, h_prev, at_start=False):
            rows = pl.ds(t0, tc)
            d = _window_mean(_causal_window(ext_p, deeper), t0, row, window, at_start) - ext_p[HALO:]
            yp_ref[rows, :] = _mm(d.astype(BF16), pw) * ps
            shift_ref[...] = ext_l
            xc = _conv([shift_ref[HALO - k:HALO - k + tc, :] for k in range(CONV_WIDTH)], cw, cb)
            _, _, i, a, mult = _lru_gates(xc, wa, ba, wi, bi, sp)
            h = _scan_causal(a, mult * (i * xc), h_prev, a_pad, b_pad, tc)
            h_ref[rows, :] = h
            yl_ref[rows, :] = h * _gelu(ug_ref[rows, :])[0]
            return h[tc - 1:tc, :]

        pad = jnp.zeros((HALO, SLAB), F32)
        h0 = chunk(0, jnp.concatenate([pad, up_ref[pl.ds(0, tc), :]], axis=0),
                   jnp.concatenate([pad, ul_ref[pl.ds(0, tc), :]], axis=0), jnp.zeros((1, SLAB), F32), at_start=True)

        def step(c, h_prev):
            t0 = pl.multiple_of(c * tc, tc)
            ext = pl.ds(pl.multiple_of(c * tc - HALO, HALO), tc + HALO)
            return chunk(t0, up_ref[ext, :], ul_ref[ext, :], h_prev)

        lax.fori_loop(1, n_chunks, step, h0)

    seq, mat, vec, taps = _slab_specs(s)
    out = _slab_major(s)
    return pl.pallas_call(
        body, name="mixer_fwd", grid=(N_SLAB,),
        in_specs=[seq, seq, seq, mat, vec, taps, vec, mat, vec, mat, vec, vec],
        out_specs=[seq, seq, seq], out_shape=[out, out, out],
        scratch_shapes=[pltpu.VMEM((tc + HALO, SLAB), F32), pltpu.VMEM((tc + SUBLANES, SLAB), F32),
                        pltpu.VMEM((tc + SUBLANES, SLAB), F32)],
        compiler_params=_params(48, 1),
    )(*_in_hbm(u_pool, u_lru, u_gate, pool_w, pool_scale, conv_w, conv_b, wa_bd, b_a, wi_bd, b_i, lam))


def _fwd_out(x, y_pool, y_lru, gn_pool_g, gn_lru_g, w_out, ln2_g):
    s = x.shape[0]
    tb = _token_block(s, 1024)

    def body(x_ref, yp_ref, yl_ref, gp_ref, gl_ref, w_ref, g2_ref, h1_ref, n2_ref):
        mp, _, _ = _rms(_read_slabs(yp_ref), gp_ref[...])
        ml, _, _ = _rms(_read_slabs(yl_ref), gl_ref[...])
        h1 = x_ref[...] + _mm(mp.astype(BF16), w_ref[:POOL_WIDTH, :]) + _mm(ml.astype(BF16), w_ref[POOL_WIDTH:, :])
        h1_ref[...] = h1
        n2_ref[...] = _rms(h1, g2_ref[...])[0].astype(BF16)

    row = pl.BlockSpec((tb, D_MODEL), lambda i: (i, 0))
    half = _slabs_spec(tb)
    return pl.pallas_call(
        body, name="fwd_out", grid=(s // tb,),
        in_specs=[row, half, half, pl.BlockSpec((1, 512), lambda i: (0, 0)), pl.BlockSpec((1, 512), lambda i: (0, 0)),
                  pl.BlockSpec((D_MODEL, D_MODEL), lambda i: (0, 0)), pl.BlockSpec((1, D_MODEL), lambda i: (0, 0))],
        out_specs=[row, row],
        out_shape=[pltpu.HBM((s, D_MODEL), F32), pltpu.HBM((s, D_MODEL), BF16)],
        compiler_params=_params(40, 1),
    )(*_in_hbm(x, y_pool, y_lru, gn_pool_g, gn_lru_g, w_out, ln2_g))


def _ffn_fwd(h1, n2, target, lnf_g, w_gate, w_up, w_down):
    s = h1.shape[0]
    tb = 512
    sub = 256
    n_ff = D_FF // FF_CHUNK

    def body(h1_ref, n2_ref, t_ref, gf_ref, wg_hbm, wu_hbm, wd_hbm,
             g_ref, u_ref, dh_ref, dhb_ref, dgf_ref, sq_ref, wg, wu, wd, sem):
        @pl.when(pl.program_id(0) == 0)
        def _():
            loads = [pltpu.make_async_copy(src, dst, sem.at[k])
                     for k, (src, dst) in enumerate(((wg_hbm, wg), (wu_hbm, wu), (wd_hbm, wd)))]
            for cp in loads:
                cp.start()
            for cp in loads:
                cp.wait()
            dgf_ref[...] = jnp.zeros_like(dgf_ref)
            sq_ref[...] = jnp.zeros_like(sq_ref)

        n2v = n2_ref[...]
        acc = jnp.zeros((tb, D_MODEL), F32)
        for c in range(n_ff):
            cols = slice(c * FF_CHUNK, (c + 1) * FF_CHUNK)
            g = _mm_nt(n2v, wg[cols, :])
            u = _mm_nt(n2v, wu[cols, :])
            g_ref[:, cols] = g.astype(BF16)
            u_ref[:, cols] = u.astype(BF16)
            act = g * jax.nn.sigmoid(g) * u
            acc = acc + _mm(act.astype(BF16), wd[cols, :])
        gf = gf_ref[...]
        for r in range(tb // sub):
            rows = slice(r * sub, (r + 1) * sub)
            y, xhat, rstd = _rms(h1_ref[rows, :] + acc[rows, :], gf)
            err = y - t_ref[rows, :]
            sq_ref[...] += jnp.sum(err * err, axis=0, keepdims=True)
            dh2, dgf = _rms_bwd(err * (1.0 / D_MODEL), xhat, rstd, gf)
            dgf_ref[...] += dgf
            dh_ref[rows, :] = dh2
            dhb_ref[rows, :] = dh2.astype(BF16)

    row = pl.BlockSpec((tb, D_MODEL), lambda i: (i, 0))
    ff = pl.BlockSpec((tb, D_FF), lambda i: (i, 0))
    vec = pl.BlockSpec((1, D_MODEL), lambda i: (0, 0))
    anyspace = pl.BlockSpec(memory_space=pl.ANY)
    return pl.pallas_call(
        body, name="ffn_fwd", grid=(s // tb,),
        in_specs=[row, row, row, vec, anyspace, anyspace, anyspace],
        out_specs=[ff, ff, row, row, vec, vec],
        out_shape=[pltpu.HBM((s, D_FF), BF16), pltpu.HBM((s, D_FF), BF16),
                   pltpu.HBM((s, D_MODEL), F32), pltpu.HBM((s, D_MODEL), BF16),
                   jax.ShapeDtypeStruct((1, D_MODEL), F32), jax.ShapeDtypeStruct((1, D_MODEL), F32)],
        scratch_shapes=[pltpu.VMEM((D_FF, D_MODEL), BF16), pltpu.VMEM((D_FF, D_MODEL), BF16),
                        pltpu.VMEM((D_FF, D_MODEL), BF16), pltpu.SemaphoreType.DMA((3,))],
        compiler_params=_params(60, 1),
    )(*_in_hbm(h1, n2, target, lnf_g, w_gate, w_up, w_down))


def _ffn_bwd(n2, dh2b, g, u, w_gate_t, w_up_t, w_down):
    s = n2.shape[0]
    tb = min(1024, s)
    n_ff = D_FF // FF_CHUNK
    n_tb = s // tb

    def body(n2_ref, dh_ref, g_ref, u_ref, wg_ref, wu_ref, wd_ref, dwg_ref, dwu_ref, dwd_ref, dn2_ref,
             dn2_acc, acc_g, acc_u, acc_d):
        j = pl.program_id(0)
        t = pl.program_id(1)
        rows = pl.ds(pl.multiple_of(t * tb, tb), tb)

        @pl.when(j == 0)
        def _():
            dn2_acc[rows, :] = jnp.zeros((tb, D_MODEL), F32)

        @pl.when(t == 0)
        def _():
            acc_g[...] = jnp.zeros_like(acc_g)
            acc_u[...] = jnp.zeros_like(acc_u)
            acc_d[...] = jnp.zeros_like(acc_d)

        n2v = n2_ref[...]
        dh = dh_ref[...]
        gv = g_ref[...].astype(F32)
        uv = u_ref[...].astype(F32)
        sg = jax.nn.sigmoid(gv)
        silu = gv * sg
        dact = _mm_nt(dh, wd_ref[...])
        dub = (dact * silu).astype(BF16)
        dgb = (dact * uv * (sg * (1.0 + gv * (1.0 - sg)))).astype(BF16)
        for acc, out, new in ((acc_d, dwd_ref, _mm_tn((silu * uv).astype(BF16), dh)),
                              (acc_g, dwg_ref, _mm_tn(dgb, n2v)), (acc_u, dwu_ref, _mm_tn(dub, n2v))):
            total = acc[...] + new
            acc[...] = total
            out[...] = total.astype(BF16)
        dn2_acc[rows, :] += _mm(jnp.concatenate([dgb, dub], axis=1),
                                jnp.concatenate([wg_ref[...], wu_ref[...]], axis=0))

        @pl.when(j == n_ff - 1)
        def _():
            dn2_ref[...] = dn2_acc[rows, :]

    row = pl.BlockSpec((tb, D_MODEL), lambda j, t: (t, 0))
    act = pl.BlockSpec((tb, FF_CHUNK), lambda j, t: (t, j))
    w_row = pl.BlockSpec((FF_CHUNK, D_MODEL), lambda j, t: (j, 0))
    last = pl.BlockSpec((tb, D_MODEL), lambda j, t: (jnp.where(j == n_ff - 1, t, 0), 0))
    grad = pltpu.HBM((D_FF, D_MODEL), BF16)
    chunk_acc = pltpu.VMEM((FF_CHUNK, D_MODEL), F32)
    return pl.pallas_call(
        body, name="ffn_bwd", grid=(n_ff, n_tb),
        in_specs=[row, row, act, act, w_row, w_row, w_row],
        out_specs=[w_row, w_row, w_row, last],
        out_shape=[grad, grad, grad, pltpu.HBM((s, D_MODEL), F32)],
        scratch_shapes=[pltpu.VMEM((s, D_MODEL), F32), chunk_acc, chunk_acc, chunk_acc],
        compiler_params=_params(56, 2),
    )(*_in_hbm(n2, dh2b, g, u, w_gate_t, w_up_t, w_down))


def _bwd_out(dn2, dh2, h1, y_pool, y_lru, gn_pool_g, gn_lru_g, w_out, ln2_g, after):
    s = h1.shape[0]
    tb = _token_block(s)

    def body(dn2_ref, dh2_ref, h1_ref, yp_ref, yl_ref, gp_ref, gl_ref, w_ref, g2_ref, _after,
             dh1_ref, dyp_ref, dyl_ref, dwb_ref, dg2_ref, dgp_ref, dgl_ref, dw_ref):
        @pl.when(pl.program_id(0) == 0)
        def _():
            dw_ref[...] = jnp.zeros_like(dw_ref)
            dg2_ref[...] = jnp.zeros_like(dg2_ref)
            dgp_ref[...] = jnp.zeros_like(dgp_ref)
            dgl_ref[...] = jnp.zeros_like(dgl_ref)

        g2 = g2_ref[...]
        _, xhat2, rstd2 = _rms(h1_ref[...], g2)
        dres, dg2 = _rms_bwd(dn2_ref[...], xhat2, rstd2, g2)
        dg2_ref[...] += dg2
        dh1 = dh2_ref[...] + dres
        dh1_ref[...] = dh1
        dh1b = dh1.astype(BF16)
        gp, gl = gp_ref[...], gl_ref[...]
        mp, xhat_p, rstd_p = _rms(_read_slabs(yp_ref), gp)
        ml, xhat_l, rstd_l = _rms(_read_slabs(yl_ref), gl)
        dw_ref[:POOL_WIDTH, :] += _mm_tn(mp.astype(BF16), dh1b)
        dw_ref[POOL_WIDTH:, :] += _mm_tn(ml.astype(BF16), dh1b)
        dyp, dgp = _rms_bwd(_mm_nt(dh1b, w_ref[:POOL_WIDTH, :]), xhat_p, rstd_p, gp)
        dyl, dgl = _rms_bwd(_mm_nt(dh1b, w_ref[POOL_WIDTH:, :]), xhat_l, rstd_l, gl)
        _write_slabs(dyp_ref, dyp)
        _write_slabs(dyl_ref, dyl)
        dgp_ref[...] += dgp
        dgl_ref[...] += dgl

        @pl.when(pl.program_id(0) == s // tb - 1)
        def _():
            dwb_ref[...] = dw_ref[...].astype(BF16)

    row = pl.BlockSpec((tb, D_MODEL), lambda i: (i, 0))
    half = _slabs_spec(tb)
    vec = pl.BlockSpec((1, D_MODEL), lambda i: (0, 0))
    hvec = pl.BlockSpec((1, 512), lambda i: (0, 0))
    mat = pl.BlockSpec((D_MODEL, D_MODEL), lambda i: (0, 0))
    return pl.pallas_call(
        body, name="bwd_out", grid=(s // tb,),
        in_specs=[row, row, row, half, half, hvec, hvec, mat, vec, pl.BlockSpec(memory_space=pl.ANY)],
        out_specs=[row, half, half, mat, vec, hvec, hvec],
        out_shape=[pltpu.HBM((s, D_MODEL), F32), _slab_major(s),
                   _slab_major(s), pltpu.HBM((D_MODEL, D_MODEL), BF16),
                   jax.ShapeDtypeStruct((1, D_MODEL), F32), jax.ShapeDtypeStruct((1, 512), F32),
                   jax.ShapeDtypeStruct((1, 512), F32)],
        scratch_shapes=[pltpu.VMEM((D_MODEL, D_MODEL), F32)],
        compiler_params=_params(48, 1),
    )(*_in_hbm(dn2, dh2, h1, y_pool, y_lru, gn_pool_g, gn_lru_g, w_out, ln2_g), after)


def _mixer_bwd(u_pool, u_lru, u_gate, h, dy_pool, dy_lru,
               pool_w, pool_scale, conv_w, conv_b, wa_bd, b_a, wi_bd, b_i, lam, after):
    s = u_pool.shape[1]
    tc = _time_chunk(s)
    n_chunks = s // tc

    def body(up_ref, ul_ref, ug_ref, h_ref, dyp_ref, dyl_ref,
             pw_ref, ps_ref, cw_ref, cb_ref, wa_ref, ba_ref, wi_ref, bi_ref, lam_ref, _after,
             dup_ref, dul_ref, dug_ref, dpw_ref, dps_ref, dcw_ref, dcb_ref, dwa_ref, dba_ref, dwi_ref, dbi_ref, dlam_ref,
             a_pad, b_pad, u_shift, h_shift, a_shift, d_shift):
        _scan_pads(a_pad, b_pad, tc, causal=False)
        deeper, window = _slab_scalars()
        pw = pw_ref[0].astype(BF16)
        wa = wa_ref[0].astype(BF16)
        wi = wi_ref[0].astype(BF16)
        ps, cb, ba, bi = ps_ref[...], cb_ref[...], ba_ref[...], bi_ref[...]
        cw = [cw_ref[k:k + 1, :] for k in range(CONV_WIDTH)]
        lam_v = lam_ref[...]
        sp = _softplus_neg(lam_v)
        row = lax.broadcasted_iota(jnp.int32, (tc, SLAB), 0)
        for ref in (dpw_ref, dps_ref, dcw_ref, dcb_ref, dwa_ref, dba_ref, dwi_ref, dbi_ref, dlam_ref):
            ref[...] = jnp.zeros_like(ref)

        def chunk(t0, ext_p, ext_l, ext_h, carry, at_start=False):
            l_next, a_next, dxc_next, ddn_next = carry
            rows = pl.ds(t0, tc)
            u_shift[...] = ext_l
            taps = [u_shift[HALO - k:HALO - k + tc, :] for k in range(CONV_WIDTH)]
            xc = _conv(taps, cw, cb)
            xb, r, i, a, mult = _lru_gates(xc, wa, ba, wi, bi, sp)
            hv = ext_h[HALO:]
            h_shift[...] = ext_h
            h_before = h_shift[HALO - 1:HALO - 1 + tc, :]
            ug = ug_ref[rows, :]
            dyl = dyl_ref[rows, :]
            gel, th = _gelu(ug)
            dug_ref[rows, :] = (dyl * hv * _gelu_grad(ug, th)).astype(BF16)
            a_shift[:tc, :] = a
            a_shift[tc:, :] = jnp.broadcast_to(a_next, (SUBLANES, SLAB))
            a_after = a_shift[1:1 + tc, :]
            l = _scan_anticausal(a_after, dyl * gel, l_next, a_pad, b_pad, tc)
            dmult = l * (i * xc)
            di = l * mult * xc
            dxc = l * mult * i
            dla = (l * h_before) * a - jnp.where(mult > 0.0, dmult * (a * a) / mult, 0.0)
            dlam_ref[...] += jnp.sum(dla * r, axis=0, keepdims=True)
            dpa = (dla * ((-LRU_C) * sp)) * (r * (1.0 - r))
            dpi = di * (i * (1.0 - i))
            dpab = dpa.astype(BF16)
            dpib = dpi.astype(BF16)
            dwa_ref[0] += _mm_tn(xb, dpab)
            dwi_ref[0] += _mm_tn(xb, dpib)
            dba_ref[...] += jnp.sum(dpa, axis=0, keepdims=True)
            dbi_ref[...] += jnp.sum(dpi, axis=0, keepdims=True)
            dxc = dxc + _mm_nt(dpab, wa) + _mm_nt(dpib, wi)
            d_shift[:tc, :] = dxc
            d_shift[tc:, :] = dxc_next
            dul_ref[rows, :] = (cw[3] * dxc + cw[2] * d_shift[1:1 + tc, :]
                                + cw[1] * d_shift[2:2 + tc, :] + cw[0] * d_shift[3:3 + tc, :]).astype(BF16)
            for k in range(CONV_WIDTH):
                dcw_ref[k:k + 1, :] += jnp.sum(dxc * taps[CONV_WIDTH - 1 - k], axis=0, keepdims=True)
            dcb_ref[...] += jnp.sum(dxc, axis=0, keepdims=True)
            db = (_window_mean(_causal_window(ext_p, deeper), t0, row, window, at_start) - ext_p[HALO:]).astype(BF16)
            dyp = dyp_ref[rows, :]
            dps_ref[...] += jnp.sum(dyp * _mm(db, pw), axis=0, keepdims=True)
            dys = (dyp * ps).astype(BF16)
            dpw_ref[0] += _mm_tn(db, dys)
            dd = _mm_nt(dys, pw)
            ddn = _window_mean(dd, t0, row, window, at_start)
            ext_q = jnp.concatenate([ddn, ddn_next], axis=0)
            dup_ref[rows, :] = (_anticausal_window(ext_q, deeper, tc) - dd).astype(BF16)
            return l[0:1, :], a[0:1, :], dxc[0:8, :], ddn[0:HALO, :]

        def step(k, carry):
            c = n_chunks - 1 - k
            t0 = pl.multiple_of(c * tc, tc)
            ext = pl.ds(pl.multiple_of(c * tc - HALO, HALO), tc + HALO)
            return chunk(t0, up_ref[ext, :], ul_ref[ext, :], h_ref[ext, :], carry)

        carry = (jnp.zeros((1, SLAB), F32), jnp.zeros((1, SLAB), F32),
                 jnp.zeros((8, SLAB), F32), jnp.zeros((HALO, SLAB), F32))
        carry = lax.fori_loop(0, n_chunks - 1, step, carry)
        pad = jnp.zeros((HALO, SLAB), F32)
        first = pl.ds(0, tc)
        chunk(0, jnp.concatenate([pad, up_ref[first, :]], axis=0), jnp.concatenate([pad, ul_ref[first, :]], axis=0),
              jnp.concatenate([pad, h_ref[first, :]], axis=0), carry, at_start=True)
        dlam_ref[...] = dlam_ref[...] * (LRU_C * jax.nn.sigmoid(-lam_v))

    seq, mat, vec, taps = _slab_specs(s)
    grad = _slab_major(s, BF16)
    mats = jax.ShapeDtypeStruct((N_SLAB, SLAB, SLAB), F32)
    vecs = jax.ShapeDtypeStruct((1, 512), F32)
    return pl.pallas_call(
        body, name="mixer_bwd", grid=(N_SLAB,),
        in_specs=[seq] * 6 + [mat, vec, taps, vec, mat, vec, mat, vec, vec, pl.BlockSpec(memory_space=pl.ANY)],
        out_specs=[seq, seq, seq, mat, vec, taps, vec, mat, vec, mat, vec, vec],
        out_shape=[grad, grad, grad, mats, vecs, jax.ShapeDtypeStruct((CONV_WIDTH, 512), F32), vecs,
                   mats, vecs, mats, vecs, vecs],
        scratch_shapes=[pltpu.VMEM((tc + SUBLANES, SLAB), F32), pltpu.VMEM((tc + SUBLANES, SLAB), F32),
                        pltpu.VMEM((tc + HALO, SLAB), F32), pltpu.VMEM((tc + HALO, SLAB), F32),
                        pltpu.VMEM((tc + SUBLANES, SLAB), F32), pltpu.VMEM((tc + SUBLANES, SLAB), F32)],
        compiler_params=_params(56, 1),
    )(*_in_hbm(u_pool, u_lru, u_gate, h, dy_pool, dy_lru, pool_w, pool_scale, conv_w, conv_b, wa_bd, b_a, wi_bd,
               b_i, lam), after)


def _bwd_in(x, dh1, du_pool, du_lru, du_gate, ln1_g, w_in_t, after):
    s = x.shape[0]
    tb = _token_block(s, 1024)

    def body(x_ref, dh1_ref, dup_ref, dul_ref, dug_ref, g_ref, w_ref, _after, dx_ref, dwb_ref, dg_ref, dw_ref):
        @pl.when(pl.program_id(0) == 0)
        def _():
            dw_ref[...] = jnp.zeros_like(dw_ref)
            dg_ref[...] = jnp.zeros_like(dg_ref)

        g1 = g_ref[...]
        n, xhat, rstd = _rms(x_ref[...], g1)
        nb = n.astype(BF16)
        db = jnp.concatenate([_read_slabs(ref) for ref in (dup_ref, dul_ref, dug_ref)], axis=1)
        dw_ref[...] += _mm_tn(db, nb)
        dx, dg1 = _rms_bwd(_mm(db, w_ref[...]), xhat, rstd, g1)
        dx_ref[...] = dh1_ref[...] + dx
        dg_ref[...] += dg1

        @pl.when(pl.program_id(0) == s // tb - 1)
        def _():
            dwb_ref[...] = dw_ref[...].astype(BF16)

    row = pl.BlockSpec((tb, D_MODEL), lambda i: (i, 0))
    half = _slabs_spec(tb)
    vec = pl.BlockSpec((1, D_MODEL), lambda i: (0, 0))
    mat = pl.BlockSpec((IN_WIDTH, D_MODEL), lambda i: (0, 0))
    return pl.pallas_call(
        body, name="bwd_in", grid=(s // tb,),
        in_specs=[row, row, half, half, half, vec, mat, pl.BlockSpec(memory_space=pl.ANY)],
        out_specs=[row, mat, vec],
        out_shape=[pltpu.HBM((s, D_MODEL), F32), pltpu.HBM((IN_WIDTH, D_MODEL), BF16),
                   jax.ShapeDtypeStruct((1, D_MODEL), F32)],
        scratch_shapes=[pltpu.VMEM((IN_WIDTH, D_MODEL), F32)],
        compiler_params=_params(60, 1),
    )(*_in_hbm(x, dh1, du_pool, du_lru, du_gate, ln1_g, w_in_t), after)


def _mesh_position():
    x, y, c = lax.axis_index("x"), lax.axis_index("y"), lax.axis_index("c")
    return x, y, c, 4 * x + 2 * y + c


def _peer(x, y, c, p):
    px = 1 - x if p & 4 else x
    py = 1 - y if p & 2 else y
    pc = 1 - c if p & 1 else c
    return (px, py, pc), 4 * px + 2 * py + pc


HBM_SPEC = pl.BlockSpec(memory_space=pltpu.HBM)
SEM_SPEC = pl.BlockSpec(memory_space=pltpu.SEMAPHORE)
DATAFLOW = pltpu.SideEffectType.DATAFLOW_SIDE_EFFECTING


class Copy(NamedTuple):
    src: int
    src_at: Any
    dst: int
    dst_at: Any
    peer: int
    group: int
    slot: int


SIBLING = (1,)
SAME_CORE = (2, 4, 6)
EVERYONE = tuple(range(1, N_DEV))
MYSELF = (0,)


def _same(index):
    return index


def _chip(index):
    return jnp.right_shift(index, 1)


def _fan_out(srcs, lands, peers, group):
    return [Copy(s, None, d, _same, p, group, N_DEV * i + p) for i, (s, d) in enumerate(zip(srcs, lands)) for p in peers]


def _scatter(stacks, lands, peers):
    return [Copy(s, lambda me, p=p: jnp.bitwise_xor(me, p), d, _same, p, 0, 0)
            for s, d in zip(stacks, lands) for p in peers]


def _numbered(copies, group=0):
    return [cp._replace(group=group, slot=i) for i, cp in enumerate(copies)]


def _relay(lands, peers):
    return [Copy(b, lambda s, q=q: jnp.bitwise_xor(s, q), b, lambda s, q=q: jnp.bitwise_xor(s, q), 1, 0, N_DEV * i + q)
            for i, b in enumerate(lands) for q in peers]


def _to_sibling(stacks, lands):
    return [Copy(s, lambda me, k=k: 2 * k + 1 - jnp.bitwise_and(me, 1), d, lambda me, k=k: k, 1, 0, 4 * i + k)
            for i, (s, d) in enumerate(zip(stacks, lands)) for k in range(N_DEV // 2)]


def _to_chips(sums, lands):
    return [Copy(s, lambda me, p=p: jnp.bitwise_xor(_chip(me), p // 2), d, _chip, p, 0, 4 * i + p // 2)
            for i, (s, d) in enumerate(zip(sums, lands)) for p in SAME_CORE]


def _comm_call(name, bufs, wait=None, start=None, after=()):
    nb = len(bufs)
    waits = [wait] if isinstance(wait, tuple) else list(wait or [])
    slots = list(start[1]) if start else []
    n_out_sem = 2 * len(slots)
    after = [a for a in (after if isinstance(after, (list, tuple)) else [after]) if a is not None]

    def body(*refs):
        b = refs[:nb]
        at = nb + 2 * len(waits) + len(after)
        out_sems = refs[at:at + n_out_sem]
        token = refs[at + n_out_sem + nb]
        x, y, c, me = _mesh_position()

        def part(i, row_of, sender):
            return b[i] if row_of is None else b[i].at[row_of(sender)]

        for k, (_, copies) in enumerate(waits):
            w_send, w_recv = refs[nb + 2 * k], refs[nb + 2 * k + 1]
            for cp in copies:
                peer, peer_index = _peer(x, y, c, cp.peer)
                arrival = pltpu.make_async_remote_copy(part(cp.src, cp.src_at, me), part(cp.dst, cp.dst_at, peer_index),
                                                       w_send.at[cp.slot], w_recv.at[cp.slot],
                                                       device_id=peer, device_id_type=MESH)
                arrival.wait_send()
                arrival.wait_recv()
        if start:
            for cp in start[0]:
                peer, _ = _peer(x, y, c, cp.peer)
                pltpu.make_async_remote_copy(part(cp.src, cp.src_at, me), part(cp.dst, cp.dst_at, me),
                                             out_sems[2 * cp.group].at[cp.slot], out_sems[2 * cp.group + 1].at[cp.slot],
                                             device_id=peer, device_id_type=MESH).start()
        token[...] = jnp.zeros_like(token)

    sem_shapes = []
    for n_slots in slots:
        sem_shapes += [pltpu.SemaphoreType.DMA((n_slots,))] * 2
    operands = [pltpu.with_memory_space_constraint(a, pltpu.HBM) for a in bufs]
    in_specs = [HBM_SPEC] * nb
    for sems, _ in waits:
        operands += list(sems)
        in_specs += [SEM_SPEC, SEM_SPEC]
    operands += after
    in_specs += [pl.BlockSpec(memory_space=pl.ANY)] * len(after)
    outs = pl.pallas_call(
        body, name=name, in_specs=in_specs,
        out_specs=[SEM_SPEC] * n_out_sem + [HBM_SPEC] * nb + [pl.BlockSpec(memory_space=pltpu.VMEM)],
        out_shape=sem_shapes + [pltpu.HBM(a.shape, a.dtype) for a in bufs] + [jax.ShapeDtypeStruct((8, SLAB), F32)],
        input_output_aliases={i: n_out_sem + i for i in range(nb)},
        compiler_params=pltpu.CompilerParams(has_side_effects=DATAFLOW),
    )(*operands)
    sems = [(outs[2 * k], outs[2 * k + 1]) for k in range(len(slots))]
    return sems, list(outs[n_out_sem:n_out_sem + nb]), outs[-1]


def _pair_sum(stacks, lands, place):
    n = len(stacks)

    def body(place_ref, *refs):
        k = pl.program_id(0)
        for m in range(n):
            mine, theirs, out, land = refs[m], refs[n + m], refs[2 * n + m], refs[3 * n + m]
            total = (mine[0, 0].astype(F32) + theirs[0].astype(F32)).astype(out.dtype)
            out[0] = total

            @pl.when(k == place_ref[1])
            def _():
                land[0] = total

    in_specs = [pl.BlockSpec((1, 1) + a.shape[1:], lambda k, place_ref: (k, place_ref[0], 0, 0)) for a in stacks]
    in_specs += [pl.BlockSpec((1,) + a.shape[1:], lambda k, place_ref: (k, 0, 0)) for a in lands]
    out_specs = [pl.BlockSpec((1,) + a.shape[1:], lambda k, place_ref: (k, 0, 0)) for a in lands]
    out_specs += [pl.BlockSpec((1,) + a.shape[1:], lambda k, place_ref: (place_ref[1], 0, 0)) for a in lands]
    outs = pl.pallas_call(
        body, name="pair_sum_" + "_".join(str(a.shape[1]) for a in stacks),
        grid_spec=pltpu.PrefetchScalarGridSpec(num_scalar_prefetch=1, grid=(N_DEV // 2,), in_specs=in_specs,
                                               out_specs=out_specs),
        out_shape=[pltpu.HBM(a.shape, a.dtype) for a in lands] * 2,
        compiler_params=_params(40, 1),
    )(place, *_in_hbm(*[a.reshape((N_DEV // 2, 2) + a.shape[1:]) for a in stacks], *lands))
    return list(outs[:n]), list(outs[n:])


def _reduce_adam(parts, w, m, v, name):
    rows, cols = w.shape
    n_parts = parts.shape[0]
    rb = rows
    for cand in (256, 128):
        if rows % cand == 0 and rows > cand:
            rb = cand
            break

    def body(p_ref, w_ref, m_ref, v_ref, g_out, d_out, m_out, v_out):
        g = p_ref[0].astype(F32)
        for j in range(1, n_parts):
            g = g + p_ref[j].astype(F32)
        g_out[...] = g
        d_out[...], m_out[...], v_out[...] = _adam(g, w_ref[...], m_ref[...], v_ref[...])

    blk = pl.BlockSpec((rb, cols), lambda i: (i, 0))
    out = jax.ShapeDtypeStruct((rows, cols), F32)
    return pl.pallas_call(
        body, name=name, grid=(rows // rb,),
        in_specs=[pl.BlockSpec((n_parts, rb, cols), lambda i: (0, i, 0)), blk, blk, blk],
        out_specs=[blk] * 4, out_shape=[out] * 4,
        compiler_params=_params(40, 1),
    )(*_in_hbm(parts, w, m, v))


def _cols_from_stack(stack):
    n, r, c = stack.shape
    return jnp.transpose(stack, (1, 0, 2)).reshape(r, n * c)


def _block_diag(w):
    z = jnp.zeros((N_SLAB, 64, 64), w.dtype)
    pairs = w.reshape(N_SLAB, 2, 64, 64)
    top = jnp.concatenate([pairs[:, 0], z], axis=2)
    bottom = jnp.concatenate([z, pairs[:, 1]], axis=2)
    return jnp.concatenate([top, bottom], axis=1)


def _adam(g, w, m, v):
    m_new = ADAM_B1 * m + (1.0 - ADAM_B1) * g
    v_new = ADAM_B2 * v + (1.0 - ADAM_B2) * (g * g)
    m_hat = m_new / (1.0 - ADAM_B1 ** ADAM_STEP)
    v_hat = v_new / (1.0 - ADAM_B2 ** ADAM_STEP)
    return (-ADAM_LR) * (m_hat / (jnp.sqrt(v_hat) + ADAM_EPS) + ADAM_WD * w), m_new, v_new


WIDE = ("ln2_g", "lnf_g")
HALF = ("pool_scale", "conv_b", "b_a", "b_i", "lam", "gn_pool_g", "gn_lru_g")
VECTORS = [(k, D_MODEL) for k in WIDE] + [(k, 512) for k in HALF]
VECTOR_ROWS = sum(width // SLAB for _, width in VECTORS)
LOSS_ROW = -(-VECTOR_ROWS // 8) * 8
CONV_AT = LOSS_ROW + 8
CONV_LANES = LRU_WIDTH // SLAB
PACK_F_ROWS = CONV_AT + CONV_WIDTH * CONV_LANES
MATRIX_ROWS = N_SLAB * SLAB
HEAD = SLAB // 2
GATE_ROWS = N_SLAB * HEAD
PACK_B_ROWS = MATRIX_ROWS + 2 * GATE_ROWS


def _pack_small(vectors, pool_g, wa_g, wi_g, conv, sq):
    n_vec = len(vectors)

    def body(*refs):
        vec = refs[:n_vec]
        pw_ref, wa_ref, wi_ref, cw_ref, sq_ref, out, out_b = refs[n_vec:]
        out[...] = jnp.zeros_like(out)
        row = 0
        for ref, (_, width) in zip(vec, VECTORS):
            for k in range(width // SLAB):
                out[row:row + 1, :] = ref[:, k * SLAB:(k + 1) * SLAB]
                row += 1
        for tap in range(CONV_WIDTH):
            for k in range(CONV_LANES):
                at = CONV_AT + tap * CONV_LANES + k
                out[at:at + 1, :] = cw_ref[tap:tap + 1, k * SLAB:(k + 1) * SLAB]
        total = sq_ref[:, 0:SLAB]
        for k in range(1, D_MODEL // SLAB):
            total = total + sq_ref[:, k * SLAB:(k + 1) * SLAB]
        out[LOSS_ROW:LOSS_ROW + 1, :] = total
        left = lax.broadcasted_iota(jnp.int32, (HEAD, SLAB), 1) < HEAD
        for s in range(N_SLAB):
            out_b[s * SLAB:(s + 1) * SLAB, :] = pw_ref[s].astype(BF16)
            for i, ref in enumerate((wa_ref, wi_ref)):
                at = MATRIX_ROWS + i * GATE_ROWS + s * HEAD
                out_b[at:at + HEAD, :] = jnp.where(left, ref[s, 0:HEAD, :], ref[s, HEAD:SLAB, :]).astype(BF16)

    return pl.pallas_call(
        body, name="pack_small",
        out_shape=[jax.ShapeDtypeStruct((PACK_F_ROWS, SLAB), F32), jax.ShapeDtypeStruct((PACK_B_ROWS, SLAB), BF16)],
    )(*vectors, pool_g, wa_g, wi_g, conv, sq)


def _small_reduce_adam(parts, parts_b, vec_w, vec_m, vec_v, pool_wmv):
    n_vec = len(VECTORS)
    n_parts = parts.shape[0]

    def body(*refs):
        p_ref, pb_ref = refs[0], refs[1]
        w_refs, m_refs, v_refs = (refs[2 + k * n_vec:2 + (k + 1) * n_vec] for k in range(3))
        pw_w, pw_m, pw_v = refs[2 + 3 * n_vec:5 + 3 * n_vec]
        outs = refs[5 + 3 * n_vec:-1]
        total = refs[-1]
        total[...] = p_ref[0]
        for j in range(1, n_parts):
            total[...] += p_ref[j]
        row = 0
        for i, (_, width) in enumerate(VECTORS):
            n_rows = width // SLAB
            g = jnp.concatenate([total[row + k:row + k + 1, :] for k in range(n_rows)], axis=1)
            row += n_rows
            d, m_new, v_new = _adam(g, w_refs[i][...], m_refs[i][...], v_refs[i][...])
            for ref, val in zip(outs[4 * i:4 * i + 4], (g, d, m_new, v_new)):
                ref[...] = val
        tail = outs[4 * n_vec:]

        def summed(first, count):
            g = pb_ref[0, first:first + count, :].astype(F32)
            for j in range(1, n_parts):
                g = g + pb_ref[j, first:first + count, :].astype(F32)
            return g

        g = summed(0, MATRIX_ROWS)
        d, m_new, v_new = _adam(g, pw_w[...], pw_m[...], pw_v[...])
        for ref, val in zip(tail[0:4], (g, d, m_new, v_new)):
            ref[...] = val
        tail[4][...] = summed(MATRIX_ROWS, GATE_ROWS)
        tail[5][...] = summed(MATRIX_ROWS + GATE_ROWS, GATE_ROWS)
        for tap in range(CONV_WIDTH):
            at = CONV_AT + tap * CONV_LANES
            tail[6][tap:tap + 1, :] = jnp.concatenate([total[at + k:at + k + 1, :] for k in range(CONV_LANES)], axis=1)
        tail[7][...] = (0.5 / D_MODEL) * jnp.sum(total[LOSS_ROW:LOSS_ROW + 1, :], axis=1, keepdims=True)

    out_shape = []
    for _, width in VECTORS:
        out_shape += [jax.ShapeDtypeStruct((1, width), F32)] * 4
    out_shape += [jax.ShapeDtypeStruct((MATRIX_ROWS, SLAB), F32)] * 4 + [jax.ShapeDtypeStruct((GATE_ROWS, SLAB), F32)] * 2
    out_shape += [jax.ShapeDtypeStruct((CONV_WIDTH, LRU_WIDTH), F32), jax.ShapeDtypeStruct((1, 1), F32)]
    outs = pl.pallas_call(
        body, name="adam_small", out_shape=out_shape,
        scratch_shapes=[pltpu.VMEM((PACK_F_ROWS, SLAB), F32)],
        compiler_params=_params(40),
    )(parts, parts_b, *vec_w, *vec_m, *vec_v, *pool_wmv)
    vec_out = [tuple(outs[4 * i:4 * i + 4]) for i in range(n_vec)]
    tail = outs[4 * n_vec:]
    return vec_out, tuple(tail[0:4]), tail[4], tail[5], tail[6], tail[7]


def _plain_adam(grads, ws, ms, vs):
    n = len(grads)

    def body(*refs):
        ins, outs = refs[:4 * n], refs[4 * n:]
        for i in range(n):
            d, m_new, v_new = _adam(ins[i][...], ins[n + i][...], ins[2 * n + i][...], ins[3 * n + i][...])
            for ref, val in zip(outs[3 * i:3 * i + 3], (d, m_new, v_new)):
                ref[...] = val

    out_shape = []
    for g in grads:
        out_shape += [jax.ShapeDtypeStruct(g.shape, F32)] * 3
    outs = pl.pallas_call(body, name="adam_plain", out_shape=out_shape)(*grads, *ws, *ms, *vs)
    return [tuple(outs[3 * i:3 * i + 3]) for i in range(n)]


def kernel(x, ln1_g, w_in, pool_w, pool_scale, conv_w, conv_b, w_a, b_a, w_i, b_i, lam, gn_pool_g, gn_lru_g, w_out, ln2_g, w_ffn_gate, w_ffn_up, w_ffn_down, lnf_g, loss_target, m_ln1_g, m_w_in, m_pool_w, m_pool_scale, m_conv_w, m_conv_b, m_w_a, m_b_a, m_w_i, m_b_i, m_lam, m_gn_pool_g, m_gn_lru_g, m_w_out, m_ln2_g, m_w_ffn_gate, m_w_ffn_up, m_w_ffn_down, m_lnf_g, v_ln1_g, v_w_in, v_pool_w, v_pool_scale, v_conv_w, v_conv_b, v_w_a, v_b_a, v_w_i, v_b_i, v_lam, v_gn_pool_g, v_gn_lru_g, v_w_out, v_ln2_g, v_w_ffn_gate, v_w_ffn_up, v_w_ffn_down, v_lnf_g):
    weights = dict(ln1_g=ln1_g, w_in=w_in, pool_w=pool_w, pool_scale=pool_scale, conv_w=conv_w, conv_b=conv_b,
                   w_a=w_a, b_a=b_a, w_i=w_i, b_i=b_i, lam=lam, gn_pool_g=gn_pool_g, gn_lru_g=gn_lru_g,
                   w_out=w_out, ln2_g=ln2_g, w_ffn_gate=w_ffn_gate, w_ffn_up=w_ffn_up, w_ffn_down=w_ffn_down,
                   lnf_g=lnf_g)
    mom1 = dict(ln1_g=m_ln1_g, w_in=m_w_in, pool_w=m_pool_w, pool_scale=m_pool_scale, conv_w=m_conv_w,
                conv_b=m_conv_b, w_a=m_w_a, b_a=m_b_a, w_i=m_w_i, b_i=m_b_i, lam=m_lam, gn_pool_g=m_gn_pool_g,
                gn_lru_g=m_gn_lru_g, w_out=m_w_out, ln2_g=m_ln2_g, w_ffn_gate=m_w_ffn_gate,
                w_ffn_up=m_w_ffn_up, w_ffn_down=m_w_ffn_down, lnf_g=m_lnf_g)
    mom2 = dict(ln1_g=v_ln1_g, w_in=v_w_in, pool_w=v_pool_w, pool_scale=v_pool_scale, conv_w=v_conv_w,
                conv_b=v_conv_b, w_a=v_w_a, b_a=v_b_a, w_i=v_w_i, b_i=v_b_i, lam=v_lam, gn_pool_g=v_gn_pool_g,
                gn_lru_g=v_gn_lru_g, w_out=v_w_out, ln2_g=v_ln2_g, w_ffn_gate=v_w_ffn_gate,
                w_ffn_up=v_w_ffn_up, w_ffn_down=v_w_ffn_down, lnf_g=v_lnf_g)

    xs = x[0]
    target = loss_target[0]

    shard = dict(w_in=lambda a: a[0].T, w_ffn_gate=lambda a: a[0].T, w_ffn_up=lambda a: a[0].T,
                 w_out=lambda a: a[0], w_ffn_down=lambda a: a[0], conv_w=lambda a: a[0])
    unshard = dict(w_in=lambda a: a.T[None], w_ffn_gate=lambda a: a.T[None], w_ffn_up=lambda a: a.T[None],
                   w_out=lambda a: a[None], w_ffn_down=lambda a: a[None], conv_w=lambda a: a[None])

    gathered = ("w_in", "conv_w", "w_out", "w_ffn_gate", "w_ffn_up", "w_ffn_down")
    groups = ((0, 1), (2,), (3, 4), (5,))
    sources = [shard[k](weights[k]) if k == "conv_w" else shard[k](weights[k]).astype(BF16) for k in gathered]
    my_index = 4 * lax.axis_index("x") + 2 * lax.axis_index("y") + lax.axis_index("c")
    lands = [lax.empty((N_DEV,) + a.shape, a.dtype) for a in sources]

    def first_hop(n, at=0):
        return _numbered(_fan_out(range(at, at + n), range(at + n, at + 2 * n), SAME_CORE + SIBLING + MYSELF, 0))

    def second_hop(n, at=0):
        return _numbered(_relay(range(at, at + n), SAME_CORE))

    start = []
    for g, members in enumerate(groups):
        start += _numbered(_fan_out(members, [6 + m for m in members], SAME_CORE + SIBLING + MYSELF, 0), g)
    sems, bufs, _ = _comm_call("gather_start", sources + lands,
                               start=(start, [sum(cp.group == g for cp in start) for g in range(len(groups))]))
    sources, lands = bufs[:6], bufs[6:]

    def group_buffers(g):
        return [sources[m] for m in groups[g]] + [lands[m] for m in groups[g]]

    def relayed(tag, g, after):
        n = len(groups[g])
        relay = second_hop(n, n)
        relay_sems, bufs, _ = _comm_call("gather_relay_" + tag, group_buffers(g), wait=(sems[g], first_hop(n)),
                                         start=(relay, (len(relay),)), after=after)
        return relay_sems[0], bufs[n:]

    relay_sem, bufs = relayed("in", 0, None)
    g_in, g_conv = _comm_call("gather_wait_in", bufs, wait=(relay_sem, second_hop(2)))[1]
    w_in_f = g_in.reshape(IN_WIDTH, D_MODEL)
    conv_w_f = _cols_from_stack(g_conv)

    wa_bd = _block_diag(w_a[0])
    wi_bd = _block_diag(w_i[0])
    lnf_row = lnf_g.reshape(1, D_MODEL)

    u_pool, u_lru, u_gate = _fwd_in(xs, ln1_g, w_in_f)
    y_pool, h, y_lru = _mixer_fwd(u_pool, u_lru, u_gate, pool_w[0], pool_scale, conv_w_f, conv_b,
                                  wa_bd, b_a, wi_bd, b_i, lam)
    relay_sem, bufs = relayed("out", 1, y_pool)
    gate_up = second_hop(2, 3)
    (gate_up_sem,), bufs, _ = _comm_call("gather_wait_out", bufs + group_buffers(2),
                                         wait=[(relay_sem, second_hop(1)), (sems[2], first_hop(2, 1))],
                                         start=(gate_up, (len(gate_up),)))
    g_out, gate_up_lands = bufs[0], bufs[3:]
    w_out_f = g_out.reshape(D_MODEL, D_MODEL)
    h1, n2 = _fwd_out(xs, y_pool, y_lru, gn_pool_g, gn_lru_g, w_out_f, ln2_g)
    relay_sem, bufs = relayed("down", 3, n2)
    g_gate, g_up, g_down = _comm_call("gather_wait_ffn", gate_up_lands + bufs,
                                      wait=[(gate_up_sem, second_hop(2)), (relay_sem, second_hop(1, 2))])[1]
    w_gate_f = g_gate.reshape(D_FF, D_MODEL)
    w_up_f = g_up.reshape(D_FF, D_MODEL)
    w_down_f = g_down.reshape(D_FF, D_MODEL)
    g_act, u_act, dh2, dh2b, d_lnf, sq = _ffn_fwd(h1, n2, target, lnf_row, w_gate_f, w_up_f, w_down_f)

    def direct(tag, stacks, wholes, after):
        sources = list(stacks) + list(wholes)
        n, n_st = len(sources), len(stacks)
        lands = [lax.empty(a.shape if i < n_st else (N_DEV,) + a.shape, a.dtype) for i, a in enumerate(sources)]
        copies = _numbered(_scatter(range(n_st), range(n, n + n_st), EVERYONE + MYSELF)
                           + _fan_out(range(n_st, n), range(n + n_st, 2 * n), EVERYONE + MYSELF, 0))
        sem, bufs, token = _comm_call(tag + "_start", sources + lands, start=(copies, (len(copies),)), after=after)
        return (tag, sem[0], bufs, copies), token

    def direct_finish(started, after):
        tag, sem, bufs, copies = started
        _, bufs, _ = _comm_call(tag + "_wait", bufs, wait=(sem, copies), after=after)
        return bufs[len(bufs) // 2:]

    place = jnp.stack([lax.axis_index("c"), 2 * lax.axis_index("x") + lax.axis_index("y")]).astype(jnp.int32)

    d_gate, d_up, d_down, dn2 = _ffn_bwd(n2, dh2b, g_act, u_act, w_gate_f, w_up_f, w_down_f)
    down_stack, *ffn_stacks = [d.reshape(N_DEV, D_FF // N_DEV, D_MODEL) for d in (d_down, d_gate, d_up)]
    pair_lands = [lax.empty((N_DEV // 2,) + a.shape[1:], a.dtype) for a in ffn_stacks]
    pair_copies = _numbered(_to_sibling(range(2), range(2, 4)))
    down_copies = _numbered(_scatter([4], [5], EVERYONE + MYSELF), 1)
    (sem, down_sem), bufs, token = _comm_call(
        "ffn_grads_start", ffn_stacks + pair_lands + [down_stack, lax.empty(down_stack.shape, down_stack.dtype)],
        start=(pair_copies + down_copies, (len(pair_copies), len(down_copies))))
    bufs, down_bufs = bufs[:4], bufs[4:]
    dh1, dy_pool, dy_lru, d_out, d_ln2, d_gnp, d_gnl = _bwd_out(dn2, dh2, h1, y_pool, y_lru, gn_pool_g, gn_lru_g,
                                                                 w_out_f, ln2_g, token)
    _, bufs, _ = _comm_call("ffn_pair_wait", bufs, wait=(sem, pair_copies), after=dh1)
    ffn_sums, ffn_lands = _pair_sum(bufs[:2], bufs[2:], place)
    ffn_copies = _numbered(_to_chips(range(2), range(2, 4)))
    ffn_sem, ffn_bufs, token = _comm_call("ffn_chip_start", ffn_sums + ffn_lands, start=(ffn_copies, (len(ffn_copies),)))
    (du_pool, du_lru, du_gate, d_pw, d_ps, d_cw, d_cb, d_wa, d_ba, d_wi, d_bi, d_lam) = _mixer_bwd(
        u_pool, u_lru, u_gate, h, dy_pool, dy_lru, pool_w[0], pool_scale, conv_w_f, conv_b,
        wa_bd, b_a, wi_bd, b_i, lam, token)

    vec_grads = dict(ln2_g=d_ln2, lnf_g=d_lnf, pool_scale=d_ps, conv_b=d_cb, b_a=d_ba, b_i=d_bi,
                     lam=d_lam, gn_pool_g=d_gnp, gn_lru_g=d_gnl)
    packed, packed_b = _pack_small([vec_grads[k] for k, _ in VECTORS], d_pw, d_wa, d_wi, d_cw, sq)
    small_started, token = direct("small", [d_out.reshape(N_DEV, D_MODEL // N_DEV, D_MODEL)], [packed, packed_b], None)
    grad_x, d_in, d_ln1 = _bwd_in(xs, dh1, du_pool, du_lru, du_gate, ln1_g, w_in_f, token)
    in_started, token = direct("in", [d_in.reshape(N_DEV, IN_WIDTH // N_DEV, D_MODEL)], [d_ln1], None)

    results = {}

    def reduce_adam(name, parts):
        outs = _reduce_adam(parts, shard[name](weights[name]), shard[name](mom1[name]), shard[name](mom2[name]),
                            "adam_" + name)
        results[name] = tuple(unshard[name](o) for o in outs)
        return outs[0]

    shard["ln1_g"] = unshard["ln1_g"] = lambda a: a
    _, ffn_bufs, _ = _comm_call("ffn_chip_wait", ffn_bufs, wait=(ffn_sem[0], ffn_copies), after=token)
    done = [reduce_adam(name, parts) for name, parts in zip(("w_ffn_gate", "w_ffn_up"), ffn_bufs[2:])]
    _, (_, r_down), _ = _comm_call("down_wait", down_bufs, after=done,
                                   wait=(down_sem, _numbered(_scatter([0], [1], EVERYONE + MYSELF))))
    done.append(reduce_adam("w_ffn_down", r_down))
    r_out, r_small, r_small_b = direct_finish(small_started, done)
    done.append(reduce_adam("w_out", r_out))

    def as_row(a, width):
        return a.reshape(1, width)

    def as_matrix(a):
        return a.reshape(MATRIX_ROWS, SLAB)

    def as_heads(a):
        return a.reshape(2 * GATE_ROWS, HEAD)

    def heads_apart(g):
        return jnp.transpose(g.reshape(N_SLAB, HEAD, 2, HEAD), (0, 2, 1, 3)).reshape(2 * GATE_ROWS, HEAD)

    vec_out, pool_out, g_wa, g_wi, g_conv, loss_11 = _small_reduce_adam(
        r_small, r_small_b, [as_row(weights[k], w) for k, w in VECTORS], [as_row(mom1[k], w) for k, w in VECTORS],
        [as_row(mom2[k], w) for k, w in VECTORS], [as_matrix(t["pool_w"]) for t in (weights, mom1, mom2)])
    for (k, _), outs in zip(VECTORS, vec_out):
        results[k] = tuple(o.reshape(weights[k].shape) for o in outs)
    results["pool_w"] = tuple(o.reshape(pool_w.shape) for o in pool_out)
    my_columns = conv_w.shape[-1]
    plain_names = ("w_a", "w_i", "conv_w")
    plain_grads = [heads_apart(g) for g in (g_wa, g_wi)]
    plain_grads.append(lax.dynamic_slice_in_dim(g_conv, my_index * my_columns, my_columns, axis=1))
    views = (as_heads, as_heads, lambda a: a[0])
    plain_out = _plain_adam(plain_grads, *[[view(t[k]) for k, view in zip(plain_names, views)]
                                           for t in (weights, mom1, mom2)])
    for k, g, outs in zip(plain_names, plain_grads, plain_out):
        results[k] = tuple(o.reshape(weights[k].shape) for o in (g,) + outs)
    loss = loss_11[0, 0]
    r_in, r_ln1 = direct_finish(in_started, done + [plain_out[0][0], loss_11])
    reduce_adam("w_in", r_in)
    reduce_adam("ln1_g", r_ln1)

    order = ["ln1_g", "w_in", "pool_w", "pool_scale", "conv_w", "conv_b", "w_a", "b_a", "w_i", "b_i", "lam",
             "gn_pool_g", "gn_lru_g", "w_out", "ln2_g", "w_ffn_gate", "w_ffn_up", "w_ffn_down", "lnf_g"]
    return (loss, grad_x[None],
            *[results[k][0] for k in order], *[results[k][1] for k in order],
            *[results[k][2] for k in order], *[results[k][3] for k in order])
```

```python
from typing import Any, NamedTuple

import jax
import jax.numpy as jnp
from jax import lax
from jax.experimental import pallas as pl
from jax.experimental.pallas import tpu as pltpu

F32 = jnp.float32
BF16 = jnp.bfloat16

N_DEV = 8
D_MODEL = 1024
POOL_WIDTH = 512
LRU_WIDTH = 512
IN_WIDTH = 1536
D_FF = 2816
N_SLAB = 4
SLAB = 128
CONV_WIDTH = 4
LRU_C = 8.0
EPS = 1e-6
HALO = 16
FF_CHUNK = 256

ADAM_LR = 0.001
ADAM_B1 = 0.9
ADAM_B2 = 0.999
ADAM_EPS = 1e-08
ADAM_WD = 0.01
ADAM_STEP = 10

MIB = 1 << 20
MESH = pl.DeviceIdType.MESH


def _params(vmem_mib, n_axes=0):
    sem = ("arbitrary",) * n_axes if n_axes else None
    return pltpu.CompilerParams(dimension_semantics=sem, vmem_limit_bytes=vmem_mib * MIB)


def _in_hbm(*arrays):
    return [pltpu.with_memory_space_constraint(a, pltpu.HBM) for a in arrays]


def _mm(a, b):
    return jnp.dot(a, b, preferred_element_type=F32)


def _mm_nt(a, b):
    return lax.dot_general(a, b, (((1,), (1,)), ((), ())), preferred_element_type=F32)


def _mm_tn(a, b):
    return lax.dot_general(a, b, (((0,), (0,)), ((), ())), preferred_element_type=F32)


def _rms(x, g):
    rstd = lax.rsqrt(jnp.mean(x * x, axis=-1, keepdims=True) + EPS)
    xhat = x * rstd
    return xhat * g, xhat, rstd


def _rms_bwd(dy, xhat, rstd, g):
    gy = dy * g
    dx = rstd * (gy - xhat * jnp.mean(gy * xhat, axis=-1, keepdims=True))
    return dx, jnp.sum(dy * xhat, axis=0, keepdims=True)


def _gelu(z):
    t = jnp.tanh(0.7978845608028654 * (z + 0.044715 * z * z * z))
    return 0.5 * z * (1.0 + t), t


def _gelu_grad(z, t):
    return 0.5 * (1.0 + t) + 0.5 * z * (1.0 - t * t) * 0.7978845608028654 * (1.0 + 3.0 * 0.044715 * z * z)


def _softplus_neg(lam):
    x = -lam
    e = jnp.exp(-jnp.abs(x))
    u = 1.0 + e
    l1p = jnp.where(u == 1.0, e, jnp.log(u) * e / (u - 1.0))
    return jnp.maximum(x, 0.0) + l1p


def _one_minus_square(la, a):
    x = 2.0 * la
    series = -x * (1.0 + x * (0.5 + x * (1.0 / 6.0 + x * (1.0 / 24.0))))
    return jnp.where(x > -0.06, series, 1.0 - a * a)


def _sigmoid(x):
    return 0.5 * jnp.tanh(0.5 * x) + 0.5


def _down(v, d):
    return pltpu.roll(v, d, 0)


def _up(v, d):
    return pltpu.roll(v, v.shape[0] - d, 0)


def _slab_major(s, dtype=F32):
    return pltpu.HBM((N_SLAB, s, SLAB), dtype)


def _slabs_spec(tb):
    return pl.BlockSpec((N_SLAB, tb, SLAB), lambda i: (0, i, 0))


def _read_slabs(ref):
    return jnp.concatenate([ref[k] for k in range(N_SLAB)], axis=1)


def _write_slabs(ref, value):
    for k in range(N_SLAB):
        ref[k] = value[:, k * SLAB:(k + 1) * SLAB]


def _token_block(s, most=512):
    for rows in (most, 512, 256):
        if rows <= most and s % rows == 0 and s > rows:
            return rows
    return s


def _time_chunk(s):
    return 512 if s % 512 == 0 and s > 512 else 256 if s % 256 == 0 else s


def _fwd_in(x, ln1_g, w_in_t):
    s = x.shape[0]
    tb = _token_block(s, 1024)

    def body(x_ref, g_ref, w_ref, up_ref, ul_ref, ug_ref):
        n, _, _ = _rms(x_ref[...], g_ref[...])
        proj = _mm_nt(n.astype(BF16), w_ref[...])
        _write_slabs(up_ref, proj[:, :POOL_WIDTH])
        _write_slabs(ul_ref, proj[:, POOL_WIDTH:POOL_WIDTH + LRU_WIDTH])
        _write_slabs(ug_ref, proj[:, POOL_WIDTH + LRU_WIDTH:])

    out = _slab_major(s)
    return pl.pallas_call(
        body, name="fwd_in", grid=(s // tb,),
        in_specs=[pl.BlockSpec((tb, D_MODEL), lambda i: (i, 0)),
                  pl.BlockSpec((1, D_MODEL), lambda i: (0, 0)),
                  pl.BlockSpec((IN_WIDTH, D_MODEL), lambda i: (0, 0))],
        out_specs=[_slabs_spec(tb)] * 3,
        out_shape=[out, out, out],
        compiler_params=_params(40, 1),
    )(*_in_hbm(x, ln1_g, w_in_t))


def _pool_denominator(t0, row, window):
    return jnp.minimum((t0 + row + 1).astype(F32), window)


def _causal_window(ext, deeper):
    s = ext + _down(ext, 1)
    s = s + deeper[0] * _down(s, 2)
    s = s + deeper[1] * _down(s, 4)
    s = s + deeper[2] * _down(s, 8)
    return s[HALO:]


def _anticausal_window(ext, deeper, rows):
    s = ext + _up(ext, 1)
    s = s + deeper[0] * _up(s, 2)
    s = s + deeper[1] * _up(s, 4)
    s = s + deeper[2] * _up(s, 8)
    return s[:rows]


def _conv(taps, cw, cb):
    return cw[3] * taps[0] + cw[2] * taps[1] + cw[1] * taps[2] + cw[0] * taps[3] + cb


def _lru_gates(xc, wa, ba, wi, bi, sp):
    xb = xc.astype(BF16)
    r = _sigmoid(_mm(xb, wa) + ba)
    i = _sigmoid(_mm(xb, wi) + bi)
    la = (-LRU_C) * r * sp
    a = jnp.exp(la)
    mult = jnp.sqrt(jnp.maximum(_one_minus_square(la, a), 0.0))
    return xb, r, i, a, mult


SUBLANES = 8


def _scan_pads(a_pad, b_pad, rows, causal):
    spare = slice(0, SUBLANES) if causal else slice(rows, rows + SUBLANES)
    a_pad[spare, :] = jnp.ones((SUBLANES, SLAB), F32)
    b_pad[spare, :] = jnp.zeros((SUBLANES, SLAB), F32)


def _scan_causal(a, b, h_prev, a_pad, b_pad, rows):
    d = 1
    while d < min(SUBLANES, rows):
        a_pad[SUBLANES:, :] = a
        b_pad[SUBLANES:, :] = b
        b = a * b_pad[SUBLANES - d:SUBLANES - d + rows, :] + b
        a = a * a_pad[SUBLANES - d:SUBLANES - d + rows, :]
        d *= 2
    while d < rows:
        b = jnp.concatenate([b[:d], a[d:] * b[:-d] + b[d:]], axis=0)
        a = jnp.concatenate([a[:d], a[d:] * a[:-d]], axis=0)
        d *= 2
    return b + a * h_prev


def _scan_anticausal(a, b, l_next, a_pad, b_pad, rows):
    d = 1
    while d < min(SUBLANES, rows):
        a_pad[:rows, :] = a
        b_pad[:rows, :] = b
        b = a * b_pad[d:d + rows, :] + b
        a = a * a_pad[d:d + rows, :]
        d *= 2
    while d < rows:
        b = jnp.concatenate([a[:-d] * b[d:] + b[:-d], b[-d:]], axis=0)
        a = jnp.concatenate([a[:-d] * a[d:], a[-d:]], axis=0)
        d *= 2
    return b + a * l_next


def _slab_scalars():
    slab = pl.program_id(0)
    deeper = [jnp.where(slab > k, 1.0, 0.0).astype(F32) for k in range(N_SLAB - 1)]
    window = jnp.left_shift(jnp.int32(2), slab).astype(F32)
    inverse = jnp.where(slab == 0, 0.5, jnp.where(slab == 1, 0.25, jnp.where(slab == 2, 0.125, 0.0625))).astype(F32)
    return deeper, (window, inverse)


def _window_mean(total, t0, row, window, at_start):
    if at_start:
        return total / _pool_denominator(t0, row, window[0])
    return total * window[1]


def _slab_specs(s):
    seq = pl.BlockSpec((None, s, SLAB), lambda k: (k, 0, 0))
    mat = pl.BlockSpec((1, SLAB, SLAB), lambda k: (k, 0, 0))
    vec = pl.BlockSpec((1, SLAB), lambda k: (0, k))
    taps = pl.BlockSpec((CONV_WIDTH, SLAB), lambda k: (0, k))
    return seq, mat, vec, taps


def _mixer_fwd(u_pool, u_lru, u_gate, pool_w, pool_scale, conv_w, conv_b, wa_bd, b_a, wi_bd, b_i, lam):
    s = u_pool.shape[1]
    tc = _time_chunk(s)
    n_chunks = s // tc

    def body(up_ref, ul_ref, ug_ref, pw_ref, ps_ref, cw_ref, cb_ref, wa_ref, ba_ref, wi_ref, bi_ref, lam_ref,
             yp_ref, h_ref, yl_ref, shift_ref, a_pad, b_pad):
        _scan_pads(a_pad, b_pad, tc, causal=True)
        deeper, window = _slab_scalars()
        pw = pw_ref[0].astype(BF16)
        wa = wa_ref[0].astype(BF16)
        wi = wi_ref[0].astype(BF16)
        ps, cb, ba, bi = ps_ref[...], cb_ref[...], ba_ref[...], bi_ref[...]
        cw = [cw_ref[k:k + 1, :] for k in range(CONV_WIDTH)]
        sp = _softplus_neg(lam_ref[...])
        row = lax.broadcasted_iota(jnp.int32, (tc, SLAB), 0)

        def chunk(t0, ext_p, ext_l, h_prev, at_start=False):
            rows = pl.ds(t0, tc)
            d = _window_mean(_causal_window(ext_p, deeper), t0, row, window, at_start) - ext_p[HALO:]
            yp_ref[rows, :] = _mm(d.astype(BF16), pw) * ps
            shift_ref[...] = ext_l
            xc = _conv([shift_ref[HALO - k:HALO - k + tc, :] for k in range(CONV_WIDTH)], cw, cb)
            _, _, i, a, mult = _lru_gates(xc, wa, ba, wi, bi, sp)
            h = _scan_causal(a, mult * (i * xc), h_prev, a_pad, b_pad, tc)
            h_ref[rows, :] = h
            yl_ref[rows, :] = h * _gelu(ug_ref[rows, :])[0]
            return h[tc - 1:tc, :]

        pad = jnp.zeros((HALO, SLAB), F32)
        h0 = chunk(0, jnp.concatenate([pad, up_ref[pl.ds(0, tc), :]], axis=0),
                   jnp.concatenate([pad, ul_ref[pl.ds(0, tc), :]], axis=0), jnp.zeros((1, SLAB), F32), at_start=True)

        def step(c, h_prev):
            t0 = pl.multiple_of(c * tc, tc)
            ext = pl.ds(pl.multiple_of(c * tc - HALO, HALO), tc + HALO)
            return chunk(t0, up_ref[ext, :], ul_ref[ext, :], h_prev)

        lax.fori_loop(1, n_chunks, step, h0)

    seq, mat, vec, taps = _slab_specs(s)
    out = _slab_major(s)
    return pl.pallas_call(
        body, name="mixer_fwd", grid=(N_SLAB,),
        in_specs=[seq, seq, seq, mat, vec, taps, vec, mat, vec, mat, vec, vec],
        out_specs=[seq, seq, seq], out_shape=[out, out, out],
        scratch_shapes=[pltpu.VMEM((tc + HALO, SLAB), F32), pltpu.VMEM((tc + SUBLANES, SLAB), F32),
                        pltpu.VMEM((tc + SUBLANES, SLAB), F32)],
        compiler_params=_params(48, 1),
    )(*_in_hbm(u_pool, u_lru, u_gate, pool_w, pool_scale, conv_w, conv_b, wa_bd, b_a, wi_bd, b_i, lam))


def _fwd_out(x, y_pool, y_lru, gn_pool_g, gn_lru_g, w_out, ln2_g):
    s = x.shape[0]
    tb = _token_block(s, 1024)

    def body(x_ref, yp_ref, yl_ref, gp_ref, gl_ref, w_ref, g2_ref, h1_ref, n2_ref):
        mp, _, _ = _rms(_read_slabs(yp_ref), gp_ref[...])
        ml, _, _ = _rms(_read_slabs(yl_ref), gl_ref[...])
        h1 = x_ref[...] + _mm(mp.astype(BF16), w_ref[:POOL_WIDTH, :]) + _mm(ml.astype(BF16), w_ref[POOL_WIDTH:, :])
        h1_ref[...] = h1
        n2_ref[...] = _rms(h1, g2_ref[...])[0].astype(BF16)

    row = pl.BlockSpec((tb, D_MODEL), lambda i: (i, 0))
    half = _slabs_spec(tb)
    return pl.pallas_call(
        body, name="fwd_out", grid=(s // tb,),
        in_specs=[row, half, half, pl.BlockSpec((1, 512), lambda i: (0, 0)), pl.BlockSpec((1, 512), lambda i: (0, 0)),
                  pl.BlockSpec((D_MODEL, D_MODEL), lambda i: (0, 0)), pl.BlockSpec((1, D_MODEL), lambda i: (0, 0))],
        out_specs=[row, row],
        out_shape=[pltpu.HBM((s, D_MODEL), F32), pltpu.HBM((s, D_MODEL), BF16)],
        compiler_params=_params(40, 1),
    )(*_in_hbm(x, y_pool, y_lru, gn_pool_g, gn_lru_g, w_out, ln2_g))


def _ffn_fwd(h1, n2, target, lnf_g, w_gate, w_up, w_down):
    s = h1.shape[0]
    tb = 512
    sub = 256
    n_ff = D_FF // FF_CHUNK

    def body(h1_ref, n2_ref, t_ref, gf_ref, wg_hbm, wu_hbm, wd_hbm,
             g_ref, u_ref, dh_ref, dhb_ref, dgf_ref, sq_ref, wg, wu, wd, sem):
        @pl.when(pl.program_id(0) == 0)
        def _():
            loads = [pltpu.make_async_copy(src, dst, sem.at[k])
                     for k, (src, dst) in enumerate(((wg_hbm, wg), (wu_hbm, wu), (wd_hbm, wd)))]
            for cp in loads:
                cp.start()
            for cp in loads:
                cp.wait()
            dgf_ref[...] = jnp.zeros_like(dgf_ref)
            sq_ref[...] = jnp.zeros_like(sq_ref)

        n2v = n2_ref[...]
        acc = jnp.zeros((tb, D_MODEL), F32)
        for c in range(n_ff):
            cols = slice(c * FF_CHUNK, (c + 1) * FF_CHUNK)
            g = _mm_nt(n2v, wg[cols, :])
            u = _mm_nt(n2v, wu[cols, :])
            g_ref[:, cols] = g.astype(BF16)
            u_ref[:, cols] = u.astype(BF16)
            act = g * jax.nn.sigmoid(g) * u
            acc = acc + _mm(act.astype(BF16), wd[cols, :])
        gf = gf_ref[...]
        for r in range(tb // sub):
            rows = slice(r * sub, (r + 1) * sub)
            y, xhat, rstd = _rms(h1_ref[rows, :] + acc[rows, :], gf)
            err = y - t_ref[rows, :]
            sq_ref[...] += jnp.sum(err * err, axis=0, keepdims=True)
            dh2, dgf = _rms_bwd(err * (1.0 / D_MODEL), xhat, rstd, gf)
            dgf_ref[...] += dgf
            dh_ref[rows, :] = dh2
            dhb_ref[rows, :] = dh2.astype(BF16)

    row = pl.BlockSpec((tb, D_MODEL), lambda i: (i, 0))
    ff = pl.BlockSpec((tb, D_FF), lambda i: (i, 0))
    vec = pl.BlockSpec((1, D_MODEL), lambda i: (0, 0))
    anyspace = pl.BlockSpec(memory_space=pl.ANY)
    return pl.pallas_call(
        body, name="ffn_fwd", grid=(s // tb,),
        in_specs=[row, row, row, vec, anyspace, anyspace, anyspace],
        out_specs=[ff, ff, row, row, vec, vec],
        out_shape=[pltpu.HBM((s, D_FF), BF16), pltpu.HBM((s, D_FF), BF16),
                   pltpu.HBM((s, D_MODEL), F32), pltpu.HBM((s, D_MODEL), BF16),
                   jax.ShapeDtypeStruct((1, D_MODEL), F32), jax.ShapeDtypeStruct((1, D_MODEL), F32)],
        scratch_shapes=[pltpu.VMEM((D_FF, D_MODEL), BF16), pltpu.VMEM((D_FF, D_MODEL), BF16),
                        pltpu.VMEM((D_FF, D_MODEL), BF16), pltpu.SemaphoreType.DMA((3,))],
        compiler_params=_params(60, 1),
    )(*_in_hbm(h1, n2, target, lnf_g, w_gate, w_up, w_down))


def _ffn_bwd(n2, dh2b, g, u, w_gate_t, w_up_t, w_down):
    s = n2.shape[0]
    tb = min(1024, s)
    n_ff = D_FF // FF_CHUNK
    n_tb = s // tb

    def body(n2_ref, dh_ref, g_ref, u_ref, wg_ref, wu_ref, wd_ref, dwg_ref, dwu_ref, dwd_ref, dn2_ref,
             dn2_acc, acc_g, acc_u, acc_d):
        j = pl.program_id(0)
        t = pl.program_id(1)
        rows = pl.ds(pl.multiple_of(t * tb, tb), tb)

        @pl.when(t == 0)
        def _():
            acc_g[...] = jnp.zeros_like(acc_g)
            acc_u[...] = jnp.zeros_like(acc_u)
            acc_d[...] = jnp.zeros_like(acc_d)

        @pl.when(j == 0)
        def _():
            dn2_acc[rows, :] = jnp.zeros((tb, D_MODEL), F32)

        n2v = n2_ref[...]
        dh = dh_ref[...]
        gv = g_ref[...].astype(F32)
        uv = u_ref[...].astype(F32)
        sg = jax.nn.sigmoid(gv)
        silu = gv * sg
        dact = _mm_nt(dh, wd_ref[...])
        dub = (dact * silu).astype(BF16)
        dgb = (dact * uv * (sg * (1.0 + gv * (1.0 - sg)))).astype(BF16)
        acc_d[...] += _mm_tn((silu * uv).astype(BF16), dh)
        acc_g[...] += _mm_tn(dgb, n2v)
        acc_u[...] += _mm_tn(dub, n2v)
        dn2_acc[rows, :] += _mm(jnp.concatenate([dgb, dub], axis=1),
                                jnp.concatenate([wg_ref[...], wu_ref[...]], axis=0))

        @pl.when(t == n_tb - 1)
        def _():
            dwg_ref[...] = acc_g[...].astype(BF16)
            dwu_ref[...] = acc_u[...].astype(BF16)
            dwd_ref[...] = acc_d[...].astype(BF16)

        @pl.when(j == n_ff - 1)
        def _():
            dn2_ref[...] = dn2_acc[rows, :]

    row = pl.BlockSpec((tb, D_MODEL), lambda j, t: (t, 0))
    act = pl.BlockSpec((tb, FF_CHUNK), lambda j, t: (t, j))
    w_row = pl.BlockSpec((FF_CHUNK, D_MODEL), lambda j, t: (j, 0))
    last = pl.BlockSpec((tb, D_MODEL), lambda j, t: (jnp.where(j == n_ff - 1, t, 0), 0))
    grad = pltpu.HBM((D_FF, D_MODEL), BF16)
    chunk_acc = pltpu.VMEM((FF_CHUNK, D_MODEL), F32)
    return pl.pallas_call(
        body, name="ffn_bwd", grid=(n_ff, n_tb),
        in_specs=[row, row, act, act, w_row, w_row, w_row],
        out_specs=[w_row, w_row, w_row, last],
        out_shape=[grad, grad, grad, pltpu.HBM((s, D_MODEL), F32)],
        scratch_shapes=[pltpu.VMEM((s, D_MODEL), F32), chunk_acc, chunk_acc, chunk_acc],
        compiler_params=_params(56, 2),
    )(*_in_hbm(n2, dh2b, g, u, w_gate_t, w_up_t, w_down))


def _bwd_out(dn2, dh2, h1, y_pool, y_lru, gn_pool_g, gn_lru_g, w_out, ln2_g, after):
    s = h1.shape[0]
    tb = _token_block(s)

    def body(dn2_ref, dh2_ref, h1_ref, yp_ref, yl_ref, gp_ref, gl_ref, w_ref, g2_ref, _after,
             dh1_ref, dyp_ref, dyl_ref, dwb_ref, dg2_ref, dgp_ref, dgl_ref, dw_ref):
        @pl.when(pl.program_id(0) == 0)
        def _():
            dw_ref[...] = jnp.zeros_like(dw_ref)
            dg2_ref[...] = jnp.zeros_like(dg2_ref)
            dgp_ref[...] = jnp.zeros_like(dgp_ref)
            dgl_ref[...] = jnp.zeros_like(dgl_ref)

        g2 = g2_ref[...]
        _, xhat2, rstd2 = _rms(h1_ref[...], g2)
        dres, dg2 = _rms_bwd(dn2_ref[...], xhat2, rstd2, g2)
        dg2_ref[...] += dg2
        dh1 = dh2_ref[...] + dres
        dh1_ref[...] = dh1
        dh1b = dh1.astype(BF16)
        gp, gl = gp_ref[...], gl_ref[...]
        mp, xhat_p, rstd_p = _rms(_read_slabs(yp_ref), gp)
        ml, xhat_l, rstd_l = _rms(_read_slabs(yl_ref), gl)
        dw_ref[:POOL_WIDTH, :] += _mm_tn(mp.astype(BF16), dh1b)
        dw_ref[POOL_WIDTH:, :] += _mm_tn(ml.astype(BF16), dh1b)
        dyp, dgp = _rms_bwd(_mm_nt(dh1b, w_ref[:POOL_WIDTH, :]), xhat_p, rstd_p, gp)
        dyl, dgl = _rms_bwd(_mm_nt(dh1b, w_ref[POOL_WIDTH:, :]), xhat_l, rstd_l, gl)
        _write_slabs(dyp_ref, dyp)
        _write_slabs(dyl_ref, dyl)
        dgp_ref[...] += dgp
        dgl_ref[...] += dgl

        @pl.when(pl.program_id(0) == s // tb - 1)
        def _():
            dwb_ref[...] = dw_ref[...].astype(BF16)

    row = pl.BlockSpec((tb, D_MODEL), lambda i: (i, 0))
    half = _slabs_spec(tb)
    vec = pl.BlockSpec((1, D_MODEL), lambda i: (0, 0))
    hvec = pl.BlockSpec((1, 512), lambda i: (0, 0))
    mat = pl.BlockSpec((D_MODEL, D_MODEL), lambda i: (0, 0))
    return pl.pallas_call(
        body, name="bwd_out", grid=(s // tb,),
        in_specs=[row, row, row, half, half, hvec, hvec, mat, vec, pl.BlockSpec(memory_space=pl.ANY)],
        out_specs=[row, half, half, mat, vec, hvec, hvec],
        out_shape=[pltpu.HBM((s, D_MODEL), F32), _slab_major(s),
                   _slab_major(s), pltpu.HBM((D_MODEL, D_MODEL), BF16),
                   jax.ShapeDtypeStruct((1, D_MODEL), F32), jax.ShapeDtypeStruct((1, 512), F32),
                   jax.ShapeDtypeStruct((1, 512), F32)],
        scratch_shapes=[pltpu.VMEM((D_MODEL, D_MODEL), F32)],
        compiler_params=_params(48, 1),
    )(*_in_hbm(dn2, dh2, h1, y_pool, y_lru, gn_pool_g, gn_lru_g, w_out, ln2_g), after)


def _mixer_bwd(u_pool, u_lru, u_gate, h, dy_pool, dy_lru,
               pool_w, pool_scale, conv_w, conv_b, wa_bd, b_a, wi_bd, b_i, lam, after):
    s = u_pool.shape[1]
    tc = _time_chunk(s)
    n_chunks = s // tc

    def body(up_ref, ul_ref, ug_ref, h_ref, dyp_ref, dyl_ref,
             pw_ref, ps_ref, cw_ref, cb_ref, wa_ref, ba_ref, wi_ref, bi_ref, lam_ref, _after,
             dup_ref, dul_ref, dug_ref, dpw_ref, dps_ref, dcw_ref, dcb_ref, dwa_ref, dba_ref, dwi_ref, dbi_ref, dlam_ref,
             a_pad, b_pad, u_shift, h_shift, a_shift, d_shift):
        _scan_pads(a_pad, b_pad, tc, causal=False)
        deeper, window = _slab_scalars()
        pw = pw_ref[0].astype(BF16)
        wa = wa_ref[0].astype(BF16)
        wi = wi_ref[0].astype(BF16)
        ps, cb, ba, bi = ps_ref[...], cb_ref[...], ba_ref[...], bi_ref[...]
        cw = [cw_ref[k:k + 1, :] for k in range(CONV_WIDTH)]
        lam_v = lam_ref[...]
        sp = _softplus_neg(lam_v)
        row = lax.broadcasted_iota(jnp.int32, (tc, SLAB), 0)
        for ref in (dpw_ref, dps_ref, dcw_ref, dcb_ref, dwa_ref, dba_ref, dwi_ref, dbi_ref, dlam_ref):
            ref[...] = jnp.zeros_like(ref)

        def chunk(t0, ext_p, ext_l, ext_h, carry, at_start=False):
            l_next, a_next, dxc_next, ddn_next = carry
            rows = pl.ds(t0, tc)
            u_shift[...] = ext_l
            taps = [u_shift[HALO - k:HALO - k + tc, :] for k in range(CONV_WIDTH)]
            xc = _conv(taps, cw, cb)
            xb, r, i, a, mult = _lru_gates(xc, wa, ba, wi, bi, sp)
            hv = ext_h[HALO:]
            h_shift[...] = ext_h
            h_before = h_shift[HALO - 1:HALO - 1 + tc, :]
            ug = ug_ref[rows, :]
            dyl = dyl_ref[rows, :]
            gel, th = _gelu(ug)
            dug_ref[rows, :] = (dyl * hv * _gelu_grad(ug, th)).astype(BF16)
            a_shift[:tc, :] = a
            a_shift[tc:, :] = jnp.broadcast_to(a_next, (SUBLANES, SLAB))
            a_after = a_shift[1:1 + tc, :]
            l = _scan_anticausal(a_after, dyl * gel, l_next, a_pad, b_pad, tc)
            dmult = l * (i * xc)
            di = l * mult * xc
            dxc = l * mult * i
            dla = (l * h_before) * a - jnp.where(mult > 0.0, dmult * (a * a) / mult, 0.0)
            dlam_ref[...] += jnp.sum(dla * r, axis=0, keepdims=True)
            dpa = (dla * ((-LRU_C) * sp)) * (r * (1.0 - r))
            dpi = di * (i * (1.0 - i))
            dpab = dpa.astype(BF16)
            dpib = dpi.astype(BF16)
            dwa_ref[0] += _mm_tn(xb, dpab)
            dwi_ref[0] += _mm_tn(xb, dpib)
            dba_ref[...] += jnp.sum(dpa, axis=0, keepdims=True)
            dbi_ref[...] += jnp.sum(dpi, axis=0, keepdims=True)
            dxc = dxc + _mm_nt(dpab, wa) + _mm_nt(dpib, wi)
            d_shift[:tc, :] = dxc
            d_shift[tc:, :] = dxc_next
            dul_ref[rows, :] = (cw[3] * dxc + cw[2] * d_shift[1:1 + tc, :]
                                + cw[1] * d_shift[2:2 + tc, :] + cw[0] * d_shift[3:3 + tc, :]).astype(BF16)
            for k in range(CONV_WIDTH):
                dcw_ref[k:k + 1, :] += jnp.sum(dxc * taps[CONV_WIDTH - 1 - k], axis=0, keepdims=True)
            dcb_ref[...] += jnp.sum(dxc, axis=0, keepdims=True)
            db = (_window_mean(_causal_window(ext_p, deeper), t0, row, window, at_start) - ext_p[HALO:]).astype(BF16)
            dyp = dyp_ref[rows, :]
            dps_ref[...] += jnp.sum(dyp * _mm(db, pw), axis=0, keepdims=True)
            dys = (dyp * ps).astype(BF16)
            dpw_ref[0] += _mm_tn(db, dys)
            dd = _mm_nt(dys, pw)
            ddn = _window_mean(dd, t0, row, window, at_start)
            ext_q = jnp.concatenate([ddn, ddn_next], axis=0)
            dup_ref[rows, :] = (_anticausal_window(ext_q, deeper, tc) - dd).astype(BF16)
            return l[0:1, :], a[0:1, :], dxc[0:8, :], ddn[0:HALO, :]

        def step(k, carry):
            c = n_chunks - 1 - k
            t0 = pl.multiple_of(c * tc, tc)
            ext = pl.ds(pl.multiple_of(c * tc - HALO, HALO), tc + HALO)
            return chunk(t0, up_ref[ext, :], ul_ref[ext, :], h_ref[ext, :], carry)

        carry = (jnp.zeros((1, SLAB), F32), jnp.zeros((1, SLAB), F32),
                 jnp.zeros((8, SLAB), F32), jnp.zeros((HALO, SLAB), F32))
        carry = lax.fori_loop(0, n_chunks - 1, step, carry)
        pad = jnp.zeros((HALO, SLAB), F32)
        first = pl.ds(0, tc)
        chunk(0, jnp.concatenate([pad, up_ref[first, :]], axis=0), jnp.concatenate([pad, ul_ref[first, :]], axis=0),
              jnp.concatenate([pad, h_ref[first, :]], axis=0), carry, at_start=True)
        dlam_ref[...] = dlam_ref[...] * (LRU_C * jax.nn.sigmoid(-lam_v))

    seq, mat, vec, taps = _slab_specs(s)
    grad = _slab_major(s, BF16)
    mats = jax.ShapeDtypeStruct((N_SLAB, SLAB, SLAB), F32)
    vecs = jax.ShapeDtypeStruct((1, 512), F32)
    return pl.pallas_call(
        body, name="mixer_bwd", grid=(N_SLAB,),
        in_specs=[seq] * 6 + [mat, vec, taps, vec, mat, vec, mat, vec, vec, pl.BlockSpec(memory_space=pl.ANY)],
        out_specs=[seq, seq, seq, mat, vec, taps, vec, mat, vec, mat, vec, vec],
        out_shape=[grad, grad, grad, mats, vecs, jax.ShapeDtypeStruct((CONV_WIDTH, 512), F32), vecs,
                   mats, vecs, mats, vecs, vecs],
        scratch_shapes=[pltpu.VMEM((tc + SUBLANES, SLAB), F32), pltpu.VMEM((tc + SUBLANES, SLAB), F32),
                        pltpu.VMEM((tc + HALO, SLAB), F32), pltpu.VMEM((tc + HALO, SLAB), F32),
                        pltpu.VMEM((tc + SUBLANES, SLAB), F32), pltpu.VMEM((tc + SUBLANES, SLAB), F32)],
        compiler_params=_params(56, 1),
    )(*_in_hbm(u_pool, u_lru, u_gate, h, dy_pool, dy_lru, pool_w, pool_scale, conv_w, conv_b, wa_bd, b_a, wi_bd,
               b_i, lam), after)


def _bwd_in(x, dh1, du_pool, du_lru, du_gate, ln1_g, w_in_t, after):
    s = x.shape[0]
    tb = _token_block(s)

    def body(x_ref, dh1_ref, dup_ref, dul_ref, dug_ref, g_ref, w_ref, _after, dx_ref, dwb_ref, dg_ref, dw_ref):
        @pl.when(pl.program_id(0) == 0)
        def _():
            dw_ref[...] = jnp.zeros_like(dw_ref)
            dg_ref[...] = jnp.zeros_like(dg_ref)

        g1 = g_ref[...]
        n, xhat, rstd = _rms(x_ref[...], g1)
        nb = n.astype(BF16)
        db = jnp.concatenate([_read_slabs(ref) for ref in (dup_ref, dul_ref, dug_ref)], axis=1)
        dw_ref[...] += _mm_tn(db, nb)
        dx, dg1 = _rms_bwd(_mm(db, w_ref[...]), xhat, rstd, g1)
        dx_ref[...] = dh1_ref[...] + dx
        dg_ref[...] += dg1

        @pl.when(pl.program_id(0) == s // tb - 1)
        def _():
            dwb_ref[...] = dw_ref[...].astype(BF16)

    row = pl.BlockSpec((tb, D_MODEL), lambda i: (i, 0))
    half = _slabs_spec(tb)
    vec = pl.BlockSpec((1, D_MODEL), lambda i: (0, 0))
    mat = pl.BlockSpec((IN_WIDTH, D_MODEL), lambda i: (0, 0))
    return pl.pallas_call(
        body, name="bwd_in", grid=(s // tb,),
        in_specs=[row, row, half, half, half, vec, mat, pl.BlockSpec(memory_space=pl.ANY)],
        out_specs=[row, mat, vec],
        out_shape=[pltpu.HBM((s, D_MODEL), F32), pltpu.HBM((IN_WIDTH, D_MODEL), BF16),
                   jax.ShapeDtypeStruct((1, D_MODEL), F32)],
        scratch_shapes=[pltpu.VMEM((IN_WIDTH, D_MODEL), F32)],
        compiler_params=_params(48, 1),
    )(*_in_hbm(x, dh1, du_pool, du_lru, du_gate, ln1_g, w_in_t), after)


def _mesh_position():
    x, y, c = lax.axis_index("x"), lax.axis_index("y"), lax.axis_index("c")
    return x, y, c, 4 * x + 2 * y + c


def _peer(x, y, c, p):
    px = 1 - x if p & 4 else x
    py = 1 - y if p & 2 else y
    pc = 1 - c if p & 1 else c
    return (px, py, pc), 4 * px + 2 * py + pc


HBM_SPEC = pl.BlockSpec(memory_space=pltpu.HBM)
SEM_SPEC = pl.BlockSpec(memory_space=pltpu.SEMAPHORE)
DATAFLOW = pltpu.SideEffectType.DATAFLOW_SIDE_EFFECTING


class Copy(NamedTuple):
    src: int
    src_at: Any
    dst: int
    dst_at: Any
    peer: int
    group: int
    slot: int


SIBLING = (1,)
SAME_CORE = (2, 4, 6)
EVERYONE = tuple(range(1, N_DEV))
MYSELF = (0,)


def _same(index):
    return index


def _chip(index):
    return jnp.right_shift(index, 1)


def _fan_out(srcs, lands, peers, group):
    return [Copy(s, None, d, _same, p, group, N_DEV * i + p) for i, (s, d) in enumerate(zip(srcs, lands)) for p in peers]


def _scatter(stacks, lands, peers):
    return [Copy(s, lambda me, p=p: jnp.bitwise_xor(me, p), d, _same, p, 0, 0)
            for s, d in zip(stacks, lands) for p in peers]


def _numbered(copies, group=0):
    return [cp._replace(group=group, slot=i) for i, cp in enumerate(copies)]


def _relay(lands, peers):
    return [Copy(b, lambda s, q=q: jnp.bitwise_xor(s, q), b, lambda s, q=q: jnp.bitwise_xor(s, q), 1, 0, N_DEV * i + q)
            for i, b in enumerate(lands) for q in peers]


def _to_sibling(stacks, lands):
    return [Copy(s, lambda me, k=k: 2 * k + 1 - jnp.bitwise_and(me, 1), d, lambda me, k=k: k, 1, 0, 4 * i + k)
            for i, (s, d) in enumerate(zip(stacks, lands)) for k in range(N_DEV // 2)]


def _to_chips(sums, lands):
    return [Copy(s, lambda me, p=p: jnp.bitwise_xor(_chip(me), p // 2), d, _chip, p, 0, 4 * i + p // 2)
            for i, (s, d) in enumerate(zip(sums, lands)) for p in SAME_CORE]


def _comm_call(name, bufs, wait=None, start=None, after=()):
    nb = len(bufs)
    waits = [wait] if isinstance(wait, tuple) else list(wait or [])
    slots = list(start[1]) if start else []
    n_out_sem = 2 * len(slots)
    after = [a for a in (after if isinstance(after, (list, tuple)) else [after]) if a is not None]

    def body(*refs):
        b = refs[:nb]
        at = nb + 2 * len(waits) + len(after)
        out_sems = refs[at:at + n_out_sem]
        token = refs[at + n_out_sem + nb]
        x, y, c, me = _mesh_position()

        def part(i, row_of, sender):
            return b[i] if row_of is None else b[i].at[row_of(sender)]

        for k, (_, copies) in enumerate(waits):
            w_send, w_recv = refs[nb + 2 * k], refs[nb + 2 * k + 1]
            for cp in copies:
                peer, peer_index = _peer(x, y, c, cp.peer)
                arrival = pltpu.make_async_remote_copy(part(cp.src, cp.src_at, me), part(cp.dst, cp.dst_at, peer_index),
                                                       w_send.at[cp.slot], w_recv.at[cp.slot],
                                                       device_id=peer, device_id_type=MESH)
                arrival.wait_send()
                arrival.wait_recv()
        if start:
            for cp in start[0]:
                peer, _ = _peer(x, y, c, cp.peer)
                pltpu.make_async_remote_copy(part(cp.src, cp.src_at, me), part(cp.dst, cp.dst_at, me),
                                             out_sems[2 * cp.group].at[cp.slot], out_sems[2 * cp.group + 1].at[cp.slot],
                                             device_id=peer, device_id_type=MESH).start()
        token[...] = jnp.zeros_like(token)

    sem_shapes = []
    for n_slots in slots:
        sem_shapes += [pltpu.SemaphoreType.DMA((n_slots,))] * 2
    operands = [pltpu.with_memory_space_constraint(a, pltpu.HBM) for a in bufs]
    in_specs = [HBM_SPEC] * nb
    for sems, _ in waits:
        operands += list(sems)
        in_specs += [SEM_SPEC, SEM_SPEC]
    operands += after
    in_specs += [pl.BlockSpec(memory_space=pl.ANY)] * len(after)
    outs = pl.pallas_call(
        body, name=name, in_specs=in_specs,
        out_specs=[SEM_SPEC] * n_out_sem + [HBM_SPEC] * nb + [pl.BlockSpec(memory_space=pltpu.VMEM)],
        out_shape=sem_shapes + [pltpu.HBM(a.shape, a.dtype) for a in bufs] + [jax.ShapeDtypeStruct((8, SLAB), F32)],
        input_output_aliases={i: n_out_sem + i for i in range(nb)},
        compiler_params=pltpu.CompilerParams(has_side_effects=DATAFLOW),
    )(*operands)
    sems = [(outs[2 * k], outs[2 * k + 1]) for k in range(len(slots))]
    return sems, list(outs[n_out_sem:n_out_sem + nb]), outs[-1]


def _pair_sum(stacks, lands, place):
    n = len(stacks)

    def body(place_ref, *refs):
        k = pl.program_id(0)
        for m in range(n):
            mine, theirs, out, land = refs[m], refs[n + m], refs[2 * n + m], refs[3 * n + m]
            total = (mine[0, 0].astype(F32) + theirs[0].astype(F32)).astype(out.dtype)
            out[0] = total

            @pl.when(k == place_ref[1])
            def _():
                land[0] = total

    in_specs = [pl.BlockSpec((1, 1) + a.shape[1:], lambda k, place_ref: (k, place_ref[0], 0, 0)) for a in stacks]
    in_specs += [pl.BlockSpec((1,) + a.shape[1:], lambda k, place_ref: (k, 0, 0)) for a in lands]
    out_specs = [pl.BlockSpec((1,) + a.shape[1:], lambda k, place_ref: (k, 0, 0)) for a in lands]
    out_specs += [pl.BlockSpec((1,) + a.shape[1:], lambda k, place_ref: (place_ref[1], 0, 0)) for a in lands]
    outs = pl.pallas_call(
        body, name="pair_sum_" + "_".join(str(a.shape[1]) for a in stacks),
        grid_spec=pltpu.PrefetchScalarGridSpec(num_scalar_prefetch=1, grid=(N_DEV // 2,), in_specs=in_specs,
                                               out_specs=out_specs),
        out_shape=[pltpu.HBM(a.shape, a.dtype) for a in lands] * 2,
        compiler_params=_params(40, 1),
    )(place, *_in_hbm(*[a.reshape((N_DEV // 2, 2) + a.shape[1:]) for a in stacks], *lands))
    return list(outs[:n]), list(outs[n:])


def _reduce_adam(parts, w, m, v, name):
    rows, cols = w.shape
    n_parts = parts.shape[0]
    rb = rows
    for cand in (256, 176, 128):
        if rows % cand == 0 and rows > cand:
            rb = cand
            break

    def body(p_ref, w_ref, m_ref, v_ref, g_out, d_out, m_out, v_out):
        g = p_ref[0].astype(F32)
        for j in range(1, n_parts):
            g = g + p_ref[j].astype(F32)
        g_out[...] = g
        d_out[...], m_out[...], v_out[...] = _adam(g, w_ref[...], m_ref[...], v_ref[...])

    blk = pl.BlockSpec((rb, cols), lambda i: (i, 0))
    out = jax.ShapeDtypeStruct((rows, cols), F32)
    return pl.pallas_call(
        body, name=name, grid=(rows // rb,),
        in_specs=[pl.BlockSpec((n_parts, rb, cols), lambda i: (0, i, 0)), blk, blk, blk],
        out_specs=[blk] * 4, out_shape=[out] * 4,
        compiler_params=_params(40, 1),
    )(*_in_hbm(parts, w, m, v))


def _cols_from_stack(stack):
    n, r, c = stack.shape
    return jnp.transpose(stack, (1, 0, 2)).reshape(r, n * c)


def _block_diag(w):
    z = jnp.zeros((N_SLAB, 64, 64), w.dtype)
    pairs = w.reshape(N_SLAB, 2, 64, 64)
    top = jnp.concatenate([pairs[:, 0], z], axis=2)
    bottom = jnp.concatenate([z, pairs[:, 1]], axis=2)
    return jnp.concatenate([top, bottom], axis=1)


def _adam(g, w, m, v):
    m_new = ADAM_B1 * m + (1.0 - ADAM_B1) * g
    v_new = ADAM_B2 * v + (1.0 - ADAM_B2) * (g * g)
    m_hat = m_new / (1.0 - ADAM_B1 ** ADAM_STEP)
    v_hat = v_new / (1.0 - ADAM_B2 ** ADAM_STEP)
    return (-ADAM_LR) * (m_hat / (jnp.sqrt(v_hat) + ADAM_EPS) + ADAM_WD * w), m_new, v_new


WIDE = ("ln2_g", "lnf_g")
HALF = ("pool_scale", "conv_b", "b_a", "b_i", "lam", "gn_pool_g", "gn_lru_g")
VECTORS = [(k, D_MODEL) for k in WIDE] + [(k, 512) for k in HALF]
VECTOR_ROWS = sum(width // SLAB for _, width in VECTORS)
LOSS_ROW = -(-VECTOR_ROWS // 8) * 8
CONV_AT = LOSS_ROW + 8
CONV_LANES = LRU_WIDTH // SLAB
PACK_F_ROWS = CONV_AT + CONV_WIDTH * CONV_LANES
MATRIX_ROWS = N_SLAB * SLAB
HEAD = SLAB // 2
GATE_ROWS = N_SLAB * HEAD
PACK_B_ROWS = MATRIX_ROWS + 2 * GATE_ROWS


def _pack_small(vectors, pool_g, wa_g, wi_g, conv, sq):
    n_vec = len(vectors)

    def body(*refs):
        vec = refs[:n_vec]
        pw_ref, wa_ref, wi_ref, cw_ref, sq_ref, out, out_b = refs[n_vec:]
        out[...] = jnp.zeros_like(out)
        row = 0
        for ref, (_, width) in zip(vec, VECTORS):
            for k in range(width // SLAB):
                out[row:row + 1, :] = ref[:, k * SLAB:(k + 1) * SLAB]
                row += 1
        for tap in range(CONV_WIDTH):
            for k in range(CONV_LANES):
                at = CONV_AT + tap * CONV_LANES + k
                out[at:at + 1, :] = cw_ref[tap:tap + 1, k * SLAB:(k + 1) * SLAB]
        total = sq_ref[:, 0:SLAB]
        for k in range(1, D_MODEL // SLAB):
            total = total + sq_ref[:, k * SLAB:(k + 1) * SLAB]
        out[LOSS_ROW:LOSS_ROW + 1, :] = total
        left = lax.broadcasted_iota(jnp.int32, (HEAD, SLAB), 1) < HEAD
        for s in range(N_SLAB):
            out_b[s * SLAB:(s + 1) * SLAB, :] = pw_ref[s].astype(BF16)
            for i, ref in enumerate((wa_ref, wi_ref)):
                at = MATRIX_ROWS + i * GATE_ROWS + s * HEAD
                out_b[at:at + HEAD, :] = jnp.where(left, ref[s, 0:HEAD, :], ref[s, HEAD:SLAB, :]).astype(BF16)

    return pl.pallas_call(
        body, name="pack_small",
        out_shape=[jax.ShapeDtypeStruct((PACK_F_ROWS, SLAB), F32), jax.ShapeDtypeStruct((PACK_B_ROWS, SLAB), BF16)],
    )(*vectors, pool_g, wa_g, wi_g, conv, sq)


def _small_reduce_adam(my_index, parts, parts_b, vec_w, vec_m, vec_v, pool_wmv, wa_wmv, wi_wmv, conv_wmv):
    n_vec = len(VECTORS)
    n_parts = parts.shape[0]

    def body(*refs):
        me_ref, p_ref, pb_ref = refs[0], refs[1], refs[2]
        refs = refs[1:]
        w_refs, m_refs, v_refs = (refs[2 + k * n_vec:2 + (k + 1) * n_vec] for k in range(3))
        at = 2 + 3 * n_vec
        pw_w, pw_m, pw_v = refs[at:at + 3]
        gates = (refs[at + 3:at + 6], refs[at + 6:at + 9])
        cv_w, cv_m, cv_v = refs[at + 9:at + 12]
        outs = refs[at + 12:-1]
        total = refs[-1]
        total[...] = p_ref[0]
        for j in range(1, n_parts):
            total[...] += p_ref[j]
        row = 0
        for i, (_, width) in enumerate(VECTORS):
            n_rows = width // SLAB
            g = jnp.concatenate([total[row + k:row + k + 1, :] for k in range(n_rows)], axis=1)
            row += n_rows
            d, m_new, v_new = _adam(g, w_refs[i][...], m_refs[i][...], v_refs[i][...])
            for ref, val in zip(outs[4 * i:4 * i + 4], (g, d, m_new, v_new)):
                ref[...] = val
        tail = outs[4 * n_vec:]

        def summed(first, count):
            g = pb_ref[0, first:first + count, :].astype(F32)
            for j in range(1, n_parts):
                g = g + pb_ref[j, first:first + count, :].astype(F32)
            return g

        g = summed(0, MATRIX_ROWS)
        d, m_new, v_new = _adam(g, pw_w[...], pw_m[...], pw_v[...])
        for ref, val in zip(tail[0:4], (g, d, m_new, v_new)):
            ref[...] = val
        for i, (w, m, v) in enumerate(gates):
            g = summed(MATRIX_ROWS + i * GATE_ROWS, GATE_ROWS)
            for s in range(N_SLAB):
                for hd in range(2):
                    block = g[s * HEAD:(s + 1) * HEAD, hd * HEAD:(hd + 1) * HEAD]
                    rows = slice((2 * s + hd) * HEAD, (2 * s + hd + 1) * HEAD)
                    d, m_new, v_new = _adam(block, w[rows, :], m[rows, :], v[rows, :])
                    for ref, val in zip(tail[4 + 4 * i:8 + 4 * i], (block, d, m_new, v_new)):
                        ref[rows, :] = val
        me = me_ref[0, 0]
        taps = []
        for tap in range(CONV_WIDTH):
            row = total[pl.ds(CONV_AT + tap * CONV_LANES + me // 2, 1), :]
            taps.append(jnp.where(me % 2 == 0, row[:, :HEAD], row[:, HEAD:]))
        g = jnp.concatenate(taps, axis=0)
        d, m_new, v_new = _adam(g, cv_w[...], cv_m[...], cv_v[...])
        for ref, val in zip(tail[12:16], (g, d, m_new, v_new)):
            ref[...] = val
        tail[16][...] = (0.5 / D_MODEL) * jnp.sum(total[LOSS_ROW:LOSS_ROW + 1, :], axis=1, keepdims=True)

    out_shape = []
    for _, width in VECTORS:
        out_shape += [jax.ShapeDtypeStruct((1, width), F32)] * 4
    out_shape += [jax.ShapeDtypeStruct((MATRIX_ROWS, SLAB), F32)] * 4
    out_shape += [jax.ShapeDtypeStruct((2 * GATE_ROWS, HEAD), F32)] * 8
    out_shape += [jax.ShapeDtypeStruct(conv_wmv[0].shape, F32)] * 4 + [jax.ShapeDtypeStruct((1, 1), F32)]
    operands = [parts, parts_b, *vec_w, *vec_m, *vec_v, *pool_wmv, *wa_wmv, *wi_wmv, *conv_wmv]

    def whole(a):
        return pl.BlockSpec(a.shape, lambda i, n=len(a.shape): (0,) * n)

    outs = pl.pallas_call(
        body, name="adam_small", grid=(1,),
        in_specs=[pl.BlockSpec(memory_space=pltpu.SMEM)] + [whole(a) for a in operands],
        out_specs=[whole(a) for a in out_shape], out_shape=out_shape,
        scratch_shapes=[pltpu.VMEM((PACK_F_ROWS, SLAB), F32)],
        compiler_params=_params(40, 1),
    )(my_index, *_in_hbm(*operands))
    vec_out = [tuple(outs[4 * i:4 * i + 4]) for i in range(n_vec)]
    tail = outs[4 * n_vec:]
    return vec_out, tuple(tail[0:4]), tuple(tail[4:8]), tuple(tail[8:12]), tuple(tail[12:16]), tail[16]


def kernel(x, ln1_g, w_in, pool_w, pool_scale, conv_w, conv_b, w_a, b_a, w_i, b_i, lam, gn_pool_g, gn_lru_g, w_out, ln2_g, w_ffn_gate, w_ffn_up, w_ffn_down, lnf_g, loss_target, m_ln1_g, m_w_in, m_pool_w, m_pool_scale, m_conv_w, m_conv_b, m_w_a, m_b_a, m_w_i, m_b_i, m_lam, m_gn_pool_g, m_gn_lru_g, m_w_out, m_ln2_g, m_w_ffn_gate, m_w_ffn_up, m_w_ffn_down, m_lnf_g, v_ln1_g, v_w_in, v_pool_w, v_pool_scale, v_conv_w, v_conv_b, v_w_a, v_b_a, v_w_i, v_b_i, v_lam, v_gn_pool_g, v_gn_lru_g, v_w_out, v_ln2_g, v_w_ffn_gate, v_w_ffn_up, v_w_ffn_down, v_lnf_g):
    weights = dict(ln1_g=ln1_g, w_in=w_in, pool_w=pool_w, pool_scale=pool_scale, conv_w=conv_w, conv_b=conv_b,
                   w_a=w_a, b_a=b_a, w_i=w_i, b_i=b_i, lam=lam, gn_pool_g=gn_pool_g, gn_lru_g=gn_lru_g,
                   w_out=w_out, ln2_g=ln2_g, w_ffn_gate=w_ffn_gate, w_ffn_up=w_ffn_up, w_ffn_down=w_ffn_down,
                   lnf_g=lnf_g)
    mom1 = dict(ln1_g=m_ln1_g, w_in=m_w_in, pool_w=m_pool_w, pool_scale=m_pool_scale, conv_w=m_conv_w,
                conv_b=m_conv_b, w_a=m_w_a, b_a=m_b_a, w_i=m_w_i, b_i=m_b_i, lam=m_lam, gn_pool_g=m_gn_pool_g,
                gn_lru_g=m_gn_lru_g, w_out=m_w_out, ln2_g=m_ln2_g, w_ffn_gate=m_w_ffn_gate,
                w_ffn_up=m_w_ffn_up, w_ffn_down=m_w_ffn_down, lnf_g=m_lnf_g)
    mom2 = dict(ln1_g=v_ln1_g, w_in=v_w_in, pool_w=v_pool_w, pool_scale=v_pool_scale, conv_w=v_conv_w,
                conv_b=v_conv_b, w_a=v_w_a, b_a=v_b_a, w_i=v_w_i, b_i=v_b_i, lam=v_lam, gn_pool_g=v_gn_pool_g,
                gn_lru_g=v_gn_lru_g, w_out=v_w_out, ln2_g=v_ln2_g, w_ffn_gate=v_w_ffn_gate,
                w_ffn_up=v_w_ffn_up, w_ffn_down=v_w_ffn_down, lnf_g=v_lnf_g)

    xs = x[0]
    target = loss_target[0]

    shard = dict(w_in=lambda a: a[0].T, w_ffn_gate=lambda a: a[0].T, w_ffn_up=lambda a: a[0].T,
                 w_out=lambda a: a[0], w_ffn_down=lambda a: a[0], conv_w=lambda a: a[0])
    unshard = dict(w_in=lambda a: a.T[None], w_ffn_gate=lambda a: a.T[None], w_ffn_up=lambda a: a.T[None],
                   w_out=lambda a: a[None], w_ffn_down=lambda a: a[None], conv_w=lambda a: a[None])

    gathered = ("w_in", "conv_w", "w_out", "w_ffn_gate", "w_ffn_up", "w_ffn_down")
    groups = ((0, 1), (2,), (3, 4), (5,))
    sources = [shard[k](weights[k]) if k == "conv_w" else shard[k](weights[k]).astype(BF16) for k in gathered]
    my_index = 4 * lax.axis_index("x") + 2 * lax.axis_index("y") + lax.axis_index("c")
    lands = [lax.empty((N_DEV,) + a.shape, a.dtype) for a in sources]

    def first_hop(n, at=0):
        return _numbered(_fan_out(range(at, at + n), range(at + n, at + 2 * n), SAME_CORE + SIBLING + MYSELF, 0))

    def second_hop(n, at=0):
        return _numbered(_relay(range(at, at + n), SAME_CORE))

    start = []
    for g, members in enumerate(groups):
        start += _numbered(_fan_out(members, [6 + m for m in members], SAME_CORE + SIBLING + MYSELF, 0), g)
    sems, bufs, _ = _comm_call("gather_start", sources + lands,
                               start=(start, [sum(cp.group == g for cp in start) for g in range(len(groups))]))
    sources, lands = bufs[:6], bufs[6:]

    def group_buffers(g):
        return [sources[m] for m in groups[g]] + [lands[m] for m in groups[g]]

    def relayed(tag, g, after):
        n = len(groups[g])
        relay = second_hop(n, n)
        relay_sems, bufs, _ = _comm_call("gather_relay_" + tag, group_buffers(g), wait=(sems[g], first_hop(n)),
                                         start=(relay, (len(relay),)), after=after)
        return relay_sems[0], bufs[n:]

    relay_sem, bufs = relayed("in", 0, None)
    g_in, g_conv = _comm_call("gather_wait_in", bufs, wait=(relay_sem, second_hop(2)))[1]
    w_in_f = g_in.reshape(IN_WIDTH, D_MODEL)
    conv_w_f = _cols_from_stack(g_conv)

    wa_bd = _block_diag(w_a[0])
    wi_bd = _block_diag(w_i[0])
    lnf_row = lnf_g.reshape(1, D_MODEL)

    u_pool, u_lru, u_gate = _fwd_in(xs, ln1_g, w_in_f)
    y_pool, h, y_lru = _mixer_fwd(u_pool, u_lru, u_gate, pool_w[0], pool_scale, conv_w_f, conv_b,
                                  wa_bd, b_a, wi_bd, b_i, lam)
    relay_sem, bufs = relayed("out", 1, y_pool)
    gate_up = second_hop(2, 3)
    (gate_up_sem,), bufs, _ = _comm_call("gather_wait_out", bufs + group_buffers(2),
                                         wait=[(relay_sem, second_hop(1)), (sems[2], first_hop(2, 1))],
                                         start=(gate_up, (len(gate_up),)))
    g_out, gate_up_lands = bufs[0], bufs[3:]
    w_out_f = g_out.reshape(D_MODEL, D_MODEL)
    h1, n2 = _fwd_out(xs, y_pool, y_lru, gn_pool_g, gn_lru_g, w_out_f, ln2_g)
    relay_sem, bufs = relayed("down", 3, n2)
    g_gate, g_up, g_down = _comm_call("gather_wait_ffn", gate_up_lands + bufs,
                                      wait=[(gate_up_sem, second_hop(2)), (relay_sem, second_hop(1, 2))])[1]
    w_gate_f = g_gate.reshape(D_FF, D_MODEL)
    w_up_f = g_up.reshape(D_FF, D_MODEL)
    w_down_f = g_down.reshape(D_FF, D_MODEL)
    g_act, u_act, dh2, dh2b, d_lnf, sq = _ffn_fwd(h1, n2, target, lnf_row, w_gate_f, w_up_f, w_down_f)

    def direct(tag, stacks, wholes, after):
        sources = list(stacks) + list(wholes)
        n, n_st = len(sources), len(stacks)
        lands = [lax.empty(a.shape if i < n_st else (N_DEV,) + a.shape, a.dtype) for i, a in enumerate(sources)]
        copies = _numbered(_scatter(range(n_st), range(n, n + n_st), EVERYONE + MYSELF)
                           + _fan_out(range(n_st, n), range(n + n_st, 2 * n), EVERYONE + MYSELF, 0))
        sem, bufs, token = _comm_call(tag + "_start", sources + lands, start=(copies, (len(copies),)), after=after)
        return (tag, sem[0], bufs, copies), token

    def direct_finish(started, after):
        tag, sem, bufs, copies = started
        _, bufs, _ = _comm_call(tag + "_wait", bufs, wait=(sem, copies), after=after)
        return bufs[len(bufs) // 2:]

    place = jnp.stack([lax.axis_index("c"), 2 * lax.axis_index("x") + lax.axis_index("y")]).astype(jnp.int32)

    d_gate, d_up, d_down, dn2 = _ffn_bwd(n2, dh2b, g_act, u_act, w_gate_f, w_up_f, w_down_f)
    down_stack, *ffn_stacks = [d.reshape(N_DEV, D_FF // N_DEV, D_MODEL) for d in (d_down, d_gate, d_up)]
    pair_lands = [lax.empty((N_DEV // 2,) + a.shape[1:], a.dtype) for a in ffn_stacks]
    pair_copies = _numbered(_to_sibling(range(2), range(2, 4)))
    down_copies = _numbered(_scatter([4], [5], EVERYONE + MYSELF), 1)
    (sem, down_sem), bufs, token = _comm_call(
        "ffn_grads_start", ffn_stacks + pair_lands + [down_stack, lax.empty(down_stack.shape, down_stack.dtype)],
        start=(pair_copies + down_copies, (len(pair_copies), len(down_copies))))
    bufs, down_bufs = bufs[:4], bufs[4:]
    dh1, dy_pool, dy_lru, d_out, d_ln2, d_gnp, d_gnl = _bwd_out(dn2, dh2, h1, y_pool, y_lru, gn_pool_g, gn_lru_g,
                                                                 w_out_f, ln2_g, token)
    _, bufs, _ = _comm_call("ffn_pair_wait", bufs, wait=(sem, pair_copies), after=dh1)
    ffn_sums, ffn_lands = _pair_sum(bufs[:2], bufs[2:], place)
    ffn_copies = _numbered(_to_chips(range(2), range(2, 4)))
    ffn_sem, ffn_bufs, token = _comm_call("ffn_chip_start", ffn_sums + ffn_lands, start=(ffn_copies, (len(ffn_copies),)))
    (du_pool, du_lru, du_gate, d_pw, d_ps, d_cw, d_cb, d_wa, d_ba, d_wi, d_bi, d_lam) = _mixer_bwd(
        u_pool, u_lru, u_gate, h, dy_pool, dy_lru, pool_w[0], pool_scale, conv_w_f, conv_b,
        wa_bd, b_a, wi_bd, b_i, lam, token)

    vec_grads = dict(ln2_g=d_ln2, lnf_g=d_lnf, pool_scale=d_ps, conv_b=d_cb, b_a=d_ba, b_i=d_bi,
                     lam=d_lam, gn_pool_g=d_gnp, gn_lru_g=d_gnl)
    packed, packed_b = _pack_small([vec_grads[k] for k, _ in VECTORS], d_pw, d_wa, d_wi, d_cw, sq)
    small_started, token = direct("small", [d_out.reshape(N_DEV, D_MODEL // N_DEV, D_MODEL)], [packed, packed_b], None)
    grad_x, d_in, d_ln1 = _bwd_in(xs, dh1, du_pool, du_lru, du_gate, ln1_g, w_in_f, token)
    in_started, token = direct("in", [d_in.reshape(N_DEV, IN_WIDTH // N_DEV, D_MODEL)], [d_ln1], None)

    results = {}

    def reduce_adam(name, parts):
        outs = _reduce_adam(parts, shard[name](weights[name]), shard[name](mom1[name]), shard[name](mom2[name]),
                            "adam_" + name)
        results[name] = tuple(unshard[name](o) for o in outs)
        return outs[0]

    shard["ln1_g"] = unshard["ln1_g"] = lambda a: a
    _, ffn_bufs, _ = _comm_call("ffn_chip_wait", ffn_bufs, wait=(ffn_sem[0], ffn_copies), after=token)
    done = [reduce_adam(name, parts) for name, parts in zip(("w_ffn_gate", "w_ffn_up"), ffn_bufs[2:])]
    _, (_, r_down), _ = _comm_call("down_wait", down_bufs, after=done,
                                   wait=(down_sem, _numbered(_scatter([0], [1], EVERYONE + MYSELF))))
    done.append(reduce_adam("w_ffn_down", r_down))
    r_out, r_small, r_small_b = direct_finish(small_started, done)
    done.append(reduce_adam("w_out", r_out))

    def as_row(a, width):
        return a.reshape(1, width)

    def as_matrix(a):
        return a.reshape(MATRIX_ROWS, SLAB)

    def as_heads(a):
        return a.reshape(2 * GATE_ROWS, HEAD)

    states = (weights, mom1, mom2)
    vec_out, pool_out, wa_out, wi_out, conv_out, loss_11 = _small_reduce_adam(
        my_index.astype(jnp.int32).reshape(1, 1), r_small, r_small_b,
        *[[as_row(t[k], w) for k, w in VECTORS] for t in states], [as_matrix(t["pool_w"]) for t in states],
        [as_heads(t["w_a"]) for t in states], [as_heads(t["w_i"]) for t in states], [t["conv_w"][0] for t in states])
    for (k, _), outs in zip(VECTORS, vec_out):
        results[k] = tuple(o.reshape(weights[k].shape) for o in outs)
    for k, outs in (("pool_w", pool_out), ("w_a", wa_out), ("w_i", wi_out), ("conv_w", conv_out)):
        results[k] = tuple(o.reshape(weights[k].shape) for o in outs)
    loss = loss_11[0, 0]
    r_in, r_ln1 = direct_finish(in_started, done + [loss_11])
    reduce_adam("w_in", r_in)
    reduce_adam("ln1_g", r_ln1)

    order = ["ln1_g", "w_in", "pool_w", "pool_scale", "conv_w", "conv_b", "w_a", "b_a", "w_i", "b_i", "lam",
             "gn_pool_g", "gn_lru_g", "w_out", "ln2_g", "w_ffn_gate", "w_ffn_up", "w_ffn_down", "lnf_g"]
    return (loss, grad_x[None],
            *[results[k][0] for k in order], *[results[k][1] for k in order],
            *[results[k][2] for k in order], *[results[k][3] for k in order])
```

```python
from typing import Any, NamedTuple

import jax
import jax.numpy as jnp
from jax import lax
from jax.experimental import pallas as pl
from jax.experimental.pallas import tpu as pltpu

F32 = jnp.float32
BF16 = jnp.bfloat16

N_DEV = 8
D_MODEL = 1024
POOL_WIDTH = 512
LRU_WIDTH = 512
IN_WIDTH = 1536
D_FF = 2816
N_SLAB = 4
SLAB = 128
CONV_WIDTH = 4
LRU_C = 8.0
EPS = 1e-6
HALO = 16
FF_CHUNK = 256

ADAM_LR = 0.001
ADAM_B1 = 0.9
ADAM_B2 = 0.999
ADAM_EPS = 1e-08
ADAM_WD = 0.01
ADAM_STEP = 10

MIB = 1 << 20
MESH = pl.DeviceIdType.MESH


def _params(vmem_mib, n_axes=0):
    sem = ("arbitrary",) * n_axes if n_axes else None
    return pltpu.CompilerParams(dimension_semantics=sem, vmem_limit_bytes=vmem_mib * MIB)


def _in_hbm(*arrays):
    return [pltpu.with_memory_space_constraint(a, pltpu.HBM) for a in arrays]


def _mm(a, b):
    return jnp.dot(a, b, preferred_element_type=F32)


def _mm_nt(a, b):
    return lax.dot_general(a, b, (((1,), (1,)), ((), ())), preferred_element_type=F32)


def _mm_tn(a, b):
    return lax.dot_general(a, b, (((0,), (0,)), ((), ())), preferred_element_type=F32)


def _rms(x, g):
    rstd = lax.rsqrt(jnp.mean(x * x, axis=-1, keepdims=True) + EPS)
    xhat = x * rstd
    return xhat * g, xhat, rstd


def _rms_bwd(dy, xhat, rstd, g):
    gy = dy * g
    dx = rstd * (gy - xhat * jnp.mean(gy * xhat, axis=-1, keepdims=True))
    return dx, jnp.sum(dy * xhat, axis=0, keepdims=True)


def _gelu(z):
    t = jnp.tanh(0.7978845608028654 * (z + 0.044715 * z * z * z))
    return 0.5 * z * (1.0 + t), t


def _gelu_grad(z, t):
    return 0.5 * (1.0 + t) + 0.5 * z * (1.0 - t * t) * 0.7978845608028654 * (1.0 + 3.0 * 0.044715 * z * z)


def _softplus_neg(lam):
    x = -lam
    e = jnp.exp(-jnp.abs(x))
    u = 1.0 + e
    l1p = jnp.where(u == 1.0, e, jnp.log(u) * e / (u - 1.0))
    return jnp.maximum(x, 0.0) + l1p


def _one_minus_square(la, a):
    x = 2.0 * la
    series = -x * (1.0 + x * (0.5 + x * (1.0 / 6.0 + x * (1.0 / 24.0))))
    return jnp.where(x > -0.06, series, 1.0 - a * a)


def _sigmoid(x):
    return 0.5 * jnp.tanh(0.5 * x) + 0.5


def _down(v, d):
    return pltpu.roll(v, d, 0)


def _up(v, d):
    return pltpu.roll(v, v.shape[0] - d, 0)


def _slab_major(s, dtype=F32):
    return pltpu.HBM((N_SLAB, s, SLAB), dtype)


def _slabs_spec(tb):
    return pl.BlockSpec((N_SLAB, tb, SLAB), lambda i: (0, i, 0))


def _read_slabs(ref):
    return jnp.concatenate([ref[k] for k in range(N_SLAB)], axis=1)


def _write_slabs(ref, value):
    for k in range(N_SLAB):
        ref[k] = value[:, k * SLAB:(k + 1) * SLAB]


def _token_block(s, most=512):
    for rows in (most, 512, 256):
        if rows <= most and s % rows == 0 and s > rows:
            return rows
    return s


def _time_chunk(s):
    return 512 if s % 512 == 0 and s > 512 else 256 if s % 256 == 0 else s


def _fwd_in(x, ln1_g, w_in_t):
    s = x.shape[0]
    tb = _token_block(s, 1024)

    def body(x_ref, g_ref, w_ref, up_ref, ul_ref, ug_ref):
        n, _, _ = _rms(x_ref[...], g_ref[...])
        proj = _mm_nt(n.astype(BF16), w_ref[...])
        _write_slabs(up_ref, proj[:, :POOL_WIDTH])
        _write_slabs(ul_ref, proj[:, POOL_WIDTH:POOL_WIDTH + LRU_WIDTH])
        _write_slabs(ug_ref, proj[:, POOL_WIDTH + LRU_WIDTH:])

    out = _slab_major(s)
    return pl.pallas_call(
        body, name="fwd_in", grid=(s // tb,),
        in_specs=[pl.BlockSpec((tb, D_MODEL), lambda i: (i, 0)),
                  pl.BlockSpec((1, D_MODEL), lambda i: (0, 0)),
                  pl.BlockSpec((IN_WIDTH, D_MODEL), lambda i: (0, 0))],
        out_specs=[_slabs_spec(tb)] * 3,
        out_shape=[out, out, out],
        compiler_params=_params(40, 1),
    )(*_in_hbm(x, ln1_g, w_in_t))


def _pool_denominator(t0, row, window):
    return jnp.minimum((t0 + row + 1).astype(F32), window)


def _causal_window(ext, deeper):
    s = ext + _down(ext, 1)
    s = s + deeper[0] * _down(s, 2)
    s = s + deeper[1] * _down(s, 4)
    s = s + deeper[2] * _down(s, 8)
    return s[HALO:]


def _anticausal_window(ext, deeper, rows):
    s = ext + _up(ext, 1)
    s = s + deeper[0] * _up(s, 2)
    s = s + deeper[1] * _up(s, 4)
    s = s + deeper[2] * _up(s, 8)
    return s[:rows]


def _conv(taps, cw, cb):
    return cw[3] * taps[0] + cw[2] * taps[1] + cw[1] * taps[2] + cw[0] * taps[3] + cb


def _lru_gates(xc, wa, ba, wi, bi, sp):
    xb = xc.astype(BF16)
    r = _sigmoid(_mm(xb, wa) + ba)
    i = _sigmoid(_mm(xb, wi) + bi)
    la = (-LRU_C) * r * sp
    a = jnp.exp(la)
    mult = jnp.sqrt(jnp.maximum(_one_minus_square(la, a), 0.0))
    return xb, r, i, a, mult


SUBLANES = 8


def _scan_pads(a_pad, b_pad, rows, causal):
    spare = slice(0, SUBLANES) if causal else slice(rows, rows + SUBLANES)
    a_pad[spare, :] = jnp.ones((SUBLANES, SLAB), F32)
    b_pad[spare, :] = jnp.zeros((SUBLANES, SLAB), F32)


def _scan_causal(a, b, h_prev, a_pad, b_pad, rows):
    d = 1
    while d < min(SUBLANES, rows):
        a_pad[SUBLANES:, :] = a
        b_pad[SUBLANES:, :] = b
        b = a * b_pad[SUBLANES - d:SUBLANES - d + rows, :] + b
        a = a * a_pad[SUBLANES - d:SUBLANES - d + rows, :]
        d *= 2
    h = jnp.broadcast_to(h_prev, (d, SLAB))
    out = []
    for at in range(0, rows, d):
        h = a[at:at + d] * h + b[at:at + d]
        out.append(h)
    return jnp.concatenate(out, axis=0)


def _scan_anticausal(a, b, l_next, a_pad, b_pad, rows):
    d = 1
    while d < min(SUBLANES, rows):
        a_pad[:rows, :] = a
        b_pad[:rows, :] = b
        b = a * b_pad[d:d + rows, :] + b
        a = a * a_pad[d:d + rows, :]
        d *= 2
    lead = jnp.broadcast_to(l_next, (d, SLAB))
    out = []
    for at in range(rows - d, -1, -d):
        lead = a[at:at + d] * lead + b[at:at + d]
        out.append(lead)
    return jnp.concatenate(out[::-1], axis=0)


def _slab_scalars():
    slab = pl.program_id(0)
    deeper = [jnp.where(slab > k, 1.0, 0.0).astype(F32) for k in range(N_SLAB - 1)]
    window = jnp.left_shift(jnp.int32(2), slab).astype(F32)
    inverse = jnp.where(slab == 0, 0.5, jnp.where(slab == 1, 0.25, jnp.where(slab == 2, 0.125, 0.0625))).astype(F32)
    return deeper, (window, inverse)


def _window_mean(total, t0, row, window, at_start):
    if at_start:
        return total / _pool_denominator(t0, row, window[0])
    return total * window[1]


def _slab_specs(s):
    seq = pl.BlockSpec((None, s, SLAB), lambda k: (k, 0, 0))
    mat = pl.BlockSpec((1, SLAB, SLAB), lambda k: (k, 0, 0))
    vec = pl.BlockSpec((1, SLAB), lambda k: (0, k))
    taps = pl.BlockSpec((CONV_WIDTH, SLAB), lambda k: (0, k))
    return seq, mat, vec, taps


def _mixer_fwd(u_pool, u_lru, u_gate, pool_w, pool_scale, conv_w, conv_b, wa_bd, b_a, wi_bd, b_i, lam):
    s = u_pool.shape[1]
    tc = _time_chunk(s)
    n_chunks = s // tc

    def body(up_ref, ul_ref, ug_ref, pw_ref, ps_ref, cw_ref, cb_ref, wa_ref, ba_ref, wi_ref, bi_ref, lam_ref,
             yp_ref, h_ref, yl_ref, shift_ref, a_pad, b_pad):
        _scan_pads(a_pad, b_pad, tc, causal=True)
        deeper, window = _slab_scalars()
        pw = pw_ref[0].astype(BF16)
        wa = wa_ref[0].astype(BF16)
        wi = wi_ref[0].astype(BF16)
        ps, cb, ba, bi = ps_ref[...], cb_ref[...], ba_ref[...], bi_ref[...]
        cw = [cw_ref[k:k + 1, :] for k in range(CONV_WIDTH)]
        sp = _softplus_neg(lam_ref[...])
        row = lax.broadcasted_iota(jnp.int32, (tc, SLAB), 0)

        def chunk(t0, ext_p, ext_l, h_prev, at_start=False):
            rows = pl.ds(t0, tc)
            d = _window_mean(_causal_window(ext_p, deeper), t0, row, window, at_start) - ext_p[HALO:]
            yp_ref[rows, :] = _mm(d.astype(BF16), pw) * ps
            shift_ref[...] = ext_l
            xc = _conv([shift_ref[HALO - k:HALO - k + tc, :] for k in range(CONV_WIDTH)], cw, cb)
            _, _, i, a, mult = _lru_gates(xc, wa, ba, wi, bi, sp)
            h = _scan_causal(a, mult * (i * xc), h_prev, a_pad, b_pad, tc)
            h_ref[rows, :] = h
            yl_ref[rows, :] = h * _gelu(ug_ref[rows, :])[0]
            return h[tc - 1:tc, :]

        pad = jnp.zeros((HALO, SLAB), F32)
        h0 = chunk(0, jnp.concatenate([pad, up_ref[pl.ds(0, tc), :]], axis=0),
                   jnp.concatenate([pad, ul_ref[pl.ds(0, tc), :]], axis=0), jnp.zeros((1, SLAB), F32), at_start=True)

        def step(c, h_prev):
            t0 = pl.multiple_of(c * tc, tc)
            ext = pl.ds(pl.multiple_of(c * tc - HALO, HALO), tc + HALO)
            return chunk(t0, up_ref[ext, :], ul_ref[ext, :], h_prev)

        lax.fori_loop(1, n_chunks, step, h0)

    seq, mat, vec, taps = _slab_specs(s)
    out = _slab_major(s)
    return pl.pallas_call(
        body, name="mixer_fwd", grid=(N_SLAB,),
        in_specs=[seq, seq, seq, mat, vec, taps, vec, mat, vec, mat, vec, vec],
        out_specs=[seq, seq, seq], out_shape=[out, out, out],
        scratch_shapes=[pltpu.VMEM((tc + HALO, SLAB), F32), pltpu.VMEM((tc + SUBLANES, SLAB), F32),
                        pltpu.VMEM((tc + SUBLANES, SLAB), F32)],
        compiler_params=_params(48, 1),
    )(*_in_hbm(u_pool, u_lru, u_gate, pool_w, pool_scale, conv_w, conv_b, wa_bd, b_a, wi_bd, b_i, lam))


def _fwd_out(x, y_pool, y_lru, gn_pool_g, gn_lru_g, w_out, ln2_g):
    s = x.shape[0]
    tb = _token_block(s, 1024)

    def body(x_ref, yp_ref, yl_ref, gp_ref, gl_ref, w_ref, g2_ref, h1_ref, n2_ref):
        mp, _, _ = _rms(_read_slabs(yp_ref), gp_ref[...])
        ml, _, _ = _rms(_read_slabs(yl_ref), gl_ref[...])
        h1 = x_ref[...] + _mm(mp.astype(BF16), w_ref[:POOL_WIDTH, :]) + _mm(ml.astype(BF16), w_ref[POOL_WIDTH:, :])
        h1_ref[...] = h1
        n2_ref[...] = _rms(h1, g2_ref[...])[0].astype(BF16)

    row = pl.BlockSpec((tb, D_MODEL), lambda i: (i, 0))
    half = _slabs_spec(tb)
    return pl.pallas_call(
        body, name="fwd_out", grid=(s // tb,),
        in_specs=[row, half, half, pl.BlockSpec((1, 512), lambda i: (0, 0)), pl.BlockSpec((1, 512), lambda i: (0, 0)),
                  pl.BlockSpec((D_MODEL, D_MODEL), lambda i: (0, 0)), pl.BlockSpec((1, D_MODEL), lambda i: (0, 0))],
        out_specs=[row, row],
        out_shape=[pltpu.HBM((s, D_MODEL), F32), pltpu.HBM((s, D_MODEL), BF16)],
        compiler_params=_params(40, 1),
    )(*_in_hbm(x, y_pool, y_lru, gn_pool_g, gn_lru_g, w_out, ln2_g))


def _ffn_fwd(h1, n2, target, lnf_g, w_gate, w_up, w_down):
    s = h1.shape[0]
    tb = 512
    sub = 256
    n_ff = D_FF // FF_CHUNK

    def body(h1_ref, n2_ref, t_ref, gf_ref, wg_hbm, wu_hbm, wd_hbm,
             g_ref, u_ref, dh_ref, dhb_ref, dgf_ref, sq_ref, wg, wu, wd, sem):
        @pl.when(pl.program_id(0) == 0)
        def _():
            loads = [pltpu.make_async_copy(src, dst, sem.at[k])
                     for k, (src, dst) in enumerate(((wg_hbm, wg), (wu_hbm, wu), (wd_hbm, wd)))]
            for cp in loads:
                cp.start()
            for cp in loads:
                cp.wait()
            dgf_ref[...] = jnp.zeros_like(dgf_ref)
            sq_ref[...] = jnp.zeros_like(sq_ref)

        n2v = n2_ref[...]
        acc = jnp.zeros((tb, D_MODEL), F32)
        for c in range(n_ff):
            cols = slice(c * FF_CHUNK, (c + 1) * FF_CHUNK)
            g = _mm_nt(n2v, wg[cols, :])
            u = _mm_nt(n2v, wu[cols, :])
            g_ref[:, cols] = g.astype(BF16)
            u_ref[:, cols] = u.astype(BF16)
            act = g * jax.nn.sigmoid(g) * u
            acc = acc + _mm(act.astype(BF16), wd[cols, :])
        gf = gf_ref[...]
        for r in range(tb // sub):
            rows = slice(r * sub, (r + 1) * sub)
            y, xhat, rstd = _rms(h1_ref[rows, :] + acc[rows, :], gf)
            err = y - t_ref[rows, :]
            sq_ref[...] += jnp.sum(err * err, axis=0, keepdims=True)
            dh2, dgf = _rms_bwd(err * (1.0 / D_MODEL), xhat, rstd, gf)
            dgf_ref[...] += dgf
            dh_ref[rows, :] = dh2
            dhb_ref[rows, :] = dh2.astype(BF16)

    row = pl.BlockSpec((tb, D_MODEL), lambda i: (i, 0))
    ff = pl.BlockSpec((tb, D_FF), lambda i: (i, 0))
    vec = pl.BlockSpec((1, D_MODEL), lambda i: (0, 0))
    anyspace = pl.BlockSpec(memory_space=pl.ANY)
    return pl.pallas_call(
        body, name="ffn_fwd", grid=(s // tb,),
        in_specs=[row, row, row, vec, anyspace, anyspace, anyspace],
        out_specs=[ff, ff, row, row, vec, vec],
        out_shape=[pltpu.HBM((s, D_FF), BF16), pltpu.HBM((s, D_FF), BF16),
                   pltpu.HBM((s, D_MODEL), F32), pltpu.HBM((s, D_MODEL), BF16),
                   jax.ShapeDtypeStruct((1, D_MODEL), F32), jax.ShapeDtypeStruct((1, D_MODEL), F32)],
        scratch_shapes=[pltpu.VMEM((D_FF, D_MODEL), BF16), pltpu.VMEM((D_FF, D_MODEL), BF16),
                        pltpu.VMEM((D_FF, D_MODEL), BF16), pltpu.SemaphoreType.DMA((3,))],
        compiler_params=_params(60, 1),
    )(*_in_hbm(h1, n2, target, lnf_g, w_gate, w_up, w_down))


def _ffn_bwd(n2, dh2b, g, u, w_gate_t, w_up_t, w_down):
    s = n2.shape[0]
    tb = min(1024, s)
    n_ff = D_FF // FF_CHUNK
    n_tb = s // tb

    def body(n2_ref, dh_ref, g_ref, u_ref, wg_ref, wu_ref, wd_ref, dwg_ref, dwu_ref, dwd_ref, dn2_ref,
             dn2_acc, acc_g, acc_u, acc_d):
        j = pl.program_id(0)
        t = pl.program_id(1)
        rows = pl.ds(pl.multiple_of(t * tb, tb), tb)

        @pl.when(t == 0)
        def _():
            acc_g[...] = jnp.zeros_like(acc_g)
            acc_u[...] = jnp.zeros_like(acc_u)
            acc_d[...] = jnp.zeros_like(acc_d)

        @pl.when(j == 0)
        def _():
            dn2_acc[rows, :] = jnp.zeros((tb, D_MODEL), F32)

        n2v = n2_ref[...]
        dh = dh_ref[...]
        gv = g_ref[...].astype(F32)
        uv = u_ref[...].astype(F32)
        sg = jax.nn.sigmoid(gv)
        silu = gv * sg
        dact = _mm_nt(dh, wd_ref[...])
        dub = (dact * silu).astype(BF16)
        dgb = (dact * uv * (sg * (1.0 + gv * (1.0 - sg)))).astype(BF16)
        acc_d[...] += _mm_tn((silu * uv).astype(BF16), dh)
        acc_g[...] += _mm_tn(dgb, n2v)
        acc_u[...] += _mm_tn(dub, n2v)
        dn2_acc[rows, :] += _mm(jnp.concatenate([dgb, dub], axis=1),
                                jnp.concatenate([wg_ref[...], wu_ref[...]], axis=0))

        @pl.when(t == n_tb - 1)
        def _():
            dwg_ref[...] = acc_g[...].astype(BF16)
            dwu_ref[...] = acc_u[...].astype(BF16)
            dwd_ref[...] = acc_d[...].astype(BF16)

        @pl.when(j == n_ff - 1)
        def _():
            dn2_ref[...] = dn2_acc[rows, :]

    row = pl.BlockSpec((tb, D_MODEL), lambda j, t: (t, 0))
    act = pl.BlockSpec((tb, FF_CHUNK), lambda j, t: (t, j))
    w_row = pl.BlockSpec((FF_CHUNK, D_MODEL), lambda j, t: (j, 0))
    last = pl.BlockSpec((tb, D_MODEL), lambda j, t: (jnp.where(j == n_ff - 1, t, 0), 0))
    grad = pltpu.HBM((D_FF, D_MODEL), BF16)
    chunk_acc = pltpu.VMEM((FF_CHUNK, D_MODEL), F32)
    return pl.pallas_call(
        body, name="ffn_bwd", grid=(n_ff, n_tb),
        in_specs=[row, row, act, act, w_row, w_row, w_row],
        out_specs=[w_row, w_row, w_row, last],
        out_shape=[grad, grad, grad, pltpu.HBM((s, D_MODEL), F32)],
        scratch_shapes=[pltpu.VMEM((s, D_MODEL), F32), chunk_acc, chunk_acc, chunk_acc],
        compiler_params=_params(56, 2),
    )(*_in_hbm(n2, dh2b, g, u, w_gate_t, w_up_t, w_down))


def _bwd_out(dn2, dh2, h1, y_pool, y_lru, gn_pool_g, gn_lru_g, w_out, ln2_g, after):
    s = h1.shape[0]
    tb = _token_block(s)

    def body(dn2_ref, dh2_ref, h1_ref, yp_ref, yl_ref, gp_ref, gl_ref, w_ref, g2_ref, _after,
             dh1_ref, dyp_ref, dyl_ref, dwb_ref, dg2_ref, dgp_ref, dgl_ref, dw_ref):
        @pl.when(pl.program_id(0) == 0)
        def _():
            dw_ref[...] = jnp.zeros_like(dw_ref)
            dg2_ref[...] = jnp.zeros_like(dg2_ref)
            dgp_ref[...] = jnp.zeros_like(dgp_ref)
            dgl_ref[...] = jnp.zeros_like(dgl_ref)

        g2 = g2_ref[...]
        _, xhat2, rstd2 = _rms(h1_ref[...], g2)
        dres, dg2 = _rms_bwd(dn2_ref[...], xhat2, rstd2, g2)
        dg2_ref[...] += dg2
        dh1 = dh2_ref[...] + dres
        dh1_ref[...] = dh1
        dh1b = dh1.astype(BF16)
        gp, gl = gp_ref[...], gl_ref[...]
        mp, xhat_p, rstd_p = _rms(_read_slabs(yp_ref), gp)
        ml, xhat_l, rstd_l = _rms(_read_slabs(yl_ref), gl)
        dw_ref[:POOL_WIDTH, :] += _mm_tn(mp.astype(BF16), dh1b)
        dw_ref[POOL_WIDTH:, :] += _mm_tn(ml.astype(BF16), dh1b)
        dyp, dgp = _rms_bwd(_mm_nt(dh1b, w_ref[:POOL_WIDTH, :]), xhat_p, rstd_p, gp)
        dyl, dgl = _rms_bwd(_mm_nt(dh1b, w_ref[POOL_WIDTH:, :]), xhat_l, rstd_l, gl)
        _write_slabs(dyp_ref, dyp)
        _write_slabs(dyl_ref, dyl)
        dgp_ref[...] += dgp
        dgl_ref[...] += dgl

        @pl.when(pl.program_id(0) == s // tb - 1)
        def _():
            dwb_ref[...] = dw_ref[...].astype(BF16)

    row = pl.BlockSpec((tb, D_MODEL), lambda i: (i, 0))
    half = _slabs_spec(tb)
    vec = pl.BlockSpec((1, D_MODEL), lambda i: (0, 0))
    hvec = pl.BlockSpec((1, 512), lambda i: (0, 0))
    mat = pl.BlockSpec((D_MODEL, D_MODEL), lambda i: (0, 0))
    return pl.pallas_call(
        body, name="bwd_out", grid=(s // tb,),
        in_specs=[row, row, row, half, half, hvec, hvec, mat, vec, pl.BlockSpec(memory_space=pl.ANY)],
        out_specs=[row, half, half, mat, vec, hvec, hvec],
        out_shape=[pltpu.HBM((s, D_MODEL), F32), _slab_major(s),
                   _slab_major(s), pltpu.HBM((D_MODEL, D_MODEL), BF16),
                   jax.ShapeDtypeStruct((1, D_MODEL), F32), jax.ShapeDtypeStruct((1, 512), F32),
                   jax.ShapeDtypeStruct((1, 512), F32)],
        scratch_shapes=[pltpu.VMEM((D_MODEL, D_MODEL), F32)],
        compiler_params=_params(48, 1),
    )(*_in_hbm(dn2, dh2, h1, y_pool, y_lru, gn_pool_g, gn_lru_g, w_out, ln2_g), after)


def _mixer_bwd(u_pool, u_lru, u_gate, h, dy_pool, dy_lru,
               pool_w, pool_scale, conv_w, conv_b, wa_bd, b_a, wi_bd, b_i, lam, after):
    s = u_pool.shape[1]
    tc = _time_chunk(s)
    n_chunks = s // tc

    def body(up_ref, ul_ref, ug_ref, h_ref, dyp_ref, dyl_ref,
             pw_ref, ps_ref, cw_ref, cb_ref, wa_ref, ba_ref, wi_ref, bi_ref, lam_ref, _after,
             dup_ref, dul_ref, dug_ref, dpw_ref, dps_ref, dcw_ref, dcb_ref, dwa_ref, dba_ref, dwi_ref, dbi_ref, dlam_ref,
             a_pad, b_pad, u_shift, h_shift, a_shift, d_shift):
        _scan_pads(a_pad, b_pad, tc, causal=False)
        deeper, window = _slab_scalars()
        pw = pw_ref[0].astype(BF16)
        wa = wa_ref[0].astype(BF16)
        wi = wi_ref[0].astype(BF16)
        ps, cb, ba, bi = ps_ref[...], cb_ref[...], ba_ref[...], bi_ref[...]
        cw = [cw_ref[k:k + 1, :] for k in range(CONV_WIDTH)]
        lam_v = lam_ref[...]
        sp = _softplus_neg(lam_v)
        row = lax.broadcasted_iota(jnp.int32, (tc, SLAB), 0)
        for ref in (dpw_ref, dps_ref, dcw_ref, dcb_ref, dwa_ref, dba_ref, dwi_ref, dbi_ref, dlam_ref):
            ref[...] = jnp.zeros_like(ref)

        def chunk(t0, ext_p, ext_l, ext_h, carry, at_start=False):
            l_next, a_next, dxc_next, ddn_next = carry
            rows = pl.ds(t0, tc)
            u_shift[...] = ext_l
            taps = [u_shift[HALO - k:HALO - k + tc, :] for k in range(CONV_WIDTH)]
            xc = _conv(taps, cw, cb)
            xb, r, i, a, mult = _lru_gates(xc, wa, ba, wi, bi, sp)
            hv = ext_h[HALO:]
            h_shift[...] = ext_h
            h_before = h_shift[HALO - 1:HALO - 1 + tc, :]
            ug = ug_ref[rows, :]
            dyl = dyl_ref[rows, :]
            gel, th = _gelu(ug)
            dug_ref[rows, :] = (dyl * hv * _gelu_grad(ug, th)).astype(BF16)
            a_shift[:tc, :] = a
            a_shift[tc:, :] = jnp.broadcast_to(a_next, (SUBLANES, SLAB))
            a_after = a_shift[1:1 + tc, :]
            l = _scan_anticausal(a_after, dyl * gel, l_next, a_pad, b_pad, tc)
            dmult = l * (i * xc)
            di = l * mult * xc
            dxc = l * mult * i
            dla = (l * h_before) * a - jnp.where(mult > 0.0, dmult * (a * a) / mult, 0.0)
            dlam_ref[...] += jnp.sum(dla * r, axis=0, keepdims=True)
            dpa = (dla * ((-LRU_C) * sp)) * (r * (1.0 - r))
            dpi = di * (i * (1.0 - i))
            dpab = dpa.astype(BF16)
            dpib = dpi.astype(BF16)
            dwa_ref[0] += _mm_tn(xb, dpab)
            dwi_ref[0] += _mm_tn(xb, dpib)
            dba_ref[...] += jnp.sum(dpa, axis=0, keepdims=True)
            dbi_ref[...] += jnp.sum(dpi, axis=0, keepdims=True)
            dxc = dxc + _mm_nt(dpab, wa) + _mm_nt(dpib, wi)
            d_shift[:tc, :] = dxc
            d_shift[tc:, :] = dxc_next
            dul_ref[rows, :] = (cw[3] * dxc + cw[2] * d_shift[1:1 + tc, :]
                                + cw[1] * d_shift[2:2 + tc, :] + cw[0] * d_shift[3:3 + tc, :]).astype(BF16)
            for k in range(CONV_WIDTH):
                dcw_ref[k:k + 1, :] += jnp.sum(dxc * taps[CONV_WIDTH - 1 - k], axis=0, keepdims=True)
            dcb_ref[...] += jnp.sum(dxc, axis=0, keepdims=True)
            db = (_window_mean(_causal_window(ext_p, deeper), t0, row, window, at_start) - ext_p[HALO:]).astype(BF16)
            dyp = dyp_ref[rows, :]
            dps_ref[...] += jnp.sum(dyp * _mm(db, pw), axis=0, keepdims=True)
            dys = (dyp * ps).astype(BF16)
            dpw_ref[0] += _mm_tn(db, dys)
            dd = _mm_nt(dys, pw)
            ddn = _window_mean(dd, t0, row, window, at_start)
            ext_q = jnp.concatenate([ddn, ddn_next], axis=0)
            dup_ref[rows, :] = (_anticausal_window(ext_q, deeper, tc) - dd).astype(BF16)
            return l[0:1, :], a[0:1, :], dxc[0:8, :], ddn[0:HALO, :]

        def step(k, carry):
            c = n_chunks - 1 - k
            t0 = pl.multiple_of(c * tc, tc)
            ext = pl.ds(pl.multiple_of(c * tc - HALO, HALO), tc + HALO)
            return chunk(t0, up_ref[ext, :], ul_ref[ext, :], h_ref[ext, :], carry)

        carry = (jnp.zeros((1, SLAB), F32), jnp.zeros((1, SLAB), F32),
                 jnp.zeros((8, SLAB), F32), jnp.zeros((HALO, SLAB), F32))
        carry = lax.fori_loop(0, n_chunks - 1, step, carry)
        pad = jnp.zeros((HALO, SLAB), F32)
        first = pl.ds(0, tc)
        chunk(0, jnp.concatenate([pad, up_ref[first, :]], axis=0), jnp.concatenate([pad, ul_ref[first, :]], axis=0),
              jnp.concatenate([pad, h_ref[first, :]], axis=0), carry, at_start=True)
        dlam_ref[...] = dlam_ref[...] * (LRU_C * jax.nn.sigmoid(-lam_v))

    seq, mat, vec, taps = _slab_specs(s)
    grad = _slab_major(s, BF16)
    mats = jax.ShapeDtypeStruct((N_SLAB, SLAB, SLAB), F32)
    vecs = jax.ShapeDtypeStruct((1, 512), F32)
    return pl.pallas_call(
        body, name="mixer_bwd", grid=(N_SLAB,),
        in_specs=[seq] * 6 + [mat, vec, taps, vec, mat, vec, mat, vec, vec, pl.BlockSpec(memory_space=pl.ANY)],
        out_specs=[seq, seq, seq, mat, vec, taps, vec, mat, vec, mat, vec, vec],
        out_shape=[grad, grad, grad, mats, vecs, jax.ShapeDtypeStruct((CONV_WIDTH, 512), F32), vecs,
                   mats, vecs, mats, vecs, vecs],
        scratch_shapes=[pltpu.VMEM((tc + SUBLANES, SLAB), F32), pltpu.VMEM((tc + SUBLANES, SLAB), F32),
                        pltpu.VMEM((tc + HALO, SLAB), F32), pltpu.VMEM((tc + HALO, SLAB), F32),
                        pltpu.VMEM((tc + SUBLANES, SLAB), F32), pltpu.VMEM((tc + SUBLANES, SLAB), F32)],
        compiler_params=_params(56, 1),
    )(*_in_hbm(u_pool, u_lru, u_gate, h, dy_pool, dy_lru, pool_w, pool_scale, conv_w, conv_b, wa_bd, b_a, wi_bd,
               b_i, lam), after)


def _bwd_in(x, dh1, du_pool, du_lru, du_gate, ln1_g, w_in_t, after):
    s = x.shape[0]
    tb = _token_block(s)

    def body(x_ref, dh1_ref, dup_ref, dul_ref, dug_ref, g_ref, w_ref, _after, dx_ref, dwb_ref, dg_ref, dw_ref):
        @pl.when(pl.program_id(0) == 0)
        def _():
            dw_ref[...] = jnp.zeros_like(dw_ref)
            dg_ref[...] = jnp.zeros_like(dg_ref)

        g1 = g_ref[...]
        n, xhat, rstd = _rms(x_ref[...], g1)
        nb = n.astype(BF16)
        db = jnp.concatenate([_read_slabs(ref) for ref in (dup_ref, dul_ref, dug_ref)], axis=1)
        dw_ref[...] += _mm_tn(db, nb)
        dx, dg1 = _rms_bwd(_mm(db, w_ref[...]), xhat, rstd, g1)
        dx_ref[...] = dh1_ref[...] + dx
        dg_ref[...] += dg1

        @pl.when(pl.program_id(0) == s // tb - 1)
        def _():
            dwb_ref[...] = dw_ref[...].astype(BF16)

    row = pl.BlockSpec((tb, D_MODEL), lambda i: (i, 0))
    half = _slabs_spec(tb)
    vec = pl.BlockSpec((1, D_MODEL), lambda i: (0, 0))
    mat = pl.BlockSpec((IN_WIDTH, D_MODEL), lambda i: (0, 0))
    return pl.pallas_call(
        body, name="bwd_in", grid=(s // tb,),
        in_specs=[row, row, half, half, half, vec, mat, pl.BlockSpec(memory_space=pl.ANY)],
        out_specs=[row, mat, vec],
        out_shape=[pltpu.HBM((s, D_MODEL), F32), pltpu.HBM((IN_WIDTH, D_MODEL), BF16),
                   jax.ShapeDtypeStruct((1, D_MODEL), F32)],
        scratch_shapes=[pltpu.VMEM((IN_WIDTH, D_MODEL), F32)],
        compiler_params=_params(48, 1),
    )(*_in_hbm(x, dh1, du_pool, du_lru, du_gate, ln1_g, w_in_t), after)


def _mesh_position():
    x, y, c = lax.axis_index("x"), lax.axis_index("y"), lax.axis_index("c")
    return x, y, c, 4 * x + 2 * y + c


def _peer(x, y, c, p):
    px = 1 - x if p & 4 else x
    py = 1 - y if p & 2 else y
    pc = 1 - c if p & 1 else c
    return (px, py, pc), 4 * px + 2 * py + pc


HBM_SPEC = pl.BlockSpec(memory_space=pltpu.HBM)
SEM_SPEC = pl.BlockSpec(memory_space=pltpu.SEMAPHORE)
DATAFLOW = pltpu.SideEffectType.DATAFLOW_SIDE_EFFECTING


class Copy(NamedTuple):
    src: int
    src_at: Any
    dst: int
    dst_at: Any
    peer: int
    group: int
    slot: int


SIBLING = (1,)
SAME_CORE = (2, 4, 6)
EVERYONE = tuple(range(1, N_DEV))
MYSELF = (0,)


def _same(index):
    return index


def _chip(index):
    return jnp.right_shift(index, 1)


def _fan_out(srcs, lands, peers, group):
    return [Copy(s, None, d, _same, p, group, N_DEV * i + p) for i, (s, d) in enumerate(zip(srcs, lands)) for p in peers]


def _scatter(stacks, lands, peers):
    return [Copy(s, lambda me, p=p: jnp.bitwise_xor(me, p), d, _same, p, 0, 0)
            for s, d in zip(stacks, lands) for p in peers]


def _numbered(copies, group=0):
    return [cp._replace(group=group, slot=i) for i, cp in enumerate(copies)]


def _relay(lands, peers):
    return [Copy(b, lambda s, q=q: jnp.bitwise_xor(s, q), b, lambda s, q=q: jnp.bitwise_xor(s, q), 1, 0, N_DEV * i + q)
            for i, b in enumerate(lands) for q in peers]


def _to_sibling(stacks, lands):
    return [Copy(s, lambda me, k=k: 2 * k + 1 - jnp.bitwise_and(me, 1), d, lambda me, k=k: k, 1, 0, 4 * i + k)
            for i, (s, d) in enumerate(zip(stacks, lands)) for k in range(N_DEV // 2)]


def _to_chips(sums, lands):
    return [Copy(s, lambda me, p=p: jnp.bitwise_xor(_chip(me), p // 2), d, _chip, p, 0, 4 * i + p // 2)
            for i, (s, d) in enumerate(zip(sums, lands)) for p in SAME_CORE]


def _comm_call(name, bufs, wait=None, start=None, after=()):
    nb = len(bufs)
    waits = [wait] if isinstance(wait, tuple) else list(wait or [])
    slots = list(start[1]) if start else []
    n_out_sem = 2 * len(slots)
    after = [a for a in (after if isinstance(after, (list, tuple)) else [after]) if a is not None]

    def body(*refs):
        b = refs[:nb]
        at = nb + 2 * len(waits) + len(after)
        out_sems = refs[at:at + n_out_sem]
        token = refs[at + n_out_sem + nb]
        x, y, c, me = _mesh_position()

        def part(i, row_of, sender):
            return b[i] if row_of is None else b[i].at[row_of(sender)]

        for k, (_, copies) in enumerate(waits):
            w_send, w_recv = refs[nb + 2 * k], refs[nb + 2 * k + 1]
            for cp in copies:
                peer, peer_index = _peer(x, y, c, cp.peer)
                arrival = pltpu.make_async_remote_copy(part(cp.src, cp.src_at, me), part(cp.dst, cp.dst_at, peer_index),
                                                       w_send.at[cp.slot], w_recv.at[cp.slot],
                                                       device_id=peer, device_id_type=MESH)
                arrival.wait_send()
                arrival.wait_recv()
        if start:
            for cp in start[0]:
                peer, _ = _peer(x, y, c, cp.peer)
                pltpu.make_async_remote_copy(part(cp.src, cp.src_at, me), part(cp.dst, cp.dst_at, me),
                                             out_sems[2 * cp.group].at[cp.slot], out_sems[2 * cp.group + 1].at[cp.slot],
                                             device_id=peer, device_id_type=MESH).start()
        token[...] = jnp.zeros_like(token)

    sem_shapes = []
    for n_slots in slots:
        sem_shapes += [pltpu.SemaphoreType.DMA((n_slots,))] * 2
    operands = [pltpu.with_memory_space_constraint(a, pltpu.HBM) for a in bufs]
    in_specs = [HBM_SPEC] * nb
    for sems, _ in waits:
        operands += list(sems)
        in_specs += [SEM_SPEC, SEM_SPEC]
    operands += after
    in_specs += [pl.BlockSpec(memory_space=pl.ANY)] * len(after)
    outs = pl.pallas_call(
        body, name=name, in_specs=in_specs,
        out_specs=[SEM_SPEC] * n_out_sem + [HBM_SPEC] * nb + [pl.BlockSpec(memory_space=pltpu.VMEM)],
        out_shape=sem_shapes + [pltpu.HBM(a.shape, a.dtype) for a in bufs] + [jax.ShapeDtypeStruct((8, SLAB), F32)],
        input_output_aliases={i: n_out_sem + i for i in range(nb)},
        compiler_params=pltpu.CompilerParams(has_side_effects=DATAFLOW),
    )(*operands)
    sems = [(outs[2 * k], outs[2 * k + 1]) for k in range(len(slots))]
    return sems, list(outs[n_out_sem:n_out_sem + nb]), outs[-1]


def _pair_sum(stacks, lands, place):
    n = len(stacks)

    def body(place_ref, *refs):
        k = pl.program_id(0)
        for m in range(n):
            mine, theirs, out, land = refs[m], refs[n + m], refs[2 * n + m], refs[3 * n + m]
            total = (mine[0, 0].astype(F32) + theirs[0].astype(F32)).astype(out.dtype)
            out[0] = total

            @pl.when(k == place_ref[1])
            def _():
                land[0] = total

    in_specs = [pl.BlockSpec((1, 1) + a.shape[1:], lambda k, place_ref: (k, place_ref[0], 0, 0)) for a in stacks]
    in_specs += [pl.BlockSpec((1,) + a.shape[1:], lambda k, place_ref: (k, 0, 0)) for a in lands]
    out_specs = [pl.BlockSpec((1,) + a.shape[1:], lambda k, place_ref: (k, 0, 0)) for a in lands]
    out_specs += [pl.BlockSpec((1,) + a.shape[1:], lambda k, place_ref: (place_ref[1], 0, 0)) for a in lands]
    outs = pl.pallas_call(
        body, name="pair_sum_" + "_".join(str(a.shape[1]) for a in stacks),
        grid_spec=pltpu.PrefetchScalarGridSpec(num_scalar_prefetch=1, grid=(N_DEV // 2,), in_specs=in_specs,
                                               out_specs=out_specs),
        out_shape=[pltpu.HBM(a.shape, a.dtype) for a in lands] * 2,
        compiler_params=_params(40, 1),
    )(place, *_in_hbm(*[a.reshape((N_DEV // 2, 2) + a.shape[1:]) for a in stacks], *lands))
    return list(outs[:n]), list(outs[n:])


def _reduce_adam(parts, w, m, v, name):
    rows, cols = w.shape
    n_parts = parts.shape[0]
    rb = rows
    for cand in (256, 176, 128):
        if rows % cand == 0 and rows > cand:
            rb = cand
            break

    def body(p_ref, w_ref, m_ref, v_ref, g_out, d_out, m_out, v_out):
        g = p_ref[0].astype(F32)
        for j in range(1, n_parts):
            g = g + p_ref[j].astype(F32)
        g_out[...] = g
        d_out[...], m_out[...], v_out[...] = _adam(g, w_ref[...], m_ref[...], v_ref[...])

    blk = pl.BlockSpec((rb, cols), lambda i: (i, 0))
    out = jax.ShapeDtypeStruct((rows, cols), F32)
    return pl.pallas_call(
        body, name=name, grid=(rows // rb,),
        in_specs=[pl.BlockSpec((n_parts, rb, cols), lambda i: (0, i, 0)), blk, blk, blk],
        out_specs=[blk] * 4, out_shape=[out] * 4,
        compiler_params=_params(40, 1),
    )(*_in_hbm(parts, w, m, v))


def _cols_from_stack(stack):
    n, r, c = stack.shape
    return jnp.transpose(stack, (1, 0, 2)).reshape(r, n * c)


def _block_diag(w):
    z = jnp.zeros((N_SLAB, 64, 64), w.dtype)
    pairs = w.reshape(N_SLAB, 2, 64, 64)
    top = jnp.concatenate([pairs[:, 0], z], axis=2)
    bottom = jnp.concatenate([z, pairs[:, 1]], axis=2)
    return jnp.concatenate([top, bottom], axis=1)


def _adam(g, w, m, v):
    m_new = ADAM_B1 * m + (1.0 - ADAM_B1) * g
    v_new = ADAM_B2 * v + (1.0 - ADAM_B2) * (g * g)
    m_hat = m_new / (1.0 - ADAM_B1 ** ADAM_STEP)
    v_hat = v_new / (1.0 - ADAM_B2 ** ADAM_STEP)
    return (-ADAM_LR) * (m_hat / (jnp.sqrt(v_hat) + ADAM_EPS) + ADAM_WD * w), m_new, v_new


WIDE = ("ln2_g", "lnf_g")
HALF = ("pool_scale", "conv_b", "b_a", "b_i", "lam", "gn_pool_g", "gn_lru_g")
VECTORS = [(k, D_MODEL) for k in WIDE] + [(k, 512) for k in HALF]
VECTOR_ROWS = sum(width // SLAB for _, width in VECTORS)
LOSS_ROW = -(-VECTOR_ROWS // 8) * 8
CONV_AT = LOSS_ROW + 8
CONV_LANES = LRU_WIDTH // SLAB
PACK_F_ROWS = CONV_AT + CONV_WIDTH * CONV_LANES
MATRIX_ROWS = N_SLAB * SLAB
HEAD = SLAB // 2
GATE_ROWS = N_SLAB * HEAD
PACK_B_ROWS = MATRIX_ROWS + 2 * GATE_ROWS


def _pack_small(vectors, pool_g, wa_g, wi_g, conv, sq):
    n_vec = len(vectors)

    def body(*refs):
        vec = refs[:n_vec]
        pw_ref, wa_ref, wi_ref, cw_ref, sq_ref, out, out_b = refs[n_vec:]
        out[...] = jnp.zeros_like(out)
        row = 0
        for ref, (_, width) in zip(vec, VECTORS):
            for k in range(width // SLAB):
                out[row:row + 1, :] = ref[:, k * SLAB:(k + 1) * SLAB]
                row += 1
        for tap in range(CONV_WIDTH):
            for k in range(CONV_LANES):
                at = CONV_AT + tap * CONV_LANES + k
                out[at:at + 1, :] = cw_ref[tap:tap + 1, k * SLAB:(k + 1) * SLAB]
        total = sq_ref[:, 0:SLAB]
        for k in range(1, D_MODEL // SLAB):
            total = total + sq_ref[:, k * SLAB:(k + 1) * SLAB]
        out[LOSS_ROW:LOSS_ROW + 1, :] = total
        left = lax.broadcasted_iota(jnp.int32, (HEAD, SLAB), 1) < HEAD
        for s in range(N_SLAB):
            out_b[s * SLAB:(s + 1) * SLAB, :] = pw_ref[s].astype(BF16)
            for i, ref in enumerate((wa_ref, wi_ref)):
                at = MATRIX_ROWS + i * GATE_ROWS + s * HEAD
                out_b[at:at + HEAD, :] = jnp.where(left, ref[s, 0:HEAD, :], ref[s, HEAD:SLAB, :]).astype(BF16)

    return pl.pallas_call(
        body, name="pack_small",
        out_shape=[jax.ShapeDtypeStruct((PACK_F_ROWS, SLAB), F32), jax.ShapeDtypeStruct((PACK_B_ROWS, SLAB), BF16)],
    )(*vectors, pool_g, wa_g, wi_g, conv, sq)


def _small_reduce_adam(my_index, parts, parts_b, vec_w, vec_m, vec_v, pool_wmv, wa_wmv, wi_wmv, conv_wmv):
    n_vec = len(VECTORS)
    n_parts = parts.shape[0]

    def body(*refs):
        me_ref, p_ref, pb_ref = refs[0], refs[1], refs[2]
        refs = refs[1:]
        w_refs, m_refs, v_refs = (refs[2 + k * n_vec:2 + (k + 1) * n_vec] for k in range(3))
        at = 2 + 3 * n_vec
        pw_w, pw_m, pw_v = refs[at:at + 3]
        gates = (refs[at + 3:at + 6], refs[at + 6:at + 9])
        cv_w, cv_m, cv_v = refs[at + 9:at + 12]
        outs = refs[at + 12:-1]
        total = refs[-1]
        total[...] = p_ref[0]
        for j in range(1, n_parts):
            total[...] += p_ref[j]
        row = 0
        for i, (_, width) in enumerate(VECTORS):
            n_rows = width // SLAB
            g = jnp.concatenate([total[row + k:row + k + 1, :] for k in range(n_rows)], axis=1)
            row += n_rows
            d, m_new, v_new = _adam(g, w_refs[i][...], m_refs[i][...], v_refs[i][...])
            for ref, val in zip(outs[4 * i:4 * i + 4], (g, d, m_new, v_new)):
                ref[...] = val
        tail = outs[4 * n_vec:]

        def summed(first, count):
            g = pb_ref[0, first:first + count, :].astype(F32)
            for j in range(1, n_parts):
                g = g + pb_ref[j, first:first + count, :].astype(F32)
            return g

        g = summed(0, MATRIX_ROWS)
        d, m_new, v_new = _adam(g, pw_w[...], pw_m[...], pw_v[...])
        for ref, val in zip(tail[0:4], (g, d, m_new, v_new)):
            ref[...] = val
        for i, (w, m, v) in enumerate(gates):
            g = summed(MATRIX_ROWS + i * GATE_ROWS, GATE_ROWS)
            for s in range(N_SLAB):
                for hd in range(2):
                    block = g[s * HEAD:(s + 1) * HEAD, hd * HEAD:(hd + 1) * HEAD]
                    rows = slice((2 * s + hd) * HEAD, (2 * s + hd + 1) * HEAD)
                    d, m_new, v_new = _adam(block, w[rows, :], m[rows, :], v[rows, :])
                    for ref, val in zip(tail[4 + 4 * i:8 + 4 * i], (block, d, m_new, v_new)):
                        ref[rows, :] = val
        me = me_ref[0, 0]
        taps = []
        for tap in range(CONV_WIDTH):
            row = total[pl.ds(CONV_AT + tap * CONV_LANES + me // 2, 1), :]
            taps.append(jnp.where(me % 2 == 0, row[:, :HEAD], row[:, HEAD:]))
        g = jnp.concatenate(taps, axis=0)
        d, m_new, v_new = _adam(g, cv_w[...], cv_m[...], cv_v[...])
        for ref, val in zip(tail[12:16], (g, d, m_new, v_new)):
            ref[...] = val
        tail[16][...] = (0.5 / D_MODEL) * jnp.sum(total[LOSS_ROW:LOSS_ROW + 1, :], axis=1, keepdims=True)

    out_shape = []
    for _, width in VECTORS:
        out_shape += [jax.ShapeDtypeStruct((1, width), F32)] * 4
    out_shape += [jax.ShapeDtypeStruct((MATRIX_ROWS, SLAB), F32)] * 4
    out_shape += [jax.ShapeDtypeStruct((2 * GATE_ROWS, HEAD), F32)] * 8
    out_shape += [jax.ShapeDtypeStruct(conv_wmv[0].shape, F32)] * 4 + [jax.ShapeDtypeStruct((1, 1), F32)]
    operands = [parts, parts_b, *vec_w, *vec_m, *vec_v, *pool_wmv, *wa_wmv, *wi_wmv, *conv_wmv]

    def whole(a):
        return pl.BlockSpec(a.shape, lambda i, n=len(a.shape): (0,) * n)

    outs = pl.pallas_call(
        body, name="adam_small", grid=(1,),
        in_specs=[pl.BlockSpec(memory_space=pltpu.SMEM)] + [whole(a) for a in operands],
        out_specs=[whole(a) for a in out_shape], out_shape=out_shape,
        scratch_shapes=[pltpu.VMEM((PACK_F_ROWS, SLAB), F32)],
        compiler_params=_params(40, 1),
    )(my_index, *_in_hbm(*operands))
    vec_out = [tuple(outs[4 * i:4 * i + 4]) for i in range(n_vec)]
    tail = outs[4 * n_vec:]
    return vec_out, tuple(tail[0:4]), tuple(tail[4:8]), tuple(tail[8:12]), tuple(tail[12:16]), tail[16]


def kernel(x, ln1_g, w_in, pool_w, pool_scale, conv_w, conv_b, w_a, b_a, w_i, b_i, lam, gn_pool_g, gn_lru_g, w_out, ln2_g, w_ffn_gate, w_ffn_up, w_ffn_down, lnf_g, loss_target, m_ln1_g, m_w_in, m_pool_w, m_pool_scale, m_conv_w, m_conv_b, m_w_a, m_b_a, m_w_i, m_b_i, m_lam, m_gn_pool_g, m_gn_lru_g, m_w_out, m_ln2_g, m_w_ffn_gate, m_w_ffn_up, m_w_ffn_down, m_lnf_g, v_ln1_g, v_w_in, v_pool_w, v_pool_scale, v_conv_w, v_conv_b, v_w_a, v_b_a, v_w_i, v_b_i, v_lam, v_gn_pool_g, v_gn_lru_g, v_w_out, v_ln2_g, v_w_ffn_gate, v_w_ffn_up, v_w_ffn_down, v_lnf_g):
    weights = dict(ln1_g=ln1_g, w_in=w_in, pool_w=pool_w, pool_scale=pool_scale, conv_w=conv_w, conv_b=conv_b,
                   w_a=w_a, b_a=b_a, w_i=w_i, b_i=b_i, lam=lam, gn_pool_g=gn_pool_g, gn_lru_g=gn_lru_g,
                   w_out=w_out, ln2_g=ln2_g, w_ffn_gate=w_ffn_gate, w_ffn_up=w_ffn_up, w_ffn_down=w_ffn_down,
                   lnf_g=lnf_g)
    mom1 = dict(ln1_g=m_ln1_g, w_in=m_w_in, pool_w=m_pool_w, pool_scale=m_pool_scale, conv_w=m_conv_w,
                conv_b=m_conv_b, w_a=m_w_a, b_a=m_b_a, w_i=m_w_i, b_i=m_b_i, lam=m_lam, gn_pool_g=m_gn_pool_g,
                gn_lru_g=m_gn_lru_g, w_out=m_w_out, ln2_g=m_ln2_g, w_ffn_gate=m_w_ffn_gate,
                w_ffn_up=m_w_ffn_up, w_ffn_down=m_w_ffn_down, lnf_g=m_lnf_g)
    mom2 = dict(ln1_g=v_ln1_g, w_in=v_w_in, pool_w=v_pool_w, pool_scale=v_pool_scale, conv_w=v_conv_w,
                conv_b=v_conv_b, w_a=v_w_a, b_a=v_b_a, w_i=v_w_i, b_i=v_b_i, lam=v_lam, gn_pool_g=v_gn_pool_g,
                gn_lru_g=v_gn_lru_g, w_out=v_w_out, ln2_g=v_ln2_g, w_ffn_gate=v_w_ffn_gate,
                w_ffn_up=v_w_ffn_up, w_ffn_down=v_w_ffn_down, lnf_g=v_lnf_g)

    xs = x[0]
    target = loss_target[0]

    shard = dict(w_in=lambda a: a[0].T, w_ffn_gate=lambda a: a[0].T, w_ffn_up=lambda a: a[0].T,
                 w_out=lambda a: a[0], w_ffn_down=lambda a: a[0], conv_w=lambda a: a[0])
    unshard = dict(w_in=lambda a: a.T[None], w_ffn_gate=lambda a: a.T[None], w_ffn_up=lambda a: a.T[None],
                   w_out=lambda a: a[None], w_ffn_down=lambda a: a[None], conv_w=lambda a: a[None])

    gathered = ("w_in", "conv_w", "w_out", "w_ffn_gate", "w_ffn_up", "w_ffn_down")
    groups = ((0, 1), (2,), (3, 4), (5,))
    sources = [shard[k](weights[k]) if k == "conv_w" else shard[k](weights[k]).astype(BF16) for k in gathered]
    my_index = 4 * lax.axis_index("x") + 2 * lax.axis_index("y") + lax.axis_index("c")
    lands = [lax.empty((N_DEV,) + a.shape, a.dtype) for a in sources]

    def first_hop(n, at=0):
        return _numbered(_fan_out(range(at, at + n), range(at + n, at + 2 * n), SAME_CORE + SIBLING + MYSELF, 0))

    def second_hop(n, at=0):
        return _numbered(_relay(range(at, at + n), SAME_CORE))

    start = []
    for g, members in enumerate(groups):
        start += _numbered(_fan_out(members, [6 + m for m in members], SAME_CORE + SIBLING + MYSELF, 0), g)
    sems, bufs, _ = _comm_call("gather_start", sources + lands,
                               start=(start, [sum(cp.group == g for cp in start) for g in range(len(groups))]))
    sources, lands = bufs[:6], bufs[6:]

    def group_buffers(g):
        return [sources[m] for m in groups[g]] + [lands[m] for m in groups[g]]

    def relayed(tag, g, after):
        n = len(groups[g])
        relay = second_hop(n, n)
        relay_sems, bufs, _ = _comm_call("gather_relay_" + tag, group_buffers(g), wait=(sems[g], first_hop(n)),
                                         start=(relay, (len(relay),)), after=after)
        return relay_sems[0], bufs[n:]

    relay_sem, bufs = relayed("in", 0, None)
    g_in, g_conv = _comm_call("gather_wait_in", bufs, wait=(relay_sem, second_hop(2)))[1]
    w_in_f = g_in.reshape(IN_WIDTH, D_MODEL)
    conv_w_f = _cols_from_stack(g_conv)

    wa_bd = _block_diag(w_a[0])
    wi_bd = _block_diag(w_i[0])
    lnf_row = lnf_g.reshape(1, D_MODEL)

    u_pool, u_lru, u_gate = _fwd_in(xs, ln1_g, w_in_f)
    y_pool, h, y_lru = _mixer_fwd(u_pool, u_lru, u_gate, pool_w[0], pool_scale, conv_w_f, conv_b,
                                  wa_bd, b_a, wi_bd, b_i, lam)
    relay_sem, bufs = relayed("out", 1, y_pool)
    gate_up = second_hop(2, 3)
    (gate_up_sem,), bufs, _ = _comm_call("gather_wait_out", bufs + group_buffers(2),
                                         wait=[(relay_sem, second_hop(1)), (sems[2], first_hop(2, 1))],
                                         start=(gate_up, (len(gate_up),)))
    g_out, gate_up_lands = bufs[0], bufs[3:]
    w_out_f = g_out.reshape(D_MODEL, D_MODEL)
    h1, n2 = _fwd_out(xs, y_pool, y_lru, gn_pool_g, gn_lru_g, w_out_f, ln2_g)
    relay_sem, bufs = relayed("down", 3, n2)
    g_gate, g_up, g_down = _comm_call("gather_wait_ffn", gate_up_lands + bufs,
                                      wait=[(gate_up_sem, second_hop(2)), (relay_sem, second_hop(1, 2))])[1]
    w_gate_f = g_gate.reshape(D_FF, D_MODEL)
    w_up_f = g_up.reshape(D_FF, D_MODEL)
    w_down_f = g_down.reshape(D_FF, D_MODEL)
    g_act, u_act, dh2, dh2b, d_lnf, sq = _ffn_fwd(h1, n2, target, lnf_row, w_gate_f, w_up_f, w_down_f)

    def direct(tag, stacks, wholes, after):
        sources = list(stacks) + list(wholes)
        n, n_st = len(sources), len(stacks)
        lands = [lax.empty(a.shape if i < n_st else (N_DEV,) + a.shape, a.dtype) for i, a in enumerate(sources)]
        copies = _numbered(_scatter(range(n_st), range(n, n + n_st), EVERYONE + MYSELF)
                           + _fan_out(range(n_st, n), range(n + n_st, 2 * n), EVERYONE + MYSELF, 0))
        sem, bufs, token = _comm_call(tag + "_start", sources + lands, start=(copies, (len(copies),)), after=after)
        return (tag, sem[0], bufs, copies), token

    def direct_finish(started, after):
        tag, sem, bufs, copies = started
        _, bufs, _ = _comm_call(tag + "_wait", bufs, wait=(sem, copies), after=after)
        return bufs[len(bufs) // 2:]

    place = jnp.stack([lax.axis_index("c"), 2 * lax.axis_index("x") + lax.axis_index("y")]).astype(jnp.int32)

    d_gate, d_up, d_down, dn2 = _ffn_bwd(n2, dh2b, g_act, u_act, w_gate_f, w_up_f, w_down_f)
    down_stack, *ffn_stacks = [d.reshape(N_DEV, D_FF // N_DEV, D_MODEL) for d in (d_down, d_gate, d_up)]
    pair_lands = [lax.empty((N_DEV // 2,) + a.shape[1:], a.dtype) for a in ffn_stacks]
    pair_copies = _numbered(_to_sibling(range(2), range(2, 4)))
    down_copies = _numbered(_scatter([4], [5], EVERYONE + MYSELF), 1)
    (sem, down_sem), bufs, token = _comm_call(
        "ffn_grads_start", ffn_stacks + pair_lands + [down_stack, lax.empty(down_stack.shape, down_stack.dtype)],
        start=(pair_copies + down_copies, (len(pair_copies), len(down_copies))))
    bufs, down_bufs = bufs[:4], bufs[4:]
    dh1, dy_pool, dy_lru, d_out, d_ln2, d_gnp, d_gnl = _bwd_out(dn2, dh2, h1, y_pool, y_lru, gn_pool_g, gn_lru_g,
                                                                 w_out_f, ln2_g, token)
    _, bufs, _ = _comm_call("ffn_pair_wait", bufs, wait=(sem, pair_copies), after=dh1)
    ffn_sums, ffn_lands = _pair_sum(bufs[:2], bufs[2:], place)
    ffn_copies = _numbered(_to_chips(range(2), range(2, 4)))
    ffn_sem, ffn_bufs, token = _comm_call("ffn_chip_start", ffn_sums + ffn_lands, start=(ffn_copies, (len(ffn_copies),)))
    (du_pool, du_lru, du_gate, d_pw, d_ps, d_cw, d_cb, d_wa, d_ba, d_wi, d_bi, d_lam) = _mixer_bwd(
        u_pool, u_lru, u_gate, h, dy_pool, dy_lru, pool_w[0], pool_scale, conv_w_f, conv_b,
        wa_bd, b_a, wi_bd, b_i, lam, token)

    vec_grads = dict(ln2_g=d_ln2, lnf_g=d_lnf, pool_scale=d_ps, conv_b=d_cb, b_a=d_ba, b_i=d_bi,
                     lam=d_lam, gn_pool_g=d_gnp, gn_lru_g=d_gnl)
    packed, packed_b = _pack_small([vec_grads[k] for k, _ in VECTORS], d_pw, d_wa, d_wi, d_cw, sq)
    small_started, token = direct("small", [d_out.reshape(N_DEV, D_MODEL // N_DEV, D_MODEL)], [packed, packed_b], None)
    grad_x, d_in, d_ln1 = _bwd_in(xs, dh1, du_pool, du_lru, du_gate, ln1_g, w_in_f, token)
    in_started, token = direct("in", [d_in.reshape(N_DEV, IN_WIDTH // N_DEV, D_MODEL)], [d_ln1], None)

    results = {}

    def reduce_adam(name, parts):
        outs = _reduce_adam(parts, shard[name](weights[name]), shard[name](mom1[name]), shard[name](mom2[name]),
                            "adam_" + name)
        results[name] = tuple(unshard[name](o) for o in outs)
        return outs[0]

    shard["ln1_g"] = unshard["ln1_g"] = lambda a: a
    _, ffn_bufs, _ = _comm_call("ffn_chip_wait", ffn_bufs, wait=(ffn_sem[0], ffn_copies), after=token)
    done = [reduce_adam(name, parts) for name, parts in zip(("w_ffn_gate", "w_ffn_up"), ffn_bufs[2:])]
    _, (_, r_down), _ = _comm_call("down_wait", down_bufs, after=done,
                                   wait=(down_sem, _numbered(_scatter([0], [1], EVERYONE + MYSELF))))
    done.append(reduce_adam("w_ffn_down", r_down))
    r_out, r_small, r_small_b = direct_finish(small_started, done)
    done.append(reduce_adam("w_out", r_out))

    def as_row(a, width):
        return a.reshape(1, width)

    def as_matrix(a):
        return a.reshape(MATRIX_ROWS, SLAB)

    def as_heads(a):
        return a.reshape(2 * GATE_ROWS, HEAD)

    states = (weights, mom1, mom2)
    vec_out, pool_out, wa_out, wi_out, conv_out, loss_11 = _small_reduce_adam(
        my_index.astype(jnp.int32).reshape(1, 1), r_small, r_small_b,
        *[[as_row(t[k], w) for k, w in VECTORS] for t in states], [as_matrix(t["pool_w"]) for t in states],
        [as_heads(t["w_a"]) for t in states], [as_heads(t["w_i"]) for t in states], [t["conv_w"][0] for t in states])
    for (k, _), outs in zip(VECTORS, vec_out):
        results[k] = tuple(o.reshape(weights[k].shape) for o in outs)
    for k, outs in (("pool_w", pool_out), ("w_a", wa_out), ("w_i", wi_out), ("conv_w", conv_out)):
        results[k] = tuple(o.reshape(weights[k].shape) for o in outs)
    loss = loss_11[0, 0]
    r_in, r_ln1 = direct_finish(in_started, done + [loss_11])
    reduce_adam("w_in", r_in)
    reduce_adam("ln1_g", r_ln1)

    order = ["ln1_g", "w_in", "pool_w", "pool_scale", "conv_w", "conv_b", "w_a", "b_a", "w_i", "b_i", "lam",
             "gn_pool_g", "gn_lru_g", "w_out", "ln2_g", "w_ffn_gate", "w_ffn_up", "w_ffn_down", "lnf_g"]
    return (loss, grad_x[None],
            *[results[k][0] for k in order], *[results[k][1] for k in order],
            *[results[k][2] for k in order], *[results[k][3] for k in order])
```

```python
from typing import Any, NamedTuple

import jax
import jax.numpy as jnp
from jax import lax
from jax.experimental import pallas as pl
from jax.experimental.pallas import tpu as pltpu

F32 = jnp.float32
BF16 = jnp.bfloat16

N_DEV = 8
D_MODEL = 1024
POOL_WIDTH = 512
LRU_WIDTH = 512
IN_WIDTH = 1536
D_FF = 2816
N_SLAB = 4
SLAB = 128
CONV_WIDTH = 4
LRU_C = 8.0
EPS = 1e-6
HALO = 16
FF_CHUNK = 256

ADAM_LR = 0.001
ADAM_B1 = 0.9
ADAM_B2 = 0.999
ADAM_EPS = 1e-08
ADAM_WD = 0.01
ADAM_STEP = 10

MIB = 1 << 20
MESH = pl.DeviceIdType.MESH


def _params(vmem_mib, n_axes=0):
    sem = ("arbitrary",) * n_axes if n_axes else None
    return pltpu.CompilerParams(dimension_semantics=sem, vmem_limit_bytes=vmem_mib * MIB)


def _in_hbm(*arrays):
    return [pltpu.with_memory_space_constraint(a, pltpu.HBM) for a in arrays]


def _mm(a, b):
    return jnp.dot(a, b, preferred_element_type=F32)


def _mm_nt(a, b):
    return lax.dot_general(a, b, (((1,), (1,)), ((), ())), preferred_element_type=F32)


def _mm_tn(a, b):
    return lax.dot_general(a, b, (((0,), (0,)), ((), ())), preferred_element_type=F32)


def _rms(x, g):
    rstd = lax.rsqrt(jnp.mean(x * x, axis=-1, keepdims=True) + EPS)
    xhat = x * rstd
    return xhat * g, xhat, rstd


def _rms_bwd(dy, xhat, rstd, g):
    gy = dy * g
    dx = rstd * (gy - xhat * jnp.mean(gy * xhat, axis=-1, keepdims=True))
    return dx, jnp.sum(dy * xhat, axis=0, keepdims=True)


def _gelu(z):
    t = jnp.tanh(0.7978845608028654 * (z + 0.044715 * z * z * z))
    return 0.5 * z * (1.0 + t), t


def _gelu_grad(z, t):
    return 0.5 * (1.0 + t) + 0.5 * z * (1.0 - t * t) * 0.7978845608028654 * (1.0 + 3.0 * 0.044715 * z * z)


def _softplus_neg(lam):
    x = -lam
    e = jnp.exp(-jnp.abs(x))
    u = 1.0 + e
    l1p = jnp.where(u == 1.0, e, jnp.log(u) * e / (u - 1.0))
    return jnp.maximum(x, 0.0) + l1p


def _one_minus_square(la, a):
    x = 2.0 * la
    series = -x * (1.0 + x * (0.5 + x * (1.0 / 6.0 + x * (1.0 / 24.0))))
    return jnp.where(x > -0.06, series, 1.0 - a * a)


def _sigmoid(x):
    return 0.5 * jnp.tanh(0.5 * x) + 0.5


def _down(v, d):
    return pltpu.roll(v, d, 0)


def _up(v, d):
    return pltpu.roll(v, v.shape[0] - d, 0)


def _slab_major(s, dtype=F32):
    return pltpu.HBM((N_SLAB, s, SLAB), dtype)


def _slabs_spec(tb):
    return pl.BlockSpec((N_SLAB, tb, SLAB), lambda i: (0, i, 0))


def _read_slabs(ref):
    return jnp.concatenate([ref[k] for k in range(N_SLAB)], axis=1)


def _write_slabs(ref, value):
    for k in range(N_SLAB):
        ref[k] = value[:, k * SLAB:(k + 1) * SLAB]


def _token_block(s, most=512):
    for rows in (most, 512, 256):
        if rows <= most and s % rows == 0 and s > rows:
            return rows
    return s


def _time_chunk(s):
    return 512 if s % 512 == 0 and s > 512 else 256 if s % 256 == 0 else s


def _fwd_in(x, ln1_g, w_in_t):
    s = x.shape[0]
    tb = _token_block(s, 1024)

    def body(x_ref, g_ref, w_ref, up_ref, ul_ref, ug_ref):
        n, _, _ = _rms(x_ref[...], g_ref[...])
        proj = _mm_nt(n.astype(BF16), w_ref[...])
        _write_slabs(up_ref, proj[:, :POOL_WIDTH])
        _write_slabs(ul_ref, proj[:, POOL_WIDTH:POOL_WIDTH + LRU_WIDTH])
        _write_slabs(ug_ref, proj[:, POOL_WIDTH + LRU_WIDTH:])

    out = _slab_major(s)
    return pl.pallas_call(
        body, name="fwd_in", grid=(s // tb,),
        in_specs=[pl.BlockSpec((tb, D_MODEL), lambda i: (i, 0)),
                  pl.BlockSpec((1, D_MODEL), lambda i: (0, 0)),
                  pl.BlockSpec((IN_WIDTH, D_MODEL), lambda i: (0, 0))],
        out_specs=[_slabs_spec(tb)] * 3,
        out_shape=[out, out, out],
        compiler_params=_params(40, 1),
    )(*_in_hbm(x, ln1_g, w_in_t))


def _pool_denominator(t0, row, window):
    return jnp.minimum((t0 + row + 1).astype(F32), window)


def _causal_window(ext, deeper):
    s = ext + _down(ext, 1)
    s = s + deeper[0] * _down(s, 2)
    s = s + deeper[1] * _down(s, 4)
    s = s + deeper[2] * _down(s, 8)
    return s[HALO:]


def _anticausal_window(ext, deeper, rows):
    s = ext + _up(ext, 1)
    s = s + deeper[0] * _up(s, 2)
    s = s + deeper[1] * _up(s, 4)
    s = s + deeper[2] * _up(s, 8)
    return s[:rows]


def _conv(taps, cw, cb):
    return cw[3] * taps[0] + cw[2] * taps[1] + cw[1] * taps[2] + cw[0] * taps[3] + cb


def _lru_gates(xc, wa, ba, wi, bi):
    xb = xc.astype(BF16)
    r = _sigmoid(_mm(xb, wa) + ba)
    i = _sigmoid(_mm(xb, wi) + bi)
    return xb, r, i


def _lru_decay(r, sp):
    la = (-LRU_C) * r * sp
    a = jnp.exp(la)
    return a, jnp.sqrt(jnp.maximum(_one_minus_square(la, a), 0.0))


SUBLANES = 8


def _scan_pads(a_pad, b_pad, rows, causal):
    spare = slice(0, SUBLANES) if causal else slice(rows, rows + SUBLANES)
    a_pad[spare, :] = jnp.ones((SUBLANES, SLAB), F32)
    b_pad[spare, :] = jnp.zeros((SUBLANES, SLAB), F32)


def _scan_causal(a, b, h_prev, a_pad, b_pad, rows):
    d = 1
    while d < min(SUBLANES, rows):
        a_pad[SUBLANES:, :] = a
        b_pad[SUBLANES:, :] = b
        b = a * b_pad[SUBLANES - d:SUBLANES - d + rows, :] + b
        a = a * a_pad[SUBLANES - d:SUBLANES - d + rows, :]
        d *= 2
    h = jnp.broadcast_to(h_prev, (d, SLAB))
    out = []
    for at in range(0, rows, d):
        h = a[at:at + d] * h + b[at:at + d]
        out.append(h)
    return jnp.concatenate(out, axis=0)


def _scan_anticausal(a, b, l_next, a_pad, b_pad, rows):
    d = 1
    while d < min(SUBLANES, rows):
        a_pad[:rows, :] = a
        b_pad[:rows, :] = b
        b = a * b_pad[d:d + rows, :] + b
        a = a * a_pad[d:d + rows, :]
        d *= 2
    lead = jnp.broadcast_to(l_next, (d, SLAB))
    out = []
    for at in range(rows - d, -1, -d):
        lead = a[at:at + d] * lead + b[at:at + d]
        out.append(lead)
    return jnp.concatenate(out[::-1], axis=0)


def _slab_scalars():
    slab = pl.program_id(0)
    deeper = [jnp.where(slab > k, 1.0, 0.0).astype(F32) for k in range(N_SLAB - 1)]
    window = jnp.left_shift(jnp.int32(2), slab).astype(F32)
    inverse = jnp.where(slab == 0, 0.5, jnp.where(slab == 1, 0.25, jnp.where(slab == 2, 0.125, 0.0625))).astype(F32)
    return deeper, (window, inverse)


def _window_mean(total, t0, row, window, at_start):
    if at_start:
        return total / _pool_denominator(t0, row, window[0])
    return total * window[1]


def _slab_specs(s):
    seq = pl.BlockSpec((None, s, SLAB), lambda k: (k, 0, 0))
    mat = pl.BlockSpec((1, SLAB, SLAB), lambda k: (k, 0, 0))
    vec = pl.BlockSpec((1, SLAB), lambda k: (0, k))
    taps = pl.BlockSpec((CONV_WIDTH, SLAB), lambda k: (0, k))
    return seq, mat, vec, taps


def _mixer_fwd(u_pool, u_lru, u_gate, pool_w, pool_scale, conv_w, conv_b, wa_bd, b_a, wi_bd, b_i, lam):
    s = u_pool.shape[1]
    tc = _time_chunk(s)
    n_chunks = s // tc

    def body(up_ref, ul_ref, ug_ref, pw_ref, ps_ref, cw_ref, cb_ref, wa_ref, ba_ref, wi_ref, bi_ref, lam_ref,
             yp_ref, h_ref, yl_ref, a_ref, mult_ref, shift_ref, a_pad, b_pad):
        _scan_pads(a_pad, b_pad, tc, causal=True)
        deeper, window = _slab_scalars()
        pw = pw_ref[0].astype(BF16)
        wa = wa_ref[0].astype(BF16)
        wi = wi_ref[0].astype(BF16)
        ps, cb, ba, bi = ps_ref[...], cb_ref[...], ba_ref[...], bi_ref[...]
        cw = [cw_ref[k:k + 1, :] for k in range(CONV_WIDTH)]
        sp = _softplus_neg(lam_ref[...])
        row = lax.broadcasted_iota(jnp.int32, (tc, SLAB), 0)

        def chunk(t0, ext_p, ext_l, h_prev, at_start=False):
            rows = pl.ds(t0, tc)
            d = _window_mean(_causal_window(ext_p, deeper), t0, row, window, at_start) - ext_p[HALO:]
            yp_ref[rows, :] = _mm(d.astype(BF16), pw) * ps
            shift_ref[...] = ext_l
            xc = _conv([shift_ref[HALO - k:HALO - k + tc, :] for k in range(CONV_WIDTH)], cw, cb)
            _, r, i = _lru_gates(xc, wa, ba, wi, bi)
            a, mult = _lru_decay(r, sp)
            a_ref[rows, :] = a
            mult_ref[rows, :] = mult
            h = _scan_causal(a, mult * (i * xc), h_prev, a_pad, b_pad, tc)
            h_ref[rows, :] = h
            yl_ref[rows, :] = h * _gelu(ug_ref[rows, :])[0]
            return h[tc - 1:tc, :]

        pad = jnp.zeros((HALO, SLAB), F32)
        h0 = chunk(0, jnp.concatenate([pad, up_ref[pl.ds(0, tc), :]], axis=0),
                   jnp.concatenate([pad, ul_ref[pl.ds(0, tc), :]], axis=0), jnp.zeros((1, SLAB), F32), at_start=True)

        def step(c, h_prev):
            t0 = pl.multiple_of(c * tc, tc)
            ext = pl.ds(pl.multiple_of(c * tc - HALO, HALO), tc + HALO)
            return chunk(t0, up_ref[ext, :], ul_ref[ext, :], h_prev)

        lax.fori_loop(1, n_chunks, step, h0)

    seq, mat, vec, taps = _slab_specs(s)
    out = _slab_major(s)
    return pl.pallas_call(
        body, name="mixer_fwd", grid=(N_SLAB,),
        in_specs=[seq, seq, seq, mat, vec, taps, vec, mat, vec, mat, vec, vec],
        out_specs=[seq] * 5, out_shape=[out] * 5,
        scratch_shapes=[pltpu.VMEM((tc + HALO, SLAB), F32), pltpu.VMEM((tc + SUBLANES, SLAB), F32),
                        pltpu.VMEM((tc + SUBLANES, SLAB), F32)],
        compiler_params=_params(48, 1),
    )(*_in_hbm(u_pool, u_lru, u_gate, pool_w, pool_scale, conv_w, conv_b, wa_bd, b_a, wi_bd, b_i, lam))


def _fwd_out(x, y_pool, y_lru, gn_pool_g, gn_lru_g, w_out, ln2_g):
    s = x.shape[0]
    tb = _token_block(s, 1024)

    def body(x_ref, yp_ref, yl_ref, gp_ref, gl_ref, w_ref, g2_ref, h1_ref, n2_ref):
        mp, _, _ = _rms(_read_slabs(yp_ref), gp_ref[...])
        ml, _, _ = _rms(_read_slabs(yl_ref), gl_ref[...])
        h1 = x_ref[...] + _mm(mp.astype(BF16), w_ref[:POOL_WIDTH, :]) + _mm(ml.astype(BF16), w_ref[POOL_WIDTH:, :])
        h1_ref[...] = h1
        n2_ref[...] = _rms(h1, g2_ref[...])[0].astype(BF16)

    row = pl.BlockSpec((tb, D_MODEL), lambda i: (i, 0))
    half = _slabs_spec(tb)
    return pl.pallas_call(
        body, name="fwd_out", grid=(s // tb,),
        in_specs=[row, half, half, pl.BlockSpec((1, 512), lambda i: (0, 0)), pl.BlockSpec((1, 512), lambda i: (0, 0)),
                  pl.BlockSpec((D_MODEL, D_MODEL), lambda i: (0, 0)), pl.BlockSpec((1, D_MODEL), lambda i: (0, 0))],
        out_specs=[row, row],
        out_shape=[pltpu.HBM((s, D_MODEL), F32), pltpu.HBM((s, D_MODEL), BF16)],
        compiler_params=_params(40, 1),
    )(*_in_hbm(x, y_pool, y_lru, gn_pool_g, gn_lru_g, w_out, ln2_g))


def _ffn_fwd(h1, n2, target, lnf_g, w_gate, w_up, w_down):
    s = h1.shape[0]
    tb = 512
    sub = 256
    n_ff = D_FF // FF_CHUNK

    def body(h1_ref, n2_ref, t_ref, gf_ref, wg_hbm, wu_hbm, wd_hbm,
             g_ref, u_ref, dh_ref, dhb_ref, dgf_ref, sq_ref, wg, wu, wd, sem):
        @pl.when(pl.program_id(0) == 0)
        def _():
            loads = [pltpu.make_async_copy(src, dst, sem.at[k])
                     for k, (src, dst) in enumerate(((wg_hbm, wg), (wu_hbm, wu), (wd_hbm, wd)))]
            for cp in loads:
                cp.start()
            for cp in loads:
                cp.wait()
            dgf_ref[...] = jnp.zeros_like(dgf_ref)
            sq_ref[...] = jnp.zeros_like(sq_ref)

        n2v = n2_ref[...]
        acc = jnp.zeros((tb, D_MODEL), F32)
        for c in range(n_ff):
            cols = slice(c * FF_CHUNK, (c + 1) * FF_CHUNK)
            g = _mm_nt(n2v, wg[cols, :])
            u = _mm_nt(n2v, wu[cols, :])
            g_ref[:, cols] = g.astype(BF16)
            u_ref[:, cols] = u.astype(BF16)
            act = g * jax.nn.sigmoid(g) * u
            acc = acc + _mm(act.astype(BF16), wd[cols, :])
        gf = gf_ref[...]
        for r in range(tb // sub):
            rows = slice(r * sub, (r + 1) * sub)
            y, xhat, rstd = _rms(h1_ref[rows, :] + acc[rows, :], gf)
            err = y - t_ref[rows, :]
            sq_ref[...] += jnp.sum(err * err, axis=0, keepdims=True)
            dh2, dgf = _rms_bwd(err * (1.0 / D_MODEL), xhat, rstd, gf)
            dgf_ref[...] += dgf
            dh_ref[rows, :] = dh2
            dhb_ref[rows, :] = dh2.astype(BF16)

    row = pl.BlockSpec((tb, D_MODEL), lambda i: (i, 0))
    ff = pl.BlockSpec((tb, D_FF), lambda i: (i, 0))
    vec = pl.BlockSpec((1, D_MODEL), lambda i: (0, 0))
    anyspace = pl.BlockSpec(memory_space=pl.ANY)
    return pl.pallas_call(
        body, name="ffn_fwd", grid=(s // tb,),
        in_specs=[row, row, row, vec, anyspace, anyspace, anyspace],
        out_specs=[ff, ff, row, row, vec, vec],
        out_shape=[pltpu.HBM((s, D_FF), BF16), pltpu.HBM((s, D_FF), BF16),
                   pltpu.HBM((s, D_MODEL), F32), pltpu.HBM((s, D_MODEL), BF16),
                   jax.ShapeDtypeStruct((1, D_MODEL), F32), jax.ShapeDtypeStruct((1, D_MODEL), F32)],
        scratch_shapes=[pltpu.VMEM((D_FF, D_MODEL), BF16), pltpu.VMEM((D_FF, D_MODEL), BF16),
                        pltpu.VMEM((D_FF, D_MODEL), BF16), pltpu.SemaphoreType.DMA((3,))],
        compiler_params=_params(60, 1),
    )(*_in_hbm(h1, n2, target, lnf_g, w_gate, w_up, w_down))


def _ffn_bwd(n2, dh2b, g, u, w_gate_t, w_up_t, w_down):
    s = n2.shape[0]
    tb = min(1024, s)
    n_ff = D_FF // FF_CHUNK
    n_tb = s // tb

    def body(n2_ref, dh_ref, g_ref, u_ref, wg_ref, wu_ref, wd_ref, dwg_ref, dwu_ref, dwd_ref, dn2_ref,
             dn2_acc, acc_g, acc_u, acc_d):
        j = pl.program_id(0)
        t = pl.program_id(1)
        rows = pl.ds(pl.multiple_of(t * tb, tb), tb)

        @pl.when(t == 0)
        def _():
            acc_g[...] = jnp.zeros_like(acc_g)
            acc_u[...] = jnp.zeros_like(acc_u)
            acc_d[...] = jnp.zeros_like(acc_d)

        @pl.when(j == 0)
        def _():
            dn2_acc[rows, :] = jnp.zeros((tb, D_MODEL), F32)

        n2v = n2_ref[...]
        dh = dh_ref[...]
        gv = g_ref[...].astype(F32)
        uv = u_ref[...].astype(F32)
        sg = jax.nn.sigmoid(gv)
        silu = gv * sg
        dact = _mm_nt(dh, wd_ref[...])
        dub = (dact * silu).astype(BF16)
        dgb = (dact * uv * (sg * (1.0 + gv * (1.0 - sg)))).astype(BF16)
        acc_d[...] += _mm_tn((silu * uv).astype(BF16), dh)
        acc_g[...] += _mm_tn(dgb, n2v)
        acc_u[...] += _mm_tn(dub, n2v)
        dn2_acc[rows, :] += _mm(jnp.concatenate([dgb, dub], axis=1),
                                jnp.concatenate([wg_ref[...], wu_ref[...]], axis=0))

        @pl.when(t == n_tb - 1)
        def _():
            dwg_ref[...] = acc_g[...].astype(BF16)
            dwu_ref[...] = acc_u[...].astype(BF16)
            dwd_ref[...] = acc_d[...].astype(BF16)

        @pl.when(j == n_ff - 1)
        def _():
            dn2_ref[...] = dn2_acc[rows, :]

    row = pl.BlockSpec((tb, D_MODEL), lambda j, t: (t, 0))
    act = pl.BlockSpec((tb, FF_CHUNK), lambda j, t: (t, j))
    w_row = pl.BlockSpec((FF_CHUNK, D_MODEL), lambda j, t: (j, 0))
    last = pl.BlockSpec((tb, D_MODEL), lambda j, t: (jnp.where(j == n_ff - 1, t, 0), 0))
    grad = pltpu.HBM((D_FF, D_MODEL), BF16)
    chunk_acc = pltpu.VMEM((FF_CHUNK, D_MODEL), F32)
    return pl.pallas_call(
        body, name="ffn_bwd", grid=(n_ff, n_tb),
        in_specs=[row, row, act, act, w_row, w_row, w_row],
        out_specs=[w_row, w_row, w_row, last],
        out_shape=[grad, grad, grad, pltpu.HBM((s, D_MODEL), F32)],
        scratch_shapes=[pltpu.VMEM((s, D_MODEL), F32), chunk_acc, chunk_acc, chunk_acc],
        compiler_params=_params(56, 2),
    )(*_in_hbm(n2, dh2b, g, u, w_gate_t, w_up_t, w_down))


def _bwd_out(dn2, dh2, h1, y_pool, y_lru, gn_pool_g, gn_lru_g, w_out, ln2_g, after):
    s = h1.shape[0]
    tb = _token_block(s)

    def body(dn2_ref, dh2_ref, h1_ref, yp_ref, yl_ref, gp_ref, gl_ref, w_ref, g2_ref, _after,
             dh1_ref, dyp_ref, dyl_ref, dwb_ref, dg2_ref, dgp_ref, dgl_ref, dw_ref):
        @pl.when(pl.program_id(0) == 0)
        def _():
            dw_ref[...] = jnp.zeros_like(dw_ref)
            dg2_ref[...] = jnp.zeros_like(dg2_ref)
            dgp_ref[...] = jnp.zeros_like(dgp_ref)
            dgl_ref[...] = jnp.zeros_like(dgl_ref)

        g2 = g2_ref[...]
        _, xhat2, rstd2 = _rms(h1_ref[...], g2)
        dres, dg2 = _rms_bwd(dn2_ref[...], xhat2, rstd2, g2)
        dg2_ref[...] += dg2
        dh1 = dh2_ref[...] + dres
        dh1_ref[...] = dh1
        dh1b = dh1.astype(BF16)
        gp, gl = gp_ref[...], gl_ref[...]
        mp, xhat_p, rstd_p = _rms(_read_slabs(yp_ref), gp)
        ml, xhat_l, rstd_l = _rms(_read_slabs(yl_ref), gl)
        dw_ref[:POOL_WIDTH, :] += _mm_tn(mp.astype(BF16), dh1b)
        dw_ref[POOL_WIDTH:, :] += _mm_tn(ml.astype(BF16), dh1b)
        dyp, dgp = _rms_bwd(_mm_nt(dh1b, w_ref[:POOL_WIDTH, :]), xhat_p, rstd_p, gp)
        dyl, dgl = _rms_bwd(_mm_nt(dh1b, w_ref[POOL_WIDTH:, :]), xhat_l, rstd_l, gl)
        _write_slabs(dyp_ref, dyp)
        _write_slabs(dyl_ref, dyl)
        dgp_ref[...] += dgp
        dgl_ref[...] += dgl

        @pl.when(pl.program_id(0) == s // tb - 1)
        def _():
            dwb_ref[...] = dw_ref[...].astype(BF16)

    row = pl.BlockSpec((tb, D_MODEL), lambda i: (i, 0))
    half = _slabs_spec(tb)
    vec = pl.BlockSpec((1, D_MODEL), lambda i: (0, 0))
    hvec = pl.BlockSpec((1, 512), lambda i: (0, 0))
    mat = pl.BlockSpec((D_MODEL, D_MODEL), lambda i: (0, 0))
    return pl.pallas_call(
        body, name="bwd_out", grid=(s // tb,),
        in_specs=[row, row, row, half, half, hvec, hvec, mat, vec, pl.BlockSpec(memory_space=pl.ANY)],
        out_specs=[row, half, half, mat, vec, hvec, hvec],
        out_shape=[pltpu.HBM((s, D_MODEL), F32), _slab_major(s),
                   _slab_major(s), pltpu.HBM((D_MODEL, D_MODEL), BF16),
                   jax.ShapeDtypeStruct((1, D_MODEL), F32), jax.ShapeDtypeStruct((1, 512), F32),
                   jax.ShapeDtypeStruct((1, 512), F32)],
        scratch_shapes=[pltpu.VMEM((D_MODEL, D_MODEL), F32)],
        compiler_params=_params(48, 1),
    )(*_in_hbm(dn2, dh2, h1, y_pool, y_lru, gn_pool_g, gn_lru_g, w_out, ln2_g), after)


def _mixer_bwd(u_pool, u_lru, u_gate, h, decay, root, dy_pool, dy_lru,
               pool_w, pool_scale, conv_w, conv_b, wa_bd, b_a, wi_bd, b_i, lam, after):
    s = u_pool.shape[1]
    tc = _time_chunk(s)
    n_chunks = s // tc

    def body(up_ref, ul_ref, ug_ref, h_ref, a_ref, mult_ref, dyp_ref, dyl_ref,
             pw_ref, ps_ref, cw_ref, cb_ref, wa_ref, ba_ref, wi_ref, bi_ref, lam_ref, _after,
             dup_ref, dul_ref, dug_ref, dpw_ref, dps_ref, dcw_ref, dcb_ref, dwa_ref, dba_ref, dwi_ref, dbi_ref, dlam_ref,
             a_pad, b_pad, u_shift, h_shift, a_shift, d_shift):
        _scan_pads(a_pad, b_pad, tc, causal=False)
        deeper, window = _slab_scalars()
        pw = pw_ref[0].astype(BF16)
        wa = wa_ref[0].astype(BF16)
        wi = wi_ref[0].astype(BF16)
        ps, cb, ba, bi = ps_ref[...], cb_ref[...], ba_ref[...], bi_ref[...]
        cw = [cw_ref[k:k + 1, :] for k in range(CONV_WIDTH)]
        lam_v = lam_ref[...]
        sp = _softplus_neg(lam_v)
        row = lax.broadcasted_iota(jnp.int32, (tc, SLAB), 0)
        for ref in (dpw_ref, dps_ref, dcw_ref, dcb_ref, dwa_ref, dba_ref, dwi_ref, dbi_ref, dlam_ref):
            ref[...] = jnp.zeros_like(ref)

        def chunk(t0, ext_p, ext_l, ext_h, carry, at_start=False):
            l_next, a_next, dxc_next, ddn_next = carry
            rows = pl.ds(t0, tc)
            u_shift[...] = ext_l
            taps = [u_shift[HALO - k:HALO - k + tc, :] for k in range(CONV_WIDTH)]
            xc = _conv(taps, cw, cb)
            xb, r, i = _lru_gates(xc, wa, ba, wi, bi)
            a, mult = a_ref[rows, :], mult_ref[rows, :]
            hv = ext_h[HALO:]
            h_shift[...] = ext_h
            h_before = h_shift[HALO - 1:HALO - 1 + tc, :]
            ug = ug_ref[rows, :]
            dyl = dyl_ref[rows, :]
            gel, th = _gelu(ug)
            dug_ref[rows, :] = (dyl * hv * _gelu_grad(ug, th)).astype(BF16)
            a_shift[:tc, :] = a
            a_shift[tc:, :] = jnp.broadcast_to(a_next, (SUBLANES, SLAB))
            a_after = a_shift[1:1 + tc, :]
            l = _scan_anticausal(a_after, dyl * gel, l_next, a_pad, b_pad, tc)
            dmult = l * (i * xc)
            di = l * mult * xc
            dxc = l * mult * i
            dla = (l * h_before) * a - jnp.where(mult > 0.0, dmult * (a * a) / mult, 0.0)
            dlam_ref[...] += jnp.sum(dla * r, axis=0, keepdims=True)
            dpa = (dla * ((-LRU_C) * sp)) * (r * (1.0 - r))
            dpi = di * (i * (1.0 - i))
            dpab = dpa.astype(BF16)
            dpib = dpi.astype(BF16)
            dwa_ref[0] += _mm_tn(xb, dpab)
            dwi_ref[0] += _mm_tn(xb, dpib)
            dba_ref[...] += jnp.sum(dpa, axis=0, keepdims=True)
            dbi_ref[...] += jnp.sum(dpi, axis=0, keepdims=True)
            dxc = dxc + _mm_nt(dpab, wa) + _mm_nt(dpib, wi)
            d_shift[:tc, :] = dxc
            d_shift[tc:, :] = dxc_next
            dul_ref[rows, :] = (cw[3] * dxc + cw[2] * d_shift[1:1 + tc, :]
                                + cw[1] * d_shift[2:2 + tc, :] + cw[0] * d_shift[3:3 + tc, :]).astype(BF16)
            for k in range(CONV_WIDTH):
                dcw_ref[k:k + 1, :] += jnp.sum(dxc * taps[CONV_WIDTH - 1 - k], axis=0, keepdims=True)
            dcb_ref[...] += jnp.sum(dxc, axis=0, keepdims=True)
            db = (_window_mean(_causal_window(ext_p, deeper), t0, row, window, at_start) - ext_p[HALO:]).astype(BF16)
            dyp = dyp_ref[rows, :]
            dps_ref[...] += jnp.sum(dyp * _mm(db, pw), axis=0, keepdims=True)
            dys = (dyp * ps).astype(BF16)
            dpw_ref[0] += _mm_tn(db, dys)
            dd = _mm_nt(dys, pw)
            ddn = _window_mean(dd, t0, row, window, at_start)
            ext_q = jnp.concatenate([ddn, ddn_next], axis=0)
            dup_ref[rows, :] = (_anticausal_window(ext_q, deeper, tc) - dd).astype(BF16)
            return l[0:1, :], a[0:1, :], dxc[0:8, :], ddn[0:HALO, :]

        def step(k, carry):
            c = n_chunks - 1 - k
            t0 = pl.multiple_of(c * tc, tc)
            ext = pl.ds(pl.multiple_of(c * tc - HALO, HALO), tc + HALO)
            return chunk(t0, up_ref[ext, :], ul_ref[ext, :], h_ref[ext, :], carry)

        carry = (jnp.zeros((1, SLAB), F32), jnp.zeros((1, SLAB), F32),
                 jnp.zeros((8, SLAB), F32), jnp.zeros((HALO, SLAB), F32))
        carry = lax.fori_loop(0, n_chunks - 1, step, carry)
        pad = jnp.zeros((HALO, SLAB), F32)
        first = pl.ds(0, tc)
        chunk(0, jnp.concatenate([pad, up_ref[first, :]], axis=0), jnp.concatenate([pad, ul_ref[first, :]], axis=0),
              jnp.concatenate([pad, h_ref[first, :]], axis=0), carry, at_start=True)
        dlam_ref[...] = dlam_ref[...] * (LRU_C * jax.nn.sigmoid(-lam_v))

    seq, mat, vec, taps = _slab_specs(s)
    grad = _slab_major(s, BF16)
    mats = jax.ShapeDtypeStruct((N_SLAB, SLAB, SLAB), F32)
    vecs = jax.ShapeDtypeStruct((1, 512), F32)
    return pl.pallas_call(
        body, name="mixer_bwd", grid=(N_SLAB,),
        in_specs=[seq] * 8 + [mat, vec, taps, vec, mat, vec, mat, vec, vec, pl.BlockSpec(memory_space=pl.ANY)],
        out_specs=[seq, seq, seq, mat, vec, taps, vec, mat, vec, mat, vec, vec],
        out_shape=[grad, grad, grad, mats, vecs, jax.ShapeDtypeStruct((CONV_WIDTH, 512), F32), vecs,
                   mats, vecs, mats, vecs, vecs],
        scratch_shapes=[pltpu.VMEM((tc + SUBLANES, SLAB), F32), pltpu.VMEM((tc + SUBLANES, SLAB), F32),
                        pltpu.VMEM((tc + HALO, SLAB), F32), pltpu.VMEM((tc + HALO, SLAB), F32),
                        pltpu.VMEM((tc + SUBLANES, SLAB), F32), pltpu.VMEM((tc + SUBLANES, SLAB), F32)],
        compiler_params=_params(56, 1),
    )(*_in_hbm(u_pool, u_lru, u_gate, h, decay, root, dy_pool, dy_lru, pool_w, pool_scale, conv_w, conv_b, wa_bd,
               b_a, wi_bd, b_i, lam), after)


def _bwd_in(x, dh1, du_pool, du_lru, du_gate, ln1_g, w_in_t, after):
    s = x.shape[0]
    tb = _token_block(s)

    def body(x_ref, dh1_ref, dup_ref, dul_ref, dug_ref, g_ref, w_ref, _after, dx_ref, dwb_ref, dg_ref, dw_ref):
        @pl.when(pl.program_id(0) == 0)
        def _():
            dw_ref[...] = jnp.zeros_like(dw_ref)
            dg_ref[...] = jnp.zeros_like(dg_ref)

        g1 = g_ref[...]
        n, xhat, rstd = _rms(x_ref[...], g1)
        nb = n.astype(BF16)
        db = jnp.concatenate([_read_slabs(ref) for ref in (dup_ref, dul_ref, dug_ref)], axis=1)
        dw_ref[...] += _mm_tn(db, nb)
        dx, dg1 = _rms_bwd(_mm(db, w_ref[...]), xhat, rstd, g1)
        dx_ref[...] = dh1_ref[...] + dx
        dg_ref[...] += dg1

        @pl.when(pl.program_id(0) == s // tb - 1)
        def _():
            dwb_ref[...] = dw_ref[...].astype(BF16)

    row = pl.BlockSpec((tb, D_MODEL), lambda i: (i, 0))
    half = _slabs_spec(tb)
    vec = pl.BlockSpec((1, D_MODEL), lambda i: (0, 0))
    mat = pl.BlockSpec((IN_WIDTH, D_MODEL), lambda i: (0, 0))
    return pl.pallas_call(
        body, name="bwd_in", grid=(s // tb,),
        in_specs=[row, row, half, half, half, vec, mat, pl.BlockSpec(memory_space=pl.ANY)],
        out_specs=[row, mat, vec],
        out_shape=[pltpu.HBM((s, D_MODEL), F32), pltpu.HBM((IN_WIDTH, D_MODEL), BF16),
                   jax.ShapeDtypeStruct((1, D_MODEL), F32)],
        scratch_shapes=[pltpu.VMEM((IN_WIDTH, D_MODEL), F32)],
        compiler_params=_params(48, 1),
    )(*_in_hbm(x, dh1, du_pool, du_lru, du_gate, ln1_g, w_in_t), after)


def _mesh_position():
    x, y, c = lax.axis_index("x"), lax.axis_index("y"), lax.axis_index("c")
    return x, y, c, 4 * x + 2 * y + c


def _peer(x, y, c, p):
    px = 1 - x if p & 4 else x
    py = 1 - y if p & 2 else y
    pc = 1 - c if p & 1 else c
    return (px, py, pc), 4 * px + 2 * py + pc


HBM_SPEC = pl.BlockSpec(memory_space=pltpu.HBM)
SEM_SPEC = pl.BlockSpec(memory_space=pltpu.SEMAPHORE)
DATAFLOW = pltpu.SideEffectType.DATAFLOW_SIDE_EFFECTING


class Copy(NamedTuple):
    src: int
    src_at: Any
    dst: int
    dst_at: Any
    peer: int
    group: int
    slot: int


SIBLING = (1,)
SAME_CORE = (2, 4, 6)
EVERYONE = tuple(range(1, N_DEV))
MYSELF = (0,)


def _same(index):
    return index


def _chip(index):
    return jnp.right_shift(index, 1)


def _fan_out(srcs, lands, peers, group):
    return [Copy(s, None, d, _same, p, group, N_DEV * i + p) for i, (s, d) in enumerate(zip(srcs, lands)) for p in peers]


def _scatter(stacks, lands, peers):
    return [Copy(s, lambda me, p=p: jnp.bitwise_xor(me, p), d, _same, p, 0, 0)
            for s, d in zip(stacks, lands) for p in peers]


def _numbered(copies, group=0):
    return [cp._replace(group=group, slot=i) for i, cp in enumerate(copies)]


def _relay(lands, peers):
    return [Copy(b, lambda s, q=q: jnp.bitwise_xor(s, q), b, lambda s, q=q: jnp.bitwise_xor(s, q), 1, 0, N_DEV * i + q)
            for i, b in enumerate(lands) for q in peers]


def _to_sibling(stacks, lands):
    return [Copy(s, lambda me, k=k: 2 * k + 1 - jnp.bitwise_and(me, 1), d, lambda me, k=k: k, 1, 0, 4 * i + k)
            for i, (s, d) in enumerate(zip(stacks, lands)) for k in range(N_DEV // 2)]


def _to_chips(sums, lands):
    return [Copy(s, lambda me, p=p: jnp.bitwise_xor(_chip(me), p // 2), d, _chip, p, 0, 4 * i + p // 2)
            for i, (s, d) in enumerate(zip(sums, lands)) for p in SAME_CORE]


def _comm_call(name, bufs, wait=None, start=None, after=()):
    nb = len(bufs)
    waits = [wait] if isinstance(wait, tuple) else list(wait or [])
    slots = list(start[1]) if start else []
    n_out_sem = 2 * len(slots)
    after = [a for a in (after if isinstance(after, (list, tuple)) else [after]) if a is not None]

    def body(*refs):
        b = refs[:nb]
        at = nb + 2 * len(waits) + len(after)
        out_sems = refs[at:at + n_out_sem]
        token = refs[at + n_out_sem + nb]
        x, y, c, me = _mesh_position()

        def part(i, row_of, sender):
            return b[i] if row_of is None else b[i].at[row_of(sender)]

        for k, (_, copies) in enumerate(waits):
            w_send, w_recv = refs[nb + 2 * k], refs[nb + 2 * k + 1]
            for cp in copies:
                peer, peer_index = _peer(x, y, c, cp.peer)
                arrival = pltpu.make_async_remote_copy(part(cp.src, cp.src_at, me), part(cp.dst, cp.dst_at, peer_index),
                                                       w_send.at[cp.slot], w_recv.at[cp.slot],
                                                       device_id=peer, device_id_type=MESH)
                arrival.wait_send()
                arrival.wait_recv()
        if start:
            for cp in start[0]:
                peer, _ = _peer(x, y, c, cp.peer)
                pltpu.make_async_remote_copy(part(cp.src, cp.src_at, me), part(cp.dst, cp.dst_at, me),
                                             out_sems[2 * cp.group].at[cp.slot], out_sems[2 * cp.group + 1].at[cp.slot],
                                             device_id=peer, device_id_type=MESH).start()
        token[...] = jnp.zeros_like(token)

    sem_shapes = []
    for n_slots in slots:
        sem_shapes += [pltpu.SemaphoreType.DMA((n_slots,))] * 2
    operands = [pltpu.with_memory_space_constraint(a, pltpu.HBM) for a in bufs]
    in_specs = [HBM_SPEC] * nb
    for sems, _ in waits:
        operands += list(sems)
        in_specs += [SEM_SPEC, SEM_SPEC]
    operands += after
    in_specs += [pl.BlockSpec(memory_space=pl.ANY)] * len(after)
    outs = pl.pallas_call(
        body, name=name, in_specs=in_specs,
        out_specs=[SEM_SPEC] * n_out_sem + [HBM_SPEC] * nb + [pl.BlockSpec(memory_space=pltpu.VMEM)],
        out_shape=sem_shapes + [pltpu.HBM(a.shape, a.dtype) for a in bufs] + [jax.ShapeDtypeStruct((8, SLAB), F32)],
        input_output_aliases={i: n_out_sem + i for i in range(nb)},
        compiler_params=pltpu.CompilerParams(has_side_effects=DATAFLOW),
    )(*operands)
    sems = [(outs[2 * k], outs[2 * k + 1]) for k in range(len(slots))]
    return sems, list(outs[n_out_sem:n_out_sem + nb]), outs[-1]


def _pair_sum(stacks, lands, place):
    n = len(stacks)

    def body(place_ref, *refs):
        k = pl.program_id(0)
        for m in range(n):
            mine, theirs, out, land = refs[m], refs[n + m], refs[2 * n + m], refs[3 * n + m]
            total = (mine[0, 0].astype(F32) + theirs[0].astype(F32)).astype(out.dtype)
            out[0] = total

            @pl.when(k == place_ref[1])
            def _():
                land[0] = total

    in_specs = [pl.BlockSpec((1, 1) + a.shape[1:], lambda k, place_ref: (k, place_ref[0], 0, 0)) for a in stacks]
    in_specs += [pl.BlockSpec((1,) + a.shape[1:], lambda k, place_ref: (k, 0, 0)) for a in lands]
    out_specs = [pl.BlockSpec((1,) + a.shape[1:], lambda k, place_ref: (k, 0, 0)) for a in lands]
    out_specs += [pl.BlockSpec((1,) + a.shape[1:], lambda k, place_ref: (place_ref[1], 0, 0)) for a in lands]
    outs = pl.pallas_call(
        body, name="pair_sum_" + "_".join(str(a.shape[1]) for a in stacks),
        grid_spec=pltpu.PrefetchScalarGridSpec(num_scalar_prefetch=1, grid=(N_DEV // 2,), in_specs=in_specs,
                                               out_specs=out_specs),
        out_shape=[pltpu.HBM(a.shape, a.dtype) for a in lands] * 2,
        compiler_params=_params(40, 1),
    )(place, *_in_hbm(*[a.reshape((N_DEV // 2, 2) + a.shape[1:]) for a in stacks], *lands))
    return list(outs[:n]), list(outs[n:])


def _reduce_adam(parts, w, m, v, name):
    rows, cols = w.shape
    n_parts = parts.shape[0]
    rb = rows
    for cand in (256, 176, 128):
        if rows % cand == 0 and rows > cand:
            rb = cand
            break

    def body(p_ref, w_ref, m_ref, v_ref, g_out, d_out, m_out, v_out):
        g = p_ref[0].astype(F32)
        for j in range(1, n_parts):
            g = g + p_ref[j].astype(F32)
        g_out[...] = g
        d_out[...], m_out[...], v_out[...] = _adam(g, w_ref[...], m_ref[...], v_ref[...])

    blk = pl.BlockSpec((rb, cols), lambda i: (i, 0))
    out = jax.ShapeDtypeStruct((rows, cols), F32)
    return pl.pallas_call(
        body, name=name, grid=(rows // rb,),
        in_specs=[pl.BlockSpec((n_parts, rb, cols), lambda i: (0, i, 0)), blk, blk, blk],
        out_specs=[blk] * 4, out_shape=[out] * 4,
        compiler_params=_params(40, 1),
    )(*_in_hbm(parts, w, m, v))


def _cols_from_stack(stack):
    n, r, c = stack.shape
    return jnp.transpose(stack, (1, 0, 2)).reshape(r, n * c)


def _block_diag(w):
    z = jnp.zeros((N_SLAB, 64, 64), w.dtype)
    pairs = w.reshape(N_SLAB, 2, 64, 64)
    top = jnp.concatenate([pairs[:, 0], z], axis=2)
    bottom = jnp.concatenate([z, pairs[:, 1]], axis=2)
    return jnp.concatenate([top, bottom], axis=1)


def _adam(g, w, m, v):
    m_new = ADAM_B1 * m + (1.0 - ADAM_B1) * g
    v_new = ADAM_B2 * v + (1.0 - ADAM_B2) * (g * g)
    m_hat = m_new / (1.0 - ADAM_B1 ** ADAM_STEP)
    v_hat = v_new / (1.0 - ADAM_B2 ** ADAM_STEP)
    return (-ADAM_LR) * (m_hat / (jnp.sqrt(v_hat) + ADAM_EPS) + ADAM_WD * w), m_new, v_new


WIDE = ("ln2_g", "lnf_g")
HALF = ("pool_scale", "conv_b", "b_a", "b_i", "lam", "gn_pool_g", "gn_lru_g")
VECTORS = [(k, D_MODEL) for k in WIDE] + [(k, 512) for k in HALF]
VECTOR_ROWS = sum(width // SLAB for _, width in VECTORS)
LOSS_ROW = -(-VECTOR_ROWS // 8) * 8
CONV_AT = LOSS_ROW + 8
CONV_LANES = LRU_WIDTH // SLAB
PACK_F_ROWS = CONV_AT + CONV_WIDTH * CONV_LANES
MATRIX_ROWS = N_SLAB * SLAB
HEAD = SLAB // 2
GATE_ROWS = N_SLAB * HEAD
PACK_B_ROWS = MATRIX_ROWS + 2 * GATE_ROWS


def _pack_small(vectors, pool_g, wa_g, wi_g, conv, sq):
    n_vec = len(vectors)

    def body(*refs):
        vec = refs[:n_vec]
        pw_ref, wa_ref, wi_ref, cw_ref, sq_ref, out, out_b = refs[n_vec:]
        out[...] = jnp.zeros_like(out)
        row = 0
        for ref, (_, width) in zip(vec, VECTORS):
            for k in range(width // SLAB):
                out[row:row + 1, :] = ref[:, k * SLAB:(k + 1) * SLAB]
                row += 1
        for tap in range(CONV_WIDTH):
            for k in range(CONV_LANES):
                at = CONV_AT + tap * CONV_LANES + k
                out[at:at + 1, :] = cw_ref[tap:tap + 1, k * SLAB:(k + 1) * SLAB]
        total = sq_ref[:, 0:SLAB]
        for k in range(1, D_MODEL // SLAB):
            total = total + sq_ref[:, k * SLAB:(k + 1) * SLAB]
        out[LOSS_ROW:LOSS_ROW + 1, :] = total
        left = lax.broadcasted_iota(jnp.int32, (HEAD, SLAB), 1) < HEAD
        for s in range(N_SLAB):
            out_b[s * SLAB:(s + 1) * SLAB, :] = pw_ref[s].astype(BF16)
            for i, ref in enumerate((wa_ref, wi_ref)):
                at = MATRIX_ROWS + i * GATE_ROWS + s * HEAD
                out_b[at:at + HEAD, :] = jnp.where(left, ref[s, 0:HEAD, :], ref[s, HEAD:SLAB, :]).astype(BF16)

    return pl.pallas_call(
        body, name="pack_small",
        out_shape=[jax.ShapeDtypeStruct((PACK_F_ROWS, SLAB), F32), jax.ShapeDtypeStruct((PACK_B_ROWS, SLAB), BF16)],
    )(*vectors, pool_g, wa_g, wi_g, conv, sq)


def _small_reduce_adam(my_index, parts, parts_b, vec_w, vec_m, vec_v, pool_wmv, wa_wmv, wi_wmv, conv_wmv):
    n_vec = len(VECTORS)
    n_parts = parts.shape[0]

    def body(*refs):
        me_ref, p_ref, pb_ref = refs[0], refs[1], refs[2]
        refs = refs[1:]
        w_refs, m_refs, v_refs = (refs[2 + k * n_vec:2 + (k + 1) * n_vec] for k in range(3))
        at = 2 + 3 * n_vec
        pw_w, pw_m, pw_v = refs[at:at + 3]
        gates = (refs[at + 3:at + 6], refs[at + 6:at + 9])
        cv_w, cv_m, cv_v = refs[at + 9:at + 12]
        outs = refs[at + 12:-1]
        total = refs[-1]
        total[...] = p_ref[0]
        for j in range(1, n_parts):
            total[...] += p_ref[j]
        row = 0
        for i, (_, width) in enumerate(VECTORS):
            n_rows = width // SLAB
            g = jnp.concatenate([total[row + k:row + k + 1, :] for k in range(n_rows)], axis=1)
            row += n_rows
            d, m_new, v_new = _adam(g, w_refs[i][...], m_refs[i][...], v_refs[i][...])
            for ref, val in zip(outs[4 * i:4 * i + 4], (g, d, m_new, v_new)):
                ref[...] = val
        tail = outs[4 * n_vec:]

        def summed(first, count):
            g = pb_ref[0, first:first + count, :].astype(F32)
            for j in range(1, n_parts):
                g = g + pb_ref[j, first:first + count, :].astype(F32)
            return g

        g = summed(0, MATRIX_ROWS)
        d, m_new, v_new = _adam(g, pw_w[...], pw_m[...], pw_v[...])
        for ref, val in zip(tail[0:4], (g, d, m_new, v_new)):
            ref[...] = val
        for i, (w, m, v) in enumerate(gates):
            g = summed(MATRIX_ROWS + i * GATE_ROWS, GATE_ROWS)
            for s in range(N_SLAB):
                for hd in range(2):
                    block = g[s * HEAD:(s + 1) * HEAD, hd * HEAD:(hd + 1) * HEAD]
                    rows = slice((2 * s + hd) * HEAD, (2 * s + hd + 1) * HEAD)
                    d, m_new, v_new = _adam(block, w[rows, :], m[rows, :], v[rows, :])
                    for ref, val in zip(tail[4 + 4 * i:8 + 4 * i], (block, d, m_new, v_new)):
                        ref[rows, :] = val
        me = me_ref[0, 0]
        taps = []
        for tap in range(CONV_WIDTH):
            row = total[pl.ds(CONV_AT + tap * CONV_LANES + me // 2, 1), :]
            taps.append(jnp.where(me % 2 == 0, row[:, :HEAD], row[:, HEAD:]))
        g = jnp.concatenate(taps, axis=0)
        d, m_new, v_new = _adam(g, cv_w[...], cv_m[...], cv_v[...])
        for ref, val in zip(tail[12:16], (g, d, m_new, v_new)):
            ref[...] = val
        tail[16][...] = (0.5 / D_MODEL) * jnp.sum(total[LOSS_ROW:LOSS_ROW + 1, :], axis=1, keepdims=True)

    out_shape = []
    for _, width in VECTORS:
        out_shape += [jax.ShapeDtypeStruct((1, width), F32)] * 4
    out_shape += [jax.ShapeDtypeStruct((MATRIX_ROWS, SLAB), F32)] * 4
    out_shape += [jax.ShapeDtypeStruct((2 * GATE_ROWS, HEAD), F32)] * 8
    out_shape += [jax.ShapeDtypeStruct(conv_wmv[0].shape, F32)] * 4 + [jax.ShapeDtypeStruct((1, 1), F32)]
    operands = [parts, parts_b, *vec_w, *vec_m, *vec_v, *pool_wmv, *wa_wmv, *wi_wmv, *conv_wmv]

    def whole(a):
        return pl.BlockSpec(a.shape, lambda i, n=len(a.shape): (0,) * n)

    outs = pl.pallas_call(
        body, name="adam_small", grid=(1,),
        in_specs=[pl.BlockSpec(memory_space=pltpu.SMEM)] + [whole(a) for a in operands],
        out_specs=[whole(a) for a in out_shape], out_shape=out_shape,
        scratch_shapes=[pltpu.VMEM((PACK_F_ROWS, SLAB), F32)],
        compiler_params=_params(40, 1),
    )(my_index, *_in_hbm(*operands))
    vec_out = [tuple(outs[4 * i:4 * i + 4]) for i in range(n_vec)]
    tail = outs[4 * n_vec:]
    return vec_out, tuple(tail[0:4]), tuple(tail[4:8]), tuple(tail[8:12]), tuple(tail[12:16]), tail[16]


def kernel(x, ln1_g, w_in, pool_w, pool_scale, conv_w, conv_b, w_a, b_a, w_i, b_i, lam, gn_pool_g, gn_lru_g, w_out, ln2_g, w_ffn_gate, w_ffn_up, w_ffn_down, lnf_g, loss_target, m_ln1_g, m_w_in, m_pool_w, m_pool_scale, m_conv_w, m_conv_b, m_w_a, m_b_a, m_w_i, m_b_i, m_lam, m_gn_pool_g, m_gn_lru_g, m_w_out, m_ln2_g, m_w_ffn_gate, m_w_ffn_up, m_w_ffn_down, m_lnf_g, v_ln1_g, v_w_in, v_pool_w, v_pool_scale, v_conv_w, v_conv_b, v_w_a, v_b_a, v_w_i, v_b_i, v_lam, v_gn_pool_g, v_gn_lru_g, v_w_out, v_ln2_g, v_w_ffn_gate, v_w_ffn_up, v_w_ffn_down, v_lnf_g):
    weights = dict(ln1_g=ln1_g, w_in=w_in, pool_w=pool_w, pool_scale=pool_scale, conv_w=conv_w, conv_b=conv_b,
                   w_a=w_a, b_a=b_a, w_i=w_i, b_i=b_i, lam=lam, gn_pool_g=gn_pool_g, gn_lru_g=gn_lru_g,
                   w_out=w_out, ln2_g=ln2_g, w_ffn_gate=w_ffn_gate, w_ffn_up=w_ffn_up, w_ffn_down=w_ffn_down,
                   lnf_g=lnf_g)
    mom1 = dict(ln1_g=m_ln1_g, w_in=m_w_in, pool_w=m_pool_w, pool_scale=m_pool_scale, conv_w=m_conv_w,
                conv_b=m_conv_b, w_a=m_w_a, b_a=m_b_a, w_i=m_w_i, b_i=m_b_i, lam=m_lam, gn_pool_g=m_gn_pool_g,
                gn_lru_g=m_gn_lru_g, w_out=m_w_out, ln2_g=m_ln2_g, w_ffn_gate=m_w_ffn_gate,
                w_ffn_up=m_w_ffn_up, w_ffn_down=m_w_ffn_down, lnf_g=m_lnf_g)
    mom2 = dict(ln1_g=v_ln1_g, w_in=v_w_in, pool_w=v_pool_w, pool_scale=v_pool_scale, conv_w=v_conv_w,
                conv_b=v_conv_b, w_a=v_w_a, b_a=v_b_a, w_i=v_w_i, b_i=v_b_i, lam=v_lam, gn_pool_g=v_gn_pool_g,
                gn_lru_g=v_gn_lru_g, w_out=v_w_out, ln2_g=v_ln2_g, w_ffn_gate=v_w_ffn_gate,
                w_ffn_up=v_w_ffn_up, w_ffn_down=v_w_ffn_down, lnf_g=v_lnf_g)

    xs = x[0]
    target = loss_target[0]

    shard = dict(w_in=lambda a: a[0].T, w_ffn_gate=lambda a: a[0].T, w_ffn_up=lambda a: a[0].T,
                 w_out=lambda a: a[0], w_ffn_down=lambda a: a[0], conv_w=lambda a: a[0])
    unshard = dict(w_in=lambda a: a.T[None], w_ffn_gate=lambda a: a.T[None], w_ffn_up=lambda a: a.T[None],
                   w_out=lambda a: a[None], w_ffn_down=lambda a: a[None], conv_w=lambda a: a[None])

    gathered = ("w_in", "conv_w", "w_out", "w_ffn_gate", "w_ffn_up", "w_ffn_down")
    groups = ((0, 1), (2,), (3, 4), (5,))
    sources = [shard[k](weights[k]) if k == "conv_w" else shard[k](weights[k]).astype(BF16) for k in gathered]
    my_index = 4 * lax.axis_index("x") + 2 * lax.axis_index("y") + lax.axis_index("c")
    lands = [lax.empty((N_DEV,) + a.shape, a.dtype) for a in sources]

    def first_hop(n, at=0):
        return _numbered(_fan_out(range(at, at + n), range(at + n, at + 2 * n), SAME_CORE + SIBLING + MYSELF, 0))

    def second_hop(n, at=0):
        return _numbered(_relay(range(at, at + n), SAME_CORE))

    start = []
    for g, members in enumerate(groups):
        start += _numbered(_fan_out(members, [6 + m for m in members], SAME_CORE + SIBLING + MYSELF, 0), g)
    sems, bufs, _ = _comm_call("gather_start", sources + lands,
                               start=(start, [sum(cp.group == g for cp in start) for g in range(len(groups))]))
    sources, lands = bufs[:6], bufs[6:]

    def group_buffers(g):
        return [sources[m] for m in groups[g]] + [lands[m] for m in groups[g]]

    def relayed(tag, g, after):
        n = len(groups[g])
        relay = second_hop(n, n)
        relay_sems, bufs, _ = _comm_call("gather_relay_" + tag, group_buffers(g), wait=(sems[g], first_hop(n)),
                                         start=(relay, (len(relay),)), after=after)
        return relay_sems[0], bufs[n:]

    relay_sem, bufs = relayed("in", 0, None)
    g_in, g_conv = _comm_call("gather_wait_in", bufs, wait=(relay_sem, second_hop(2)))[1]
    w_in_f = g_in.reshape(IN_WIDTH, D_MODEL)
    conv_w_f = _cols_from_stack(g_conv)

    wa_bd = _block_diag(w_a[0])
    wi_bd = _block_diag(w_i[0])
    lnf_row = lnf_g.reshape(1, D_MODEL)

    u_pool, u_lru, u_gate = _fwd_in(xs, ln1_g, w_in_f)
    y_pool, h, y_lru, decay, root = _mixer_fwd(u_pool, u_lru, u_gate, pool_w[0], pool_scale, conv_w_f, conv_b,
                                               wa_bd, b_a, wi_bd, b_i, lam)
    relay_sem, bufs = relayed("out", 1, y_pool)
    gate_up = second_hop(2, 3)
    (gate_up_sem,), bufs, _ = _comm_call("gather_wait_out", bufs + group_buffers(2),
                                         wait=[(relay_sem, second_hop(1)), (sems[2], first_hop(2, 1))],
                                         start=(gate_up, (len(gate_up),)))
    g_out, gate_up_lands = bufs[0], bufs[3:]
    w_out_f = g_out.reshape(D_MODEL, D_MODEL)
    h1, n2 = _fwd_out(xs, y_pool, y_lru, gn_pool_g, gn_lru_g, w_out_f, ln2_g)
    relay_sem, bufs = relayed("down", 3, n2)
    g_gate, g_up, g_down = _comm_call("gather_wait_ffn", gate_up_lands + bufs,
                                      wait=[(gate_up_sem, second_hop(2)), (relay_sem, second_hop(1, 2))])[1]
    w_gate_f = g_gate.reshape(D_FF, D_MODEL)
    w_up_f = g_up.reshape(D_FF, D_MODEL)
    w_down_f = g_down.reshape(D_FF, D_MODEL)
    g_act, u_act, dh2, dh2b, d_lnf, sq = _ffn_fwd(h1, n2, target, lnf_row, w_gate_f, w_up_f, w_down_f)

    def direct(tag, stacks, wholes, after):
        sources = list(stacks) + list(wholes)
        n, n_st = len(sources), len(stacks)
        lands = [lax.empty(a.shape if i < n_st else (N_DEV,) + a.shape, a.dtype) for i, a in enumerate(sources)]
        copies = _numbered(_scatter(range(n_st), range(n, n + n_st), EVERYONE + MYSELF)
                           + _fan_out(range(n_st, n), range(n + n_st, 2 * n), EVERYONE + MYSELF, 0))
        sem, bufs, token = _comm_call(tag + "_start", sources + lands, start=(copies, (len(copies),)), after=after)
        return (tag, sem[0], bufs, copies), token

    def direct_finish(started, after):
        tag, sem, bufs, copies = started
        _, bufs, _ = _comm_call(tag + "_wait", bufs, wait=(sem, copies), after=after)
        return bufs[len(bufs) // 2:]

    place = jnp.stack([lax.axis_index("c"), 2 * lax.axis_index("x") + lax.axis_index("y")]).astype(jnp.int32)

    d_gate, d_up, d_down, dn2 = _ffn_bwd(n2, dh2b, g_act, u_act, w_gate_f, w_up_f, w_down_f)
    down_stack, *ffn_stacks = [d.reshape(N_DEV, D_FF // N_DEV, D_MODEL) for d in (d_down, d_gate, d_up)]
    pair_lands = [lax.empty((N_DEV // 2,) + a.shape[1:], a.dtype) for a in ffn_stacks]
    pair_copies = _numbered(_to_sibling(range(2), range(2, 4)))
    down_copies = _numbered(_scatter([4], [5], EVERYONE + MYSELF), 1)
    (sem, down_sem), bufs, token = _comm_call(
        "ffn_grads_start", ffn_stacks + pair_lands + [down_stack, lax.empty(down_stack.shape, down_stack.dtype)],
        start=(pair_copies + down_copies, (len(pair_copies), len(down_copies))))
    bufs, down_bufs = bufs[:4], bufs[4:]
    dh1, dy_pool, dy_lru, d_out, d_ln2, d_gnp, d_gnl = _bwd_out(dn2, dh2, h1, y_pool, y_lru, gn_pool_g, gn_lru_g,
                                                                 w_out_f, ln2_g, token)
    _, bufs, _ = _comm_call("ffn_pair_wait", bufs, wait=(sem, pair_copies), after=dh1)
    ffn_sums, ffn_lands = _pair_sum(bufs[:2], bufs[2:], place)
    ffn_copies = _numbered(_to_chips(range(2), range(2, 4)))
    ffn_sem, ffn_bufs, token = _comm_call("ffn_chip_start", ffn_sums + ffn_lands, start=(ffn_copies, (len(ffn_copies),)))
    (du_pool, du_lru, du_gate, d_pw, d_ps, d_cw, d_cb, d_wa, d_ba, d_wi, d_bi, d_lam) = _mixer_bwd(
        u_pool, u_lru, u_gate, h, decay, root, dy_pool, dy_lru, pool_w[0], pool_scale, conv_w_f, conv_b,
        wa_bd, b_a, wi_bd, b_i, lam, token)

    vec_grads = dict(ln2_g=d_ln2, lnf_g=d_lnf, pool_scale=d_ps, conv_b=d_cb, b_a=d_ba, b_i=d_bi,
                     lam=d_lam, gn_pool_g=d_gnp, gn_lru_g=d_gnl)
    packed, packed_b = _pack_small([vec_grads[k] for k, _ in VECTORS], d_pw, d_wa, d_wi, d_cw, sq)
    small_started, token = direct("small", [d_out.reshape(N_DEV, D_MODEL // N_DEV, D_MODEL)], [packed, packed_b], None)
    grad_x, d_in, d_ln1 = _bwd_in(xs, dh1, du_pool, du_lru, du_gate, ln1_g, w_in_f, token)
    in_started, token = direct("in", [d_in.reshape(N_DEV, IN_WIDTH // N_DEV, D_MODEL)], [d_ln1], None)

    results = {}

    def reduce_adam(name, parts):
        outs = _reduce_adam(parts, shard[name](weights[name]), shard[name](mom1[name]), shard[name](mom2[name]),
                            "adam_" + name)
        results[name] = tuple(unshard[name](o) for o in outs)
        return outs[0]

    shard["ln1_g"] = unshard["ln1_g"] = lambda a: a
    _, ffn_bufs, _ = _comm_call("ffn_chip_wait", ffn_bufs, wait=(ffn_sem[0], ffn_copies), after=token)
    done = [reduce_adam(name, parts) for name, parts in zip(("w_ffn_gate", "w_ffn_up"), ffn_bufs[2:])]
    _, (_, r_down), _ = _comm_call("down_wait", down_bufs, after=done,
                                   wait=(down_sem, _numbered(_scatter([0], [1], EVERYONE + MYSELF))))
    done.append(reduce_adam("w_ffn_down", r_down))
    r_out, r_small, r_small_b = direct_finish(small_started, done)
    done.append(reduce_adam("w_out", r_out))

    def as_row(a, width):
        return a.reshape(1, width)

    def as_matrix(a):
        return a.reshape(MATRIX_ROWS, SLAB)

    def as_heads(a):
        return a.reshape(2 * GATE_ROWS, HEAD)

    states = (weights, mom1, mom2)
    vec_out, pool_out, wa_out, wi_out, conv_out, loss_11 = _small_reduce_adam(
        my_index.astype(jnp.int32).reshape(1, 1), r_small, r_small_b,
        *[[as_row(t[k], w) for k, w in VECTORS] for t in states], [as_matrix(t["pool_w"]) for t in states],
        [as_heads(t["w_a"]) for t in states], [as_heads(t["w_i"]) for t in states], [t["conv_w"][0] for t in states])
    for (k, _), outs in zip(VECTORS, vec_out):
        results[k] = tuple(o.reshape(weights[k].shape) for o in outs)
    for k, outs in (("pool_w", pool_out), ("w_a", wa_out), ("w_i", wi_out), ("conv_w", conv_out)):
        results[k] = tuple(o.reshape(weights[k].shape) for o in outs)
    loss = loss_11[0, 0]
    r_in, r_ln1 = direct_finish(in_started, done + [loss_11])
    reduce_adam("w_in", r_in)
    reduce_adam("ln1_g", r_ln1)

    order = ["ln1_g", "w_in", "pool_w", "pool_scale", "conv_w", "conv_b", "w_a", "b_a", "w_i", "b_i", "lam",
             "gn_pool_g", "gn_lru_g", "w_out", "ln2_g", "w_ffn_gate", "w_ffn_up", "w_ffn_down", "lnf_g"]
    return (loss, grad_x[None],
            *[results[k][0] for k in order], *[results[k][1] for k in order],
            *[results[k][2] for k in order], *[results[k][3] for k in order])
```

```python
from typing import Any, NamedTuple

import jax
import jax.numpy as jnp
from jax import lax
from jax.experimental import pallas as pl
from jax.experimental.pallas import tpu as pltpu

F32 = jnp.float32
BF16 = jnp.bfloat16

N_DEV = 8
D_MODEL = 1024
POOL_WIDTH = 512
LRU_WIDTH = 512
IN_WIDTH = 1536
D_FF = 2816
N_SLAB = 4
SLAB = 128
CONV_WIDTH = 4
LRU_C = 8.0
EPS = 1e-6
HALO = 16
FF_CHUNK = 256

ADAM_LR = 0.001
ADAM_B1 = 0.9
ADAM_B2 = 0.999
ADAM_EPS = 1e-08
ADAM_WD = 0.01
ADAM_STEP = 10

MIB = 1 << 20
MESH = pl.DeviceIdType.MESH


def _params(vmem_mib, n_axes=0):
    sem = ("arbitrary",) * n_axes if n_axes else None
    return pltpu.CompilerParams(dimension_semantics=sem, vmem_limit_bytes=vmem_mib * MIB)


def _in_hbm(*arrays):
    return [pltpu.with_memory_space_constraint(a, pltpu.HBM) for a in arrays]


def _mm(a, b):
    return jnp.dot(a, b, preferred_element_type=F32)


def _mm_nt(a, b):
    return lax.dot_general(a, b, (((1,), (1,)), ((), ())), preferred_element_type=F32)


def _mm_tn(a, b):
    return lax.dot_general(a, b, (((0,), (0,)), ((), ())), preferred_element_type=F32)


def _rms(x, g):
    rstd = lax.rsqrt(jnp.mean(x * x, axis=-1, keepdims=True) + EPS)
    xhat = x * rstd
    return xhat * g, xhat, rstd


def _rms_bwd(dy, xhat, rstd, g):
    gy = dy * g
    dx = rstd * (gy - xhat * jnp.mean(gy * xhat, axis=-1, keepdims=True))
    return dx, jnp.sum(dy * xhat, axis=0, keepdims=True)


def _gelu(z):
    t = jnp.tanh(0.7978845608028654 * (z + 0.044715 * z * z * z))
    return 0.5 * z * (1.0 + t), t


def _gelu_grad(z, t):
    return 0.5 * (1.0 + t) + 0.5 * z * (1.0 - t * t) * 0.7978845608028654 * (1.0 + 3.0 * 0.044715 * z * z)


def _softplus_neg(lam):
    x = -lam
    e = jnp.exp(-jnp.abs(x))
    u = 1.0 + e
    l1p = jnp.where(u == 1.0, e, jnp.log(u) * e / (u - 1.0))
    return jnp.maximum(x, 0.0) + l1p


def _one_minus_square(la, a):
    x = 2.0 * la
    series = -x * (1.0 + x * (0.5 + x * (1.0 / 6.0 + x * (1.0 / 24.0))))
    return jnp.where(x > -0.06, series, 1.0 - a * a)


def _sigmoid(x):
    return 0.5 * jnp.tanh(0.5 * x) + 0.5


def _down(v, d):
    return pltpu.roll(v, d, 0)


def _up(v, d):
    return pltpu.roll(v, v.shape[0] - d, 0)


def _slab_major(s, dtype=F32):
    return pltpu.HBM((N_SLAB, s, SLAB), dtype)


def _slabs_spec(tb):
    return pl.BlockSpec((N_SLAB, tb, SLAB), lambda i: (0, i, 0))


def _read_slabs(ref):
    return jnp.concatenate([ref[k] for k in range(N_SLAB)], axis=1)


def _write_slabs(ref, value):
    for k in range(N_SLAB):
        ref[k] = value[:, k * SLAB:(k + 1) * SLAB]


def _token_block(s, most=512):
    for rows in (most, 512, 256):
        if rows <= most and s % rows == 0 and s > rows:
            return rows
    return s


def _time_chunk(s):
    return 512 if s % 512 == 0 and s > 512 else 256 if s % 256 == 0 else s


def _fwd_in(x, ln1_g, w_in_t):
    s = x.shape[0]
    tb = _token_block(s, 1024)

    def body(x_ref, g_ref, w_ref, up_ref, ul_ref, ug_ref):
        n, _, _ = _rms(x_ref[...], g_ref[...])
        proj = _mm_nt(n.astype(BF16), w_ref[...])
        _write_slabs(up_ref, proj[:, :POOL_WIDTH])
        _write_slabs(ul_ref, proj[:, POOL_WIDTH:POOL_WIDTH + LRU_WIDTH])
        _write_slabs(ug_ref, proj[:, POOL_WIDTH + LRU_WIDTH:])

    out = _slab_major(s)
    return pl.pallas_call(
        body, name="fwd_in", grid=(s // tb,),
        in_specs=[pl.BlockSpec((tb, D_MODEL), lambda i: (i, 0)),
                  pl.BlockSpec((1, D_MODEL), lambda i: (0, 0)),
                  pl.BlockSpec((IN_WIDTH, D_MODEL), lambda i: (0, 0))],
        out_specs=[_slabs_spec(tb)] * 3,
        out_shape=[out, out, out],
        compiler_params=_params(40, 1),
    )(*_in_hbm(x, ln1_g, w_in_t))


def _pool_denominator(t0, row, window):
    return jnp.minimum((t0 + row + 1).astype(F32), window)


def _causal_window(ext, deeper):
    s = ext + _down(ext, 1)
    s = s + deeper[0] * _down(s, 2)
    s = s + deeper[1] * _down(s, 4)
    s = s + deeper[2] * _down(s, 8)
    return s[HALO:]


def _anticausal_window(ext, deeper, rows):
    s = ext + _up(ext, 1)
    s = s + deeper[0] * _up(s, 2)
    s = s + deeper[1] * _up(s, 4)
    s = s + deeper[2] * _up(s, 8)
    return s[:rows]


def _conv(taps, cw, cb):
    return cw[3] * taps[0] + cw[2] * taps[1] + cw[1] * taps[2] + cw[0] * taps[3] + cb


def _lru_gates(xc, wa, ba, wi, bi):
    xb = xc.astype(BF16)
    r = _sigmoid(_mm(xb, wa) + ba)
    i = _sigmoid(_mm(xb, wi) + bi)
    return xb, r, i


def _lru_decay(r, sp):
    la = (-LRU_C) * r * sp
    a = jnp.exp(la)
    return a, jnp.sqrt(jnp.maximum(_one_minus_square(la, a), 0.0))


SUBLANES = 8


def _scan_pads(a_pad, b_pad, rows, causal):
    spare = slice(0, SUBLANES) if causal else slice(rows, rows + SUBLANES)
    a_pad[spare, :] = jnp.ones((SUBLANES, SLAB), F32)
    b_pad[spare, :] = jnp.zeros((SUBLANES, SLAB), F32)


def _scan_causal(a, b, h_prev, a_pad, b_pad, rows):
    d = 1
    while d < min(SUBLANES, rows):
        a_pad[SUBLANES:, :] = a
        b_pad[SUBLANES:, :] = b
        b = a * b_pad[SUBLANES - d:SUBLANES - d + rows, :] + b
        a = a * a_pad[SUBLANES - d:SUBLANES - d + rows, :]
        d *= 2
    h = jnp.broadcast_to(h_prev, (d, SLAB))
    out = []
    for at in range(0, rows, d):
        h = a[at:at + d] * h + b[at:at + d]
        out.append(h)
    return jnp.concatenate(out, axis=0)


def _scan_anticausal(a, b, l_next, a_pad, b_pad, rows):
    d = 1
    while d < min(SUBLANES, rows):
        a_pad[:rows, :] = a
        b_pad[:rows, :] = b
        b = a * b_pad[d:d + rows, :] + b
        a = a * a_pad[d:d + rows, :]
        d *= 2
    lead = jnp.broadcast_to(l_next, (d, SLAB))
    out = []
    for at in range(rows - d, -1, -d):
        lead = a[at:at + d] * lead + b[at:at + d]
        out.append(lead)
    return jnp.concatenate(out[::-1], axis=0)


def _slab_scalars():
    slab = pl.program_id(0)
    deeper = [jnp.where(slab > k, 1.0, 0.0).astype(F32) for k in range(N_SLAB - 1)]
    window = jnp.left_shift(jnp.int32(2), slab).astype(F32)
    inverse = jnp.where(slab == 0, 0.5, jnp.where(slab == 1, 0.25, jnp.where(slab == 2, 0.125, 0.0625))).astype(F32)
    return deeper, (window, inverse)


def _window_mean(total, t0, row, window, at_start):
    if at_start:
        return total / _pool_denominator(t0, row, window[0])
    return total * window[1]


def _slab_specs(s):
    seq = pl.BlockSpec((None, s, SLAB), lambda k: (k, 0, 0))
    mat = pl.BlockSpec((1, SLAB, SLAB), lambda k: (k, 0, 0))
    vec = pl.BlockSpec((1, SLAB), lambda k: (0, k))
    taps = pl.BlockSpec((CONV_WIDTH, SLAB), lambda k: (0, k))
    return seq, mat, vec, taps


def _mixer_fwd(u_pool, u_lru, u_gate, pool_w, pool_scale, conv_w, conv_b, wa_bd, b_a, wi_bd, b_i, lam):
    s = u_pool.shape[1]
    tc = _time_chunk(s)
    n_chunks = s // tc

    def body(up_ref, ul_ref, ug_ref, pw_ref, ps_ref, cw_ref, cb_ref, wa_ref, ba_ref, wi_ref, bi_ref, lam_ref,
             yp_ref, h_ref, yl_ref, a_ref, mult_ref, shift_ref, a_pad, b_pad):
        _scan_pads(a_pad, b_pad, tc, causal=True)
        deeper, window = _slab_scalars()
        pw = pw_ref[0].astype(BF16)
        wa = wa_ref[0].astype(BF16)
        wi = wi_ref[0].astype(BF16)
        ps, cb, ba, bi = ps_ref[...], cb_ref[...], ba_ref[...], bi_ref[...]
        cw = [cw_ref[k:k + 1, :] for k in range(CONV_WIDTH)]
        sp = _softplus_neg(lam_ref[...])
        row = lax.broadcasted_iota(jnp.int32, (tc, SLAB), 0)

        def chunk(t0, ext_p, ext_l, h_prev, at_start=False):
            rows = pl.ds(t0, tc)
            d = _window_mean(_causal_window(ext_p, deeper), t0, row, window, at_start) - ext_p[HALO:]
            yp_ref[rows, :] = _mm(d.astype(BF16), pw) * ps
            shift_ref[...] = ext_l
            xc = _conv([shift_ref[HALO - k:HALO - k + tc, :] for k in range(CONV_WIDTH)], cw, cb)
            _, r, i = _lru_gates(xc, wa, ba, wi, bi)
            a, mult = _lru_decay(r, sp)
            a_ref[rows, :] = a
            mult_ref[rows, :] = mult
            h = _scan_causal(a, mult * (i * xc), h_prev, a_pad, b_pad, tc)
            h_ref[rows, :] = h
            yl_ref[rows, :] = h * _gelu(ug_ref[rows, :])[0]
            return h[tc - 1:tc, :]

        pad = jnp.zeros((HALO, SLAB), F32)
        h0 = chunk(0, jnp.concatenate([pad, up_ref[pl.ds(0, tc), :]], axis=0),
                   jnp.concatenate([pad, ul_ref[pl.ds(0, tc), :]], axis=0), jnp.zeros((1, SLAB), F32), at_start=True)

        def step(c, h_prev):
            t0 = pl.multiple_of(c * tc, tc)
            ext = pl.ds(pl.multiple_of(c * tc - HALO, HALO), tc + HALO)
            return chunk(t0, up_ref[ext, :], ul_ref[ext, :], h_prev)

        lax.fori_loop(1, n_chunks, step, h0)

    seq, mat, vec, taps = _slab_specs(s)
    out = _slab_major(s)
    return pl.pallas_call(
        body, name="mixer_fwd", grid=(N_SLAB,),
        in_specs=[seq, seq, seq, mat, vec, taps, vec, mat, vec, mat, vec, vec],
        out_specs=[seq] * 5, out_shape=[out] * 5,
        scratch_shapes=[pltpu.VMEM((tc + HALO, SLAB), F32), pltpu.VMEM((tc + SUBLANES, SLAB), F32),
                        pltpu.VMEM((tc + SUBLANES, SLAB), F32)],
        compiler_params=_params(48, 1),
    )(*_in_hbm(u_pool, u_lru, u_gate, pool_w, pool_scale, conv_w, conv_b, wa_bd, b_a, wi_bd, b_i, lam))


def _fwd_out(x, y_pool, y_lru, gn_pool_g, gn_lru_g, w_out, ln2_g):
    s = x.shape[0]
    tb = _token_block(s, 1024)

    def body(x_ref, yp_ref, yl_ref, gp_ref, gl_ref, w_ref, g2_ref, h1_ref, n2_ref):
        mp, _, _ = _rms(_read_slabs(yp_ref), gp_ref[...])
        ml, _, _ = _rms(_read_slabs(yl_ref), gl_ref[...])
        h1 = x_ref[...] + _mm(mp.astype(BF16), w_ref[:POOL_WIDTH, :]) + _mm(ml.astype(BF16), w_ref[POOL_WIDTH:, :])
        h1_ref[...] = h1
        n2_ref[...] = _rms(h1, g2_ref[...])[0].astype(BF16)

    row = pl.BlockSpec((tb, D_MODEL), lambda i: (i, 0))
    half = _slabs_spec(tb)
    return pl.pallas_call(
        body, name="fwd_out", grid=(s // tb,),
        in_specs=[row, half, half, pl.BlockSpec((1, 512), lambda i: (0, 0)), pl.BlockSpec((1, 512), lambda i: (0, 0)),
                  pl.BlockSpec((D_MODEL, D_MODEL), lambda i: (0, 0)), pl.BlockSpec((1, D_MODEL), lambda i: (0, 0))],
        out_specs=[row, row],
        out_shape=[pltpu.HBM((s, D_MODEL), F32), pltpu.HBM((s, D_MODEL), BF16)],
        compiler_params=_params(40, 1),
    )(*_in_hbm(x, y_pool, y_lru, gn_pool_g, gn_lru_g, w_out, ln2_g))


def _ffn_fwd(h1, n2, target, lnf_g, w_gate, w_up, w_down):
    s = h1.shape[0]
    tb = 512
    sub = 256
    n_ff = D_FF // FF_CHUNK

    def body(h1_ref, n2_ref, t_ref, gf_ref, wg_hbm, wu_hbm, wd_hbm,
             g_ref, u_ref, dh_ref, dhb_ref, dgf_ref, sq_ref, wg, wu, wd, act_ref, sem):
        first = pl.program_id(0) == 0
        loads = [pltpu.make_async_copy(src, dst, sem.at[k])
                 for k, (src, dst) in enumerate(((wg_hbm, wg), (wu_hbm, wu), (wd_hbm, wd)))]

        @pl.when(first)
        def _():
            for cp in loads:
                cp.start()
            for cp in loads[:2]:
                cp.wait()
            dgf_ref[...] = jnp.zeros_like(dgf_ref)
            sq_ref[...] = jnp.zeros_like(sq_ref)

        n2v = n2_ref[...]
        for c in range(n_ff):
            cols = slice(c * FF_CHUNK, (c + 1) * FF_CHUNK)
            g = _mm_nt(n2v, wg[cols, :])
            u = _mm_nt(n2v, wu[cols, :])
            g_ref[:, cols] = g.astype(BF16)
            u_ref[:, cols] = u.astype(BF16)
            act_ref[:, cols] = (g * jax.nn.sigmoid(g) * u).astype(BF16)

        @pl.when(first)
        def _():
            loads[2].wait()

        acc = _mm(act_ref[...], wd[...])
        gf = gf_ref[...]
        for r in range(tb // sub):
            rows = slice(r * sub, (r + 1) * sub)
            y, xhat, rstd = _rms(h1_ref[rows, :] + acc[rows, :], gf)
            err = y - t_ref[rows, :]
            sq_ref[...] += jnp.sum(err * err, axis=0, keepdims=True)
            dh2, dgf = _rms_bwd(err * (1.0 / D_MODEL), xhat, rstd, gf)
            dgf_ref[...] += dgf
            dh_ref[rows, :] = dh2
            dhb_ref[rows, :] = dh2.astype(BF16)

    row = pl.BlockSpec((tb, D_MODEL), lambda i: (i, 0))
    ff = pl.BlockSpec((tb, D_FF), lambda i: (i, 0))
    vec = pl.BlockSpec((1, D_MODEL), lambda i: (0, 0))
    anyspace = pl.BlockSpec(memory_space=pl.ANY)
    return pl.pallas_call(
        body, name="ffn_fwd", grid=(s // tb,),
        in_specs=[row, row, row, vec, anyspace, anyspace, anyspace],
        out_specs=[ff, ff, row, row, vec, vec],
        out_shape=[pltpu.HBM((s, D_FF), BF16), pltpu.HBM((s, D_FF), BF16),
                   pltpu.HBM((s, D_MODEL), F32), pltpu.HBM((s, D_MODEL), BF16),
                   jax.ShapeDtypeStruct((1, D_MODEL), F32), jax.ShapeDtypeStruct((1, D_MODEL), F32)],
        scratch_shapes=[pltpu.VMEM((D_FF, D_MODEL), BF16), pltpu.VMEM((D_FF, D_MODEL), BF16),
                        pltpu.VMEM((D_FF, D_MODEL), BF16), pltpu.VMEM((tb, D_FF), BF16),
                        pltpu.SemaphoreType.DMA((3,))],
        compiler_params=_params(60, 1),
    )(*_in_hbm(h1, n2, target, lnf_g, w_gate, w_up, w_down))


def _ffn_bwd(n2, dh2b, g, u, w_gate_t, w_up_t, w_down):
    s = n2.shape[0]
    tb = min(1024, s)
    n_ff = D_FF // FF_CHUNK
    n_tb = s // tb

    def body(n2_ref, dh_ref, g_ref, u_ref, wg_ref, wu_ref, wd_ref, dwg_ref, dwu_ref, dwd_ref, dn2_ref,
             dn2_acc, acc_g, acc_u, acc_d):
        j = pl.program_id(0)
        t = pl.program_id(1)
        rows = pl.ds(pl.multiple_of(t * tb, tb), tb)

        @pl.when(t == 0)
        def _():
            acc_g[...] = jnp.zeros_like(acc_g)
            acc_u[...] = jnp.zeros_like(acc_u)
            acc_d[...] = jnp.zeros_like(acc_d)

        @pl.when(j == 0)
        def _():
            dn2_acc[rows, :] = jnp.zeros((tb, D_MODEL), F32)

        n2v = n2_ref[...]
        dh = dh_ref[...]
        gv = g_ref[...].astype(F32)
        uv = u_ref[...].astype(F32)
        sg = jax.nn.sigmoid(gv)
        silu = gv * sg
        dact = _mm_nt(dh, wd_ref[...])
        dub = (dact * silu).astype(BF16)
        dgb = (dact * uv * (sg * (1.0 + gv * (1.0 - sg)))).astype(BF16)
        acc_d[...] += _mm_tn((silu * uv).astype(BF16), dh)
        acc_g[...] += _mm_tn(dgb, n2v)
        acc_u[...] += _mm_tn(dub, n2v)
        dn2_acc[rows, :] += _mm(jnp.concatenate([dgb, dub], axis=1),
                                jnp.concatenate([wg_ref[...], wu_ref[...]], axis=0))

        @pl.when(t == n_tb - 1)
        def _():
            dwg_ref[...] = acc_g[...].astype(BF16)
            dwu_ref[...] = acc_u[...].astype(BF16)
            dwd_ref[...] = acc_d[...].astype(BF16)

        @pl.when(j == n_ff - 1)
        def _():
            dn2_ref[...] = dn2_acc[rows, :]

    row = pl.BlockSpec((tb, D_MODEL), lambda j, t: (t, 0))
    act = pl.BlockSpec((tb, FF_CHUNK), lambda j, t: (t, j))
    w_row = pl.BlockSpec((FF_CHUNK, D_MODEL), lambda j, t: (j, 0))
    last = pl.BlockSpec((tb, D_MODEL), lambda j, t: (jnp.where(j == n_ff - 1, t, 0), 0))
    grad = pltpu.HBM((D_FF, D_MODEL), BF16)
    chunk_acc = pltpu.VMEM((FF_CHUNK, D_MODEL), F32)
    return pl.pallas_call(
        body, name="ffn_bwd", grid=(n_ff, n_tb),
        in_specs=[row, row, act, act, w_row, w_row, w_row],
        out_specs=[w_row, w_row, w_row, last],
        out_shape=[grad, grad, grad, pltpu.HBM((s, D_MODEL), F32)],
        scratch_shapes=[pltpu.VMEM((s, D_MODEL), F32), chunk_acc, chunk_acc, chunk_acc],
        compiler_params=_params(56, 2),
    )(*_in_hbm(n2, dh2b, g, u, w_gate_t, w_up_t, w_down))


def _bwd_out(dn2, dh2, h1, y_pool, y_lru, gn_pool_g, gn_lru_g, w_out, ln2_g, after):
    s = h1.shape[0]
    tb = _token_block(s)

    def body(dn2_ref, dh2_ref, h1_ref, yp_ref, yl_ref, gp_ref, gl_ref, w_ref, g2_ref, _after,
             dh1_ref, dyp_ref, dyl_ref, dwb_ref, dg2_ref, dgp_ref, dgl_ref, dw_ref):
        @pl.when(pl.program_id(0) == 0)
        def _():
            dw_ref[...] = jnp.zeros_like(dw_ref)
            dg2_ref[...] = jnp.zeros_like(dg2_ref)
            dgp_ref[...] = jnp.zeros_like(dgp_ref)
            dgl_ref[...] = jnp.zeros_like(dgl_ref)

        g2 = g2_ref[...]
        _, xhat2, rstd2 = _rms(h1_ref[...], g2)
        dres, dg2 = _rms_bwd(dn2_ref[...], xhat2, rstd2, g2)
        dg2_ref[...] += dg2
        dh1 = dh2_ref[...] + dres
        dh1_ref[...] = dh1
        dh1b = dh1.astype(BF16)
        gp, gl = gp_ref[...], gl_ref[...]
        mp, xhat_p, rstd_p = _rms(_read_slabs(yp_ref), gp)
        ml, xhat_l, rstd_l = _rms(_read_slabs(yl_ref), gl)
        dw_ref[:POOL_WIDTH, :] += _mm_tn(mp.astype(BF16), dh1b)
        dw_ref[POOL_WIDTH:, :] += _mm_tn(ml.astype(BF16), dh1b)
        dyp, dgp = _rms_bwd(_mm_nt(dh1b, w_ref[:POOL_WIDTH, :]), xhat_p, rstd_p, gp)
        dyl, dgl = _rms_bwd(_mm_nt(dh1b, w_ref[POOL_WIDTH:, :]), xhat_l, rstd_l, gl)
        _write_slabs(dyp_ref, dyp)
        _write_slabs(dyl_ref, dyl)
        dgp_ref[...] += dgp
        dgl_ref[...] += dgl

        @pl.when(pl.program_id(0) == s // tb - 1)
        def _():
            dwb_ref[...] = dw_ref[...].astype(BF16)

    row = pl.BlockSpec((tb, D_MODEL), lambda i: (i, 0))
    half = _slabs_spec(tb)
    vec = pl.BlockSpec((1, D_MODEL), lambda i: (0, 0))
    hvec = pl.BlockSpec((1, 512), lambda i: (0, 0))
    mat = pl.BlockSpec((D_MODEL, D_MODEL), lambda i: (0, 0))
    return pl.pallas_call(
        body, name="bwd_out", grid=(s // tb,),
        in_specs=[row, row, row, half, half, hvec, hvec, mat, vec, pl.BlockSpec(memory_space=pl.ANY)],
        out_specs=[row, half, half, mat, vec, hvec, hvec],
        out_shape=[pltpu.HBM((s, D_MODEL), F32), _slab_major(s),
                   _slab_major(s), pltpu.HBM((D_MODEL, D_MODEL), BF16),
                   jax.ShapeDtypeStruct((1, D_MODEL), F32), jax.ShapeDtypeStruct((1, 512), F32),
                   jax.ShapeDtypeStruct((1, 512), F32)],
        scratch_shapes=[pltpu.VMEM((D_MODEL, D_MODEL), F32)],
        compiler_params=_params(48, 1),
    )(*_in_hbm(dn2, dh2, h1, y_pool, y_lru, gn_pool_g, gn_lru_g, w_out, ln2_g), after)


def _mixer_bwd(u_pool, u_lru, u_gate, h, decay, root, dy_pool, dy_lru,
               pool_w, pool_scale, conv_w, conv_b, wa_bd, b_a, wi_bd, b_i, lam, after):
    s = u_pool.shape[1]
    tc = _time_chunk(s)
    n_chunks = s // tc

    def body(up_ref, ul_ref, ug_ref, h_ref, a_ref, mult_ref, dyp_ref, dyl_ref,
             pw_ref, ps_ref, cw_ref, cb_ref, wa_ref, ba_ref, wi_ref, bi_ref, lam_ref, _after,
             dup_ref, dul_ref, dug_ref, dpw_ref, dps_ref, dcw_ref, dcb_ref, dwa_ref, dba_ref, dwi_ref, dbi_ref, dlam_ref,
             a_pad, b_pad, u_shift, h_shift, a_shift, d_shift):
        _scan_pads(a_pad, b_pad, tc, causal=False)
        deeper, window = _slab_scalars()
        pw = pw_ref[0].astype(BF16)
        wa = wa_ref[0].astype(BF16)
        wi = wi_ref[0].astype(BF16)
        ps, cb, ba, bi = ps_ref[...], cb_ref[...], ba_ref[...], bi_ref[...]
        cw = [cw_ref[k:k + 1, :] for k in range(CONV_WIDTH)]
        lam_v = lam_ref[...]
        sp = _softplus_neg(lam_v)
        row = lax.broadcasted_iota(jnp.int32, (tc, SLAB), 0)
        for ref in (dpw_ref, dps_ref, dcw_ref, dcb_ref, dwa_ref, dba_ref, dwi_ref, dbi_ref, dlam_ref):
            ref[...] = jnp.zeros_like(ref)

        def chunk(t0, ext_p, ext_l, ext_h, carry, at_start=False):
            l_next, a_next, dxc_next, ddn_next = carry
            rows = pl.ds(t0, tc)
            u_shift[...] = ext_l
            taps = [u_shift[HALO - k:HALO - k + tc, :] for k in range(CONV_WIDTH)]
            xc = _conv(taps, cw, cb)
            xb, r, i = _lru_gates(xc, wa, ba, wi, bi)
            a, mult = a_ref[rows, :], mult_ref[rows, :]
            hv = ext_h[HALO:]
            h_shift[...] = ext_h
            h_before = h_shift[HALO - 1:HALO - 1 + tc, :]
            ug = ug_ref[rows, :]
            dyl = dyl_ref[rows, :]
            gel, th = _gelu(ug)
            dug_ref[rows, :] = (dyl * hv * _gelu_grad(ug, th)).astype(BF16)
            a_shift[:tc, :] = a
            a_shift[tc:, :] = jnp.broadcast_to(a_next, (SUBLANES, SLAB))
            a_after = a_shift[1:1 + tc, :]
            l = _scan_anticausal(a_after, dyl * gel, l_next, a_pad, b_pad, tc)
            dmult = l * (i * xc)
            di = l * mult * xc
            dxc = l * mult * i
            dla = (l * h_before) * a - jnp.where(mult > 0.0, dmult * (a * a) / mult, 0.0)
            dlam_ref[...] += jnp.sum(dla * r, axis=0, keepdims=True)
            dpa = (dla * ((-LRU_C) * sp)) * (r * (1.0 - r))
            dpi = di * (i * (1.0 - i))
            dpab = dpa.astype(BF16)
            dpib = dpi.astype(BF16)
            dwa_ref[0] += _mm_tn(xb, dpab)
            dwi_ref[0] += _mm_tn(xb, dpib)
            dba_ref[...] += jnp.sum(dpa, axis=0, keepdims=True)
            dbi_ref[...] += jnp.sum(dpi, axis=0, keepdims=True)
            dxc = dxc + _mm_nt(dpab, wa) + _mm_nt(dpib, wi)
            d_shift[:tc, :] = dxc
            d_shift[tc:, :] = dxc_next
            dul_ref[rows, :] = (cw[3] * dxc + cw[2] * d_shift[1:1 + tc, :]
                                + cw[1] * d_shift[2:2 + tc, :] + cw[0] * d_shift[3:3 + tc, :]).astype(BF16)
            for k in range(CONV_WIDTH):
                dcw_ref[k:k + 1, :] += jnp.sum(dxc * taps[CONV_WIDTH - 1 - k], axis=0, keepdims=True)
            dcb_ref[...] += jnp.sum(dxc, axis=0, keepdims=True)
            db = (_window_mean(_causal_window(ext_p, deeper), t0, row, window, at_start) - ext_p[HALO:]).astype(BF16)
            dyp = dyp_ref[rows, :]
            dps_ref[...] += jnp.sum(dyp * _mm(db, pw), axis=0, keepdims=True)
            dys = (dyp * ps).astype(BF16)
            dpw_ref[0] += _mm_tn(db, dys)
            dd = _mm_nt(dys, pw)
            ddn = _window_mean(dd, t0, row, window, at_start)
            ext_q = jnp.concatenate([ddn, ddn_next], axis=0)
            dup_ref[rows, :] = (_anticausal_window(ext_q, deeper, tc) - dd).astype(BF16)
            return l[0:1, :], a[0:1, :], dxc[0:8, :], ddn[0:HALO, :]

        def step(k, carry):
            c = n_chunks - 1 - k
            t0 = pl.multiple_of(c * tc, tc)
            ext = pl.ds(pl.multiple_of(c * tc - HALO, HALO), tc + HALO)
            return chunk(t0, up_ref[ext, :], ul_ref[ext, :], h_ref[ext, :], carry)

        carry = (jnp.zeros((1, SLAB), F32), jnp.zeros((1, SLAB), F32),
                 jnp.zeros((8, SLAB), F32), jnp.zeros((HALO, SLAB), F32))
        carry = lax.fori_loop(0, n_chunks - 1, step, carry)
        pad = jnp.zeros((HALO, SLAB), F32)
        first = pl.ds(0, tc)
        chunk(0, jnp.concatenate([pad, up_ref[first, :]], axis=0), jnp.concatenate([pad, ul_ref[first, :]], axis=0),
              jnp.concatenate([pad, h_ref[first, :]], axis=0), carry, at_start=True)
        dlam_ref[...] = dlam_ref[...] * (LRU_C * jax.nn.sigmoid(-lam_v))

    seq, mat, vec, taps = _slab_specs(s)
    grad = _slab_major(s, BF16)
    mats = jax.ShapeDtypeStruct((N_SLAB, SLAB, SLAB), F32)
    vecs = jax.ShapeDtypeStruct((1, 512), F32)
    return pl.pallas_call(
        body, name="mixer_bwd", grid=(N_SLAB,),
        in_specs=[seq] * 8 + [mat, vec, taps, vec, mat, vec, mat, vec, vec, pl.BlockSpec(memory_space=pl.ANY)],
        out_specs=[seq, seq, seq, mat, vec, taps, vec, mat, vec, mat, vec, vec],
        out_shape=[grad, grad, grad, mats, vecs, jax.ShapeDtypeStruct((CONV_WIDTH, 512), F32), vecs,
                   mats, vecs, mats, vecs, vecs],
        scratch_shapes=[pltpu.VMEM((tc + SUBLANES, SLAB), F32), pltpu.VMEM((tc + SUBLANES, SLAB), F32),
                        pltpu.VMEM((tc + HALO, SLAB), F32), pltpu.VMEM((tc + HALO, SLAB), F32),
                        pltpu.VMEM((tc + SUBLANES, SLAB), F32), pltpu.VMEM((tc + SUBLANES, SLAB), F32)],
        compiler_params=_params(56, 1),
    )(*_in_hbm(u_pool, u_lru, u_gate, h, decay, root, dy_pool, dy_lru, pool_w, pool_scale, conv_w, conv_b, wa_bd,
               b_a, wi_bd, b_i, lam), after)


def _bwd_in(x, dh1, du_pool, du_lru, du_gate, ln1_g, w_in_t, after):
    s = x.shape[0]
    tb = _token_block(s)

    def body(x_ref, dh1_ref, dup_ref, dul_ref, dug_ref, g_ref, w_ref, _after, dx_ref, dwb_ref, dg_ref, dw_ref):
        @pl.when(pl.program_id(0) == 0)
        def _():
            dw_ref[...] = jnp.zeros_like(dw_ref)
            dg_ref[...] = jnp.zeros_like(dg_ref)

        g1 = g_ref[...]
        n, xhat, rstd = _rms(x_ref[...], g1)
        nb = n.astype(BF16)
        db = jnp.concatenate([_read_slabs(ref) for ref in (dup_ref, dul_ref, dug_ref)], axis=1)
        dw_ref[...] += _mm_tn(db, nb)
        dx, dg1 = _rms_bwd(_mm(db, w_ref[...]), xhat, rstd, g1)
        dx_ref[...] = dh1_ref[...] + dx
        dg_ref[...] += dg1

        @pl.when(pl.program_id(0) == s // tb - 1)
        def _():
            dwb_ref[...] = dw_ref[...].astype(BF16)

    row = pl.BlockSpec((tb, D_MODEL), lambda i: (i, 0))
    half = _slabs_spec(tb)
    vec = pl.BlockSpec((1, D_MODEL), lambda i: (0, 0))
    mat = pl.BlockSpec((IN_WIDTH, D_MODEL), lambda i: (0, 0))
    return pl.pallas_call(
        body, name="bwd_in", grid=(s // tb,),
        in_specs=[row, row, half, half, half, vec, mat, pl.BlockSpec(memory_space=pl.ANY)],
        out_specs=[row, mat, vec],
        out_shape=[pltpu.HBM((s, D_MODEL), F32), pltpu.HBM((IN_WIDTH, D_MODEL), BF16),
                   jax.ShapeDtypeStruct((1, D_MODEL), F32)],
        scratch_shapes=[pltpu.VMEM((IN_WIDTH, D_MODEL), F32)],
        compiler_params=_params(48, 1),
    )(*_in_hbm(x, dh1, du_pool, du_lru, du_gate, ln1_g, w_in_t), after)


def _mesh_position():
    x, y, c = lax.axis_index("x"), lax.axis_index("y"), lax.axis_index("c")
    return x, y, c, 4 * x + 2 * y + c


def _peer(x, y, c, p):
    px = 1 - x if p & 4 else x
    py = 1 - y if p & 2 else y
    pc = 1 - c if p & 1 else c
    return (px, py, pc), 4 * px + 2 * py + pc


HBM_SPEC = pl.BlockSpec(memory_space=pltpu.HBM)
SEM_SPEC = pl.BlockSpec(memory_space=pltpu.SEMAPHORE)
DATAFLOW = pltpu.SideEffectType.DATAFLOW_SIDE_EFFECTING


class Copy(NamedTuple):
    src: int
    src_at: Any
    dst: int
    dst_at: Any
    peer: int
    group: int
    slot: int


SIBLING = (1,)
SAME_CORE = (2, 4, 6)
EVERYONE = tuple(range(1, N_DEV))
MYSELF = (0,)


def _same(index):
    return index


def _chip(index):
    return jnp.right_shift(index, 1)


def _fan_out(srcs, lands, peers, group):
    return [Copy(s, None, d, _same, p, group, N_DEV * i + p) for i, (s, d) in enumerate(zip(srcs, lands)) for p in peers]


def _scatter(stacks, lands, peers):
    return [Copy(s, lambda me, p=p: jnp.bitwise_xor(me, p), d, _same, p, 0, 0)
            for s, d in zip(stacks, lands) for p in peers]


def _numbered(copies, group=0):
    return [cp._replace(group=group, slot=i) for i, cp in enumerate(copies)]


def _relay(lands, peers):
    return [Copy(b, lambda s, q=q: jnp.bitwise_xor(s, q), b, lambda s, q=q: jnp.bitwise_xor(s, q), 1, 0, N_DEV * i + q)
            for i, b in enumerate(lands) for q in peers]


def _to_sibling(stacks, lands):
    return [Copy(s, lambda me, k=k: 2 * k + 1 - jnp.bitwise_and(me, 1), d, lambda me, k=k: k, 1, 0, 4 * i + k)
            for i, (s, d) in enumerate(zip(stacks, lands)) for k in range(N_DEV // 2)]


def _to_chips(sums, lands):
    return [Copy(s, lambda me, p=p: jnp.bitwise_xor(_chip(me), p // 2), d, _chip, p, 0, 4 * i + p // 2)
            for i, (s, d) in enumerate(zip(sums, lands)) for p in SAME_CORE]


def _comm_call(name, bufs, wait=None, start=None, after=()):
    nb = len(bufs)
    waits = [wait] if isinstance(wait, tuple) else list(wait or [])
    slots = list(start[1]) if start else []
    n_out_sem = 2 * len(slots)
    after = [a for a in (after if isinstance(after, (list, tuple)) else [after]) if a is not None]

    def body(*refs):
        b = refs[:nb]
        at = nb + 2 * len(waits) + len(after)
        out_sems = refs[at:at + n_out_sem]
        token = refs[at + n_out_sem + nb]
        x, y, c, me = _mesh_position()

        def part(i, row_of, sender):
            return b[i] if row_of is None else b[i].at[row_of(sender)]

        for k, (_, copies) in enumerate(waits):
            w_send, w_recv = refs[nb + 2 * k], refs[nb + 2 * k + 1]
            for cp in copies:
                peer, peer_index = _peer(x, y, c, cp.peer)
                arrival = pltpu.make_async_remote_copy(part(cp.src, cp.src_at, me), part(cp.dst, cp.dst_at, peer_index),
                                                       w_send.at[cp.slot], w_recv.at[cp.slot],
                                                       device_id=peer, device_id_type=MESH)
                arrival.wait_send()
                arrival.wait_recv()
        if start:
            for cp in start[0]:
                peer, _ = _peer(x, y, c, cp.peer)
                pltpu.make_async_remote_copy(part(cp.src, cp.src_at, me), part(cp.dst, cp.dst_at, me),
                                             out_sems[2 * cp.group].at[cp.slot], out_sems[2 * cp.group + 1].at[cp.slot],
                                             device_id=peer, device_id_type=MESH).start()
        token[...] = jnp.zeros_like(token)

    sem_shapes = []
    for n_slots in slots:
        sem_shapes += [pltpu.SemaphoreType.DMA((n_slots,))] * 2
    operands = [pltpu.with_memory_space_constraint(a, pltpu.HBM) for a in bufs]
    in_specs = [HBM_SPEC] * nb
    for sems, _ in waits:
        operands += list(sems)
        in_specs += [SEM_SPEC, SEM_SPEC]
    operands += after
    in_specs += [pl.BlockSpec(memory_space=pl.ANY)] * len(after)
    outs = pl.pallas_call(
        body, name=name, in_specs=in_specs,
        out_specs=[SEM_SPEC] * n_out_sem + [HBM_SPEC] * nb + [pl.BlockSpec(memory_space=pltpu.VMEM)],
        out_shape=sem_shapes + [pltpu.HBM(a.shape, a.dtype) for a in bufs] + [jax.ShapeDtypeStruct((8, SLAB), F32)],
        input_output_aliases={i: n_out_sem + i for i in range(nb)},
        compiler_params=pltpu.CompilerParams(has_side_effects=DATAFLOW),
    )(*operands)
    sems = [(outs[2 * k], outs[2 * k + 1]) for k in range(len(slots))]
    return sems, list(outs[n_out_sem:n_out_sem + nb]), outs[-1]


def _pair_sum(stacks, lands, place):
    n = len(stacks)

    def body(place_ref, *refs):
        k = pl.program_id(0)
        for m in range(n):
            mine, theirs, out, land = refs[m], refs[n + m], refs[2 * n + m], refs[3 * n + m]
            total = (mine[0, 0].astype(F32) + theirs[0].astype(F32)).astype(out.dtype)
            out[0] = total

            @pl.when(k == place_ref[1])
            def _():
                land[0] = total

    in_specs = [pl.BlockSpec((1, 1) + a.shape[1:], lambda k, place_ref: (k, place_ref[0], 0, 0)) for a in stacks]
    in_specs += [pl.BlockSpec((1,) + a.shape[1:], lambda k, place_ref: (k, 0, 0)) for a in lands]
    out_specs = [pl.BlockSpec((1,) + a.shape[1:], lambda k, place_ref: (k, 0, 0)) for a in lands]
    out_specs += [pl.BlockSpec((1,) + a.shape[1:], lambda k, place_ref: (place_ref[1], 0, 0)) for a in lands]
    outs = pl.pallas_call(
        body, name="pair_sum_" + "_".join(str(a.shape[1]) for a in stacks),
        grid_spec=pltpu.PrefetchScalarGridSpec(num_scalar_prefetch=1, grid=(N_DEV // 2,), in_specs=in_specs,
                                               out_specs=out_specs),
        out_shape=[pltpu.HBM(a.shape, a.dtype) for a in lands] * 2,
        compiler_params=_params(40, 1),
    )(place, *_in_hbm(*[a.reshape((N_DEV // 2, 2) + a.shape[1:]) for a in stacks], *lands))
    return list(outs[:n]), list(outs[n:])


def _reduce_adam(parts, w, m, v, name):
    rows, cols = w.shape
    n_parts = parts.shape[0]
    rb = rows
    for cand in (256, 176, 128):
        if rows % cand == 0 and rows > cand:
            rb = cand
            break

    def body(p_ref, w_ref, m_ref, v_ref, g_out, d_out, m_out, v_out):
        g = p_ref[0].astype(F32)
        for j in range(1, n_parts):
            g = g + p_ref[j].astype(F32)
        g_out[...] = g
        d_out[...], m_out[...], v_out[...] = _adam(g, w_ref[...], m_ref[...], v_ref[...])

    blk = pl.BlockSpec((rb, cols), lambda i: (i, 0))
    out = jax.ShapeDtypeStruct((rows, cols), F32)
    return pl.pallas_call(
        body, name=name, grid=(rows // rb,),
        in_specs=[pl.BlockSpec((n_parts, rb, cols), lambda i: (0, i, 0)), blk, blk, blk],
        out_specs=[blk] * 4, out_shape=[out] * 4,
        compiler_params=_params(40, 1),
    )(*_in_hbm(parts, w, m, v))


def _cols_from_stack(stack):
    n, r, c = stack.shape
    return jnp.transpose(stack, (1, 0, 2)).reshape(r, n * c)


def _block_diag(w):
    z = jnp.zeros((N_SLAB, 64, 64), w.dtype)
    pairs = w.reshape(N_SLAB, 2, 64, 64)
    top = jnp.concatenate([pairs[:, 0], z], axis=2)
    bottom = jnp.concatenate([z, pairs[:, 1]], axis=2)
    return jnp.concatenate([top, bottom], axis=1)


def _adam(g, w, m, v):
    m_new = ADAM_B1 * m + (1.0 - ADAM_B1) * g
    v_new = ADAM_B2 * v + (1.0 - ADAM_B2) * (g * g)
    m_hat = m_new / (1.0 - ADAM_B1 ** ADAM_STEP)
    v_hat = v_new / (1.0 - ADAM_B2 ** ADAM_STEP)
    return (-ADAM_LR) * (m_hat / (jnp.sqrt(v_hat) + ADAM_EPS) + ADAM_WD * w), m_new, v_new


WIDE = ("ln2_g", "lnf_g")
HALF = ("pool_scale", "conv_b", "b_a", "b_i", "lam", "gn_pool_g", "gn_lru_g")
VECTORS = [(k, D_MODEL) for k in WIDE] + [(k, 512) for k in HALF]
VECTOR_ROWS = sum(width // SLAB for _, width in VECTORS)
LOSS_ROW = -(-VECTOR_ROWS // 8) * 8
CONV_AT = LOSS_ROW + 8
CONV_LANES = LRU_WIDTH // SLAB
PACK_F_ROWS = CONV_AT + CONV_WIDTH * CONV_LANES
MATRIX_ROWS = N_SLAB * SLAB
HEAD = SLAB // 2
GATE_ROWS = N_SLAB * HEAD
PACK_B_ROWS = MATRIX_ROWS + 2 * GATE_ROWS


def _pack_small(vectors, pool_g, wa_g, wi_g, conv, sq):
    n_vec = len(vectors)

    def body(*refs):
        vec = refs[:n_vec]
        pw_ref, wa_ref, wi_ref, cw_ref, sq_ref, out, out_b = refs[n_vec:]
        out[...] = jnp.zeros_like(out)
        row = 0
        for ref, (_, width) in zip(vec, VECTORS):
            for k in range(width // SLAB):
                out[row:row + 1, :] = ref[:, k * SLAB:(k + 1) * SLAB]
                row += 1
        for tap in range(CONV_WIDTH):
            for k in range(CONV_LANES):
                at = CONV_AT + tap * CONV_LANES + k
                out[at:at + 1, :] = cw_ref[tap:tap + 1, k * SLAB:(k + 1) * SLAB]
        total = sq_ref[:, 0:SLAB]
        for k in range(1, D_MODEL // SLAB):
            total = total + sq_ref[:, k * SLAB:(k + 1) * SLAB]
        out[LOSS_ROW:LOSS_ROW + 1, :] = total
        left = lax.broadcasted_iota(jnp.int32, (HEAD, SLAB), 1) < HEAD
        for s in range(N_SLAB):
            out_b[s * SLAB:(s + 1) * SLAB, :] = pw_ref[s].astype(BF16)
            for i, ref in enumerate((wa_ref, wi_ref)):
                at = MATRIX_ROWS + i * GATE_ROWS + s * HEAD
                out_b[at:at + HEAD, :] = jnp.where(left, ref[s, 0:HEAD, :], ref[s, HEAD:SLAB, :]).astype(BF16)

    return pl.pallas_call(
        body, name="pack_small",
        out_shape=[jax.ShapeDtypeStruct((PACK_F_ROWS, SLAB), F32), jax.ShapeDtypeStruct((PACK_B_ROWS, SLAB), BF16)],
    )(*vectors, pool_g, wa_g, wi_g, conv, sq)


def _small_reduce_adam(my_index, parts, parts_b, vec_w, vec_m, vec_v, pool_wmv, wa_wmv, wi_wmv, conv_wmv):
    n_vec = len(VECTORS)
    n_parts = parts.shape[0]

    def body(*refs):
        me_ref, p_ref, pb_ref = refs[0], refs[1], refs[2]
        refs = refs[1:]
        w_refs, m_refs, v_refs = (refs[2 + k * n_vec:2 + (k + 1) * n_vec] for k in range(3))
        at = 2 + 3 * n_vec
        pw_w, pw_m, pw_v = refs[at:at + 3]
        gates = (refs[at + 3:at + 6], refs[at + 6:at + 9])
        cv_w, cv_m, cv_v = refs[at + 9:at + 12]
        outs = refs[at + 12:-1]
        total = refs[-1]
        total[...] = p_ref[0]
        for j in range(1, n_parts):
            total[...] += p_ref[j]
        row = 0
        for i, (_, width) in enumerate(VECTORS):
            n_rows = width // SLAB
            g = jnp.concatenate([total[row + k:row + k + 1, :] for k in range(n_rows)], axis=1)
            row += n_rows
            d, m_new, v_new = _adam(g, w_refs[i][...], m_refs[i][...], v_refs[i][...])
            for ref, val in zip(outs[4 * i:4 * i + 4], (g, d, m_new, v_new)):
                ref[...] = val
        tail = outs[4 * n_vec:]

        def summed(first, count):
            g = pb_ref[0, first:first + count, :].astype(F32)
            for j in range(1, n_parts):
                g = g + pb_ref[j, first:first + count, :].astype(F32)
            return g

        g = summed(0, MATRIX_ROWS)
        d, m_new, v_new = _adam(g, pw_w[...], pw_m[...], pw_v[...])
        for ref, val in zip(tail[0:4], (g, d, m_new, v_new)):
            ref[...] = val
        for i, (w, m, v) in enumerate(gates):
            g = summed(MATRIX_ROWS + i * GATE_ROWS, GATE_ROWS)
            for s in range(N_SLAB):
                for hd in range(2):
                    block = g[s * HEAD:(s + 1) * HEAD, hd * HEAD:(hd + 1) * HEAD]
                    rows = slice((2 * s + hd) * HEAD, (2 * s + hd + 1) * HEAD)
                    d, m_new, v_new = _adam(block, w[rows, :], m[rows, :], v[rows, :])
                    for ref, val in zip(tail[4 + 4 * i:8 + 4 * i], (block, d, m_new, v_new)):
                        ref[rows, :] = val
        me = me_ref[0, 0]
        taps = []
        for tap in range(CONV_WIDTH):
            row = total[pl.ds(CONV_AT + tap * CONV_LANES + me // 2, 1), :]
            taps.append(jnp.where(me % 2 == 0, row[:, :HEAD], row[:, HEAD:]))
        g = jnp.concatenate(taps, axis=0)
        d, m_new, v_new = _adam(g, cv_w[...], cv_m[...], cv_v[...])
        for ref, val in zip(tail[12:16], (g, d, m_new, v_new)):
            ref[...] = val
        tail[16][...] = (0.5 / D_MODEL) * jnp.sum(total[LOSS_ROW:LOSS_ROW + 1, :], axis=1, keepdims=True)

    out_shape = []
    for _, width in VECTORS:
        out_shape += [jax.ShapeDtypeStruct((1, width), F32)] * 4
    out_shape += [jax.ShapeDtypeStruct((MATRIX_ROWS, SLAB), F32)] * 4
    out_shape += [jax.ShapeDtypeStruct((2 * GATE_ROWS, HEAD), F32)] * 8
    out_shape += [jax.ShapeDtypeStruct(conv_wmv[0].shape, F32)] * 4 + [jax.ShapeDtypeStruct((1, 1), F32)]
    operands = [parts, parts_b, *vec_w, *vec_m, *vec_v, *pool_wmv, *wa_wmv, *wi_wmv, *conv_wmv]

    def whole(a):
        return pl.BlockSpec(a.shape, lambda i, n=len(a.shape): (0,) * n)

    outs = pl.pallas_call(
        body, name="adam_small", grid=(1,),
        in_specs=[pl.BlockSpec(memory_space=pltpu.SMEM)] + [whole(a) for a in operands],
        out_specs=[whole(a) for a in out_shape], out_shape=out_shape,
        scratch_shapes=[pltpu.VMEM((PACK_F_ROWS, SLAB), F32)],
        compiler_params=_params(40, 1),
    )(my_index, *_in_hbm(*operands))
    vec_out = [tuple(outs[4 * i:4 * i + 4]) for i in range(n_vec)]
    tail = outs[4 * n_vec:]
    return vec_out, tuple(tail[0:4]), tuple(tail[4:8]), tuple(tail[8:12]), tuple(tail[12:16]), tail[16]


def kernel(x, ln1_g, w_in, pool_w, pool_scale, conv_w, conv_b, w_a, b_a, w_i, b_i, lam, gn_pool_g, gn_lru_g, w_out, ln2_g, w_ffn_gate, w_ffn_up, w_ffn_down, lnf_g, loss_target, m_ln1_g, m_w_in, m_pool_w, m_pool_scale, m_conv_w, m_conv_b, m_w_a, m_b_a, m_w_i, m_b_i, m_lam, m_gn_pool_g, m_gn_lru_g, m_w_out, m_ln2_g, m_w_ffn_gate, m_w_ffn_up, m_w_ffn_down, m_lnf_g, v_ln1_g, v_w_in, v_pool_w, v_pool_scale, v_conv_w, v_conv_b, v_w_a, v_b_a, v_w_i, v_b_i, v_lam, v_gn_pool_g, v_gn_lru_g, v_w_out, v_ln2_g, v_w_ffn_gate, v_w_ffn_up, v_w_ffn_down, v_lnf_g):
    weights = dict(ln1_g=ln1_g, w_in=w_in, pool_w=pool_w, pool_scale=pool_scale, conv_w=conv_w, conv_b=conv_b,
                   w_a=w_a, b_a=b_a, w_i=w_i, b_i=b_i, lam=lam, gn_pool_g=gn_pool_g, gn_lru_g=gn_lru_g,
                   w_out=w_out, ln2_g=ln2_g, w_ffn_gate=w_ffn_gate, w_ffn_up=w_ffn_up, w_ffn_down=w_ffn_down,
                   lnf_g=lnf_g)
    mom1 = dict(ln1_g=m_ln1_g, w_in=m_w_in, pool_w=m_pool_w, pool_scale=m_pool_scale, conv_w=m_conv_w,
                conv_b=m_conv_b, w_a=m_w_a, b_a=m_b_a, w_i=m_w_i, b_i=m_b_i, lam=m_lam, gn_pool_g=m_gn_pool_g,
                gn_lru_g=m_gn_lru_g, w_out=m_w_out, ln2_g=m_ln2_g, w_ffn_gate=m_w_ffn_gate,
                w_ffn_up=m_w_ffn_up, w_ffn_down=m_w_ffn_down, lnf_g=m_lnf_g)
    mom2 = dict(ln1_g=v_ln1_g, w_in=v_w_in, pool_w=v_pool_w, pool_scale=v_pool_scale, conv_w=v_conv_w,
                conv_b=v_conv_b, w_a=v_w_a, b_a=v_b_a, w_i=v_w_i, b_i=v_b_i, lam=v_lam, gn_pool_g=v_gn_pool_g,
                gn_lru_g=v_gn_lru_g, w_out=v_w_out, ln2_g=v_ln2_g, w_ffn_gate=v_w_ffn_gate,
                w_ffn_up=v_w_ffn_up, w_ffn_down=v_w_ffn_down, lnf_g=v_lnf_g)

    xs = x[0]
    target = loss_target[0]

    shard = dict(w_in=lambda a: a[0].T, w_ffn_gate=lambda a: a[0].T, w_ffn_up=lambda a: a[0].T,
                 w_out=lambda a: a[0], w_ffn_down=lambda a: a[0], conv_w=lambda a: a[0])
    unshard = dict(w_in=lambda a: a.T[None], w_ffn_gate=lambda a: a.T[None], w_ffn_up=lambda a: a.T[None],
                   w_out=lambda a: a[None], w_ffn_down=lambda a: a[None], conv_w=lambda a: a[None])

    gathered = ("w_in", "conv_w", "w_out", "w_ffn_gate", "w_ffn_up", "w_ffn_down")
    groups = ((0, 1), (2,), (3, 4), (5,))
    sources = [shard[k](weights[k]) if k == "conv_w" else shard[k](weights[k]).astype(BF16) for k in gathered]
    my_index = 4 * lax.axis_index("x") + 2 * lax.axis_index("y") + lax.axis_index("c")
    lands = [lax.empty((N_DEV,) + a.shape, a.dtype) for a in sources]

    def first_hop(n, at=0):
        return _numbered(_fan_out(range(at, at + n), range(at + n, at + 2 * n), SAME_CORE + SIBLING + MYSELF, 0))

    def second_hop(n, at=0):
        return _numbered(_relay(range(at, at + n), SAME_CORE))

    start = []
    for g, members in enumerate(groups):
        start += _numbered(_fan_out(members, [6 + m for m in members], SAME_CORE + SIBLING + MYSELF, 0), g)
    sems, bufs, _ = _comm_call("gather_start", sources + lands,
                               start=(start, [sum(cp.group == g for cp in start) for g in range(len(groups))]))
    sources, lands = bufs[:6], bufs[6:]

    def group_buffers(g):
        return [sources[m] for m in groups[g]] + [lands[m] for m in groups[g]]

    def relayed(tag, g, after):
        n = len(groups[g])
        relay = second_hop(n, n)
        relay_sems, bufs, _ = _comm_call("gather_relay_" + tag, group_buffers(g), wait=(sems[g], first_hop(n)),
                                         start=(relay, (len(relay),)), after=after)
        return relay_sems[0], bufs[n:]

    relay_sem, bufs = relayed("in", 0, None)
    g_in, g_conv = _comm_call("gather_wait_in", bufs, wait=(relay_sem, second_hop(2)))[1]
    w_in_f = g_in.reshape(IN_WIDTH, D_MODEL)
    conv_w_f = _cols_from_stack(g_conv)

    wa_bd = _block_diag(w_a[0])
    wi_bd = _block_diag(w_i[0])
    lnf_row = lnf_g.reshape(1, D_MODEL)

    u_pool, u_lru, u_gate = _fwd_in(xs, ln1_g, w_in_f)
    y_pool, h, y_lru, decay, root = _mixer_fwd(u_pool, u_lru, u_gate, pool_w[0], pool_scale, conv_w_f, conv_b,
                                               wa_bd, b_a, wi_bd, b_i, lam)
    relay_sem, bufs = relayed("out", 1, y_pool)
    gate_up = second_hop(2, 3)
    (gate_up_sem,), bufs, _ = _comm_call("gather_wait_out", bufs + group_buffers(2),
                                         wait=[(relay_sem, second_hop(1)), (sems[2], first_hop(2, 1))],
                                         start=(gate_up, (len(gate_up),)))
    g_out, gate_up_lands = bufs[0], bufs[3:]
    w_out_f = g_out.reshape(D_MODEL, D_MODEL)
    h1, n2 = _fwd_out(xs, y_pool, y_lru, gn_pool_g, gn_lru_g, w_out_f, ln2_g)
    relay_sem, bufs = relayed("down", 3, n2)
    g_gate, g_up, g_down = _comm_call("gather_wait_ffn", gate_up_lands + bufs,
                                      wait=[(gate_up_sem, second_hop(2)), (relay_sem, second_hop(1, 2))])[1]
    w_gate_f = g_gate.reshape(D_FF, D_MODEL)
    w_up_f = g_up.reshape(D_FF, D_MODEL)
    w_down_f = g_down.reshape(D_FF, D_MODEL)
    g_act, u_act, dh2, dh2b, d_lnf, sq = _ffn_fwd(h1, n2, target, lnf_row, w_gate_f, w_up_f, w_down_f)

    def direct(tag, stacks, wholes, after):
        sources = list(stacks) + list(wholes)
        n, n_st = len(sources), len(stacks)
        lands = [lax.empty(a.shape if i < n_st else (N_DEV,) + a.shape, a.dtype) for i, a in enumerate(sources)]
        copies = _numbered(_scatter(range(n_st), range(n, n + n_st), EVERYONE + MYSELF)
                           + _fan_out(range(n_st, n), range(n + n_st, 2 * n), EVERYONE + MYSELF, 0))
        sem, bufs, token = _comm_call(tag + "_start", sources + lands, start=(copies, (len(copies),)), after=after)
        return (tag, sem[0], bufs, copies), token

    def direct_finish(started, after):
        tag, sem, bufs, copies = started
        _, bufs, _ = _comm_call(tag + "_wait", bufs, wait=(sem, copies), after=after)
        return bufs[len(bufs) // 2:]

    place = jnp.stack([lax.axis_index("c"), 2 * lax.axis_index("x") + lax.axis_index("y")]).astype(jnp.int32)

    d_gate, d_up, d_down, dn2 = _ffn_bwd(n2, dh2b, g_act, u_act, w_gate_f, w_up_f, w_down_f)
    down_stack, *ffn_stacks = [d.reshape(N_DEV, D_FF // N_DEV, D_MODEL) for d in (d_down, d_gate, d_up)]
    pair_lands = [lax.empty((N_DEV // 2,) + a.shape[1:], a.dtype) for a in ffn_stacks]
    pair_copies = _numbered(_to_sibling(range(2), range(2, 4)))
    down_copies = _numbered(_scatter([4], [5], EVERYONE + MYSELF), 1)
    (sem, down_sem), bufs, token = _comm_call(
        "ffn_grads_start", ffn_stacks + pair_lands + [down_stack, lax.empty(down_stack.shape, down_stack.dtype)],
        start=(pair_copies + down_copies, (len(pair_copies), len(down_copies))))
    bufs, down_bufs = bufs[:4], bufs[4:]
    dh1, dy_pool, dy_lru, d_out, d_ln2, d_gnp, d_gnl = _bwd_out(dn2, dh2, h1, y_pool, y_lru, gn_pool_g, gn_lru_g,
                                                                 w_out_f, ln2_g, token)
    _, bufs, _ = _comm_call("ffn_pair_wait", bufs, wait=(sem, pair_copies), after=dh1)
    ffn_sums, ffn_lands = _pair_sum(bufs[:2], bufs[2:], place)
    ffn_copies = _numbered(_to_chips(range(2), range(2, 4)))
    ffn_sem, ffn_bufs, token = _comm_call("ffn_chip_start", ffn_sums + ffn_lands, start=(ffn_copies, (len(ffn_copies),)))
    (du_pool, du_lru, du_gate, d_pw, d_ps, d_cw, d_cb, d_wa, d_ba, d_wi, d_bi, d_lam) = _mixer_bwd(
        u_pool, u_lru, u_gate, h, decay, root, dy_pool, dy_lru, pool_w[0], pool_scale, conv_w_f, conv_b,
        wa_bd, b_a, wi_bd, b_i, lam, token)

    vec_grads = dict(ln2_g=d_ln2, lnf_g=d_lnf, pool_scale=d_ps, conv_b=d_cb, b_a=d_ba, b_i=d_bi,
                     lam=d_lam, gn_pool_g=d_gnp, gn_lru_g=d_gnl)
    packed, packed_b = _pack_small([vec_grads[k] for k, _ in VECTORS], d_pw, d_wa, d_wi, d_cw, sq)
    small_started, token = direct("small", [d_out.reshape(N_DEV, D_MODEL // N_DEV, D_MODEL)], [packed, packed_b], None)
    grad_x, d_in, d_ln1 = _bwd_in(xs, dh1, du_pool, du_lru, du_gate, ln1_g, w_in_f, token)
    in_started, token = direct("in", [d_in.reshape(N_DEV, IN_WIDTH // N_DEV, D_MODEL)], [d_ln1], None)

    results = {}

    def reduce_adam(name, parts):
        outs = _reduce_adam(parts, shard[name](weights[name]), shard[name](mom1[name]), shard[name](mom2[name]),
                            "adam_" + name)
        results[name] = tuple(unshard[name](o) for o in outs)
        return outs[0]

    shard["ln1_g"] = unshard["ln1_g"] = lambda a: a
    _, ffn_bufs, _ = _comm_call("ffn_chip_wait", ffn_bufs, wait=(ffn_sem[0], ffn_copies), after=token)
    done = [reduce_adam(name, parts) for name, parts in zip(("w_ffn_gate", "w_ffn_up"), ffn_bufs[2:])]
    _, (_, r_down), _ = _comm_call("down_wait", down_bufs, after=done,
                                   wait=(down_sem, _numbered(_scatter([0], [1], EVERYONE + MYSELF))))
    done.append(reduce_adam("w_ffn_down", r_down))
    r_out, r_small, r_small_b = direct_finish(small_started, done)
    done.append(reduce_adam("w_out", r_out))

    def as_row(a, width):
        return a.reshape(1, width)

    def as_matrix(a):
        return a.reshape(MATRIX_ROWS, SLAB)

    def as_heads(a):
        return a.reshape(2 * GATE_ROWS, HEAD)

    states = (weights, mom1, mom2)
    vec_out, pool_out, wa_out, wi_out, conv_out, loss_11 = _small_reduce_adam(
        my_index.astype(jnp.int32).reshape(1, 1), r_small, r_small_b,
        *[[as_row(t[k], w) for k, w in VECTORS] for t in states], [as_matrix(t["pool_w"]) for t in states],
        [as_heads(t["w_a"]) for t in states], [as_heads(t["w_i"]) for t in states], [t["conv_w"][0] for t in states])
    for (k, _), outs in zip(VECTORS, vec_out):
        results[k] = tuple(o.reshape(weights[k].shape) for o in outs)
    for k, outs in (("pool_w", pool_out), ("w_a", wa_out), ("w_i", wi_out), ("conv_w", conv_out)):
        results[k] = tuple(o.reshape(weights[k].shape) for o in outs)
    loss = loss_11[0, 0]
    r_in, r_ln1 = direct_finish(in_started, done + [loss_11])
    reduce_adam("w_in", r_in)
    reduce_adam("ln1_g", r_ln1)

    order = ["ln1_g", "w_in", "pool_w", "pool_scale", "conv_w", "conv_b", "w_a", "b_a", "w_i", "b_i", "lam",
             "gn_pool_g", "gn_lru_g", "w_out", "ln2_g", "w_ffn_gate", "w_ffn_up", "w_ffn_down", "lnf_g"]
    return (loss, grad_x[None],
            *[results[k][0] for k in order], *[results[k][1] for k in order],
            *[results[k][2] for k in order], *[results[k][3] for k in order])
```

```python
from typing import Any, NamedTuple

import jax
import jax.numpy as jnp
from jax import lax
from jax.experimental import pallas as pl
from jax.experimental.pallas import tpu as pltpu

F32 = jnp.float32
BF16 = jnp.bfloat16

N_DEV = 8
D_MODEL = 1024
POOL_WIDTH = 512
LRU_WIDTH = 512
IN_WIDTH = 1536
D_FF = 2816
N_SLAB = 4
SLAB = 128
CONV_WIDTH = 4
LRU_C = 8.0
EPS = 1e-6
HALO = 16
FF_CHUNK = 256

ADAM_LR = 0.001
ADAM_B1 = 0.9
ADAM_B2 = 0.999
ADAM_EPS = 1e-08
ADAM_WD = 0.01
ADAM_STEP = 10

MIB = 1 << 20
MESH = pl.DeviceIdType.MESH


def _params(vmem_mib, n_axes=0):
    sem = ("arbitrary",) * n_axes if n_axes else None
    return pltpu.CompilerParams(dimension_semantics=sem, vmem_limit_bytes=vmem_mib * MIB)


def _in_hbm(*arrays):
    return [pltpu.with_memory_space_constraint(a, pltpu.HBM) for a in arrays]


def _mm(a, b):
    return jnp.dot(a, b, preferred_element_type=F32)


def _mm_nt(a, b):
    return lax.dot_general(a, b, (((1,), (1,)), ((), ())), preferred_element_type=F32)


def _mm_tn(a, b):
    return lax.dot_general(a, b, (((0,), (0,)), ((), ())), preferred_element_type=F32)


def _rms(x, g):
    rstd = lax.rsqrt(jnp.mean(x * x, axis=-1, keepdims=True) + EPS)
    xhat = x * rstd
    return xhat * g, xhat, rstd


def _rms_bwd(dy, xhat, rstd, g):
    gy = dy * g
    dx = rstd * (gy - xhat * jnp.mean(gy * xhat, axis=-1, keepdims=True))
    return dx, jnp.sum(dy * xhat, axis=0, keepdims=True)


def _gelu(z):
    t = jnp.tanh(0.7978845608028654 * (z + 0.044715 * z * z * z))
    return 0.5 * z * (1.0 + t), t


def _gelu_grad(z, t):
    return 0.5 * (1.0 + t) + 0.5 * z * (1.0 - t * t) * 0.7978845608028654 * (1.0 + 3.0 * 0.044715 * z * z)


def _softplus_neg(lam):
    x = -lam
    e = jnp.exp(-jnp.abs(x))
    u = 1.0 + e
    l1p = jnp.where(u == 1.0, e, jnp.log(u) * e / (u - 1.0))
    return jnp.maximum(x, 0.0) + l1p


def _one_minus_square(la, a):
    x = 2.0 * la
    series = -x * (1.0 + x * (0.5 + x * (1.0 / 6.0 + x * (1.0 / 24.0))))
    return jnp.where(x > -0.06, series, 1.0 - a * a)


def _sigmoid(x):
    return 0.5 * jnp.tanh(0.5 * x) + 0.5


def _down(v, d):
    return pltpu.roll(v, d, 0)


def _up(v, d):
    return pltpu.roll(v, v.shape[0] - d, 0)


def _slab_major(s, dtype=F32):
    return pltpu.HBM((N_SLAB, s, SLAB), dtype)


def _slabs_spec(tb):
    return pl.BlockSpec((N_SLAB, tb, SLAB), lambda i: (0, i, 0))


def _read_slabs(ref):
    return jnp.concatenate([ref[k] for k in range(N_SLAB)], axis=1)


def _write_slabs(ref, value):
    for k in range(N_SLAB):
        ref[k] = value[:, k * SLAB:(k + 1) * SLAB]


def _token_block(s, most=512):
    for rows in (most, 512, 256):
        if rows <= most and s % rows == 0 and s > rows:
            return rows
    return s


def _time_chunk(s):
    return 512 if s % 512 == 0 and s > 512 else 256 if s % 256 == 0 else s


def _fwd_in(x, ln1_g, w_in_t):
    s = x.shape[0]
    tb = _token_block(s, 1024)

    def body(x_ref, g_ref, w_ref, up_ref, ul_ref, ug_ref):
        n, _, _ = _rms(x_ref[...], g_ref[...])
        proj = _mm_nt(n.astype(BF16), w_ref[...])
        _write_slabs(up_ref, proj[:, :POOL_WIDTH])
        _write_slabs(ul_ref, proj[:, POOL_WIDTH:POOL_WIDTH + LRU_WIDTH])
        _write_slabs(ug_ref, proj[:, POOL_WIDTH + LRU_WIDTH:])

    out = _slab_major(s)
    return pl.pallas_call(
        body, name="fwd_in", grid=(s // tb,),
        in_specs=[pl.BlockSpec((tb, D_MODEL), lambda i: (i, 0)),
                  pl.BlockSpec((1, D_MODEL), lambda i: (0, 0)),
                  pl.BlockSpec((IN_WIDTH, D_MODEL), lambda i: (0, 0))],
        out_specs=[_slabs_spec(tb)] * 3,
        out_shape=[out, out, out],
        compiler_params=_params(40, 1),
    )(*_in_hbm(x, ln1_g, w_in_t))


def _pool_denominator(t0, row, window):
    return jnp.minimum((t0 + row + 1).astype(F32), window)


def _causal_window(ext, deeper):
    s = ext + _down(ext, 1)
    s = s + deeper[0] * _down(s, 2)
    s = s + deeper[1] * _down(s, 4)
    s = s + deeper[2] * _down(s, 8)
    return s[HALO:]


def _anticausal_window(ext, deeper, rows):
    s = ext + _up(ext, 1)
    s = s + deeper[0] * _up(s, 2)
    s = s + deeper[1] * _up(s, 4)
    s = s + deeper[2] * _up(s, 8)
    return s[:rows]


def _conv(taps, cw, cb):
    return cw[3] * taps[0] + cw[2] * taps[1] + cw[1] * taps[2] + cw[0] * taps[3] + cb


def _lru_gates(xc, wa, ba, wi, bi):
    xb = xc.astype(BF16)
    r = _sigmoid(_mm(xb, wa) + ba)
    i = _sigmoid(_mm(xb, wi) + bi)
    return xb, r, i


def _lru_decay(r, sp):
    la = (-LRU_C) * r * sp
    a = jnp.exp(la)
    return a, jnp.sqrt(jnp.maximum(_one_minus_square(la, a), 0.0))


SUBLANES = 8


def _scan_pads(a_pad, b_pad, rows, causal):
    spare = slice(0, SUBLANES) if causal else slice(rows, rows + SUBLANES)
    a_pad[spare, :] = jnp.ones((SUBLANES, SLAB), F32)
    b_pad[spare, :] = jnp.zeros((SUBLANES, SLAB), F32)


def _scan_causal(a, b, h_prev, a_pad, b_pad, rows):
    d = 1
    while d < min(SUBLANES, rows):
        a_pad[SUBLANES:, :] = a
        b_pad[SUBLANES:, :] = b
        b = a * b_pad[SUBLANES - d:SUBLANES - d + rows, :] + b
        a = a * a_pad[SUBLANES - d:SUBLANES - d + rows, :]
        d *= 2
    h = jnp.broadcast_to(h_prev, (d, SLAB))
    out = []
    for at in range(0, rows, d):
        h = a[at:at + d] * h + b[at:at + d]
        out.append(h)
    return jnp.concatenate(out, axis=0)


def _scan_anticausal(a, b, l_next, a_pad, b_pad, rows):
    d = 1
    while d < min(SUBLANES, rows):
        a_pad[:rows, :] = a
        b_pad[:rows, :] = b
        b = a * b_pad[d:d + rows, :] + b
        a = a * a_pad[d:d + rows, :]
        d *= 2
    lead = jnp.broadcast_to(l_next, (d, SLAB))
    out = []
    for at in range(rows - d, -1, -d):
        lead = a[at:at + d] * lead + b[at:at + d]
        out.append(lead)
    return jnp.concatenate(out[::-1], axis=0)


def _slab_scalars():
    slab = pl.program_id(0)
    deeper = [jnp.where(slab > k, 1.0, 0.0).astype(F32) for k in range(N_SLAB - 1)]
    window = jnp.left_shift(jnp.int32(2), slab).astype(F32)
    inverse = jnp.where(slab == 0, 0.5, jnp.where(slab == 1, 0.25, jnp.where(slab == 2, 0.125, 0.0625))).astype(F32)
    return deeper, (window, inverse)


def _window_mean(total, t0, row, window, at_start):
    if at_start:
        return total / _pool_denominator(t0, row, window[0])
    return total * window[1]


def _slab_specs(s):
    seq = pl.BlockSpec((None, s, SLAB), lambda k: (k, 0, 0))
    mat = pl.BlockSpec((1, SLAB, SLAB), lambda k: (k, 0, 0))
    vec = pl.BlockSpec((1, SLAB), lambda k: (0, k))
    taps = pl.BlockSpec((CONV_WIDTH, SLAB), lambda k: (0, k))
    return seq, mat, vec, taps


def _mixer_fwd(u_pool, u_lru, u_gate, pool_w, pool_scale, conv_w, conv_b, wa_bd, b_a, wi_bd, b_i, lam):
    s = u_pool.shape[1]
    tc = _time_chunk(s)
    n_chunks = s // tc

    def body(up_ref, ul_ref, ug_ref, pw_ref, ps_ref, cw_ref, cb_ref, wa_ref, ba_ref, wi_ref, bi_ref, lam_ref,
             yp_ref, h_ref, yl_ref, a_ref, mult_ref, shift_ref, a_pad, b_pad):
        _scan_pads(a_pad, b_pad, tc, causal=True)
        deeper, window = _slab_scalars()
        pw = pw_ref[0].astype(BF16)
        wa = wa_ref[0].astype(BF16)
        wi = wi_ref[0].astype(BF16)
        ps, cb, ba, bi = ps_ref[...], cb_ref[...], ba_ref[...], bi_ref[...]
        cw = [cw_ref[k:k + 1, :] for k in range(CONV_WIDTH)]
        sp = _softplus_neg(lam_ref[...])
        row = lax.broadcasted_iota(jnp.int32, (tc, SLAB), 0)

        def chunk(t0, ext_p, ext_l, h_prev, at_start=False):
            rows = pl.ds(t0, tc)
            d = _window_mean(_causal_window(ext_p, deeper), t0, row, window, at_start) - ext_p[HALO:]
            yp_ref[rows, :] = _mm(d.astype(BF16), pw) * ps
            shift_ref[...] = ext_l
            xc = _conv([shift_ref[HALO - k:HALO - k + tc, :] for k in range(CONV_WIDTH)], cw, cb)
            _, r, i = _lru_gates(xc, wa, ba, wi, bi)
            a, mult = _lru_decay(r, sp)
            a_ref[rows, :] = a
            mult_ref[rows, :] = mult
            h = _scan_causal(a, mult * (i * xc), h_prev, a_pad, b_pad, tc)
            h_ref[rows, :] = h
            yl_ref[rows, :] = h * _gelu(ug_ref[rows, :])[0]
            return h[tc - 1:tc, :]

        pad = jnp.zeros((HALO, SLAB), F32)
        h0 = chunk(0, jnp.concatenate([pad, up_ref[pl.ds(0, tc), :]], axis=0),
                   jnp.concatenate([pad, ul_ref[pl.ds(0, tc), :]], axis=0), jnp.zeros((1, SLAB), F32), at_start=True)

        def step(c, h_prev):
            t0 = pl.multiple_of(c * tc, tc)
            ext = pl.ds(pl.multiple_of(c * tc - HALO, HALO), tc + HALO)
            return chunk(t0, up_ref[ext, :], ul_ref[ext, :], h_prev)

        lax.fori_loop(1, n_chunks, step, h0)

    seq, mat, vec, taps = _slab_specs(s)
    out = _slab_major(s)
    return pl.pallas_call(
        body, name="mixer_fwd", grid=(N_SLAB,),
        in_specs=[seq, seq, seq, mat, vec, taps, vec, mat, vec, mat, vec, vec],
        out_specs=[seq] * 5, out_shape=[out] * 5,
        scratch_shapes=[pltpu.VMEM((tc + HALO, SLAB), F32), pltpu.VMEM((tc + SUBLANES, SLAB), F32),
                        pltpu.VMEM((tc + SUBLANES, SLAB), F32)],
        compiler_params=_params(48, 1),
    )(*_in_hbm(u_pool, u_lru, u_gate, pool_w, pool_scale, conv_w, conv_b, wa_bd, b_a, wi_bd, b_i, lam))


def _fwd_out(x, y_pool, y_lru, gn_pool_g, gn_lru_g, w_out, ln2_g):
    s = x.shape[0]
    tb = _token_block(s, 1024)

    def body(x_ref, yp_ref, yl_ref, gp_ref, gl_ref, w_ref, g2_ref, h1_ref, n2_ref):
        mp, _, _ = _rms(_read_slabs(yp_ref), gp_ref[...])
        ml, _, _ = _rms(_read_slabs(yl_ref), gl_ref[...])
        h1 = x_ref[...] + _mm(mp.astype(BF16), w_ref[:POOL_WIDTH, :]) + _mm(ml.astype(BF16), w_ref[POOL_WIDTH:, :])
        h1_ref[...] = h1
        n2_ref[...] = _rms(h1, g2_ref[...])[0].astype(BF16)

    row = pl.BlockSpec((tb, D_MODEL), lambda i: (i, 0))
    half = _slabs_spec(tb)
    return pl.pallas_call(
        body, name="fwd_out", grid=(s // tb,),
        in_specs=[row, half, half, pl.BlockSpec((1, 512), lambda i: (0, 0)), pl.BlockSpec((1, 512), lambda i: (0, 0)),
                  pl.BlockSpec((D_MODEL, D_MODEL), lambda i: (0, 0)), pl.BlockSpec((1, D_MODEL), lambda i: (0, 0))],
        out_specs=[row, row],
        out_shape=[pltpu.HBM((s, D_MODEL), F32), pltpu.HBM((s, D_MODEL), BF16)],
        compiler_params=_params(40, 1),
    )(*_in_hbm(x, y_pool, y_lru, gn_pool_g, gn_lru_g, w_out, ln2_g))


def _ffn_fwd(h1, n2, target, lnf_g, w_gate, w_up, w_down):
    s = h1.shape[0]
    tb = 512
    sub = 256
    n_ff = D_FF // FF_CHUNK

    def body(h1_ref, n2_ref, t_ref, gf_ref, wg_hbm, wu_hbm, wd_hbm,
             g_ref, u_ref, dh_ref, dhb_ref, dgf_ref, sq_ref, wg, wu, wd, sem):
        @pl.when(pl.program_id(0) == 0)
        def _():
            loads = [pltpu.make_async_copy(src, dst, sem.at[k])
                     for k, (src, dst) in enumerate(((wg_hbm, wg), (wu_hbm, wu), (wd_hbm, wd)))]
            for cp in loads:
                cp.start()
            for cp in loads:
                cp.wait()
            dgf_ref[...] = jnp.zeros_like(dgf_ref)
            sq_ref[...] = jnp.zeros_like(sq_ref)

        n2v = n2_ref[...]
        acc = jnp.zeros((tb, D_MODEL), F32)
        for c in range(n_ff):
            cols = slice(c * FF_CHUNK, (c + 1) * FF_CHUNK)
            g = _mm_nt(n2v, wg[cols, :])
            u = _mm_nt(n2v, wu[cols, :])
            g_ref[:, cols] = g.astype(BF16)
            u_ref[:, cols] = u.astype(BF16)
            act = g * jax.nn.sigmoid(g) * u
            acc = acc + _mm(act.astype(BF16), wd[cols, :])
        gf = gf_ref[...]
        for r in range(tb // sub):
            rows = slice(r * sub, (r + 1) * sub)
            y, xhat, rstd = _rms(h1_ref[rows, :] + acc[rows, :], gf)
            err = y - t_ref[rows, :]
            sq_ref[...] += jnp.sum(err * err, axis=0, keepdims=True)
            dh2, dgf = _rms_bwd(err * (1.0 / D_MODEL), xhat, rstd, gf)
            dgf_ref[...] += dgf
            dh_ref[rows, :] = dh2
            dhb_ref[rows, :] = dh2.astype(BF16)

    row = pl.BlockSpec((tb, D_MODEL), lambda i: (i, 0))
    ff = pl.BlockSpec((tb, D_FF), lambda i: (i, 0))
    vec = pl.BlockSpec((1, D_MODEL), lambda i: (0, 0))
    anyspace = pl.BlockSpec(memory_space=pl.ANY)
    return pl.pallas_call(
        body, name="ffn_fwd", grid=(s // tb,),
        in_specs=[row, row, row, vec, anyspace, anyspace, anyspace],
        out_specs=[ff, ff, row, row, vec, vec],
        out_shape=[pltpu.HBM((s, D_FF), BF16), pltpu.HBM((s, D_FF), BF16),
                   pltpu.HBM((s, D_MODEL), F32), pltpu.HBM((s, D_MODEL), BF16),
                   jax.ShapeDtypeStruct((1, D_MODEL), F32), jax.ShapeDtypeStruct((1, D_MODEL), F32)],
        scratch_shapes=[pltpu.VMEM((D_FF, D_MODEL), BF16), pltpu.VMEM((D_FF, D_MODEL), BF16),
                        pltpu.VMEM((D_FF, D_MODEL), BF16), pltpu.SemaphoreType.DMA((3,))],
        compiler_params=_params(60, 1),
    )(*_in_hbm(h1, n2, target, lnf_g, w_gate, w_up, w_down))


def _ffn_bwd(n2, dh2b, g, u, w_gate_t, w_up_t, w_down):
    s = n2.shape[0]
    tb = min(1024, s)
    n_ff = D_FF // FF_CHUNK
    n_tb = s // tb

    def body(n2_hbm, dh_hbm, g_ref, u_ref, wg_ref, wu_ref, wd_ref, dwg_ref, dwu_ref, dwd_ref, dn2_ref,
             dn2_acc, acc_g, acc_u, acc_d, n2_all, dh_all, sem):
        j = pl.program_id(0)
        t = pl.program_id(1)
        rows = pl.ds(pl.multiple_of(t * tb, tb), tb)

        def fetch(block):
            at = pl.ds(pl.multiple_of(block * tb, tb), tb)
            return [pltpu.make_async_copy(src.at[at], dst.at[at], sem.at[2 * block + k])
                    for k, (src, dst) in enumerate(((n2_hbm, n2_all), (dh_hbm, dh_all)))]

        @pl.when(jnp.logical_and(j == 0, t == 0))
        def _():
            for cp in fetch(t):
                cp.start()

        @pl.when(jnp.logical_and(j == 0, t + 1 < n_tb))
        def _():
            for cp in fetch(t + 1):
                cp.start()

        @pl.when(j == 0)
        def _():
            for cp in fetch(t):
                cp.wait()

        @pl.when(t == 0)
        def _():
            acc_g[...] = jnp.zeros_like(acc_g)
            acc_u[...] = jnp.zeros_like(acc_u)
            acc_d[...] = jnp.zeros_like(acc_d)

        @pl.when(j == 0)
        def _():
            dn2_acc[rows, :] = jnp.zeros((tb, D_MODEL), F32)

        n2v = n2_all[rows, :]
        dh = dh_all[rows, :]
        gv = g_ref[...].astype(F32)
        uv = u_ref[...].astype(F32)
        sg = jax.nn.sigmoid(gv)
        silu = gv * sg
        dact = _mm_nt(dh, wd_ref[...])
        dub = (dact * silu).astype(BF16)
        dgb = (dact * uv * (sg * (1.0 + gv * (1.0 - sg)))).astype(BF16)
        acc_d[...] += _mm_tn((silu * uv).astype(BF16), dh)
        acc_g[...] += _mm_tn(dgb, n2v)
        acc_u[...] += _mm_tn(dub, n2v)
        dn2_acc[rows, :] += _mm(jnp.concatenate([dgb, dub], axis=1),
                                jnp.concatenate([wg_ref[...], wu_ref[...]], axis=0))

        @pl.when(t == n_tb - 1)
        def _():
            dwg_ref[...] = acc_g[...].astype(BF16)
            dwu_ref[...] = acc_u[...].astype(BF16)
            dwd_ref[...] = acc_d[...].astype(BF16)

        @pl.when(j == n_ff - 1)
        def _():
            dn2_ref[...] = dn2_acc[rows, :]

    whole = pl.BlockSpec(memory_space=pl.ANY)
    act = pl.BlockSpec((tb, FF_CHUNK), lambda j, t: (t, j))
    w_row = pl.BlockSpec((FF_CHUNK, D_MODEL), lambda j, t: (j, 0))
    last = pl.BlockSpec((tb, D_MODEL), lambda j, t: (jnp.where(j == n_ff - 1, t, 0), 0))
    grad = pltpu.HBM((D_FF, D_MODEL), BF16)
    chunk_acc = pltpu.VMEM((FF_CHUNK, D_MODEL), F32)
    return pl.pallas_call(
        body, name="ffn_bwd", grid=(n_ff, n_tb),
        in_specs=[whole, whole, act, act, w_row, w_row, w_row],
        out_specs=[w_row, w_row, w_row, last],
        out_shape=[grad, grad, grad, pltpu.HBM((s, D_MODEL), F32)],
        scratch_shapes=[pltpu.VMEM((s, D_MODEL), F32), chunk_acc, chunk_acc, chunk_acc,
                        pltpu.VMEM((s, D_MODEL), BF16), pltpu.VMEM((s, D_MODEL), BF16),
                        pltpu.SemaphoreType.DMA((2 * n_tb,))],
        compiler_params=_params(60, 2),
    )(*_in_hbm(n2, dh2b, g, u, w_gate_t, w_up_t, w_down))


def _bwd_out(dn2, dh2, h1, y_pool, y_lru, gn_pool_g, gn_lru_g, w_out, ln2_g, after):
    s = h1.shape[0]
    tb = _token_block(s)

    def body(dn2_ref, dh2_ref, h1_ref, yp_ref, yl_ref, gp_ref, gl_ref, w_ref, g2_ref, _after,
             dh1_ref, dyp_ref, dyl_ref, dwb_ref, dg2_ref, dgp_ref, dgl_ref, dw_ref):
        @pl.when(pl.program_id(0) == 0)
        def _():
            dw_ref[...] = jnp.zeros_like(dw_ref)
            dg2_ref[...] = jnp.zeros_like(dg2_ref)
            dgp_ref[...] = jnp.zeros_like(dgp_ref)
            dgl_ref[...] = jnp.zeros_like(dgl_ref)

        g2 = g2_ref[...]
        _, xhat2, rstd2 = _rms(h1_ref[...], g2)
        dres, dg2 = _rms_bwd(dn2_ref[...], xhat2, rstd2, g2)
        dg2_ref[...] += dg2
        dh1 = dh2_ref[...] + dres
        dh1_ref[...] = dh1
        dh1b = dh1.astype(BF16)
        gp, gl = gp_ref[...], gl_ref[...]
        mp, xhat_p, rstd_p = _rms(_read_slabs(yp_ref), gp)
        ml, xhat_l, rstd_l = _rms(_read_slabs(yl_ref), gl)
        dw_ref[:POOL_WIDTH, :] += _mm_tn(mp.astype(BF16), dh1b)
        dw_ref[POOL_WIDTH:, :] += _mm_tn(ml.astype(BF16), dh1b)
        dyp, dgp = _rms_bwd(_mm_nt(dh1b, w_ref[:POOL_WIDTH, :]), xhat_p, rstd_p, gp)
        dyl, dgl = _rms_bwd(_mm_nt(dh1b, w_ref[POOL_WIDTH:, :]), xhat_l, rstd_l, gl)
        _write_slabs(dyp_ref, dyp)
        _write_slabs(dyl_ref, dyl)
        dgp_ref[...] += dgp
        dgl_ref[...] += dgl

        @pl.when(pl.program_id(0) == s // tb - 1)
        def _():
            dwb_ref[...] = dw_ref[...].astype(BF16)

    row = pl.BlockSpec((tb, D_MODEL), lambda i: (i, 0))
    half = _slabs_spec(tb)
    vec = pl.BlockSpec((1, D_MODEL), lambda i: (0, 0))
    hvec = pl.BlockSpec((1, 512), lambda i: (0, 0))
    mat = pl.BlockSpec((D_MODEL, D_MODEL), lambda i: (0, 0))
    return pl.pallas_call(
        body, name="bwd_out", grid=(s // tb,),
        in_specs=[row, row, row, half, half, hvec, hvec, mat, vec, pl.BlockSpec(memory_space=pl.ANY)],
        out_specs=[row, half, half, mat, vec, hvec, hvec],
        out_shape=[pltpu.HBM((s, D_MODEL), F32), _slab_major(s),
                   _slab_major(s), pltpu.HBM((D_MODEL, D_MODEL), BF16),
                   jax.ShapeDtypeStruct((1, D_MODEL), F32), jax.ShapeDtypeStruct((1, 512), F32),
                   jax.ShapeDtypeStruct((1, 512), F32)],
        scratch_shapes=[pltpu.VMEM((D_MODEL, D_MODEL), F32)],
        compiler_params=_params(48, 1),
    )(*_in_hbm(dn2, dh2, h1, y_pool, y_lru, gn_pool_g, gn_lru_g, w_out, ln2_g), after)


def _mixer_bwd(u_pool, u_lru, u_gate, h, decay, root, dy_pool, dy_lru,
               pool_w, pool_scale, conv_w, conv_b, wa_bd, b_a, wi_bd, b_i, lam, after):
    s = u_pool.shape[1]
    tc = _time_chunk(s)
    n_chunks = s // tc

    def body(up_ref, ul_ref, ug_ref, h_ref, a_ref, mult_ref, dyp_ref, dyl_ref,
             pw_ref, ps_ref, cw_ref, cb_ref, wa_ref, ba_ref, wi_ref, bi_ref, lam_ref, _after,
             dup_ref, dul_ref, dug_ref, dpw_ref, dps_ref, dcw_ref, dcb_ref, dwa_ref, dba_ref, dwi_ref, dbi_ref, dlam_ref,
             a_pad, b_pad, u_shift, h_shift, a_shift, d_shift):
        _scan_pads(a_pad, b_pad, tc, causal=False)
        deeper, window = _slab_scalars()
        pw = pw_ref[0].astype(BF16)
        wa = wa_ref[0].astype(BF16)
        wi = wi_ref[0].astype(BF16)
        ps, cb, ba, bi = ps_ref[...], cb_ref[...], ba_ref[...], bi_ref[...]
        cw = [cw_ref[k:k + 1, :] for k in range(CONV_WIDTH)]
        lam_v = lam_ref[...]
        sp = _softplus_neg(lam_v)
        row = lax.broadcasted_iota(jnp.int32, (tc, SLAB), 0)
        for ref in (dpw_ref, dps_ref, dcw_ref, dcb_ref, dwa_ref, dba_ref, dwi_ref, dbi_ref, dlam_ref):
            ref[...] = jnp.zeros_like(ref)

        def chunk(t0, ext_p, ext_l, ext_h, carry, at_start=False):
            l_next, a_next, dxc_next, ddn_next = carry
            rows = pl.ds(t0, tc)
            u_shift[...] = ext_l
            taps = [u_shift[HALO - k:HALO - k + tc, :] for k in range(CONV_WIDTH)]
            xc = _conv(taps, cw, cb)
            xb, r, i = _lru_gates(xc, wa, ba, wi, bi)
            a, mult = a_ref[rows, :], mult_ref[rows, :]
            hv = ext_h[HALO:]
            h_shift[...] = ext_h
            h_before = h_shift[HALO - 1:HALO - 1 + tc, :]
            ug = ug_ref[rows, :]
            dyl = dyl_ref[rows, :]
            gel, th = _gelu(ug)
            dug_ref[rows, :] = (dyl * hv * _gelu_grad(ug, th)).astype(BF16)
            a_shift[:tc, :] = a
            a_shift[tc:, :] = jnp.broadcast_to(a_next, (SUBLANES, SLAB))
            a_after = a_shift[1:1 + tc, :]
            l = _scan_anticausal(a_after, dyl * gel, l_next, a_pad, b_pad, tc)
            dmult = l * (i * xc)
            di = l * mult * xc
            dxc = l * mult * i
            dla = (l * h_before) * a - jnp.where(mult > 0.0, dmult * (a * a) / mult, 0.0)
            dlam_ref[...] += jnp.sum(dla * r, axis=0, keepdims=True)
            dpa = (dla * ((-LRU_C) * sp)) * (r * (1.0 - r))
            dpi = di * (i * (1.0 - i))
            dpab = dpa.astype(BF16)
            dpib = dpi.astype(BF16)
            dwa_ref[0] += _mm_tn(xb, dpab)
            dwi_ref[0] += _mm_tn(xb, dpib)
            dba_ref[...] += jnp.sum(dpa, axis=0, keepdims=True)
            dbi_ref[...] += jnp.sum(dpi, axis=0, keepdims=True)
            dxc = dxc + _mm_nt(dpab, wa) + _mm_nt(dpib, wi)
            d_shift[:tc, :] = dxc
            d_shift[tc:, :] = dxc_next
            dul_ref[rows, :] = (cw[3] * dxc + cw[2] * d_shift[1:1 + tc, :]
                                + cw[1] * d_shift[2:2 + tc, :] + cw[0] * d_shift[3:3 + tc, :]).astype(BF16)
            for k in range(CONV_WIDTH):
                dcw_ref[k:k + 1, :] += jnp.sum(dxc * taps[CONV_WIDTH - 1 - k], axis=0, keepdims=True)
            dcb_ref[...] += jnp.sum(dxc, axis=0, keepdims=True)
            db = (_window_mean(_causal_window(ext_p, deeper), t0, row, window, at_start) - ext_p[HALO:]).astype(BF16)
            dyp = dyp_ref[rows, :]
            dps_ref[...] += jnp.sum(dyp * _mm(db, pw), axis=0, keepdims=True)
            dys = (dyp * ps).astype(BF16)
            dpw_ref[0] += _mm_tn(db, dys)
            dd = _mm_nt(dys, pw)
            ddn = _window_mean(dd, t0, row, window, at_start)
            ext_q = jnp.concatenate([ddn, ddn_next], axis=0)
            dup_ref[rows, :] = (_anticausal_window(ext_q, deeper, tc) - dd).astype(BF16)
            return l[0:1, :], a[0:1, :], dxc[0:8, :], ddn[0:HALO, :]

        def step(k, carry):
            c = n_chunks - 1 - k
            t0 = pl.multiple_of(c * tc, tc)
            ext = pl.ds(pl.multiple_of(c * tc - HALO, HALO), tc + HALO)
            return chunk(t0, up_ref[ext, :], ul_ref[ext, :], h_ref[ext, :], carry)

        carry = (jnp.zeros((1, SLAB), F32), jnp.zeros((1, SLAB), F32),
                 jnp.zeros((8, SLAB), F32), jnp.zeros((HALO, SLAB), F32))
        carry = lax.fori_loop(0, n_chunks - 1, step, carry)
        pad = jnp.zeros((HALO, SLAB), F32)
        first = pl.ds(0, tc)
        chunk(0, jnp.concatenate([pad, up_ref[first, :]], axis=0), jnp.concatenate([pad, ul_ref[first, :]], axis=0),
              jnp.concatenate([pad, h_ref[first, :]], axis=0), carry, at_start=True)
        dlam_ref[...] = dlam_ref[...] * (LRU_C * jax.nn.sigmoid(-lam_v))

    seq, mat, vec, taps = _slab_specs(s)
    grad = _slab_major(s, BF16)
    mats = jax.ShapeDtypeStruct((N_SLAB, SLAB, SLAB), F32)
    vecs = jax.ShapeDtypeStruct((1, 512), F32)
    return pl.pallas_call(
        body, name="mixer_bwd", grid=(N_SLAB,),
        in_specs=[seq] * 8 + [mat, vec, taps, vec, mat, vec, mat, vec, vec, pl.BlockSpec(memory_space=pl.ANY)],
        out_specs=[seq, seq, seq, mat, vec, taps, vec, mat, vec, mat, vec, vec],
        out_shape=[grad, grad, grad, mats, vecs, jax.ShapeDtypeStruct((CONV_WIDTH, 512), F32), vecs,
                   mats, vecs, mats, vecs, vecs],
        scratch_shapes=[pltpu.VMEM((tc + SUBLANES, SLAB), F32), pltpu.VMEM((tc + SUBLANES, SLAB), F32),
                        pltpu.VMEM((tc + HALO, SLAB), F32), pltpu.VMEM((tc + HALO, SLAB), F32),
                        pltpu.VMEM((tc + SUBLANES, SLAB), F32), pltpu.VMEM((tc + SUBLANES, SLAB), F32)],
        compiler_params=_params(56, 1),
    )(*_in_hbm(u_pool, u_lru, u_gate, h, decay, root, dy_pool, dy_lru, pool_w, pool_scale, conv_w, conv_b, wa_bd,
               b_a, wi_bd, b_i, lam), after)


def _bwd_in(x, dh1, du_pool, du_lru, du_gate, ln1_g, w_in_t, after):
    s = x.shape[0]
    tb = _token_block(s)

    def body(x_ref, dh1_ref, dup_ref, dul_ref, dug_ref, g_ref, w_ref, _after, dx_ref, dwb_ref, dg_ref, dw_ref):
        @pl.when(pl.program_id(0) == 0)
        def _():
            dw_ref[...] = jnp.zeros_like(dw_ref)
            dg_ref[...] = jnp.zeros_like(dg_ref)

        g1 = g_ref[...]
        n, xhat, rstd = _rms(x_ref[...], g1)
        nb = n.astype(BF16)
        db = jnp.concatenate([_read_slabs(ref) for ref in (dup_ref, dul_ref, dug_ref)], axis=1)
        dw_ref[...] += _mm_tn(db, nb)
        dx, dg1 = _rms_bwd(_mm(db, w_ref[...]), xhat, rstd, g1)
        dx_ref[...] = dh1_ref[...] + dx
        dg_ref[...] += dg1

        @pl.when(pl.program_id(0) == s // tb - 1)
        def _():
            dwb_ref[...] = dw_ref[...].astype(BF16)

    row = pl.BlockSpec((tb, D_MODEL), lambda i: (i, 0))
    half = _slabs_spec(tb)
    vec = pl.BlockSpec((1, D_MODEL), lambda i: (0, 0))
    mat = pl.BlockSpec((IN_WIDTH, D_MODEL), lambda i: (0, 0))
    return pl.pallas_call(
        body, name="bwd_in", grid=(s // tb,),
        in_specs=[row, row, half, half, half, vec, mat, pl.BlockSpec(memory_space=pl.ANY)],
        out_specs=[row, mat, vec],
        out_shape=[pltpu.HBM((s, D_MODEL), F32), pltpu.HBM((IN_WIDTH, D_MODEL), BF16),
                   jax.ShapeDtypeStruct((1, D_MODEL), F32)],
        scratch_shapes=[pltpu.VMEM((IN_WIDTH, D_MODEL), F32)],
        compiler_params=_params(48, 1),
    )(*_in_hbm(x, dh1, du_pool, du_lru, du_gate, ln1_g, w_in_t), after)


def _mesh_position():
    x, y, c = lax.axis_index("x"), lax.axis_index("y"), lax.axis_index("c")
    return x, y, c, 4 * x + 2 * y + c


def _peer(x, y, c, p):
    px = 1 - x if p & 4 else x
    py = 1 - y if p & 2 else y
    pc = 1 - c if p & 1 else c
    return (px, py, pc), 4 * px + 2 * py + pc


HBM_SPEC = pl.BlockSpec(memory_space=pltpu.HBM)
SEM_SPEC = pl.BlockSpec(memory_space=pltpu.SEMAPHORE)
DATAFLOW = pltpu.SideEffectType.DATAFLOW_SIDE_EFFECTING


class Copy(NamedTuple):
    src: int
    src_at: Any
    dst: int
    dst_at: Any
    peer: int
    group: int
    slot: int


SIBLING = (1,)
SAME_CORE = (2, 4, 6)
EVERYONE = tuple(range(1, N_DEV))
MYSELF = (0,)


def _same(index):
    return index


def _chip(index):
    return jnp.right_shift(index, 1)


def _fan_out(srcs, lands, peers, group):
    return [Copy(s, None, d, _same, p, group, N_DEV * i + p) for i, (s, d) in enumerate(zip(srcs, lands)) for p in peers]


def _scatter(stacks, lands, peers):
    return [Copy(s, lambda me, p=p: jnp.bitwise_xor(me, p), d, _same, p, 0, 0)
            for s, d in zip(stacks, lands) for p in peers]


def _numbered(copies, group=0):
    return [cp._replace(group=group, slot=i) for i, cp in enumerate(copies)]


def _relay(lands, peers):
    return [Copy(b, lambda s, q=q: jnp.bitwise_xor(s, q), b, lambda s, q=q: jnp.bitwise_xor(s, q), 1, 0, N_DEV * i + q)
            for i, b in enumerate(lands) for q in peers]


def _to_sibling(stacks, lands):
    return [Copy(s, lambda me, k=k: 2 * k + 1 - jnp.bitwise_and(me, 1), d, lambda me, k=k: k, 1, 0, 4 * i + k)
            for i, (s, d) in enumerate(zip(stacks, lands)) for k in range(N_DEV // 2)]


def _to_chips(sums, lands):
    return [Copy(s, lambda me, p=p: jnp.bitwise_xor(_chip(me), p // 2), d, _chip, p, 0, 4 * i + p // 2)
            for i, (s, d) in enumerate(zip(sums, lands)) for p in SAME_CORE]


def _comm_call(name, bufs, wait=None, start=None, after=()):
    nb = len(bufs)
    waits = [wait] if isinstance(wait, tuple) else list(wait or [])
    slots = list(start[1]) if start else []
    n_out_sem = 2 * len(slots)
    after = [a for a in (after if isinstance(after, (list, tuple)) else [after]) if a is not None]

    def body(*refs):
        b = refs[:nb]
        at = nb + 2 * len(waits) + len(after)
        out_sems = refs[at:at + n_out_sem]
        token = refs[at + n_out_sem + nb]
        x, y, c, me = _mesh_position()

        def part(i, row_of, sender):
            return b[i] if row_of is None else b[i].at[row_of(sender)]

        for k, (_, copies) in enumerate(waits):
            w_send, w_recv = refs[nb + 2 * k], refs[nb + 2 * k + 1]
            for cp in copies:
                peer, peer_index = _peer(x, y, c, cp.peer)
                arrival = pltpu.make_async_remote_copy(part(cp.src, cp.src_at, me), part(cp.dst, cp.dst_at, peer_index),
                                                       w_send.at[cp.slot], w_recv.at[cp.slot],
                                                       device_id=peer, device_id_type=MESH)
                arrival.wait_send()
                arrival.wait_recv()
        if start:
            for cp in start[0]:
                peer, _ = _peer(x, y, c, cp.peer)
                pltpu.make_async_remote_copy(part(cp.src, cp.src_at, me), part(cp.dst, cp.dst_at, me),
                                             out_sems[2 * cp.group].at[cp.slot], out_sems[2 * cp.group + 1].at[cp.slot],
                                             device_id=peer, device_id_type=MESH).start()
        token[...] = jnp.zeros_like(token)

    sem_shapes = []
    for n_slots in slots:
        sem_shapes += [pltpu.SemaphoreType.DMA((n_slots,))] * 2
    operands = [pltpu.with_memory_space_constraint(a, pltpu.HBM) for a in bufs]
    in_specs = [HBM_SPEC] * nb
    for sems, _ in waits:
        operands += list(sems)
        in_specs += [SEM_SPEC, SEM_SPEC]
    operands += after
    in_specs += [pl.BlockSpec(memory_space=pl.ANY)] * len(after)
    outs = pl.pallas_call(
        body, name=name, in_specs=in_specs,
        out_specs=[SEM_SPEC] * n_out_sem + [HBM_SPEC] * nb + [pl.BlockSpec(memory_space=pltpu.VMEM)],
        out_shape=sem_shapes + [pltpu.HBM(a.shape, a.dtype) for a in bufs] + [jax.ShapeDtypeStruct((8, SLAB), F32)],
        input_output_aliases={i: n_out_sem + i for i in range(nb)},
        compiler_params=pltpu.CompilerParams(has_side_effects=DATAFLOW),
    )(*operands)
    sems = [(outs[2 * k], outs[2 * k + 1]) for k in range(len(slots))]
    return sems, list(outs[n_out_sem:n_out_sem + nb]), outs[-1]


def _pair_sum(stacks, lands, place):
    n = len(stacks)

    def body(place_ref, *refs):
        k = pl.program_id(0)
        for m in range(n):
            mine, theirs, out, land = refs[m], refs[n + m], refs[2 * n + m], refs[3 * n + m]
            total = (mine[0, 0].astype(F32) + theirs[0].astype(F32)).astype(out.dtype)
            out[0] = total

            @pl.when(k == place_ref[1])
            def _():
                land[0] = total

    in_specs = [pl.BlockSpec((1, 1) + a.shape[1:], lambda k, place_ref: (k, place_ref[0], 0, 0)) for a in stacks]
    in_specs += [pl.BlockSpec((1,) + a.shape[1:], lambda k, place_ref: (k, 0, 0)) for a in lands]
    out_specs = [pl.BlockSpec((1,) + a.shape[1:], lambda k, place_ref: (k, 0, 0)) for a in lands]
    out_specs += [pl.BlockSpec((1,) + a.shape[1:], lambda k, place_ref: (place_ref[1], 0, 0)) for a in lands]
    outs = pl.pallas_call(
        body, name="pair_sum_" + "_".join(str(a.shape[1]) for a in stacks),
        grid_spec=pltpu.PrefetchScalarGridSpec(num_scalar_prefetch=1, grid=(N_DEV // 2,), in_specs=in_specs,
                                               out_specs=out_specs),
        out_shape=[pltpu.HBM(a.shape, a.dtype) for a in lands] * 2,
        compiler_params=_params(40, 1),
    )(place, *_in_hbm(*[a.reshape((N_DEV // 2, 2) + a.shape[1:]) for a in stacks], *lands))
    return list(outs[:n]), list(outs[n:])


def _reduce_adam(parts, w, m, v, name):
    rows, cols = w.shape
    n_parts = parts.shape[0]
    rb = rows
    for cand in (256, 176, 128):
        if rows % cand == 0 and rows > cand:
            rb = cand
            break

    def body(p_ref, w_ref, m_ref, v_ref, g_out, d_out, m_out, v_out):
        g = p_ref[0].astype(F32)
        for j in range(1, n_parts):
            g = g + p_ref[j].astype(F32)
        g_out[...] = g
        d_out[...], m_out[...], v_out[...] = _adam(g, w_ref[...], m_ref[...], v_ref[...])

    blk = pl.BlockSpec((rb, cols), lambda i: (i, 0))
    out = jax.ShapeDtypeStruct((rows, cols), F32)
    return pl.pallas_call(
        body, name=name, grid=(rows // rb,),
        in_specs=[pl.BlockSpec((n_parts, rb, cols), lambda i: (0, i, 0)), blk, blk, blk],
        out_specs=[blk] * 4, out_shape=[out] * 4,
        compiler_params=_params(40, 1),
    )(*_in_hbm(parts, w, m, v))


def _cols_from_stack(stack):
    n, r, c = stack.shape
    return jnp.transpose(stack, (1, 0, 2)).reshape(r, n * c)


def _block_diag(w):
    z = jnp.zeros((N_SLAB, 64, 64), w.dtype)
    pairs = w.reshape(N_SLAB, 2, 64, 64)
    top = jnp.concatenate([pairs[:, 0], z], axis=2)
    bottom = jnp.concatenate([z, pairs[:, 1]], axis=2)
    return jnp.concatenate([top, bottom], axis=1)


def _adam(g, w, m, v):
    m_new = ADAM_B1 * m + (1.0 - ADAM_B1) * g
    v_new = ADAM_B2 * v + (1.0 - ADAM_B2) * (g * g)
    m_hat = m_new / (1.0 - ADAM_B1 ** ADAM_STEP)
    v_hat = v_new / (1.0 - ADAM_B2 ** ADAM_STEP)
    return (-ADAM_LR) * (m_hat / (jnp.sqrt(v_hat) + ADAM_EPS) + ADAM_WD * w), m_new, v_new


WIDE = ("ln2_g", "lnf_g")
HALF = ("pool_scale", "conv_b", "b_a", "b_i", "lam", "gn_pool_g", "gn_lru_g")
VECTORS = [(k, D_MODEL) for k in WIDE] + [(k, 512) for k in HALF]
VECTOR_ROWS = sum(width // SLAB for _, width in VECTORS)
LOSS_ROW = -(-VECTOR_ROWS // 8) * 8
CONV_AT = LOSS_ROW + 8
CONV_LANES = LRU_WIDTH // SLAB
PACK_F_ROWS = CONV_AT + CONV_WIDTH * CONV_LANES
MATRIX_ROWS = N_SLAB * SLAB
HEAD = SLAB // 2
GATE_ROWS = N_SLAB * HEAD
PACK_B_ROWS = MATRIX_ROWS + 2 * GATE_ROWS


def _pack_small(vectors, pool_g, wa_g, wi_g, conv, sq):
    n_vec = len(vectors)

    def body(*refs):
        vec = refs[:n_vec]
        pw_ref, wa_ref, wi_ref, cw_ref, sq_ref, out, out_b = refs[n_vec:]
        out[...] = jnp.zeros_like(out)
        row = 0
        for ref, (_, width) in zip(vec, VECTORS):
            for k in range(width // SLAB):
                out[row:row + 1, :] = ref[:, k * SLAB:(k + 1) * SLAB]
                row += 1
        for tap in range(CONV_WIDTH):
            for k in range(CONV_LANES):
                at = CONV_AT + tap * CONV_LANES + k
                out[at:at + 1, :] = cw_ref[tap:tap + 1, k * SLAB:(k + 1) * SLAB]
        total = sq_ref[:, 0:SLAB]
        for k in range(1, D_MODEL // SLAB):
            total = total + sq_ref[:, k * SLAB:(k + 1) * SLAB]
        out[LOSS_ROW:LOSS_ROW + 1, :] = total
        left = lax.broadcasted_iota(jnp.int32, (HEAD, SLAB), 1) < HEAD
        for s in range(N_SLAB):
            out_b[s * SLAB:(s + 1) * SLAB, :] = pw_ref[s].astype(BF16)
            for i, ref in enumerate((wa_ref, wi_ref)):
                at = MATRIX_ROWS + i * GATE_ROWS + s * HEAD
                out_b[at:at + HEAD, :] = jnp.where(left, ref[s, 0:HEAD, :], ref[s, HEAD:SLAB, :]).astype(BF16)

    return pl.pallas_call(
        body, name="pack_small",
        out_shape=[jax.ShapeDtypeStruct((PACK_F_ROWS, SLAB), F32), jax.ShapeDtypeStruct((PACK_B_ROWS, SLAB), BF16)],
    )(*vectors, pool_g, wa_g, wi_g, conv, sq)


def _small_reduce_adam(my_index, parts, parts_b, vec_w, vec_m, vec_v, pool_wmv, wa_wmv, wi_wmv, conv_wmv):
    n_vec = len(VECTORS)
    n_parts = parts.shape[0]

    def body(*refs):
        me_ref, p_ref, pb_ref = refs[0], refs[1], refs[2]
        refs = refs[1:]
        w_refs, m_refs, v_refs = (refs[2 + k * n_vec:2 + (k + 1) * n_vec] for k in range(3))
        at = 2 + 3 * n_vec
        pw_w, pw_m, pw_v = refs[at:at + 3]
        gates = (refs[at + 3:at + 6], refs[at + 6:at + 9])
        cv_w, cv_m, cv_v = refs[at + 9:at + 12]
        outs = refs[at + 12:-1]
        total = refs[-1]
        total[...] = p_ref[0]
        for j in range(1, n_parts):
            total[...] += p_ref[j]
        row = 0
        for i, (_, width) in enumerate(VECTORS):
            n_rows = width // SLAB
            g = jnp.concatenate([total[row + k:row + k + 1, :] for k in range(n_rows)], axis=1)
            row += n_rows
            d, m_new, v_new = _adam(g, w_refs[i][...], m_refs[i][...], v_refs[i][...])
            for ref, val in zip(outs[4 * i:4 * i + 4], (g, d, m_new, v_new)):
                ref[...] = val
        tail = outs[4 * n_vec:]

        def summed(first, count):
            g = pb_ref[0, first:first + count, :].astype(F32)
            for j in range(1, n_parts):
                g = g + pb_ref[j, first:first + count, :].astype(F32)
            return g

        g = summed(0, MATRIX_ROWS)
        d, m_new, v_new = _adam(g, pw_w[...], pw_m[...], pw_v[...])
        for ref, val in zip(tail[0:4], (g, d, m_new, v_new)):
            ref[...] = val
        for i, (w, m, v) in enumerate(gates):
            g = summed(MATRIX_ROWS + i * GATE_ROWS, GATE_ROWS)
            for s in range(N_SLAB):
                for hd in range(2):
                    block = g[s * HEAD:(s + 1) * HEAD, hd * HEAD:(hd + 1) * HEAD]
                    rows = slice((2 * s + hd) * HEAD, (2 * s + hd + 1) * HEAD)
                    d, m_new, v_new = _adam(block, w[rows, :], m[rows, :], v[rows, :])
                    for ref, val in zip(tail[4 + 4 * i:8 + 4 * i], (block, d, m_new, v_new)):
                        ref[rows, :] = val
        me = me_ref[0, 0]
        taps = []
        for tap in range(CONV_WIDTH):
            row = total[pl.ds(CONV_AT + tap * CONV_LANES + me // 2, 1), :]
            taps.append(jnp.where(me % 2 == 0, row[:, :HEAD], row[:, HEAD:]))
        g = jnp.concatenate(taps, axis=0)
        d, m_new, v_new = _adam(g, cv_w[...], cv_m[...], cv_v[...])
        for ref, val in zip(tail[12:16], (g, d, m_new, v_new)):
            ref[...] = val
        tail[16][...] = (0.5 / D_MODEL) * jnp.sum(total[LOSS_ROW:LOSS_ROW + 1, :], axis=1, keepdims=True)

    out_shape = []
    for _, width in VECTORS:
        out_shape += [jax.ShapeDtypeStruct((1, width), F32)] * 4
    out_shape += [jax.ShapeDtypeStruct((MATRIX_ROWS, SLAB), F32)] * 4
    out_shape += [jax.ShapeDtypeStruct((2 * GATE_ROWS, HEAD), F32)] * 8
    out_shape += [jax.ShapeDtypeStruct(conv_wmv[0].shape, F32)] * 4 + [jax.ShapeDtypeStruct((1, 1), F32)]
    operands = [parts, parts_b, *vec_w, *vec_m, *vec_v, *pool_wmv, *wa_wmv, *wi_wmv, *conv_wmv]

    def whole(a):
        return pl.BlockSpec(a.shape, lambda i, n=len(a.shape): (0,) * n)

    outs = pl.pallas_call(
        body, name="adam_small", grid=(1,),
        in_specs=[pl.BlockSpec(memory_space=pltpu.SMEM)] + [whole(a) for a in operands],
        out_specs=[whole(a) for a in out_shape], out_shape=out_shape,
        scratch_shapes=[pltpu.VMEM((PACK_F_ROWS, SLAB), F32)],
        compiler_params=_params(40, 1),
    )(my_index, *_in_hbm(*operands))
    vec_out = [tuple(outs[4 * i:4 * i + 4]) for i in range(n_vec)]
    tail = outs[4 * n_vec:]
    return vec_out, tuple(tail[0:4]), tuple(tail[4:8]), tuple(tail[8:12]), tuple(tail[12:16]), tail[16]


def kernel(x, ln1_g, w_in, pool_w, pool_scale, conv_w, conv_b, w_a, b_a, w_i, b_i, lam, gn_pool_g, gn_lru_g, w_out, ln2_g, w_ffn_gate, w_ffn_up, w_ffn_down, lnf_g, loss_target, m_ln1_g, m_w_in, m_pool_w, m_pool_scale, m_conv_w, m_conv_b, m_w_a, m_b_a, m_w_i, m_b_i, m_lam, m_gn_pool_g, m_gn_lru_g, m_w_out, m_ln2_g, m_w_ffn_gate, m_w_ffn_up, m_w_ffn_down, m_lnf_g, v_ln1_g, v_w_in, v_pool_w, v_pool_scale, v_conv_w, v_conv_b, v_w_a, v_b_a, v_w_i, v_b_i, v_lam, v_gn_pool_g, v_gn_lru_g, v_w_out, v_ln2_g, v_w_ffn_gate, v_w_ffn_up, v_w_ffn_down, v_lnf_g):
    weights = dict(ln1_g=ln1_g, w_in=w_in, pool_w=pool_w, pool_scale=pool_scale, conv_w=conv_w, conv_b=conv_b,
                   w_a=w_a, b_a=b_a, w_i=w_i, b_i=b_i, lam=lam, gn_pool_g=gn_pool_g, gn_lru_g=gn_lru_g,
                   w_out=w_out, ln2_g=ln2_g, w_ffn_gate=w_ffn_gate, w_ffn_up=w_ffn_up, w_ffn_down=w_ffn_down,
                   lnf_g=lnf_g)
    mom1 = dict(ln1_g=m_ln1_g, w_in=m_w_in, pool_w=m_pool_w, pool_scale=m_pool_scale, conv_w=m_conv_w,
                conv_b=m_conv_b, w_a=m_w_a, b_a=m_b_a, w_i=m_w_i, b_i=m_b_i, lam=m_lam, gn_pool_g=m_gn_pool_g,
                gn_lru_g=m_gn_lru_g, w_out=m_w_out, ln2_g=m_ln2_g, w_ffn_gate=m_w_ffn_gate,
                w_ffn_up=m_w_ffn_up, w_ffn_down=m_w_ffn_down, lnf_g=m_lnf_g)
    mom2 = dict(ln1_g=v_ln1_g, w_in=v_w_in, pool_w=v_pool_w, pool_scale=v_pool_scale, conv_w=v_conv_w,
                conv_b=v_conv_b, w_a=v_w_a, b_a=v_b_a, w_i=v_w_i, b_i=v_b_i, lam=v_lam, gn_pool_g=v_gn_pool_g,
                gn_lru_g=v_gn_lru_g, w_out=v_w_out, ln2_g=v_ln2_g, w_ffn_gate=v_w_ffn_gate,
                w_ffn_up=v_w_ffn_up, w_ffn_down=v_w_ffn_down, lnf_g=v_lnf_g)

    xs = x[0]
    target = loss_target[0]

    shard = dict(w_in=lambda a: a[0].T, w_ffn_gate=lambda a: a[0].T, w_ffn_up=lambda a: a[0].T,
                 w_out=lambda a: a[0], w_ffn_down=lambda a: a[0], conv_w=lambda a: a[0])
    unshard = dict(w_in=lambda a: a.T[None], w_ffn_gate=lambda a: a.T[None], w_ffn_up=lambda a: a.T[None],
                   w_out=lambda a: a[None], w_ffn_down=lambda a: a[None], conv_w=lambda a: a[None])

    gathered = ("w_in", "conv_w", "w_out", "w_ffn_gate", "w_ffn_up", "w_ffn_down")
    groups = ((0, 1), (2,), (3, 4), (5,))
    sources = [shard[k](weights[k]) if k == "conv_w" else shard[k](weights[k]).astype(BF16) for k in gathered]
    my_index = 4 * lax.axis_index("x") + 2 * lax.axis_index("y") + lax.axis_index("c")
    lands = [lax.empty((N_DEV,) + a.shape, a.dtype) for a in sources]

    def first_hop(n, at=0):
        return _numbered(_fan_out(range(at, at + n), range(at + n, at + 2 * n), SAME_CORE + SIBLING + MYSELF, 0))

    def second_hop(n, at=0):
        return _numbered(_relay(range(at, at + n), SAME_CORE))

    start = []
    for g, members in enumerate(groups):
        start += _numbered(_fan_out(members, [6 + m for m in members], SAME_CORE + SIBLING + MYSELF, 0), g)
    sems, bufs, _ = _comm_call("gather_start", sources + lands,
                               start=(start, [sum(cp.group == g for cp in start) for g in range(len(groups))]))
    sources, lands = bufs[:6], bufs[6:]

    def group_buffers(g):
        return [sources[m] for m in groups[g]] + [lands[m] for m in groups[g]]

    def relayed(tag, g, after):
        n = len(groups[g])
        relay = second_hop(n, n)
        relay_sems, bufs, _ = _comm_call("gather_relay_" + tag, group_buffers(g), wait=(sems[g], first_hop(n)),
                                         start=(relay, (len(relay),)), after=after)
        return relay_sems[0], bufs[n:]

    relay_sem, bufs = relayed("in", 0, None)
    g_in, g_conv = _comm_call("gather_wait_in", bufs, wait=(relay_sem, second_hop(2)))[1]
    w_in_f = g_in.reshape(IN_WIDTH, D_MODEL)
    conv_w_f = _cols_from_stack(g_conv)

    wa_bd = _block_diag(w_a[0])
    wi_bd = _block_diag(w_i[0])
    lnf_row = lnf_g.reshape(1, D_MODEL)

    u_pool, u_lru, u_gate = _fwd_in(xs, ln1_g, w_in_f)
    y_pool, h, y_lru, decay, root = _mixer_fwd(u_pool, u_lru, u_gate, pool_w[0], pool_scale, conv_w_f, conv_b,
                                               wa_bd, b_a, wi_bd, b_i, lam)
    relay_sem, bufs = relayed("out", 1, y_pool)
    gate_up = second_hop(2, 3)
    (gate_up_sem,), bufs, _ = _comm_call("gather_wait_out", bufs + group_buffers(2),
                                         wait=[(relay_sem, second_hop(1)), (sems[2], first_hop(2, 1))],
                                         start=(gate_up, (len(gate_up),)))
    g_out, gate_up_lands = bufs[0], bufs[3:]
    w_out_f = g_out.reshape(D_MODEL, D_MODEL)
    h1, n2 = _fwd_out(xs, y_pool, y_lru, gn_pool_g, gn_lru_g, w_out_f, ln2_g)
    relay_sem, bufs = relayed("down", 3, n2)
    g_gate, g_up, g_down = _comm_call("gather_wait_ffn", gate_up_lands + bufs,
                                      wait=[(gate_up_sem, second_hop(2)), (relay_sem, second_hop(1, 2))])[1]
    w_gate_f = g_gate.reshape(D_FF, D_MODEL)
    w_up_f = g_up.reshape(D_FF, D_MODEL)
    w_down_f = g_down.reshape(D_FF, D_MODEL)
    g_act, u_act, dh2, dh2b, d_lnf, sq = _ffn_fwd(h1, n2, target, lnf_row, w_gate_f, w_up_f, w_down_f)

    def direct(tag, stacks, wholes, after):
        sources = list(stacks) + list(wholes)
        n, n_st = len(sources), len(stacks)
        lands = [lax.empty(a.shape if i < n_st else (N_DEV,) + a.shape, a.dtype) for i, a in enumerate(sources)]
        copies = _numbered(_scatter(range(n_st), range(n, n + n_st), EVERYONE + MYSELF)
                           + _fan_out(range(n_st, n), range(n + n_st, 2 * n), EVERYONE + MYSELF, 0))
        sem, bufs, token = _comm_call(tag + "_start", sources + lands, start=(copies, (len(copies),)), after=after)
        return (tag, sem[0], bufs, copies), token

    def direct_finish(started, after):
        tag, sem, bufs, copies = started
        _, bufs, _ = _comm_call(tag + "_wait", bufs, wait=(sem, copies), after=after)
        return bufs[len(bufs) // 2:]

    place = jnp.stack([lax.axis_index("c"), 2 * lax.axis_index("x") + lax.axis_index("y")]).astype(jnp.int32)

    d_gate, d_up, d_down, dn2 = _ffn_bwd(n2, dh2b, g_act, u_act, w_gate_f, w_up_f, w_down_f)
    down_stack, *ffn_stacks = [d.reshape(N_DEV, D_FF // N_DEV, D_MODEL) for d in (d_down, d_gate, d_up)]
    pair_lands = [lax.empty((N_DEV // 2,) + a.shape[1:], a.dtype) for a in ffn_stacks]
    pair_copies = _numbered(_to_sibling(range(2), range(2, 4)))
    down_copies = _numbered(_scatter([4], [5], EVERYONE + MYSELF), 1)
    (sem, down_sem), bufs, token = _comm_call(
        "ffn_grads_start", ffn_stacks + pair_lands + [down_stack, lax.empty(down_stack.shape, down_stack.dtype)],
        start=(pair_copies + down_copies, (len(pair_copies), len(down_copies))))
    bufs, down_bufs = bufs[:4], bufs[4:]
    dh1, dy_pool, dy_lru, d_out, d_ln2, d_gnp, d_gnl = _bwd_out(dn2, dh2, h1, y_pool, y_lru, gn_pool_g, gn_lru_g,
                                                                 w_out_f, ln2_g, token)
    _, bufs, _ = _comm_call("ffn_pair_wait", bufs, wait=(sem, pair_copies), after=dh1)
    ffn_sums, ffn_lands = _pair_sum(bufs[:2], bufs[2:], place)
    ffn_copies = _numbered(_to_chips(range(2), range(2, 4)))
    ffn_sem, ffn_bufs, token = _comm_call("ffn_chip_start", ffn_sums + ffn_lands, start=(ffn_copies, (len(ffn_copies),)))
    (du_pool, du_lru, du_gate, d_pw, d_ps, d_cw, d_cb, d_wa, d_ba, d_wi, d_bi, d_lam) = _mixer_bwd(
        u_pool, u_lru, u_gate, h, decay, root, dy_pool, dy_lru, pool_w[0], pool_scale, conv_w_f, conv_b,
        wa_bd, b_a, wi_bd, b_i, lam, token)

    vec_grads = dict(ln2_g=d_ln2, lnf_g=d_lnf, pool_scale=d_ps, conv_b=d_cb, b_a=d_ba, b_i=d_bi,
                     lam=d_lam, gn_pool_g=d_gnp, gn_lru_g=d_gnl)
    packed, packed_b = _pack_small([vec_grads[k] for k, _ in VECTORS], d_pw, d_wa, d_wi, d_cw, sq)
    small_started, token = direct("small", [d_out.reshape(N_DEV, D_MODEL // N_DEV, D_MODEL)], [packed, packed_b], None)
    grad_x, d_in, d_ln1 = _bwd_in(xs, dh1, du_pool, du_lru, du_gate, ln1_g, w_in_f, token)
    in_started, token = direct("in", [d_in.reshape(N_DEV, IN_WIDTH // N_DEV, D_MODEL)], [d_ln1], None)

    results = {}

    def reduce_adam(name, parts):
        outs = _reduce_adam(parts, shard[name](weights[name]), shard[name](mom1[name]), shard[name](mom2[name]),
                            "adam_" + name)
        results[name] = tuple(unshard[name](o) for o in outs)
        return outs[0]

    shard["ln1_g"] = unshard["ln1_g"] = lambda a: a
    _, ffn_bufs, _ = _comm_call("ffn_chip_wait", ffn_bufs, wait=(ffn_sem[0], ffn_copies), after=token)
    done = [reduce_adam(name, parts) for name, parts in zip(("w_ffn_gate", "w_ffn_up"), ffn_bufs[2:])]
    _, (_, r_down), _ = _comm_call("down_wait", down_bufs, after=done,
                                   wait=(down_sem, _numbered(_scatter([0], [1], EVERYONE + MYSELF))))
    done.append(reduce_adam("w_ffn_down", r_down))
    r_out, r_small, r_small_b = direct_finish(small_started, done)
    done.append(reduce_adam("w_out", r_out))

    def as_row(a, width):
        return a.reshape(1, width)

    def as_matrix(a):
        return a.reshape(MATRIX_ROWS, SLAB)

    def as_heads(a):
        return a.reshape(2 * GATE_ROWS, HEAD)

    states = (weights, mom1, mom2)
    vec_out, pool_out, wa_out, wi_out, conv_out, loss_11 = _small_reduce_adam(
        my_index.astype(jnp.int32).reshape(1, 1), r_small, r_small_b,
        *[[as_row(t[k], w) for k, w in VECTORS] for t in states], [as_matrix(t["pool_w"]) for t in states],
        [as_heads(t["w_a"]) for t in states], [as_heads(t["w_i"]) for t in states], [t["conv_w"][0] for t in states])
    for (k, _), outs in zip(VECTORS, vec_out):
        results[k] = tuple(o.reshape(weights[k].shape) for o in outs)
    for k, outs in (("pool_w", pool_out), ("w_a", wa_out), ("w_i", wi_out), ("conv_w", conv_out)):
        results[k] = tuple(o.reshape(weights[k].shape) for o in outs)
    loss = loss_11[0, 0]
    r_in, r_ln1 = direct_finish(in_started, done + [loss_11])
    reduce_adam("w_in", r_in)
    reduce_adam("ln1_g", r_ln1)

    order = ["ln1_g", "w_in", "pool_w", "pool_scale", "conv_w", "conv_b", "w_a", "b_a", "w_i", "b_i", "lam",
             "gn_pool_g", "gn_lru_g", "w_out", "ln2_g", "w_ffn_gate", "w_ffn_up", "w_ffn_down", "lnf_g"]
    return (loss, grad_x[None],
            *[results[k][0] for k in order], *[results[k][1] for k in order],
            *[results[k][2] for k in order], *[results[k][3] for k in order])
```

```python
from typing import Any, NamedTuple

import jax
import jax.numpy as jnp
from jax import lax
from jax.experimental import pallas as pl
from jax.experimental.pallas import tpu as pltpu

F32 = jnp.float32
BF16 = jnp.bfloat16

N_DEV = 8
D_MODEL = 1024
POOL_WIDTH = 512
LRU_WIDTH = 512
IN_WIDTH = 1536
D_FF = 2816
N_SLAB = 4
SLAB = 128
CONV_WIDTH = 4
LRU_C = 8.0
EPS = 1e-6
HALO = 16
FF_CHUNK = 256

ADAM_LR = 0.001
ADAM_B1 = 0.9
ADAM_B2 = 0.999
ADAM_EPS = 1e-08
ADAM_WD = 0.01
ADAM_STEP = 10

MIB = 1 << 20
MESH = pl.DeviceIdType.MESH


def _params(vmem_mib, n_axes=0):
    sem = ("arbitrary",) * n_axes if n_axes else None
    return pltpu.CompilerParams(dimension_semantics=sem, vmem_limit_bytes=vmem_mib * MIB)


def _in_hbm(*arrays):
    return [pltpu.with_memory_space_constraint(a, pltpu.HBM) for a in arrays]


def _mm(a, b):
    return jnp.dot(a, b, preferred_element_type=F32)


def _mm_nt(a, b):
    return lax.dot_general(a, b, (((1,), (1,)), ((), ())), preferred_element_type=F32)


def _mm_tn(a, b):
    return lax.dot_general(a, b, (((0,), (0,)), ((), ())), preferred_element_type=F32)


def _rms(x, g):
    rstd = lax.rsqrt(jnp.mean(x * x, axis=-1, keepdims=True) + EPS)
    xhat = x * rstd
    return xhat * g, xhat, rstd


def _rms_bwd(dy, xhat, rstd, g):
    gy = dy * g
    dx = rstd * (gy - xhat * jnp.mean(gy * xhat, axis=-1, keepdims=True))
    return dx, jnp.sum(dy * xhat, axis=0, keepdims=True)


def _gelu(z):
    t = jnp.tanh(0.7978845608028654 * (z + 0.044715 * z * z * z))
    return 0.5 * z * (1.0 + t), t


def _gelu_grad(z, t):
    return 0.5 * (1.0 + t) + 0.5 * z * (1.0 - t * t) * 0.7978845608028654 * (1.0 + 3.0 * 0.044715 * z * z)


def _softplus_neg(lam):
    x = -lam
    e = jnp.exp(-jnp.abs(x))
    u = 1.0 + e
    l1p = jnp.where(u == 1.0, e, jnp.log(u) * e / (u - 1.0))
    return jnp.maximum(x, 0.0) + l1p


def _one_minus_square(la, a):
    x = 2.0 * la
    series = -x * (1.0 + x * (0.5 + x * (1.0 / 6.0 + x * (1.0 / 24.0))))
    return jnp.where(x > -0.06, series, 1.0 - a * a)


def _sigmoid(x):
    return 0.5 * jnp.tanh(0.5 * x) + 0.5


def _down(v, d):
    return pltpu.roll(v, d, 0)


def _up(v, d):
    return pltpu.roll(v, v.shape[0] - d, 0)


def _slab_major(s, dtype=F32):
    return pltpu.HBM((N_SLAB, s, SLAB), dtype)


def _slabs_spec(tb):
    return pl.BlockSpec((N_SLAB, tb, SLAB), lambda i: (0, i, 0))


def _read_slabs(ref):
    return jnp.concatenate([ref[k] for k in range(N_SLAB)], axis=1)


def _write_slabs(ref, value):
    for k in range(N_SLAB):
        ref[k] = value[:, k * SLAB:(k + 1) * SLAB]


def _token_block(s, most=512):
    for rows in (most, 512, 256):
        if rows <= most and s % rows == 0 and s > rows:
            return rows
    return s


def _time_chunk(s):
    return 512 if s % 512 == 0 and s > 512 else 256 if s % 256 == 0 else s


def _fwd_in(x, ln1_g, w_in_t):
    s = x.shape[0]
    tb = _token_block(s, 1024)

    def body(x_ref, g_ref, w_ref, up_ref, ul_ref, ug_ref):
        n, _, _ = _rms(x_ref[...], g_ref[...])
        proj = _mm_nt(n.astype(BF16), w_ref[...])
        _write_slabs(up_ref, proj[:, :POOL_WIDTH])
        _write_slabs(ul_ref, proj[:, POOL_WIDTH:POOL_WIDTH + LRU_WIDTH])
        _write_slabs(ug_ref, proj[:, POOL_WIDTH + LRU_WIDTH:])

    out = _slab_major(s)
    return pl.pallas_call(
        body, name="fwd_in", grid=(s // tb,),
        in_specs=[pl.BlockSpec((tb, D_MODEL), lambda i: (i, 0)),
                  pl.BlockSpec((1, D_MODEL), lambda i: (0, 0)),
                  pl.BlockSpec((IN_WIDTH, D_MODEL), lambda i: (0, 0))],
        out_specs=[_slabs_spec(tb)] * 3,
        out_shape=[out, out, out],
        compiler_params=_params(40, 1),
    )(*_in_hbm(x, ln1_g, w_in_t))


def _pool_denominator(t0, row, window):
    return jnp.minimum((t0 + row + 1).astype(F32), window)


def _causal_window(ext, deeper):
    s = ext + _down(ext, 1)
    s = s + deeper[0] * _down(s, 2)
    s = s + deeper[1] * _down(s, 4)
    s = s + deeper[2] * _down(s, 8)
    return s[HALO:]


def _anticausal_window(ext, deeper, rows):
    s = ext + _up(ext, 1)
    s = s + deeper[0] * _up(s, 2)
    s = s + deeper[1] * _up(s, 4)
    s = s + deeper[2] * _up(s, 8)
    return s[:rows]


def _conv(taps, cw, cb):
    return cw[3] * taps[0] + cw[2] * taps[1] + cw[1] * taps[2] + cw[0] * taps[3] + cb


def _lru_gates(xc, wa, ba, wi, bi):
    xb = xc.astype(BF16)
    r = _sigmoid(_mm(xb, wa) + ba)
    i = _sigmoid(_mm(xb, wi) + bi)
    return xb, r, i


def _lru_decay(r, sp):
    la = (-LRU_C) * r * sp
    a = jnp.exp(la)
    return a, jnp.sqrt(jnp.maximum(_one_minus_square(la, a), 0.0))


SUBLANES = 8


def _scan_pads(a_pad, b_pad, rows, causal):
    spare = slice(0, SUBLANES) if causal else slice(rows, rows + SUBLANES)
    a_pad[spare, :] = jnp.ones((SUBLANES, SLAB), F32)
    b_pad[spare, :] = jnp.zeros((SUBLANES, SLAB), F32)


def _scan_causal(a, b, h_prev, a_pad, b_pad, rows):
    d = 1
    while d < min(SUBLANES, rows):
        a_pad[SUBLANES:, :] = a
        b_pad[SUBLANES:, :] = b
        b = a * b_pad[SUBLANES - d:SUBLANES - d + rows, :] + b
        a = a * a_pad[SUBLANES - d:SUBLANES - d + rows, :]
        d *= 2
    h = jnp.broadcast_to(h_prev, (d, SLAB))
    out = []
    for at in range(0, rows, d):
        h = a[at:at + d] * h + b[at:at + d]
        out.append(h)
    return jnp.concatenate(out, axis=0)


def _scan_anticausal(a, b, l_next, a_pad, b_pad, rows):
    d = 1
    while d < min(SUBLANES, rows):
        a_pad[:rows, :] = a
        b_pad[:rows, :] = b
        b = a * b_pad[d:d + rows, :] + b
        a = a * a_pad[d:d + rows, :]
        d *= 2
    lead = jnp.broadcast_to(l_next, (d, SLAB))
    out = []
    for at in range(rows - d, -1, -d):
        lead = a[at:at + d] * lead + b[at:at + d]
        out.append(lead)
    return jnp.concatenate(out[::-1], axis=0)


def _slab_scalars():
    slab = pl.program_id(0)
    deeper = [jnp.where(slab > k, 1.0, 0.0).astype(F32) for k in range(N_SLAB - 1)]
    window = jnp.left_shift(jnp.int32(2), slab).astype(F32)
    inverse = jnp.where(slab == 0, 0.5, jnp.where(slab == 1, 0.25, jnp.where(slab == 2, 0.125, 0.0625))).astype(F32)
    return deeper, (window, inverse)


def _window_mean(total, t0, row, window, at_start):
    if at_start:
        return total / _pool_denominator(t0, row, window[0])
    return total * window[1]


def _slab_specs(s):
    seq = pl.BlockSpec((None, s, SLAB), lambda k: (k, 0, 0))
    mat = pl.BlockSpec((1, SLAB, SLAB), lambda k: (k, 0, 0))
    vec = pl.BlockSpec((1, SLAB), lambda k: (0, k))
    taps = pl.BlockSpec((CONV_WIDTH, SLAB), lambda k: (0, k))
    return seq, mat, vec, taps


def _mixer_fwd(u_pool, u_lru, u_gate, pool_w, pool_scale, conv_w, conv_b, wa_bd, b_a, wi_bd, b_i, lam):
    s = u_pool.shape[1]
    tc = _time_chunk(s)
    n_chunks = s // tc

    def body(up_ref, ul_ref, ug_ref, pw_ref, ps_ref, cw_ref, cb_ref, wa_ref, ba_ref, wi_ref, bi_ref, lam_ref,
             yp_ref, h_ref, yl_ref, a_ref, mult_ref, shift_ref, a_pad, b_pad):
        _scan_pads(a_pad, b_pad, tc, causal=True)
        deeper, window = _slab_scalars()
        pw = pw_ref[0].astype(BF16)
        wa = wa_ref[0].astype(BF16)
        wi = wi_ref[0].astype(BF16)
        ps, cb, ba, bi = ps_ref[...], cb_ref[...], ba_ref[...], bi_ref[...]
        cw = [cw_ref[k:k + 1, :] for k in range(CONV_WIDTH)]
        sp = _softplus_neg(lam_ref[...])
        row = lax.broadcasted_iota(jnp.int32, (tc, SLAB), 0)

        def chunk(t0, ext_p, ext_l, h_prev, at_start=False):
            rows = pl.ds(t0, tc)
            d = _window_mean(_causal_window(ext_p, deeper), t0, row, window, at_start) - ext_p[HALO:]
            yp_ref[rows, :] = _mm(d.astype(BF16), pw) * ps
            shift_ref[...] = ext_l
            xc = _conv([shift_ref[HALO - k:HALO - k + tc, :] for k in range(CONV_WIDTH)], cw, cb)
            _, r, i = _lru_gates(xc, wa, ba, wi, bi)
            a, mult = _lru_decay(r, sp)
            a_ref[rows, :] = a
            mult_ref[rows, :] = mult
            h = _scan_causal(a, mult * (i * xc), h_prev, a_pad, b_pad, tc)
            h_ref[rows, :] = h
            yl_ref[rows, :] = h * _gelu(ug_ref[rows, :])[0]
            return h[tc - 1:tc, :]

        pad = jnp.zeros((HALO, SLAB), F32)
        h0 = chunk(0, jnp.concatenate([pad, up_ref[pl.ds(0, tc), :]], axis=0),
                   jnp.concatenate([pad, ul_ref[pl.ds(0, tc), :]], axis=0), jnp.zeros((1, SLAB), F32), at_start=True)

        def step(c, h_prev):
            t0 = pl.multiple_of(c * tc, tc)
            ext = pl.ds(pl.multiple_of(c * tc - HALO, HALO), tc + HALO)
            return chunk(t0, up_ref[ext, :], ul_ref[ext, :], h_prev)

        lax.fori_loop(1, n_chunks, step, h0)

    seq, mat, vec, taps = _slab_specs(s)
    out = _slab_major(s)
    return pl.pallas_call(
        body, name="mixer_fwd", grid=(N_SLAB,),
        in_specs=[seq, seq, seq, mat, vec, taps, vec, mat, vec, mat, vec, vec],
        out_specs=[seq] * 5, out_shape=[out] * 5,
        scratch_shapes=[pltpu.VMEM((tc + HALO, SLAB), F32), pltpu.VMEM((tc + SUBLANES, SLAB), F32),
                        pltpu.VMEM((tc + SUBLANES, SLAB), F32)],
        compiler_params=_params(48, 1),
    )(*_in_hbm(u_pool, u_lru, u_gate, pool_w, pool_scale, conv_w, conv_b, wa_bd, b_a, wi_bd, b_i, lam))


def _fwd_out(x, y_pool, y_lru, gn_pool_g, gn_lru_g, w_out, ln2_g):
    s = x.shape[0]
    tb = _token_block(s, 1024)

    def body(x_ref, yp_ref, yl_ref, gp_ref, gl_ref, w_ref, g2_ref, h1_ref, n2_ref):
        mp, _, _ = _rms(_read_slabs(yp_ref), gp_ref[...])
        ml, _, _ = _rms(_read_slabs(yl_ref), gl_ref[...])
        h1 = x_ref[...] + _mm(mp.astype(BF16), w_ref[:POOL_WIDTH, :]) + _mm(ml.astype(BF16), w_ref[POOL_WIDTH:, :])
        h1_ref[...] = h1
        n2_ref[...] = _rms(h1, g2_ref[...])[0].astype(BF16)

    row = pl.BlockSpec((tb, D_MODEL), lambda i: (i, 0))
    half = _slabs_spec(tb)
    return pl.pallas_call(
        body, name="fwd_out", grid=(s // tb,),
        in_specs=[row, half, half, pl.BlockSpec((1, 512), lambda i: (0, 0)), pl.BlockSpec((1, 512), lambda i: (0, 0)),
                  pl.BlockSpec((D_MODEL, D_MODEL), lambda i: (0, 0)), pl.BlockSpec((1, D_MODEL), lambda i: (0, 0))],
        out_specs=[row, row],
        out_shape=[pltpu.HBM((s, D_MODEL), F32), pltpu.HBM((s, D_MODEL), BF16)],
        compiler_params=_params(40, 1),
    )(*_in_hbm(x, y_pool, y_lru, gn_pool_g, gn_lru_g, w_out, ln2_g))


def _ffn_fwd(h1, n2, target, lnf_g, w_gate, w_up, w_down):
    s = h1.shape[0]
    tb = 512
    sub = 256
    n_ff = D_FF // FF_CHUNK

    def body(h1_ref, n2_ref, t_ref, gf_ref, wg_hbm, wu_hbm, wd_hbm,
             g_ref, u_ref, dh_ref, dhb_ref, dgf_ref, sq_ref, wg, wu, wd, sem):
        @pl.when(pl.program_id(0) == 0)
        def _():
            loads = [pltpu.make_async_copy(src, dst, sem.at[k])
                     for k, (src, dst) in enumerate(((wg_hbm, wg), (wu_hbm, wu), (wd_hbm, wd)))]
            for cp in loads:
                cp.start()
            for cp in loads:
                cp.wait()
            dgf_ref[...] = jnp.zeros_like(dgf_ref)
            sq_ref[...] = jnp.zeros_like(sq_ref)

        n2v = n2_ref[...]
        acc = jnp.zeros((tb, D_MODEL), F32)
        for c in range(n_ff):
            cols = slice(c * FF_CHUNK, (c + 1) * FF_CHUNK)
            g = _mm_nt(n2v, wg[cols, :])
            u = _mm_nt(n2v, wu[cols, :])
            g_ref[:, cols] = g.astype(BF16)
            u_ref[:, cols] = u.astype(BF16)
            act = g * jax.nn.sigmoid(g) * u
            acc = acc + _mm(act.astype(BF16), wd[cols, :])
        gf = gf_ref[...]
        for r in range(tb // sub):
            rows = slice(r * sub, (r + 1) * sub)
            y, xhat, rstd = _rms(h1_ref[rows, :] + acc[rows, :], gf)
            err = y - t_ref[rows, :]
            sq_ref[...] += jnp.sum(err * err, axis=0, keepdims=True)
            dh2, dgf = _rms_bwd(err * (1.0 / D_MODEL), xhat, rstd, gf)
            dgf_ref[...] += dgf
            dh_ref[rows, :] = dh2
            dhb_ref[rows, :] = dh2.astype(BF16)

    row = pl.BlockSpec((tb, D_MODEL), lambda i: (i, 0))
    ff = pl.BlockSpec((tb, D_FF), lambda i: (i, 0))
    vec = pl.BlockSpec((1, D_MODEL), lambda i: (0, 0))
    anyspace = pl.BlockSpec(memory_space=pl.ANY)
    return pl.pallas_call(
        body, name="ffn_fwd", grid=(s // tb,),
        in_specs=[row, row, row, vec, anyspace, anyspace, anyspace],
        out_specs=[ff, ff, row, row, vec, vec],
        out_shape=[pltpu.HBM((s, D_FF), BF16), pltpu.HBM((s, D_FF), BF16),
                   pltpu.HBM((s, D_MODEL), F32), pltpu.HBM((s, D_MODEL), BF16),
                   jax.ShapeDtypeStruct((1, D_MODEL), F32), jax.ShapeDtypeStruct((1, D_MODEL), F32)],
        scratch_shapes=[pltpu.VMEM((D_FF, D_MODEL), BF16), pltpu.VMEM((D_FF, D_MODEL), BF16),
                        pltpu.VMEM((D_FF, D_MODEL), BF16), pltpu.SemaphoreType.DMA((3,))],
        compiler_params=_params(60, 1),
    )(*_in_hbm(h1, n2, target, lnf_g, w_gate, w_up, w_down))


def _ffn_bwd(n2, dh2b, g, u, w_gate_t, w_up_t, w_down):
    s = n2.shape[0]
    tb = min(1024, s)
    n_ff = D_FF // FF_CHUNK
    n_tb = s // tb

    def body(n2_ref, dh_ref, g_ref, u_ref, wg_ref, wu_ref, wd_ref, dwg_ref, dwu_ref, dwd_ref, dn2_ref,
             dn2_acc, acc_g, acc_u, acc_d):
        j = pl.program_id(0)
        t = pl.program_id(1)
        rows = pl.ds(pl.multiple_of(t * tb, tb), tb)

        @pl.when(t == 0)
        def _():
            acc_g[...] = jnp.zeros_like(acc_g)
            acc_u[...] = jnp.zeros_like(acc_u)
            acc_d[...] = jnp.zeros_like(acc_d)

        @pl.when(j == 0)
        def _():
            dn2_acc[rows, :] = jnp.zeros((tb, D_MODEL), F32)

        n2v = n2_ref[...]
        dh = dh_ref[...]
        gv = g_ref[...].astype(F32)
        uv = u_ref[...].astype(F32)
        sg = jax.nn.sigmoid(gv)
        silu = gv * sg
        dact = _mm_nt(dh, wd_ref[...])
        dub = (dact * silu).astype(BF16)
        dgb = (dact * uv * (sg * (1.0 + gv * (1.0 - sg)))).astype(BF16)
        acc_d[...] += _mm_tn(dh, (silu * uv).astype(BF16))
        acc_g[...] += _mm_tn(n2v, dgb)
        acc_u[...] += _mm_tn(n2v, dub)
        dn2_acc[rows, :] += _mm(jnp.concatenate([dgb, dub], axis=1),
                                jnp.concatenate([wg_ref[...], wu_ref[...]], axis=0))

        @pl.when(t == n_tb - 1)
        def _():
            dwg_ref[...] = acc_g[...].T.astype(BF16)
            dwu_ref[...] = acc_u[...].T.astype(BF16)
            dwd_ref[...] = acc_d[...].T.astype(BF16)

        @pl.when(j == n_ff - 1)
        def _():
            dn2_ref[...] = dn2_acc[rows, :]

    row = pl.BlockSpec((tb, D_MODEL), lambda j, t: (t, 0))
    act = pl.BlockSpec((tb, FF_CHUNK), lambda j, t: (t, j))
    w_row = pl.BlockSpec((FF_CHUNK, D_MODEL), lambda j, t: (j, 0))
    last = pl.BlockSpec((tb, D_MODEL), lambda j, t: (jnp.where(j == n_ff - 1, t, 0), 0))
    grad = pltpu.HBM((D_FF, D_MODEL), BF16)
    chunk_acc = pltpu.VMEM((D_MODEL, FF_CHUNK), F32)
    return pl.pallas_call(
        body, name="ffn_bwd", grid=(n_ff, n_tb),
        in_specs=[row, row, act, act, w_row, w_row, w_row],
        out_specs=[w_row, w_row, w_row, last],
        out_shape=[grad, grad, grad, pltpu.HBM((s, D_MODEL), F32)],
        scratch_shapes=[pltpu.VMEM((s, D_MODEL), F32), chunk_acc, chunk_acc, chunk_acc],
        compiler_params=_params(56, 2),
    )(*_in_hbm(n2, dh2b, g, u, w_gate_t, w_up_t, w_down))


def _bwd_out(dn2, dh2, h1, y_pool, y_lru, gn_pool_g, gn_lru_g, w_out, ln2_g, after):
    s = h1.shape[0]
    tb = _token_block(s)

    def body(dn2_ref, dh2_ref, h1_ref, yp_ref, yl_ref, gp_ref, gl_ref, w_ref, g2_ref, _after,
             dh1_ref, dyp_ref, dyl_ref, dwb_ref, dg2_ref, dgp_ref, dgl_ref, dw_ref):
        @pl.when(pl.program_id(0) == 0)
        def _():
            dw_ref[...] = jnp.zeros_like(dw_ref)
            dg2_ref[...] = jnp.zeros_like(dg2_ref)
            dgp_ref[...] = jnp.zeros_like(dgp_ref)
            dgl_ref[...] = jnp.zeros_like(dgl_ref)

        g2 = g2_ref[...]
        _, xhat2, rstd2 = _rms(h1_ref[...], g2)
        dres, dg2 = _rms_bwd(dn2_ref[...], xhat2, rstd2, g2)
        dg2_ref[...] += dg2
        dh1 = dh2_ref[...] + dres
        dh1_ref[...] = dh1
        dh1b = dh1.astype(BF16)
        gp, gl = gp_ref[...], gl_ref[...]
        mp, xhat_p, rstd_p = _rms(_read_slabs(yp_ref), gp)
        ml, xhat_l, rstd_l = _rms(_read_slabs(yl_ref), gl)
        dw_ref[:POOL_WIDTH, :] += _mm_tn(mp.astype(BF16), dh1b)
        dw_ref[POOL_WIDTH:, :] += _mm_tn(ml.astype(BF16), dh1b)
        dyp, dgp = _rms_bwd(_mm_nt(dh1b, w_ref[:POOL_WIDTH, :]), xhat_p, rstd_p, gp)
        dyl, dgl = _rms_bwd(_mm_nt(dh1b, w_ref[POOL_WIDTH:, :]), xhat_l, rstd_l, gl)
        _write_slabs(dyp_ref, dyp)
        _write_slabs(dyl_ref, dyl)
        dgp_ref[...] += dgp
        dgl_ref[...] += dgl

        @pl.when(pl.program_id(0) == s // tb - 1)
        def _():
            dwb_ref[...] = dw_ref[...].astype(BF16)

    row = pl.BlockSpec((tb, D_MODEL), lambda i: (i, 0))
    half = _slabs_spec(tb)
    vec = pl.BlockSpec((1, D_MODEL), lambda i: (0, 0))
    hvec = pl.BlockSpec((1, 512), lambda i: (0, 0))
    mat = pl.BlockSpec((D_MODEL, D_MODEL), lambda i: (0, 0))
    return pl.pallas_call(
        body, name="bwd_out", grid=(s // tb,),
        in_specs=[row, row, row, half, half, hvec, hvec, mat, vec, pl.BlockSpec(memory_space=pl.ANY)],
        out_specs=[row, half, half, mat, vec, hvec, hvec],
        out_shape=[pltpu.HBM((s, D_MODEL), F32), _slab_major(s),
                   _slab_major(s), pltpu.HBM((D_MODEL, D_MODEL), BF16),
                   jax.ShapeDtypeStruct((1, D_MODEL), F32), jax.ShapeDtypeStruct((1, 512), F32),
                   jax.ShapeDtypeStruct((1, 512), F32)],
        scratch_shapes=[pltpu.VMEM((D_MODEL, D_MODEL), F32)],
        compiler_params=_params(48, 1),
    )(*_in_hbm(dn2, dh2, h1, y_pool, y_lru, gn_pool_g, gn_lru_g, w_out, ln2_g), after)


def _mixer_bwd(u_pool, u_lru, u_gate, h, decay, root, dy_pool, dy_lru,
               pool_w, pool_scale, conv_w, conv_b, wa_bd, b_a, wi_bd, b_i, lam, after):
    s = u_pool.shape[1]
    tc = _time_chunk(s)
    n_chunks = s // tc

    def body(up_ref, ul_ref, ug_ref, h_ref, a_ref, mult_ref, dyp_ref, dyl_ref,
             pw_ref, ps_ref, cw_ref, cb_ref, wa_ref, ba_ref, wi_ref, bi_ref, lam_ref, _after,
             dup_ref, dul_ref, dug_ref, dpw_ref, dps_ref, dcw_ref, dcb_ref, dwa_ref, dba_ref, dwi_ref, dbi_ref, dlam_ref,
             a_pad, b_pad, u_shift, h_shift, a_shift, d_shift):
        _scan_pads(a_pad, b_pad, tc, causal=False)
        deeper, window = _slab_scalars()
        pw = pw_ref[0].astype(BF16)
        wa = wa_ref[0].astype(BF16)
        wi = wi_ref[0].astype(BF16)
        ps, cb, ba, bi = ps_ref[...], cb_ref[...], ba_ref[...], bi_ref[...]
        cw = [cw_ref[k:k + 1, :] for k in range(CONV_WIDTH)]
        lam_v = lam_ref[...]
        sp = _softplus_neg(lam_v)
        row = lax.broadcasted_iota(jnp.int32, (tc, SLAB), 0)
        for ref in (dpw_ref, dps_ref, dcw_ref, dcb_ref, dwa_ref, dba_ref, dwi_ref, dbi_ref, dlam_ref):
            ref[...] = jnp.zeros_like(ref)

        def chunk(t0, ext_p, ext_l, ext_h, carry, at_start=False):
            l_next, a_next, dxc_next, ddn_next = carry
            rows = pl.ds(t0, tc)
            u_shift[...] = ext_l
            taps = [u_shift[HALO - k:HALO - k + tc, :] for k in range(CONV_WIDTH)]
            xc = _conv(taps, cw, cb)
            xb, r, i = _lru_gates(xc, wa, ba, wi, bi)
            a, mult = a_ref[rows, :], mult_ref[rows, :]
            hv = ext_h[HALO:]
            h_shift[...] = ext_h
            h_before = h_shift[HALO - 1:HALO - 1 + tc, :]
            ug = ug_ref[rows, :]
            dyl = dyl_ref[rows, :]
            gel, th = _gelu(ug)
            dug_ref[rows, :] = (dyl * hv * _gelu_grad(ug, th)).astype(BF16)
            a_shift[:tc, :] = a
            a_shift[tc:, :] = jnp.broadcast_to(a_next, (SUBLANES, SLAB))
            a_after = a_shift[1:1 + tc, :]
            l = _scan_anticausal(a_after, dyl * gel, l_next, a_pad, b_pad, tc)
            dmult = l * (i * xc)
            di = l * mult * xc
            dxc = l * mult * i
            dla = (l * h_before) * a - jnp.where(mult > 0.0, dmult * (a * a) / mult, 0.0)
            dlam_ref[...] += jnp.sum(dla * r, axis=0, keepdims=True)
            dpa = (dla * ((-LRU_C) * sp)) * (r * (1.0 - r))
            dpi = di * (i * (1.0 - i))
            dpab = dpa.astype(BF16)
            dpib = dpi.astype(BF16)
            dwa_ref[0] += _mm_tn(xb, dpab)
            dwi_ref[0] += _mm_tn(xb, dpib)
            dba_ref[...] += jnp.sum(dpa, axis=0, keepdims=True)
            dbi_ref[...] += jnp.sum(dpi, axis=0, keepdims=True)
            dxc = dxc + _mm_nt(dpab, wa) + _mm_nt(dpib, wi)
            d_shift[:tc, :] = dxc
            d_shift[tc:, :] = dxc_next
            dul_ref[rows, :] = (cw[3] * dxc + cw[2] * d_shift[1:1 + tc, :]
                                + cw[1] * d_shift[2:2 + tc, :] + cw[0] * d_shift[3:3 + tc, :]).astype(BF16)
            for k in range(CONV_WIDTH):
                dcw_ref[k:k + 1, :] += jnp.sum(dxc * taps[CONV_WIDTH - 1 - k], axis=0, keepdims=True)
            dcb_ref[...] += jnp.sum(dxc, axis=0, keepdims=True)
            db = (_window_mean(_causal_window(ext_p, deeper), t0, row, window, at_start) - ext_p[HALO:]).astype(BF16)
            dyp = dyp_ref[rows, :]
            dps_ref[...] += jnp.sum(dyp * _mm(db, pw), axis=0, keepdims=True)
            dys = (dyp * ps).astype(BF16)
            dpw_ref[0] += _mm_tn(db, dys)
            dd = _mm_nt(dys, pw)
            ddn = _window_mean(dd, t0, row, window, at_start)
            ext_q = jnp.concatenate([ddn, ddn_next], axis=0)
            dup_ref[rows, :] = (_anticausal_window(ext_q, deeper, tc) - dd).astype(BF16)
            return l[0:1, :], a[0:1, :], dxc[0:8, :], ddn[0:HALO, :]

        def step(k, carry):
            c = n_chunks - 1 - k
            t0 = pl.multiple_of(c * tc, tc)
            ext = pl.ds(pl.multiple_of(c * tc - HALO, HALO), tc + HALO)
            return chunk(t0, up_ref[ext, :], ul_ref[ext, :], h_ref[ext, :], carry)

        carry = (jnp.zeros((1, SLAB), F32), jnp.zeros((1, SLAB), F32),
                 jnp.zeros((8, SLAB), F32), jnp.zeros((HALO, SLAB), F32))
        carry = lax.fori_loop(0, n_chunks - 1, step, carry)
        pad = jnp.zeros((HALO, SLAB), F32)
        first = pl.ds(0, tc)
        chunk(0, jnp.concatenate([pad, up_ref[first, :]], axis=0), jnp.concatenate([pad, ul_ref[first, :]], axis=0),
              jnp.concatenate([pad, h_ref[first, :]], axis=0), carry, at_start=True)
        dlam_ref[...] = dlam_ref[...] * (LRU_C * jax.nn.sigmoid(-lam_v))

    seq, mat, vec, taps = _slab_specs(s)
    grad = _slab_major(s, BF16)
    mats = jax.ShapeDtypeStruct((N_SLAB, SLAB, SLAB), F32)
    vecs = jax.ShapeDtypeStruct((1, 512), F32)
    return pl.pallas_call(
        body, name="mixer_bwd", grid=(N_SLAB,),
        in_specs=[seq] * 8 + [mat, vec, taps, vec, mat, vec, mat, vec, vec, pl.BlockSpec(memory_space=pl.ANY)],
        out_specs=[seq, seq, seq, mat, vec, taps, vec, mat, vec, mat, vec, vec],
        out_shape=[grad, grad, grad, mats, vecs, jax.ShapeDtypeStruct((CONV_WIDTH, 512), F32), vecs,
                   mats, vecs, mats, vecs, vecs],
        scratch_shapes=[pltpu.VMEM((tc + SUBLANES, SLAB), F32), pltpu.VMEM((tc + SUBLANES, SLAB), F32),
                        pltpu.VMEM((tc + HALO, SLAB), F32), pltpu.VMEM((tc + HALO, SLAB), F32),
                        pltpu.VMEM((tc + SUBLANES, SLAB), F32), pltpu.VMEM((tc + SUBLANES, SLAB), F32)],
        compiler_params=_params(56, 1),
    )(*_in_hbm(u_pool, u_lru, u_gate, h, decay, root, dy_pool, dy_lru, pool_w, pool_scale, conv_w, conv_b, wa_bd,
               b_a, wi_bd, b_i, lam), after)


def _bwd_in(x, dh1, du_pool, du_lru, du_gate, ln1_g, w_in_t, after):
    s = x.shape[0]
    tb = _token_block(s)

    def body(x_ref, dh1_ref, dup_ref, dul_ref, dug_ref, g_ref, w_ref, _after, dx_ref, dwb_ref, dg_ref, dw_ref):
        @pl.when(pl.program_id(0) == 0)
        def _():
            dw_ref[...] = jnp.zeros_like(dw_ref)
            dg_ref[...] = jnp.zeros_like(dg_ref)

        g1 = g_ref[...]
        n, xhat, rstd = _rms(x_ref[...], g1)
        nb = n.astype(BF16)
        db = jnp.concatenate([_read_slabs(ref) for ref in (dup_ref, dul_ref, dug_ref)], axis=1)
        dw_ref[...] += _mm_tn(db, nb)
        dx, dg1 = _rms_bwd(_mm(db, w_ref[...]), xhat, rstd, g1)
        dx_ref[...] = dh1_ref[...] + dx
        dg_ref[...] += dg1

        @pl.when(pl.program_id(0) == s // tb - 1)
        def _():
            dwb_ref[...] = dw_ref[...].astype(BF16)

    row = pl.BlockSpec((tb, D_MODEL), lambda i: (i, 0))
    half = _slabs_spec(tb)
    vec = pl.BlockSpec((1, D_MODEL), lambda i: (0, 0))
    mat = pl.BlockSpec((IN_WIDTH, D_MODEL), lambda i: (0, 0))
    return pl.pallas_call(
        body, name="bwd_in", grid=(s // tb,),
        in_specs=[row, row, half, half, half, vec, mat, pl.BlockSpec(memory_space=pl.ANY)],
        out_specs=[row, mat, vec],
        out_shape=[pltpu.HBM((s, D_MODEL), F32), pltpu.HBM((IN_WIDTH, D_MODEL), BF16),
                   jax.ShapeDtypeStruct((1, D_MODEL), F32)],
        scratch_shapes=[pltpu.VMEM((IN_WIDTH, D_MODEL), F32)],
        compiler_params=_params(48, 1),
    )(*_in_hbm(x, dh1, du_pool, du_lru, du_gate, ln1_g, w_in_t), after)


def _mesh_position():
    x, y, c = lax.axis_index("x"), lax.axis_index("y"), lax.axis_index("c")
    return x, y, c, 4 * x + 2 * y + c


def _peer(x, y, c, p):
    px = 1 - x if p & 4 else x
    py = 1 - y if p & 2 else y
    pc = 1 - c if p & 1 else c
    return (px, py, pc), 4 * px + 2 * py + pc


HBM_SPEC = pl.BlockSpec(memory_space=pltpu.HBM)
SEM_SPEC = pl.BlockSpec(memory_space=pltpu.SEMAPHORE)
DATAFLOW = pltpu.SideEffectType.DATAFLOW_SIDE_EFFECTING


class Copy(NamedTuple):
    src: int
    src_at: Any
    dst: int
    dst_at: Any
    peer: int
    group: int
    slot: int


SIBLING = (1,)
SAME_CORE = (2, 4, 6)
EVERYONE = tuple(range(1, N_DEV))
MYSELF = (0,)


def _same(index):
    return index


def _chip(index):
    return jnp.right_shift(index, 1)


def _fan_out(srcs, lands, peers, group):
    return [Copy(s, None, d, _same, p, group, N_DEV * i + p) for i, (s, d) in enumerate(zip(srcs, lands)) for p in peers]


def _scatter(stacks, lands, peers):
    return [Copy(s, lambda me, p=p: jnp.bitwise_xor(me, p), d, _same, p, 0, 0)
            for s, d in zip(stacks, lands) for p in peers]


def _numbered(copies, group=0):
    return [cp._replace(group=group, slot=i) for i, cp in enumerate(copies)]


def _relay(lands, peers):
    return [Copy(b, lambda s, q=q: jnp.bitwise_xor(s, q), b, lambda s, q=q: jnp.bitwise_xor(s, q), 1, 0, N_DEV * i + q)
            for i, b in enumerate(lands) for q in peers]


def _to_sibling(stacks, lands):
    return [Copy(s, lambda me, k=k: 2 * k + 1 - jnp.bitwise_and(me, 1), d, lambda me, k=k: k, 1, 0, 4 * i + k)
            for i, (s, d) in enumerate(zip(stacks, lands)) for k in range(N_DEV // 2)]


def _to_chips(sums, lands):
    return [Copy(s, lambda me, p=p: jnp.bitwise_xor(_chip(me), p // 2), d, _chip, p, 0, 4 * i + p // 2)
            for i, (s, d) in enumerate(zip(sums, lands)) for p in SAME_CORE]


def _comm_call(name, bufs, wait=None, start=None, after=()):
    nb = len(bufs)
    waits = [wait] if isinstance(wait, tuple) else list(wait or [])
    slots = list(start[1]) if start else []
    n_out_sem = 2 * len(slots)
    after = [a for a in (after if isinstance(after, (list, tuple)) else [after]) if a is not None]

    def body(*refs):
        b = refs[:nb]
        at = nb + 2 * len(waits) + len(after)
        out_sems = refs[at:at + n_out_sem]
        token = refs[at + n_out_sem + nb]
        x, y, c, me = _mesh_position()

        def part(i, row_of, sender):
            return b[i] if row_of is None else b[i].at[row_of(sender)]

        for k, (_, copies) in enumerate(waits):
            w_send, w_recv = refs[nb + 2 * k], refs[nb + 2 * k + 1]
            for cp in copies:
                peer, peer_index = _peer(x, y, c, cp.peer)
                arrival = pltpu.make_async_remote_copy(part(cp.src, cp.src_at, me), part(cp.dst, cp.dst_at, peer_index),
                                                       w_send.at[cp.slot], w_recv.at[cp.slot],
                                                       device_id=peer, device_id_type=MESH)
                arrival.wait_send()
                arrival.wait_recv()
        if start:
            for cp in start[0]:
                peer, _ = _peer(x, y, c, cp.peer)
                pltpu.make_async_remote_copy(part(cp.src, cp.src_at, me), part(cp.dst, cp.dst_at, me),
                                             out_sems[2 * cp.group].at[cp.slot], out_sems[2 * cp.group + 1].at[cp.slot],
                                             device_id=peer, device_id_type=MESH).start()
        token[...] = jnp.zeros_like(token)

    sem_shapes = []
    for n_slots in slots:
        sem_shapes += [pltpu.SemaphoreType.DMA((n_slots,))] * 2
    operands = [pltpu.with_memory_space_constraint(a, pltpu.HBM) for a in bufs]
    in_specs = [HBM_SPEC] * nb
    for sems, _ in waits:
        operands += list(sems)
        in_specs += [SEM_SPEC, SEM_SPEC]
    operands += after
    in_specs += [pl.BlockSpec(memory_space=pl.ANY)] * len(after)
    outs = pl.pallas_call(
        body, name=name, in_specs=in_specs,
        out_specs=[SEM_SPEC] * n_out_sem + [HBM_SPEC] * nb + [pl.BlockSpec(memory_space=pltpu.VMEM)],
        out_shape=sem_shapes + [pltpu.HBM(a.shape, a.dtype) for a in bufs] + [jax.ShapeDtypeStruct((8, SLAB), F32)],
        input_output_aliases={i: n_out_sem + i for i in range(nb)},
        compiler_params=pltpu.CompilerParams(has_side_effects=DATAFLOW),
    )(*operands)
    sems = [(outs[2 * k], outs[2 * k + 1]) for k in range(len(slots))]
    return sems, list(outs[n_out_sem:n_out_sem + nb]), outs[-1]


def _pair_sum(stacks, lands, place):
    n = len(stacks)

    def body(place_ref, *refs):
        k = pl.program_id(0)
        for m in range(n):
            mine, theirs, out, land = refs[m], refs[n + m], refs[2 * n + m], refs[3 * n + m]
            total = (mine[0, 0].astype(F32) + theirs[0].astype(F32)).astype(out.dtype)
            out[0] = total

            @pl.when(k == place_ref[1])
            def _():
                land[0] = total

    in_specs = [pl.BlockSpec((1, 1) + a.shape[1:], lambda k, place_ref: (k, place_ref[0], 0, 0)) for a in stacks]
    in_specs += [pl.BlockSpec((1,) + a.shape[1:], lambda k, place_ref: (k, 0, 0)) for a in lands]
    out_specs = [pl.BlockSpec((1,) + a.shape[1:], lambda k, place_ref: (k, 0, 0)) for a in lands]
    out_specs += [pl.BlockSpec((1,) + a.shape[1:], lambda k, place_ref: (place_ref[1], 0, 0)) for a in lands]
    outs = pl.pallas_call(
        body, name="pair_sum_" + "_".join(str(a.shape[1]) for a in stacks),
        grid_spec=pltpu.PrefetchScalarGridSpec(num_scalar_prefetch=1, grid=(N_DEV // 2,), in_specs=in_specs,
                                               out_specs=out_specs),
        out_shape=[pltpu.HBM(a.shape, a.dtype) for a in lands] * 2,
        compiler_params=_params(40, 1),
    )(place, *_in_hbm(*[a.reshape((N_DEV // 2, 2) + a.shape[1:]) for a in stacks], *lands))
    return list(outs[:n]), list(outs[n:])


def _reduce_adam(parts, w, m, v, name):
    rows, cols = w.shape
    n_parts = parts.shape[0]
    rb = rows
    for cand in (256, 176, 128):
        if rows % cand == 0 and rows > cand:
            rb = cand
            break

    def body(p_ref, w_ref, m_ref, v_ref, g_out, d_out, m_out, v_out):
        g = p_ref[0].astype(F32)
        for j in range(1, n_parts):
            g = g + p_ref[j].astype(F32)
        g_out[...] = g
        d_out[...], m_out[...], v_out[...] = _adam(g, w_ref[...], m_ref[...], v_ref[...])

    blk = pl.BlockSpec((rb, cols), lambda i: (i, 0))
    out = jax.ShapeDtypeStruct((rows, cols), F32)
    return pl.pallas_call(
        body, name=name, grid=(rows // rb,),
        in_specs=[pl.BlockSpec((n_parts, rb, cols), lambda i: (0, i, 0)), blk, blk, blk],
        out_specs=[blk] * 4, out_shape=[out] * 4,
        compiler_params=_params(40, 1),
    )(*_in_hbm(parts, w, m, v))


def _cols_from_stack(stack):
    n, r, c = stack.shape
    return jnp.transpose(stack, (1, 0, 2)).reshape(r, n * c)


def _block_diag(w):
    z = jnp.zeros((N_SLAB, 64, 64), w.dtype)
    pairs = w.reshape(N_SLAB, 2, 64, 64)
    top = jnp.concatenate([pairs[:, 0], z], axis=2)
    bottom = jnp.concatenate([z, pairs[:, 1]], axis=2)
    return jnp.concatenate([top, bottom], axis=1)


def _adam(g, w, m, v):
    m_new = ADAM_B1 * m + (1.0 - ADAM_B1) * g
    v_new = ADAM_B2 * v + (1.0 - ADAM_B2) * (g * g)
    m_hat = m_new / (1.0 - ADAM_B1 ** ADAM_STEP)
    v_hat = v_new / (1.0 - ADAM_B2 ** ADAM_STEP)
    return (-ADAM_LR) * (m_hat / (jnp.sqrt(v_hat) + ADAM_EPS) + ADAM_WD * w), m_new, v_new


WIDE = ("ln2_g", "lnf_g")
HALF = ("pool_scale", "conv_b", "b_a", "b_i", "lam", "gn_pool_g", "gn_lru_g")
VECTORS = [(k, D_MODEL) for k in WIDE] + [(k, 512) for k in HALF]
VECTOR_ROWS = sum(width // SLAB for _, width in VECTORS)
LOSS_ROW = -(-VECTOR_ROWS // 8) * 8
CONV_AT = LOSS_ROW + 8
CONV_LANES = LRU_WIDTH // SLAB
PACK_F_ROWS = CONV_AT + CONV_WIDTH * CONV_LANES
MATRIX_ROWS = N_SLAB * SLAB
HEAD = SLAB // 2
GATE_ROWS = N_SLAB * HEAD
PACK_B_ROWS = MATRIX_ROWS + 2 * GATE_ROWS


def _pack_small(vectors, pool_g, wa_g, wi_g, conv, sq):
    n_vec = len(vectors)

    def body(*refs):
        vec = refs[:n_vec]
        pw_ref, wa_ref, wi_ref, cw_ref, sq_ref, out, out_b = refs[n_vec:]
        out[...] = jnp.zeros_like(out)
        row = 0
        for ref, (_, width) in zip(vec, VECTORS):
            for k in range(width // SLAB):
                out[row:row + 1, :] = ref[:, k * SLAB:(k + 1) * SLAB]
                row += 1
        for tap in range(CONV_WIDTH):
            for k in range(CONV_LANES):
                at = CONV_AT + tap * CONV_LANES + k
                out[at:at + 1, :] = cw_ref[tap:tap + 1, k * SLAB:(k + 1) * SLAB]
        total = sq_ref[:, 0:SLAB]
        for k in range(1, D_MODEL // SLAB):
            total = total + sq_ref[:, k * SLAB:(k + 1) * SLAB]
        out[LOSS_ROW:LOSS_ROW + 1, :] = total
        left = lax.broadcasted_iota(jnp.int32, (HEAD, SLAB), 1) < HEAD
        for s in range(N_SLAB):
            out_b[s * SLAB:(s + 1) * SLAB, :] = pw_ref[s].astype(BF16)
            for i, ref in enumerate((wa_ref, wi_ref)):
                at = MATRIX_ROWS + i * GATE_ROWS + s * HEAD
                out_b[at:at + HEAD, :] = jnp.where(left, ref[s, 0:HEAD, :], ref[s, HEAD:SLAB, :]).astype(BF16)

    return pl.pallas_call(
        body, name="pack_small",
        out_shape=[jax.ShapeDtypeStruct((PACK_F_ROWS, SLAB), F32), jax.ShapeDtypeStruct((PACK_B_ROWS, SLAB), BF16)],
    )(*vectors, pool_g, wa_g, wi_g, conv, sq)


def _small_reduce_adam(my_index, parts, parts_b, vec_w, vec_m, vec_v, pool_wmv, wa_wmv, wi_wmv, conv_wmv):
    n_vec = len(VECTORS)
    n_parts = parts.shape[0]

    def body(*refs):
        me_ref, p_ref, pb_ref = refs[0], refs[1], refs[2]
        refs = refs[1:]
        w_refs, m_refs, v_refs = (refs[2 + k * n_vec:2 + (k + 1) * n_vec] for k in range(3))
        at = 2 + 3 * n_vec
        pw_w, pw_m, pw_v = refs[at:at + 3]
        gates = (refs[at + 3:at + 6], refs[at + 6:at + 9])
        cv_w, cv_m, cv_v = refs[at + 9:at + 12]
        outs = refs[at + 12:-1]
        total = refs[-1]
        total[...] = p_ref[0]
        for j in range(1, n_parts):
            total[...] += p_ref[j]
        row = 0
        for i, (_, width) in enumerate(VECTORS):
            n_rows = width // SLAB
            g = jnp.concatenate([total[row + k:row + k + 1, :] for k in range(n_rows)], axis=1)
            row += n_rows
            d, m_new, v_new = _adam(g, w_refs[i][...], m_refs[i][...], v_refs[i][...])
            for ref, val in zip(outs[4 * i:4 * i + 4], (g, d, m_new, v_new)):
                ref[...] = val
        tail = outs[4 * n_vec:]

        def summed(first, count):
            g = pb_ref[0, first:first + count, :].astype(F32)
            for j in range(1, n_parts):
                g = g + pb_ref[j, first:first + count, :].astype(F32)
            return g

        g = summed(0, MATRIX_ROWS)
        d, m_new, v_new = _adam(g, pw_w[...], pw_m[...], pw_v[...])
        for ref, val in zip(tail[0:4], (g, d, m_new, v_new)):
            ref[...] = val
        for i, (w, m, v) in enumerate(gates):
            g = summed(MATRIX_ROWS + i * GATE_ROWS, GATE_ROWS)
            for s in range(N_SLAB):
                for hd in range(2):
                    block = g[s * HEAD:(s + 1) * HEAD, hd * HEAD:(hd + 1) * HEAD]
                    rows = slice((2 * s + hd) * HEAD, (2 * s + hd + 1) * HEAD)
                    d, m_new, v_new = _adam(block, w[rows, :], m[rows, :], v[rows, :])
                    for ref, val in zip(tail[4 + 4 * i:8 + 4 * i], (block, d, m_new, v_new)):
                        ref[rows, :] = val
        me = me_ref[0, 0]
        taps = []
        for tap in range(CONV_WIDTH):
            row = total[pl.ds(CONV_AT + tap * CONV_LANES + me // 2, 1), :]
            taps.append(jnp.where(me % 2 == 0, row[:, :HEAD], row[:, HEAD:]))
        g = jnp.concatenate(taps, axis=0)
        d, m_new, v_new = _adam(g, cv_w[...], cv_m[...], cv_v[...])
        for ref, val in zip(tail[12:16], (g, d, m_new, v_new)):
            ref[...] = val
        tail[16][...] = (0.5 / D_MODEL) * jnp.sum(total[LOSS_ROW:LOSS_ROW + 1, :], axis=1, keepdims=True)

    out_shape = []
    for _, width in VECTORS:
        out_shape += [jax.ShapeDtypeStruct((1, width), F32)] * 4
    out_shape += [jax.ShapeDtypeStruct((MATRIX_ROWS, SLAB), F32)] * 4
    out_shape += [jax.ShapeDtypeStruct((2 * GATE_ROWS, HEAD), F32)] * 8
    out_shape += [jax.ShapeDtypeStruct(conv_wmv[0].shape, F32)] * 4 + [jax.ShapeDtypeStruct((1, 1), F32)]
    operands = [parts, parts_b, *vec_w, *vec_m, *vec_v, *pool_wmv, *wa_wmv, *wi_wmv, *conv_wmv]

    def whole(a):
        return pl.BlockSpec(a.shape, lambda i, n=len(a.shape): (0,) * n)

    outs = pl.pallas_call(
        body, name="adam_small", grid=(1,),
        in_specs=[pl.BlockSpec(memory_space=pltpu.SMEM)] + [whole(a) for a in operands],
        out_specs=[whole(a) for a in out_shape], out_shape=out_shape,
        scratch_shapes=[pltpu.VMEM((PACK_F_ROWS, SLAB), F32)],
        compiler_params=_params(40, 1),
    )(my_index, *_in_hbm(*operands))
    vec_out = [tuple(outs[4 * i:4 * i + 4]) for i in range(n_vec)]
    tail = outs[4 * n_vec:]
    return vec_out, tuple(tail[0:4]), tuple(tail[4:8]), tuple(tail[8:12]), tuple(tail[12:16]), tail[16]


def kernel(x, ln1_g, w_in, pool_w, pool_scale, conv_w, conv_b, w_a, b_a, w_i, b_i, lam, gn_pool_g, gn_lru_g, w_out, ln2_g, w_ffn_gate, w_ffn_up, w_ffn_down, lnf_g, loss_target, m_ln1_g, m_w_in, m_pool_w, m_pool_scale, m_conv_w, m_conv_b, m_w_a, m_b_a, m_w_i, m_b_i, m_lam, m_gn_pool_g, m_gn_lru_g, m_w_out, m_ln2_g, m_w_ffn_gate, m_w_ffn_up, m_w_ffn_down, m_lnf_g, v_ln1_g, v_w_in, v_pool_w, v_pool_scale, v_conv_w, v_conv_b, v_w_a, v_b_a, v_w_i, v_b_i, v_lam, v_gn_pool_g, v_gn_lru_g, v_w_out, v_ln2_g, v_w_ffn_gate, v_w_ffn_up, v_w_ffn_down, v_lnf_g):
    weights = dict(ln1_g=ln1_g, w_in=w_in, pool_w=pool_w, pool_scale=pool_scale, conv_w=conv_w, conv_b=conv_b,
                   w_a=w_a, b_a=b_a, w_i=w_i, b_i=b_i, lam=lam, gn_pool_g=gn_pool_g, gn_lru_g=gn_lru_g,
                   w_out=w_out, ln2_g=ln2_g, w_ffn_gate=w_ffn_gate, w_ffn_up=w_ffn_up, w_ffn_down=w_ffn_down,
                   lnf_g=lnf_g)
    mom1 = dict(ln1_g=m_ln1_g, w_in=m_w_in, pool_w=m_pool_w, pool_scale=m_pool_scale, conv_w=m_conv_w,
                conv_b=m_conv_b, w_a=m_w_a, b_a=m_b_a, w_i=m_w_i, b_i=m_b_i, lam=m_lam, gn_pool_g=m_gn_pool_g,
                gn_lru_g=m_gn_lru_g, w_out=m_w_out, ln2_g=m_ln2_g, w_ffn_gate=m_w_ffn_gate,
                w_ffn_up=m_w_ffn_up, w_ffn_down=m_w_ffn_down, lnf_g=m_lnf_g)
    mom2 = dict(ln1_g=v_ln1_g, w_in=v_w_in, pool_w=v_pool_w, pool_scale=v_pool_scale, conv_w=v_conv_w,
                conv_b=v_conv_b, w_a=v_w_a, b_a=v_b_a, w_i=v_w_i, b_i=v_b_i, lam=v_lam, gn_pool_g=v_gn_pool_g,
                gn_lru_g=v_gn_lru_g, w_out=v_w_out, ln2_g=v_ln2_g, w_ffn_gate=v_w_ffn_gate,
                w_ffn_up=v_w_ffn_up, w_ffn_down=v_w_ffn_down, lnf_g=v_lnf_g)

    xs = x[0]
    target = loss_target[0]

    shard = dict(w_in=lambda a: a[0].T, w_ffn_gate=lambda a: a[0].T, w_ffn_up=lambda a: a[0].T,
                 w_out=lambda a: a[0], w_ffn_down=lambda a: a[0], conv_w=lambda a: a[0])
    unshard = dict(w_in=lambda a: a.T[None], w_ffn_gate=lambda a: a.T[None], w_ffn_up=lambda a: a.T[None],
                   w_out=lambda a: a[None], w_ffn_down=lambda a: a[None], conv_w=lambda a: a[None])

    gathered = ("w_in", "conv_w", "w_out", "w_ffn_gate", "w_ffn_up", "w_ffn_down")
    groups = ((0, 1), (2,), (3, 4), (5,))
    sources = [shard[k](weights[k]) if k == "conv_w" else shard[k](weights[k]).astype(BF16) for k in gathered]
    my_index = 4 * lax.axis_index("x") + 2 * lax.axis_index("y") + lax.axis_index("c")
    lands = [lax.empty((N_DEV,) + a.shape, a.dtype) for a in sources]

    def first_hop(n, at=0):
        return _numbered(_fan_out(range(at, at + n), range(at + n, at + 2 * n), SAME_CORE + SIBLING + MYSELF, 0))

    def second_hop(n, at=0):
        return _numbered(_relay(range(at, at + n), SAME_CORE))

    start = []
    for g, members in enumerate(groups):
        start += _numbered(_fan_out(members, [6 + m for m in members], SAME_CORE + SIBLING + MYSELF, 0), g)
    sems, bufs, _ = _comm_call("gather_start", sources + lands,
                               start=(start, [sum(cp.group == g for cp in start) for g in range(len(groups))]))
    sources, lands = bufs[:6], bufs[6:]

    def group_buffers(g):
        return [sources[m] for m in groups[g]] + [lands[m] for m in groups[g]]

    def relayed(tag, g, after):
        n = len(groups[g])
        relay = second_hop(n, n)
        relay_sems, bufs, _ = _comm_call("gather_relay_" + tag, group_buffers(g), wait=(sems[g], first_hop(n)),
                                         start=(relay, (len(relay),)), after=after)
        return relay_sems[0], bufs[n:]

    relay_sem, bufs = relayed("in", 0, None)
    g_in, g_conv = _comm_call("gather_wait_in", bufs, wait=(relay_sem, second_hop(2)))[1]
    w_in_f = g_in.reshape(IN_WIDTH, D_MODEL)
    conv_w_f = _cols_from_stack(g_conv)

    wa_bd = _block_diag(w_a[0])
    wi_bd = _block_diag(w_i[0])
    lnf_row = lnf_g.reshape(1, D_MODEL)

    u_pool, u_lru, u_gate = _fwd_in(xs, ln1_g, w_in_f)
    y_pool, h, y_lru, decay, root = _mixer_fwd(u_pool, u_lru, u_gate, pool_w[0], pool_scale, conv_w_f, conv_b,
                                               wa_bd, b_a, wi_bd, b_i, lam)
    relay_sem, bufs = relayed("out", 1, y_pool)
    gate_up = second_hop(2, 3)
    (gate_up_sem,), bufs, _ = _comm_call("gather_wait_out", bufs + group_buffers(2),
                                         wait=[(relay_sem, second_hop(1)), (sems[2], first_hop(2, 1))],
                                         start=(gate_up, (len(gate_up),)))
    g_out, gate_up_lands = bufs[0], bufs[3:]
    w_out_f = g_out.reshape(D_MODEL, D_MODEL)
    h1, n2 = _fwd_out(xs, y_pool, y_lru, gn_pool_g, gn_lru_g, w_out_f, ln2_g)
    relay_sem, bufs = relayed("down", 3, n2)
    g_gate, g_up, g_down = _comm_call("gather_wait_ffn", gate_up_lands + bufs,
                                      wait=[(gate_up_sem, second_hop(2)), (relay_sem, second_hop(1, 2))])[1]
    w_gate_f = g_gate.reshape(D_FF, D_MODEL)
    w_up_f = g_up.reshape(D_FF, D_MODEL)
    w_down_f = g_down.reshape(D_FF, D_MODEL)
    g_act, u_act, dh2, dh2b, d_lnf, sq = _ffn_fwd(h1, n2, target, lnf_row, w_gate_f, w_up_f, w_down_f)

    def direct(tag, stacks, wholes, after):
        sources = list(stacks) + list(wholes)
        n, n_st = len(sources), len(stacks)
        lands = [lax.empty(a.shape if i < n_st else (N_DEV,) + a.shape, a.dtype) for i, a in enumerate(sources)]
        copies = _numbered(_scatter(range(n_st), range(n, n + n_st), EVERYONE + MYSELF)
                           + _fan_out(range(n_st, n), range(n + n_st, 2 * n), EVERYONE + MYSELF, 0))
        sem, bufs, token = _comm_call(tag + "_start", sources + lands, start=(copies, (len(copies),)), after=after)
        return (tag, sem[0], bufs, copies), token

    def direct_finish(started, after):
        tag, sem, bufs, copies = started
        _, bufs, _ = _comm_call(tag + "_wait", bufs, wait=(sem, copies), after=after)
        return bufs[len(bufs) // 2:]

    place = jnp.stack([lax.axis_index("c"), 2 * lax.axis_index("x") + lax.axis_index("y")]).astype(jnp.int32)

    d_gate, d_up, d_down, dn2 = _ffn_bwd(n2, dh2b, g_act, u_act, w_gate_f, w_up_f, w_down_f)
    down_stack, *ffn_stacks = [d.reshape(N_DEV, D_FF // N_DEV, D_MODEL) for d in (d_down, d_gate, d_up)]
    pair_lands = [lax.empty((N_DEV // 2,) + a.shape[1:], a.dtype) for a in ffn_stacks]
    pair_copies = _numbered(_to_sibling(range(2), range(2, 4)))
    down_copies = _numbered(_scatter([4], [5], EVERYONE + MYSELF), 1)
    (sem, down_sem), bufs, token = _comm_call(
        "ffn_grads_start", ffn_stacks + pair_lands + [down_stack, lax.empty(down_stack.shape, down_stack.dtype)],
        start=(pair_copies + down_copies, (len(pair_copies), len(down_copies))))
    bufs, down_bufs = bufs[:4], bufs[4:]
    dh1, dy_pool, dy_lru, d_out, d_ln2, d_gnp, d_gnl = _bwd_out(dn2, dh2, h1, y_pool, y_lru, gn_pool_g, gn_lru_g,
                                                                 w_out_f, ln2_g, token)
    _, bufs, _ = _comm_call("ffn_pair_wait", bufs, wait=(sem, pair_copies), after=dh1)
    ffn_sums, ffn_lands = _pair_sum(bufs[:2], bufs[2:], place)
    ffn_copies = _numbered(_to_chips(range(2), range(2, 4)))
    ffn_sem, ffn_bufs, token = _comm_call("ffn_chip_start", ffn_sums + ffn_lands, start=(ffn_copies, (len(ffn_copies),)))
    (du_pool, du_lru, du_gate, d_pw, d_ps, d_cw, d_cb, d_wa, d_ba, d_wi, d_bi, d_lam) = _mixer_bwd(
        u_pool, u_lru, u_gate, h, decay, root, dy_pool, dy_lru, pool_w[0], pool_scale, conv_w_f, conv_b,
        wa_bd, b_a, wi_bd, b_i, lam, token)

    vec_grads = dict(ln2_g=d_ln2, lnf_g=d_lnf, pool_scale=d_ps, conv_b=d_cb, b_a=d_ba, b_i=d_bi,
                     lam=d_lam, gn_pool_g=d_gnp, gn_lru_g=d_gnl)
    packed, packed_b = _pack_small([vec_grads[k] for k, _ in VECTORS], d_pw, d_wa, d_wi, d_cw, sq)
    small_started, token = direct("small", [d_out.reshape(N_DEV, D_MODEL // N_DEV, D_MODEL)], [packed, packed_b], None)
    grad_x, d_in, d_ln1 = _bwd_in(xs, dh1, du_pool, du_lru, du_gate, ln1_g, w_in_f, token)
    in_started, token = direct("in", [d_in.reshape(N_DEV, IN_WIDTH // N_DEV, D_MODEL)], [d_ln1], None)

    results = {}

    def reduce_adam(name, parts):
        outs = _reduce_adam(parts, shard[name](weights[name]), shard[name](mom1[name]), shard[name](mom2[name]),
                            "adam_" + name)
        results[name] = tuple(unshard[name](o) for o in outs)
        return outs[0]

    shard["ln1_g"] = unshard["ln1_g"] = lambda a: a
    _, ffn_bufs, _ = _comm_call("ffn_chip_wait", ffn_bufs, wait=(ffn_sem[0], ffn_copies), after=token)
    done = [reduce_adam(name, parts) for name, parts in zip(("w_ffn_gate", "w_ffn_up"), ffn_bufs[2:])]
    _, (_, r_down), _ = _comm_call("down_wait", down_bufs, after=done,
                                   wait=(down_sem, _numbered(_scatter([0], [1], EVERYONE + MYSELF))))
    done.append(reduce_adam("w_ffn_down", r_down))
    r_out, r_small, r_small_b = direct_finish(small_started, done)
    done.append(reduce_adam("w_out", r_out))

    def as_row(a, width):
        return a.reshape(1, width)

    def as_matrix(a):
        return a.reshape(MATRIX_ROWS, SLAB)

    def as_heads(a):
        return a.reshape(2 * GATE_ROWS, HEAD)

    states = (weights, mom1, mom2)
    vec_out, pool_out, wa_out, wi_out, conv_out, loss_11 = _small_reduce_adam(
        my_index.astype(jnp.int32).reshape(1, 1), r_small, r_small_b,
        *[[as_row(t[k], w) for k, w in VECTORS] for t in states], [as_matrix(t["pool_w"]) for t in states],
        [as_heads(t["w_a"]) for t in states], [as_heads(t["w_i"]) for t in states], [t["conv_w"][0] for t in states])
    for (k, _), outs in zip(VECTORS, vec_out):
        results[k] = tuple(o.reshape(weights[k].shape) for o in outs)
    for k, outs in (("pool_w", pool_out), ("w_a", wa_out), ("w_i", wi_out), ("conv_w", conv_out)):
        results[k] = tuple(o.reshape(weights[k].shape) for o in outs)
    loss = loss_11[0, 0]
    r_in, r_ln1 = direct_finish(in_started, done + [loss_11])
    reduce_adam("w_in", r_in)
    reduce_adam("ln1_g", r_ln1)

    order = ["ln1_g", "w_in", "pool_w", "pool_scale", "conv_w", "conv_b", "w_a", "b_a", "w_i", "b_i", "lam",
             "gn_pool_g", "gn_lru_g", "w_out", "ln2_g", "w_ffn_gate", "w_ffn_up", "w_ffn_down", "lnf_g"]
    return (loss, grad_x[None],
            *[results[k][0] for k in order], *[results[k][1] for k in order],
            *[results[k][2] for k in order], *[results[k][3] for k in order])
```

```python
from typing import Any, NamedTuple

import jax
import jax.numpy as jnp
from jax import lax
from jax.experimental import pallas as pl
from jax.experimental.pallas import tpu as pltpu

F32 = jnp.float32
BF16 = jnp.bfloat16

N_DEV = 8
D_MODEL = 1024
POOL_WIDTH = 512
LRU_WIDTH = 512
IN_WIDTH = 1536
D_FF = 2816
N_SLAB = 4
SLAB = 128
CONV_WIDTH = 4
LRU_C = 8.0
EPS = 1e-6
HALO = 16
FF_CHUNK = 256

ADAM_LR = 0.001
ADAM_B1 = 0.9
ADAM_B2 = 0.999
ADAM_EPS = 1e-08
ADAM_WD = 0.01
ADAM_STEP = 10

MIB = 1 << 20
MESH = pl.DeviceIdType.MESH


def _params(vmem_mib, n_axes=0):
    sem = ("arbitrary",) * n_axes if n_axes else None
    return pltpu.CompilerParams(dimension_semantics=sem, vmem_limit_bytes=vmem_mib * MIB)


def _in_hbm(*arrays):
    return [pltpu.with_memory_space_constraint(a, pltpu.HBM) for a in arrays]


def _mm(a, b):
    return jnp.dot(a, b, preferred_element_type=F32)


def _mm_nt(a, b):
    return lax.dot_general(a, b, (((1,), (1,)), ((), ())), preferred_element_type=F32)


def _mm_tn(a, b):
    return lax.dot_general(a, b, (((0,), (0,)), ((), ())), preferred_element_type=F32)


def _rms(x, g):
    rstd = lax.rsqrt(jnp.mean(x * x, axis=-1, keepdims=True) + EPS)
    xhat = x * rstd
    return xhat * g, xhat, rstd


def _rms_bwd(dy, xhat, rstd, g):
    gy = dy * g
    dx = rstd * (gy - xhat * jnp.mean(gy * xhat, axis=-1, keepdims=True))
    return dx, jnp.sum(dy * xhat, axis=0, keepdims=True)


def _gelu(z):
    t = jnp.tanh(0.7978845608028654 * (z + 0.044715 * z * z * z))
    return 0.5 * z * (1.0 + t), t


def _gelu_grad(z, t):
    return 0.5 * (1.0 + t) + 0.5 * z * (1.0 - t * t) * 0.7978845608028654 * (1.0 + 3.0 * 0.044715 * z * z)


def _softplus_neg(lam):
    x = -lam
    e = jnp.exp(-jnp.abs(x))
    u = 1.0 + e
    l1p = jnp.where(u == 1.0, e, jnp.log(u) * e / (u - 1.0))
    return jnp.maximum(x, 0.0) + l1p


def _one_minus_square(la, a):
    x = 2.0 * la
    series = -x * (1.0 + x * (0.5 + x * (1.0 / 6.0 + x * (1.0 / 24.0))))
    return jnp.where(x > -0.06, series, 1.0 - a * a)


def _sigmoid(x):
    return 0.5 * jnp.tanh(0.5 * x) + 0.5


def _down(v, d):
    return pltpu.roll(v, d, 0)


def _up(v, d):
    return pltpu.roll(v, v.shape[0] - d, 0)


def _slab_major(s, dtype=F32):
    return pltpu.HBM((N_SLAB, s, SLAB), dtype)


def _slabs_spec(tb):
    return pl.BlockSpec((N_SLAB, tb, SLAB), lambda i: (0, i, 0))


def _read_slabs(ref):
    return jnp.concatenate([ref[k] for k in range(N_SLAB)], axis=1)


def _write_slabs(ref, value):
    for k in range(N_SLAB):
        ref[k] = value[:, k * SLAB:(k + 1) * SLAB]


def _token_block(s, most=512):
    for rows in (most, 512, 256):
        if rows <= most and s % rows == 0 and s > rows:
            return rows
    return s


def _time_chunk(s):
    return 512 if s % 512 == 0 and s > 512 else 256 if s % 256 == 0 else s


def _fwd_in(x, ln1_g, w_in_t):
    s = x.shape[0]
    tb = _token_block(s, 1024)

    def body(x_ref, g_ref, w_ref, up_ref, ul_ref, ug_ref):
        n, _, _ = _rms(x_ref[...], g_ref[...])
        proj = _mm_nt(n.astype(BF16), w_ref[...])
        _write_slabs(up_ref, proj[:, :POOL_WIDTH])
        _write_slabs(ul_ref, proj[:, POOL_WIDTH:POOL_WIDTH + LRU_WIDTH])
        _write_slabs(ug_ref, proj[:, POOL_WIDTH + LRU_WIDTH:])

    out = _slab_major(s)
    return pl.pallas_call(
        body, name="fwd_in", grid=(s // tb,),
        in_specs=[pl.BlockSpec((tb, D_MODEL), lambda i: (i, 0)),
                  pl.BlockSpec((1, D_MODEL), lambda i: (0, 0)),
                  pl.BlockSpec((IN_WIDTH, D_MODEL), lambda i: (0, 0))],
        out_specs=[_slabs_spec(tb)] * 3,
        out_shape=[out, out, out],
        compiler_params=_params(40, 1),
    )(*_in_hbm(x, ln1_g, w_in_t))


def _pool_denominator(t0, row, window):
    return jnp.minimum((t0 + row + 1).astype(F32), window)


def _causal_window(ext, deeper):
    s = ext + _down(ext, 1)
    s = s + deeper[0] * _down(s, 2)
    s = s + deeper[1] * _down(s, 4)
    s = s + deeper[2] * _down(s, 8)
    return s[HALO:]


def _anticausal_window(ext, deeper, rows):
    s = ext + _up(ext, 1)
    s = s + deeper[0] * _up(s, 2)
    s = s + deeper[1] * _up(s, 4)
    s = s + deeper[2] * _up(s, 8)
    return s[:rows]


def _conv(taps, cw, cb):
    return cw[3] * taps[0] + cw[2] * taps[1] + cw[1] * taps[2] + cw[0] * taps[3] + cb


def _lru_gates(xc, wa, ba, wi, bi):
    xb = xc.astype(BF16)
    r = _sigmoid(_mm(xb, wa) + ba)
    i = _sigmoid(_mm(xb, wi) + bi)
    return xb, r, i


def _lru_decay(r, sp):
    la = (-LRU_C) * r * sp
    a = jnp.exp(la)
    return a, jnp.sqrt(jnp.maximum(_one_minus_square(la, a), 0.0))


SUBLANES = 8


def _scan_pads(a_pad, b_pad, rows, causal):
    spare = slice(0, SUBLANES) if causal else slice(rows, rows + SUBLANES)
    a_pad[spare, :] = jnp.ones((SUBLANES, SLAB), F32)
    b_pad[spare, :] = jnp.zeros((SUBLANES, SLAB), F32)


def _scan_causal(a, b, h_prev, a_pad, b_pad, rows):
    d = 1
    while d < min(SUBLANES, rows):
        a_pad[SUBLANES:, :] = a
        b_pad[SUBLANES:, :] = b
        b = a * b_pad[SUBLANES - d:SUBLANES - d + rows, :] + b
        a = a * a_pad[SUBLANES - d:SUBLANES - d + rows, :]
        d *= 2
    h = jnp.broadcast_to(h_prev, (d, SLAB))
    out = []
    for at in range(0, rows, d):
        h = a[at:at + d] * h + b[at:at + d]
        out.append(h)
    return jnp.concatenate(out, axis=0)


def _scan_anticausal(a, b, l_next, a_pad, b_pad, rows):
    d = 1
    while d < min(SUBLANES, rows):
        a_pad[:rows, :] = a
        b_pad[:rows, :] = b
        b = a * b_pad[d:d + rows, :] + b
        a = a * a_pad[d:d + rows, :]
        d *= 2
    lead = jnp.broadcast_to(l_next, (d, SLAB))
    out = []
    for at in range(rows - d, -1, -d):
        lead = a[at:at + d] * lead + b[at:at + d]
        out.append(lead)
    return jnp.concatenate(out[::-1], axis=0)


def _slab_scalars():
    slab = pl.program_id(0)
    deeper = [jnp.where(slab > k, 1.0, 0.0).astype(F32) for k in range(N_SLAB - 1)]
    window = jnp.left_shift(jnp.int32(2), slab).astype(F32)
    inverse = jnp.where(slab == 0, 0.5, jnp.where(slab == 1, 0.25, jnp.where(slab == 2, 0.125, 0.0625))).astype(F32)
    return deeper, (window, inverse)


def _window_mean(total, t0, row, window, at_start):
    if at_start:
        return total / _pool_denominator(t0, row, window[0])
    return total * window[1]


def _slab_specs(s):
    seq = pl.BlockSpec((None, s, SLAB), lambda k: (k, 0, 0))
    mat = pl.BlockSpec((1, SLAB, SLAB), lambda k: (k, 0, 0))
    vec = pl.BlockSpec((1, SLAB), lambda k: (0, k))
    taps = pl.BlockSpec((CONV_WIDTH, SLAB), lambda k: (0, k))
    return seq, mat, vec, taps


def _mixer_fwd(u_pool, u_lru, u_gate, pool_w, pool_scale, conv_w, conv_b, wa_bd, b_a, wi_bd, b_i, lam):
    s = u_pool.shape[1]
    tc = _time_chunk(s)
    n_chunks = s // tc

    def body(up_ref, ul_ref, ug_ref, pw_ref, ps_ref, cw_ref, cb_ref, wa_ref, ba_ref, wi_ref, bi_ref, lam_ref,
             yp_ref, h_ref, yl_ref, a_ref, mult_ref, shift_ref, a_pad, b_pad):
        _scan_pads(a_pad, b_pad, tc, causal=True)
        deeper, window = _slab_scalars()
        pw = pw_ref[0].astype(BF16)
        wa = wa_ref[0].astype(BF16)
        wi = wi_ref[0].astype(BF16)
        ps, cb, ba, bi = ps_ref[...], cb_ref[...], ba_ref[...], bi_ref[...]
        cw = [cw_ref[k:k + 1, :] for k in range(CONV_WIDTH)]
        sp = _softplus_neg(lam_ref[...])
        row = lax.broadcasted_iota(jnp.int32, (tc, SLAB), 0)

        def chunk(t0, ext_p, ext_l, h_prev, at_start=False):
            rows = pl.ds(t0, tc)
            d = _window_mean(_causal_window(ext_p, deeper), t0, row, window, at_start) - ext_p[HALO:]
            yp_ref[rows, :] = _mm(d.astype(BF16), pw) * ps
            shift_ref[...] = ext_l
            xc = _conv([shift_ref[HALO - k:HALO - k + tc, :] for k in range(CONV_WIDTH)], cw, cb)
            _, r, i = _lru_gates(xc, wa, ba, wi, bi)
            a, mult = _lru_decay(r, sp)
            a_ref[rows, :] = a
            mult_ref[rows, :] = mult
            h = _scan_causal(a, mult * (i * xc), h_prev, a_pad, b_pad, tc)
            h_ref[rows, :] = h
            yl_ref[rows, :] = h * _gelu(ug_ref[rows, :])[0]
            return h[tc - 1:tc, :]

        pad = jnp.zeros((HALO, SLAB), F32)
        h0 = chunk(0, jnp.concatenate([pad, up_ref[pl.ds(0, tc), :]], axis=0),
                   jnp.concatenate([pad, ul_ref[pl.ds(0, tc), :]], axis=0), jnp.zeros((1, SLAB), F32), at_start=True)

        def step(c, h_prev):
            t0 = pl.multiple_of(c * tc, tc)
            ext = pl.ds(pl.multiple_of(c * tc - HALO, HALO), tc + HALO)
            return chunk(t0, up_ref[ext, :], ul_ref[ext, :], h_prev)

        lax.fori_loop(1, n_chunks, step, h0)

    seq, mat, vec, taps = _slab_specs(s)
    out = _slab_major(s)
    return pl.pallas_call(
        body, name="mixer_fwd", grid=(N_SLAB,),
        in_specs=[seq, seq, seq, mat, vec, taps, vec, mat, vec, mat, vec, vec],
        out_specs=[seq] * 5, out_shape=[out] * 5,
        scratch_shapes=[pltpu.VMEM((tc + HALO, SLAB), F32), pltpu.VMEM((tc + SUBLANES, SLAB), F32),
                        pltpu.VMEM((tc + SUBLANES, SLAB), F32)],
        compiler_params=_params(48, 1),
    )(*_in_hbm(u_pool, u_lru, u_gate, pool_w, pool_scale, conv_w, conv_b, wa_bd, b_a, wi_bd, b_i, lam))


def _fwd_out(x, y_pool, y_lru, gn_pool_g, gn_lru_g, w_out, ln2_g):
    s = x.shape[0]
    tb = _token_block(s, 1024)

    def body(x_ref, yp_ref, yl_ref, gp_ref, gl_ref, w_ref, g2_ref, h1_ref, n2_ref):
        mp, _, _ = _rms(_read_slabs(yp_ref), gp_ref[...])
        ml, _, _ = _rms(_read_slabs(yl_ref), gl_ref[...])
        h1 = x_ref[...] + _mm(mp.astype(BF16), w_ref[:POOL_WIDTH, :]) + _mm(ml.astype(BF16), w_ref[POOL_WIDTH:, :])
        h1_ref[...] = h1
        n2_ref[...] = _rms(h1, g2_ref[...])[0].astype(BF16)

    row = pl.BlockSpec((tb, D_MODEL), lambda i: (i, 0))
    half = _slabs_spec(tb)
    return pl.pallas_call(
        body, name="fwd_out", grid=(s // tb,),
        in_specs=[row, half, half, pl.BlockSpec((1, 512), lambda i: (0, 0)), pl.BlockSpec((1, 512), lambda i: (0, 0)),
                  pl.BlockSpec((D_MODEL, D_MODEL), lambda i: (0, 0)), pl.BlockSpec((1, D_MODEL), lambda i: (0, 0))],
        out_specs=[row, row],
        out_shape=[pltpu.HBM((s, D_MODEL), F32), pltpu.HBM((s, D_MODEL), BF16)],
        compiler_params=_params(40, 1),
    )(*_in_hbm(x, y_pool, y_lru, gn_pool_g, gn_lru_g, w_out, ln2_g))


def _ffn_fwd(h1, n2, target, lnf_g, w_gate, w_up, w_down):
    s = h1.shape[0]
    tb = 512
    sub = 256
    n_ff = D_FF // FF_CHUNK

    n_steps = s // tb
    ring = 3

    def body(h1_hbm, n2_hbm, t_hbm, gf_ref, wg_hbm, wu_hbm, wd_hbm,
             g_ref, u_ref, dh_ref, dhb_ref, dgf_ref, sq_ref, wg, wu, wd, sem, h1_ring, n2_ring, t_ring, ring_sem):
        step = pl.program_id(0)

        def fetch(at):
            slot = at % ring
            rows = pl.ds(pl.multiple_of(at * tb, tb), tb)
            return [pltpu.make_async_copy(src.at[rows], dst.at[slot], ring_sem.at[3 * slot + k])
                    for k, (src, dst) in enumerate(((h1_hbm, h1_ring), (n2_hbm, n2_ring), (t_hbm, t_ring)))]

        @pl.when(step == 0)
        def _():
            for at in range(min(ring - 1, n_steps)):
                for cp in fetch(at):
                    cp.start()
            loads = [pltpu.make_async_copy(src, dst, sem.at[k])
                     for k, (src, dst) in enumerate(((wg_hbm, wg), (wu_hbm, wu), (wd_hbm, wd)))]
            for cp in loads:
                cp.start()
            for cp in loads:
                cp.wait()
            dgf_ref[...] = jnp.zeros_like(dgf_ref)
            sq_ref[...] = jnp.zeros_like(sq_ref)

        @pl.when(step + ring - 1 < n_steps)
        def _():
            for cp in fetch(step + ring - 1):
                cp.start()

        for cp in fetch(step):
            cp.wait()
        slot = step % ring
        h1_ref, t_ref = h1_ring.at[slot], t_ring.at[slot]
        n2v = n2_ring[slot]
        acc = jnp.zeros((tb, D_MODEL), F32)
        for c in range(n_ff):
            cols = slice(c * FF_CHUNK, (c + 1) * FF_CHUNK)
            g = _mm_nt(n2v, wg[cols, :])
            u = _mm_nt(n2v, wu[cols, :])
            g_ref[:, cols] = g.astype(BF16)
            u_ref[:, cols] = u.astype(BF16)
            act = g * jax.nn.sigmoid(g) * u
            acc = acc + _mm(act.astype(BF16), wd[cols, :])
        gf = gf_ref[...]
        for r in range(tb // sub):
            rows = slice(r * sub, (r + 1) * sub)
            y, xhat, rstd = _rms(h1_ref[rows, :] + acc[rows, :], gf)
            err = y - t_ref[rows, :]
            sq_ref[...] += jnp.sum(err * err, axis=0, keepdims=True)
            dh2, dgf = _rms_bwd(err * (1.0 / D_MODEL), xhat, rstd, gf)
            dgf_ref[...] += dgf
            dh_ref[rows, :] = dh2
            dhb_ref[rows, :] = dh2.astype(BF16)

    row = pl.BlockSpec((tb, D_MODEL), lambda i: (i, 0))
    ff = pl.BlockSpec((tb, D_FF), lambda i: (i, 0))
    vec = pl.BlockSpec((1, D_MODEL), lambda i: (0, 0))
    anyspace = pl.BlockSpec(memory_space=pl.ANY)
    return pl.pallas_call(
        body, name="ffn_fwd", grid=(s // tb,),
        in_specs=[anyspace, anyspace, anyspace, vec, anyspace, anyspace, anyspace],
        out_specs=[ff, ff, row, row, vec, vec],
        out_shape=[pltpu.HBM((s, D_FF), BF16), pltpu.HBM((s, D_FF), BF16),
                   pltpu.HBM((s, D_MODEL), F32), pltpu.HBM((s, D_MODEL), BF16),
                   jax.ShapeDtypeStruct((1, D_MODEL), F32), jax.ShapeDtypeStruct((1, D_MODEL), F32)],
        scratch_shapes=[pltpu.VMEM((D_FF, D_MODEL), BF16), pltpu.VMEM((D_FF, D_MODEL), BF16),
                        pltpu.VMEM((D_FF, D_MODEL), BF16), pltpu.SemaphoreType.DMA((3,)),
                        pltpu.VMEM((ring, tb, D_MODEL), F32), pltpu.VMEM((ring, tb, D_MODEL), BF16),
                        pltpu.VMEM((ring, tb, D_MODEL), F32), pltpu.SemaphoreType.DMA((3 * ring,))],
        compiler_params=_params(60, 1),
    )(*_in_hbm(h1, n2, target, lnf_g, w_gate, w_up, w_down))


def _ffn_bwd(n2, dh2b, g, u, w_gate_t, w_up_t, w_down):
    s = n2.shape[0]
    tb = min(1024, s)
    n_ff = D_FF // FF_CHUNK
    n_tb = s // tb

    def body(n2_ref, dh_ref, g_ref, u_ref, wg_ref, wu_ref, wd_ref, dwg_ref, dwu_ref, dwd_ref, dn2_ref,
             dn2_acc, acc_g, acc_u, acc_d):
        j = pl.program_id(0)
        t = pl.program_id(1)
        rows = pl.ds(pl.multiple_of(t * tb, tb), tb)

        @pl.when(t == 0)
        def _():
            acc_g[...] = jnp.zeros_like(acc_g)
            acc_u[...] = jnp.zeros_like(acc_u)
            acc_d[...] = jnp.zeros_like(acc_d)

        @pl.when(j == 0)
        def _():
            dn2_acc[rows, :] = jnp.zeros((tb, D_MODEL), F32)

        n2v = n2_ref[...]
        dh = dh_ref[...]
        gv = g_ref[...].astype(F32)
        uv = u_ref[...].astype(F32)
        sg = jax.nn.sigmoid(gv)
        silu = gv * sg
        dact = _mm_nt(dh, wd_ref[...])
        dub = (dact * silu).astype(BF16)
        dgb = (dact * uv * (sg * (1.0 + gv * (1.0 - sg)))).astype(BF16)
        acc_d[...] += _mm_tn((silu * uv).astype(BF16), dh)
        acc_g[...] += _mm_tn(dgb, n2v)
        acc_u[...] += _mm_tn(dub, n2v)
        dn2_acc[rows, :] += _mm(jnp.concatenate([dgb, dub], axis=1),
                                jnp.concatenate([wg_ref[...], wu_ref[...]], axis=0))

        @pl.when(t == n_tb - 1)
        def _():
            dwg_ref[...] = acc_g[...].astype(BF16)
            dwu_ref[...] = acc_u[...].astype(BF16)
            dwd_ref[...] = acc_d[...].astype(BF16)

        @pl.when(j == n_ff - 1)
        def _():
            dn2_ref[...] = dn2_acc[rows, :]

    row = pl.BlockSpec((tb, D_MODEL), lambda j, t: (t, 0))
    act = pl.BlockSpec((tb, FF_CHUNK), lambda j, t: (t, j))
    w_row = pl.BlockSpec((FF_CHUNK, D_MODEL), lambda j, t: (j, 0))
    last = pl.BlockSpec((tb, D_MODEL), lambda j, t: (jnp.where(j == n_ff - 1, t, 0), 0))
    grad = pltpu.HBM((D_FF, D_MODEL), BF16)
    chunk_acc = pltpu.VMEM((FF_CHUNK, D_MODEL), F32)
    return pl.pallas_call(
        body, name="ffn_bwd", grid=(n_ff, n_tb),
        in_specs=[row, row, act, act, w_row, w_row, w_row],
        out_specs=[w_row, w_row, w_row, last],
        out_shape=[grad, grad, grad, pltpu.HBM((s, D_MODEL), F32)],
        scratch_shapes=[pltpu.VMEM((s, D_MODEL), F32), chunk_acc, chunk_acc, chunk_acc],
        compiler_params=_params(56, 2),
    )(*_in_hbm(n2, dh2b, g, u, w_gate_t, w_up_t, w_down))


def _bwd_out(dn2, dh2, h1, y_pool, y_lru, gn_pool_g, gn_lru_g, w_out, ln2_g, after):
    s = h1.shape[0]
    tb = _token_block(s)

    def body(dn2_ref, dh2_ref, h1_ref, yp_ref, yl_ref, gp_ref, gl_ref, w_ref, g2_ref, _after,
             dh1_ref, dyp_ref, dyl_ref, dwb_ref, dg2_ref, dgp_ref, dgl_ref, dw_ref):
        @pl.when(pl.program_id(0) == 0)
        def _():
            dw_ref[...] = jnp.zeros_like(dw_ref)
            dg2_ref[...] = jnp.zeros_like(dg2_ref)
            dgp_ref[...] = jnp.zeros_like(dgp_ref)
            dgl_ref[...] = jnp.zeros_like(dgl_ref)

        g2 = g2_ref[...]
        _, xhat2, rstd2 = _rms(h1_ref[...], g2)
        dres, dg2 = _rms_bwd(dn2_ref[...], xhat2, rstd2, g2)
        dg2_ref[...] += dg2
        dh1 = dh2_ref[...] + dres
        dh1_ref[...] = dh1
        dh1b = dh1.astype(BF16)
        gp, gl = gp_ref[...], gl_ref[...]
        mp, xhat_p, rstd_p = _rms(_read_slabs(yp_ref), gp)
        ml, xhat_l, rstd_l = _rms(_read_slabs(yl_ref), gl)
        dw_ref[:POOL_WIDTH, :] += _mm_tn(mp.astype(BF16), dh1b)
        dw_ref[POOL_WIDTH:, :] += _mm_tn(ml.astype(BF16), dh1b)
        dyp, dgp = _rms_bwd(_mm_nt(dh1b, w_ref[:POOL_WIDTH, :]), xhat_p, rstd_p, gp)
        dyl, dgl = _rms_bwd(_mm_nt(dh1b, w_ref[POOL_WIDTH:, :]), xhat_l, rstd_l, gl)
        _write_slabs(dyp_ref, dyp)
        _write_slabs(dyl_ref, dyl)
        dgp_ref[...] += dgp
        dgl_ref[...] += dgl

        @pl.when(pl.program_id(0) == s // tb - 1)
        def _():
            dwb_ref[...] = dw_ref[...].astype(BF16)

    row = pl.BlockSpec((tb, D_MODEL), lambda i: (i, 0))
    half = _slabs_spec(tb)
    vec = pl.BlockSpec((1, D_MODEL), lambda i: (0, 0))
    hvec = pl.BlockSpec((1, 512), lambda i: (0, 0))
    mat = pl.BlockSpec((D_MODEL, D_MODEL), lambda i: (0, 0))
    return pl.pallas_call(
        body, name="bwd_out", grid=(s // tb,),
        in_specs=[row, row, row, half, half, hvec, hvec, mat, vec, pl.BlockSpec(memory_space=pl.ANY)],
        out_specs=[row, half, half, mat, vec, hvec, hvec],
        out_shape=[pltpu.HBM((s, D_MODEL), F32), _slab_major(s),
                   _slab_major(s), pltpu.HBM((D_MODEL, D_MODEL), BF16),
                   jax.ShapeDtypeStruct((1, D_MODEL), F32), jax.ShapeDtypeStruct((1, 512), F32),
                   jax.ShapeDtypeStruct((1, 512), F32)],
        scratch_shapes=[pltpu.VMEM((D_MODEL, D_MODEL), F32)],
        compiler_params=_params(48, 1),
    )(*_in_hbm(dn2, dh2, h1, y_pool, y_lru, gn_pool_g, gn_lru_g, w_out, ln2_g), after)


def _mixer_bwd(u_pool, u_lru, u_gate, h, decay, root, dy_pool, dy_lru,
               pool_w, pool_scale, conv_w, conv_b, wa_bd, b_a, wi_bd, b_i, lam, after):
    s = u_pool.shape[1]
    tc = _time_chunk(s)
    n_chunks = s // tc

    def body(up_ref, ul_ref, ug_ref, h_ref, a_ref, mult_ref, dyp_ref, dyl_ref,
             pw_ref, ps_ref, cw_ref, cb_ref, wa_ref, ba_ref, wi_ref, bi_ref, lam_ref, _after,
             dup_ref, dul_ref, dug_ref, dpw_ref, dps_ref, dcw_ref, dcb_ref, dwa_ref, dba_ref, dwi_ref, dbi_ref, dlam_ref,
             a_pad, b_pad, u_shift, h_shift, a_shift, d_shift):
        _scan_pads(a_pad, b_pad, tc, causal=False)
        deeper, window = _slab_scalars()
        pw = pw_ref[0].astype(BF16)
        wa = wa_ref[0].astype(BF16)
        wi = wi_ref[0].astype(BF16)
        ps, cb, ba, bi = ps_ref[...], cb_ref[...], ba_ref[...], bi_ref[...]
        cw = [cw_ref[k:k + 1, :] for k in range(CONV_WIDTH)]
        lam_v = lam_ref[...]
        sp = _softplus_neg(lam_v)
        row = lax.broadcasted_iota(jnp.int32, (tc, SLAB), 0)
        for ref in (dpw_ref, dps_ref, dcw_ref, dcb_ref, dwa_ref, dba_ref, dwi_ref, dbi_ref, dlam_ref):
            ref[...] = jnp.zeros_like(ref)

        def chunk(t0, ext_p, ext_l, ext_h, carry, at_start=False):
            l_next, a_next, dxc_next, ddn_next = carry
            rows = pl.ds(t0, tc)
            u_shift[...] = ext_l
            taps = [u_shift[HALO - k:HALO - k + tc, :] for k in range(CONV_WIDTH)]
            xc = _conv(taps, cw, cb)
            xb, r, i = _lru_gates(xc, wa, ba, wi, bi)
            a, mult = a_ref[rows, :], mult_ref[rows, :]
            hv = ext_h[HALO:]
            h_shift[...] = ext_h
            h_before = h_shift[HALO - 1:HALO - 1 + tc, :]
            ug = ug_ref[rows, :]
            dyl = dyl_ref[rows, :]
            gel, th = _gelu(ug)
            dug_ref[rows, :] = (dyl * hv * _gelu_grad(ug, th)).astype(BF16)
            a_shift[:tc, :] = a
            a_shift[tc:, :] = jnp.broadcast_to(a_next, (SUBLANES, SLAB))
            a_after = a_shift[1:1 + tc, :]
            l = _scan_anticausal(a_after, dyl * gel, l_next, a_pad, b_pad, tc)
            dmult = l * (i * xc)
            di = l * mult * xc
            dxc = l * mult * i
            dla = (l * h_before) * a - jnp.where(mult > 0.0, dmult * (a * a) / mult, 0.0)
            dlam_ref[...] += jnp.sum(dla * r, axis=0, keepdims=True)
            dpa = (dla * ((-LRU_C) * sp)) * (r * (1.0 - r))
            dpi = di * (i * (1.0 - i))
            dpab = dpa.astype(BF16)
            dpib = dpi.astype(BF16)
            dwa_ref[0] += _mm_tn(xb, dpab)
            dwi_ref[0] += _mm_tn(xb, dpib)
            dba_ref[...] += jnp.sum(dpa, axis=0, keepdims=True)
            dbi_ref[...] += jnp.sum(dpi, axis=0, keepdims=True)
            dxc = dxc + _mm_nt(dpab, wa) + _mm_nt(dpib, wi)
            d_shift[:tc, :] = dxc
            d_shift[tc:, :] = dxc_next
            dul_ref[rows, :] = (cw[3] * dxc + cw[2] * d_shift[1:1 + tc, :]
                                + cw[1] * d_shift[2:2 + tc, :] + cw[0] * d_shift[3:3 + tc, :]).astype(BF16)
            for k in range(CONV_WIDTH):
                dcw_ref[k:k + 1, :] += jnp.sum(dxc * taps[CONV_WIDTH - 1 - k], axis=0, keepdims=True)
            dcb_ref[...] += jnp.sum(dxc, axis=0, keepdims=True)
            db = (_window_mean(_causal_window(ext_p, deeper), t0, row, window, at_start) - ext_p[HALO:]).astype(BF16)
            dyp = dyp_ref[rows, :]
            dps_ref[...] += jnp.sum(dyp * _mm(db, pw), axis=0, keepdims=True)
            dys = (dyp * ps).astype(BF16)
            dpw_ref[0] += _mm_tn(db, dys)
            dd = _mm_nt(dys, pw)
            ddn = _window_mean(dd, t0, row, window, at_start)
            ext_q = jnp.concatenate([ddn, ddn_next], axis=0)
            dup_ref[rows, :] = (_anticausal_window(ext_q, deeper, tc) - dd).astype(BF16)
            return l[0:1, :], a[0:1, :], dxc[0:8, :], ddn[0:HALO, :]

        def step(k, carry):
            c = n_chunks - 1 - k
            t0 = pl.multiple_of(c * tc, tc)
            ext = pl.ds(pl.multiple_of(c * tc - HALO, HALO), tc + HALO)
            return chunk(t0, up_ref[ext, :], ul_ref[ext, :], h_ref[ext, :], carry)

        carry = (jnp.zeros((1, SLAB), F32), jnp.zeros((1, SLAB), F32),
                 jnp.zeros((8, SLAB), F32), jnp.zeros((HALO, SLAB), F32))
        carry = lax.fori_loop(0, n_chunks - 1, step, carry)
        pad = jnp.zeros((HALO, SLAB), F32)
        first = pl.ds(0, tc)
        chunk(0, jnp.concatenate([pad, up_ref[first, :]], axis=0), jnp.concatenate([pad, ul_ref[first, :]], axis=0),
              jnp.concatenate([pad, h_ref[first, :]], axis=0), carry, at_start=True)
        dlam_ref[...] = dlam_ref[...] * (LRU_C * jax.nn.sigmoid(-lam_v))

    seq, mat, vec, taps = _slab_specs(s)
    grad = _slab_major(s, BF16)
    mats = jax.ShapeDtypeStruct((N_SLAB, SLAB, SLAB), F32)
    vecs = jax.ShapeDtypeStruct((1, 512), F32)
    return pl.pallas_call(
        body, name="mixer_bwd", grid=(N_SLAB,),
        in_specs=[seq] * 8 + [mat, vec, taps, vec, mat, vec, mat, vec, vec, pl.BlockSpec(memory_space=pl.ANY)],
        out_specs=[seq, seq, seq, mat, vec, taps, vec, mat, vec, mat, vec, vec],
        out_shape=[grad, grad, grad, mats, vecs, jax.ShapeDtypeStruct((CONV_WIDTH, 512), F32), vecs,
                   mats, vecs, mats, vecs, vecs],
        scratch_shapes=[pltpu.VMEM((tc + SUBLANES, SLAB), F32), pltpu.VMEM((tc + SUBLANES, SLAB), F32),
                        pltpu.VMEM((tc + HALO, SLAB), F32), pltpu.VMEM((tc + HALO, SLAB), F32),
                        pltpu.VMEM((tc + SUBLANES, SLAB), F32), pltpu.VMEM((tc + SUBLANES, SLAB), F32)],
        compiler_params=_params(56, 1),
    )(*_in_hbm(u_pool, u_lru, u_gate, h, decay, root, dy_pool, dy_lru, pool_w, pool_scale, conv_w, conv_b, wa_bd,
               b_a, wi_bd, b_i, lam), after)


def _bwd_in(x, dh1, du_pool, du_lru, du_gate, ln1_g, w_in_t, after):
    s = x.shape[0]
    tb = _token_block(s)

    def body(x_ref, dh1_ref, dup_ref, dul_ref, dug_ref, g_ref, w_ref, _after, dx_ref, dwb_ref, dg_ref, dw_ref):
        @pl.when(pl.program_id(0) == 0)
        def _():
            dw_ref[...] = jnp.zeros_like(dw_ref)
            dg_ref[...] = jnp.zeros_like(dg_ref)

        g1 = g_ref[...]
        n, xhat, rstd = _rms(x_ref[...], g1)
        nb = n.astype(BF16)
        db = jnp.concatenate([_read_slabs(ref) for ref in (dup_ref, dul_ref, dug_ref)], axis=1)
        dw_ref[...] += _mm_tn(db, nb)
        dx, dg1 = _rms_bwd(_mm(db, w_ref[...]), xhat, rstd, g1)
        dx_ref[...] = dh1_ref[...] + dx
        dg_ref[...] += dg1

        @pl.when(pl.program_id(0) == s // tb - 1)
        def _():
            dwb_ref[...] = dw_ref[...].astype(BF16)

    row = pl.BlockSpec((tb, D_MODEL), lambda i: (i, 0))
    half = _slabs_spec(tb)
    vec = pl.BlockSpec((1, D_MODEL), lambda i: (0, 0))
    mat = pl.BlockSpec((IN_WIDTH, D_MODEL), lambda i: (0, 0))
    return pl.pallas_call(
        body, name="bwd_in", grid=(s // tb,),
        in_specs=[row, row, half, half, half, vec, mat, pl.BlockSpec(memory_space=pl.ANY)],
        out_specs=[row, mat, vec],
        out_shape=[pltpu.HBM((s, D_MODEL), F32), pltpu.HBM((IN_WIDTH, D_MODEL), BF16),
                   jax.ShapeDtypeStruct((1, D_MODEL), F32)],
        scratch_shapes=[pltpu.VMEM((IN_WIDTH, D_MODEL), F32)],
        compiler_params=_params(48, 1),
    )(*_in_hbm(x, dh1, du_pool, du_lru, du_gate, ln1_g, w_in_t), after)


def _mesh_position():
    x, y, c = lax.axis_index("x"), lax.axis_index("y"), lax.axis_index("c")
    return x, y, c, 4 * x + 2 * y + c


def _peer(x, y, c, p):
    px = 1 - x if p & 4 else x
    py = 1 - y if p & 2 else y
    pc = 1 - c if p & 1 else c
    return (px, py, pc), 4 * px + 2 * py + pc


HBM_SPEC = pl.BlockSpec(memory_space=pltpu.HBM)
SEM_SPEC = pl.BlockSpec(memory_space=pltpu.SEMAPHORE)
DATAFLOW = pltpu.SideEffectType.DATAFLOW_SIDE_EFFECTING


class Copy(NamedTuple):
    src: int
    src_at: Any
    dst: int
    dst_at: Any
    peer: int
    group: int
    slot: int


SIBLING = (1,)
SAME_CORE = (2, 4, 6)
EVERYONE = tuple(range(1, N_DEV))
MYSELF = (0,)


def _same(index):
    return index


def _chip(index):
    return jnp.right_shift(index, 1)


def _fan_out(srcs, lands, peers, group):
    return [Copy(s, None, d, _same, p, group, N_DEV * i + p) for i, (s, d) in enumerate(zip(srcs, lands)) for p in peers]


def _scatter(stacks, lands, peers):
    return [Copy(s, lambda me, p=p: jnp.bitwise_xor(me, p), d, _same, p, 0, 0)
            for s, d in zip(stacks, lands) for p in peers]


def _numbered(copies, group=0):
    return [cp._replace(group=group, slot=i) for i, cp in enumerate(copies)]


def _relay(lands, peers):
    return [Copy(b, lambda s, q=q: jnp.bitwise_xor(s, q), b, lambda s, q=q: jnp.bitwise_xor(s, q), 1, 0, N_DEV * i + q)
            for i, b in enumerate(lands) for q in peers]


def _to_sibling(stacks, lands):
    return [Copy(s, lambda me, k=k: 2 * k + 1 - jnp.bitwise_and(me, 1), d, lambda me, k=k: k, 1, 0, 4 * i + k)
            for i, (s, d) in enumerate(zip(stacks, lands)) for k in range(N_DEV // 2)]


def _to_chips(sums, lands):
    return [Copy(s, lambda me, p=p: jnp.bitwise_xor(_chip(me), p // 2), d, _chip, p, 0, 4 * i + p // 2)
            for i, (s, d) in enumerate(zip(sums, lands)) for p in SAME_CORE]


def _comm_call(name, bufs, wait=None, start=None, after=()):
    nb = len(bufs)
    waits = [wait] if isinstance(wait, tuple) else list(wait or [])
    slots = list(start[1]) if start else []
    n_out_sem = 2 * len(slots)
    after = [a for a in (after if isinstance(after, (list, tuple)) else [after]) if a is not None]

    def body(*refs):
        b = refs[:nb]
        at = nb + 2 * len(waits) + len(after)
        out_sems = refs[at:at + n_out_sem]
        token = refs[at + n_out_sem + nb]
        x, y, c, me = _mesh_position()

        def part(i, row_of, sender):
            return b[i] if row_of is None else b[i].at[row_of(sender)]

        for k, (_, copies) in enumerate(waits):
            w_send, w_recv = refs[nb + 2 * k], refs[nb + 2 * k + 1]
            for cp in copies:
                peer, peer_index = _peer(x, y, c, cp.peer)
                arrival = pltpu.make_async_remote_copy(part(cp.src, cp.src_at, me), part(cp.dst, cp.dst_at, peer_index),
                                                       w_send.at[cp.slot], w_recv.at[cp.slot],
                                                       device_id=peer, device_id_type=MESH)
                arrival.wait_send()
                arrival.wait_recv()
        if start:
            for cp in start[0]:
                peer, _ = _peer(x, y, c, cp.peer)
                pltpu.make_async_remote_copy(part(cp.src, cp.src_at, me), part(cp.dst, cp.dst_at, me),
                                             out_sems[2 * cp.group].at[cp.slot], out_sems[2 * cp.group + 1].at[cp.slot],
                                             device_id=peer, device_id_type=MESH).start()
        token[...] = jnp.zeros_like(token)

    sem_shapes = []
    for n_slots in slots:
        sem_shapes += [pltpu.SemaphoreType.DMA((n_slots,))] * 2
    operands = [pltpu.with_memory_space_constraint(a, pltpu.HBM) for a in bufs]
    in_specs = [HBM_SPEC] * nb
    for sems, _ in waits:
        operands += list(sems)
        in_specs += [SEM_SPEC, SEM_SPEC]
    operands += after
    in_specs += [pl.BlockSpec(memory_space=pl.ANY)] * len(after)
    outs = pl.pallas_call(
        body, name=name, in_specs=in_specs,
        out_specs=[SEM_SPEC] * n_out_sem + [HBM_SPEC] * nb + [pl.BlockSpec(memory_space=pltpu.VMEM)],
        out_shape=sem_shapes + [pltpu.HBM(a.shape, a.dtype) for a in bufs] + [jax.ShapeDtypeStruct((8, SLAB), F32)],
        input_output_aliases={i: n_out_sem + i for i in range(nb)},
        compiler_params=pltpu.CompilerParams(has_side_effects=DATAFLOW),
    )(*operands)
    sems = [(outs[2 * k], outs[2 * k + 1]) for k in range(len(slots))]
    return sems, list(outs[n_out_sem:n_out_sem + nb]), outs[-1]


def _pair_sum(stacks, lands, place):
    n = len(stacks)

    def body(place_ref, *refs):
        k = pl.program_id(0)
        for m in range(n):
            mine, theirs, out, land = refs[m], refs[n + m], refs[2 * n + m], refs[3 * n + m]
            total = (mine[0, 0].astype(F32) + theirs[0].astype(F32)).astype(out.dtype)
            out[0] = total

            @pl.when(k == place_ref[1])
            def _():
                land[0] = total

    in_specs = [pl.BlockSpec((1, 1) + a.shape[1:], lambda k, place_ref: (k, place_ref[0], 0, 0)) for a in stacks]
    in_specs += [pl.BlockSpec((1,) + a.shape[1:], lambda k, place_ref: (k, 0, 0)) for a in lands]
    out_specs = [pl.BlockSpec((1,) + a.shape[1:], lambda k, place_ref: (k, 0, 0)) for a in lands]
    out_specs += [pl.BlockSpec((1,) + a.shape[1:], lambda k, place_ref: (place_ref[1], 0, 0)) for a in lands]
    outs = pl.pallas_call(
        body, name="pair_sum_" + "_".join(str(a.shape[1]) for a in stacks),
        grid_spec=pltpu.PrefetchScalarGridSpec(num_scalar_prefetch=1, grid=(N_DEV // 2,), in_specs=in_specs,
                                               out_specs=out_specs),
        out_shape=[pltpu.HBM(a.shape, a.dtype) for a in lands] * 2,
        compiler_params=_params(40, 1),
    )(place, *_in_hbm(*[a.reshape((N_DEV // 2, 2) + a.shape[1:]) for a in stacks], *lands))
    return list(outs[:n]), list(outs[n:])


def _reduce_adam(parts, w, m, v, name):
    rows, cols = w.shape
    n_parts = parts.shape[0]
    rb = rows
    for cand in (256, 176, 128):
        if rows % cand == 0 and rows > cand:
            rb = cand
            break

    def body(p_ref, w_ref, m_ref, v_ref, g_out, d_out, m_out, v_out):
        g = p_ref[0].astype(F32)
        for j in range(1, n_parts):
            g = g + p_ref[j].astype(F32)
        g_out[...] = g
        d_out[...], m_out[...], v_out[...] = _adam(g, w_ref[...], m_ref[...], v_ref[...])

    blk = pl.BlockSpec((rb, cols), lambda i: (i, 0))
    out = jax.ShapeDtypeStruct((rows, cols), F32)
    return pl.pallas_call(
        body, name=name, grid=(rows // rb,),
        in_specs=[pl.BlockSpec((n_parts, rb, cols), lambda i: (0, i, 0)), blk, blk, blk],
        out_specs=[blk] * 4, out_shape=[out] * 4,
        compiler_params=_params(40, 1),
    )(*_in_hbm(parts, w, m, v))


def _cols_from_stack(stack):
    n, r, c = stack.shape
    return jnp.transpose(stack, (1, 0, 2)).reshape(r, n * c)


def _block_diag(w):
    z = jnp.zeros((N_SLAB, 64, 64), w.dtype)
    pairs = w.reshape(N_SLAB, 2, 64, 64)
    top = jnp.concatenate([pairs[:, 0], z], axis=2)
    bottom = jnp.concatenate([z, pairs[:, 1]], axis=2)
    return jnp.concatenate([top, bottom], axis=1)


def _adam(g, w, m, v):
    m_new = ADAM_B1 * m + (1.0 - ADAM_B1) * g
    v_new = ADAM_B2 * v + (1.0 - ADAM_B2) * (g * g)
    m_hat = m_new / (1.0 - ADAM_B1 ** ADAM_STEP)
    v_hat = v_new / (1.0 - ADAM_B2 ** ADAM_STEP)
    return (-ADAM_LR) * (m_hat / (jnp.sqrt(v_hat) + ADAM_EPS) + ADAM_WD * w), m_new, v_new


WIDE = ("ln2_g", "lnf_g")
HALF = ("pool_scale", "conv_b", "b_a", "b_i", "lam", "gn_pool_g", "gn_lru_g")
VECTORS = [(k, D_MODEL) for k in WIDE] + [(k, 512) for k in HALF]
VECTOR_ROWS = sum(width // SLAB for _, width in VECTORS)
LOSS_ROW = -(-VECTOR_ROWS // 8) * 8
CONV_AT = LOSS_ROW + 8
CONV_LANES = LRU_WIDTH // SLAB
PACK_F_ROWS = CONV_AT + CONV_WIDTH * CONV_LANES
MATRIX_ROWS = N_SLAB * SLAB
HEAD = SLAB // 2
GATE_ROWS = N_SLAB * HEAD
PACK_B_ROWS = MATRIX_ROWS + 2 * GATE_ROWS


def _pack_small(vectors, pool_g, wa_g, wi_g, conv, sq):
    n_vec = len(vectors)

    def body(*refs):
        vec = refs[:n_vec]
        pw_ref, wa_ref, wi_ref, cw_ref, sq_ref, out, out_b = refs[n_vec:]
        out[...] = jnp.zeros_like(out)
        row = 0
        for ref, (_, width) in zip(vec, VECTORS):
            for k in range(width // SLAB):
                out[row:row + 1, :] = ref[:, k * SLAB:(k + 1) * SLAB]
                row += 1
        for tap in range(CONV_WIDTH):
            for k in range(CONV_LANES):
                at = CONV_AT + tap * CONV_LANES + k
                out[at:at + 1, :] = cw_ref[tap:tap + 1, k * SLAB:(k + 1) * SLAB]
        total = sq_ref[:, 0:SLAB]
        for k in range(1, D_MODEL // SLAB):
            total = total + sq_ref[:, k * SLAB:(k + 1) * SLAB]
        out[LOSS_ROW:LOSS_ROW + 1, :] = total
        left = lax.broadcasted_iota(jnp.int32, (HEAD, SLAB), 1) < HEAD
        for s in range(N_SLAB):
            out_b[s * SLAB:(s + 1) * SLAB, :] = pw_ref[s].astype(BF16)
            for i, ref in enumerate((wa_ref, wi_ref)):
                at = MATRIX_ROWS + i * GATE_ROWS + s * HEAD
                out_b[at:at + HEAD, :] = jnp.where(left, ref[s, 0:HEAD, :], ref[s, HEAD:SLAB, :]).astype(BF16)

    return pl.pallas_call(
        body, name="pack_small",
        out_shape=[jax.ShapeDtypeStruct((PACK_F_ROWS, SLAB), F32), jax.ShapeDtypeStruct((PACK_B_ROWS, SLAB), BF16)],
    )(*vectors, pool_g, wa_g, wi_g, conv, sq)


def _small_reduce_adam(my_index, parts, parts_b, vec_w, vec_m, vec_v, pool_wmv, wa_wmv, wi_wmv, conv_wmv):
    n_vec = len(VECTORS)
    n_parts = parts.shape[0]

    def body(*refs):
        me_ref, p_ref, pb_ref = refs[0], refs[1], refs[2]
        refs = refs[1:]
        w_refs, m_refs, v_refs = (refs[2 + k * n_vec:2 + (k + 1) * n_vec] for k in range(3))
        at = 2 + 3 * n_vec
        pw_w, pw_m, pw_v = refs[at:at + 3]
        gates = (refs[at + 3:at + 6], refs[at + 6:at + 9])
        cv_w, cv_m, cv_v = refs[at + 9:at + 12]
        outs = refs[at + 12:-1]
        total = refs[-1]
        total[...] = p_ref[0]
        for j in range(1, n_parts):
            total[...] += p_ref[j]
        row = 0
        for i, (_, width) in enumerate(VECTORS):
            n_rows = width // SLAB
            g = jnp.concatenate([total[row + k:row + k + 1, :] for k in range(n_rows)], axis=1)
            row += n_rows
            d, m_new, v_new = _adam(g, w_refs[i][...], m_refs[i][...], v_refs[i][...])
            for ref, val in zip(outs[4 * i:4 * i + 4], (g, d, m_new, v_new)):
                ref[...] = val
        tail = outs[4 * n_vec:]

        def summed(first, count):
            g = pb_ref[0, first:first + count, :].astype(F32)
            for j in range(1, n_parts):
                g = g + pb_ref[j, first:first + count, :].astype(F32)
            return g

        g = summed(0, MATRIX_ROWS)
        d, m_new, v_new = _adam(g, pw_w[...], pw_m[...], pw_v[...])
        for ref, val in zip(tail[0:4], (g, d, m_new, v_new)):
            ref[...] = val
        for i, (w, m, v) in enumerate(gates):
            g = summed(MATRIX_ROWS + i * GATE_ROWS, GATE_ROWS)
            for s in range(N_SLAB):
                for hd in range(2):
                    block = g[s * HEAD:(s + 1) * HEAD, hd * HEAD:(hd + 1) * HEAD]
                    rows = slice((2 * s + hd) * HEAD, (2 * s + hd + 1) * HEAD)
                    d, m_new, v_new = _adam(block, w[rows, :], m[rows, :], v[rows, :])
                    for ref, val in zip(tail[4 + 4 * i:8 + 4 * i], (block, d, m_new, v_new)):
                        ref[rows, :] = val
        me = me_ref[0, 0]
        taps = []
        for tap in range(CONV_WIDTH):
            row = total[pl.ds(CONV_AT + tap * CONV_LANES + me // 2, 1), :]
            taps.append(jnp.where(me % 2 == 0, row[:, :HEAD], row[:, HEAD:]))
        g = jnp.concatenate(taps, axis=0)
        d, m_new, v_new = _adam(g, cv_w[...], cv_m[...], cv_v[...])
        for ref, val in zip(tail[12:16], (g, d, m_new, v_new)):
            ref[...] = val
        tail[16][...] = (0.5 / D_MODEL) * jnp.sum(total[LOSS_ROW:LOSS_ROW + 1, :], axis=1, keepdims=True)

    out_shape = []
    for _, width in VECTORS:
        out_shape += [jax.ShapeDtypeStruct((1, width), F32)] * 4
    out_shape += [jax.ShapeDtypeStruct((MATRIX_ROWS, SLAB), F32)] * 4
    out_shape += [jax.ShapeDtypeStruct((2 * GATE_ROWS, HEAD), F32)] * 8
    out_shape += [jax.ShapeDtypeStruct(conv_wmv[0].shape, F32)] * 4 + [jax.ShapeDtypeStruct((1, 1), F32)]
    operands = [parts, parts_b, *vec_w, *vec_m, *vec_v, *pool_wmv, *wa_wmv, *wi_wmv, *conv_wmv]

    def whole(a):
        return pl.BlockSpec(a.shape, lambda i, n=len(a.shape): (0,) * n)

    outs = pl.pallas_call(
        body, name="adam_small", grid=(1,),
        in_specs=[pl.BlockSpec(memory_space=pltpu.SMEM)] + [whole(a) for a in operands],
        out_specs=[whole(a) for a in out_shape], out_shape=out_shape,
        scratch_shapes=[pltpu.VMEM((PACK_F_ROWS, SLAB), F32)],
        compiler_params=_params(40, 1),
    )(my_index, *_in_hbm(*operands))
    vec_out = [tuple(outs[4 * i:4 * i + 4]) for i in range(n_vec)]
    tail = outs[4 * n_vec:]
    return vec_out, tuple(tail[0:4]), tuple(tail[4:8]), tuple(tail[8:12]), tuple(tail[12:16]), tail[16]


def kernel(x, ln1_g, w_in, pool_w, pool_scale, conv_w, conv_b, w_a, b_a, w_i, b_i, lam, gn_pool_g, gn_lru_g, w_out, ln2_g, w_ffn_gate, w_ffn_up, w_ffn_down, lnf_g, loss_target, m_ln1_g, m_w_in, m_pool_w, m_pool_scale, m_conv_w, m_conv_b, m_w_a, m_b_a, m_w_i, m_b_i, m_lam, m_gn_pool_g, m_gn_lru_g, m_w_out, m_ln2_g, m_w_ffn_gate, m_w_ffn_up, m_w_ffn_down, m_lnf_g, v_ln1_g, v_w_in, v_pool_w, v_pool_scale, v_conv_w, v_conv_b, v_w_a, v_b_a, v_w_i, v_b_i, v_lam, v_gn_pool_g, v_gn_lru_g, v_w_out, v_ln2_g, v_w_ffn_gate, v_w_ffn_up, v_w_ffn_down, v_lnf_g):
    weights = dict(ln1_g=ln1_g, w_in=w_in, pool_w=pool_w, pool_scale=pool_scale, conv_w=conv_w, conv_b=conv_b,
                   w_a=w_a, b_a=b_a, w_i=w_i, b_i=b_i, lam=lam, gn_pool_g=gn_pool_g, gn_lru_g=gn_lru_g,
                   w_out=w_out, ln2_g=ln2_g, w_ffn_gate=w_ffn_gate, w_ffn_up=w_ffn_up, w_ffn_down=w_ffn_down,
                   lnf_g=lnf_g)
    mom1 = dict(ln1_g=m_ln1_g, w_in=m_w_in, pool_w=m_pool_w, pool_scale=m_pool_scale, conv_w=m_conv_w,
                conv_b=m_conv_b, w_a=m_w_a, b_a=m_b_a, w_i=m_w_i, b_i=m_b_i, lam=m_lam, gn_pool_g=m_gn_pool_g,
                gn_lru_g=m_gn_lru_g, w_out=m_w_out, ln2_g=m_ln2_g, w_ffn_gate=m_w_ffn_gate,
                w_ffn_up=m_w_ffn_up, w_ffn_down=m_w_ffn_down, lnf_g=m_lnf_g)
    mom2 = dict(ln1_g=v_ln1_g, w_in=v_w_in, pool_w=v_pool_w, pool_scale=v_pool_scale, conv_w=v_conv_w,
                conv_b=v_conv_b, w_a=v_w_a, b_a=v_b_a, w_i=v_w_i, b_i=v_b_i, lam=v_lam, gn_pool_g=v_gn_pool_g,
                gn_lru_g=v_gn_lru_g, w_out=v_w_out, ln2_g=v_ln2_g, w_ffn_gate=v_w_ffn_gate,
                w_ffn_up=v_w_ffn_up, w_ffn_down=v_w_ffn_down, lnf_g=v_lnf_g)

    xs = x[0]
    target = loss_target[0]

    shard = dict(w_in=lambda a: a[0].T, w_ffn_gate=lambda a: a[0].T, w_ffn_up=lambda a: a[0].T,
                 w_out=lambda a: a[0], w_ffn_down=lambda a: a[0], conv_w=lambda a: a[0])
    unshard = dict(w_in=lambda a: a.T[None], w_ffn_gate=lambda a: a.T[None], w_ffn_up=lambda a: a.T[None],
                   w_out=lambda a: a[None], w_ffn_down=lambda a: a[None], conv_w=lambda a: a[None])

    gathered = ("w_in", "conv_w", "w_out", "w_ffn_gate", "w_ffn_up", "w_ffn_down")
    groups = ((0, 1), (2,), (3, 4), (5,))
    sources = [shard[k](weights[k]) if k == "conv_w" else shard[k](weights[k]).astype(BF16) for k in gathered]
    my_index = 4 * lax.axis_index("x") + 2 * lax.axis_index("y") + lax.axis_index("c")
    lands = [lax.empty((N_DEV,) + a.shape, a.dtype) for a in sources]

    def first_hop(n, at=0):
        return _numbered(_fan_out(range(at, at + n), range(at + n, at + 2 * n), SAME_CORE + SIBLING + MYSELF, 0))

    def second_hop(n, at=0):
        return _numbered(_relay(range(at, at + n), SAME_CORE))

    start = []
    for g, members in enumerate(groups):
        start += _numbered(_fan_out(members, [6 + m for m in members], SAME_CORE + SIBLING + MYSELF, 0), g)
    sems, bufs, _ = _comm_call("gather_start", sources + lands,
                               start=(start, [sum(cp.group == g for cp in start) for g in range(len(groups))]))
    sources, lands = bufs[:6], bufs[6:]

    def group_buffers(g):
        return [sources[m] for m in groups[g]] + [lands[m] for m in groups[g]]

    def relayed(tag, g, after):
        n = len(groups[g])
        relay = second_hop(n, n)
        relay_sems, bufs, _ = _comm_call("gather_relay_" + tag, group_buffers(g), wait=(sems[g], first_hop(n)),
                                         start=(relay, (len(relay),)), after=after)
        return relay_sems[0], bufs[n:]

    relay_sem, bufs = relayed("in", 0, None)
    g_in, g_conv = _comm_call("gather_wait_in", bufs, wait=(relay_sem, second_hop(2)))[1]
    w_in_f = g_in.reshape(IN_WIDTH, D_MODEL)
    conv_w_f = _cols_from_stack(g_conv)

    wa_bd = _block_diag(w_a[0])
    wi_bd = _block_diag(w_i[0])
    lnf_row = lnf_g.reshape(1, D_MODEL)

    u_pool, u_lru, u_gate = _fwd_in(xs, ln1_g, w_in_f)
    y_pool, h, y_lru, decay, root = _mixer_fwd(u_pool, u_lru, u_gate, pool_w[0], pool_scale, conv_w_f, conv_b,
                                               wa_bd, b_a, wi_bd, b_i, lam)
    relay_sem, bufs = relayed("out", 1, y_pool)
    gate_up = second_hop(2, 3)
    (gate_up_sem,), bufs, _ = _comm_call("gather_wait_out", bufs + group_buffers(2),
                                         wait=[(relay_sem, second_hop(1)), (sems[2], first_hop(2, 1))],
                                         start=(gate_up, (len(gate_up),)))
    g_out, gate_up_lands = bufs[0], bufs[3:]
    w_out_f = g_out.reshape(D_MODEL, D_MODEL)
    h1, n2 = _fwd_out(xs, y_pool, y_lru, gn_pool_g, gn_lru_g, w_out_f, ln2_g)
    relay_sem, bufs = relayed("down", 3, n2)
    g_gate, g_up, g_down = _comm_call("gather_wait_ffn", gate_up_lands + bufs,
                                      wait=[(gate_up_sem, second_hop(2)), (relay_sem, second_hop(1, 2))])[1]
    w_gate_f = g_gate.reshape(D_FF, D_MODEL)
    w_up_f = g_up.reshape(D_FF, D_MODEL)
    w_down_f = g_down.reshape(D_FF, D_MODEL)
    g_act, u_act, dh2, dh2b, d_lnf, sq = _ffn_fwd(h1, n2, target, lnf_row, w_gate_f, w_up_f, w_down_f)

    def direct(tag, stacks, wholes, after):
        sources = list(stacks) + list(wholes)
        n, n_st = len(sources), len(stacks)
        lands = [lax.empty(a.shape if i < n_st else (N_DEV,) + a.shape, a.dtype) for i, a in enumerate(sources)]
        copies = _numbered(_scatter(range(n_st), range(n, n + n_st), EVERYONE + MYSELF)
                           + _fan_out(range(n_st, n), range(n + n_st, 2 * n), EVERYONE + MYSELF, 0))
        sem, bufs, token = _comm_call(tag + "_start", sources + lands, start=(copies, (len(copies),)), after=after)
        return (tag, sem[0], bufs, copies), token

    def direct_finish(started, after):
        tag, sem, bufs, copies = started
        _, bufs, _ = _comm_call(tag + "_wait", bufs, wait=(sem, copies), after=after)
        return bufs[len(bufs) // 2:]

    place = jnp.stack([lax.axis_index("c"), 2 * lax.axis_index("x") + lax.axis_index("y")]).astype(jnp.int32)

    d_gate, d_up, d_down, dn2 = _ffn_bwd(n2, dh2b, g_act, u_act, w_gate_f, w_up_f, w_down_f)
    down_stack, *ffn_stacks = [d.reshape(N_DEV, D_FF // N_DEV, D_MODEL) for d in (d_down, d_gate, d_up)]
    pair_lands = [lax.empty((N_DEV // 2,) + a.shape[1:], a.dtype) for a in ffn_stacks]
    pair_copies = _numbered(_to_sibling(range(2), range(2, 4)))
    down_copies = _numbered(_scatter([4], [5], EVERYONE + MYSELF), 1)
    (sem, down_sem), bufs, token = _comm_call(
        "ffn_grads_start", ffn_stacks + pair_lands + [down_stack, lax.empty(down_stack.shape, down_stack.dtype)],
        start=(pair_copies + down_copies, (len(pair_copies), len(down_copies))))
    bufs, down_bufs = bufs[:4], bufs[4:]
    dh1, dy_pool, dy_lru, d_out, d_ln2, d_gnp, d_gnl = _bwd_out(dn2, dh2, h1, y_pool, y_lru, gn_pool_g, gn_lru_g,
                                                                 w_out_f, ln2_g, token)
    _, bufs, _ = _comm_call("ffn_pair_wait", bufs, wait=(sem, pair_copies), after=dh1)
    ffn_sums, ffn_lands = _pair_sum(bufs[:2], bufs[2:], place)
    ffn_copies = _numbered(_to_chips(range(2), range(2, 4)))
    ffn_sem, ffn_bufs, token = _comm_call("ffn_chip_start", ffn_sums + ffn_lands, start=(ffn_copies, (len(ffn_copies),)))
    (du_pool, du_lru, du_gate, d_pw, d_ps, d_cw, d_cb, d_wa, d_ba, d_wi, d_bi, d_lam) = _mixer_bwd(
        u_pool, u_lru, u_gate, h, decay, root, dy_pool, dy_lru, pool_w[0], pool_scale, conv_w_f, conv_b,
        wa_bd, b_a, wi_bd, b_i, lam, token)

    vec_grads = dict(ln2_g=d_ln2, lnf_g=d_lnf, pool_scale=d_ps, conv_b=d_cb, b_a=d_ba, b_i=d_bi,
                     lam=d_lam, gn_pool_g=d_gnp, gn_lru_g=d_gnl)
    packed, packed_b = _pack_small([vec_grads[k] for k, _ in VECTORS], d_pw, d_wa, d_wi, d_cw, sq)
    small_started, token = direct("small", [d_out.reshape(N_DEV, D_MODEL // N_DEV, D_MODEL)], [packed, packed_b], None)
    grad_x, d_in, d_ln1 = _bwd_in(xs, dh1, du_pool, du_lru, du_gate, ln1_g, w_in_f, token)
    in_started, token = direct("in", [d_in.reshape(N_DEV, IN_WIDTH // N_DEV, D_MODEL)], [d_ln1], None)

    results = {}

    def reduce_adam(name, parts):
        outs = _reduce_adam(parts, shard[name](weights[name]), shard[name](mom1[name]), shard[name](mom2[name]),
                            "adam_" + name)
        results[name] = tuple(unshard[name](o) for o in outs)
        return outs[0]

    shard["ln1_g"] = unshard["ln1_g"] = lambda a: a
    _, ffn_bufs, _ = _comm_call("ffn_chip_wait", ffn_bufs, wait=(ffn_sem[0], ffn_copies), after=token)
    done = [reduce_adam(name, parts) for name, parts in zip(("w_ffn_gate", "w_ffn_up"), ffn_bufs[2:])]
    _, (_, r_down), _ = _comm_call("down_wait", down_bufs, after=done,
                                   wait=(down_sem, _numbered(_scatter([0], [1], EVERYONE + MYSELF))))
    done.append(reduce_adam("w_ffn_down", r_down))
    r_out, r_small, r_small_b = direct_finish(small_started, done)
    done.append(reduce_adam("w_out", r_out))

    def as_row(a, width):
        return a.reshape(1, width)

    def as_matrix(a):
        return a.reshape(MATRIX_ROWS, SLAB)

    def as_heads(a):
        return a.reshape(2 * GATE_ROWS, HEAD)

    states = (weights, mom1, mom2)
    vec_out, pool_out, wa_out, wi_out, conv_out, loss_11 = _small_reduce_adam(
        my_index.astype(jnp.int32).reshape(1, 1), r_small, r_small_b,
        *[[as_row(t[k], w) for k, w in VECTORS] for t in states], [as_matrix(t["pool_w"]) for t in states],
        [as_heads(t["w_a"]) for t in states], [as_heads(t["w_i"]) for t in states], [t["conv_w"][0] for t in states])
    for (k, _), outs in zip(VECTORS, vec_out):
        results[k] = tuple(o.reshape(weights[k].shape) for o in outs)
    for k, outs in (("pool_w", pool_out), ("w_a", wa_out), ("w_i", wi_out), ("conv_w", conv_out)):
        results[k] = tuple(o.reshape(weights[k].shape) for o in outs)
    loss = loss_11[0, 0]
    r_in, r_ln1 = direct_finish(in_started, done + [loss_11])
    reduce_adam("w_in", r_in)
    reduce_adam("ln1_g", r_ln1)

    order = ["ln1_g", "w_in", "pool_w", "pool_scale", "conv_w", "conv_b", "w_a", "b_a", "w_i", "b_i", "lam",
             "gn_pool_g", "gn_lru_g", "w_out", "ln2_g", "w_ffn_gate", "w_ffn_up", "w_ffn_down", "lnf_g"]
    return (loss, grad_x[None],
            *[results[k][0] for k in order], *[results[k][1] for k in order],
            *[results[k][2] for k in order], *[results[k][3] for k in order])
```

```python
from typing import Any, NamedTuple

import jax
import jax.numpy as jnp
from jax import lax
from jax.experimental import pallas as pl
from jax.experimental.pallas import tpu as pltpu

F32 = jnp.float32
BF16 = jnp.bfloat16

N_DEV = 8
D_MODEL = 1024
POOL_WIDTH = 512
LRU_WIDTH = 512
IN_WIDTH = 1536
D_FF = 2816
N_SLAB = 4
SLAB = 128
CONV_WIDTH = 4
LRU_C = 8.0
EPS = 1e-6
HALO = 16
FF_CHUNK = 256

ADAM_LR = 0.001
ADAM_B1 = 0.9
ADAM_B2 = 0.999
ADAM_EPS = 1e-08
ADAM_WD = 0.01
ADAM_STEP = 10

MIB = 1 << 20
MESH = pl.DeviceIdType.MESH


def _params(vmem_mib, n_axes=0):
    sem = ("arbitrary",) * n_axes if n_axes else None
    return pltpu.CompilerParams(dimension_semantics=sem, vmem_limit_bytes=vmem_mib * MIB)


def _in_hbm(*arrays):
    return [pltpu.with_memory_space_constraint(a, pltpu.HBM) for a in arrays]


def _mm(a, b):
    return jnp.dot(a, b, preferred_element_type=F32)


def _mm_nt(a, b):
    return lax.dot_general(a, b, (((1,), (1,)), ((), ())), preferred_element_type=F32)


def _mm_tn(a, b):
    return lax.dot_general(a, b, (((0,), (0,)), ((), ())), preferred_element_type=F32)


def _rms(x, g):
    rstd = lax.rsqrt(jnp.mean(x * x, axis=-1, keepdims=True) + EPS)
    xhat = x * rstd
    return xhat * g, xhat, rstd


def _rms_bwd(dy, xhat, rstd, g):
    gy = dy * g
    dx = rstd * (gy - xhat * jnp.mean(gy * xhat, axis=-1, keepdims=True))
    return dx, jnp.sum(dy * xhat, axis=0, keepdims=True)


def _gelu(z):
    t = jnp.tanh(0.7978845608028654 * (z + 0.044715 * z * z * z))
    return 0.5 * z * (1.0 + t), t


def _gelu_grad(z, t):
    return 0.5 * (1.0 + t) + 0.5 * z * (1.0 - t * t) * 0.7978845608028654 * (1.0 + 3.0 * 0.044715 * z * z)


def _softplus_neg(lam):
    x = -lam
    e = jnp.exp(-jnp.abs(x))
    u = 1.0 + e
    l1p = jnp.where(u == 1.0, e, jnp.log(u) * e / (u - 1.0))
    return jnp.maximum(x, 0.0) + l1p


def _one_minus_square(la, a):
    x = 2.0 * la
    series = -x * (1.0 + x * (0.5 + x * (1.0 / 6.0 + x * (1.0 / 24.0))))
    return jnp.where(x > -0.06, series, 1.0 - a * a)


def _sigmoid(x):
    return 0.5 * jnp.tanh(0.5 * x) + 0.5


def _down(v, d):
    return pltpu.roll(v, d, 0)


def _up(v, d):
    return pltpu.roll(v, v.shape[0] - d, 0)


def _slab_major(s, dtype=F32):
    return pltpu.HBM((N_SLAB, s, SLAB), dtype)


def _slabs_spec(tb):
    return pl.BlockSpec((N_SLAB, tb, SLAB), lambda i: (0, i, 0))


def _read_slabs(ref):
    return jnp.concatenate([ref[k] for k in range(N_SLAB)], axis=1)


def _write_slabs(ref, value):
    for k in range(N_SLAB):
        ref[k] = value[:, k * SLAB:(k + 1) * SLAB]


def _token_block(s, most=512):
    for rows in (most, 512, 256):
        if rows <= most and s % rows == 0 and s > rows:
            return rows
    return s


def _time_chunk(s):
    return 512 if s % 512 == 0 and s > 512 else 256 if s % 256 == 0 else s


def _fwd_in(x, ln1_g, w_in_t):
    s = x.shape[0]
    tb = _token_block(s, 1024)

    def body(x_ref, g_ref, w_ref, up_ref, ul_ref, ug_ref):
        n, _, _ = _rms(x_ref[...], g_ref[...])
        proj = _mm_nt(n.astype(BF16), w_ref[...])
        _write_slabs(up_ref, proj[:, :POOL_WIDTH])
        _write_slabs(ul_ref, proj[:, POOL_WIDTH:POOL_WIDTH + LRU_WIDTH])
        _write_slabs(ug_ref, proj[:, POOL_WIDTH + LRU_WIDTH:])

    out = _slab_major(s)
    return pl.pallas_call(
        body, name="fwd_in", grid=(s // tb,),
        in_specs=[pl.BlockSpec((tb, D_MODEL), lambda i: (i, 0)),
                  pl.BlockSpec((1, D_MODEL), lambda i: (0, 0)),
                  pl.BlockSpec((IN_WIDTH, D_MODEL), lambda i: (0, 0))],
        out_specs=[_slabs_spec(tb)] * 3,
        out_shape=[out, out, out],
        compiler_params=_params(40, 1),
    )(*_in_hbm(x, ln1_g, w_in_t))


def _pool_denominator(t0, row, window):
    return jnp.minimum((t0 + row + 1).astype(F32), window)


def _causal_window(ext, deeper):
    s = ext + _down(ext, 1)
    s = s + deeper[0] * _down(s, 2)
    s = s + deeper[1] * _down(s, 4)
    s = s + deeper[2] * _down(s, 8)
    return s[HALO:]


def _anticausal_window(ext, deeper, rows):
    s = ext + _up(ext, 1)
    s = s + deeper[0] * _up(s, 2)
    s = s + deeper[1] * _up(s, 4)
    s = s + deeper[2] * _up(s, 8)
    return s[:rows]


def _conv(taps, cw, cb):
    return cw[3] * taps[0] + cw[2] * taps[1] + cw[1] * taps[2] + cw[0] * taps[3] + cb


def _lru_gates(xc, wa, ba, wi, bi):
    xb = xc.astype(BF16)
    r = _sigmoid(_mm(xb, wa) + ba)
    i = _sigmoid(_mm(xb, wi) + bi)
    return xb, r, i


def _lru_decay(r, sp):
    la = (-LRU_C) * r * sp
    a = jnp.exp(la)
    return a, jnp.sqrt(jnp.maximum(_one_minus_square(la, a), 0.0))


SUBLANES = 8


def _scan_pads(a_pad, b_pad, rows, causal):
    spare = slice(0, SUBLANES) if causal else slice(rows, rows + SUBLANES)
    a_pad[spare, :] = jnp.ones((SUBLANES, SLAB), F32)
    b_pad[spare, :] = jnp.zeros((SUBLANES, SLAB), F32)


def _scan_causal(a, b, h_prev, a_pad, b_pad, rows):
    d = 1
    while d < min(SUBLANES, rows):
        a_pad[SUBLANES:, :] = a
        b_pad[SUBLANES:, :] = b
        b = a * b_pad[SUBLANES - d:SUBLANES - d + rows, :] + b
        a = a * a_pad[SUBLANES - d:SUBLANES - d + rows, :]
        d *= 2
    h = jnp.broadcast_to(h_prev, (d, SLAB))
    out = []
    for at in range(0, rows, d):
        h = a[at:at + d] * h + b[at:at + d]
        out.append(h)
    return jnp.concatenate(out, axis=0)


def _scan_anticausal(a, b, l_next, a_pad, b_pad, rows):
    d = 1
    while d < min(SUBLANES, rows):
        a_pad[:rows, :] = a
        b_pad[:rows, :] = b
        b = a * b_pad[d:d + rows, :] + b
        a = a * a_pad[d:d + rows, :]
        d *= 2
    lead = jnp.broadcast_to(l_next, (d, SLAB))
    out = []
    for at in range(rows - d, -1, -d):
        lead = a[at:at + d] * lead + b[at:at + d]
        out.append(lead)
    return jnp.concatenate(out[::-1], axis=0)


def _slab_scalars():
    slab = pl.program_id(0)
    deeper = [jnp.where(slab > k, 1.0, 0.0).astype(F32) for k in range(N_SLAB - 1)]
    window = jnp.left_shift(jnp.int32(2), slab).astype(F32)
    inverse = jnp.where(slab == 0, 0.5, jnp.where(slab == 1, 0.25, jnp.where(slab == 2, 0.125, 0.0625))).astype(F32)
    return deeper, (window, inverse)


def _window_mean(total, t0, row, window, at_start):
    if at_start:
        return total / _pool_denominator(t0, row, window[0])
    return total * window[1]


def _slab_specs(s):
    seq = pl.BlockSpec((None, s, SLAB), lambda k: (k, 0, 0))
    mat = pl.BlockSpec((1, SLAB, SLAB), lambda k: (k, 0, 0))
    vec = pl.BlockSpec((1, SLAB), lambda k: (0, k))
    taps = pl.BlockSpec((CONV_WIDTH, SLAB), lambda k: (0, k))
    return seq, mat, vec, taps


def _mixer_fwd(u_pool, u_lru, u_gate, pool_w, pool_scale, conv_w, conv_b, wa_bd, b_a, wi_bd, b_i, lam):
    s = u_pool.shape[1]
    tc = _time_chunk(s)
    n_chunks = s // tc

    def body(up_ref, ul_ref, ug_ref, pw_ref, ps_ref, cw_ref, cb_ref, wa_ref, ba_ref, wi_ref, bi_ref, lam_ref,
             yp_ref, h_ref, yl_ref, a_ref, mult_ref, shift_ref, a_pad, b_pad):
        _scan_pads(a_pad, b_pad, tc, causal=True)
        deeper, window = _slab_scalars()
        pw = pw_ref[0].astype(BF16)
        wa = wa_ref[0].astype(BF16)
        wi = wi_ref[0].astype(BF16)
        ps, cb, ba, bi = ps_ref[...], cb_ref[...], ba_ref[...], bi_ref[...]
        cw = [cw_ref[k:k + 1, :] for k in range(CONV_WIDTH)]
        sp = _softplus_neg(lam_ref[...])
        row = lax.broadcasted_iota(jnp.int32, (tc, SLAB), 0)

        def chunk(t0, ext_p, ext_l, h_prev, at_start=False):
            rows = pl.ds(t0, tc)
            d = _window_mean(_causal_window(ext_p, deeper), t0, row, window, at_start) - ext_p[HALO:]
            yp_ref[rows, :] = _mm(d.astype(BF16), pw) * ps
            shift_ref[...] = ext_l
            xc = _conv([shift_ref[HALO - k:HALO - k + tc, :] for k in range(CONV_WIDTH)], cw, cb)
            _, r, i = _lru_gates(xc, wa, ba, wi, bi)
            a, mult = _lru_decay(r, sp)
            a_ref[rows, :] = a
            mult_ref[rows, :] = mult
            h = _scan_causal(a, mult * (i * xc), h_prev, a_pad, b_pad, tc)
            h_ref[rows, :] = h
            yl_ref[rows, :] = h * _gelu(ug_ref[rows, :])[0]
            return h[tc - 1:tc, :]

        pad = jnp.zeros((HALO, SLAB), F32)
        h0 = chunk(0, jnp.concatenate([pad, up_ref[pl.ds(0, tc), :]], axis=0),
                   jnp.concatenate([pad, ul_ref[pl.ds(0, tc), :]], axis=0), jnp.zeros((1, SLAB), F32), at_start=True)

        def step(c, h_prev):
            t0 = pl.multiple_of(c * tc, tc)
            ext = pl.ds(pl.multiple_of(c * tc - HALO, HALO), tc + HALO)
            return chunk(t0, up_ref[ext, :], ul_ref[ext, :], h_prev)

        lax.fori_loop(1, n_chunks, step, h0)

    seq, mat, vec, taps = _slab_specs(s)
    out = _slab_major(s)
    return pl.pallas_call(
        body, name="mixer_fwd", grid=(N_SLAB,),
        in_specs=[seq, seq, seq, mat, vec, taps, vec, mat, vec, mat, vec, vec],
        out_specs=[seq] * 5, out_shape=[out] * 5,
        scratch_shapes=[pltpu.VMEM((tc + HALO, SLAB), F32), pltpu.VMEM((tc + SUBLANES, SLAB), F32),
                        pltpu.VMEM((tc + SUBLANES, SLAB), F32)],
        compiler_params=_params(48, 1),
    )(*_in_hbm(u_pool, u_lru, u_gate, pool_w, pool_scale, conv_w, conv_b, wa_bd, b_a, wi_bd, b_i, lam))


def _fwd_out(x, y_pool, y_lru, gn_pool_g, gn_lru_g, w_out, ln2_g):
    s = x.shape[0]
    tb = _token_block(s, 1024)

    def body(x_ref, yp_ref, yl_ref, gp_ref, gl_ref, w_ref, g2_ref, h1_ref, n2_ref):
        mp, _, _ = _rms(_read_slabs(yp_ref), gp_ref[...])
        ml, _, _ = _rms(_read_slabs(yl_ref), gl_ref[...])
        h1 = x_ref[...] + _mm(mp.astype(BF16), w_ref[:POOL_WIDTH, :]) + _mm(ml.astype(BF16), w_ref[POOL_WIDTH:, :])
        h1_ref[...] = h1
        n2_ref[...] = _rms(h1, g2_ref[...])[0].astype(BF16)

    row = pl.BlockSpec((tb, D_MODEL), lambda i: (i, 0))
    half = _slabs_spec(tb)
    return pl.pallas_call(
        body, name="fwd_out", grid=(s // tb,),
        in_specs=[row, half, half, pl.BlockSpec((1, 512), lambda i: (0, 0)), pl.BlockSpec((1, 512), lambda i: (0, 0)),
                  pl.BlockSpec((D_MODEL, D_MODEL), lambda i: (0, 0)), pl.BlockSpec((1, D_MODEL), lambda i: (0, 0))],
        out_specs=[row, row],
        out_shape=[pltpu.HBM((s, D_MODEL), F32), pltpu.HBM((s, D_MODEL), BF16)],
        compiler_params=_params(40, 1),
    )(*_in_hbm(x, y_pool, y_lru, gn_pool_g, gn_lru_g, w_out, ln2_g))


def _ffn_fwd(h1, n2, target, lnf_g, w_gate, w_up, w_down):
    s = h1.shape[0]
    tb = 512
    sub = 256
    n_ff = D_FF // FF_CHUNK

    def body(h1_ref, n2_ref, t_ref, gf_ref, wg_hbm, wu_hbm, wd_hbm,
             g_ref, u_ref, dh_ref, dhb_ref, dgf_ref, sq_ref, wg, wu, wd, sem):
        @pl.when(pl.program_id(0) == 0)
        def _():
            loads = [pltpu.make_async_copy(src, dst, sem.at[k])
                     for k, (src, dst) in enumerate(((wg_hbm, wg), (wu_hbm, wu), (wd_hbm, wd)))]
            for cp in loads:
                cp.start()
            for cp in loads:
                cp.wait()
            dgf_ref[...] = jnp.zeros_like(dgf_ref)
            sq_ref[...] = jnp.zeros_like(sq_ref)

        n2v = n2_ref[...]
        acc = jnp.zeros((tb, D_MODEL), F32)
        for c in range(n_ff):
            cols = slice(c * FF_CHUNK, (c + 1) * FF_CHUNK)
            g = _mm_nt(n2v, wg[cols, :])
            u = _mm_nt(n2v, wu[cols, :])
            g_ref[:, cols] = g.astype(BF16)
            u_ref[:, cols] = u.astype(BF16)
            act = g * jax.nn.sigmoid(g) * u
            acc = acc + _mm(act.astype(BF16), wd[cols, :])
        gf = gf_ref[...]
        for r in range(tb // sub):
            rows = slice(r * sub, (r + 1) * sub)
            y, xhat, rstd = _rms(h1_ref[rows, :] + acc[rows, :], gf)
            err = y - t_ref[rows, :]
            sq_ref[...] += jnp.sum(err * err, axis=0, keepdims=True)
            dh2, dgf = _rms_bwd(err * (1.0 / D_MODEL), xhat, rstd, gf)
            dgf_ref[...] += dgf
            dh_ref[rows, :] = dh2
            dhb_ref[rows, :] = dh2.astype(BF16)

    row = pl.BlockSpec((tb, D_MODEL), lambda i: (i, 0))
    ff = pl.BlockSpec((tb, D_FF), lambda i: (i, 0))
    vec = pl.BlockSpec((1, D_MODEL), lambda i: (0, 0))
    anyspace = pl.BlockSpec(memory_space=pl.ANY)
    return pl.pallas_call(
        body, name="ffn_fwd", grid=(s // tb,),
        in_specs=[row, row, row, vec, anyspace, anyspace, anyspace],
        out_specs=[ff, ff, row, row, vec, vec],
        out_shape=[pltpu.HBM((s, D_FF), BF16), pltpu.HBM((s, D_FF), BF16),
                   pltpu.HBM((s, D_MODEL), F32), pltpu.HBM((s, D_MODEL), BF16),
                   jax.ShapeDtypeStruct((1, D_MODEL), F32), jax.ShapeDtypeStruct((1, D_MODEL), F32)],
        scratch_shapes=[pltpu.VMEM((D_FF, D_MODEL), BF16), pltpu.VMEM((D_FF, D_MODEL), BF16),
                        pltpu.VMEM((D_FF, D_MODEL), BF16), pltpu.SemaphoreType.DMA((3,))],
        compiler_params=_params(60, 1),
    )(*_in_hbm(h1, n2, target, lnf_g, w_gate, w_up, w_down))


def _ffn_bwd(n2, dh2b, g, u, w_gate_t, w_up_t, w_down):
    s = n2.shape[0]
    tb = min(1024, s)
    n_ff = D_FF // FF_CHUNK
    n_tb = s // tb

    def body(n2_ref, dh_ref, g_ref, u_ref, wg_ref, wu_ref, wd_ref, dwg_ref, dwu_ref, dwd_ref, dn2_ref,
             dn2_acc, acc_g, acc_u, acc_d):
        j = pl.program_id(0)
        t = pl.program_id(1)
        rows = pl.ds(pl.multiple_of(t * tb, tb), tb)

        @pl.when(t == 0)
        def _():
            acc_g[...] = jnp.zeros_like(acc_g)
            acc_u[...] = jnp.zeros_like(acc_u)
            acc_d[...] = jnp.zeros_like(acc_d)

        @pl.when(j == 0)
        def _():
            dn2_acc[rows, :] = jnp.zeros((tb, D_MODEL), F32)

        n2v = n2_ref[...]
        dh = dh_ref[...]
        gv = g_ref[...].astype(F32)
        uv = u_ref[...].astype(F32)
        sg = jax.nn.sigmoid(gv)
        silu = gv * sg
        dact = _mm_nt(dh, wd_ref[...])
        dub = (dact * silu).astype(BF16)
        dgb = (dact * uv * (sg * (1.0 + gv * (1.0 - sg)))).astype(BF16)
        acc_d[...] += _mm_tn((silu * uv).astype(BF16), dh)
        acc_g[...] += _mm_tn(dgb, n2v)
        acc_u[...] += _mm_tn(dub, n2v)
        dn2_acc[rows, :] += _mm(jnp.concatenate([dgb, dub], axis=1),
                                jnp.concatenate([wg_ref[...], wu_ref[...]], axis=0))

        @pl.when(t == n_tb - 1)
        def _():
            dwg_ref[...] = acc_g[...].astype(BF16)
            dwu_ref[...] = acc_u[...].astype(BF16)
            dwd_ref[...] = acc_d[...].astype(BF16)

        @pl.when(j == n_ff - 1)
        def _():
            dn2_ref[...] = dn2_acc[rows, :]

    row = pl.BlockSpec((tb, D_MODEL), lambda j, t: (t, 0))
    act = pl.BlockSpec((tb, FF_CHUNK), lambda j, t: (t, j))
    w_row = pl.BlockSpec((FF_CHUNK, D_MODEL), lambda j, t: (j, 0))
    last = pl.BlockSpec((tb, D_MODEL), lambda j, t: (jnp.where(j == n_ff - 1, t, 0), 0))
    grad = pltpu.HBM((D_FF, D_MODEL), BF16)
    chunk_acc = pltpu.VMEM((FF_CHUNK, D_MODEL), F32)
    return pl.pallas_call(
        body, name="ffn_bwd", grid=(n_ff, n_tb),
        in_specs=[row, row, act, act, w_row, w_row, w_row],
        out_specs=[w_row, w_row, w_row, last],
        out_shape=[grad, grad, grad, pltpu.HBM((s, D_MODEL), F32)],
        scratch_shapes=[pltpu.VMEM((s, D_MODEL), F32), chunk_acc, chunk_acc, chunk_acc],
        compiler_params=_params(56, 2),
    )(*_in_hbm(n2, dh2b, g, u, w_gate_t, w_up_t, w_down))


def _bwd_out(dn2, dh2, h1, y_pool, y_lru, gn_pool_g, gn_lru_g, w_out, ln2_g, after):
    s = h1.shape[0]
    tb = _token_block(s)

    n_steps = s // tb
    ring = 3

    def body(dn2_hbm, dh2_hbm, h1_hbm, yp_ref, yl_ref, gp_ref, gl_ref, w_ref, g2_ref, _after,
             dh1_ref, dyp_ref, dyl_ref, dwb_ref, dg2_ref, dgp_ref, dgl_ref, dw_ref,
             dn2_ring, dh2_ring, h1_ring, ring_sem):
        step = pl.program_id(0)

        def fetch(at):
            slot = at % ring
            rows = pl.ds(pl.multiple_of(at * tb, tb), tb)
            return [pltpu.make_async_copy(src.at[rows], dst.at[slot], ring_sem.at[3 * slot + k])
                    for k, (src, dst) in enumerate(((dn2_hbm, dn2_ring), (dh2_hbm, dh2_ring), (h1_hbm, h1_ring)))]

        @pl.when(step == 0)
        def _():
            for at in range(min(ring - 1, n_steps)):
                for cp in fetch(at):
                    cp.start()

        @pl.when(step + ring - 1 < n_steps)
        def _():
            for cp in fetch(step + ring - 1):
                cp.start()

        for cp in fetch(step):
            cp.wait()
        slot = step % ring
        dn2_ref, dh2_ref, h1_ref = dn2_ring.at[slot], dh2_ring.at[slot], h1_ring.at[slot]

        @pl.when(step == 0)
        def _():
            dw_ref[...] = jnp.zeros_like(dw_ref)
            dg2_ref[...] = jnp.zeros_like(dg2_ref)
            dgp_ref[...] = jnp.zeros_like(dgp_ref)
            dgl_ref[...] = jnp.zeros_like(dgl_ref)

        g2 = g2_ref[...]
        _, xhat2, rstd2 = _rms(h1_ref[...], g2)
        dres, dg2 = _rms_bwd(dn2_ref[...], xhat2, rstd2, g2)
        dg2_ref[...] += dg2
        dh1 = dh2_ref[...] + dres
        dh1_ref[...] = dh1
        dh1b = dh1.astype(BF16)
        gp, gl = gp_ref[...], gl_ref[...]
        mp, xhat_p, rstd_p = _rms(_read_slabs(yp_ref), gp)
        ml, xhat_l, rstd_l = _rms(_read_slabs(yl_ref), gl)
        dw_ref[:POOL_WIDTH, :] += _mm_tn(mp.astype(BF16), dh1b)
        dw_ref[POOL_WIDTH:, :] += _mm_tn(ml.astype(BF16), dh1b)
        dyp, dgp = _rms_bwd(_mm_nt(dh1b, w_ref[:POOL_WIDTH, :]), xhat_p, rstd_p, gp)
        dyl, dgl = _rms_bwd(_mm_nt(dh1b, w_ref[POOL_WIDTH:, :]), xhat_l, rstd_l, gl)
        _write_slabs(dyp_ref, dyp)
        _write_slabs(dyl_ref, dyl)
        dgp_ref[...] += dgp
        dgl_ref[...] += dgl

        @pl.when(pl.program_id(0) == s // tb - 1)
        def _():
            dwb_ref[...] = dw_ref[...].astype(BF16)

    row = pl.BlockSpec((tb, D_MODEL), lambda i: (i, 0))
    half = _slabs_spec(tb)
    vec = pl.BlockSpec((1, D_MODEL), lambda i: (0, 0))
    hvec = pl.BlockSpec((1, 512), lambda i: (0, 0))
    mat = pl.BlockSpec((D_MODEL, D_MODEL), lambda i: (0, 0))
    return pl.pallas_call(
        body, name="bwd_out", grid=(s // tb,),
        in_specs=[pl.BlockSpec(memory_space=pl.ANY)] * 3
        + [half, half, hvec, hvec, mat, vec, pl.BlockSpec(memory_space=pl.ANY)],
        out_specs=[row, half, half, mat, vec, hvec, hvec],
        out_shape=[pltpu.HBM((s, D_MODEL), F32), _slab_major(s),
                   _slab_major(s), pltpu.HBM((D_MODEL, D_MODEL), BF16),
                   jax.ShapeDtypeStruct((1, D_MODEL), F32), jax.ShapeDtypeStruct((1, 512), F32),
                   jax.ShapeDtypeStruct((1, 512), F32)],
        scratch_shapes=[pltpu.VMEM((D_MODEL, D_MODEL), F32)] + [pltpu.VMEM((ring, tb, D_MODEL), F32)] * 3
        + [pltpu.SemaphoreType.DMA((3 * ring,))],
        compiler_params=_params(56, 1),
    )(*_in_hbm(dn2, dh2, h1, y_pool, y_lru, gn_pool_g, gn_lru_g, w_out, ln2_g), after)


def _mixer_bwd(u_pool, u_lru, u_gate, h, decay, root, dy_pool, dy_lru,
               pool_w, pool_scale, conv_w, conv_b, wa_bd, b_a, wi_bd, b_i, lam, after):
    s = u_pool.shape[1]
    tc = _time_chunk(s)
    n_chunks = s // tc

    def body(up_ref, ul_ref, ug_ref, h_ref, a_ref, mult_ref, dyp_ref, dyl_ref,
             pw_ref, ps_ref, cw_ref, cb_ref, wa_ref, ba_ref, wi_ref, bi_ref, lam_ref, _after,
             dup_ref, dul_ref, dug_ref, dpw_ref, dps_ref, dcw_ref, dcb_ref, dwa_ref, dba_ref, dwi_ref, dbi_ref, dlam_ref,
             a_pad, b_pad, u_shift, h_shift, a_shift, d_shift):
        _scan_pads(a_pad, b_pad, tc, causal=False)
        deeper, window = _slab_scalars()
        pw = pw_ref[0].astype(BF16)
        wa = wa_ref[0].astype(BF16)
        wi = wi_ref[0].astype(BF16)
        ps, cb, ba, bi = ps_ref[...], cb_ref[...], ba_ref[...], bi_ref[...]
        cw = [cw_ref[k:k + 1, :] for k in range(CONV_WIDTH)]
        lam_v = lam_ref[...]
        sp = _softplus_neg(lam_v)
        row = lax.broadcasted_iota(jnp.int32, (tc, SLAB), 0)
        for ref in (dpw_ref, dps_ref, dcw_ref, dcb_ref, dwa_ref, dba_ref, dwi_ref, dbi_ref, dlam_ref):
            ref[...] = jnp.zeros_like(ref)

        def chunk(t0, ext_p, ext_l, ext_h, carry, at_start=False):
            l_next, a_next, dxc_next, ddn_next = carry
            rows = pl.ds(t0, tc)
            u_shift[...] = ext_l
            taps = [u_shift[HALO - k:HALO - k + tc, :] for k in range(CONV_WIDTH)]
            xc = _conv(taps, cw, cb)
            xb, r, i = _lru_gates(xc, wa, ba, wi, bi)
            a, mult = a_ref[rows, :], mult_ref[rows, :]
            hv = ext_h[HALO:]
            h_shift[...] = ext_h
            h_before = h_shift[HALO - 1:HALO - 1 + tc, :]
            ug = ug_ref[rows, :]
            dyl = dyl_ref[rows, :]
            gel, th = _gelu(ug)
            dug_ref[rows, :] = (dyl * hv * _gelu_grad(ug, th)).astype(BF16)
            a_shift[:tc, :] = a
            a_shift[tc:, :] = jnp.broadcast_to(a_next, (SUBLANES, SLAB))
            a_after = a_shift[1:1 + tc, :]
            l = _scan_anticausal(a_after, dyl * gel, l_next, a_pad, b_pad, tc)
            dmult = l * (i * xc)
            di = l * mult * xc
            dxc = l * mult * i
            dla = (l * h_before) * a - jnp.where(mult > 0.0, dmult * (a * a) / mult, 0.0)
            dlam_ref[...] += jnp.sum(dla * r, axis=0, keepdims=True)
            dpa = (dla * ((-LRU_C) * sp)) * (r * (1.0 - r))
            dpi = di * (i * (1.0 - i))
            dpab = dpa.astype(BF16)
            dpib = dpi.astype(BF16)
            dwa_ref[0] += _mm_tn(xb, dpab)
            dwi_ref[0] += _mm_tn(xb, dpib)
            dba_ref[...] += jnp.sum(dpa, axis=0, keepdims=True)
            dbi_ref[...] += jnp.sum(dpi, axis=0, keepdims=True)
            dxc = dxc + _mm_nt(dpab, wa) + _mm_nt(dpib, wi)
            d_shift[:tc, :] = dxc
            d_shift[tc:, :] = dxc_next
            dul_ref[rows, :] = (cw[3] * dxc + cw[2] * d_shift[1:1 + tc, :]
                                + cw[1] * d_shift[2:2 + tc, :] + cw[0] * d_shift[3:3 + tc, :]).astype(BF16)
            for k in range(CONV_WIDTH):
                dcw_ref[k:k + 1, :] += jnp.sum(dxc * taps[CONV_WIDTH - 1 - k], axis=0, keepdims=True)
            dcb_ref[...] += jnp.sum(dxc, axis=0, keepdims=True)
            db = (_window_mean(_causal_window(ext_p, deeper), t0, row, window, at_start) - ext_p[HALO:]).astype(BF16)
            dyp = dyp_ref[rows, :]
            dps_ref[...] += jnp.sum(dyp * _mm(db, pw), axis=0, keepdims=True)
            dys = (dyp * ps).astype(BF16)
            dpw_ref[0] += _mm_tn(db, dys)
            dd = _mm_nt(dys, pw)
            ddn = _window_mean(dd, t0, row, window, at_start)
            ext_q = jnp.concatenate([ddn, ddn_next], axis=0)
            dup_ref[rows, :] = (_anticausal_window(ext_q, deeper, tc) - dd).astype(BF16)
            return l[0:1, :], a[0:1, :], dxc[0:8, :], ddn[0:HALO, :]

        def step(k, carry):
            c = n_chunks - 1 - k
            t0 = pl.multiple_of(c * tc, tc)
            ext = pl.ds(pl.multiple_of(c * tc - HALO, HALO), tc + HALO)
            return chunk(t0, up_ref[ext, :], ul_ref[ext, :], h_ref[ext, :], carry)

        carry = (jnp.zeros((1, SLAB), F32), jnp.zeros((1, SLAB), F32),
                 jnp.zeros((8, SLAB), F32), jnp.zeros((HALO, SLAB), F32))
        carry = lax.fori_loop(0, n_chunks - 1, step, carry)
        pad = jnp.zeros((HALO, SLAB), F32)
        first = pl.ds(0, tc)
        chunk(0, jnp.concatenate([pad, up_ref[first, :]], axis=0), jnp.concatenate([pad, ul_ref[first, :]], axis=0),
              jnp.concatenate([pad, h_ref[first, :]], axis=0), carry, at_start=True)
        dlam_ref[...] = dlam_ref[...] * (LRU_C * jax.nn.sigmoid(-lam_v))

    seq, mat, vec, taps = _slab_specs(s)
    grad = _slab_major(s, BF16)
    mats = jax.ShapeDtypeStruct((N_SLAB, SLAB, SLAB), F32)
    vecs = jax.ShapeDtypeStruct((1, 512), F32)
    return pl.pallas_call(
        body, name="mixer_bwd", grid=(N_SLAB,),
        in_specs=[seq] * 8 + [mat, vec, taps, vec, mat, vec, mat, vec, vec, pl.BlockSpec(memory_space=pl.ANY)],
        out_specs=[seq, seq, seq, mat, vec, taps, vec, mat, vec, mat, vec, vec],
        out_shape=[grad, grad, grad, mats, vecs, jax.ShapeDtypeStruct((CONV_WIDTH, 512), F32), vecs,
                   mats, vecs, mats, vecs, vecs],
        scratch_shapes=[pltpu.VMEM((tc + SUBLANES, SLAB), F32), pltpu.VMEM((tc + SUBLANES, SLAB), F32),
                        pltpu.VMEM((tc + HALO, SLAB), F32), pltpu.VMEM((tc + HALO, SLAB), F32),
                        pltpu.VMEM((tc + SUBLANES, SLAB), F32), pltpu.VMEM((tc + SUBLANES, SLAB), F32)],
        compiler_params=_params(56, 1),
    )(*_in_hbm(u_pool, u_lru, u_gate, h, decay, root, dy_pool, dy_lru, pool_w, pool_scale, conv_w, conv_b, wa_bd,
               b_a, wi_bd, b_i, lam), after)


def _bwd_in(x, dh1, du_pool, du_lru, du_gate, ln1_g, w_in_t, after):
    s = x.shape[0]
    tb = _token_block(s)

    def body(x_ref, dh1_ref, dup_ref, dul_ref, dug_ref, g_ref, w_ref, _after, dx_ref, dwb_ref, dg_ref, dw_ref):
        @pl.when(pl.program_id(0) == 0)
        def _():
            dw_ref[...] = jnp.zeros_like(dw_ref)
            dg_ref[...] = jnp.zeros_like(dg_ref)

        g1 = g_ref[...]
        n, xhat, rstd = _rms(x_ref[...], g1)
        nb = n.astype(BF16)
        db = jnp.concatenate([_read_slabs(ref) for ref in (dup_ref, dul_ref, dug_ref)], axis=1)
        dw_ref[...] += _mm_tn(db, nb)
        dx, dg1 = _rms_bwd(_mm(db, w_ref[...]), xhat, rstd, g1)
        dx_ref[...] = dh1_ref[...] + dx
        dg_ref[...] += dg1

        @pl.when(pl.program_id(0) == s // tb - 1)
        def _():
            dwb_ref[...] = dw_ref[...].astype(BF16)

    row = pl.BlockSpec((tb, D_MODEL), lambda i: (i, 0))
    half = _slabs_spec(tb)
    vec = pl.BlockSpec((1, D_MODEL), lambda i: (0, 0))
    mat = pl.BlockSpec((IN_WIDTH, D_MODEL), lambda i: (0, 0))
    return pl.pallas_call(
        body, name="bwd_in", grid=(s // tb,),
        in_specs=[row, row, half, half, half, vec, mat, pl.BlockSpec(memory_space=pl.ANY)],
        out_specs=[row, mat, vec],
        out_shape=[pltpu.HBM((s, D_MODEL), F32), pltpu.HBM((IN_WIDTH, D_MODEL), BF16),
                   jax.ShapeDtypeStruct((1, D_MODEL), F32)],
        scratch_shapes=[pltpu.VMEM((IN_WIDTH, D_MODEL), F32)],
        compiler_params=_params(48, 1),
    )(*_in_hbm(x, dh1, du_pool, du_lru, du_gate, ln1_g, w_in_t), after)


def _mesh_position():
    x, y, c = lax.axis_index("x"), lax.axis_index("y"), lax.axis_index("c")
    return x, y, c, 4 * x + 2 * y + c


def _peer(x, y, c, p):
    px = 1 - x if p & 4 else x
    py = 1 - y if p & 2 else y
    pc = 1 - c if p & 1 else c
    return (px, py, pc), 4 * px + 2 * py + pc


HBM_SPEC = pl.BlockSpec(memory_space=pltpu.HBM)
SEM_SPEC = pl.BlockSpec(memory_space=pltpu.SEMAPHORE)
DATAFLOW = pltpu.SideEffectType.DATAFLOW_SIDE_EFFECTING


class Copy(NamedTuple):
    src: int
    src_at: Any
    dst: int
    dst_at: Any
    peer: int
    group: int
    slot: int


SIBLING = (1,)
SAME_CORE = (2, 4, 6)
EVERYONE = tuple(range(1, N_DEV))
MYSELF = (0,)


def _same(index):
    return index


def _chip(index):
    return jnp.right_shift(index, 1)


def _fan_out(srcs, lands, peers, group):
    return [Copy(s, None, d, _same, p, group, N_DEV * i + p) for i, (s, d) in enumerate(zip(srcs, lands)) for p in peers]


def _scatter(stacks, lands, peers):
    return [Copy(s, lambda me, p=p: jnp.bitwise_xor(me, p), d, _same, p, 0, 0)
            for s, d in zip(stacks, lands) for p in peers]


def _numbered(copies, group=0):
    return [cp._replace(group=group, slot=i) for i, cp in enumerate(copies)]


def _relay(lands, peers):
    return [Copy(b, lambda s, q=q: jnp.bitwise_xor(s, q), b, lambda s, q=q: jnp.bitwise_xor(s, q), 1, 0, N_DEV * i + q)
            for i, b in enumerate(lands) for q in peers]


def _to_sibling(stacks, lands):
    return [Copy(s, lambda me, k=k: 2 * k + 1 - jnp.bitwise_and(me, 1), d, lambda me, k=k: k, 1, 0, 4 * i + k)
            for i, (s, d) in enumerate(zip(stacks, lands)) for k in range(N_DEV // 2)]


def _to_chips(sums, lands):
    return [Copy(s, lambda me, p=p: jnp.bitwise_xor(_chip(me), p // 2), d, _chip, p, 0, 4 * i + p // 2)
            for i, (s, d) in enumerate(zip(sums, lands)) for p in SAME_CORE]


def _comm_call(name, bufs, wait=None, start=None, after=()):
    nb = len(bufs)
    waits = [wait] if isinstance(wait, tuple) else list(wait or [])
    slots = list(start[1]) if start else []
    n_out_sem = 2 * len(slots)
    after = [a for a in (after if isinstance(after, (list, tuple)) else [after]) if a is not None]

    def body(*refs):
        b = refs[:nb]
        at = nb + 2 * len(waits) + len(after)
        out_sems = refs[at:at + n_out_sem]
        token = refs[at + n_out_sem + nb]
        x, y, c, me = _mesh_position()

        def part(i, row_of, sender):
            return b[i] if row_of is None else b[i].at[row_of(sender)]

        for k, (_, copies) in enumerate(waits):
            w_send, w_recv = refs[nb + 2 * k], refs[nb + 2 * k + 1]
            for cp in copies:
                peer, peer_index = _peer(x, y, c, cp.peer)
                arrival = pltpu.make_async_remote_copy(part(cp.src, cp.src_at, me), part(cp.dst, cp.dst_at, peer_index),
                                                       w_send.at[cp.slot], w_recv.at[cp.slot],
                                                       device_id=peer, device_id_type=MESH)
                arrival.wait_send()
                arrival.wait_recv()
        if start:
            for cp in start[0]:
                peer, _ = _peer(x, y, c, cp.peer)
                pltpu.make_async_remote_copy(part(cp.src, cp.src_at, me), part(cp.dst, cp.dst_at, me),
                                             out_sems[2 * cp.group].at[cp.slot], out_sems[2 * cp.group + 1].at[cp.slot],
                                             device_id=peer, device_id_type=MESH).start()
        token[...] = jnp.zeros_like(token)

    sem_shapes = []
    for n_slots in slots:
        sem_shapes += [pltpu.SemaphoreType.DMA((n_slots,))] * 2
    operands = [pltpu.with_memory_space_constraint(a, pltpu.HBM) for a in bufs]
    in_specs = [HBM_SPEC] * nb
    for sems, _ in waits:
        operands += list(sems)
        in_specs += [SEM_SPEC, SEM_SPEC]
    operands += after
    in_specs += [pl.BlockSpec(memory_space=pl.ANY)] * len(after)
    outs = pl.pallas_call(
        body, name=name, in_specs=in_specs,
        out_specs=[SEM_SPEC] * n_out_sem + [HBM_SPEC] * nb + [pl.BlockSpec(memory_space=pltpu.VMEM)],
        out_shape=sem_shapes + [pltpu.HBM(a.shape, a.dtype) for a in bufs] + [jax.ShapeDtypeStruct((8, SLAB), F32)],
        input_output_aliases={i: n_out_sem + i for i in range(nb)},
        compiler_params=pltpu.CompilerParams(has_side_effects=DATAFLOW),
    )(*operands)
    sems = [(outs[2 * k], outs[2 * k + 1]) for k in range(len(slots))]
    return sems, list(outs[n_out_sem:n_out_sem + nb]), outs[-1]


def _pair_sum(stacks, lands, place):
    n = len(stacks)

    def body(place_ref, *refs):
        k = pl.program_id(0)
        for m in range(n):
            mine, theirs, out, land = refs[m], refs[n + m], refs[2 * n + m], refs[3 * n + m]
            total = (mine[0, 0].astype(F32) + theirs[0].astype(F32)).astype(out.dtype)
            out[0] = total

            @pl.when(k == place_ref[1])
            def _():
                land[0] = total

    in_specs = [pl.BlockSpec((1, 1) + a.shape[1:], lambda k, place_ref: (k, place_ref[0], 0, 0)) for a in stacks]
    in_specs += [pl.BlockSpec((1,) + a.shape[1:], lambda k, place_ref: (k, 0, 0)) for a in lands]
    out_specs = [pl.BlockSpec((1,) + a.shape[1:], lambda k, place_ref: (k, 0, 0)) for a in lands]
    out_specs += [pl.BlockSpec((1,) + a.shape[1:], lambda k, place_ref: (place_ref[1], 0, 0)) for a in lands]
    outs = pl.pallas_call(
        body, name="pair_sum_" + "_".join(str(a.shape[1]) for a in stacks),
        grid_spec=pltpu.PrefetchScalarGridSpec(num_scalar_prefetch=1, grid=(N_DEV // 2,), in_specs=in_specs,
                                               out_specs=out_specs),
        out_shape=[pltpu.HBM(a.shape, a.dtype) for a in lands] * 2,
        compiler_params=_params(40, 1),
    )(place, *_in_hbm(*[a.reshape((N_DEV // 2, 2) + a.shape[1:]) for a in stacks], *lands))
    return list(outs[:n]), list(outs[n:])


def _reduce_adam(parts, w, m, v, name):
    rows, cols = w.shape
    n_parts = parts.shape[0]
    rb = rows
    for cand in (256, 176, 128):
        if rows % cand == 0 and rows > cand:
            rb = cand
            break

    def body(p_ref, w_ref, m_ref, v_ref, g_out, d_out, m_out, v_out):
        g = p_ref[0].astype(F32)
        for j in range(1, n_parts):
            g = g + p_ref[j].astype(F32)
        g_out[...] = g
        d_out[...], m_out[...], v_out[...] = _adam(g, w_ref[...], m_ref[...], v_ref[...])

    blk = pl.BlockSpec((rb, cols), lambda i: (i, 0))
    out = jax.ShapeDtypeStruct((rows, cols), F32)
    return pl.pallas_call(
        body, name=name, grid=(rows // rb,),
        in_specs=[pl.BlockSpec((n_parts, rb, cols), lambda i: (0, i, 0)), blk, blk, blk],
        out_specs=[blk] * 4, out_shape=[out] * 4,
        compiler_params=_params(40, 1),
    )(*_in_hbm(parts, w, m, v))


def _cols_from_stack(stack):
    n, r, c = stack.shape
    return jnp.transpose(stack, (1, 0, 2)).reshape(r, n * c)


def _block_diag(w):
    z = jnp.zeros((N_SLAB, 64, 64), w.dtype)
    pairs = w.reshape(N_SLAB, 2, 64, 64)
    top = jnp.concatenate([pairs[:, 0], z], axis=2)
    bottom = jnp.concatenate([z, pairs[:, 1]], axis=2)
    return jnp.concatenate([top, bottom], axis=1)


def _adam(g, w, m, v):
    m_new = ADAM_B1 * m + (1.0 - ADAM_B1) * g
    v_new = ADAM_B2 * v + (1.0 - ADAM_B2) * (g * g)
    m_hat = m_new / (1.0 - ADAM_B1 ** ADAM_STEP)
    v_hat = v_new / (1.0 - ADAM_B2 ** ADAM_STEP)
    return (-ADAM_LR) * (m_hat / (jnp.sqrt(v_hat) + ADAM_EPS) + ADAM_WD * w), m_new, v_new


WIDE = ("ln2_g", "lnf_g")
HALF = ("pool_scale", "conv_b", "b_a", "b_i", "lam", "gn_pool_g", "gn_lru_g")
VECTORS = [(k, D_MODEL) for k in WIDE] + [(k, 512) for k in HALF]
VECTOR_ROWS = sum(width // SLAB for _, width in VECTORS)
LOSS_ROW = -(-VECTOR_ROWS // 8) * 8
CONV_AT = LOSS_ROW + 8
CONV_LANES = LRU_WIDTH // SLAB
PACK_F_ROWS = CONV_AT + CONV_WIDTH * CONV_LANES
MATRIX_ROWS = N_SLAB * SLAB
HEAD = SLAB // 2
GATE_ROWS = N_SLAB * HEAD
PACK_B_ROWS = MATRIX_ROWS + 2 * GATE_ROWS


def _pack_small(vectors, pool_g, wa_g, wi_g, conv, sq):
    n_vec = len(vectors)

    def body(*refs):
        vec = refs[:n_vec]
        pw_ref, wa_ref, wi_ref, cw_ref, sq_ref, out, out_b = refs[n_vec:]
        out[...] = jnp.zeros_like(out)
        row = 0
        for ref, (_, width) in zip(vec, VECTORS):
            for k in range(width // SLAB):
                out[row:row + 1, :] = ref[:, k * SLAB:(k + 1) * SLAB]
                row += 1
        for tap in range(CONV_WIDTH):
            for k in range(CONV_LANES):
                at = CONV_AT + tap * CONV_LANES + k
                out[at:at + 1, :] = cw_ref[tap:tap + 1, k * SLAB:(k + 1) * SLAB]
        total = sq_ref[:, 0:SLAB]
        for k in range(1, D_MODEL // SLAB):
            total = total + sq_ref[:, k * SLAB:(k + 1) * SLAB]
        out[LOSS_ROW:LOSS_ROW + 1, :] = total
        left = lax.broadcasted_iota(jnp.int32, (HEAD, SLAB), 1) < HEAD
        for s in range(N_SLAB):
            out_b[s * SLAB:(s + 1) * SLAB, :] = pw_ref[s].astype(BF16)
            for i, ref in enumerate((wa_ref, wi_ref)):
                at = MATRIX_ROWS + i * GATE_ROWS + s * HEAD
                out_b[at:at + HEAD, :] = jnp.where(left, ref[s, 0:HEAD, :], ref[s, HEAD:SLAB, :]).astype(BF16)

    return pl.pallas_call(
        body, name="pack_small",
        out_shape=[jax.ShapeDtypeStruct((PACK_F_ROWS, SLAB), F32), jax.ShapeDtypeStruct((PACK_B_ROWS, SLAB), BF16)],
    )(*vectors, pool_g, wa_g, wi_g, conv, sq)


def _small_reduce_adam(my_index, parts, parts_b, vec_w, vec_m, vec_v, pool_wmv, wa_wmv, wi_wmv, conv_wmv):
    n_vec = len(VECTORS)
    n_parts = parts.shape[0]

    def body(*refs):
        me_ref, p_ref, pb_ref = refs[0], refs[1], refs[2]
        refs = refs[1:]
        w_refs, m_refs, v_refs = (refs[2 + k * n_vec:2 + (k + 1) * n_vec] for k in range(3))
        at = 2 + 3 * n_vec
        pw_w, pw_m, pw_v = refs[at:at + 3]
        gates = (refs[at + 3:at + 6], refs[at + 6:at + 9])
        cv_w, cv_m, cv_v = refs[at + 9:at + 12]
        outs = refs[at + 12:-1]
        total = refs[-1]
        total[...] = p_ref[0]
        for j in range(1, n_parts):
            total[...] += p_ref[j]
        row = 0
        for i, (_, width) in enumerate(VECTORS):
            n_rows = width // SLAB
            g = jnp.concatenate([total[row + k:row + k + 1, :] for k in range(n_rows)], axis=1)
            row += n_rows
            d, m_new, v_new = _adam(g, w_refs[i][...], m_refs[i][...], v_refs[i][...])
            for ref, val in zip(outs[4 * i:4 * i + 4], (g, d, m_new, v_new)):
                ref[...] = val
        tail = outs[4 * n_vec:]

        def summed(first, count):
            g = pb_ref[0, first:first + count, :].astype(F32)
            for j in range(1, n_parts):
                g = g + pb_ref[j, first:first + count, :].astype(F32)
            return g

        g = summed(0, MATRIX_ROWS)
        d, m_new, v_new = _adam(g, pw_w[...], pw_m[...], pw_v[...])
        for ref, val in zip(tail[0:4], (g, d, m_new, v_new)):
            ref[...] = val
        for i, (w, m, v) in enumerate(gates):
            g = summed(MATRIX_ROWS + i * GATE_ROWS, GATE_ROWS)
            for s in range(N_SLAB):
                for hd in range(2):
                    block = g[s * HEAD:(s + 1) * HEAD, hd * HEAD:(hd + 1) * HEAD]
                    rows = slice((2 * s + hd) * HEAD, (2 * s + hd + 1) * HEAD)
                    d, m_new, v_new = _adam(block, w[rows, :], m[rows, :], v[rows, :])
                    for ref, val in zip(tail[4 + 4 * i:8 + 4 * i], (block, d, m_new, v_new)):
                        ref[rows, :] = val
        me = me_ref[0, 0]
        taps = []
        for tap in range(CONV_WIDTH):
            row = total[pl.ds(CONV_AT + tap * CONV_LANES + me // 2, 1), :]
            taps.append(jnp.where(me % 2 == 0, row[:, :HEAD], row[:, HEAD:]))
        g = jnp.concatenate(taps, axis=0)
        d, m_new, v_new = _adam(g, cv_w[...], cv_m[...], cv_v[...])
        for ref, val in zip(tail[12:16], (g, d, m_new, v_new)):
            ref[...] = val
        tail[16][...] = (0.5 / D_MODEL) * jnp.sum(total[LOSS_ROW:LOSS_ROW + 1, :], axis=1, keepdims=True)

    out_shape = []
    for _, width in VECTORS:
        out_shape += [jax.ShapeDtypeStruct((1, width), F32)] * 4
    out_shape += [jax.ShapeDtypeStruct((MATRIX_ROWS, SLAB), F32)] * 4
    out_shape += [jax.ShapeDtypeStruct((2 * GATE_ROWS, HEAD), F32)] * 8
    out_shape += [jax.ShapeDtypeStruct(conv_wmv[0].shape, F32)] * 4 + [jax.ShapeDtypeStruct((1, 1), F32)]
    operands = [parts, parts_b, *vec_w, *vec_m, *vec_v, *pool_wmv, *wa_wmv, *wi_wmv, *conv_wmv]

    def whole(a):
        return pl.BlockSpec(a.shape, lambda i, n=len(a.shape): (0,) * n)

    outs = pl.pallas_call(
        body, name="adam_small", grid=(1,),
        in_specs=[pl.BlockSpec(memory_space=pltpu.SMEM)] + [whole(a) for a in operands],
        out_specs=[whole(a) for a in out_shape], out_shape=out_shape,
        scratch_shapes=[pltpu.VMEM((PACK_F_ROWS, SLAB), F32)],
        compiler_params=_params(40, 1),
    )(my_index, *_in_hbm(*operands))
    vec_out = [tuple(outs[4 * i:4 * i + 4]) for i in range(n_vec)]
    tail = outs[4 * n_vec:]
    return vec_out, tuple(tail[0:4]), tuple(tail[4:8]), tuple(tail[8:12]), tuple(tail[12:16]), tail[16]


def kernel(x, ln1_g, w_in, pool_w, pool_scale, conv_w, conv_b, w_a, b_a, w_i, b_i, lam, gn_pool_g, gn_lru_g, w_out, ln2_g, w_ffn_gate, w_ffn_up, w_ffn_down, lnf_g, loss_target, m_ln1_g, m_w_in, m_pool_w, m_pool_scale, m_conv_w, m_conv_b, m_w_a, m_b_a, m_w_i, m_b_i, m_lam, m_gn_pool_g, m_gn_lru_g, m_w_out, m_ln2_g, m_w_ffn_gate, m_w_ffn_up, m_w_ffn_down, m_lnf_g, v_ln1_g, v_w_in, v_pool_w, v_pool_scale, v_conv_w, v_conv_b, v_w_a, v_b_a, v_w_i, v_b_i, v_lam, v_gn_pool_g, v_gn_lru_g, v_w_out, v_ln2_g, v_w_ffn_gate, v_w_ffn_up, v_w_ffn_down, v_lnf_g):
    weights = dict(ln1_g=ln1_g, w_in=w_in, pool_w=pool_w, pool_scale=pool_scale, conv_w=conv_w, conv_b=conv_b,
                   w_a=w_a, b_a=b_a, w_i=w_i, b_i=b_i, lam=lam, gn_pool_g=gn_pool_g, gn_lru_g=gn_lru_g,
                   w_out=w_out, ln2_g=ln2_g, w_ffn_gate=w_ffn_gate, w_ffn_up=w_ffn_up, w_ffn_down=w_ffn_down,
                   lnf_g=lnf_g)
    mom1 = dict(ln1_g=m_ln1_g, w_in=m_w_in, pool_w=m_pool_w, pool_scale=m_pool_scale, conv_w=m_conv_w,
                conv_b=m_conv_b, w_a=m_w_a, b_a=m_b_a, w_i=m_w_i, b_i=m_b_i, lam=m_lam, gn_pool_g=m_gn_pool_g,
                gn_lru_g=m_gn_lru_g, w_out=m_w_out, ln2_g=m_ln2_g, w_ffn_gate=m_w_ffn_gate,
                w_ffn_up=m_w_ffn_up, w_ffn_down=m_w_ffn_down, lnf_g=m_lnf_g)
    mom2 = dict(ln1_g=v_ln1_g, w_in=v_w_in, pool_w=v_pool_w, pool_scale=v_pool_scale, conv_w=v_conv_w,
                conv_b=v_conv_b, w_a=v_w_a, b_a=v_b_a, w_i=v_w_i, b_i=v_b_i, lam=v_lam, gn_pool_g=v_gn_pool_g,
                gn_lru_g=v_gn_lru_g, w_out=v_w_out, ln2_g=v_ln2_g, w_ffn_gate=v_w_ffn_gate,
                w_ffn_up=v_w_ffn_up, w_ffn_down=v_w_ffn_down, lnf_g=v_lnf_g)

    xs = x[0]
    target = loss_target[0]

    shard = dict(w_in=lambda a: a[0].T, w_ffn_gate=lambda a: a[0].T, w_ffn_up=lambda a: a[0].T,
                 w_out=lambda a: a[0], w_ffn_down=lambda a: a[0], conv_w=lambda a: a[0])
    unshard = dict(w_in=lambda a: a.T[None], w_ffn_gate=lambda a: a.T[None], w_ffn_up=lambda a: a.T[None],
                   w_out=lambda a: a[None], w_ffn_down=lambda a: a[None], conv_w=lambda a: a[None])

    gathered = ("w_in", "conv_w", "w_out", "w_ffn_gate", "w_ffn_up", "w_ffn_down")
    groups = ((0, 1), (2,), (3, 4), (5,))
    sources = [shard[k](weights[k]) if k == "conv_w" else shard[k](weights[k]).astype(BF16) for k in gathered]
    my_index = 4 * lax.axis_index("x") + 2 * lax.axis_index("y") + lax.axis_index("c")
    lands = [lax.empty((N_DEV,) + a.shape, a.dtype) for a in sources]

    def first_hop(n, at=0):
        return _numbered(_fan_out(range(at, at + n), range(at + n, at + 2 * n), SAME_CORE + SIBLING + MYSELF, 0))

    def second_hop(n, at=0):
        return _numbered(_relay(range(at, at + n), SAME_CORE))

    start = []
    for g, members in enumerate(groups):
        start += _numbered(_fan_out(members, [6 + m for m in members], SAME_CORE + SIBLING + MYSELF, 0), g)
    sems, bufs, _ = _comm_call("gather_start", sources + lands,
                               start=(start, [sum(cp.group == g for cp in start) for g in range(len(groups))]))
    sources, lands = bufs[:6], bufs[6:]

    def group_buffers(g):
        return [sources[m] for m in groups[g]] + [lands[m] for m in groups[g]]

    def relayed(tag, g, after):
        n = len(groups[g])
        relay = second_hop(n, n)
        relay_sems, bufs, _ = _comm_call("gather_relay_" + tag, group_buffers(g), wait=(sems[g], first_hop(n)),
                                         start=(relay, (len(relay),)), after=after)
        return relay_sems[0], bufs[n:]

    relay_sem, bufs = relayed("in", 0, None)
    g_in, g_conv = _comm_call("gather_wait_in", bufs, wait=(relay_sem, second_hop(2)))[1]
    w_in_f = g_in.reshape(IN_WIDTH, D_MODEL)
    conv_w_f = _cols_from_stack(g_conv)

    wa_bd = _block_diag(w_a[0])
    wi_bd = _block_diag(w_i[0])
    lnf_row = lnf_g.reshape(1, D_MODEL)

    u_pool, u_lru, u_gate = _fwd_in(xs, ln1_g, w_in_f)
    y_pool, h, y_lru, decay, root = _mixer_fwd(u_pool, u_lru, u_gate, pool_w[0], pool_scale, conv_w_f, conv_b,
                                               wa_bd, b_a, wi_bd, b_i, lam)
    relay_sem, bufs = relayed("out", 1, y_pool)
    gate_up = second_hop(2, 3)
    (gate_up_sem,), bufs, _ = _comm_call("gather_wait_out", bufs + group_buffers(2),
                                         wait=[(relay_sem, second_hop(1)), (sems[2], first_hop(2, 1))],
                                         start=(gate_up, (len(gate_up),)))
    g_out, gate_up_lands = bufs[0], bufs[3:]
    w_out_f = g_out.reshape(D_MODEL, D_MODEL)
    h1, n2 = _fwd_out(xs, y_pool, y_lru, gn_pool_g, gn_lru_g, w_out_f, ln2_g)
    relay_sem, bufs = relayed("down", 3, n2)
    g_gate, g_up, g_down = _comm_call("gather_wait_ffn", gate_up_lands + bufs,
                                      wait=[(gate_up_sem, second_hop(2)), (relay_sem, second_hop(1, 2))])[1]
    w_gate_f = g_gate.reshape(D_FF, D_MODEL)
    w_up_f = g_up.reshape(D_FF, D_MODEL)
    w_down_f = g_down.reshape(D_FF, D_MODEL)
    g_act, u_act, dh2, dh2b, d_lnf, sq = _ffn_fwd(h1, n2, target, lnf_row, w_gate_f, w_up_f, w_down_f)

    def direct(tag, stacks, wholes, after):
        sources = list(stacks) + list(wholes)
        n, n_st = len(sources), len(stacks)
        lands = [lax.empty(a.shape if i < n_st else (N_DEV,) + a.shape, a.dtype) for i, a in enumerate(sources)]
        copies = _numbered(_scatter(range(n_st), range(n, n + n_st), EVERYONE + MYSELF)
                           + _fan_out(range(n_st, n), range(n + n_st, 2 * n), EVERYONE + MYSELF, 0))
        sem, bufs, token = _comm_call(tag + "_start", sources + lands, start=(copies, (len(copies),)), after=after)
        return (tag, sem[0], bufs, copies), token

    def direct_finish(started, after):
        tag, sem, bufs, copies = started
        _, bufs, _ = _comm_call(tag + "_wait", bufs, wait=(sem, copies), after=after)
        return bufs[len(bufs) // 2:]

    place = jnp.stack([lax.axis_index("c"), 2 * lax.axis_index("x") + lax.axis_index("y")]).astype(jnp.int32)

    d_gate, d_up, d_down, dn2 = _ffn_bwd(n2, dh2b, g_act, u_act, w_gate_f, w_up_f, w_down_f)
    down_stack, *ffn_stacks = [d.reshape(N_DEV, D_FF // N_DEV, D_MODEL) for d in (d_down, d_gate, d_up)]
    pair_lands = [lax.empty((N_DEV // 2,) + a.shape[1:], a.dtype) for a in ffn_stacks]
    pair_copies = _numbered(_to_sibling(range(2), range(2, 4)))
    down_copies = _numbered(_scatter([4], [5], EVERYONE + MYSELF), 1)
    (sem, down_sem), bufs, token = _comm_call(
        "ffn_grads_start", ffn_stacks + pair_lands + [down_stack, lax.empty(down_stack.shape, down_stack.dtype)],
        start=(pair_copies + down_copies, (len(pair_copies), len(down_copies))))
    bufs, down_bufs = bufs[:4], bufs[4:]
    dh1, dy_pool, dy_lru, d_out, d_ln2, d_gnp, d_gnl = _bwd_out(dn2, dh2, h1, y_pool, y_lru, gn_pool_g, gn_lru_g,
                                                                 w_out_f, ln2_g, token)
    _, bufs, _ = _comm_call("ffn_pair_wait", bufs, wait=(sem, pair_copies), after=dh1)
    ffn_sums, ffn_lands = _pair_sum(bufs[:2], bufs[2:], place)
    ffn_copies = _numbered(_to_chips(range(2), range(2, 4)))
    ffn_sem, ffn_bufs, token = _comm_call("ffn_chip_start", ffn_sums + ffn_lands, start=(ffn_copies, (len(ffn_copies),)))
    (du_pool, du_lru, du_gate, d_pw, d_ps, d_cw, d_cb, d_wa, d_ba, d_wi, d_bi, d_lam) = _mixer_bwd(
        u_pool, u_lru, u_gate, h, decay, root, dy_pool, dy_lru, pool_w[0], pool_scale, conv_w_f, conv_b,
        wa_bd, b_a, wi_bd, b_i, lam, token)

    vec_grads = dict(ln2_g=d_ln2, lnf_g=d_lnf, pool_scale=d_ps, conv_b=d_cb, b_a=d_ba, b_i=d_bi,
                     lam=d_lam, gn_pool_g=d_gnp, gn_lru_g=d_gnl)
    packed, packed_b = _pack_small([vec_grads[k] for k, _ in VECTORS], d_pw, d_wa, d_wi, d_cw, sq)
    small_started, token = direct("small", [d_out.reshape(N_DEV, D_MODEL // N_DEV, D_MODEL)], [packed, packed_b], None)
    grad_x, d_in, d_ln1 = _bwd_in(xs, dh1, du_pool, du_lru, du_gate, ln1_g, w_in_f, token)
    in_started, token = direct("in", [d_in.reshape(N_DEV, IN_WIDTH // N_DEV, D_MODEL)], [d_ln1], None)

    results = {}

    def reduce_adam(name, parts):
        outs = _reduce_adam(parts, shard[name](weights[name]), shard[name](mom1[name]), shard[name](mom2[name]),
                            "adam_" + name)
        results[name] = tuple(unshard[name](o) for o in outs)
        return outs[0]

    shard["ln1_g"] = unshard["ln1_g"] = lambda a: a
    _, ffn_bufs, _ = _comm_call("ffn_chip_wait", ffn_bufs, wait=(ffn_sem[0], ffn_copies), after=token)
    done = [reduce_adam(name, parts) for name, parts in zip(("w_ffn_gate", "w_ffn_up"), ffn_bufs[2:])]
    _, (_, r_down), _ = _comm_call("down_wait", down_bufs, after=done,
                                   wait=(down_sem, _numbered(_scatter([0], [1], EVERYONE + MYSELF))))
    done.append(reduce_adam("w_ffn_down", r_down))
    r_out, r_small, r_small_b = direct_finish(small_started, done)
    done.append(reduce_adam("w_out", r_out))

    def as_row(a, width):
        return a.reshape(1, width)

    def as_matrix(a):
        return a.reshape(MATRIX_ROWS, SLAB)

    def as_heads(a):
        return a.reshape(2 * GATE_ROWS, HEAD)

    states = (weights, mom1, mom2)
    vec_out, pool_out, wa_out, wi_out, conv_out, loss_11 = _small_reduce_adam(
        my_index.astype(jnp.int32).reshape(1, 1), r_small, r_small_b,
        *[[as_row(t[k], w) for k, w in VECTORS] for t in states], [as_matrix(t["pool_w"]) for t in states],
        [as_heads(t["w_a"]) for t in states], [as_heads(t["w_i"]) for t in states], [t["conv_w"][0] for t in states])
    for (k, _), outs in zip(VECTORS, vec_out):
        results[k] = tuple(o.reshape(weights[k].shape) for o in outs)
    for k, outs in (("pool_w", pool_out), ("w_a", wa_out), ("w_i", wi_out), ("conv_w", conv_out)):
        results[k] = tuple(o.reshape(weights[k].shape) for o in outs)
    loss = loss_11[0, 0]
    r_in, r_ln1 = direct_finish(in_started, done + [loss_11])
    reduce_adam("w_in", r_in)
    reduce_adam("ln1_g", r_ln1)

    order = ["ln1_g", "w_in", "pool_w", "pool_scale", "conv_w", "conv_b", "w_a", "b_a", "w_i", "b_i", "lam",
             "gn_pool_g", "gn_lru_g", "w_out", "ln2_g", "w_ffn_gate", "w_ffn_up", "w_ffn_down", "lnf_g"]
    return (loss, grad_x[None],
            *[results[k][0] for k in order], *[results[k][1] for k in order],
            *[results[k][2] for k in order], *[results[k][3] for k in order])
```

```python
from typing import Any, NamedTuple

import jax
import jax.numpy as jnp
from jax import lax
from jax.experimental import pallas as pl
from jax.experimental.pallas import tpu as pltpu

F32 = jnp.float32
BF16 = jnp.bfloat16

N_DEV = 8
D_MODEL = 1024
POOL_WIDTH = 512
LRU_WIDTH = 512
IN_WIDTH = 1536
D_FF = 2816
N_SLAB = 4
SLAB = 128
CONV_WIDTH = 4
LRU_C = 8.0
EPS = 1e-6
HALO = 16
FF_CHUNK = 256

ADAM_LR = 0.001
ADAM_B1 = 0.9
ADAM_B2 = 0.999
ADAM_EPS = 1e-08
ADAM_WD = 0.01
ADAM_STEP = 10

MIB = 1 << 20
MESH = pl.DeviceIdType.MESH


def _params(vmem_mib, n_axes=0):
    sem = ("arbitrary",) * n_axes if n_axes else None
    return pltpu.CompilerParams(dimension_semantics=sem, vmem_limit_bytes=vmem_mib * MIB)


def _in_hbm(*arrays):
    return [pltpu.with_memory_space_constraint(a, pltpu.HBM) for a in arrays]


def _mm(a, b):
    return jnp.dot(a, b, preferred_element_type=F32)


def _mm_nt(a, b):
    return lax.dot_general(a, b, (((1,), (1,)), ((), ())), preferred_element_type=F32)


def _mm_tn(a, b):
    return lax.dot_general(a, b, (((0,), (0,)), ((), ())), preferred_element_type=F32)


def _rms(x, g):
    rstd = lax.rsqrt(jnp.mean(x * x, axis=-1, keepdims=True) + EPS)
    xhat = x * rstd
    return xhat * g, xhat, rstd


def _rms_bwd(dy, xhat, rstd, g):
    gy = dy * g
    dx = rstd * (gy - xhat * jnp.mean(gy * xhat, axis=-1, keepdims=True))
    return dx, jnp.sum(dy * xhat, axis=0, keepdims=True)


def _gelu(z):
    t = jnp.tanh(0.7978845608028654 * (z + 0.044715 * z * z * z))
    return 0.5 * z * (1.0 + t), t


def _gelu_grad(z, t):
    return 0.5 * (1.0 + t) + 0.5 * z * (1.0 - t * t) * 0.7978845608028654 * (1.0 + 3.0 * 0.044715 * z * z)


def _softplus_neg(lam):
    x = -lam
    e = jnp.exp(-jnp.abs(x))
    u = 1.0 + e
    l1p = jnp.where(u == 1.0, e, jnp.log(u) * e / (u - 1.0))
    return jnp.maximum(x, 0.0) + l1p


def _one_minus_square(la, a):
    x = 2.0 * la
    series = -x * (1.0 + x * (0.5 + x * (1.0 / 6.0 + x * (1.0 / 24.0))))
    return jnp.where(x > -0.06, series, 1.0 - a * a)


def _sigmoid(x):
    return 0.5 * jnp.tanh(0.5 * x) + 0.5


def _down(v, d):
    return pltpu.roll(v, d, 0)


def _up(v, d):
    return pltpu.roll(v, v.shape[0] - d, 0)


def _slab_major(s, dtype=F32):
    return pltpu.HBM((N_SLAB, s, SLAB), dtype)


def _slabs_spec(tb):
    return pl.BlockSpec((N_SLAB, tb, SLAB), lambda i: (0, i, 0))


def _read_slabs(ref):
    return jnp.concatenate([ref[k] for k in range(N_SLAB)], axis=1)


def _write_slabs(ref, value):
    for k in range(N_SLAB):
        ref[k] = value[:, k * SLAB:(k + 1) * SLAB]


def _token_block(s, most=512):
    for rows in (most, 512, 256):
        if rows <= most and s % rows == 0 and s > rows:
            return rows
    return s


def _time_chunk(s):
    return 512 if s % 512 == 0 and s > 512 else 256 if s % 256 == 0 else s


def _fwd_in(x, ln1_g, w_in_t):
    s = x.shape[0]
    tb = _token_block(s, 1024)

    def body(x_ref, g_ref, w_ref, up_ref, ul_ref, ug_ref):
        n, _, _ = _rms(x_ref[...], g_ref[...])
        proj = _mm_nt(n.astype(BF16), w_ref[...])
        _write_slabs(up_ref, proj[:, :POOL_WIDTH])
        _write_slabs(ul_ref, proj[:, POOL_WIDTH:POOL_WIDTH + LRU_WIDTH])
        _write_slabs(ug_ref, proj[:, POOL_WIDTH + LRU_WIDTH:])

    out = _slab_major(s)
    return pl.pallas_call(
        body, name="fwd_in", grid=(s // tb,),
        in_specs=[pl.BlockSpec((tb, D_MODEL), lambda i: (i, 0)),
                  pl.BlockSpec((1, D_MODEL), lambda i: (0, 0)),
                  pl.BlockSpec((IN_WIDTH, D_MODEL), lambda i: (0, 0))],
        out_specs=[_slabs_spec(tb)] * 3,
        out_shape=[out, out, out],
        compiler_params=_params(40, 1),
    )(*_in_hbm(x, ln1_g, w_in_t))


def _pool_denominator(t0, row, window):
    return jnp.minimum((t0 + row + 1).astype(F32), window)


def _causal_window(ext, deeper):
    s = ext + _down(ext, 1)
    s = s + deeper[0] * _down(s, 2)
    s = s + deeper[1] * _down(s, 4)
    s = s + deeper[2] * _down(s, 8)
    return s[HALO:]


def _anticausal_window(ext, deeper, rows):
    s = ext + _up(ext, 1)
    s = s + deeper[0] * _up(s, 2)
    s = s + deeper[1] * _up(s, 4)
    s = s + deeper[2] * _up(s, 8)
    return s[:rows]


def _conv(taps, cw, cb):
    return cw[3] * taps[0] + cw[2] * taps[1] + cw[1] * taps[2] + cw[0] * taps[3] + cb


def _lru_gates(xc, wa, ba, wi, bi):
    xb = xc.astype(BF16)
    r = _sigmoid(_mm(xb, wa) + ba)
    i = _sigmoid(_mm(xb, wi) + bi)
    return xb, r, i


def _lru_decay(r, sp):
    la = (-LRU_C) * r * sp
    a = jnp.exp(la)
    return a, jnp.sqrt(jnp.maximum(_one_minus_square(la, a), 0.0))


SUBLANES = 8


def _scan_pads(a_pad, b_pad, rows, causal):
    spare = slice(0, SUBLANES) if causal else slice(rows, rows + SUBLANES)
    a_pad[spare, :] = jnp.ones((SUBLANES, SLAB), F32)
    b_pad[spare, :] = jnp.zeros((SUBLANES, SLAB), F32)


def _scan_causal(a, b, h_prev, a_pad, b_pad, rows):
    d = 1
    while d < min(SUBLANES, rows):
        a_pad[SUBLANES:, :] = a
        b_pad[SUBLANES:, :] = b
        b = a * b_pad[SUBLANES - d:SUBLANES - d + rows, :] + b
        a = a * a_pad[SUBLANES - d:SUBLANES - d + rows, :]
        d *= 2
    h = jnp.broadcast_to(h_prev, (d, SLAB))
    out = []
    for at in range(0, rows, d):
        h = a[at:at + d] * h + b[at:at + d]
        out.append(h)
    return jnp.concatenate(out, axis=0)


def _scan_anticausal(a, b, l_next, a_pad, b_pad, rows):
    d = 1
    while d < min(SUBLANES, rows):
        a_pad[:rows, :] = a
        b_pad[:rows, :] = b
        b = a * b_pad[d:d + rows, :] + b
        a = a * a_pad[d:d + rows, :]
        d *= 2
    lead = jnp.broadcast_to(l_next, (d, SLAB))
    out = []
    for at in range(rows - d, -1, -d):
        lead = a[at:at + d] * lead + b[at:at + d]
        out.append(lead)
    return jnp.concatenate(out[::-1], axis=0)


def _slab_scalars():
    slab = pl.program_id(0)
    deeper = [jnp.where(slab > k, 1.0, 0.0).astype(F32) for k in range(N_SLAB - 1)]
    window = jnp.left_shift(jnp.int32(2), slab).astype(F32)
    inverse = jnp.where(slab == 0, 0.5, jnp.where(slab == 1, 0.25, jnp.where(slab == 2, 0.125, 0.0625))).astype(F32)
    return deeper, (window, inverse)


def _window_mean(total, t0, row, window, at_start):
    if at_start:
        return total / _pool_denominator(t0, row, window[0])
    return total * window[1]


def _slab_specs(s):
    seq = pl.BlockSpec((None, s, SLAB), lambda k: (k, 0, 0))
    mat = pl.BlockSpec((1, SLAB, SLAB), lambda k: (k, 0, 0))
    vec = pl.BlockSpec((1, SLAB), lambda k: (0, k))
    taps = pl.BlockSpec((CONV_WIDTH, SLAB), lambda k: (0, k))
    return seq, mat, vec, taps


def _mixer_fwd(u_pool, u_lru, u_gate, pool_w, pool_scale, conv_w, conv_b, wa_bd, b_a, wi_bd, b_i, lam):
    s = u_pool.shape[1]
    tc = _time_chunk(s)
    n_chunks = s // tc

    def body(up_ref, ul_ref, ug_ref, pw_ref, ps_ref, cw_ref, cb_ref, wa_ref, ba_ref, wi_ref, bi_ref, lam_ref,
             yp_ref, h_ref, yl_ref, a_ref, mult_ref, shift_ref, a_pad, b_pad):
        _scan_pads(a_pad, b_pad, tc, causal=True)
        deeper, window = _slab_scalars()
        pw = pw_ref[0].astype(BF16)
        wa = wa_ref[0].astype(BF16)
        wi = wi_ref[0].astype(BF16)
        ps, cb, ba, bi = ps_ref[...], cb_ref[...], ba_ref[...], bi_ref[...]
        cw = [cw_ref[k:k + 1, :] for k in range(CONV_WIDTH)]
        sp = _softplus_neg(lam_ref[...])
        row = lax.broadcasted_iota(jnp.int32, (tc, SLAB), 0)

        def chunk(t0, ext_p, ext_l, h_prev, at_start=False):
            rows = pl.ds(t0, tc)
            d = _window_mean(_causal_window(ext_p, deeper), t0, row, window, at_start) - ext_p[HALO:]
            yp_ref[rows, :] = _mm(d.astype(BF16), pw) * ps
            shift_ref[...] = ext_l
            xc = _conv([shift_ref[HALO - k:HALO - k + tc, :] for k in range(CONV_WIDTH)], cw, cb)
            _, r, i = _lru_gates(xc, wa, ba, wi, bi)
            a, mult = _lru_decay(r, sp)
            a_ref[rows, :] = a
            mult_ref[rows, :] = mult
            h = _scan_causal(a, mult * (i * xc), h_prev, a_pad, b_pad, tc)
            h_ref[rows, :] = h
            yl_ref[rows, :] = h * _gelu(ug_ref[rows, :])[0]
            return h[tc - 1:tc, :]

        pad = jnp.zeros((HALO, SLAB), F32)
        h0 = chunk(0, jnp.concatenate([pad, up_ref[pl.ds(0, tc), :]], axis=0),
                   jnp.concatenate([pad, ul_ref[pl.ds(0, tc), :]], axis=0), jnp.zeros((1, SLAB), F32), at_start=True)

        def step(c, h_prev):
            t0 = pl.multiple_of(c * tc, tc)
            ext = pl.ds(pl.multiple_of(c * tc - HALO, HALO), tc + HALO)
            return chunk(t0, up_ref[ext, :], ul_ref[ext, :], h_prev)

        lax.fori_loop(1, n_chunks, step, h0)

    seq, mat, vec, taps = _slab_specs(s)
    out = _slab_major(s)
    return pl.pallas_call(
        body, name="mixer_fwd", grid=(N_SLAB,),
        in_specs=[seq, seq, seq, mat, vec, taps, vec, mat, vec, mat, vec, vec],
        out_specs=[seq] * 5, out_shape=[out] * 5,
        scratch_shapes=[pltpu.VMEM((tc + HALO, SLAB), F32), pltpu.VMEM((tc + SUBLANES, SLAB), F32),
                        pltpu.VMEM((tc + SUBLANES, SLAB), F32)],
        compiler_params=_params(48, 1),
    )(*_in_hbm(u_pool, u_lru, u_gate, pool_w, pool_scale, conv_w, conv_b, wa_bd, b_a, wi_bd, b_i, lam))


def _fwd_out(x, y_pool, y_lru, gn_pool_g, gn_lru_g, w_out, ln2_g):
    s = x.shape[0]
    tb = _token_block(s, 1024)

    def body(x_ref, yp_ref, yl_ref, gp_ref, gl_ref, w_ref, g2_ref, h1_ref, n2_ref):
        mp, _, _ = _rms(_read_slabs(yp_ref), gp_ref[...])
        ml, _, _ = _rms(_read_slabs(yl_ref), gl_ref[...])
        h1 = x_ref[...] + _mm(mp.astype(BF16), w_ref[:POOL_WIDTH, :]) + _mm(ml.astype(BF16), w_ref[POOL_WIDTH:, :])
        h1_ref[...] = h1
        n2_ref[...] = _rms(h1, g2_ref[...])[0].astype(BF16)

    row = pl.BlockSpec((tb, D_MODEL), lambda i: (i, 0))
    half = _slabs_spec(tb)
    return pl.pallas_call(
        body, name="fwd_out", grid=(s // tb,),
        in_specs=[row, half, half, pl.BlockSpec((1, 512), lambda i: (0, 0)), pl.BlockSpec((1, 512), lambda i: (0, 0)),
                  pl.BlockSpec((D_MODEL, D_MODEL), lambda i: (0, 0)), pl.BlockSpec((1, D_MODEL), lambda i: (0, 0))],
        out_specs=[row, row],
        out_shape=[pltpu.HBM((s, D_MODEL), F32), pltpu.HBM((s, D_MODEL), BF16)],
        compiler_params=_params(40, 1),
    )(*_in_hbm(x, y_pool, y_lru, gn_pool_g, gn_lru_g, w_out, ln2_g))


def _ffn_fwd(h1, n2, target, lnf_g, w_gate, w_up, w_down):
    s = h1.shape[0]
    tb = 512
    sub = 256
    n_ff = D_FF // FF_CHUNK

    def body(h1_ref, n2_ref, t_ref, gf_ref, wg_hbm, wu_hbm, wd_hbm,
             g_ref, u_ref, dh_ref, dhb_ref, dgf_ref, sq_ref, wg, wu, wd, sem):
        @pl.when(pl.program_id(0) == 0)
        def _():
            loads = [pltpu.make_async_copy(src, dst, sem.at[k])
                     for k, (src, dst) in enumerate(((wg_hbm, wg), (wu_hbm, wu), (wd_hbm, wd)))]
            for cp in loads:
                cp.start()
            for cp in loads:
                cp.wait()
            dgf_ref[...] = jnp.zeros_like(dgf_ref)
            sq_ref[...] = jnp.zeros_like(sq_ref)

        n2v = n2_ref[...]
        acc = jnp.zeros((tb, D_MODEL), F32)
        for c in range(n_ff):
            cols = slice(c * FF_CHUNK, (c + 1) * FF_CHUNK)
            g = _mm_nt(n2v, wg[cols, :])
            u = _mm_nt(n2v, wu[cols, :])
            g_ref[:, cols] = g.astype(BF16)
            u_ref[:, cols] = u.astype(BF16)
            act = g * jax.nn.sigmoid(g) * u
            acc = acc + _mm(act.astype(BF16), wd[cols, :])
        gf = gf_ref[...]
        for r in range(tb // sub):
            rows = slice(r * sub, (r + 1) * sub)
            y, xhat, rstd = _rms(h1_ref[rows, :] + acc[rows, :], gf)
            err = y - t_ref[rows, :]
            sq_ref[...] += jnp.sum(err * err, axis=0, keepdims=True)
            dh2, dgf = _rms_bwd(err * (1.0 / D_MODEL), xhat, rstd, gf)
            dgf_ref[...] += dgf
            dh_ref[rows, :] = dh2
            dhb_ref[rows, :] = dh2.astype(BF16)

    row = pl.BlockSpec((tb, D_MODEL), lambda i: (i, 0))
    ff = pl.BlockSpec((tb, D_FF), lambda i: (i, 0))
    vec = pl.BlockSpec((1, D_MODEL), lambda i: (0, 0))
    anyspace = pl.BlockSpec(memory_space=pl.ANY)
    return pl.pallas_call(
        body, name="ffn_fwd", grid=(s // tb,),
        in_specs=[row, row, row, vec, anyspace, anyspace, anyspace],
        out_specs=[ff, ff, row, row, vec, vec],
        out_shape=[pltpu.HBM((s, D_FF), BF16), pltpu.HBM((s, D_FF), BF16),
                   pltpu.HBM((s, D_MODEL), F32), pltpu.HBM((s, D_MODEL), BF16),
                   jax.ShapeDtypeStruct((1, D_MODEL), F32), jax.ShapeDtypeStruct((1, D_MODEL), F32)],
        scratch_shapes=[pltpu.VMEM((D_FF, D_MODEL), BF16), pltpu.VMEM((D_FF, D_MODEL), BF16),
                        pltpu.VMEM((D_FF, D_MODEL), BF16), pltpu.SemaphoreType.DMA((3,))],
        compiler_params=_params(60, 1),
    )(*_in_hbm(h1, n2, target, lnf_g, w_gate, w_up, w_down))


def _ffn_bwd(n2, dh2b, g, u, w_gate_t, w_up_t, w_down):
    s = n2.shape[0]
    tb = min(1024, s)
    n_ff = D_FF // FF_CHUNK
    n_tb = s // tb

    def body(n2_ref, dh_ref, g_ref, u_ref, wg_ref, wu_ref, wd_ref, dwg_ref, dwu_ref, dwd_ref, dn2_ref,
             dn2_acc, acc_g, acc_u, acc_d):
        j = pl.program_id(0)
        t = pl.program_id(1)
        rows = pl.ds(pl.multiple_of(t * tb, tb), tb)

        @pl.when(t == 0)
        def _():
            acc_g[...] = jnp.zeros_like(acc_g)
            acc_u[...] = jnp.zeros_like(acc_u)
            acc_d[...] = jnp.zeros_like(acc_d)

        @pl.when(j == 0)
        def _():
            dn2_acc[rows, :] = jnp.zeros((tb, D_MODEL), F32)

        n2v = n2_ref[...]
        dh = dh_ref[...]
        gv = g_ref[...].astype(F32)
        uv = u_ref[...].astype(F32)
        sg = jax.nn.sigmoid(gv)
        silu = gv * sg
        dact = _mm_nt(dh, wd_ref[...])
        dub = (dact * silu).astype(BF16)
        dgb = (dact * uv * (sg * (1.0 + gv * (1.0 - sg)))).astype(BF16)
        acc_d[...] += _mm_tn((silu * uv).astype(BF16), dh)
        acc_g[...] += _mm_tn(dgb, n2v)
        acc_u[...] += _mm_tn(dub, n2v)
        dn2_acc[rows, :] += _mm(jnp.concatenate([dgb, dub], axis=1),
                                jnp.concatenate([wg_ref[...], wu_ref[...]], axis=0))

        @pl.when(t == n_tb - 1)
        def _():
            dwg_ref[...] = acc_g[...].astype(BF16)
            dwu_ref[...] = acc_u[...].astype(BF16)
            dwd_ref[...] = acc_d[...].astype(BF16)

        @pl.when(j == n_ff - 1)
        def _():
            dn2_ref[...] = dn2_acc[rows, :]

    row = pl.BlockSpec((tb, D_MODEL), lambda j, t: (t, 0))
    act = pl.BlockSpec((tb, FF_CHUNK), lambda j, t: (t, j))
    w_row = pl.BlockSpec((FF_CHUNK, D_MODEL), lambda j, t: (j, 0))
    last = pl.BlockSpec((tb, D_MODEL), lambda j, t: (jnp.where(j == n_ff - 1, t, 0), 0))
    grad = pltpu.HBM((D_FF, D_MODEL), BF16)
    chunk_acc = pltpu.VMEM((FF_CHUNK, D_MODEL), F32)
    return pl.pallas_call(
        body, name="ffn_bwd", grid=(n_ff, n_tb),
        in_specs=[row, row, act, act, w_row, w_row, w_row],
        out_specs=[w_row, w_row, w_row, last],
        out_shape=[grad, grad, grad, pltpu.HBM((s, D_MODEL), F32)],
        scratch_shapes=[pltpu.VMEM((s, D_MODEL), F32), chunk_acc, chunk_acc, chunk_acc],
        compiler_params=_params(56, 2),
    )(*_in_hbm(n2, dh2b, g, u, w_gate_t, w_up_t, w_down))


def _bwd_out(dn2, dh2, h1, y_pool, y_lru, gn_pool_g, gn_lru_g, w_out, ln2_g, after):
    s = h1.shape[0]
    tb = _token_block(s)

    n_steps = s // tb
    ring = 3

    def body(dn2_hbm, dh2_hbm, h1_hbm, yp_ref, yl_ref, gp_ref, gl_ref, w_ref, g2_ref, _after,
             dh1_ref, dyp_ref, dyl_ref, dwb_ref, dg2_ref, dgp_ref, dgl_ref, dw_ref,
             dn2_ring, dh2_ring, h1_ring, ring_sem):
        step = pl.program_id(0)

        def fetch(at):
            slot = at % ring
            rows = pl.ds(pl.multiple_of(at * tb, tb), tb)
            return [pltpu.make_async_copy(src.at[rows], dst.at[slot], ring_sem.at[3 * slot + k])
                    for k, (src, dst) in enumerate(((dn2_hbm, dn2_ring), (dh2_hbm, dh2_ring), (h1_hbm, h1_ring)))]

        @pl.when(step == 0)
        def _():
            for at in range(min(ring - 1, n_steps)):
                for cp in fetch(at):
                    cp.start()

        @pl.when(step + ring - 1 < n_steps)
        def _():
            for cp in fetch(step + ring - 1):
                cp.start()

        for cp in fetch(step):
            cp.wait()
        slot = step % ring
        dn2_ref, dh2_ref, h1_ref = dn2_ring.at[slot], dh2_ring.at[slot], h1_ring.at[slot]

        @pl.when(step == 0)
        def _():
            dw_ref[...] = jnp.zeros_like(dw_ref)
            dg2_ref[...] = jnp.zeros_like(dg2_ref)
            dgp_ref[...] = jnp.zeros_like(dgp_ref)
            dgl_ref[...] = jnp.zeros_like(dgl_ref)

        g2 = g2_ref[...]
        _, xhat2, rstd2 = _rms(h1_ref[...], g2)
        dres, dg2 = _rms_bwd(dn2_ref[...], xhat2, rstd2, g2)
        dg2_ref[...] += dg2
        dh1 = dh2_ref[...] + dres
        dh1_ref[...] = dh1
        dh1b = dh1.astype(BF16)
        gp, gl = gp_ref[...], gl_ref[...]
        mp, xhat_p, rstd_p = _rms(_read_slabs(yp_ref), gp)
        ml, xhat_l, rstd_l = _rms(_read_slabs(yl_ref), gl)
        dw_ref[:POOL_WIDTH, :] += _mm_tn(mp.astype(BF16), dh1b)
        dw_ref[POOL_WIDTH:, :] += _mm_tn(ml.astype(BF16), dh1b)
        dyp, dgp = _rms_bwd(_mm_nt(dh1b, w_ref[:POOL_WIDTH, :]), xhat_p, rstd_p, gp)
        dyl, dgl = _rms_bwd(_mm_nt(dh1b, w_ref[POOL_WIDTH:, :]), xhat_l, rstd_l, gl)
        _write_slabs(dyp_ref, dyp)
        _write_slabs(dyl_ref, dyl)
        dgp_ref[...] += dgp
        dgl_ref[...] += dgl

        @pl.when(pl.program_id(0) == s // tb - 1)
        def _():
            dwb_ref[...] = dw_ref[...].astype(BF16)

    row = pl.BlockSpec((tb, D_MODEL), lambda i: (i, 0))
    half = _slabs_spec(tb)
    vec = pl.BlockSpec((1, D_MODEL), lambda i: (0, 0))
    hvec = pl.BlockSpec((1, 512), lambda i: (0, 0))
    mat = pl.BlockSpec((D_MODEL, D_MODEL), lambda i: (0, 0))
    return pl.pallas_call(
        body, name="bwd_out", grid=(s // tb,),
        in_specs=[pl.BlockSpec(memory_space=pl.ANY)] * 3
        + [half, half, hvec, hvec, mat, vec, pl.BlockSpec(memory_space=pl.ANY)],
        out_specs=[row, half, half, mat, vec, hvec, hvec],
        out_shape=[pltpu.HBM((s, D_MODEL), F32), _slab_major(s),
                   _slab_major(s), pltpu.HBM((D_MODEL, D_MODEL), BF16),
                   jax.ShapeDtypeStruct((1, D_MODEL), F32), jax.ShapeDtypeStruct((1, 512), F32),
                   jax.ShapeDtypeStruct((1, 512), F32)],
        scratch_shapes=[pltpu.VMEM((D_MODEL, D_MODEL), F32)] + [pltpu.VMEM((ring, tb, D_MODEL), F32)] * 3
        + [pltpu.SemaphoreType.DMA((3 * ring,))],
        compiler_params=_params(56, 1),
    )(*_in_hbm(dn2, dh2, h1, y_pool, y_lru, gn_pool_g, gn_lru_g, w_out, ln2_g), after)


def _mixer_bwd(u_pool, u_lru, u_gate, h, decay, root, dy_pool, dy_lru,
               pool_w, pool_scale, conv_w, conv_b, wa_bd, b_a, wi_bd, b_i, lam, after):
    s = u_pool.shape[1]
    tc = _time_chunk(s)
    n_chunks = s // tc

    def body(up_ref, ul_ref, ug_ref, h_ref, a_ref, mult_ref, dyp_ref, dyl_ref,
             pw_ref, ps_ref, cw_ref, cb_ref, wa_ref, ba_ref, wi_ref, bi_ref, lam_ref, _after,
             dup_ref, dul_ref, dug_ref, dpw_ref, dps_ref, dcw_ref, dcb_ref, dwa_ref, dba_ref, dwi_ref, dbi_ref, dlam_ref,
             a_pad, b_pad, u_shift, h_shift, a_shift, d_shift):
        _scan_pads(a_pad, b_pad, tc, causal=False)
        deeper, window = _slab_scalars()
        pw = pw_ref[0].astype(BF16)
        wa = wa_ref[0].astype(BF16)
        wi = wi_ref[0].astype(BF16)
        ps, cb, ba, bi = ps_ref[...], cb_ref[...], ba_ref[...], bi_ref[...]
        cw = [cw_ref[k:k + 1, :] for k in range(CONV_WIDTH)]
        lam_v = lam_ref[...]
        sp = _softplus_neg(lam_v)
        row = lax.broadcasted_iota(jnp.int32, (tc, SLAB), 0)
        for ref in (dpw_ref, dps_ref, dcw_ref, dcb_ref, dwa_ref, dba_ref, dwi_ref, dbi_ref, dlam_ref):
            ref[...] = jnp.zeros_like(ref)

        def chunk(t0, ext_p, ext_l, ext_h, carry, at_start=False):
            l_next, a_next, dxc_next, ddn_next = carry
            rows = pl.ds(t0, tc)
            u_shift[...] = ext_l
            taps = [u_shift[HALO - k:HALO - k + tc, :] for k in range(CONV_WIDTH)]
            xc = _conv(taps, cw, cb)
            xb, r, i = _lru_gates(xc, wa, ba, wi, bi)
            a, mult = a_ref[rows, :], mult_ref[rows, :]
            hv = ext_h[HALO:]
            h_shift[...] = ext_h
            h_before = h_shift[HALO - 1:HALO - 1 + tc, :]
            ug = ug_ref[rows, :]
            dyl = dyl_ref[rows, :]
            gel, th = _gelu(ug)
            dug_ref[rows, :] = (dyl * hv * _gelu_grad(ug, th)).astype(BF16)
            a_shift[:tc, :] = a
            a_shift[tc:, :] = jnp.broadcast_to(a_next, (SUBLANES, SLAB))
            a_after = a_shift[1:1 + tc, :]
            l = _scan_anticausal(a_after, dyl * gel, l_next, a_pad, b_pad, tc)
            dmult = l * (i * xc)
            di = l * mult * xc
            dxc = l * mult * i
            dla = (l * h_before) * a - jnp.where(mult > 0.0, dmult * (a * a) / mult, 0.0)
            dlam_ref[...] += jnp.sum(dla * r, axis=0, keepdims=True)
            dpa = (dla * ((-LRU_C) * sp)) * (r * (1.0 - r))
            dpi = di * (i * (1.0 - i))
            dpab = dpa.astype(BF16)
            dpib = dpi.astype(BF16)
            dwa_ref[0] += _mm_tn(xb, dpab)
            dwi_ref[0] += _mm_tn(xb, dpib)
            dba_ref[...] += jnp.sum(dpa, axis=0, keepdims=True)
            dbi_ref[...] += jnp.sum(dpi, axis=0, keepdims=True)
            dxc = dxc + _mm_nt(dpab, wa) + _mm_nt(dpib, wi)
            d_shift[:tc, :] = dxc
            d_shift[tc:, :] = dxc_next
            dul_ref[rows, :] = (cw[3] * dxc + cw[2] * d_shift[1:1 + tc, :]
                                + cw[1] * d_shift[2:2 + tc, :] + cw[0] * d_shift[3:3 + tc, :]).astype(BF16)
            for k in range(CONV_WIDTH):
                dcw_ref[k:k + 1, :] += jnp.sum(dxc * taps[CONV_WIDTH - 1 - k], axis=0, keepdims=True)
            dcb_ref[...] += jnp.sum(dxc, axis=0, keepdims=True)
            db = (_window_mean(_causal_window(ext_p, deeper), t0, row, window, at_start) - ext_p[HALO:]).astype(BF16)
            dyp = dyp_ref[rows, :]
            dps_ref[...] += jnp.sum(dyp * _mm(db, pw), axis=0, keepdims=True)
            dys = (dyp * ps).astype(BF16)
            dpw_ref[0] += _mm_tn(db, dys)
            dd = _mm_nt(dys, pw)
            ddn = _window_mean(dd, t0, row, window, at_start)
            ext_q = jnp.concatenate([ddn, ddn_next], axis=0)
            dup_ref[rows, :] = (_anticausal_window(ext_q, deeper, tc) - dd).astype(BF16)
            return l[0:1, :], a[0:1, :], dxc[0:8, :], ddn[0:HALO, :]

        def step(k, carry):
            c = n_chunks - 1 - k
            t0 = pl.multiple_of(c * tc, tc)
            ext = pl.ds(pl.multiple_of(c * tc - HALO, HALO), tc + HALO)
            return chunk(t0, up_ref[ext, :], ul_ref[ext, :], h_ref[ext, :], carry)

        carry = (jnp.zeros((1, SLAB), F32), jnp.zeros((1, SLAB), F32),
                 jnp.zeros((8, SLAB), F32), jnp.zeros((HALO, SLAB), F32))
        carry = lax.fori_loop(0, n_chunks - 1, step, carry)
        pad = jnp.zeros((HALO, SLAB), F32)
        first = pl.ds(0, tc)
        chunk(0, jnp.concatenate([pad, up_ref[first, :]], axis=0), jnp.concatenate([pad, ul_ref[first, :]], axis=0),
              jnp.concatenate([pad, h_ref[first, :]], axis=0), carry, at_start=True)
        dlam_ref[...] = dlam_ref[...] * (LRU_C * jax.nn.sigmoid(-lam_v))

    seq, mat, vec, taps = _slab_specs(s)
    grad = _slab_major(s, BF16)
    mats = jax.ShapeDtypeStruct((N_SLAB, SLAB, SLAB), F32)
    vecs = jax.ShapeDtypeStruct((1, 512), F32)
    return pl.pallas_call(
        body, name="mixer_bwd", grid=(N_SLAB,),
        in_specs=[seq] * 8 + [mat, vec, taps, vec, mat, vec, mat, vec, vec, pl.BlockSpec(memory_space=pl.ANY)],
        out_specs=[seq, seq, seq, mat, vec, taps, vec, mat, vec, mat, vec, vec],
        out_shape=[grad, grad, grad, mats, vecs, jax.ShapeDtypeStruct((CONV_WIDTH, 512), F32), vecs,
                   mats, vecs, mats, vecs, vecs],
        scratch_shapes=[pltpu.VMEM((tc + SUBLANES, SLAB), F32), pltpu.VMEM((tc + SUBLANES, SLAB), F32),
                        pltpu.VMEM((tc + HALO, SLAB), F32), pltpu.VMEM((tc + HALO, SLAB), F32),
                        pltpu.VMEM((tc + SUBLANES, SLAB), F32), pltpu.VMEM((tc + SUBLANES, SLAB), F32)],
        compiler_params=_params(56, 1),
    )(*_in_hbm(u_pool, u_lru, u_gate, h, decay, root, dy_pool, dy_lru, pool_w, pool_scale, conv_w, conv_b, wa_bd,
               b_a, wi_bd, b_i, lam), after)


def _bwd_in(x, dh1, du_pool, du_lru, du_gate, ln1_g, w_in_t, after):
    s = x.shape[0]
    tb = _token_block(s)

    def body(x_ref, dh1_ref, dup_ref, dul_ref, dug_ref, g_ref, w_ref, _after, dx_ref, dwb_ref, dg_ref, dw_ref):
        @pl.when(pl.program_id(0) == 0)
        def _():
            dw_ref[...] = jnp.zeros_like(dw_ref)
            dg_ref[...] = jnp.zeros_like(dg_ref)

        g1 = g_ref[...]
        n, xhat, rstd = _rms(x_ref[...], g1)
        nb = n.astype(BF16)
        db = jnp.concatenate([_read_slabs(ref) for ref in (dup_ref, dul_ref, dug_ref)], axis=1)
        dw_ref[...] += _mm_tn(db, nb)
        dx, dg1 = _rms_bwd(_mm(db, w_ref[...]), xhat, rstd, g1)
        dx_ref[...] = dh1_ref[...] + dx
        dg_ref[...] += dg1

        @pl.when(pl.program_id(0) == s // tb - 1)
        def _():
            dwb_ref[...] = dw_ref[...].astype(BF16)

    row = pl.BlockSpec((tb, D_MODEL), lambda i: (i, 0))
    half = _slabs_spec(tb)
    vec = pl.BlockSpec((1, D_MODEL), lambda i: (0, 0))
    mat = pl.BlockSpec((IN_WIDTH, D_MODEL), lambda i: (0, 0))
    return pl.pallas_call(
        body, name="bwd_in", grid=(s // tb,),
        in_specs=[row, row, half, half, half, vec, mat, pl.BlockSpec(memory_space=pl.ANY)],
        out_specs=[row, mat, vec],
        out_shape=[pltpu.HBM((s, D_MODEL), F32), pltpu.HBM((IN_WIDTH, D_MODEL), BF16),
                   jax.ShapeDtypeStruct((1, D_MODEL), F32)],
        scratch_shapes=[pltpu.VMEM((IN_WIDTH, D_MODEL), F32)],
        compiler_params=_params(48, 1),
    )(*_in_hbm(x, dh1, du_pool, du_lru, du_gate, ln1_g, w_in_t), after)


def _mesh_position():
    x, y, c = lax.axis_index("x"), lax.axis_index("y"), lax.axis_index("c")
    return x, y, c, 4 * x + 2 * y + c


def _peer(x, y, c, p):
    px = 1 - x if p & 4 else x
    py = 1 - y if p & 2 else y
    pc = 1 - c if p & 1 else c
    return (px, py, pc), 4 * px + 2 * py + pc


HBM_SPEC = pl.BlockSpec(memory_space=pltpu.HBM)
SEM_SPEC = pl.BlockSpec(memory_space=pltpu.SEMAPHORE)
DATAFLOW = pltpu.SideEffectType.DATAFLOW_SIDE_EFFECTING


class Copy(NamedTuple):
    src: int
    src_at: Any
    dst: int
    dst_at: Any
    peer: int
    group: int
    slot: int


SIBLING = (1,)
SAME_CORE = (2, 4, 6)
EVERYONE = tuple(range(1, N_DEV))
MYSELF = (0,)


def _same(index):
    return index


def _chip(index):
    return jnp.right_shift(index, 1)


def _fan_out(srcs, lands, peers, group):
    return [Copy(s, None, d, _same, p, group, N_DEV * i + p) for i, (s, d) in enumerate(zip(srcs, lands)) for p in peers]


def _scatter(stacks, lands, peers):
    return [Copy(s, lambda me, p=p: jnp.bitwise_xor(me, p), d, _same, p, 0, 0)
            for s, d in zip(stacks, lands) for p in peers]


def _numbered(copies, group=0):
    return [cp._replace(group=group, slot=i) for i, cp in enumerate(copies)]


def _relay(lands, peers):
    return [Copy(b, lambda s, q=q: jnp.bitwise_xor(s, q), b, lambda s, q=q: jnp.bitwise_xor(s, q), 1, 0, N_DEV * i + q)
            for i, b in enumerate(lands) for q in peers]


def _to_sibling(stacks, lands):
    return [Copy(s, lambda me, k=k: 2 * k + 1 - jnp.bitwise_and(me, 1), d, lambda me, k=k: k, 1, 0, 4 * i + k)
            for i, (s, d) in enumerate(zip(stacks, lands)) for k in range(N_DEV // 2)]


def _to_chips(sums, lands):
    return [Copy(s, lambda me, p=p: jnp.bitwise_xor(_chip(me), p // 2), d, _chip, p, 0, 4 * i + p // 2)
            for i, (s, d) in enumerate(zip(sums, lands)) for p in SAME_CORE]


def _comm_call(name, bufs, wait=None, start=None, after=()):
    nb = len(bufs)
    waits = [wait] if isinstance(wait, tuple) else list(wait or [])
    slots = list(start[1]) if start else []
    n_out_sem = 2 * len(slots)
    after = [a for a in (after if isinstance(after, (list, tuple)) else [after]) if a is not None]

    def body(*refs):
        b = refs[:nb]
        at = nb + 2 * len(waits) + len(after)
        out_sems = refs[at:at + n_out_sem]
        token = refs[at + n_out_sem + nb]
        x, y, c, me = _mesh_position()

        def part(i, row_of, sender):
            return b[i] if row_of is None else b[i].at[row_of(sender)]

        for k, (_, copies) in enumerate(waits):
            w_send, w_recv = refs[nb + 2 * k], refs[nb + 2 * k + 1]
            for cp in copies:
                peer, peer_index = _peer(x, y, c, cp.peer)
                arrival = pltpu.make_async_remote_copy(part(cp.src, cp.src_at, me), part(cp.dst, cp.dst_at, peer_index),
                                                       w_send.at[cp.slot], w_recv.at[cp.slot],
                                                       device_id=peer, device_id_type=MESH)
                arrival.wait_send()
                arrival.wait_recv()
        if start:
            for cp in start[0]:
                peer, _ = _peer(x, y, c, cp.peer)
                pltpu.make_async_remote_copy(part(cp.src, cp.src_at, me), part(cp.dst, cp.dst_at, me),
                                             out_sems[2 * cp.group].at[cp.slot], out_sems[2 * cp.group + 1].at[cp.slot],
                                             device_id=peer, device_id_type=MESH).start()
        token[...] = jnp.zeros_like(token)

    sem_shapes = []
    for n_slots in slots:
        sem_shapes += [pltpu.SemaphoreType.DMA((n_slots,))] * 2
    operands = [pltpu.with_memory_space_constraint(a, pltpu.HBM) for a in bufs]
    in_specs = [HBM_SPEC] * nb
    for sems, _ in waits:
        operands += list(sems)
        in_specs += [SEM_SPEC, SEM_SPEC]
    operands += after
    in_specs += [pl.BlockSpec(memory_space=pl.ANY)] * len(after)
    outs = pl.pallas_call(
        body, name=name, in_specs=in_specs,
        out_specs=[SEM_SPEC] * n_out_sem + [HBM_SPEC] * nb + [pl.BlockSpec(memory_space=pltpu.VMEM)],
        out_shape=sem_shapes + [pltpu.HBM(a.shape, a.dtype) for a in bufs] + [jax.ShapeDtypeStruct((8, SLAB), F32)],
        input_output_aliases={i: n_out_sem + i for i in range(nb)},
        compiler_params=pltpu.CompilerParams(has_side_effects=DATAFLOW),
    )(*operands)
    sems = [(outs[2 * k], outs[2 * k + 1]) for k in range(len(slots))]
    return sems, list(outs[n_out_sem:n_out_sem + nb]), outs[-1]


def _pair_sum(stacks, lands, place):
    n = len(stacks)

    def body(place_ref, *refs):
        k = pl.program_id(0)
        for m in range(n):
            mine, theirs, out, land = refs[m], refs[n + m], refs[2 * n + m], refs[3 * n + m]
            total = (mine[0, 0].astype(F32) + theirs[0].astype(F32)).astype(out.dtype)
            out[0] = total

            @pl.when(k == place_ref[1])
            def _():
                land[0] = total

    in_specs = [pl.BlockSpec((1, 1) + a.shape[1:], lambda k, place_ref: (k, place_ref[0], 0, 0)) for a in stacks]
    in_specs += [pl.BlockSpec((1,) + a.shape[1:], lambda k, place_ref: (k, 0, 0)) for a in lands]
    out_specs = [pl.BlockSpec((1,) + a.shape[1:], lambda k, place_ref: (k, 0, 0)) for a in lands]
    out_specs += [pl.BlockSpec((1,) + a.shape[1:], lambda k, place_ref: (place_ref[1], 0, 0)) for a in lands]
    outs = pl.pallas_call(
        body, name="pair_sum_" + "_".join(str(a.shape[1]) for a in stacks),
        grid_spec=pltpu.PrefetchScalarGridSpec(num_scalar_prefetch=1, grid=(N_DEV // 2,), in_specs=in_specs,
                                               out_specs=out_specs),
        out_shape=[pltpu.HBM(a.shape, a.dtype) for a in lands] * 2,
        compiler_params=_params(40, 1),
    )(place, *_in_hbm(*[a.reshape((N_DEV // 2, 2) + a.shape[1:]) for a in stacks], *lands))
    return list(outs[:n]), list(outs[n:])


def _reduce_adam(parts, w, m, v, name):
    rows, cols = w.shape
    n_parts = parts.shape[0]
    rb = rows
    for cand in (256, 176, 128):
        if rows % cand == 0 and rows > cand:
            rb = cand
            break

    def body(p_ref, w_ref, m_ref, v_ref, g_out, d_out, m_out, v_out):
        g = p_ref[0].astype(F32)
        for j in range(1, n_parts):
            g = g + p_ref[j].astype(F32)
        g_out[...] = g
        d_out[...], m_out[...], v_out[...] = _adam(g, w_ref[...], m_ref[...], v_ref[...])

    blk = pl.BlockSpec((rb, cols), lambda i: (i, 0))
    out = jax.ShapeDtypeStruct((rows, cols), F32)
    return pl.pallas_call(
        body, name=name, grid=(rows // rb,),
        in_specs=[pl.BlockSpec((n_parts, rb, cols), lambda i: (0, i, 0)), blk, blk, blk],
        out_specs=[blk] * 4, out_shape=[out] * 4,
        compiler_params=_params(40, 1),
    )(*_in_hbm(parts, w, m, v))


def _cols_from_stack(stack):
    n, r, c = stack.shape
    return jnp.transpose(stack, (1, 0, 2)).reshape(r, n * c)


def _block_diag(w):
    z = jnp.zeros((N_SLAB, 64, 64), w.dtype)
    pairs = w.reshape(N_SLAB, 2, 64, 64)
    top = jnp.concatenate([pairs[:, 0], z], axis=2)
    bottom = jnp.concatenate([z, pairs[:, 1]], axis=2)
    return jnp.concatenate([top, bottom], axis=1)


def _adam(g, w, m, v):
    m_new = ADAM_B1 * m + (1.0 - ADAM_B1) * g
    v_new = ADAM_B2 * v + (1.0 - ADAM_B2) * (g * g)
    m_hat = m_new / (1.0 - ADAM_B1 ** ADAM_STEP)
    v_hat = v_new / (1.0 - ADAM_B2 ** ADAM_STEP)
    return (-ADAM_LR) * (m_hat / (jnp.sqrt(v_hat) + ADAM_EPS) + ADAM_WD * w), m_new, v_new


WIDE = ("ln2_g", "lnf_g")
HALF = ("pool_scale", "conv_b", "b_a", "b_i", "lam", "gn_pool_g", "gn_lru_g")
VECTORS = [(k, D_MODEL) for k in WIDE] + [(k, 512) for k in HALF]
VECTOR_ROWS = sum(width // SLAB for _, width in VECTORS)
LOSS_ROW = -(-VECTOR_ROWS // 8) * 8
CONV_AT = LOSS_ROW + 8
CONV_LANES = LRU_WIDTH // SLAB
PACK_F_ROWS = CONV_AT + CONV_WIDTH * CONV_LANES
MATRIX_ROWS = N_SLAB * SLAB
HEAD = SLAB // 2
GATE_ROWS = N_SLAB * HEAD
PACK_B_ROWS = MATRIX_ROWS + 2 * GATE_ROWS


def _pack_small(vectors, pool_g, wa_g, wi_g, conv, sq):
    n_vec = len(vectors)

    def body(*refs):
        vec = refs[:n_vec]
        pw_ref, wa_ref, wi_ref, cw_ref, sq_ref, out, out_b = refs[n_vec:]
        out[...] = jnp.zeros_like(out)
        row = 0
        for ref, (_, width) in zip(vec, VECTORS):
            for k in range(width // SLAB):
                out[row:row + 1, :] = ref[:, k * SLAB:(k + 1) * SLAB]
                row += 1
        for tap in range(CONV_WIDTH):
            for k in range(CONV_LANES):
                at = CONV_AT + tap * CONV_LANES + k
                out[at:at + 1, :] = cw_ref[tap:tap + 1, k * SLAB:(k + 1) * SLAB]
        total = sq_ref[:, 0:SLAB]
        for k in range(1, D_MODEL // SLAB):
            total = total + sq_ref[:, k * SLAB:(k + 1) * SLAB]
        out[LOSS_ROW:LOSS_ROW + 1, :] = total
        left = lax.broadcasted_iota(jnp.int32, (HEAD, SLAB), 1) < HEAD
        for s in range(N_SLAB):
            out_b[s * SLAB:(s + 1) * SLAB, :] = pw_ref[s].astype(BF16)
            for i, ref in enumerate((wa_ref, wi_ref)):
                at = MATRIX_ROWS + i * GATE_ROWS + s * HEAD
                out_b[at:at + HEAD, :] = jnp.where(left, ref[s, 0:HEAD, :], ref[s, HEAD:SLAB, :]).astype(BF16)

    return pl.pallas_call(
        body, name="pack_small",
        out_shape=[jax.ShapeDtypeStruct((PACK_F_ROWS, SLAB), F32), jax.ShapeDtypeStruct((PACK_B_ROWS, SLAB), BF16)],
    )(*vectors, pool_g, wa_g, wi_g, conv, sq)


def _small_reduce_adam(my_index, parts, parts_b, vec_w, vec_m, vec_v, pool_wmv, wa_wmv, wi_wmv, conv_wmv):
    n_vec = len(VECTORS)
    n_parts = parts.shape[0]

    def body(*refs):
        me_ref, p_ref, pb_ref = refs[0], refs[1], refs[2]
        refs = refs[1:]
        w_refs, m_refs, v_refs = (refs[2 + k * n_vec:2 + (k + 1) * n_vec] for k in range(3))
        at = 2 + 3 * n_vec
        pw_w, pw_m, pw_v = refs[at:at + 3]
        gates = (refs[at + 3:at + 6], refs[at + 6:at + 9])
        cv_w, cv_m, cv_v = refs[at + 9:at + 12]
        outs = refs[at + 12:-1]
        total = refs[-1]
        total[...] = p_ref[0]
        for j in range(1, n_parts):
            total[...] += p_ref[j]
        row = 0
        for i, (_, width) in enumerate(VECTORS):
            n_rows = width // SLAB
            g = jnp.concatenate([total[row + k:row + k + 1, :] for k in range(n_rows)], axis=1)
            row += n_rows
            d, m_new, v_new = _adam(g, w_refs[i][...], m_refs[i][...], v_refs[i][...])
            for ref, val in zip(outs[4 * i:4 * i + 4], (g, d, m_new, v_new)):
                ref[...] = val
        tail = outs[4 * n_vec:]

        def summed(first, count):
            g = pb_ref[0, first:first + count, :].astype(F32)
            for j in range(1, n_parts):
                g = g + pb_ref[j, first:first + count, :].astype(F32)
            return g

        g = summed(0, MATRIX_ROWS)
        d, m_new, v_new = _adam(g, pw_w[...], pw_m[...], pw_v[...])
        for ref, val in zip(tail[0:4], (g, d, m_new, v_new)):
            ref[...] = val
        for i, (w, m, v) in enumerate(gates):
            g = summed(MATRIX_ROWS + i * GATE_ROWS, GATE_ROWS)
            for s in range(N_SLAB):
                for hd in range(2):
                    block = g[s * HEAD:(s + 1) * HEAD, hd * HEAD:(hd + 1) * HEAD]
                    rows = slice((2 * s + hd) * HEAD, (2 * s + hd + 1) * HEAD)
                    d, m_new, v_new = _adam(block, w[rows, :], m[rows, :], v[rows, :])
                    for ref, val in zip(tail[4 + 4 * i:8 + 4 * i], (block, d, m_new, v_new)):
                        ref[rows, :] = val
        me = me_ref[0, 0]
        taps = []
        for tap in range(CONV_WIDTH):
            row = total[pl.ds(CONV_AT + tap * CONV_LANES + me // 2, 1), :]
            taps.append(jnp.where(me % 2 == 0, row[:, :HEAD], row[:, HEAD:]))
        g = jnp.concatenate(taps, axis=0)
        d, m_new, v_new = _adam(g, cv_w[...], cv_m[...], cv_v[...])
        for ref, val in zip(tail[12:16], (g, d, m_new, v_new)):
            ref[...] = val
        tail[16][...] = (0.5 / D_MODEL) * jnp.sum(total[LOSS_ROW:LOSS_ROW + 1, :], axis=1, keepdims=True)

    out_shape = []
    for _, width in VECTORS:
        out_shape += [jax.ShapeDtypeStruct((1, width), F32)] * 4
    out_shape += [jax.ShapeDtypeStruct((MATRIX_ROWS, SLAB), F32)] * 4
    out_shape += [jax.ShapeDtypeStruct((2 * GATE_ROWS, HEAD), F32)] * 8
    out_shape += [jax.ShapeDtypeStruct(conv_wmv[0].shape, F32)] * 4 + [jax.ShapeDtypeStruct((1, 1), F32)]
    operands = [parts, parts_b, *vec_w, *vec_m, *vec_v, *pool_wmv, *wa_wmv, *wi_wmv, *conv_wmv]

    def whole(a):
        return pl.BlockSpec(a.shape, lambda i, n=len(a.shape): (0,) * n)

    outs = pl.pallas_call(
        body, name="adam_small", grid=(1,),
        in_specs=[pl.BlockSpec(memory_space=pltpu.SMEM)] + [whole(a) for a in operands],
        out_specs=[whole(a) for a in out_shape], out_shape=out_shape,
        scratch_shapes=[pltpu.VMEM((PACK_F_ROWS, SLAB), F32)],
        compiler_params=_params(40, 1),
    )(my_index, *_in_hbm(*operands))
    vec_out = [tuple(outs[4 * i:4 * i + 4]) for i in range(n_vec)]
    tail = outs[4 * n_vec:]
    return vec_out, tuple(tail[0:4]), tuple(tail[4:8]), tuple(tail[8:12]), tuple(tail[12:16]), tail[16]


def kernel(x, ln1_g, w_in, pool_w, pool_scale, conv_w, conv_b, w_a, b_a, w_i, b_i, lam, gn_pool_g, gn_lru_g, w_out, ln2_g, w_ffn_gate, w_ffn_up, w_ffn_down, lnf_g, loss_target, m_ln1_g, m_w_in, m_pool_w, m_pool_scale, m_conv_w, m_conv_b, m_w_a, m_b_a, m_w_i, m_b_i, m_lam, m_gn_pool_g, m_gn_lru_g, m_w_out, m_ln2_g, m_w_ffn_gate, m_w_ffn_up, m_w_ffn_down, m_lnf_g, v_ln1_g, v_w_in, v_pool_w, v_pool_scale, v_conv_w, v_conv_b, v_w_a, v_b_a, v_w_i, v_b_i, v_lam, v_gn_pool_g, v_gn_lru_g, v_w_out, v_ln2_g, v_w_ffn_gate, v_w_ffn_up, v_w_ffn_down, v_lnf_g):
    weights = dict(ln1_g=ln1_g, w_in=w_in, pool_w=pool_w, pool_scale=pool_scale, conv_w=conv_w, conv_b=conv_b,
                   w_a=w_a, b_a=b_a, w_i=w_i, b_i=b_i, lam=lam, gn_pool_g=gn_pool_g, gn_lru_g=gn_lru_g,
                   w_out=w_out, ln2_g=ln2_g, w_ffn_gate=w_ffn_gate, w_ffn_up=w_ffn_up, w_ffn_down=w_ffn_down,
                   lnf_g=lnf_g)
    mom1 = dict(ln1_g=m_ln1_g, w_in=m_w_in, pool_w=m_pool_w, pool_scale=m_pool_scale, conv_w=m_conv_w,
                conv_b=m_conv_b, w_a=m_w_a, b_a=m_b_a, w_i=m_w_i, b_i=m_b_i, lam=m_lam, gn_pool_g=m_gn_pool_g,
                gn_lru_g=m_gn_lru_g, w_out=m_w_out, ln2_g=m_ln2_g, w_ffn_gate=m_w_ffn_gate,
                w_ffn_up=m_w_ffn_up, w_ffn_down=m_w_ffn_down, lnf_g=m_lnf_g)
    mom2 = dict(ln1_g=v_ln1_g, w_in=v_w_in, pool_w=v_pool_w, pool_scale=v_pool_scale, conv_w=v_conv_w,
                conv_b=v_conv_b, w_a=v_w_a, b_a=v_b_a, w_i=v_w_i, b_i=v_b_i, lam=v_lam, gn_pool_g=v_gn_pool_g,
                gn_lru_g=v_gn_lru_g, w_out=v_w_out, ln2_g=v_ln2_g, w_ffn_gate=v_w_ffn_gate,
                w_ffn_up=v_w_ffn_up, w_ffn_down=v_w_ffn_down, lnf_g=v_lnf_g)

    xs = x[0]
    target = loss_target[0]

    shard = dict(w_in=lambda a: a[0].T, w_ffn_gate=lambda a: a[0].T, w_ffn_up=lambda a: a[0].T,
                 w_out=lambda a: a[0], w_ffn_down=lambda a: a[0], conv_w=lambda a: a[0])
    unshard = dict(w_in=lambda a: a.T[None], w_ffn_gate=lambda a: a.T[None], w_ffn_up=lambda a: a.T[None],
                   w_out=lambda a: a[None], w_ffn_down=lambda a: a[None], conv_w=lambda a: a[None])

    gathered = ("w_in", "conv_w", "w_out", "w_ffn_gate", "w_ffn_up", "w_ffn_down")
    groups = ((0, 1), (2,), (3, 4), (5,))
    sources = [shard[k](weights[k]) if k == "conv_w" else shard[k](weights[k]).astype(BF16) for k in gathered]
    my_index = 4 * lax.axis_index("x") + 2 * lax.axis_index("y") + lax.axis_index("c")
    lands = [lax.empty((N_DEV,) + a.shape, a.dtype) for a in sources]

    def first_hop(n, at=0):
        return _numbered(_fan_out(range(at, at + n), range(at + n, at + 2 * n), SAME_CORE + SIBLING + MYSELF, 0))

    def second_hop(n, at=0):
        return _numbered(_relay(range(at, at + n), SAME_CORE))

    start = []
    for g, members in enumerate(groups):
        start += _numbered(_fan_out(members, [6 + m for m in members], SAME_CORE + SIBLING + MYSELF, 0), g)
    sems, bufs, _ = _comm_call("gather_start", sources + lands,
                               start=(start, [sum(cp.group == g for cp in start) for g in range(len(groups))]))
    sources, lands = bufs[:6], bufs[6:]

    def group_buffers(g):
        return [sources[m] for m in groups[g]] + [lands[m] for m in groups[g]]

    def relayed(tag, g, after):
        n = len(groups[g])
        relay = second_hop(n, n)
        relay_sems, bufs, _ = _comm_call("gather_relay_" + tag, group_buffers(g), wait=(sems[g], first_hop(n)),
                                         start=(relay, (len(relay),)), after=after)
        return relay_sems[0], bufs[n:]

    relay_sem, bufs = relayed("in", 0, None)
    g_in, g_conv = _comm_call("gather_wait_in", bufs, wait=(relay_sem, second_hop(2)))[1]
    w_in_f = g_in.reshape(IN_WIDTH, D_MODEL)
    conv_w_f = _cols_from_stack(g_conv)

    wa_bd = _block_diag(w_a[0])
    wi_bd = _block_diag(w_i[0])
    lnf_row = lnf_g.reshape(1, D_MODEL)

    u_pool, u_lru, u_gate = _fwd_in(xs, ln1_g, w_in_f)
    y_pool, h, y_lru, decay, root = _mixer_fwd(u_pool, u_lru, u_gate, pool_w[0], pool_scale, conv_w_f, conv_b,
                                               wa_bd, b_a, wi_bd, b_i, lam)
    relay_sem, bufs = relayed("out", 1, y_pool)
    gate_up = second_hop(2, 3)
    (gate_up_sem,), bufs, _ = _comm_call("gather_wait_out", bufs + group_buffers(2),
                                         wait=[(relay_sem, second_hop(1)), (sems[2], first_hop(2, 1))],
                                         start=(gate_up, (len(gate_up),)))
    g_out, gate_up_lands = bufs[0], bufs[3:]
    w_out_f = g_out.reshape(D_MODEL, D_MODEL)
    h1, n2 = _fwd_out(xs, y_pool, y_lru, gn_pool_g, gn_lru_g, w_out_f, ln2_g)
    relay_sem, bufs = relayed("down", 3, n2)
    g_gate, g_up, g_down = _comm_call("gather_wait_ffn", gate_up_lands + bufs,
                                      wait=[(gate_up_sem, second_hop(2)), (relay_sem, second_hop(1, 2))])[1]
    w_gate_f = g_gate.reshape(D_FF, D_MODEL)
    w_up_f = g_up.reshape(D_FF, D_MODEL)
    w_down_f = g_down.reshape(D_FF, D_MODEL)
    g_act, u_act, dh2, dh2b, d_lnf, sq = _ffn_fwd(h1, n2, target, lnf_row, w_gate_f, w_up_f, w_down_f)

    def direct(tag, stacks, wholes, after):
        sources = list(stacks) + list(wholes)
        n, n_st = len(sources), len(stacks)
        lands = [lax.empty(a.shape if i < n_st else (N_DEV,) + a.shape, a.dtype) for i, a in enumerate(sources)]
        copies = _numbered(_scatter(range(n_st), range(n, n + n_st), EVERYONE + MYSELF)
                           + _fan_out(range(n_st, n), range(n + n_st, 2 * n), EVERYONE + MYSELF, 0))
        sem, bufs, token = _comm_call(tag + "_start", sources + lands, start=(copies, (len(copies),)), after=after)
        return (tag, sem[0], bufs, copies), token

    def direct_finish(started, after):
        tag, sem, bufs, copies = started
        _, bufs, _ = _comm_call(tag + "_wait", bufs, wait=(sem, copies), after=after)
        return bufs[len(bufs) // 2:]

    place = jnp.stack([lax.axis_index("c"), 2 * lax.axis_index("x") + lax.axis_index("y")]).astype(jnp.int32)

    d_gate, d_up, d_down, dn2 = _ffn_bwd(n2, dh2b, g_act, u_act, w_gate_f, w_up_f, w_down_f)
    down_stack, *ffn_stacks = [d.reshape(N_DEV, D_FF // N_DEV, D_MODEL) for d in (d_down, d_gate, d_up)]
    pair_lands = [lax.empty((N_DEV // 2,) + a.shape[1:], a.dtype) for a in ffn_stacks]
    pair_copies = _numbered(_to_sibling(range(2), range(2, 4)))
    down_copies = _numbered(_scatter([4], [5], EVERYONE + MYSELF), 1)
    (sem, down_sem), bufs, token = _comm_call(
        "ffn_grads_start", ffn_stacks + pair_lands + [down_stack, lax.empty(down_stack.shape, down_stack.dtype)],
        start=(pair_copies + down_copies, (len(pair_copies), len(down_copies))))
    bufs, down_bufs = bufs[:4], bufs[4:]
    dh1, dy_pool, dy_lru, d_out, d_ln2, d_gnp, d_gnl = _bwd_out(dn2, dh2, h1, y_pool, y_lru, gn_pool_g, gn_lru_g,
                                                                 w_out_f, ln2_g, token)
    _, bufs, _ = _comm_call("ffn_pair_wait", bufs, wait=(sem, pair_copies), after=dh1)
    ffn_sums, ffn_lands = _pair_sum(bufs[:2], bufs[2:], place)
    ffn_copies = _numbered(_to_chips(range(2), range(2, 4)))
    ffn_sem, ffn_bufs, token = _comm_call("ffn_chip_start", ffn_sums + ffn_lands, start=(ffn_copies, (len(ffn_copies),)))
    (du_pool, du_lru, du_gate, d_pw, d_ps, d_cw, d_cb, d_wa, d_ba, d_wi, d_bi, d_lam) = _mixer_bwd(
        u_pool, u_lru, u_gate, h, decay, root, dy_pool, dy_lru, pool_w[0], pool_scale, conv_w_f, conv_b,
        wa_bd, b_a, wi_bd, b_i, lam, token)

    vec_grads = dict(ln2_g=d_ln2, lnf_g=d_lnf, pool_scale=d_ps, conv_b=d_cb, b_a=d_ba, b_i=d_bi,
                     lam=d_lam, gn_pool_g=d_gnp, gn_lru_g=d_gnl)
    packed, packed_b = _pack_small([vec_grads[k] for k, _ in VECTORS], d_pw, d_wa, d_wi, d_cw, sq)
    small_started, token = direct("small", [d_out.reshape(N_DEV, D_MODEL // N_DEV, D_MODEL)], [packed, packed_b], None)
    grad_x, d_in, d_ln1 = _bwd_in(xs, dh1, du_pool, du_lru, du_gate, ln1_g, w_in_f, token)
    in_stack = d_in.reshape(N_DEV, IN_WIDTH // N_DEV, D_MODEL)
    in_pair = _numbered(_to_sibling([0], [1]))
    ln1_copies = _numbered(_fan_out([2], [3], EVERYONE + MYSELF, 0), 1)
    (in_pair_sem, ln1_sem), bufs, token = _comm_call(
        "in_pair_start", [in_stack, lax.empty((N_DEV // 2,) + in_stack.shape[1:], in_stack.dtype),
                          d_ln1, lax.empty((N_DEV,) + d_ln1.shape, d_ln1.dtype)],
        start=(in_pair + ln1_copies, (len(in_pair), len(ln1_copies))))
    in_bufs, ln1_bufs = bufs[:2], bufs[2:]

    results = {}

    def reduce_adam(name, parts):
        outs = _reduce_adam(parts, shard[name](weights[name]), shard[name](mom1[name]), shard[name](mom2[name]),
                            "adam_" + name)
        results[name] = tuple(unshard[name](o) for o in outs)
        return outs[0]

    shard["ln1_g"] = unshard["ln1_g"] = lambda a: a
    _, ffn_bufs, _ = _comm_call("ffn_chip_wait", ffn_bufs, wait=(ffn_sem[0], ffn_copies), after=token)
    done = [reduce_adam("w_ffn_gate", ffn_bufs[2])]
    _, in_bufs, _ = _comm_call("in_pair_wait", in_bufs, wait=(in_pair_sem, in_pair), after=done)
    in_sums, in_lands = _pair_sum(in_bufs[:1], in_bufs[1:], place)
    in_copies = _numbered(_to_chips([0], [1]))
    in_sem, in_bufs, _ = _comm_call("in_chip_start", in_sums + in_lands, start=(in_copies, (len(in_copies),)))
    done.append(reduce_adam("w_ffn_up", ffn_bufs[3]))
    _, (_, r_down), _ = _comm_call("down_wait", down_bufs, after=done,
                                   wait=(down_sem, _numbered(_scatter([0], [1], EVERYONE + MYSELF))))
    done.append(reduce_adam("w_ffn_down", r_down))
    r_out, r_small, r_small_b = direct_finish(small_started, done)
    done.append(reduce_adam("w_out", r_out))

    def as_row(a, width):
        return a.reshape(1, width)

    def as_matrix(a):
        return a.reshape(MATRIX_ROWS, SLAB)

    def as_heads(a):
        return a.reshape(2 * GATE_ROWS, HEAD)

    states = (weights, mom1, mom2)
    vec_out, pool_out, wa_out, wi_out, conv_out, loss_11 = _small_reduce_adam(
        my_index.astype(jnp.int32).reshape(1, 1), r_small, r_small_b,
        *[[as_row(t[k], w) for k, w in VECTORS] for t in states], [as_matrix(t["pool_w"]) for t in states],
        [as_heads(t["w_a"]) for t in states], [as_heads(t["w_i"]) for t in states], [t["conv_w"][0] for t in states])
    for (k, _), outs in zip(VECTORS, vec_out):
        results[k] = tuple(o.reshape(weights[k].shape) for o in outs)
    for k, outs in (("pool_w", pool_out), ("w_a", wa_out), ("w_i", wi_out), ("conv_w", conv_out)):
        results[k] = tuple(o.reshape(weights[k].shape) for o in outs)
    loss = loss_11[0, 0]
    _, (_, r_in, _, r_ln1), _ = _comm_call(
        "in_chip_wait", in_bufs + ln1_bufs, after=done + [loss_11],
        wait=[(in_sem[0], in_copies), (ln1_sem, _numbered(_fan_out([2], [3], EVERYONE + MYSELF, 0)))])
    reduce_adam("w_in", r_in)
    reduce_adam("ln1_g", r_ln1)

    order = ["ln1_g", "w_in", "pool_w", "pool_scale", "conv_w", "conv_b", "w_a", "b_a", "w_i", "b_i", "lam",
             "gn_pool_g", "gn_lru_g", "w_out", "ln2_g", "w_ffn_gate", "w_ffn_up", "w_ffn_down", "lnf_g"]
    return (loss, grad_x[None],
            *[results[k][0] for k in order], *[results[k][1] for k in order],
            *[results[k][2] for k in order], *[results[k][3] for k in order])
```
